```python
import math
import jax, jax.numpy as jnp
from jax import lax
import numpy as np

D_MODEL = 1024
BATCH = 8
SEQ = 8192
DEPTH = 1

SSM_GROUP_SIZE = 16
SSM_GROUPS = 32
SSM_WIDTH = SSM_GROUP_SIZE * SSM_GROUPS
SSM_STATE = 64
SSM_CHUNK = 128
SSM_DT_MIN = 1e-3
SSM_DT_MAX = 1e-1
ATTN_PATTERNS = ((128, 1), (512, 4), (2048, 16))
ATTN_HEADS_PER_GROUP = 4
ATTN_HEAD_DIM = 64
ATTN_HEADS = ATTN_HEADS_PER_GROUP * len(ATTN_PATTERNS)
ATTN_WIDTH = ATTN_HEADS * ATTN_HEAD_DIM
ATTN_OUT_WIDTH = ATTN_HEADS_PER_GROUP * ATTN_HEAD_DIM
MEM_LEN = 256
MEM_HEADS = 4
MEM_HEAD_DIM = 128
MEM_WIDTH = MEM_HEADS * MEM_HEAD_DIM
N_BRANCHES = 3
D_FF = 4 * D_MODEL
IN_SPLITS = (SSM_WIDTH, ATTN_WIDTH, ATTN_WIDTH, ATTN_WIDTH, MEM_WIDTH, N_BRANCHES * D_MODEL)
IN_WIDTH = sum(IN_SPLITS)
IN_OFFSETS = tuple(int(o) for o in np.cumsum(IN_SPLITS)[:-1])
RMS_EPS = 1e-6
NEG_INF = -1e30

kernel_name = "hybrid_s5_dilated_attn_memory_gated_block"


def rms_norm(x, g):
    xf = x.astype(jnp.float32)
    y = xf * lax.rsqrt(jnp.mean(xf * xf, axis=-1, keepdims=True) + RMS_EPS)
    return (y * g.astype(jnp.float32)).astype(x.dtype)


def _complex_affine_combine(e1, e2):
    a1r, a1i, b1r, b1i = e1
    a2r, a2i, b2r, b2i = e2
    ar = a2r * a1r - a2i * a1i
    ai = a2r * a1i + a2i * a1r
    br = a2r * b1r - a2i * b1i + b2r
    bi = a2r * b1i + a2i * b1r + b2i
    return ar, ai, br, bi


def s5_ssm(u, lam_re, lam_im, log_dt, b_re, b_im, c_re, c_im, d_skip):
    f32 = jnp.float32
    bsz, l, _ = u.shape
    n_chunks = l // SSM_CHUNK
    u = u.astype(f32).reshape(bsz, n_chunks, SSM_CHUNK, SSM_GROUPS, SSM_GROUP_SIZE)
    u = u.transpose(1, 0, 2, 3, 4)
    lr, li = lam_re.astype(f32), lam_im.astype(f32)
    dt = jnp.exp(log_dt.astype(f32))[:, None]
    mag = jnp.exp(lr * dt)
    a_re, a_im = mag * jnp.cos(li * dt), mag * jnp.sin(li * dt)
    nr, ni = a_re - 1.0, a_im
    den = lr * lr + li * li
    coef_re = (nr * lr + ni * li) / den
    coef_im = (ni * lr - nr * li) / den
    br_, bi_ = b_re.astype(f32), b_im.astype(f32)
    bb_re = coef_re[..., None] * br_ - coef_im[..., None] * bi_
    bb_im = coef_re[..., None] * bi_ + coef_im[..., None] * br_
    cr, ci = c_re.astype(f32), c_im.astype(f32)
    dd = d_skip.astype(f32)

    def chunk_step(carry, u_c):
        s_re0, s_im0 = carry
        bu_re = jnp.einsum('bcgh,gph->bcgp', u_c, bb_re)
        bu_im = jnp.einsum('bcgh,gph->bcgp', u_c, bb_im)
        ar = jnp.broadcast_to(a_re, bu_re.shape)
        ai = jnp.broadcast_to(a_im, bu_re.shape)
        pr, pi, hr, hi = lax.associative_scan(_complex_affine_combine, (ar, ai, bu_re, bu_im), axis=1)
        s_re = hr + pr * s_re0[:, None] - pi * s_im0[:, None]
        s_im = hi + pr * s_im0[:, None] + pi * s_re0[:, None]
        y = (jnp.einsum('bcgp,ghp->bcgh', s_re, cr)
             - jnp.einsum('bcgp,ghp->bcgh', s_im, ci)
             + dd * u_c)
        return (s_re[:, -1], s_im[:, -1]), y

    init = (jnp.zeros((bsz, SSM_GROUPS, SSM_STATE), f32), jnp.zeros((bsz, SSM_GROUPS, SSM_STATE), f32))
    _, y = lax.scan(chunk_step, init, u)
    return y.transpose(1, 0, 2, 3, 4).reshape(bsz, l, SSM_WIDTH)


def dilated_window_attention(q, k, v, window, dilation):
    f32 = jnp.float32
    b, l, h, e = q.shape
    w = window // dilation
    m = l // dilation
    nb = -(-m // w)
    mp = nb * w

    def to_sub(t):
        t = t.reshape(b, m, dilation, h, e).transpose(0, 2, 3, 1, 4)
        t = jnp.pad(t, ((0, 0), (0, 0), (0, 0), (0, mp - m), (0, 0)))
        return t.reshape(b, dilation, h, nb, w, e)

    def with_prev(t):
        prev = jnp.pad(t, ((0, 0), (0, 0), (0, 0), (1, 0), (0, 0), (0, 0)))[:, :, :, :-1]
        return jnp.concatenate([prev, t], axis=4)

    qs, ks, vs = to_sub(q), to_sub(k), to_sub(v)
    kb, vb = with_prev(ks), with_prev(vs)
    s = jnp.einsum('bdhnqe,bdhnke->bdhnqk', qs, kb).astype(f32) * (e ** -0.5)
    qi = jnp.arange(w)[:, None]
    kj = jnp.arange(2 * w)[None, :]
    dist = w + qi - kj
    blk = jnp.arange(nb)[:, None, None]
    valid = (dist >= 0) & (dist <= w) & ((blk > 0) | (kj >= w))
    s = jnp.where(valid, s, NEG_INF)
    mx = jnp.max(s, axis=-1, keepdims=True)
    p = jnp.exp(s - mx)
    den = jnp.sum(p, axis=-1)
    o = jnp.einsum('bdhnqk,bdhnke->bdhnqe', p, vb.astype(f32)) / den[..., None]
    lse = mx[..., 0] + jnp.log(den)
    o = o.reshape(b, dilation, h, mp, e)[:, :, :, :m].transpose(0, 3, 1, 2, 4).reshape(b, l, h, e)
    lse = lse.reshape(b, dilation, h, mp)[..., :m].transpose(0, 3, 1, 2).reshape(b, l, h)
    return o, lse


def memory_cross_attention(q, k, v):
    b, l, hm, e = q.shape
    s = jnp.einsum('blhe,bmhe->bhlm', q, k).astype(jnp.float32) * (e ** -0.5)
    p = jax.nn.softmax(s, axis=-1)
    o = jnp.einsum('bhlm,bmhe->blhe', p, v.astype(jnp.float32))
    return o.astype(q.dtype).reshape(b, l, hm * e)


def _fwd_setup_inputs(seed: int = 0) -> dict:
    key = jax.random.key(seed)
    ks = jax.random.split(key, 26)
    f32 = jnp.float32

    def nrm(k, shape, scale):
        return jax.random.normal(k, shape, f32) * scale

    G, P, H = SSM_GROUPS, SSM_STATE, SSM_GROUP_SIZE
    return {
        "x": nrm(ks[0], (BATCH, SEQ, D_MODEL), 1.0),
        "mem": nrm(ks[1], (BATCH, MEM_LEN, D_MODEL), 1.0),
        "norm1_g": 1.0 + nrm(ks[2], (DEPTH, D_MODEL), 0.02),
        "mem_norm_g": 1.0 + nrm(ks[3], (DEPTH, D_MODEL), 0.02),
        "w_in": nrm(ks[4], (DEPTH, D_MODEL, IN_WIDTH), D_MODEL ** -0.5),
        "b_gate": nrm(ks[5], (DEPTH, N_BRANCHES * D_MODEL), 0.02),
        "ssm_lambda_re": -0.5 + nrm(ks[6], (DEPTH, G, P), 0.01),
        "ssm_lambda_im": math.pi * jnp.arange(P, dtype=f32) + nrm(ks[7], (DEPTH, G, P), 0.01),
        "ssm_log_dt": jax.random.uniform(ks[8], (DEPTH, G), f32, math.log(SSM_DT_MIN), math.log(SSM_DT_MAX)),
        "ssm_b_re": nrm(ks[9], (DEPTH, G, P, H), (2 * H) ** -0.5),
        "ssm_b_im": nrm(ks[10], (DEPTH, G, P, H), (2 * H) ** -0.5),
        "ssm_c_re": nrm(ks[11], (DEPTH, G, H, P), P ** -0.5),
        "ssm_c_im": nrm(ks[12], (DEPTH, G, H, P), P ** -0.5),
        "ssm_d": nrm(ks[13], (DEPTH, G, H), 1.0),
        "w_glu": nrm(ks[14], (DEPTH, SSM_WIDTH, SSM_WIDTH), SSM_WIDTH ** -0.5),
        "b_glu": nrm(ks[15], (DEPTH, SSM_WIDTH), 0.02),
        "w_ssm_br": nrm(ks[16], (DEPTH, SSM_WIDTH, D_MODEL), SSM_WIDTH ** -0.5),
        "w_attn_br": nrm(ks[17], (DEPTH, ATTN_OUT_WIDTH, D_MODEL), ATTN_OUT_WIDTH ** -0.5),
        "w_mem_kv": nrm(ks[18], (DEPTH, D_MODEL, 2 * MEM_WIDTH), D_MODEL ** -0.5),
        "w_mem_br": nrm(ks[19], (DEPTH, MEM_WIDTH, D_MODEL), MEM_WIDTH ** -0.5),
        "w_o": nrm(ks[20], (DEPTH, D_MODEL, D_MODEL), D_MODEL ** -0.5),
        "norm2_g": 1.0 + nrm(ks[21], (DEPTH, D_MODEL), 0.02),
        "w_up": nrm(ks[22], (DEPTH, D_MODEL, D_FF), D_MODEL ** -0.5),
        "w_down": nrm(ks[23], (DEPTH, D_FF, D_MODEL), D_FF ** -0.5),
        "final_g": 1.0 + nrm(ks[24], (D_MODEL,), 0.02),
    }


def _fwd_reference(x, mem, norm1_g, mem_norm_g, w_in, b_gate, ssm_lambda_re, ssm_lambda_im, ssm_log_dt,
              ssm_b_re, ssm_b_im, ssm_c_re, ssm_c_im, ssm_d, w_glu, b_glu, w_ssm_br, w_attn_br,
              w_mem_kv, w_mem_br, w_o, norm2_g, w_up, w_down, final_g):
    bsz, seq, _ = x.shape
    h = x
    for i in range(DEPTH):
        n = rms_norm(h, norm1_g[i])
        z = n @ w_in[i]
        u, q, k, v, mq, zg = jnp.split(z, IN_OFFSETS, axis=-1)
        gates = jax.nn.sigmoid(zg + b_gate[i]).reshape(bsz, seq, N_BRANCHES, D_MODEL)

        y = s5_ssm(u, ssm_lambda_re[i], ssm_lambda_im[i], ssm_log_dt[i], ssm_b_re[i], ssm_b_im[i],
                   ssm_c_re[i], ssm_c_im[i], ssm_d[i]).astype(x.dtype)
        y = jax.nn.gelu(y)
        y = y * jax.nn.sigmoid(y @ w_glu[i] + b_glu[i])
        br_ssm = y @ w_ssm_br[i]

        q = q.reshape(bsz, seq, ATTN_HEADS, ATTN_HEAD_DIM)
        k = k.reshape(bsz, seq, ATTN_HEADS, ATTN_HEAD_DIM)
        v = v.reshape(bsz, seq, ATTN_HEADS, ATTN_HEAD_DIM)
        outs, lses = [], []
        for g, (window, dilation) in enumerate(ATTN_PATTERNS):
            sl = slice(g * ATTN_HEADS_PER_GROUP, (g + 1) * ATTN_HEADS_PER_GROUP)
            o_g, lse_g = dilated_window_attention(q[:, :, sl], k[:, :, sl], v[:, :, sl], window, dilation)
            outs.append(o_g)
            lses.append(lse_g)
        wts = jax.nn.softmax(jnp.stack(lses, axis=0), axis=0)
        o = jnp.sum(wts[..., None] * jnp.stack(outs, axis=0), axis=0)
        br_attn = o.astype(x.dtype).reshape(bsz, seq, ATTN_OUT_WIDTH) @ w_attn_br[i]

        kv = rms_norm(mem, mem_norm_g[i]) @ w_mem_kv[i]
        mk, mv = jnp.split(kv, 2, axis=-1)
        mk = mk.reshape(bsz, MEM_LEN, MEM_HEADS, MEM_HEAD_DIM)
        mv = mv.reshape(bsz, MEM_LEN, MEM_HEADS, MEM_HEAD_DIM)
        mq = mq.reshape(bsz, seq, MEM_HEADS, MEM_HEAD_DIM)
        br_mem = memory_cross_attention(mq, mk, mv) @ w_mem_br[i]

        merged = gates[:, :, 0] * br_ssm + gates[:, :, 1] * br_attn + gates[:, :, 2] * br_mem
        h = h + merged @ w_o[i]

        n2 = rms_norm(h, norm2_g[i])
        h = h + jnp.square(jax.nn.relu(n2 @ w_up[i])) @ w_down[i]
    return rms_norm(h, final_g)


import jax as _jax
import jax.numpy as _jnp

TWIN_FORMAT = 'train_step'
FWD_PARAMS = ['x', 'mem', 'norm1_g', 'mem_norm_g', 'w_in', 'b_gate', 'ssm_lambda_re', 'ssm_lambda_im', 'ssm_log_dt', 'ssm_b_re', 'ssm_b_im', 'ssm_c_re', 'ssm_c_im', 'ssm_d', 'w_glu', 'b_glu', 'w_ssm_br', 'w_attn_br', 'w_mem_kv', 'w_mem_br', 'w_o', 'norm2_g', 'w_up', 'w_down', 'final_g']
TWIN_WEIGHTS = ['norm1_g', 'mem_norm_g', 'w_in', 'b_gate', 'ssm_lambda_re', 'ssm_lambda_im', 'ssm_log_dt', 'ssm_b_re', 'ssm_b_im', 'ssm_c_re', 'ssm_c_im', 'ssm_d', 'w_glu', 'b_glu', 'w_ssm_br', 'w_attn_br', 'w_mem_kv', 'w_mem_br', 'w_o', 'norm2_g', 'w_up', 'w_down', 'final_g']
TWIN_DIFF_INPUT = 'x'
TWIN_INPUTS = ['x', 'mem', 'norm1_g', 'mem_norm_g', 'w_in', 'b_gate', 'ssm_lambda_re', 'ssm_lambda_im', 'ssm_log_dt', 'ssm_b_re', 'ssm_b_im', 'ssm_c_re', 'ssm_c_im', 'ssm_d', 'w_glu', 'b_glu', 'w_ssm_br', 'w_attn_br', 'w_mem_kv', 'w_mem_br', 'w_o', 'norm2_g', 'w_up', 'w_down', 'final_g', 'loss_target', 'm_norm1_g', 'm_mem_norm_g', 'm_w_in', 'm_b_gate', 'm_ssm_lambda_re', 'm_ssm_lambda_im', 'm_ssm_log_dt', 'm_ssm_b_re', 'm_ssm_b_im', 'm_ssm_c_re', 'm_ssm_c_im', 'm_ssm_d', 'm_w_glu', 'm_b_glu', 'm_w_ssm_br', 'm_w_attn_br', 'm_w_mem_kv', 'm_w_mem_br', 'm_w_o', 'm_norm2_g', 'm_w_up', 'm_w_down', 'm_final_g', 'v_norm1_g', 'v_mem_norm_g', 'v_w_in', 'v_b_gate', 'v_ssm_lambda_re', 'v_ssm_lambda_im', 'v_ssm_log_dt', 'v_ssm_b_re', 'v_ssm_b_im', 'v_ssm_c_re', 'v_ssm_c_im', 'v_ssm_d', 'v_w_glu', 'v_b_glu', 'v_w_ssm_br', 'v_w_attn_br', 'v_w_mem_kv', 'v_w_mem_br', 'v_w_o', 'v_norm2_g', 'v_w_up', 'v_w_down', 'v_final_g']
TWIN_OUTPUTS = ['loss', 'grad_x', 'grad_norm1_g', 'grad_mem_norm_g', 'grad_w_in', 'grad_b_gate', 'grad_ssm_lambda_re', 'grad_ssm_lambda_im', 'grad_ssm_log_dt', 'grad_ssm_b_re', 'grad_ssm_b_im', 'grad_ssm_c_re', 'grad_ssm_c_im', 'grad_ssm_d', 'grad_w_glu', 'grad_b_glu', 'grad_w_ssm_br', 'grad_w_attn_br', 'grad_w_mem_kv', 'grad_w_mem_br', 'grad_w_o', 'grad_norm2_g', 'grad_w_up', 'grad_w_down', 'grad_final_g', 'delta_norm1_g', 'delta_mem_norm_g', 'delta_w_in', 'delta_b_gate', 'delta_ssm_lambda_re', 'delta_ssm_lambda_im', 'delta_ssm_log_dt', 'delta_ssm_b_re', 'delta_ssm_b_im', 'delta_ssm_c_re', 'delta_ssm_c_im', 'delta_ssm_d', 'delta_w_glu', 'delta_b_glu', 'delta_w_ssm_br', 'delta_w_attn_br', 'delta_w_mem_kv', 'delta_w_mem_br', 'delta_w_o', 'delta_norm2_g', 'delta_w_up', 'delta_w_down', 'delta_final_g', 'new_m_norm1_g', 'new_m_mem_norm_g', 'new_m_w_in', 'new_m_b_gate', 'new_m_ssm_lambda_re', 'new_m_ssm_lambda_im', 'new_m_ssm_log_dt', 'new_m_ssm_b_re', 'new_m_ssm_b_im', 'new_m_ssm_c_re', 'new_m_ssm_c_im', 'new_m_ssm_d', 'new_m_w_glu', 'new_m_b_glu', 'new_m_w_ssm_br', 'new_m_w_attn_br', 'new_m_w_mem_kv', 'new_m_w_mem_br', 'new_m_w_o', 'new_m_norm2_g', 'new_m_w_up', 'new_m_w_down', 'new_m_final_g', 'new_v_norm1_g', 'new_v_mem_norm_g', 'new_v_w_in', 'new_v_b_gate', 'new_v_ssm_lambda_re', 'new_v_ssm_lambda_im', 'new_v_ssm_log_dt', 'new_v_ssm_b_re', 'new_v_ssm_b_im', 'new_v_ssm_c_re', 'new_v_ssm_c_im', 'new_v_ssm_d', 'new_v_w_glu', 'new_v_b_glu', 'new_v_w_ssm_br', 'new_v_w_attn_br', 'new_v_w_mem_kv', 'new_v_w_mem_br', 'new_v_w_o', 'new_v_norm2_g', 'new_v_w_up', 'new_v_w_down', 'new_v_final_g']
TWIN_LEAF_KINDS = {'loss': 'loss', 'grad_x': 'grad_x', 'grad_norm1_g': 'grad_w', 'grad_mem_norm_g': 'grad_w', 'grad_w_in': 'grad_w', 'grad_b_gate': 'grad_w', 'grad_ssm_lambda_re': 'grad_w', 'grad_ssm_lambda_im': 'grad_w', 'grad_ssm_log_dt': 'grad_w', 'grad_ssm_b_re': 'grad_w', 'grad_ssm_b_im': 'grad_w', 'grad_ssm_c_re': 'grad_w', 'grad_ssm_c_im': 'grad_w', 'grad_ssm_d': 'grad_w', 'grad_w_glu': 'grad_w', 'grad_b_glu': 'grad_w', 'grad_w_ssm_br': 'grad_w', 'grad_w_attn_br': 'grad_w', 'grad_w_mem_kv': 'grad_w', 'grad_w_mem_br': 'grad_w', 'grad_w_o': 'grad_w', 'grad_norm2_g': 'grad_w', 'grad_w_up': 'grad_w', 'grad_w_down': 'grad_w', 'grad_final_g': 'grad_w', 'delta_norm1_g': 'delta_w', 'delta_mem_norm_g': 'delta_w', 'delta_w_in': 'delta_w', 'delta_b_gate': 'delta_w', 'delta_ssm_lambda_re': 'delta_w', 'delta_ssm_lambda_im': 'delta_w', 'delta_ssm_log_dt': 'delta_w', 'delta_ssm_b_re': 'delta_w', 'delta_ssm_b_im': 'delta_w', 'delta_ssm_c_re': 'delta_w', 'delta_ssm_c_im': 'delta_w', 'delta_ssm_d': 'delta_w', 'delta_w_glu': 'delta_w', 'delta_b_glu': 'delta_w', 'delta_w_ssm_br': 'delta_w', 'delta_w_attn_br': 'delta_w', 'delta_w_mem_kv': 'delta_w', 'delta_w_mem_br': 'delta_w', 'delta_w_o': 'delta_w', 'delta_norm2_g': 'delta_w', 'delta_w_up': 'delta_w', 'delta_w_down': 'delta_w', 'delta_final_g': 'delta_w', 'new_m_norm1_g': 'new_m', 'new_m_mem_norm_g': 'new_m', 'new_m_w_in': 'new_m', 'new_m_b_gate': 'new_m', 'new_m_ssm_lambda_re': 'new_m', 'new_m_ssm_lambda_im': 'new_m', 'new_m_ssm_log_dt': 'new_m', 'new_m_ssm_b_re': 'new_m', 'new_m_ssm_b_im': 'new_m', 'new_m_ssm_c_re': 'new_m', 'new_m_ssm_c_im': 'new_m', 'new_m_ssm_d': 'new_m', 'new_m_w_glu': 'new_m', 'new_m_b_glu': 'new_m', 'new_m_w_ssm_br': 'new_m', 'new_m_w_attn_br': 'new_m', 'new_m_w_mem_kv': 'new_m', 'new_m_w_mem_br': 'new_m', 'new_m_w_o': 'new_m', 'new_m_norm2_g': 'new_m', 'new_m_w_up': 'new_m', 'new_m_w_down': 'new_m', 'new_m_final_g': 'new_m', 'new_v_norm1_g': 'new_v', 'new_v_mem_norm_g': 'new_v', 'new_v_w_in': 'new_v', 'new_v_b_gate': 'new_v', 'new_v_ssm_lambda_re': 'new_v', 'new_v_ssm_lambda_im': 'new_v', 'new_v_ssm_log_dt': 'new_v', 'new_v_ssm_b_re': 'new_v', 'new_v_ssm_b_im': 'new_v', 'new_v_ssm_c_re': 'new_v', 'new_v_ssm_c_im': 'new_v', 'new_v_ssm_d': 'new_v', 'new_v_w_glu': 'new_v', 'new_v_b_glu': 'new_v', 'new_v_w_ssm_br': 'new_v', 'new_v_w_attn_br': 'new_v', 'new_v_w_mem_kv': 'new_v', 'new_v_w_mem_br': 'new_v', 'new_v_w_o': 'new_v', 'new_v_norm2_g': 'new_v', 'new_v_w_up': 'new_v', 'new_v_w_down': 'new_v', 'new_v_final_g': 'new_v'}


def _forward(args):
    return _fwd_reference(*[args[k] for k in FWD_PARAMS])


def _output_shape():
    out = _jax.eval_shape(lambda: _forward(_fwd_setup_inputs(0)))
    return out.shape, out.dtype

N_MICROBATCH = 1
ADAM_LR = 0.001
ADAM_B1 = 0.9
ADAM_B2 = 0.999
ADAM_EPS = 1e-08
ADAM_WD = 0.01
ADAM_STEP = 10
PER_EXAMPLE_BATCH_AXIS = {'x': 0, 'mem': 0, 'loss_target': 0}
SHARED_INPUTS = []
_WEIGHT_DTYPES = {'norm1_g': _jnp.float32, 'mem_norm_g': _jnp.float32, 'w_in': _jnp.float32, 'b_gate': _jnp.float32, 'ssm_lambda_re': _jnp.float32, 'ssm_lambda_im': _jnp.float32, 'ssm_log_dt': _jnp.float32, 'ssm_b_re': _jnp.float32, 'ssm_b_im': _jnp.float32, 'ssm_c_re': _jnp.float32, 'ssm_c_im': _jnp.float32, 'ssm_d': _jnp.float32, 'w_glu': _jnp.float32, 'b_glu': _jnp.float32, 'w_ssm_br': _jnp.float32, 'w_attn_br': _jnp.float32, 'w_mem_kv': _jnp.float32, 'w_mem_br': _jnp.float32, 'w_o': _jnp.float32, 'norm2_g': _jnp.float32, 'w_up': _jnp.float32, 'w_down': _jnp.float32, 'final_g': _jnp.float32}
MOMENT_SCALE = {'norm1_g': 7.598377e-02, 'mem_norm_g': 2.319541e-02, 'w_in': 3.026627e-02, 'b_gate': 1.361375e-02, 'ssm_lambda_re': 7.301716e-03, 'ssm_lambda_im': 6.362663e-03, 'ssm_log_dt': 5.727046e+00, 'ssm_b_re': 3.754737e-03, 'ssm_b_im': 3.750746e-03, 'ssm_c_re': 5.144710e-03, 'ssm_c_im': 5.118836e-03, 'ssm_d': 7.887176e-02, 'w_glu': 2.241028e-02, 'b_glu': 3.394632e-02, 'w_ssm_br': 5.620694e-02, 'w_attn_br': 2.721072e-02, 'w_mem_kv': 2.173365e-02, 'w_mem_br': 1.520775e-02, 'w_o': 6.016021e-02, 'norm2_g': 2.395918e-01, 'w_up': 1.119098e-01, 'w_down': 2.137061e-01, 'final_g': 6.457905e+01}


def _to_microbatches(a, axis):
    t = _jnp.moveaxis(a, axis, 0)
    t = t.reshape((N_MICROBATCH, t.shape[0] // N_MICROBATCH) + t.shape[1:])
    return _jnp.moveaxis(t, 1, axis + 1)


def setup_inputs(seed: int = 0) -> dict:
    inp = _fwd_setup_inputs(seed)
    key = _jax.random.fold_in(_jax.random.key(seed), 7919)
    shape, _ = _output_shape()
    out = dict(inp)
    out["loss_target"] = _jax.random.normal(_jax.random.fold_in(key, 0), shape, _jnp.float32)
    for i, name in enumerate(TWIN_WEIGHTS):
        w = inp[name].astype(_jnp.float32)
        if MOMENT_SCALE is None:
            s = _jnp.sqrt(_jnp.mean(_jnp.square(w)) + 1e-30)
        else:
            s = MOMENT_SCALE[name]
        km, kv = _jax.random.split(_jax.random.fold_in(key, i + 1))
        out[name] = w
        out["m_" + name] = s * _jax.random.normal(km, w.shape, _jnp.float32)
        out["v_" + name] = (s * s) * _jax.random.uniform(kv, w.shape, _jnp.float32, 0.5, 1.5)
    if N_MICROBATCH > 1:
        for name, axis in PER_EXAMPLE_BATCH_AXIS.items():
            out[name] = _to_microbatches(out[name], axis)
    return {'x': out['x'], 'mem': out['mem'], 'norm1_g': out['norm1_g'], 'mem_norm_g': out['mem_norm_g'], 'w_in': out['w_in'], 'b_gate': out['b_gate'], 'ssm_lambda_re': out['ssm_lambda_re'], 'ssm_lambda_im': out['ssm_lambda_im'], 'ssm_log_dt': out['ssm_log_dt'], 'ssm_b_re': out['ssm_b_re'], 'ssm_b_im': out['ssm_b_im'], 'ssm_c_re': out['ssm_c_re'], 'ssm_c_im': out['ssm_c_im'], 'ssm_d': out['ssm_d'], 'w_glu': out['w_glu'], 'b_glu': out['b_glu'], 'w_ssm_br': out['w_ssm_br'], 'w_attn_br': out['w_attn_br'], 'w_mem_kv': out['w_mem_kv'], 'w_mem_br': out['w_mem_br'], 'w_o': out['w_o'], 'norm2_g': out['norm2_g'], 'w_up': out['w_up'], 'w_down': out['w_down'], 'final_g': out['final_g'], 'loss_target': out['loss_target'], 'm_norm1_g': out['m_norm1_g'], 'm_mem_norm_g': out['m_mem_norm_g'], 'm_w_in': out['m_w_in'], 'm_b_gate': out['m_b_gate'], 'm_ssm_lambda_re': out['m_ssm_lambda_re'], 'm_ssm_lambda_im': out['m_ssm_lambda_im'], 'm_ssm_log_dt': out['m_ssm_log_dt'], 'm_ssm_b_re': out['m_ssm_b_re'], 'm_ssm_b_im': out['m_ssm_b_im'], 'm_ssm_c_re': out['m_ssm_c_re'], 'm_ssm_c_im': out['m_ssm_c_im'], 'm_ssm_d': out['m_ssm_d'], 'm_w_glu': out['m_w_glu'], 'm_b_glu': out['m_b_glu'], 'm_w_ssm_br': out['m_w_ssm_br'], 'm_w_attn_br': out['m_w_attn_br'], 'm_w_mem_kv': out['m_w_mem_kv'], 'm_w_mem_br': out['m_w_mem_br'], 'm_w_o': out['m_w_o'], 'm_norm2_g': out['m_norm2_g'], 'm_w_up': out['m_w_up'], 'm_w_down': out['m_w_down'], 'm_final_g': out['m_final_g'], 'v_norm1_g': out['v_norm1_g'], 'v_mem_norm_g': out['v_mem_norm_g'], 'v_w_in': out['v_w_in'], 'v_b_gate': out['v_b_gate'], 'v_ssm_lambda_re': out['v_ssm_lambda_re'], 'v_ssm_lambda_im': out['v_ssm_lambda_im'], 'v_ssm_log_dt': out['v_ssm_log_dt'], 'v_ssm_b_re': out['v_ssm_b_re'], 'v_ssm_b_im': out['v_ssm_b_im'], 'v_ssm_c_re': out['v_ssm_c_re'], 'v_ssm_c_im': out['v_ssm_c_im'], 'v_ssm_d': out['v_ssm_d'], 'v_w_glu': out['v_w_glu'], 'v_b_glu': out['v_b_glu'], 'v_w_ssm_br': out['v_w_ssm_br'], 'v_w_attn_br': out['v_w_attn_br'], 'v_w_mem_kv': out['v_w_mem_kv'], 'v_w_mem_br': out['v_w_mem_br'], 'v_w_o': out['v_w_o'], 'v_norm2_g': out['v_norm2_g'], 'v_w_up': out['v_w_up'], 'v_w_down': out['v_w_down'], 'v_final_g': out['v_final_g']}


def _loss(weights, diff, rest, loss_target):
    with _jax.named_scope("forward"):
        args = {**rest, TWIN_DIFF_INPUT: diff, **{k: w.astype(_WEIGHT_DTYPES[k]) for k, w in weights.items()}}
        y = _forward(args)
    with _jax.named_scope("loss_head"):
        err = _jnp.square(y.astype(_jnp.float32) - loss_target)
        return 0.5 * _jnp.sum(_jnp.mean(err, axis=-1)) if err.ndim else 0.5 * err


def _adamw(w, g, m, v):
    m = ADAM_B1 * m + (1.0 - ADAM_B1) * g
    v = ADAM_B2 * v + (1.0 - ADAM_B2) * _jnp.square(g)
    m_hat = m / (1.0 - ADAM_B1 ** ADAM_STEP)
    v_hat = v / (1.0 - ADAM_B2 ** ADAM_STEP)
    delta = -ADAM_LR * (m_hat / (_jnp.sqrt(v_hat) + ADAM_EPS) + ADAM_WD * w)
    return delta, m, v


def reference(x, mem, norm1_g, mem_norm_g, w_in, b_gate, ssm_lambda_re, ssm_lambda_im, ssm_log_dt, ssm_b_re, ssm_b_im, ssm_c_re, ssm_c_im, ssm_d, w_glu, b_glu, w_ssm_br, w_attn_br, w_mem_kv, w_mem_br, w_o, norm2_g, w_up, w_down, final_g, loss_target, m_norm1_g, m_mem_norm_g, m_w_in, m_b_gate, m_ssm_lambda_re, m_ssm_lambda_im, m_ssm_log_dt, m_ssm_b_re, m_ssm_b_im, m_ssm_c_re, m_ssm_c_im, m_ssm_d, m_w_glu, m_b_glu, m_w_ssm_br, m_w_attn_br, m_w_mem_kv, m_w_mem_br, m_w_o, m_norm2_g, m_w_up, m_w_down, m_final_g, v_norm1_g, v_mem_norm_g, v_w_in, v_b_gate, v_ssm_lambda_re, v_ssm_lambda_im, v_ssm_log_dt, v_ssm_b_re, v_ssm_b_im, v_ssm_c_re, v_ssm_c_im, v_ssm_d, v_w_glu, v_b_glu, v_w_ssm_br, v_w_attn_br, v_w_mem_kv, v_w_mem_br, v_w_o, v_norm2_g, v_w_up, v_w_down, v_final_g):
    given = dict(x=x, mem=mem, norm1_g=norm1_g, mem_norm_g=mem_norm_g, w_in=w_in, b_gate=b_gate, ssm_lambda_re=ssm_lambda_re, ssm_lambda_im=ssm_lambda_im, ssm_log_dt=ssm_log_dt, ssm_b_re=ssm_b_re, ssm_b_im=ssm_b_im, ssm_c_re=ssm_c_re, ssm_c_im=ssm_c_im, ssm_d=ssm_d, w_glu=w_glu, b_glu=b_glu, w_ssm_br=w_ssm_br, w_attn_br=w_attn_br, w_mem_kv=w_mem_kv, w_mem_br=w_mem_br, w_o=w_o, norm2_g=norm2_g, w_up=w_up, w_down=w_down, final_g=final_g, loss_target=loss_target, m_norm1_g=m_norm1_g, m_mem_norm_g=m_mem_norm_g, m_w_in=m_w_in, m_b_gate=m_b_gate, m_ssm_lambda_re=m_ssm_lambda_re, m_ssm_lambda_im=m_ssm_lambda_im, m_ssm_log_dt=m_ssm_log_dt, m_ssm_b_re=m_ssm_b_re, m_ssm_b_im=m_ssm_b_im, m_ssm_c_re=m_ssm_c_re, m_ssm_c_im=m_ssm_c_im, m_ssm_d=m_ssm_d, m_w_glu=m_w_glu, m_b_glu=m_b_glu, m_w_ssm_br=m_w_ssm_br, m_w_attn_br=m_w_attn_br, m_w_mem_kv=m_w_mem_kv, m_w_mem_br=m_w_mem_br, m_w_o=m_w_o, m_norm2_g=m_norm2_g, m_w_up=m_w_up, m_w_down=m_w_down, m_final_g=m_final_g, v_norm1_g=v_norm1_g, v_mem_norm_g=v_mem_norm_g, v_w_in=v_w_in, v_b_gate=v_b_gate, v_ssm_lambda_re=v_ssm_lambda_re, v_ssm_lambda_im=v_ssm_lambda_im, v_ssm_log_dt=v_ssm_log_dt, v_ssm_b_re=v_ssm_b_re, v_ssm_b_im=v_ssm_b_im, v_ssm_c_re=v_ssm_c_re, v_ssm_c_im=v_ssm_c_im, v_ssm_d=v_ssm_d, v_w_glu=v_w_glu, v_b_glu=v_b_glu, v_w_ssm_br=v_w_ssm_br, v_w_attn_br=v_w_attn_br, v_w_mem_kv=v_w_mem_kv, v_w_mem_br=v_w_mem_br, v_w_o=v_w_o, v_norm2_g=v_norm2_g, v_w_up=v_w_up, v_w_down=v_w_down, v_final_g=v_final_g)
    weights = {n: given[n] for n in TWIN_WEIGHTS}
    shared = {n: given[n] for n in SHARED_INPUTS}
    per_example = {n: given[n] for n in ['x', 'mem']}
    grad_fn = _jax.value_and_grad(_loss, argnums=(0, 1))

    def one_microbatch(ex, loss_target):
        ex = dict(ex)
        diff = ex.pop(TWIN_DIFF_INPUT)
        return grad_fn(weights, diff, {**shared, **ex}, loss_target)

    if N_MICROBATCH == 1:
        loss, (grad_w, grad_x) = one_microbatch(per_example, given["loss_target"])
    else:
        def body(carry, xs):
            loss_sum, grad_sum = carry
            l_k, (gw_k, gx_k) = one_microbatch(xs[0], xs[1])
            with _jax.named_scope("update"):
                return (loss_sum + l_k, _jax.tree.map(_jnp.add, grad_sum, gw_k)), gx_k

        init = (_jnp.zeros((), _jnp.float32), _jax.tree.map(_jnp.zeros_like, weights))
        (loss, grad_w), grad_x = _jax.lax.scan(body, init, (per_example, given["loss_target"]))
    with _jax.named_scope("update"):
        delta_w, new_m, new_v = {}, {}, {}
        for n in TWIN_WEIGHTS:
            delta_w[n], new_m[n], new_v[n] = _adamw(weights[n], grad_w[n], given["m_" + n], given["v_" + n])
    return (loss, grad_x, *[grad_w[n] for n in TWIN_WEIGHTS], *[delta_w[n] for n in TWIN_WEIGHTS],
            *[new_m[n] for n in TWIN_WEIGHTS], *[new_v[n] for n in TWIN_WEIGHTS])
```

```python
import math

import numpy as np
import jax
import jax.numpy as jnp
from jax import lax
from jax.experimental import pallas as pl
from jax.experimental.pallas import tpu as pltpu

F32 = jnp.float32
BF16 = jnp.bfloat16
_MXU = jnp.bfloat16

D_MODEL = 1024
SSM_G, SSM_H, SSM_P = 32, 16, 64
SSM_W = SSM_G * SSM_H
SSM_S = SSM_G * SSM_P
ATT_E = 64
ATT_HG = 4
ATT_GW = ATT_HG * ATT_E
ATT_WIN = 128
DILATIONS = (1, 4, 16)
MEM_H, MEM_E = 4, 128
MEM_W = MEM_H * MEM_E
ZA_W = SSM_W + 9 * ATT_GW + MEM_W
ZG_W = 3 * D_MODEL
IN_W = ZA_W + ZG_W
RMS_EPS = 1e-6
NEG_INF = -1e30

ADAM_LR, ADAM_B1, ADAM_B2, ADAM_EPS, ADAM_WD, ADAM_STEP = 0.001, 0.9, 0.999, 1e-08, 0.01, 10

N_DEV = 8
PACK_C = 512
_VMEM_LIMIT = 56 * 1024 * 1024
SCAN_TB = 128
SCAN_W = 256
SCAN_NT = SSM_S // SCAN_W

BIG = ("w_in", "w_glu", "w_ssm_br", "w_attn_br", "w_mem_kv", "w_mem_br", "w_o", "w_up", "w_down")
BIG_SHAPE = {
    "w_in": (D_MODEL, IN_W, 1), "w_glu": (SSM_W, SSM_W, 0), "w_ssm_br": (SSM_W, D_MODEL, 1),
    "w_attn_br": (ATT_GW, D_MODEL, 1), "w_mem_kv": (D_MODEL, 2 * MEM_W, 0), "w_mem_br": (MEM_W, D_MODEL, 1),
    "w_o": (D_MODEL, D_MODEL, 0), "w_up": (D_MODEL, 4 * D_MODEL, 1), "w_down": (4 * D_MODEL, D_MODEL, 0),
}
SMALL = ("norm1_g", "mem_norm_g", "b_gate", "ssm_lambda_re", "ssm_lambda_im", "ssm_log_dt", "ssm_b_re",
         "ssm_b_im", "ssm_c_re", "ssm_c_im", "ssm_d", "b_glu", "norm2_g", "final_g")
ALL_W = ("norm1_g", "mem_norm_g", "w_in", "b_gate", "ssm_lambda_re", "ssm_lambda_im", "ssm_log_dt", "ssm_b_re",
         "ssm_b_im", "ssm_c_re", "ssm_c_im", "ssm_d", "w_glu", "b_glu", "w_ssm_br", "w_attn_br", "w_mem_kv",
         "w_mem_br", "w_o", "norm2_g", "w_up", "w_down", "final_g")


def _params(sem):
    return pltpu.CompilerParams(dimension_semantics=sem, vmem_limit_bytes=_VMEM_LIMIT)


def _pick(n, cap):
    if n <= cap:
        return n
    t = (cap // 128) * 128
    while n % t:
        t -= 128
    return t


def _mm(a, b, outs, *, name, ta=False, tb=False, epi=None, mn=(), rows=(), tm=1024, tn=1024, tk=512):
    m = a.shape[1] if ta else a.shape[0]
    k = a.shape[0] if ta else a.shape[1]
    n = b.shape[0] if tb else b.shape[1]
    assert k == (b.shape[1] if tb else b.shape[0]), (name, a.shape, b.shape)
    tm, tn, tk = _pick(m, tm), _pick(n, tn), _pick(k, tk)
    nk = k // tk
    a_spec = pl.BlockSpec((tk, tm), lambda i, j, kk: (kk, i)) if ta else pl.BlockSpec((tm, tk), lambda i, j, kk: (i, kk))
    b_spec = pl.BlockSpec((tn, tk), lambda i, j, kk: (j, kk)) if tb else pl.BlockSpec((tk, tn), lambda i, j, kk: (kk, j))
    mn_spec = pl.BlockSpec((tm, tn), lambda i, j, kk: (i, j))
    row_spec = pl.BlockSpec((1, tn), lambda i, j, kk: (0, j))
    n_ex, n_out = len(mn) + len(rows), len(outs)
    dims = (((0 if ta else 1,), (1 if tb else 0,)), ((), ()))

    def body(a_ref, b_ref, *rest):
        ex, o_refs, acc = rest[:n_ex], rest[n_ex:n_ex + n_out], rest[-1]
        kk = pl.program_id(2)

        @pl.when(kk == 0)
        def _():
            acc[...] = jnp.zeros_like(acc)

        acc[...] += lax.dot_general(a_ref[...].astype(_MXU), b_ref[...].astype(_MXU), dims,
                                    preferred_element_type=F32)

        @pl.when(kk == nk - 1)
        def _():
            vals = (acc[...],) if epi is None else epi(acc[...], *[r[...] for r in ex])
            for r, v in zip(o_refs, vals):
                r[...] = v.astype(r.dtype)

    res = pl.pallas_call(
        body, grid=(m // tm, n // tn, nk),
        in_specs=[a_spec, b_spec] + [mn_spec] * len(mn) + [row_spec] * len(rows),
        out_specs=[mn_spec] * n_out,
        out_shape=[jax.ShapeDtypeStruct((m, n), dt) for dt in outs],
        scratch_shapes=[pltpu.VMEM((tm, tn), F32)],
        compiler_params=_params(("parallel", "parallel", "arbitrary")), name=name,
    )(a, b, *mn, *rows)
    return res[0] if n_out == 1 else res


def _ew(fn, rows, bcs, out_rows, out_accs, *, name, tm=256):
    r = rows[0].shape[0]
    tm = min(tm, r)
    assert r % tm == 0
    nr, nb, no, na = len(rows), len(bcs), len(out_rows), len(out_accs)

    def body(*refs):
        i = pl.program_id(0)
        r_in, b_in = refs[:nr], refs[nr:nr + nb]
        o_r, o_a = refs[nr + nb:nr + nb + no], refs[nr + nb + no:]
        outs, accs = fn([x[...] for x in r_in], [x[...] for x in b_in])
        for ref, v in zip(o_r, outs):
            ref[...] = v.astype(ref.dtype)
        if na:
            @pl.when(i == 0)
            def _():
                for ref in o_a:
                    ref[...] = jnp.zeros_like(ref)

            for ref, v in zip(o_a, accs):
                ref[...] += v

    res = pl.pallas_call(
        body, grid=(r // tm,),
        in_specs=[pl.BlockSpec((tm, x.shape[1]), lambda i: (i, 0)) for x in rows]
        + [pl.BlockSpec((1, x.shape[1]), lambda i: (0, 0)) for x in bcs],
        out_specs=[pl.BlockSpec((tm, c), lambda i: (i, 0)) for c, _ in out_rows]
        + [pl.BlockSpec((1, c), lambda i: (0, 0)) for c in out_accs],
        out_shape=[jax.ShapeDtypeStruct((r, c), dt) for c, dt in out_rows]
        + [jax.ShapeDtypeStruct((1, c), F32) for c in out_accs],
        compiler_params=_params(("arbitrary",)), name=name,
    )(*rows, *bcs)
    return res


def _colsum(x):
    return jnp.sum(x, axis=0, keepdims=True)


def _sigmoid(x):
    return 1.0 / (1.0 + jnp.exp(-x))


def _rms_fwd(x, g, name):
    def fn(r, b):
        xv = r[0]
        rs = lax.rsqrt(jnp.mean(xv * xv, axis=-1, keepdims=True) + RMS_EPS)
        return [xv * rs * b[0]], []
    return _ew(fn, [x], [g], [(x.shape[1], BF16)], [], name=name)[0]


def _rms_bwd(x, dn, res, g, name):
    def fn(r, b):
        xv, dv = r[0], r[1]
        rs = lax.rsqrt(jnp.mean(xv * xv, axis=-1, keepdims=True) + RMS_EPS)
        gd = dv * b[0]
        dx = rs * gd - xv * (rs * rs * rs) * jnp.mean(gd * xv, axis=-1, keepdims=True)
        if res is not None:
            dx = dx + r[2]
        return [dx], [_colsum(dv * xv * rs)]
    rows = [x, dn] + ([res] if res is not None else [])
    return _ew(fn, rows, [g], [(x.shape[1], F32)], [x.shape[1]], name=name)


def _cscan(br, bi, ar, ai, reverse):
    tb = br.shape[0]
    row = lax.broadcasted_iota(jnp.int32, br.shape, 0)
    d = 1
    while d < tb:
        keep = (row < tb - d) if reverse else (row >= d)
        shift = tb - d if reverse else d
        sr = jnp.where(keep, pltpu.roll(br, shift, 0), 0.0)
        si = jnp.where(keep, pltpu.roll(bi, shift, 0), 0.0)
        br, bi = br + ar * sr - ai * si, bi + ar * si + ai * sr
        ar, ai = ar * ar - ai * ai, 2.0 * ar * ai
        d *= 2
    return br, bi


def _ssm_scan(b, a_cat, *, reverse, s_fwd=None, name):
    l = b.shape[0]
    tb, w = SCAN_TB, SCAN_W
    nt = l // tb
    with_da = s_fwd is not None
    edge = tb - 1 if reverse else 0

    def tt(t):
        return nt - 1 - t if reverse else t

    def body(*refs):
        if with_da:
            b_ref, a_ref, sf_ref, sp_ref, s_ref, da_ref, p_ref, c_ref, acc_ref = refs
        else:
            b_ref, a_ref, s_ref, p_ref, c_ref = refs
        t = pl.program_id(1)
        ar, ai = a_ref[:, :w], a_ref[:, w:]
        row = lax.broadcasted_iota(jnp.int32, (tb, w), 0)

        @pl.when(t == 0)
        def _():
            imp = row == edge
            pr, pi = _cscan(jnp.where(imp, ar, 0.0), jnp.where(imp, ai, 0.0), ar, ai, reverse)
            p_ref[:, :w] = pr
            p_ref[:, w:] = pi
            c_ref[...] = jnp.zeros_like(c_ref)
            if with_da:
                acc_ref[...] = jnp.zeros_like(acc_ref)

        lr, li = _cscan(b_ref[:, :w], b_ref[:, w:], ar, ai, reverse)
        cr, ci = c_ref[:, :w], c_ref[:, w:]
        pr, pi = p_ref[:, :w], p_ref[:, w:]
        sr = lr + pr * cr - pi * ci
        si = li + pr * ci + pi * cr
        s_ref[:, :w] = sr
        s_ref[:, w:] = si
        last = tb - 1 - edge
        c_ref[:, :w] = sr[last:last + 1, :]
        c_ref[:, w:] = si[last:last + 1, :]

        if with_da:
            first_block = tt(t) == 0
            pvr = jnp.where(first_block, 0.0, sp_ref[7:8, :w])
            pvi = jnp.where(first_block, 0.0, sp_ref[7:8, w:])
            fr = jnp.where(row == 0, pvr, pltpu.roll(sf_ref[:, :w], 1, 0))
            fi = jnp.where(row == 0, pvi, pltpu.roll(sf_ref[:, w:], 1, 0))
            acc_ref[:, :w] += sr * fr + si * fi
            acc_ref[:, w:] += si * fr - sr * fi

            @pl.when(t == nt - 1)
            def _():
                da_ref[...] = jnp.sum(acc_ref[...], axis=0, keepdims=True)

    blk = pl.BlockSpec((tb, 2 * w), lambda j, t: (tt(t), j))
    a_spec = pl.BlockSpec((1, 2 * w), lambda j, t: (0, j))
    in_specs, args = [blk, a_spec], [b, a_cat]
    out_specs, out_shape = [blk], [jax.ShapeDtypeStruct((l, 2 * SSM_S), F32)]
    scratch = [pltpu.VMEM((tb, 2 * w), F32), pltpu.VMEM((1, 2 * w), F32)]
    if with_da:
        in_specs += [blk, pl.BlockSpec((8, 2 * w), lambda j, t: (jnp.maximum(tt(t) * (tb // 8) - 1, 0), j))]
        args += [s_fwd, s_fwd]
        out_specs.append(a_spec)
        out_shape.append(jax.ShapeDtypeStruct((1, 2 * SSM_S), F32))
        scratch.append(pltpu.VMEM((tb, 2 * w), F32))
    res = pl.pallas_call(
        body, grid=(SCAN_NT, nt), in_specs=in_specs, out_specs=out_specs, out_shape=out_shape,
        scratch_shapes=scratch, compiler_params=_params(("parallel", "arbitrary")), name=name,
    )(*args)
    return res if with_da else res[0]


def _nt_dot(x, y):
    return lax.dot_general(x.astype(_MXU), y.astype(_MXU), (((1,), (1,)), ((), ())), preferred_element_type=F32)


def _tn_dot(x, y):
    return lax.dot_general(x.astype(_MXU), y.astype(_MXU), (((0,), (0,)), ((), ())), preferred_element_type=F32)


def _nn_dot(x, y):
    return jnp.dot(x.astype(_MXU), y.astype(_MXU), preferred_element_type=F32)


def _attn_masks(i, nb):
    qi = lax.broadcasted_iota(jnp.int32, (ATT_WIN, ATT_WIN), 0)
    kj = lax.broadcasted_iota(jnp.int32, (ATT_WIN, ATT_WIN), 1)
    has_prev = (i % nb) != 0
    return kj <= qi, jnp.logical_and(kj >= qi, has_prev)


def _attn_specs(l):
    cur = pl.BlockSpec((ATT_WIN, ATT_GW), lambda i: (i, 0))
    prev = pl.BlockSpec((ATT_WIN, ATT_GW), lambda i: (jnp.maximum(i - 1, 0), 0))
    return cur, prev


def _attn_fwd(q, k, v, nb, name):
    l = q.shape[0]
    scale = ATT_E ** -0.5

    def body(q_ref, kc_ref, kp_ref, vc_ref, vp_ref, o_ref, lse_ref):
        mask_c, mask_p = _attn_masks(pl.program_id(0), nb)
        for h in range(ATT_HG):
            sl = slice(h * ATT_E, (h + 1) * ATT_E)
            qh = q_ref[:, sl]
            sc = jnp.where(mask_c, _nt_dot(qh, kc_ref[:, sl]) * scale, NEG_INF)
            sp = jnp.where(mask_p, _nt_dot(qh, kp_ref[:, sl]) * scale, NEG_INF)
            mx = jnp.maximum(jnp.max(sc, axis=-1, keepdims=True), jnp.max(sp, axis=-1, keepdims=True))
            pc, pp = jnp.exp(sc - mx), jnp.exp(sp - mx)
            den = jnp.sum(pc, axis=-1, keepdims=True) + jnp.sum(pp, axis=-1, keepdims=True)
            o_ref[:, sl] = (_nn_dot(pc, vc_ref[:, sl]) + _nn_dot(pp, vp_ref[:, sl])) / den
            lse_ref[:, sl] = jnp.broadcast_to(mx + jnp.log(den), (ATT_WIN, ATT_E))

    cur, prev = _attn_specs(l)
    return pl.pallas_call(
        body, grid=(l // ATT_WIN,), in_specs=[cur, cur, prev, cur, prev], out_specs=[cur, cur],
        out_shape=[jax.ShapeDtypeStruct((l, ATT_GW), F32)] * 2,
        compiler_params=_params(("parallel",)), name=name,
    )(q, k, k, v, v)


def _attn_bwd(q, k, v, do, lse, dd, nb, name):
    l = q.shape[0]
    scale = ATT_E ** -0.5

    def body(q_ref, kc_ref, kp_ref, vc_ref, vp_ref, do_ref, lse_ref, dd_ref, dq_ref, dkc_ref, dkp_ref, dvc_ref, dvp_ref):
        mask_c, mask_p = _attn_masks(pl.program_id(0), nb)
        for h in range(ATT_HG):
            sl = slice(h * ATT_E, (h + 1) * ATT_E)
            qh, doh = q_ref[:, sl], do_ref[:, sl]
            kc, kp, vc, vp = kc_ref[:, sl], kp_ref[:, sl], vc_ref[:, sl], vp_ref[:, sl]
            lh = lse_ref[:, h * ATT_E:h * ATT_E + 1]
            dh = dd_ref[:, h * ATT_E:h * ATT_E + 1]
            pc = jnp.where(mask_c, jnp.exp(_nt_dot(qh, kc) * scale - lh), 0.0)
            pp = jnp.where(mask_p, jnp.exp(_nt_dot(qh, kp) * scale - lh), 0.0)
            dsc = pc * (_nt_dot(doh, vc) - dh) * scale
            dsp = pp * (_nt_dot(doh, vp) - dh) * scale
            dq_ref[:, sl] = (_nn_dot(dsc, kc) + _nn_dot(dsp, kp)).astype(dq_ref.dtype)
            dkc_ref[:, sl] = _tn_dot(dsc, qh)
            dkp_ref[:, sl] = _tn_dot(dsp, qh)
            dvc_ref[:, sl] = _tn_dot(pc, doh)
            dvp_ref[:, sl] = _tn_dot(pp, doh)

    cur, prev = _attn_specs(l)
    return pl.pallas_call(
        body, grid=(l // ATT_WIN,), in_specs=[cur, cur, prev, cur, prev, cur, cur, cur], out_specs=[cur] * 5,
        out_shape=[jax.ShapeDtypeStruct((l, ATT_GW), BF16)] + [jax.ShapeDtypeStruct((l, ATT_GW), F32)] * 4,
        compiler_params=_params(("parallel",)), name=name,
    )(q, k, k, v, v, do, lse, dd)


def _attn_kv_combine(dkc, dkp, dvc, dvp, nb, name):
    l = dkc.shape[0]
    nblk = l // ATT_WIN

    def body(kc_ref, kp_ref, vc_ref, vp_ref, dk_ref, dv_ref):
        has_next = ((pl.program_id(0) + 1) % nb) != 0
        dk_ref[...] = (kc_ref[...] + jnp.where(has_next, kp_ref[...], 0.0)).astype(dk_ref.dtype)
        dv_ref[...] = (vc_ref[...] + jnp.where(has_next, vp_ref[...], 0.0)).astype(dv_ref.dtype)

    cur = pl.BlockSpec((ATT_WIN, ATT_GW), lambda i: (i, 0))
    nxt = pl.BlockSpec((ATT_WIN, ATT_GW), lambda i: (jnp.minimum(i + 1, nblk - 1), 0))
    return pl.pallas_call(
        body, grid=(nblk,), in_specs=[cur, nxt, cur, nxt], out_specs=[cur, cur],
        out_shape=[jax.ShapeDtypeStruct((l, ATT_GW), BF16)] * 2,
        compiler_params=_params(("parallel",)), name=name,
    )(dkc, dkp, dvc, dvp)


def _to_perm(a, d):
    if d == 1:
        return a
    l, c = a.shape
    return a.reshape(l // d, d, c).transpose(1, 0, 2).reshape(l, c)


def _from_perm(a, d):
    if d == 1:
        return a
    l, c = a.shape
    return a.reshape(d, l // d, c).transpose(1, 0, 2).reshape(l, c)


def _mem_probs(qh, kh):
    s = _nt_dot(qh, kh) * (MEM_E ** -0.5)
    e = jnp.exp(s - jnp.max(s, axis=-1, keepdims=True))
    return e / jnp.sum(e, axis=-1, keepdims=True)


def _mem_fwd(mq, kv, name, tm=512):
    l, nm = mq.shape[0], kv.shape[0]

    def body(q_ref, kv_ref, o_ref):
        for h in range(MEM_H):
            sl = slice(h * MEM_E, (h + 1) * MEM_E)
            p = _mem_probs(q_ref[:, sl], kv_ref[:, sl])
            o_ref[:, sl] = _nn_dot(p, kv_ref[:, MEM_W + h * MEM_E:MEM_W + (h + 1) * MEM_E]).astype(o_ref.dtype)

    return pl.pallas_call(
        body, grid=(l // tm,),
        in_specs=[pl.BlockSpec((tm, MEM_W), lambda i: (i, 0)), pl.BlockSpec((nm, 2 * MEM_W), lambda i: (0, 0))],
        out_specs=pl.BlockSpec((tm, MEM_W), lambda i: (i, 0)),
        out_shape=jax.ShapeDtypeStruct((l, MEM_W), BF16),
        compiler_params=_params(("parallel",)), name=name,
    )(mq, kv)


def _mem_bwd(mq, kv, dmo, name, tm=512):
    l, nm = mq.shape[0], kv.shape[0]
    scale = MEM_E ** -0.5

    def body(q_ref, kv_ref, do_ref, dq_ref, dkv_ref):
        @pl.when(pl.program_id(0) == 0)
        def _():
            dkv_ref[...] = jnp.zeros_like(dkv_ref)

        for h in range(MEM_H):
            sl = slice(h * MEM_E, (h + 1) * MEM_E)
            vsl = slice(MEM_W + h * MEM_E, MEM_W + (h + 1) * MEM_E)
            qh, kh, vh, doh = q_ref[:, sl], kv_ref[:, sl], kv_ref[:, vsl], do_ref[:, sl]
            p = _mem_probs(qh, kh)
            dp = _nt_dot(doh, vh)
            ds = p * (dp - jnp.sum(dp * p, axis=-1, keepdims=True)) * scale
            dq_ref[:, sl] = _nn_dot(ds, kh).astype(dq_ref.dtype)
            dkv_ref[:, sl] += _tn_dot(ds, qh)
            dkv_ref[:, vsl] += _tn_dot(p, doh)

    row = pl.BlockSpec((tm, MEM_W), lambda i: (i, 0))
    full = pl.BlockSpec((nm, 2 * MEM_W), lambda i: (0, 0))
    return pl.pallas_call(
        body, grid=(l // tm,), in_specs=[row, full, row], out_specs=[row, full],
        out_shape=[jax.ShapeDtypeStruct((l, MEM_W), BF16), jax.ShapeDtypeStruct((nm, 2 * MEM_W), F32)],
        compiler_params=_params(("arbitrary",)), name=name,
    )(mq, kv, dmo)


def _discretize(lam_re, lam_im, log_dt, b_re, b_im):
    dt = jnp.exp(log_dt)[:, None]
    mag = jnp.exp(lam_re * dt)
    a_re, a_im = mag * jnp.cos(lam_im * dt), mag * jnp.sin(lam_im * dt)
    nr, ni = a_re - 1.0, a_im
    den = lam_re * lam_re + lam_im * lam_im
    coef_re = (nr * lam_re + ni * lam_im) / den
    coef_im = (ni * lam_re - nr * lam_im) / den
    bb_re = coef_re[..., None] * b_re - coef_im[..., None] * b_im
    bb_im = coef_re[..., None] * b_im + coef_im[..., None] * b_re
    return a_re, a_im, bb_re, bb_im


def _tile_cat(re, im):
    lead = re.shape[:-1]
    t = jnp.stack([re.reshape(*lead, SCAN_NT, SCAN_W), im.reshape(*lead, SCAN_NT, SCAN_W)], axis=-2)
    return t.reshape(*lead, 2 * SSM_S)


def _tile_split(x):
    lead = x.shape[:-1]
    t = x.reshape(*lead, SCAN_NT, 2, SCAN_W)
    return t[..., 0, :].reshape(*lead, SSM_S), t[..., 1, :].reshape(*lead, SSM_S)


def _bd_in(bb):
    return jnp.einsum("gph,gk->ghkp", bb, jnp.eye(SSM_G, dtype=bb.dtype)).reshape(SSM_W, SSM_S)


def _bd_in_diag(x):
    idx = jnp.arange(SSM_G)
    return x.reshape(SSM_G, SSM_H, SSM_G, SSM_P)[idx, :, idx, :].transpose(0, 2, 1)


_ANY = pl.BlockSpec(memory_space=pl.ANY)
_MESH = pl.DeviceIdType.MESH


def _allgather(x, name):
    def body(x_ref, out_ref, send_sems, recv_sems, local_sem):
        mx, my, mc = lax.axis_index("x"), lax.axis_index("y"), lax.axis_index("c")
        me, sibling = (mx, my, mc), (mx, my, 1 - mc)
        chips = [(1 - mx, my), (mx, 1 - my), (1 - mx, 1 - my)]

        def blk(px, py, pc):
            return out_ref.at[4 * px + 2 * py + pc]

        def copy(k, block, to, src=None):
            return pltpu.make_async_remote_copy(
                src_ref=blk(*block) if src is None else src, dst_ref=blk(*block),
                send_sem=send_sems.at[k], recv_sem=recv_sems.at[k], device_id=to, device_id_type=_MESH)

        mine = pltpu.make_async_copy(x_ref, blk(*me), local_sem)
        mine.start()
        first = [copy(0, me, sibling, src=x_ref)]
        first += [copy(1 + j, me, (*chip, mc), src=x_ref) for j, chip in enumerate(chips)]
        for cp in first:
            cp.start()
        passed = [copy(4 + j, (*chip, mc), sibling) for j, chip in enumerate(chips)]
        for j, chip in enumerate(chips):
            copy(1 + j, (*chip, mc), me).wait_recv()
            passed[j].start()
        copy(0, sibling, me).wait_recv()
        for j, chip in enumerate(chips):
            copy(4 + j, (*chip, 1 - mc), me).wait_recv()
        for cp in first + passed:
            cp.wait_send()
        mine.wait()

    return pl.pallas_call(
        body, out_shape=jax.ShapeDtypeStruct((N_DEV,) + x.shape, x.dtype), in_specs=[_ANY], out_specs=_ANY,
        scratch_shapes=[pltpu.SemaphoreType.DMA((7,)), pltpu.SemaphoreType.DMA((7,)), pltpu.SemaphoreType.DMA],
        name=name,
    )(x)


def _pair_exchange(g, name):
    def body(g_ref, out_ref, send_sems, recv_sems):
        mx, my, mc = lax.axis_index("x"), lax.axis_index("y"), lax.axis_index("c")
        copies = [pltpu.make_async_remote_copy(
            src_ref=g_ref.at[2 * k + (1 - mc)], dst_ref=out_ref.at[k], send_sem=send_sems.at[k],
            recv_sem=recv_sems.at[k], device_id=(mx, my, 1 - mc), device_id_type=_MESH) for k in range(4)]
        for cp in copies:
            cp.start()
        for cp in copies:
            cp.wait()

    return pl.pallas_call(
        body, out_shape=jax.ShapeDtypeStruct((4,) + g.shape[1:], g.dtype), in_specs=[_ANY], out_specs=_ANY,
        scratch_shapes=[pltpu.SemaphoreType.DMA((4,)), pltpu.SemaphoreType.DMA((4,))], name=name,
    )(g)


def _chip_exchange(p, name):
    def body(p_ref, out_ref, send_sems, recv_sems):
        mx, my, mc = lax.axis_index("x"), lax.axis_index("y"), lax.axis_index("c")
        chips = [(1 - mx, my), (mx, 1 - my), (1 - mx, 1 - my)]
        copies = [pltpu.make_async_remote_copy(
            src_ref=p_ref.at[2 * px + py], dst_ref=out_ref.at[j], send_sem=send_sems.at[j],
            recv_sem=recv_sems.at[j], device_id=(px, py, mc), device_id_type=_MESH)
            for j, (px, py) in enumerate(chips)]
        for cp in copies:
            cp.start()
        for cp in copies:
            cp.wait()

    return pl.pallas_call(
        body, out_shape=jax.ShapeDtypeStruct((3,) + p.shape[1:], p.dtype), in_specs=[_ANY], out_specs=_ANY,
        scratch_shapes=[pltpu.SemaphoreType.DMA((3,)), pltpu.SemaphoreType.DMA((3,))], name=name,
    )(p)


def _pair_sum(g, t1, my_c, name, tr=568):
    _, r, c = g.shape

    def body(c_ref, g_ref, t_ref, o_ref):
        o_ref[...] = g_ref[...] + t_ref[...]

    return pl.pallas_call(
        body,
        grid_spec=pltpu.PrefetchScalarGridSpec(
            num_scalar_prefetch=1, grid=(4, r // tr),
            in_specs=[pl.BlockSpec((None, tr, c), lambda k, i, cr: (2 * k + cr[0], i, 0)),
                      pl.BlockSpec((None, tr, c), lambda k, i, cr: (k, i, 0))],
            out_specs=pl.BlockSpec((None, tr, c), lambda k, i, cr: (k, i, 0))),
        out_shape=jax.ShapeDtypeStruct((4, r, c), F32),
        compiler_params=_params(("parallel", "parallel")), name=name,
    )(my_c, g, t1)


def _adam_math(g, w, m, v):
    m = ADAM_B1 * m + (1.0 - ADAM_B1) * g
    v = ADAM_B2 * v + (1.0 - ADAM_B2) * (g * g)
    m_hat = m / (1.0 - ADAM_B1 ** ADAM_STEP)
    v_hat = v / (1.0 - ADAM_B2 ** ADAM_STEP)
    delta = -ADAM_LR * (m_hat / (jnp.sqrt(v_hat) + ADAM_EPS) + ADAM_WD * w)
    return delta, m, v


def _adam_big(p, t2, my_chip, w, m, v, name, tr=568):
    r, c = w.shape

    def body(k_ref, p_ref, t0_ref, t1_ref, t2_ref, w_ref, m_ref, v_ref, g_out, d_out, m_out, v_out):
        g = ((p_ref[...] + t0_ref[...]) + t1_ref[...]) + t2_ref[...]
        d, mn, vn = _adam_math(g, w_ref[...], m_ref[...], v_ref[...])
        g_out[...], d_out[...], m_out[...], v_out[...] = g, d, mn, vn

    flat = pl.BlockSpec((tr, c), lambda i, kr: (i, 0))

    def rel(j):
        return pl.BlockSpec((None, tr, c), lambda i, kr: (j, i, 0))

    return pl.pallas_call(
        body,
        grid_spec=pltpu.PrefetchScalarGridSpec(
            num_scalar_prefetch=1, grid=(r // tr,),
            in_specs=[pl.BlockSpec((None, tr, c), lambda i, kr: (kr[0], i, 0)), rel(0), rel(1), rel(2), flat, flat, flat],
            out_specs=[flat] * 4),
        out_shape=[jax.ShapeDtypeStruct((r, c), F32)] * 4,
        compiler_params=_params(("parallel",)), name=name,
    )(my_chip, p, t2, t2, t2, w, m, v)


def _sum8(g8, name):
    _, r, c = g8.shape

    def body(g_ref, o_ref):
        acc = g_ref[0]
        for j in range(1, N_DEV):
            acc = acc + g_ref[j]
        o_ref[...] = acc

    return pl.pallas_call(
        body, grid=(1,), in_specs=[pl.BlockSpec((N_DEV, r, c), lambda i: (0, 0, 0))],
        out_specs=pl.BlockSpec((r, c), lambda i: (0, 0)), out_shape=jax.ShapeDtypeStruct((r, c), F32),
        compiler_params=_params(("arbitrary",)), name=name,
    )(g8)


def _adam_small(g, w, m, v, name):
    def fn(r, b):
        return list(_adam_math(*r)), []
    c = g.shape[1]
    return _ew(fn, [g, w, m, v], [], [(c, F32)] * 3, [], name=name, tm=g.shape[0])


def _pack(arrs, pad_rows=8):
    flat = jnp.concatenate([a.reshape(-1) for a in arrs])
    n = flat.shape[0]
    q = PACK_C * pad_rows
    tot = -(-n // q) * q
    if tot != n:
        flat = jnp.concatenate([flat, jnp.zeros((tot - n,), flat.dtype)])
    return flat.reshape(tot // PACK_C, PACK_C)


def _unpack(buf, shapes):
    flat = buf.reshape(-1)
    out, off = [], 0
    for s in shapes:
        n = int(np.prod(s))
        out.append(flat[off:off + n].reshape(s))
        off += n
    return out


def _shards_of(full, name):
    r, c, ax = BIG_SHAPE[name]
    if ax == 0:
        return full.reshape(N_DEV, (r // N_DEV) * c)
    return full.reshape(r, N_DEV, c // N_DEV).transpose(1, 0, 2).reshape(N_DEV, r * (c // N_DEV))


def _full_of(shards, name):
    r, c, ax = BIG_SHAPE[name]
    if ax == 0:
        return shards.reshape(r, c)
    return shards.reshape(N_DEV, r, c // N_DEV).transpose(1, 0, 2).reshape(r, c)


def _shard_shape(name):
    r, c, ax = BIG_SHAPE[name]
    return (r // N_DEV, c) if ax == 0 else (r, c // N_DEV)


def _gelu_parts(x):
    c0, c1 = math.sqrt(2.0 / math.pi), 0.044715
    th = jnp.tanh(c0 * (x + c1 * x * x * x))
    return th, c0, c1


def _local_step(x, mem, tgt, wb, sp):
    l = x.shape[0]
    w_a, w_g = wb["w_in"][:, :ZA_W], wb["w_in"][:, ZA_W:]

    a_re, a_im, bb_re, bb_im = _discretize(sp["ssm_lambda_re"], sp["ssm_lambda_im"], sp["ssm_log_dt"],
                                           sp["ssm_b_re"], sp["ssm_b_im"])
    a_cat = _tile_cat(a_re.reshape(1, SSM_S), a_im.reshape(1, SSM_S))
    a_conj = _tile_cat(a_re.reshape(1, SSM_S), -a_im.reshape(1, SSM_S))
    bbd = _tile_cat(_bd_in(bb_re), _bd_in(bb_im)).astype(BF16)
    c_t = lambda c: _bd_in(c.transpose(0, 2, 1))
    cbd_t = _tile_cat(c_t(sp["ssm_c_re"]), -c_t(sp["ssm_c_im"])).astype(BF16)
    d_row = sp["ssm_d"].reshape(1, SSM_W)

    n1 = _rms_fwd(x, sp["norm1_g"], "rms1")
    za = _mm(n1, w_a, [BF16], name="in_proj_a", tn=1664)
    zg = _mm(n1, w_g, [F32], name="in_proj_g")
    u = za[:, :SSM_W]
    mq = za[:, ZA_W - MEM_W:]

    bu = _mm(u, bbd, [F32], name="ssm_bu")
    s_all = _ssm_scan(bu, a_cat, reverse=False, name="ssm_scan_fwd")
    ys = _mm(s_all, cbd_t, [F32], tb=True, name="ssm_cs")

    def gelu_fn(r, b):
        y0 = r[0] + b[0] * r[1].astype(F32)
        th, _, _ = _gelu_parts(y0)
        return [y0, 0.5 * y0 * (1.0 + th)], []
    y0, y1 = _ew(gelu_fn, [ys, u], [d_row], [(SSM_W, F32), (SSM_W, BF16)], [], name="ssm_gelu", tm=512)

    def glu_epi(acc, y1t, bg):
        t = acc + bg
        return t, y1t.astype(F32) * _sigmoid(t)
    t_glu, y2 = _mm(y1, wb["w_glu"], [F32, BF16], epi=glu_epi, mn=[y1], rows=[sp["b_glu"]], name="ssm_glu")
    br_ssm = _mm(y2, wb["w_ssm_br"], [F32], name="ssm_br")

    qkv_p, o_g, lse_g = [], [], []
    for g, d in enumerate(DILATIONS):
        nb = l // d // ATT_WIN
        cols = [za[:, SSM_W + (3 * j + g) * ATT_GW: SSM_W + (3 * j + g + 1) * ATT_GW] for j in range(3)]
        qp, kp, vp = [_to_perm(cc, d) for cc in cols]
        qkv_p.append((qp, kp, vp))
        og, lg = _attn_fwd(qp, kp, vp, nb, "attn_fwd%d" % g)
        o_g.append(_from_perm(og, d))
        lse_g.append(_from_perm(lg, d))

    def merge_fn(r, b):
        o0, o1, o2, l0, l1, l2 = r
        mx = jnp.maximum(jnp.maximum(l0, l1), l2)
        e0, e1, e2 = jnp.exp(l0 - mx), jnp.exp(l1 - mx), jnp.exp(l2 - mx)
        tot = e0 + e1 + e2
        return [(e0 * o0 + e1 * o1 + e2 * o2) / tot, mx + jnp.log(tot)], []
    o_att, lse_tot = _ew(merge_fn, o_g + lse_g, [], [(ATT_GW, F32), (ATT_GW, F32)], [], name="attn_merge", tm=512)
    br_attn = _mm(o_att, wb["w_attn_br"], [F32], name="attn_br")

    mn = _rms_fwd(mem, sp["mem_norm_g"], "rms_mem")
    kv = _mm(mn, wb["w_mem_kv"], [BF16], name="mem_kv")
    mo = _mem_fwd(mq, kv, "mem_attn_fwd")
    br_mem = _mm(mo, wb["w_mem_br"], [F32], name="mem_br")

    def gate_fn(r, b):
        zgt, b0, b1, b2 = r
        gt = _sigmoid(zgt + b[0])
        return [gt[:, :D_MODEL] * b0 + gt[:, D_MODEL:2 * D_MODEL] * b1 + gt[:, 2 * D_MODEL:] * b2], []
    merged = _ew(gate_fn, [zg, br_ssm, br_attn, br_mem], [sp["b_gate"]], [(D_MODEL, BF16)], [], name="gate_merge")[0]
    h1 = _mm(merged, wb["w_o"], [F32], epi=lambda acc, xt: (acc + xt,), mn=[x], name="o_proj")
    n2 = _rms_fwd(h1, sp["norm2_g"], "rms2")

    def up_epi(acc):
        ra = jnp.maximum(acc, 0.0)
        return ra * ra, ra
    f_act, r_act = _mm(n2, wb["w_up"], [BF16, BF16], epi=up_epi, name="mlp_up")
    h2 = _mm(f_act, wb["w_down"], [F32], epi=lambda acc, ht: (acc + ht,), mn=[h1], name="mlp_down")

    def final_fn(r, b):
        hv, tv = r
        gf = b[0]
        rs = lax.rsqrt(jnp.mean(hv * hv, axis=-1, keepdims=True) + RMS_EPS)
        err = hv * rs * gf - tv
        dy = err * (1.0 / D_MODEL)
        gd = dy * gf
        dh = rs * gd - hv * (rs * rs * rs) * jnp.mean(gd * hv, axis=-1, keepdims=True)
        loss = _colsum(jnp.sum(err * err, axis=-1, keepdims=True)) * (0.5 / D_MODEL)
        return [dh], [_colsum(dy * hv * rs), loss]
    dh2, d_final_g, loss = _ew(final_fn, [h2, tgt], [sp["final_g"]], [(D_MODEL, F32)], [D_MODEL, 1], name="final_loss")

    gw, gs = {}, {"final_g": d_final_g}
    d_act = _mm(dh2, wb["w_down"], [BF16], tb=True, epi=lambda acc, ra: (acc * 2.0 * ra.astype(F32),), mn=[r_act],
                name="mlp_down_dx")
    gw["w_down"] = _mm(f_act, dh2, [F32], ta=True, name="mlp_down_dw")
    dn2 = _mm(d_act, wb["w_up"], [F32], tb=True, name="mlp_up_dx")
    gw["w_up"] = _mm(n2, d_act, [F32], ta=True, name="mlp_up_dw")
    dh1, gs["norm2_g"] = _rms_bwd(h1, dn2, dh2, sp["norm2_g"], "rms2_bwd")
    dmerged = _mm(dh1, wb["w_o"], [F32], tb=True, name="o_proj_dx")
    gw["w_o"] = _mm(merged, dh1, [F32], ta=True, name="o_proj_dw")

    def gate_bwd_fn(r, b):
        dm, zgt, b0, b1, b2 = r
        gt = _sigmoid(zgt + b[0])
        g0, g1, g2 = gt[:, :D_MODEL], gt[:, D_MODEL:2 * D_MODEL], gt[:, 2 * D_MODEL:]
        dzg = jnp.concatenate([dm * b0 * g0 * (1.0 - g0), dm * b1 * g1 * (1.0 - g1), dm * b2 * g2 * (1.0 - g2)], axis=1)
        return [dm * g0, dm * g1, dm * g2, dzg], [_colsum(dzg)]
    dbr_ssm, dbr_attn, dbr_mem, dzg, gs["b_gate"] = _ew(
        gate_bwd_fn, [dmerged, zg, br_ssm, br_attn, br_mem], [sp["b_gate"]],
        [(D_MODEL, BF16)] * 3 + [(ZG_W, BF16)], [ZG_W], name="gate_bwd")

    gw["w_ssm_br"] = _mm(y2, dbr_ssm, [F32], ta=True, name="ssm_br_dw")
    dy2 = _mm(dbr_ssm, wb["w_ssm_br"], [F32], tb=True, name="ssm_br_dx")

    def glu_bwd_fn(r, b):
        dy, y1t, tt = r
        sg = _sigmoid(tt)
        dt = dy * y1t.astype(F32) * sg * (1.0 - sg)
        return [dt, dy * sg], [_colsum(dt)]
    dt_glu, dy1a, gs["b_glu"] = _ew(glu_bwd_fn, [dy2, y1, t_glu], [], [(SSM_W, BF16), (SSM_W, F32)], [SSM_W],
                                    name="ssm_glu_bwd", tm=512)
    gw["w_glu"] = _mm(y1, dt_glu, [F32], ta=True, name="ssm_glu_dw")

    def gelu_bwd_epi(acc, dy1t, y0t):
        th, c0, c1 = _gelu_parts(y0t)
        dg = 0.5 * (1.0 + th) + 0.5 * y0t * (1.0 - th * th) * c0 * (1.0 + 3.0 * c1 * y0t * y0t)
        return ((acc + dy1t) * dg,)
    dy0 = _mm(dt_glu, wb["w_glu"], [F32], tb=True, epi=gelu_bwd_epi, mn=[dy1a, y0], name="ssm_glu_dx")
    gs["ssm_d"] = _ew(lambda r, b: ([], [_colsum(r[0] * r[1].astype(F32))]), [dy0, u], [], [], [SSM_W],
                      name="ssm_dd", tm=512)[0]
    g_adj = _mm(dy0, cbd_t, [F32], name="ssm_cs_dx")
    d_cbd = _mm(s_all, dy0, [F32], ta=True, name="ssm_cs_dw")
    lam, da_cat = _ssm_scan(g_adj, a_conj, reverse=True, s_fwd=s_all, name="ssm_scan_bwd")
    du = _mm(lam, bbd, [BF16], tb=True, epi=lambda acc, dyt, dr: (acc + dyt * dr,), mn=[dy0], rows=[d_row],
             name="ssm_bu_dx")
    d_bbd = _mm(u, lam, [F32], ta=True, name="ssm_bu_dw")
    gs["a_re"], gs["a_im"] = _tile_split(da_cat)
    dbr, dbi = _tile_split(d_bbd)
    gs["bb_re"], gs["bb_im"] = _bd_in_diag(dbr), _bd_in_diag(dbi)
    dcr, dci = _tile_split(d_cbd.T)
    gs["ssm_c_re"] = _bd_in_diag(dcr).transpose(0, 2, 1)
    gs["ssm_c_im"] = -_bd_in_diag(dci).transpose(0, 2, 1)

    gw["w_attn_br"] = _mm(o_att, dbr_attn, [F32], ta=True, name="attn_br_dw")

    def do_epi(acc, ot):
        prod = acc * ot
        head = lax.broadcasted_iota(jnp.int32, prod.shape, 1) // ATT_E
        dd = jnp.zeros_like(prod)
        for h in range(ATT_HG):
            dd = jnp.where(head == h, jnp.sum(jnp.where(head == h, prod, 0.0), axis=1, keepdims=True), dd)
        return acc, dd
    do_att, dd_att = _mm(dbr_attn, wb["w_attn_br"], [BF16, F32], tb=True, epi=do_epi, mn=[o_att], name="attn_br_dx")
    dq_l, dk_l, dv_l = [], [], []
    for g, d in enumerate(DILATIONS):
        nb = l // d // ATT_WIN
        qp, kp, vp = qkv_p[g]
        dq, dkc, dkp, dvc, dvp = _attn_bwd(qp, kp, vp, _to_perm(do_att, d), _to_perm(lse_tot, d), _to_perm(dd_att, d),
                                           nb, "attn_bwd%d" % g)
        dk, dv = _attn_kv_combine(dkc, dkp, dvc, dvp, nb, "attn_kv_combine%d" % g)
        dq_l.append(_from_perm(dq, d))
        dk_l.append(_from_perm(dk, d))
        dv_l.append(_from_perm(dv, d))

    gw["w_mem_br"] = _mm(mo, dbr_mem, [F32], ta=True, name="mem_br_dw")
    dmo = _mm(dbr_mem, wb["w_mem_br"], [BF16], tb=True, name="mem_br_dx")
    dmq, dkv = _mem_bwd(mq, kv, dmo, "mem_attn_bwd")
    gw["w_mem_kv"] = _mm(mn, dkv, [F32], ta=True, name="mem_kv_dw")
    dmn = _mm(dkv, wb["w_mem_kv"], [F32], tb=True, name="mem_kv_dx")
    gs["mem_norm_g"] = _rms_bwd(mem, dmn, None, sp["mem_norm_g"], "rms_mem_bwd")[1]

    dza = jnp.concatenate([du] + dq_l + dk_l + dv_l + [dmq], axis=1)
    dw_a = _mm(n1, dza, [F32], ta=True, name="in_proj_a_dw", tn=1664)
    dw_g = _mm(n1, dzg, [F32], ta=True, name="in_proj_g_dw")
    gw["w_in"] = jnp.concatenate([dw_a, dw_g], axis=1)
    dn_a = _mm(dza, w_a, [F32], tb=True, name="in_proj_a_dx", tk=1664)
    dn1 = _mm(dzg, w_g, [F32], tb=True, epi=lambda acc, pt: (acc + pt,), mn=[dn_a], name="in_proj_g_dx")
    grad_x, gs["norm1_g"] = _rms_bwd(x, dn1, dh1, sp["norm1_g"], "rms1_bwd")
    return loss, grad_x, gw, gs


_SMALL_GRAD_ORDER = ("norm1_g", "mem_norm_g", "b_gate", "a_re", "a_im", "bb_re", "bb_im", "ssm_c_re", "ssm_c_im",
                     "ssm_d", "b_glu", "norm2_g", "final_g")


def kernel(x, mem, norm1_g, mem_norm_g, w_in, b_gate, ssm_lambda_re, ssm_lambda_im, ssm_log_dt, ssm_b_re, ssm_b_im, ssm_c_re, ssm_c_im, ssm_d, w_glu, b_glu, w_ssm_br, w_attn_br, w_mem_kv, w_mem_br, w_o, norm2_g, w_up, w_down, final_g, loss_target, m_norm1_g, m_mem_norm_g, m_w_in, m_b_gate, m_ssm_lambda_re, m_ssm_lambda_im, m_ssm_log_dt, m_ssm_b_re, m_ssm_b_im, m_ssm_c_re, m_ssm_c_im, m_ssm_d, m_w_glu, m_b_glu, m_w_ssm_br, m_w_attn_br, m_w_mem_kv, m_w_mem_br, m_w_o, m_norm2_g, m_w_up, m_w_down, m_final_g, v_norm1_g, v_mem_norm_g, v_w_in, v_b_gate, v_ssm_lambda_re, v_ssm_lambda_im, v_ssm_log_dt, v_ssm_b_re, v_ssm_b_im, v_ssm_c_re, v_ssm_c_im, v_ssm_d, v_w_glu, v_b_glu, v_w_ssm_br, v_w_attn_br, v_w_mem_kv, v_w_mem_br, v_w_o, v_norm2_g, v_w_up, v_w_down, v_final_g):
    args = dict(locals())
    w = {n: args[n] for n in ALL_W}
    m = {n: args["m_" + n] for n in ALL_W}
    v = {n: args["v_" + n] for n in ALL_W}
    my_c = lax.axis_index("c").astype(jnp.int32).reshape(1)
    my_chip = (2 * lax.axis_index("x") + lax.axis_index("y")).astype(jnp.int32).reshape(1)

    shard_shapes = [_shard_shape(n) for n in BIG]
    w_pack = _pack([w[n] for n in BIG])
    w_all = _allgather(w_pack.astype(BF16), "allgather_weights")
    flat_all = w_all.reshape(N_DEV, -1)
    wb, off = {}, 0
    for n, s in zip(BIG, shard_shapes):
        cnt = s[0] * s[1]
        wb[n] = _full_of(flat_all[:, off:off + cnt], n)
        off += cnt

    sp = {
        "norm1_g": norm1_g, "mem_norm_g": mem_norm_g, "b_gate": b_gate, "b_glu": b_glu, "norm2_g": norm2_g,
        "final_g": final_g.reshape(1, D_MODEL),
        "ssm_lambda_re": ssm_lambda_re[0], "ssm_lambda_im": ssm_lambda_im[0], "ssm_log_dt": ssm_log_dt[0],
        "ssm_b_re": ssm_b_re[0], "ssm_b_im": ssm_b_im[0], "ssm_c_re": ssm_c_re[0], "ssm_c_im": ssm_c_im[0],
        "ssm_d": ssm_d[0],
    }
    loss, grad_x, gw, gs = _local_step(x[0], mem[0], loss_target[0], wb, sp)
    loss = lax.psum(loss[0, 0], ("x", "y", "c"))

    g_pack = jnp.concatenate([_shards_of(gw[n], n) for n in BIG], axis=1).reshape(N_DEV, -1, PACK_C)
    t1 = _pair_exchange(g_pack, "grad_pair_exchange")
    p_sum = _pair_sum(g_pack, t1, my_c, "grad_pair_sum")
    t2 = _chip_exchange(p_sum, "grad_chip_exchange")
    m_pack = _pack([m[n] for n in BIG])
    v_pack = _pack([v[n] for n in BIG])
    big_out = _adam_big(p_sum, t2, my_chip, w_pack, m_pack, v_pack, "adam_big")
    big = [dict(zip(BIG, _unpack(b, [w[n].shape for n in BIG]))) for b in big_out]

    sg_shapes = [gs[n].shape for n in _SMALL_GRAD_ORDER]
    sg_all = _allgather(_pack([gs[n] for n in _SMALL_GRAD_ORDER]), "allgather_small_grads")
    sg = dict(zip(_SMALL_GRAD_ORDER, _unpack(_sum8(sg_all, "sum_small_grads"), sg_shapes)))
    _, disc_vjp = jax.vjp(_discretize, sp["ssm_lambda_re"], sp["ssm_lambda_im"], sp["ssm_log_dt"],
                          sp["ssm_b_re"], sp["ssm_b_im"])
    d_lre, d_lim, d_ldt, d_bre, d_bim = disc_vjp((sg["a_re"].reshape(SSM_G, SSM_P), sg["a_im"].reshape(SSM_G, SSM_P),
                                                  sg["bb_re"], sg["bb_im"]))
    small_grad = {
        "norm1_g": sg["norm1_g"], "mem_norm_g": sg["mem_norm_g"], "b_gate": sg["b_gate"],
        "ssm_lambda_re": d_lre, "ssm_lambda_im": d_lim, "ssm_log_dt": d_ldt, "ssm_b_re": d_bre, "ssm_b_im": d_bim,
        "ssm_c_re": sg["ssm_c_re"], "ssm_c_im": sg["ssm_c_im"], "ssm_d": sg["ssm_d"], "b_glu": sg["b_glu"],
        "norm2_g": sg["norm2_g"], "final_g": sg["final_g"],
    }
    small_grad = {n: small_grad[n].reshape(w[n].shape) for n in SMALL}
    s_shapes = [w[n].shape for n in SMALL]
    small_out = _adam_small(_pack([small_grad[n] for n in SMALL]), _pack([w[n] for n in SMALL]),
                            _pack([m[n] for n in SMALL]), _pack([v[n] for n in SMALL]), "adam_small")
    small = [small_grad] + [dict(zip(SMALL, _unpack(b, s_shapes))) for b in small_out]

    outs = [loss, grad_x[None]]
    for kind in range(4):
        for n in ALL_W:
            outs.append(big[kind][n] if n in BIG else small[kind][n])
    return tuple(outs)
```

```python
import math

import numpy as np
import jax
import jax.numpy as jnp
from jax import lax
from jax.experimental import pallas as pl
from jax.experimental.pallas import tpu as pltpu

F32 = jnp.float32
BF16 = jnp.bfloat16
_MXU = jnp.bfloat16

D_MODEL = 1024
SSM_G, SSM_H, SSM_P = 32, 16, 64
SSM_W = SSM_G * SSM_H
SSM_S = SSM_G * SSM_P
ATT_E = 64
ATT_HG = 4
ATT_GW = ATT_HG * ATT_E
ATT_WIN = 128
DILATIONS = (1, 4, 16)
MEM_H, MEM_E = 4, 128
MEM_W = MEM_H * MEM_E
ZA_W = SSM_W + 9 * ATT_GW + MEM_W
ZG_W = 3 * D_MODEL
IN_W = ZA_W + ZG_W
RMS_EPS = 1e-6
NEG_INF = -1e30

ADAM_LR, ADAM_B1, ADAM_B2, ADAM_EPS, ADAM_WD, ADAM_STEP = 0.001, 0.9, 0.999, 1e-08, 0.01, 10

N_DEV = 8
PACK_C = 512
_VMEM_LIMIT = 56 * 1024 * 1024
SUBLANES = 8
SCAN_SEG = 128
SCAN_CHAINS = 4
SCAN_W = 128
SCAN_NT = SSM_S // SCAN_W

BIG = ("w_in", "w_glu", "w_ssm_br", "w_attn_br", "w_mem_kv", "w_mem_br", "w_o", "w_up", "w_down")
BIG_SHAPE = {
    "w_in": (D_MODEL, IN_W, 1), "w_glu": (SSM_W, SSM_W, 0), "w_ssm_br": (SSM_W, D_MODEL, 1),
    "w_attn_br": (ATT_GW, D_MODEL, 1), "w_mem_kv": (D_MODEL, 2 * MEM_W, 0), "w_mem_br": (MEM_W, D_MODEL, 1),
    "w_o": (D_MODEL, D_MODEL, 0), "w_up": (D_MODEL, 4 * D_MODEL, 1), "w_down": (4 * D_MODEL, D_MODEL, 0),
}
SMALL = ("norm1_g", "mem_norm_g", "b_gate", "ssm_lambda_re", "ssm_lambda_im", "ssm_log_dt", "ssm_b_re",
         "ssm_b_im", "ssm_c_re", "ssm_c_im", "ssm_d", "b_glu", "norm2_g", "final_g")
ALL_W = ("norm1_g", "mem_norm_g", "w_in", "b_gate", "ssm_lambda_re", "ssm_lambda_im", "ssm_log_dt", "ssm_b_re",
         "ssm_b_im", "ssm_c_re", "ssm_c_im", "ssm_d", "w_glu", "b_glu", "w_ssm_br", "w_attn_br", "w_mem_kv",
         "w_mem_br", "w_o", "norm2_g", "w_up", "w_down", "final_g")


def _params(sem):
    return pltpu.CompilerParams(dimension_semantics=sem, vmem_limit_bytes=_VMEM_LIMIT)


def _pick(n, cap):
    if n <= cap:
        return n
    t = (cap // 128) * 128
    while n % t:
        t -= 128
    return t


def _mm(a, b, outs, *, name, ta=False, tb=False, epi=None, mn=(), rows=(), pair2=None, tm=1024, tn=1024, tk=512):
    ab = [a, b] + (list(pair2) if pair2 is not None else [])
    m = a.shape[1] if ta else a.shape[0]
    k = a.shape[0] if ta else a.shape[1]
    n = b.shape[0] if tb else b.shape[1]
    assert k == (b.shape[1] if tb else b.shape[0]), (name, a.shape, b.shape)
    tm, tn, tk = _pick(m, tm), _pick(n, tn), _pick(k, tk)
    nk = k // tk
    a_spec = pl.BlockSpec((tk, tm), lambda i, j, kk: (kk, i)) if ta else pl.BlockSpec((tm, tk), lambda i, j, kk: (i, kk))
    b_spec = pl.BlockSpec((tn, tk), lambda i, j, kk: (j, kk)) if tb else pl.BlockSpec((tk, tn), lambda i, j, kk: (kk, j))
    mn_spec = pl.BlockSpec((tm, tn), lambda i, j, kk: (i, j))
    row_spec = pl.BlockSpec((1, tn), lambda i, j, kk: (0, j))
    n_ex, n_out = len(mn) + len(rows), len(outs)
    dims = (((0 if ta else 1,), (1 if tb else 0,)), ((), ()))

    def body(*refs):
        ab_refs, rest = refs[:len(ab)], refs[len(ab):]
        ex, o_refs, acc = rest[:n_ex], rest[n_ex:n_ex + n_out], rest[-1]
        kk = pl.program_id(2)

        @pl.when(kk == 0)
        def _():
            acc[...] = jnp.zeros_like(acc)

        for a_ref, b_ref in zip(ab_refs[0::2], ab_refs[1::2]):
            acc[...] += lax.dot_general(a_ref[...].astype(_MXU), b_ref[...].astype(_MXU), dims,
                                        preferred_element_type=F32)

        @pl.when(kk == nk - 1)
        def _():
            vals = (acc[...],) if epi is None else epi(acc[...], *[r[...] for r in ex])
            for r, v in zip(o_refs, vals):
                r[...] = v.astype(r.dtype)

    res = pl.pallas_call(
        body, grid=(m // tm, n // tn, nk),
        in_specs=[a_spec, b_spec] * (len(ab) // 2) + [mn_spec] * len(mn) + [row_spec] * len(rows),
        out_specs=[mn_spec] * n_out,
        out_shape=[jax.ShapeDtypeStruct((m, n), dt) for dt in outs],
        scratch_shapes=[pltpu.VMEM((tm, tn), F32)],
        compiler_params=_params(("parallel", "parallel", "arbitrary")), name=name,
    )(*ab, *mn, *rows)
    return res[0] if n_out == 1 else res


def _ew(fn, rows, bcs, out_rows, out_accs, *, name, tm=256):
    r = rows[0].shape[0]
    tm = min(tm, r)
    assert r % tm == 0
    nr, nb, no, na = len(rows), len(bcs), len(out_rows), len(out_accs)

    def body(*refs):
        i = pl.program_id(0)
        r_in, b_in = refs[:nr], refs[nr:nr + nb]
        o_r, o_a = refs[nr + nb:nr + nb + no], refs[nr + nb + no:]
        outs, accs = fn([x[...] for x in r_in], [x[...] for x in b_in])
        for ref, v in zip(o_r, outs):
            ref[...] = v.astype(ref.dtype)
        if na:
            @pl.when(i == 0)
            def _():
                for ref in o_a:
                    ref[...] = jnp.zeros_like(ref)

            for ref, v in zip(o_a, accs):
                ref[...] += v

    res = pl.pallas_call(
        body, grid=(r // tm,),
        in_specs=[pl.BlockSpec((tm, x.shape[1]), lambda i: (i, 0)) for x in rows]
        + [pl.BlockSpec((1, x.shape[1]), lambda i: (0, 0)) for x in bcs],
        out_specs=[pl.BlockSpec((tm, c), lambda i: (i, 0)) for c, _ in out_rows]
        + [pl.BlockSpec((1, c), lambda i: (0, 0)) for c in out_accs],
        out_shape=[jax.ShapeDtypeStruct((r, c), dt) for c, dt in out_rows]
        + [jax.ShapeDtypeStruct((1, c), F32) for c in out_accs],
        compiler_params=_params(("arbitrary",)), name=name,
    )(*rows, *bcs)
    return res


def _colsum(x):
    return jnp.sum(x, axis=0, keepdims=True)


def _sigmoid(x):
    return 1.0 / (1.0 + jnp.exp(-x))


def _rms_fwd(x, g, name):
    def fn(r, b):
        xv = r[0]
        rs = lax.rsqrt(jnp.mean(xv * xv, axis=-1, keepdims=True) + RMS_EPS)
        return [xv * rs * b[0]], []
    return _ew(fn, [x], [g], [(x.shape[1], BF16)], [], name=name)[0]


def _rms_bwd(x, dn, res, g, name):
    def fn(r, b):
        xv, dv = r[0], r[1]
        rs = lax.rsqrt(jnp.mean(xv * xv, axis=-1, keepdims=True) + RMS_EPS)
        gd = dv * b[0]
        dx = rs * gd - xv * (rs * rs * rs) * jnp.mean(gd * xv, axis=-1, keepdims=True)
        if res is not None:
            dx = dx + r[2]
        return [dx], [_colsum(dv * xv * rs)]
    rows = [x, dn] + ([res] if res is not None else [])
    return _ew(fn, rows, [g], [(x.shape[1], F32)], [x.shape[1]], name=name)


def _ssm_scan(b, a_pair, *, reverse, s_fwd=None, name):
    l = b.shape[0]
    seg, w = SCAN_SEG, SCAN_W
    nch = min(SCAN_CHAINS, l // (SUBLANES * seg))
    chain_rows = SUBLANES * seg
    tb = nch * chain_rows
    nt = l // tb
    with_da = s_fwd is not None
    assert reverse or not with_da

    def tt(t):
        return nt - 1 - t if reverse else t

    def body(*refs):
        if with_da:
            br_ref, bi_ref, a_ref, sf_ref, sp_ref, s_ref, da_ref, p_ref, c_ref = refs
        else:
            br_ref, bi_ref, a_ref, s_ref, p_ref, c_ref = refs
        t_blk = pl.program_id(1)
        ar, ai = a_ref[0], a_ref[1]

        @pl.when(t_blk == 0)
        def _():
            def pstep(i, carry):
                pr, pi = carry
                p_ref[0, pl.ds(i, 1), :] = pr
                p_ref[1, pl.ds(i, 1), :] = pi
                return pr * ar - pi * ai, pr * ai + pi * ar

            lax.fori_loop(0, seg, pstep, (ar, ai))
            c_ref[...] = jnp.zeros_like(c_ref)
            if with_da:
                da_ref[...] = jnp.zeros_like(da_ref)

        arb, aib = jnp.broadcast_to(ar, (SUBLANES, w)), jnp.broadcast_to(ai, (SUBLANES, w))
        zero = jnp.zeros((SUBLANES, w), F32)

        def rows(g, i):
            return pl.ds(g * chain_rows + (seg - 1 - i if reverse else i), SUBLANES, stride=seg)

        def local_step(i, carry):
            out = []
            for g in range(nch):
                sr, si = carry[2 * g], carry[2 * g + 1]
                idx = rows(g, i)
                sr, si = arb * sr - aib * si + br_ref[idx, :], arb * si + aib * sr + bi_ref[idx, :]
                s_ref.at[0][idx, :] = sr
                s_ref.at[1][idx, :] = si
                out += [sr, si]
            return tuple(out)

        ends = lax.fori_loop(0, seg, local_step, (zero,) * (2 * nch), unroll=2)

        a_seg_r, a_seg_i = p_ref[0, seg - 1:seg, :], p_ref[1, seg - 1:seg, :]
        cr, ci = c_ref[0], c_ref[1]
        sub = lax.broadcasted_iota(jnp.int32, (SUBLANES, w), 0)
        ins = [[zero, zero] for _ in range(nch)]
        order = [(g, k) for g in range(nch) for k in range(SUBLANES)]
        for g, k in (order[::-1] if reverse else order):
            ins[g] = [jnp.where(sub == k, cr, ins[g][0]), jnp.where(sub == k, ci, ins[g][1])]
            er, ei = ends[2 * g][k:k + 1], ends[2 * g + 1][k:k + 1]
            cr, ci = er + a_seg_r * cr - a_seg_i * ci, ei + a_seg_r * ci + a_seg_i * cr
        c_ref[0] = cr
        c_ref[1] = ci

        def fix(g, i):
            idx = rows(g, i)
            pr, pi = p_ref[0, pl.ds(i, 1), :], p_ref[1, pl.ds(i, 1), :]
            sr = s_ref.at[0][idx, :] + pr * ins[g][0] - pi * ins[g][1]
            si = s_ref.at[1][idx, :] + pr * ins[g][1] + pi * ins[g][0]
            s_ref.at[0][idx, :] = sr
            s_ref.at[1][idx, :] = si
            return sr, si

        if not with_da:
            def fix_step(i, carry):
                for g in range(nch):
                    fix(g, i)
                return carry

            lax.fori_loop(0, seg, fix_step, 0, unroll=2)
        else:
            def adj_step(i, acc):
                acc_r, acc_i = acc
                for g in range(nch):
                    lr, li = fix(g, i)
                    prev = pl.ds(g * chain_rows + seg - 2 - i, SUBLANES, stride=seg)
                    fr, fi = sf_ref.at[0][prev, :], sf_ref.at[1][prev, :]
                    acc_r, acc_i = acc_r + lr * fr + li * fi, acc_i + li * fr - lr * fi
                return acc_r, acc_i

            acc_r, acc_i = lax.fori_loop(0, seg - 1, adj_step, (zero, zero), unroll=2)
            first_block = tt(t_blk) == 0
            for g in range(nch):
                lr, li = fix(g, seg - 1)
                seg_ends = pl.ds(g * chain_rows + seg - 1, SUBLANES, stride=seg)
                if g == 0:
                    pvr = jnp.where(first_block, 0.0, sp_ref[0, SUBLANES - 1:SUBLANES, :])
                    pvi = jnp.where(first_block, 0.0, sp_ref[1, SUBLANES - 1:SUBLANES, :])
                else:
                    pvr = sf_ref[0, g * chain_rows - 1:g * chain_rows, :]
                    pvi = sf_ref[1, g * chain_rows - 1:g * chain_rows, :]
                fr = jnp.where(sub == 0, pvr, pltpu.roll(sf_ref.at[0][seg_ends, :], 1, 0))
                fi = jnp.where(sub == 0, pvi, pltpu.roll(sf_ref.at[1][seg_ends, :], 1, 0))
                acc_r = acc_r + lr * fr + li * fi
                acc_i = acc_i + li * fr - lr * fi
            da_ref[0] += jnp.sum(acc_r, axis=0, keepdims=True)
            da_ref[1] += jnp.sum(acc_i, axis=0, keepdims=True)

    re_spec = pl.BlockSpec((tb, w), lambda j, t: (tt(t), 2 * j))
    im_spec = pl.BlockSpec((tb, w), lambda j, t: (tt(t), 2 * j + 1))
    a_spec = pl.BlockSpec((2, 1, w), lambda j, t: (0, 0, j))
    s_spec = pl.BlockSpec((2, tb, w), lambda j, t: (0, tt(t), j))
    in_specs, args = [re_spec, im_spec, a_spec], [b, b, a_pair]
    out_specs, out_shape = [s_spec], [jax.ShapeDtypeStruct((2, l, SSM_S), F32)]
    scratch = [pltpu.VMEM((2, seg, w), F32), pltpu.VMEM((2, 1, w), F32)]
    if with_da:
        in_specs += [s_spec, pl.BlockSpec((2, SUBLANES, w),
                                          lambda j, t: (0, jnp.maximum(tt(t) * (tb // SUBLANES) - 1, 0), j))]
        args += [s_fwd, s_fwd]
        out_specs.append(a_spec)
        out_shape.append(jax.ShapeDtypeStruct((2, 1, SSM_S), F32))
    res = pl.pallas_call(
        body, grid=(SSM_S // w, nt), in_specs=in_specs, out_specs=out_specs, out_shape=out_shape,
        scratch_shapes=scratch, compiler_params=_params(("parallel", "arbitrary")), name=name,
    )(*args)
    return res if with_da else res[0]


def _nt_dot(x, y):
    return lax.dot_general(x.astype(_MXU), y.astype(_MXU), (((1,), (1,)), ((), ())), preferred_element_type=F32)


def _tn_dot(x, y):
    return lax.dot_general(x.astype(_MXU), y.astype(_MXU), (((0,), (0,)), ((), ())), preferred_element_type=F32)


def _nn_dot(x, y):
    return jnp.dot(x.astype(_MXU), y.astype(_MXU), preferred_element_type=F32)


def _attn_masks(i, nb):
    qi = lax.broadcasted_iota(jnp.int32, (ATT_WIN, ATT_WIN), 0)
    kj = lax.broadcasted_iota(jnp.int32, (ATT_WIN, ATT_WIN), 1)
    has_prev = (i % nb) != 0
    return kj <= qi, jnp.logical_and(kj >= qi, has_prev)


def _attn_specs(l):
    cur = pl.BlockSpec((ATT_WIN, ATT_GW), lambda i: (i, 0))
    prev = pl.BlockSpec((ATT_WIN, ATT_GW), lambda i: (jnp.maximum(i - 1, 0), 0))
    return cur, prev


def _attn_fwd(q, k, v, nb, name):
    l = q.shape[0]
    scale = ATT_E ** -0.5

    def body(q_ref, kc_ref, kp_ref, vc_ref, vp_ref, o_ref, lse_ref):
        mask_c, mask_p = _attn_masks(pl.program_id(0), nb)
        for h in range(ATT_HG):
            sl = slice(h * ATT_E, (h + 1) * ATT_E)
            qh = q_ref[:, sl]
            sc = jnp.where(mask_c, _nt_dot(qh, kc_ref[:, sl]) * scale, NEG_INF)
            sp = jnp.where(mask_p, _nt_dot(qh, kp_ref[:, sl]) * scale, NEG_INF)
            mx = jnp.maximum(jnp.max(sc, axis=-1, keepdims=True), jnp.max(sp, axis=-1, keepdims=True))
            pc, pp = jnp.exp(sc - mx), jnp.exp(sp - mx)
            den = jnp.sum(pc, axis=-1, keepdims=True) + jnp.sum(pp, axis=-1, keepdims=True)
            o_ref[:, sl] = (_nn_dot(pc, vc_ref[:, sl]) + _nn_dot(pp, vp_ref[:, sl])) / den
            lse_ref[:, sl] = jnp.broadcast_to(mx + jnp.log(den), (ATT_WIN, ATT_E))

    cur, prev = _attn_specs(l)
    return pl.pallas_call(
        body, grid=(l // ATT_WIN,), in_specs=[cur, cur, prev, cur, prev], out_specs=[cur, cur],
        out_shape=[jax.ShapeDtypeStruct((l, ATT_GW), F32)] * 2,
        compiler_params=_params(("parallel",)), name=name,
    )(q, k, k, v, v)


def _attn_bwd(q, k, v, do, lse, dd, nb, name):
    l = q.shape[0]
    scale = ATT_E ** -0.5

    def body(q_ref, kc_ref, kp_ref, vc_ref, vp_ref, do_ref, lse_ref, dd_ref, dq_ref, dkc_ref, dkp_ref, dvc_ref, dvp_ref):
        mask_c, mask_p = _attn_masks(pl.program_id(0), nb)
        for h in range(ATT_HG):
            sl = slice(h * ATT_E, (h + 1) * ATT_E)
            qh, doh = q_ref[:, sl], do_ref[:, sl]
            kc, kp, vc, vp = kc_ref[:, sl], kp_ref[:, sl], vc_ref[:, sl], vp_ref[:, sl]
            lh = lse_ref[:, h * ATT_E:h * ATT_E + 1]
            dh = dd_ref[:, h * ATT_E:h * ATT_E + 1]
            pc = jnp.where(mask_c, jnp.exp(_nt_dot(qh, kc) * scale - lh), 0.0)
            pp = jnp.where(mask_p, jnp.exp(_nt_dot(qh, kp) * scale - lh), 0.0)
            dsc = pc * (_nt_dot(doh, vc) - dh) * scale
            dsp = pp * (_nt_dot(doh, vp) - dh) * scale
            dq_ref[:, sl] = (_nn_dot(dsc, kc) + _nn_dot(dsp, kp)).astype(dq_ref.dtype)
            dkc_ref[:, sl] = _tn_dot(dsc, qh)
            dkp_ref[:, sl] = _tn_dot(dsp, qh)
            dvc_ref[:, sl] = _tn_dot(pc, doh)
            dvp_ref[:, sl] = _tn_dot(pp, doh)

    cur, prev = _attn_specs(l)
    return pl.pallas_call(
        body, grid=(l // ATT_WIN,), in_specs=[cur, cur, prev, cur, prev, cur, cur, cur], out_specs=[cur] * 5,
        out_shape=[jax.ShapeDtypeStruct((l, ATT_GW), BF16)] + [jax.ShapeDtypeStruct((l, ATT_GW), F32)] * 4,
        compiler_params=_params(("parallel",)), name=name,
    )(q, k, k, v, v, do, lse, dd)


def _attn_kv_combine(dkc, dkp, dvc, dvp, nb, name):
    l = dkc.shape[0]
    nblk = l // ATT_WIN

    def body(kc_ref, kp_ref, vc_ref, vp_ref, dk_ref, dv_ref):
        has_next = ((pl.program_id(0) + 1) % nb) != 0
        dk_ref[...] = (kc_ref[...] + jnp.where(has_next, kp_ref[...], 0.0)).astype(dk_ref.dtype)
        dv_ref[...] = (vc_ref[...] + jnp.where(has_next, vp_ref[...], 0.0)).astype(dv_ref.dtype)

    cur = pl.BlockSpec((ATT_WIN, ATT_GW), lambda i: (i, 0))
    nxt = pl.BlockSpec((ATT_WIN, ATT_GW), lambda i: (jnp.minimum(i + 1, nblk - 1), 0))
    return pl.pallas_call(
        body, grid=(nblk,), in_specs=[cur, nxt, cur, nxt], out_specs=[cur, cur],
        out_shape=[jax.ShapeDtypeStruct((l, ATT_GW), BF16)] * 2,
        compiler_params=_params(("parallel",)), name=name,
    )(dkc, dkp, dvc, dvp)


def _to_perm(a, d):
    if d == 1:
        return a
    l, c = a.shape
    return a.reshape(l // d, d, c).transpose(1, 0, 2).reshape(l, c)


def _from_perm(a, d):
    if d == 1:
        return a
    l, c = a.shape
    return a.reshape(d, l // d, c).transpose(1, 0, 2).reshape(l, c)


def _mem_probs(qh, kh):
    s = _nt_dot(qh, kh) * (MEM_E ** -0.5)
    e = jnp.exp(s - jnp.max(s, axis=-1, keepdims=True))
    return e / jnp.sum(e, axis=-1, keepdims=True)


def _mem_fwd(mq, kv, name, tm=512):
    l, nm = mq.shape[0], kv.shape[0]

    def body(q_ref, kv_ref, o_ref):
        for h in range(MEM_H):
            sl = slice(h * MEM_E, (h + 1) * MEM_E)
            p = _mem_probs(q_ref[:, sl], kv_ref[:, sl])
            o_ref[:, sl] = _nn_dot(p, kv_ref[:, MEM_W + h * MEM_E:MEM_W + (h + 1) * MEM_E]).astype(o_ref.dtype)

    return pl.pallas_call(
        body, grid=(l // tm,),
        in_specs=[pl.BlockSpec((tm, MEM_W), lambda i: (i, 0)), pl.BlockSpec((nm, 2 * MEM_W), lambda i: (0, 0))],
        out_specs=pl.BlockSpec((tm, MEM_W), lambda i: (i, 0)),
        out_shape=jax.ShapeDtypeStruct((l, MEM_W), BF16),
        compiler_params=_params(("parallel",)), name=name,
    )(mq, kv)


def _mem_bwd(mq, kv, dmo, name, tm=512):
    l, nm = mq.shape[0], kv.shape[0]
    scale = MEM_E ** -0.5

    def body(q_ref, kv_ref, do_ref, dq_ref, dkv_ref):
        @pl.when(pl.program_id(0) == 0)
        def _():
            dkv_ref[...] = jnp.zeros_like(dkv_ref)

        for h in range(MEM_H):
            sl = slice(h * MEM_E, (h + 1) * MEM_E)
            vsl = slice(MEM_W + h * MEM_E, MEM_W + (h + 1) * MEM_E)
            qh, kh, vh, doh = q_ref[:, sl], kv_ref[:, sl], kv_ref[:, vsl], do_ref[:, sl]
            p = _mem_probs(qh, kh)
            dp = _nt_dot(doh, vh)
            ds = p * (dp - jnp.sum(dp * p, axis=-1, keepdims=True)) * scale
            dq_ref[:, sl] = _nn_dot(ds, kh).astype(dq_ref.dtype)
            dkv_ref[:, sl] += _tn_dot(ds, qh)
            dkv_ref[:, vsl] += _tn_dot(p, doh)

    row = pl.BlockSpec((tm, MEM_W), lambda i: (i, 0))
    full = pl.BlockSpec((nm, 2 * MEM_W), lambda i: (0, 0))
    return pl.pallas_call(
        body, grid=(l // tm,), in_specs=[row, full, row], out_specs=[row, full],
        out_shape=[jax.ShapeDtypeStruct((l, MEM_W), BF16), jax.ShapeDtypeStruct((nm, 2 * MEM_W), F32)],
        compiler_params=_params(("arbitrary",)), name=name,
    )(mq, kv, dmo)


def _discretize(lam_re, lam_im, log_dt, b_re, b_im):
    dt = jnp.exp(log_dt)[:, None]
    mag = jnp.exp(lam_re * dt)
    a_re, a_im = mag * jnp.cos(lam_im * dt), mag * jnp.sin(lam_im * dt)
    nr, ni = a_re - 1.0, a_im
    den = lam_re * lam_re + lam_im * lam_im
    coef_re = (nr * lam_re + ni * lam_im) / den
    coef_im = (ni * lam_re - nr * lam_im) / den
    bb_re = coef_re[..., None] * b_re - coef_im[..., None] * b_im
    bb_im = coef_re[..., None] * b_im + coef_im[..., None] * b_re
    return a_re, a_im, bb_re, bb_im


def _tile_cat(re, im):
    lead = re.shape[:-1]
    t = jnp.stack([re.reshape(*lead, SCAN_NT, SCAN_W), im.reshape(*lead, SCAN_NT, SCAN_W)], axis=-2)
    return t.reshape(*lead, 2 * SSM_S)


def _bd_in(bb):
    return jnp.einsum("gph,gk->ghkp", bb, jnp.eye(SSM_G, dtype=bb.dtype)).reshape(SSM_W, SSM_S)


def _bd_in_diag(x):
    idx = jnp.arange(SSM_G)
    return x.reshape(SSM_G, SSM_H, SSM_G, SSM_P)[idx, :, idx, :].transpose(0, 2, 1)


_ANY = pl.BlockSpec(memory_space=pl.ANY)
_MESH = pl.DeviceIdType.MESH


def _allgather(x, name):
    def body(x_ref, out_ref, send_sems, recv_sems, local_sem):
        mx, my, mc = lax.axis_index("x"), lax.axis_index("y"), lax.axis_index("c")
        me, sibling = (mx, my, mc), (mx, my, 1 - mc)
        chips = [(1 - mx, my), (mx, 1 - my), (1 - mx, 1 - my)]

        def blk(px, py, pc):
            return out_ref.at[4 * px + 2 * py + pc]

        def copy(k, block, to, src=None):
            return pltpu.make_async_remote_copy(
                src_ref=blk(*block) if src is None else src, dst_ref=blk(*block),
                send_sem=send_sems.at[k], recv_sem=recv_sems.at[k], device_id=to, device_id_type=_MESH)

        mine = pltpu.make_async_copy(x_ref, blk(*me), local_sem)
        mine.start()
        first = [copy(0, me, sibling, src=x_ref)]
        first += [copy(1 + j, me, (*chip, mc), src=x_ref) for j, chip in enumerate(chips)]
        for cp in first:
            cp.start()
        passed = [copy(4 + j, (*chip, mc), sibling) for j, chip in enumerate(chips)]
        for j, chip in enumerate(chips):
            copy(1 + j, (*chip, mc), me).wait_recv()
            passed[j].start()
        copy(0, sibling, me).wait_recv()
        for j, chip in enumerate(chips):
            copy(4 + j, (*chip, 1 - mc), me).wait_recv()
        for cp in first + passed:
            cp.wait_send()
        mine.wait()

    return pl.pallas_call(
        body, out_shape=jax.ShapeDtypeStruct((N_DEV,) + x.shape, x.dtype), in_specs=[_ANY], out_specs=_ANY,
        scratch_shapes=[pltpu.SemaphoreType.DMA((7,)), pltpu.SemaphoreType.DMA((7,)), pltpu.SemaphoreType.DMA],
        name=name,
    )(x)


def _pair_exchange(g, name):
    def body(g_ref, out_ref, send_sems, recv_sems):
        mx, my, mc = lax.axis_index("x"), lax.axis_index("y"), lax.axis_index("c")
        copies = [pltpu.make_async_remote_copy(
            src_ref=g_ref.at[2 * k + (1 - mc)], dst_ref=out_ref.at[k], send_sem=send_sems.at[k],
            recv_sem=recv_sems.at[k], device_id=(mx, my, 1 - mc), device_id_type=_MESH) for k in range(4)]
        for cp in copies:
            cp.start()
        for cp in copies:
            cp.wait()

    return pl.pallas_call(
        body, out_shape=jax.ShapeDtypeStruct((4,) + g.shape[1:], g.dtype), in_specs=[_ANY], out_specs=_ANY,
        scratch_shapes=[pltpu.SemaphoreType.DMA((4,)), pltpu.SemaphoreType.DMA((4,))], name=name,
    )(g)


def _chip_exchange(p, name):
    def body(p_ref, out_ref, send_sems, recv_sems):
        mx, my, mc = lax.axis_index("x"), lax.axis_index("y"), lax.axis_index("c")
        chips = [(1 - mx, my), (mx, 1 - my), (1 - mx, 1 - my)]
        copies = [pltpu.make_async_remote_copy(
            src_ref=p_ref.at[2 * px + py], dst_ref=out_ref.at[j], send_sem=send_sems.at[j],
            recv_sem=recv_sems.at[j], device_id=(px, py, mc), device_id_type=_MESH)
            for j, (px, py) in enumerate(chips)]
        for cp in copies:
            cp.start()
        for cp in copies:
            cp.wait()

    return pl.pallas_call(
        body, out_shape=jax.ShapeDtypeStruct((3,) + p.shape[1:], p.dtype), in_specs=[_ANY], out_specs=_ANY,
        scratch_shapes=[pltpu.SemaphoreType.DMA((3,)), pltpu.SemaphoreType.DMA((3,))], name=name,
    )(p)


def _pair_sum(g, t1, my_c, name, tr):
    _, r, c = g.shape

    def body(c_ref, g_ref, t_ref, o_ref, ob_ref):
        s = g_ref[...] + t_ref[...]
        o_ref[...] = s
        ob_ref[...] = s.astype(BF16)

    blk = pl.BlockSpec((None, tr, c), lambda k, i, cr: (k, i, 0))
    return pl.pallas_call(
        body,
        grid_spec=pltpu.PrefetchScalarGridSpec(
            num_scalar_prefetch=1, grid=(4, r // tr),
            in_specs=[pl.BlockSpec((None, tr, c), lambda k, i, cr: (2 * k + cr[0], i, 0)), blk],
            out_specs=[blk, blk]),
        out_shape=[jax.ShapeDtypeStruct((4, r, c), F32), jax.ShapeDtypeStruct((4, r, c), BF16)],
        compiler_params=_params(("parallel", "parallel")), name=name,
    )(my_c, g, t1)


def _adam_math(g, w, m, v):
    m = ADAM_B1 * m + (1.0 - ADAM_B1) * g
    v = ADAM_B2 * v + (1.0 - ADAM_B2) * (g * g)
    m_hat = m / (1.0 - ADAM_B1 ** ADAM_STEP)
    v_hat = v / (1.0 - ADAM_B2 ** ADAM_STEP)
    delta = -ADAM_LR * (m_hat / (jnp.sqrt(v_hat) + ADAM_EPS) + ADAM_WD * w)
    return delta, m, v


def _adam_big(p, t2, my_chip, w, m, v, name, tr):
    r, c = w.shape

    def body(k_ref, p_ref, t0_ref, t1_ref, t2_ref, w_ref, m_ref, v_ref, g_out, d_out, m_out, v_out):
        g = ((p_ref[...] + t0_ref[...].astype(F32)) + t1_ref[...].astype(F32)) + t2_ref[...].astype(F32)
        d, mn, vn = _adam_math(g, w_ref[...], m_ref[...], v_ref[...])
        g_out[...], d_out[...], m_out[...], v_out[...] = g, d, mn, vn

    flat = pl.BlockSpec((tr, c), lambda i, kr: (i, 0))

    def rel(j):
        return pl.BlockSpec((None, tr, c), lambda i, kr: (j, i, 0))

    return pl.pallas_call(
        body,
        grid_spec=pltpu.PrefetchScalarGridSpec(
            num_scalar_prefetch=1, grid=(r // tr,),
            in_specs=[pl.BlockSpec((None, tr, c), lambda i, kr: (kr[0], i, 0)), rel(0), rel(1), rel(2), flat, flat, flat],
            out_specs=[flat] * 4),
        out_shape=[jax.ShapeDtypeStruct((r, c), F32)] * 4,
        compiler_params=_params(("parallel",)), name=name,
    )(my_chip, p, t2, t2, t2, w, m, v)


def _sum8(g8, name):
    _, r, c = g8.shape

    def body(g_ref, o_ref):
        acc = g_ref[0]
        for j in range(1, N_DEV):
            acc = acc + g_ref[j]
        o_ref[...] = acc

    return pl.pallas_call(
        body, grid=(1,), in_specs=[pl.BlockSpec((N_DEV, r, c), lambda i: (0, 0, 0))],
        out_specs=pl.BlockSpec((r, c), lambda i: (0, 0)), out_shape=jax.ShapeDtypeStruct((r, c), F32),
        compiler_params=_params(("arbitrary",)), name=name,
    )(g8)


def _adam_small(g, w, m, v, name):
    def fn(r, b):
        return list(_adam_math(*r)), []
    c = g.shape[1]
    return _ew(fn, [g, w, m, v], [], [(c, F32)] * 3, [], name=name, tm=g.shape[0])


def _pack(arrs, pad_rows=8):
    flat = jnp.concatenate([a.reshape(-1) for a in arrs])
    n = flat.shape[0]
    q = PACK_C * pad_rows
    tot = -(-n // q) * q
    if tot != n:
        flat = jnp.concatenate([flat, jnp.zeros((tot - n,), flat.dtype)])
    return flat.reshape(tot // PACK_C, PACK_C)


def _unpack(buf, shapes):
    flat = buf.reshape(-1)
    out, off = [], 0
    for s in shapes:
        n = int(np.prod(s))
        out.append(flat[off:off + n].reshape(s))
        off += n
    return out


GROUPS = (("w_in", "w_mem_kv", "w_o", "w_up", "w_down"),
          ("w_glu", "w_ssm_br", "w_mem_br", "w_attn_br"))
GROUP_TR = (208, 384)
ATTN_BR_FOLD = 2


def _stored_shape(name):
    r, c, ax = BIG_SHAPE[name]
    rows, cols = (r // N_DEV, c) if ax == 0 else (c // N_DEV, r)
    return (rows // ATTN_BR_FOLD, cols * ATTN_BR_FOLD) if name == "w_attn_br" else (rows, cols)


def _stored(shard, name):
    a = shard[0].T if BIG_SHAPE[name][2] == 1 else shard[0]
    return a.reshape(_stored_shape(name))


def _unstored(a, name):
    r, c, ax = BIG_SHAPE[name]
    if ax == 0:
        return a.reshape(1, r // N_DEV, c)
    return a.reshape(c // N_DEV, r).T[None]


def _pack_group(d, names):
    return jnp.concatenate([_stored(d[n], n) for n in names], axis=0)


def _split_group(buf, names):
    out, off = {}, 0
    for n in names:
        rows = _stored_shape(n)[0]
        out[n] = buf[..., off:off + rows, :]
        off += rows
    return out


def _full_stored(stacked, name):
    r, c, ax = BIG_SHAPE[name]
    return stacked.reshape((r, c) if ax == 0 else (c, r))


def _stacked_stored(full, name):
    return full.reshape((N_DEV,) + _stored_shape(name))


def _gelu_parts(x):
    c0, c1 = math.sqrt(2.0 / math.pi), 0.044715
    th = jnp.tanh(c0 * (x + c1 * x * x * x))
    return th, c0, c1


def _local_step(x, mem, tgt, wb, sp):
    l = x.shape[0]
    w_a, w_g = wb["w_in"][:ZA_W], wb["w_in"][ZA_W:]

    a_re, a_im, bb_re, bb_im = _discretize(sp["ssm_lambda_re"], sp["ssm_lambda_im"], sp["ssm_log_dt"],
                                           sp["ssm_b_re"], sp["ssm_b_im"])
    a_pair = jnp.stack([a_re.reshape(1, SSM_S), a_im.reshape(1, SSM_S)])
    a_conj = jnp.stack([a_re.reshape(1, SSM_S), -a_im.reshape(1, SSM_S)])
    b_re_t, b_im_t = _bd_in(bb_re).astype(BF16), _bd_in(bb_im).astype(BF16)
    c_re_t = _bd_in(sp["ssm_c_re"].transpose(0, 2, 1)).astype(BF16)
    c_im_t = (-_bd_in(sp["ssm_c_im"].transpose(0, 2, 1))).astype(BF16)
    bbd = _tile_cat(b_re_t, b_im_t)
    cbd_t = _tile_cat(c_re_t, c_im_t)
    d_row = sp["ssm_d"].reshape(1, SSM_W)

    n1 = _rms_fwd(x, sp["norm1_g"], "rms1")
    za = _mm(n1, w_a, [BF16], tb=True, name="in_proj_a", tn=1664)
    zg = _mm(n1, w_g, [F32], tb=True, name="in_proj_g")
    u = za[:, :SSM_W]
    mq = za[:, ZA_W - MEM_W:]

    bu = _mm(u, bbd, [F32], name="ssm_bu")
    s_all = _ssm_scan(bu, a_pair, reverse=False, name="ssm_scan_fwd")
    ys = _mm(s_all[0], c_re_t, [F32], tb=True, pair2=(s_all[1], c_im_t), name="ssm_cs")

    def gelu_fn(r, b):
        y0 = r[0] + b[0] * r[1].astype(F32)
        th, _, _ = _gelu_parts(y0)
        return [y0, 0.5 * y0 * (1.0 + th)], []
    y0, y1 = _ew(gelu_fn, [ys, u], [d_row], [(SSM_W, F32), (SSM_W, BF16)], [], name="ssm_gelu", tm=512)

    def glu_epi(acc, y1t, bg):
        t = acc + bg
        return t, y1t.astype(F32) * _sigmoid(t)
    t_glu, y2 = _mm(y1, wb["w_glu"], [F32, BF16], epi=glu_epi, mn=[y1], rows=[sp["b_glu"]], name="ssm_glu")
    br_ssm = _mm(y2, wb["w_ssm_br"], [F32], tb=True, name="ssm_br")

    qkv_p, o_g, lse_g = [], [], []
    for g, d in enumerate(DILATIONS):
        nb = l // d // ATT_WIN
        cols = [za[:, SSM_W + (3 * j + g) * ATT_GW: SSM_W + (3 * j + g + 1) * ATT_GW] for j in range(3)]
        qp, kp, vp = [_to_perm(cc, d) for cc in cols]
        qkv_p.append((qp, kp, vp))
        og, lg = _attn_fwd(qp, kp, vp, nb, "attn_fwd%d" % g)
        o_g.append(_from_perm(og, d))
        lse_g.append(_from_perm(lg, d))

    def merge_fn(r, b):
        o0, o1, o2, l0, l1, l2 = r
        mx = jnp.maximum(jnp.maximum(l0, l1), l2)
        e0, e1, e2 = jnp.exp(l0 - mx), jnp.exp(l1 - mx), jnp.exp(l2 - mx)
        tot = e0 + e1 + e2
        return [(e0 * o0 + e1 * o1 + e2 * o2) / tot, mx + jnp.log(tot)], []
    o_att, lse_tot = _ew(merge_fn, o_g + lse_g, [], [(ATT_GW, F32), (ATT_GW, F32)], [], name="attn_merge", tm=512)
    br_attn = _mm(o_att, wb["w_attn_br"], [F32], tb=True, name="attn_br")

    mn = _rms_fwd(mem, sp["mem_norm_g"], "rms_mem")
    kv = _mm(mn, wb["w_mem_kv"], [BF16], name="mem_kv")
    mo = _mem_fwd(mq, kv, "mem_attn_fwd")
    br_mem = _mm(mo, wb["w_mem_br"], [F32], tb=True, name="mem_br")

    def gate_fn(r, b):
        zgt, b0, b1, b2 = r
        gt = _sigmoid(zgt + b[0])
        return [gt[:, :D_MODEL] * b0 + gt[:, D_MODEL:2 * D_MODEL] * b1 + gt[:, 2 * D_MODEL:] * b2], []
    merged = _ew(gate_fn, [zg, br_ssm, br_attn, br_mem], [sp["b_gate"]], [(D_MODEL, BF16)], [], name="gate_merge")[0]
    h1 = _mm(merged, wb["w_o"], [F32], epi=lambda acc, xt: (acc + xt,), mn=[x], name="o_proj")
    n2 = _rms_fwd(h1, sp["norm2_g"], "rms2")

    def up_epi(acc):
        ra = jnp.maximum(acc, 0.0)
        return ra * ra, ra
    f_act, r_act = _mm(n2, wb["w_up"], [BF16, BF16], tb=True, epi=up_epi, name="mlp_up")
    h2 = _mm(f_act, wb["w_down"], [F32], epi=lambda acc, ht: (acc + ht,), mn=[h1], name="mlp_down")

    def final_fn(r, b):
        hv, tv = r
        gf = b[0]
        rs = lax.rsqrt(jnp.mean(hv * hv, axis=-1, keepdims=True) + RMS_EPS)
        err = hv * rs * gf - tv
        dy = err * (1.0 / D_MODEL)
        gd = dy * gf
        dh = rs * gd - hv * (rs * rs * rs) * jnp.mean(gd * hv, axis=-1, keepdims=True)
        loss = _colsum(jnp.sum(err * err, axis=-1, keepdims=True)) * (0.5 / D_MODEL)
        return [dh], [_colsum(dy * hv * rs), loss]
    dh2, d_final_g, loss = _ew(final_fn, [h2, tgt], [sp["final_g"]], [(D_MODEL, F32)], [D_MODEL, 1], name="final_loss")

    gw, gs = {}, {"final_g": d_final_g}
    d_act = _mm(dh2, wb["w_down"], [BF16], tb=True, epi=lambda acc, ra: (acc * 2.0 * ra.astype(F32),), mn=[r_act],
                name="mlp_down_dx")
    gw["w_down"] = _mm(f_act, dh2, [F32], ta=True, name="mlp_down_dw")
    dn2 = _mm(d_act, wb["w_up"], [F32], name="mlp_up_dx")
    gw["w_up"] = _mm(d_act, n2, [F32], ta=True, name="mlp_up_dw")
    dh1, gs["norm2_g"] = _rms_bwd(h1, dn2, dh2, sp["norm2_g"], "rms2_bwd")
    dmerged = _mm(dh1, wb["w_o"], [F32], tb=True, name="o_proj_dx")
    gw["w_o"] = _mm(merged, dh1, [F32], ta=True, name="o_proj_dw")

    def gate_bwd_fn(r, b):
        dm, zgt, b0, b1, b2 = r
        gt = _sigmoid(zgt + b[0])
        g0, g1, g2 = gt[:, :D_MODEL], gt[:, D_MODEL:2 * D_MODEL], gt[:, 2 * D_MODEL:]
        dzg = jnp.concatenate([dm * b0 * g0 * (1.0 - g0), dm * b1 * g1 * (1.0 - g1), dm * b2 * g2 * (1.0 - g2)], axis=1)
        return [dm * g0, dm * g1, dm * g2, dzg], [_colsum(dzg)]
    dbr_ssm, dbr_attn, dbr_mem, dzg, gs["b_gate"] = _ew(
        gate_bwd_fn, [dmerged, zg, br_ssm, br_attn, br_mem], [sp["b_gate"]],
        [(D_MODEL, BF16)] * 3 + [(ZG_W, BF16)], [ZG_W], name="gate_bwd")

    gw["w_ssm_br"] = _mm(dbr_ssm, y2, [F32], ta=True, name="ssm_br_dw")
    dy2 = _mm(dbr_ssm, wb["w_ssm_br"], [F32], name="ssm_br_dx")

    def glu_bwd_fn(r, b):
        dy, y1t, tt = r
        sg = _sigmoid(tt)
        dt = dy * y1t.astype(F32) * sg * (1.0 - sg)
        return [dt, dy * sg], [_colsum(dt)]
    dt_glu, dy1a, gs["b_glu"] = _ew(glu_bwd_fn, [dy2, y1, t_glu], [], [(SSM_W, BF16), (SSM_W, F32)], [SSM_W],
                                    name="ssm_glu_bwd", tm=512)
    gw["w_glu"] = _mm(y1, dt_glu, [F32], ta=True, name="ssm_glu_dw")

    def gelu_bwd_epi(acc, dy1t, y0t):
        th, c0, c1 = _gelu_parts(y0t)
        dg = 0.5 * (1.0 + th) + 0.5 * y0t * (1.0 - th * th) * c0 * (1.0 + 3.0 * c1 * y0t * y0t)
        return ((acc + dy1t) * dg,)
    dy0 = _mm(dt_glu, wb["w_glu"], [F32], tb=True, epi=gelu_bwd_epi, mn=[dy1a, y0], name="ssm_glu_dx")
    gs["ssm_d"] = _ew(lambda r, b: ([], [_colsum(r[0] * r[1].astype(F32))]), [dy0, u], [], [], [SSM_W],
                      name="ssm_dd", tm=512)[0]
    g_adj = _mm(dy0, cbd_t, [F32], name="ssm_cs_dx")
    dcr = _mm(dy0, s_all[0], [F32], ta=True, name="ssm_cs_dw_re")
    dci = _mm(dy0, s_all[1], [F32], ta=True, name="ssm_cs_dw_im")
    lam, da = _ssm_scan(g_adj, a_conj, reverse=True, s_fwd=s_all, name="ssm_scan_bwd")
    du = _mm(lam[0], b_re_t, [BF16], tb=True, pair2=(lam[1], b_im_t), epi=lambda acc, dyt, dr: (acc + dyt * dr,),
             mn=[dy0], rows=[d_row], name="ssm_bu_dx")
    dbr = _mm(u, lam[0], [F32], ta=True, name="ssm_bu_dw_re")
    dbi = _mm(u, lam[1], [F32], ta=True, name="ssm_bu_dw_im")
    gs["a_re"], gs["a_im"] = da[0], da[1]
    gs["bb_re"], gs["bb_im"] = _bd_in_diag(dbr), _bd_in_diag(dbi)
    gs["ssm_c_re"] = _bd_in_diag(dcr).transpose(0, 2, 1)
    gs["ssm_c_im"] = -_bd_in_diag(dci).transpose(0, 2, 1)

    gw["w_attn_br"] = _mm(dbr_attn, o_att, [F32], ta=True, name="attn_br_dw")

    def do_epi(acc, ot):
        prod = acc * ot
        head = lax.broadcasted_iota(jnp.int32, prod.shape, 1) // ATT_E
        dd = jnp.zeros_like(prod)
        for h in range(ATT_HG):
            dd = jnp.where(head == h, jnp.sum(jnp.where(head == h, prod, 0.0), axis=1, keepdims=True), dd)
        return acc, dd
    do_att, dd_att = _mm(dbr_attn, wb["w_attn_br"], [BF16, F32], epi=do_epi, mn=[o_att], name="attn_br_dx")
    dq_l, dk_l, dv_l = [], [], []
    for g, d in enumerate(DILATIONS):
        nb = l // d // ATT_WIN
        qp, kp, vp = qkv_p[g]
        dq, dkc, dkp, dvc, dvp = _attn_bwd(qp, kp, vp, _to_perm(do_att, d), _to_perm(lse_tot, d), _to_perm(dd_att, d),
                                           nb, "attn_bwd%d" % g)
        dk, dv = _attn_kv_combine(dkc, dkp, dvc, dvp, nb, "attn_kv_combine%d" % g)
        dq_l.append(_from_perm(dq, d))
        dk_l.append(_from_perm(dk, d))
        dv_l.append(_from_perm(dv, d))

    gw["w_mem_br"] = _mm(dbr_mem, mo, [F32], ta=True, name="mem_br_dw")
    dmo = _mm(dbr_mem, wb["w_mem_br"], [BF16], name="mem_br_dx")
    dmq, dkv = _mem_bwd(mq, kv, dmo, "mem_attn_bwd")
    gw["w_mem_kv"] = _mm(mn, dkv, [F32], ta=True, name="mem_kv_dw")
    dmn = _mm(dkv, wb["w_mem_kv"], [F32], tb=True, name="mem_kv_dx")
    gs["mem_norm_g"] = _rms_bwd(mem, dmn, None, sp["mem_norm_g"], "rms_mem_bwd")[1]

    dza = jnp.concatenate([du] + dq_l + dk_l + dv_l + [dmq], axis=1)
    dw_a = _mm(dza, n1, [F32], ta=True, name="in_proj_a_dw", tm=1664)
    dw_g = _mm(dzg, n1, [F32], ta=True, name="in_proj_g_dw")
    gw["w_in"] = jnp.concatenate([dw_a, dw_g], axis=0)
    dn_a = _mm(dza, w_a, [F32], name="in_proj_a_dx", tk=1664)
    dn1 = _mm(dzg, w_g, [F32], epi=lambda acc, pt: (acc + pt,), mn=[dn_a], name="in_proj_g_dx")
    grad_x, gs["norm1_g"] = _rms_bwd(x, dn1, dh1, sp["norm1_g"], "rms1_bwd")
    return loss, grad_x, gw, gs


_SMALL_GRAD_ORDER = ("norm1_g", "mem_norm_g", "b_gate", "a_re", "a_im", "bb_re", "bb_im", "ssm_c_re", "ssm_c_im",
                     "ssm_d", "b_glu", "norm2_g", "final_g")


def kernel(x, mem, norm1_g, mem_norm_g, w_in, b_gate, ssm_lambda_re, ssm_lambda_im, ssm_log_dt, ssm_b_re, ssm_b_im, ssm_c_re, ssm_c_im, ssm_d, w_glu, b_glu, w_ssm_br, w_attn_br, w_mem_kv, w_mem_br, w_o, norm2_g, w_up, w_down, final_g, loss_target, m_norm1_g, m_mem_norm_g, m_w_in, m_b_gate, m_ssm_lambda_re, m_ssm_lambda_im, m_ssm_log_dt, m_ssm_b_re, m_ssm_b_im, m_ssm_c_re, m_ssm_c_im, m_ssm_d, m_w_glu, m_b_glu, m_w_ssm_br, m_w_attn_br, m_w_mem_kv, m_w_mem_br, m_w_o, m_norm2_g, m_w_up, m_w_down, m_final_g, v_norm1_g, v_mem_norm_g, v_w_in, v_b_gate, v_ssm_lambda_re, v_ssm_lambda_im, v_ssm_log_dt, v_ssm_b_re, v_ssm_b_im, v_ssm_c_re, v_ssm_c_im, v_ssm_d, v_w_glu, v_b_glu, v_w_ssm_br, v_w_attn_br, v_w_mem_kv, v_w_mem_br, v_w_o, v_norm2_g, v_w_up, v_w_down, v_final_g):
    args = dict(locals())
    w = {n: args[n] for n in ALL_W}
    m = {n: args["m_" + n] for n in ALL_W}
    v = {n: args["v_" + n] for n in ALL_W}
    my_c = lax.axis_index("c").astype(jnp.int32).reshape(1)
    my_chip = (2 * lax.axis_index("x") + lax.axis_index("y")).astype(jnp.int32).reshape(1)

    w_pack = [_pack_group(w, names) for names in GROUPS]
    wb = {}
    for gi, names in enumerate(GROUPS):
        w_all = _allgather(w_pack[gi].astype(BF16), "allgather_weights%d" % gi)
        for n, part in _split_group(w_all, names).items():
            wb[n] = _full_stored(part, n)

    sp = {
        "norm1_g": norm1_g, "mem_norm_g": mem_norm_g, "b_gate": b_gate, "b_glu": b_glu, "norm2_g": norm2_g,
        "final_g": final_g.reshape(1, D_MODEL),
        "ssm_lambda_re": ssm_lambda_re[0], "ssm_lambda_im": ssm_lambda_im[0], "ssm_log_dt": ssm_log_dt[0],
        "ssm_b_re": ssm_b_re[0], "ssm_b_im": ssm_b_im[0], "ssm_c_re": ssm_c_re[0], "ssm_c_im": ssm_c_im[0],
        "ssm_d": ssm_d[0],
    }
    loss, grad_x, gw, gs = _local_step(x[0], mem[0], loss_target[0], wb, sp)
    loss = lax.psum(loss[0, 0], ("x", "y", "c"))

    big = [{}, {}, {}, {}]
    for gi, names in enumerate(GROUPS):
        g_pack = jnp.concatenate([_stacked_stored(gw[n], n) for n in names], axis=1)
        t1 = _pair_exchange(g_pack, "grad_pair_exchange%d" % gi)
        p_sum, p_bf = _pair_sum(g_pack, t1, my_c, "grad_pair_sum%d" % gi, GROUP_TR[gi])
        t2 = _chip_exchange(p_bf, "grad_chip_exchange%d" % gi)
        outs = _adam_big(p_sum, t2, my_chip, w_pack[gi], _pack_group(m, names), _pack_group(v, names),
                         "adam_big%d" % gi, GROUP_TR[gi])
        for kind, buf in enumerate(outs):
            for n, part in _split_group(buf, names).items():
                big[kind][n] = _unstored(part, n)

    sg_shapes = [gs[n].shape for n in _SMALL_GRAD_ORDER]
    sg_all = _allgather(_pack([gs[n] for n in _SMALL_GRAD_ORDER]), "allgather_small_grads")
    sg = dict(zip(_SMALL_GRAD_ORDER, _unpack(_sum8(sg_all, "sum_small_grads"), sg_shapes)))
    _, disc_vjp = jax.vjp(_discretize, sp["ssm_lambda_re"], sp["ssm_lambda_im"], sp["ssm_log_dt"],
                          sp["ssm_b_re"], sp["ssm_b_im"])
    d_lre, d_lim, d_ldt, d_bre, d_bim = disc_vjp((sg["a_re"].reshape(SSM_G, SSM_P), sg["a_im"].reshape(SSM_G, SSM_P),
                                                  sg["bb_re"], sg["bb_im"]))
    small_grad = {
        "norm1_g": sg["norm1_g"], "mem_norm_g": sg["mem_norm_g"], "b_gate": sg["b_gate"],
        "ssm_lambda_re": d_lre, "ssm_lambda_im": d_lim, "ssm_log_dt": d_ldt, "ssm_b_re": d_bre, "ssm_b_im": d_bim,
        "ssm_c_re": sg["ssm_c_re"], "ssm_c_im": sg["ssm_c_im"], "ssm_d": sg["ssm_d"], "b_glu": sg["b_glu"],
        "norm2_g": sg["norm2_g"], "final_g": sg["final_g"],
    }
    small_grad = {n: small_grad[n].reshape(w[n].shape) for n in SMALL}
    s_shapes = [w[n].shape for n in SMALL]
    small_out = _adam_small(_pack([small_grad[n] for n in SMALL]), _pack([w[n] for n in SMALL]),
                            _pack([m[n] for n in SMALL]), _pack([v[n] for n in SMALL]), "adam_small")
    small = [small_grad] + [dict(zip(SMALL, _unpack(b, s_shapes))) for b in small_out]

    outs = [loss, grad_x[None]]
    for kind in range(4):
        for n in ALL_W:
            outs.append(big[kind][n] if n in BIG else small[kind][n])
    return tuple(outs)
```

```python
import math

import numpy as np
import jax
import jax.numpy as jnp
from jax import lax
from jax.experimental import pallas as pl
from jax.experimental.pallas import tpu as pltpu

F32 = jnp.float32
BF16 = jnp.bfloat16
_MXU = jnp.bfloat16

D_MODEL = 1024
SSM_G, SSM_H, SSM_P = 32, 16, 64
SSM_W = SSM_G * SSM_H
SSM_S = SSM_G * SSM_P
SSM_BD = 4
ATT_E = 64
ATT_HG = 4
ATT_GW = ATT_HG * ATT_E
ATT_WIN = 128
DILATIONS = (1, 4, 16)
MEM_H, MEM_E = 4, 128
MEM_W = MEM_H * MEM_E
ZA_W = SSM_W + 9 * ATT_GW + MEM_W
ZG_W = 3 * D_MODEL
IN_W = ZA_W + ZG_W
RMS_EPS = 1e-6
NEG_INF = -1e30

ADAM_LR, ADAM_B1, ADAM_B2, ADAM_EPS, ADAM_WD, ADAM_STEP = 0.001, 0.9, 0.999, 1e-08, 0.01, 10

N_DEV = 8
PACK_C = 512
_VMEM_LIMIT = 56 * 1024 * 1024
SUBLANES = 8
SCAN_SEG = 128
SCAN_CHAINS = 4
SCAN_W = 128
SCAN_NT = SSM_S // SCAN_W

BIG = ("w_in", "w_glu", "w_ssm_br", "w_attn_br", "w_mem_kv", "w_mem_br", "w_o", "w_up", "w_down")
BIG_SHAPE = {
    "w_in": (D_MODEL, IN_W, 1), "w_glu": (SSM_W, SSM_W, 0), "w_ssm_br": (SSM_W, D_MODEL, 1),
    "w_attn_br": (ATT_GW, D_MODEL, 1), "w_mem_kv": (D_MODEL, 2 * MEM_W, 0), "w_mem_br": (MEM_W, D_MODEL, 1),
    "w_o": (D_MODEL, D_MODEL, 0), "w_up": (D_MODEL, 4 * D_MODEL, 1), "w_down": (4 * D_MODEL, D_MODEL, 0),
}
SMALL = ("norm1_g", "mem_norm_g", "b_gate", "ssm_lambda_re", "ssm_lambda_im", "ssm_log_dt", "ssm_b_re",
         "ssm_b_im", "ssm_c_re", "ssm_c_im", "ssm_d", "b_glu", "norm2_g", "final_g")
ALL_W = ("norm1_g", "mem_norm_g", "w_in", "b_gate", "ssm_lambda_re", "ssm_lambda_im", "ssm_log_dt", "ssm_b_re",
         "ssm_b_im", "ssm_c_re", "ssm_c_im", "ssm_d", "w_glu", "b_glu", "w_ssm_br", "w_attn_br", "w_mem_kv",
         "w_mem_br", "w_o", "norm2_g", "w_up", "w_down", "final_g")


def _params(sem):
    return pltpu.CompilerParams(dimension_semantics=sem, vmem_limit_bytes=_VMEM_LIMIT)


def _pick(n, cap):
    if n <= cap:
        return n
    t = (cap // 128) * 128
    while n % t:
        t -= 128
    return t


def _mm(a, b, outs, *, name, ta=False, tb=False, epi=None, mn=(), rows=(), pair2=None, bd=0,
        tm=1024, tn=1024, tk=512):
    ab = [a, b] + (list(pair2) if pair2 is not None else [])
    m = a.shape[1] if ta else a.shape[0]
    k = a.shape[0] if ta else a.shape[1]
    n = b.shape[0] if tb else b.shape[1]
    assert k == (b.shape[1] if tb else b.shape[0]), (name, a.shape, b.shape)
    out_n = n
    if bd and ta:
        assert not tb
        tm, tn, tk = m // bd, n // bd, _pick(k, tk)
        grid, out_n = (bd, 1, k // tk), tn
        a_spec = pl.BlockSpec((tk, tm), lambda i, j, kk: (kk, i))
        b_spec = pl.BlockSpec((tk, tn), lambda i, j, kk: (kk, i))
        mn_spec = pl.BlockSpec((tm, tn), lambda i, j, kk: (i, 0))
    elif bd:
        tm, tn, tk = _pick(m, tm), n // bd, k // bd
        grid = (m // tm, bd, 1)
        a_spec = pl.BlockSpec((tm, tk), lambda i, j, kk: (i, j))
        b_spec = pl.BlockSpec((tn, tk) if tb else (tk, tn), lambda i, j, kk: (j, j))
        mn_spec = pl.BlockSpec((tm, tn), lambda i, j, kk: (i, j))
    else:
        tm, tn, tk = _pick(m, tm), _pick(n, tn), _pick(k, tk)
        grid = (m // tm, n // tn, k // tk)
        a_spec = (pl.BlockSpec((tk, tm), lambda i, j, kk: (kk, i)) if ta
                  else pl.BlockSpec((tm, tk), lambda i, j, kk: (i, kk)))
        b_spec = (pl.BlockSpec((tn, tk), lambda i, j, kk: (j, kk)) if tb
                  else pl.BlockSpec((tk, tn), lambda i, j, kk: (kk, j)))
        mn_spec = pl.BlockSpec((tm, tn), lambda i, j, kk: (i, j))
    nk = grid[2]
    row_spec = pl.BlockSpec((1, tn), lambda i, j, kk: (0, j))
    n_ex, n_out = len(mn) + len(rows), len(outs)
    dims = (((0 if ta else 1,), (1 if tb else 0,)), ((), ()))

    def body(*refs):
        ab_refs, rest = refs[:len(ab)], refs[len(ab):]
        ex, o_refs, acc = rest[:n_ex], rest[n_ex:n_ex + n_out], rest[-1]
        kk = pl.program_id(2)

        @pl.when(kk == 0)
        def _():
            acc[...] = jnp.zeros_like(acc)

        for a_ref, b_ref in zip(ab_refs[0::2], ab_refs[1::2]):
            acc[...] += lax.dot_general(a_ref[...].astype(_MXU), b_ref[...].astype(_MXU), dims,
                                        preferred_element_type=F32)

        @pl.when(kk == nk - 1)
        def _():
            vals = (acc[...],) if epi is None else epi(acc[...], *[r[...] for r in ex])
            for r, v in zip(o_refs, vals):
                r[...] = v.astype(r.dtype)

    res = pl.pallas_call(
        body, grid=grid,
        in_specs=[a_spec, b_spec] * (len(ab) // 2) + [mn_spec] * len(mn) + [row_spec] * len(rows),
        out_specs=[mn_spec] * n_out,
        out_shape=[jax.ShapeDtypeStruct((m, out_n), dt) for dt in outs],
        scratch_shapes=[pltpu.VMEM((tm, tn), F32)],
        compiler_params=_params(("parallel", "parallel", "arbitrary")), name=name,
    )(*ab, *mn, *rows)
    return res[0] if n_out == 1 else res


def _ew(fn, rows, bcs, out_rows, out_accs, *, name, tm=256):
    r = rows[0].shape[0]
    tm = min(tm, r)
    assert r % tm == 0
    nr, nb, no, na = len(rows), len(bcs), len(out_rows), len(out_accs)

    def body(*refs):
        i = pl.program_id(0)
        r_in, b_in = refs[:nr], refs[nr:nr + nb]
        o_r, o_a = refs[nr + nb:nr + nb + no], refs[nr + nb + no:]
        outs, accs = fn([x[...] for x in r_in], [x[...] for x in b_in])
        for ref, v in zip(o_r, outs):
            ref[...] = v.astype(ref.dtype)
        if na:
            @pl.when(i == 0)
            def _():
                for ref in o_a:
                    ref[...] = jnp.zeros_like(ref)

            for ref, v in zip(o_a, accs):
                ref[...] += v

    res = pl.pallas_call(
        body, grid=(r // tm,),
        in_specs=[pl.BlockSpec((tm, x.shape[1]), lambda i: (i, 0)) for x in rows]
        + [pl.BlockSpec((1, x.shape[1]), lambda i: (0, 0)) for x in bcs],
        out_specs=[pl.BlockSpec((tm, c), lambda i: (i, 0)) for c, _ in out_rows]
        + [pl.BlockSpec((1, c), lambda i: (0, 0)) for c in out_accs],
        out_shape=[jax.ShapeDtypeStruct((r, c), dt) for c, dt in out_rows]
        + [jax.ShapeDtypeStruct((1, c), F32) for c in out_accs],
        compiler_params=_params(("arbitrary",)), name=name,
    )(*rows, *bcs)
    return res


def _colsum(x):
    return jnp.sum(x, axis=0, keepdims=True)


def _sigmoid(x):
    return 1.0 / (1.0 + jnp.exp(-x))


def _rms_fwd(x, g, name):
    def fn(r, b):
        xv = r[0]
        rs = lax.rsqrt(jnp.mean(xv * xv, axis=-1, keepdims=True) + RMS_EPS)
        return [xv * rs * b[0]], []
    return _ew(fn, [x], [g], [(x.shape[1], BF16)], [], name=name)[0]


def _rms_bwd(x, dn, res, g, name):
    def fn(r, b):
        xv, dv = r[0], r[1]
        rs = lax.rsqrt(jnp.mean(xv * xv, axis=-1, keepdims=True) + RMS_EPS)
        gd = dv * b[0]
        dx = rs * gd - xv * (rs * rs * rs) * jnp.mean(gd * xv, axis=-1, keepdims=True)
        if res is not None:
            dx = dx + r[2]
        return [dx], [_colsum(dv * xv * rs)]
    rows = [x, dn] + ([res] if res is not None else [])
    return _ew(fn, rows, [g], [(x.shape[1], F32)], [x.shape[1]], name=name)


def _scan_order(x):
    l, c = x.shape
    return x.reshape(l // (SUBLANES * SCAN_SEG), SUBLANES, SCAN_SEG, c).transpose(0, 2, 1, 3).reshape(l, c)


def _time_order(x):
    l, c = x.shape
    return x.reshape(l // (SUBLANES * SCAN_SEG), SCAN_SEG, SUBLANES, c).transpose(0, 2, 1, 3).reshape(l, c)


def _ssm_scan(b, a_pair, *, reverse, s_fwd=None, name):
    l = b.shape[0]
    seg, w = SCAN_SEG, SCAN_W
    nch = min(SCAN_CHAINS, l // (SUBLANES * seg))
    chain_rows = SUBLANES * seg
    tb = nch * chain_rows
    nt = l // tb
    with_da = s_fwd is not None
    assert reverse or not with_da

    def tt(t):
        return nt - 1 - t if reverse else t

    def body(*refs):
        if with_da:
            br_ref, bi_ref, a_ref, sf_ref, sp_ref, s_ref, da_ref, p_ref, c_ref = refs
        else:
            br_ref, bi_ref, a_ref, s_ref, p_ref, c_ref = refs
        t_blk = pl.program_id(1)
        ar, ai = a_ref[0], a_ref[1]

        @pl.when(t_blk == 0)
        def _():
            def pstep(i, carry):
                pr, pi = carry
                p_ref[0, pl.ds(i, 1), :] = pr
                p_ref[1, pl.ds(i, 1), :] = pi
                return pr * ar - pi * ai, pr * ai + pi * ar

            lax.fori_loop(0, seg, pstep, (ar, ai))
            c_ref[...] = jnp.zeros_like(c_ref)
            if with_da:
                da_ref[...] = jnp.zeros_like(da_ref)

        arb, aib = jnp.broadcast_to(ar, (SUBLANES, w)), jnp.broadcast_to(ai, (SUBLANES, w))
        zero = jnp.zeros((SUBLANES, w), F32)

        def tile(g, step):
            return pl.ds(pl.multiple_of(g * chain_rows + step * SUBLANES, SUBLANES), SUBLANES)

        def rows(g, i):
            return tile(g, seg - 1 - i if reverse else i)

        def local_step(i, carry):
            out = []
            for g in range(nch):
                sr, si = carry[2 * g], carry[2 * g + 1]
                idx = rows(g, i)
                sr, si = arb * sr - aib * si + br_ref[idx, :], arb * si + aib * sr + bi_ref[idx, :]
                s_ref.at[0][idx, :] = sr
                s_ref.at[1][idx, :] = si
                out += [sr, si]
            return tuple(out)

        ends = lax.fori_loop(0, seg, local_step, (zero,) * (2 * nch), unroll=2)

        a_seg_r, a_seg_i = p_ref[0, seg - 1:seg, :], p_ref[1, seg - 1:seg, :]
        cr, ci = c_ref[0], c_ref[1]
        sub = lax.broadcasted_iota(jnp.int32, (SUBLANES, w), 0)
        ins = [[zero, zero] for _ in range(nch)]
        order = [(g, k) for g in range(nch) for k in range(SUBLANES)]
        for g, k in (order[::-1] if reverse else order):
            ins[g] = [jnp.where(sub == k, cr, ins[g][0]), jnp.where(sub == k, ci, ins[g][1])]
            er, ei = ends[2 * g][k:k + 1], ends[2 * g + 1][k:k + 1]
            cr, ci = er + a_seg_r * cr - a_seg_i * ci, ei + a_seg_r * ci + a_seg_i * cr
        c_ref[0] = cr
        c_ref[1] = ci

        def fix(g, i):
            idx = rows(g, i)
            pr, pi = p_ref[0, pl.ds(i, 1), :], p_ref[1, pl.ds(i, 1), :]
            sr = s_ref.at[0][idx, :] + pr * ins[g][0] - pi * ins[g][1]
            si = s_ref.at[1][idx, :] + pr * ins[g][1] + pi * ins[g][0]
            s_ref.at[0][idx, :] = sr
            s_ref.at[1][idx, :] = si
            return sr, si

        if not with_da:
            def fix_step(i, carry):
                for g in range(nch):
                    fix(g, i)
                return carry

            lax.fori_loop(0, seg, fix_step, 0, unroll=2)
        else:
            def adj_step(i, acc):
                acc_r, acc_i = acc
                for g in range(nch):
                    lr, li = fix(g, i)
                    prev = tile(g, seg - 2 - i)
                    fr, fi = sf_ref.at[0][prev, :], sf_ref.at[1][prev, :]
                    acc_r, acc_i = acc_r + lr * fr + li * fi, acc_i + li * fr - lr * fi
                return acc_r, acc_i

            acc_r, acc_i = lax.fori_loop(0, seg - 1, adj_step, (zero, zero), unroll=2)
            first_block = tt(t_blk) == 0
            for g in range(nch):
                lr, li = fix(g, seg - 1)
                seg_ends = tile(g, seg - 1)
                if g == 0:
                    pvr = jnp.where(first_block, 0.0, sp_ref[0, SUBLANES - 1:SUBLANES, :])
                    pvi = jnp.where(first_block, 0.0, sp_ref[1, SUBLANES - 1:SUBLANES, :])
                else:
                    pvr = sf_ref[0, g * chain_rows - 1:g * chain_rows, :]
                    pvi = sf_ref[1, g * chain_rows - 1:g * chain_rows, :]
                fr = jnp.where(sub == 0, pvr, pltpu.roll(sf_ref.at[0][seg_ends, :], 1, 0))
                fi = jnp.where(sub == 0, pvi, pltpu.roll(sf_ref.at[1][seg_ends, :], 1, 0))
                acc_r = acc_r + lr * fr + li * fi
                acc_i = acc_i + li * fr - lr * fi
            da_ref[0] += jnp.sum(acc_r, axis=0, keepdims=True)
            da_ref[1] += jnp.sum(acc_i, axis=0, keepdims=True)

    re_spec = pl.BlockSpec((tb, w), lambda j, t: (tt(t), 2 * j))
    im_spec = pl.BlockSpec((tb, w), lambda j, t: (tt(t), 2 * j + 1))
    a_spec = pl.BlockSpec((2, 1, w), lambda j, t: (0, 0, j))
    s_spec = pl.BlockSpec((2, tb, w), lambda j, t: (0, tt(t), j))
    in_specs, args = [re_spec, im_spec, a_spec], [b, b, a_pair]
    out_specs, out_shape = [s_spec], [jax.ShapeDtypeStruct((2, l, SSM_S), F32)]
    scratch = [pltpu.VMEM((2, seg, w), F32), pltpu.VMEM((2, 1, w), F32)]
    if with_da:
        in_specs += [s_spec, pl.BlockSpec((2, SUBLANES, w),
                                          lambda j, t: (0, jnp.maximum(tt(t) * (tb // SUBLANES) - 1, 0), j))]
        args += [s_fwd, s_fwd]
        out_specs.append(a_spec)
        out_shape.append(jax.ShapeDtypeStruct((2, 1, SSM_S), F32))
    res = pl.pallas_call(
        body, grid=(SSM_S // w, nt), in_specs=in_specs, out_specs=out_specs, out_shape=out_shape,
        scratch_shapes=scratch, compiler_params=_params(("parallel", "arbitrary")), name=name,
    )(*args)
    return res if with_da else res[0]


def _nt_dot(x, y):
    return lax.dot_general(x.astype(_MXU), y.astype(_MXU), (((1,), (1,)), ((), ())), preferred_element_type=F32)


def _tn_dot(x, y):
    return lax.dot_general(x.astype(_MXU), y.astype(_MXU), (((0,), (0,)), ((), ())), preferred_element_type=F32)


def _nn_dot(x, y):
    return jnp.dot(x.astype(_MXU), y.astype(_MXU), preferred_element_type=F32)


def _attn_masks(i, nb):
    qi = lax.broadcasted_iota(jnp.int32, (ATT_WIN, ATT_WIN), 0)
    kj = lax.broadcasted_iota(jnp.int32, (ATT_WIN, ATT_WIN), 1)
    has_prev = (i % nb) != 0
    return kj <= qi, jnp.logical_and(kj >= qi, has_prev)


def _attn_specs(l):
    cur = pl.BlockSpec((ATT_WIN, ATT_GW), lambda i: (i, 0))
    prev = pl.BlockSpec((ATT_WIN, ATT_GW), lambda i: (jnp.maximum(i - 1, 0), 0))
    return cur, prev


def _attn_fwd(q, k, v, nb, name):
    l = q.shape[0]
    scale = ATT_E ** -0.5

    def body(q_ref, kc_ref, kp_ref, vc_ref, vp_ref, o_ref, lse_ref):
        mask_c, mask_p = _attn_masks(pl.program_id(0), nb)
        for h in range(ATT_HG):
            sl = slice(h * ATT_E, (h + 1) * ATT_E)
            qh = q_ref[:, sl]
            sc = jnp.where(mask_c, _nt_dot(qh, kc_ref[:, sl]) * scale, NEG_INF)
            sp = jnp.where(mask_p, _nt_dot(qh, kp_ref[:, sl]) * scale, NEG_INF)
            mx = jnp.maximum(jnp.max(sc, axis=-1, keepdims=True), jnp.max(sp, axis=-1, keepdims=True))
            pc, pp = jnp.exp(sc - mx), jnp.exp(sp - mx)
            den = jnp.sum(pc, axis=-1, keepdims=True) + jnp.sum(pp, axis=-1, keepdims=True)
            o_ref[:, sl] = (_nn_dot(pc, vc_ref[:, sl]) + _nn_dot(pp, vp_ref[:, sl])) / den
            lse_ref[:, sl] = jnp.broadcast_to(mx + jnp.log(den), (ATT_WIN, ATT_E))

    cur, prev = _attn_specs(l)
    return pl.pallas_call(
        body, grid=(l // ATT_WIN,), in_specs=[cur, cur, prev, cur, prev], out_specs=[cur, cur],
        out_shape=[jax.ShapeDtypeStruct((l, ATT_GW), F32)] * 2,
        compiler_params=_params(("parallel",)), name=name,
    )(q, k, k, v, v)


def _attn_bwd(q, k, v, do, lse, dd, nb, name):
    l = q.shape[0]
    scale = ATT_E ** -0.5

    def body(q_ref, kc_ref, kp_ref, vc_ref, vp_ref, do_ref, lse_ref, dd_ref, dq_ref, dkc_ref, dkp_ref, dvc_ref, dvp_ref):
        mask_c, mask_p = _attn_masks(pl.program_id(0), nb)
        for h in range(ATT_HG):
            sl = slice(h * ATT_E, (h + 1) * ATT_E)
            qh, doh = q_ref[:, sl], do_ref[:, sl]
            kc, kp, vc, vp = kc_ref[:, sl], kp_ref[:, sl], vc_ref[:, sl], vp_ref[:, sl]
            lh = lse_ref[:, h * ATT_E:h * ATT_E + 1]
            dh = dd_ref[:, h * ATT_E:h * ATT_E + 1]
            pc = jnp.where(mask_c, jnp.exp(_nt_dot(qh, kc) * scale - lh), 0.0)
            pp = jnp.where(mask_p, jnp.exp(_nt_dot(qh, kp) * scale - lh), 0.0)
            dsc = pc * (_nt_dot(doh, vc) - dh) * scale
            dsp = pp * (_nt_dot(doh, vp) - dh) * scale
            dq_ref[:, sl] = (_nn_dot(dsc, kc) + _nn_dot(dsp, kp)).astype(dq_ref.dtype)
            dkc_ref[:, sl] = _tn_dot(dsc, qh)
            dkp_ref[:, sl] = _tn_dot(dsp, qh)
            dvc_ref[:, sl] = _tn_dot(pc, doh)
            dvp_ref[:, sl] = _tn_dot(pp, doh)

    cur, prev = _attn_specs(l)
    return pl.pallas_call(
        body, grid=(l // ATT_WIN,), in_specs=[cur, cur, prev, cur, prev, cur, cur, cur], out_specs=[cur] * 5,
        out_shape=[jax.ShapeDtypeStruct((l, ATT_GW), BF16)] + [jax.ShapeDtypeStruct((l, ATT_GW), F32)] * 4,
        compiler_params=_params(("parallel",)), name=name,
    )(q, k, k, v, v, do, lse, dd)


def _attn_kv_combine(dkc, dkp, dvc, dvp, nb, name):
    l = dkc.shape[0]
    nblk = l // ATT_WIN

    def body(kc_ref, kp_ref, vc_ref, vp_ref, dk_ref, dv_ref):
        has_next = ((pl.program_id(0) + 1) % nb) != 0
        dk_ref[...] = (kc_ref[...] + jnp.where(has_next, kp_ref[...], 0.0)).astype(dk_ref.dtype)
        dv_ref[...] = (vc_ref[...] + jnp.where(has_next, vp_ref[...], 0.0)).astype(dv_ref.dtype)

    cur = pl.BlockSpec((ATT_WIN, ATT_GW), lambda i: (i, 0))
    nxt = pl.BlockSpec((ATT_WIN, ATT_GW), lambda i: (jnp.minimum(i + 1, nblk - 1), 0))
    return pl.pallas_call(
        body, grid=(nblk,), in_specs=[cur, nxt, cur, nxt], out_specs=[cur, cur],
        out_shape=[jax.ShapeDtypeStruct((l, ATT_GW), BF16)] * 2,
        compiler_params=_params(("parallel",)), name=name,
    )(dkc, dkp, dvc, dvp)


def _to_perm(a, d):
    if d == 1:
        return a
    l, c = a.shape
    return a.reshape(l // d, d, c).transpose(1, 0, 2).reshape(l, c)


def _from_perm(a, d):
    if d == 1:
        return a
    l, c = a.shape
    return a.reshape(d, l // d, c).transpose(1, 0, 2).reshape(l, c)


def _mem_probs(qh, kh):
    s = _nt_dot(qh, kh) * (MEM_E ** -0.5)
    e = jnp.exp(s - jnp.max(s, axis=-1, keepdims=True))
    return e / jnp.sum(e, axis=-1, keepdims=True)


def _mem_fwd(mq, kv, name, tm=512):
    l, nm = mq.shape[0], kv.shape[0]

    def body(q_ref, kv_ref, o_ref):
        for h in range(MEM_H):
            sl = slice(h * MEM_E, (h + 1) * MEM_E)
            p = _mem_probs(q_ref[:, sl], kv_ref[:, sl])
            o_ref[:, sl] = _nn_dot(p, kv_ref[:, MEM_W + h * MEM_E:MEM_W + (h + 1) * MEM_E]).astype(o_ref.dtype)

    return pl.pallas_call(
        body, grid=(l // tm,),
        in_specs=[pl.BlockSpec((tm, MEM_W), lambda i: (i, 0)), pl.BlockSpec((nm, 2 * MEM_W), lambda i: (0, 0))],
        out_specs=pl.BlockSpec((tm, MEM_W), lambda i: (i, 0)),
        out_shape=jax.ShapeDtypeStruct((l, MEM_W), BF16),
        compiler_params=_params(("parallel",)), name=name,
    )(mq, kv)


def _mem_bwd(mq, kv, dmo, name, tm=512):
    l, nm = mq.shape[0], kv.shape[0]
    scale = MEM_E ** -0.5

    def body(q_ref, kv_ref, do_ref, dq_ref, dkv_ref):
        @pl.when(pl.program_id(0) == 0)
        def _():
            dkv_ref[...] = jnp.zeros_like(dkv_ref)

        for h in range(MEM_H):
            sl = slice(h * MEM_E, (h + 1) * MEM_E)
            vsl = slice(MEM_W + h * MEM_E, MEM_W + (h + 1) * MEM_E)
            qh, kh, vh, doh = q_ref[:, sl], kv_ref[:, sl], kv_ref[:, vsl], do_ref[:, sl]
            p = _mem_probs(qh, kh)
            dp = _nt_dot(doh, vh)
            ds = p * (dp - jnp.sum(dp * p, axis=-1, keepdims=True)) * scale
            dq_ref[:, sl] = _nn_dot(ds, kh).astype(dq_ref.dtype)
            dkv_ref[:, sl] += _tn_dot(ds, qh)
            dkv_ref[:, vsl] += _tn_dot(p, doh)

    row = pl.BlockSpec((tm, MEM_W), lambda i: (i, 0))
    full = pl.BlockSpec((nm, 2 * MEM_W), lambda i: (0, 0))
    return pl.pallas_call(
        body, grid=(l // tm,), in_specs=[row, full, row], out_specs=[row, full],
        out_shape=[jax.ShapeDtypeStruct((l, MEM_W), BF16), jax.ShapeDtypeStruct((nm, 2 * MEM_W), F32)],
        compiler_params=_params(("arbitrary",)), name=name,
    )(mq, kv, dmo)


def _discretize(lam_re, lam_im, log_dt, b_re, b_im):
    dt = jnp.exp(log_dt)[:, None]
    mag = jnp.exp(lam_re * dt)
    a_re, a_im = mag * jnp.cos(lam_im * dt), mag * jnp.sin(lam_im * dt)
    nr, ni = a_re - 1.0, a_im
    den = lam_re * lam_re + lam_im * lam_im
    coef_re = (nr * lam_re + ni * lam_im) / den
    coef_im = (ni * lam_re - nr * lam_im) / den
    bb_re = coef_re[..., None] * b_re - coef_im[..., None] * b_im
    bb_im = coef_re[..., None] * b_im + coef_im[..., None] * b_re
    return a_re, a_im, bb_re, bb_im


def _tile_cat(re, im):
    lead = re.shape[:-1]
    t = jnp.stack([re.reshape(*lead, SCAN_NT, SCAN_W), im.reshape(*lead, SCAN_NT, SCAN_W)], axis=-2)
    return t.reshape(*lead, 2 * SSM_S)


def _bd_in(bb):
    return jnp.einsum("gph,gk->ghkp", bb, jnp.eye(SSM_G, dtype=bb.dtype)).reshape(SSM_W, SSM_S)


def _bd_diag(x):
    gb = SSM_G // SSM_BD
    t = x.reshape(SSM_BD, gb, SSM_H, gb, SSM_P)
    return jnp.einsum("bghgp->bghp", t).reshape(SSM_G, SSM_H, SSM_P)


_ANY = pl.BlockSpec(memory_space=pl.ANY)
_MESH = pl.DeviceIdType.MESH


def _allgather(x, name):
    def body(x_ref, out_ref, send_sems, recv_sems, local_sem):
        mx, my, mc = lax.axis_index("x"), lax.axis_index("y"), lax.axis_index("c")
        me, sibling = (mx, my, mc), (mx, my, 1 - mc)
        chips = [(1 - mx, my), (mx, 1 - my), (1 - mx, 1 - my)]

        def blk(px, py, pc):
            return out_ref.at[4 * px + 2 * py + pc]

        def copy(k, block, to, src=None):
            return pltpu.make_async_remote_copy(
                src_ref=blk(*block) if src is None else src, dst_ref=blk(*block),
                send_sem=send_sems.at[k], recv_sem=recv_sems.at[k], device_id=to, device_id_type=_MESH)

        mine = pltpu.make_async_copy(x_ref, blk(*me), local_sem)
        mine.start()
        first = [copy(0, me, sibling, src=x_ref)]
        first += [copy(1 + j, me, (*chip, mc), src=x_ref) for j, chip in enumerate(chips)]
        for cp in first:
            cp.start()
        passed = [copy(4 + j, (*chip, mc), sibling) for j, chip in enumerate(chips)]
        for j, chip in enumerate(chips):
            copy(1 + j, (*chip, mc), me).wait_recv()
            passed[j].start()
        copy(0, sibling, me).wait_recv()
        for j, chip in enumerate(chips):
            copy(4 + j, (*chip, 1 - mc), me).wait_recv()
        for cp in first + passed:
            cp.wait_send()
        mine.wait()

    return pl.pallas_call(
        body, out_shape=jax.ShapeDtypeStruct((N_DEV,) + x.shape, x.dtype), in_specs=[_ANY], out_specs=_ANY,
        scratch_shapes=[pltpu.SemaphoreType.DMA((7,)), pltpu.SemaphoreType.DMA((7,)), pltpu.SemaphoreType.DMA],
        name=name,
    )(x)


def _pair_exchange(g, name):
    def body(g_ref, out_ref, send_sems, recv_sems):
        mx, my, mc = lax.axis_index("x"), lax.axis_index("y"), lax.axis_index("c")
        copies = [pltpu.make_async_remote_copy(
            src_ref=g_ref.at[2 * k + (1 - mc)], dst_ref=out_ref.at[k], send_sem=send_sems.at[k],
            recv_sem=recv_sems.at[k], device_id=(mx, my, 1 - mc), device_id_type=_MESH) for k in range(4)]
        for cp in copies:
            cp.start()
        for cp in copies:
            cp.wait()

    return pl.pallas_call(
        body, out_shape=jax.ShapeDtypeStruct((4,) + g.shape[1:], g.dtype), in_specs=[_ANY], out_specs=_ANY,
        scratch_shapes=[pltpu.SemaphoreType.DMA((4,)), pltpu.SemaphoreType.DMA((4,))], name=name,
    )(g)


def _chip_exchange(p, name):
    def body(p_ref, out_ref, send_sems, recv_sems):
        mx, my, mc = lax.axis_index("x"), lax.axis_index("y"), lax.axis_index("c")
        chips = [(1 - mx, my), (mx, 1 - my), (1 - mx, 1 - my)]
        copies = [pltpu.make_async_remote_copy(
            src_ref=p_ref.at[2 * px + py], dst_ref=out_ref.at[j], send_sem=send_sems.at[j],
            recv_sem=recv_sems.at[j], device_id=(px, py, mc), device_id_type=_MESH)
            for j, (px, py) in enumerate(chips)]
        for cp in copies:
            cp.start()
        for cp in copies:
            cp.wait()

    return pl.pallas_call(
        body, out_shape=jax.ShapeDtypeStruct((3,) + p.shape[1:], p.dtype), in_specs=[_ANY], out_specs=_ANY,
        scratch_shapes=[pltpu.SemaphoreType.DMA((3,)), pltpu.SemaphoreType.DMA((3,))], name=name,
    )(p)


def _pair_sum(g, t1, my_c, name, tr):
    _, r, c = g.shape

    def body(c_ref, g_ref, t_ref, o_ref, ob_ref):
        s = g_ref[...] + t_ref[...]
        o_ref[...] = s
        ob_ref[...] = s.astype(BF16)

    blk = pl.BlockSpec((None, tr, c), lambda k, i, cr: (k, i, 0))
    return pl.pallas_call(
        body,
        grid_spec=pltpu.PrefetchScalarGridSpec(
            num_scalar_prefetch=1, grid=(4, r // tr),
            in_specs=[pl.BlockSpec((None, tr, c), lambda k, i, cr: (2 * k + cr[0], i, 0)), blk],
            out_specs=[blk, blk]),
        out_shape=[jax.ShapeDtypeStruct((4, r, c), F32), jax.ShapeDtypeStruct((4, r, c), BF16)],
        compiler_params=_params(("parallel", "parallel")), name=name,
    )(my_c, g, t1)


def _adam_math(g, w, m, v):
    m = ADAM_B1 * m + (1.0 - ADAM_B1) * g
    v = ADAM_B2 * v + (1.0 - ADAM_B2) * (g * g)
    m_hat = m / (1.0 - ADAM_B1 ** ADAM_STEP)
    v_hat = v / (1.0 - ADAM_B2 ** ADAM_STEP)
    delta = -ADAM_LR * (m_hat / (jnp.sqrt(v_hat) + ADAM_EPS) + ADAM_WD * w)
    return delta, m, v


def _adam_big(p, t2, my_chip, w, m, v, name, tr):
    r, c = w.shape

    def body(k_ref, p_ref, t0_ref, t1_ref, t2_ref, w_ref, m_ref, v_ref, g_out, d_out, m_out, v_out):
        g = ((p_ref[...] + t0_ref[...].astype(F32)) + t1_ref[...].astype(F32)) + t2_ref[...].astype(F32)
        d, mn, vn = _adam_math(g, w_ref[...], m_ref[...], v_ref[...])
        g_out[...], d_out[...], m_out[...], v_out[...] = g, d, mn, vn

    flat = pl.BlockSpec((tr, c), lambda i, kr: (i, 0))

    def rel(j):
        return pl.BlockSpec((None, tr, c), lambda i, kr: (j, i, 0))

    return pl.pallas_call(
        body,
        grid_spec=pltpu.PrefetchScalarGridSpec(
            num_scalar_prefetch=1, grid=(r // tr,),
            in_specs=[pl.BlockSpec((None, tr, c), lambda i, kr: (kr[0], i, 0)), rel(0), rel(1), rel(2), flat, flat, flat],
            out_specs=[flat] * 4),
        out_shape=[jax.ShapeDtypeStruct((r, c), F32)] * 4,
        compiler_params=_params(("parallel",)), name=name,
    )(my_chip, p, t2, t2, t2, w, m, v)


def _sum8(g8, name):
    _, r, c = g8.shape

    def body(g_ref, o_ref):
        acc = g_ref[0]
        for j in range(1, N_DEV):
            acc = acc + g_ref[j]
        o_ref[...] = acc

    return pl.pallas_call(
        body, grid=(1,), in_specs=[pl.BlockSpec((N_DEV, r, c), lambda i: (0, 0, 0))],
        out_specs=pl.BlockSpec((r, c), lambda i: (0, 0)), out_shape=jax.ShapeDtypeStruct((r, c), F32),
        compiler_params=_params(("arbitrary",)), name=name,
    )(g8)


def _adam_small(g, w, m, v, name):
    def fn(r, b):
        return list(_adam_math(*r)), []
    c = g.shape[1]
    return _ew(fn, [g, w, m, v], [], [(c, F32)] * 3, [], name=name, tm=g.shape[0])


def _pack(arrs, pad_rows=8):
    flat = jnp.concatenate([a.reshape(-1) for a in arrs])
    n = flat.shape[0]
    q = PACK_C * pad_rows
    tot = -(-n // q) * q
    if tot != n:
        flat = jnp.concatenate([flat, jnp.zeros((tot - n,), flat.dtype)])
    return flat.reshape(tot // PACK_C, PACK_C)


def _unpack(buf, shapes):
    flat = buf.reshape(-1)
    out, off = [], 0
    for s in shapes:
        n = int(np.prod(s))
        out.append(flat[off:off + n].reshape(s))
        off += n
    return out


GROUPS = (("w_in", "w_mem_kv", "w_o", "w_up", "w_down"),
          ("w_glu", "w_ssm_br", "w_mem_br", "w_attn_br"))
GROUP_TR = (208, 384)
ATTN_BR_FOLD = 2


def _stored_shape(name):
    r, c, ax = BIG_SHAPE[name]
    rows, cols = (r // N_DEV, c) if ax == 0 else (c // N_DEV, r)
    return (rows // ATTN_BR_FOLD, cols * ATTN_BR_FOLD) if name == "w_attn_br" else (rows, cols)


def _stored(shard, name):
    a = shard[0].T if BIG_SHAPE[name][2] == 1 else shard[0]
    return a.reshape(_stored_shape(name))


def _unstored(a, name):
    r, c, ax = BIG_SHAPE[name]
    if ax == 0:
        return a.reshape(1, r // N_DEV, c)
    return a.reshape(c // N_DEV, r).T[None]


def _pack_group(d, names):
    return jnp.concatenate([_stored(d[n], n) for n in names], axis=0)


def _split_group(buf, names):
    out, off = {}, 0
    for n in names:
        rows = _stored_shape(n)[0]
        out[n] = buf[..., off:off + rows, :]
        off += rows
    return out


def _full_stored(stacked, name):
    r, c, ax = BIG_SHAPE[name]
    return stacked.reshape((r, c) if ax == 0 else (c, r))


def _stacked_stored(full, name):
    return full.reshape((N_DEV,) + _stored_shape(name))


def _gelu_parts(x):
    c0, c1 = math.sqrt(2.0 / math.pi), 0.044715
    th = jnp.tanh(c0 * (x + c1 * x * x * x))
    return th, c0, c1


def _local_step(x, mem, tgt, wb, sp):
    l = x.shape[0]
    w_a, w_g = wb["w_in"][:ZA_W], wb["w_in"][ZA_W:]

    a_re, a_im, bb_re, bb_im = _discretize(sp["ssm_lambda_re"], sp["ssm_lambda_im"], sp["ssm_log_dt"],
                                           sp["ssm_b_re"], sp["ssm_b_im"])
    a_pair = jnp.stack([a_re.reshape(1, SSM_S), a_im.reshape(1, SSM_S)])
    a_conj = jnp.stack([a_re.reshape(1, SSM_S), -a_im.reshape(1, SSM_S)])
    b_re_t, b_im_t = _bd_in(bb_re).astype(BF16), _bd_in(bb_im).astype(BF16)
    c_re_t = _bd_in(sp["ssm_c_re"].transpose(0, 2, 1)).astype(BF16)
    c_im_t = (-_bd_in(sp["ssm_c_im"].transpose(0, 2, 1))).astype(BF16)
    bbd = _tile_cat(b_re_t, b_im_t)
    cbd_t = _tile_cat(c_re_t, c_im_t)
    d_row = sp["ssm_d"].reshape(1, SSM_W)

    n1 = _rms_fwd(x, sp["norm1_g"], "rms1")
    za = _mm(n1, w_a, [BF16], tb=True, name="in_proj_a", tn=1664)
    zg = _mm(n1, w_g, [F32], tb=True, name="in_proj_g")
    u = za[:, :SSM_W]
    mq = za[:, ZA_W - MEM_W:]

    u_s = _scan_order(u)
    bu = _mm(u_s, bbd, [F32], bd=SSM_BD, name="ssm_bu")
    s_all = _ssm_scan(bu, a_pair, reverse=False, name="ssm_scan_fwd")
    ys = _time_order(_mm(s_all[0], c_re_t, [F32], tb=True, pair2=(s_all[1], c_im_t), bd=SSM_BD, name="ssm_cs"))

    def gelu_fn(r, b):
        y0 = r[0] + b[0] * r[1].astype(F32)
        th, _, _ = _gelu_parts(y0)
        return [y0, 0.5 * y0 * (1.0 + th)], []
    y0, y1 = _ew(gelu_fn, [ys, u], [d_row], [(SSM_W, F32), (SSM_W, BF16)], [], name="ssm_gelu", tm=512)

    def glu_epi(acc, y1t, bg):
        t = acc + bg
        return t, y1t.astype(F32) * _sigmoid(t)
    t_glu, y2 = _mm(y1, wb["w_glu"], [F32, BF16], epi=glu_epi, mn=[y1], rows=[sp["b_glu"]], name="ssm_glu")
    br_ssm = _mm(y2, wb["w_ssm_br"], [F32], tb=True, name="ssm_br")

    qkv_p, o_g, lse_g = [], [], []
    for g, d in enumerate(DILATIONS):
        nb = l // d // ATT_WIN
        cols = [za[:, SSM_W + (3 * j + g) * ATT_GW: SSM_W + (3 * j + g + 1) * ATT_GW] for j in range(3)]
        qp, kp, vp = [_to_perm(cc, d) for cc in cols]
        qkv_p.append((qp, kp, vp))
        og, lg = _attn_fwd(qp, kp, vp, nb, "attn_fwd%d" % g)
        o_g.append(_from_perm(og, d))
        lse_g.append(_from_perm(lg, d))

    def merge_fn(r, b):
        o0, o1, o2, l0, l1, l2 = r
        mx = jnp.maximum(jnp.maximum(l0, l1), l2)
        e0, e1, e2 = jnp.exp(l0 - mx), jnp.exp(l1 - mx), jnp.exp(l2 - mx)
        tot = e0 + e1 + e2
        return [(e0 * o0 + e1 * o1 + e2 * o2) / tot, mx + jnp.log(tot)], []
    o_att, lse_tot = _ew(merge_fn, o_g + lse_g, [], [(ATT_GW, F32), (ATT_GW, F32)], [], name="attn_merge", tm=512)
    br_attn = _mm(o_att, wb["w_attn_br"], [F32], tb=True, name="attn_br")

    mn = _rms_fwd(mem, sp["mem_norm_g"], "rms_mem")
    kv = _mm(mn, wb["w_mem_kv"], [BF16], name="mem_kv")
    mo = _mem_fwd(mq, kv, "mem_attn_fwd")
    br_mem = _mm(mo, wb["w_mem_br"], [F32], tb=True, name="mem_br")

    def gate_fn(r, b):
        zgt, b0, b1, b2 = r
        gt = _sigmoid(zgt + b[0])
        return [gt[:, :D_MODEL] * b0 + gt[:, D_MODEL:2 * D_MODEL] * b1 + gt[:, 2 * D_MODEL:] * b2], []
    merged = _ew(gate_fn, [zg, br_ssm, br_attn, br_mem], [sp["b_gate"]], [(D_MODEL, BF16)], [], name="gate_merge")[0]
    h1 = _mm(merged, wb["w_o"], [F32], epi=lambda acc, xt: (acc + xt,), mn=[x], name="o_proj")
    n2 = _rms_fwd(h1, sp["norm2_g"], "rms2")

    def up_epi(acc):
        ra = jnp.maximum(acc, 0.0)
        return ra * ra, ra
    f_act, r_act = _mm(n2, wb["w_up"], [BF16, BF16], tb=True, epi=up_epi, name="mlp_up")
    h2 = _mm(f_act, wb["w_down"], [F32], epi=lambda acc, ht: (acc + ht,), mn=[h1], name="mlp_down")

    def final_fn(r, b):
        hv, tv = r
        gf = b[0]
        rs = lax.rsqrt(jnp.mean(hv * hv, axis=-1, keepdims=True) + RMS_EPS)
        err = hv * rs * gf - tv
        dy = err * (1.0 / D_MODEL)
        gd = dy * gf
        dh = rs * gd - hv * (rs * rs * rs) * jnp.mean(gd * hv, axis=-1, keepdims=True)
        loss = _colsum(jnp.sum(err * err, axis=-1, keepdims=True)) * (0.5 / D_MODEL)
        return [dh], [_colsum(dy * hv * rs), loss]
    dh2, d_final_g, loss = _ew(final_fn, [h2, tgt], [sp["final_g"]], [(D_MODEL, F32)], [D_MODEL, 1], name="final_loss")

    gw, gs = {}, {"final_g": d_final_g}
    d_act = _mm(dh2, wb["w_down"], [BF16], tb=True, epi=lambda acc, ra: (acc * 2.0 * ra.astype(F32),), mn=[r_act],
                name="mlp_down_dx")
    gw["w_down"] = _mm(f_act, dh2, [F32], ta=True, name="mlp_down_dw")
    dn2 = _mm(d_act, wb["w_up"], [F32], name="mlp_up_dx")
    gw["w_up"] = _mm(d_act, n2, [F32], ta=True, name="mlp_up_dw")
    dh1, gs["norm2_g"] = _rms_bwd(h1, dn2, dh2, sp["norm2_g"], "rms2_bwd")
    dmerged = _mm(dh1, wb["w_o"], [F32], tb=True, name="o_proj_dx")
    gw["w_o"] = _mm(merged, dh1, [F32], ta=True, name="o_proj_dw")

    def gate_bwd_fn(r, b):
        dm, zgt, b0, b1, b2 = r
        gt = _sigmoid(zgt + b[0])
        g0, g1, g2 = gt[:, :D_MODEL], gt[:, D_MODEL:2 * D_MODEL], gt[:, 2 * D_MODEL:]
        dzg = jnp.concatenate([dm * b0 * g0 * (1.0 - g0), dm * b1 * g1 * (1.0 - g1), dm * b2 * g2 * (1.0 - g2)], axis=1)
        return [dm * g0, dm * g1, dm * g2, dzg], [_colsum(dzg)]
    dbr_ssm, dbr_attn, dbr_mem, dzg, gs["b_gate"] = _ew(
        gate_bwd_fn, [dmerged, zg, br_ssm, br_attn, br_mem], [sp["b_gate"]],
        [(D_MODEL, BF16)] * 3 + [(ZG_W, BF16)], [ZG_W], name="gate_bwd")

    gw["w_ssm_br"] = _mm(dbr_ssm, y2, [F32], ta=True, name="ssm_br_dw")
    dy2 = _mm(dbr_ssm, wb["w_ssm_br"], [F32], name="ssm_br_dx")

    def glu_bwd_fn(r, b):
        dy, y1t, tt = r
        sg = _sigmoid(tt)
        dt = dy * y1t.astype(F32) * sg * (1.0 - sg)
        return [dt, dy * sg], [_colsum(dt)]
    dt_glu, dy1a, gs["b_glu"] = _ew(glu_bwd_fn, [dy2, y1, t_glu], [], [(SSM_W, BF16), (SSM_W, F32)], [SSM_W],
                                    name="ssm_glu_bwd", tm=512)
    gw["w_glu"] = _mm(y1, dt_glu, [F32], ta=True, name="ssm_glu_dw")

    def gelu_bwd_epi(acc, dy1t, y0t):
        th, c0, c1 = _gelu_parts(y0t)
        dg = 0.5 * (1.0 + th) + 0.5 * y0t * (1.0 - th * th) * c0 * (1.0 + 3.0 * c1 * y0t * y0t)
        return ((acc + dy1t) * dg,)
    dy0 = _mm(dt_glu, wb["w_glu"], [F32], tb=True, epi=gelu_bwd_epi, mn=[dy1a, y0], name="ssm_glu_dx")
    gs["ssm_d"] = _ew(lambda r, b: ([], [_colsum(r[0] * r[1].astype(F32))]), [dy0, u], [], [], [SSM_W],
                      name="ssm_dd", tm=512)[0]
    dy0_s = _scan_order(dy0)
    g_adj = _mm(dy0_s, cbd_t, [F32], bd=SSM_BD, name="ssm_cs_dx")
    dcr = _mm(dy0_s, s_all[0], [F32], ta=True, bd=SSM_BD, name="ssm_cs_dw_re")
    dci = _mm(dy0_s, s_all[1], [F32], ta=True, bd=SSM_BD, name="ssm_cs_dw_im")
    lam, da = _ssm_scan(g_adj, a_conj, reverse=True, s_fwd=s_all, name="ssm_scan_bwd")
    du = _time_order(_mm(lam[0], b_re_t, [BF16], tb=True, pair2=(lam[1], b_im_t),
                         epi=lambda acc, dyt, dr: (acc + dyt * dr,), mn=[dy0_s], rows=[d_row], bd=SSM_BD, name="ssm_bu_dx"))
    dbr = _mm(u_s, lam[0], [F32], ta=True, bd=SSM_BD, name="ssm_bu_dw_re")
    dbi = _mm(u_s, lam[1], [F32], ta=True, bd=SSM_BD, name="ssm_bu_dw_im")
    gs["a_re"], gs["a_im"] = da[0], da[1]
    gs["bb_re"], gs["bb_im"] = _bd_diag(dbr).transpose(0, 2, 1), _bd_diag(dbi).transpose(0, 2, 1)
    gs["ssm_c_re"], gs["ssm_c_im"] = _bd_diag(dcr), -_bd_diag(dci)

    gw["w_attn_br"] = _mm(dbr_attn, o_att, [F32], ta=True, name="attn_br_dw")

    def do_epi(acc, ot):
        prod = acc * ot
        head = lax.broadcasted_iota(jnp.int32, prod.shape, 1) // ATT_E
        dd = jnp.zeros_like(prod)
        for h in range(ATT_HG):
            dd = jnp.where(head == h, jnp.sum(jnp.where(head == h, prod, 0.0), axis=1, keepdims=True), dd)
        return acc, dd
    do_att, dd_att = _mm(dbr_attn, wb["w_attn_br"], [BF16, F32], epi=do_epi, mn=[o_att], name="attn_br_dx")
    dq_l, dk_l, dv_l = [], [], []
    for g, d in enumerate(DILATIONS):
        nb = l // d // ATT_WIN
        qp, kp, vp = qkv_p[g]
        dq, dkc, dkp, dvc, dvp = _attn_bwd(qp, kp, vp, _to_perm(do_att, d), _to_perm(lse_tot, d), _to_perm(dd_att, d),
                                           nb, "attn_bwd%d" % g)
        dk, dv = _attn_kv_combine(dkc, dkp, dvc, dvp, nb, "attn_kv_combine%d" % g)
        dq_l.append(_from_perm(dq, d))
        dk_l.append(_from_perm(dk, d))
        dv_l.append(_from_perm(dv, d))

    gw["w_mem_br"] = _mm(dbr_mem, mo, [F32], ta=True, name="mem_br_dw")
    dmo = _mm(dbr_mem, wb["w_mem_br"], [BF16], name="mem_br_dx")
    dmq, dkv = _mem_bwd(mq, kv, dmo, "mem_attn_bwd")
    gw["w_mem_kv"] = _mm(mn, dkv, [F32], ta=True, name="mem_kv_dw")
    dmn = _mm(dkv, wb["w_mem_kv"], [F32], tb=True, name="mem_kv_dx")
    gs["mem_norm_g"] = _rms_bwd(mem, dmn, None, sp["mem_norm_g"], "rms_mem_bwd")[1]

    dza = jnp.concatenate([du] + dq_l + dk_l + dv_l + [dmq], axis=1)
    dw_a = _mm(dza, n1, [F32], ta=True, name="in_proj_a_dw", tm=1664)
    dw_g = _mm(dzg, n1, [F32], ta=True, name="in_proj_g_dw")
    gw["w_in"] = jnp.concatenate([dw_a, dw_g], axis=0)
    dn_a = _mm(dza, w_a, [F32], name="in_proj_a_dx", tk=1664)
    dn1 = _mm(dzg, w_g, [F32], epi=lambda acc, pt: (acc + pt,), mn=[dn_a], name="in_proj_g_dx")
    grad_x, gs["norm1_g"] = _rms_bwd(x, dn1, dh1, sp["norm1_g"], "rms1_bwd")
    return loss, grad_x, gw, gs


_SMALL_GRAD_ORDER = ("norm1_g", "mem_norm_g", "b_gate", "a_re", "a_im", "bb_re", "bb_im", "ssm_c_re", "ssm_c_im",
                     "ssm_d", "b_glu", "norm2_g", "final_g")


def kernel(x, mem, norm1_g, mem_norm_g, w_in, b_gate, ssm_lambda_re, ssm_lambda_im, ssm_log_dt, ssm_b_re, ssm_b_im, ssm_c_re, ssm_c_im, ssm_d, w_glu, b_glu, w_ssm_br, w_attn_br, w_mem_kv, w_mem_br, w_o, norm2_g, w_up, w_down, final_g, loss_target, m_norm1_g, m_mem_norm_g, m_w_in, m_b_gate, m_ssm_lambda_re, m_ssm_lambda_im, m_ssm_log_dt, m_ssm_b_re, m_ssm_b_im, m_ssm_c_re, m_ssm_c_im, m_ssm_d, m_w_glu, m_b_glu, m_w_ssm_br, m_w_attn_br, m_w_mem_kv, m_w_mem_br, m_w_o, m_norm2_g, m_w_up, m_w_down, m_final_g, v_norm1_g, v_mem_norm_g, v_w_in, v_b_gate, v_ssm_lambda_re, v_ssm_lambda_im, v_ssm_log_dt, v_ssm_b_re, v_ssm_b_im, v_ssm_c_re, v_ssm_c_im, v_ssm_d, v_w_glu, v_b_glu, v_w_ssm_br, v_w_attn_br, v_w_mem_kv, v_w_mem_br, v_w_o, v_norm2_g, v_w_up, v_w_down, v_final_g):
    args = dict(locals())
    w = {n: args[n] for n in ALL_W}
    m = {n: args["m_" + n] for n in ALL_W}
    v = {n: args["v_" + n] for n in ALL_W}
    my_c = lax.axis_index("c").astype(jnp.int32).reshape(1)
    my_chip = (2 * lax.axis_index("x") + lax.axis_index("y")).astype(jnp.int32).reshape(1)

    w_pack = [_pack_group(w, names) for names in GROUPS]
    wb = {}
    for gi, names in enumerate(GROUPS):
        w_all = _allgather(w_pack[gi].astype(BF16), "allgather_weights%d" % gi)
        for n, part in _split_group(w_all, names).items():
            wb[n] = _full_stored(part, n)

    sp = {
        "norm1_g": norm1_g, "mem_norm_g": mem_norm_g, "b_gate": b_gate, "b_glu": b_glu, "norm2_g": norm2_g,
        "final_g": final_g.reshape(1, D_MODEL),
        "ssm_lambda_re": ssm_lambda_re[0], "ssm_lambda_im": ssm_lambda_im[0], "ssm_log_dt": ssm_log_dt[0],
        "ssm_b_re": ssm_b_re[0], "ssm_b_im": ssm_b_im[0], "ssm_c_re": ssm_c_re[0], "ssm_c_im": ssm_c_im[0],
        "ssm_d": ssm_d[0],
    }
    loss, grad_x, gw, gs = _local_step(x[0], mem[0], loss_target[0], wb, sp)
    loss = lax.psum(loss[0, 0], ("x", "y", "c"))

    big = [{}, {}, {}, {}]
    for gi, names in enumerate(GROUPS):
        g_pack = jnp.concatenate([_stacked_stored(gw[n], n) for n in names], axis=1)
        t1 = _pair_exchange(g_pack, "grad_pair_exchange%d" % gi)
        p_sum, p_bf = _pair_sum(g_pack, t1, my_c, "grad_pair_sum%d" % gi, GROUP_TR[gi])
        t2 = _chip_exchange(p_bf, "grad_chip_exchange%d" % gi)
        outs = _adam_big(p_sum, t2, my_chip, w_pack[gi], _pack_group(m, names), _pack_group(v, names),
                         "adam_big%d" % gi, GROUP_TR[gi])
        for kind, buf in enumerate(outs):
            for n, part in _split_group(buf, names).items():
                big[kind][n] = _unstored(part, n)

    sg_shapes = [gs[n].shape for n in _SMALL_GRAD_ORDER]
    sg_all = _allgather(_pack([gs[n] for n in _SMALL_GRAD_ORDER]), "allgather_small_grads")
    sg = dict(zip(_SMALL_GRAD_ORDER, _unpack(_sum8(sg_all, "sum_small_grads"), sg_shapes)))
    _, disc_vjp = jax.vjp(_discretize, sp["ssm_lambda_re"], sp["ssm_lambda_im"], sp["ssm_log_dt"],
                          sp["ssm_b_re"], sp["ssm_b_im"])
    d_lre, d_lim, d_ldt, d_bre, d_bim = disc_vjp((sg["a_re"].reshape(SSM_G, SSM_P), sg["a_im"].reshape(SSM_G, SSM_P),
                                                  sg["bb_re"], sg["bb_im"]))
    small_grad = {
        "norm1_g": sg["norm1_g"], "mem_norm_g": sg["mem_norm_g"], "b_gate": sg["b_gate"],
        "ssm_lambda_re": d_lre, "ssm_lambda_im": d_lim, "ssm_log_dt": d_ldt, "ssm_b_re": d_bre, "ssm_b_im": d_bim,
        "ssm_c_re": sg["ssm_c_re"], "ssm_c_im": sg["ssm_c_im"], "ssm_d": sg["ssm_d"], "b_glu": sg["b_glu"],
        "norm2_g": sg["norm2_g"], "final_g": sg["final_g"],
    }
    small_grad = {n: small_grad[n].reshape(w[n].shape) for n in SMALL}
    s_shapes = [w[n].shape for n in SMALL]
    small_out = _adam_small(_pack([small_grad[n] for n in SMALL]), _pack([w[n] for n in SMALL]),
                            _pack([m[n] for n in SMALL]), _pack([v[n] for n in SMALL]), "adam_small")
    small = [small_grad] + [dict(zip(SMALL, _unpack(b, s_shapes))) for b in small_out]

    outs = [loss, grad_x[None]]
    for kind in range(4):
        for n in ALL_W:
            outs.append(big[kind][n] if n in BIG else small[kind][n])
    return tuple(outs)
```

```python
import math

import numpy as np
import jax
import jax.numpy as jnp
from jax import lax
from jax.experimental import pallas as pl
from jax.experimental.pallas import tpu as pltpu

F32 = jnp.float32
BF16 = jnp.bfloat16
_MXU = jnp.bfloat16

D_MODEL = 1024
SSM_G, SSM_H, SSM_P = 32, 16, 64
SSM_W = SSM_G * SSM_H
SSM_S = SSM_G * SSM_P
SSM_BD = 4
ATT_E = 64
ATT_HG = 4
ATT_GW = ATT_HG * ATT_E
ATT_WIN = 128
ATT_QB = 4
DILATIONS = (1, 4, 16)
MEM_H, MEM_E = 4, 128
MEM_W = MEM_H * MEM_E
ZA_W = SSM_W + 9 * ATT_GW + MEM_W
ZG_W = 3 * D_MODEL
IN_W = ZA_W + ZG_W
RMS_EPS = 1e-6
NEG_INF = -1e30

ADAM_LR, ADAM_B1, ADAM_B2, ADAM_EPS, ADAM_WD, ADAM_STEP = 0.001, 0.9, 0.999, 1e-08, 0.01, 10

N_DEV = 8
PACK_C = 512
_VMEM_LIMIT = 56 * 1024 * 1024
SUBLANES = 8
SCAN_SEG = 128
SCAN_CHAINS = 4
SCAN_W = 128
SCAN_NT = SSM_S // SCAN_W

BIG = ("w_in", "w_glu", "w_ssm_br", "w_attn_br", "w_mem_kv", "w_mem_br", "w_o", "w_up", "w_down")
BIG_SHAPE = {
    "w_in": (D_MODEL, IN_W, 1), "w_glu": (SSM_W, SSM_W, 0), "w_ssm_br": (SSM_W, D_MODEL, 1),
    "w_attn_br": (ATT_GW, D_MODEL, 1), "w_mem_kv": (D_MODEL, 2 * MEM_W, 0), "w_mem_br": (MEM_W, D_MODEL, 1),
    "w_o": (D_MODEL, D_MODEL, 0), "w_up": (D_MODEL, 4 * D_MODEL, 1), "w_down": (4 * D_MODEL, D_MODEL, 0),
}
SMALL = ("norm1_g", "mem_norm_g", "b_gate", "ssm_lambda_re", "ssm_lambda_im", "ssm_log_dt", "ssm_b_re",
         "ssm_b_im", "ssm_c_re", "ssm_c_im", "ssm_d", "b_glu", "norm2_g", "final_g")
ALL_W = ("norm1_g", "mem_norm_g", "w_in", "b_gate", "ssm_lambda_re", "ssm_lambda_im", "ssm_log_dt", "ssm_b_re",
         "ssm_b_im", "ssm_c_re", "ssm_c_im", "ssm_d", "w_glu", "b_glu", "w_ssm_br", "w_attn_br", "w_mem_kv",
         "w_mem_br", "w_o", "norm2_g", "w_up", "w_down", "final_g")


def _params(sem):
    return pltpu.CompilerParams(dimension_semantics=sem, vmem_limit_bytes=_VMEM_LIMIT)


def _pick(n, cap):
    if n <= cap:
        return n
    t = (cap // 128) * 128
    while n % t:
        t -= 128
    return t


def _mm(a, b, outs, *, name, ta=False, tb=False, epi=None, mn=(), rows=(), pair2=None, bd=0,
        tm=1024, tn=1024, tk=1024):
    ab = [a, b] + (list(pair2) if pair2 is not None else [])
    planes = [op[1] if isinstance(op, tuple) else None for op in ab]
    ab = [op[0] if isinstance(op, tuple) else op for op in ab]
    a_shape, b_shape = ab[0].shape[-2:], ab[1].shape[-2:]
    m = a_shape[1] if ta else a_shape[0]
    k = a_shape[0] if ta else a_shape[1]
    n = b_shape[0] if tb else b_shape[1]
    assert k == (b_shape[1] if tb else b_shape[0]), (name, a_shape, b_shape)
    out_n = n
    if bd and ta:
        assert not tb
        tm, tn, tk = m // bd, n // bd, _pick(k, tk)
        grid, out_n = (bd, 1, k // tk), tn
        a_blk = ((tk, tm), lambda i, j, kk: (kk, i))
        b_blk = ((tk, tn), lambda i, j, kk: (kk, i))
        mn_spec = pl.BlockSpec((tm, tn), lambda i, j, kk: (i, 0))
    elif bd:
        tm, tn, tk = _pick(m, tm), n // bd, k // bd
        grid = (m // tm, bd, 1)
        a_blk = ((tm, tk), lambda i, j, kk: (i, j))
        b_blk = ((tn, tk) if tb else (tk, tn), lambda i, j, kk: (j, j))
        mn_spec = pl.BlockSpec((tm, tn), lambda i, j, kk: (i, j))
    else:
        tm, tn, tk = _pick(m, tm), _pick(n, tn), _pick(k, tk)
        grid = (m // tm, n // tn, k // tk)
        a_blk = ((tk, tm), lambda i, j, kk: (kk, i)) if ta else ((tm, tk), lambda i, j, kk: (i, kk))
        b_blk = ((tn, tk), lambda i, j, kk: (j, kk)) if tb else ((tk, tn), lambda i, j, kk: (kk, j))
        mn_spec = pl.BlockSpec((tm, tn), lambda i, j, kk: (i, j))

    def operand_spec(blk, plane):
        shape, imap = blk
        if plane is None:
            return pl.BlockSpec(shape, imap)
        return pl.BlockSpec((None,) + shape, lambda i, j, kk: (plane,) + imap(i, j, kk))

    ab_specs = [operand_spec(a_blk if q % 2 == 0 else b_blk, p) for q, p in enumerate(planes)]
    nk = grid[2]
    row_spec = pl.BlockSpec((1, tn), lambda i, j, kk: (0, j))
    n_ex, n_out = len(mn) + len(rows), len(outs)
    dims = (((0 if ta else 1,), (1 if tb else 0,)), ((), ()))

    def body(*refs):
        ab_refs, rest = refs[:len(ab)], refs[len(ab):]
        ex, o_refs, acc = rest[:n_ex], rest[n_ex:n_ex + n_out], rest[-1]
        kk = pl.program_id(2)

        @pl.when(kk == 0)
        def _():
            acc[...] = jnp.zeros_like(acc)

        for a_ref, b_ref in zip(ab_refs[0::2], ab_refs[1::2]):
            acc[...] += lax.dot_general(a_ref[...].astype(_MXU), b_ref[...].astype(_MXU), dims,
                                        preferred_element_type=F32)

        @pl.when(kk == nk - 1)
        def _():
            vals = (acc[...],) if epi is None else epi(acc[...], *[r[...] for r in ex])
            for r, v in zip(o_refs, vals):
                r[...] = v.astype(r.dtype)

    res = pl.pallas_call(
        body, grid=grid,
        in_specs=ab_specs + [mn_spec] * len(mn) + [row_spec] * len(rows),
        out_specs=[mn_spec] * n_out,
        out_shape=[jax.ShapeDtypeStruct((m, out_n), dt) for dt in outs],
        scratch_shapes=[pltpu.VMEM((tm, tn), F32)],
        compiler_params=_params(("parallel", "parallel", "arbitrary")), name=name,
    )(*ab, *mn, *rows)
    return res[0] if n_out == 1 else res


def _ew(fn, rows, bcs, out_rows, out_accs, *, name, tm=256):
    r = rows[0].shape[0]
    tm = min(tm, r)
    assert r % tm == 0
    nr, nb, no, na = len(rows), len(bcs), len(out_rows), len(out_accs)

    def body(*refs):
        i = pl.program_id(0)
        r_in, b_in = refs[:nr], refs[nr:nr + nb]
        o_r, o_a = refs[nr + nb:nr + nb + no], refs[nr + nb + no:]
        outs, accs = fn([x[...] for x in r_in], [x[...] for x in b_in])
        for ref, v in zip(o_r, outs):
            ref[...] = v.astype(ref.dtype)
        if na:
            @pl.when(i == 0)
            def _():
                for ref in o_a:
                    ref[...] = jnp.zeros_like(ref)

            for ref, v in zip(o_a, accs):
                ref[...] += v

    res = pl.pallas_call(
        body, grid=(r // tm,),
        in_specs=[pl.BlockSpec((tm, x.shape[1]), lambda i: (i, 0)) for x in rows]
        + [pl.BlockSpec((1, x.shape[1]), lambda i: (0, 0)) for x in bcs],
        out_specs=[pl.BlockSpec((tm, c), lambda i: (i, 0)) for c, _ in out_rows]
        + [pl.BlockSpec((1, c), lambda i: (0, 0)) for c in out_accs],
        out_shape=[jax.ShapeDtypeStruct((r, c), dt) for c, dt in out_rows]
        + [jax.ShapeDtypeStruct((1, c), F32) for c in out_accs],
        compiler_params=_params(("arbitrary",)), name=name,
    )(*rows, *bcs)
    return res


def _colsum(x):
    return jnp.sum(x, axis=0, keepdims=True)


def _sigmoid(x):
    return 1.0 / (1.0 + jnp.exp(-x))


def _rms_fwd(x, g, name):
    def fn(r, b):
        xv = r[0]
        rs = lax.rsqrt(jnp.mean(xv * xv, axis=-1, keepdims=True) + RMS_EPS)
        return [xv * rs * b[0]], []
    return _ew(fn, [x], [g], [(x.shape[1], BF16)], [], name=name)[0]


def _rms_bwd(x, dn, res, g, name):
    def fn(r, b):
        xv, dv = r[0], r[1]
        rs = lax.rsqrt(jnp.mean(xv * xv, axis=-1, keepdims=True) + RMS_EPS)
        gd = dv * b[0]
        dx = rs * gd - xv * (rs * rs * rs) * jnp.mean(gd * xv, axis=-1, keepdims=True)
        if res is not None:
            dx = dx + r[2]
        return [dx], [_colsum(dv * xv * rs)]
    rows = [x, dn] + ([res] if res is not None else [])
    return _ew(fn, rows, [g], [(x.shape[1], F32)], [x.shape[1]], name=name)


def _scan_order(x):
    l, c = x.shape
    return x.reshape(l // (SUBLANES * SCAN_SEG), SUBLANES, SCAN_SEG, c).transpose(0, 2, 1, 3).reshape(l, c)


def _time_order(x):
    l, c = x.shape
    return x.reshape(l // (SUBLANES * SCAN_SEG), SCAN_SEG, SUBLANES, c).transpose(0, 2, 1, 3).reshape(l, c)


def _ssm_scan(b, a_pair, *, reverse, s_fwd=None, name):
    l = b.shape[0]
    seg, w = SCAN_SEG, SCAN_W
    nch = min(SCAN_CHAINS, l // (SUBLANES * seg))
    chain_rows = SUBLANES * seg
    tb = nch * chain_rows
    nt = l // tb
    with_da = s_fwd is not None
    assert reverse or not with_da

    def tt(t):
        return nt - 1 - t if reverse else t

    def body(*refs):
        if with_da:
            br_ref, bi_ref, a_ref, sf_ref, sp_ref, s_ref, da_ref, p_ref, c_ref = refs
        else:
            br_ref, bi_ref, a_ref, s_ref, p_ref, c_ref = refs
        t_blk = pl.program_id(1)
        ar, ai = a_ref[0], a_ref[1]

        @pl.when(t_blk == 0)
        def _():
            def pstep(i, carry):
                pr, pi = carry
                p_ref[0, pl.ds(i, 1), :] = pr
                p_ref[1, pl.ds(i, 1), :] = pi
                return pr * ar - pi * ai, pr * ai + pi * ar

            lax.fori_loop(0, seg, pstep, (ar, ai))
            c_ref[...] = jnp.zeros_like(c_ref)
            if with_da:
                da_ref[...] = jnp.zeros_like(da_ref)

        arb, aib = jnp.broadcast_to(ar, (SUBLANES, w)), jnp.broadcast_to(ai, (SUBLANES, w))
        zero = jnp.zeros((SUBLANES, w), F32)

        def tile(g, step):
            return pl.ds(pl.multiple_of(g * chain_rows + step * SUBLANES, SUBLANES), SUBLANES)

        def rows(g, i):
            return tile(g, seg - 1 - i if reverse else i)

        def local_step(i, carry):
            out = []
            for g in range(nch):
                sr, si = carry[2 * g], carry[2 * g + 1]
                idx = rows(g, i)
                sr, si = arb * sr - aib * si + br_ref[idx, :], arb * si + aib * sr + bi_ref[idx, :]
                s_ref.at[0][idx, :] = sr
                s_ref.at[1][idx, :] = si
                out += [sr, si]
            return tuple(out)

        ends = lax.fori_loop(0, seg, local_step, (zero,) * (2 * nch), unroll=2)

        a_seg_r, a_seg_i = p_ref[0, seg - 1:seg, :], p_ref[1, seg - 1:seg, :]
        cr, ci = c_ref[0], c_ref[1]
        sub = lax.broadcasted_iota(jnp.int32, (SUBLANES, w), 0)
        ins = [[zero, zero] for _ in range(nch)]
        order = [(g, k) for g in range(nch) for k in range(SUBLANES)]
        for g, k in (order[::-1] if reverse else order):
            ins[g] = [jnp.where(sub == k, cr, ins[g][0]), jnp.where(sub == k, ci, ins[g][1])]
            er, ei = ends[2 * g][k:k + 1], ends[2 * g + 1][k:k + 1]
            cr, ci = er + a_seg_r * cr - a_seg_i * ci, ei + a_seg_r * ci + a_seg_i * cr
        c_ref[0] = cr
        c_ref[1] = ci

        def fix(g, i):
            idx = rows(g, i)
            pr, pi = p_ref[0, pl.ds(i, 1), :], p_ref[1, pl.ds(i, 1), :]
            sr = s_ref.at[0][idx, :] + pr * ins[g][0] - pi * ins[g][1]
            si = s_ref.at[1][idx, :] + pr * ins[g][1] + pi * ins[g][0]
            s_ref.at[0][idx, :] = sr
            s_ref.at[1][idx, :] = si
            return sr, si

        if not with_da:
            def fix_step(i, carry):
                for g in range(nch):
                    fix(g, i)
                return carry

            lax.fori_loop(0, seg, fix_step, 0, unroll=2)
        else:
            def adj_step(i, acc):
                acc_r, acc_i = acc
                for g in range(nch):
                    lr, li = fix(g, i)
                    prev = tile(g, seg - 2 - i)
                    fr, fi = sf_ref.at[0][prev, :], sf_ref.at[1][prev, :]
                    acc_r, acc_i = acc_r + lr * fr + li * fi, acc_i + li * fr - lr * fi
                return acc_r, acc_i

            acc_r, acc_i = lax.fori_loop(0, seg - 1, adj_step, (zero, zero), unroll=2)
            first_block = tt(t_blk) == 0
            for g in range(nch):
                lr, li = fix(g, seg - 1)
                seg_ends = tile(g, seg - 1)
                if g == 0:
                    pvr = jnp.where(first_block, 0.0, sp_ref[0, SUBLANES - 1:SUBLANES, :])
                    pvi = jnp.where(first_block, 0.0, sp_ref[1, SUBLANES - 1:SUBLANES, :])
                else:
                    pvr = sf_ref[0, g * chain_rows - 1:g * chain_rows, :]
                    pvi = sf_ref[1, g * chain_rows - 1:g * chain_rows, :]
                fr = jnp.where(sub == 0, pvr, pltpu.roll(sf_ref.at[0][seg_ends, :], 1, 0))
                fi = jnp.where(sub == 0, pvi, pltpu.roll(sf_ref.at[1][seg_ends, :], 1, 0))
                acc_r = acc_r + lr * fr + li * fi
                acc_i = acc_i + li * fr - lr * fi
            da_ref[0] += jnp.sum(acc_r, axis=0, keepdims=True)
            da_ref[1] += jnp.sum(acc_i, axis=0, keepdims=True)

    re_spec = pl.BlockSpec((tb, w), lambda j, t: (tt(t), 2 * j))
    im_spec = pl.BlockSpec((tb, w), lambda j, t: (tt(t), 2 * j + 1))
    a_spec = pl.BlockSpec((2, 1, w), lambda j, t: (0, 0, j))
    s_spec = pl.BlockSpec((2, tb, w), lambda j, t: (0, tt(t), j))
    in_specs, args = [re_spec, im_spec, a_spec], [b, b, a_pair]
    out_specs, out_shape = [s_spec], [jax.ShapeDtypeStruct((2, l, SSM_S), F32)]
    scratch = [pltpu.VMEM((2, seg, w), F32), pltpu.VMEM((2, 1, w), F32)]
    if with_da:
        in_specs += [s_spec, pl.BlockSpec((2, SUBLANES, w),
                                          lambda j, t: (0, jnp.maximum(tt(t) * (tb // SUBLANES) - 1, 0), j))]
        args += [s_fwd, s_fwd]
        out_specs.append(a_spec)
        out_shape.append(jax.ShapeDtypeStruct((2, 1, SSM_S), F32))
    res = pl.pallas_call(
        body, grid=(SSM_S // w, nt), in_specs=in_specs, out_specs=out_specs, out_shape=out_shape,
        scratch_shapes=scratch, compiler_params=_params(("parallel", "arbitrary")), name=name,
    )(*args)
    return res if with_da else res[0]


def _nt_dot(x, y):
    return lax.dot_general(x.astype(_MXU), y.astype(_MXU), (((1,), (1,)), ((), ())), preferred_element_type=F32)


def _tn_dot(x, y):
    return lax.dot_general(x.astype(_MXU), y.astype(_MXU), (((0,), (0,)), ((), ())), preferred_element_type=F32)


def _nn_dot(x, y):
    return jnp.dot(x.astype(_MXU), y.astype(_MXU), preferred_element_type=F32)


def _attn_mask2(gb, nb):
    qi = lax.broadcasted_iota(jnp.int32, (ATT_WIN, 2 * ATT_WIN), 0)
    c = lax.broadcasted_iota(jnp.int32, (ATT_WIN, 2 * ATT_WIN), 1)
    has_prev = (gb % nb) != 0
    prev_ok = jnp.logical_and(jnp.logical_and(c < ATT_WIN, c >= qi), has_prev)
    own_ok = jnp.logical_and(c >= ATT_WIN, c - ATT_WIN <= qi)
    return jnp.logical_or(prev_ok, own_ok)


def _attn_specs():
    cur = pl.BlockSpec((ATT_QB * ATT_WIN, ATT_GW), lambda i: (i, 0))
    prev = pl.BlockSpec((ATT_WIN, ATT_GW), lambda i: (jnp.maximum(ATT_QB * i - 1, 0), 0))
    return cur, prev


def _attn_fwd(q, k, v, nb, name):
    l = q.shape[0]
    scale = ATT_E ** -0.5
    w = ATT_WIN

    def body(q_ref, kc_ref, kp_ref, vc_ref, vp_ref, o_ref, lse_ref):
        i = pl.program_id(0)
        masks = [_attn_mask2(ATT_QB * i + b, nb) for b in range(ATT_QB)]
        for h in range(ATT_HG):
            sl = slice(h * ATT_E, (h + 1) * ATT_E)
            k_ext = jnp.concatenate([kp_ref[:, sl], kc_ref[:, sl]], axis=0)
            v_ext = jnp.concatenate([vp_ref[:, sl], vc_ref[:, sl]], axis=0)
            for b in range(ATT_QB):
                r, kr = slice(b * w, (b + 1) * w), slice(b * w, (b + 2) * w)
                s = jnp.where(masks[b], _nt_dot(q_ref[r, sl], k_ext[kr]) * scale, NEG_INF)
                mx = jnp.max(s, axis=-1, keepdims=True)
                p = jnp.exp(s - mx)
                den = jnp.sum(p, axis=-1, keepdims=True)
                o_ref[r, sl] = _nn_dot(p, v_ext[kr]) / den
                lse_ref[r, sl] = jnp.broadcast_to(mx + jnp.log(den), (w, ATT_E))

    cur, prev = _attn_specs()
    return pl.pallas_call(
        body, grid=(l // (ATT_QB * w),), in_specs=[cur, cur, prev, cur, prev], out_specs=[cur, cur],
        out_shape=[jax.ShapeDtypeStruct((l, ATT_GW), F32)] * 2,
        compiler_params=_params(("parallel",)), name=name,
    )(q, k, k, v, v)


def _attn_bwd(q, k, v, do, lse, dd, nb, name):
    l = q.shape[0]
    scale = ATT_E ** -0.5
    w = ATT_WIN
    nblk = l // w

    def body(q_ref, kc_ref, kp_ref, vc_ref, vp_ref, do_ref, lse_ref, dd_ref, qn_ref, don_ref, lsen_ref, ddn_ref,
             dq_ref, dk_ref, dv_ref, dk_acc, dv_acc):
        i = pl.program_id(0)
        masks = [_attn_mask2(ATT_QB * i + b, nb) for b in range(ATT_QB)]
        nxt = ATT_QB * (i + 1)
        nxt_attends = jnp.logical_and(nxt < nblk, (nxt % nb) != 0)
        qi = lax.broadcasted_iota(jnp.int32, (w, w), 0)
        kj = lax.broadcasted_iota(jnp.int32, (w, w), 1)
        mask_n = jnp.logical_and(kj >= qi, nxt_attends)
        dk_acc[...] = jnp.zeros_like(dk_acc)
        dv_acc[...] = jnp.zeros_like(dv_acc)
        for h in range(ATT_HG):
            sl, col = slice(h * ATT_E, (h + 1) * ATT_E), slice(h * ATT_E, h * ATT_E + 1)
            k_ext = jnp.concatenate([kp_ref[:, sl], kc_ref[:, sl]], axis=0)
            v_ext = jnp.concatenate([vp_ref[:, sl], vc_ref[:, sl]], axis=0)
            for b in range(ATT_QB):
                r, kr = slice(b * w, (b + 1) * w), slice(b * w, (b + 2) * w)
                qh, doh, k2, v2 = q_ref[r, sl], do_ref[r, sl], k_ext[kr], v_ext[kr]
                p = jnp.where(masks[b], jnp.exp(_nt_dot(qh, k2) * scale - lse_ref[r, col]), 0.0)
                ds = p * (_nt_dot(doh, v2) - dd_ref[r, col]) * scale
                dq_ref[r, sl] = _nn_dot(ds, k2).astype(dq_ref.dtype)
                dk2, dv2 = _tn_dot(ds, qh), _tn_dot(p, doh)
                dk_acc[r, sl] += dk2[w:]
                dv_acc[r, sl] += dv2[w:]
                if b > 0:
                    rp = slice((b - 1) * w, b * w)
                    dk_acc[rp, sl] += dk2[:w]
                    dv_acc[rp, sl] += dv2[:w]
            last = slice((ATT_QB - 1) * w, ATT_QB * w)
            kl, vl, qn, don = kc_ref[last, sl], vc_ref[last, sl], qn_ref[:, sl], don_ref[:, sl]
            pn = jnp.where(mask_n, jnp.exp(_nt_dot(qn, kl) * scale - lsen_ref[:, col]), 0.0)
            dsn = pn * (_nt_dot(don, vl) - ddn_ref[:, col]) * scale
            dk_acc[last, sl] += _tn_dot(dsn, qn)
            dv_acc[last, sl] += _tn_dot(pn, don)
        dk_ref[...] = dk_acc[...].astype(dk_ref.dtype)
        dv_ref[...] = dv_acc[...].astype(dv_ref.dtype)

    cur, prev = _attn_specs()
    nxt_spec = pl.BlockSpec((w, ATT_GW), lambda i: (jnp.minimum(ATT_QB * (i + 1), nblk - 1), 0))
    return pl.pallas_call(
        body, grid=(l // (ATT_QB * w),),
        in_specs=[cur, cur, prev, cur, prev, cur, cur, cur, nxt_spec, nxt_spec, nxt_spec, nxt_spec],
        out_specs=[cur] * 3, out_shape=[jax.ShapeDtypeStruct((l, ATT_GW), BF16)] * 3,
        scratch_shapes=[pltpu.VMEM((ATT_QB * w, ATT_GW), F32)] * 2,
        compiler_params=_params(("parallel",)), name=name,
    )(q, k, k, v, v, do, lse, dd, q, do, lse, dd)


def _to_perm(a, d):
    if d == 1:
        return a
    l, c = a.shape
    return a.reshape(l // d, d, c).transpose(1, 0, 2).reshape(l, c)


def _from_perm(a, d):
    if d == 1:
        return a
    l, c = a.shape
    return a.reshape(d, l // d, c).transpose(1, 0, 2).reshape(l, c)


def _mem_probs(qh, kh):
    s = _nt_dot(qh, kh) * (MEM_E ** -0.5)
    e = jnp.exp(s - jnp.max(s, axis=-1, keepdims=True))
    return e / jnp.sum(e, axis=-1, keepdims=True)


def _mem_fwd(mq, kv, name, tm=512):
    l, nm = mq.shape[0], kv.shape[0]

    def body(q_ref, kv_ref, o_ref):
        for h in range(MEM_H):
            sl = slice(h * MEM_E, (h + 1) * MEM_E)
            p = _mem_probs(q_ref[:, sl], kv_ref[:, sl])
            o_ref[:, sl] = _nn_dot(p, kv_ref[:, MEM_W + h * MEM_E:MEM_W + (h + 1) * MEM_E]).astype(o_ref.dtype)

    return pl.pallas_call(
        body, grid=(l // tm,),
        in_specs=[pl.BlockSpec((tm, MEM_W), lambda i: (i, 0)), pl.BlockSpec((nm, 2 * MEM_W), lambda i: (0, 0))],
        out_specs=pl.BlockSpec((tm, MEM_W), lambda i: (i, 0)),
        out_shape=jax.ShapeDtypeStruct((l, MEM_W), BF16),
        compiler_params=_params(("parallel",)), name=name,
    )(mq, kv)


def _mem_bwd(mq, kv, dmo, name, tm=512):
    l, nm = mq.shape[0], kv.shape[0]
    scale = MEM_E ** -0.5

    def body(q_ref, kv_ref, do_ref, dq_ref, dkv_ref):
        @pl.when(pl.program_id(0) == 0)
        def _():
            dkv_ref[...] = jnp.zeros_like(dkv_ref)

        for h in range(MEM_H):
            sl = slice(h * MEM_E, (h + 1) * MEM_E)
            vsl = slice(MEM_W + h * MEM_E, MEM_W + (h + 1) * MEM_E)
            qh, kh, vh, doh = q_ref[:, sl], kv_ref[:, sl], kv_ref[:, vsl], do_ref[:, sl]
            p = _mem_probs(qh, kh)
            dp = _nt_dot(doh, vh)
            ds = p * (dp - jnp.sum(dp * p, axis=-1, keepdims=True)) * scale
            dq_ref[:, sl] = _nn_dot(ds, kh).astype(dq_ref.dtype)
            dkv_ref[:, sl] += _tn_dot(ds, qh)
            dkv_ref[:, vsl] += _tn_dot(p, doh)

    row = pl.BlockSpec((tm, MEM_W), lambda i: (i, 0))
    full = pl.BlockSpec((nm, 2 * MEM_W), lambda i: (0, 0))
    return pl.pallas_call(
        body, grid=(l // tm,), in_specs=[row, full, row], out_specs=[row, full],
        out_shape=[jax.ShapeDtypeStruct((l, MEM_W), BF16), jax.ShapeDtypeStruct((nm, 2 * MEM_W), F32)],
        compiler_params=_params(("arbitrary",)), name=name,
    )(mq, kv, dmo)


def _discretize(lam_re, lam_im, log_dt, b_re, b_im):
    dt = jnp.exp(log_dt)[:, None]
    mag = jnp.exp(lam_re * dt)
    a_re, a_im = mag * jnp.cos(lam_im * dt), mag * jnp.sin(lam_im * dt)
    nr, ni = a_re - 1.0, a_im
    den = lam_re * lam_re + lam_im * lam_im
    coef_re = (nr * lam_re + ni * lam_im) / den
    coef_im = (ni * lam_re - nr * lam_im) / den
    bb_re = coef_re[..., None] * b_re - coef_im[..., None] * b_im
    bb_im = coef_re[..., None] * b_im + coef_im[..., None] * b_re
    return a_re, a_im, bb_re, bb_im


def _tile_cat(re, im):
    lead = re.shape[:-1]
    t = jnp.stack([re.reshape(*lead, SCAN_NT, SCAN_W), im.reshape(*lead, SCAN_NT, SCAN_W)], axis=-2)
    return t.reshape(*lead, 2 * SSM_S)


def _bd_in(bb):
    return jnp.einsum("gph,gk->ghkp", bb, jnp.eye(SSM_G, dtype=bb.dtype)).reshape(SSM_W, SSM_S)


def _bd_diag(x):
    gb = SSM_G // SSM_BD
    t = x.reshape(SSM_BD, gb, SSM_H, gb, SSM_P)
    return jnp.einsum("bghgp->bghp", t).reshape(SSM_G, SSM_H, SSM_P)


_ANY = pl.BlockSpec(memory_space=pl.ANY)
_MESH = pl.DeviceIdType.MESH


def _allgather(x, name):
    def body(x_ref, out_ref, send_sems, recv_sems, local_sem):
        mx, my, mc = lax.axis_index("x"), lax.axis_index("y"), lax.axis_index("c")
        me, sibling = (mx, my, mc), (mx, my, 1 - mc)
        chips = [(1 - mx, my), (mx, 1 - my), (1 - mx, 1 - my)]

        def blk(px, py, pc):
            return out_ref.at[4 * px + 2 * py + pc]

        def copy(k, block, to, src=None):
            return pltpu.make_async_remote_copy(
                src_ref=blk(*block) if src is None else src, dst_ref=blk(*block),
                send_sem=send_sems.at[k], recv_sem=recv_sems.at[k], device_id=to, device_id_type=_MESH)

        mine = pltpu.make_async_copy(x_ref, blk(*me), local_sem)
        mine.start()
        first = [copy(0, me, sibling, src=x_ref)]
        first += [copy(1 + j, me, (*chip, mc), src=x_ref) for j, chip in enumerate(chips)]
        for cp in first:
            cp.start()
        passed = [copy(4 + j, (*chip, mc), sibling) for j, chip in enumerate(chips)]
        for j, chip in enumerate(chips):
            copy(1 + j, (*chip, mc), me).wait_recv()
            passed[j].start()
        copy(0, sibling, me).wait_recv()
        for j, chip in enumerate(chips):
            copy(4 + j, (*chip, 1 - mc), me).wait_recv()
        for cp in first + passed:
            cp.wait_send()
        mine.wait()

    return pl.pallas_call(
        body, out_shape=jax.ShapeDtypeStruct((N_DEV,) + x.shape, x.dtype), in_specs=[_ANY], out_specs=_ANY,
        scratch_shapes=[pltpu.SemaphoreType.DMA((7,)), pltpu.SemaphoreType.DMA((7,)), pltpu.SemaphoreType.DMA],
        name=name,
    )(x)


def _pair_exchange(g, name):
    def body(g_ref, out_ref, send_sems, recv_sems):
        mx, my, mc = lax.axis_index("x"), lax.axis_index("y"), lax.axis_index("c")
        copies = [pltpu.make_async_remote_copy(
            src_ref=g_ref.at[2 * k + (1 - mc)], dst_ref=out_ref.at[k], send_sem=send_sems.at[k],
            recv_sem=recv_sems.at[k], device_id=(mx, my, 1 - mc), device_id_type=_MESH) for k in range(4)]
        for cp in copies:
            cp.start()
        for cp in copies:
            cp.wait()

    return pl.pallas_call(
        body, out_shape=jax.ShapeDtypeStruct((4,) + g.shape[1:], g.dtype), in_specs=[_ANY], out_specs=_ANY,
        scratch_shapes=[pltpu.SemaphoreType.DMA((4,)), pltpu.SemaphoreType.DMA((4,))], name=name,
    )(g)


def _chip_exchange(p, name):
    def body(p_ref, out_ref, send_sems, recv_sems):
        mx, my, mc = lax.axis_index("x"), lax.axis_index("y"), lax.axis_index("c")
        chips = [(1 - mx, my), (mx, 1 - my), (1 - mx, 1 - my)]
        copies = [pltpu.make_async_remote_copy(
            src_ref=p_ref.at[2 * px + py], dst_ref=out_ref.at[j], send_sem=send_sems.at[j],
            recv_sem=recv_sems.at[j], device_id=(px, py, mc), device_id_type=_MESH)
            for j, (px, py) in enumerate(chips)]
        for cp in copies:
            cp.start()
        for cp in copies:
            cp.wait()

    return pl.pallas_call(
        body, out_shape=jax.ShapeDtypeStruct((3,) + p.shape[1:], p.dtype), in_specs=[_ANY], out_specs=_ANY,
        scratch_shapes=[pltpu.SemaphoreType.DMA((3,)), pltpu.SemaphoreType.DMA((3,))], name=name,
    )(p)


def _pair_sum(g, t1, my_c, name, tr):
    _, r, c = g.shape

    def body(c_ref, g_ref, t_ref, o_ref, ob_ref):
        s = g_ref[...] + t_ref[...]
        o_ref[...] = s
        ob_ref[...] = s.astype(BF16)

    blk = pl.BlockSpec((None, tr, c), lambda k, i, cr: (k, i, 0))
    return pl.pallas_call(
        body,
        grid_spec=pltpu.PrefetchScalarGridSpec(
            num_scalar_prefetch=1, grid=(4, r // tr),
            in_specs=[pl.BlockSpec((None, tr, c), lambda k, i, cr: (2 * k + cr[0], i, 0)), blk],
            out_specs=[blk, blk]),
        out_shape=[jax.ShapeDtypeStruct((4, r, c), F32), jax.ShapeDtypeStruct((4, r, c), BF16)],
        compiler_params=_params(("parallel", "parallel")), name=name,
    )(my_c, g, t1)


def _adam_math(g, w, m, v):
    m = ADAM_B1 * m + (1.0 - ADAM_B1) * g
    v = ADAM_B2 * v + (1.0 - ADAM_B2) * (g * g)
    m_hat = m / (1.0 - ADAM_B1 ** ADAM_STEP)
    v_hat = v / (1.0 - ADAM_B2 ** ADAM_STEP)
    delta = -ADAM_LR * (m_hat / (jnp.sqrt(v_hat) + ADAM_EPS) + ADAM_WD * w)
    return delta, m, v


def _adam_big(p, t2, my_chip, w, m, v, name, tr):
    r, c = w.shape

    def body(k_ref, p_ref, t0_ref, t1_ref, t2_ref, w_ref, m_ref, v_ref, g_out, d_out, m_out, v_out):
        g = ((p_ref[...] + t0_ref[...].astype(F32)) + t1_ref[...].astype(F32)) + t2_ref[...].astype(F32)
        d, mn, vn = _adam_math(g, w_ref[...], m_ref[...], v_ref[...])
        g_out[...], d_out[...], m_out[...], v_out[...] = g, d, mn, vn

    flat = pl.BlockSpec((tr, c), lambda i, kr: (i, 0))

    def rel(j):
        return pl.BlockSpec((None, tr, c), lambda i, kr: (j, i, 0))

    return pl.pallas_call(
        body,
        grid_spec=pltpu.PrefetchScalarGridSpec(
            num_scalar_prefetch=1, grid=(r // tr,),
            in_specs=[pl.BlockSpec((None, tr, c), lambda i, kr: (kr[0], i, 0)), rel(0), rel(1), rel(2), flat, flat, flat],
            out_specs=[flat] * 4),
        out_shape=[jax.ShapeDtypeStruct((r, c), F32)] * 4,
        compiler_params=_params(("parallel",)), name=name,
    )(my_chip, p, t2, t2, t2, w, m, v)


def _sum8(g8, name):
    _, r, c = g8.shape

    def body(g_ref, o_ref):
        acc = g_ref[0]
        for j in range(1, N_DEV):
            acc = acc + g_ref[j]
        o_ref[...] = acc

    return pl.pallas_call(
        body, grid=(1,), in_specs=[pl.BlockSpec((N_DEV, r, c), lambda i: (0, 0, 0))],
        out_specs=pl.BlockSpec((r, c), lambda i: (0, 0)), out_shape=jax.ShapeDtypeStruct((r, c), F32),
        compiler_params=_params(("arbitrary",)), name=name,
    )(g8)


def _adam_small(g, w, m, v, name):
    def fn(r, b):
        return list(_adam_math(*r)), []
    c = g.shape[1]
    return _ew(fn, [g, w, m, v], [], [(c, F32)] * 3, [], name=name, tm=g.shape[0])


def _pack(arrs, pad_rows=8):
    flat = jnp.concatenate([a.reshape(-1) for a in arrs])
    n = flat.shape[0]
    q = PACK_C * pad_rows
    tot = -(-n // q) * q
    if tot != n:
        flat = jnp.concatenate([flat, jnp.zeros((tot - n,), flat.dtype)])
    return flat.reshape(tot // PACK_C, PACK_C)


def _unpack(buf, shapes):
    flat = buf.reshape(-1)
    out, off = [], 0
    for s in shapes:
        n = int(np.prod(s))
        out.append(flat[off:off + n].reshape(s))
        off += n
    return out


GROUPS = (("w_in", "w_mem_kv", "w_o", "w_up", "w_down"),
          ("w_glu", "w_ssm_br", "w_mem_br", "w_attn_br"))
GROUP_TR = (208, 384)
ATTN_BR_FOLD = 2


def _stored_shape(name):
    r, c, ax = BIG_SHAPE[name]
    rows, cols = (r // N_DEV, c) if ax == 0 else (c // N_DEV, r)
    return (rows // ATTN_BR_FOLD, cols * ATTN_BR_FOLD) if name == "w_attn_br" else (rows, cols)


def _stored(shard, name):
    a = shard[0].T if BIG_SHAPE[name][2] == 1 else shard[0]
    return a.reshape(_stored_shape(name))


def _unstored(a, name):
    r, c, ax = BIG_SHAPE[name]
    if ax == 0:
        return a.reshape(1, r // N_DEV, c)
    return a.reshape(c // N_DEV, r).T[None]


def _pack_group(d, names):
    return jnp.concatenate([_stored(d[n], n) for n in names], axis=0)


def _split_group(buf, names):
    out, off = {}, 0
    for n in names:
        rows = _stored_shape(n)[0]
        out[n] = buf[..., off:off + rows, :]
        off += rows
    return out


def _full_stored(stacked, name):
    r, c, ax = BIG_SHAPE[name]
    return stacked.reshape((r, c) if ax == 0 else (c, r))


def _stacked_stored(full, name):
    return full.reshape((N_DEV,) + _stored_shape(name))


def _gelu_parts(x):
    c0, c1 = math.sqrt(2.0 / math.pi), 0.044715
    th = jnp.tanh(c0 * (x + c1 * x * x * x))
    return th, c0, c1


def _local_step(x, mem, tgt, wb, sp):
    l = x.shape[0]
    w_a, w_g = wb["w_in"][:ZA_W], wb["w_in"][ZA_W:]

    a_re, a_im, bb_re, bb_im = _discretize(sp["ssm_lambda_re"], sp["ssm_lambda_im"], sp["ssm_log_dt"],
                                           sp["ssm_b_re"], sp["ssm_b_im"])
    a_pair = jnp.stack([a_re.reshape(1, SSM_S), a_im.reshape(1, SSM_S)])
    a_conj = jnp.stack([a_re.reshape(1, SSM_S), -a_im.reshape(1, SSM_S)])
    b_re_t, b_im_t = _bd_in(bb_re).astype(BF16), _bd_in(bb_im).astype(BF16)
    c_re_t = _bd_in(sp["ssm_c_re"].transpose(0, 2, 1)).astype(BF16)
    c_im_t = (-_bd_in(sp["ssm_c_im"].transpose(0, 2, 1))).astype(BF16)
    bbd = _tile_cat(b_re_t, b_im_t)
    cbd_t = _tile_cat(c_re_t, c_im_t)
    d_row = sp["ssm_d"].reshape(1, SSM_W)

    n1 = _rms_fwd(x, sp["norm1_g"], "rms1")
    za = _mm(n1, w_a, [BF16], tb=True, name="in_proj_a", tn=1664)
    zg = _mm(n1, w_g, [BF16], tb=True, name="in_proj_g")
    u = za[:, :SSM_W]
    mq = za[:, ZA_W - MEM_W:]

    u_s = _scan_order(u)
    bu = _mm(u_s, bbd, [F32], bd=SSM_BD, name="ssm_bu")
    s_all = _ssm_scan(bu, a_pair, reverse=False, name="ssm_scan_fwd")
    ys = _time_order(_mm((s_all, 0), c_re_t, [F32], tb=True, pair2=((s_all, 1), c_im_t), bd=SSM_BD, name="ssm_cs"))

    def gelu_fn(r, b):
        y0 = r[0] + b[0] * r[1].astype(F32)
        th, _, _ = _gelu_parts(y0)
        return [y0, 0.5 * y0 * (1.0 + th)], []
    y0, y1 = _ew(gelu_fn, [ys, u], [d_row], [(SSM_W, F32), (SSM_W, BF16)], [], name="ssm_gelu", tm=512)

    def glu_epi(acc, y1t, bg):
        t = acc + bg
        return t, y1t.astype(F32) * _sigmoid(t)
    t_glu, y2 = _mm(y1, wb["w_glu"], [F32, BF16], epi=glu_epi, mn=[y1], rows=[sp["b_glu"]], name="ssm_glu")
    br_ssm = _mm(y2, wb["w_ssm_br"], [BF16], tb=True, name="ssm_br")

    qkv_p, o_g, lse_g = [], [], []
    for g, d in enumerate(DILATIONS):
        nb = l // d // ATT_WIN
        cols = [za[:, SSM_W + (3 * j + g) * ATT_GW: SSM_W + (3 * j + g + 1) * ATT_GW] for j in range(3)]
        qp, kp, vp = [_to_perm(cc, d) for cc in cols]
        qkv_p.append((qp, kp, vp))
        og, lg = _attn_fwd(qp, kp, vp, nb, "attn_fwd%d" % g)
        o_g.append(_from_perm(og, d))
        lse_g.append(_from_perm(lg, d))

    def merge_fn(r, b):
        o0, o1, o2, l0, l1, l2 = r
        mx = jnp.maximum(jnp.maximum(l0, l1), l2)
        e0, e1, e2 = jnp.exp(l0 - mx), jnp.exp(l1 - mx), jnp.exp(l2 - mx)
        tot = e0 + e1 + e2
        return [(e0 * o0 + e1 * o1 + e2 * o2) / tot, mx + jnp.log(tot)], []
    o_att, lse_tot = _ew(merge_fn, o_g + lse_g, [], [(ATT_GW, F32), (ATT_GW, F32)], [], name="attn_merge", tm=512)
    br_attn = _mm(o_att, wb["w_attn_br"], [BF16], tb=True, name="attn_br")

    mn = _rms_fwd(mem, sp["mem_norm_g"], "rms_mem")
    kv = _mm(mn, wb["w_mem_kv"], [BF16], name="mem_kv")
    mo = _mem_fwd(mq, kv, "mem_attn_fwd")
    br_mem = _mm(mo, wb["w_mem_br"], [BF16], tb=True, name="mem_br")

    def gate_fn(r, b):
        zgt, b0, b1, b2 = [t.astype(F32) for t in r]
        gt = _sigmoid(zgt + b[0])
        return [gt[:, :D_MODEL] * b0 + gt[:, D_MODEL:2 * D_MODEL] * b1 + gt[:, 2 * D_MODEL:] * b2], []
    merged = _ew(gate_fn, [zg, br_ssm, br_attn, br_mem], [sp["b_gate"]], [(D_MODEL, BF16)], [], name="gate_merge")[0]
    h1 = _mm(merged, wb["w_o"], [F32], epi=lambda acc, xt: (acc + xt,), mn=[x], name="o_proj")
    n2 = _rms_fwd(h1, sp["norm2_g"], "rms2")

    def up_epi(acc):
        ra = jnp.maximum(acc, 0.0)
        return ra * ra, ra
    f_act, r_act = _mm(n2, wb["w_up"], [BF16, BF16], tb=True, epi=up_epi, name="mlp_up")
    h2 = _mm(f_act, wb["w_down"], [F32], epi=lambda acc, ht: (acc + ht,), mn=[h1], name="mlp_down")

    def final_fn(r, b):
        hv, tv = r
        gf = b[0]
        rs = lax.rsqrt(jnp.mean(hv * hv, axis=-1, keepdims=True) + RMS_EPS)
        err = hv * rs * gf - tv
        dy = err * (1.0 / D_MODEL)
        gd = dy * gf
        dh = rs * gd - hv * (rs * rs * rs) * jnp.mean(gd * hv, axis=-1, keepdims=True)
        loss = _colsum(jnp.sum(err * err, axis=-1, keepdims=True)) * (0.5 / D_MODEL)
        return [dh], [_colsum(dy * hv * rs), loss]
    dh2, d_final_g, loss = _ew(final_fn, [h2, tgt], [sp["final_g"]], [(D_MODEL, F32)], [D_MODEL, 1], name="final_loss")

    gw, gs = {}, {"final_g": d_final_g}
    d_act = _mm(dh2, wb["w_down"], [BF16], tb=True, epi=lambda acc, ra: (acc * 2.0 * ra.astype(F32),), mn=[r_act],
                name="mlp_down_dx")
    gw["w_down"] = _mm(f_act, dh2, [F32], ta=True, name="mlp_down_dw")
    dn2 = _mm(d_act, wb["w_up"], [F32], name="mlp_up_dx")
    gw["w_up"] = _mm(d_act, n2, [F32], ta=True, name="mlp_up_dw")
    dh1, gs["norm2_g"] = _rms_bwd(h1, dn2, dh2, sp["norm2_g"], "rms2_bwd")
    dmerged = _mm(dh1, wb["w_o"], [F32], tb=True, name="o_proj_dx")
    gw["w_o"] = _mm(merged, dh1, [F32], ta=True, name="o_proj_dw")

    def gate_bwd_fn(r, b):
        dm, zgt, b0, b1, b2 = [t.astype(F32) for t in r]
        gt = _sigmoid(zgt + b[0])
        g0, g1, g2 = gt[:, :D_MODEL], gt[:, D_MODEL:2 * D_MODEL], gt[:, 2 * D_MODEL:]
        dzg = jnp.concatenate([dm * b0 * g0 * (1.0 - g0), dm * b1 * g1 * (1.0 - g1), dm * b2 * g2 * (1.0 - g2)], axis=1)
        return [dm * g0, dm * g1, dm * g2, dzg], [_colsum(dzg)]
    dbr_ssm, dbr_attn, dbr_mem, dzg, gs["b_gate"] = _ew(
        gate_bwd_fn, [dmerged, zg, br_ssm, br_attn, br_mem], [sp["b_gate"]],
        [(D_MODEL, BF16)] * 3 + [(ZG_W, BF16)], [ZG_W], name="gate_bwd")

    gw["w_ssm_br"] = _mm(dbr_ssm, y2, [F32], ta=True, name="ssm_br_dw")
    dy2 = _mm(dbr_ssm, wb["w_ssm_br"], [F32], name="ssm_br_dx")

    def glu_bwd_fn(r, b):
        dy, y1t, tt = r
        sg = _sigmoid(tt)
        dt = dy * y1t.astype(F32) * sg * (1.0 - sg)
        return [dt, dy * sg], [_colsum(dt)]
    dt_glu, dy1a, gs["b_glu"] = _ew(glu_bwd_fn, [dy2, y1, t_glu], [], [(SSM_W, BF16), (SSM_W, F32)], [SSM_W],
                                    name="ssm_glu_bwd", tm=512)
    gw["w_glu"] = _mm(y1, dt_glu, [F32], ta=True, name="ssm_glu_dw")

    def gelu_bwd_epi(acc, dy1t, y0t):
        th, c0, c1 = _gelu_parts(y0t)
        dg = 0.5 * (1.0 + th) + 0.5 * y0t * (1.0 - th * th) * c0 * (1.0 + 3.0 * c1 * y0t * y0t)
        return ((acc + dy1t) * dg,)
    dy0 = _mm(dt_glu, wb["w_glu"], [F32], tb=True, epi=gelu_bwd_epi, mn=[dy1a, y0], name="ssm_glu_dx")
    gs["ssm_d"] = _ew(lambda r, b: ([], [_colsum(r[0] * r[1].astype(F32))]), [dy0, u], [], [], [SSM_W],
                      name="ssm_dd", tm=512)[0]
    dy0_s = _scan_order(dy0)
    g_adj = _mm(dy0_s, cbd_t, [F32], bd=SSM_BD, name="ssm_cs_dx")
    dcr = _mm(dy0_s, (s_all, 0), [F32], ta=True, bd=SSM_BD, name="ssm_cs_dw_re")
    dci = _mm(dy0_s, (s_all, 1), [F32], ta=True, bd=SSM_BD, name="ssm_cs_dw_im")
    lam, da = _ssm_scan(g_adj, a_conj, reverse=True, s_fwd=s_all, name="ssm_scan_bwd")
    du = _time_order(_mm((lam, 0), b_re_t, [BF16], tb=True, pair2=((lam, 1), b_im_t),
                         epi=lambda acc, dyt, dr: (acc + dyt * dr,), mn=[dy0_s], rows=[d_row], bd=SSM_BD, name="ssm_bu_dx"))
    dbr = _mm(u_s, (lam, 0), [F32], ta=True, bd=SSM_BD, name="ssm_bu_dw_re")
    dbi = _mm(u_s, (lam, 1), [F32], ta=True, bd=SSM_BD, name="ssm_bu_dw_im")
    gs["a_re"], gs["a_im"] = da[0], da[1]
    gs["bb_re"], gs["bb_im"] = _bd_diag(dbr).transpose(0, 2, 1), _bd_diag(dbi).transpose(0, 2, 1)
    gs["ssm_c_re"], gs["ssm_c_im"] = _bd_diag(dcr), -_bd_diag(dci)

    gw["w_attn_br"] = _mm(dbr_attn, o_att, [F32], ta=True, name="attn_br_dw")

    def do_epi(acc, ot):
        prod = acc * ot
        head = lax.broadcasted_iota(jnp.int32, prod.shape, 1) // ATT_E
        dd = jnp.zeros_like(prod)
        for h in range(ATT_HG):
            dd = jnp.where(head == h, jnp.sum(jnp.where(head == h, prod, 0.0), axis=1, keepdims=True), dd)
        return acc, dd
    do_att, dd_att = _mm(dbr_attn, wb["w_attn_br"], [BF16, F32], epi=do_epi, mn=[o_att], name="attn_br_dx")
    dq_l, dk_l, dv_l = [], [], []
    for g, d in enumerate(DILATIONS):
        nb = l // d // ATT_WIN
        qp, kp, vp = qkv_p[g]
        dq, dk, dv = _attn_bwd(qp, kp, vp, _to_perm(do_att, d), _to_perm(lse_tot, d), _to_perm(dd_att, d),
                               nb, "attn_bwd%d" % g)
        dq_l.append(_from_perm(dq, d))
        dk_l.append(_from_perm(dk, d))
        dv_l.append(_from_perm(dv, d))

    gw["w_mem_br"] = _mm(dbr_mem, mo, [F32], ta=True, name="mem_br_dw")
    dmo = _mm(dbr_mem, wb["w_mem_br"], [BF16], name="mem_br_dx")
    dmq, dkv = _mem_bwd(mq, kv, dmo, "mem_attn_bwd")
    gw["w_mem_kv"] = _mm(mn, dkv, [F32], ta=True, name="mem_kv_dw")
    dmn = _mm(dkv, wb["w_mem_kv"], [F32], tb=True, name="mem_kv_dx")
    gs["mem_norm_g"] = _rms_bwd(mem, dmn, None, sp["mem_norm_g"], "rms_mem_bwd")[1]

    dza = jnp.concatenate([du] + dq_l + dk_l + dv_l + [dmq], axis=1)
    dw_a = _mm(dza, n1, [F32], ta=True, name="in_proj_a_dw", tm=1664)
    dw_g = _mm(dzg, n1, [F32], ta=True, name="in_proj_g_dw")
    gw["w_in"] = jnp.concatenate([dw_a, dw_g], axis=0)
    dn_a = _mm(dza, w_a, [F32], name="in_proj_a_dx", tk=1664)
    dn1 = _mm(dzg, w_g, [F32], epi=lambda acc, pt: (acc + pt,), mn=[dn_a], name="in_proj_g_dx")
    grad_x, gs["norm1_g"] = _rms_bwd(x, dn1, dh1, sp["norm1_g"], "rms1_bwd")
    return loss, grad_x, gw, gs


_SMALL_GRAD_ORDER = ("norm1_g", "mem_norm_g", "b_gate", "a_re", "a_im", "bb_re", "bb_im", "ssm_c_re", "ssm_c_im",
                     "ssm_d", "b_glu", "norm2_g", "final_g")


def kernel(x, mem, norm1_g, mem_norm_g, w_in, b_gate, ssm_lambda_re, ssm_lambda_im, ssm_log_dt, ssm_b_re, ssm_b_im, ssm_c_re, ssm_c_im, ssm_d, w_glu, b_glu, w_ssm_br, w_attn_br, w_mem_kv, w_mem_br, w_o, norm2_g, w_up, w_down, final_g, loss_target, m_norm1_g, m_mem_norm_g, m_w_in, m_b_gate, m_ssm_lambda_re, m_ssm_lambda_im, m_ssm_log_dt, m_ssm_b_re, m_ssm_b_im, m_ssm_c_re, m_ssm_c_im, m_ssm_d, m_w_glu, m_b_glu, m_w_ssm_br, m_w_attn_br, m_w_mem_kv, m_w_mem_br, m_w_o, m_norm2_g, m_w_up, m_w_down, m_final_g, v_norm1_g, v_mem_norm_g, v_w_in, v_b_gate, v_ssm_lambda_re, v_ssm_lambda_im, v_ssm_log_dt, v_ssm_b_re, v_ssm_b_im, v_ssm_c_re, v_ssm_c_im, v_ssm_d, v_w_glu, v_b_glu, v_w_ssm_br, v_w_attn_br, v_w_mem_kv, v_w_mem_br, v_w_o, v_norm2_g, v_w_up, v_w_down, v_final_g):
    args = dict(locals())
    w = {n: args[n] for n in ALL_W}
    m = {n: args["m_" + n] for n in ALL_W}
    v = {n: args["v_" + n] for n in ALL_W}
    my_c = lax.axis_index("c").astype(jnp.int32).reshape(1)
    my_chip = (2 * lax.axis_index("x") + lax.axis_index("y")).astype(jnp.int32).reshape(1)

    w_pack = [_pack_group(w, names) for names in GROUPS]
    wb = {}
    for gi, names in enumerate(GROUPS):
        w_all = _allgather(w_pack[gi].astype(BF16), "allgather_weights%d" % gi)
        for n, part in _split_group(w_all, names).items():
            wb[n] = _full_stored(part, n)

    sp = {
        "norm1_g": norm1_g, "mem_norm_g": mem_norm_g, "b_gate": b_gate, "b_glu": b_glu, "norm2_g": norm2_g,
        "final_g": final_g.reshape(1, D_MODEL),
        "ssm_lambda_re": ssm_lambda_re[0], "ssm_lambda_im": ssm_lambda_im[0], "ssm_log_dt": ssm_log_dt[0],
        "ssm_b_re": ssm_b_re[0], "ssm_b_im": ssm_b_im[0], "ssm_c_re": ssm_c_re[0], "ssm_c_im": ssm_c_im[0],
        "ssm_d": ssm_d[0],
    }
    loss, grad_x, gw, gs = _local_step(x[0], mem[0], loss_target[0], wb, sp)
    loss = lax.psum(loss[0, 0], ("x", "y", "c"))

    big = [{}, {}, {}, {}]
    for gi, names in enumerate(GROUPS):
        g_pack = jnp.concatenate([_stacked_stored(gw[n], n) for n in names], axis=1)
        t1 = _pair_exchange(g_pack, "grad_pair_exchange%d" % gi)
        p_sum, p_bf = _pair_sum(g_pack, t1, my_c, "grad_pair_sum%d" % gi, GROUP_TR[gi])
        t2 = _chip_exchange(p_bf, "grad_chip_exchange%d" % gi)
        outs = _adam_big(p_sum, t2, my_chip, w_pack[gi], _pack_group(m, names), _pack_group(v, names),
                         "adam_big%d" % gi, GROUP_TR[gi])
        for kind, buf in enumerate(outs):
            for n, part in _split_group(buf, names).items():
                big[kind][n] = _unstored(part, n)

    sg_shapes = [gs[n].shape for n in _SMALL_GRAD_ORDER]
    sg_all = _allgather(_pack([gs[n] for n in _SMALL_GRAD_ORDER]), "allgather_small_grads")
    sg = dict(zip(_SMALL_GRAD_ORDER, _unpack(_sum8(sg_all, "sum_small_grads"), sg_shapes)))
    _, disc_vjp = jax.vjp(_discretize, sp["ssm_lambda_re"], sp["ssm_lambda_im"], sp["ssm_log_dt"],
                          sp["ssm_b_re"], sp["ssm_b_im"])
    d_lre, d_lim, d_ldt, d_bre, d_bim = disc_vjp((sg["a_re"].reshape(SSM_G, SSM_P), sg["a_im"].reshape(SSM_G, SSM_P),
                                                  sg["bb_re"], sg["bb_im"]))
    small_grad = {
        "norm1_g": sg["norm1_g"], "mem_norm_g": sg["mem_norm_g"], "b_gate": sg["b_gate"],
        "ssm_lambda_re": d_lre, "ssm_lambda_im": d_lim, "ssm_log_dt": d_ldt, "ssm_b_re": d_bre, "ssm_b_im": d_bim,
        "ssm_c_re": sg["ssm_c_re"], "ssm_c_im": sg["ssm_c_im"], "ssm_d": sg["ssm_d"], "b_glu": sg["b_glu"],
        "norm2_g": sg["norm2_g"], "final_g": sg["final_g"],
    }
    small_grad = {n: small_grad[n].reshape(w[n].shape) for n in SMALL}
    s_shapes = [w[n].shape for n in SMALL]
    small_out = _adam_small(_pack([small_grad[n] for n in SMALL]), _pack([w[n] for n in SMALL]),
                            _pack([m[n] for n in SMALL]), _pack([v[n] for n in SMALL]), "adam_small")
    small = [small_grad] + [dict(zip(SMALL, _unpack(b, s_shapes))) for b in small_out]

    outs = [loss, grad_x[None]]
    for kind in range(4):
        for n in ALL_W:
            outs.append(big[kind][n] if n in BIG else small[kind][n])
    return tuple(outs)
```

```python
import math

import numpy as np
import jax
import jax.numpy as jnp
from jax import lax
from jax.experimental import pallas as pl
from jax.experimental.pallas import tpu as pltpu

F32 = jnp.float32
BF16 = jnp.bfloat16
_MXU = jnp.bfloat16

D_MODEL = 1024
SSM_G, SSM_H, SSM_P = 32, 16, 64
SSM_W = SSM_G * SSM_H
SSM_S = SSM_G * SSM_P
SSM_BD = 4
ATT_E = 64
ATT_HG = 4
ATT_GW = ATT_HG * ATT_E
ATT_WIN = 128
ATT_QB = 4
DILATIONS = (1, 4, 16)
MEM_H, MEM_E = 4, 128
MEM_W = MEM_H * MEM_E
ZA_W = SSM_W + 9 * ATT_GW + MEM_W
ZG_W = 3 * D_MODEL
IN_W = ZA_W + ZG_W
RMS_EPS = 1e-6
NEG_INF = -1e30

ADAM_LR, ADAM_B1, ADAM_B2, ADAM_EPS, ADAM_WD, ADAM_STEP = 0.001, 0.9, 0.999, 1e-08, 0.01, 10

N_DEV = 8
PACK_C = 512
_VMEM_LIMIT = 56 * 1024 * 1024
SUBLANES = 16
SCAN_SEG = 128
SCAN_CHAINS = 2
SCAN_W = 128
SCAN_NT = SSM_S // SCAN_W

BIG = ("w_in", "w_glu", "w_ssm_br", "w_attn_br", "w_mem_kv", "w_mem_br", "w_o", "w_up", "w_down")
BIG_SHAPE = {
    "w_in": (D_MODEL, IN_W, 1), "w_glu": (SSM_W, SSM_W, 0), "w_ssm_br": (SSM_W, D_MODEL, 1),
    "w_attn_br": (ATT_GW, D_MODEL, 1), "w_mem_kv": (D_MODEL, 2 * MEM_W, 0), "w_mem_br": (MEM_W, D_MODEL, 1),
    "w_o": (D_MODEL, D_MODEL, 0), "w_up": (D_MODEL, 4 * D_MODEL, 1), "w_down": (4 * D_MODEL, D_MODEL, 0),
}
SMALL = ("norm1_g", "mem_norm_g", "b_gate", "ssm_lambda_re", "ssm_lambda_im", "ssm_log_dt", "ssm_b_re",
         "ssm_b_im", "ssm_c_re", "ssm_c_im", "ssm_d", "b_glu", "norm2_g", "final_g")
ALL_W = ("norm1_g", "mem_norm_g", "w_in", "b_gate", "ssm_lambda_re", "ssm_lambda_im", "ssm_log_dt", "ssm_b_re",
         "ssm_b_im", "ssm_c_re", "ssm_c_im", "ssm_d", "w_glu", "b_glu", "w_ssm_br", "w_attn_br", "w_mem_kv",
         "w_mem_br", "w_o", "norm2_g", "w_up", "w_down", "final_g")


def _params(sem):
    return pltpu.CompilerParams(dimension_semantics=sem, vmem_limit_bytes=_VMEM_LIMIT)


def _pick(n, cap):
    if n <= cap:
        return n
    t = (cap // 128) * 128
    while n % t:
        t -= 128
    return t


def _mm(a, b, outs, *, name, ta=False, tb=False, epi=None, mn=(), rows=(), pair2=None, bd=0,
        tm=1024, tn=1024, tk=2048):
    ab = [a, b] + (list(pair2) if pair2 is not None else [])
    planes = [op[1] if isinstance(op, tuple) else None for op in ab]
    ab = [op[0] if isinstance(op, tuple) else op for op in ab]
    a_shape, b_shape = ab[0].shape[-2:], ab[1].shape[-2:]
    m = a_shape[1] if ta else a_shape[0]
    k = a_shape[0] if ta else a_shape[1]
    n = b_shape[0] if tb else b_shape[1]
    assert k == (b_shape[1] if tb else b_shape[0]), (name, a_shape, b_shape)
    out_n = n
    if bd and ta:
        assert not tb
        tm, tn, tk = m // bd, n // bd, _pick(k, tk)
        grid, out_n = (bd, 1, k // tk), tn
        a_blk = ((tk, tm), lambda i, j, kk: (kk, i))
        b_blk = ((tk, tn), lambda i, j, kk: (kk, i))
        mn_spec = pl.BlockSpec((tm, tn), lambda i, j, kk: (i, 0))
    elif bd:
        tm, tn, tk = _pick(m, tm), n // bd, k // bd
        grid = (m // tm, bd, 1)
        a_blk = ((tm, tk), lambda i, j, kk: (i, j))
        b_blk = ((tn, tk) if tb else (tk, tn), lambda i, j, kk: (j, j))
        mn_spec = pl.BlockSpec((tm, tn), lambda i, j, kk: (i, j))
    else:
        tm, tn, tk = _pick(m, tm), _pick(n, tn), _pick(k, tk)
        grid = (m // tm, n // tn, k // tk)
        a_blk = ((tk, tm), lambda i, j, kk: (kk, i)) if ta else ((tm, tk), lambda i, j, kk: (i, kk))
        b_blk = ((tn, tk), lambda i, j, kk: (j, kk)) if tb else ((tk, tn), lambda i, j, kk: (kk, j))
        mn_spec = pl.BlockSpec((tm, tn), lambda i, j, kk: (i, j))

    def operand_spec(blk, plane):
        shape, imap = blk
        if plane is None:
            return pl.BlockSpec(shape, imap)
        return pl.BlockSpec((None,) + shape, lambda i, j, kk: (plane,) + imap(i, j, kk))

    ab_specs = [operand_spec(a_blk if q % 2 == 0 else b_blk, p) for q, p in enumerate(planes)]
    nk = grid[2]
    row_spec = pl.BlockSpec((1, tn), lambda i, j, kk: (0, j))
    n_ex, n_out = len(mn) + len(rows), len(outs)
    dims = (((0 if ta else 1,), (1 if tb else 0,)), ((), ()))

    def body(*refs):
        ab_refs, rest = refs[:len(ab)], refs[len(ab):]
        ex, o_refs, acc = rest[:n_ex], rest[n_ex:n_ex + n_out], rest[-1]
        kk = pl.program_id(2)

        @pl.when(kk == 0)
        def _():
            acc[...] = jnp.zeros_like(acc)

        for a_ref, b_ref in zip(ab_refs[0::2], ab_refs[1::2]):
            acc[...] += lax.dot_general(a_ref[...].astype(_MXU), b_ref[...].astype(_MXU), dims,
                                        preferred_element_type=F32)

        @pl.when(kk == nk - 1)
        def _():
            vals = (acc[...],) if epi is None else epi(acc[...], *[r[...] for r in ex])
            for r, v in zip(o_refs, vals):
                r[...] = v.astype(r.dtype)

    res = pl.pallas_call(
        body, grid=grid,
        in_specs=ab_specs + [mn_spec] * len(mn) + [row_spec] * len(rows),
        out_specs=[mn_spec] * n_out,
        out_shape=[jax.ShapeDtypeStruct((m, out_n), dt) for dt in outs],
        scratch_shapes=[pltpu.VMEM((tm, tn), F32)],
        compiler_params=_params(("parallel", "parallel", "arbitrary")), name=name,
    )(*ab, *mn, *rows)
    return res[0] if n_out == 1 else res


def _ew(fn, rows, bcs, out_rows, out_accs, *, name, tm=256):
    r = rows[0].shape[0]
    tm = min(tm, r)
    assert r % tm == 0
    nr, nb, no, na = len(rows), len(bcs), len(out_rows), len(out_accs)

    def body(*refs):
        i = pl.program_id(0)
        r_in, b_in = refs[:nr], refs[nr:nr + nb]
        o_r, o_a = refs[nr + nb:nr + nb + no], refs[nr + nb + no:]
        outs, accs = fn([x[...] for x in r_in], [x[...] for x in b_in])
        for ref, v in zip(o_r, outs):
            ref[...] = v.astype(ref.dtype)
        if na:
            @pl.when(i == 0)
            def _():
                for ref in o_a:
                    ref[...] = jnp.zeros_like(ref)

            for ref, v in zip(o_a, accs):
                ref[...] += v

    res = pl.pallas_call(
        body, grid=(r // tm,),
        in_specs=[pl.BlockSpec((tm, x.shape[1]), lambda i: (i, 0)) for x in rows]
        + [pl.BlockSpec((1, x.shape[1]), lambda i: (0, 0)) for x in bcs],
        out_specs=[pl.BlockSpec((tm, c), lambda i: (i, 0)) for c, _ in out_rows]
        + [pl.BlockSpec((1, c), lambda i: (0, 0)) for c in out_accs],
        out_shape=[jax.ShapeDtypeStruct((r, c), dt) for c, dt in out_rows]
        + [jax.ShapeDtypeStruct((1, c), F32) for c in out_accs],
        compiler_params=_params(("arbitrary",)), name=name,
    )(*rows, *bcs)
    return res


def _colsum(x):
    return jnp.sum(x, axis=0, keepdims=True)


def _sigmoid(x):
    return 1.0 / (1.0 + jnp.exp(-x))


def _rms_fwd(x, g, name):
    def fn(r, b):
        xv = r[0]
        rs = lax.rsqrt(jnp.mean(xv * xv, axis=-1, keepdims=True) + RMS_EPS)
        return [xv * rs * b[0]], []
    return _ew(fn, [x], [g], [(x.shape[1], BF16)], [], name=name)[0]


def _rms_bwd(x, dn, res, g, name):
    def fn(r, b):
        xv, dv = r[0], r[1]
        rs = lax.rsqrt(jnp.mean(xv * xv, axis=-1, keepdims=True) + RMS_EPS)
        gd = dv * b[0]
        dx = rs * gd - xv * (rs * rs * rs) * jnp.mean(gd * xv, axis=-1, keepdims=True)
        if res is not None:
            dx = dx + r[2]
        return [dx], [_colsum(dv * xv * rs)]
    rows = [x, dn] + ([res] if res is not None else [])
    return _ew(fn, rows, [g], [(x.shape[1], F32)], [x.shape[1]], name=name)


def _scan_order(x):
    l, c = x.shape
    return x.reshape(l // (SUBLANES * SCAN_SEG), SUBLANES, SCAN_SEG, c).transpose(0, 2, 1, 3).reshape(l, c)


def _time_order(x):
    l, c = x.shape
    return x.reshape(l // (SUBLANES * SCAN_SEG), SCAN_SEG, SUBLANES, c).transpose(0, 2, 1, 3).reshape(l, c)


def _ssm_scan(x, wmat, a_pair, *, reverse, s_fwd=None, name):
    l = x.shape[0]
    seg, w = SCAN_SEG, SCAN_W
    bd_w = SSM_W // SSM_BD
    tiles_per_bd = SSM_S // SSM_BD // w
    nch = min(SCAN_CHAINS, l // (SUBLANES * seg))
    chain_rows = SUBLANES * seg
    tb = nch * chain_rows
    nt = l // tb
    with_da = s_fwd is not None
    assert reverse or not with_da

    def tt(t):
        return nt - 1 - t if reverse else t

    def body(*refs):
        if with_da:
            x_ref, w_ref, a_ref, sf_ref, sp_ref, s_ref, da_ref, p_ref, c_ref, b_scr = refs
        else:
            x_ref, w_ref, a_ref, s_ref, p_ref, c_ref, b_scr = refs
        t_blk = pl.program_id(1)
        ar, ai = a_ref[0], a_ref[1]

        @pl.when(t_blk == 0)
        def _():
            def pstep(i, carry):
                pr, pi = carry
                p_ref[0, pl.ds(i, 1), :] = pr
                p_ref[1, pl.ds(i, 1), :] = pi
                return pr * ar - pi * ai, pr * ai + pi * ar

            lax.fori_loop(0, seg, pstep, (ar, ai))
            c_ref[...] = jnp.zeros_like(c_ref)
            if with_da:
                da_ref[...] = jnp.zeros_like(da_ref)

        b_scr[...] = jnp.dot(x_ref[...].astype(_MXU), w_ref[...], preferred_element_type=F32)
        arb, aib = jnp.broadcast_to(ar, (SUBLANES, w)), jnp.broadcast_to(ai, (SUBLANES, w))
        zero = jnp.zeros((SUBLANES, w), F32)

        def tile(g, step):
            return pl.ds(pl.multiple_of(g * chain_rows + step * SUBLANES, SUBLANES), SUBLANES)

        def rows(g, i):
            return tile(g, seg - 1 - i if reverse else i)

        def local_step(i, carry):
            out = []
            for g in range(nch):
                sr, si = carry[2 * g], carry[2 * g + 1]
                idx = rows(g, i)
                sr, si = arb * sr - aib * si + b_scr[idx, :w], arb * si + aib * sr + b_scr[idx, w:]
                b_scr[idx, :w] = sr
                b_scr[idx, w:] = si
                out += [sr, si]
            return tuple(out)

        ends = lax.fori_loop(0, seg, local_step, (zero,) * (2 * nch), unroll=2)

        a_seg_r, a_seg_i = p_ref[0, seg - 1:seg, :], p_ref[1, seg - 1:seg, :]
        cr, ci = c_ref[0], c_ref[1]
        sub = lax.broadcasted_iota(jnp.int32, (SUBLANES, w), 0)
        ins = [[zero, zero] for _ in range(nch)]
        order = [(g, k) for g in range(nch) for k in range(SUBLANES)]
        for g, k in (order[::-1] if reverse else order):
            ins[g] = [jnp.where(sub == k, cr, ins[g][0]), jnp.where(sub == k, ci, ins[g][1])]
            er, ei = ends[2 * g][k:k + 1], ends[2 * g + 1][k:k + 1]
            cr, ci = er + a_seg_r * cr - a_seg_i * ci, ei + a_seg_r * ci + a_seg_i * cr
        c_ref[0] = cr
        c_ref[1] = ci

        def fix(g, i):
            idx = rows(g, i)
            pr, pi = p_ref[0, pl.ds(i, 1), :], p_ref[1, pl.ds(i, 1), :]
            sr = b_scr[idx, :w] + pr * ins[g][0] - pi * ins[g][1]
            si = b_scr[idx, w:] + pr * ins[g][1] + pi * ins[g][0]
            s_ref.at[0][idx, :] = sr.astype(s_ref.dtype)
            s_ref.at[1][idx, :] = si.astype(s_ref.dtype)
            return sr, si

        if not with_da:
            def fix_step(i, carry):
                for g in range(nch):
                    fix(g, i)
                return carry

            lax.fori_loop(0, seg, fix_step, 0, unroll=2)
        else:
            def adj_step(i, acc):
                acc_r, acc_i = acc
                for g in range(nch):
                    lr, li = fix(g, i)
                    prev = tile(g, seg - 2 - i)
                    fr, fi = sf_ref.at[0][prev, :].astype(F32), sf_ref.at[1][prev, :].astype(F32)
                    acc_r, acc_i = acc_r + lr * fr + li * fi, acc_i + li * fr - lr * fi
                return acc_r, acc_i

            acc_r, acc_i = lax.fori_loop(0, seg - 1, adj_step, (zero, zero), unroll=2)
            first_block = tt(t_blk) == 0
            for g in range(nch):
                lr, li = fix(g, seg - 1)
                seg_ends = tile(g, seg - 1)
                if g == 0:
                    pvr = jnp.where(first_block, 0.0, sp_ref[0, SUBLANES - 1:SUBLANES, :].astype(F32))
                    pvi = jnp.where(first_block, 0.0, sp_ref[1, SUBLANES - 1:SUBLANES, :].astype(F32))
                else:
                    pvr = sf_ref[0, g * chain_rows - 1:g * chain_rows, :].astype(F32)
                    pvi = sf_ref[1, g * chain_rows - 1:g * chain_rows, :].astype(F32)
                fr = jnp.where(sub == 0, pvr, pltpu.roll(sf_ref.at[0][seg_ends, :].astype(F32), 1, 0))
                fi = jnp.where(sub == 0, pvi, pltpu.roll(sf_ref.at[1][seg_ends, :].astype(F32), 1, 0))
                acc_r = acc_r + lr * fr + li * fi
                acc_i = acc_i + li * fr - lr * fi
            da_ref[0] += jnp.sum(acc_r, axis=0, keepdims=True)
            da_ref[1] += jnp.sum(acc_i, axis=0, keepdims=True)

    x_spec = pl.BlockSpec((tb, bd_w), lambda j, t: (tt(t), j // tiles_per_bd))
    w_spec = pl.BlockSpec((bd_w, 2 * w), lambda j, t: (j // tiles_per_bd, j))
    a_spec = pl.BlockSpec((2, 1, w), lambda j, t: (0, 0, j))
    s_spec = pl.BlockSpec((2, tb, w), lambda j, t: (0, tt(t), j))
    in_specs, args = [x_spec, w_spec, a_spec], [x, wmat, a_pair]
    out_specs, out_shape = [s_spec], [jax.ShapeDtypeStruct((2, l, SSM_S), BF16)]
    scratch = [pltpu.VMEM((2, seg, w), F32), pltpu.VMEM((2, 1, w), F32), pltpu.VMEM((tb, 2 * w), F32)]
    if with_da:
        in_specs += [s_spec, pl.BlockSpec((2, SUBLANES, w),
                                          lambda j, t: (0, jnp.maximum(tt(t) * (tb // SUBLANES) - 1, 0), j))]
        args += [s_fwd, s_fwd]
        out_specs.append(a_spec)
        out_shape.append(jax.ShapeDtypeStruct((2, 1, SSM_S), F32))
    res = pl.pallas_call(
        body, grid=(SSM_S // w, nt), in_specs=in_specs, out_specs=out_specs, out_shape=out_shape,
        scratch_shapes=scratch, compiler_params=_params(("parallel", "arbitrary")), name=name,
    )(*args)
    return res if with_da else res[0]


def _nt_dot(x, y):
    return lax.dot_general(x.astype(_MXU), y.astype(_MXU), (((1,), (1,)), ((), ())), preferred_element_type=F32)


def _tn_dot(x, y):
    return lax.dot_general(x.astype(_MXU), y.astype(_MXU), (((0,), (0,)), ((), ())), preferred_element_type=F32)


def _nn_dot(x, y):
    return jnp.dot(x.astype(_MXU), y.astype(_MXU), preferred_element_type=F32)


def _attn_mask2(gb, nb):
    qi = lax.broadcasted_iota(jnp.int32, (ATT_WIN, 2 * ATT_WIN), 0)
    c = lax.broadcasted_iota(jnp.int32, (ATT_WIN, 2 * ATT_WIN), 1)
    has_prev = (gb % nb) != 0
    prev_ok = jnp.logical_and(jnp.logical_and(c < ATT_WIN, c >= qi), has_prev)
    own_ok = jnp.logical_and(c >= ATT_WIN, c - ATT_WIN <= qi)
    return jnp.logical_or(prev_ok, own_ok)


def _attn_specs():
    cur = pl.BlockSpec((ATT_QB * ATT_WIN, ATT_GW), lambda i: (i, 0))
    prev = pl.BlockSpec((ATT_WIN, ATT_GW), lambda i: (jnp.maximum(ATT_QB * i - 1, 0), 0))
    return cur, prev


def _attn_fwd(q, k, v, nb, name):
    l = q.shape[0]
    scale = ATT_E ** -0.5
    w = ATT_WIN

    def body(q_ref, kc_ref, kp_ref, vc_ref, vp_ref, o_ref, lse_ref):
        i = pl.program_id(0)
        masks = [_attn_mask2(ATT_QB * i + b, nb) for b in range(ATT_QB)]
        for h in range(ATT_HG):
            sl = slice(h * ATT_E, (h + 1) * ATT_E)
            k_ext = jnp.concatenate([kp_ref[:, sl], kc_ref[:, sl]], axis=0)
            v_ext = jnp.concatenate([vp_ref[:, sl], vc_ref[:, sl]], axis=0)
            for b in range(ATT_QB):
                r, kr = slice(b * w, (b + 1) * w), slice(b * w, (b + 2) * w)
                s = jnp.where(masks[b], _nt_dot(q_ref[r, sl], k_ext[kr]) * scale, NEG_INF)
                mx = jnp.max(s, axis=-1, keepdims=True)
                p = jnp.exp(s - mx)
                den = jnp.sum(p, axis=-1, keepdims=True)
                o_ref[r, sl] = _nn_dot(p, v_ext[kr]) / den
                lse_ref[r, sl] = jnp.broadcast_to(mx + jnp.log(den), (w, ATT_E))

    cur, prev = _attn_specs()
    return pl.pallas_call(
        body, grid=(l // (ATT_QB * w),), in_specs=[cur, cur, prev, cur, prev], out_specs=[cur, cur],
        out_shape=[jax.ShapeDtypeStruct((l, ATT_GW), F32)] * 2,
        compiler_params=_params(("parallel",)), name=name,
    )(q, k, k, v, v)


def _attn_bwd(q, k, v, do, lse, dd, nb, name):
    l = q.shape[0]
    scale = ATT_E ** -0.5
    w = ATT_WIN
    nblk = l // w

    def body(q_ref, kc_ref, kp_ref, vc_ref, vp_ref, do_ref, lse_ref, dd_ref, qn_ref, don_ref, lsen_ref, ddn_ref,
             dq_ref, dk_ref, dv_ref, dk_acc, dv_acc):
        i = pl.program_id(0)
        masks = [_attn_mask2(ATT_QB * i + b, nb) for b in range(ATT_QB)]
        nxt = ATT_QB * (i + 1)
        nxt_attends = jnp.logical_and(nxt < nblk, (nxt % nb) != 0)
        qi = lax.broadcasted_iota(jnp.int32, (w, w), 0)
        kj = lax.broadcasted_iota(jnp.int32, (w, w), 1)
        mask_n = jnp.logical_and(kj >= qi, nxt_attends)
        dk_acc[...] = jnp.zeros_like(dk_acc)
        dv_acc[...] = jnp.zeros_like(dv_acc)
        for h in range(ATT_HG):
            sl, col = slice(h * ATT_E, (h + 1) * ATT_E), slice(h * ATT_E, h * ATT_E + 1)
            k_ext = jnp.concatenate([kp_ref[:, sl], kc_ref[:, sl]], axis=0)
            v_ext = jnp.concatenate([vp_ref[:, sl], vc_ref[:, sl]], axis=0)
            for b in range(ATT_QB):
                r, kr = slice(b * w, (b + 1) * w), slice(b * w, (b + 2) * w)
                qh, doh, k2, v2 = q_ref[r, sl], do_ref[r, sl], k_ext[kr], v_ext[kr]
                p = jnp.where(masks[b], jnp.exp(_nt_dot(qh, k2) * scale - lse_ref[r, col]), 0.0)
                ds = p * (_nt_dot(doh, v2) - dd_ref[r, col]) * scale
                dq_ref[r, sl] = _nn_dot(ds, k2).astype(dq_ref.dtype)
                dk2, dv2 = _tn_dot(ds, qh), _tn_dot(p, doh)
                dk_acc[r, sl] += dk2[w:]
                dv_acc[r, sl] += dv2[w:]
                if b > 0:
                    rp = slice((b - 1) * w, b * w)
                    dk_acc[rp, sl] += dk2[:w]
                    dv_acc[rp, sl] += dv2[:w]
            last = slice((ATT_QB - 1) * w, ATT_QB * w)
            kl, vl, qn, don = kc_ref[last, sl], vc_ref[last, sl], qn_ref[:, sl], don_ref[:, sl]
            pn = jnp.where(mask_n, jnp.exp(_nt_dot(qn, kl) * scale - lsen_ref[:, col]), 0.0)
            dsn = pn * (_nt_dot(don, vl) - ddn_ref[:, col]) * scale
            dk_acc[last, sl] += _tn_dot(dsn, qn)
            dv_acc[last, sl] += _tn_dot(pn, don)
        dk_ref[...] = dk_acc[...].astype(dk_ref.dtype)
        dv_ref[...] = dv_acc[...].astype(dv_ref.dtype)

    cur, prev = _attn_specs()
    nxt_spec = pl.BlockSpec((w, ATT_GW), lambda i: (jnp.minimum(ATT_QB * (i + 1), nblk - 1), 0))
    return pl.pallas_call(
        body, grid=(l // (ATT_QB * w),),
        in_specs=[cur, cur, prev, cur, prev, cur, cur, cur, nxt_spec, nxt_spec, nxt_spec, nxt_spec],
        out_specs=[cur] * 3, out_shape=[jax.ShapeDtypeStruct((l, ATT_GW), BF16)] * 3,
        scratch_shapes=[pltpu.VMEM((ATT_QB * w, ATT_GW), F32)] * 2,
        compiler_params=_params(("parallel",)), name=name,
    )(q, k, k, v, v, do, lse, dd, q, do, lse, dd)


def _to_perm(a, d):
    if d == 1:
        return a
    l, c = a.shape
    return a.reshape(l // d, d, c).transpose(1, 0, 2).reshape(l, c)


def _from_perm(a, d):
    if d == 1:
        return a
    l, c = a.shape
    return a.reshape(d, l // d, c).transpose(1, 0, 2).reshape(l, c)


def _mem_probs(qh, kh):
    s = _nt_dot(qh, kh) * (MEM_E ** -0.5)
    e = jnp.exp(s - jnp.max(s, axis=-1, keepdims=True))
    return e / jnp.sum(e, axis=-1, keepdims=True)


def _mem_fwd(mq, kv, name, tm=512):
    l, nm = mq.shape[0], kv.shape[0]

    def body(q_ref, kv_ref, o_ref):
        for h in range(MEM_H):
            sl = slice(h * MEM_E, (h + 1) * MEM_E)
            p = _mem_probs(q_ref[:, sl], kv_ref[:, sl])
            o_ref[:, sl] = _nn_dot(p, kv_ref[:, MEM_W + h * MEM_E:MEM_W + (h + 1) * MEM_E]).astype(o_ref.dtype)

    return pl.pallas_call(
        body, grid=(l // tm,),
        in_specs=[pl.BlockSpec((tm, MEM_W), lambda i: (i, 0)), pl.BlockSpec((nm, 2 * MEM_W), lambda i: (0, 0))],
        out_specs=pl.BlockSpec((tm, MEM_W), lambda i: (i, 0)),
        out_shape=jax.ShapeDtypeStruct((l, MEM_W), BF16),
        compiler_params=_params(("parallel",)), name=name,
    )(mq, kv)


def _mem_bwd(mq, kv, dmo, name, tm=512):
    l, nm = mq.shape[0], kv.shape[0]
    scale = MEM_E ** -0.5

    def body(q_ref, kv_ref, do_ref, dq_ref, dkv_ref):
        @pl.when(pl.program_id(0) == 0)
        def _():
            dkv_ref[...] = jnp.zeros_like(dkv_ref)

        for h in range(MEM_H):
            sl = slice(h * MEM_E, (h + 1) * MEM_E)
            vsl = slice(MEM_W + h * MEM_E, MEM_W + (h + 1) * MEM_E)
            qh, kh, vh, doh = q_ref[:, sl], kv_ref[:, sl], kv_ref[:, vsl], do_ref[:, sl]
            p = _mem_probs(qh, kh)
            dp = _nt_dot(doh, vh)
            ds = p * (dp - jnp.sum(dp * p, axis=-1, keepdims=True)) * scale
            dq_ref[:, sl] = _nn_dot(ds, kh).astype(dq_ref.dtype)
            dkv_ref[:, sl] += _tn_dot(ds, qh)
            dkv_ref[:, vsl] += _tn_dot(p, doh)

    row = pl.BlockSpec((tm, MEM_W), lambda i: (i, 0))
    full = pl.BlockSpec((nm, 2 * MEM_W), lambda i: (0, 0))
    return pl.pallas_call(
        body, grid=(l // tm,), in_specs=[row, full, row], out_specs=[row, full],
        out_shape=[jax.ShapeDtypeStruct((l, MEM_W), BF16), jax.ShapeDtypeStruct((nm, 2 * MEM_W), F32)],
        compiler_params=_params(("arbitrary",)), name=name,
    )(mq, kv, dmo)


def _discretize(lam_re, lam_im, log_dt, b_re, b_im):
    dt = jnp.exp(log_dt)[:, None]
    mag = jnp.exp(lam_re * dt)
    a_re, a_im = mag * jnp.cos(lam_im * dt), mag * jnp.sin(lam_im * dt)
    nr, ni = a_re - 1.0, a_im
    den = lam_re * lam_re + lam_im * lam_im
    coef_re = (nr * lam_re + ni * lam_im) / den
    coef_im = (ni * lam_re - nr * lam_im) / den
    bb_re = coef_re[..., None] * b_re - coef_im[..., None] * b_im
    bb_im = coef_re[..., None] * b_im + coef_im[..., None] * b_re
    return a_re, a_im, bb_re, bb_im


def _tile_cat(re, im):
    lead = re.shape[:-1]
    t = jnp.stack([re.reshape(*lead, SCAN_NT, SCAN_W), im.reshape(*lead, SCAN_NT, SCAN_W)], axis=-2)
    return t.reshape(*lead, 2 * SSM_S)


def _bd_in(bb):
    return jnp.einsum("gph,gk->ghkp", bb, jnp.eye(SSM_G, dtype=bb.dtype)).reshape(SSM_W, SSM_S)


def _bd_diag(x):
    gb = SSM_G // SSM_BD
    t = x.reshape(SSM_BD, gb, SSM_H, gb, SSM_P)
    return jnp.einsum("bghgp->bghp", t).reshape(SSM_G, SSM_H, SSM_P)


_ANY = pl.BlockSpec(memory_space=pl.ANY)
_MESH = pl.DeviceIdType.MESH


def _allgather(x, name):
    def body(x_ref, out_ref, send_sems, recv_sems, local_sem):
        mx, my, mc = lax.axis_index("x"), lax.axis_index("y"), lax.axis_index("c")
        me, sibling = (mx, my, mc), (mx, my, 1 - mc)
        chips = [(1 - mx, my), (mx, 1 - my), (1 - mx, 1 - my)]

        def blk(px, py, pc):
            return out_ref.at[4 * px + 2 * py + pc]

        def copy(k, block, to, src=None):
            return pltpu.make_async_remote_copy(
                src_ref=blk(*block) if src is None else src, dst_ref=blk(*block),
                send_sem=send_sems.at[k], recv_sem=recv_sems.at[k], device_id=to, device_id_type=_MESH)

        mine = pltpu.make_async_copy(x_ref, blk(*me), local_sem)
        mine.start()
        first = [copy(0, me, sibling, src=x_ref)]
        first += [copy(1 + j, me, (*chip, mc), src=x_ref) for j, chip in enumerate(chips)]
        for cp in first:
            cp.start()
        passed = [copy(4 + j, (*chip, mc), sibling) for j, chip in enumerate(chips)]
        for j, chip in enumerate(chips):
            copy(1 + j, (*chip, mc), me).wait_recv()
            passed[j].start()
        copy(0, sibling, me).wait_recv()
        for j, chip in enumerate(chips):
            copy(4 + j, (*chip, 1 - mc), me).wait_recv()
        for cp in first + passed:
            cp.wait_send()
        mine.wait()

    return pl.pallas_call(
        body, out_shape=jax.ShapeDtypeStruct((N_DEV,) + x.shape, x.dtype), in_specs=[_ANY], out_specs=_ANY,
        scratch_shapes=[pltpu.SemaphoreType.DMA((7,)), pltpu.SemaphoreType.DMA((7,)), pltpu.SemaphoreType.DMA],
        name=name,
    )(x)


def _pair_exchange(g, name):
    def body(g_ref, out_ref, send_sems, recv_sems):
        mx, my, mc = lax.axis_index("x"), lax.axis_index("y"), lax.axis_index("c")
        copies = [pltpu.make_async_remote_copy(
            src_ref=g_ref.at[2 * k + (1 - mc)], dst_ref=out_ref.at[k], send_sem=send_sems.at[k],
            recv_sem=recv_sems.at[k], device_id=(mx, my, 1 - mc), device_id_type=_MESH) for k in range(4)]
        for cp in copies:
            cp.start()
        for cp in copies:
            cp.wait()

    return pl.pallas_call(
        body, out_shape=jax.ShapeDtypeStruct((4,) + g.shape[1:], g.dtype), in_specs=[_ANY], out_specs=_ANY,
        scratch_shapes=[pltpu.SemaphoreType.DMA((4,)), pltpu.SemaphoreType.DMA((4,))], name=name,
    )(g)


def _chip_exchange(p, name):
    def body(p_ref, out_ref, send_sems, recv_sems):
        mx, my, mc = lax.axis_index("x"), lax.axis_index("y"), lax.axis_index("c")
        chips = [(1 - mx, my), (mx, 1 - my), (1 - mx, 1 - my)]
        copies = [pltpu.make_async_remote_copy(
            src_ref=p_ref.at[2 * px + py], dst_ref=out_ref.at[j], send_sem=send_sems.at[j],
            recv_sem=recv_sems.at[j], device_id=(px, py, mc), device_id_type=_MESH)
            for j, (px, py) in enumerate(chips)]
        for cp in copies:
            cp.start()
        for cp in copies:
            cp.wait()

    return pl.pallas_call(
        body, out_shape=jax.ShapeDtypeStruct((3,) + p.shape[1:], p.dtype), in_specs=[_ANY], out_specs=_ANY,
        scratch_shapes=[pltpu.SemaphoreType.DMA((3,)), pltpu.SemaphoreType.DMA((3,))], name=name,
    )(p)


def _pair_sum(g, t1, my_c, name, tr):
    _, r, c = g.shape

    def body(c_ref, g_ref, t_ref, o_ref, ob_ref):
        s = g_ref[...] + t_ref[...]
        o_ref[...] = s
        ob_ref[...] = s.astype(BF16)

    blk = pl.BlockSpec((None, tr, c), lambda k, i, cr: (k, i, 0))
    return pl.pallas_call(
        body,
        grid_spec=pltpu.PrefetchScalarGridSpec(
            num_scalar_prefetch=1, grid=(4, r // tr),
            in_specs=[pl.BlockSpec((None, tr, c), lambda k, i, cr: (2 * k + cr[0], i, 0)), blk],
            out_specs=[blk, blk]),
        out_shape=[jax.ShapeDtypeStruct((4, r, c), F32), jax.ShapeDtypeStruct((4, r, c), BF16)],
        compiler_params=_params(("parallel", "parallel")), name=name,
    )(my_c, g, t1)


def _adam_math(g, w, m, v):
    m = ADAM_B1 * m + (1.0 - ADAM_B1) * g
    v = ADAM_B2 * v + (1.0 - ADAM_B2) * (g * g)
    m_hat = m / (1.0 - ADAM_B1 ** ADAM_STEP)
    v_hat = v / (1.0 - ADAM_B2 ** ADAM_STEP)
    delta = -ADAM_LR * (m_hat / (jnp.sqrt(v_hat) + ADAM_EPS) + ADAM_WD * w)
    return delta, m, v


def _adam_big(p, t2, my_chip, w, m, v, name, tr):
    r, c = w.shape

    def body(k_ref, p_ref, t0_ref, t1_ref, t2_ref, w_ref, m_ref, v_ref, g_out, d_out, m_out, v_out):
        g = ((p_ref[...] + t0_ref[...].astype(F32)) + t1_ref[...].astype(F32)) + t2_ref[...].astype(F32)
        d, mn, vn = _adam_math(g, w_ref[...], m_ref[...], v_ref[...])
        g_out[...], d_out[...], m_out[...], v_out[...] = g, d, mn, vn

    flat = pl.BlockSpec((tr, c), lambda i, kr: (i, 0))

    def rel(j):
        return pl.BlockSpec((None, tr, c), lambda i, kr: (j, i, 0))

    return pl.pallas_call(
        body,
        grid_spec=pltpu.PrefetchScalarGridSpec(
            num_scalar_prefetch=1, grid=(r // tr,),
            in_specs=[pl.BlockSpec((None, tr, c), lambda i, kr: (kr[0], i, 0)), rel(0), rel(1), rel(2), flat, flat, flat],
            out_specs=[flat] * 4),
        out_shape=[jax.ShapeDtypeStruct((r, c), F32)] * 4,
        compiler_params=_params(("parallel",)), name=name,
    )(my_chip, p, t2, t2, t2, w, m, v)


def _sum8(g8, name):
    _, r, c = g8.shape

    def body(g_ref, o_ref):
        acc = g_ref[0]
        for j in range(1, N_DEV):
            acc = acc + g_ref[j]
        o_ref[...] = acc

    return pl.pallas_call(
        body, grid=(1,), in_specs=[pl.BlockSpec((N_DEV, r, c), lambda i: (0, 0, 0))],
        out_specs=pl.BlockSpec((r, c), lambda i: (0, 0)), out_shape=jax.ShapeDtypeStruct((r, c), F32),
        compiler_params=_params(("arbitrary",)), name=name,
    )(g8)


def _adam_small(g, w, m, v, name):
    def fn(r, b):
        return list(_adam_math(*r)), []
    c = g.shape[1]
    return _ew(fn, [g, w, m, v], [], [(c, F32)] * 3, [], name=name, tm=g.shape[0])


def _pack(arrs, pad_rows=8):
    flat = jnp.concatenate([a.reshape(-1) for a in arrs])
    n = flat.shape[0]
    q = PACK_C * pad_rows
    tot = -(-n // q) * q
    if tot != n:
        flat = jnp.concatenate([flat, jnp.zeros((tot - n,), flat.dtype)])
    return flat.reshape(tot // PACK_C, PACK_C)


def _unpack(buf, shapes):
    flat = buf.reshape(-1)
    out, off = [], 0
    for s in shapes:
        n = int(np.prod(s))
        out.append(flat[off:off + n].reshape(s))
        off += n
    return out


GROUPS = (("w_in", "w_mem_kv", "w_o", "w_up", "w_down"),
          ("w_glu", "w_ssm_br", "w_mem_br", "w_attn_br"))
GROUP_TR = (208, 384)
ATTN_BR_FOLD = 2


def _stored_shape(name):
    r, c, ax = BIG_SHAPE[name]
    rows, cols = (r // N_DEV, c) if ax == 0 else (c // N_DEV, r)
    return (rows // ATTN_BR_FOLD, cols * ATTN_BR_FOLD) if name == "w_attn_br" else (rows, cols)


def _stored(shard, name):
    a = shard[0].T if BIG_SHAPE[name][2] == 1 else shard[0]
    return a.reshape(_stored_shape(name))


def _unstored(a, name):
    r, c, ax = BIG_SHAPE[name]
    if ax == 0:
        return a.reshape(1, r // N_DEV, c)
    return a.reshape(c // N_DEV, r).T[None]


def _pack_group(d, names):
    return jnp.concatenate([_stored(d[n], n) for n in names], axis=0)


def _split_group(buf, names):
    out, off = {}, 0
    for n in names:
        rows = _stored_shape(n)[0]
        out[n] = buf[..., off:off + rows, :]
        off += rows
    return out


def _full_stored(stacked, name):
    r, c, ax = BIG_SHAPE[name]
    return stacked.reshape((r, c) if ax == 0 else (c, r))


def _stacked_stored(full, name):
    return full.reshape((N_DEV,) + _stored_shape(name))


def _gelu_parts(x):
    c0, c1 = math.sqrt(2.0 / math.pi), 0.044715
    th = jnp.tanh(c0 * (x + c1 * x * x * x))
    return th, c0, c1


def _local_step(x, mem, tgt, wb, sp):
    l = x.shape[0]
    w_a, w_g = wb["w_in"][:ZA_W], wb["w_in"][ZA_W:]

    a_re, a_im, bb_re, bb_im = _discretize(sp["ssm_lambda_re"], sp["ssm_lambda_im"], sp["ssm_log_dt"],
                                           sp["ssm_b_re"], sp["ssm_b_im"])
    a_pair = jnp.stack([a_re.reshape(1, SSM_S), a_im.reshape(1, SSM_S)])
    a_conj = jnp.stack([a_re.reshape(1, SSM_S), -a_im.reshape(1, SSM_S)])
    b_re_t, b_im_t = _bd_in(bb_re).astype(BF16), _bd_in(bb_im).astype(BF16)
    c_re_t = _bd_in(sp["ssm_c_re"].transpose(0, 2, 1)).astype(BF16)
    c_im_t = (-_bd_in(sp["ssm_c_im"].transpose(0, 2, 1))).astype(BF16)
    bbd = _tile_cat(b_re_t, b_im_t)
    cbd_t = _tile_cat(c_re_t, c_im_t)
    d_row = sp["ssm_d"].reshape(1, SSM_W)

    n1 = _rms_fwd(x, sp["norm1_g"], "rms1")
    za = _mm(n1, w_a, [BF16], tb=True, name="in_proj_a", tn=1664)
    zg = _mm(n1, w_g, [BF16], tb=True, name="in_proj_g")
    u = za[:, :SSM_W]
    mq = za[:, ZA_W - MEM_W:]

    u_s = _scan_order(u)
    s_all = _ssm_scan(u_s, bbd, a_pair, reverse=False, name="ssm_scan_fwd")
    ys = _time_order(_mm((s_all, 0), c_re_t, [F32], tb=True, pair2=((s_all, 1), c_im_t), bd=SSM_BD, name="ssm_cs"))

    def gelu_fn(r, b):
        y0 = r[0] + b[0] * r[1].astype(F32)
        th, _, _ = _gelu_parts(y0)
        return [y0, 0.5 * y0 * (1.0 + th)], []
    y0, y1 = _ew(gelu_fn, [ys, u], [d_row], [(SSM_W, F32), (SSM_W, BF16)], [], name="ssm_gelu", tm=512)

    def glu_epi(acc, y1t, bg):
        t = acc + bg
        return t, y1t.astype(F32) * _sigmoid(t)
    t_glu, y2 = _mm(y1, wb["w_glu"], [F32, BF16], epi=glu_epi, mn=[y1], rows=[sp["b_glu"]], name="ssm_glu")
    br_ssm = _mm(y2, wb["w_ssm_br"], [BF16], tb=True, name="ssm_br")

    qkv_p, o_g, lse_g = [], [], []
    for g, d in enumerate(DILATIONS):
        nb = l // d // ATT_WIN
        cols = [za[:, SSM_W + (3 * j + g) * ATT_GW: SSM_W + (3 * j + g + 1) * ATT_GW] for j in range(3)]
        qp, kp, vp = [_to_perm(cc, d) for cc in cols]
        qkv_p.append((qp, kp, vp))
        og, lg = _attn_fwd(qp, kp, vp, nb, "attn_fwd%d" % g)
        o_g.append(_from_perm(og, d))
        lse_g.append(_from_perm(lg, d))

    def merge_fn(r, b):
        o0, o1, o2, l0, l1, l2 = r
        mx = jnp.maximum(jnp.maximum(l0, l1), l2)
        e0, e1, e2 = jnp.exp(l0 - mx), jnp.exp(l1 - mx), jnp.exp(l2 - mx)
        tot = e0 + e1 + e2
        return [(e0 * o0 + e1 * o1 + e2 * o2) / tot, mx + jnp.log(tot)], []
    o_att, lse_tot = _ew(merge_fn, o_g + lse_g, [], [(ATT_GW, F32), (ATT_GW, F32)], [], name="attn_merge", tm=512)
    br_attn = _mm(o_att, wb["w_attn_br"], [BF16], tb=True, name="attn_br")

    mn = _rms_fwd(mem, sp["mem_norm_g"], "rms_mem")
    kv = _mm(mn, wb["w_mem_kv"], [BF16], name="mem_kv")
    mo = _mem_fwd(mq, kv, "mem_attn_fwd")
    br_mem = _mm(mo, wb["w_mem_br"], [BF16], tb=True, name="mem_br")

    def gate_fn(r, b):
        zgt, b0, b1, b2 = [t.astype(F32) for t in r]
        gt = _sigmoid(zgt + b[0])
        return [gt[:, :D_MODEL] * b0 + gt[:, D_MODEL:2 * D_MODEL] * b1 + gt[:, 2 * D_MODEL:] * b2], []
    merged = _ew(gate_fn, [zg, br_ssm, br_attn, br_mem], [sp["b_gate"]], [(D_MODEL, BF16)], [], name="gate_merge")[0]
    h1 = _mm(merged, wb["w_o"], [F32], epi=lambda acc, xt: (acc + xt,), mn=[x], name="o_proj")
    n2 = _rms_fwd(h1, sp["norm2_g"], "rms2")

    def up_epi(acc):
        ra = jnp.maximum(acc, 0.0)
        return ra * ra, ra
    f_act, r_act = _mm(n2, wb["w_up"], [BF16, BF16], tb=True, epi=up_epi, name="mlp_up")
    h2 = _mm(f_act, wb["w_down"], [F32], epi=lambda acc, ht: (acc + ht,), mn=[h1], name="mlp_down")

    def final_fn(r, b):
        hv, tv = r
        gf = b[0]
        rs = lax.rsqrt(jnp.mean(hv * hv, axis=-1, keepdims=True) + RMS_EPS)
        err = hv * rs * gf - tv
        dy = err * (1.0 / D_MODEL)
        gd = dy * gf
        dh = rs * gd - hv * (rs * rs * rs) * jnp.mean(gd * hv, axis=-1, keepdims=True)
        loss = _colsum(jnp.sum(err * err, axis=-1, keepdims=True)) * (0.5 / D_MODEL)
        return [dh], [_colsum(dy * hv * rs), loss]
    dh2, d_final_g, loss = _ew(final_fn, [h2, tgt], [sp["final_g"]], [(D_MODEL, F32)], [D_MODEL, 1], name="final_loss")

    gw, gs = {}, {"final_g": d_final_g}
    d_act = _mm(dh2, wb["w_down"], [BF16], tb=True, epi=lambda acc, ra: (acc * 2.0 * ra.astype(F32),), mn=[r_act],
                name="mlp_down_dx")
    gw["w_down"] = _mm(f_act, dh2, [F32], ta=True, name="mlp_down_dw")
    dn2 = _mm(d_act, wb["w_up"], [F32], name="mlp_up_dx")
    gw["w_up"] = _mm(d_act, n2, [F32], ta=True, name="mlp_up_dw")
    dh1, gs["norm2_g"] = _rms_bwd(h1, dn2, dh2, sp["norm2_g"], "rms2_bwd")
    dmerged = _mm(dh1, wb["w_o"], [F32], tb=True, name="o_proj_dx")
    gw["w_o"] = _mm(merged, dh1, [F32], ta=True, name="o_proj_dw")

    def gate_bwd_fn(r, b):
        dm, zgt, b0, b1, b2 = [t.astype(F32) for t in r]
        gt = _sigmoid(zgt + b[0])
        g0, g1, g2 = gt[:, :D_MODEL], gt[:, D_MODEL:2 * D_MODEL], gt[:, 2 * D_MODEL:]
        dzg = jnp.concatenate([dm * b0 * g0 * (1.0 - g0), dm * b1 * g1 * (1.0 - g1), dm * b2 * g2 * (1.0 - g2)], axis=1)
        return [dm * g0, dm * g1, dm * g2, dzg], [_colsum(dzg)]
    dbr_ssm, dbr_attn, dbr_mem, dzg, gs["b_gate"] = _ew(
        gate_bwd_fn, [dmerged, zg, br_ssm, br_attn, br_mem], [sp["b_gate"]],
        [(D_MODEL, BF16)] * 3 + [(ZG_W, BF16)], [ZG_W], name="gate_bwd")

    gw["w_ssm_br"] = _mm(dbr_ssm, y2, [F32], ta=True, name="ssm_br_dw")
    dy2 = _mm(dbr_ssm, wb["w_ssm_br"], [F32], name="ssm_br_dx")

    def glu_bwd_fn(r, b):
        dy, y1t, tt = r
        sg = _sigmoid(tt)
        dt = dy * y1t.astype(F32) * sg * (1.0 - sg)
        return [dt, dy * sg], [_colsum(dt)]
    dt_glu, dy1a, gs["b_glu"] = _ew(glu_bwd_fn, [dy2, y1, t_glu], [], [(SSM_W, BF16), (SSM_W, F32)], [SSM_W],
                                    name="ssm_glu_bwd", tm=512)
    gw["w_glu"] = _mm(y1, dt_glu, [F32], ta=True, name="ssm_glu_dw")

    def gelu_bwd_epi(acc, dy1t, y0t):
        th, c0, c1 = _gelu_parts(y0t)
        dg = 0.5 * (1.0 + th) + 0.5 * y0t * (1.0 - th * th) * c0 * (1.0 + 3.0 * c1 * y0t * y0t)
        return ((acc + dy1t) * dg,)
    dy0 = _mm(dt_glu, wb["w_glu"], [F32], tb=True, epi=gelu_bwd_epi, mn=[dy1a, y0], name="ssm_glu_dx")
    gs["ssm_d"] = _ew(lambda r, b: ([], [_colsum(r[0] * r[1].astype(F32))]), [dy0, u], [], [], [SSM_W],
                      name="ssm_dd", tm=512)[0]
    dy0_s = _scan_order(dy0)
    dcr = _mm(dy0_s, (s_all, 0), [F32], ta=True, bd=SSM_BD, name="ssm_cs_dw_re")
    dci = _mm(dy0_s, (s_all, 1), [F32], ta=True, bd=SSM_BD, name="ssm_cs_dw_im")
    lam, da = _ssm_scan(dy0_s, cbd_t, a_conj, reverse=True, s_fwd=s_all, name="ssm_scan_bwd")
    du = _time_order(_mm((lam, 0), b_re_t, [BF16], tb=True, pair2=((lam, 1), b_im_t),
                         epi=lambda acc, dyt, dr: (acc + dyt * dr,), mn=[dy0_s], rows=[d_row], bd=SSM_BD, name="ssm_bu_dx"))
    dbr = _mm(u_s, (lam, 0), [F32], ta=True, bd=SSM_BD, name="ssm_bu_dw_re")
    dbi = _mm(u_s, (lam, 1), [F32], ta=True, bd=SSM_BD, name="ssm_bu_dw_im")
    gs["a_re"], gs["a_im"] = da[0], da[1]
    gs["bb_re"], gs["bb_im"] = _bd_diag(dbr).transpose(0, 2, 1), _bd_diag(dbi).transpose(0, 2, 1)
    gs["ssm_c_re"], gs["ssm_c_im"] = _bd_diag(dcr), -_bd_diag(dci)

    gw["w_attn_br"] = _mm(dbr_attn, o_att, [F32], ta=True, name="attn_br_dw")

    def do_epi(acc, ot):
        prod = acc * ot
        head = lax.broadcasted_iota(jnp.int32, prod.shape, 1) // ATT_E
        dd = jnp.zeros_like(prod)
        for h in range(ATT_HG):
            dd = jnp.where(head == h, jnp.sum(jnp.where(head == h, prod, 0.0), axis=1, keepdims=True), dd)
        return acc, dd
    do_att, dd_att = _mm(dbr_attn, wb["w_attn_br"], [BF16, F32], epi=do_epi, mn=[o_att], name="attn_br_dx")
    dq_l, dk_l, dv_l = [], [], []
    for g, d in enumerate(DILATIONS):
        nb = l // d // ATT_WIN
        qp, kp, vp = qkv_p[g]
        dq, dk, dv = _attn_bwd(qp, kp, vp, _to_perm(do_att, d), _to_perm(lse_tot, d), _to_perm(dd_att, d),
                               nb, "attn_bwd%d" % g)
        dq_l.append(_from_perm(dq, d))
        dk_l.append(_from_perm(dk, d))
        dv_l.append(_from_perm(dv, d))

    gw["w_mem_br"] = _mm(dbr_mem, mo, [F32], ta=True, name="mem_br_dw")
    dmo = _mm(dbr_mem, wb["w_mem_br"], [BF16], name="mem_br_dx")
    dmq, dkv = _mem_bwd(mq, kv, dmo, "mem_attn_bwd")
    gw["w_mem_kv"] = _mm(mn, dkv, [F32], ta=True, name="mem_kv_dw")
    dmn = _mm(dkv, wb["w_mem_kv"], [F32], tb=True, name="mem_kv_dx")
    gs["mem_norm_g"] = _rms_bwd(mem, dmn, None, sp["mem_norm_g"], "rms_mem_bwd")[1]

    dza = jnp.concatenate([du] + dq_l + dk_l + dv_l + [dmq], axis=1)
    dw_a = _mm(dza, n1, [F32], ta=True, name="in_proj_a_dw", tm=1664)
    dw_g = _mm(dzg, n1, [F32], ta=True, name="in_proj_g_dw")
    gw["w_in"] = jnp.concatenate([dw_a, dw_g], axis=0)
    dn_a = _mm(dza, w_a, [F32], name="in_proj_a_dx", tk=1664)
    dn1 = _mm(dzg, w_g, [F32], epi=lambda acc, pt: (acc + pt,), mn=[dn_a], name="in_proj_g_dx")
    grad_x, gs["norm1_g"] = _rms_bwd(x, dn1, dh1, sp["norm1_g"], "rms1_bwd")
    return loss, grad_x, gw, gs


_SMALL_GRAD_ORDER = ("norm1_g", "mem_norm_g", "b_gate", "a_re", "a_im", "bb_re", "bb_im", "ssm_c_re", "ssm_c_im",
                     "ssm_d", "b_glu", "norm2_g", "final_g")


def kernel(x, mem, norm1_g, mem_norm_g, w_in, b_gate, ssm_lambda_re, ssm_lambda_im, ssm_log_dt, ssm_b_re, ssm_b_im, ssm_c_re, ssm_c_im, ssm_d, w_glu, b_glu, w_ssm_br, w_attn_br, w_mem_kv, w_mem_br, w_o, norm2_g, w_up, w_down, final_g, loss_target, m_norm1_g, m_mem_norm_g, m_w_in, m_b_gate, m_ssm_lambda_re, m_ssm_lambda_im, m_ssm_log_dt, m_ssm_b_re, m_ssm_b_im, m_ssm_c_re, m_ssm_c_im, m_ssm_d, m_w_glu, m_b_glu, m_w_ssm_br, m_w_attn_br, m_w_mem_kv, m_w_mem_br, m_w_o, m_norm2_g, m_w_up, m_w_down, m_final_g, v_norm1_g, v_mem_norm_g, v_w_in, v_b_gate, v_ssm_lambda_re, v_ssm_lambda_im, v_ssm_log_dt, v_ssm_b_re, v_ssm_b_im, v_ssm_c_re, v_ssm_c_im, v_ssm_d, v_w_glu, v_b_glu, v_w_ssm_br, v_w_attn_br, v_w_mem_kv, v_w_mem_br, v_w_o, v_norm2_g, v_w_up, v_w_down, v_final_g):
    args = dict(locals())
    w = {n: args[n] for n in ALL_W}
    m = {n: args["m_" + n] for n in ALL_W}
    v = {n: args["v_" + n] for n in ALL_W}
    my_c = lax.axis_index("c").astype(jnp.int32).reshape(1)
    my_chip = (2 * lax.axis_index("x") + lax.axis_index("y")).astype(jnp.int32).reshape(1)

    w_pack = [_pack_group(w, names) for names in GROUPS]
    wb = {}
    for gi, names in enumerate(GROUPS):
        w_all = _allgather(w_pack[gi].astype(BF16), "allgather_weights%d" % gi)
        for n, part in _split_group(w_all, names).items():
            wb[n] = _full_stored(part, n)

    sp = {
        "norm1_g": norm1_g, "mem_norm_g": mem_norm_g, "b_gate": b_gate, "b_glu": b_glu, "norm2_g": norm2_g,
        "final_g": final_g.reshape(1, D_MODEL),
        "ssm_lambda_re": ssm_lambda_re[0], "ssm_lambda_im": ssm_lambda_im[0], "ssm_log_dt": ssm_log_dt[0],
        "ssm_b_re": ssm_b_re[0], "ssm_b_im": ssm_b_im[0], "ssm_c_re": ssm_c_re[0], "ssm_c_im": ssm_c_im[0],
        "ssm_d": ssm_d[0],
    }
    loss, grad_x, gw, gs = _local_step(x[0], mem[0], loss_target[0], wb, sp)
    loss = lax.psum(loss[0, 0], ("x", "y", "c"))

    big = [{}, {}, {}, {}]
    for gi, names in enumerate(GROUPS):
        g_pack = jnp.concatenate([_stacked_stored(gw[n], n) for n in names], axis=1)
        t1 = _pair_exchange(g_pack, "grad_pair_exchange%d" % gi)
        p_sum, p_bf = _pair_sum(g_pack, t1, my_c, "grad_pair_sum%d" % gi, GROUP_TR[gi])
        t2 = _chip_exchange(p_bf, "grad_chip_exchange%d" % gi)
        outs = _adam_big(p_sum, t2, my_chip, w_pack[gi], _pack_group(m, names), _pack_group(v, names),
                         "adam_big%d" % gi, GROUP_TR[gi])
        for kind, buf in enumerate(outs):
            for n, part in _split_group(buf, names).items():
                big[kind][n] = _unstored(part, n)

    sg_shapes = [gs[n].shape for n in _SMALL_GRAD_ORDER]
    sg_all = _allgather(_pack([gs[n] for n in _SMALL_GRAD_ORDER]), "allgather_small_grads")
    sg = dict(zip(_SMALL_GRAD_ORDER, _unpack(_sum8(sg_all, "sum_small_grads"), sg_shapes)))
    _, disc_vjp = jax.vjp(_discretize, sp["ssm_lambda_re"], sp["ssm_lambda_im"], sp["ssm_log_dt"],
                          sp["ssm_b_re"], sp["ssm_b_im"])
    d_lre, d_lim, d_ldt, d_bre, d_bim = disc_vjp((sg["a_re"].reshape(SSM_G, SSM_P), sg["a_im"].reshape(SSM_G, SSM_P),
                                                  sg["bb_re"], sg["bb_im"]))
    small_grad = {
        "norm1_g": sg["norm1_g"], "mem_norm_g": sg["mem_norm_g"], "b_gate": sg["b_gate"],
        "ssm_lambda_re": d_lre, "ssm_lambda_im": d_lim, "ssm_log_dt": d_ldt, "ssm_b_re": d_bre, "ssm_b_im": d_bim,
        "ssm_c_re": sg["ssm_c_re"], "ssm_c_im": sg["ssm_c_im"], "ssm_d": sg["ssm_d"], "b_glu": sg["b_glu"],
        "norm2_g": sg["norm2_g"], "final_g": sg["final_g"],
    }
    small_grad = {n: small_grad[n].reshape(w[n].shape) for n in SMALL}
    s_shapes = [w[n].shape for n in SMALL]
    small_out = _adam_small(_pack([small_grad[n] for n in SMALL]), _pack([w[n] for n in SMALL]),
                            _pack([m[n] for n in SMALL]), _pack([v[n] for n in SMALL]), "adam_small")
    small = [small_grad] + [dict(zip(SMALL, _unpack(b, s_shapes))) for b in small_out]

    outs = [loss, grad_x[None]]
    for kind in range(4):
        for n in ALL_W:
            outs.append(big[kind][n] if n in BIG else small[kind][n])
    return tuple(outs)
```

```python
import math

import numpy as np
import jax
import jax.numpy as jnp
from jax import lax
from jax.experimental import pallas as pl
from jax.experimental.pallas import tpu as pltpu

F32 = jnp.float32
BF16 = jnp.bfloat16
_MXU = jnp.bfloat16

D_MODEL = 1024
SSM_G, SSM_H, SSM_P = 32, 16, 64
SSM_W = SSM_G * SSM_H
SSM_S = SSM_G * SSM_P
SSM_BD = 4
ATT_E = 64
ATT_HG = 4
ATT_GW = ATT_HG * ATT_E
ATT_WIN = 128
ATT_QB = 4
DILATIONS = (1, 4, 16)
MEM_H, MEM_E = 4, 128
MEM_W = MEM_H * MEM_E
ZA_W = SSM_W + 9 * ATT_GW + MEM_W
ZG_W = 3 * D_MODEL
IN_W = ZA_W + ZG_W
RMS_EPS = 1e-6
NEG_INF = -1e30

ADAM_LR, ADAM_B1, ADAM_B2, ADAM_EPS, ADAM_WD, ADAM_STEP = 0.001, 0.9, 0.999, 1e-08, 0.01, 10

N_DEV = 8
PACK_C = 512
_VMEM_LIMIT = 56 * 1024 * 1024
SUBLANES = 16
SCAN_SEG = 128
SCAN_CHAINS = 2
SCAN_W = 128

BIG = ("w_in", "w_glu", "w_ssm_br", "w_attn_br", "w_mem_kv", "w_mem_br", "w_o", "w_up", "w_down")
BIG_SHAPE = {
    "w_in": (D_MODEL, IN_W, 1), "w_glu": (SSM_W, SSM_W, 0), "w_ssm_br": (SSM_W, D_MODEL, 1),
    "w_attn_br": (ATT_GW, D_MODEL, 1), "w_mem_kv": (D_MODEL, 2 * MEM_W, 0), "w_mem_br": (MEM_W, D_MODEL, 1),
    "w_o": (D_MODEL, D_MODEL, 0), "w_up": (D_MODEL, 4 * D_MODEL, 1), "w_down": (4 * D_MODEL, D_MODEL, 0),
}
SMALL = ("norm1_g", "mem_norm_g", "b_gate", "ssm_lambda_re", "ssm_lambda_im", "ssm_log_dt", "ssm_b_re",
         "ssm_b_im", "ssm_c_re", "ssm_c_im", "ssm_d", "b_glu", "norm2_g", "final_g")
ALL_W = ("norm1_g", "mem_norm_g", "w_in", "b_gate", "ssm_lambda_re", "ssm_lambda_im", "ssm_log_dt", "ssm_b_re",
         "ssm_b_im", "ssm_c_re", "ssm_c_im", "ssm_d", "w_glu", "b_glu", "w_ssm_br", "w_attn_br", "w_mem_kv",
         "w_mem_br", "w_o", "norm2_g", "w_up", "w_down", "final_g")


def _params(sem):
    return pltpu.CompilerParams(dimension_semantics=sem, vmem_limit_bytes=_VMEM_LIMIT)


def _pick(n, cap):
    if n <= cap:
        return n
    t = (cap // 128) * 128
    while n % t:
        t -= 128
    return t


def _mm(a, b, outs, *, name, ta=False, tb=False, epi=None, mn=(), rows=(), pair2=None, bd=0,
        tm=1024, tn=1024, tk=2048):
    ab = [a, b] + (list(pair2) if pair2 is not None else [])
    planes = [op[1] if isinstance(op, tuple) else None for op in ab]
    ab = [op[0] if isinstance(op, tuple) else op for op in ab]
    a_shape, b_shape = ab[0].shape[-2:], ab[1].shape[-2:]
    m = a_shape[1] if ta else a_shape[0]
    k = a_shape[0] if ta else a_shape[1]
    n = b_shape[0] if tb else b_shape[1]
    assert k == (b_shape[1] if tb else b_shape[0]), (name, a_shape, b_shape)
    out_n = n
    if bd and ta:
        assert not tb
        tm, tn, tk = m // bd, n // bd, _pick(k, tk)
        grid, out_n = (bd, 1, k // tk), tn
        a_blk = ((tk, tm), lambda i, j, kk: (kk, i))
        b_blk = ((tk, tn), lambda i, j, kk: (kk, i))
        mn_spec = pl.BlockSpec((tm, tn), lambda i, j, kk: (i, 0))
    elif bd:
        tm, tn, tk = _pick(m, tm), n // bd, k // bd
        grid = (m // tm, bd, 1)
        a_blk = ((tm, tk), lambda i, j, kk: (i, j))
        b_blk = ((tn, tk) if tb else (tk, tn), lambda i, j, kk: (j, j))
        mn_spec = pl.BlockSpec((tm, tn), lambda i, j, kk: (i, j))
    else:
        tm, tn, tk = _pick(m, tm), _pick(n, tn), _pick(k, tk)
        grid = (m // tm, n // tn, k // tk)
        a_blk = ((tk, tm), lambda i, j, kk: (kk, i)) if ta else ((tm, tk), lambda i, j, kk: (i, kk))
        b_blk = ((tn, tk), lambda i, j, kk: (j, kk)) if tb else ((tk, tn), lambda i, j, kk: (kk, j))
        mn_spec = pl.BlockSpec((tm, tn), lambda i, j, kk: (i, j))

    def operand_spec(blk, plane):
        shape, imap = blk
        if plane is None:
            return pl.BlockSpec(shape, imap)
        return pl.BlockSpec((None,) + shape, lambda i, j, kk: (plane,) + imap(i, j, kk))

    ab_specs = [operand_spec(a_blk if q % 2 == 0 else b_blk, p) for q, p in enumerate(planes)]
    nk = grid[2]
    row_spec = pl.BlockSpec((1, tn), lambda i, j, kk: (0, j))
    n_ex, n_out = len(mn) + len(rows), len(outs)
    dims = (((0 if ta else 1,), (1 if tb else 0,)), ((), ()))

    def body(*refs):
        ab_refs, rest = refs[:len(ab)], refs[len(ab):]
        ex, o_refs, acc = rest[:n_ex], rest[n_ex:n_ex + n_out], rest[-1]
        kk = pl.program_id(2)

        @pl.when(kk == 0)
        def _():
            acc[...] = jnp.zeros_like(acc)

        for a_ref, b_ref in zip(ab_refs[0::2], ab_refs[1::2]):
            acc[...] += lax.dot_general(a_ref[...].astype(_MXU), b_ref[...].astype(_MXU), dims,
                                        preferred_element_type=F32)

        @pl.when(kk == nk - 1)
        def _():
            vals = (acc[...],) if epi is None else epi(acc[...], *[r[...] for r in ex])
            for r, v in zip(o_refs, vals):
                r[...] = v.astype(r.dtype)

    res = pl.pallas_call(
        body, grid=grid,
        in_specs=ab_specs + [mn_spec] * len(mn) + [row_spec] * len(rows),
        out_specs=[mn_spec] * n_out,
        out_shape=[jax.ShapeDtypeStruct((m, out_n), dt) for dt in outs],
        scratch_shapes=[pltpu.VMEM((tm, tn), F32)],
        compiler_params=_params(("parallel", "parallel", "arbitrary")), name=name,
    )(*ab, *mn, *rows)
    return res[0] if n_out == 1 else res


def _ew(fn, rows, bcs, out_rows, out_accs, *, name, tm=256):
    r = rows[0].shape[0]
    tm = min(tm, r)
    assert r % tm == 0
    nr, nb, no, na = len(rows), len(bcs), len(out_rows), len(out_accs)

    def body(*refs):
        i = pl.program_id(0)
        r_in, b_in = refs[:nr], refs[nr:nr + nb]
        o_r, o_a = refs[nr + nb:nr + nb + no], refs[nr + nb + no:]
        outs, accs = fn([x[...] for x in r_in], [x[...] for x in b_in])
        for ref, v in zip(o_r, outs):
            ref[...] = v.astype(ref.dtype)
        if na:
            @pl.when(i == 0)
            def _():
                for ref in o_a:
                    ref[...] = jnp.zeros_like(ref)

            for ref, v in zip(o_a, accs):
                ref[...] += v

    res = pl.pallas_call(
        body, grid=(r // tm,),
        in_specs=[pl.BlockSpec((tm, x.shape[1]), lambda i: (i, 0)) for x in rows]
        + [pl.BlockSpec((1, x.shape[1]), lambda i: (0, 0)) for x in bcs],
        out_specs=[pl.BlockSpec((tm, c), lambda i: (i, 0)) for c, _ in out_rows]
        + [pl.BlockSpec((1, c), lambda i: (0, 0)) for c in out_accs],
        out_shape=[jax.ShapeDtypeStruct((r, c), dt) for c, dt in out_rows]
        + [jax.ShapeDtypeStruct((1, c), F32) for c in out_accs],
        compiler_params=_params(("arbitrary",)), name=name,
    )(*rows, *bcs)
    return res


def _colsum(x):
    return jnp.sum(x, axis=0, keepdims=True)


def _sigmoid(x):
    return 1.0 / (1.0 + jnp.exp(-x))


def _rms_fwd(x, g, name):
    def fn(r, b):
        xv = r[0]
        rs = lax.rsqrt(jnp.mean(xv * xv, axis=-1, keepdims=True) + RMS_EPS)
        return [xv * rs * b[0]], []
    return _ew(fn, [x], [g], [(x.shape[1], BF16)], [], name=name)[0]


def _rms_bwd(x, dn, res, g, name):
    def fn(r, b):
        xv, dv = r[0], r[1]
        rs = lax.rsqrt(jnp.mean(xv * xv, axis=-1, keepdims=True) + RMS_EPS)
        gd = dv * b[0]
        dx = rs * gd - xv * (rs * rs * rs) * jnp.mean(gd * xv, axis=-1, keepdims=True)
        if res is not None:
            dx = dx + r[2]
        return [dx], [_colsum(dv * xv * rs)]
    rows = [x, dn] + ([res] if res is not None else [])
    return _ew(fn, rows, [g], [(x.shape[1], F32)], [x.shape[1]], name=name)


def _scan_order(x):
    l, c = x.shape
    return x.reshape(l // (SUBLANES * SCAN_SEG), SUBLANES, SCAN_SEG, c).transpose(0, 2, 1, 3).reshape(l, c)


def _time_order(x):
    l, c = x.shape
    return x.reshape(l // (SUBLANES * SCAN_SEG), SCAN_SEG, SUBLANES, c).transpose(0, 2, 1, 3).reshape(l, c)


def _ssm_scan(x, w_re, w_im, a_pair, *, reverse, s_fwd=None, u=None, name):
    l = x.shape[0]
    seg, w = SCAN_SEG, SCAN_W
    bd_w = SSM_W // SSM_BD
    tiles_per_bd = SSM_S // SSM_BD // w
    nch = min(SCAN_CHAINS, l // (SUBLANES * seg))
    chain_rows = SUBLANES * seg
    tb = nch * chain_rows
    nt = l // tb
    with_da = s_fwd is not None
    assert reverse or not with_da

    def tt(t):
        return nt - 1 - t if reverse else t

    def body(*refs):
        if with_da:
            (x_ref, wr_ref, wi_ref, a_ref, sf_ref, sp_ref, u_ref, s_ref, da_ref, dw_ref, dx_ref,
             p_ref, c_ref, b_scr) = refs
        else:
            x_ref, wr_ref, wi_ref, a_ref, s_ref, p_ref, c_ref, b_scr = refs
        t_blk = pl.program_id(1)
        ar, ai = a_ref[0], a_ref[1]

        @pl.when(t_blk == 0)
        def _():
            def pstep(i, carry):
                pr, pi = carry
                p_ref[0, pl.ds(i, 1), :] = pr
                p_ref[1, pl.ds(i, 1), :] = pi
                return pr * ar - pi * ai, pr * ai + pi * ar

            lax.fori_loop(0, seg, pstep, (ar, ai))
            c_ref[...] = jnp.zeros_like(c_ref)
            if with_da:
                da_ref[...] = jnp.zeros_like(da_ref)
                dw_ref[...] = jnp.zeros_like(dw_ref)
                dx_ref[...] = jnp.zeros_like(dx_ref)

        xb = x_ref[...].astype(_MXU)
        b_scr[:, :w] = jnp.dot(xb, wr_ref[...], preferred_element_type=F32)
        b_scr[:, w:] = jnp.dot(xb, wi_ref[...], preferred_element_type=F32)
        arb, aib = jnp.broadcast_to(ar, (SUBLANES, w)), jnp.broadcast_to(ai, (SUBLANES, w))
        zero = jnp.zeros((SUBLANES, w), F32)

        def tile(g, step):
            return pl.ds(pl.multiple_of(g * chain_rows + step * SUBLANES, SUBLANES), SUBLANES)

        def rows(g, i):
            return tile(g, seg - 1 - i if reverse else i)

        def local_step(i, carry):
            out = []
            for g in range(nch):
                sr, si = carry[2 * g], carry[2 * g + 1]
                idx = rows(g, i)
                sr, si = arb * sr - aib * si + b_scr[idx, :w], arb * si + aib * sr + b_scr[idx, w:]
                b_scr[idx, :w] = sr
                b_scr[idx, w:] = si
                out += [sr, si]
            return tuple(out)

        ends = lax.fori_loop(0, seg, local_step, (zero,) * (2 * nch), unroll=2)

        a_seg_r, a_seg_i = p_ref[0, seg - 1:seg, :], p_ref[1, seg - 1:seg, :]
        cr, ci = c_ref[0], c_ref[1]
        sub = lax.broadcasted_iota(jnp.int32, (SUBLANES, w), 0)
        ins = [[zero, zero] for _ in range(nch)]
        order = [(g, k) for g in range(nch) for k in range(SUBLANES)]
        for g, k in (order[::-1] if reverse else order):
            ins[g] = [jnp.where(sub == k, cr, ins[g][0]), jnp.where(sub == k, ci, ins[g][1])]
            er, ei = ends[2 * g][k:k + 1], ends[2 * g + 1][k:k + 1]
            cr, ci = er + a_seg_r * cr - a_seg_i * ci, ei + a_seg_r * ci + a_seg_i * cr
        c_ref[0] = cr
        c_ref[1] = ci

        def fix(g, i):
            idx = rows(g, i)
            pr, pi = p_ref[0, pl.ds(i, 1), :], p_ref[1, pl.ds(i, 1), :]
            sr = b_scr[idx, :w] + pr * ins[g][0] - pi * ins[g][1]
            si = b_scr[idx, w:] + pr * ins[g][1] + pi * ins[g][0]
            s_ref.at[0][idx, :] = sr.astype(s_ref.dtype)
            s_ref.at[1][idx, :] = si.astype(s_ref.dtype)
            return sr, si

        if not with_da:
            def fix_step(i, carry):
                for g in range(nch):
                    fix(g, i)
                return carry

            lax.fori_loop(0, seg, fix_step, 0, unroll=2)
        else:
            def adj_step(i, acc):
                acc_r, acc_i = acc
                for g in range(nch):
                    lr, li = fix(g, i)
                    prev = tile(g, seg - 2 - i)
                    fr, fi = sf_ref.at[0][prev, :].astype(F32), sf_ref.at[1][prev, :].astype(F32)
                    acc_r, acc_i = acc_r + lr * fr + li * fi, acc_i + li * fr - lr * fi
                return acc_r, acc_i

            acc_r, acc_i = lax.fori_loop(0, seg - 1, adj_step, (zero, zero), unroll=2)
            first_block = tt(t_blk) == 0
            for g in range(nch):
                lr, li = fix(g, seg - 1)
                seg_ends = tile(g, seg - 1)
                if g == 0:
                    pvr = jnp.where(first_block, 0.0, sp_ref[0, SUBLANES - 1:SUBLANES, :].astype(F32))
                    pvi = jnp.where(first_block, 0.0, sp_ref[1, SUBLANES - 1:SUBLANES, :].astype(F32))
                else:
                    pvr = sf_ref[0, g * chain_rows - 1:g * chain_rows, :].astype(F32)
                    pvi = sf_ref[1, g * chain_rows - 1:g * chain_rows, :].astype(F32)
                fr = jnp.where(sub == 0, pvr, pltpu.roll(sf_ref.at[0][seg_ends, :].astype(F32), 1, 0))
                fi = jnp.where(sub == 0, pvi, pltpu.roll(sf_ref.at[1][seg_ends, :].astype(F32), 1, 0))
                acc_r = acc_r + lr * fr + li * fi
                acc_i = acc_i + li * fr - lr * fi
            da_ref[0] += jnp.sum(acc_r, axis=0, keepdims=True)
            da_ref[1] += jnp.sum(acc_i, axis=0, keepdims=True)
            for plane in range(2):
                dw_ref[plane] += _tn_dot(u_ref[...], s_ref[plane])
                dx_ref[plane] += _tn_dot(xb, sf_ref[plane])

    x_spec = pl.BlockSpec((tb, bd_w), lambda j, t: (tt(t), j // tiles_per_bd))
    w_spec = pl.BlockSpec((bd_w, w), lambda j, t: (j // tiles_per_bd, j))
    d_spec = pl.BlockSpec((2, bd_w, w), lambda j, t: (0, j // tiles_per_bd, j % tiles_per_bd))
    a_spec = pl.BlockSpec((2, 1, w), lambda j, t: (0, 0, j))
    s_spec = pl.BlockSpec((2, tb, w), lambda j, t: (0, tt(t), j))
    in_specs, args = [x_spec, w_spec, w_spec, a_spec], [x, w_re, w_im, a_pair]
    out_specs, out_shape = [s_spec], [jax.ShapeDtypeStruct((2, l, SSM_S), BF16)]
    scratch = [pltpu.VMEM((2, seg, w), F32), pltpu.VMEM((2, 1, w), F32), pltpu.VMEM((tb, 2 * w), F32)]
    if with_da:
        in_specs += [s_spec, pl.BlockSpec((2, SUBLANES, w),
                                          lambda j, t: (0, jnp.maximum(tt(t) * (tb // SUBLANES) - 1, 0), j)),
                     x_spec]
        args += [s_fwd, s_fwd, u]
        out_specs += [a_spec, d_spec, d_spec]
        out_shape += ([jax.ShapeDtypeStruct((2, 1, SSM_S), F32)]
                      + [jax.ShapeDtypeStruct((2, SSM_W, SSM_S // SSM_BD), F32)] * 2)
    res = pl.pallas_call(
        body, grid=(SSM_S // w, nt), in_specs=in_specs, out_specs=out_specs, out_shape=out_shape,
        scratch_shapes=scratch, compiler_params=_params(("parallel", "arbitrary")), name=name,
    )(*args)
    return res if with_da else res[0]


def _nt_dot(x, y):
    return lax.dot_general(x.astype(_MXU), y.astype(_MXU), (((1,), (1,)), ((), ())), preferred_element_type=F32)


def _tn_dot(x, y):
    return lax.dot_general(x.astype(_MXU), y.astype(_MXU), (((0,), (0,)), ((), ())), preferred_element_type=F32)


def _nn_dot(x, y):
    return jnp.dot(x.astype(_MXU), y.astype(_MXU), preferred_element_type=F32)


def _attn_mask2(gb, nb):
    qi = lax.broadcasted_iota(jnp.int32, (ATT_WIN, 2 * ATT_WIN), 0)
    c = lax.broadcasted_iota(jnp.int32, (ATT_WIN, 2 * ATT_WIN), 1)
    has_prev = (gb % nb) != 0
    prev_ok = jnp.logical_and(jnp.logical_and(c < ATT_WIN, c >= qi), has_prev)
    own_ok = jnp.logical_and(c >= ATT_WIN, c - ATT_WIN <= qi)
    return jnp.logical_or(prev_ok, own_ok)


def _attn_specs():
    cur = pl.BlockSpec((ATT_QB * ATT_WIN, ATT_GW), lambda i: (i, 0))
    prev = pl.BlockSpec((ATT_WIN, ATT_GW), lambda i: (jnp.maximum(ATT_QB * i - 1, 0), 0))
    return cur, prev


def _attn_fwd(q, k, v, nb, name):
    l = q.shape[0]
    scale = ATT_E ** -0.5
    w = ATT_WIN

    def body(q_ref, kc_ref, kp_ref, vc_ref, vp_ref, o_ref, lse_ref):
        i = pl.program_id(0)
        masks = [_attn_mask2(ATT_QB * i + b, nb) for b in range(ATT_QB)]
        for h in range(ATT_HG):
            sl = slice(h * ATT_E, (h + 1) * ATT_E)
            k_ext = jnp.concatenate([kp_ref[:, sl], kc_ref[:, sl]], axis=0)
            v_ext = jnp.concatenate([vp_ref[:, sl], vc_ref[:, sl]], axis=0)
            for b in range(ATT_QB):
                r, kr = slice(b * w, (b + 1) * w), slice(b * w, (b + 2) * w)
                s = jnp.where(masks[b], _nt_dot(q_ref[r, sl], k_ext[kr]) * scale, NEG_INF)
                mx = jnp.max(s, axis=-1, keepdims=True)
                p = jnp.exp(s - mx)
                den = jnp.sum(p, axis=-1, keepdims=True)
                o_ref[r, sl] = _nn_dot(p, v_ext[kr]) / den
                lse_ref[r, sl] = jnp.broadcast_to(mx + jnp.log(den), (w, ATT_E))

    cur, prev = _attn_specs()
    return pl.pallas_call(
        body, grid=(l // (ATT_QB * w),), in_specs=[cur, cur, prev, cur, prev], out_specs=[cur, cur],
        out_shape=[jax.ShapeDtypeStruct((l, ATT_GW), F32)] * 2,
        compiler_params=_params(("parallel",)), name=name,
    )(q, k, k, v, v)


def _attn_bwd(q, k, v, do, lse, dd, nb, name):
    l = q.shape[0]
    scale = ATT_E ** -0.5
    w = ATT_WIN
    nblk = l // w

    def body(q_ref, kc_ref, kp_ref, vc_ref, vp_ref, do_ref, lse_ref, dd_ref, qn_ref, don_ref, lsen_ref, ddn_ref,
             dq_ref, dk_ref, dv_ref, dk_acc, dv_acc):
        i = pl.program_id(0)
        masks = [_attn_mask2(ATT_QB * i + b, nb) for b in range(ATT_QB)]
        nxt = ATT_QB * (i + 1)
        nxt_attends = jnp.logical_and(nxt < nblk, (nxt % nb) != 0)
        qi = lax.broadcasted_iota(jnp.int32, (w, w), 0)
        kj = lax.broadcasted_iota(jnp.int32, (w, w), 1)
        mask_n = jnp.logical_and(kj >= qi, nxt_attends)
        dk_acc[...] = jnp.zeros_like(dk_acc)
        dv_acc[...] = jnp.zeros_like(dv_acc)
        for h in range(ATT_HG):
            sl, col = slice(h * ATT_E, (h + 1) * ATT_E), slice(h * ATT_E, h * ATT_E + 1)
            k_ext = jnp.concatenate([kp_ref[:, sl], kc_ref[:, sl]], axis=0)
            v_ext = jnp.concatenate([vp_ref[:, sl], vc_ref[:, sl]], axis=0)
            for b in range(ATT_QB):
                r, kr = slice(b * w, (b + 1) * w), slice(b * w, (b + 2) * w)
                qh, doh, k2, v2 = q_ref[r, sl], do_ref[r, sl], k_ext[kr], v_ext[kr]
                p = jnp.where(masks[b], jnp.exp(_nt_dot(qh, k2) * scale - lse_ref[r, col]), 0.0)
                ds = p * (_nt_dot(doh, v2) - dd_ref[r, col]) * scale
                dq_ref[r, sl] = _nn_dot(ds, k2).astype(dq_ref.dtype)
                dk2, dv2 = _tn_dot(ds, qh), _tn_dot(p, doh)
                dk_acc[r, sl] += dk2[w:]
                dv_acc[r, sl] += dv2[w:]
                if b > 0:
                    rp = slice((b - 1) * w, b * w)
                    dk_acc[rp, sl] += dk2[:w]
                    dv_acc[rp, sl] += dv2[:w]
            last = slice((ATT_QB - 1) * w, ATT_QB * w)
            kl, vl, qn, don = kc_ref[last, sl], vc_ref[last, sl], qn_ref[:, sl], don_ref[:, sl]
            pn = jnp.where(mask_n, jnp.exp(_nt_dot(qn, kl) * scale - lsen_ref[:, col]), 0.0)
            dsn = pn * (_nt_dot(don, vl) - ddn_ref[:, col]) * scale
            dk_acc[last, sl] += _tn_dot(dsn, qn)
            dv_acc[last, sl] += _tn_dot(pn, don)
        dk_ref[...] = dk_acc[...].astype(dk_ref.dtype)
        dv_ref[...] = dv_acc[...].astype(dv_ref.dtype)

    cur, prev = _attn_specs()
    nxt_spec = pl.BlockSpec((w, ATT_GW), lambda i: (jnp.minimum(ATT_QB * (i + 1), nblk - 1), 0))
    return pl.pallas_call(
        body, grid=(l // (ATT_QB * w),),
        in_specs=[cur, cur, prev, cur, prev, cur, cur, cur, nxt_spec, nxt_spec, nxt_spec, nxt_spec],
        out_specs=[cur] * 3, out_shape=[jax.ShapeDtypeStruct((l, ATT_GW), BF16)] * 3,
        scratch_shapes=[pltpu.VMEM((ATT_QB * w, ATT_GW), F32)] * 2,
        compiler_params=_params(("parallel",)), name=name,
    )(q, k, k, v, v, do, lse, dd, q, do, lse, dd)


def _to_perm(a, d):
    if d == 1:
        return a
    l, c = a.shape
    return a.reshape(l // d, d, c).transpose(1, 0, 2).reshape(l, c)


def _from_perm(a, d):
    if d == 1:
        return a
    l, c = a.shape
    return a.reshape(d, l // d, c).transpose(1, 0, 2).reshape(l, c)


def _mem_probs(qh, kh):
    s = _nt_dot(qh, kh) * (MEM_E ** -0.5)
    e = jnp.exp(s - jnp.max(s, axis=-1, keepdims=True))
    return e / jnp.sum(e, axis=-1, keepdims=True)


def _mem_fwd(mq, kv, name, tm=512):
    l, nm = mq.shape[0], kv.shape[0]

    def body(q_ref, kv_ref, o_ref):
        for h in range(MEM_H):
            sl = slice(h * MEM_E, (h + 1) * MEM_E)
            p = _mem_probs(q_ref[:, sl], kv_ref[:, sl])
            o_ref[:, sl] = _nn_dot(p, kv_ref[:, MEM_W + h * MEM_E:MEM_W + (h + 1) * MEM_E]).astype(o_ref.dtype)

    return pl.pallas_call(
        body, grid=(l // tm,),
        in_specs=[pl.BlockSpec((tm, MEM_W), lambda i: (i, 0)), pl.BlockSpec((nm, 2 * MEM_W), lambda i: (0, 0))],
        out_specs=pl.BlockSpec((tm, MEM_W), lambda i: (i, 0)),
        out_shape=jax.ShapeDtypeStruct((l, MEM_W), BF16),
        compiler_params=_params(("parallel",)), name=name,
    )(mq, kv)


def _mem_bwd(mq, kv, dmo, name, tm=512):
    l, nm = mq.shape[0], kv.shape[0]
    scale = MEM_E ** -0.5

    def body(q_ref, kv_ref, do_ref, dq_ref, dkv_ref):
        @pl.when(pl.program_id(0) == 0)
        def _():
            dkv_ref[...] = jnp.zeros_like(dkv_ref)

        for h in range(MEM_H):
            sl = slice(h * MEM_E, (h + 1) * MEM_E)
            vsl = slice(MEM_W + h * MEM_E, MEM_W + (h + 1) * MEM_E)
            qh, kh, vh, doh = q_ref[:, sl], kv_ref[:, sl], kv_ref[:, vsl], do_ref[:, sl]
            p = _mem_probs(qh, kh)
            dp = _nt_dot(doh, vh)
            ds = p * (dp - jnp.sum(dp * p, axis=-1, keepdims=True)) * scale
            dq_ref[:, sl] = _nn_dot(ds, kh).astype(dq_ref.dtype)
            dkv_ref[:, sl] += _tn_dot(ds, qh)
            dkv_ref[:, vsl] += _tn_dot(p, doh)

    row = pl.BlockSpec((tm, MEM_W), lambda i: (i, 0))
    full = pl.BlockSpec((nm, 2 * MEM_W), lambda i: (0, 0))
    return pl.pallas_call(
        body, grid=(l // tm,), in_specs=[row, full, row], out_specs=[row, full],
        out_shape=[jax.ShapeDtypeStruct((l, MEM_W), BF16), jax.ShapeDtypeStruct((nm, 2 * MEM_W), F32)],
        compiler_params=_params(("arbitrary",)), name=name,
    )(mq, kv, dmo)


def _discretize(lam_re, lam_im, log_dt, b_re, b_im):
    dt = jnp.exp(log_dt)[:, None]
    mag = jnp.exp(lam_re * dt)
    a_re, a_im = mag * jnp.cos(lam_im * dt), mag * jnp.sin(lam_im * dt)
    nr, ni = a_re - 1.0, a_im
    den = lam_re * lam_re + lam_im * lam_im
    coef_re = (nr * lam_re + ni * lam_im) / den
    coef_im = (ni * lam_re - nr * lam_im) / den
    bb_re = coef_re[..., None] * b_re - coef_im[..., None] * b_im
    bb_im = coef_re[..., None] * b_im + coef_im[..., None] * b_re
    return a_re, a_im, bb_re, bb_im


def _bd_in(bb):
    return jnp.einsum("gph,gk->ghkp", bb, jnp.eye(SSM_G, dtype=bb.dtype)).reshape(SSM_W, SSM_S)


def _bd_diag(x):
    gb = SSM_G // SSM_BD
    t = x.reshape(SSM_BD, gb, SSM_H, gb, SSM_P)
    return jnp.einsum("bghgp->bghp", t).reshape(SSM_G, SSM_H, SSM_P)


_ANY = pl.BlockSpec(memory_space=pl.ANY)
_MESH = pl.DeviceIdType.MESH


def _allgather(x, name):
    def body(x_ref, out_ref, send_sems, recv_sems, local_sem):
        mx, my, mc = lax.axis_index("x"), lax.axis_index("y"), lax.axis_index("c")
        me, sibling = (mx, my, mc), (mx, my, 1 - mc)
        chips = [(1 - mx, my), (mx, 1 - my), (1 - mx, 1 - my)]

        def blk(px, py, pc):
            return out_ref.at[4 * px + 2 * py + pc]

        def copy(k, block, to, src=None):
            return pltpu.make_async_remote_copy(
                src_ref=blk(*block) if src is None else src, dst_ref=blk(*block),
                send_sem=send_sems.at[k], recv_sem=recv_sems.at[k], device_id=to, device_id_type=_MESH)

        mine = pltpu.make_async_copy(x_ref, blk(*me), local_sem)
        mine.start()
        first = [copy(0, me, sibling, src=x_ref)]
        first += [copy(1 + j, me, (*chip, mc), src=x_ref) for j, chip in enumerate(chips)]
        for cp in first:
            cp.start()
        passed = [copy(4 + j, (*chip, mc), sibling) for j, chip in enumerate(chips)]
        for j, chip in enumerate(chips):
            copy(1 + j, (*chip, mc), me).wait_recv()
            passed[j].start()
        copy(0, sibling, me).wait_recv()
        for j, chip in enumerate(chips):
            copy(4 + j, (*chip, 1 - mc), me).wait_recv()
        for cp in first + passed:
            cp.wait_send()
        mine.wait()

    return pl.pallas_call(
        body, out_shape=jax.ShapeDtypeStruct((N_DEV,) + x.shape, x.dtype), in_specs=[_ANY], out_specs=_ANY,
        scratch_shapes=[pltpu.SemaphoreType.DMA((7,)), pltpu.SemaphoreType.DMA((7,)), pltpu.SemaphoreType.DMA],
        name=name,
    )(x)


def _pair_exchange(g, name):
    def body(g_ref, out_ref, send_sems, recv_sems):
        mx, my, mc = lax.axis_index("x"), lax.axis_index("y"), lax.axis_index("c")
        copies = [pltpu.make_async_remote_copy(
            src_ref=g_ref.at[2 * k + (1 - mc)], dst_ref=out_ref.at[k], send_sem=send_sems.at[k],
            recv_sem=recv_sems.at[k], device_id=(mx, my, 1 - mc), device_id_type=_MESH) for k in range(4)]
        for cp in copies:
            cp.start()
        for cp in copies:
            cp.wait()

    return pl.pallas_call(
        body, out_shape=jax.ShapeDtypeStruct((4,) + g.shape[1:], g.dtype), in_specs=[_ANY], out_specs=_ANY,
        scratch_shapes=[pltpu.SemaphoreType.DMA((4,)), pltpu.SemaphoreType.DMA((4,))], name=name,
    )(g)


def _chip_exchange(p, name):
    def body(p_ref, out_ref, send_sems, recv_sems):
        mx, my, mc = lax.axis_index("x"), lax.axis_index("y"), lax.axis_index("c")
        chips = [(1 - mx, my), (mx, 1 - my), (1 - mx, 1 - my)]
        copies = [pltpu.make_async_remote_copy(
            src_ref=p_ref.at[2 * px + py], dst_ref=out_ref.at[j], send_sem=send_sems.at[j],
            recv_sem=recv_sems.at[j], device_id=(px, py, mc), device_id_type=_MESH)
            for j, (px, py) in enumerate(chips)]
        for cp in copies:
            cp.start()
        for cp in copies:
            cp.wait()

    return pl.pallas_call(
        body, out_shape=jax.ShapeDtypeStruct((3,) + p.shape[1:], p.dtype), in_specs=[_ANY], out_specs=_ANY,
        scratch_shapes=[pltpu.SemaphoreType.DMA((3,)), pltpu.SemaphoreType.DMA((3,))], name=name,
    )(p)


def _pair_sum(g, t1, my_c, name, tr):
    _, r, c = g.shape

    def body(c_ref, g_ref, t_ref, o_ref, ob_ref):
        s = g_ref[...] + t_ref[...]
        o_ref[...] = s
        ob_ref[...] = s.astype(BF16)

    blk = pl.BlockSpec((None, tr, c), lambda k, i, cr: (k, i, 0))
    return pl.pallas_call(
        body,
        grid_spec=pltpu.PrefetchScalarGridSpec(
            num_scalar_prefetch=1, grid=(4, r // tr),
            in_specs=[pl.BlockSpec((None, tr, c), lambda k, i, cr: (2 * k + cr[0], i, 0)), blk],
            out_specs=[blk, blk]),
        out_shape=[jax.ShapeDtypeStruct((4, r, c), F32), jax.ShapeDtypeStruct((4, r, c), BF16)],
        compiler_params=_params(("parallel", "parallel")), name=name,
    )(my_c, g, t1)


def _adam_math(g, w, m, v):
    m = ADAM_B1 * m + (1.0 - ADAM_B1) * g
    v = ADAM_B2 * v + (1.0 - ADAM_B2) * (g * g)
    m_hat = m / (1.0 - ADAM_B1 ** ADAM_STEP)
    v_hat = v / (1.0 - ADAM_B2 ** ADAM_STEP)
    delta = -ADAM_LR * (m_hat / (jnp.sqrt(v_hat) + ADAM_EPS) + ADAM_WD * w)
    return delta, m, v


def _adam_big(p, t2, my_chip, w, m, v, name, tr):
    r, c = w.shape

    def body(k_ref, p_ref, t0_ref, t1_ref, t2_ref, w_ref, m_ref, v_ref, g_out, d_out, m_out, v_out):
        g = ((p_ref[...] + t0_ref[...].astype(F32)) + t1_ref[...].astype(F32)) + t2_ref[...].astype(F32)
        d, mn, vn = _adam_math(g, w_ref[...], m_ref[...], v_ref[...])
        g_out[...], d_out[...], m_out[...], v_out[...] = g, d, mn, vn

    flat = pl.BlockSpec((tr, c), lambda i, kr: (i, 0))

    def rel(j):
        return pl.BlockSpec((None, tr, c), lambda i, kr: (j, i, 0))

    return pl.pallas_call(
        body,
        grid_spec=pltpu.PrefetchScalarGridSpec(
            num_scalar_prefetch=1, grid=(r // tr,),
            in_specs=[pl.BlockSpec((None, tr, c), lambda i, kr: (kr[0], i, 0)), rel(0), rel(1), rel(2), flat, flat, flat],
            out_specs=[flat] * 4),
        out_shape=[jax.ShapeDtypeStruct((r, c), F32)] * 4,
        compiler_params=_params(("parallel",)), name=name,
    )(my_chip, p, t2, t2, t2, w, m, v)


def _sum8(g8, name):
    _, r, c = g8.shape

    def body(g_ref, o_ref):
        acc = g_ref[0]
        for j in range(1, N_DEV):
            acc = acc + g_ref[j]
        o_ref[...] = acc

    return pl.pallas_call(
        body, grid=(1,), in_specs=[pl.BlockSpec((N_DEV, r, c), lambda i: (0, 0, 0))],
        out_specs=pl.BlockSpec((r, c), lambda i: (0, 0)), out_shape=jax.ShapeDtypeStruct((r, c), F32),
        compiler_params=_params(("arbitrary",)), name=name,
    )(g8)


def _adam_small(g, w, m, v, name):
    def fn(r, b):
        return list(_adam_math(*r)), []
    c = g.shape[1]
    return _ew(fn, [g, w, m, v], [], [(c, F32)] * 3, [], name=name, tm=g.shape[0])


def _pack(arrs, pad_rows=8):
    flat = jnp.concatenate([a.reshape(-1) for a in arrs])
    n = flat.shape[0]
    q = PACK_C * pad_rows
    tot = -(-n // q) * q
    if tot != n:
        flat = jnp.concatenate([flat, jnp.zeros((tot - n,), flat.dtype)])
    return flat.reshape(tot // PACK_C, PACK_C)


def _unpack(buf, shapes):
    flat = buf.reshape(-1)
    out, off = [], 0
    for s in shapes:
        n = int(np.prod(s))
        out.append(flat[off:off + n].reshape(s))
        off += n
    return out


GROUPS = (("w_in", "w_mem_kv", "w_o", "w_up", "w_down"),
          ("w_glu", "w_ssm_br", "w_mem_br", "w_attn_br"))
GROUP_TR = (208, 384)
ATTN_BR_FOLD = 2


def _stored_shape(name):
    r, c, ax = BIG_SHAPE[name]
    rows, cols = (r // N_DEV, c) if ax == 0 else (c // N_DEV, r)
    return (rows // ATTN_BR_FOLD, cols * ATTN_BR_FOLD) if name == "w_attn_br" else (rows, cols)


def _stored(shard, name):
    a = shard[0].T if BIG_SHAPE[name][2] == 1 else shard[0]
    return a.reshape(_stored_shape(name))


def _unstored(a, name):
    r, c, ax = BIG_SHAPE[name]
    if ax == 0:
        return a.reshape(1, r // N_DEV, c)
    return a.reshape(c // N_DEV, r).T[None]


def _pack_group(d, names):
    return jnp.concatenate([_stored(d[n], n) for n in names], axis=0)


def _split_group(buf, names):
    out, off = {}, 0
    for n in names:
        rows = _stored_shape(n)[0]
        out[n] = buf[..., off:off + rows, :]
        off += rows
    return out


def _full_stored(stacked, name):
    r, c, ax = BIG_SHAPE[name]
    return stacked.reshape((r, c) if ax == 0 else (c, r))


def _stacked_stored(full, name):
    return full.reshape((N_DEV,) + _stored_shape(name))


def _gelu_parts(x):
    c0, c1 = math.sqrt(2.0 / math.pi), 0.044715
    th = jnp.tanh(c0 * (x + c1 * x * x * x))
    return th, c0, c1


def _local_step(x, mem, tgt, wb, sp):
    l = x.shape[0]
    w_a, w_g = wb["w_in"][:ZA_W], wb["w_in"][ZA_W:]

    a_re, a_im, bb_re, bb_im = _discretize(sp["ssm_lambda_re"], sp["ssm_lambda_im"], sp["ssm_log_dt"],
                                           sp["ssm_b_re"], sp["ssm_b_im"])
    a_pair = jnp.stack([a_re.reshape(1, SSM_S), a_im.reshape(1, SSM_S)])
    a_conj = jnp.stack([a_re.reshape(1, SSM_S), -a_im.reshape(1, SSM_S)])
    b_re_t, b_im_t = _bd_in(bb_re).astype(BF16), _bd_in(bb_im).astype(BF16)
    c_re_t = _bd_in(sp["ssm_c_re"].transpose(0, 2, 1)).astype(BF16)
    c_im_t = (-_bd_in(sp["ssm_c_im"].transpose(0, 2, 1))).astype(BF16)
    d_row = sp["ssm_d"].reshape(1, SSM_W)

    n1 = _rms_fwd(x, sp["norm1_g"], "rms1")
    za = _mm(n1, w_a, [BF16], tb=True, name="in_proj_a", tn=1664)
    zg = _mm(n1, w_g, [BF16], tb=True, name="in_proj_g")
    u = za[:, :SSM_W]
    mq = za[:, ZA_W - MEM_W:]

    u_s = _scan_order(u)
    s_all = _ssm_scan(u_s, b_re_t, b_im_t, a_pair, reverse=False, name="ssm_scan_fwd")
    ys = _time_order(_mm((s_all, 0), c_re_t, [F32], tb=True, pair2=((s_all, 1), c_im_t), bd=SSM_BD, name="ssm_cs"))

    def gelu_fn(r, b):
        y0 = r[0] + b[0] * r[1].astype(F32)
        th, _, _ = _gelu_parts(y0)
        return [y0, 0.5 * y0 * (1.0 + th)], []
    y0, y1 = _ew(gelu_fn, [ys, u], [d_row], [(SSM_W, F32), (SSM_W, BF16)], [], name="ssm_gelu", tm=512)

    def glu_epi(acc, y1t, bg):
        t = acc + bg
        return t, y1t.astype(F32) * _sigmoid(t)
    t_glu, y2 = _mm(y1, wb["w_glu"], [F32, BF16], epi=glu_epi, mn=[y1], rows=[sp["b_glu"]], name="ssm_glu")
    br_ssm = _mm(y2, wb["w_ssm_br"], [BF16], tb=True, name="ssm_br")

    qkv_p, o_g, lse_g = [], [], []
    for g, d in enumerate(DILATIONS):
        nb = l // d // ATT_WIN
        cols = [za[:, SSM_W + (3 * j + g) * ATT_GW: SSM_W + (3 * j + g + 1) * ATT_GW] for j in range(3)]
        qp, kp, vp = [_to_perm(cc, d) for cc in cols]
        qkv_p.append((qp, kp, vp))
        og, lg = _attn_fwd(qp, kp, vp, nb, "attn_fwd%d" % g)
        o_g.append(_from_perm(og, d))
        lse_g.append(_from_perm(lg, d))

    def merge_fn(r, b):
        o0, o1, o2, l0, l1, l2 = r
        mx = jnp.maximum(jnp.maximum(l0, l1), l2)
        e0, e1, e2 = jnp.exp(l0 - mx), jnp.exp(l1 - mx), jnp.exp(l2 - mx)
        tot = e0 + e1 + e2
        return [(e0 * o0 + e1 * o1 + e2 * o2) / tot, mx + jnp.log(tot)], []
    o_att, lse_tot = _ew(merge_fn, o_g + lse_g, [], [(ATT_GW, F32), (ATT_GW, F32)], [], name="attn_merge", tm=512)
    br_attn = _mm(o_att, wb["w_attn_br"], [BF16], tb=True, name="attn_br")

    mn = _rms_fwd(mem, sp["mem_norm_g"], "rms_mem")
    kv = _mm(mn, wb["w_mem_kv"], [BF16], name="mem_kv")
    mo = _mem_fwd(mq, kv, "mem_attn_fwd")
    br_mem = _mm(mo, wb["w_mem_br"], [BF16], tb=True, name="mem_br")

    def gate_fn(r, b):
        zgt, b0, b1, b2 = [t.astype(F32) for t in r]
        gt = _sigmoid(zgt + b[0])
        return [gt[:, :D_MODEL] * b0 + gt[:, D_MODEL:2 * D_MODEL] * b1 + gt[:, 2 * D_MODEL:] * b2], []
    merged = _ew(gate_fn, [zg, br_ssm, br_attn, br_mem], [sp["b_gate"]], [(D_MODEL, BF16)], [], name="gate_merge")[0]
    h1 = _mm(merged, wb["w_o"], [F32], epi=lambda acc, xt: (acc + xt,), mn=[x], name="o_proj")
    n2 = _rms_fwd(h1, sp["norm2_g"], "rms2")

    def up_epi(acc):
        ra = jnp.maximum(acc, 0.0)
        return ra * ra, ra
    f_act, r_act = _mm(n2, wb["w_up"], [BF16, BF16], tb=True, epi=up_epi, name="mlp_up")
    h2 = _mm(f_act, wb["w_down"], [F32], epi=lambda acc, ht: (acc + ht,), mn=[h1], name="mlp_down")

    def final_fn(r, b):
        hv, tv = r
        gf = b[0]
        rs = lax.rsqrt(jnp.mean(hv * hv, axis=-1, keepdims=True) + RMS_EPS)
        err = hv * rs * gf - tv
        dy = err * (1.0 / D_MODEL)
        gd = dy * gf
        dh = rs * gd - hv * (rs * rs * rs) * jnp.mean(gd * hv, axis=-1, keepdims=True)
        loss = _colsum(jnp.sum(err * err, axis=-1, keepdims=True)) * (0.5 / D_MODEL)
        return [dh], [_colsum(dy * hv * rs), loss]
    dh2, d_final_g, loss = _ew(final_fn, [h2, tgt], [sp["final_g"]], [(D_MODEL, F32)], [D_MODEL, 1], name="final_loss")

    gw, gs = {}, {"final_g": d_final_g}
    d_act = _mm(dh2, wb["w_down"], [BF16], tb=True, epi=lambda acc, ra: (acc * 2.0 * ra.astype(F32),), mn=[r_act],
                name="mlp_down_dx")
    gw["w_down"] = _mm(f_act, dh2, [F32], ta=True, name="mlp_down_dw")
    dn2 = _mm(d_act, wb["w_up"], [F32], name="mlp_up_dx")
    gw["w_up"] = _mm(d_act, n2, [F32], ta=True, name="mlp_up_dw")
    dh1, gs["norm2_g"] = _rms_bwd(h1, dn2, dh2, sp["norm2_g"], "rms2_bwd")
    dmerged = _mm(dh1, wb["w_o"], [F32], tb=True, name="o_proj_dx")
    gw["w_o"] = _mm(merged, dh1, [F32], ta=True, name="o_proj_dw")

    def gate_bwd_fn(r, b):
        dm, zgt, b0, b1, b2 = [t.astype(F32) for t in r]
        gt = _sigmoid(zgt + b[0])
        g0, g1, g2 = gt[:, :D_MODEL], gt[:, D_MODEL:2 * D_MODEL], gt[:, 2 * D_MODEL:]
        dzg = jnp.concatenate([dm * b0 * g0 * (1.0 - g0), dm * b1 * g1 * (1.0 - g1), dm * b2 * g2 * (1.0 - g2)], axis=1)
        return [dm * g0, dm * g1, dm * g2, dzg], [_colsum(dzg)]
    dbr_ssm, dbr_attn, dbr_mem, dzg, gs["b_gate"] = _ew(
        gate_bwd_fn, [dmerged, zg, br_ssm, br_attn, br_mem], [sp["b_gate"]],
        [(D_MODEL, BF16)] * 3 + [(ZG_W, BF16)], [ZG_W], name="gate_bwd")

    gw["w_ssm_br"] = _mm(dbr_ssm, y2, [F32], ta=True, name="ssm_br_dw")
    dy2 = _mm(dbr_ssm, wb["w_ssm_br"], [F32], name="ssm_br_dx")

    def glu_bwd_fn(r, b):
        dy, y1t, tt = r
        sg = _sigmoid(tt)
        dt = dy * y1t.astype(F32) * sg * (1.0 - sg)
        return [dt, dy * sg], [_colsum(dt)]
    dt_glu, dy1a, gs["b_glu"] = _ew(glu_bwd_fn, [dy2, y1, t_glu], [], [(SSM_W, BF16), (SSM_W, F32)], [SSM_W],
                                    name="ssm_glu_bwd", tm=512)
    gw["w_glu"] = _mm(y1, dt_glu, [F32], ta=True, name="ssm_glu_dw")

    def gelu_bwd_epi(acc, dy1t, y0t):
        th, c0, c1 = _gelu_parts(y0t)
        dg = 0.5 * (1.0 + th) + 0.5 * y0t * (1.0 - th * th) * c0 * (1.0 + 3.0 * c1 * y0t * y0t)
        return ((acc + dy1t) * dg,)
    dy0 = _mm(dt_glu, wb["w_glu"], [F32], tb=True, epi=gelu_bwd_epi, mn=[dy1a, y0], name="ssm_glu_dx")
    gs["ssm_d"] = _ew(lambda r, b: ([], [_colsum(r[0] * r[1].astype(F32))]), [dy0, u], [], [], [SSM_W],
                      name="ssm_dd", tm=512)[0]
    dy0_s = _scan_order(dy0)
    lam, da, d_b, d_c = _ssm_scan(dy0_s, c_re_t, c_im_t, a_conj, reverse=True, s_fwd=s_all, u=u_s,
                                  name="ssm_scan_bwd")
    du = _time_order(_mm((lam, 0), b_re_t, [BF16], tb=True, pair2=((lam, 1), b_im_t),
                         epi=lambda acc, dyt, dr: (acc + dyt * dr,), mn=[dy0_s], rows=[d_row], bd=SSM_BD, name="ssm_bu_dx"))
    gs["a_re"], gs["a_im"] = da[0], da[1]
    gs["bb_re"], gs["bb_im"] = _bd_diag(d_b[0]).transpose(0, 2, 1), _bd_diag(d_b[1]).transpose(0, 2, 1)
    gs["ssm_c_re"], gs["ssm_c_im"] = _bd_diag(d_c[0]), -_bd_diag(d_c[1])

    gw["w_attn_br"] = _mm(dbr_attn, o_att, [F32], ta=True, name="attn_br_dw")

    def do_epi(acc, ot):
        prod = acc * ot
        head = lax.broadcasted_iota(jnp.int32, prod.shape, 1) // ATT_E
        dd = jnp.zeros_like(prod)
        for h in range(ATT_HG):
            dd = jnp.where(head == h, jnp.sum(jnp.where(head == h, prod, 0.0), axis=1, keepdims=True), dd)
        return acc, dd
    do_att, dd_att = _mm(dbr_attn, wb["w_attn_br"], [BF16, F32], epi=do_epi, mn=[o_att], name="attn_br_dx")
    dq_l, dk_l, dv_l = [], [], []
    for g, d in enumerate(DILATIONS):
        nb = l // d // ATT_WIN
        qp, kp, vp = qkv_p[g]
        dq, dk, dv = _attn_bwd(qp, kp, vp, _to_perm(do_att, d), _to_perm(lse_tot, d), _to_perm(dd_att, d),
                               nb, "attn_bwd%d" % g)
        dq_l.append(_from_perm(dq, d))
        dk_l.append(_from_perm(dk, d))
        dv_l.append(_from_perm(dv, d))

    gw["w_mem_br"] = _mm(dbr_mem, mo, [F32], ta=True, name="mem_br_dw")
    dmo = _mm(dbr_mem, wb["w_mem_br"], [BF16], name="mem_br_dx")
    dmq, dkv = _mem_bwd(mq, kv, dmo, "mem_attn_bwd")
    gw["w_mem_kv"] = _mm(mn, dkv, [F32], ta=True, name="mem_kv_dw")
    dmn = _mm(dkv, wb["w_mem_kv"], [F32], tb=True, name="mem_kv_dx")
    gs["mem_norm_g"] = _rms_bwd(mem, dmn, None, sp["mem_norm_g"], "rms_mem_bwd")[1]

    dza = jnp.concatenate([du] + dq_l + dk_l + dv_l + [dmq], axis=1)
    dw_a = _mm(dza, n1, [F32], ta=True, name="in_proj_a_dw", tm=1664)
    dw_g = _mm(dzg, n1, [F32], ta=True, name="in_proj_g_dw")
    gw["w_in"] = jnp.concatenate([dw_a, dw_g], axis=0)
    dn_a = _mm(dza, w_a, [F32], name="in_proj_a_dx", tk=1664)
    dn1 = _mm(dzg, w_g, [F32], epi=lambda acc, pt: (acc + pt,), mn=[dn_a], name="in_proj_g_dx")
    grad_x, gs["norm1_g"] = _rms_bwd(x, dn1, dh1, sp["norm1_g"], "rms1_bwd")
    return loss, grad_x, gw, gs


_SMALL_GRAD_ORDER = ("norm1_g", "mem_norm_g", "b_gate", "a_re", "a_im", "bb_re", "bb_im", "ssm_c_re", "ssm_c_im",
                     "ssm_d", "b_glu", "norm2_g", "final_g")


def kernel(x, mem, norm1_g, mem_norm_g, w_in, b_gate, ssm_lambda_re, ssm_lambda_im, ssm_log_dt, ssm_b_re, ssm_b_im, ssm_c_re, ssm_c_im, ssm_d, w_glu, b_glu, w_ssm_br, w_attn_br, w_mem_kv, w_mem_br, w_o, norm2_g, w_up, w_down, final_g, loss_target, m_norm1_g, m_mem_norm_g, m_w_in, m_b_gate, m_ssm_lambda_re, m_ssm_lambda_im, m_ssm_log_dt, m_ssm_b_re, m_ssm_b_im, m_ssm_c_re, m_ssm_c_im, m_ssm_d, m_w_glu, m_b_glu, m_w_ssm_br, m_w_attn_br, m_w_mem_kv, m_w_mem_br, m_w_o, m_norm2_g, m_w_up, m_w_down, m_final_g, v_norm1_g, v_mem_norm_g, v_w_in, v_b_gate, v_ssm_lambda_re, v_ssm_lambda_im, v_ssm_log_dt, v_ssm_b_re, v_ssm_b_im, v_ssm_c_re, v_ssm_c_im, v_ssm_d, v_w_glu, v_b_glu, v_w_ssm_br, v_w_attn_br, v_w_mem_kv, v_w_mem_br, v_w_o, v_norm2_g, v_w_up, v_w_down, v_final_g):
    args = dict(locals())
    w = {n: args[n] for n in ALL_W}
    m = {n: args["m_" + n] for n in ALL_W}
    v = {n: args["v_" + n] for n in ALL_W}
    my_c = lax.axis_index("c").astype(jnp.int32).reshape(1)
    my_chip = (2 * lax.axis_index("x") + lax.axis_index("y")).astype(jnp.int32).reshape(1)

    w_pack = [_pack_group(w, names) for names in GROUPS]
    wb = {}
    for gi, names in enumerate(GROUPS):
        w_all = _allgather(w_pack[gi].astype(BF16), "allgather_weights%d" % gi)
        for n, part in _split_group(w_all, names).items():
            wb[n] = _full_stored(part, n)

    sp = {
        "norm1_g": norm1_g, "mem_norm_g": mem_norm_g, "b_gate": b_gate, "b_glu": b_glu, "norm2_g": norm2_g,
        "final_g": final_g.reshape(1, D_MODEL),
        "ssm_lambda_re": ssm_lambda_re[0], "ssm_lambda_im": ssm_lambda_im[0], "ssm_log_dt": ssm_log_dt[0],
        "ssm_b_re": ssm_b_re[0], "ssm_b_im": ssm_b_im[0], "ssm_c_re": ssm_c_re[0], "ssm_c_im": ssm_c_im[0],
        "ssm_d": ssm_d[0],
    }
    loss, grad_x, gw, gs = _local_step(x[0], mem[0], loss_target[0], wb, sp)
    loss = lax.psum(loss[0, 0], ("x", "y", "c"))

    big = [{}, {}, {}, {}]
    for gi, names in enumerate(GROUPS):
        g_pack = jnp.concatenate([_stacked_stored(gw[n], n) for n in names], axis=1)
        t1 = _pair_exchange(g_pack, "grad_pair_exchange%d" % gi)
        p_sum, p_bf = _pair_sum(g_pack, t1, my_c, "grad_pair_sum%d" % gi, GROUP_TR[gi])
        t2 = _chip_exchange(p_bf, "grad_chip_exchange%d" % gi)
        outs = _adam_big(p_sum, t2, my_chip, w_pack[gi], _pack_group(m, names), _pack_group(v, names),
                         "adam_big%d" % gi, GROUP_TR[gi])
        for kind, buf in enumerate(outs):
            for n, part in _split_group(buf, names).items():
                big[kind][n] = _unstored(part, n)

    sg_shapes = [gs[n].shape for n in _SMALL_GRAD_ORDER]
    sg_all = _allgather(_pack([gs[n] for n in _SMALL_GRAD_ORDER]), "allgather_small_grads")
    sg = dict(zip(_SMALL_GRAD_ORDER, _unpack(_sum8(sg_all, "sum_small_grads"), sg_shapes)))
    _, disc_vjp = jax.vjp(_discretize, sp["ssm_lambda_re"], sp["ssm_lambda_im"], sp["ssm_log_dt"],
                          sp["ssm_b_re"], sp["ssm_b_im"])
    d_lre, d_lim, d_ldt, d_bre, d_bim = disc_vjp((sg["a_re"].reshape(SSM_G, SSM_P), sg["a_im"].reshape(SSM_G, SSM_P),
                                                  sg["bb_re"], sg["bb_im"]))
    small_grad = {
        "norm1_g": sg["norm1_g"], "mem_norm_g": sg["mem_norm_g"], "b_gate": sg["b_gate"],
        "ssm_lambda_re": d_lre, "ssm_lambda_im": d_lim, "ssm_log_dt": d_ldt, "ssm_b_re": d_bre, "ssm_b_im": d_bim,
        "ssm_c_re": sg["ssm_c_re"], "ssm_c_im": sg["ssm_c_im"], "ssm_d": sg["ssm_d"], "b_glu": sg["b_glu"],
        "norm2_g": sg["norm2_g"], "final_g": sg["final_g"],
    }
    small_grad = {n: small_grad[n].reshape(w[n].shape) for n in SMALL}
    s_shapes = [w[n].shape for n in SMALL]
    small_out = _adam_small(_pack([small_grad[n] for n in SMALL]), _pack([w[n] for n in SMALL]),
                            _pack([m[n] for n in SMALL]), _pack([v[n] for n in SMALL]), "adam_small")
    small = [small_grad] + [dict(zip(SMALL, _unpack(b, s_shapes))) for b in small_out]

    outs = [loss, grad_x[None]]
    for kind in range(4):
        for n in ALL_W:
            outs.append(big[kind][n] if n in BIG else small[kind][n])
    return tuple(outs)
```

```python
import math

import numpy as np
import jax
import jax.numpy as jnp
from jax import lax
from jax.experimental import pallas as pl
from jax.experimental.pallas import tpu as pltpu

F32 = jnp.float32
BF16 = jnp.bfloat16
_MXU = jnp.bfloat16

D_MODEL = 1024
SSM_G, SSM_H, SSM_P = 32, 16, 64
SSM_W = SSM_G * SSM_H
SSM_S = SSM_G * SSM_P
SSM_BD = 4
ATT_E = 64
ATT_HG = 4
ATT_GW = ATT_HG * ATT_E
ATT_WIN = 128
ATT_QB = 4
DILATIONS = (1, 4, 16)
MEM_H, MEM_E = 4, 128
MEM_W = MEM_H * MEM_E
ZA_W = SSM_W + 9 * ATT_GW + MEM_W
ZG_W = 3 * D_MODEL
IN_W = ZA_W + ZG_W
RMS_EPS = 1e-6
NEG_INF = -1e30

ADAM_LR, ADAM_B1, ADAM_B2, ADAM_EPS, ADAM_WD, ADAM_STEP = 0.001, 0.9, 0.999, 1e-08, 0.01, 10

N_DEV = 8
PACK_C = 512
_VMEM_LIMIT = 56 * 1024 * 1024
SUBLANES = 16
SCAN_SEG = 128
SCAN_CHAINS = 2
SCAN_W = 128

BIG = ("w_in", "w_glu", "w_ssm_br", "w_attn_br", "w_mem_kv", "w_mem_br", "w_o", "w_up", "w_down")
BIG_SHAPE = {
    "w_in": (D_MODEL, IN_W, 1), "w_glu": (SSM_W, SSM_W, 0), "w_ssm_br": (SSM_W, D_MODEL, 1),
    "w_attn_br": (ATT_GW, D_MODEL, 1), "w_mem_kv": (D_MODEL, 2 * MEM_W, 0), "w_mem_br": (MEM_W, D_MODEL, 1),
    "w_o": (D_MODEL, D_MODEL, 0), "w_up": (D_MODEL, 4 * D_MODEL, 1), "w_down": (4 * D_MODEL, D_MODEL, 0),
}
SMALL = ("norm1_g", "mem_norm_g", "b_gate", "ssm_lambda_re", "ssm_lambda_im", "ssm_log_dt", "ssm_b_re",
         "ssm_b_im", "ssm_c_re", "ssm_c_im", "ssm_d", "b_glu", "norm2_g", "final_g")
ALL_W = ("norm1_g", "mem_norm_g", "w_in", "b_gate", "ssm_lambda_re", "ssm_lambda_im", "ssm_log_dt", "ssm_b_re",
         "ssm_b_im", "ssm_c_re", "ssm_c_im", "ssm_d", "w_glu", "b_glu", "w_ssm_br", "w_attn_br", "w_mem_kv",
         "w_mem_br", "w_o", "norm2_g", "w_up", "w_down", "final_g")


def _params(sem):
    return pltpu.CompilerParams(dimension_semantics=sem, vmem_limit_bytes=_VMEM_LIMIT)


def _pick(n, cap):
    if n <= cap:
        return n
    t = (cap // 128) * 128
    while n % t:
        t -= 128
    return t


def _mm(a, b, outs, *, name, ta=False, tb=False, epi=None, mn=(), rows=(), pair2=None, bd=0,
        tm=1024, tn=1024, tk=2048):
    ab = [a, b] + (list(pair2) if pair2 is not None else [])
    planes = [op[1] if isinstance(op, tuple) else None for op in ab]
    ab = [op[0] if isinstance(op, tuple) else op for op in ab]
    a_shape, b_shape = ab[0].shape[-2:], ab[1].shape[-2:]
    m = a_shape[1] if ta else a_shape[0]
    k = a_shape[0] if ta else a_shape[1]
    n = b_shape[0] if tb else b_shape[1]
    assert k == (b_shape[1] if tb else b_shape[0]), (name, a_shape, b_shape)
    out_n = n
    if bd and ta:
        assert not tb
        tm, tn, tk = m // bd, n // bd, _pick(k, tk)
        grid, out_n = (bd, 1, k // tk), tn
        a_blk = ((tk, tm), lambda i, j, kk: (kk, i))
        b_blk = ((tk, tn), lambda i, j, kk: (kk, i))
        mn_spec = pl.BlockSpec((tm, tn), lambda i, j, kk: (i, 0))
    elif bd:
        tm, tn, tk = _pick(m, tm), n // bd, k // bd
        grid = (m // tm, bd, 1)
        a_blk = ((tm, tk), lambda i, j, kk: (i, j))
        b_blk = ((tn, tk) if tb else (tk, tn), lambda i, j, kk: (j, j))
        mn_spec = pl.BlockSpec((tm, tn), lambda i, j, kk: (i, j))
    else:
        tm, tn, tk = _pick(m, tm), _pick(n, tn), _pick(k, tk)
        grid = (m // tm, n // tn, k // tk)
        a_blk = ((tk, tm), lambda i, j, kk: (kk, i)) if ta else ((tm, tk), lambda i, j, kk: (i, kk))
        b_blk = ((tn, tk), lambda i, j, kk: (j, kk)) if tb else ((tk, tn), lambda i, j, kk: (kk, j))
        mn_spec = pl.BlockSpec((tm, tn), lambda i, j, kk: (i, j))

    def operand_spec(blk, plane):
        shape, imap = blk
        if plane is None:
            return pl.BlockSpec(shape, imap)
        return pl.BlockSpec((None,) + shape, lambda i, j, kk: (plane,) + imap(i, j, kk))

    ab_specs = [operand_spec(a_blk if q % 2 == 0 else b_blk, p) for q, p in enumerate(planes)]
    nk = grid[2]
    row_spec = pl.BlockSpec((1, tn), lambda i, j, kk: (0, j))
    n_ex, n_out = len(mn) + len(rows), len(outs)
    dims = (((0 if ta else 1,), (1 if tb else 0,)), ((), ()))

    def body(*refs):
        ab_refs, rest = refs[:len(ab)], refs[len(ab):]
        ex, o_refs, acc = rest[:n_ex], rest[n_ex:n_ex + n_out], rest[-1]
        kk = pl.program_id(2)

        @pl.when(kk == 0)
        def _():
            acc[...] = jnp.zeros_like(acc)

        for a_ref, b_ref in zip(ab_refs[0::2], ab_refs[1::2]):
            acc[...] += lax.dot_general(a_ref[...].astype(_MXU), b_ref[...].astype(_MXU), dims,
                                        preferred_element_type=F32)

        @pl.when(kk == nk - 1)
        def _():
            vals = (acc[...],) if epi is None else epi(acc[...], *[r[...] for r in ex])
            for r, v in zip(o_refs, vals):
                r[...] = v.astype(r.dtype)

    res = pl.pallas_call(
        body, grid=grid,
        in_specs=ab_specs + [mn_spec] * len(mn) + [row_spec] * len(rows),
        out_specs=[mn_spec] * n_out,
        out_shape=[jax.ShapeDtypeStruct((m, out_n), dt) for dt in outs],
        scratch_shapes=[pltpu.VMEM((tm, tn), F32)],
        compiler_params=_params(("parallel", "parallel", "arbitrary")), name=name,
    )(*ab, *mn, *rows)
    return res[0] if n_out == 1 else res


def _ew(fn, rows, bcs, out_rows, out_accs, *, name, tm=256):
    r = rows[0].shape[0]
    tm = min(tm, r)
    assert r % tm == 0
    nr, nb, no, na = len(rows), len(bcs), len(out_rows), len(out_accs)

    def body(*refs):
        i = pl.program_id(0)
        r_in, b_in = refs[:nr], refs[nr:nr + nb]
        o_r, o_a = refs[nr + nb:nr + nb + no], refs[nr + nb + no:]
        outs, accs = fn([x[...] for x in r_in], [x[...] for x in b_in])
        for ref, v in zip(o_r, outs):
            ref[...] = v.astype(ref.dtype)
        if na:
            @pl.when(i == 0)
            def _():
                for ref in o_a:
                    ref[...] = jnp.zeros_like(ref)

            for ref, v in zip(o_a, accs):
                ref[...] += v

    res = pl.pallas_call(
        body, grid=(r // tm,),
        in_specs=[pl.BlockSpec((tm, x.shape[1]), lambda i: (i, 0)) for x in rows]
        + [pl.BlockSpec((1, x.shape[1]), lambda i: (0, 0)) for x in bcs],
        out_specs=[pl.BlockSpec((tm, c), lambda i: (i, 0)) for c, _ in out_rows]
        + [pl.BlockSpec((1, c), lambda i: (0, 0)) for c in out_accs],
        out_shape=[jax.ShapeDtypeStruct((r, c), dt) for c, dt in out_rows]
        + [jax.ShapeDtypeStruct((1, c), F32) for c in out_accs],
        compiler_params=_params(("arbitrary",)), name=name,
    )(*rows, *bcs)
    return res


def _colsum(x):
    return jnp.sum(x, axis=0, keepdims=True)


def _sigmoid(x):
    return 1.0 / (1.0 + jnp.exp(-x))


def _rms_fwd(x, g, name):
    def fn(r, b):
        xv = r[0]
        rs = lax.rsqrt(jnp.mean(xv * xv, axis=-1, keepdims=True) + RMS_EPS)
        return [xv * rs * b[0]], []
    return _ew(fn, [x], [g], [(x.shape[1], BF16)], [], name=name)[0]


def _rms_bwd(x, dn, res, g, name):
    def fn(r, b):
        xv, dv = r[0], r[1]
        rs = lax.rsqrt(jnp.mean(xv * xv, axis=-1, keepdims=True) + RMS_EPS)
        gd = dv * b[0]
        dx = rs * gd - xv * (rs * rs * rs) * jnp.mean(gd * xv, axis=-1, keepdims=True)
        if res is not None:
            dx = dx + r[2]
        return [dx], [_colsum(dv * xv * rs)]
    rows = [x, dn] + ([res] if res is not None else [])
    return _ew(fn, rows, [g], [(x.shape[1], F32)], [x.shape[1]], name=name)


def _scan_order(x):
    l, c = x.shape
    return x.reshape(l // (SUBLANES * SCAN_SEG), SUBLANES, SCAN_SEG, c).transpose(0, 2, 1, 3).reshape(l, c)


def _time_order(x):
    l, c = x.shape
    return x.reshape(l // (SUBLANES * SCAN_SEG), SCAN_SEG, SUBLANES, c).transpose(0, 2, 1, 3).reshape(l, c)


def _ssm_scan(x, w_re, w_im, a_pair, *, reverse, s_fwd=None, u=None, name):
    l = x.shape[0]
    seg, w = SCAN_SEG, SCAN_W
    bd_w = SSM_W // SSM_BD
    tiles_per_bd = SSM_S // SSM_BD // w
    nch = min(SCAN_CHAINS, l // (SUBLANES * seg))
    chain_rows = SUBLANES * seg
    tb = nch * chain_rows
    nt = l // tb
    with_da = s_fwd is not None
    assert reverse or not with_da

    def tt(t):
        return nt - 1 - t if reverse else t

    def body(*refs):
        if with_da:
            (x_ref, wr_ref, wi_ref, a_ref, sf_ref, sp_ref, u_ref, s_ref, da_ref, dw_ref, dx_ref,
             p_ref, c_ref, b_scr) = refs
        else:
            x_ref, wr_ref, wi_ref, a_ref, s_ref, p_ref, c_ref, b_scr = refs
        t_blk = pl.program_id(1)
        ar, ai = a_ref[0], a_ref[1]

        @pl.when(t_blk == 0)
        def _():
            def pstep(i, carry):
                pr, pi = carry
                p_ref[0, pl.ds(i, 1), :] = pr
                p_ref[1, pl.ds(i, 1), :] = pi
                return pr * ar - pi * ai, pr * ai + pi * ar

            lax.fori_loop(0, seg, pstep, (ar, ai))
            c_ref[...] = jnp.zeros_like(c_ref)
            if with_da:
                da_ref[...] = jnp.zeros_like(da_ref)
                dw_ref[...] = jnp.zeros_like(dw_ref)
                dx_ref[...] = jnp.zeros_like(dx_ref)

        xb = x_ref[...].astype(_MXU)
        b_scr[:, :w] = jnp.dot(xb, wr_ref[...], preferred_element_type=F32)
        b_scr[:, w:] = jnp.dot(xb, wi_ref[...], preferred_element_type=F32)
        arb, aib = jnp.broadcast_to(ar, (SUBLANES, w)), jnp.broadcast_to(ai, (SUBLANES, w))
        zero = jnp.zeros((SUBLANES, w), F32)

        def tile(g, step):
            return pl.ds(pl.multiple_of(g * chain_rows + step * SUBLANES, SUBLANES), SUBLANES)

        def rows(g, i):
            return tile(g, seg - 1 - i if reverse else i)

        def local_step(i, carry):
            out = []
            for g in range(nch):
                sr, si = carry[2 * g], carry[2 * g + 1]
                idx = rows(g, i)
                sr, si = arb * sr - aib * si + b_scr[idx, :w], arb * si + aib * sr + b_scr[idx, w:]
                b_scr[idx, :w] = sr
                b_scr[idx, w:] = si
                out += [sr, si]
            return tuple(out)

        ends = lax.fori_loop(0, seg, local_step, (zero,) * (2 * nch), unroll=2)

        a_seg_r, a_seg_i = p_ref[0, seg - 1:seg, :], p_ref[1, seg - 1:seg, :]
        cr, ci = c_ref[0], c_ref[1]
        sub = lax.broadcasted_iota(jnp.int32, (SUBLANES, w), 0)
        ins = [[zero, zero] for _ in range(nch)]
        order = [(g, k) for g in range(nch) for k in range(SUBLANES)]
        for g, k in (order[::-1] if reverse else order):
            ins[g] = [jnp.where(sub == k, cr, ins[g][0]), jnp.where(sub == k, ci, ins[g][1])]
            er, ei = ends[2 * g][k:k + 1], ends[2 * g + 1][k:k + 1]
            cr, ci = er + a_seg_r * cr - a_seg_i * ci, ei + a_seg_r * ci + a_seg_i * cr
        c_ref[0] = cr
        c_ref[1] = ci

        def fix(g, i):
            idx = rows(g, i)
            pr, pi = p_ref[0, pl.ds(i, 1), :], p_ref[1, pl.ds(i, 1), :]
            sr = b_scr[idx, :w] + pr * ins[g][0] - pi * ins[g][1]
            si = b_scr[idx, w:] + pr * ins[g][1] + pi * ins[g][0]
            s_ref.at[0][idx, :] = sr.astype(s_ref.dtype)
            s_ref.at[1][idx, :] = si.astype(s_ref.dtype)
            return sr, si

        if not with_da:
            def fix_step(i, carry):
                for g in range(nch):
                    fix(g, i)
                return carry

            lax.fori_loop(0, seg, fix_step, 0, unroll=2)
        else:
            def adj_step(i, acc):
                acc_r, acc_i = acc
                for g in range(nch):
                    lr, li = fix(g, i)
                    prev = tile(g, seg - 2 - i)
                    fr, fi = sf_ref.at[0][prev, :].astype(F32), sf_ref.at[1][prev, :].astype(F32)
                    acc_r, acc_i = acc_r + lr * fr + li * fi, acc_i + li * fr - lr * fi
                return acc_r, acc_i

            acc_r, acc_i = lax.fori_loop(0, seg - 1, adj_step, (zero, zero), unroll=2)
            first_block = tt(t_blk) == 0
            for g in range(nch):
                lr, li = fix(g, seg - 1)
                seg_ends = tile(g, seg - 1)
                if g == 0:
                    pvr = jnp.where(first_block, 0.0, sp_ref[0, SUBLANES - 1:SUBLANES, :].astype(F32))
                    pvi = jnp.where(first_block, 0.0, sp_ref[1, SUBLANES - 1:SUBLANES, :].astype(F32))
                else:
                    pvr = sf_ref[0, g * chain_rows - 1:g * chain_rows, :].astype(F32)
                    pvi = sf_ref[1, g * chain_rows - 1:g * chain_rows, :].astype(F32)
                fr = jnp.where(sub == 0, pvr, pltpu.roll(sf_ref.at[0][seg_ends, :].astype(F32), 1, 0))
                fi = jnp.where(sub == 0, pvi, pltpu.roll(sf_ref.at[1][seg_ends, :].astype(F32), 1, 0))
                acc_r = acc_r + lr * fr + li * fi
                acc_i = acc_i + li * fr - lr * fi
            da_ref[0] += jnp.sum(acc_r, axis=0, keepdims=True)
            da_ref[1] += jnp.sum(acc_i, axis=0, keepdims=True)
            for plane in range(2):
                dw_ref[plane] += _tn_dot(u_ref[...], s_ref[plane])
                dx_ref[plane] += _tn_dot(xb, sf_ref[plane])

    x_spec = pl.BlockSpec((tb, bd_w), lambda j, t: (tt(t), j // tiles_per_bd))
    w_spec = pl.BlockSpec((bd_w, w), lambda j, t: (j // tiles_per_bd, j))
    d_spec = pl.BlockSpec((2, bd_w, w), lambda j, t: (0, j // tiles_per_bd, j % tiles_per_bd))
    a_spec = pl.BlockSpec((2, 1, w), lambda j, t: (0, 0, j))
    s_spec = pl.BlockSpec((2, tb, w), lambda j, t: (0, tt(t), j))
    in_specs, args = [x_spec, w_spec, w_spec, a_spec], [x, w_re, w_im, a_pair]
    out_specs, out_shape = [s_spec], [jax.ShapeDtypeStruct((2, l, SSM_S), BF16)]
    scratch = [pltpu.VMEM((2, seg, w), F32), pltpu.VMEM((2, 1, w), F32), pltpu.VMEM((tb, 2 * w), F32)]
    if with_da:
        in_specs += [s_spec, pl.BlockSpec((2, SUBLANES, w),
                                          lambda j, t: (0, jnp.maximum(tt(t) * (tb // SUBLANES) - 1, 0), j)),
                     x_spec]
        args += [s_fwd, s_fwd, u]
        out_specs += [a_spec, d_spec, d_spec]
        out_shape += ([jax.ShapeDtypeStruct((2, 1, SSM_S), F32)]
                      + [jax.ShapeDtypeStruct((2, SSM_W, SSM_S // SSM_BD), F32)] * 2)
    res = pl.pallas_call(
        body, grid=(SSM_S // w, nt), in_specs=in_specs, out_specs=out_specs, out_shape=out_shape,
        scratch_shapes=scratch, compiler_params=_params(("parallel", "arbitrary")), name=name,
    )(*args)
    return res if with_da else res[0]


def _nt_dot(x, y):
    return lax.dot_general(x.astype(_MXU), y.astype(_MXU), (((1,), (1,)), ((), ())), preferred_element_type=F32)


def _tn_dot(x, y):
    return lax.dot_general(x.astype(_MXU), y.astype(_MXU), (((0,), (0,)), ((), ())), preferred_element_type=F32)


def _nn_dot(x, y):
    return jnp.dot(x.astype(_MXU), y.astype(_MXU), preferred_element_type=F32)


def _attn_mask2(gb, nb):
    qi = lax.broadcasted_iota(jnp.int32, (ATT_WIN, 2 * ATT_WIN), 0)
    c = lax.broadcasted_iota(jnp.int32, (ATT_WIN, 2 * ATT_WIN), 1)
    has_prev = (gb % nb) != 0
    prev_ok = jnp.logical_and(jnp.logical_and(c < ATT_WIN, c >= qi), has_prev)
    own_ok = jnp.logical_and(c >= ATT_WIN, c - ATT_WIN <= qi)
    return jnp.logical_or(prev_ok, own_ok)


def _attn_specs():
    cur = pl.BlockSpec((ATT_QB * ATT_WIN, ATT_GW), lambda i: (i, 0))
    prev = pl.BlockSpec((ATT_WIN, ATT_GW), lambda i: (jnp.maximum(ATT_QB * i - 1, 0), 0))
    return cur, prev


def _attn_fwd(q, k, v, nb, name):
    l = q.shape[0]
    scale = ATT_E ** -0.5
    w = ATT_WIN

    def body(q_ref, kc_ref, kp_ref, vc_ref, vp_ref, o_ref, lse_ref):
        i = pl.program_id(0)
        masks = [_attn_mask2(ATT_QB * i + b, nb) for b in range(ATT_QB)]
        for h in range(ATT_HG):
            sl = slice(h * ATT_E, (h + 1) * ATT_E)
            k_ext = jnp.concatenate([kp_ref[:, sl], kc_ref[:, sl]], axis=0)
            v_ext = jnp.concatenate([vp_ref[:, sl], vc_ref[:, sl]], axis=0)
            for b in range(ATT_QB):
                r, kr = slice(b * w, (b + 1) * w), slice(b * w, (b + 2) * w)
                s = jnp.where(masks[b], _nt_dot(q_ref[r, sl], k_ext[kr]) * scale, NEG_INF)
                mx = jnp.max(s, axis=-1, keepdims=True)
                p = jnp.exp(s - mx)
                den = jnp.sum(p, axis=-1, keepdims=True)
                o_ref[r, sl] = _nn_dot(p, v_ext[kr]) / den
                lse_ref[r, sl] = jnp.broadcast_to(mx + jnp.log(den), (w, ATT_E))

    cur, prev = _attn_specs()
    return pl.pallas_call(
        body, grid=(l // (ATT_QB * w),), in_specs=[cur, cur, prev, cur, prev], out_specs=[cur, cur],
        out_shape=[jax.ShapeDtypeStruct((l, ATT_GW), F32)] * 2,
        compiler_params=_params(("parallel",)), name=name,
    )(q, k, k, v, v)


def _attn_bwd(q, k, v, do, lse, dd, nb, name):
    l = q.shape[0]
    scale = ATT_E ** -0.5
    w = ATT_WIN
    nblk = l // w

    def body(q_ref, kc_ref, kp_ref, vc_ref, vp_ref, do_ref, lse_ref, dd_ref, qn_ref, don_ref, lsen_ref, ddn_ref,
             dq_ref, dk_ref, dv_ref, dk_acc, dv_acc):
        i = pl.program_id(0)
        masks = [_attn_mask2(ATT_QB * i + b, nb) for b in range(ATT_QB)]
        nxt = ATT_QB * (i + 1)
        nxt_attends = jnp.logical_and(nxt < nblk, (nxt % nb) != 0)
        qi = lax.broadcasted_iota(jnp.int32, (w, w), 0)
        kj = lax.broadcasted_iota(jnp.int32, (w, w), 1)
        mask_n = jnp.logical_and(kj >= qi, nxt_attends)
        dk_acc[...] = jnp.zeros_like(dk_acc)
        dv_acc[...] = jnp.zeros_like(dv_acc)
        for h in range(ATT_HG):
            sl, col = slice(h * ATT_E, (h + 1) * ATT_E), slice(h * ATT_E, h * ATT_E + 1)
            k_ext = jnp.concatenate([kp_ref[:, sl], kc_ref[:, sl]], axis=0)
            v_ext = jnp.concatenate([vp_ref[:, sl], vc_ref[:, sl]], axis=0)
            for b in range(ATT_QB):
                r, kr = slice(b * w, (b + 1) * w), slice(b * w, (b + 2) * w)
                qh, doh, k2, v2 = q_ref[r, sl], do_ref[r, sl], k_ext[kr], v_ext[kr]
                p = jnp.where(masks[b], jnp.exp(_nt_dot(qh, k2) * scale - lse_ref[r, col]), 0.0)
                ds = p * (_nt_dot(doh, v2) - dd_ref[r, col]) * scale
                dq_ref[r, sl] = _nn_dot(ds, k2).astype(dq_ref.dtype)
                dk2, dv2 = _tn_dot(ds, qh), _tn_dot(p, doh)
                dk_acc[r, sl] += dk2[w:]
                dv_acc[r, sl] += dv2[w:]
                if b > 0:
                    rp = slice((b - 1) * w, b * w)
                    dk_acc[rp, sl] += dk2[:w]
                    dv_acc[rp, sl] += dv2[:w]
            last = slice((ATT_QB - 1) * w, ATT_QB * w)
            kl, vl, qn, don = kc_ref[last, sl], vc_ref[last, sl], qn_ref[:, sl], don_ref[:, sl]
            pn = jnp.where(mask_n, jnp.exp(_nt_dot(qn, kl) * scale - lsen_ref[:, col]), 0.0)
            dsn = pn * (_nt_dot(don, vl) - ddn_ref[:, col]) * scale
            dk_acc[last, sl] += _tn_dot(dsn, qn)
            dv_acc[last, sl] += _tn_dot(pn, don)
        dk_ref[...] = dk_acc[...].astype(dk_ref.dtype)
        dv_ref[...] = dv_acc[...].astype(dv_ref.dtype)

    cur, prev = _attn_specs()
    nxt_spec = pl.BlockSpec((w, ATT_GW), lambda i: (jnp.minimum(ATT_QB * (i + 1), nblk - 1), 0))
    return pl.pallas_call(
        body, grid=(l // (ATT_QB * w),),
        in_specs=[cur, cur, prev, cur, prev, cur, cur, cur, nxt_spec, nxt_spec, nxt_spec, nxt_spec],
        out_specs=[cur] * 3, out_shape=[jax.ShapeDtypeStruct((l, ATT_GW), BF16)] * 3,
        scratch_shapes=[pltpu.VMEM((ATT_QB * w, ATT_GW), F32)] * 2,
        compiler_params=_params(("parallel",)), name=name,
    )(q, k, k, v, v, do, lse, dd, q, do, lse, dd)


def _to_perm(a, d):
    if d == 1:
        return a
    l, c = a.shape
    return a.reshape(l // d, d, c).transpose(1, 0, 2).reshape(l, c)


def _from_perm(a, d):
    if d == 1:
        return a
    l, c = a.shape
    return a.reshape(d, l // d, c).transpose(1, 0, 2).reshape(l, c)


def _mem_probs(qh, kh):
    s = _nt_dot(qh, kh) * (MEM_E ** -0.5)
    e = jnp.exp(s - jnp.max(s, axis=-1, keepdims=True))
    return e / jnp.sum(e, axis=-1, keepdims=True)


def _mem_fwd(mq, kv, name, tm=512):
    l, nm = mq.shape[0], kv.shape[0]

    def body(q_ref, kv_ref, o_ref):
        for h in range(MEM_H):
            sl = slice(h * MEM_E, (h + 1) * MEM_E)
            p = _mem_probs(q_ref[:, sl], kv_ref[:, sl])
            o_ref[:, sl] = _nn_dot(p, kv_ref[:, MEM_W + h * MEM_E:MEM_W + (h + 1) * MEM_E]).astype(o_ref.dtype)

    return pl.pallas_call(
        body, grid=(l // tm,),
        in_specs=[pl.BlockSpec((tm, MEM_W), lambda i: (i, 0)), pl.BlockSpec((nm, 2 * MEM_W), lambda i: (0, 0))],
        out_specs=pl.BlockSpec((tm, MEM_W), lambda i: (i, 0)),
        out_shape=jax.ShapeDtypeStruct((l, MEM_W), BF16),
        compiler_params=_params(("parallel",)), name=name,
    )(mq, kv)


def _mem_bwd(mq, kv, dmo, name, tm=512):
    l, nm = mq.shape[0], kv.shape[0]
    scale = MEM_E ** -0.5

    def body(q_ref, kv_ref, do_ref, dq_ref, dkv_ref):
        @pl.when(pl.program_id(0) == 0)
        def _():
            dkv_ref[...] = jnp.zeros_like(dkv_ref)

        for h in range(MEM_H):
            sl = slice(h * MEM_E, (h + 1) * MEM_E)
            vsl = slice(MEM_W + h * MEM_E, MEM_W + (h + 1) * MEM_E)
            qh, kh, vh, doh = q_ref[:, sl], kv_ref[:, sl], kv_ref[:, vsl], do_ref[:, sl]
            p = _mem_probs(qh, kh)
            dp = _nt_dot(doh, vh)
            ds = p * (dp - jnp.sum(dp * p, axis=-1, keepdims=True)) * scale
            dq_ref[:, sl] = _nn_dot(ds, kh).astype(dq_ref.dtype)
            dkv_ref[:, sl] += _tn_dot(ds, qh)
            dkv_ref[:, vsl] += _tn_dot(p, doh)

    row = pl.BlockSpec((tm, MEM_W), lambda i: (i, 0))
    full = pl.BlockSpec((nm, 2 * MEM_W), lambda i: (0, 0))
    return pl.pallas_call(
        body, grid=(l // tm,), in_specs=[row, full, row], out_specs=[row, full],
        out_shape=[jax.ShapeDtypeStruct((l, MEM_W), BF16), jax.ShapeDtypeStruct((nm, 2 * MEM_W), F32)],
        compiler_params=_params(("arbitrary",)), name=name,
    )(mq, kv, dmo)


def _discretize(lam_re, lam_im, log_dt, b_re, b_im):
    dt = jnp.exp(log_dt)[:, None]
    mag = jnp.exp(lam_re * dt)
    a_re, a_im = mag * jnp.cos(lam_im * dt), mag * jnp.sin(lam_im * dt)
    nr, ni = a_re - 1.0, a_im
    den = lam_re * lam_re + lam_im * lam_im
    coef_re = (nr * lam_re + ni * lam_im) / den
    coef_im = (ni * lam_re - nr * lam_im) / den
    bb_re = coef_re[..., None] * b_re - coef_im[..., None] * b_im
    bb_im = coef_re[..., None] * b_im + coef_im[..., None] * b_re
    return a_re, a_im, bb_re, bb_im


def _bd_in(bb):
    return jnp.einsum("gph,gk->ghkp", bb, jnp.eye(SSM_G, dtype=bb.dtype)).reshape(SSM_W, SSM_S)


def _bd_diag(x):
    gb = SSM_G // SSM_BD
    t = x.reshape(SSM_BD, gb, SSM_H, gb, SSM_P)
    return jnp.einsum("bghgp->bghp", t).reshape(SSM_G, SSM_H, SSM_P)


_ANY = pl.BlockSpec(memory_space=pl.ANY)
_MESH = pl.DeviceIdType.MESH


def _allgather(x, name):
    def body(x_ref, out_ref, send_sems, recv_sems, local_sem):
        mx, my, mc = lax.axis_index("x"), lax.axis_index("y"), lax.axis_index("c")
        me, sibling = (mx, my, mc), (mx, my, 1 - mc)
        chips = [(1 - mx, my), (mx, 1 - my), (1 - mx, 1 - my)]

        def blk(px, py, pc):
            return out_ref.at[4 * px + 2 * py + pc]

        def copy(k, block, to, src=None):
            return pltpu.make_async_remote_copy(
                src_ref=blk(*block) if src is None else src, dst_ref=blk(*block),
                send_sem=send_sems.at[k], recv_sem=recv_sems.at[k], device_id=to, device_id_type=_MESH)

        mine = pltpu.make_async_copy(x_ref, blk(*me), local_sem)
        mine.start()
        first = [copy(0, me, sibling, src=x_ref)]
        first += [copy(1 + j, me, (*chip, mc), src=x_ref) for j, chip in enumerate(chips)]
        for cp in first:
            cp.start()
        passed = [copy(4 + j, (*chip, mc), sibling) for j, chip in enumerate(chips)]
        for j, chip in enumerate(chips):
            copy(1 + j, (*chip, mc), me).wait_recv()
            passed[j].start()
        copy(0, sibling, me).wait_recv()
        for j, chip in enumerate(chips):
            copy(4 + j, (*chip, 1 - mc), me).wait_recv()
        for cp in first + passed:
            cp.wait_send()
        mine.wait()

    return pl.pallas_call(
        body, out_shape=jax.ShapeDtypeStruct((N_DEV,) + x.shape, x.dtype), in_specs=[_ANY], out_specs=_ANY,
        scratch_shapes=[pltpu.SemaphoreType.DMA((7,)), pltpu.SemaphoreType.DMA((7,)), pltpu.SemaphoreType.DMA],
        name=name,
    )(x)


def _pair_exchange(g, name):
    def body(g_ref, out_ref, send_sems, recv_sems):
        mx, my, mc = lax.axis_index("x"), lax.axis_index("y"), lax.axis_index("c")
        copies = [pltpu.make_async_remote_copy(
            src_ref=g_ref.at[2 * k + (1 - mc)], dst_ref=out_ref.at[k], send_sem=send_sems.at[k],
            recv_sem=recv_sems.at[k], device_id=(mx, my, 1 - mc), device_id_type=_MESH) for k in range(4)]
        for cp in copies:
            cp.start()
        for cp in copies:
            cp.wait()

    return pl.pallas_call(
        body, out_shape=jax.ShapeDtypeStruct((4,) + g.shape[1:], g.dtype), in_specs=[_ANY], out_specs=_ANY,
        scratch_shapes=[pltpu.SemaphoreType.DMA((4,)), pltpu.SemaphoreType.DMA((4,))], name=name,
    )(g)


def _chip_exchange(p, name):
    def body(p_ref, out_ref, send_sems, recv_sems):
        mx, my, mc = lax.axis_index("x"), lax.axis_index("y"), lax.axis_index("c")
        chips = [(1 - mx, my), (mx, 1 - my), (1 - mx, 1 - my)]
        copies = [pltpu.make_async_remote_copy(
            src_ref=p_ref.at[2 * px + py], dst_ref=out_ref.at[j], send_sem=send_sems.at[j],
            recv_sem=recv_sems.at[j], device_id=(px, py, mc), device_id_type=_MESH)
            for j, (px, py) in enumerate(chips)]
        for cp in copies:
            cp.start()
        for cp in copies:
            cp.wait()

    return pl.pallas_call(
        body, out_shape=jax.ShapeDtypeStruct((3,) + p.shape[1:], p.dtype), in_specs=[_ANY], out_specs=_ANY,
        scratch_shapes=[pltpu.SemaphoreType.DMA((3,)), pltpu.SemaphoreType.DMA((3,))], name=name,
    )(p)


_HBM = pl.BlockSpec(memory_space=pltpu.HBM)
_SEM = pl.BlockSpec(memory_space=pltpu.SEMAPHORE)
_EFFECT = pltpu.SideEffectType.DATAFLOW_SIDE_EFFECTING
_TOKEN = jax.ShapeDtypeStruct((8, 128), F32)


def _peer(rel):
    pos = (lax.axis_index("x"), lax.axis_index("y"), lax.axis_index("c"))
    return tuple(1 - p if (rel >> (2 - i)) & 1 else p for i, p in enumerate(pos))


def _index_of(dev):
    return 4 * dev[0] + 2 * dev[1] + dev[2]


def _split_copies(src_ref, land_ref, sems, plan):
    n = len(plan)
    return [pltpu.make_async_remote_copy(
        src_ref=src_ref if s is None else src_ref.at[s], dst_ref=land_ref.at[d], send_sem=sems[k],
        recv_sem=sems[n + k], device_id=peer, device_id_type=_MESH) for k, (s, d, peer) in enumerate(plan)]


def _split_start(src, n_land, plan_fn, after, name):
    blk = src.shape[-2:]
    land = lax.empty((n_land,) + blk, src.dtype)
    n = len(plan_fn())

    def body(src_ref, land_ref, after_ref, *outs):
        for cp in _split_copies(src_ref, land_ref, outs[:2 * n], plan_fn()):
            cp.start()
        outs[2 * n + 2][...] = jnp.zeros_like(outs[2 * n + 2])

    res = pl.pallas_call(
        body, name=name,
        out_shape=(pltpu.SemaphoreType.DMA(()),) * (2 * n)
        + (pltpu.HBM(src.shape, src.dtype), pltpu.HBM(land.shape, land.dtype), _TOKEN),
        in_specs=(_HBM, _HBM, _ANY),
        out_specs=(_SEM,) * (2 * n) + (_HBM, _HBM, pl.BlockSpec(memory_space=pltpu.VMEM)),
        input_output_aliases={0: 2 * n, 1: 2 * n + 1},
        compiler_params=pltpu.CompilerParams(has_side_effects=_EFFECT),
    )(pltpu.with_memory_space_constraint(src, pltpu.HBM), pltpu.with_memory_space_constraint(land, pltpu.HBM), after)
    return res[:2 * n], res[2 * n], res[2 * n + 1], res[2 * n + 2]


def _split_wait(sems, src, land, plan_fn, after, name):
    n = len(sems) // 2

    def body(src_ref, land_ref, *rest):
        for cp in _split_copies(src_ref, land_ref, rest[:2 * n], plan_fn()):
            cp.wait_send()
            cp.wait_recv()

    return pl.pallas_call(
        body, name=name,
        out_shape=(pltpu.HBM(src.shape, src.dtype), pltpu.HBM(land.shape, land.dtype)),
        in_specs=(_HBM, _HBM) + (_SEM,) * (2 * n) + (_ANY,), out_specs=(_HBM, _HBM),
        input_output_aliases={0: 0, 1: 1},
        compiler_params=pltpu.CompilerParams(has_side_effects=_EFFECT),
    )(src, land, *sems, after)


def _gather_plan():
    me = _index_of(_peer(0))
    return [(None, me, _peer(rel)) for rel in range(1, N_DEV)]


def _gather_wait_plan():
    return [(None, _index_of(_peer(rel)), _peer(rel)) for rel in range(1, N_DEV)]


def _chip_plan():
    return [(_index_of(_peer(rel)) // 2, j, _peer(rel)) for j, rel in enumerate((4, 2, 6))]


def _pair_sum(g, t1, my_c, name, tr):
    _, r, c = g.shape

    def body(c_ref, g_ref, t_ref, o_ref, ob_ref):
        s = g_ref[...] + t_ref[...]
        o_ref[...] = s
        ob_ref[...] = s.astype(BF16)

    blk = pl.BlockSpec((None, tr, c), lambda k, i, cr: (k, i, 0))
    return pl.pallas_call(
        body,
        grid_spec=pltpu.PrefetchScalarGridSpec(
            num_scalar_prefetch=1, grid=(4, r // tr),
            in_specs=[pl.BlockSpec((None, tr, c), lambda k, i, cr: (2 * k + cr[0], i, 0)), blk],
            out_specs=[blk, blk]),
        out_shape=[jax.ShapeDtypeStruct((4, r, c), F32), jax.ShapeDtypeStruct((4, r, c), BF16)],
        compiler_params=_params(("parallel", "parallel")), name=name,
    )(my_c, g, t1)


def _adam_math(g, w, m, v):
    m = ADAM_B1 * m + (1.0 - ADAM_B1) * g
    v = ADAM_B2 * v + (1.0 - ADAM_B2) * (g * g)
    m_hat = m / (1.0 - ADAM_B1 ** ADAM_STEP)
    v_hat = v / (1.0 - ADAM_B2 ** ADAM_STEP)
    delta = -ADAM_LR * (m_hat / (jnp.sqrt(v_hat) + ADAM_EPS) + ADAM_WD * w)
    return delta, m, v


def _adam_big(p, t2, my_chip, w, m, v, name, tr):
    r, c = w.shape

    def body(k_ref, p_ref, t0_ref, t1_ref, t2_ref, w_ref, m_ref, v_ref, g_out, d_out, m_out, v_out):
        g = ((p_ref[...] + t0_ref[...].astype(F32)) + t1_ref[...].astype(F32)) + t2_ref[...].astype(F32)
        d, mn, vn = _adam_math(g, w_ref[...], m_ref[...], v_ref[...])
        g_out[...], d_out[...], m_out[...], v_out[...] = g, d, mn, vn

    flat = pl.BlockSpec((tr, c), lambda i, kr: (i, 0))

    def rel(j):
        return pl.BlockSpec((None, tr, c), lambda i, kr: (j, i, 0))

    return pl.pallas_call(
        body,
        grid_spec=pltpu.PrefetchScalarGridSpec(
            num_scalar_prefetch=1, grid=(r // tr,),
            in_specs=[pl.BlockSpec((None, tr, c), lambda i, kr: (kr[0], i, 0)), rel(0), rel(1), rel(2), flat, flat, flat],
            out_specs=[flat] * 4),
        out_shape=[jax.ShapeDtypeStruct((r, c), F32)] * 4,
        compiler_params=_params(("parallel",)), name=name,
    )(my_chip, p, t2, t2, t2, w, m, v)


def _sum8(g8, name):
    _, r, c = g8.shape

    def body(g_ref, o_ref):
        acc = g_ref[0]
        for j in range(1, N_DEV):
            acc = acc + g_ref[j]
        o_ref[...] = acc

    return pl.pallas_call(
        body, grid=(1,), in_specs=[pl.BlockSpec((N_DEV, r, c), lambda i: (0, 0, 0))],
        out_specs=pl.BlockSpec((r, c), lambda i: (0, 0)), out_shape=jax.ShapeDtypeStruct((r, c), F32),
        compiler_params=_params(("arbitrary",)), name=name,
    )(g8)


def _adam_small(g, w, m, v, name):
    def fn(r, b):
        return list(_adam_math(*r)), []
    c = g.shape[1]
    return _ew(fn, [g, w, m, v], [], [(c, F32)] * 3, [], name=name, tm=g.shape[0])


def _pack(arrs, pad_rows=8):
    flat = jnp.concatenate([a.reshape(-1) for a in arrs])
    n = flat.shape[0]
    q = PACK_C * pad_rows
    tot = -(-n // q) * q
    if tot != n:
        flat = jnp.concatenate([flat, jnp.zeros((tot - n,), flat.dtype)])
    return flat.reshape(tot // PACK_C, PACK_C)


def _unpack(buf, shapes):
    flat = buf.reshape(-1)
    out, off = [], 0
    for s in shapes:
        n = int(np.prod(s))
        out.append(flat[off:off + n].reshape(s))
        off += n
    return out


GROUPS = (("w_in", "w_mem_kv", "w_o"),
          ("w_glu", "w_ssm_br", "w_mem_br", "w_attn_br"),
          ("w_up", "w_down"))
GROUP_TR = (528, 384, 512)
MLP_GROUP = 2
ATTN_BR_FOLD = 2


def _stored_shape(name):
    r, c, ax = BIG_SHAPE[name]
    rows, cols = (r // N_DEV, c) if ax == 0 else (c // N_DEV, r)
    return (rows // ATTN_BR_FOLD, cols * ATTN_BR_FOLD) if name == "w_attn_br" else (rows, cols)


def _stored(shard, name):
    a = shard[0].T if BIG_SHAPE[name][2] == 1 else shard[0]
    return a.reshape(_stored_shape(name))


def _unstored(a, name):
    r, c, ax = BIG_SHAPE[name]
    if ax == 0:
        return a.reshape(1, r // N_DEV, c)
    return a.reshape(c // N_DEV, r).T[None]


def _pack_group(d, names):
    return jnp.concatenate([_stored(d[n], n) for n in names], axis=0)


def _split_group(buf, names):
    out, off = {}, 0
    for n in names:
        rows = _stored_shape(n)[0]
        out[n] = buf[..., off:off + rows, :]
        off += rows
    return out


def _full_stored(stacked, name):
    r, c, ax = BIG_SHAPE[name]
    return stacked.reshape((r, c) if ax == 0 else (c, r))


def _stacked_stored(full, name):
    return full.reshape((N_DEV,) + _stored_shape(name))


def _gelu_parts(x):
    c0, c1 = math.sqrt(2.0 / math.pi), 0.044715
    th = jnp.tanh(c0 * (x + c1 * x * x * x))
    return th, c0, c1


def _local_step(x, mem, tgt, wb, sp, mlp_weights, mlp_grads_ready):
    l = x.shape[0]
    w_a, w_g = wb["w_in"][:ZA_W], wb["w_in"][ZA_W:]

    a_re, a_im, bb_re, bb_im = _discretize(sp["ssm_lambda_re"], sp["ssm_lambda_im"], sp["ssm_log_dt"],
                                           sp["ssm_b_re"], sp["ssm_b_im"])
    a_pair = jnp.stack([a_re.reshape(1, SSM_S), a_im.reshape(1, SSM_S)])
    a_conj = jnp.stack([a_re.reshape(1, SSM_S), -a_im.reshape(1, SSM_S)])
    b_re_t, b_im_t = _bd_in(bb_re).astype(BF16), _bd_in(bb_im).astype(BF16)
    c_re_t = _bd_in(sp["ssm_c_re"].transpose(0, 2, 1)).astype(BF16)
    c_im_t = (-_bd_in(sp["ssm_c_im"].transpose(0, 2, 1))).astype(BF16)
    d_row = sp["ssm_d"].reshape(1, SSM_W)

    n1 = _rms_fwd(x, sp["norm1_g"], "rms1")
    za = _mm(n1, w_a, [BF16], tb=True, name="in_proj_a", tn=1664)
    zg = _mm(n1, w_g, [BF16], tb=True, name="in_proj_g")
    u = za[:, :SSM_W]
    mq = za[:, ZA_W - MEM_W:]

    u_s = _scan_order(u)
    s_all = _ssm_scan(u_s, b_re_t, b_im_t, a_pair, reverse=False, name="ssm_scan_fwd")
    ys = _time_order(_mm((s_all, 0), c_re_t, [F32], tb=True, pair2=((s_all, 1), c_im_t), bd=SSM_BD, name="ssm_cs"))

    def gelu_fn(r, b):
        y0 = r[0] + b[0] * r[1].astype(F32)
        th, _, _ = _gelu_parts(y0)
        return [y0, 0.5 * y0 * (1.0 + th)], []
    y0, y1 = _ew(gelu_fn, [ys, u], [d_row], [(SSM_W, F32), (SSM_W, BF16)], [], name="ssm_gelu", tm=512)

    def glu_epi(acc, y1t, bg):
        t = acc + bg
        return t, y1t.astype(F32) * _sigmoid(t)
    t_glu, y2 = _mm(y1, wb["w_glu"], [F32, BF16], epi=glu_epi, mn=[y1], rows=[sp["b_glu"]], name="ssm_glu")
    br_ssm = _mm(y2, wb["w_ssm_br"], [BF16], tb=True, name="ssm_br")

    qkv_p, o_g, lse_g = [], [], []
    for g, d in enumerate(DILATIONS):
        nb = l // d // ATT_WIN
        cols = [za[:, SSM_W + (3 * j + g) * ATT_GW: SSM_W + (3 * j + g + 1) * ATT_GW] for j in range(3)]
        qp, kp, vp = [_to_perm(cc, d) for cc in cols]
        qkv_p.append((qp, kp, vp))
        og, lg = _attn_fwd(qp, kp, vp, nb, "attn_fwd%d" % g)
        o_g.append(_from_perm(og, d))
        lse_g.append(_from_perm(lg, d))

    def merge_fn(r, b):
        o0, o1, o2, l0, l1, l2 = r
        mx = jnp.maximum(jnp.maximum(l0, l1), l2)
        e0, e1, e2 = jnp.exp(l0 - mx), jnp.exp(l1 - mx), jnp.exp(l2 - mx)
        tot = e0 + e1 + e2
        return [(e0 * o0 + e1 * o1 + e2 * o2) / tot, mx + jnp.log(tot)], []
    o_att, lse_tot = _ew(merge_fn, o_g + lse_g, [], [(ATT_GW, F32), (ATT_GW, F32)], [], name="attn_merge", tm=512)
    br_attn = _mm(o_att, wb["w_attn_br"], [BF16], tb=True, name="attn_br")

    mn = _rms_fwd(mem, sp["mem_norm_g"], "rms_mem")
    kv = _mm(mn, wb["w_mem_kv"], [BF16], name="mem_kv")
    mo = _mem_fwd(mq, kv, "mem_attn_fwd")
    br_mem = _mm(mo, wb["w_mem_br"], [BF16], tb=True, name="mem_br")

    def gate_fn(r, b):
        zgt, b0, b1, b2 = [t.astype(F32) for t in r]
        gt = _sigmoid(zgt + b[0])
        return [gt[:, :D_MODEL] * b0 + gt[:, D_MODEL:2 * D_MODEL] * b1 + gt[:, 2 * D_MODEL:] * b2], []
    merged = _ew(gate_fn, [zg, br_ssm, br_attn, br_mem], [sp["b_gate"]], [(D_MODEL, BF16)], [], name="gate_merge")[0]
    h1 = _mm(merged, wb["w_o"], [F32], epi=lambda acc, xt: (acc + xt,), mn=[x], name="o_proj")
    n2 = _rms_fwd(h1, sp["norm2_g"], "rms2")

    def up_epi(acc):
        ra = jnp.maximum(acc, 0.0)
        return ra * ra, ra
    wm = mlp_weights(n2)
    f_act, r_act = _mm(n2, wm["w_up"], [BF16, BF16], tb=True, epi=up_epi, name="mlp_up")
    h2 = _mm(f_act, wm["w_down"], [F32], epi=lambda acc, ht: (acc + ht,), mn=[h1], name="mlp_down")

    def final_fn(r, b):
        hv, tv = r
        gf = b[0]
        rs = lax.rsqrt(jnp.mean(hv * hv, axis=-1, keepdims=True) + RMS_EPS)
        err = hv * rs * gf - tv
        dy = err * (1.0 / D_MODEL)
        gd = dy * gf
        dh = rs * gd - hv * (rs * rs * rs) * jnp.mean(gd * hv, axis=-1, keepdims=True)
        loss = _colsum(jnp.sum(err * err, axis=-1, keepdims=True)) * (0.5 / D_MODEL)
        return [dh], [_colsum(dy * hv * rs), loss]
    dh2, d_final_g, loss = _ew(final_fn, [h2, tgt], [sp["final_g"]], [(D_MODEL, F32)], [D_MODEL, 1], name="final_loss")

    gw, gs = {}, {"final_g": d_final_g}
    d_act = _mm(dh2, wm["w_down"], [BF16], tb=True, epi=lambda acc, ra: (acc * 2.0 * ra.astype(F32),), mn=[r_act],
                name="mlp_down_dx")
    dw_down = _mm(f_act, dh2, [F32], ta=True, name="mlp_down_dw")
    dw_up = _mm(d_act, n2, [F32], ta=True, name="mlp_up_dw")
    token = mlp_grads_ready(dw_up, dw_down)
    dn2 = _mm(d_act, wm["w_up"], [F32], name="mlp_up_dx")
    dh1, gs["norm2_g"] = _rms_bwd(h1, dn2, dh2, sp["norm2_g"] + token[:1, :1], "rms2_bwd")
    dmerged = _mm(dh1, wb["w_o"], [F32], tb=True, name="o_proj_dx")
    gw["w_o"] = _mm(merged, dh1, [F32], ta=True, name="o_proj_dw")

    def gate_bwd_fn(r, b):
        dm, zgt, b0, b1, b2 = [t.astype(F32) for t in r]
        gt = _sigmoid(zgt + b[0])
        g0, g1, g2 = gt[:, :D_MODEL], gt[:, D_MODEL:2 * D_MODEL], gt[:, 2 * D_MODEL:]
        dzg = jnp.concatenate([dm * b0 * g0 * (1.0 - g0), dm * b1 * g1 * (1.0 - g1), dm * b2 * g2 * (1.0 - g2)], axis=1)
        return [dm * g0, dm * g1, dm * g2, dzg], [_colsum(dzg)]
    dbr_ssm, dbr_attn, dbr_mem, dzg, gs["b_gate"] = _ew(
        gate_bwd_fn, [dmerged, zg, br_ssm, br_attn, br_mem], [sp["b_gate"]],
        [(D_MODEL, BF16)] * 3 + [(ZG_W, BF16)], [ZG_W], name="gate_bwd")

    gw["w_ssm_br"] = _mm(dbr_ssm, y2, [F32], ta=True, name="ssm_br_dw")
    dy2 = _mm(dbr_ssm, wb["w_ssm_br"], [F32], name="ssm_br_dx")

    def glu_bwd_fn(r, b):
        dy, y1t, tt = r
        sg = _sigmoid(tt)
        dt = dy * y1t.astype(F32) * sg * (1.0 - sg)
        return [dt, dy * sg], [_colsum(dt)]
    dt_glu, dy1a, gs["b_glu"] = _ew(glu_bwd_fn, [dy2, y1, t_glu], [], [(SSM_W, BF16), (SSM_W, F32)], [SSM_W],
                                    name="ssm_glu_bwd", tm=512)
    gw["w_glu"] = _mm(y1, dt_glu, [F32], ta=True, name="ssm_glu_dw")

    def gelu_bwd_epi(acc, dy1t, y0t):
        th, c0, c1 = _gelu_parts(y0t)
        dg = 0.5 * (1.0 + th) + 0.5 * y0t * (1.0 - th * th) * c0 * (1.0 + 3.0 * c1 * y0t * y0t)
        return ((acc + dy1t) * dg,)
    dy0 = _mm(dt_glu, wb["w_glu"], [F32], tb=True, epi=gelu_bwd_epi, mn=[dy1a, y0], name="ssm_glu_dx")
    gs["ssm_d"] = _ew(lambda r, b: ([], [_colsum(r[0] * r[1].astype(F32))]), [dy0, u], [], [], [SSM_W],
                      name="ssm_dd", tm=512)[0]
    dy0_s = _scan_order(dy0)
    lam, da, d_b, d_c = _ssm_scan(dy0_s, c_re_t, c_im_t, a_conj, reverse=True, s_fwd=s_all, u=u_s,
                                  name="ssm_scan_bwd")
    du = _time_order(_mm((lam, 0), b_re_t, [BF16], tb=True, pair2=((lam, 1), b_im_t),
                         epi=lambda acc, dyt, dr: (acc + dyt * dr,), mn=[dy0_s], rows=[d_row], bd=SSM_BD, name="ssm_bu_dx"))
    gs["a_re"], gs["a_im"] = da[0], da[1]
    gs["bb_re"], gs["bb_im"] = _bd_diag(d_b[0]).transpose(0, 2, 1), _bd_diag(d_b[1]).transpose(0, 2, 1)
    gs["ssm_c_re"], gs["ssm_c_im"] = _bd_diag(d_c[0]), -_bd_diag(d_c[1])

    gw["w_attn_br"] = _mm(dbr_attn, o_att, [F32], ta=True, name="attn_br_dw")

    def do_epi(acc, ot):
        prod = acc * ot
        head = lax.broadcasted_iota(jnp.int32, prod.shape, 1) // ATT_E
        dd = jnp.zeros_like(prod)
        for h in range(ATT_HG):
            dd = jnp.where(head == h, jnp.sum(jnp.where(head == h, prod, 0.0), axis=1, keepdims=True), dd)
        return acc, dd
    do_att, dd_att = _mm(dbr_attn, wb["w_attn_br"], [BF16, F32], epi=do_epi, mn=[o_att], name="attn_br_dx")
    dq_l, dk_l, dv_l = [], [], []
    for g, d in enumerate(DILATIONS):
        nb = l // d // ATT_WIN
        qp, kp, vp = qkv_p[g]
        dq, dk, dv = _attn_bwd(qp, kp, vp, _to_perm(do_att, d), _to_perm(lse_tot, d), _to_perm(dd_att, d),
                               nb, "attn_bwd%d" % g)
        dq_l.append(_from_perm(dq, d))
        dk_l.append(_from_perm(dk, d))
        dv_l.append(_from_perm(dv, d))

    gw["w_mem_br"] = _mm(dbr_mem, mo, [F32], ta=True, name="mem_br_dw")
    dmo = _mm(dbr_mem, wb["w_mem_br"], [BF16], name="mem_br_dx")
    dmq, dkv = _mem_bwd(mq, kv, dmo, "mem_attn_bwd")
    gw["w_mem_kv"] = _mm(mn, dkv, [F32], ta=True, name="mem_kv_dw")
    dmn = _mm(dkv, wb["w_mem_kv"], [F32], tb=True, name="mem_kv_dx")
    gs["mem_norm_g"] = _rms_bwd(mem, dmn, None, sp["mem_norm_g"], "rms_mem_bwd")[1]

    dza = jnp.concatenate([du] + dq_l + dk_l + dv_l + [dmq], axis=1)
    dw_a = _mm(dza, n1, [F32], ta=True, name="in_proj_a_dw", tm=1664)
    dw_g = _mm(dzg, n1, [F32], ta=True, name="in_proj_g_dw")
    gw["w_in"] = jnp.concatenate([dw_a, dw_g], axis=0)
    dn_a = _mm(dza, w_a, [F32], name="in_proj_a_dx", tk=1664)
    dn1 = _mm(dzg, w_g, [F32], epi=lambda acc, pt: (acc + pt,), mn=[dn_a], name="in_proj_g_dx")
    grad_x, gs["norm1_g"] = _rms_bwd(x, dn1, dh1, sp["norm1_g"], "rms1_bwd")
    return loss, grad_x, gw, gs


_SMALL_GRAD_ORDER = ("norm1_g", "mem_norm_g", "b_gate", "a_re", "a_im", "bb_re", "bb_im", "ssm_c_re", "ssm_c_im",
                     "ssm_d", "b_glu", "norm2_g", "final_g")


def kernel(x, mem, norm1_g, mem_norm_g, w_in, b_gate, ssm_lambda_re, ssm_lambda_im, ssm_log_dt, ssm_b_re, ssm_b_im, ssm_c_re, ssm_c_im, ssm_d, w_glu, b_glu, w_ssm_br, w_attn_br, w_mem_kv, w_mem_br, w_o, norm2_g, w_up, w_down, final_g, loss_target, m_norm1_g, m_mem_norm_g, m_w_in, m_b_gate, m_ssm_lambda_re, m_ssm_lambda_im, m_ssm_log_dt, m_ssm_b_re, m_ssm_b_im, m_ssm_c_re, m_ssm_c_im, m_ssm_d, m_w_glu, m_b_glu, m_w_ssm_br, m_w_attn_br, m_w_mem_kv, m_w_mem_br, m_w_o, m_norm2_g, m_w_up, m_w_down, m_final_g, v_norm1_g, v_mem_norm_g, v_w_in, v_b_gate, v_ssm_lambda_re, v_ssm_lambda_im, v_ssm_log_dt, v_ssm_b_re, v_ssm_b_im, v_ssm_c_re, v_ssm_c_im, v_ssm_d, v_w_glu, v_b_glu, v_w_ssm_br, v_w_attn_br, v_w_mem_kv, v_w_mem_br, v_w_o, v_norm2_g, v_w_up, v_w_down, v_final_g):
    args = dict(locals())
    w = {n: args[n] for n in ALL_W}
    m = {n: args["m_" + n] for n in ALL_W}
    v = {n: args["v_" + n] for n in ALL_W}
    my_c = lax.axis_index("c").astype(jnp.int32).reshape(1)
    my_chip = (2 * lax.axis_index("x") + lax.axis_index("y")).astype(jnp.int32).reshape(1)

    w_pack = [_pack_group(w, names) for names in GROUPS]
    wb = {}
    for gi, names in enumerate(GROUPS):
        if gi == MLP_GROUP:
            continue
        w_all = _allgather(w_pack[gi].astype(BF16), "allgather_weights%d" % gi)
        for n, part in _split_group(w_all, names).items():
            wb[n] = _full_stored(part, n)
    mlp_names = GROUPS[MLP_GROUP]
    g_sems, g_src, g_land, g_token = _split_start(w_pack[MLP_GROUP].astype(BF16), N_DEV, _gather_plan, w_all,
                                                  "mlp_weights_gather_start")

    def mlp_weights(after):
        src, land = _split_wait(g_sems, g_src, g_land, _gather_wait_plan, after, "mlp_weights_gather_wait")
        my_index = (4 * lax.axis_index("x") + 2 * lax.axis_index("y") + lax.axis_index("c")).astype(jnp.int32)
        zero = jnp.zeros((), jnp.int32)
        w_mlp = lax.dynamic_update_slice(land, src[None], (my_index, zero, zero))
        return {n: _full_stored(part, n) for n, part in _split_group(w_mlp, mlp_names).items()}

    pending = {}

    def mlp_grads_ready(dw_up, dw_down):
        gwm = {"w_up": dw_up, "w_down": dw_down}
        g_pack = jnp.concatenate([_stacked_stored(gwm[n], n) for n in mlp_names], axis=1)
        t1 = _pair_exchange(g_pack, "grad_pair_exchange%d" % MLP_GROUP)
        p_sum, p_bf = _pair_sum(g_pack, t1, my_c, "grad_pair_sum%d" % MLP_GROUP, GROUP_TR[MLP_GROUP])
        sems, src, land, token = _split_start(p_bf, 3, _chip_plan, p_sum, "mlp_grad_chip_exchange_start")
        pending.update(p_sum=p_sum, sems=sems, src=src, land=land)
        return token

    sp = {
        "norm1_g": norm1_g + g_token[:1, :1], "mem_norm_g": mem_norm_g, "b_gate": b_gate, "b_glu": b_glu, "norm2_g": norm2_g,
        "final_g": final_g.reshape(1, D_MODEL),
        "ssm_lambda_re": ssm_lambda_re[0], "ssm_lambda_im": ssm_lambda_im[0], "ssm_log_dt": ssm_log_dt[0],
        "ssm_b_re": ssm_b_re[0], "ssm_b_im": ssm_b_im[0], "ssm_c_re": ssm_c_re[0], "ssm_c_im": ssm_c_im[0],
        "ssm_d": ssm_d[0],
    }
    loss, grad_x, gw, gs = _local_step(x[0], mem[0], loss_target[0], wb, sp, mlp_weights, mlp_grads_ready)
    loss = lax.psum(loss[0, 0], ("x", "y", "c"))

    big = [{}, {}, {}, {}]
    for gi, names in enumerate(GROUPS):
        if gi == MLP_GROUP:
            p_sum = pending["p_sum"]
            t2 = _split_wait(pending["sems"], pending["src"], pending["land"], _chip_plan, grad_x,
                             "mlp_grad_chip_exchange_wait")[1]
        else:
            g_pack = jnp.concatenate([_stacked_stored(gw[n], n) for n in names], axis=1)
            t1 = _pair_exchange(g_pack, "grad_pair_exchange%d" % gi)
            p_sum, p_bf = _pair_sum(g_pack, t1, my_c, "grad_pair_sum%d" % gi, GROUP_TR[gi])
            t2 = _chip_exchange(p_bf, "grad_chip_exchange%d" % gi)
        outs = _adam_big(p_sum, t2, my_chip, w_pack[gi], _pack_group(m, names), _pack_group(v, names),
                         "adam_big%d" % gi, GROUP_TR[gi])
        for kind, buf in enumerate(outs):
            for n, part in _split_group(buf, names).items():
                big[kind][n] = _unstored(part, n)

    sg_shapes = [gs[n].shape for n in _SMALL_GRAD_ORDER]
    sg_all = _allgather(_pack([gs[n] for n in _SMALL_GRAD_ORDER]), "allgather_small_grads")
    sg = dict(zip(_SMALL_GRAD_ORDER, _unpack(_sum8(sg_all, "sum_small_grads"), sg_shapes)))
    _, disc_vjp = jax.vjp(_discretize, sp["ssm_lambda_re"], sp["ssm_lambda_im"], sp["ssm_log_dt"],
                          sp["ssm_b_re"], sp["ssm_b_im"])
    d_lre, d_lim, d_ldt, d_bre, d_bim = disc_vjp((sg["a_re"].reshape(SSM_G, SSM_P), sg["a_im"].reshape(SSM_G, SSM_P),
                                                  sg["bb_re"], sg["bb_im"]))
    small_grad = {
        "norm1_g": sg["norm1_g"], "mem_norm_g": sg["mem_norm_g"], "b_gate": sg["b_gate"],
        "ssm_lambda_re": d_lre, "ssm_lambda_im": d_lim, "ssm_log_dt": d_ldt, "ssm_b_re": d_bre, "ssm_b_im": d_bim,
        "ssm_c_re": sg["ssm_c_re"], "ssm_c_im": sg["ssm_c_im"], "ssm_d": sg["ssm_d"], "b_glu": sg["b_glu"],
        "norm2_g": sg["norm2_g"], "final_g": sg["final_g"],
    }
    small_grad = {n: small_grad[n].reshape(w[n].shape) for n in SMALL}
    s_shapes = [w[n].shape for n in SMALL]
    small_out = _adam_small(_pack([small_grad[n] for n in SMALL]), _pack([w[n] for n in SMALL]),
                            _pack([m[n] for n in SMALL]), _pack([v[n] for n in SMALL]), "adam_small")
    small = [small_grad] + [dict(zip(SMALL, _unpack(b, s_shapes))) for b in small_out]

    outs = [loss, grad_x[None]]
    for kind in range(4):
        for n in ALL_W:
            outs.append(big[kind][n] if n in BIG else small[kind][n])
    return tuple(outs)
```

```python
import math

import numpy as np
import jax
import jax.numpy as jnp
from jax import lax
from jax.experimental import pallas as pl
from jax.experimental.pallas import tpu as pltpu

F32 = jnp.float32
BF16 = jnp.bfloat16
_MXU = jnp.bfloat16

D_MODEL = 1024
SSM_G, SSM_H, SSM_P = 32, 16, 64
SSM_W = SSM_G * SSM_H
SSM_S = SSM_G * SSM_P
SSM_BD = 4
ATT_E = 64
ATT_HG = 4
ATT_GW = ATT_HG * ATT_E
ATT_WIN = 128
ATT_QB = 4
DILATIONS = (1, 4, 16)
MEM_H, MEM_E = 4, 128
MEM_W = MEM_H * MEM_E
ZA_W = SSM_W + 9 * ATT_GW + MEM_W
ZG_W = 3 * D_MODEL
IN_W = ZA_W + ZG_W
RMS_EPS = 1e-6
NEG_INF = -1e30

ADAM_LR, ADAM_B1, ADAM_B2, ADAM_EPS, ADAM_WD, ADAM_STEP = 0.001, 0.9, 0.999, 1e-08, 0.01, 10

N_DEV = 8
PACK_C = 512
_VMEM_LIMIT = 56 * 1024 * 1024
SUBLANES = 16
SCAN_SEG = 128
SCAN_CHAINS = 2
SCAN_W = 128

BIG = ("w_in", "w_glu", "w_ssm_br", "w_attn_br", "w_mem_kv", "w_mem_br", "w_o", "w_up", "w_down")
BIG_SHAPE = {
    "w_in": (D_MODEL, IN_W, 1), "w_glu": (SSM_W, SSM_W, 0), "w_ssm_br": (SSM_W, D_MODEL, 1),
    "w_attn_br": (ATT_GW, D_MODEL, 1), "w_mem_kv": (D_MODEL, 2 * MEM_W, 0), "w_mem_br": (MEM_W, D_MODEL, 1),
    "w_o": (D_MODEL, D_MODEL, 0), "w_up": (D_MODEL, 4 * D_MODEL, 1), "w_down": (4 * D_MODEL, D_MODEL, 0),
}
SMALL = ("norm1_g", "mem_norm_g", "b_gate", "ssm_lambda_re", "ssm_lambda_im", "ssm_log_dt", "ssm_b_re",
         "ssm_b_im", "ssm_c_re", "ssm_c_im", "ssm_d", "b_glu", "norm2_g", "final_g")
ALL_W = ("norm1_g", "mem_norm_g", "w_in", "b_gate", "ssm_lambda_re", "ssm_lambda_im", "ssm_log_dt", "ssm_b_re",
         "ssm_b_im", "ssm_c_re", "ssm_c_im", "ssm_d", "w_glu", "b_glu", "w_ssm_br", "w_attn_br", "w_mem_kv",
         "w_mem_br", "w_o", "norm2_g", "w_up", "w_down", "final_g")


def _params(sem):
    return pltpu.CompilerParams(dimension_semantics=sem, vmem_limit_bytes=_VMEM_LIMIT)


def _pick(n, cap):
    if n <= cap:
        return n
    t = (cap // 128) * 128
    while n % t:
        t -= 128
    return t


def _mm(a, b, outs, *, name, ta=False, tb=False, epi=None, mn=(), rows=(), pair2=None, bd=0, n_sums=0,
        tm=1024, tn=1024, tk=2048):
    ab = [a, b] + (list(pair2) if pair2 is not None else [])
    planes = [op[1] if isinstance(op, tuple) else None for op in ab]
    ab = [op[0] if isinstance(op, tuple) else op for op in ab]
    a_shape, b_shape = ab[0].shape[-2:], ab[1].shape[-2:]
    m = a_shape[1] if ta else a_shape[0]
    k = a_shape[0] if ta else a_shape[1]
    n = b_shape[0] if tb else b_shape[1]
    assert k == (b_shape[1] if tb else b_shape[0]), (name, a_shape, b_shape)
    out_n = n
    if bd and ta:
        assert not tb
        tm, tn, tk = m // bd, n // bd, _pick(k, tk)
        grid, out_n = (bd, 1, k // tk), tn
        a_blk = ((tk, tm), lambda i, j, kk: (kk, i))
        b_blk = ((tk, tn), lambda i, j, kk: (kk, i))
        mn_spec = pl.BlockSpec((tm, tn), lambda i, j, kk: (i, 0))
    elif bd:
        tm, tn, tk = _pick(m, tm), n // bd, k // bd
        grid = (m // tm, bd, 1)
        a_blk = ((tm, tk), lambda i, j, kk: (i, j))
        b_blk = ((tn, tk) if tb else (tk, tn), lambda i, j, kk: (j, j))
        mn_spec = pl.BlockSpec((tm, tn), lambda i, j, kk: (i, j))
    else:
        tm, tn, tk = _pick(m, tm), _pick(n, tn), _pick(k, tk)
        grid = (m // tm, n // tn, k // tk)
        a_blk = ((tk, tm), lambda i, j, kk: (kk, i)) if ta else ((tm, tk), lambda i, j, kk: (i, kk))
        b_blk = ((tn, tk), lambda i, j, kk: (j, kk)) if tb else ((tk, tn), lambda i, j, kk: (kk, j))
        mn_spec = pl.BlockSpec((tm, tn), lambda i, j, kk: (i, j))

    def operand_spec(blk, plane):
        shape, imap = blk
        if plane is None:
            return pl.BlockSpec(shape, imap)
        return pl.BlockSpec((None,) + shape, lambda i, j, kk: (plane,) + imap(i, j, kk))

    ab_specs = [operand_spec(a_blk if q % 2 == 0 else b_blk, p) for q, p in enumerate(planes)]
    nk = grid[2]
    row_spec = pl.BlockSpec((1, tn), lambda i, j, kk: (0, j))
    n_ex, n_out = len(mn) + len(rows), len(outs)
    assert n_sums == 0 or (grid[1] == 1 and not bd)
    dims = (((0 if ta else 1,), (1 if tb else 0,)), ((), ()))

    def body(*refs):
        ab_refs, rest = refs[:len(ab)], refs[len(ab):]
        ex, o_refs, acc = rest[:n_ex], rest[n_ex:n_ex + n_out], rest[-1]
        s_refs = rest[n_ex + n_out:n_ex + n_out + n_sums]
        first_row_tile = pl.program_id(0) == 0
        kk = pl.program_id(2)

        @pl.when(kk == 0)
        def _():
            acc[...] = jnp.zeros_like(acc)

        for a_ref, b_ref in zip(ab_refs[0::2], ab_refs[1::2]):
            acc[...] += lax.dot_general(a_ref[...].astype(_MXU), b_ref[...].astype(_MXU), dims,
                                        preferred_element_type=F32)

        @pl.when(kk == nk - 1)
        def _():
            vals = (acc[...],) if epi is None else epi(acc[...], *[r[...] for r in ex])
            for r, v in zip(o_refs, vals):
                r[...] = v.astype(r.dtype)
            for r, v in zip(s_refs, vals[n_out:]):
                r[...] = jnp.where(first_row_tile, v, r[...] + v)

    res = pl.pallas_call(
        body, grid=grid,
        in_specs=ab_specs + [mn_spec] * len(mn) + [row_spec] * len(rows),
        out_specs=[mn_spec] * n_out + [row_spec] * n_sums,
        out_shape=[jax.ShapeDtypeStruct((m, out_n), dt) for dt in outs]
        + [jax.ShapeDtypeStruct((1, out_n), F32)] * n_sums,
        scratch_shapes=[pltpu.VMEM((tm, tn), F32)],
        compiler_params=_params(("arbitrary" if n_sums else "parallel", "parallel", "arbitrary")), name=name,
    )(*ab, *mn, *rows)
    return res[0] if n_out + n_sums == 1 else res


def _ew(fn, rows, bcs, out_rows, out_accs, *, name, tm=256):
    r = rows[0].shape[0]
    tm = min(tm, r)
    assert r % tm == 0
    nr, nb, no, na = len(rows), len(bcs), len(out_rows), len(out_accs)

    def body(*refs):
        i = pl.program_id(0)
        r_in, b_in = refs[:nr], refs[nr:nr + nb]
        o_r, o_a = refs[nr + nb:nr + nb + no], refs[nr + nb + no:]
        outs, accs = fn([x[...] for x in r_in], [x[...] for x in b_in])
        for ref, v in zip(o_r, outs):
            ref[...] = v.astype(ref.dtype)
        if na:
            @pl.when(i == 0)
            def _():
                for ref in o_a:
                    ref[...] = jnp.zeros_like(ref)

            for ref, v in zip(o_a, accs):
                ref[...] += v

    res = pl.pallas_call(
        body, grid=(r // tm,),
        in_specs=[pl.BlockSpec((tm, x.shape[1]), lambda i: (i, 0)) for x in rows]
        + [pl.BlockSpec((1, x.shape[1]), lambda i: (0, 0)) for x in bcs],
        out_specs=[pl.BlockSpec((tm, c), lambda i: (i, 0)) for c, _ in out_rows]
        + [pl.BlockSpec((1, c), lambda i: (0, 0)) for c in out_accs],
        out_shape=[jax.ShapeDtypeStruct((r, c), dt) for c, dt in out_rows]
        + [jax.ShapeDtypeStruct((1, c), F32) for c in out_accs],
        compiler_params=_params(("arbitrary",)), name=name,
    )(*rows, *bcs)
    return res


def _colsum(x):
    return jnp.sum(x, axis=0, keepdims=True)


def _sigmoid(x):
    return 1.0 / (1.0 + jnp.exp(-x))


def _rms_bwd_tile(xv, dv, g):
    rs = lax.rsqrt(jnp.mean(xv * xv, axis=-1, keepdims=True) + RMS_EPS)
    gd = dv * g
    dx = rs * gd - xv * (rs * rs * rs) * jnp.mean(gd * xv, axis=-1, keepdims=True)
    return dx, _colsum(dv * xv * rs)


def _rms_fwd(x, g, name):
    def fn(r, b):
        xv = r[0]
        rs = lax.rsqrt(jnp.mean(xv * xv, axis=-1, keepdims=True) + RMS_EPS)
        return [xv * rs * b[0]], []
    return _ew(fn, [x], [g], [(x.shape[1], BF16)], [], name=name)[0]


def _rms_bwd(x, dn, res, g, name):
    def fn(r, b):
        dx, dg = _rms_bwd_tile(r[0], r[1], b[0])
        if res is not None:
            dx = dx + r[2]
        return [dx], [dg]
    rows = [x, dn] + ([res] if res is not None else [])
    return _ew(fn, rows, [g], [(x.shape[1], F32)], [x.shape[1]], name=name)


def _scan_order(x):
    l, c = x.shape
    return x.reshape(l // (SUBLANES * SCAN_SEG), SUBLANES, SCAN_SEG, c).transpose(0, 2, 1, 3).reshape(l, c)


def _time_order(x):
    l, c = x.shape
    return x.reshape(l // (SUBLANES * SCAN_SEG), SCAN_SEG, SUBLANES, c).transpose(0, 2, 1, 3).reshape(l, c)


def _ssm_scan(x, w_re, w_im, a_pair, *, reverse, s_fwd=None, u=None, name):
    l = x.shape[0]
    seg, w = SCAN_SEG, SCAN_W
    bd_w = SSM_W // SSM_BD
    tiles_per_bd = SSM_S // SSM_BD // w
    nch = min(SCAN_CHAINS, l // (SUBLANES * seg))
    chain_rows = SUBLANES * seg
    tb = nch * chain_rows
    nt = l // tb
    with_da = s_fwd is not None
    assert reverse or not with_da

    def tt(t):
        return nt - 1 - t if reverse else t

    def body(*refs):
        if with_da:
            (x_ref, wr_ref, wi_ref, a_ref, sf_ref, sp_ref, u_ref, s_ref, da_ref, dw_ref, dx_ref,
             p_ref, c_ref, b_scr) = refs
        else:
            x_ref, wr_ref, wi_ref, a_ref, s_ref, p_ref, c_ref, b_scr = refs
        t_blk = pl.program_id(1)
        ar, ai = a_ref[0], a_ref[1]

        @pl.when(t_blk == 0)
        def _():
            def pstep(i, carry):
                pr, pi = carry
                p_ref[0, pl.ds(i, 1), :] = pr
                p_ref[1, pl.ds(i, 1), :] = pi
                return pr * ar - pi * ai, pr * ai + pi * ar

            lax.fori_loop(0, seg, pstep, (ar, ai))
            c_ref[...] = jnp.zeros_like(c_ref)
            if with_da:
                da_ref[...] = jnp.zeros_like(da_ref)
                dw_ref[...] = jnp.zeros_like(dw_ref)
                dx_ref[...] = jnp.zeros_like(dx_ref)

        xb = x_ref[...].astype(_MXU)
        b_scr[:, :w] = jnp.dot(xb, wr_ref[...], preferred_element_type=F32)
        b_scr[:, w:] = jnp.dot(xb, wi_ref[...], preferred_element_type=F32)
        arb, aib = jnp.broadcast_to(ar, (SUBLANES, w)), jnp.broadcast_to(ai, (SUBLANES, w))
        zero = jnp.zeros((SUBLANES, w), F32)

        def tile(g, step):
            return pl.ds(pl.multiple_of(g * chain_rows + step * SUBLANES, SUBLANES), SUBLANES)

        def rows(g, i):
            return tile(g, seg - 1 - i if reverse else i)

        def local_step(i, carry):
            out = []
            for g in range(nch):
                sr, si = carry[2 * g], carry[2 * g + 1]
                idx = rows(g, i)
                sr, si = arb * sr - aib * si + b_scr[idx, :w], arb * si + aib * sr + b_scr[idx, w:]
                b_scr[idx, :w] = sr
                b_scr[idx, w:] = si
                out += [sr, si]
            return tuple(out)

        ends = lax.fori_loop(0, seg, local_step, (zero,) * (2 * nch), unroll=2)

        a_seg_r, a_seg_i = p_ref[0, seg - 1:seg, :], p_ref[1, seg - 1:seg, :]
        cr, ci = c_ref[0], c_ref[1]
        sub = lax.broadcasted_iota(jnp.int32, (SUBLANES, w), 0)
        ins = [[zero, zero] for _ in range(nch)]
        order = [(g, k) for g in range(nch) for k in range(SUBLANES)]
        for g, k in (order[::-1] if reverse else order):
            ins[g] = [jnp.where(sub == k, cr, ins[g][0]), jnp.where(sub == k, ci, ins[g][1])]
            er, ei = ends[2 * g][k:k + 1], ends[2 * g + 1][k:k + 1]
            cr, ci = er + a_seg_r * cr - a_seg_i * ci, ei + a_seg_r * ci + a_seg_i * cr
        c_ref[0] = cr
        c_ref[1] = ci

        def fix(g, i):
            idx = rows(g, i)
            pr, pi = p_ref[0, pl.ds(i, 1), :], p_ref[1, pl.ds(i, 1), :]
            sr = b_scr[idx, :w] + pr * ins[g][0] - pi * ins[g][1]
            si = b_scr[idx, w:] + pr * ins[g][1] + pi * ins[g][0]
            s_ref.at[0][idx, :] = sr.astype(s_ref.dtype)
            s_ref.at[1][idx, :] = si.astype(s_ref.dtype)
            return sr, si

        if not with_da:
            def fix_step(i, carry):
                for g in range(nch):
                    fix(g, i)
                return carry

            lax.fori_loop(0, seg, fix_step, 0, unroll=2)
        else:
            def adj_step(i, acc):
                acc_r, acc_i = acc
                for g in range(nch):
                    lr, li = fix(g, i)
                    prev = tile(g, seg - 2 - i)
                    fr, fi = sf_ref.at[0][prev, :].astype(F32), sf_ref.at[1][prev, :].astype(F32)
                    acc_r, acc_i = acc_r + lr * fr + li * fi, acc_i + li * fr - lr * fi
                return acc_r, acc_i

            acc_r, acc_i = lax.fori_loop(0, seg - 1, adj_step, (zero, zero), unroll=2)
            first_block = tt(t_blk) == 0
            for g in range(nch):
                lr, li = fix(g, seg - 1)
                seg_ends = tile(g, seg - 1)
                if g == 0:
                    pvr = jnp.where(first_block, 0.0, sp_ref[0, SUBLANES - 1:SUBLANES, :].astype(F32))
                    pvi = jnp.where(first_block, 0.0, sp_ref[1, SUBLANES - 1:SUBLANES, :].astype(F32))
                else:
                    pvr = sf_ref[0, g * chain_rows - 1:g * chain_rows, :].astype(F32)
                    pvi = sf_ref[1, g * chain_rows - 1:g * chain_rows, :].astype(F32)
                fr = jnp.where(sub == 0, pvr, pltpu.roll(sf_ref.at[0][seg_ends, :].astype(F32), 1, 0))
                fi = jnp.where(sub == 0, pvi, pltpu.roll(sf_ref.at[1][seg_ends, :].astype(F32), 1, 0))
                acc_r = acc_r + lr * fr + li * fi
                acc_i = acc_i + li * fr - lr * fi
            da_ref[0] += jnp.sum(acc_r, axis=0, keepdims=True)
            da_ref[1] += jnp.sum(acc_i, axis=0, keepdims=True)
            for plane in range(2):
                dw_ref[plane] += _tn_dot(u_ref[...], s_ref[plane])
                dx_ref[plane] += _tn_dot(xb, sf_ref[plane])

    x_spec = pl.BlockSpec((tb, bd_w), lambda j, t: (tt(t), j // tiles_per_bd))
    w_spec = pl.BlockSpec((bd_w, w), lambda j, t: (j // tiles_per_bd, j))
    d_spec = pl.BlockSpec((2, bd_w, w), lambda j, t: (0, j // tiles_per_bd, j % tiles_per_bd))
    a_spec = pl.BlockSpec((2, 1, w), lambda j, t: (0, 0, j))
    s_spec = pl.BlockSpec((2, tb, w), lambda j, t: (0, tt(t), j))
    in_specs, args = [x_spec, w_spec, w_spec, a_spec], [x, w_re, w_im, a_pair]
    out_specs, out_shape = [s_spec], [jax.ShapeDtypeStruct((2, l, SSM_S), BF16)]
    scratch = [pltpu.VMEM((2, seg, w), F32), pltpu.VMEM((2, 1, w), F32), pltpu.VMEM((tb, 2 * w), F32)]
    if with_da:
        in_specs += [s_spec, pl.BlockSpec((2, SUBLANES, w),
                                          lambda j, t: (0, jnp.maximum(tt(t) * (tb // SUBLANES) - 1, 0), j)),
                     x_spec]
        args += [s_fwd, s_fwd, u]
        out_specs += [a_spec, d_spec, d_spec]
        out_shape += ([jax.ShapeDtypeStruct((2, 1, SSM_S), F32)]
                      + [jax.ShapeDtypeStruct((2, SSM_W, SSM_S // SSM_BD), F32)] * 2)
    res = pl.pallas_call(
        body, grid=(SSM_S // w, nt), in_specs=in_specs, out_specs=out_specs, out_shape=out_shape,
        scratch_shapes=scratch, compiler_params=_params(("parallel", "arbitrary")), name=name,
    )(*args)
    return res if with_da else res[0]


def _nt_dot(x, y):
    return lax.dot_general(x.astype(_MXU), y.astype(_MXU), (((1,), (1,)), ((), ())), preferred_element_type=F32)


def _tn_dot(x, y):
    return lax.dot_general(x.astype(_MXU), y.astype(_MXU), (((0,), (0,)), ((), ())), preferred_element_type=F32)


def _nn_dot(x, y):
    return jnp.dot(x.astype(_MXU), y.astype(_MXU), preferred_element_type=F32)


def _attn_mask2(gb, nb):
    qi = lax.broadcasted_iota(jnp.int32, (ATT_WIN, 2 * ATT_WIN), 0)
    c = lax.broadcasted_iota(jnp.int32, (ATT_WIN, 2 * ATT_WIN), 1)
    has_prev = (gb % nb) != 0
    prev_ok = jnp.logical_and(jnp.logical_and(c < ATT_WIN, c >= qi), has_prev)
    own_ok = jnp.logical_and(c >= ATT_WIN, c - ATT_WIN <= qi)
    return jnp.logical_or(prev_ok, own_ok)


def _attn_specs():
    cur = pl.BlockSpec((ATT_QB * ATT_WIN, ATT_GW), lambda i: (i, 0))
    prev = pl.BlockSpec((ATT_WIN, ATT_GW), lambda i: (jnp.maximum(ATT_QB * i - 1, 0), 0))
    return cur, prev


def _attn_fwd(q, k, v, nb, name):
    l = q.shape[0]
    scale = ATT_E ** -0.5
    w = ATT_WIN

    def body(q_ref, kc_ref, kp_ref, vc_ref, vp_ref, o_ref, lse_ref):
        i = pl.program_id(0)
        masks = [_attn_mask2(ATT_QB * i + b, nb) for b in range(ATT_QB)]
        for h in range(ATT_HG):
            sl = slice(h * ATT_E, (h + 1) * ATT_E)
            k_ext = jnp.concatenate([kp_ref[:, sl], kc_ref[:, sl]], axis=0)
            v_ext = jnp.concatenate([vp_ref[:, sl], vc_ref[:, sl]], axis=0)
            for b in range(ATT_QB):
                r, kr = slice(b * w, (b + 1) * w), slice(b * w, (b + 2) * w)
                s = jnp.where(masks[b], _nt_dot(q_ref[r, sl], k_ext[kr]) * scale, NEG_INF)
                mx = jnp.max(s, axis=-1, keepdims=True)
                p = jnp.exp(s - mx)
                den = jnp.sum(p, axis=-1, keepdims=True)
                o_ref[r, sl] = _nn_dot(p, v_ext[kr]) / den
                lse_ref[r, sl] = jnp.broadcast_to(mx + jnp.log(den), (w, ATT_E))

    cur, prev = _attn_specs()
    return pl.pallas_call(
        body, grid=(l // (ATT_QB * w),), in_specs=[cur, cur, prev, cur, prev], out_specs=[cur, cur],
        out_shape=[jax.ShapeDtypeStruct((l, ATT_GW), F32)] * 2,
        compiler_params=_params(("parallel",)), name=name,
    )(q, k, k, v, v)


def _attn_bwd(q, k, v, do, lse, dd, nb, name):
    l = q.shape[0]
    scale = ATT_E ** -0.5
    w = ATT_WIN
    nblk = l // w

    def body(q_ref, kc_ref, kp_ref, vc_ref, vp_ref, do_ref, lse_ref, dd_ref, qn_ref, don_ref, lsen_ref, ddn_ref,
             dq_ref, dk_ref, dv_ref, dk_acc, dv_acc):
        i = pl.program_id(0)
        masks = [_attn_mask2(ATT_QB * i + b, nb) for b in range(ATT_QB)]
        nxt = ATT_QB * (i + 1)
        nxt_attends = jnp.logical_and(nxt < nblk, (nxt % nb) != 0)
        qi = lax.broadcasted_iota(jnp.int32, (w, w), 0)
        kj = lax.broadcasted_iota(jnp.int32, (w, w), 1)
        mask_n = jnp.logical_and(kj >= qi, nxt_attends)
        dk_acc[...] = jnp.zeros_like(dk_acc)
        dv_acc[...] = jnp.zeros_like(dv_acc)
        for h in range(ATT_HG):
            sl, col = slice(h * ATT_E, (h + 1) * ATT_E), slice(h * ATT_E, h * ATT_E + 1)
            k_ext = jnp.concatenate([kp_ref[:, sl], kc_ref[:, sl]], axis=0)
            v_ext = jnp.concatenate([vp_ref[:, sl], vc_ref[:, sl]], axis=0)
            for b in range(ATT_QB):
                r, kr = slice(b * w, (b + 1) * w), slice(b * w, (b + 2) * w)
                qh, doh, k2, v2 = q_ref[r, sl], do_ref[r, sl], k_ext[kr], v_ext[kr]
                p = jnp.where(masks[b], jnp.exp(_nt_dot(qh, k2) * scale - lse_ref[r, col]), 0.0)
                ds = p * (_nt_dot(doh, v2) - dd_ref[r, col]) * scale
                dq_ref[r, sl] = _nn_dot(ds, k2).astype(dq_ref.dtype)
                dk2, dv2 = _tn_dot(ds, qh), _tn_dot(p, doh)
                dk_acc[r, sl] += dk2[w:]
                dv_acc[r, sl] += dv2[w:]
                if b > 0:
                    rp = slice((b - 1) * w, b * w)
                    dk_acc[rp, sl] += dk2[:w]
                    dv_acc[rp, sl] += dv2[:w]
            last = slice((ATT_QB - 1) * w, ATT_QB * w)
            kl, vl, qn, don = kc_ref[last, sl], vc_ref[last, sl], qn_ref[:, sl], don_ref[:, sl]
            pn = jnp.where(mask_n, jnp.exp(_nt_dot(qn, kl) * scale - lsen_ref[:, col]), 0.0)
            dsn = pn * (_nt_dot(don, vl) - ddn_ref[:, col]) * scale
            dk_acc[last, sl] += _tn_dot(dsn, qn)
            dv_acc[last, sl] += _tn_dot(pn, don)
        dk_ref[...] = dk_acc[...].astype(dk_ref.dtype)
        dv_ref[...] = dv_acc[...].astype(dv_ref.dtype)

    cur, prev = _attn_specs()
    nxt_spec = pl.BlockSpec((w, ATT_GW), lambda i: (jnp.minimum(ATT_QB * (i + 1), nblk - 1), 0))
    return pl.pallas_call(
        body, grid=(l // (ATT_QB * w),),
        in_specs=[cur, cur, prev, cur, prev, cur, cur, cur, nxt_spec, nxt_spec, nxt_spec, nxt_spec],
        out_specs=[cur] * 3, out_shape=[jax.ShapeDtypeStruct((l, ATT_GW), BF16)] * 3,
        scratch_shapes=[pltpu.VMEM((ATT_QB * w, ATT_GW), F32)] * 2,
        compiler_params=_params(("parallel",)), name=name,
    )(q, k, k, v, v, do, lse, dd, q, do, lse, dd)


def _to_perm(a, d):
    if d == 1:
        return a
    l, c = a.shape
    return a.reshape(l // d, d, c).transpose(1, 0, 2).reshape(l, c)


def _from_perm(a, d):
    if d == 1:
        return a
    l, c = a.shape
    return a.reshape(d, l // d, c).transpose(1, 0, 2).reshape(l, c)


def _mem_probs(qh, kh):
    s = _nt_dot(qh, kh) * (MEM_E ** -0.5)
    e = jnp.exp(s - jnp.max(s, axis=-1, keepdims=True))
    return e / jnp.sum(e, axis=-1, keepdims=True)


def _mem_fwd(mq, kv, name, tm=512):
    l, nm = mq.shape[0], kv.shape[0]

    def body(q_ref, kv_ref, o_ref):
        for h in range(MEM_H):
            sl = slice(h * MEM_E, (h + 1) * MEM_E)
            p = _mem_probs(q_ref[:, sl], kv_ref[:, sl])
            o_ref[:, sl] = _nn_dot(p, kv_ref[:, MEM_W + h * MEM_E:MEM_W + (h + 1) * MEM_E]).astype(o_ref.dtype)

    return pl.pallas_call(
        body, grid=(l // tm,),
        in_specs=[pl.BlockSpec((tm, MEM_W), lambda i: (i, 0)), pl.BlockSpec((nm, 2 * MEM_W), lambda i: (0, 0))],
        out_specs=pl.BlockSpec((tm, MEM_W), lambda i: (i, 0)),
        out_shape=jax.ShapeDtypeStruct((l, MEM_W), BF16),
        compiler_params=_params(("parallel",)), name=name,
    )(mq, kv)


def _mem_bwd(mq, kv, dmo, name, tm=512):
    l, nm = mq.shape[0], kv.shape[0]
    scale = MEM_E ** -0.5

    def body(q_ref, kv_ref, do_ref, dq_ref, dkv_ref):
        @pl.when(pl.program_id(0) == 0)
        def _():
            dkv_ref[...] = jnp.zeros_like(dkv_ref)

        for h in range(MEM_H):
            sl = slice(h * MEM_E, (h + 1) * MEM_E)
            vsl = slice(MEM_W + h * MEM_E, MEM_W + (h + 1) * MEM_E)
            qh, kh, vh, doh = q_ref[:, sl], kv_ref[:, sl], kv_ref[:, vsl], do_ref[:, sl]
            p = _mem_probs(qh, kh)
            dp = _nt_dot(doh, vh)
            ds = p * (dp - jnp.sum(dp * p, axis=-1, keepdims=True)) * scale
            dq_ref[:, sl] = _nn_dot(ds, kh).astype(dq_ref.dtype)
            dkv_ref[:, sl] += _tn_dot(ds, qh)
            dkv_ref[:, vsl] += _tn_dot(p, doh)

    row = pl.BlockSpec((tm, MEM_W), lambda i: (i, 0))
    full = pl.BlockSpec((nm, 2 * MEM_W), lambda i: (0, 0))
    return pl.pallas_call(
        body, grid=(l // tm,), in_specs=[row, full, row], out_specs=[row, full],
        out_shape=[jax.ShapeDtypeStruct((l, MEM_W), BF16), jax.ShapeDtypeStruct((nm, 2 * MEM_W), F32)],
        compiler_params=_params(("arbitrary",)), name=name,
    )(mq, kv, dmo)


def _discretize(lam_re, lam_im, log_dt, b_re, b_im):
    dt = jnp.exp(log_dt)[:, None]
    mag = jnp.exp(lam_re * dt)
    a_re, a_im = mag * jnp.cos(lam_im * dt), mag * jnp.sin(lam_im * dt)
    nr, ni = a_re - 1.0, a_im
    den = lam_re * lam_re + lam_im * lam_im
    coef_re = (nr * lam_re + ni * lam_im) / den
    coef_im = (ni * lam_re - nr * lam_im) / den
    bb_re = coef_re[..., None] * b_re - coef_im[..., None] * b_im
    bb_im = coef_re[..., None] * b_im + coef_im[..., None] * b_re
    return a_re, a_im, bb_re, bb_im


def _bd_in(bb):
    return jnp.einsum("gph,gk->ghkp", bb, jnp.eye(SSM_G, dtype=bb.dtype)).reshape(SSM_W, SSM_S)


def _bd_diag(x):
    gb = SSM_G // SSM_BD
    t = x.reshape(SSM_BD, gb, SSM_H, gb, SSM_P)
    return jnp.einsum("bghgp->bghp", t).reshape(SSM_G, SSM_H, SSM_P)


_ANY = pl.BlockSpec(memory_space=pl.ANY)
_MESH = pl.DeviceIdType.MESH


def _allgather(x, name):
    def body(x_ref, out_ref, send_sems, recv_sems, local_sem):
        mx, my, mc = lax.axis_index("x"), lax.axis_index("y"), lax.axis_index("c")
        me, sibling = (mx, my, mc), (mx, my, 1 - mc)
        chips = [(1 - mx, my), (mx, 1 - my), (1 - mx, 1 - my)]

        def blk(px, py, pc):
            return out_ref.at[4 * px + 2 * py + pc]

        def copy(k, block, to, src=None):
            return pltpu.make_async_remote_copy(
                src_ref=blk(*block) if src is None else src, dst_ref=blk(*block),
                send_sem=send_sems.at[k], recv_sem=recv_sems.at[k], device_id=to, device_id_type=_MESH)

        mine = pltpu.make_async_copy(x_ref, blk(*me), local_sem)
        mine.start()
        first = [copy(0, me, sibling, src=x_ref)]
        first += [copy(1 + j, me, (*chip, mc), src=x_ref) for j, chip in enumerate(chips)]
        for cp in first:
            cp.start()
        passed = [copy(4 + j, (*chip, mc), sibling) for j, chip in enumerate(chips)]
        for j, chip in enumerate(chips):
            copy(1 + j, (*chip, mc), me).wait_recv()
            passed[j].start()
        copy(0, sibling, me).wait_recv()
        for j, chip in enumerate(chips):
            copy(4 + j, (*chip, 1 - mc), me).wait_recv()
        for cp in first + passed:
            cp.wait_send()
        mine.wait()

    return pl.pallas_call(
        body, out_shape=jax.ShapeDtypeStruct((N_DEV,) + x.shape, x.dtype), in_specs=[_ANY], out_specs=_ANY,
        scratch_shapes=[pltpu.SemaphoreType.DMA((7,)), pltpu.SemaphoreType.DMA((7,)), pltpu.SemaphoreType.DMA],
        name=name,
    )(x)


def _pair_exchange(g, name):
    def body(g_ref, out_ref, send_sems, recv_sems):
        mx, my, mc = lax.axis_index("x"), lax.axis_index("y"), lax.axis_index("c")
        copies = [pltpu.make_async_remote_copy(
            src_ref=g_ref.at[2 * k + (1 - mc)], dst_ref=out_ref.at[k], send_sem=send_sems.at[k],
            recv_sem=recv_sems.at[k], device_id=(mx, my, 1 - mc), device_id_type=_MESH) for k in range(4)]
        for cp in copies:
            cp.start()
        for cp in copies:
            cp.wait()

    return pl.pallas_call(
        body, out_shape=jax.ShapeDtypeStruct((4,) + g.shape[1:], g.dtype), in_specs=[_ANY], out_specs=_ANY,
        scratch_shapes=[pltpu.SemaphoreType.DMA((4,)), pltpu.SemaphoreType.DMA((4,))], name=name,
    )(g)


def _chip_exchange(p, name):
    def body(p_ref, out_ref, send_sems, recv_sems):
        mx, my, mc = lax.axis_index("x"), lax.axis_index("y"), lax.axis_index("c")
        chips = [(1 - mx, my), (mx, 1 - my), (1 - mx, 1 - my)]
        copies = [pltpu.make_async_remote_copy(
            src_ref=p_ref.at[2 * px + py], dst_ref=out_ref.at[j], send_sem=send_sems.at[j],
            recv_sem=recv_sems.at[j], device_id=(px, py, mc), device_id_type=_MESH)
            for j, (px, py) in enumerate(chips)]
        for cp in copies:
            cp.start()
        for cp in copies:
            cp.wait()

    return pl.pallas_call(
        body, out_shape=jax.ShapeDtypeStruct((3,) + p.shape[1:], p.dtype), in_specs=[_ANY], out_specs=_ANY,
        scratch_shapes=[pltpu.SemaphoreType.DMA((3,)), pltpu.SemaphoreType.DMA((3,))], name=name,
    )(p)


_HBM = pl.BlockSpec(memory_space=pltpu.HBM)
_SEM = pl.BlockSpec(memory_space=pltpu.SEMAPHORE)
_EFFECT = pltpu.SideEffectType.DATAFLOW_SIDE_EFFECTING
_TOKEN = jax.ShapeDtypeStruct((8, 128), F32)


def _peer(rel):
    pos = (lax.axis_index("x"), lax.axis_index("y"), lax.axis_index("c"))
    return tuple(1 - p if (rel >> (2 - i)) & 1 else p for i, p in enumerate(pos))


def _index_of(dev):
    return 4 * dev[0] + 2 * dev[1] + dev[2]


def _split_copies(src_ref, land_ref, sems, plan):
    n = len(plan)
    return [pltpu.make_async_remote_copy(
        src_ref=src_ref if s is None else src_ref.at[s], dst_ref=land_ref.at[d], send_sem=sems[k],
        recv_sem=sems[n + k], device_id=peer, device_id_type=_MESH) for k, (s, d, peer) in enumerate(plan)]


def _split_start(src, n_land, plan_fn, after, name):
    blk = src.shape[-2:]
    land = lax.empty((n_land,) + blk, src.dtype)
    n = len(plan_fn())

    def body(src_ref, land_ref, after_ref, *outs):
        for cp in _split_copies(src_ref, land_ref, outs[:2 * n], plan_fn()):
            cp.start()
        outs[2 * n + 2][...] = jnp.zeros_like(outs[2 * n + 2])

    res = pl.pallas_call(
        body, name=name,
        out_shape=(pltpu.SemaphoreType.DMA(()),) * (2 * n)
        + (pltpu.HBM(src.shape, src.dtype), pltpu.HBM(land.shape, land.dtype), _TOKEN),
        in_specs=(_HBM, _HBM, _ANY),
        out_specs=(_SEM,) * (2 * n) + (_HBM, _HBM, pl.BlockSpec(memory_space=pltpu.VMEM)),
        input_output_aliases={0: 2 * n, 1: 2 * n + 1},
        compiler_params=pltpu.CompilerParams(has_side_effects=_EFFECT),
    )(pltpu.with_memory_space_constraint(src, pltpu.HBM), pltpu.with_memory_space_constraint(land, pltpu.HBM), after)
    return res[:2 * n], res[2 * n], res[2 * n + 1], res[2 * n + 2]


def _split_wait(sems, src, land, plan_fn, after, name):
    n = len(sems) // 2

    def body(src_ref, land_ref, *rest):
        for cp in _split_copies(src_ref, land_ref, rest[:2 * n], plan_fn()):
            cp.wait_send()
            cp.wait_recv()

    return pl.pallas_call(
        body, name=name,
        out_shape=(pltpu.HBM(src.shape, src.dtype), pltpu.HBM(land.shape, land.dtype)),
        in_specs=(_HBM, _HBM) + (_SEM,) * (2 * n) + (_ANY,), out_specs=(_HBM, _HBM),
        input_output_aliases={0: 0, 1: 1},
        compiler_params=pltpu.CompilerParams(has_side_effects=_EFFECT),
    )(src, land, *sems, after)


def _gather_plan():
    me = _index_of(_peer(0))
    return [(None, me, _peer(rel)) for rel in range(1, N_DEV)]


def _gather_wait_plan():
    return [(None, _index_of(_peer(rel)), _peer(rel)) for rel in range(1, N_DEV)]


def _chip_plan():
    return [(_index_of(_peer(rel)) // 2, j, _peer(rel)) for j, rel in enumerate((4, 2, 6))]


def _pair_sum(g, t1, my_c, name, tr):
    _, r, c = g.shape

    def body(c_ref, g_ref, t_ref, o_ref, ob_ref):
        s = g_ref[...] + t_ref[...]
        o_ref[...] = s
        ob_ref[...] = s.astype(BF16)

    blk = pl.BlockSpec((None, tr, c), lambda k, i, cr: (k, i, 0))
    return pl.pallas_call(
        body,
        grid_spec=pltpu.PrefetchScalarGridSpec(
            num_scalar_prefetch=1, grid=(4, r // tr),
            in_specs=[pl.BlockSpec((None, tr, c), lambda k, i, cr: (2 * k + cr[0], i, 0)), blk],
            out_specs=[blk, blk]),
        out_shape=[jax.ShapeDtypeStruct((4, r, c), F32), jax.ShapeDtypeStruct((4, r, c), BF16)],
        compiler_params=_params(("parallel", "parallel")), name=name,
    )(my_c, g, t1)


def _adam_math(g, w, m, v):
    m = ADAM_B1 * m + (1.0 - ADAM_B1) * g
    v = ADAM_B2 * v + (1.0 - ADAM_B2) * (g * g)
    m_hat = m / (1.0 - ADAM_B1 ** ADAM_STEP)
    v_hat = v / (1.0 - ADAM_B2 ** ADAM_STEP)
    delta = -ADAM_LR * (m_hat / (jnp.sqrt(v_hat) + ADAM_EPS) + ADAM_WD * w)
    return delta, m, v


def _adam_big(p, t2, my_chip, w, m, v, name, tr):
    r, c = w.shape

    def body(k_ref, p_ref, t0_ref, t1_ref, t2_ref, w_ref, m_ref, v_ref, g_out, d_out, m_out, v_out):
        g = ((p_ref[...] + t0_ref[...].astype(F32)) + t1_ref[...].astype(F32)) + t2_ref[...].astype(F32)
        d, mn, vn = _adam_math(g, w_ref[...], m_ref[...], v_ref[...])
        g_out[...], d_out[...], m_out[...], v_out[...] = g, d, mn, vn

    flat = pl.BlockSpec((tr, c), lambda i, kr: (i, 0))

    def rel(j):
        return pl.BlockSpec((None, tr, c), lambda i, kr: (j, i, 0))

    return pl.pallas_call(
        body,
        grid_spec=pltpu.PrefetchScalarGridSpec(
            num_scalar_prefetch=1, grid=(r // tr,),
            in_specs=[pl.BlockSpec((None, tr, c), lambda i, kr: (kr[0], i, 0)), rel(0), rel(1), rel(2), flat, flat, flat],
            out_specs=[flat] * 4),
        out_shape=[jax.ShapeDtypeStruct((r, c), F32)] * 4,
        compiler_params=_params(("parallel",)), name=name,
    )(my_chip, p, t2, t2, t2, w, m, v)


def _sum8(g8, name):
    _, r, c = g8.shape

    def body(g_ref, o_ref):
        acc = g_ref[0]
        for j in range(1, N_DEV):
            acc = acc + g_ref[j]
        o_ref[...] = acc

    return pl.pallas_call(
        body, grid=(1,), in_specs=[pl.BlockSpec((N_DEV, r, c), lambda i: (0, 0, 0))],
        out_specs=pl.BlockSpec((r, c), lambda i: (0, 0)), out_shape=jax.ShapeDtypeStruct((r, c), F32),
        compiler_params=_params(("arbitrary",)), name=name,
    )(g8)


def _adam_small(g, w, m, v, name):
    def fn(r, b):
        return list(_adam_math(*r)), []
    c = g.shape[1]
    return _ew(fn, [g, w, m, v], [], [(c, F32)] * 3, [], name=name, tm=g.shape[0])


def _pack(arrs, pad_rows=8):
    flat = jnp.concatenate([a.reshape(-1) for a in arrs])
    n = flat.shape[0]
    q = PACK_C * pad_rows
    tot = -(-n // q) * q
    if tot != n:
        flat = jnp.concatenate([flat, jnp.zeros((tot - n,), flat.dtype)])
    return flat.reshape(tot // PACK_C, PACK_C)


def _unpack(buf, shapes):
    flat = buf.reshape(-1)
    out, off = [], 0
    for s in shapes:
        n = int(np.prod(s))
        out.append(flat[off:off + n].reshape(s))
        off += n
    return out


GROUPS = (("w_in", "w_mem_kv", "w_o"),
          ("w_glu", "w_ssm_br", "w_mem_br", "w_attn_br"),
          ("w_up", "w_down"))
GROUP_TR = (528, 384, 512)
MLP_GROUP = 2
ATTN_BR_FOLD = 2


def _stored_shape(name):
    r, c, ax = BIG_SHAPE[name]
    rows, cols = (r // N_DEV, c) if ax == 0 else (c // N_DEV, r)
    return (rows // ATTN_BR_FOLD, cols * ATTN_BR_FOLD) if name == "w_attn_br" else (rows, cols)


def _stored(shard, name):
    a = shard[0].T if BIG_SHAPE[name][2] == 1 else shard[0]
    return a.reshape(_stored_shape(name))


def _unstored(a, name):
    r, c, ax = BIG_SHAPE[name]
    if ax == 0:
        return a.reshape(1, r // N_DEV, c)
    return a.reshape(c // N_DEV, r).T[None]


def _pack_group(d, names):
    return jnp.concatenate([_stored(d[n], n) for n in names], axis=0)


def _split_group(buf, names):
    out, off = {}, 0
    for n in names:
        rows = _stored_shape(n)[0]
        out[n] = buf[..., off:off + rows, :]
        off += rows
    return out


def _full_stored(stacked, name):
    r, c, ax = BIG_SHAPE[name]
    return stacked.reshape((r, c) if ax == 0 else (c, r))


def _stacked_stored(full, name):
    return full.reshape((N_DEV,) + _stored_shape(name))


def _gelu_parts(x):
    c0, c1 = math.sqrt(2.0 / math.pi), 0.044715
    th = jnp.tanh(c0 * (x + c1 * x * x * x))
    return th, c0, c1


def _local_step(x, mem, tgt, wb, sp, mlp_weights, mlp_grads_ready):
    l = x.shape[0]
    w_a, w_g = wb["w_in"][:ZA_W], wb["w_in"][ZA_W:]

    a_re, a_im, bb_re, bb_im = _discretize(sp["ssm_lambda_re"], sp["ssm_lambda_im"], sp["ssm_log_dt"],
                                           sp["ssm_b_re"], sp["ssm_b_im"])
    a_pair = jnp.stack([a_re.reshape(1, SSM_S), a_im.reshape(1, SSM_S)])
    a_conj = jnp.stack([a_re.reshape(1, SSM_S), -a_im.reshape(1, SSM_S)])
    b_re_t, b_im_t = _bd_in(bb_re).astype(BF16), _bd_in(bb_im).astype(BF16)
    c_re_t = _bd_in(sp["ssm_c_re"].transpose(0, 2, 1)).astype(BF16)
    c_im_t = (-_bd_in(sp["ssm_c_im"].transpose(0, 2, 1))).astype(BF16)
    d_row = sp["ssm_d"].reshape(1, SSM_W)

    n1 = _rms_fwd(x, sp["norm1_g"], "rms1")
    za = _mm(n1, w_a, [BF16], tb=True, name="in_proj_a", tn=1664)
    zg = _mm(n1, w_g, [BF16], tb=True, name="in_proj_g")
    u = za[:, :SSM_W]
    mq = za[:, ZA_W - MEM_W:]

    u_s = _scan_order(u)
    s_all = _ssm_scan(u_s, b_re_t, b_im_t, a_pair, reverse=False, name="ssm_scan_fwd")
    ys = _time_order(_mm((s_all, 0), c_re_t, [F32], tb=True, pair2=((s_all, 1), c_im_t), bd=SSM_BD, name="ssm_cs"))

    def gelu_fn(r, b):
        y0 = r[0] + b[0] * r[1].astype(F32)
        th, _, _ = _gelu_parts(y0)
        return [y0, 0.5 * y0 * (1.0 + th)], []
    y0, y1 = _ew(gelu_fn, [ys, u], [d_row], [(SSM_W, F32), (SSM_W, BF16)], [], name="ssm_gelu", tm=512)

    def glu_epi(acc, y1t, bg):
        t = acc + bg
        return t, y1t.astype(F32) * _sigmoid(t)
    t_glu, y2 = _mm(y1, wb["w_glu"], [F32, BF16], epi=glu_epi, mn=[y1], rows=[sp["b_glu"]], name="ssm_glu")
    br_ssm = _mm(y2, wb["w_ssm_br"], [BF16], tb=True, name="ssm_br")

    qkv_p, o_g, lse_g = [], [], []
    for g, d in enumerate(DILATIONS):
        nb = l // d // ATT_WIN
        cols = [za[:, SSM_W + (3 * j + g) * ATT_GW: SSM_W + (3 * j + g + 1) * ATT_GW] for j in range(3)]
        qp, kp, vp = [_to_perm(cc, d) for cc in cols]
        qkv_p.append((qp, kp, vp))
        og, lg = _attn_fwd(qp, kp, vp, nb, "attn_fwd%d" % g)
        o_g.append(_from_perm(og, d))
        lse_g.append(_from_perm(lg, d))

    def merge_fn(r, b):
        o0, o1, o2, l0, l1, l2 = r
        mx = jnp.maximum(jnp.maximum(l0, l1), l2)
        e0, e1, e2 = jnp.exp(l0 - mx), jnp.exp(l1 - mx), jnp.exp(l2 - mx)
        tot = e0 + e1 + e2
        return [(e0 * o0 + e1 * o1 + e2 * o2) / tot, mx + jnp.log(tot)], []
    o_att, lse_tot = _ew(merge_fn, o_g + lse_g, [], [(ATT_GW, F32), (ATT_GW, F32)], [], name="attn_merge", tm=512)
    br_attn = _mm(o_att, wb["w_attn_br"], [BF16], tb=True, name="attn_br")

    mn = _rms_fwd(mem, sp["mem_norm_g"], "rms_mem")
    kv = _mm(mn, wb["w_mem_kv"], [BF16], name="mem_kv")
    mo = _mem_fwd(mq, kv, "mem_attn_fwd")
    br_mem = _mm(mo, wb["w_mem_br"], [BF16], tb=True, name="mem_br")

    def gate_fn(r, b):
        zgt, b0, b1, b2 = [t.astype(F32) for t in r]
        gt = _sigmoid(zgt + b[0])
        return [gt[:, :D_MODEL] * b0 + gt[:, D_MODEL:2 * D_MODEL] * b1 + gt[:, 2 * D_MODEL:] * b2], []
    merged = _ew(gate_fn, [zg, br_ssm, br_attn, br_mem], [sp["b_gate"]], [(D_MODEL, BF16)], [], name="gate_merge")[0]
    def o_epi(acc, xt, g2):
        hv = acc + xt
        rs = lax.rsqrt(jnp.mean(hv * hv, axis=-1, keepdims=True) + RMS_EPS)
        return hv, hv * rs * g2
    h1, n2 = _mm(merged, wb["w_o"], [F32, BF16], epi=o_epi, mn=[x], rows=[sp["norm2_g"]], tm=512, name="o_proj")

    def up_epi(acc):
        ra = jnp.maximum(acc, 0.0)
        return ra * ra, ra
    wm = mlp_weights(n2)
    f_act, r_act = _mm(n2, wm["w_up"], [BF16, BF16], tb=True, epi=up_epi, name="mlp_up")
    def down_epi(acc, ht, tv, gf):
        hv = acc + ht
        rs = lax.rsqrt(jnp.mean(hv * hv, axis=-1, keepdims=True) + RMS_EPS)
        err = hv * rs * gf - tv
        dh, dgf = _rms_bwd_tile(hv, err * (1.0 / D_MODEL), gf)
        return dh, dgf, _colsum(err * err) * (0.5 / D_MODEL)
    dh2, d_final_g, loss_cols = _mm(f_act, wm["w_down"], [F32], epi=down_epi, mn=[h1, tgt], rows=[sp["final_g"]],
                                    n_sums=2, tm=512, name="mlp_down")
    loss = jnp.sum(loss_cols, axis=1, keepdims=True)

    gw, gs = {}, {"final_g": d_final_g}
    d_act = _mm(dh2, wm["w_down"], [BF16], tb=True, epi=lambda acc, ra: (acc * 2.0 * ra.astype(F32),), mn=[r_act],
                name="mlp_down_dx")
    dw_down = _mm(f_act, dh2, [F32], ta=True, name="mlp_down_dw")
    dw_up = _mm(d_act, n2, [F32], ta=True, name="mlp_up_dw")
    token = mlp_grads_ready(dw_up, dw_down)
    def up_dx_epi(acc, ht, dht, g2):
        dx, dg = _rms_bwd_tile(ht, acc, g2)
        return dx + dht, dg
    dh1, gs["norm2_g"] = _mm(d_act, wm["w_up"], [F32], epi=up_dx_epi, mn=[h1, dh2],
                             rows=[sp["norm2_g"] + token[:1, :1]], n_sums=1, tm=512, name="mlp_up_dx")
    dmerged = _mm(dh1, wb["w_o"], [F32], tb=True, name="o_proj_dx")
    gw["w_o"] = _mm(merged, dh1, [F32], ta=True, name="o_proj_dw")

    def gate_bwd_fn(r, b):
        dm, zgt, b0, b1, b2 = [t.astype(F32) for t in r]
        gt = _sigmoid(zgt + b[0])
        g0, g1, g2 = gt[:, :D_MODEL], gt[:, D_MODEL:2 * D_MODEL], gt[:, 2 * D_MODEL:]
        dzg = jnp.concatenate([dm * b0 * g0 * (1.0 - g0), dm * b1 * g1 * (1.0 - g1), dm * b2 * g2 * (1.0 - g2)], axis=1)
        return [dm * g0, dm * g1, dm * g2, dzg], [_colsum(dzg)]
    dbr_ssm, dbr_attn, dbr_mem, dzg, gs["b_gate"] = _ew(
        gate_bwd_fn, [dmerged, zg, br_ssm, br_attn, br_mem], [sp["b_gate"]],
        [(D_MODEL, BF16)] * 3 + [(ZG_W, BF16)], [ZG_W], name="gate_bwd")

    gw["w_ssm_br"] = _mm(dbr_ssm, y2, [F32], ta=True, name="ssm_br_dw")
    dy2 = _mm(dbr_ssm, wb["w_ssm_br"], [F32], name="ssm_br_dx")

    def glu_bwd_fn(r, b):
        dy, y1t, tt = r
        sg = _sigmoid(tt)
        dt = dy * y1t.astype(F32) * sg * (1.0 - sg)
        return [dt, dy * sg], [_colsum(dt)]
    dt_glu, dy1a, gs["b_glu"] = _ew(glu_bwd_fn, [dy2, y1, t_glu], [], [(SSM_W, BF16), (SSM_W, F32)], [SSM_W],
                                    name="ssm_glu_bwd", tm=512)
    gw["w_glu"] = _mm(y1, dt_glu, [F32], ta=True, name="ssm_glu_dw")

    def gelu_bwd_epi(acc, dy1t, y0t):
        th, c0, c1 = _gelu_parts(y0t)
        dg = 0.5 * (1.0 + th) + 0.5 * y0t * (1.0 - th * th) * c0 * (1.0 + 3.0 * c1 * y0t * y0t)
        return ((acc + dy1t) * dg,)
    dy0 = _mm(dt_glu, wb["w_glu"], [F32], tb=True, epi=gelu_bwd_epi, mn=[dy1a, y0], name="ssm_glu_dx")
    gs["ssm_d"] = _ew(lambda r, b: ([], [_colsum(r[0] * r[1].astype(F32))]), [dy0, u], [], [], [SSM_W],
                      name="ssm_dd", tm=512)[0]
    dy0_s = _scan_order(dy0)
    lam, da, d_b, d_c = _ssm_scan(dy0_s, c_re_t, c_im_t, a_conj, reverse=True, s_fwd=s_all, u=u_s,
                                  name="ssm_scan_bwd")
    du = _time_order(_mm((lam, 0), b_re_t, [BF16], tb=True, pair2=((lam, 1), b_im_t),
                         epi=lambda acc, dyt, dr: (acc + dyt * dr,), mn=[dy0_s], rows=[d_row], bd=SSM_BD, name="ssm_bu_dx"))
    gs["a_re"], gs["a_im"] = da[0], da[1]
    gs["bb_re"], gs["bb_im"] = _bd_diag(d_b[0]).transpose(0, 2, 1), _bd_diag(d_b[1]).transpose(0, 2, 1)
    gs["ssm_c_re"], gs["ssm_c_im"] = _bd_diag(d_c[0]), -_bd_diag(d_c[1])

    gw["w_attn_br"] = _mm(dbr_attn, o_att, [F32], ta=True, name="attn_br_dw")

    def do_epi(acc, ot):
        prod = acc * ot
        head = lax.broadcasted_iota(jnp.int32, prod.shape, 1) // ATT_E
        dd = jnp.zeros_like(prod)
        for h in range(ATT_HG):
            dd = jnp.where(head == h, jnp.sum(jnp.where(head == h, prod, 0.0), axis=1, keepdims=True), dd)
        return acc, dd
    do_att, dd_att = _mm(dbr_attn, wb["w_attn_br"], [BF16, F32], epi=do_epi, mn=[o_att], name="attn_br_dx")
    dq_l, dk_l, dv_l = [], [], []
    for g, d in enumerate(DILATIONS):
        nb = l // d // ATT_WIN
        qp, kp, vp = qkv_p[g]
        dq, dk, dv = _attn_bwd(qp, kp, vp, _to_perm(do_att, d), _to_perm(lse_tot, d), _to_perm(dd_att, d),
                               nb, "attn_bwd%d" % g)
        dq_l.append(_from_perm(dq, d))
        dk_l.append(_from_perm(dk, d))
        dv_l.append(_from_perm(dv, d))

    gw["w_mem_br"] = _mm(dbr_mem, mo, [F32], ta=True, name="mem_br_dw")
    dmo = _mm(dbr_mem, wb["w_mem_br"], [BF16], name="mem_br_dx")
    dmq, dkv = _mem_bwd(mq, kv, dmo, "mem_attn_bwd")
    gw["w_mem_kv"] = _mm(mn, dkv, [F32], ta=True, name="mem_kv_dw")
    dmn = _mm(dkv, wb["w_mem_kv"], [F32], tb=True, name="mem_kv_dx")
    gs["mem_norm_g"] = _rms_bwd(mem, dmn, None, sp["mem_norm_g"], "rms_mem_bwd")[1]

    dza = jnp.concatenate([du] + dq_l + dk_l + dv_l + [dmq], axis=1)
    dw_a = _mm(dza, n1, [F32], ta=True, name="in_proj_a_dw", tm=1664)
    dw_g = _mm(dzg, n1, [F32], ta=True, name="in_proj_g_dw")
    gw["w_in"] = jnp.concatenate([dw_a, dw_g], axis=0)
    dn_a = _mm(dza, w_a, [F32], name="in_proj_a_dx", tk=1664)
    def in_dx_epi(acc, pt, xt, dht, g1):
        dx, dg = _rms_bwd_tile(xt, acc + pt, g1)
        return dx + dht, dg
    grad_x, gs["norm1_g"] = _mm(dzg, w_g, [F32], epi=in_dx_epi, mn=[dn_a, x, dh1], rows=[sp["norm1_g"]],
                                n_sums=1, tm=512, name="in_proj_g_dx")
    return loss, grad_x, gw, gs


_SMALL_GRAD_ORDER = ("norm1_g", "mem_norm_g", "b_gate", "a_re", "a_im", "bb_re", "bb_im", "ssm_c_re", "ssm_c_im",
                     "ssm_d", "b_glu", "norm2_g", "final_g")


def kernel(x, mem, norm1_g, mem_norm_g, w_in, b_gate, ssm_lambda_re, ssm_lambda_im, ssm_log_dt, ssm_b_re, ssm_b_im, ssm_c_re, ssm_c_im, ssm_d, w_glu, b_glu, w_ssm_br, w_attn_br, w_mem_kv, w_mem_br, w_o, norm2_g, w_up, w_down, final_g, loss_target, m_norm1_g, m_mem_norm_g, m_w_in, m_b_gate, m_ssm_lambda_re, m_ssm_lambda_im, m_ssm_log_dt, m_ssm_b_re, m_ssm_b_im, m_ssm_c_re, m_ssm_c_im, m_ssm_d, m_w_glu, m_b_glu, m_w_ssm_br, m_w_attn_br, m_w_mem_kv, m_w_mem_br, m_w_o, m_norm2_g, m_w_up, m_w_down, m_final_g, v_norm1_g, v_mem_norm_g, v_w_in, v_b_gate, v_ssm_lambda_re, v_ssm_lambda_im, v_ssm_log_dt, v_ssm_b_re, v_ssm_b_im, v_ssm_c_re, v_ssm_c_im, v_ssm_d, v_w_glu, v_b_glu, v_w_ssm_br, v_w_attn_br, v_w_mem_kv, v_w_mem_br, v_w_o, v_norm2_g, v_w_up, v_w_down, v_final_g):
    args = dict(locals())
    w = {n: args[n] for n in ALL_W}
    m = {n: args["m_" + n] for n in ALL_W}
    v = {n: args["v_" + n] for n in ALL_W}
    my_c = lax.axis_index("c").astype(jnp.int32).reshape(1)
    my_chip = (2 * lax.axis_index("x") + lax.axis_index("y")).astype(jnp.int32).reshape(1)

    w_pack = [_pack_group(w, names) for names in GROUPS]
    wb = {}
    for gi in (1, 0):
        w_all = _allgather(w_pack[gi].astype(BF16), "allgather_weights%d" % gi)
        for n, part in _split_group(w_all, GROUPS[gi]).items():
            wb[n] = _full_stored(part, n)
    mlp_names = GROUPS[MLP_GROUP]
    g_sems, g_src, g_land, g_token = _split_start(w_pack[MLP_GROUP].astype(BF16), N_DEV, _gather_plan, w_all,
                                                  "mlp_weights_gather_start")

    def mlp_weights(after):
        src, land = _split_wait(g_sems, g_src, g_land, _gather_wait_plan, after, "mlp_weights_gather_wait")
        my_index = (4 * lax.axis_index("x") + 2 * lax.axis_index("y") + lax.axis_index("c")).astype(jnp.int32)
        zero = jnp.zeros((), jnp.int32)
        w_mlp = lax.dynamic_update_slice(land, src[None], (my_index, zero, zero))
        return {n: _full_stored(part, n) for n, part in _split_group(w_mlp, mlp_names).items()}

    pending = {}

    def mlp_grads_ready(dw_up, dw_down):
        gwm = {"w_up": dw_up, "w_down": dw_down}
        g_pack = jnp.concatenate([_stacked_stored(gwm[n], n) for n in mlp_names], axis=1)
        t1 = _pair_exchange(g_pack, "grad_pair_exchange%d" % MLP_GROUP)
        p_sum, p_bf = _pair_sum(g_pack, t1, my_c, "grad_pair_sum%d" % MLP_GROUP, GROUP_TR[MLP_GROUP])
        sems, src, land, token = _split_start(p_bf, 3, _chip_plan, p_sum, "mlp_grad_chip_exchange_start")
        pending.update(p_sum=p_sum, sems=sems, src=src, land=land)
        return token

    sp = {
        "norm1_g": norm1_g + g_token[:1, :1], "mem_norm_g": mem_norm_g, "b_gate": b_gate, "b_glu": b_glu, "norm2_g": norm2_g,
        "final_g": final_g.reshape(1, D_MODEL),
        "ssm_lambda_re": ssm_lambda_re[0], "ssm_lambda_im": ssm_lambda_im[0], "ssm_log_dt": ssm_log_dt[0],
        "ssm_b_re": ssm_b_re[0], "ssm_b_im": ssm_b_im[0], "ssm_c_re": ssm_c_re[0], "ssm_c_im": ssm_c_im[0],
        "ssm_d": ssm_d[0],
    }
    loss, grad_x, gw, gs = _local_step(x[0], mem[0], loss_target[0], wb, sp, mlp_weights, mlp_grads_ready)
    loss = lax.psum(loss[0, 0], ("x", "y", "c"))

    big = [{}, {}, {}, {}]
    for gi, names in enumerate(GROUPS):
        if gi == MLP_GROUP:
            p_sum = pending["p_sum"]
            t2 = _split_wait(pending["sems"], pending["src"], pending["land"], _chip_plan, grad_x,
                             "mlp_grad_chip_exchange_wait")[1]
        else:
            g_pack = jnp.concatenate([_stacked_stored(gw[n], n) for n in names], axis=1)
            t1 = _pair_exchange(g_pack, "grad_pair_exchange%d" % gi)
            p_sum, p_bf = _pair_sum(g_pack, t1, my_c, "grad_pair_sum%d" % gi, GROUP_TR[gi])
            t2 = _chip_exchange(p_bf, "grad_chip_exchange%d" % gi)
        outs = _adam_big(p_sum, t2, my_chip, w_pack[gi], _pack_group(m, names), _pack_group(v, names),
                         "adam_big%d" % gi, GROUP_TR[gi])
        for kind, buf in enumerate(outs):
            for n, part in _split_group(buf, names).items():
                big[kind][n] = _unstored(part, n)

    sg_shapes = [gs[n].shape for n in _SMALL_GRAD_ORDER]
    sg_all = _allgather(_pack([gs[n] for n in _SMALL_GRAD_ORDER]), "allgather_small_grads")
    sg = dict(zip(_SMALL_GRAD_ORDER, _unpack(_sum8(sg_all, "sum_small_grads"), sg_shapes)))
    _, disc_vjp = jax.vjp(_discretize, sp["ssm_lambda_re"], sp["ssm_lambda_im"], sp["ssm_log_dt"],
                          sp["ssm_b_re"], sp["ssm_b_im"])
    d_lre, d_lim, d_ldt, d_bre, d_bim = disc_vjp((sg["a_re"].reshape(SSM_G, SSM_P), sg["a_im"].reshape(SSM_G, SSM_P),
                                                  sg["bb_re"], sg["bb_im"]))
    small_grad = {
        "norm1_g": sg["norm1_g"], "mem_norm_g": sg["mem_norm_g"], "b_gate": sg["b_gate"],
        "ssm_lambda_re": d_lre, "ssm_lambda_im": d_lim, "ssm_log_dt": d_ldt, "ssm_b_re": d_bre, "ssm_b_im": d_bim,
        "ssm_c_re": sg["ssm_c_re"], "ssm_c_im": sg["ssm_c_im"], "ssm_d": sg["ssm_d"], "b_glu": sg["b_glu"],
        "norm2_g": sg["norm2_g"], "final_g": sg["final_g"],
    }
    small_grad = {n: small_grad[n].reshape(w[n].shape) for n in SMALL}
    s_shapes = [w[n].shape for n in SMALL]
    small_out = _adam_small(_pack([small_grad[n] for n in SMALL]), _pack([w[n] for n in SMALL]),
                            _pack([m[n] for n in SMALL]), _pack([v[n] for n in SMALL]), "adam_small")
    small = [small_grad] + [dict(zip(SMALL, _unpack(b, s_shapes))) for b in small_out]

    outs = [loss, grad_x[None]]
    for kind in range(4):
        for n in ALL_W:
            outs.append(big[kind][n] if n in BIG else small[kind][n])
    return tuple(outs)
```

```python
import math

import numpy as np
import jax
import jax.numpy as jnp
from jax import lax
from jax.experimental import pallas as pl
from jax.experimental.pallas import tpu as pltpu

F32 = jnp.float32
BF16 = jnp.bfloat16
_MXU = jnp.bfloat16

D_MODEL = 1024
SSM_G, SSM_H, SSM_P = 32, 16, 64
SSM_W = SSM_G * SSM_H
SSM_S = SSM_G * SSM_P
SSM_BD = 4
ATT_E = 64
ATT_HG = 4
ATT_GW = ATT_HG * ATT_E
ATT_WIN = 128
ATT_QB = 4
DILATIONS = (1, 4, 16)
MEM_H, MEM_E = 4, 128
MEM_W = MEM_H * MEM_E
ZA_W = SSM_W + 9 * ATT_GW + MEM_W
ZG_W = 3 * D_MODEL
IN_W = ZA_W + ZG_W
RMS_EPS = 1e-6
NEG_INF = -1e30

ADAM_LR, ADAM_B1, ADAM_B2, ADAM_EPS, ADAM_WD, ADAM_STEP = 0.001, 0.9, 0.999, 1e-08, 0.01, 10

N_DEV = 8
PACK_C = 512
_VMEM_LIMIT = 56 * 1024 * 1024
SUBLANES = 16
SCAN_SEG = 128
SCAN_CHAINS = 2
SCAN_UNROLL = 4
SCAN_W = 128

BIG = ("w_in", "w_glu", "w_ssm_br", "w_attn_br", "w_mem_kv", "w_mem_br", "w_o", "w_up", "w_down")
BIG_SHAPE = {
    "w_in": (D_MODEL, IN_W, 1), "w_glu": (SSM_W, SSM_W, 0), "w_ssm_br": (SSM_W, D_MODEL, 1),
    "w_attn_br": (ATT_GW, D_MODEL, 1), "w_mem_kv": (D_MODEL, 2 * MEM_W, 0), "w_mem_br": (MEM_W, D_MODEL, 1),
    "w_o": (D_MODEL, D_MODEL, 0), "w_up": (D_MODEL, 4 * D_MODEL, 1), "w_down": (4 * D_MODEL, D_MODEL, 0),
}
SMALL = ("norm1_g", "mem_norm_g", "b_gate", "ssm_lambda_re", "ssm_lambda_im", "ssm_log_dt", "ssm_b_re",
         "ssm_b_im", "ssm_c_re", "ssm_c_im", "ssm_d", "b_glu", "norm2_g", "final_g")
ALL_W = ("norm1_g", "mem_norm_g", "w_in", "b_gate", "ssm_lambda_re", "ssm_lambda_im", "ssm_log_dt", "ssm_b_re",
         "ssm_b_im", "ssm_c_re", "ssm_c_im", "ssm_d", "w_glu", "b_glu", "w_ssm_br", "w_attn_br", "w_mem_kv",
         "w_mem_br", "w_o", "norm2_g", "w_up", "w_down", "final_g")


def _params(sem):
    return pltpu.CompilerParams(dimension_semantics=sem, vmem_limit_bytes=_VMEM_LIMIT)


def _pick(n, cap):
    if n <= cap:
        return n
    t = (cap // 128) * 128
    while n % t:
        t -= 128
    return t


def _mm(a, b, outs, *, name, ta=False, tb=False, epi=None, mn=(), rows=(), pair2=None, bd=0, n_sums=0,
        tm=1024, tn=1024, tk=2048):
    ab = [a, b] + (list(pair2) if pair2 is not None else [])
    planes = [op[1] if isinstance(op, tuple) else None for op in ab]
    ab = [op[0] if isinstance(op, tuple) else op for op in ab]
    a_shape, b_shape = ab[0].shape[-2:], ab[1].shape[-2:]
    m = a_shape[1] if ta else a_shape[0]
    k = a_shape[0] if ta else a_shape[1]
    n = b_shape[0] if tb else b_shape[1]
    assert k == (b_shape[1] if tb else b_shape[0]), (name, a_shape, b_shape)
    out_n = n
    if bd and ta:
        assert not tb
        tm, tn, tk = m // bd, n // bd, _pick(k, tk)
        grid, out_n = (bd, 1, k // tk), tn
        a_blk = ((tk, tm), lambda i, j, kk: (kk, i))
        b_blk = ((tk, tn), lambda i, j, kk: (kk, i))
        mn_spec = pl.BlockSpec((tm, tn), lambda i, j, kk: (i, 0))
    elif bd:
        tm, tn, tk = _pick(m, tm), n // bd, k // bd
        grid = (m // tm, bd, 1)
        a_blk = ((tm, tk), lambda i, j, kk: (i, j))
        b_blk = ((tn, tk) if tb else (tk, tn), lambda i, j, kk: (j, j))
        mn_spec = pl.BlockSpec((tm, tn), lambda i, j, kk: (i, j))
    else:
        tm, tn, tk = _pick(m, tm), _pick(n, tn), _pick(k, tk)
        grid = (m // tm, n // tn, k // tk)
        a_blk = ((tk, tm), lambda i, j, kk: (kk, i)) if ta else ((tm, tk), lambda i, j, kk: (i, kk))
        b_blk = ((tn, tk), lambda i, j, kk: (j, kk)) if tb else ((tk, tn), lambda i, j, kk: (kk, j))
        mn_spec = pl.BlockSpec((tm, tn), lambda i, j, kk: (i, j))

    def operand_spec(blk, plane):
        shape, imap = blk
        if plane is None:
            return pl.BlockSpec(shape, imap)
        return pl.BlockSpec((None,) + shape, lambda i, j, kk: (plane,) + imap(i, j, kk))

    ab_specs = [operand_spec(a_blk if q % 2 == 0 else b_blk, p) for q, p in enumerate(planes)]
    nk = grid[2]
    row_spec = pl.BlockSpec((1, tn), lambda i, j, kk: (0, j))
    n_ex, n_out = len(mn) + len(rows), len(outs)
    assert n_sums == 0 or (grid[1] == 1 and not bd)
    dims = (((0 if ta else 1,), (1 if tb else 0,)), ((), ()))

    def body(*refs):
        ab_refs, rest = refs[:len(ab)], refs[len(ab):]
        ex, o_refs, acc = rest[:n_ex], rest[n_ex:n_ex + n_out], rest[-1]
        s_refs = rest[n_ex + n_out:n_ex + n_out + n_sums]
        first_row_tile = pl.program_id(0) == 0
        kk = pl.program_id(2)

        @pl.when(kk == 0)
        def _():
            acc[...] = jnp.zeros_like(acc)

        for a_ref, b_ref in zip(ab_refs[0::2], ab_refs[1::2]):
            acc[...] += lax.dot_general(a_ref[...].astype(_MXU), b_ref[...].astype(_MXU), dims,
                                        preferred_element_type=F32)

        @pl.when(kk == nk - 1)
        def _():
            vals = (acc[...],) if epi is None else epi(acc[...], *[r[...] for r in ex])
            for r, v in zip(o_refs, vals):
                r[...] = v.astype(r.dtype)
            for r, v in zip(s_refs, vals[n_out:]):
                r[...] = jnp.where(first_row_tile, v, r[...] + v)

    res = pl.pallas_call(
        body, grid=grid,
        in_specs=ab_specs + [mn_spec] * len(mn) + [row_spec] * len(rows),
        out_specs=[mn_spec] * n_out + [row_spec] * n_sums,
        out_shape=[jax.ShapeDtypeStruct((m, out_n), dt) for dt in outs]
        + [jax.ShapeDtypeStruct((1, out_n), F32)] * n_sums,
        scratch_shapes=[pltpu.VMEM((tm, tn), F32)],
        compiler_params=_params(("arbitrary" if n_sums else "parallel", "parallel", "arbitrary")), name=name,
    )(*ab, *mn, *rows)
    return res[0] if n_out + n_sums == 1 else res


def _ew(fn, rows, bcs, out_rows, out_accs, *, name, tm=256):
    r = rows[0].shape[0]
    tm = min(tm, r)
    assert r % tm == 0
    nr, nb, no, na = len(rows), len(bcs), len(out_rows), len(out_accs)

    def body(*refs):
        i = pl.program_id(0)
        r_in, b_in = refs[:nr], refs[nr:nr + nb]
        o_r, o_a = refs[nr + nb:nr + nb + no], refs[nr + nb + no:]
        outs, accs = fn([x[...] for x in r_in], [x[...] for x in b_in])
        for ref, v in zip(o_r, outs):
            ref[...] = v.astype(ref.dtype)
        if na:
            @pl.when(i == 0)
            def _():
                for ref in o_a:
                    ref[...] = jnp.zeros_like(ref)

            for ref, v in zip(o_a, accs):
                ref[...] += v

    res = pl.pallas_call(
        body, grid=(r // tm,),
        in_specs=[pl.BlockSpec((tm, x.shape[1]), lambda i: (i, 0)) for x in rows]
        + [pl.BlockSpec((1, x.shape[1]), lambda i: (0, 0)) for x in bcs],
        out_specs=[pl.BlockSpec((tm, c), lambda i: (i, 0)) for c, _ in out_rows]
        + [pl.BlockSpec((1, c), lambda i: (0, 0)) for c in out_accs],
        out_shape=[jax.ShapeDtypeStruct((r, c), dt) for c, dt in out_rows]
        + [jax.ShapeDtypeStruct((1, c), F32) for c in out_accs],
        compiler_params=_params(("arbitrary",)), name=name,
    )(*rows, *bcs)
    return res


def _colsum(x):
    return jnp.sum(x, axis=0, keepdims=True)


def _sigmoid(x):
    return 1.0 / (1.0 + jnp.exp(-x))


def _rms_bwd_tile(xv, dv, g):
    rs = lax.rsqrt(jnp.mean(xv * xv, axis=-1, keepdims=True) + RMS_EPS)
    gd = dv * g
    dx = rs * gd - xv * (rs * rs * rs) * jnp.mean(gd * xv, axis=-1, keepdims=True)
    return dx, _colsum(dv * xv * rs)


def _rms_fwd(x, g, name):
    def fn(r, b):
        xv = r[0]
        rs = lax.rsqrt(jnp.mean(xv * xv, axis=-1, keepdims=True) + RMS_EPS)
        return [xv * rs * b[0]], []
    return _ew(fn, [x], [g], [(x.shape[1], BF16)], [], name=name)[0]


def _rms_bwd(x, dn, res, g, name):
    def fn(r, b):
        dx, dg = _rms_bwd_tile(r[0], r[1], b[0])
        if res is not None:
            dx = dx + r[2]
        return [dx], [dg]
    rows = [x, dn] + ([res] if res is not None else [])
    return _ew(fn, rows, [g], [(x.shape[1], F32)], [x.shape[1]], name=name)


def _scan_order(x):
    l, c = x.shape
    return x.reshape(l // (SUBLANES * SCAN_SEG), SUBLANES, SCAN_SEG, c).transpose(0, 2, 1, 3).reshape(l, c)


def _time_order(x):
    l, c = x.shape
    return x.reshape(l // (SUBLANES * SCAN_SEG), SCAN_SEG, SUBLANES, c).transpose(0, 2, 1, 3).reshape(l, c)


def _ssm_scan(x, w_re, w_im, a_pair, *, reverse, s_fwd=None, u=None, name):
    l = x.shape[0]
    seg, w = SCAN_SEG, SCAN_W
    bd_w = SSM_W // SSM_BD
    tiles_per_bd = SSM_S // SSM_BD // w
    nch = min(SCAN_CHAINS, l // (SUBLANES * seg))
    chain_rows = SUBLANES * seg
    tb = nch * chain_rows
    nt = l // tb
    with_da = s_fwd is not None
    assert reverse or not with_da

    def tt(t):
        return nt - 1 - t if reverse else t

    def body(*refs):
        if with_da:
            (x_ref, wr_ref, wi_ref, a_ref, sf_ref, sp_ref, u_ref, s_ref, da_ref, dw_ref, dx_ref,
             p_ref, c_ref, b_scr) = refs
        else:
            x_ref, wr_ref, wi_ref, a_ref, s_ref, p_ref, c_ref, b_scr = refs
        t_blk = pl.program_id(1)
        ar, ai = a_ref[0], a_ref[1]

        @pl.when(t_blk == 0)
        def _():
            def pstep(i, carry):
                pr, pi = carry
                p_ref[0, pl.ds(i, 1), :] = pr
                p_ref[1, pl.ds(i, 1), :] = pi
                return pr * ar - pi * ai, pr * ai + pi * ar

            lax.fori_loop(0, seg, pstep, (ar, ai))
            c_ref[...] = jnp.zeros_like(c_ref)
            if with_da:
                da_ref[...] = jnp.zeros_like(da_ref)
                dw_ref[...] = jnp.zeros_like(dw_ref)
                dx_ref[...] = jnp.zeros_like(dx_ref)

        xb = x_ref[...].astype(_MXU)
        b_scr[:, :w] = jnp.dot(xb, wr_ref[...], preferred_element_type=F32)
        b_scr[:, w:] = jnp.dot(xb, wi_ref[...], preferred_element_type=F32)
        arb, aib = jnp.broadcast_to(ar, (SUBLANES, w)), jnp.broadcast_to(ai, (SUBLANES, w))
        zero = jnp.zeros((SUBLANES, w), F32)

        def tile(g, step):
            return pl.ds(pl.multiple_of(g * chain_rows + step * SUBLANES, SUBLANES), SUBLANES)

        def rows(g, i):
            return tile(g, seg - 1 - i if reverse else i)

        def local_step(i, carry):
            out = []
            for g in range(nch):
                sr, si = carry[2 * g], carry[2 * g + 1]
                idx = rows(g, i)
                sr, si = arb * sr - aib * si + b_scr[idx, :w], arb * si + aib * sr + b_scr[idx, w:]
                b_scr[idx, :w] = sr
                b_scr[idx, w:] = si
                out += [sr, si]
            return tuple(out)

        def unrolled(step_fn, first):
            def trip(q, carry):
                for r in range(SCAN_UNROLL):
                    carry = step_fn(first + q * SCAN_UNROLL + r, carry)
                return carry
            return trip

        ends = lax.fori_loop(0, seg // SCAN_UNROLL, unrolled(local_step, 0), (zero,) * (2 * nch))

        a_seg_r, a_seg_i = p_ref[0, seg - 1:seg, :], p_ref[1, seg - 1:seg, :]
        cr, ci = c_ref[0], c_ref[1]
        sub = lax.broadcasted_iota(jnp.int32, (SUBLANES, w), 0)
        ins = [[zero, zero] for _ in range(nch)]
        order = [(g, k) for g in range(nch) for k in range(SUBLANES)]
        for g, k in (order[::-1] if reverse else order):
            ins[g] = [jnp.where(sub == k, cr, ins[g][0]), jnp.where(sub == k, ci, ins[g][1])]
            er, ei = ends[2 * g][k:k + 1], ends[2 * g + 1][k:k + 1]
            cr, ci = er + a_seg_r * cr - a_seg_i * ci, ei + a_seg_r * ci + a_seg_i * cr
        c_ref[0] = cr
        c_ref[1] = ci

        def fix(g, i):
            idx = rows(g, i)
            pr, pi = p_ref[0, pl.ds(i, 1), :], p_ref[1, pl.ds(i, 1), :]
            sr = b_scr[idx, :w] + pr * ins[g][0] - pi * ins[g][1]
            si = b_scr[idx, w:] + pr * ins[g][1] + pi * ins[g][0]
            s_ref.at[0][idx, :] = sr.astype(s_ref.dtype)
            s_ref.at[1][idx, :] = si.astype(s_ref.dtype)
            return sr, si

        if not with_da:
            def fix_step(i, carry):
                for g in range(nch):
                    fix(g, i)
                return carry

            lax.fori_loop(0, seg // SCAN_UNROLL, unrolled(fix_step, 0), 0)
        else:
            def adj_step(i, acc):
                acc_r, acc_i = acc
                for g in range(nch):
                    lr, li = fix(g, i)
                    prev = tile(g, seg - 2 - i)
                    fr, fi = sf_ref.at[0][prev, :].astype(F32), sf_ref.at[1][prev, :].astype(F32)
                    acc_r, acc_i = acc_r + lr * fr + li * fi, acc_i + li * fr - lr * fi
                return acc_r, acc_i

            acc = lax.fori_loop(0, seg // SCAN_UNROLL - 1, unrolled(adj_step, 0), (zero, zero))
            for i in range(seg - SCAN_UNROLL, seg - 1):
                acc = adj_step(i, acc)
            acc_r, acc_i = acc
            first_block = tt(t_blk) == 0
            for g in range(nch):
                lr, li = fix(g, seg - 1)
                seg_ends = tile(g, seg - 1)
                if g == 0:
                    pvr = jnp.where(first_block, 0.0, sp_ref[0, SUBLANES - 1:SUBLANES, :].astype(F32))
                    pvi = jnp.where(first_block, 0.0, sp_ref[1, SUBLANES - 1:SUBLANES, :].astype(F32))
                else:
                    pvr = sf_ref[0, g * chain_rows - 1:g * chain_rows, :].astype(F32)
                    pvi = sf_ref[1, g * chain_rows - 1:g * chain_rows, :].astype(F32)
                fr = jnp.where(sub == 0, pvr, pltpu.roll(sf_ref.at[0][seg_ends, :].astype(F32), 1, 0))
                fi = jnp.where(sub == 0, pvi, pltpu.roll(sf_ref.at[1][seg_ends, :].astype(F32), 1, 0))
                acc_r = acc_r + lr * fr + li * fi
                acc_i = acc_i + li * fr - lr * fi
            da_ref[0] += jnp.sum(acc_r, axis=0, keepdims=True)
            da_ref[1] += jnp.sum(acc_i, axis=0, keepdims=True)
            for plane in range(2):
                dw_ref[plane] += _tn_dot(u_ref[...], s_ref[plane])
                dx_ref[plane] += _tn_dot(xb, sf_ref[plane])

    x_spec = pl.BlockSpec((tb, bd_w), lambda j, t: (tt(t), j // tiles_per_bd))
    w_spec = pl.BlockSpec((bd_w, w), lambda j, t: (j // tiles_per_bd, j))
    d_spec = pl.BlockSpec((2, bd_w, w), lambda j, t: (0, j // tiles_per_bd, j % tiles_per_bd))
    a_spec = pl.BlockSpec((2, 1, w), lambda j, t: (0, 0, j))
    s_spec = pl.BlockSpec((2, tb, w), lambda j, t: (0, tt(t), j))
    in_specs, args = [x_spec, w_spec, w_spec, a_spec], [x, w_re, w_im, a_pair]
    out_specs, out_shape = [s_spec], [jax.ShapeDtypeStruct((2, l, SSM_S), BF16)]
    scratch = [pltpu.VMEM((2, seg, w), F32), pltpu.VMEM((2, 1, w), F32), pltpu.VMEM((tb, 2 * w), F32)]
    if with_da:
        in_specs += [s_spec, pl.BlockSpec((2, SUBLANES, w),
                                          lambda j, t: (0, jnp.maximum(tt(t) * (tb // SUBLANES) - 1, 0), j)),
                     x_spec]
        args += [s_fwd, s_fwd, u]
        out_specs += [a_spec, d_spec, d_spec]
        out_shape += ([jax.ShapeDtypeStruct((2, 1, SSM_S), F32)]
                      + [jax.ShapeDtypeStruct((2, SSM_W, SSM_S // SSM_BD), F32)] * 2)
    res = pl.pallas_call(
        body, grid=(SSM_S // w, nt), in_specs=in_specs, out_specs=out_specs, out_shape=out_shape,
        scratch_shapes=scratch, compiler_params=_params(("parallel", "arbitrary")), name=name,
    )(*args)
    return res if with_da else res[0]


def _nt_dot(x, y):
    return lax.dot_general(x.astype(_MXU), y.astype(_MXU), (((1,), (1,)), ((), ())), preferred_element_type=F32)


def _tn_dot(x, y):
    return lax.dot_general(x.astype(_MXU), y.astype(_MXU), (((0,), (0,)), ((), ())), preferred_element_type=F32)


def _nn_dot(x, y):
    return jnp.dot(x.astype(_MXU), y.astype(_MXU), preferred_element_type=F32)


def _attn_mask2(gb, nb):
    qi = lax.broadcasted_iota(jnp.int32, (ATT_WIN, 2 * ATT_WIN), 0)
    c = lax.broadcasted_iota(jnp.int32, (ATT_WIN, 2 * ATT_WIN), 1)
    has_prev = (gb % nb) != 0
    prev_ok = jnp.logical_and(jnp.logical_and(c < ATT_WIN, c >= qi), has_prev)
    own_ok = jnp.logical_and(c >= ATT_WIN, c - ATT_WIN <= qi)
    return jnp.logical_or(prev_ok, own_ok)


def _attn_specs():
    cur = pl.BlockSpec((ATT_QB * ATT_WIN, ATT_GW), lambda i: (i, 0))
    prev = pl.BlockSpec((ATT_WIN, ATT_GW), lambda i: (jnp.maximum(ATT_QB * i - 1, 0), 0))
    return cur, prev


def _attn_fwd(q, k, v, nb, name):
    l = q.shape[0]
    scale = ATT_E ** -0.5
    w = ATT_WIN

    def body(q_ref, kc_ref, kp_ref, vc_ref, vp_ref, o_ref, lse_ref):
        i = pl.program_id(0)
        masks = [_attn_mask2(ATT_QB * i + b, nb) for b in range(ATT_QB)]
        for h in range(ATT_HG):
            sl = slice(h * ATT_E, (h + 1) * ATT_E)
            k_ext = jnp.concatenate([kp_ref[:, sl], kc_ref[:, sl]], axis=0)
            v_ext = jnp.concatenate([vp_ref[:, sl], vc_ref[:, sl]], axis=0)
            for b in range(ATT_QB):
                r, kr = slice(b * w, (b + 1) * w), slice(b * w, (b + 2) * w)
                s = jnp.where(masks[b], _nt_dot(q_ref[r, sl], k_ext[kr]) * scale, NEG_INF)
                mx = jnp.max(s, axis=-1, keepdims=True)
                p = jnp.exp(s - mx)
                den = jnp.sum(p, axis=-1, keepdims=True)
                o_ref[r, sl] = _nn_dot(p, v_ext[kr]) / den
                lse_ref[r, sl] = jnp.broadcast_to(mx + jnp.log(den), (w, ATT_E))

    cur, prev = _attn_specs()
    return pl.pallas_call(
        body, grid=(l // (ATT_QB * w),), in_specs=[cur, cur, prev, cur, prev], out_specs=[cur, cur],
        out_shape=[jax.ShapeDtypeStruct((l, ATT_GW), F32)] * 2,
        compiler_params=_params(("parallel",)), name=name,
    )(q, k, k, v, v)


def _attn_bwd(q, k, v, do, lse, dd, nb, name):
    l = q.shape[0]
    scale = ATT_E ** -0.5
    w = ATT_WIN
    nblk = l // w

    def body(q_ref, kc_ref, kp_ref, vc_ref, vp_ref, do_ref, lse_ref, dd_ref, qn_ref, don_ref, lsen_ref, ddn_ref,
             dq_ref, dk_ref, dv_ref, dk_acc, dv_acc):
        i = pl.program_id(0)
        masks = [_attn_mask2(ATT_QB * i + b, nb) for b in range(ATT_QB)]
        nxt = ATT_QB * (i + 1)
        nxt_attends = jnp.logical_and(nxt < nblk, (nxt % nb) != 0)
        qi = lax.broadcasted_iota(jnp.int32, (w, w), 0)
        kj = lax.broadcasted_iota(jnp.int32, (w, w), 1)
        mask_n = jnp.logical_and(kj >= qi, nxt_attends)
        dk_acc[...] = jnp.zeros_like(dk_acc)
        dv_acc[...] = jnp.zeros_like(dv_acc)
        for h in range(ATT_HG):
            sl, col = slice(h * ATT_E, (h + 1) * ATT_E), slice(h * ATT_E, h * ATT_E + 1)
            k_ext = jnp.concatenate([kp_ref[:, sl], kc_ref[:, sl]], axis=0)
            v_ext = jnp.concatenate([vp_ref[:, sl], vc_ref[:, sl]], axis=0)
            for b in range(ATT_QB):
                r, kr = slice(b * w, (b + 1) * w), slice(b * w, (b + 2) * w)
                qh, doh, k2, v2 = q_ref[r, sl], do_ref[r, sl], k_ext[kr], v_ext[kr]
                p = jnp.where(masks[b], jnp.exp(_nt_dot(qh, k2) * scale - lse_ref[r, col]), 0.0)
                ds = p * (_nt_dot(doh, v2) - dd_ref[r, col]) * scale
                dq_ref[r, sl] = _nn_dot(ds, k2).astype(dq_ref.dtype)
                dk2, dv2 = _tn_dot(ds, qh), _tn_dot(p, doh)
                dk_acc[r, sl] += dk2[w:]
                dv_acc[r, sl] += dv2[w:]
                if b > 0:
                    rp = slice((b - 1) * w, b * w)
                    dk_acc[rp, sl] += dk2[:w]
                    dv_acc[rp, sl] += dv2[:w]
            last = slice((ATT_QB - 1) * w, ATT_QB * w)
            kl, vl, qn, don = kc_ref[last, sl], vc_ref[last, sl], qn_ref[:, sl], don_ref[:, sl]
            pn = jnp.where(mask_n, jnp.exp(_nt_dot(qn, kl) * scale - lsen_ref[:, col]), 0.0)
            dsn = pn * (_nt_dot(don, vl) - ddn_ref[:, col]) * scale
            dk_acc[last, sl] += _tn_dot(dsn, qn)
            dv_acc[last, sl] += _tn_dot(pn, don)
        dk_ref[...] = dk_acc[...].astype(dk_ref.dtype)
        dv_ref[...] = dv_acc[...].astype(dv_ref.dtype)

    cur, prev = _attn_specs()
    nxt_spec = pl.BlockSpec((w, ATT_GW), lambda i: (jnp.minimum(ATT_QB * (i + 1), nblk - 1), 0))
    return pl.pallas_call(
        body, grid=(l // (ATT_QB * w),),
        in_specs=[cur, cur, prev, cur, prev, cur, cur, cur, nxt_spec, nxt_spec, nxt_spec, nxt_spec],
        out_specs=[cur] * 3, out_shape=[jax.ShapeDtypeStruct((l, ATT_GW), BF16)] * 3,
        scratch_shapes=[pltpu.VMEM((ATT_QB * w, ATT_GW), F32)] * 2,
        compiler_params=_params(("parallel",)), name=name,
    )(q, k, k, v, v, do, lse, dd, q, do, lse, dd)


def _to_perm(a, d):
    if d == 1:
        return a
    l, c = a.shape
    return a.reshape(l // d, d, c).transpose(1, 0, 2).reshape(l, c)


def _from_perm(a, d):
    if d == 1:
        return a
    l, c = a.shape
    return a.reshape(d, l // d, c).transpose(1, 0, 2).reshape(l, c)


def _mem_probs(qh, kh):
    s = _nt_dot(qh, kh) * (MEM_E ** -0.5)
    e = jnp.exp(s - jnp.max(s, axis=-1, keepdims=True))
    return e / jnp.sum(e, axis=-1, keepdims=True)


def _mem_fwd(mq, kv, name, tm=512):
    l, nm = mq.shape[0], kv.shape[0]

    def body(q_ref, kv_ref, o_ref):
        for h in range(MEM_H):
            sl = slice(h * MEM_E, (h + 1) * MEM_E)
            p = _mem_probs(q_ref[:, sl], kv_ref[:, sl])
            o_ref[:, sl] = _nn_dot(p, kv_ref[:, MEM_W + h * MEM_E:MEM_W + (h + 1) * MEM_E]).astype(o_ref.dtype)

    return pl.pallas_call(
        body, grid=(l // tm,),
        in_specs=[pl.BlockSpec((tm, MEM_W), lambda i: (i, 0)), pl.BlockSpec((nm, 2 * MEM_W), lambda i: (0, 0))],
        out_specs=pl.BlockSpec((tm, MEM_W), lambda i: (i, 0)),
        out_shape=jax.ShapeDtypeStruct((l, MEM_W), BF16),
        compiler_params=_params(("parallel",)), name=name,
    )(mq, kv)


def _mem_bwd(mq, kv, dmo, name, tm=512):
    l, nm = mq.shape[0], kv.shape[0]
    scale = MEM_E ** -0.5

    def body(q_ref, kv_ref, do_ref, dq_ref, dkv_ref):
        @pl.when(pl.program_id(0) == 0)
        def _():
            dkv_ref[...] = jnp.zeros_like(dkv_ref)

        for h in range(MEM_H):
            sl = slice(h * MEM_E, (h + 1) * MEM_E)
            vsl = slice(MEM_W + h * MEM_E, MEM_W + (h + 1) * MEM_E)
            qh, kh, vh, doh = q_ref[:, sl], kv_ref[:, sl], kv_ref[:, vsl], do_ref[:, sl]
            p = _mem_probs(qh, kh)
            dp = _nt_dot(doh, vh)
            ds = p * (dp - jnp.sum(dp * p, axis=-1, keepdims=True)) * scale
            dq_ref[:, sl] = _nn_dot(ds, kh).astype(dq_ref.dtype)
            dkv_ref[:, sl] += _tn_dot(ds, qh)
            dkv_ref[:, vsl] += _tn_dot(p, doh)

    row = pl.BlockSpec((tm, MEM_W), lambda i: (i, 0))
    full = pl.BlockSpec((nm, 2 * MEM_W), lambda i: (0, 0))
    return pl.pallas_call(
        body, grid=(l // tm,), in_specs=[row, full, row], out_specs=[row, full],
        out_shape=[jax.ShapeDtypeStruct((l, MEM_W), BF16), jax.ShapeDtypeStruct((nm, 2 * MEM_W), F32)],
        compiler_params=_params(("arbitrary",)), name=name,
    )(mq, kv, dmo)


def _discretize(lam_re, lam_im, log_dt, b_re, b_im):
    dt = jnp.exp(log_dt)[:, None]
    mag = jnp.exp(lam_re * dt)
    a_re, a_im = mag * jnp.cos(lam_im * dt), mag * jnp.sin(lam_im * dt)
    nr, ni = a_re - 1.0, a_im
    den = lam_re * lam_re + lam_im * lam_im
    coef_re = (nr * lam_re + ni * lam_im) / den
    coef_im = (ni * lam_re - nr * lam_im) / den
    bb_re = coef_re[..., None] * b_re - coef_im[..., None] * b_im
    bb_im = coef_re[..., None] * b_im + coef_im[..., None] * b_re
    return a_re, a_im, bb_re, bb_im


def _bd_in(bb):
    return jnp.einsum("gph,gk->ghkp", bb, jnp.eye(SSM_G, dtype=bb.dtype)).reshape(SSM_W, SSM_S)


def _bd_diag(x):
    gb = SSM_G // SSM_BD
    t = x.reshape(SSM_BD, gb, SSM_H, gb, SSM_P)
    return jnp.einsum("bghgp->bghp", t).reshape(SSM_G, SSM_H, SSM_P)


_ANY = pl.BlockSpec(memory_space=pl.ANY)
_MESH = pl.DeviceIdType.MESH


def _allgather(x, name):
    def body(x_ref, out_ref, send_sems, recv_sems, local_sem):
        mx, my, mc = lax.axis_index("x"), lax.axis_index("y"), lax.axis_index("c")
        me, sibling = (mx, my, mc), (mx, my, 1 - mc)
        chips = [(1 - mx, my), (mx, 1 - my), (1 - mx, 1 - my)]

        def blk(px, py, pc):
            return out_ref.at[4 * px + 2 * py + pc]

        def copy(k, block, to, src=None):
            return pltpu.make_async_remote_copy(
                src_ref=blk(*block) if src is None else src, dst_ref=blk(*block),
                send_sem=send_sems.at[k], recv_sem=recv_sems.at[k], device_id=to, device_id_type=_MESH)

        mine = pltpu.make_async_copy(x_ref, blk(*me), local_sem)
        mine.start()
        first = [copy(0, me, sibling, src=x_ref)]
        first += [copy(1 + j, me, (*chip, mc), src=x_ref) for j, chip in enumerate(chips)]
        for cp in first:
            cp.start()
        passed = [copy(4 + j, (*chip, mc), sibling) for j, chip in enumerate(chips)]
        for j, chip in enumerate(chips):
            copy(1 + j, (*chip, mc), me).wait_recv()
            passed[j].start()
        copy(0, sibling, me).wait_recv()
        for j, chip in enumerate(chips):
            copy(4 + j, (*chip, 1 - mc), me).wait_recv()
        for cp in first + passed:
            cp.wait_send()
        mine.wait()

    return pl.pallas_call(
        body, out_shape=jax.ShapeDtypeStruct((N_DEV,) + x.shape, x.dtype), in_specs=[_ANY], out_specs=_ANY,
        scratch_shapes=[pltpu.SemaphoreType.DMA((7,)), pltpu.SemaphoreType.DMA((7,)), pltpu.SemaphoreType.DMA],
        name=name,
    )(x)


def _pair_exchange(g, name):
    def body(g_ref, out_ref, send_sems, recv_sems):
        mx, my, mc = lax.axis_index("x"), lax.axis_index("y"), lax.axis_index("c")
        copies = [pltpu.make_async_remote_copy(
            src_ref=g_ref.at[2 * k + (1 - mc)], dst_ref=out_ref.at[k], send_sem=send_sems.at[k],
            recv_sem=recv_sems.at[k], device_id=(mx, my, 1 - mc), device_id_type=_MESH) for k in range(4)]
        for cp in copies:
            cp.start()
        for cp in copies:
            cp.wait()

    return pl.pallas_call(
        body, out_shape=jax.ShapeDtypeStruct((4,) + g.shape[1:], g.dtype), in_specs=[_ANY], out_specs=_ANY,
        scratch_shapes=[pltpu.SemaphoreType.DMA((4,)), pltpu.SemaphoreType.DMA((4,))], name=name,
    )(g)


_HBM = pl.BlockSpec(memory_space=pltpu.HBM)
_SEM = pl.BlockSpec(memory_space=pltpu.SEMAPHORE)
_EFFECT = pltpu.SideEffectType.DATAFLOW_SIDE_EFFECTING
_TOKEN = jax.ShapeDtypeStruct((8, 128), F32)


def _peer(rel):
    pos = (lax.axis_index("x"), lax.axis_index("y"), lax.axis_index("c"))
    return tuple(1 - p if (rel >> (2 - i)) & 1 else p for i, p in enumerate(pos))


def _index_of(dev):
    return 4 * dev[0] + 2 * dev[1] + dev[2]


def _split_copies(src_ref, land_ref, sems, plan):
    n = len(plan)
    return [pltpu.make_async_remote_copy(
        src_ref=src_ref if s is None else src_ref.at[s], dst_ref=land_ref.at[d], send_sem=sems[k],
        recv_sem=sems[n + k], device_id=peer, device_id_type=_MESH) for k, (s, d, peer) in enumerate(plan)]


def _split_start(src, n_land, plan_fn, after, name):
    blk = src.shape[-2:]
    land = lax.empty((n_land,) + blk, src.dtype)
    n = len(plan_fn())

    def body(src_ref, land_ref, after_ref, *outs):
        for cp in _split_copies(src_ref, land_ref, outs[:2 * n], plan_fn()):
            cp.start()
        outs[2 * n + 2][...] = jnp.zeros_like(outs[2 * n + 2])

    res = pl.pallas_call(
        body, name=name,
        out_shape=(pltpu.SemaphoreType.DMA(()),) * (2 * n)
        + (pltpu.HBM(src.shape, src.dtype), pltpu.HBM(land.shape, land.dtype), _TOKEN),
        in_specs=(_HBM, _HBM, _ANY),
        out_specs=(_SEM,) * (2 * n) + (_HBM, _HBM, pl.BlockSpec(memory_space=pltpu.VMEM)),
        input_output_aliases={0: 2 * n, 1: 2 * n + 1},
        compiler_params=pltpu.CompilerParams(has_side_effects=_EFFECT),
    )(pltpu.with_memory_space_constraint(src, pltpu.HBM), pltpu.with_memory_space_constraint(land, pltpu.HBM), after)
    return res[:2 * n], res[2 * n], res[2 * n + 1], res[2 * n + 2]


def _split_wait(sems, src, land, plan_fn, after, name):
    n = len(sems) // 2

    def body(src_ref, land_ref, *rest):
        for cp in _split_copies(src_ref, land_ref, rest[:2 * n], plan_fn()):
            cp.wait_send()
            cp.wait_recv()

    return pl.pallas_call(
        body, name=name,
        out_shape=(pltpu.HBM(src.shape, src.dtype), pltpu.HBM(land.shape, land.dtype)),
        in_specs=(_HBM, _HBM) + (_SEM,) * (2 * n) + (_ANY,), out_specs=(_HBM, _HBM),
        input_output_aliases={0: 0, 1: 1},
        compiler_params=pltpu.CompilerParams(has_side_effects=_EFFECT),
    )(src, land, *sems, after)


def _gather_plan():
    me = _index_of(_peer(0))
    return [(None, me, _peer(rel)) for rel in range(1, N_DEV)]


def _gather_wait_plan():
    return [(None, _index_of(_peer(rel)), _peer(rel)) for rel in range(1, N_DEV)]


def _chip_plan():
    return [(_index_of(_peer(rel)) // 2, j, _peer(rel)) for j, rel in enumerate((4, 2, 6))]


def _pair_sum(g, t1, my_c, name, tr):
    _, r, c = g.shape

    def body(c_ref, g_ref, t_ref, o_ref, ob_ref):
        s = g_ref[...] + t_ref[...]
        o_ref[...] = s
        ob_ref[...] = s.astype(BF16)

    blk = pl.BlockSpec((None, tr, c), lambda k, i, cr: (k, i, 0))
    return pl.pallas_call(
        body,
        grid_spec=pltpu.PrefetchScalarGridSpec(
            num_scalar_prefetch=1, grid=(4, r // tr),
            in_specs=[pl.BlockSpec((None, tr, c), lambda k, i, cr: (2 * k + cr[0], i, 0)), blk],
            out_specs=[blk, blk]),
        out_shape=[jax.ShapeDtypeStruct((4, r, c), F32), jax.ShapeDtypeStruct((4, r, c), BF16)],
        compiler_params=_params(("parallel", "parallel")), name=name,
    )(my_c, g, t1)


def _adam_math(g, w, m, v):
    m = ADAM_B1 * m + (1.0 - ADAM_B1) * g
    v = ADAM_B2 * v + (1.0 - ADAM_B2) * (g * g)
    m_hat = m / (1.0 - ADAM_B1 ** ADAM_STEP)
    v_hat = v / (1.0 - ADAM_B2 ** ADAM_STEP)
    delta = -ADAM_LR * (m_hat / (jnp.sqrt(v_hat) + ADAM_EPS) + ADAM_WD * w)
    return delta, m, v


def _adam_big(p, t2, my_chip, w, m, v, name, tr):
    r, c = w.shape

    def body(k_ref, p_ref, t0_ref, t1_ref, t2_ref, w_ref, m_ref, v_ref, g_out, d_out, m_out, v_out):
        g = ((p_ref[...] + t0_ref[...].astype(F32)) + t1_ref[...].astype(F32)) + t2_ref[...].astype(F32)
        d, mn, vn = _adam_math(g, w_ref[...], m_ref[...], v_ref[...])
        g_out[...], d_out[...], m_out[...], v_out[...] = g, d, mn, vn

    flat = pl.BlockSpec((tr, c), lambda i, kr: (i, 0))

    def rel(j):
        return pl.BlockSpec((None, tr, c), lambda i, kr: (j, i, 0))

    return pl.pallas_call(
        body,
        grid_spec=pltpu.PrefetchScalarGridSpec(
            num_scalar_prefetch=1, grid=(r // tr,),
            in_specs=[pl.BlockSpec((None, tr, c), lambda i, kr: (kr[0], i, 0)), rel(0), rel(1), rel(2), flat, flat, flat],
            out_specs=[flat] * 4),
        out_shape=[jax.ShapeDtypeStruct((r, c), F32)] * 4,
        compiler_params=_params(("parallel",)), name=name,
    )(my_chip, p, t2, t2, t2, w, m, v)


def _sum8(g8, name):
    _, r, c = g8.shape

    def body(g_ref, o_ref):
        acc = g_ref[0]
        for j in range(1, N_DEV):
            acc = acc + g_ref[j]
        o_ref[...] = acc

    return pl.pallas_call(
        body, grid=(1,), in_specs=[pl.BlockSpec((N_DEV, r, c), lambda i: (0, 0, 0))],
        out_specs=pl.BlockSpec((r, c), lambda i: (0, 0)), out_shape=jax.ShapeDtypeStruct((r, c), F32),
        compiler_params=_params(("arbitrary",)), name=name,
    )(g8)


def _adam_small(g, w, m, v, name):
    def fn(r, b):
        return list(_adam_math(*r)), []
    c = g.shape[1]
    return _ew(fn, [g, w, m, v], [], [(c, F32)] * 3, [], name=name, tm=g.shape[0])


def _pack(arrs, pad_rows=8):
    flat = jnp.concatenate([a.reshape(-1) for a in arrs])
    n = flat.shape[0]
    q = PACK_C * pad_rows
    tot = -(-n // q) * q
    if tot != n:
        flat = jnp.concatenate([flat, jnp.zeros((tot - n,), flat.dtype)])
    return flat.reshape(tot // PACK_C, PACK_C)


def _unpack(buf, shapes):
    flat = buf.reshape(-1)
    out, off = [], 0
    for s in shapes:
        n = int(np.prod(s))
        out.append(flat[off:off + n].reshape(s))
        off += n
    return out


GROUPS = (("w_in", "w_mem_kv", "w_o"),
          ("w_glu", "w_ssm_br", "w_mem_br", "w_attn_br"),
          ("w_up", "w_down"))
GROUP_TR = (528, 384, 512)
MLP_GROUP = 2
ATTN_BR_FOLD = 2


def _stored_shape(name):
    r, c, ax = BIG_SHAPE[name]
    rows, cols = (r // N_DEV, c) if ax == 0 else (c // N_DEV, r)
    return (rows // ATTN_BR_FOLD, cols * ATTN_BR_FOLD) if name == "w_attn_br" else (rows, cols)


def _stored(shard, name):
    a = shard[0].T if BIG_SHAPE[name][2] == 1 else shard[0]
    return a.reshape(_stored_shape(name))


def _unstored(a, name):
    r, c, ax = BIG_SHAPE[name]
    if ax == 0:
        return a.reshape(1, r // N_DEV, c)
    return a.reshape(c // N_DEV, r).T[None]


def _pack_group(d, names):
    return jnp.concatenate([_stored(d[n], n) for n in names], axis=0)


def _split_group(buf, names):
    out, off = {}, 0
    for n in names:
        rows = _stored_shape(n)[0]
        out[n] = buf[..., off:off + rows, :]
        off += rows
    return out


def _full_stored(stacked, name):
    r, c, ax = BIG_SHAPE[name]
    return stacked.reshape((r, c) if ax == 0 else (c, r))


def _stacked_stored(full, name):
    return full.reshape((N_DEV,) + _stored_shape(name))


def _gelu_parts(x):
    c0, c1 = math.sqrt(2.0 / math.pi), 0.044715
    th = jnp.tanh(c0 * (x + c1 * x * x * x))
    return th, c0, c1


def _local_step(x, mem, tgt, wb, sp, mlp_weights, grads_ready):
    l = x.shape[0]
    w_a, w_g = wb["w_in"][:ZA_W], wb["w_in"][ZA_W:]

    a_re, a_im, bb_re, bb_im = _discretize(sp["ssm_lambda_re"], sp["ssm_lambda_im"], sp["ssm_log_dt"],
                                           sp["ssm_b_re"], sp["ssm_b_im"])
    a_pair = jnp.stack([a_re.reshape(1, SSM_S), a_im.reshape(1, SSM_S)])
    a_conj = jnp.stack([a_re.reshape(1, SSM_S), -a_im.reshape(1, SSM_S)])
    b_re_t, b_im_t = _bd_in(bb_re).astype(BF16), _bd_in(bb_im).astype(BF16)
    c_re_t = _bd_in(sp["ssm_c_re"].transpose(0, 2, 1)).astype(BF16)
    c_im_t = (-_bd_in(sp["ssm_c_im"].transpose(0, 2, 1))).astype(BF16)
    d_row = sp["ssm_d"].reshape(1, SSM_W)

    n1 = _rms_fwd(x, sp["norm1_g"], "rms1")
    za = _mm(n1, w_a, [BF16], tb=True, name="in_proj_a", tn=1664)
    zg = _mm(n1, w_g, [BF16], tb=True, name="in_proj_g")
    u = za[:, :SSM_W]
    mq = za[:, ZA_W - MEM_W:]

    u_s = _scan_order(u)
    s_all = _ssm_scan(u_s, b_re_t, b_im_t, a_pair, reverse=False, name="ssm_scan_fwd")
    ys = _time_order(_mm((s_all, 0), c_re_t, [F32], tb=True, pair2=((s_all, 1), c_im_t), bd=SSM_BD, name="ssm_cs"))

    def gelu_fn(r, b):
        y0 = r[0] + b[0] * r[1].astype(F32)
        th, _, _ = _gelu_parts(y0)
        return [y0, 0.5 * y0 * (1.0 + th)], []
    y0, y1 = _ew(gelu_fn, [ys, u], [d_row], [(SSM_W, F32), (SSM_W, BF16)], [], name="ssm_gelu", tm=512)

    def glu_epi(acc, y1t, bg):
        t = acc + bg
        return t, y1t.astype(F32) * _sigmoid(t)
    t_glu, y2 = _mm(y1, wb["w_glu"], [F32, BF16], epi=glu_epi, mn=[y1], rows=[sp["b_glu"]], name="ssm_glu")
    br_ssm = _mm(y2, wb["w_ssm_br"], [BF16], tb=True, name="ssm_br")

    qkv_p, o_g, lse_g = [], [], []
    for g, d in enumerate(DILATIONS):
        nb = l // d // ATT_WIN
        cols = [za[:, SSM_W + (3 * j + g) * ATT_GW: SSM_W + (3 * j + g + 1) * ATT_GW] for j in range(3)]
        qp, kp, vp = [_to_perm(cc, d) for cc in cols]
        qkv_p.append((qp, kp, vp))
        og, lg = _attn_fwd(qp, kp, vp, nb, "attn_fwd%d" % g)
        o_g.append(_from_perm(og, d))
        lse_g.append(_from_perm(lg, d))

    def merge_fn(r, b):
        o0, o1, o2, l0, l1, l2 = r
        mx = jnp.maximum(jnp.maximum(l0, l1), l2)
        e0, e1, e2 = jnp.exp(l0 - mx), jnp.exp(l1 - mx), jnp.exp(l2 - mx)
        tot = e0 + e1 + e2
        return [(e0 * o0 + e1 * o1 + e2 * o2) / tot, mx + jnp.log(tot)], []
    o_att, lse_tot = _ew(merge_fn, o_g + lse_g, [], [(ATT_GW, F32), (ATT_GW, F32)], [], name="attn_merge", tm=512)
    br_attn = _mm(o_att, wb["w_attn_br"], [BF16], tb=True, name="attn_br")

    mn = _rms_fwd(mem, sp["mem_norm_g"], "rms_mem")
    kv = _mm(mn, wb["w_mem_kv"], [BF16], name="mem_kv")
    mo = _mem_fwd(mq, kv, "mem_attn_fwd")
    br_mem = _mm(mo, wb["w_mem_br"], [BF16], tb=True, name="mem_br")

    def gate_fn(r, b):
        zgt, b0, b1, b2 = [t.astype(F32) for t in r]
        gt = _sigmoid(zgt + b[0])
        return [gt[:, :D_MODEL] * b0 + gt[:, D_MODEL:2 * D_MODEL] * b1 + gt[:, 2 * D_MODEL:] * b2], []
    merged = _ew(gate_fn, [zg, br_ssm, br_attn, br_mem], [sp["b_gate"]], [(D_MODEL, BF16)], [], name="gate_merge")[0]
    def o_epi(acc, xt, g2):
        hv = acc + xt
        rs = lax.rsqrt(jnp.mean(hv * hv, axis=-1, keepdims=True) + RMS_EPS)
        return hv, hv * rs * g2
    h1, n2 = _mm(merged, wb["w_o"], [F32, BF16], epi=o_epi, mn=[x], rows=[sp["norm2_g"]], tm=512, name="o_proj")

    def up_epi(acc):
        ra = jnp.maximum(acc, 0.0)
        return ra * ra, ra
    wm = mlp_weights(n2)
    f_act, r_act = _mm(n2, wm["w_up"], [BF16, BF16], tb=True, epi=up_epi, name="mlp_up")
    def down_epi(acc, ht, tv, gf):
        hv = acc + ht
        rs = lax.rsqrt(jnp.mean(hv * hv, axis=-1, keepdims=True) + RMS_EPS)
        err = hv * rs * gf - tv
        dh, dgf = _rms_bwd_tile(hv, err * (1.0 / D_MODEL), gf)
        return dh, dgf, _colsum(err * err) * (0.5 / D_MODEL)
    dh2, d_final_g, loss_cols = _mm(f_act, wm["w_down"], [F32], epi=down_epi, mn=[h1, tgt], rows=[sp["final_g"]],
                                    n_sums=2, tm=512, name="mlp_down")
    loss = jnp.sum(loss_cols, axis=1, keepdims=True)

    gw, gs = {}, {"final_g": d_final_g}
    d_act = _mm(dh2, wm["w_down"], [BF16], tb=True, epi=lambda acc, ra: (acc * 2.0 * ra.astype(F32),), mn=[r_act],
                name="mlp_down_dx")
    dw_down = _mm(f_act, dh2, [F32], ta=True, name="mlp_down_dw")
    dw_up = _mm(d_act, n2, [F32], ta=True, name="mlp_up_dw")
    token = grads_ready(MLP_GROUP, {"w_up": dw_up, "w_down": dw_down})
    def up_dx_epi(acc, ht, dht, g2):
        dx, dg = _rms_bwd_tile(ht, acc, g2)
        return dx + dht, dg
    dh1, gs["norm2_g"] = _mm(d_act, wm["w_up"], [F32], epi=up_dx_epi, mn=[h1, dh2],
                             rows=[sp["norm2_g"] + token[:1, :1]], n_sums=1, tm=512, name="mlp_up_dx")
    dmerged = _mm(dh1, wb["w_o"], [F32], tb=True, name="o_proj_dx")
    gw["w_o"] = _mm(merged, dh1, [F32], ta=True, name="o_proj_dw")

    def gate_bwd_fn(r, b):
        dm, zgt, b0, b1, b2 = [t.astype(F32) for t in r]
        gt = _sigmoid(zgt + b[0])
        g0, g1, g2 = gt[:, :D_MODEL], gt[:, D_MODEL:2 * D_MODEL], gt[:, 2 * D_MODEL:]
        dzg = jnp.concatenate([dm * b0 * g0 * (1.0 - g0), dm * b1 * g1 * (1.0 - g1), dm * b2 * g2 * (1.0 - g2)], axis=1)
        return [dm * g0, dm * g1, dm * g2, dzg], [_colsum(dzg)]
    dbr_ssm, dbr_attn, dbr_mem, dzg, gs["b_gate"] = _ew(
        gate_bwd_fn, [dmerged, zg, br_ssm, br_attn, br_mem], [sp["b_gate"]],
        [(D_MODEL, BF16)] * 3 + [(ZG_W, BF16)], [ZG_W], name="gate_bwd")

    gw["w_ssm_br"] = _mm(dbr_ssm, y2, [F32], ta=True, name="ssm_br_dw")
    dy2 = _mm(dbr_ssm, wb["w_ssm_br"], [F32], name="ssm_br_dx")

    def glu_bwd_fn(r, b):
        dy, y1t, tt = r
        sg = _sigmoid(tt)
        dt = dy * y1t.astype(F32) * sg * (1.0 - sg)
        return [dt, dy * sg], [_colsum(dt)]
    dt_glu, dy1a, gs["b_glu"] = _ew(glu_bwd_fn, [dy2, y1, t_glu], [], [(SSM_W, BF16), (SSM_W, F32)], [SSM_W],
                                    name="ssm_glu_bwd", tm=512)
    gw["w_glu"] = _mm(y1, dt_glu, [F32], ta=True, name="ssm_glu_dw")

    def gelu_bwd_epi(acc, dy1t, y0t):
        th, c0, c1 = _gelu_parts(y0t)
        dg = 0.5 * (1.0 + th) + 0.5 * y0t * (1.0 - th * th) * c0 * (1.0 + 3.0 * c1 * y0t * y0t)
        return ((acc + dy1t) * dg,)
    dy0 = _mm(dt_glu, wb["w_glu"], [F32], tb=True, epi=gelu_bwd_epi, mn=[dy1a, y0], name="ssm_glu_dx")
    gs["ssm_d"] = _ew(lambda r, b: ([], [_colsum(r[0] * r[1].astype(F32))]), [dy0, u], [], [], [SSM_W],
                      name="ssm_dd", tm=512)[0]
    dy0_s = _scan_order(dy0)
    lam, da, d_b, d_c = _ssm_scan(dy0_s, c_re_t, c_im_t, a_conj, reverse=True, s_fwd=s_all, u=u_s,
                                  name="ssm_scan_bwd")
    du = _time_order(_mm((lam, 0), b_re_t, [BF16], tb=True, pair2=((lam, 1), b_im_t),
                         epi=lambda acc, dyt, dr: (acc + dyt * dr,), mn=[dy0_s], rows=[d_row], bd=SSM_BD, name="ssm_bu_dx"))
    gs["a_re"], gs["a_im"] = da[0], da[1]
    gs["bb_re"], gs["bb_im"] = _bd_diag(d_b[0]).transpose(0, 2, 1), _bd_diag(d_b[1]).transpose(0, 2, 1)
    gs["ssm_c_re"], gs["ssm_c_im"] = _bd_diag(d_c[0]), -_bd_diag(d_c[1])

    gw["w_attn_br"] = _mm(dbr_attn, o_att, [F32], ta=True, name="attn_br_dw")

    def do_epi(acc, ot):
        prod = acc * ot
        head = lax.broadcasted_iota(jnp.int32, prod.shape, 1) // ATT_E
        dd = jnp.zeros_like(prod)
        for h in range(ATT_HG):
            dd = jnp.where(head == h, jnp.sum(jnp.where(head == h, prod, 0.0), axis=1, keepdims=True), dd)
        return acc, dd
    do_att, dd_att = _mm(dbr_attn, wb["w_attn_br"], [BF16, F32], epi=do_epi, mn=[o_att], name="attn_br_dx")
    dq_l, dk_l, dv_l = [], [], []
    for g, d in enumerate(DILATIONS):
        nb = l // d // ATT_WIN
        qp, kp, vp = qkv_p[g]
        dq, dk, dv = _attn_bwd(qp, kp, vp, _to_perm(do_att, d), _to_perm(lse_tot, d), _to_perm(dd_att, d),
                               nb, "attn_bwd%d" % g)
        dq_l.append(_from_perm(dq, d))
        dk_l.append(_from_perm(dk, d))
        dv_l.append(_from_perm(dv, d))

    gw["w_mem_br"] = _mm(dbr_mem, mo, [F32], ta=True, name="mem_br_dw")
    dmo = _mm(dbr_mem, wb["w_mem_br"], [BF16], name="mem_br_dx")
    dmq, dkv = _mem_bwd(mq, kv, dmo, "mem_attn_bwd")
    gw["w_mem_kv"] = _mm(mn, dkv, [F32], ta=True, name="mem_kv_dw")
    dmn = _mm(dkv, wb["w_mem_kv"], [F32], tb=True, name="mem_kv_dx")
    token = grads_ready(1, gw)
    gs["mem_norm_g"] = _rms_bwd(mem, dmn, None, sp["mem_norm_g"] + token[:1, :1], "rms_mem_bwd")[1]

    dza = jnp.concatenate([du] + dq_l + dk_l + dv_l + [dmq], axis=1)
    dn_a = _mm(dza, w_a, [F32], name="in_proj_a_dx", tk=1664)
    dw_a = _mm(dza, n1, [F32], ta=True, name="in_proj_a_dw", tm=1664)
    dw_g = _mm(dzg, n1, [F32], ta=True, name="in_proj_g_dw")
    gw["w_in"] = jnp.concatenate([dw_a, dw_g], axis=0)
    token = grads_ready(0, gw)
    def in_dx_epi(acc, pt, xt, dht, g1):
        dx, dg = _rms_bwd_tile(xt, acc + pt, g1)
        return dx + dht, dg
    grad_x, gs["norm1_g"] = _mm(dzg, w_g, [F32], epi=in_dx_epi, mn=[dn_a, x, dh1],
                                rows=[sp["norm1_g"] + token[:1, :1]], n_sums=1, tm=512, name="in_proj_g_dx")
    return loss, grad_x, gs


_SMALL_GRAD_ORDER = ("norm1_g", "mem_norm_g", "b_gate", "a_re", "a_im", "bb_re", "bb_im", "ssm_c_re", "ssm_c_im",
                     "ssm_d", "b_glu", "norm2_g", "final_g")


def kernel(x, mem, norm1_g, mem_norm_g, w_in, b_gate, ssm_lambda_re, ssm_lambda_im, ssm_log_dt, ssm_b_re, ssm_b_im, ssm_c_re, ssm_c_im, ssm_d, w_glu, b_glu, w_ssm_br, w_attn_br, w_mem_kv, w_mem_br, w_o, norm2_g, w_up, w_down, final_g, loss_target, m_norm1_g, m_mem_norm_g, m_w_in, m_b_gate, m_ssm_lambda_re, m_ssm_lambda_im, m_ssm_log_dt, m_ssm_b_re, m_ssm_b_im, m_ssm_c_re, m_ssm_c_im, m_ssm_d, m_w_glu, m_b_glu, m_w_ssm_br, m_w_attn_br, m_w_mem_kv, m_w_mem_br, m_w_o, m_norm2_g, m_w_up, m_w_down, m_final_g, v_norm1_g, v_mem_norm_g, v_w_in, v_b_gate, v_ssm_lambda_re, v_ssm_lambda_im, v_ssm_log_dt, v_ssm_b_re, v_ssm_b_im, v_ssm_c_re, v_ssm_c_im, v_ssm_d, v_w_glu, v_b_glu, v_w_ssm_br, v_w_attn_br, v_w_mem_kv, v_w_mem_br, v_w_o, v_norm2_g, v_w_up, v_w_down, v_final_g):
    args = dict(locals())
    w = {n: args[n] for n in ALL_W}
    m = {n: args["m_" + n] for n in ALL_W}
    v = {n: args["v_" + n] for n in ALL_W}
    my_c = lax.axis_index("c").astype(jnp.int32).reshape(1)
    my_chip = (2 * lax.axis_index("x") + lax.axis_index("y")).astype(jnp.int32).reshape(1)

    w_pack = [_pack_group(w, names) for names in GROUPS]
    wb = {}
    for gi in (1, 0):
        w_all = _allgather(w_pack[gi].astype(BF16), "allgather_weights%d" % gi)
        for n, part in _split_group(w_all, GROUPS[gi]).items():
            wb[n] = _full_stored(part, n)
    mlp_names = GROUPS[MLP_GROUP]
    g_sems, g_src, g_land, g_token = _split_start(w_pack[MLP_GROUP].astype(BF16), N_DEV, _gather_plan, w_all,
                                                  "mlp_weights_gather_start")

    def mlp_weights(after):
        src, land = _split_wait(g_sems, g_src, g_land, _gather_wait_plan, after, "mlp_weights_gather_wait")
        my_index = (4 * lax.axis_index("x") + 2 * lax.axis_index("y") + lax.axis_index("c")).astype(jnp.int32)
        zero = jnp.zeros((), jnp.int32)
        w_mlp = lax.dynamic_update_slice(land, src[None], (my_index, zero, zero))
        return {n: _full_stored(part, n) for n, part in _split_group(w_mlp, mlp_names).items()}

    pending = {}

    def grads_ready(gi, grads):
        g_pack = jnp.concatenate([_stacked_stored(grads[n], n) for n in GROUPS[gi]], axis=1)
        t1 = _pair_exchange(g_pack, "grad_pair_exchange%d" % gi)
        p_sum, p_bf = _pair_sum(g_pack, t1, my_c, "grad_pair_sum%d" % gi, GROUP_TR[gi])
        sems, src, land, token = _split_start(p_bf, 3, _chip_plan, p_sum, "grad_chip_exchange_start%d" % gi)
        pending[gi] = (p_sum, sems, src, land)
        return token

    sp = {
        "norm1_g": norm1_g + g_token[:1, :1], "mem_norm_g": mem_norm_g, "b_gate": b_gate, "b_glu": b_glu, "norm2_g": norm2_g,
        "final_g": final_g.reshape(1, D_MODEL),
        "ssm_lambda_re": ssm_lambda_re[0], "ssm_lambda_im": ssm_lambda_im[0], "ssm_log_dt": ssm_log_dt[0],
        "ssm_b_re": ssm_b_re[0], "ssm_b_im": ssm_b_im[0], "ssm_c_re": ssm_c_re[0], "ssm_c_im": ssm_c_im[0],
        "ssm_d": ssm_d[0],
    }
    loss, grad_x, gs = _local_step(x[0], mem[0], loss_target[0], wb, sp, mlp_weights, grads_ready)
    loss = lax.psum(loss[0, 0], ("x", "y", "c"))

    big = [{}, {}, {}, {}]
    for gi, names in enumerate(GROUPS):
        p_sum, sems, src, land = pending[gi]
        t2 = _split_wait(sems, src, land, _chip_plan, grad_x, "grad_chip_exchange_wait%d" % gi)[1]
        outs = _adam_big(p_sum, t2, my_chip, w_pack[gi], _pack_group(m, names), _pack_group(v, names),
                         "adam_big%d" % gi, GROUP_TR[gi])
        for kind, buf in enumerate(outs):
            for n, part in _split_group(buf, names).items():
                big[kind][n] = _unstored(part, n)

    sg_shapes = [gs[n].shape for n in _SMALL_GRAD_ORDER]
    sg_all = _allgather(_pack([gs[n] for n in _SMALL_GRAD_ORDER]), "allgather_small_grads")
    sg = dict(zip(_SMALL_GRAD_ORDER, _unpack(_sum8(sg_all, "sum_small_grads"), sg_shapes)))
    _, disc_vjp = jax.vjp(_discretize, sp["ssm_lambda_re"], sp["ssm_lambda_im"], sp["ssm_log_dt"],
                          sp["ssm_b_re"], sp["ssm_b_im"])
    d_lre, d_lim, d_ldt, d_bre, d_bim = disc_vjp((sg["a_re"].reshape(SSM_G, SSM_P), sg["a_im"].reshape(SSM_G, SSM_P),
                                                  sg["bb_re"], sg["bb_im"]))
    small_grad = {
        "norm1_g": sg["norm1_g"], "mem_norm_g": sg["mem_norm_g"], "b_gate": sg["b_gate"],
        "ssm_lambda_re": d_lre, "ssm_lambda_im": d_lim, "ssm_log_dt": d_ldt, "ssm_b_re": d_bre, "ssm_b_im": d_bim,
        "ssm_c_re": sg["ssm_c_re"], "ssm_c_im": sg["ssm_c_im"], "ssm_d": sg["ssm_d"], "b_glu": sg["b_glu"],
        "norm2_g": sg["norm2_g"], "final_g": sg["final_g"],
    }
    small_grad = {n: small_grad[n].reshape(w[n].shape) for n in SMALL}
    s_shapes = [w[n].shape for n in SMALL]
    small_out = _adam_small(_pack([small_grad[n] for n in SMALL]), _pack([w[n] for n in SMALL]),
                            _pack([m[n] for n in SMALL]), _pack([v[n] for n in SMALL]), "adam_small")
    small = [small_grad] + [dict(zip(SMALL, _unpack(b, s_shapes))) for b in small_out]

    outs = [loss, grad_x[None]]
    for kind in range(4):
        for n in ALL_W:
            outs.append(big[kind][n] if n in BIG else small[kind][n])
    return tuple(outs)
```

```python
import math

import numpy as np
import jax
import jax.numpy as jnp
from jax import lax
from jax.experimental import pallas as pl
from jax.experimental.pallas import tpu as pltpu

F32 = jnp.float32
BF16 = jnp.bfloat16
_MXU = jnp.bfloat16

D_MODEL = 1024
SSM_G, SSM_H, SSM_P = 32, 16, 64
SSM_W = SSM_G * SSM_H
SSM_S = SSM_G * SSM_P
SSM_BD = 4
ATT_E = 64
ATT_HG = 4
ATT_GW = ATT_HG * ATT_E
ATT_WIN = 128
ATT_QB = 4
DILATIONS = (1, 4, 16)
MEM_H, MEM_E = 4, 128
MEM_W = MEM_H * MEM_E
ZA_W = SSM_W + 9 * ATT_GW + MEM_W
ZG_W = 3 * D_MODEL
IN_W = ZA_W + ZG_W
RMS_EPS = 1e-6
NEG_INF = -1e30

ADAM_LR, ADAM_B1, ADAM_B2, ADAM_EPS, ADAM_WD, ADAM_STEP = 0.001, 0.9, 0.999, 1e-08, 0.01, 10

N_DEV = 8
PACK_C = 512
_VMEM_LIMIT = 56 * 1024 * 1024
SUBLANES = 16
SCAN_SEG = 128
SCAN_CHAINS = 2
SCAN_UNROLL = 4
SCAN_W = 128

BIG = ("w_in", "w_glu", "w_ssm_br", "w_attn_br", "w_mem_kv", "w_mem_br", "w_o", "w_up", "w_down")
BIG_SHAPE = {
    "w_in": (D_MODEL, IN_W, 1), "w_glu": (SSM_W, SSM_W, 0), "w_ssm_br": (SSM_W, D_MODEL, 1),
    "w_attn_br": (ATT_GW, D_MODEL, 1), "w_mem_kv": (D_MODEL, 2 * MEM_W, 0), "w_mem_br": (MEM_W, D_MODEL, 1),
    "w_o": (D_MODEL, D_MODEL, 0), "w_up": (D_MODEL, 4 * D_MODEL, 1), "w_down": (4 * D_MODEL, D_MODEL, 0),
}
SMALL = ("norm1_g", "mem_norm_g", "b_gate", "ssm_lambda_re", "ssm_lambda_im", "ssm_log_dt", "ssm_b_re",
         "ssm_b_im", "ssm_c_re", "ssm_c_im", "ssm_d", "b_glu", "norm2_g", "final_g")
ALL_W = ("norm1_g", "mem_norm_g", "w_in", "b_gate", "ssm_lambda_re", "ssm_lambda_im", "ssm_log_dt", "ssm_b_re",
         "ssm_b_im", "ssm_c_re", "ssm_c_im", "ssm_d", "w_glu", "b_glu", "w_ssm_br", "w_attn_br", "w_mem_kv",
         "w_mem_br", "w_o", "norm2_g", "w_up", "w_down", "final_g")


def _params(sem):
    return pltpu.CompilerParams(dimension_semantics=sem, vmem_limit_bytes=_VMEM_LIMIT)


def _pick(n, cap):
    if n <= cap:
        return n
    t = (cap // 128) * 128
    while n % t:
        t -= 128
    return t


def _mm(a, b, outs, *, name, ta=False, tb=False, epi=None, mn=(), rows=(), pair2=None, bd=0, n_sums=0,
        tm=1024, tn=1024, tk=2048):
    ab = [a, b] + (list(pair2) if pair2 is not None else [])
    planes = [op[1] if isinstance(op, tuple) else None for op in ab]
    ab = [op[0] if isinstance(op, tuple) else op for op in ab]
    a_shape, b_shape = ab[0].shape[-2:], ab[1].shape[-2:]
    m = a_shape[1] if ta else a_shape[0]
    k = a_shape[0] if ta else a_shape[1]
    n = b_shape[0] if tb else b_shape[1]
    assert k == (b_shape[1] if tb else b_shape[0]), (name, a_shape, b_shape)
    out_n = n
    if bd and ta:
        assert not tb
        tm, tn, tk = m // bd, n // bd, _pick(k, tk)
        grid, out_n = (bd, 1, k // tk), tn
        a_blk = ((tk, tm), lambda i, j, kk: (kk, i))
        b_blk = ((tk, tn), lambda i, j, kk: (kk, i))
        mn_spec = pl.BlockSpec((tm, tn), lambda i, j, kk: (i, 0))
    elif bd:
        tm, tn, tk = _pick(m, tm), n // bd, k // bd
        grid = (m // tm, bd, 1)
        a_blk = ((tm, tk), lambda i, j, kk: (i, j))
        b_blk = ((tn, tk) if tb else (tk, tn), lambda i, j, kk: (j, j))
        mn_spec = pl.BlockSpec((tm, tn), lambda i, j, kk: (i, j))
    else:
        tm, tn, tk = _pick(m, tm), _pick(n, tn), _pick(k, tk)
        grid = (m // tm, n // tn, k // tk)
        a_blk = ((tk, tm), lambda i, j, kk: (kk, i)) if ta else ((tm, tk), lambda i, j, kk: (i, kk))
        b_blk = ((tn, tk), lambda i, j, kk: (j, kk)) if tb else ((tk, tn), lambda i, j, kk: (kk, j))
        mn_spec = pl.BlockSpec((tm, tn), lambda i, j, kk: (i, j))

    def operand_spec(blk, plane):
        shape, imap = blk
        if plane is None:
            return pl.BlockSpec(shape, imap)
        return pl.BlockSpec((None,) + shape, lambda i, j, kk: (plane,) + imap(i, j, kk))

    ab_specs = [operand_spec(a_blk if q % 2 == 0 else b_blk, p) for q, p in enumerate(planes)]
    mn_arrays = [e[0] if isinstance(e, tuple) else e for e in mn]
    mn_specs = [pl.BlockSpec((tm, tn), lambda i, j, kk, c=e[1]: (i, c)) if isinstance(e, tuple) else mn_spec
                for e in mn]
    nk = grid[2]
    row_spec = pl.BlockSpec((1, tn), lambda i, j, kk: (0, j))
    n_ex, n_out = len(mn) + len(rows), len(outs)
    assert n_sums == 0 or (grid[1] == 1 and not bd)
    dims = (((0 if ta else 1,), (1 if tb else 0,)), ((), ()))

    def body(*refs):
        ab_refs, rest = refs[:len(ab)], refs[len(ab):]
        ex, o_refs, acc = rest[:n_ex], rest[n_ex:n_ex + n_out], rest[-1]
        s_refs = rest[n_ex + n_out:n_ex + n_out + n_sums]
        first_row_tile = pl.program_id(0) == 0
        kk = pl.program_id(2)

        @pl.when(kk == 0)
        def _():
            acc[...] = jnp.zeros_like(acc)

        for a_ref, b_ref in zip(ab_refs[0::2], ab_refs[1::2]):
            acc[...] += lax.dot_general(a_ref[...].astype(_MXU), b_ref[...].astype(_MXU), dims,
                                        preferred_element_type=F32)

        @pl.when(kk == nk - 1)
        def _():
            vals = (acc[...],) if epi is None else epi(acc[...], *[r[...] for r in ex])
            for r, v in zip(o_refs, vals):
                r[...] = v.astype(r.dtype)
            for r, v in zip(s_refs, vals[n_out:]):
                r[...] = jnp.where(first_row_tile, v, r[...] + v)

    res = pl.pallas_call(
        body, grid=grid,
        in_specs=ab_specs + mn_specs + [row_spec] * len(rows),
        out_specs=[mn_spec] * n_out + [row_spec] * n_sums,
        out_shape=[jax.ShapeDtypeStruct((m, out_n), dt) for dt in outs]
        + [jax.ShapeDtypeStruct((1, out_n), F32)] * n_sums,
        scratch_shapes=[pltpu.VMEM((tm, tn), F32)],
        compiler_params=_params(("arbitrary" if n_sums else "parallel", "parallel", "arbitrary")), name=name,
    )(*ab, *mn_arrays, *rows)
    return res[0] if n_out + n_sums == 1 else res


def _ew(fn, rows, bcs, out_rows, out_accs, *, name, tm=256):
    r = rows[0].shape[0]
    tm = min(tm, r)
    assert r % tm == 0
    nr, nb, no, na = len(rows), len(bcs), len(out_rows), len(out_accs)

    def body(*refs):
        i = pl.program_id(0)
        r_in, b_in = refs[:nr], refs[nr:nr + nb]
        o_r, o_a = refs[nr + nb:nr + nb + no], refs[nr + nb + no:]
        outs, accs = fn([x[...] for x in r_in], [x[...] for x in b_in])
        for ref, v in zip(o_r, outs):
            ref[...] = v.astype(ref.dtype)
        if na:
            @pl.when(i == 0)
            def _():
                for ref in o_a:
                    ref[...] = jnp.zeros_like(ref)

            for ref, v in zip(o_a, accs):
                ref[...] += v

    res = pl.pallas_call(
        body, grid=(r // tm,),
        in_specs=[pl.BlockSpec((tm, x.shape[1]), lambda i: (i, 0)) for x in rows]
        + [pl.BlockSpec((1, x.shape[1]), lambda i: (0, 0)) for x in bcs],
        out_specs=[pl.BlockSpec((tm, c), lambda i: (i, 0)) for c, _ in out_rows]
        + [pl.BlockSpec((1, c), lambda i: (0, 0)) for c in out_accs],
        out_shape=[jax.ShapeDtypeStruct((r, c), dt) for c, dt in out_rows]
        + [jax.ShapeDtypeStruct((1, c), F32) for c in out_accs],
        compiler_params=_params(("arbitrary",)), name=name,
    )(*rows, *bcs)
    return res


def _colsum(x):
    return jnp.sum(x, axis=0, keepdims=True)


def _sigmoid(x):
    return 1.0 / (1.0 + jnp.exp(-x))


def _rms_bwd_tile(xv, dv, g):
    rs = lax.rsqrt(jnp.mean(xv * xv, axis=-1, keepdims=True) + RMS_EPS)
    gd = dv * g
    dx = rs * gd - xv * (rs * rs * rs) * jnp.mean(gd * xv, axis=-1, keepdims=True)
    return dx, _colsum(dv * xv * rs)


def _rms_fwd(x, g, name):
    def fn(r, b):
        xv = r[0]
        rs = lax.rsqrt(jnp.mean(xv * xv, axis=-1, keepdims=True) + RMS_EPS)
        return [xv * rs * b[0]], []
    return _ew(fn, [x], [g], [(x.shape[1], BF16)], [], name=name)[0]


def _rms_bwd(x, dn, res, g, name):
    def fn(r, b):
        dx, dg = _rms_bwd_tile(r[0], r[1], b[0])
        if res is not None:
            dx = dx + r[2]
        return [dx], [dg]
    rows = [x, dn] + ([res] if res is not None else [])
    return _ew(fn, rows, [g], [(x.shape[1], F32)], [x.shape[1]], name=name)


def _scan_order(x):
    l, c = x.shape
    return x.reshape(l // (SUBLANES * SCAN_SEG), SUBLANES, SCAN_SEG, c).transpose(0, 2, 1, 3).reshape(l, c)


def _time_order(x):
    l, c = x.shape
    return x.reshape(l // (SUBLANES * SCAN_SEG), SCAN_SEG, SUBLANES, c).transpose(0, 2, 1, 3).reshape(l, c)


def _ssm_scan(x, w_re, w_im, a_pair, *, reverse, s_fwd=None, u=None, name):
    l = x.shape[0]
    seg, w = SCAN_SEG, SCAN_W
    bd_w = SSM_W // SSM_BD
    tiles_per_bd = SSM_S // SSM_BD // w
    nch = min(SCAN_CHAINS, l // (SUBLANES * seg))
    chain_rows = SUBLANES * seg
    tb = nch * chain_rows
    nt = l // tb
    with_da = s_fwd is not None
    assert reverse or not with_da

    def tt(t):
        return nt - 1 - t if reverse else t

    def body(*refs):
        if with_da:
            (x_ref, wr_ref, wi_ref, a_ref, sf_ref, sp_ref, u_ref, s_ref, da_ref, dw_ref, dx_ref,
             p_ref, c_ref, b_scr) = refs
        else:
            x_ref, wr_ref, wi_ref, a_ref, s_ref, p_ref, c_ref, b_scr = refs
        t_blk = pl.program_id(1)
        ar, ai = a_ref[0], a_ref[1]

        @pl.when(t_blk == 0)
        def _():
            def pstep(i, carry):
                pr, pi = carry
                p_ref[0, pl.ds(i, 1), :] = pr
                p_ref[1, pl.ds(i, 1), :] = pi
                return pr * ar - pi * ai, pr * ai + pi * ar

            lax.fori_loop(0, seg, pstep, (ar, ai))
            c_ref[...] = jnp.zeros_like(c_ref)
            if with_da:
                da_ref[...] = jnp.zeros_like(da_ref)
                dw_ref[...] = jnp.zeros_like(dw_ref)
                dx_ref[...] = jnp.zeros_like(dx_ref)

        xb = x_ref[...].astype(_MXU)
        b_scr[:, :w] = jnp.dot(xb, wr_ref[...], preferred_element_type=F32)
        b_scr[:, w:] = jnp.dot(xb, wi_ref[...], preferred_element_type=F32)
        arb, aib = jnp.broadcast_to(ar, (SUBLANES, w)), jnp.broadcast_to(ai, (SUBLANES, w))
        zero = jnp.zeros((SUBLANES, w), F32)

        def tile(g, step):
            return pl.ds(pl.multiple_of(g * chain_rows + step * SUBLANES, SUBLANES), SUBLANES)

        def rows(g, i):
            return tile(g, seg - 1 - i if reverse else i)

        def local_step(i, carry):
            out = []
            for g in range(nch):
                sr, si = carry[2 * g], carry[2 * g + 1]
                idx = rows(g, i)
                sr, si = arb * sr - aib * si + b_scr[idx, :w], arb * si + aib * sr + b_scr[idx, w:]
                b_scr[idx, :w] = sr
                b_scr[idx, w:] = si
                out += [sr, si]
            return tuple(out)

        def unrolled(step_fn, first):
            def trip(q, carry):
                for r in range(SCAN_UNROLL):
                    carry = step_fn(first + q * SCAN_UNROLL + r, carry)
                return carry
            return trip

        ends = lax.fori_loop(0, seg // SCAN_UNROLL, unrolled(local_step, 0), (zero,) * (2 * nch))

        a_seg_r, a_seg_i = p_ref[0, seg - 1:seg, :], p_ref[1, seg - 1:seg, :]
        cr, ci = c_ref[0], c_ref[1]
        sub = lax.broadcasted_iota(jnp.int32, (SUBLANES, w), 0)
        ins = [[zero, zero] for _ in range(nch)]
        order = [(g, k) for g in range(nch) for k in range(SUBLANES)]
        for g, k in (order[::-1] if reverse else order):
            ins[g] = [jnp.where(sub == k, cr, ins[g][0]), jnp.where(sub == k, ci, ins[g][1])]
            er, ei = ends[2 * g][k:k + 1], ends[2 * g + 1][k:k + 1]
            cr, ci = er + a_seg_r * cr - a_seg_i * ci, ei + a_seg_r * ci + a_seg_i * cr
        c_ref[0] = cr
        c_ref[1] = ci

        def fix(g, i):
            idx = rows(g, i)
            pr, pi = p_ref[0, pl.ds(i, 1), :], p_ref[1, pl.ds(i, 1), :]
            sr = b_scr[idx, :w] + pr * ins[g][0] - pi * ins[g][1]
            si = b_scr[idx, w:] + pr * ins[g][1] + pi * ins[g][0]
            s_ref.at[0][idx, :] = sr.astype(s_ref.dtype)
            s_ref.at[1][idx, :] = si.astype(s_ref.dtype)
            return sr, si

        if not with_da:
            def fix_step(i, carry):
                for g in range(nch):
                    fix(g, i)
                return carry

            lax.fori_loop(0, seg // SCAN_UNROLL, unrolled(fix_step, 0), 0)
        else:
            def adj_step(i, acc):
                acc_r, acc_i = acc
                for g in range(nch):
                    lr, li = fix(g, i)
                    prev = tile(g, seg - 2 - i)
                    fr, fi = sf_ref.at[0][prev, :].astype(F32), sf_ref.at[1][prev, :].astype(F32)
                    acc_r, acc_i = acc_r + lr * fr + li * fi, acc_i + li * fr - lr * fi
                return acc_r, acc_i

            acc = lax.fori_loop(0, seg // SCAN_UNROLL - 1, unrolled(adj_step, 0), (zero, zero))
            for i in range(seg - SCAN_UNROLL, seg - 1):
                acc = adj_step(i, acc)
            acc_r, acc_i = acc
            first_block = tt(t_blk) == 0
            for g in range(nch):
                lr, li = fix(g, seg - 1)
                seg_ends = tile(g, seg - 1)
                if g == 0:
                    pvr = jnp.where(first_block, 0.0, sp_ref[0, SUBLANES - 1:SUBLANES, :].astype(F32))
                    pvi = jnp.where(first_block, 0.0, sp_ref[1, SUBLANES - 1:SUBLANES, :].astype(F32))
                else:
                    pvr = sf_ref[0, g * chain_rows - 1:g * chain_rows, :].astype(F32)
                    pvi = sf_ref[1, g * chain_rows - 1:g * chain_rows, :].astype(F32)
                fr = jnp.where(sub == 0, pvr, pltpu.roll(sf_ref.at[0][seg_ends, :].astype(F32), 1, 0))
                fi = jnp.where(sub == 0, pvi, pltpu.roll(sf_ref.at[1][seg_ends, :].astype(F32), 1, 0))
                acc_r = acc_r + lr * fr + li * fi
                acc_i = acc_i + li * fr - lr * fi
            da_ref[0] += jnp.sum(acc_r, axis=0, keepdims=True)
            da_ref[1] += jnp.sum(acc_i, axis=0, keepdims=True)
            for plane in range(2):
                dw_ref[plane] += _tn_dot(u_ref[...], s_ref[plane])
                dx_ref[plane] += _tn_dot(xb, sf_ref[plane])

    x_spec = pl.BlockSpec((tb, bd_w), lambda j, t: (tt(t), j // tiles_per_bd))
    w_spec = pl.BlockSpec((bd_w, w), lambda j, t: (j // tiles_per_bd, j))
    d_spec = pl.BlockSpec((2, bd_w, w), lambda j, t: (0, j // tiles_per_bd, j % tiles_per_bd))
    a_spec = pl.BlockSpec((2, 1, w), lambda j, t: (0, 0, j))
    s_spec = pl.BlockSpec((2, tb, w), lambda j, t: (0, tt(t), j))
    in_specs, args = [x_spec, w_spec, w_spec, a_spec], [x, w_re, w_im, a_pair]
    out_specs, out_shape = [s_spec], [jax.ShapeDtypeStruct((2, l, SSM_S), BF16)]
    scratch = [pltpu.VMEM((2, seg, w), F32), pltpu.VMEM((2, 1, w), F32), pltpu.VMEM((tb, 2 * w), F32)]
    if with_da:
        in_specs += [s_spec, pl.BlockSpec((2, SUBLANES, w),
                                          lambda j, t: (0, jnp.maximum(tt(t) * (tb // SUBLANES) - 1, 0), j)),
                     x_spec]
        args += [s_fwd, s_fwd, u]
        out_specs += [a_spec, d_spec, d_spec]
        out_shape += ([jax.ShapeDtypeStruct((2, 1, SSM_S), F32)]
                      + [jax.ShapeDtypeStruct((2, SSM_W, SSM_S // SSM_BD), F32)] * 2)
    res = pl.pallas_call(
        body, grid=(SSM_S // w, nt), in_specs=in_specs, out_specs=out_specs, out_shape=out_shape,
        scratch_shapes=scratch, compiler_params=_params(("parallel", "arbitrary")), name=name,
    )(*args)
    return res if with_da else res[0]


def _nt_dot(x, y):
    return lax.dot_general(x.astype(_MXU), y.astype(_MXU), (((1,), (1,)), ((), ())), preferred_element_type=F32)


def _tn_dot(x, y):
    return lax.dot_general(x.astype(_MXU), y.astype(_MXU), (((0,), (0,)), ((), ())), preferred_element_type=F32)


def _nn_dot(x, y):
    return jnp.dot(x.astype(_MXU), y.astype(_MXU), preferred_element_type=F32)


def _attn_mask2(gb, nb):
    qi = lax.broadcasted_iota(jnp.int32, (ATT_WIN, 2 * ATT_WIN), 0)
    c = lax.broadcasted_iota(jnp.int32, (ATT_WIN, 2 * ATT_WIN), 1)
    has_prev = (gb % nb) != 0
    prev_ok = jnp.logical_and(jnp.logical_and(c < ATT_WIN, c >= qi), has_prev)
    own_ok = jnp.logical_and(c >= ATT_WIN, c - ATT_WIN <= qi)
    return jnp.logical_or(prev_ok, own_ok)


def _attn_specs():
    cur = pl.BlockSpec((ATT_QB * ATT_WIN, ATT_GW), lambda i: (i, 0))
    prev = pl.BlockSpec((ATT_WIN, ATT_GW), lambda i: (jnp.maximum(ATT_QB * i - 1, 0), 0))
    return cur, prev


def _attn_fwd(q, k, v, nb, name):
    l = q.shape[0]
    scale = ATT_E ** -0.5
    w = ATT_WIN

    def body(q_ref, kc_ref, kp_ref, vc_ref, vp_ref, o_ref, lse_ref):
        i = pl.program_id(0)
        masks = [_attn_mask2(ATT_QB * i + b, nb) for b in range(ATT_QB)]
        for h in range(ATT_HG):
            sl = slice(h * ATT_E, (h + 1) * ATT_E)
            k_ext = jnp.concatenate([kp_ref[:, sl], kc_ref[:, sl]], axis=0)
            v_ext = jnp.concatenate([vp_ref[:, sl], vc_ref[:, sl]], axis=0)
            for b in range(ATT_QB):
                r, kr = slice(b * w, (b + 1) * w), slice(b * w, (b + 2) * w)
                s = jnp.where(masks[b], _nt_dot(q_ref[r, sl], k_ext[kr]) * scale, NEG_INF)
                mx = jnp.max(s, axis=-1, keepdims=True)
                p = jnp.exp(s - mx)
                den = jnp.sum(p, axis=-1, keepdims=True)
                o_ref[r, sl] = _nn_dot(p, v_ext[kr]) / den
                lse_ref[r, sl] = jnp.broadcast_to(mx + jnp.log(den), (w, ATT_E))

    cur, prev = _attn_specs()
    return pl.pallas_call(
        body, grid=(l // (ATT_QB * w),), in_specs=[cur, cur, prev, cur, prev], out_specs=[cur, cur],
        out_shape=[jax.ShapeDtypeStruct((l, ATT_GW), F32)] * 2,
        compiler_params=_params(("parallel",)), name=name,
    )(q, k, k, v, v)


def _attn_bwd(q, k, v, do, lse, dd, nb, name):
    l = q.shape[0]
    scale = ATT_E ** -0.5
    w = ATT_WIN
    nblk = l // w

    def body(q_ref, kc_ref, kp_ref, vc_ref, vp_ref, do_ref, lse_ref, dd_ref, qn_ref, don_ref, lsen_ref, ddn_ref,
             dq_ref, dk_ref, dv_ref, dk_acc, dv_acc):
        i = pl.program_id(0)
        masks = [_attn_mask2(ATT_QB * i + b, nb) for b in range(ATT_QB)]
        nxt = ATT_QB * (i + 1)
        nxt_attends = jnp.logical_and(nxt < nblk, (nxt % nb) != 0)
        qi = lax.broadcasted_iota(jnp.int32, (w, w), 0)
        kj = lax.broadcasted_iota(jnp.int32, (w, w), 1)
        mask_n = jnp.logical_and(kj >= qi, nxt_attends)
        dk_acc[...] = jnp.zeros_like(dk_acc)
        dv_acc[...] = jnp.zeros_like(dv_acc)
        for h in range(ATT_HG):
            sl, col = slice(h * ATT_E, (h + 1) * ATT_E), slice(h * ATT_E, h * ATT_E + 1)
            k_ext = jnp.concatenate([kp_ref[:, sl], kc_ref[:, sl]], axis=0)
            v_ext = jnp.concatenate([vp_ref[:, sl], vc_ref[:, sl]], axis=0)
            for b in range(ATT_QB):
                r, kr = slice(b * w, (b + 1) * w), slice(b * w, (b + 2) * w)
                qh, doh, k2, v2 = q_ref[r, sl], do_ref[r, sl], k_ext[kr], v_ext[kr]
                p = jnp.where(masks[b], jnp.exp(_nt_dot(qh, k2) * scale - lse_ref[r, col]), 0.0)
                ds = p * (_nt_dot(doh, v2) - dd_ref[r, col]) * scale
                dq_ref[r, sl] = _nn_dot(ds, k2).astype(dq_ref.dtype)
                dk2, dv2 = _tn_dot(ds, qh), _tn_dot(p, doh)
                dk_acc[r, sl] += dk2[w:]
                dv_acc[r, sl] += dv2[w:]
                if b > 0:
                    rp = slice((b - 1) * w, b * w)
                    dk_acc[rp, sl] += dk2[:w]
                    dv_acc[rp, sl] += dv2[:w]
            last = slice((ATT_QB - 1) * w, ATT_QB * w)
            kl, vl, qn, don = kc_ref[last, sl], vc_ref[last, sl], qn_ref[:, sl], don_ref[:, sl]
            pn = jnp.where(mask_n, jnp.exp(_nt_dot(qn, kl) * scale - lsen_ref[:, col]), 0.0)
            dsn = pn * (_nt_dot(don, vl) - ddn_ref[:, col]) * scale
            dk_acc[last, sl] += _tn_dot(dsn, qn)
            dv_acc[last, sl] += _tn_dot(pn, don)
        dk_ref[...] = dk_acc[...].astype(dk_ref.dtype)
        dv_ref[...] = dv_acc[...].astype(dv_ref.dtype)

    cur, prev = _attn_specs()
    nxt_spec = pl.BlockSpec((w, ATT_GW), lambda i: (jnp.minimum(ATT_QB * (i + 1), nblk - 1), 0))
    return pl.pallas_call(
        body, grid=(l // (ATT_QB * w),),
        in_specs=[cur, cur, prev, cur, prev, cur, cur, cur, nxt_spec, nxt_spec, nxt_spec, nxt_spec],
        out_specs=[cur] * 3, out_shape=[jax.ShapeDtypeStruct((l, ATT_GW), BF16)] * 3,
        scratch_shapes=[pltpu.VMEM((ATT_QB * w, ATT_GW), F32)] * 2,
        compiler_params=_params(("parallel",)), name=name,
    )(q, k, k, v, v, do, lse, dd, q, do, lse, dd)


def _to_perm(a, d):
    if d == 1:
        return a
    l, c = a.shape
    return a.reshape(l // d, d, c).transpose(1, 0, 2).reshape(l, c)


def _from_perm(a, d):
    if d == 1:
        return a
    l, c = a.shape
    return a.reshape(d, l // d, c).transpose(1, 0, 2).reshape(l, c)


def _mem_probs(qh, kh):
    s = _nt_dot(qh, kh) * (MEM_E ** -0.5)
    e = jnp.exp(s - jnp.max(s, axis=-1, keepdims=True))
    return e / jnp.sum(e, axis=-1, keepdims=True)


def _mem_fwd(mq, kv, name, tm=512):
    l, nm = mq.shape[0], kv.shape[0]

    def body(q_ref, kv_ref, o_ref):
        for h in range(MEM_H):
            sl = slice(h * MEM_E, (h + 1) * MEM_E)
            p = _mem_probs(q_ref[:, sl], kv_ref[:, sl])
            o_ref[:, sl] = _nn_dot(p, kv_ref[:, MEM_W + h * MEM_E:MEM_W + (h + 1) * MEM_E]).astype(o_ref.dtype)

    return pl.pallas_call(
        body, grid=(l // tm,),
        in_specs=[pl.BlockSpec((tm, MEM_W), lambda i: (i, 0)), pl.BlockSpec((nm, 2 * MEM_W), lambda i: (0, 0))],
        out_specs=pl.BlockSpec((tm, MEM_W), lambda i: (i, 0)),
        out_shape=jax.ShapeDtypeStruct((l, MEM_W), BF16),
        compiler_params=_params(("parallel",)), name=name,
    )(mq, kv)


def _mem_bwd(mq, kv, dmo, name, tm=512):
    l, nm = mq.shape[0], kv.shape[0]
    scale = MEM_E ** -0.5

    def body(q_ref, kv_ref, do_ref, dq_ref, dkv_ref):
        @pl.when(pl.program_id(0) == 0)
        def _():
            dkv_ref[...] = jnp.zeros_like(dkv_ref)

        for h in range(MEM_H):
            sl = slice(h * MEM_E, (h + 1) * MEM_E)
            vsl = slice(MEM_W + h * MEM_E, MEM_W + (h + 1) * MEM_E)
            qh, kh, vh, doh = q_ref[:, sl], kv_ref[:, sl], kv_ref[:, vsl], do_ref[:, sl]
            p = _mem_probs(qh, kh)
            dp = _nt_dot(doh, vh)
            ds = p * (dp - jnp.sum(dp * p, axis=-1, keepdims=True)) * scale
            dq_ref[:, sl] = _nn_dot(ds, kh).astype(dq_ref.dtype)
            dkv_ref[:, sl] += _tn_dot(ds, qh)
            dkv_ref[:, vsl] += _tn_dot(p, doh)

    row = pl.BlockSpec((tm, MEM_W), lambda i: (i, 0))
    full = pl.BlockSpec((nm, 2 * MEM_W), lambda i: (0, 0))
    return pl.pallas_call(
        body, grid=(l // tm,), in_specs=[row, full, row], out_specs=[row, full],
        out_shape=[jax.ShapeDtypeStruct((l, MEM_W), BF16), jax.ShapeDtypeStruct((nm, 2 * MEM_W), F32)],
        compiler_params=_params(("arbitrary",)), name=name,
    )(mq, kv, dmo)


def _discretize(lam_re, lam_im, log_dt, b_re, b_im):
    dt = jnp.exp(log_dt)[:, None]
    mag = jnp.exp(lam_re * dt)
    a_re, a_im = mag * jnp.cos(lam_im * dt), mag * jnp.sin(lam_im * dt)
    nr, ni = a_re - 1.0, a_im
    den = lam_re * lam_re + lam_im * lam_im
    coef_re = (nr * lam_re + ni * lam_im) / den
    coef_im = (ni * lam_re - nr * lam_im) / den
    bb_re = coef_re[..., None] * b_re - coef_im[..., None] * b_im
    bb_im = coef_re[..., None] * b_im + coef_im[..., None] * b_re
    return a_re, a_im, bb_re, bb_im


def _bd_in(bb):
    return jnp.einsum("gph,gk->ghkp", bb, jnp.eye(SSM_G, dtype=bb.dtype)).reshape(SSM_W, SSM_S)


def _bd_diag(x):
    gb = SSM_G // SSM_BD
    t = x.reshape(SSM_BD, gb, SSM_H, gb, SSM_P)
    return jnp.einsum("bghgp->bghp", t).reshape(SSM_G, SSM_H, SSM_P)


_ANY = pl.BlockSpec(memory_space=pl.ANY)
_MESH = pl.DeviceIdType.MESH


def _allgather(x, name):
    def body(x_ref, out_ref, send_sems, recv_sems, local_sem):
        mx, my, mc = lax.axis_index("x"), lax.axis_index("y"), lax.axis_index("c")
        me, sibling = (mx, my, mc), (mx, my, 1 - mc)
        chips = [(1 - mx, my), (mx, 1 - my), (1 - mx, 1 - my)]

        def blk(px, py, pc):
            return out_ref.at[4 * px + 2 * py + pc]

        def copy(k, block, to, src=None):
            return pltpu.make_async_remote_copy(
                src_ref=blk(*block) if src is None else src, dst_ref=blk(*block),
                send_sem=send_sems.at[k], recv_sem=recv_sems.at[k], device_id=to, device_id_type=_MESH)

        mine = pltpu.make_async_copy(x_ref, blk(*me), local_sem)
        mine.start()
        first = [copy(0, me, sibling, src=x_ref)]
        first += [copy(1 + j, me, (*chip, mc), src=x_ref) for j, chip in enumerate(chips)]
        for cp in first:
            cp.start()
        passed = [copy(4 + j, (*chip, mc), sibling) for j, chip in enumerate(chips)]
        for j, chip in enumerate(chips):
            copy(1 + j, (*chip, mc), me).wait_recv()
            passed[j].start()
        copy(0, sibling, me).wait_recv()
        for j, chip in enumerate(chips):
            copy(4 + j, (*chip, 1 - mc), me).wait_recv()
        for cp in first + passed:
            cp.wait_send()
        mine.wait()

    return pl.pallas_call(
        body, out_shape=jax.ShapeDtypeStruct((N_DEV,) + x.shape, x.dtype), in_specs=[_ANY], out_specs=_ANY,
        scratch_shapes=[pltpu.SemaphoreType.DMA((7,)), pltpu.SemaphoreType.DMA((7,)), pltpu.SemaphoreType.DMA],
        name=name,
    )(x)


def _pair_exchange(g, name):
    def body(g_ref, out_ref, send_sems, recv_sems):
        mx, my, mc = lax.axis_index("x"), lax.axis_index("y"), lax.axis_index("c")
        copies = [pltpu.make_async_remote_copy(
            src_ref=g_ref.at[2 * k + (1 - mc)], dst_ref=out_ref.at[k], send_sem=send_sems.at[k],
            recv_sem=recv_sems.at[k], device_id=(mx, my, 1 - mc), device_id_type=_MESH) for k in range(4)]
        for cp in copies:
            cp.start()
        for cp in copies:
            cp.wait()

    return pl.pallas_call(
        body, out_shape=jax.ShapeDtypeStruct((4,) + g.shape[1:], g.dtype), in_specs=[_ANY], out_specs=_ANY,
        scratch_shapes=[pltpu.SemaphoreType.DMA((4,)), pltpu.SemaphoreType.DMA((4,))], name=name,
    )(g)


_HBM = pl.BlockSpec(memory_space=pltpu.HBM)
_SEM = pl.BlockSpec(memory_space=pltpu.SEMAPHORE)
_EFFECT = pltpu.SideEffectType.DATAFLOW_SIDE_EFFECTING
_TOKEN = jax.ShapeDtypeStruct((8, 128), F32)


def _peer(rel):
    pos = (lax.axis_index("x"), lax.axis_index("y"), lax.axis_index("c"))
    return tuple(1 - p if (rel >> (2 - i)) & 1 else p for i, p in enumerate(pos))


def _index_of(dev):
    return 4 * dev[0] + 2 * dev[1] + dev[2]


def _split_copies(src_ref, land_ref, sems, plan):
    n = len(plan)
    return [pltpu.make_async_remote_copy(
        src_ref=src_ref if s is None else src_ref.at[s], dst_ref=land_ref.at[d], send_sem=sems[k],
        recv_sem=sems[n + k], device_id=peer, device_id_type=_MESH) for k, (s, d, peer) in enumerate(plan)]


def _split_start(src, n_land, plan_fn, after, name):
    blk = src.shape[-2:]
    land = lax.empty((n_land,) + blk, src.dtype)
    n = len(plan_fn())

    def body(src_ref, land_ref, after_ref, *outs):
        for cp in _split_copies(src_ref, land_ref, outs[:2 * n], plan_fn()):
            cp.start()
        outs[2 * n + 2][...] = jnp.zeros_like(outs[2 * n + 2])

    res = pl.pallas_call(
        body, name=name,
        out_shape=(pltpu.SemaphoreType.DMA(()),) * (2 * n)
        + (pltpu.HBM(src.shape, src.dtype), pltpu.HBM(land.shape, land.dtype), _TOKEN),
        in_specs=(_HBM, _HBM, _ANY),
        out_specs=(_SEM,) * (2 * n) + (_HBM, _HBM, pl.BlockSpec(memory_space=pltpu.VMEM)),
        input_output_aliases={0: 2 * n, 1: 2 * n + 1},
        compiler_params=pltpu.CompilerParams(has_side_effects=_EFFECT),
    )(pltpu.with_memory_space_constraint(src, pltpu.HBM), pltpu.with_memory_space_constraint(land, pltpu.HBM), after)
    return res[:2 * n], res[2 * n], res[2 * n + 1], res[2 * n + 2]


def _split_wait(sems, src, land, plan_fn, after, name):
    n = len(sems) // 2

    def body(src_ref, land_ref, *rest):
        for cp in _split_copies(src_ref, land_ref, rest[:2 * n], plan_fn()):
            cp.wait_send()
            cp.wait_recv()

    return pl.pallas_call(
        body, name=name,
        out_shape=(pltpu.HBM(src.shape, src.dtype), pltpu.HBM(land.shape, land.dtype)),
        in_specs=(_HBM, _HBM) + (_SEM,) * (2 * n) + (_ANY,), out_specs=(_HBM, _HBM),
        input_output_aliases={0: 0, 1: 1},
        compiler_params=pltpu.CompilerParams(has_side_effects=_EFFECT),
    )(src, land, *sems, after)


def _gather_plan():
    me = _index_of(_peer(0))
    return [(None, me, _peer(rel)) for rel in range(1, N_DEV)]


def _gather_wait_plan():
    return [(None, _index_of(_peer(rel)), _peer(rel)) for rel in range(1, N_DEV)]


def _chip_plan():
    return [(_index_of(_peer(rel)) // 2, j, _peer(rel)) for j, rel in enumerate((4, 2, 6))]


def _pair_sum(g, t1, my_c, name, tr):
    _, r, c = g.shape

    def body(c_ref, g_ref, t_ref, o_ref, ob_ref):
        s = g_ref[...] + t_ref[...]
        o_ref[...] = s
        ob_ref[...] = s.astype(BF16)

    blk = pl.BlockSpec((None, tr, c), lambda k, i, cr: (k, i, 0))
    return pl.pallas_call(
        body,
        grid_spec=pltpu.PrefetchScalarGridSpec(
            num_scalar_prefetch=1, grid=(4, r // tr),
            in_specs=[pl.BlockSpec((None, tr, c), lambda k, i, cr: (2 * k + cr[0], i, 0)), blk],
            out_specs=[blk, blk]),
        out_shape=[jax.ShapeDtypeStruct((4, r, c), F32), jax.ShapeDtypeStruct((4, r, c), BF16)],
        compiler_params=_params(("parallel", "parallel")), name=name,
    )(my_c, g, t1)


def _adam_math(g, w, m, v):
    m = ADAM_B1 * m + (1.0 - ADAM_B1) * g
    v = ADAM_B2 * v + (1.0 - ADAM_B2) * (g * g)
    m_hat = m / (1.0 - ADAM_B1 ** ADAM_STEP)
    v_hat = v / (1.0 - ADAM_B2 ** ADAM_STEP)
    delta = -ADAM_LR * (m_hat / (jnp.sqrt(v_hat) + ADAM_EPS) + ADAM_WD * w)
    return delta, m, v


def _grad_sum(p, t2, my_chip, name, tr):
    _, r, c = p.shape

    def body(k_ref, p_ref, t0_ref, t1_ref, t2_ref, g_out):
        g_out[...] = ((p_ref[...] + t0_ref[...].astype(F32)) + t1_ref[...].astype(F32)) + t2_ref[...].astype(F32)

    def rel(j):
        return pl.BlockSpec((None, tr, c), lambda i, kr: (j, i, 0))

    return pl.pallas_call(
        body,
        grid_spec=pltpu.PrefetchScalarGridSpec(
            num_scalar_prefetch=1, grid=(r // tr,),
            in_specs=[pl.BlockSpec((None, tr, c), lambda i, kr: (kr[0], i, 0)), rel(0), rel(1), rel(2)],
            out_specs=pl.BlockSpec((tr, c), lambda i, kr: (i, 0))),
        out_shape=jax.ShapeDtypeStruct((r, c), F32),
        compiler_params=_params(("parallel",)), name=name,
    )(my_chip, p, t2, t2, t2)


def _adam_many(g, w, m, v, row_tiles, name):
    n = len(g)

    def body(*refs):
        ins, outs = refs[:4 * n], refs[4 * n:]
        for i in range(n):
            res = _adam_math(ins[i][...], ins[n + i][...], ins[2 * n + i][...], ins[3 * n + i][...])
            for kind in range(3):
                outs[kind * n + i][...] = res[kind]

    def spec(a):
        blk = (a.shape[0] // row_tiles,) + a.shape[1:]
        return pl.BlockSpec(blk, lambda t, nd=a.ndim: (t,) + (0,) * (nd - 1))

    specs = [spec(a) for a in g]
    res = pl.pallas_call(
        body, grid=(row_tiles,), in_specs=specs * 4, out_specs=specs * 3,
        out_shape=[jax.ShapeDtypeStruct(a.shape, F32) for a in g] * 3,
        compiler_params=_params(("parallel",)), name=name,
    )(*g, *w, *m, *v)
    return res[:n], res[n:2 * n], res[2 * n:]


def _sum8(g8, name):
    _, r, c = g8.shape

    def body(g_ref, o_ref):
        acc = g_ref[0]
        for j in range(1, N_DEV):
            acc = acc + g_ref[j]
        o_ref[...] = acc

    return pl.pallas_call(
        body, grid=(1,), in_specs=[pl.BlockSpec((N_DEV, r, c), lambda i: (0, 0, 0))],
        out_specs=pl.BlockSpec((r, c), lambda i: (0, 0)), out_shape=jax.ShapeDtypeStruct((r, c), F32),
        compiler_params=_params(("arbitrary",)), name=name,
    )(g8)


def _pack(arrs, pad_rows=8):
    flat = jnp.concatenate([a.reshape(-1) for a in arrs])
    n = flat.shape[0]
    q = PACK_C * pad_rows
    tot = -(-n // q) * q
    if tot != n:
        flat = jnp.concatenate([flat, jnp.zeros((tot - n,), flat.dtype)])
    return flat.reshape(tot // PACK_C, PACK_C)


def _unpack(buf, shapes):
    flat = buf.reshape(-1)
    out, off = [], 0
    for s in shapes:
        n = int(np.prod(s))
        out.append(flat[off:off + n].reshape(s))
        off += n
    return out


GROUPS = (("w_in", "w_mem_kv", "w_o"),
          ("w_glu", "w_ssm_br", "w_mem_br", "w_attn_br"),
          ("w_up", "w_down"))
GROUP_TR = (528, 384, 512)
MLP_GROUP = 2
ATTN_BR_FOLD = 2


def _stored_shape(name):
    r, c, ax = BIG_SHAPE[name]
    rows, cols = (r // N_DEV, c) if ax == 0 else (c // N_DEV, r)
    return (rows // ATTN_BR_FOLD, cols * ATTN_BR_FOLD) if name == "w_attn_br" else (rows, cols)


def _stored(shard, name):
    a = shard[0].T if BIG_SHAPE[name][2] == 1 else shard[0]
    return a.reshape(_stored_shape(name))


def _unstored(a, name):
    r, c, ax = BIG_SHAPE[name]
    if ax == 0:
        return a.reshape(1, r // N_DEV, c)
    return a.reshape(c // N_DEV, r).T[None]


def _pack_group(d, names):
    return jnp.concatenate([_stored(d[n], n) for n in names], axis=0)


def _split_group(buf, names):
    out, off = {}, 0
    for n in names:
        rows = _stored_shape(n)[0]
        out[n] = buf[..., off:off + rows, :]
        off += rows
    return out


def _full_stored(stacked, name):
    r, c, ax = BIG_SHAPE[name]
    return stacked.reshape((r, c) if ax == 0 else (c, r))


def _stacked_stored(full, name):
    return full.reshape((N_DEV,) + _stored_shape(name))


def _gelu_parts(x):
    c0, c1 = math.sqrt(2.0 / math.pi), 0.044715
    th = jnp.tanh(c0 * (x + c1 * x * x * x))
    return th, c0, c1


def _local_step(x, mem, tgt, wb, sp, mlp_weights, grads_ready):
    l = x.shape[0]
    w_a, w_g = wb["w_in"][:ZA_W], wb["w_in"][ZA_W:]

    a_re, a_im, bb_re, bb_im = _discretize(sp["ssm_lambda_re"], sp["ssm_lambda_im"], sp["ssm_log_dt"],
                                           sp["ssm_b_re"], sp["ssm_b_im"])
    a_pair = jnp.stack([a_re.reshape(1, SSM_S), a_im.reshape(1, SSM_S)])
    a_conj = jnp.stack([a_re.reshape(1, SSM_S), -a_im.reshape(1, SSM_S)])
    b_re_t, b_im_t = _bd_in(bb_re).astype(BF16), _bd_in(bb_im).astype(BF16)
    c_re_t = _bd_in(sp["ssm_c_re"].transpose(0, 2, 1)).astype(BF16)
    c_im_t = (-_bd_in(sp["ssm_c_im"].transpose(0, 2, 1))).astype(BF16)
    d_row = sp["ssm_d"].reshape(1, SSM_W)

    n1 = _rms_fwd(x, sp["norm1_g"], "rms1")
    za = _mm(n1, w_a, [BF16], tb=True, name="in_proj_a", tn=1664)
    zg = _mm(n1, w_g, [BF16], tb=True, name="in_proj_g")
    u = za[:, :SSM_W]
    mq = za[:, ZA_W - MEM_W:]

    u_s = _scan_order(u)
    s_all = _ssm_scan(u_s, b_re_t, b_im_t, a_pair, reverse=False, name="ssm_scan_fwd")
    ys = _time_order(_mm((s_all, 0), c_re_t, [F32], tb=True, pair2=((s_all, 1), c_im_t), bd=SSM_BD, name="ssm_cs"))

    def gelu_fn(r, b):
        y0 = r[0] + b[0] * r[1].astype(F32)
        th, _, _ = _gelu_parts(y0)
        return [y0, 0.5 * y0 * (1.0 + th)], []
    y0, y1 = _ew(gelu_fn, [ys, u], [d_row], [(SSM_W, F32), (SSM_W, BF16)], [], name="ssm_gelu", tm=512)

    def glu_epi(acc, y1t, bg):
        t = acc + bg
        return t, y1t.astype(F32) * _sigmoid(t)
    t_glu, y2 = _mm(y1, wb["w_glu"], [F32, BF16], epi=glu_epi, mn=[y1], rows=[sp["b_glu"]], name="ssm_glu")
    br_ssm = _mm(y2, wb["w_ssm_br"], [BF16], tb=True, name="ssm_br")

    qkv_p, o_g, lse_g = [], [], []
    for g, d in enumerate(DILATIONS):
        nb = l // d // ATT_WIN
        cols = [za[:, SSM_W + (3 * j + g) * ATT_GW: SSM_W + (3 * j + g + 1) * ATT_GW] for j in range(3)]
        qp, kp, vp = [_to_perm(cc, d) for cc in cols]
        qkv_p.append((qp, kp, vp))
        og, lg = _attn_fwd(qp, kp, vp, nb, "attn_fwd%d" % g)
        o_g.append(_from_perm(og, d))
        lse_g.append(_from_perm(lg, d))

    def merge_fn(r, b):
        o0, o1, o2, l0, l1, l2 = r
        mx = jnp.maximum(jnp.maximum(l0, l1), l2)
        e0, e1, e2 = jnp.exp(l0 - mx), jnp.exp(l1 - mx), jnp.exp(l2 - mx)
        tot = e0 + e1 + e2
        return [(e0 * o0 + e1 * o1 + e2 * o2) / tot, mx + jnp.log(tot)], []
    o_att, lse_tot = _ew(merge_fn, o_g + lse_g, [], [(ATT_GW, F32), (ATT_GW, F32)], [], name="attn_merge", tm=512)
    br_attn = _mm(o_att, wb["w_attn_br"], [BF16], tb=True, name="attn_br")

    mn = _rms_fwd(mem, sp["mem_norm_g"], "rms_mem")
    kv = _mm(mn, wb["w_mem_kv"], [BF16], name="mem_kv")
    mo = _mem_fwd(mq, kv, "mem_attn_fwd")
    br_mem = _mm(mo, wb["w_mem_br"], [BF16], tb=True, name="mem_br")

    def gate_fn(r, b):
        zgt, b0, b1, b2 = [t.astype(F32) for t in r]
        gt = _sigmoid(zgt + b[0])
        return [gt[:, :D_MODEL] * b0 + gt[:, D_MODEL:2 * D_MODEL] * b1 + gt[:, 2 * D_MODEL:] * b2], []
    merged = _ew(gate_fn, [zg, br_ssm, br_attn, br_mem], [sp["b_gate"]], [(D_MODEL, BF16)], [], name="gate_merge")[0]
    def o_epi(acc, xt, g2):
        hv = acc + xt
        rs = lax.rsqrt(jnp.mean(hv * hv, axis=-1, keepdims=True) + RMS_EPS)
        return hv, hv * rs * g2
    h1, n2 = _mm(merged, wb["w_o"], [F32, BF16], epi=o_epi, mn=[x], rows=[sp["norm2_g"]], tm=512, name="o_proj")

    def up_epi(acc):
        ra = jnp.maximum(acc, 0.0)
        return ra * ra, ra
    wm = mlp_weights(n2)
    f_act, r_act = _mm(n2, wm["w_up"], [BF16, BF16], tb=True, epi=up_epi, name="mlp_up")
    def down_epi(acc, ht, tv, gf):
        hv = acc + ht
        rs = lax.rsqrt(jnp.mean(hv * hv, axis=-1, keepdims=True) + RMS_EPS)
        err = hv * rs * gf - tv
        dh, dgf = _rms_bwd_tile(hv, err * (1.0 / D_MODEL), gf)
        return dh, dgf, _colsum(err * err) * (0.5 / D_MODEL)
    dh2, d_final_g, loss_cols = _mm(f_act, wm["w_down"], [F32], epi=down_epi, mn=[h1, tgt], rows=[sp["final_g"]],
                                    n_sums=2, tm=512, name="mlp_down")
    loss = jnp.sum(loss_cols, axis=1, keepdims=True)

    gw, gs = {}, {"final_g": d_final_g}
    d_act = _mm(dh2, wm["w_down"], [BF16], tb=True, epi=lambda acc, ra: (acc * 2.0 * ra.astype(F32),), mn=[r_act],
                name="mlp_down_dx")
    dw_down = _mm(f_act, dh2, [F32], ta=True, name="mlp_down_dw")
    dw_up = _mm(d_act, n2, [F32], ta=True, name="mlp_up_dw")
    token = grads_ready(MLP_GROUP, {"w_up": dw_up, "w_down": dw_down})
    def up_dx_epi(acc, ht, dht, g2):
        dx, dg = _rms_bwd_tile(ht, acc, g2)
        return dx + dht, dg
    dh1, gs["norm2_g"] = _mm(d_act, wm["w_up"], [F32], epi=up_dx_epi, mn=[h1, dh2],
                             rows=[sp["norm2_g"] + token[:1, :1]], n_sums=1, tm=512, name="mlp_up_dx")
    gw["w_o"] = _mm(merged, dh1, [F32], ta=True, name="o_proj_dw")

    def gate_bwd_epi(dm, *tiles):
        dbr, dz = [], []
        for zt, bt, bias in zip(tiles[0:3], tiles[3:6], tiles[6:9]):
            gt = _sigmoid(zt.astype(F32) + bias)
            dbr.append(dm * gt)
            dz.append(dm * bt.astype(F32) * gt * (1.0 - gt))
        return (*dbr, *dz, *[_colsum(t) for t in dz])
    gate_bias = [sp["b_gate"][:, i * D_MODEL:(i + 1) * D_MODEL] for i in range(3)]
    res = _mm(dh1, wb["w_o"], [BF16] * 6, tb=True, epi=gate_bwd_epi, mn=[(zg, 0), (zg, 1), (zg, 2), br_ssm, br_attn, br_mem],
              rows=gate_bias, n_sums=3, tm=512, name="o_proj_dx")
    (dbr_ssm, dbr_attn, dbr_mem), dzg = res[0:3], res[3:6]
    gs["b_gate"] = jnp.concatenate(res[6:9], axis=1)

    gw["w_ssm_br"] = _mm(dbr_ssm, y2, [F32], ta=True, name="ssm_br_dw")
    dy2 = _mm(dbr_ssm, wb["w_ssm_br"], [F32], name="ssm_br_dx")

    def glu_bwd_fn(r, b):
        dy, y1t, tt = r
        sg = _sigmoid(tt)
        dt = dy * y1t.astype(F32) * sg * (1.0 - sg)
        return [dt, dy * sg], [_colsum(dt)]
    dt_glu, dy1a, gs["b_glu"] = _ew(glu_bwd_fn, [dy2, y1, t_glu], [], [(SSM_W, BF16), (SSM_W, F32)], [SSM_W],
                                    name="ssm_glu_bwd", tm=512)
    gw["w_glu"] = _mm(y1, dt_glu, [F32], ta=True, name="ssm_glu_dw")

    def gelu_bwd_epi(acc, dy1t, y0t):
        th, c0, c1 = _gelu_parts(y0t)
        dg = 0.5 * (1.0 + th) + 0.5 * y0t * (1.0 - th * th) * c0 * (1.0 + 3.0 * c1 * y0t * y0t)
        return ((acc + dy1t) * dg,)
    dy0 = _mm(dt_glu, wb["w_glu"], [F32], tb=True, epi=gelu_bwd_epi, mn=[dy1a, y0], name="ssm_glu_dx")
    gs["ssm_d"] = _ew(lambda r, b: ([], [_colsum(r[0] * r[1].astype(F32))]), [dy0, u], [], [], [SSM_W],
                      name="ssm_dd", tm=512)[0]
    dy0_s = _scan_order(dy0)
    lam, da, d_b, d_c = _ssm_scan(dy0_s, c_re_t, c_im_t, a_conj, reverse=True, s_fwd=s_all, u=u_s,
                                  name="ssm_scan_bwd")
    du = _time_order(_mm((lam, 0), b_re_t, [BF16], tb=True, pair2=((lam, 1), b_im_t),
                         epi=lambda acc, dyt, dr: (acc + dyt * dr,), mn=[dy0_s], rows=[d_row], bd=SSM_BD, name="ssm_bu_dx"))
    gs["a_re"], gs["a_im"] = da[0], da[1]
    gs["bb_re"], gs["bb_im"] = _bd_diag(d_b[0]).transpose(0, 2, 1), _bd_diag(d_b[1]).transpose(0, 2, 1)
    gs["ssm_c_re"], gs["ssm_c_im"] = _bd_diag(d_c[0]), -_bd_diag(d_c[1])

    gw["w_attn_br"] = _mm(dbr_attn, o_att, [F32], ta=True, name="attn_br_dw")

    def do_epi(acc, ot):
        prod = acc * ot
        head = lax.broadcasted_iota(jnp.int32, prod.shape, 1) // ATT_E
        dd = jnp.zeros_like(prod)
        for h in range(ATT_HG):
            dd = jnp.where(head == h, jnp.sum(jnp.where(head == h, prod, 0.0), axis=1, keepdims=True), dd)
        return acc, dd
    do_att, dd_att = _mm(dbr_attn, wb["w_attn_br"], [BF16, F32], epi=do_epi, mn=[o_att], name="attn_br_dx")
    dq_l, dk_l, dv_l = [], [], []
    for g, d in enumerate(DILATIONS):
        nb = l // d // ATT_WIN
        qp, kp, vp = qkv_p[g]
        dq, dk, dv = _attn_bwd(qp, kp, vp, _to_perm(do_att, d), _to_perm(lse_tot, d), _to_perm(dd_att, d),
                               nb, "attn_bwd%d" % g)
        dq_l.append(_from_perm(dq, d))
        dk_l.append(_from_perm(dk, d))
        dv_l.append(_from_perm(dv, d))

    gw["w_mem_br"] = _mm(dbr_mem, mo, [F32], ta=True, name="mem_br_dw")
    dmo = _mm(dbr_mem, wb["w_mem_br"], [BF16], name="mem_br_dx")
    dmq, dkv = _mem_bwd(mq, kv, dmo, "mem_attn_bwd")
    gw["w_mem_kv"] = _mm(mn, dkv, [F32], ta=True, name="mem_kv_dw")
    dmn = _mm(dkv, wb["w_mem_kv"], [F32], tb=True, name="mem_kv_dx")
    token = grads_ready(1, gw)
    gs["mem_norm_g"] = _rms_bwd(mem, dmn, None, sp["mem_norm_g"] + token[:1, :1], "rms_mem_bwd")[1]

    dza = jnp.concatenate([du] + dq_l + dk_l + dv_l + [dmq], axis=1)
    dn_a = _mm(dza, w_a, [F32], name="in_proj_a_dx", tk=1664)
    dw_a = _mm(dza, n1, [F32], ta=True, name="in_proj_a_dw", tm=1664)
    dw_g = [_mm(dzg[i], n1, [F32], ta=True, name="in_proj_g_dw%d" % i) for i in range(3)]
    gw["w_in"] = jnp.concatenate([dw_a] + dw_g, axis=0)
    token = grads_ready(0, gw)
    def in_dx_epi(acc, pt, xt, dht, g1):
        dx, dg = _rms_bwd_tile(xt, acc + pt, g1)
        return dx + dht, dg
    w_gs = [w_g[i * D_MODEL:(i + 1) * D_MODEL] for i in range(3)]
    grad_x, gs["norm1_g"] = _mm(dzg[0], w_gs[0], [F32], pair2=(dzg[1], w_gs[1], dzg[2], w_gs[2]), epi=in_dx_epi,
                                mn=[dn_a, x, dh1],
                                rows=[sp["norm1_g"] + token[:1, :1]], n_sums=1, tm=512, name="in_proj_g_dx")
    return loss, grad_x, gs


_SMALL_GRAD_ORDER = ("norm1_g", "mem_norm_g", "b_gate", "a_re", "a_im", "bb_re", "bb_im", "ssm_c_re", "ssm_c_im",
                     "ssm_d", "b_glu", "norm2_g", "final_g")


def kernel(x, mem, norm1_g, mem_norm_g, w_in, b_gate, ssm_lambda_re, ssm_lambda_im, ssm_log_dt, ssm_b_re, ssm_b_im, ssm_c_re, ssm_c_im, ssm_d, w_glu, b_glu, w_ssm_br, w_attn_br, w_mem_kv, w_mem_br, w_o, norm2_g, w_up, w_down, final_g, loss_target, m_norm1_g, m_mem_norm_g, m_w_in, m_b_gate, m_ssm_lambda_re, m_ssm_lambda_im, m_ssm_log_dt, m_ssm_b_re, m_ssm_b_im, m_ssm_c_re, m_ssm_c_im, m_ssm_d, m_w_glu, m_b_glu, m_w_ssm_br, m_w_attn_br, m_w_mem_kv, m_w_mem_br, m_w_o, m_norm2_g, m_w_up, m_w_down, m_final_g, v_norm1_g, v_mem_norm_g, v_w_in, v_b_gate, v_ssm_lambda_re, v_ssm_lambda_im, v_ssm_log_dt, v_ssm_b_re, v_ssm_b_im, v_ssm_c_re, v_ssm_c_im, v_ssm_d, v_w_glu, v_b_glu, v_w_ssm_br, v_w_attn_br, v_w_mem_kv, v_w_mem_br, v_w_o, v_norm2_g, v_w_up, v_w_down, v_final_g):
    args = dict(locals())
    w = {n: args[n] for n in ALL_W}
    m = {n: args["m_" + n] for n in ALL_W}
    v = {n: args["v_" + n] for n in ALL_W}
    my_c = lax.axis_index("c").astype(jnp.int32).reshape(1)
    my_chip = (2 * lax.axis_index("x") + lax.axis_index("y")).astype(jnp.int32).reshape(1)

    w_pack = [_pack_group(w, names) for names in GROUPS]
    wb = {}
    for gi in (1, 0):
        w_all = _allgather(w_pack[gi].astype(BF16), "allgather_weights%d" % gi)
        for n, part in _split_group(w_all, GROUPS[gi]).items():
            wb[n] = _full_stored(part, n)
    mlp_names = GROUPS[MLP_GROUP]
    g_sems, g_src, g_land, g_token = _split_start(w_pack[MLP_GROUP].astype(BF16), N_DEV, _gather_plan, w_all,
                                                  "mlp_weights_gather_start")

    def mlp_weights(after):
        src, land = _split_wait(g_sems, g_src, g_land, _gather_wait_plan, after, "mlp_weights_gather_wait")
        my_index = (4 * lax.axis_index("x") + 2 * lax.axis_index("y") + lax.axis_index("c")).astype(jnp.int32)
        zero = jnp.zeros((), jnp.int32)
        w_mlp = lax.dynamic_update_slice(land, src[None], (my_index, zero, zero))
        return {n: _full_stored(part, n) for n, part in _split_group(w_mlp, mlp_names).items()}

    pending = {}

    def grads_ready(gi, grads):
        g_pack = jnp.concatenate([_stacked_stored(grads[n], n) for n in GROUPS[gi]], axis=1)
        t1 = _pair_exchange(g_pack, "grad_pair_exchange%d" % gi)
        p_sum, p_bf = _pair_sum(g_pack, t1, my_c, "grad_pair_sum%d" % gi, GROUP_TR[gi])
        sems, src, land, token = _split_start(p_bf, 3, _chip_plan, p_sum, "grad_chip_exchange_start%d" % gi)
        pending[gi] = (p_sum, sems, src, land)
        return token

    sp = {
        "norm1_g": norm1_g + g_token[:1, :1], "mem_norm_g": mem_norm_g, "b_gate": b_gate, "b_glu": b_glu, "norm2_g": norm2_g,
        "final_g": final_g.reshape(1, D_MODEL),
        "ssm_lambda_re": ssm_lambda_re[0], "ssm_lambda_im": ssm_lambda_im[0], "ssm_log_dt": ssm_log_dt[0],
        "ssm_b_re": ssm_b_re[0], "ssm_b_im": ssm_b_im[0], "ssm_c_re": ssm_c_re[0], "ssm_c_im": ssm_c_im[0],
        "ssm_d": ssm_d[0],
    }
    loss, grad_x, gs = _local_step(x[0], mem[0], loss_target[0], wb, sp, mlp_weights, grads_ready)
    loss = lax.psum(loss[0, 0], ("x", "y", "c"))

    big_g = {}
    for gi, names in enumerate(GROUPS):
        p_sum, sems, src, land = pending[gi]
        t2 = _split_wait(sems, src, land, _chip_plan, grad_x, "grad_chip_exchange_wait%d" % gi)[1]
        g_pack = _grad_sum(p_sum, t2, my_chip, "grad_sum%d" % gi, GROUP_TR[gi])
        for n, part in _split_group(g_pack, names).items():
            big_g[n] = _unstored(part, n)
    rows_of = lambda d, names: [d[n].reshape(d[n].shape[-2:]) for n in names]
    big_out = _adam_many(rows_of(big_g, BIG), rows_of(w, BIG), rows_of(m, BIG), rows_of(v, BIG), 8, "adam_big")
    big = [big_g] + [{n: a[None] for n, a in zip(BIG, outs)} for outs in big_out]

    sg_shapes = [gs[n].shape for n in _SMALL_GRAD_ORDER]
    sg_all = _allgather(_pack([gs[n] for n in _SMALL_GRAD_ORDER]), "allgather_small_grads")
    sg = dict(zip(_SMALL_GRAD_ORDER, _unpack(_sum8(sg_all, "sum_small_grads"), sg_shapes)))
    _, disc_vjp = jax.vjp(_discretize, sp["ssm_lambda_re"], sp["ssm_lambda_im"], sp["ssm_log_dt"],
                          sp["ssm_b_re"], sp["ssm_b_im"])
    d_lre, d_lim, d_ldt, d_bre, d_bim = disc_vjp((sg["a_re"].reshape(SSM_G, SSM_P), sg["a_im"].reshape(SSM_G, SSM_P),
                                                  sg["bb_re"], sg["bb_im"]))
    small_grad = {
        "norm1_g": sg["norm1_g"], "mem_norm_g": sg["mem_norm_g"], "b_gate": sg["b_gate"],
        "ssm_lambda_re": d_lre, "ssm_lambda_im": d_lim, "ssm_log_dt": d_ldt, "ssm_b_re": d_bre, "ssm_b_im": d_bim,
        "ssm_c_re": sg["ssm_c_re"], "ssm_c_im": sg["ssm_c_im"], "ssm_d": sg["ssm_d"], "b_glu": sg["b_glu"],
        "norm2_g": sg["norm2_g"], "final_g": sg["final_g"],
    }
    small_grad = {n: small_grad[n].reshape(w[n].shape) for n in SMALL}

    def squeezed(a):
        return a.reshape(a.shape[1:]) if a.ndim > 2 else a.reshape(1, -1)

    sq = lambda d: [squeezed(d[n]) for n in SMALL]
    small_out = _adam_many(sq(small_grad), sq(w), sq(m), sq(v), 1, "adam_small")
    small = [small_grad] + [{n: a.reshape(w[n].shape) for n, a in zip(SMALL, outs)} for outs in small_out]

    outs = [loss, grad_x[None]]
    for kind in range(4):
        for n in ALL_W:
            outs.append(big[kind][n] if n in BIG else small[kind][n])
    return tuple(outs)
```

```python
import math

import numpy as np
import jax
import jax.numpy as jnp
from jax import lax
from jax.experimental import pallas as pl
from jax.experimental.pallas import tpu as pltpu

F32 = jnp.float32
BF16 = jnp.bfloat16
_MXU = jnp.bfloat16

D_MODEL = 1024
SSM_G, SSM_H, SSM_P = 32, 16, 64
SSM_W = SSM_G * SSM_H
SSM_S = SSM_G * SSM_P
SSM_BD = 4
ATT_E = 64
ATT_HG = 4
ATT_GW = ATT_HG * ATT_E
ATT_WIN = 128
ATT_QB = 8
DILATIONS = (1, 4, 16)
MEM_H, MEM_E = 4, 128
MEM_W = MEM_H * MEM_E
ZA_W = SSM_W + 9 * ATT_GW + MEM_W
ZG_W = 3 * D_MODEL
IN_W = ZA_W + ZG_W
RMS_EPS = 1e-6
NEG_INF = -1e30

ADAM_LR, ADAM_B1, ADAM_B2, ADAM_EPS, ADAM_WD, ADAM_STEP = 0.001, 0.9, 0.999, 1e-08, 0.01, 10

N_DEV = 8
PACK_C = 512
_VMEM_LIMIT = 56 * 1024 * 1024
SUBLANES = 16
SCAN_SEG = 128
SCAN_CHAINS = 4
SCAN_UNROLL = 4
SCAN_W = 128

BIG = ("w_in", "w_glu", "w_ssm_br", "w_attn_br", "w_mem_kv", "w_mem_br", "w_o", "w_up", "w_down")
BIG_SHAPE = {
    "w_in": (D_MODEL, IN_W, 1), "w_glu": (SSM_W, SSM_W, 0), "w_ssm_br": (SSM_W, D_MODEL, 1),
    "w_attn_br": (ATT_GW, D_MODEL, 1), "w_mem_kv": (D_MODEL, 2 * MEM_W, 0), "w_mem_br": (MEM_W, D_MODEL, 1),
    "w_o": (D_MODEL, D_MODEL, 0), "w_up": (D_MODEL, 4 * D_MODEL, 1), "w_down": (4 * D_MODEL, D_MODEL, 0),
}
SMALL = ("norm1_g", "mem_norm_g", "b_gate", "ssm_lambda_re", "ssm_lambda_im", "ssm_log_dt", "ssm_b_re",
         "ssm_b_im", "ssm_c_re", "ssm_c_im", "ssm_d", "b_glu", "norm2_g", "final_g")
ALL_W = ("norm1_g", "mem_norm_g", "w_in", "b_gate", "ssm_lambda_re", "ssm_lambda_im", "ssm_log_dt", "ssm_b_re",
         "ssm_b_im", "ssm_c_re", "ssm_c_im", "ssm_d", "w_glu", "b_glu", "w_ssm_br", "w_attn_br", "w_mem_kv",
         "w_mem_br", "w_o", "norm2_g", "w_up", "w_down", "final_g")


def _params(sem):
    return pltpu.CompilerParams(dimension_semantics=sem, vmem_limit_bytes=_VMEM_LIMIT)


def _pick(n, cap):
    if n <= cap:
        return n
    t = (cap // 128) * 128
    while n % t:
        t -= 128
    return t


def _mm(a, b, outs, *, name, ta=False, tb=False, epi=None, mn=(), rows=(), pair2=None, bd=0, n_sums=0,
        tm=1024, tn=1024, tk=2048):
    ab = [a, b] + (list(pair2) if pair2 is not None else [])
    planes = [op[1] if isinstance(op, tuple) else None for op in ab]
    ab = [op[0] if isinstance(op, tuple) else op for op in ab]
    a_shape, b_shape = ab[0].shape[-2:], ab[1].shape[-2:]
    m = a_shape[1] if ta else a_shape[0]
    k = a_shape[0] if ta else a_shape[1]
    n = b_shape[0] if tb else b_shape[1]
    assert k == (b_shape[1] if tb else b_shape[0]), (name, a_shape, b_shape)
    out_n = n
    if bd and ta:
        assert not tb
        tm, tn, tk = m // bd, n // bd, _pick(k, tk)
        grid, out_n = (bd, 1, k // tk), tn
        a_blk = ((tk, tm), lambda i, j, kk: (kk, i))
        b_blk = ((tk, tn), lambda i, j, kk: (kk, i))
        mn_spec = pl.BlockSpec((tm, tn), lambda i, j, kk: (i, 0))
    elif bd:
        tm, tn, tk = _pick(m, tm), n // bd, k // bd
        grid = (m // tm, bd, 1)
        a_blk = ((tm, tk), lambda i, j, kk: (i, j))
        b_blk = ((tn, tk) if tb else (tk, tn), lambda i, j, kk: (j, j))
        mn_spec = pl.BlockSpec((tm, tn), lambda i, j, kk: (i, j))
    else:
        tm, tn, tk = _pick(m, tm), _pick(n, tn), _pick(k, tk)
        grid = (m // tm, n // tn, k // tk)
        a_blk = ((tk, tm), lambda i, j, kk: (kk, i)) if ta else ((tm, tk), lambda i, j, kk: (i, kk))
        b_blk = ((tn, tk), lambda i, j, kk: (j, kk)) if tb else ((tk, tn), lambda i, j, kk: (kk, j))
        mn_spec = pl.BlockSpec((tm, tn), lambda i, j, kk: (i, j))

    def operand_spec(blk, plane):
        shape, imap = blk
        if plane is None:
            return pl.BlockSpec(shape, imap)
        return pl.BlockSpec((None,) + shape, lambda i, j, kk: (plane,) + imap(i, j, kk))

    ab_specs = [operand_spec(a_blk if q % 2 == 0 else b_blk, p) for q, p in enumerate(planes)]
    mn_arrays = [e[0] if isinstance(e, tuple) else e for e in mn]
    mn_specs = [pl.BlockSpec((tm, tn), lambda i, j, kk, c=e[1]: (i, c)) if isinstance(e, tuple) else mn_spec
                for e in mn]
    nk = grid[2]
    row_spec = pl.BlockSpec((1, tn), lambda i, j, kk: (0, j))
    n_ex, n_out = len(mn) + len(rows), len(outs)
    assert n_sums == 0 or (grid[1] == 1 and not bd)
    dims = (((0 if ta else 1,), (1 if tb else 0,)), ((), ()))

    def body(*refs):
        ab_refs, rest = refs[:len(ab)], refs[len(ab):]
        ex, o_refs, acc = rest[:n_ex], rest[n_ex:n_ex + n_out], rest[-1]
        s_refs = rest[n_ex + n_out:n_ex + n_out + n_sums]
        first_row_tile = pl.program_id(0) == 0
        kk = pl.program_id(2)

        @pl.when(kk == 0)
        def _():
            acc[...] = jnp.zeros_like(acc)

        for a_ref, b_ref in zip(ab_refs[0::2], ab_refs[1::2]):
            acc[...] += lax.dot_general(a_ref[...].astype(_MXU), b_ref[...].astype(_MXU), dims,
                                        preferred_element_type=F32)

        @pl.when(kk == nk - 1)
        def _():
            vals = (acc[...],) if epi is None else epi(acc[...], *[r[...] for r in ex])
            for r, v in zip(o_refs, vals):
                r[...] = v.astype(r.dtype)
            for r, v in zip(s_refs, vals[n_out:]):
                r[...] = jnp.where(first_row_tile, v, r[...] + v)

    res = pl.pallas_call(
        body, grid=grid,
        in_specs=ab_specs + mn_specs + [row_spec] * len(rows),
        out_specs=[mn_spec] * n_out + [row_spec] * n_sums,
        out_shape=[jax.ShapeDtypeStruct((m, out_n), dt) for dt in outs]
        + [jax.ShapeDtypeStruct((1, out_n), F32)] * n_sums,
        scratch_shapes=[pltpu.VMEM((tm, tn), F32)],
        compiler_params=_params(("arbitrary" if n_sums else "parallel", "parallel", "arbitrary")), name=name,
    )(*ab, *mn_arrays, *rows)
    return res[0] if n_out + n_sums == 1 else res


def _ew(fn, rows, bcs, out_rows, out_accs, *, name, tm=256):
    r = rows[0].shape[0]
    tm = min(tm, r)
    assert r % tm == 0
    nr, nb, no, na = len(rows), len(bcs), len(out_rows), len(out_accs)

    def body(*refs):
        i = pl.program_id(0)
        r_in, b_in = refs[:nr], refs[nr:nr + nb]
        o_r, o_a = refs[nr + nb:nr + nb + no], refs[nr + nb + no:]
        outs, accs = fn([x[...] for x in r_in], [x[...] for x in b_in])
        for ref, v in zip(o_r, outs):
            ref[...] = v.astype(ref.dtype)
        if na:
            @pl.when(i == 0)
            def _():
                for ref in o_a:
                    ref[...] = jnp.zeros_like(ref)

            for ref, v in zip(o_a, accs):
                ref[...] += v

    res = pl.pallas_call(
        body, grid=(r // tm,),
        in_specs=[pl.BlockSpec((tm, x.shape[1]), lambda i: (i, 0)) for x in rows]
        + [pl.BlockSpec((1, x.shape[1]), lambda i: (0, 0)) for x in bcs],
        out_specs=[pl.BlockSpec((tm, c), lambda i: (i, 0)) for c, _ in out_rows]
        + [pl.BlockSpec((1, c), lambda i: (0, 0)) for c in out_accs],
        out_shape=[jax.ShapeDtypeStruct((r, c), dt) for c, dt in out_rows]
        + [jax.ShapeDtypeStruct((1, c), F32) for c in out_accs],
        compiler_params=_params(("arbitrary",)), name=name,
    )(*rows, *bcs)
    return res


def _colsum(x):
    return jnp.sum(x, axis=0, keepdims=True)


def _sigmoid(x):
    return 1.0 / (1.0 + jnp.exp(-x))


def _rms_bwd_tile(xv, dv, g):
    rs = lax.rsqrt(jnp.mean(xv * xv, axis=-1, keepdims=True) + RMS_EPS)
    gd = dv * g
    dx = rs * gd - xv * (rs * rs * rs) * jnp.mean(gd * xv, axis=-1, keepdims=True)
    return dx, _colsum(dv * xv * rs)


def _rms_fwd(x, g, name):
    def fn(r, b):
        xv = r[0]
        rs = lax.rsqrt(jnp.mean(xv * xv, axis=-1, keepdims=True) + RMS_EPS)
        return [xv * rs * b[0]], []
    return _ew(fn, [x], [g], [(x.shape[1], BF16)], [], name=name)[0]


def _rms_bwd(x, dn, res, g, name):
    def fn(r, b):
        dx, dg = _rms_bwd_tile(r[0], r[1], b[0])
        if res is not None:
            dx = dx + r[2]
        return [dx], [dg]
    rows = [x, dn] + ([res] if res is not None else [])
    return _ew(fn, rows, [g], [(x.shape[1], F32)], [x.shape[1]], name=name)


def _scan_order(x):
    l, c = x.shape
    return x.reshape(l // (SUBLANES * SCAN_SEG), SUBLANES, SCAN_SEG, c).transpose(0, 2, 1, 3).reshape(l, c)


def _time_order(x):
    l, c = x.shape
    return x.reshape(l // (SUBLANES * SCAN_SEG), SCAN_SEG, SUBLANES, c).transpose(0, 2, 1, 3).reshape(l, c)


def _ssm_scan(x, w_re, w_im, a_pair, *, reverse, s_fwd=None, u=None, name):
    l = x.shape[0]
    seg, w = SCAN_SEG, SCAN_W
    bd_w = SSM_W // SSM_BD
    tiles_per_bd = SSM_S // SSM_BD // w
    nch = min(SCAN_CHAINS, l // (SUBLANES * seg))
    chain_rows = SUBLANES * seg
    tb = nch * chain_rows
    nt = l // tb
    with_da = s_fwd is not None
    assert reverse or not with_da

    def tt(t):
        return nt - 1 - t if reverse else t

    def body(*refs):
        if with_da:
            (x_ref, wr_ref, wi_ref, a_ref, sf_ref, sp_ref, u_ref, s_ref, da_ref, dw_ref, dx_ref,
             p_ref, c_ref, b_scr) = refs
        else:
            x_ref, wr_ref, wi_ref, a_ref, s_ref, p_ref, c_ref, b_scr = refs
        t_blk = pl.program_id(1)
        ar, ai = a_ref[0], a_ref[1]

        @pl.when(t_blk == 0)
        def _():
            def pstep(i, carry):
                pr, pi = carry
                p_ref[0, pl.ds(i, 1), :] = pr
                p_ref[1, pl.ds(i, 1), :] = pi
                return pr * ar - pi * ai, pr * ai + pi * ar

            lax.fori_loop(0, seg, pstep, (ar, ai))
            c_ref[...] = jnp.zeros_like(c_ref)
            if with_da:
                da_ref[...] = jnp.zeros_like(da_ref)
                dw_ref[...] = jnp.zeros_like(dw_ref)
                dx_ref[...] = jnp.zeros_like(dx_ref)

        xb = x_ref[...].astype(_MXU)
        b_scr[:, :w] = jnp.dot(xb, wr_ref[...], preferred_element_type=F32)
        b_scr[:, w:] = jnp.dot(xb, wi_ref[...], preferred_element_type=F32)
        arb, aib = jnp.broadcast_to(ar, (SUBLANES, w)), jnp.broadcast_to(ai, (SUBLANES, w))
        zero = jnp.zeros((SUBLANES, w), F32)

        def tile(g, step):
            return pl.ds(pl.multiple_of(g * chain_rows + step * SUBLANES, SUBLANES), SUBLANES)

        def rows(g, i):
            return tile(g, seg - 1 - i if reverse else i)

        def local_step(i, carry):
            out = []
            for g in range(nch):
                sr, si = carry[2 * g], carry[2 * g + 1]
                idx = rows(g, i)
                sr, si = arb * sr - aib * si + b_scr[idx, :w], arb * si + aib * sr + b_scr[idx, w:]
                b_scr[idx, :w] = sr
                b_scr[idx, w:] = si
                out += [sr, si]
            return tuple(out)

        def unrolled(step_fn, first):
            def trip(q, carry):
                for r in range(SCAN_UNROLL):
                    carry = step_fn(first + q * SCAN_UNROLL + r, carry)
                return carry
            return trip

        ends = lax.fori_loop(0, seg // SCAN_UNROLL, unrolled(local_step, 0), (zero,) * (2 * nch))

        a_seg_r, a_seg_i = p_ref[0, seg - 1:seg, :], p_ref[1, seg - 1:seg, :]
        cr, ci = c_ref[0], c_ref[1]
        sub = lax.broadcasted_iota(jnp.int32, (SUBLANES, w), 0)
        ins = [[zero, zero] for _ in range(nch)]
        order = [(g, k) for g in range(nch) for k in range(SUBLANES)]
        for g, k in (order[::-1] if reverse else order):
            ins[g] = [jnp.where(sub == k, cr, ins[g][0]), jnp.where(sub == k, ci, ins[g][1])]
            er, ei = ends[2 * g][k:k + 1], ends[2 * g + 1][k:k + 1]
            cr, ci = er + a_seg_r * cr - a_seg_i * ci, ei + a_seg_r * ci + a_seg_i * cr
        c_ref[0] = cr
        c_ref[1] = ci

        def fix(g, i):
            idx = rows(g, i)
            pr, pi = p_ref[0, pl.ds(i, 1), :], p_ref[1, pl.ds(i, 1), :]
            sr = b_scr[idx, :w] + pr * ins[g][0] - pi * ins[g][1]
            si = b_scr[idx, w:] + pr * ins[g][1] + pi * ins[g][0]
            s_ref.at[0][idx, :] = sr.astype(s_ref.dtype)
            s_ref.at[1][idx, :] = si.astype(s_ref.dtype)
            return sr, si

        if not with_da:
            def fix_step(i, carry):
                for g in range(nch):
                    fix(g, i)
                return carry

            lax.fori_loop(0, seg // SCAN_UNROLL, unrolled(fix_step, 0), 0)
        else:
            def adj_step(i, acc):
                acc_r, acc_i = acc
                for g in range(nch):
                    lr, li = fix(g, i)
                    prev = tile(g, seg - 2 - i)
                    fr, fi = sf_ref.at[0][prev, :].astype(F32), sf_ref.at[1][prev, :].astype(F32)
                    acc_r, acc_i = acc_r + lr * fr + li * fi, acc_i + li * fr - lr * fi
                return acc_r, acc_i

            acc = lax.fori_loop(0, seg // SCAN_UNROLL - 1, unrolled(adj_step, 0), (zero, zero))
            for i in range(seg - SCAN_UNROLL, seg - 1):
                acc = adj_step(i, acc)
            acc_r, acc_i = acc
            first_block = tt(t_blk) == 0
            for g in range(nch):
                lr, li = fix(g, seg - 1)
                seg_ends = tile(g, seg - 1)
                if g == 0:
                    pvr = jnp.where(first_block, 0.0, sp_ref[0, SUBLANES - 1:SUBLANES, :].astype(F32))
                    pvi = jnp.where(first_block, 0.0, sp_ref[1, SUBLANES - 1:SUBLANES, :].astype(F32))
                else:
                    pvr = sf_ref[0, g * chain_rows - 1:g * chain_rows, :].astype(F32)
                    pvi = sf_ref[1, g * chain_rows - 1:g * chain_rows, :].astype(F32)
                fr = jnp.where(sub == 0, pvr, pltpu.roll(sf_ref.at[0][seg_ends, :].astype(F32), 1, 0))
                fi = jnp.where(sub == 0, pvi, pltpu.roll(sf_ref.at[1][seg_ends, :].astype(F32), 1, 0))
                acc_r = acc_r + lr * fr + li * fi
                acc_i = acc_i + li * fr - lr * fi
            da_ref[0] += jnp.sum(acc_r, axis=0, keepdims=True)
            da_ref[1] += jnp.sum(acc_i, axis=0, keepdims=True)
            for plane in range(2):
                dw_ref[plane] += _tn_dot(u_ref[...], s_ref[plane])
                dx_ref[plane] += _tn_dot(xb, sf_ref[plane])

    x_spec = pl.BlockSpec((tb, bd_w), lambda j, t: (tt(t), j // tiles_per_bd))
    w_spec = pl.BlockSpec((bd_w, w), lambda j, t: (j // tiles_per_bd, j))
    d_spec = pl.BlockSpec((2, bd_w, w), lambda j, t: (0, j // tiles_per_bd, j % tiles_per_bd))
    a_spec = pl.BlockSpec((2, 1, w), lambda j, t: (0, 0, j))
    s_spec = pl.BlockSpec((2, tb, w), lambda j, t: (0, tt(t), j))
    in_specs, args = [x_spec, w_spec, w_spec, a_spec], [x, w_re, w_im, a_pair]
    out_specs, out_shape = [s_spec], [jax.ShapeDtypeStruct((2, l, SSM_S), BF16)]
    scratch = [pltpu.VMEM((2, seg, w), F32), pltpu.VMEM((2, 1, w), F32), pltpu.VMEM((tb, 2 * w), F32)]
    if with_da:
        in_specs += [s_spec, pl.BlockSpec((2, SUBLANES, w),
                                          lambda j, t: (0, jnp.maximum(tt(t) * (tb // SUBLANES) - 1, 0), j)),
                     x_spec]
        args += [s_fwd, s_fwd, u]
        out_specs += [a_spec, d_spec, d_spec]
        out_shape += ([jax.ShapeDtypeStruct((2, 1, SSM_S), F32)]
                      + [jax.ShapeDtypeStruct((2, SSM_W, SSM_S // SSM_BD), F32)] * 2)
    res = pl.pallas_call(
        body, grid=(SSM_S // w, nt), in_specs=in_specs, out_specs=out_specs, out_shape=out_shape,
        scratch_shapes=scratch, compiler_params=_params(("parallel", "arbitrary")), name=name,
    )(*args)
    return res if with_da else res[0]


def _nt_dot(x, y):
    return lax.dot_general(x.astype(_MXU), y.astype(_MXU), (((1,), (1,)), ((), ())), preferred_element_type=F32)


def _tn_dot(x, y):
    return lax.dot_general(x.astype(_MXU), y.astype(_MXU), (((0,), (0,)), ((), ())), preferred_element_type=F32)


def _nn_dot(x, y):
    return jnp.dot(x.astype(_MXU), y.astype(_MXU), preferred_element_type=F32)


def _attn_mask2(gb, nb):
    qi = lax.broadcasted_iota(jnp.int32, (ATT_WIN, 2 * ATT_WIN), 0)
    c = lax.broadcasted_iota(jnp.int32, (ATT_WIN, 2 * ATT_WIN), 1)
    has_prev = (gb % nb) != 0
    prev_ok = jnp.logical_and(jnp.logical_and(c < ATT_WIN, c >= qi), has_prev)
    own_ok = jnp.logical_and(c >= ATT_WIN, c - ATT_WIN <= qi)
    return jnp.logical_or(prev_ok, own_ok)


def _attn_specs():
    cur = pl.BlockSpec((ATT_QB * ATT_WIN, ATT_GW), lambda i: (i, 0))
    prev = pl.BlockSpec((ATT_WIN, ATT_GW), lambda i: (jnp.maximum(ATT_QB * i - 1, 0), 0))
    return cur, prev


def _attn_fwd(q, k, v, nb, name):
    l = q.shape[0]
    scale = ATT_E ** -0.5
    w = ATT_WIN

    def body(q_ref, kc_ref, kp_ref, vc_ref, vp_ref, o_ref, lse_ref):
        i = pl.program_id(0)
        masks = [_attn_mask2(ATT_QB * i + b, nb) for b in range(ATT_QB)]
        for h in range(ATT_HG):
            sl = slice(h * ATT_E, (h + 1) * ATT_E)
            k_ext = jnp.concatenate([kp_ref[:, sl], kc_ref[:, sl]], axis=0)
            v_ext = jnp.concatenate([vp_ref[:, sl], vc_ref[:, sl]], axis=0)
            for b in range(ATT_QB):
                r, kr = slice(b * w, (b + 1) * w), slice(b * w, (b + 2) * w)
                s = jnp.where(masks[b], _nt_dot(q_ref[r, sl], k_ext[kr]) * scale, NEG_INF)
                mx = jnp.max(s, axis=-1, keepdims=True)
                p = jnp.exp(s - mx)
                den = jnp.sum(p, axis=-1, keepdims=True)
                o_ref[r, sl] = _nn_dot(p, v_ext[kr]) / den
                lse_ref[r, sl] = jnp.broadcast_to(mx + jnp.log(den), (w, ATT_E))

    cur, prev = _attn_specs()
    return pl.pallas_call(
        body, grid=(l // (ATT_QB * w),), in_specs=[cur, cur, prev, cur, prev], out_specs=[cur, cur],
        out_shape=[jax.ShapeDtypeStruct((l, ATT_GW), F32)] * 2,
        compiler_params=_params(("parallel",)), name=name,
    )(q, k, k, v, v)


def _attn_bwd(q, k, v, do, lse, dd, nb, name):
    l = q.shape[0]
    scale = ATT_E ** -0.5
    w = ATT_WIN
    nblk = l // w

    def body(q_ref, kc_ref, kp_ref, vc_ref, vp_ref, do_ref, lse_ref, dd_ref, qn_ref, don_ref, lsen_ref, ddn_ref,
             dq_ref, dk_ref, dv_ref, dk_acc, dv_acc):
        i = pl.program_id(0)
        masks = [_attn_mask2(ATT_QB * i + b, nb) for b in range(ATT_QB)]
        nxt = ATT_QB * (i + 1)
        nxt_attends = jnp.logical_and(nxt < nblk, (nxt % nb) != 0)
        qi = lax.broadcasted_iota(jnp.int32, (w, w), 0)
        kj = lax.broadcasted_iota(jnp.int32, (w, w), 1)
        mask_n = jnp.logical_and(kj >= qi, nxt_attends)
        dk_acc[...] = jnp.zeros_like(dk_acc)
        dv_acc[...] = jnp.zeros_like(dv_acc)
        for h in range(ATT_HG):
            sl, col = slice(h * ATT_E, (h + 1) * ATT_E), slice(h * ATT_E, h * ATT_E + 1)
            k_ext = jnp.concatenate([kp_ref[:, sl], kc_ref[:, sl]], axis=0)
            v_ext = jnp.concatenate([vp_ref[:, sl], vc_ref[:, sl]], axis=0)
            for b in range(ATT_QB):
                r, kr = slice(b * w, (b + 1) * w), slice(b * w, (b + 2) * w)
                qh, doh, k2, v2 = q_ref[r, sl], do_ref[r, sl], k_ext[kr], v_ext[kr]
                p = jnp.where(masks[b], jnp.exp(_nt_dot(qh, k2) * scale - lse_ref[r, col]), 0.0)
                ds = p * (_nt_dot(doh, v2) - dd_ref[r, col]) * scale
                dq_ref[r, sl] = _nn_dot(ds, k2).astype(dq_ref.dtype)
                dk2, dv2 = _tn_dot(ds, qh), _tn_dot(p, doh)
                dk_acc[r, sl] += dk2[w:]
                dv_acc[r, sl] += dv2[w:]
                if b > 0:
                    rp = slice((b - 1) * w, b * w)
                    dk_acc[rp, sl] += dk2[:w]
                    dv_acc[rp, sl] += dv2[:w]
            last = slice((ATT_QB - 1) * w, ATT_QB * w)
            kl, vl, qn, don = kc_ref[last, sl], vc_ref[last, sl], qn_ref[:, sl], don_ref[:, sl]
            pn = jnp.where(mask_n, jnp.exp(_nt_dot(qn, kl) * scale - lsen_ref[:, col]), 0.0)
            dsn = pn * (_nt_dot(don, vl) - ddn_ref[:, col]) * scale
            dk_acc[last, sl] += _tn_dot(dsn, qn)
            dv_acc[last, sl] += _tn_dot(pn, don)
        dk_ref[...] = dk_acc[...].astype(dk_ref.dtype)
        dv_ref[...] = dv_acc[...].astype(dv_ref.dtype)

    cur, prev = _attn_specs()
    nxt_spec = pl.BlockSpec((w, ATT_GW), lambda i: (jnp.minimum(ATT_QB * (i + 1), nblk - 1), 0))
    return pl.pallas_call(
        body, grid=(l // (ATT_QB * w),),
        in_specs=[cur, cur, prev, cur, prev, cur, cur, cur, nxt_spec, nxt_spec, nxt_spec, nxt_spec],
        out_specs=[cur] * 3, out_shape=[jax.ShapeDtypeStruct((l, ATT_GW), BF16)] * 3,
        scratch_shapes=[pltpu.VMEM((ATT_QB * w, ATT_GW), F32)] * 2,
        compiler_params=_params(("parallel",)), name=name,
    )(q, k, k, v, v, do, lse, dd, q, do, lse, dd)


def _to_perm(a, d):
    if d == 1:
        return a
    l, c = a.shape
    return a.reshape(l // d, d, c).transpose(1, 0, 2).reshape(l, c)


def _from_perm(a, d):
    if d == 1:
        return a
    l, c = a.shape
    return a.reshape(d, l // d, c).transpose(1, 0, 2).reshape(l, c)


def _mem_probs(qh, kh):
    s = _nt_dot(qh, kh) * (MEM_E ** -0.5)
    e = jnp.exp(s - jnp.max(s, axis=-1, keepdims=True))
    return e / jnp.sum(e, axis=-1, keepdims=True)


def _mem_fwd(mq, kv, name, tm=512):
    l, nm = mq.shape[0], kv.shape[0]

    def body(q_ref, kv_ref, o_ref):
        for h in range(MEM_H):
            sl = slice(h * MEM_E, (h + 1) * MEM_E)
            p = _mem_probs(q_ref[:, sl], kv_ref[:, sl])
            o_ref[:, sl] = _nn_dot(p, kv_ref[:, MEM_W + h * MEM_E:MEM_W + (h + 1) * MEM_E]).astype(o_ref.dtype)

    return pl.pallas_call(
        body, grid=(l // tm,),
        in_specs=[pl.BlockSpec((tm, MEM_W), lambda i: (i, 0)), pl.BlockSpec((nm, 2 * MEM_W), lambda i: (0, 0))],
        out_specs=pl.BlockSpec((tm, MEM_W), lambda i: (i, 0)),
        out_shape=jax.ShapeDtypeStruct((l, MEM_W), BF16),
        compiler_params=_params(("parallel",)), name=name,
    )(mq, kv)


def _mem_bwd(mq, kv, dmo, name, tm=512):
    l, nm = mq.shape[0], kv.shape[0]
    scale = MEM_E ** -0.5

    def body(q_ref, kv_ref, do_ref, dq_ref, dkv_ref):
        @pl.when(pl.program_id(0) == 0)
        def _():
            dkv_ref[...] = jnp.zeros_like(dkv_ref)

        for h in range(MEM_H):
            sl = slice(h * MEM_E, (h + 1) * MEM_E)
            vsl = slice(MEM_W + h * MEM_E, MEM_W + (h + 1) * MEM_E)
            qh, kh, vh, doh = q_ref[:, sl], kv_ref[:, sl], kv_ref[:, vsl], do_ref[:, sl]
            p = _mem_probs(qh, kh)
            dp = _nt_dot(doh, vh)
            ds = p * (dp - jnp.sum(dp * p, axis=-1, keepdims=True)) * scale
            dq_ref[:, sl] = _nn_dot(ds, kh).astype(dq_ref.dtype)
            dkv_ref[:, sl] += _tn_dot(ds, qh)
            dkv_ref[:, vsl] += _tn_dot(p, doh)

    row = pl.BlockSpec((tm, MEM_W), lambda i: (i, 0))
    full = pl.BlockSpec((nm, 2 * MEM_W), lambda i: (0, 0))
    return pl.pallas_call(
        body, grid=(l // tm,), in_specs=[row, full, row], out_specs=[row, full],
        out_shape=[jax.ShapeDtypeStruct((l, MEM_W), BF16), jax.ShapeDtypeStruct((nm, 2 * MEM_W), F32)],
        compiler_params=_params(("arbitrary",)), name=name,
    )(mq, kv, dmo)


def _discretize(lam_re, lam_im, log_dt, b_re, b_im):
    dt = jnp.exp(log_dt)[:, None]
    mag = jnp.exp(lam_re * dt)
    a_re, a_im = mag * jnp.cos(lam_im * dt), mag * jnp.sin(lam_im * dt)
    nr, ni = a_re - 1.0, a_im
    den = lam_re * lam_re + lam_im * lam_im
    coef_re = (nr * lam_re + ni * lam_im) / den
    coef_im = (ni * lam_re - nr * lam_im) / den
    bb_re = coef_re[..., None] * b_re - coef_im[..., None] * b_im
    bb_im = coef_re[..., None] * b_im + coef_im[..., None] * b_re
    return a_re, a_im, bb_re, bb_im


def _bd_in(bb):
    return jnp.einsum("gph,gk->ghkp", bb, jnp.eye(SSM_G, dtype=bb.dtype)).reshape(SSM_W, SSM_S)


def _bd_diag(x):
    gb = SSM_G // SSM_BD
    t = x.reshape(SSM_BD, gb, SSM_H, gb, SSM_P)
    return jnp.einsum("bghgp->bghp", t).reshape(SSM_G, SSM_H, SSM_P)


_ANY = pl.BlockSpec(memory_space=pl.ANY)
_MESH = pl.DeviceIdType.MESH


def _allgather(x, name):
    def body(x_ref, out_ref, send_sems, recv_sems, local_sem):
        mx, my, mc = lax.axis_index("x"), lax.axis_index("y"), lax.axis_index("c")
        me, sibling = (mx, my, mc), (mx, my, 1 - mc)
        chips = [(1 - mx, my), (mx, 1 - my), (1 - mx, 1 - my)]

        def blk(px, py, pc):
            return out_ref.at[4 * px + 2 * py + pc]

        def copy(k, block, to, src=None):
            return pltpu.make_async_remote_copy(
                src_ref=blk(*block) if src is None else src, dst_ref=blk(*block),
                send_sem=send_sems.at[k], recv_sem=recv_sems.at[k], device_id=to, device_id_type=_MESH)

        mine = pltpu.make_async_copy(x_ref, blk(*me), local_sem)
        mine.start()
        first = [copy(0, me, sibling, src=x_ref)]
        first += [copy(1 + j, me, (*chip, mc), src=x_ref) for j, chip in enumerate(chips)]
        for cp in first:
            cp.start()
        passed = [copy(4 + j, (*chip, mc), sibling) for j, chip in enumerate(chips)]
        for j, chip in enumerate(chips):
            copy(1 + j, (*chip, mc), me).wait_recv()
            passed[j].start()
        copy(0, sibling, me).wait_recv()
        for j, chip in enumerate(chips):
            copy(4 + j, (*chip, 1 - mc), me).wait_recv()
        for cp in first + passed:
            cp.wait_send()
        mine.wait()

    return pl.pallas_call(
        body, out_shape=jax.ShapeDtypeStruct((N_DEV,) + x.shape, x.dtype), in_specs=[_ANY], out_specs=_ANY,
        scratch_shapes=[pltpu.SemaphoreType.DMA((7,)), pltpu.SemaphoreType.DMA((7,)), pltpu.SemaphoreType.DMA],
        name=name,
    )(x)


def _pair_exchange(g, name):
    def body(g_ref, out_ref, send_sems, recv_sems):
        mx, my, mc = lax.axis_index("x"), lax.axis_index("y"), lax.axis_index("c")
        copies = [pltpu.make_async_remote_copy(
            src_ref=g_ref.at[2 * k + (1 - mc)], dst_ref=out_ref.at[k], send_sem=send_sems.at[k],
            recv_sem=recv_sems.at[k], device_id=(mx, my, 1 - mc), device_id_type=_MESH) for k in range(4)]
        for cp in copies:
            cp.start()
        for cp in copies:
            cp.wait()

    return pl.pallas_call(
        body, out_shape=jax.ShapeDtypeStruct((4,) + g.shape[1:], g.dtype), in_specs=[_ANY], out_specs=_ANY,
        scratch_shapes=[pltpu.SemaphoreType.DMA((4,)), pltpu.SemaphoreType.DMA((4,))], name=name,
    )(g)


_HBM = pl.BlockSpec(memory_space=pltpu.HBM)
_SEM = pl.BlockSpec(memory_space=pltpu.SEMAPHORE)
_EFFECT = pltpu.SideEffectType.DATAFLOW_SIDE_EFFECTING
_TOKEN = jax.ShapeDtypeStruct((8, 128), F32)


def _peer(rel):
    pos = (lax.axis_index("x"), lax.axis_index("y"), lax.axis_index("c"))
    return tuple(1 - p if (rel >> (2 - i)) & 1 else p for i, p in enumerate(pos))


def _index_of(dev):
    return 4 * dev[0] + 2 * dev[1] + dev[2]


def _split_copies(src_ref, land_ref, sems, plan):
    n = len(plan)
    return [pltpu.make_async_remote_copy(
        src_ref=src_ref if s is None else src_ref.at[s], dst_ref=land_ref.at[d], send_sem=sems[k],
        recv_sem=sems[n + k], device_id=peer, device_id_type=_MESH) for k, (s, d, peer) in enumerate(plan)]


def _split_start(src, n_land, plan_fn, after, name):
    blk = src.shape[-2:]
    land = lax.empty((n_land,) + blk, src.dtype)
    n = len(plan_fn())

    def body(src_ref, land_ref, after_ref, *outs):
        for cp in _split_copies(src_ref, land_ref, outs[:2 * n], plan_fn()):
            cp.start()
        outs[2 * n + 2][...] = jnp.zeros_like(outs[2 * n + 2])

    res = pl.pallas_call(
        body, name=name,
        out_shape=(pltpu.SemaphoreType.DMA(()),) * (2 * n)
        + (pltpu.HBM(src.shape, src.dtype), pltpu.HBM(land.shape, land.dtype), _TOKEN),
        in_specs=(_HBM, _HBM, _ANY),
        out_specs=(_SEM,) * (2 * n) + (_HBM, _HBM, pl.BlockSpec(memory_space=pltpu.VMEM)),
        input_output_aliases={0: 2 * n, 1: 2 * n + 1},
        compiler_params=pltpu.CompilerParams(has_side_effects=_EFFECT),
    )(pltpu.with_memory_space_constraint(src, pltpu.HBM), pltpu.with_memory_space_constraint(land, pltpu.HBM), after)
    return res[:2 * n], res[2 * n], res[2 * n + 1], res[2 * n + 2]


def _split_wait(sems, src, land, plan_fn, after, name):
    n = len(sems) // 2

    def body(src_ref, land_ref, *rest):
        for cp in _split_copies(src_ref, land_ref, rest[:2 * n], plan_fn()):
            cp.wait_send()
            cp.wait_recv()

    return pl.pallas_call(
        body, name=name,
        out_shape=(pltpu.HBM(src.shape, src.dtype), pltpu.HBM(land.shape, land.dtype)),
        in_specs=(_HBM, _HBM) + (_SEM,) * (2 * n) + (_ANY,), out_specs=(_HBM, _HBM),
        input_output_aliases={0: 0, 1: 1},
        compiler_params=pltpu.CompilerParams(has_side_effects=_EFFECT),
    )(src, land, *sems, after)


def _gather_plan():
    me = _index_of(_peer(0))
    return [(None, me, _peer(rel)) for rel in range(1, N_DEV)]


def _gather_wait_plan():
    return [(None, _index_of(_peer(rel)), _peer(rel)) for rel in range(1, N_DEV)]


def _chip_plan():
    return [(_index_of(_peer(rel)) // 2, j, _peer(rel)) for j, rel in enumerate((4, 2, 6))]


def _pair_sum(g, t1, my_c, name, tr):
    _, r, c = g.shape

    def body(c_ref, g_ref, t_ref, o_ref, ob_ref):
        s = g_ref[...] + t_ref[...]
        o_ref[...] = s
        ob_ref[...] = s.astype(BF16)

    blk = pl.BlockSpec((None, tr, c), lambda k, i, cr: (k, i, 0))
    return pl.pallas_call(
        body,
        grid_spec=pltpu.PrefetchScalarGridSpec(
            num_scalar_prefetch=1, grid=(4, r // tr),
            in_specs=[pl.BlockSpec((None, tr, c), lambda k, i, cr: (2 * k + cr[0], i, 0)), blk],
            out_specs=[blk, blk]),
        out_shape=[jax.ShapeDtypeStruct((4, r, c), F32), jax.ShapeDtypeStruct((4, r, c), BF16)],
        compiler_params=_params(("parallel", "parallel")), name=name,
    )(my_c, g, t1)


def _adam_math(g, w, m, v):
    m = ADAM_B1 * m + (1.0 - ADAM_B1) * g
    v = ADAM_B2 * v + (1.0 - ADAM_B2) * (g * g)
    m_hat = m / (1.0 - ADAM_B1 ** ADAM_STEP)
    v_hat = v / (1.0 - ADAM_B2 ** ADAM_STEP)
    delta = -ADAM_LR * (m_hat / (jnp.sqrt(v_hat) + ADAM_EPS) + ADAM_WD * w)
    return delta, m, v


def _grad_sum(p, t2, my_chip, name, tr):
    _, r, c = p.shape

    def body(k_ref, p_ref, t0_ref, t1_ref, t2_ref, g_out):
        g_out[...] = ((p_ref[...] + t0_ref[...].astype(F32)) + t1_ref[...].astype(F32)) + t2_ref[...].astype(F32)

    def rel(j):
        return pl.BlockSpec((None, tr, c), lambda i, kr: (j, i, 0))

    return pl.pallas_call(
        body,
        grid_spec=pltpu.PrefetchScalarGridSpec(
            num_scalar_prefetch=1, grid=(r // tr,),
            in_specs=[pl.BlockSpec((None, tr, c), lambda i, kr: (kr[0], i, 0)), rel(0), rel(1), rel(2)],
            out_specs=pl.BlockSpec((tr, c), lambda i, kr: (i, 0))),
        out_shape=jax.ShapeDtypeStruct((r, c), F32),
        compiler_params=_params(("parallel",)), name=name,
    )(my_chip, p, t2, t2, t2)


def _adam_many(g, w, m, v, row_tiles, name):
    n = len(g)

    def body(*refs):
        ins, outs = refs[:4 * n], refs[4 * n:]
        for i in range(n):
            res = _adam_math(ins[i][...], ins[n + i][...], ins[2 * n + i][...], ins[3 * n + i][...])
            for kind in range(3):
                outs[kind * n + i][...] = res[kind]

    def spec(a):
        blk = (a.shape[0] // row_tiles,) + a.shape[1:]
        return pl.BlockSpec(blk, lambda t, nd=a.ndim: (t,) + (0,) * (nd - 1))

    specs = [spec(a) for a in g]
    res = pl.pallas_call(
        body, grid=(row_tiles,), in_specs=specs * 4, out_specs=specs * 3,
        out_shape=[jax.ShapeDtypeStruct(a.shape, F32) for a in g] * 3,
        compiler_params=_params(("parallel",)), name=name,
    )(*g, *w, *m, *v)
    return res[:n], res[n:2 * n], res[2 * n:]


def _sum8(g8, name):
    _, r, c = g8.shape

    def body(g_ref, o_ref):
        acc = g_ref[0]
        for j in range(1, N_DEV):
            acc = acc + g_ref[j]
        o_ref[...] = acc

    return pl.pallas_call(
        body, grid=(1,), in_specs=[pl.BlockSpec((N_DEV, r, c), lambda i: (0, 0, 0))],
        out_specs=pl.BlockSpec((r, c), lambda i: (0, 0)), out_shape=jax.ShapeDtypeStruct((r, c), F32),
        compiler_params=_params(("arbitrary",)), name=name,
    )(g8)


def _pack(arrs, pad_rows=8):
    flat = jnp.concatenate([a.reshape(-1) for a in arrs])
    n = flat.shape[0]
    q = PACK_C * pad_rows
    tot = -(-n // q) * q
    if tot != n:
        flat = jnp.concatenate([flat, jnp.zeros((tot - n,), flat.dtype)])
    return flat.reshape(tot // PACK_C, PACK_C)


def _unpack(buf, shapes):
    flat = buf.reshape(-1)
    out, off = [], 0
    for s in shapes:
        n = int(np.prod(s))
        out.append(flat[off:off + n].reshape(s))
        off += n
    return out


GROUPS = (("w_in", "w_mem_kv", "w_o"),
          ("w_glu", "w_ssm_br", "w_mem_br", "w_attn_br"),
          ("w_up", "w_down"))
GROUP_TR = (528, 384, 512)
MLP_GROUP = 2
ATTN_BR_FOLD = 2


def _stored_shape(name):
    r, c, ax = BIG_SHAPE[name]
    rows, cols = (r // N_DEV, c) if ax == 0 else (c // N_DEV, r)
    return (rows // ATTN_BR_FOLD, cols * ATTN_BR_FOLD) if name == "w_attn_br" else (rows, cols)


def _stored(shard, name):
    a = shard[0].T if BIG_SHAPE[name][2] == 1 else shard[0]
    return a.reshape(_stored_shape(name))


def _unstored(a, name):
    r, c, ax = BIG_SHAPE[name]
    if ax == 0:
        return a.reshape(1, r // N_DEV, c)
    return a.reshape(c // N_DEV, r).T[None]


def _pack_group(d, names):
    return jnp.concatenate([_stored(d[n], n) for n in names], axis=0)


def _split_group(buf, names):
    out, off = {}, 0
    for n in names:
        rows = _stored_shape(n)[0]
        out[n] = buf[..., off:off + rows, :]
        off += rows
    return out


def _full_stored(stacked, name):
    r, c, ax = BIG_SHAPE[name]
    return stacked.reshape((r, c) if ax == 0 else (c, r))


def _stacked_stored(full, name):
    return full.reshape((N_DEV,) + _stored_shape(name))


def _gelu_parts(x):
    c0, c1 = math.sqrt(2.0 / math.pi), 0.044715
    th = jnp.tanh(c0 * (x + c1 * x * x * x))
    return th, c0, c1


def _local_step(x, mem, tgt, wb, sp, late_weights, grads_ready):
    l = x.shape[0]
    w_a, w_g = wb["w_in"][:ZA_W], wb["w_in"][ZA_W:]

    a_re, a_im, bb_re, bb_im = _discretize(sp["ssm_lambda_re"], sp["ssm_lambda_im"], sp["ssm_log_dt"],
                                           sp["ssm_b_re"], sp["ssm_b_im"])
    a_pair = jnp.stack([a_re.reshape(1, SSM_S), a_im.reshape(1, SSM_S)])
    a_conj = jnp.stack([a_re.reshape(1, SSM_S), -a_im.reshape(1, SSM_S)])
    b_re_t, b_im_t = _bd_in(bb_re).astype(BF16), _bd_in(bb_im).astype(BF16)
    c_re_t = _bd_in(sp["ssm_c_re"].transpose(0, 2, 1)).astype(BF16)
    c_im_t = (-_bd_in(sp["ssm_c_im"].transpose(0, 2, 1))).astype(BF16)
    d_row = sp["ssm_d"].reshape(1, SSM_W)

    n1 = _rms_fwd(x, sp["norm1_g"], "rms1")
    za = _mm(n1, w_a, [BF16], tb=True, name="in_proj_a", tn=1664)
    zg = _mm(n1, w_g, [BF16], tb=True, name="in_proj_g")
    wb = {**wb, **late_weights(1, za)}
    u = za[:, :SSM_W]
    mq = za[:, ZA_W - MEM_W:]

    u_s = _scan_order(u)
    s_all = _ssm_scan(u_s, b_re_t, b_im_t, a_pair, reverse=False, name="ssm_scan_fwd")
    ys = _time_order(_mm((s_all, 0), c_re_t, [F32], tb=True, pair2=((s_all, 1), c_im_t), bd=SSM_BD, name="ssm_cs"))

    def gelu_fn(r, b):
        y0 = r[0] + b[0] * r[1].astype(F32)
        th, _, _ = _gelu_parts(y0)
        return [y0, 0.5 * y0 * (1.0 + th)], []
    y0, y1 = _ew(gelu_fn, [ys, u], [d_row], [(SSM_W, F32), (SSM_W, BF16)], [], name="ssm_gelu", tm=512)

    def glu_epi(acc, y1t, bg):
        t = acc + bg
        return t, y1t.astype(F32) * _sigmoid(t)
    t_glu, y2 = _mm(y1, wb["w_glu"], [F32, BF16], epi=glu_epi, mn=[y1], rows=[sp["b_glu"]], name="ssm_glu")
    br_ssm = _mm(y2, wb["w_ssm_br"], [BF16], tb=True, name="ssm_br")

    qkv_p, o_g, lse_g = [], [], []
    for g, d in enumerate(DILATIONS):
        nb = l // d // ATT_WIN
        cols = [za[:, SSM_W + (3 * j + g) * ATT_GW: SSM_W + (3 * j + g + 1) * ATT_GW] for j in range(3)]
        qp, kp, vp = [_to_perm(cc, d) for cc in cols]
        qkv_p.append((qp, kp, vp))
        og, lg = _attn_fwd(qp, kp, vp, nb, "attn_fwd%d" % g)
        o_g.append(_from_perm(og, d))
        lse_g.append(_from_perm(lg, d))

    def merge_fn(r, b):
        o0, o1, o2, l0, l1, l2 = r
        mx = jnp.maximum(jnp.maximum(l0, l1), l2)
        e0, e1, e2 = jnp.exp(l0 - mx), jnp.exp(l1 - mx), jnp.exp(l2 - mx)
        tot = e0 + e1 + e2
        return [(e0 * o0 + e1 * o1 + e2 * o2) / tot, mx + jnp.log(tot)], []
    o_att, lse_tot = _ew(merge_fn, o_g + lse_g, [], [(ATT_GW, F32), (ATT_GW, F32)], [], name="attn_merge", tm=512)
    br_attn = _mm(o_att, wb["w_attn_br"], [BF16], tb=True, name="attn_br")

    mn = _rms_fwd(mem, sp["mem_norm_g"], "rms_mem")
    kv = _mm(mn, wb["w_mem_kv"], [BF16], name="mem_kv")
    mo = _mem_fwd(mq, kv, "mem_attn_fwd")
    br_mem = _mm(mo, wb["w_mem_br"], [BF16], tb=True, name="mem_br")

    def gate_fn(r, b):
        zgt, b0, b1, b2 = [t.astype(F32) for t in r]
        gt = _sigmoid(zgt + b[0])
        return [gt[:, :D_MODEL] * b0 + gt[:, D_MODEL:2 * D_MODEL] * b1 + gt[:, 2 * D_MODEL:] * b2], []
    merged = _ew(gate_fn, [zg, br_ssm, br_attn, br_mem], [sp["b_gate"]], [(D_MODEL, BF16)], [], name="gate_merge")[0]
    def o_epi(acc, xt, g2):
        hv = acc + xt
        rs = lax.rsqrt(jnp.mean(hv * hv, axis=-1, keepdims=True) + RMS_EPS)
        return hv, hv * rs * g2
    h1, n2 = _mm(merged, wb["w_o"], [F32, BF16], epi=o_epi, mn=[x], rows=[sp["norm2_g"]], tm=512, name="o_proj")

    def up_epi(acc):
        ra = jnp.maximum(acc, 0.0)
        return ra * ra, ra
    wm = late_weights(MLP_GROUP, n2)
    f_act, r_act = _mm(n2, wm["w_up"], [BF16, BF16], tb=True, epi=up_epi, name="mlp_up")
    def down_epi(acc, ht, tv, gf):
        hv = acc + ht
        rs = lax.rsqrt(jnp.mean(hv * hv, axis=-1, keepdims=True) + RMS_EPS)
        err = hv * rs * gf - tv
        dh, dgf = _rms_bwd_tile(hv, err * (1.0 / D_MODEL), gf)
        return dh, dgf, _colsum(err * err) * (0.5 / D_MODEL)
    dh2, d_final_g, loss_cols = _mm(f_act, wm["w_down"], [F32], epi=down_epi, mn=[h1, tgt], rows=[sp["final_g"]],
                                    n_sums=2, tk=1024, name="mlp_down")
    loss = jnp.sum(loss_cols, axis=1, keepdims=True)

    gw, gs = {}, {"final_g": d_final_g}
    d_act = _mm(dh2, wm["w_down"], [BF16], tb=True, epi=lambda acc, ra: (acc * 2.0 * ra.astype(F32),), mn=[r_act],
                name="mlp_down_dx")
    dw_down = _mm(f_act, dh2, [F32], ta=True, name="mlp_down_dw")
    dw_up = _mm(d_act, n2, [F32], ta=True, name="mlp_up_dw")
    token = grads_ready(MLP_GROUP, {"w_up": dw_up, "w_down": dw_down})
    def up_dx_epi(acc, ht, dht, g2):
        dx, dg = _rms_bwd_tile(ht, acc, g2)
        return dx + dht, dg
    dh1, gs["norm2_g"] = _mm(d_act, wm["w_up"], [F32], epi=up_dx_epi, mn=[h1, dh2],
                             rows=[sp["norm2_g"] + token[:1, :1]], n_sums=1, tk=1024, name="mlp_up_dx")
    gw["w_o"] = _mm(merged, dh1, [F32], ta=True, name="o_proj_dw")

    def gate_bwd_epi(dm, *tiles):
        dbr, dz = [], []
        for zt, bt, bias in zip(tiles[0:3], tiles[3:6], tiles[6:9]):
            gt = _sigmoid(zt.astype(F32) + bias)
            dbr.append(dm * gt)
            dz.append(dm * bt.astype(F32) * gt * (1.0 - gt))
        return (*dbr, *dz, *[_colsum(t) for t in dz])
    gate_bias = [sp["b_gate"][:, i * D_MODEL:(i + 1) * D_MODEL] for i in range(3)]
    res = _mm(dh1, wb["w_o"], [BF16] * 6, tb=True, epi=gate_bwd_epi, mn=[(zg, 0), (zg, 1), (zg, 2), br_ssm, br_attn, br_mem],
              rows=gate_bias, n_sums=3, tm=512, name="o_proj_dx")
    (dbr_ssm, dbr_attn, dbr_mem), dzg = res[0:3], res[3:6]
    gs["b_gate"] = jnp.concatenate(res[6:9], axis=1)

    gw["w_ssm_br"] = _mm(dbr_ssm, y2, [F32], ta=True, name="ssm_br_dw")
    dy2 = _mm(dbr_ssm, wb["w_ssm_br"], [F32], name="ssm_br_dx")

    def glu_bwd_fn(r, b):
        dy, y1t, tt = r
        sg = _sigmoid(tt)
        dt = dy * y1t.astype(F32) * sg * (1.0 - sg)
        return [dt, dy * sg], [_colsum(dt)]
    dt_glu, dy1a, gs["b_glu"] = _ew(glu_bwd_fn, [dy2, y1, t_glu], [], [(SSM_W, BF16), (SSM_W, F32)], [SSM_W],
                                    name="ssm_glu_bwd", tm=512)
    gw["w_glu"] = _mm(y1, dt_glu, [F32], ta=True, name="ssm_glu_dw")

    def gelu_bwd_epi(acc, dy1t, y0t):
        th, c0, c1 = _gelu_parts(y0t)
        dg = 0.5 * (1.0 + th) + 0.5 * y0t * (1.0 - th * th) * c0 * (1.0 + 3.0 * c1 * y0t * y0t)
        return ((acc + dy1t) * dg,)
    dy0 = _mm(dt_glu, wb["w_glu"], [F32], tb=True, epi=gelu_bwd_epi, mn=[dy1a, y0], name="ssm_glu_dx")
    gs["ssm_d"] = _ew(lambda r, b: ([], [_colsum(r[0] * r[1].astype(F32))]), [dy0, u], [], [], [SSM_W],
                      name="ssm_dd", tm=512)[0]
    dy0_s = _scan_order(dy0)
    lam, da, d_b, d_c = _ssm_scan(dy0_s, c_re_t, c_im_t, a_conj, reverse=True, s_fwd=s_all, u=u_s,
                                  name="ssm_scan_bwd")
    du = _time_order(_mm((lam, 0), b_re_t, [BF16], tb=True, pair2=((lam, 1), b_im_t),
                         epi=lambda acc, dyt, dr: (acc + dyt * dr,), mn=[dy0_s], rows=[d_row], bd=SSM_BD, name="ssm_bu_dx"))
    gs["a_re"], gs["a_im"] = da[0], da[1]
    gs["bb_re"], gs["bb_im"] = _bd_diag(d_b[0]).transpose(0, 2, 1), _bd_diag(d_b[1]).transpose(0, 2, 1)
    gs["ssm_c_re"], gs["ssm_c_im"] = _bd_diag(d_c[0]), -_bd_diag(d_c[1])

    gw["w_attn_br"] = _mm(dbr_attn, o_att, [F32], ta=True, name="attn_br_dw")

    def do_epi(acc, ot):
        prod = acc * ot
        head = lax.broadcasted_iota(jnp.int32, prod.shape, 1) // ATT_E
        dd = jnp.zeros_like(prod)
        for h in range(ATT_HG):
            dd = jnp.where(head == h, jnp.sum(jnp.where(head == h, prod, 0.0), axis=1, keepdims=True), dd)
        return acc, dd
    do_att, dd_att = _mm(dbr_attn, wb["w_attn_br"], [BF16, F32], epi=do_epi, mn=[o_att], name="attn_br_dx")
    dq_l, dk_l, dv_l = [], [], []
    for g, d in enumerate(DILATIONS):
        nb = l // d // ATT_WIN
        qp, kp, vp = qkv_p[g]
        dq, dk, dv = _attn_bwd(qp, kp, vp, _to_perm(do_att, d), _to_perm(lse_tot, d), _to_perm(dd_att, d),
                               nb, "attn_bwd%d" % g)
        dq_l.append(_from_perm(dq, d))
        dk_l.append(_from_perm(dk, d))
        dv_l.append(_from_perm(dv, d))

    gw["w_mem_br"] = _mm(dbr_mem, mo, [F32], ta=True, name="mem_br_dw")
    dmo = _mm(dbr_mem, wb["w_mem_br"], [BF16], name="mem_br_dx")
    dmq, dkv = _mem_bwd(mq, kv, dmo, "mem_attn_bwd")
    gw["w_mem_kv"] = _mm(mn, dkv, [F32], ta=True, name="mem_kv_dw")
    dmn = _mm(dkv, wb["w_mem_kv"], [F32], tb=True, name="mem_kv_dx")
    token = grads_ready(1, gw)
    gs["mem_norm_g"] = _rms_bwd(mem, dmn, None, sp["mem_norm_g"] + token[:1, :1], "rms_mem_bwd")[1]

    dza = jnp.concatenate([du] + dq_l + dk_l + dv_l + [dmq], axis=1)
    dn_a = _mm(dza, w_a, [F32], name="in_proj_a_dx", tk=1664)
    dw_a = _mm(dza, n1, [F32], ta=True, name="in_proj_a_dw", tm=1664)
    dw_g = [_mm(dzg[i], n1, [F32], ta=True, name="in_proj_g_dw%d" % i) for i in range(3)]
    gw["w_in"] = jnp.concatenate([dw_a] + dw_g, axis=0)
    token = grads_ready(0, gw)
    def in_dx_epi(acc, pt, xt, dht, g1):
        dx, dg = _rms_bwd_tile(xt, acc + pt, g1)
        return dx + dht, dg
    w_gs = [w_g[i * D_MODEL:(i + 1) * D_MODEL] for i in range(3)]
    grad_x, gs["norm1_g"] = _mm(dzg[0], w_gs[0], [F32], pair2=(dzg[1], w_gs[1], dzg[2], w_gs[2]), epi=in_dx_epi,
                                mn=[dn_a, x, dh1],
                                rows=[sp["norm1_g"] + token[:1, :1]], n_sums=1, tm=512, name="in_proj_g_dx")
    return loss, grad_x, gs


_SMALL_GRAD_ORDER = ("norm1_g", "mem_norm_g", "b_gate", "a_re", "a_im", "bb_re", "bb_im", "ssm_c_re", "ssm_c_im",
                     "ssm_d", "b_glu", "norm2_g", "final_g")


def kernel(x, mem, norm1_g, mem_norm_g, w_in, b_gate, ssm_lambda_re, ssm_lambda_im, ssm_log_dt, ssm_b_re, ssm_b_im, ssm_c_re, ssm_c_im, ssm_d, w_glu, b_glu, w_ssm_br, w_attn_br, w_mem_kv, w_mem_br, w_o, norm2_g, w_up, w_down, final_g, loss_target, m_norm1_g, m_mem_norm_g, m_w_in, m_b_gate, m_ssm_lambda_re, m_ssm_lambda_im, m_ssm_log_dt, m_ssm_b_re, m_ssm_b_im, m_ssm_c_re, m_ssm_c_im, m_ssm_d, m_w_glu, m_b_glu, m_w_ssm_br, m_w_attn_br, m_w_mem_kv, m_w_mem_br, m_w_o, m_norm2_g, m_w_up, m_w_down, m_final_g, v_norm1_g, v_mem_norm_g, v_w_in, v_b_gate, v_ssm_lambda_re, v_ssm_lambda_im, v_ssm_log_dt, v_ssm_b_re, v_ssm_b_im, v_ssm_c_re, v_ssm_c_im, v_ssm_d, v_w_glu, v_b_glu, v_w_ssm_br, v_w_attn_br, v_w_mem_kv, v_w_mem_br, v_w_o, v_norm2_g, v_w_up, v_w_down, v_final_g):
    args = dict(locals())
    w = {n: args[n] for n in ALL_W}
    m = {n: args["m_" + n] for n in ALL_W}
    v = {n: args["v_" + n] for n in ALL_W}
    my_c = lax.axis_index("c").astype(jnp.int32).reshape(1)
    my_chip = (2 * lax.axis_index("x") + lax.axis_index("y")).astype(jnp.int32).reshape(1)

    w_pack = [_pack_group(w, names) for names in GROUPS]
    my_index = (4 * lax.axis_index("x") + 2 * lax.axis_index("y") + lax.axis_index("c")).astype(jnp.int32)
    zero = jnp.zeros((), jnp.int32)
    w_all = _allgather(w_pack[0].astype(BF16), "allgather_weights0")
    wb = {n: _full_stored(part, n) for n, part in _split_group(w_all, GROUPS[0]).items()}
    gathers = {gi: _split_start(w_pack[gi].astype(BF16), N_DEV, _gather_plan, w_all, "weights_gather_start%d" % gi)
               for gi in range(1, len(GROUPS))}

    def gathered(started, after, name):
        sems, src, land, _ = started
        src, land = _split_wait(sems, src, land, _gather_wait_plan, after, name)
        return lax.dynamic_update_slice(land, src[None], (my_index, zero, zero))

    def late_weights(gi, after):
        full = gathered(gathers[gi], after, "weights_gather_wait%d" % gi)
        return {n: _full_stored(part, n) for n, part in _split_group(full, GROUPS[gi]).items()}

    pending = {}

    def grads_ready(gi, grads):
        g_pack = jnp.concatenate([_stacked_stored(grads[n], n) for n in GROUPS[gi]], axis=1)
        t1 = _pair_exchange(g_pack, "grad_pair_exchange%d" % gi)
        p_sum, p_bf = _pair_sum(g_pack, t1, my_c, "grad_pair_sum%d" % gi, GROUP_TR[gi])
        sems, src, land, token = _split_start(p_bf, 3, _chip_plan, p_sum, "grad_chip_exchange_start%d" % gi)
        pending[gi] = (p_sum, sems, src, land)
        return token

    sp = {
        "norm1_g": norm1_g + sum(started[3][:1, :1] for started in gathers.values()), "mem_norm_g": mem_norm_g, "b_gate": b_gate, "b_glu": b_glu, "norm2_g": norm2_g,
        "final_g": final_g.reshape(1, D_MODEL),
        "ssm_lambda_re": ssm_lambda_re[0], "ssm_lambda_im": ssm_lambda_im[0], "ssm_log_dt": ssm_log_dt[0],
        "ssm_b_re": ssm_b_re[0], "ssm_b_im": ssm_b_im[0], "ssm_c_re": ssm_c_re[0], "ssm_c_im": ssm_c_im[0],
        "ssm_d": ssm_d[0],
    }
    loss, grad_x, gs = _local_step(x[0], mem[0], loss_target[0], wb, sp, late_weights, grads_ready)
    loss = lax.psum(loss[0, 0], ("x", "y", "c"))
    sg_shapes = [gs[n].shape for n in _SMALL_GRAD_ORDER]
    sg_started = _split_start(_pack([gs[n] for n in _SMALL_GRAD_ORDER]), N_DEV, _gather_plan, grad_x,
                              "small_grads_gather_start")

    big_g = {}
    for gi, names in enumerate(GROUPS):
        p_sum, sems, src, land = pending[gi]
        t2 = _split_wait(sems, src, land, _chip_plan, grad_x, "grad_chip_exchange_wait%d" % gi)[1]
        g_pack = _grad_sum(p_sum, t2, my_chip, "grad_sum%d" % gi, GROUP_TR[gi])
        for n, part in _split_group(g_pack, names).items():
            big_g[n] = _unstored(part, n)
    rows_of = lambda d, names: [d[n].reshape(d[n].shape[-2:]) for n in names]
    big_out = _adam_many(rows_of(big_g, BIG), rows_of(w, BIG), rows_of(m, BIG), rows_of(v, BIG), 8, "adam_big")
    big = [big_g] + [{n: a[None] for n, a in zip(BIG, outs)} for outs in big_out]

    sg_all = gathered(sg_started, big_out[0][0], "small_grads_gather_wait")
    sg = dict(zip(_SMALL_GRAD_ORDER, _unpack(_sum8(sg_all, "sum_small_grads"), sg_shapes)))
    _, disc_vjp = jax.vjp(_discretize, sp["ssm_lambda_re"], sp["ssm_lambda_im"], sp["ssm_log_dt"],
                          sp["ssm_b_re"], sp["ssm_b_im"])
    d_lre, d_lim, d_ldt, d_bre, d_bim = disc_vjp((sg["a_re"].reshape(SSM_G, SSM_P), sg["a_im"].reshape(SSM_G, SSM_P),
                                                  sg["bb_re"], sg["bb_im"]))
    small_grad = {
        "norm1_g": sg["norm1_g"], "mem_norm_g": sg["mem_norm_g"], "b_gate": sg["b_gate"],
        "ssm_lambda_re": d_lre, "ssm_lambda_im": d_lim, "ssm_log_dt": d_ldt, "ssm_b_re": d_bre, "ssm_b_im": d_bim,
        "ssm_c_re": sg["ssm_c_re"], "ssm_c_im": sg["ssm_c_im"], "ssm_d": sg["ssm_d"], "b_glu": sg["b_glu"],
        "norm2_g": sg["norm2_g"], "final_g": sg["final_g"],
    }
    small_grad = {n: small_grad[n].reshape(w[n].shape) for n in SMALL}

    def squeezed(a):
        return a.reshape(a.shape[1:]) if a.ndim > 2 else a.reshape(1, -1)

    sq = lambda d: [squeezed(d[n]) for n in SMALL]
    small_out = _adam_many(sq(small_grad), sq(w), sq(m), sq(v), 1, "adam_small")
    small = [small_grad] + [{n: a.reshape(w[n].shape) for n, a in zip(SMALL, outs)} for outs in small_out]

    outs = [loss, grad_x[None]]
    for kind in range(4):
        for n in ALL_W:
            outs.append(big[kind][n] if n in BIG else small[kind][n])
    return tuple(outs)
```

```python
import math

import numpy as np
import jax
import jax.numpy as jnp
from jax import lax
from jax.experimental import pallas as pl
from jax.experimental.pallas import tpu as pltpu

F32 = jnp.float32
BF16 = jnp.bfloat16
_MXU = jnp.bfloat16

D_MODEL = 1024
SSM_G, SSM_H, SSM_P = 32, 16, 64
SSM_W = SSM_G * SSM_H
SSM_S = SSM_G * SSM_P
SSM_BD = 4
ATT_E = 64
ATT_HG = 4
ATT_GW = ATT_HG * ATT_E
ATT_WIN = 128
ATT_QB = 8
ATT_QB_FWD = 4
DILATIONS = (1, 4, 16)
MEM_H, MEM_E = 4, 128
MEM_W = MEM_H * MEM_E
ZA_W = SSM_W + 9 * ATT_GW + MEM_W
ZG_W = 3 * D_MODEL
IN_W = ZA_W + ZG_W
RMS_EPS = 1e-6
NEG_INF = -1e30

ADAM_LR, ADAM_B1, ADAM_B2, ADAM_EPS, ADAM_WD, ADAM_STEP = 0.001, 0.9, 0.999, 1e-08, 0.01, 10

N_DEV = 8
PACK_C = 512
_VMEM_LIMIT = 56 * 1024 * 1024
SUBLANES = 16
SCAN_SEG = 128
SCAN_CHAINS = 4
SCAN_UNROLL = 4
SCAN_W = 128

BIG = ("w_in", "w_glu", "w_ssm_br", "w_attn_br", "w_mem_kv", "w_mem_br", "w_o", "w_up", "w_down")
BIG_SHAPE = {
    "w_in": (D_MODEL, IN_W, 1), "w_glu": (SSM_W, SSM_W, 0), "w_ssm_br": (SSM_W, D_MODEL, 1),
    "w_attn_br": (ATT_GW, D_MODEL, 1), "w_mem_kv": (D_MODEL, 2 * MEM_W, 0), "w_mem_br": (MEM_W, D_MODEL, 1),
    "w_o": (D_MODEL, D_MODEL, 0), "w_up": (D_MODEL, 4 * D_MODEL, 1), "w_down": (4 * D_MODEL, D_MODEL, 0),
}
SMALL = ("norm1_g", "mem_norm_g", "b_gate", "ssm_lambda_re", "ssm_lambda_im", "ssm_log_dt", "ssm_b_re",
         "ssm_b_im", "ssm_c_re", "ssm_c_im", "ssm_d", "b_glu", "norm2_g", "final_g")
ALL_W = ("norm1_g", "mem_norm_g", "w_in", "b_gate", "ssm_lambda_re", "ssm_lambda_im", "ssm_log_dt", "ssm_b_re",
         "ssm_b_im", "ssm_c_re", "ssm_c_im", "ssm_d", "w_glu", "b_glu", "w_ssm_br", "w_attn_br", "w_mem_kv",
         "w_mem_br", "w_o", "norm2_g", "w_up", "w_down", "final_g")


def _params(sem):
    return pltpu.CompilerParams(dimension_semantics=sem, vmem_limit_bytes=_VMEM_LIMIT)


def _pick(n, cap):
    if n <= cap:
        return n
    t = (cap // 128) * 128
    while n % t:
        t -= 128
    return t


def _mm(a, b, outs, *, name, ta=False, tb=False, epi=None, mn=(), rows=(), pair2=None, bd=0, n_sums=0,
        tm=1024, tn=1024, tk=2048):
    ab = [a, b] + (list(pair2) if pair2 is not None else [])
    planes = [op[1] if isinstance(op, tuple) else None for op in ab]
    ab = [op[0] if isinstance(op, tuple) else op for op in ab]
    a_shape, b_shape = ab[0].shape[-2:], ab[1].shape[-2:]
    m = a_shape[1] if ta else a_shape[0]
    k = a_shape[0] if ta else a_shape[1]
    n = b_shape[0] if tb else b_shape[1]
    assert k == (b_shape[1] if tb else b_shape[0]), (name, a_shape, b_shape)
    out_n = n
    if bd and ta:
        assert not tb
        tm, tn, tk = m // bd, n // bd, _pick(k, tk)
        grid, out_n = (bd, 1, k // tk), tn
        a_blk = ((tk, tm), lambda i, j, kk: (kk, i))
        b_blk = ((tk, tn), lambda i, j, kk: (kk, i))
        mn_spec = pl.BlockSpec((tm, tn), lambda i, j, kk: (i, 0))
    elif bd:
        tm, tn, tk = _pick(m, tm), n // bd, k // bd
        grid = (m // tm, bd, 1)
        a_blk = ((tm, tk), lambda i, j, kk: (i, j))
        b_blk = ((tn, tk) if tb else (tk, tn), lambda i, j, kk: (j, j))
        mn_spec = pl.BlockSpec((tm, tn), lambda i, j, kk: (i, j))
    else:
        tm, tn, tk = _pick(m, tm), _pick(n, tn), _pick(k, tk)
        grid = (m // tm, n // tn, k // tk)
        a_blk = ((tk, tm), lambda i, j, kk: (kk, i)) if ta else ((tm, tk), lambda i, j, kk: (i, kk))
        b_blk = ((tn, tk), lambda i, j, kk: (j, kk)) if tb else ((tk, tn), lambda i, j, kk: (kk, j))
        mn_spec = pl.BlockSpec((tm, tn), lambda i, j, kk: (i, j))

    def operand_spec(blk, plane):
        shape, imap = blk
        if plane is None:
            return pl.BlockSpec(shape, imap)
        return pl.BlockSpec((None,) + shape, lambda i, j, kk: (plane,) + imap(i, j, kk))

    ab_specs = [operand_spec(a_blk if q % 2 == 0 else b_blk, p) for q, p in enumerate(planes)]
    mn_arrays = [e[0] if isinstance(e, tuple) else e for e in mn]
    mn_specs = [pl.BlockSpec((tm, tn), lambda i, j, kk, c=e[1]: (i, c)) if isinstance(e, tuple) else mn_spec
                for e in mn]
    nk = grid[2]
    row_spec = pl.BlockSpec((1, tn), lambda i, j, kk: (0, j))
    n_ex, n_out = len(mn) + len(rows), len(outs)
    assert n_sums == 0 or (grid[1] == 1 and not bd)
    dims = (((0 if ta else 1,), (1 if tb else 0,)), ((), ()))

    def body(*refs):
        ab_refs, rest = refs[:len(ab)], refs[len(ab):]
        ex, o_refs, acc = rest[:n_ex], rest[n_ex:n_ex + n_out], rest[-1]
        s_refs = rest[n_ex + n_out:n_ex + n_out + n_sums]
        first_row_tile = pl.program_id(0) == 0
        kk = pl.program_id(2)

        @pl.when(kk == 0)
        def _():
            acc[...] = jnp.zeros_like(acc)

        for a_ref, b_ref in zip(ab_refs[0::2], ab_refs[1::2]):
            acc[...] += lax.dot_general(a_ref[...].astype(_MXU), b_ref[...].astype(_MXU), dims,
                                        preferred_element_type=F32)

        @pl.when(kk == nk - 1)
        def _():
            vals = (acc[...],) if epi is None else epi(acc[...], *[r[...] for r in ex])
            for r, v in zip(o_refs, vals):
                r[...] = v.astype(r.dtype)
            for r, v in zip(s_refs, vals[n_out:]):
                r[...] = jnp.where(first_row_tile, v, r[...] + v)

    res = pl.pallas_call(
        body, grid=grid,
        in_specs=ab_specs + mn_specs + [row_spec] * len(rows),
        out_specs=[mn_spec] * n_out + [row_spec] * n_sums,
        out_shape=[jax.ShapeDtypeStruct((m, out_n), dt) for dt in outs]
        + [jax.ShapeDtypeStruct((1, out_n), F32)] * n_sums,
        scratch_shapes=[pltpu.VMEM((tm, tn), F32)],
        compiler_params=_params(("arbitrary" if n_sums else "parallel", "parallel", "arbitrary")), name=name,
    )(*ab, *mn_arrays, *rows)
    return res[0] if n_out + n_sums == 1 else res


def _ew(fn, rows, bcs, out_rows, out_accs, *, name, tm=256):
    r = rows[0].shape[0]
    tm = min(tm, r)
    assert r % tm == 0
    nr, nb, no, na = len(rows), len(bcs), len(out_rows), len(out_accs)

    def body(*refs):
        i = pl.program_id(0)
        r_in, b_in = refs[:nr], refs[nr:nr + nb]
        o_r, o_a = refs[nr + nb:nr + nb + no], refs[nr + nb + no:]
        outs, accs = fn([x[...] for x in r_in], [x[...] for x in b_in])
        for ref, v in zip(o_r, outs):
            ref[...] = v.astype(ref.dtype)
        if na:
            @pl.when(i == 0)
            def _():
                for ref in o_a:
                    ref[...] = jnp.zeros_like(ref)

            for ref, v in zip(o_a, accs):
                ref[...] += v

    res = pl.pallas_call(
        body, grid=(r // tm,),
        in_specs=[pl.BlockSpec((tm, x.shape[1]), lambda i: (i, 0)) for x in rows]
        + [pl.BlockSpec((1, x.shape[1]), lambda i: (0, 0)) for x in bcs],
        out_specs=[pl.BlockSpec((tm, c), lambda i: (i, 0)) for c, _ in out_rows]
        + [pl.BlockSpec((1, c), lambda i: (0, 0)) for c in out_accs],
        out_shape=[jax.ShapeDtypeStruct((r, c), dt) for c, dt in out_rows]
        + [jax.ShapeDtypeStruct((1, c), F32) for c in out_accs],
        compiler_params=_params(("arbitrary",)), name=name,
    )(*rows, *bcs)
    return res


def _colsum(x):
    return jnp.sum(x, axis=0, keepdims=True)


def _sigmoid(x):
    return 1.0 / (1.0 + jnp.exp(-x))


def _rms_bwd_tile(xv, dv, g):
    rs = lax.rsqrt(jnp.mean(xv * xv, axis=-1, keepdims=True) + RMS_EPS)
    gd = dv * g
    dx = rs * gd - xv * (rs * rs * rs) * jnp.mean(gd * xv, axis=-1, keepdims=True)
    return dx, _colsum(dv * xv * rs)


def _rms_fwd(x, g, name):
    def fn(r, b):
        xv = r[0]
        rs = lax.rsqrt(jnp.mean(xv * xv, axis=-1, keepdims=True) + RMS_EPS)
        return [xv * rs * b[0]], []
    return _ew(fn, [x], [g], [(x.shape[1], BF16)], [], name=name)[0]


def _rms_bwd(x, dn, res, g, name):
    def fn(r, b):
        dx, dg = _rms_bwd_tile(r[0], r[1], b[0])
        if res is not None:
            dx = dx + r[2]
        return [dx], [dg]
    rows = [x, dn] + ([res] if res is not None else [])
    return _ew(fn, rows, [g], [(x.shape[1], F32)], [x.shape[1]], name=name)


def _scan_order(x):
    l, c = x.shape
    return x.reshape(l // (SUBLANES * SCAN_SEG), SUBLANES, SCAN_SEG, c).transpose(0, 2, 1, 3).reshape(l, c)


def _time_order(x):
    l, c = x.shape
    return x.reshape(l // (SUBLANES * SCAN_SEG), SCAN_SEG, SUBLANES, c).transpose(0, 2, 1, 3).reshape(l, c)


def _ssm_scan(x, w_re, w_im, a_pair, *, reverse, s_fwd=None, u=None, name):
    l = x.shape[0]
    seg, w = SCAN_SEG, SCAN_W
    bd_w = SSM_W // SSM_BD
    tiles_per_bd = SSM_S // SSM_BD // w
    nch = min(SCAN_CHAINS, l // (SUBLANES * seg))
    chain_rows = SUBLANES * seg
    tb = nch * chain_rows
    nt = l // tb
    with_da = s_fwd is not None
    assert reverse or not with_da

    def tt(t):
        return nt - 1 - t if reverse else t

    def body(*refs):
        if with_da:
            (x_ref, wr_ref, wi_ref, a_ref, sf_ref, sp_ref, u_ref, s_ref, da_ref, dw_ref, dx_ref,
             p_ref, c_ref, b_scr) = refs
        else:
            x_ref, wr_ref, wi_ref, a_ref, s_ref, p_ref, c_ref, b_scr = refs
        t_blk = pl.program_id(1)
        ar, ai = a_ref[0], a_ref[1]

        @pl.when(t_blk == 0)
        def _():
            def pstep(i, carry):
                pr, pi = carry
                p_ref[0, pl.ds(i, 1), :] = pr
                p_ref[1, pl.ds(i, 1), :] = pi
                return pr * ar - pi * ai, pr * ai + pi * ar

            lax.fori_loop(0, seg, pstep, (ar, ai))
            c_ref[...] = jnp.zeros_like(c_ref)
            if with_da:
                da_ref[...] = jnp.zeros_like(da_ref)
                dw_ref[...] = jnp.zeros_like(dw_ref)
                dx_ref[...] = jnp.zeros_like(dx_ref)

        xb = x_ref[...].astype(_MXU)
        b_scr[:, :w] = jnp.dot(xb, wr_ref[...], preferred_element_type=F32)
        b_scr[:, w:] = jnp.dot(xb, wi_ref[...], preferred_element_type=F32)
        arb, aib = jnp.broadcast_to(ar, (SUBLANES, w)), jnp.broadcast_to(ai, (SUBLANES, w))
        zero = jnp.zeros((SUBLANES, w), F32)

        def tile(g, step):
            return pl.ds(pl.multiple_of(g * chain_rows + step * SUBLANES, SUBLANES), SUBLANES)

        def rows(g, i):
            return tile(g, seg - 1 - i if reverse else i)

        def local_step(i, carry):
            out = []
            for g in range(nch):
                sr, si = carry[2 * g], carry[2 * g + 1]
                idx = rows(g, i)
                sr, si = arb * sr - aib * si + b_scr[idx, :w], arb * si + aib * sr + b_scr[idx, w:]
                b_scr[idx, :w] = sr
                b_scr[idx, w:] = si
                out += [sr, si]
            return tuple(out)

        def unrolled(step_fn, first):
            def trip(q, carry):
                for r in range(SCAN_UNROLL):
                    carry = step_fn(first + q * SCAN_UNROLL + r, carry)
                return carry
            return trip

        ends = lax.fori_loop(0, seg // SCAN_UNROLL, unrolled(local_step, 0), (zero,) * (2 * nch))

        a_seg_r, a_seg_i = p_ref[0, seg - 1:seg, :], p_ref[1, seg - 1:seg, :]
        cr, ci = c_ref[0], c_ref[1]
        sub = lax.broadcasted_iota(jnp.int32, (SUBLANES, w), 0)
        ins = [[zero, zero] for _ in range(nch)]
        order = [(g, k) for g in range(nch) for k in range(SUBLANES)]
        for g, k in (order[::-1] if reverse else order):
            ins[g] = [jnp.where(sub == k, cr, ins[g][0]), jnp.where(sub == k, ci, ins[g][1])]
            er, ei = ends[2 * g][k:k + 1], ends[2 * g + 1][k:k + 1]
            cr, ci = er + a_seg_r * cr - a_seg_i * ci, ei + a_seg_r * ci + a_seg_i * cr
        c_ref[0] = cr
        c_ref[1] = ci

        def fix(g, i):
            idx = rows(g, i)
            pr, pi = p_ref[0, pl.ds(i, 1), :], p_ref[1, pl.ds(i, 1), :]
            sr = b_scr[idx, :w] + pr * ins[g][0] - pi * ins[g][1]
            si = b_scr[idx, w:] + pr * ins[g][1] + pi * ins[g][0]
            s_ref.at[0][idx, :] = sr.astype(s_ref.dtype)
            s_ref.at[1][idx, :] = si.astype(s_ref.dtype)
            return sr, si

        if not with_da:
            def fix_step(i, carry):
                for g in range(nch):
                    fix(g, i)
                return carry

            lax.fori_loop(0, seg // SCAN_UNROLL, unrolled(fix_step, 0), 0)
        else:
            def adj_step(i, acc):
                acc_r, acc_i = acc
                for g in range(nch):
                    lr, li = fix(g, i)
                    prev = tile(g, seg - 2 - i)
                    fr, fi = sf_ref.at[0][prev, :].astype(F32), sf_ref.at[1][prev, :].astype(F32)
                    acc_r, acc_i = acc_r + lr * fr + li * fi, acc_i + li * fr - lr * fi
                return acc_r, acc_i

            acc = lax.fori_loop(0, seg // SCAN_UNROLL - 1, unrolled(adj_step, 0), (zero, zero))
            for i in range(seg - SCAN_UNROLL, seg - 1):
                acc = adj_step(i, acc)
            acc_r, acc_i = acc
            first_block = tt(t_blk) == 0
            for g in range(nch):
                lr, li = fix(g, seg - 1)
                seg_ends = tile(g, seg - 1)
                if g == 0:
                    pvr = jnp.where(first_block, 0.0, sp_ref[0, SUBLANES - 1:SUBLANES, :].astype(F32))
                    pvi = jnp.where(first_block, 0.0, sp_ref[1, SUBLANES - 1:SUBLANES, :].astype(F32))
                else:
                    pvr = sf_ref[0, g * chain_rows - 1:g * chain_rows, :].astype(F32)
                    pvi = sf_ref[1, g * chain_rows - 1:g * chain_rows, :].astype(F32)
                fr = jnp.where(sub == 0, pvr, pltpu.roll(sf_ref.at[0][seg_ends, :].astype(F32), 1, 0))
                fi = jnp.where(sub == 0, pvi, pltpu.roll(sf_ref.at[1][seg_ends, :].astype(F32), 1, 0))
                acc_r = acc_r + lr * fr + li * fi
                acc_i = acc_i + li * fr - lr * fi
            da_ref[0] += jnp.sum(acc_r, axis=0, keepdims=True)
            da_ref[1] += jnp.sum(acc_i, axis=0, keepdims=True)
            for plane in range(2):
                dw_ref[plane] += _tn_dot(u_ref[...], s_ref[plane])
                dx_ref[plane] += _tn_dot(xb, sf_ref[plane])

    x_spec = pl.BlockSpec((tb, bd_w), lambda j, t: (tt(t), j // tiles_per_bd))
    w_spec = pl.BlockSpec((bd_w, w), lambda j, t: (j // tiles_per_bd, j))
    d_spec = pl.BlockSpec((2, bd_w, w), lambda j, t: (0, j // tiles_per_bd, j % tiles_per_bd))
    a_spec = pl.BlockSpec((2, 1, w), lambda j, t: (0, 0, j))
    s_spec = pl.BlockSpec((2, tb, w), lambda j, t: (0, tt(t), j))
    in_specs, args = [x_spec, w_spec, w_spec, a_spec], [x, w_re, w_im, a_pair]
    out_specs, out_shape = [s_spec], [jax.ShapeDtypeStruct((2, l, SSM_S), BF16)]
    scratch = [pltpu.VMEM((2, seg, w), F32), pltpu.VMEM((2, 1, w), F32), pltpu.VMEM((tb, 2 * w), F32)]
    if with_da:
        in_specs += [s_spec, pl.BlockSpec((2, SUBLANES, w),
                                          lambda j, t: (0, jnp.maximum(tt(t) * (tb // SUBLANES) - 1, 0), j)),
                     x_spec]
        args += [s_fwd, s_fwd, u]
        out_specs += [a_spec, d_spec, d_spec]
        out_shape += ([jax.ShapeDtypeStruct((2, 1, SSM_S), F32)]
                      + [jax.ShapeDtypeStruct((2, SSM_W, SSM_S // SSM_BD), F32)] * 2)
    res = pl.pallas_call(
        body, grid=(SSM_S // w, nt), in_specs=in_specs, out_specs=out_specs, out_shape=out_shape,
        scratch_shapes=scratch, compiler_params=_params(("parallel", "arbitrary")), name=name,
    )(*args)
    return res if with_da else res[0]


def _nt_dot(x, y):
    return lax.dot_general(x.astype(_MXU), y.astype(_MXU), (((1,), (1,)), ((), ())), preferred_element_type=F32)


def _tn_dot(x, y):
    return lax.dot_general(x.astype(_MXU), y.astype(_MXU), (((0,), (0,)), ((), ())), preferred_element_type=F32)


def _nn_dot(x, y):
    return jnp.dot(x.astype(_MXU), y.astype(_MXU), preferred_element_type=F32)


def _attn_mask2(gb, nb):
    qi = lax.broadcasted_iota(jnp.int32, (ATT_WIN, 2 * ATT_WIN), 0)
    c = lax.broadcasted_iota(jnp.int32, (ATT_WIN, 2 * ATT_WIN), 1)
    has_prev = (gb % nb) != 0
    prev_ok = jnp.logical_and(jnp.logical_and(c < ATT_WIN, c >= qi), has_prev)
    own_ok = jnp.logical_and(c >= ATT_WIN, c - ATT_WIN <= qi)
    return jnp.logical_or(prev_ok, own_ok)


def _attn_specs(qb):
    cur = pl.BlockSpec((qb * ATT_WIN, ATT_GW), lambda i: (i, 0))
    prev = pl.BlockSpec((ATT_WIN, ATT_GW), lambda i: (jnp.maximum(qb * i - 1, 0), 0))
    return cur, prev


def _attn_fwd(q, k, v, nb, name):
    l = q.shape[0]
    scale = ATT_E ** -0.5
    w = ATT_WIN

    qb = ATT_QB_FWD

    def body(q_ref, kc_ref, kp_ref, vc_ref, vp_ref, o_ref, lse_ref):
        i = pl.program_id(0)
        masks = [_attn_mask2(qb * i + b, nb) for b in range(qb)]
        for h in range(ATT_HG):
            sl = slice(h * ATT_E, (h + 1) * ATT_E)
            k_ext = jnp.concatenate([kp_ref[:, sl], kc_ref[:, sl]], axis=0)
            v_ext = jnp.concatenate([vp_ref[:, sl], vc_ref[:, sl]], axis=0)
            for b in range(qb):
                r, kr = slice(b * w, (b + 1) * w), slice(b * w, (b + 2) * w)
                s = jnp.where(masks[b], _nt_dot(q_ref[r, sl], k_ext[kr]) * scale, NEG_INF)
                mx = jnp.max(s, axis=-1, keepdims=True)
                p = jnp.exp(s - mx)
                den = jnp.sum(p, axis=-1, keepdims=True)
                o_ref[r, sl] = _nn_dot(p, v_ext[kr]) / den
                lse_ref[r, sl] = jnp.broadcast_to(mx + jnp.log(den), (w, ATT_E))

    cur, prev = _attn_specs(qb)
    return pl.pallas_call(
        body, grid=(l // (qb * w),), in_specs=[cur, cur, prev, cur, prev], out_specs=[cur, cur],
        out_shape=[jax.ShapeDtypeStruct((l, ATT_GW), F32)] * 2,
        compiler_params=_params(("parallel",)), name=name,
    )(q, k, k, v, v)


def _attn_bwd(q, k, v, do, lse, dd, nb, name):
    l = q.shape[0]
    scale = ATT_E ** -0.5
    w = ATT_WIN
    nblk = l // w

    def body(q_ref, kc_ref, kp_ref, vc_ref, vp_ref, do_ref, lse_ref, dd_ref, qn_ref, don_ref, lsen_ref, ddn_ref,
             dq_ref, dk_ref, dv_ref, dk_acc, dv_acc):
        i = pl.program_id(0)
        masks = [_attn_mask2(ATT_QB * i + b, nb) for b in range(ATT_QB)]
        nxt = ATT_QB * (i + 1)
        nxt_attends = jnp.logical_and(nxt < nblk, (nxt % nb) != 0)
        qi = lax.broadcasted_iota(jnp.int32, (w, w), 0)
        kj = lax.broadcasted_iota(jnp.int32, (w, w), 1)
        mask_n = jnp.logical_and(kj >= qi, nxt_attends)
        dk_acc[...] = jnp.zeros_like(dk_acc)
        dv_acc[...] = jnp.zeros_like(dv_acc)
        for h in range(ATT_HG):
            sl, col = slice(h * ATT_E, (h + 1) * ATT_E), slice(h * ATT_E, h * ATT_E + 1)
            k_ext = jnp.concatenate([kp_ref[:, sl], kc_ref[:, sl]], axis=0)
            v_ext = jnp.concatenate([vp_ref[:, sl], vc_ref[:, sl]], axis=0)
            for b in range(ATT_QB):
                r, kr = slice(b * w, (b + 1) * w), slice(b * w, (b + 2) * w)
                qh, doh, k2, v2 = q_ref[r, sl], do_ref[r, sl], k_ext[kr], v_ext[kr]
                p = jnp.where(masks[b], jnp.exp(_nt_dot(qh, k2) * scale - lse_ref[r, col]), 0.0)
                ds = p * (_nt_dot(doh, v2) - dd_ref[r, col]) * scale
                dq_ref[r, sl] = _nn_dot(ds, k2).astype(dq_ref.dtype)
                dk2, dv2 = _tn_dot(ds, qh), _tn_dot(p, doh)
                dk_acc[r, sl] += dk2[w:]
                dv_acc[r, sl] += dv2[w:]
                if b > 0:
                    rp = slice((b - 1) * w, b * w)
                    dk_acc[rp, sl] += dk2[:w]
                    dv_acc[rp, sl] += dv2[:w]
            last = slice((ATT_QB - 1) * w, ATT_QB * w)
            kl, vl, qn, don = kc_ref[last, sl], vc_ref[last, sl], qn_ref[:, sl], don_ref[:, sl]
            pn = jnp.where(mask_n, jnp.exp(_nt_dot(qn, kl) * scale - lsen_ref[:, col]), 0.0)
            dsn = pn * (_nt_dot(don, vl) - ddn_ref[:, col]) * scale
            dk_acc[last, sl] += _tn_dot(dsn, qn)
            dv_acc[last, sl] += _tn_dot(pn, don)
        dk_ref[...] = dk_acc[...].astype(dk_ref.dtype)
        dv_ref[...] = dv_acc[...].astype(dv_ref.dtype)

    cur, prev = _attn_specs(ATT_QB)
    nxt_spec = pl.BlockSpec((w, ATT_GW), lambda i: (jnp.minimum(ATT_QB * (i + 1), nblk - 1), 0))
    return pl.pallas_call(
        body, grid=(l // (ATT_QB * w),),
        in_specs=[cur, cur, prev, cur, prev, cur, cur, cur, nxt_spec, nxt_spec, nxt_spec, nxt_spec],
        out_specs=[cur] * 3, out_shape=[jax.ShapeDtypeStruct((l, ATT_GW), BF16)] * 3,
        scratch_shapes=[pltpu.VMEM((ATT_QB * w, ATT_GW), F32)] * 2,
        compiler_params=_params(("parallel",)), name=name,
    )(q, k, k, v, v, do, lse, dd, q, do, lse, dd)


def _to_perm(a, d):
    if d == 1:
        return a
    l, c = a.shape
    return a.reshape(l // d, d, c).transpose(1, 0, 2).reshape(l, c)


def _from_perm(a, d):
    if d == 1:
        return a
    l, c = a.shape
    return a.reshape(d, l // d, c).transpose(1, 0, 2).reshape(l, c)


def _mem_probs(qh, kh):
    s = _nt_dot(qh, kh) * (MEM_E ** -0.5)
    e = jnp.exp(s - jnp.max(s, axis=-1, keepdims=True))
    return e / jnp.sum(e, axis=-1, keepdims=True)


def _mem_fwd(mq, kv, name, tm=512):
    l, nm = mq.shape[0], kv.shape[0]

    def body(q_ref, kv_ref, o_ref):
        for h in range(MEM_H):
            sl = slice(h * MEM_E, (h + 1) * MEM_E)
            p = _mem_probs(q_ref[:, sl], kv_ref[:, sl])
            o_ref[:, sl] = _nn_dot(p, kv_ref[:, MEM_W + h * MEM_E:MEM_W + (h + 1) * MEM_E]).astype(o_ref.dtype)

    return pl.pallas_call(
        body, grid=(l // tm,),
        in_specs=[pl.BlockSpec((tm, MEM_W), lambda i: (i, 0)), pl.BlockSpec((nm, 2 * MEM_W), lambda i: (0, 0))],
        out_specs=pl.BlockSpec((tm, MEM_W), lambda i: (i, 0)),
        out_shape=jax.ShapeDtypeStruct((l, MEM_W), BF16),
        compiler_params=_params(("parallel",)), name=name,
    )(mq, kv)


def _mem_bwd(mq, kv, dmo, name, tm=512):
    l, nm = mq.shape[0], kv.shape[0]
    scale = MEM_E ** -0.5

    def body(q_ref, kv_ref, do_ref, dq_ref, dkv_ref):
        @pl.when(pl.program_id(0) == 0)
        def _():
            dkv_ref[...] = jnp.zeros_like(dkv_ref)

        for h in range(MEM_H):
            sl = slice(h * MEM_E, (h + 1) * MEM_E)
            vsl = slice(MEM_W + h * MEM_E, MEM_W + (h + 1) * MEM_E)
            qh, kh, vh, doh = q_ref[:, sl], kv_ref[:, sl], kv_ref[:, vsl], do_ref[:, sl]
            p = _mem_probs(qh, kh)
            dp = _nt_dot(doh, vh)
            ds = p * (dp - jnp.sum(dp * p, axis=-1, keepdims=True)) * scale
            dq_ref[:, sl] = _nn_dot(ds, kh).astype(dq_ref.dtype)
            dkv_ref[:, sl] += _tn_dot(ds, qh)
            dkv_ref[:, vsl] += _tn_dot(p, doh)

    row = pl.BlockSpec((tm, MEM_W), lambda i: (i, 0))
    full = pl.BlockSpec((nm, 2 * MEM_W), lambda i: (0, 0))
    return pl.pallas_call(
        body, grid=(l // tm,), in_specs=[row, full, row], out_specs=[row, full],
        out_shape=[jax.ShapeDtypeStruct((l, MEM_W), BF16), jax.ShapeDtypeStruct((nm, 2 * MEM_W), F32)],
        compiler_params=_params(("arbitrary",)), name=name,
    )(mq, kv, dmo)


def _gated_out_proj(zg, branches, b_gate, w_o, x, g2, name, tm=512):
    l, d = x.shape
    nbr = len(branches)

    def body(zg_ref, *rest):
        br_refs, (bg_ref, w_ref, x_ref, g2_ref, m_ref, h_ref, n_ref) = rest[:nbr], rest[nbr:]
        merged = jnp.zeros((tm, d), F32)
        for i, br_ref in enumerate(br_refs):
            cols = slice(i * d, (i + 1) * d)
            merged += _sigmoid(zg_ref[:, cols].astype(F32) + bg_ref[:, cols]) * br_ref[...].astype(F32)
        mb = merged.astype(BF16)
        m_ref[...] = mb
        hv = jnp.dot(mb.astype(_MXU), w_ref[...].astype(_MXU), preferred_element_type=F32) + x_ref[...]
        h_ref[...] = hv
        rs = lax.rsqrt(jnp.mean(hv * hv, axis=-1, keepdims=True) + RMS_EPS)
        n_ref[...] = (hv * rs * g2_ref[...]).astype(n_ref.dtype)

    row = lambda c: pl.BlockSpec((tm, c), lambda i: (i, 0))
    full = lambda a: pl.BlockSpec(a.shape, lambda i: (0, 0))
    return pl.pallas_call(
        body, grid=(l // tm,),
        in_specs=[row(nbr * d)] + [row(d)] * nbr + [full(b_gate), full(w_o), row(d), full(g2)],
        out_specs=[row(d)] * 3,
        out_shape=[jax.ShapeDtypeStruct((l, d), BF16), jax.ShapeDtypeStruct((l, d), F32),
                   jax.ShapeDtypeStruct((l, d), BF16)],
        compiler_params=_params(("parallel",)), name=name,
    )(zg, *branches, b_gate, w_o, x, g2)


def _discretize(lam_re, lam_im, log_dt, b_re, b_im):
    dt = jnp.exp(log_dt)[:, None]
    mag = jnp.exp(lam_re * dt)
    a_re, a_im = mag * jnp.cos(lam_im * dt), mag * jnp.sin(lam_im * dt)
    nr, ni = a_re - 1.0, a_im
    den = lam_re * lam_re + lam_im * lam_im
    coef_re = (nr * lam_re + ni * lam_im) / den
    coef_im = (ni * lam_re - nr * lam_im) / den
    bb_re = coef_re[..., None] * b_re - coef_im[..., None] * b_im
    bb_im = coef_re[..., None] * b_im + coef_im[..., None] * b_re
    return a_re, a_im, bb_re, bb_im


def _bd_in(bb):
    return jnp.einsum("gph,gk->ghkp", bb, jnp.eye(SSM_G, dtype=bb.dtype)).reshape(SSM_W, SSM_S)


def _bd_diag(x):
    gb = SSM_G // SSM_BD
    t = x.reshape(SSM_BD, gb, SSM_H, gb, SSM_P)
    return jnp.einsum("bghgp->bghp", t).reshape(SSM_G, SSM_H, SSM_P)


_ANY = pl.BlockSpec(memory_space=pl.ANY)
_MESH = pl.DeviceIdType.MESH


def _allgather(x, name):
    def body(x_ref, out_ref, send_sems, recv_sems, local_sem):
        mx, my, mc = lax.axis_index("x"), lax.axis_index("y"), lax.axis_index("c")
        me, sibling = (mx, my, mc), (mx, my, 1 - mc)
        chips = [(1 - mx, my), (mx, 1 - my), (1 - mx, 1 - my)]

        def blk(px, py, pc):
            return out_ref.at[4 * px + 2 * py + pc]

        def copy(k, block, to, src=None):
            return pltpu.make_async_remote_copy(
                src_ref=blk(*block) if src is None else src, dst_ref=blk(*block),
                send_sem=send_sems.at[k], recv_sem=recv_sems.at[k], device_id=to, device_id_type=_MESH)

        mine = pltpu.make_async_copy(x_ref, blk(*me), local_sem)
        mine.start()
        first = [copy(0, me, sibling, src=x_ref)]
        first += [copy(1 + j, me, (*chip, mc), src=x_ref) for j, chip in enumerate(chips)]
        for cp in first:
            cp.start()
        passed = [copy(4 + j, (*chip, mc), sibling) for j, chip in enumerate(chips)]
        for j, chip in enumerate(chips):
            copy(1 + j, (*chip, mc), me).wait_recv()
            passed[j].start()
        copy(0, sibling, me).wait_recv()
        for j, chip in enumerate(chips):
            copy(4 + j, (*chip, 1 - mc), me).wait_recv()
        for cp in first + passed:
            cp.wait_send()
        mine.wait()

    return pl.pallas_call(
        body, out_shape=jax.ShapeDtypeStruct((N_DEV,) + x.shape, x.dtype), in_specs=[_ANY], out_specs=_ANY,
        scratch_shapes=[pltpu.SemaphoreType.DMA((7,)), pltpu.SemaphoreType.DMA((7,)), pltpu.SemaphoreType.DMA],
        name=name,
    )(x)


def _pair_exchange(g, name):
    def body(g_ref, out_ref, send_sems, recv_sems):
        mx, my, mc = lax.axis_index("x"), lax.axis_index("y"), lax.axis_index("c")
        copies = [pltpu.make_async_remote_copy(
            src_ref=g_ref.at[2 * k + (1 - mc)], dst_ref=out_ref.at[k], send_sem=send_sems.at[k],
            recv_sem=recv_sems.at[k], device_id=(mx, my, 1 - mc), device_id_type=_MESH) for k in range(4)]
        for cp in copies:
            cp.start()
        for cp in copies:
            cp.wait()

    return pl.pallas_call(
        body, out_shape=jax.ShapeDtypeStruct((4,) + g.shape[1:], g.dtype), in_specs=[_ANY], out_specs=_ANY,
        scratch_shapes=[pltpu.SemaphoreType.DMA((4,)), pltpu.SemaphoreType.DMA((4,))], name=name,
    )(g)


_HBM = pl.BlockSpec(memory_space=pltpu.HBM)
_SEM = pl.BlockSpec(memory_space=pltpu.SEMAPHORE)
_EFFECT = pltpu.SideEffectType.DATAFLOW_SIDE_EFFECTING
_TOKEN = jax.ShapeDtypeStruct((8, 128), F32)


def _peer(rel):
    pos = (lax.axis_index("x"), lax.axis_index("y"), lax.axis_index("c"))
    return tuple(1 - p if (rel >> (2 - i)) & 1 else p for i, p in enumerate(pos))


def _index_of(dev):
    return 4 * dev[0] + 2 * dev[1] + dev[2]


def _split_copies(src_ref, land_ref, sems, plan):
    n = len(plan)
    return [pltpu.make_async_remote_copy(
        src_ref=src_ref if s is None else src_ref.at[s], dst_ref=land_ref.at[d], send_sem=sems[k],
        recv_sem=sems[n + k], device_id=peer, device_id_type=_MESH) for k, (s, d, peer) in enumerate(plan)]


def _split_start(src, n_land, plan_fn, after, name):
    blk = src.shape[-2:]
    land = lax.empty((n_land,) + blk, src.dtype)
    n = len(plan_fn())

    def body(src_ref, land_ref, after_ref, *outs):
        for cp in _split_copies(src_ref, land_ref, outs[:2 * n], plan_fn()):
            cp.start()
        outs[2 * n + 2][...] = jnp.zeros_like(outs[2 * n + 2])

    res = pl.pallas_call(
        body, name=name,
        out_shape=(pltpu.SemaphoreType.DMA(()),) * (2 * n)
        + (pltpu.HBM(src.shape, src.dtype), pltpu.HBM(land.shape, land.dtype), _TOKEN),
        in_specs=(_HBM, _HBM, _ANY),
        out_specs=(_SEM,) * (2 * n) + (_HBM, _HBM, pl.BlockSpec(memory_space=pltpu.VMEM)),
        input_output_aliases={0: 2 * n, 1: 2 * n + 1},
        compiler_params=pltpu.CompilerParams(has_side_effects=_EFFECT),
    )(pltpu.with_memory_space_constraint(src, pltpu.HBM), pltpu.with_memory_space_constraint(land, pltpu.HBM), after)
    return res[:2 * n], res[2 * n], res[2 * n + 1], res[2 * n + 2]


def _split_wait(sems, src, land, plan_fn, after, name):
    n = len(sems) // 2

    def body(src_ref, land_ref, *rest):
        for cp in _split_copies(src_ref, land_ref, rest[:2 * n], plan_fn()):
            cp.wait_send()
            cp.wait_recv()

    return pl.pallas_call(
        body, name=name,
        out_shape=(pltpu.HBM(src.shape, src.dtype), pltpu.HBM(land.shape, land.dtype)),
        in_specs=(_HBM, _HBM) + (_SEM,) * (2 * n) + (_ANY,), out_specs=(_HBM, _HBM),
        input_output_aliases={0: 0, 1: 1},
        compiler_params=pltpu.CompilerParams(has_side_effects=_EFFECT),
    )(src, land, *sems, after)


def _gather_plan():
    me = _index_of(_peer(0))
    return [(None, me, _peer(rel)) for rel in range(1, N_DEV)]


def _gather_wait_plan():
    return [(None, _index_of(_peer(rel)), _peer(rel)) for rel in range(1, N_DEV)]


def _chip_plan():
    return [(_index_of(_peer(rel)) // 2, j, _peer(rel)) for j, rel in enumerate((4, 2, 6))]


def _pair_sum(g, t1, my_c, name, tr):
    _, r, c = g.shape

    def body(c_ref, g_ref, t_ref, o_ref, ob_ref):
        s = g_ref[...] + t_ref[...]
        o_ref[...] = s
        ob_ref[...] = s.astype(BF16)

    blk = pl.BlockSpec((None, tr, c), lambda k, i, cr: (k, i, 0))
    return pl.pallas_call(
        body,
        grid_spec=pltpu.PrefetchScalarGridSpec(
            num_scalar_prefetch=1, grid=(4, r // tr),
            in_specs=[pl.BlockSpec((None, tr, c), lambda k, i, cr: (2 * k + cr[0], i, 0)), blk],
            out_specs=[blk, blk]),
        out_shape=[jax.ShapeDtypeStruct((4, r, c), F32), jax.ShapeDtypeStruct((4, r, c), BF16)],
        compiler_params=_params(("parallel", "parallel")), name=name,
    )(my_c, g, t1)


def _adam_math(g, w, m, v):
    m = ADAM_B1 * m + (1.0 - ADAM_B1) * g
    v = ADAM_B2 * v + (1.0 - ADAM_B2) * (g * g)
    m_hat = m / (1.0 - ADAM_B1 ** ADAM_STEP)
    v_hat = v / (1.0 - ADAM_B2 ** ADAM_STEP)
    delta = -ADAM_LR * (m_hat / (jnp.sqrt(v_hat) + ADAM_EPS) + ADAM_WD * w)
    return delta, m, v


def _grad_sum(p, t2, my_chip, name, tr):
    _, r, c = p.shape

    def body(k_ref, p_ref, t0_ref, t1_ref, t2_ref, g_out):
        g_out[...] = ((p_ref[...] + t0_ref[...].astype(F32)) + t1_ref[...].astype(F32)) + t2_ref[...].astype(F32)

    def rel(j):
        return pl.BlockSpec((None, tr, c), lambda i, kr: (j, i, 0))

    return pl.pallas_call(
        body,
        grid_spec=pltpu.PrefetchScalarGridSpec(
            num_scalar_prefetch=1, grid=(r // tr,),
            in_specs=[pl.BlockSpec((None, tr, c), lambda i, kr: (kr[0], i, 0)), rel(0), rel(1), rel(2)],
            out_specs=pl.BlockSpec((tr, c), lambda i, kr: (i, 0))),
        out_shape=jax.ShapeDtypeStruct((r, c), F32),
        compiler_params=_params(("parallel",)), name=name,
    )(my_chip, p, t2, t2, t2)


def _adam_many(g, w, m, v, row_tiles, name):
    n = len(g)

    def body(*refs):
        ins, outs = refs[:4 * n], refs[4 * n:]
        for i in range(n):
            res = _adam_math(ins[i][...], ins[n + i][...], ins[2 * n + i][...], ins[3 * n + i][...])
            for kind in range(3):
                outs[kind * n + i][...] = res[kind]

    def spec(a):
        blk = (a.shape[0] // row_tiles,) + a.shape[1:]
        return pl.BlockSpec(blk, lambda t, nd=a.ndim: (t,) + (0,) * (nd - 1))

    specs = [spec(a) for a in g]
    res = pl.pallas_call(
        body, grid=(row_tiles,), in_specs=specs * 4, out_specs=specs * 3,
        out_shape=[jax.ShapeDtypeStruct(a.shape, F32) for a in g] * 3,
        compiler_params=_params(("parallel",)), name=name,
    )(*g, *w, *m, *v)
    return res[:n], res[n:2 * n], res[2 * n:]


def _sum8(g8, name):
    _, r, c = g8.shape

    def body(g_ref, o_ref):
        acc = g_ref[0]
        for j in range(1, N_DEV):
            acc = acc + g_ref[j]
        o_ref[...] = acc

    return pl.pallas_call(
        body, grid=(1,), in_specs=[pl.BlockSpec((N_DEV, r, c), lambda i: (0, 0, 0))],
        out_specs=pl.BlockSpec((r, c), lambda i: (0, 0)), out_shape=jax.ShapeDtypeStruct((r, c), F32),
        compiler_params=_params(("arbitrary",)), name=name,
    )(g8)


def _pack(arrs, pad_rows=8):
    flat = jnp.concatenate([a.reshape(-1) for a in arrs])
    n = flat.shape[0]
    q = PACK_C * pad_rows
    tot = -(-n // q) * q
    if tot != n:
        flat = jnp.concatenate([flat, jnp.zeros((tot - n,), flat.dtype)])
    return flat.reshape(tot // PACK_C, PACK_C)


def _unpack(buf, shapes):
    flat = buf.reshape(-1)
    out, off = [], 0
    for s in shapes:
        n = int(np.prod(s))
        out.append(flat[off:off + n].reshape(s))
        off += n
    return out


GROUPS = (("w_in", "w_mem_kv", "w_o"),
          ("w_glu", "w_ssm_br", "w_mem_br", "w_attn_br"),
          ("w_up", "w_down"))
GROUP_TR = (528, 384, 512)
MLP_GROUP = 2
ATTN_BR_FOLD = 2


def _stored_shape(name):
    r, c, ax = BIG_SHAPE[name]
    rows, cols = (r // N_DEV, c) if ax == 0 else (c // N_DEV, r)
    return (rows // ATTN_BR_FOLD, cols * ATTN_BR_FOLD) if name == "w_attn_br" else (rows, cols)


def _stored(shard, name):
    a = shard[0].T if BIG_SHAPE[name][2] == 1 else shard[0]
    return a.reshape(_stored_shape(name))


def _unstored(a, name):
    r, c, ax = BIG_SHAPE[name]
    if ax == 0:
        return a.reshape(1, r // N_DEV, c)
    return a.reshape(c // N_DEV, r).T[None]


def _pack_group(d, names):
    return jnp.concatenate([_stored(d[n], n) for n in names], axis=0)


def _split_group(buf, names):
    out, off = {}, 0
    for n in names:
        rows = _stored_shape(n)[0]
        out[n] = buf[..., off:off + rows, :]
        off += rows
    return out


def _full_stored(stacked, name):
    r, c, ax = BIG_SHAPE[name]
    return stacked.reshape((r, c) if ax == 0 else (c, r))


def _stacked_stored(full, name):
    return full.reshape((N_DEV,) + _stored_shape(name))


def _gelu_parts(x):
    c0, c1 = math.sqrt(2.0 / math.pi), 0.044715
    th = jnp.tanh(c0 * (x + c1 * x * x * x))
    return th, c0, c1


def _local_step(x, mem, tgt, wb, sp, late_weights, grads_ready):
    l = x.shape[0]
    w_a, w_g = wb["w_in"][:ZA_W], wb["w_in"][ZA_W:]

    a_re, a_im, bb_re, bb_im = _discretize(sp["ssm_lambda_re"], sp["ssm_lambda_im"], sp["ssm_log_dt"],
                                           sp["ssm_b_re"], sp["ssm_b_im"])
    a_pair = jnp.stack([a_re.reshape(1, SSM_S), a_im.reshape(1, SSM_S)])
    a_conj = jnp.stack([a_re.reshape(1, SSM_S), -a_im.reshape(1, SSM_S)])
    b_re_t, b_im_t = _bd_in(bb_re).astype(BF16), _bd_in(bb_im).astype(BF16)
    c_re_t = _bd_in(sp["ssm_c_re"].transpose(0, 2, 1)).astype(BF16)
    c_im_t = (-_bd_in(sp["ssm_c_im"].transpose(0, 2, 1))).astype(BF16)
    d_row = sp["ssm_d"].reshape(1, SSM_W)

    n1 = _rms_fwd(x, sp["norm1_g"], "rms1")
    za = _mm(n1, w_a, [BF16], tb=True, name="in_proj_a", tn=1664)
    zg = _mm(n1, w_g, [BF16], tb=True, name="in_proj_g")
    wb = {**wb, **late_weights(1, za)}
    u = za[:, :SSM_W]
    mq = za[:, ZA_W - MEM_W:]

    u_s = _scan_order(u)
    s_all = _ssm_scan(u_s, b_re_t, b_im_t, a_pair, reverse=False, name="ssm_scan_fwd")
    ys = _time_order(_mm((s_all, 0), c_re_t, [F32], tb=True, pair2=((s_all, 1), c_im_t), bd=SSM_BD, name="ssm_cs"))

    def gelu_fn(r, b):
        y0 = r[0] + b[0] * r[1].astype(F32)
        th, _, _ = _gelu_parts(y0)
        return [y0, 0.5 * y0 * (1.0 + th)], []
    y0, y1 = _ew(gelu_fn, [ys, u], [d_row], [(SSM_W, F32), (SSM_W, BF16)], [], name="ssm_gelu", tm=512)

    def glu_epi(acc, y1t, bg):
        t = acc + bg
        return t, y1t.astype(F32) * _sigmoid(t)
    t_glu, y2 = _mm(y1, wb["w_glu"], [F32, BF16], epi=glu_epi, mn=[y1], rows=[sp["b_glu"]], name="ssm_glu")
    br_ssm = _mm(y2, wb["w_ssm_br"], [BF16], tb=True, name="ssm_br")

    qkv_p, o_g, lse_g = [], [], []
    for g, d in enumerate(DILATIONS):
        nb = l // d // ATT_WIN
        cols = [za[:, SSM_W + (3 * j + g) * ATT_GW: SSM_W + (3 * j + g + 1) * ATT_GW] for j in range(3)]
        qp, kp, vp = [_to_perm(cc, d) for cc in cols]
        qkv_p.append((qp, kp, vp))
        og, lg = _attn_fwd(qp, kp, vp, nb, "attn_fwd%d" % g)
        o_g.append(_from_perm(og, d))
        lse_g.append(_from_perm(lg, d))

    def merge_fn(r, b):
        o0, o1, o2, l0, l1, l2 = r
        mx = jnp.maximum(jnp.maximum(l0, l1), l2)
        e0, e1, e2 = jnp.exp(l0 - mx), jnp.exp(l1 - mx), jnp.exp(l2 - mx)
        tot = e0 + e1 + e2
        return [(e0 * o0 + e1 * o1 + e2 * o2) / tot, mx + jnp.log(tot)], []
    o_att, lse_tot = _ew(merge_fn, o_g + lse_g, [], [(ATT_GW, F32), (ATT_GW, F32)], [], name="attn_merge", tm=512)
    br_attn = _mm(o_att, wb["w_attn_br"], [BF16], tb=True, name="attn_br")

    mn = _rms_fwd(mem, sp["mem_norm_g"], "rms_mem")
    kv = _mm(mn, wb["w_mem_kv"], [BF16], name="mem_kv")
    mo = _mem_fwd(mq, kv, "mem_attn_fwd")
    br_mem = _mm(mo, wb["w_mem_br"], [BF16], tb=True, name="mem_br")

    merged, h1, n2 = _gated_out_proj(zg, [br_ssm, br_attn, br_mem], sp["b_gate"], wb["w_o"], x, sp["norm2_g"],
                                     "gated_o_proj")

    def up_epi(acc):
        ra = jnp.maximum(acc, 0.0)
        return ra * ra, ra
    wm = late_weights(MLP_GROUP, n2)
    f_act, r_act = _mm(n2, wm["w_up"], [BF16, BF16], tb=True, epi=up_epi, name="mlp_up")
    def down_epi(acc, ht, tv, gf):
        hv = acc + ht
        rs = lax.rsqrt(jnp.mean(hv * hv, axis=-1, keepdims=True) + RMS_EPS)
        err = hv * rs * gf - tv
        dh, dgf = _rms_bwd_tile(hv, err * (1.0 / D_MODEL), gf)
        return dh, dgf, _colsum(err * err) * (0.5 / D_MODEL)
    dh2, d_final_g, loss_cols = _mm(f_act, wm["w_down"], [F32], epi=down_epi, mn=[h1, tgt], rows=[sp["final_g"]],
                                    n_sums=2, tk=1024, name="mlp_down")
    loss = jnp.sum(loss_cols, axis=1, keepdims=True)

    gw, gs = {}, {"final_g": d_final_g}
    d_act = _mm(dh2, wm["w_down"], [BF16], tb=True, epi=lambda acc, ra: (acc * 2.0 * ra.astype(F32),), mn=[r_act],
                name="mlp_down_dx")
    dw_down = _mm(f_act, dh2, [F32], ta=True, name="mlp_down_dw")
    dw_up = _mm(d_act, n2, [F32], ta=True, name="mlp_up_dw")
    token = grads_ready(MLP_GROUP, {"w_up": dw_up, "w_down": dw_down})
    def up_dx_epi(acc, ht, dht, g2):
        dx, dg = _rms_bwd_tile(ht, acc, g2)
        return dx + dht, dg
    dh1, gs["norm2_g"] = _mm(d_act, wm["w_up"], [F32], epi=up_dx_epi, mn=[h1, dh2],
                             rows=[sp["norm2_g"] + token[:1, :1]], n_sums=1, tk=1024, name="mlp_up_dx")
    gw["w_o"] = _mm(merged, dh1, [F32], ta=True, name="o_proj_dw")

    def gate_bwd_epi(dm, *tiles):
        dbr, dz = [], []
        for zt, bt, bias in zip(tiles[0:3], tiles[3:6], tiles[6:9]):
            gt = _sigmoid(zt.astype(F32) + bias)
            dbr.append(dm * gt)
            dz.append(dm * bt.astype(F32) * gt * (1.0 - gt))
        return (*dbr, *dz, *[_colsum(t) for t in dz])
    gate_bias = [sp["b_gate"][:, i * D_MODEL:(i + 1) * D_MODEL] for i in range(3)]
    res = _mm(dh1, wb["w_o"], [BF16] * 6, tb=True, epi=gate_bwd_epi, mn=[(zg, 0), (zg, 1), (zg, 2), br_ssm, br_attn, br_mem],
              rows=gate_bias, n_sums=3, tm=512, name="o_proj_dx")
    (dbr_ssm, dbr_attn, dbr_mem), dzg = res[0:3], res[3:6]
    gs["b_gate"] = jnp.concatenate(res[6:9], axis=1)

    gw["w_ssm_br"] = _mm(dbr_ssm, y2, [F32], ta=True, name="ssm_br_dw")
    dy2 = _mm(dbr_ssm, wb["w_ssm_br"], [F32], name="ssm_br_dx")

    def glu_bwd_fn(r, b):
        dy, y1t, tt = r
        sg = _sigmoid(tt)
        dt = dy * y1t.astype(F32) * sg * (1.0 - sg)
        return [dt, dy * sg], [_colsum(dt)]
    dt_glu, dy1a, gs["b_glu"] = _ew(glu_bwd_fn, [dy2, y1, t_glu], [], [(SSM_W, BF16), (SSM_W, F32)], [SSM_W],
                                    name="ssm_glu_bwd", tm=512)
    gw["w_glu"] = _mm(y1, dt_glu, [F32], ta=True, name="ssm_glu_dw")

    def gelu_bwd_epi(acc, dy1t, y0t):
        th, c0, c1 = _gelu_parts(y0t)
        dg = 0.5 * (1.0 + th) + 0.5 * y0t * (1.0 - th * th) * c0 * (1.0 + 3.0 * c1 * y0t * y0t)
        return ((acc + dy1t) * dg,)
    dy0 = _mm(dt_glu, wb["w_glu"], [F32], tb=True, epi=gelu_bwd_epi, mn=[dy1a, y0], name="ssm_glu_dx")
    gs["ssm_d"] = _ew(lambda r, b: ([], [_colsum(r[0] * r[1].astype(F32))]), [dy0, u], [], [], [SSM_W],
                      name="ssm_dd", tm=512)[0]
    dy0_s = _scan_order(dy0)
    lam, da, d_b, d_c = _ssm_scan(dy0_s, c_re_t, c_im_t, a_conj, reverse=True, s_fwd=s_all, u=u_s,
                                  name="ssm_scan_bwd")
    du = _time_order(_mm((lam, 0), b_re_t, [BF16], tb=True, pair2=((lam, 1), b_im_t),
                         epi=lambda acc, dyt, dr: (acc + dyt * dr,), mn=[dy0_s], rows=[d_row], bd=SSM_BD, name="ssm_bu_dx"))
    gs["a_re"], gs["a_im"] = da[0], da[1]
    gs["bb_re"], gs["bb_im"] = _bd_diag(d_b[0]).transpose(0, 2, 1), _bd_diag(d_b[1]).transpose(0, 2, 1)
    gs["ssm_c_re"], gs["ssm_c_im"] = _bd_diag(d_c[0]), -_bd_diag(d_c[1])

    gw["w_attn_br"] = _mm(dbr_attn, o_att, [F32], ta=True, name="attn_br_dw")

    def do_epi(acc, ot):
        prod = acc * ot
        head = lax.broadcasted_iota(jnp.int32, prod.shape, 1) // ATT_E
        dd = jnp.zeros_like(prod)
        for h in range(ATT_HG):
            dd = jnp.where(head == h, jnp.sum(jnp.where(head == h, prod, 0.0), axis=1, keepdims=True), dd)
        return acc, dd
    do_att, dd_att = _mm(dbr_attn, wb["w_attn_br"], [BF16, F32], epi=do_epi, mn=[o_att], name="attn_br_dx")
    dq_l, dk_l, dv_l = [], [], []
    for g, d in enumerate(DILATIONS):
        nb = l // d // ATT_WIN
        qp, kp, vp = qkv_p[g]
        dq, dk, dv = _attn_bwd(qp, kp, vp, _to_perm(do_att, d), _to_perm(lse_tot, d), _to_perm(dd_att, d),
                               nb, "attn_bwd%d" % g)
        dq_l.append(_from_perm(dq, d))
        dk_l.append(_from_perm(dk, d))
        dv_l.append(_from_perm(dv, d))

    gw["w_mem_br"] = _mm(dbr_mem, mo, [F32], ta=True, name="mem_br_dw")
    dmo = _mm(dbr_mem, wb["w_mem_br"], [BF16], name="mem_br_dx")
    dmq, dkv = _mem_bwd(mq, kv, dmo, "mem_attn_bwd")
    gw["w_mem_kv"] = _mm(mn, dkv, [F32], ta=True, name="mem_kv_dw")
    dmn = _mm(dkv, wb["w_mem_kv"], [F32], tb=True, name="mem_kv_dx")
    token = grads_ready(1, gw)
    gs["mem_norm_g"] = _rms_bwd(mem, dmn, None, sp["mem_norm_g"] + token[:1, :1], "rms_mem_bwd")[1]

    dza = jnp.concatenate([du] + dq_l + dk_l + dv_l + [dmq], axis=1)
    dn_a = _mm(dza, w_a, [F32], name="in_proj_a_dx", tk=1664)
    dw_a = _mm(dza, n1, [F32], ta=True, name="in_proj_a_dw", tm=1664)
    dw_g = [_mm(dzg[i], n1, [F32], ta=True, name="in_proj_g_dw%d" % i) for i in range(3)]
    gw["w_in"] = jnp.concatenate([dw_a] + dw_g, axis=0)
    token = grads_ready(0, gw)
    def in_dx_epi(acc, pt, xt, dht, g1):
        dx, dg = _rms_bwd_tile(xt, acc + pt, g1)
        return dx + dht, dg
    w_gs = [w_g[i * D_MODEL:(i + 1) * D_MODEL] for i in range(3)]
    grad_x, gs["norm1_g"] = _mm(dzg[0], w_gs[0], [F32], pair2=(dzg[1], w_gs[1], dzg[2], w_gs[2]), epi=in_dx_epi,
                                mn=[dn_a, x, dh1],
                                rows=[sp["norm1_g"] + token[:1, :1]], n_sums=1, tm=512, name="in_proj_g_dx")
    return loss, grad_x, gs


_SMALL_GRAD_ORDER = ("norm1_g", "mem_norm_g", "b_gate", "a_re", "a_im", "bb_re", "bb_im", "ssm_c_re", "ssm_c_im",
                     "ssm_d", "b_glu", "norm2_g", "final_g")


def kernel(x, mem, norm1_g, mem_norm_g, w_in, b_gate, ssm_lambda_re, ssm_lambda_im, ssm_log_dt, ssm_b_re, ssm_b_im, ssm_c_re, ssm_c_im, ssm_d, w_glu, b_glu, w_ssm_br, w_attn_br, w_mem_kv, w_mem_br, w_o, norm2_g, w_up, w_down, final_g, loss_target, m_norm1_g, m_mem_norm_g, m_w_in, m_b_gate, m_ssm_lambda_re, m_ssm_lambda_im, m_ssm_log_dt, m_ssm_b_re, m_ssm_b_im, m_ssm_c_re, m_ssm_c_im, m_ssm_d, m_w_glu, m_b_glu, m_w_ssm_br, m_w_attn_br, m_w_mem_kv, m_w_mem_br, m_w_o, m_norm2_g, m_w_up, m_w_down, m_final_g, v_norm1_g, v_mem_norm_g, v_w_in, v_b_gate, v_ssm_lambda_re, v_ssm_lambda_im, v_ssm_log_dt, v_ssm_b_re, v_ssm_b_im, v_ssm_c_re, v_ssm_c_im, v_ssm_d, v_w_glu, v_b_glu, v_w_ssm_br, v_w_attn_br, v_w_mem_kv, v_w_mem_br, v_w_o, v_norm2_g, v_w_up, v_w_down, v_final_g):
    args = dict(locals())
    w = {n: args[n] for n in ALL_W}
    m = {n: args["m_" + n] for n in ALL_W}
    v = {n: args["v_" + n] for n in ALL_W}
    my_c = lax.axis_index("c").astype(jnp.int32).reshape(1)
    my_chip = (2 * lax.axis_index("x") + lax.axis_index("y")).astype(jnp.int32).reshape(1)

    w_pack = [_pack_group(w, names) for names in GROUPS]
    my_index = (4 * lax.axis_index("x") + 2 * lax.axis_index("y") + lax.axis_index("c")).astype(jnp.int32)
    zero = jnp.zeros((), jnp.int32)
    w_all = _allgather(w_pack[0].astype(BF16), "allgather_weights0")
    wb = {n: _full_stored(part, n) for n, part in _split_group(w_all, GROUPS[0]).items()}
    gathers = {gi: _split_start(w_pack[gi].astype(BF16), N_DEV, _gather_plan, w_all, "weights_gather_start%d" % gi)
               for gi in range(1, len(GROUPS))}

    def gathered(started, after, name):
        sems, src, land, _ = started
        src, land = _split_wait(sems, src, land, _gather_wait_plan, after, name)
        return lax.dynamic_update_slice(land, src[None], (my_index, zero, zero))

    def late_weights(gi, after):
        full = gathered(gathers[gi], after, "weights_gather_wait%d" % gi)
        return {n: _full_stored(part, n) for n, part in _split_group(full, GROUPS[gi]).items()}

    pending = {}

    def grads_ready(gi, grads):
        g_pack = jnp.concatenate([_stacked_stored(grads[n], n) for n in GROUPS[gi]], axis=1)
        t1 = _pair_exchange(g_pack, "grad_pair_exchange%d" % gi)
        p_sum, p_bf = _pair_sum(g_pack, t1, my_c, "grad_pair_sum%d" % gi, GROUP_TR[gi])
        sems, src, land, token = _split_start(p_bf, 3, _chip_plan, p_sum, "grad_chip_exchange_start%d" % gi)
        pending[gi] = (p_sum, sems, src, land)
        return token

    sp = {
        "norm1_g": norm1_g + sum(started[3][:1, :1] for started in gathers.values()), "mem_norm_g": mem_norm_g, "b_gate": b_gate, "b_glu": b_glu, "norm2_g": norm2_g,
        "final_g": final_g.reshape(1, D_MODEL),
        "ssm_lambda_re": ssm_lambda_re[0], "ssm_lambda_im": ssm_lambda_im[0], "ssm_log_dt": ssm_log_dt[0],
        "ssm_b_re": ssm_b_re[0], "ssm_b_im": ssm_b_im[0], "ssm_c_re": ssm_c_re[0], "ssm_c_im": ssm_c_im[0],
        "ssm_d": ssm_d[0],
    }
    loss, grad_x, gs = _local_step(x[0], mem[0], loss_target[0], wb, sp, late_weights, grads_ready)
    loss = lax.psum(loss[0, 0], ("x", "y", "c"))
    sg_shapes = [gs[n].shape for n in _SMALL_GRAD_ORDER]
    sg_started = _split_start(_pack([gs[n] for n in _SMALL_GRAD_ORDER]), N_DEV, _gather_plan, grad_x,
                              "small_grads_gather_start")

    big_g = {}
    for gi, names in enumerate(GROUPS):
        p_sum, sems, src, land = pending[gi]
        t2 = _split_wait(sems, src, land, _chip_plan, grad_x, "grad_chip_exchange_wait%d" % gi)[1]
        g_pack = _grad_sum(p_sum, t2, my_chip, "grad_sum%d" % gi, GROUP_TR[gi])
        for n, part in _split_group(g_pack, names).items():
            big_g[n] = _unstored(part, n)
    rows_of = lambda d, names: [d[n].reshape(d[n].shape[-2:]) for n in names]
    big_out = _adam_many(rows_of(big_g, BIG), rows_of(w, BIG), rows_of(m, BIG), rows_of(v, BIG), 8, "adam_big")
    big = [big_g] + [{n: a[None] for n, a in zip(BIG, outs)} for outs in big_out]

    sg_all = gathered(sg_started, big_out[0][0], "small_grads_gather_wait")
    sg = dict(zip(_SMALL_GRAD_ORDER, _unpack(_sum8(sg_all, "sum_small_grads"), sg_shapes)))
    _, disc_vjp = jax.vjp(_discretize, sp["ssm_lambda_re"], sp["ssm_lambda_im"], sp["ssm_log_dt"],
                          sp["ssm_b_re"], sp["ssm_b_im"])
    d_lre, d_lim, d_ldt, d_bre, d_bim = disc_vjp((sg["a_re"].reshape(SSM_G, SSM_P), sg["a_im"].reshape(SSM_G, SSM_P),
                                                  sg["bb_re"], sg["bb_im"]))
    small_grad = {
        "norm1_g": sg["norm1_g"], "mem_norm_g": sg["mem_norm_g"], "b_gate": sg["b_gate"],
        "ssm_lambda_re": d_lre, "ssm_lambda_im": d_lim, "ssm_log_dt": d_ldt, "ssm_b_re": d_bre, "ssm_b_im": d_bim,
        "ssm_c_re": sg["ssm_c_re"], "ssm_c_im": sg["ssm_c_im"], "ssm_d": sg["ssm_d"], "b_glu": sg["b_glu"],
        "norm2_g": sg["norm2_g"], "final_g": sg["final_g"],
    }
    small_grad = {n: small_grad[n].reshape(w[n].shape) for n in SMALL}

    def squeezed(a):
        return a.reshape(a.shape[1:]) if a.ndim > 2 else a.reshape(1, -1)

    sq = lambda d: [squeezed(d[n]) for n in SMALL]
    small_out = _adam_many(sq(small_grad), sq(w), sq(m), sq(v), 1, "adam_small")
    small = [small_grad] + [{n: a.reshape(w[n].shape) for n, a in zip(SMALL, outs)} for outs in small_out]

    outs = [loss, grad_x[None]]
    for kind in range(4):
        for n in ALL_W:
            outs.append(big[kind][n] if n in BIG else small[kind][n])
    return tuple(outs)
```

```python
import math

import numpy as np
import jax
import jax.numpy as jnp
from jax import lax
from jax.experimental import pallas as pl
from jax.experimental.pallas import tpu as pltpu

F32 = jnp.float32
BF16 = jnp.bfloat16
_MXU = jnp.bfloat16

D_MODEL = 1024
SSM_G, SSM_H, SSM_P = 32, 16, 64
SSM_W = SSM_G * SSM_H
SSM_S = SSM_G * SSM_P
SSM_BD = 4
ATT_E = 64
ATT_HG = 4
ATT_GW = ATT_HG * ATT_E
ATT_WIN = 128
ATT_QB = 8
ATT_QB_FWD = 4
DILATIONS = (1, 4, 16)
MEM_H, MEM_E = 4, 128
MEM_W = MEM_H * MEM_E
ZA_W = SSM_W + 9 * ATT_GW + MEM_W
ZG_W = 3 * D_MODEL
IN_W = ZA_W + ZG_W
RMS_EPS = 1e-6
NEG_INF = -1e30

ADAM_LR, ADAM_B1, ADAM_B2, ADAM_EPS, ADAM_WD, ADAM_STEP = 0.001, 0.9, 0.999, 1e-08, 0.01, 10

N_DEV = 8
PACK_C = 512
_VMEM_LIMIT = 56 * 1024 * 1024
SUBLANES = 16
SCAN_SEG = 128
SCAN_CHAINS = 4
SCAN_UNROLL = 4
SCAN_W = 128

BIG = ("w_in", "w_glu", "w_ssm_br", "w_attn_br", "w_mem_kv", "w_mem_br", "w_o", "w_up", "w_down")
BIG_SHAPE = {
    "w_in": (D_MODEL, IN_W, 1), "w_glu": (SSM_W, SSM_W, 0), "w_ssm_br": (SSM_W, D_MODEL, 1),
    "w_attn_br": (ATT_GW, D_MODEL, 1), "w_mem_kv": (D_MODEL, 2 * MEM_W, 0), "w_mem_br": (MEM_W, D_MODEL, 1),
    "w_o": (D_MODEL, D_MODEL, 0), "w_up": (D_MODEL, 4 * D_MODEL, 1), "w_down": (4 * D_MODEL, D_MODEL, 0),
}
SMALL = ("norm1_g", "mem_norm_g", "b_gate", "ssm_lambda_re", "ssm_lambda_im", "ssm_log_dt", "ssm_b_re",
         "ssm_b_im", "ssm_c_re", "ssm_c_im", "ssm_d", "b_glu", "norm2_g", "final_g")
ALL_W = ("norm1_g", "mem_norm_g", "w_in", "b_gate", "ssm_lambda_re", "ssm_lambda_im", "ssm_log_dt", "ssm_b_re",
         "ssm_b_im", "ssm_c_re", "ssm_c_im", "ssm_d", "w_glu", "b_glu", "w_ssm_br", "w_attn_br", "w_mem_kv",
         "w_mem_br", "w_o", "norm2_g", "w_up", "w_down", "final_g")


def _params(sem):
    return pltpu.CompilerParams(dimension_semantics=sem, vmem_limit_bytes=_VMEM_LIMIT)


def _pick(n, cap):
    if n <= cap:
        return n
    t = (cap // 128) * 128
    while n % t:
        t -= 128
    return t


def _mm(a, b, outs, *, name, ta=False, tb=False, epi=None, mn=(), rows=(), pair2=None, bd=0, n_sums=0,
        tm=1024, tn=1024, tk=2048):
    ab = [a, b] + (list(pair2) if pair2 is not None else [])
    planes = [op[1] if isinstance(op, tuple) else None for op in ab]
    ab = [op[0] if isinstance(op, tuple) else op for op in ab]
    a_shape, b_shape = ab[0].shape[-2:], ab[1].shape[-2:]
    m = a_shape[1] if ta else a_shape[0]
    k = a_shape[0] if ta else a_shape[1]
    n = b_shape[0] if tb else b_shape[1]
    assert k == (b_shape[1] if tb else b_shape[0]), (name, a_shape, b_shape)
    out_n = n
    if bd and ta:
        assert not tb
        tm, tn, tk = m // bd, n // bd, _pick(k, tk)
        grid, out_n = (bd, 1, k // tk), tn
        a_blk = ((tk, tm), lambda i, j, kk: (kk, i))
        b_blk = ((tk, tn), lambda i, j, kk: (kk, i))
        mn_spec = pl.BlockSpec((tm, tn), lambda i, j, kk: (i, 0))
    elif bd:
        tm, tn, tk = _pick(m, tm), n // bd, k // bd
        grid = (m // tm, bd, 1)
        a_blk = ((tm, tk), lambda i, j, kk: (i, j))
        b_blk = ((tn, tk) if tb else (tk, tn), lambda i, j, kk: (j, j))
        mn_spec = pl.BlockSpec((tm, tn), lambda i, j, kk: (i, j))
    else:
        tm, tn, tk = _pick(m, tm), _pick(n, tn), _pick(k, tk)
        grid = (m // tm, n // tn, k // tk)
        a_blk = ((tk, tm), lambda i, j, kk: (kk, i)) if ta else ((tm, tk), lambda i, j, kk: (i, kk))
        b_blk = ((tn, tk), lambda i, j, kk: (j, kk)) if tb else ((tk, tn), lambda i, j, kk: (kk, j))
        mn_spec = pl.BlockSpec((tm, tn), lambda i, j, kk: (i, j))

    def operand_spec(blk, plane):
        shape, imap = blk
        if plane is None:
            return pl.BlockSpec(shape, imap)
        return pl.BlockSpec((None,) + shape, lambda i, j, kk: (plane,) + imap(i, j, kk))

    ab_specs = [operand_spec(a_blk if q % 2 == 0 else b_blk, p) for q, p in enumerate(planes)]
    mn_arrays = [e[0] if isinstance(e, tuple) else e for e in mn]
    mn_specs = [pl.BlockSpec((tm, tn), lambda i, j, kk, c=e[1]: (i, c)) if isinstance(e, tuple) else mn_spec
                for e in mn]
    nk = grid[2]
    row_spec = pl.BlockSpec((1, tn), lambda i, j, kk: (0, j))
    n_ex, n_out = len(mn) + len(rows), len(outs)
    assert n_sums == 0 or (grid[1] == 1 and not bd)
    dims = (((0 if ta else 1,), (1 if tb else 0,)), ((), ()))

    def body(*refs):
        ab_refs, rest = refs[:len(ab)], refs[len(ab):]
        ex, o_refs, acc = rest[:n_ex], rest[n_ex:n_ex + n_out], rest[-1]
        s_refs = rest[n_ex + n_out:n_ex + n_out + n_sums]
        first_row_tile = pl.program_id(0) == 0
        kk = pl.program_id(2)

        @pl.when(kk == 0)
        def _():
            acc[...] = jnp.zeros_like(acc)

        for a_ref, b_ref in zip(ab_refs[0::2], ab_refs[1::2]):
            acc[...] += lax.dot_general(a_ref[...].astype(_MXU), b_ref[...].astype(_MXU), dims,
                                        preferred_element_type=F32)

        @pl.when(kk == nk - 1)
        def _():
            vals = (acc[...],) if epi is None else epi(acc[...], *[r[...] for r in ex])
            for r, v in zip(o_refs, vals):
                r[...] = v.astype(r.dtype)
            for r, v in zip(s_refs, vals[n_out:]):
                r[...] = jnp.where(first_row_tile, v, r[...] + v)

    res = pl.pallas_call(
        body, grid=grid,
        in_specs=ab_specs + mn_specs + [row_spec] * len(rows),
        out_specs=[mn_spec] * n_out + [row_spec] * n_sums,
        out_shape=[jax.ShapeDtypeStruct((m, out_n), dt) for dt in outs]
        + [jax.ShapeDtypeStruct((1, out_n), F32)] * n_sums,
        scratch_shapes=[pltpu.VMEM((tm, tn), F32)],
        compiler_params=_params(("arbitrary" if n_sums else "parallel", "parallel", "arbitrary")), name=name,
    )(*ab, *mn_arrays, *rows)
    return res[0] if n_out + n_sums == 1 else res


def _ew(fn, rows, bcs, out_rows, out_accs, *, name, tm=256):
    r = rows[0].shape[0]
    tm = min(tm, r)
    assert r % tm == 0
    nr, nb, no, na = len(rows), len(bcs), len(out_rows), len(out_accs)

    def body(*refs):
        i = pl.program_id(0)
        r_in, b_in = refs[:nr], refs[nr:nr + nb]
        o_r, o_a = refs[nr + nb:nr + nb + no], refs[nr + nb + no:]
        outs, accs = fn([x[...] for x in r_in], [x[...] for x in b_in])
        for ref, v in zip(o_r, outs):
            ref[...] = v.astype(ref.dtype)
        if na:
            @pl.when(i == 0)
            def _():
                for ref in o_a:
                    ref[...] = jnp.zeros_like(ref)

            for ref, v in zip(o_a, accs):
                ref[...] += v

    res = pl.pallas_call(
        body, grid=(r // tm,),
        in_specs=[pl.BlockSpec((tm, x.shape[1]), lambda i: (i, 0)) for x in rows]
        + [pl.BlockSpec((1, x.shape[1]), lambda i: (0, 0)) for x in bcs],
        out_specs=[pl.BlockSpec((tm, c), lambda i: (i, 0)) for c, _ in out_rows]
        + [pl.BlockSpec((1, c), lambda i: (0, 0)) for c in out_accs],
        out_shape=[jax.ShapeDtypeStruct((r, c), dt) for c, dt in out_rows]
        + [jax.ShapeDtypeStruct((1, c), F32) for c in out_accs],
        compiler_params=_params(("arbitrary",)), name=name,
    )(*rows, *bcs)
    return res


def _colsum(x):
    return jnp.sum(x, axis=0, keepdims=True)


def _sigmoid(x):
    return 1.0 / (1.0 + jnp.exp(-x))


def _rms_bwd_tile(xv, dv, g):
    rs = lax.rsqrt(jnp.mean(xv * xv, axis=-1, keepdims=True) + RMS_EPS)
    gd = dv * g
    dx = rs * gd - xv * (rs * rs * rs) * jnp.mean(gd * xv, axis=-1, keepdims=True)
    return dx, _colsum(dv * xv * rs)


def _rms_fwd(x, g, name):
    def fn(r, b):
        xv = r[0]
        rs = lax.rsqrt(jnp.mean(xv * xv, axis=-1, keepdims=True) + RMS_EPS)
        return [xv * rs * b[0]], []
    return _ew(fn, [x], [g], [(x.shape[1], BF16)], [], name=name)[0]


def _rms_bwd(x, dn, res, g, name):
    def fn(r, b):
        dx, dg = _rms_bwd_tile(r[0], r[1], b[0])
        if res is not None:
            dx = dx + r[2]
        return [dx], [dg]
    rows = [x, dn] + ([res] if res is not None else [])
    return _ew(fn, rows, [g], [(x.shape[1], F32)], [x.shape[1]], name=name)


def _scan_order(x):
    l, c = x.shape
    return x.reshape(l // (SUBLANES * SCAN_SEG), SUBLANES, SCAN_SEG, c).transpose(0, 2, 1, 3).reshape(l, c)


def _time_order(x):
    l, c = x.shape
    return x.reshape(l // (SUBLANES * SCAN_SEG), SCAN_SEG, SUBLANES, c).transpose(0, 2, 1, 3).reshape(l, c)


def _ssm_scan(x, w_re, w_im, a_pair, *, reverse, s_fwd=None, u=None, name):
    l = x.shape[0]
    seg, w = SCAN_SEG, SCAN_W
    bd_w = SSM_W // SSM_BD
    tiles_per_bd = SSM_S // SSM_BD // w
    nch = min(SCAN_CHAINS, l // (SUBLANES * seg))
    chain_rows = SUBLANES * seg
    tb = nch * chain_rows
    nt = l // tb
    with_da = s_fwd is not None
    assert reverse or not with_da

    def tt(t):
        return nt - 1 - t if reverse else t

    def body(*refs):
        if with_da:
            (x_ref, wr_ref, wi_ref, a_ref, sf_ref, sp_ref, u_ref, s_ref, da_ref, dw_ref, dx_ref,
             p_ref, c_ref, b_scr) = refs
        else:
            x_ref, wr_ref, wi_ref, a_ref, s_ref, p_ref, c_ref, b_scr = refs
        t_blk = pl.program_id(1)
        ar, ai = a_ref[0], a_ref[1]

        @pl.when(t_blk == 0)
        def _():
            def pstep(i, carry):
                pr, pi = carry
                p_ref[0, pl.ds(i, 1), :] = pr
                p_ref[1, pl.ds(i, 1), :] = pi
                return pr * ar - pi * ai, pr * ai + pi * ar

            lax.fori_loop(0, seg, pstep, (ar, ai))
            c_ref[...] = jnp.zeros_like(c_ref)
            if with_da:
                da_ref[...] = jnp.zeros_like(da_ref)
                dw_ref[...] = jnp.zeros_like(dw_ref)
                dx_ref[...] = jnp.zeros_like(dx_ref)

        xb = x_ref[...].astype(_MXU)
        b_scr[:, :w] = jnp.dot(xb, wr_ref[...], preferred_element_type=F32)
        b_scr[:, w:] = jnp.dot(xb, wi_ref[...], preferred_element_type=F32)
        arb, aib = jnp.broadcast_to(ar, (SUBLANES, w)), jnp.broadcast_to(ai, (SUBLANES, w))
        zero = jnp.zeros((SUBLANES, w), F32)

        def tile(g, step):
            return pl.ds(pl.multiple_of(g * chain_rows + step * SUBLANES, SUBLANES), SUBLANES)

        def rows(g, i):
            return tile(g, seg - 1 - i if reverse else i)

        def local_step(i, carry):
            out = []
            for g in range(nch):
                sr, si = carry[2 * g], carry[2 * g + 1]
                idx = rows(g, i)
                sr, si = arb * sr - aib * si + b_scr[idx, :w], arb * si + aib * sr + b_scr[idx, w:]
                b_scr[idx, :w] = sr
                b_scr[idx, w:] = si
                out += [sr, si]
            return tuple(out)

        def unrolled(step_fn, first):
            def trip(q, carry):
                for r in range(SCAN_UNROLL):
                    carry = step_fn(first + q * SCAN_UNROLL + r, carry)
                return carry
            return trip

        ends = lax.fori_loop(0, seg // SCAN_UNROLL, unrolled(local_step, 0), (zero,) * (2 * nch))

        a_seg_r, a_seg_i = p_ref[0, seg - 1:seg, :], p_ref[1, seg - 1:seg, :]
        cr, ci = c_ref[0], c_ref[1]
        sub = lax.broadcasted_iota(jnp.int32, (SUBLANES, w), 0)
        ins = [[zero, zero] for _ in range(nch)]
        order = [(g, k) for g in range(nch) for k in range(SUBLANES)]
        for g, k in (order[::-1] if reverse else order):
            ins[g] = [jnp.where(sub == k, cr, ins[g][0]), jnp.where(sub == k, ci, ins[g][1])]
            er, ei = ends[2 * g][k:k + 1], ends[2 * g + 1][k:k + 1]
            cr, ci = er + a_seg_r * cr - a_seg_i * ci, ei + a_seg_r * ci + a_seg_i * cr
        c_ref[0] = cr
        c_ref[1] = ci

        def fix(g, i):
            idx = rows(g, i)
            pr, pi = p_ref[0, pl.ds(i, 1), :], p_ref[1, pl.ds(i, 1), :]
            sr = b_scr[idx, :w] + pr * ins[g][0] - pi * ins[g][1]
            si = b_scr[idx, w:] + pr * ins[g][1] + pi * ins[g][0]
            s_ref.at[0][idx, :] = sr.astype(s_ref.dtype)
            s_ref.at[1][idx, :] = si.astype(s_ref.dtype)
            return sr, si

        if not with_da:
            def fix_step(i, carry):
                for g in range(nch):
                    fix(g, i)
                return carry

            lax.fori_loop(0, seg // SCAN_UNROLL, unrolled(fix_step, 0), 0)
        else:
            def adj_step(i, acc):
                acc_r, acc_i = acc
                for g in range(nch):
                    lr, li = fix(g, i)
                    prev = tile(g, seg - 2 - i)
                    fr, fi = sf_ref.at[0][prev, :].astype(F32), sf_ref.at[1][prev, :].astype(F32)
                    acc_r, acc_i = acc_r + lr * fr + li * fi, acc_i + li * fr - lr * fi
                return acc_r, acc_i

            acc = lax.fori_loop(0, seg // SCAN_UNROLL - 1, unrolled(adj_step, 0), (zero, zero))
            for i in range(seg - SCAN_UNROLL, seg - 1):
                acc = adj_step(i, acc)
            acc_r, acc_i = acc
            first_block = tt(t_blk) == 0
            for g in range(nch):
                lr, li = fix(g, seg - 1)
                seg_ends = tile(g, seg - 1)
                if g == 0:
                    pvr = jnp.where(first_block, 0.0, sp_ref[0, SUBLANES - 1:SUBLANES, :].astype(F32))
                    pvi = jnp.where(first_block, 0.0, sp_ref[1, SUBLANES - 1:SUBLANES, :].astype(F32))
                else:
                    pvr = sf_ref[0, g * chain_rows - 1:g * chain_rows, :].astype(F32)
                    pvi = sf_ref[1, g * chain_rows - 1:g * chain_rows, :].astype(F32)
                fr = jnp.where(sub == 0, pvr, pltpu.roll(sf_ref.at[0][seg_ends, :].astype(F32), 1, 0))
                fi = jnp.where(sub == 0, pvi, pltpu.roll(sf_ref.at[1][seg_ends, :].astype(F32), 1, 0))
                acc_r = acc_r + lr * fr + li * fi
                acc_i = acc_i + li * fr - lr * fi
            da_ref[0] += jnp.sum(acc_r, axis=0, keepdims=True)
            da_ref[1] += jnp.sum(acc_i, axis=0, keepdims=True)
            for plane in range(2):
                dw_ref[plane] += _tn_dot(u_ref[...], s_ref[plane])
                dx_ref[plane] += _tn_dot(xb, sf_ref[plane])

    x_spec = pl.BlockSpec((tb, bd_w), lambda j, t: (tt(t), j // tiles_per_bd))
    w_spec = pl.BlockSpec((bd_w, w), lambda j, t: (j // tiles_per_bd, j))
    d_spec = pl.BlockSpec((2, bd_w, w), lambda j, t: (0, j // tiles_per_bd, j % tiles_per_bd))
    a_spec = pl.BlockSpec((2, 1, w), lambda j, t: (0, 0, j))
    s_spec = pl.BlockSpec((2, tb, w), lambda j, t: (0, tt(t), j))
    in_specs, args = [x_spec, w_spec, w_spec, a_spec], [x, w_re, w_im, a_pair]
    out_specs, out_shape = [s_spec], [jax.ShapeDtypeStruct((2, l, SSM_S), BF16)]
    scratch = [pltpu.VMEM((2, seg, w), F32), pltpu.VMEM((2, 1, w), F32), pltpu.VMEM((tb, 2 * w), F32)]
    if with_da:
        in_specs += [s_spec, pl.BlockSpec((2, SUBLANES, w),
                                          lambda j, t: (0, jnp.maximum(tt(t) * (tb // SUBLANES) - 1, 0), j)),
                     x_spec]
        args += [s_fwd, s_fwd, u]
        out_specs += [a_spec, d_spec, d_spec]
        out_shape += ([jax.ShapeDtypeStruct((2, 1, SSM_S), F32)]
                      + [jax.ShapeDtypeStruct((2, SSM_W, SSM_S // SSM_BD), F32)] * 2)
    res = pl.pallas_call(
        body, grid=(SSM_S // w, nt), in_specs=in_specs, out_specs=out_specs, out_shape=out_shape,
        scratch_shapes=scratch, compiler_params=_params(("parallel", "arbitrary")), name=name,
    )(*args)
    return res if with_da else res[0]


def _nt_dot(x, y):
    return lax.dot_general(x.astype(_MXU), y.astype(_MXU), (((1,), (1,)), ((), ())), preferred_element_type=F32)


def _tn_dot(x, y):
    return lax.dot_general(x.astype(_MXU), y.astype(_MXU), (((0,), (0,)), ((), ())), preferred_element_type=F32)


def _nn_dot(x, y):
    return jnp.dot(x.astype(_MXU), y.astype(_MXU), preferred_element_type=F32)


def _attn_mask2(gb, nb):
    qi = lax.broadcasted_iota(jnp.int32, (ATT_WIN, 2 * ATT_WIN), 0)
    c = lax.broadcasted_iota(jnp.int32, (ATT_WIN, 2 * ATT_WIN), 1)
    has_prev = (gb % nb) != 0
    prev_ok = jnp.logical_and(jnp.logical_and(c < ATT_WIN, c >= qi), has_prev)
    own_ok = jnp.logical_and(c >= ATT_WIN, c - ATT_WIN <= qi)
    return jnp.logical_or(prev_ok, own_ok)


def _attn_specs(qb):
    cur = pl.BlockSpec((qb * ATT_WIN, ATT_GW), lambda i: (i, 0))
    prev = pl.BlockSpec((ATT_WIN, ATT_GW), lambda i: (jnp.maximum(qb * i - 1, 0), 0))
    return cur, prev


def _attn_fwd(q, k, v, nb, name):
    l = q.shape[0]
    scale = ATT_E ** -0.5
    w = ATT_WIN

    qb = ATT_QB_FWD

    def body(q_ref, kc_ref, kp_ref, vc_ref, vp_ref, o_ref, lse_ref):
        i = pl.program_id(0)
        masks = [_attn_mask2(qb * i + b, nb) for b in range(qb)]
        for h in range(ATT_HG):
            sl = slice(h * ATT_E, (h + 1) * ATT_E)
            k_ext = jnp.concatenate([kp_ref[:, sl], kc_ref[:, sl]], axis=0)
            v_ext = jnp.concatenate([vp_ref[:, sl], vc_ref[:, sl]], axis=0)
            for b in range(qb):
                r, kr = slice(b * w, (b + 1) * w), slice(b * w, (b + 2) * w)
                s = jnp.where(masks[b], _nt_dot(q_ref[r, sl], k_ext[kr]) * scale, NEG_INF)
                mx = jnp.max(s, axis=-1, keepdims=True)
                p = jnp.exp(s - mx)
                den = jnp.sum(p, axis=-1, keepdims=True)
                o_ref[r, sl] = _nn_dot(p, v_ext[kr]) / den
                lse_ref[r, sl] = jnp.broadcast_to(mx + jnp.log(den), (w, ATT_E))

    cur, prev = _attn_specs(qb)
    return pl.pallas_call(
        body, grid=(l // (qb * w),), in_specs=[cur, cur, prev, cur, prev], out_specs=[cur, cur],
        out_shape=[jax.ShapeDtypeStruct((l, ATT_GW), F32)] * 2,
        compiler_params=_params(("parallel",)), name=name,
    )(q, k, k, v, v)


def _attn_bwd(q, k, v, do, lse, dd, nb, name):
    l = q.shape[0]
    scale = ATT_E ** -0.5
    w = ATT_WIN
    nblk = l // w

    def body(q_ref, kc_ref, kp_ref, vc_ref, vp_ref, do_ref, lse_ref, dd_ref, qn_ref, don_ref, lsen_ref, ddn_ref,
             dq_ref, dk_ref, dv_ref, dk_acc, dv_acc):
        i = pl.program_id(0)
        masks = [_attn_mask2(ATT_QB * i + b, nb) for b in range(ATT_QB)]
        nxt = ATT_QB * (i + 1)
        nxt_attends = jnp.logical_and(nxt < nblk, (nxt % nb) != 0)
        qi = lax.broadcasted_iota(jnp.int32, (w, w), 0)
        kj = lax.broadcasted_iota(jnp.int32, (w, w), 1)
        mask_n = jnp.logical_and(kj >= qi, nxt_attends)
        dk_acc[...] = jnp.zeros_like(dk_acc)
        dv_acc[...] = jnp.zeros_like(dv_acc)
        for h in range(ATT_HG):
            sl, col = slice(h * ATT_E, (h + 1) * ATT_E), slice(h * ATT_E, h * ATT_E + 1)
            k_ext = jnp.concatenate([kp_ref[:, sl], kc_ref[:, sl]], axis=0)
            v_ext = jnp.concatenate([vp_ref[:, sl], vc_ref[:, sl]], axis=0)
            for b in range(ATT_QB):
                r, kr = slice(b * w, (b + 1) * w), slice(b * w, (b + 2) * w)
                qh, doh, k2, v2 = q_ref[r, sl], do_ref[r, sl], k_ext[kr], v_ext[kr]
                p = jnp.where(masks[b], jnp.exp(_nt_dot(qh, k2) * scale - lse_ref[r, col]), 0.0)
                ds = p * (_nt_dot(doh, v2) - dd_ref[r, col]) * scale
                dq_ref[r, sl] = _nn_dot(ds, k2).astype(dq_ref.dtype)
                dk2, dv2 = _tn_dot(ds, qh), _tn_dot(p, doh)
                dk_acc[r, sl] += dk2[w:]
                dv_acc[r, sl] += dv2[w:]
                if b > 0:
                    rp = slice((b - 1) * w, b * w)
                    dk_acc[rp, sl] += dk2[:w]
                    dv_acc[rp, sl] += dv2[:w]
            last = slice((ATT_QB - 1) * w, ATT_QB * w)
            kl, vl, qn, don = kc_ref[last, sl], vc_ref[last, sl], qn_ref[:, sl], don_ref[:, sl]
            pn = jnp.where(mask_n, jnp.exp(_nt_dot(qn, kl) * scale - lsen_ref[:, col]), 0.0)
            dsn = pn * (_nt_dot(don, vl) - ddn_ref[:, col]) * scale
            dk_acc[last, sl] += _tn_dot(dsn, qn)
            dv_acc[last, sl] += _tn_dot(pn, don)
        dk_ref[...] = dk_acc[...].astype(dk_ref.dtype)
        dv_ref[...] = dv_acc[...].astype(dv_ref.dtype)

    cur, prev = _attn_specs(ATT_QB)
    nxt_spec = pl.BlockSpec((w, ATT_GW), lambda i: (jnp.minimum(ATT_QB * (i + 1), nblk - 1), 0))
    return pl.pallas_call(
        body, grid=(l // (ATT_QB * w),),
        in_specs=[cur, cur, prev, cur, prev, cur, cur, cur, nxt_spec, nxt_spec, nxt_spec, nxt_spec],
        out_specs=[cur] * 3, out_shape=[jax.ShapeDtypeStruct((l, ATT_GW), BF16)] * 3,
        scratch_shapes=[pltpu.VMEM((ATT_QB * w, ATT_GW), F32)] * 2,
        compiler_params=_params(("parallel",)), name=name,
    )(q, k, k, v, v, do, lse, dd, q, do, lse, dd)


def _to_perm(a, d):
    if d == 1:
        return a
    l, c = a.shape
    return a.reshape(l // d, d, c).transpose(1, 0, 2).reshape(l, c)


def _from_perm(a, d):
    if d == 1:
        return a
    l, c = a.shape
    return a.reshape(d, l // d, c).transpose(1, 0, 2).reshape(l, c)


def _mem_probs(qh, kh):
    s = _nt_dot(qh, kh) * (MEM_E ** -0.5)
    e = jnp.exp(s - jnp.max(s, axis=-1, keepdims=True))
    return e / jnp.sum(e, axis=-1, keepdims=True)


def _mem_fwd(mq, kv, name, tm=512):
    l, nm = mq.shape[0], kv.shape[0]

    def body(q_ref, kv_ref, o_ref):
        for h in range(MEM_H):
            sl = slice(h * MEM_E, (h + 1) * MEM_E)
            p = _mem_probs(q_ref[:, sl], kv_ref[:, sl])
            o_ref[:, sl] = _nn_dot(p, kv_ref[:, MEM_W + h * MEM_E:MEM_W + (h + 1) * MEM_E]).astype(o_ref.dtype)

    return pl.pallas_call(
        body, grid=(l // tm,),
        in_specs=[pl.BlockSpec((tm, MEM_W), lambda i: (i, 0)), pl.BlockSpec((nm, 2 * MEM_W), lambda i: (0, 0))],
        out_specs=pl.BlockSpec((tm, MEM_W), lambda i: (i, 0)),
        out_shape=jax.ShapeDtypeStruct((l, MEM_W), BF16),
        compiler_params=_params(("parallel",)), name=name,
    )(mq, kv)


def _mem_bwd(mq, kv, dmo, name, tm=512):
    l, nm = mq.shape[0], kv.shape[0]
    scale = MEM_E ** -0.5

    def body(q_ref, kv_ref, do_ref, dq_ref, dkv_ref):
        @pl.when(pl.program_id(0) == 0)
        def _():
            dkv_ref[...] = jnp.zeros_like(dkv_ref)

        for h in range(MEM_H):
            sl = slice(h * MEM_E, (h + 1) * MEM_E)
            vsl = slice(MEM_W + h * MEM_E, MEM_W + (h + 1) * MEM_E)
            qh, kh, vh, doh = q_ref[:, sl], kv_ref[:, sl], kv_ref[:, vsl], do_ref[:, sl]
            p = _mem_probs(qh, kh)
            dp = _nt_dot(doh, vh)
            ds = p * (dp - jnp.sum(dp * p, axis=-1, keepdims=True)) * scale
            dq_ref[:, sl] = _nn_dot(ds, kh).astype(dq_ref.dtype)
            dkv_ref[:, sl] += _tn_dot(ds, qh)
            dkv_ref[:, vsl] += _tn_dot(p, doh)

    row = pl.BlockSpec((tm, MEM_W), lambda i: (i, 0))
    full = pl.BlockSpec((nm, 2 * MEM_W), lambda i: (0, 0))
    return pl.pallas_call(
        body, grid=(l // tm,), in_specs=[row, full, row], out_specs=[row, full],
        out_shape=[jax.ShapeDtypeStruct((l, MEM_W), BF16), jax.ShapeDtypeStruct((nm, 2 * MEM_W), F32)],
        compiler_params=_params(("arbitrary",)), name=name,
    )(mq, kv, dmo)


def _gated_out_proj(zg, branches, b_gate, w_o, x, g2, name, tm=512):
    l, d = x.shape
    nbr = len(branches)

    def body(zg_ref, *rest):
        br_refs, (bg_ref, w_ref, x_ref, g2_ref, m_ref, h_ref, n_ref) = rest[:nbr], rest[nbr:]
        merged = jnp.zeros((tm, d), F32)
        for i, br_ref in enumerate(br_refs):
            cols = slice(i * d, (i + 1) * d)
            merged += _sigmoid(zg_ref[:, cols].astype(F32) + bg_ref[:, cols]) * br_ref[...].astype(F32)
        mb = merged.astype(BF16)
        m_ref[...] = mb
        hv = jnp.dot(mb.astype(_MXU), w_ref[...].astype(_MXU), preferred_element_type=F32) + x_ref[...]
        h_ref[...] = hv
        rs = lax.rsqrt(jnp.mean(hv * hv, axis=-1, keepdims=True) + RMS_EPS)
        n_ref[...] = (hv * rs * g2_ref[...]).astype(n_ref.dtype)

    row = lambda c: pl.BlockSpec((tm, c), lambda i: (i, 0))
    full = lambda a: pl.BlockSpec(a.shape, lambda i: (0, 0))
    return pl.pallas_call(
        body, grid=(l // tm,),
        in_specs=[row(nbr * d)] + [row(d)] * nbr + [full(b_gate), full(w_o), row(d), full(g2)],
        out_specs=[row(d)] * 3,
        out_shape=[jax.ShapeDtypeStruct((l, d), BF16), jax.ShapeDtypeStruct((l, d), F32),
                   jax.ShapeDtypeStruct((l, d), BF16)],
        compiler_params=_params(("parallel",)), name=name,
    )(zg, *branches, b_gate, w_o, x, g2)


def _discretize(lam_re, lam_im, log_dt, b_re, b_im):
    dt = jnp.exp(log_dt)[:, None]
    mag = jnp.exp(lam_re * dt)
    a_re, a_im = mag * jnp.cos(lam_im * dt), mag * jnp.sin(lam_im * dt)
    nr, ni = a_re - 1.0, a_im
    den = lam_re * lam_re + lam_im * lam_im
    coef_re = (nr * lam_re + ni * lam_im) / den
    coef_im = (ni * lam_re - nr * lam_im) / den
    bb_re = coef_re[..., None] * b_re - coef_im[..., None] * b_im
    bb_im = coef_re[..., None] * b_im + coef_im[..., None] * b_re
    return a_re, a_im, bb_re, bb_im


def _bd_in(bb):
    return jnp.einsum("gph,gk->ghkp", bb, jnp.eye(SSM_G, dtype=bb.dtype)).reshape(SSM_W, SSM_S)


def _bd_diag(x):
    gb = SSM_G // SSM_BD
    t = x.reshape(SSM_BD, gb, SSM_H, gb, SSM_P)
    return jnp.einsum("bghgp->bghp", t).reshape(SSM_G, SSM_H, SSM_P)


_ANY = pl.BlockSpec(memory_space=pl.ANY)
_MESH = pl.DeviceIdType.MESH


def _allgather(x, name):
    def body(x_ref, out_ref, send_sems, recv_sems, local_sem):
        mx, my, mc = lax.axis_index("x"), lax.axis_index("y"), lax.axis_index("c")
        me, sibling = (mx, my, mc), (mx, my, 1 - mc)
        chips = [(1 - mx, my), (mx, 1 - my), (1 - mx, 1 - my)]

        def blk(px, py, pc):
            return out_ref.at[4 * px + 2 * py + pc]

        def copy(k, block, to, src=None):
            return pltpu.make_async_remote_copy(
                src_ref=blk(*block) if src is None else src, dst_ref=blk(*block),
                send_sem=send_sems.at[k], recv_sem=recv_sems.at[k], device_id=to, device_id_type=_MESH)

        mine = pltpu.make_async_copy(x_ref, blk(*me), local_sem)
        mine.start()
        first = [copy(0, me, sibling, src=x_ref)]
        first += [copy(1 + j, me, (*chip, mc), src=x_ref) for j, chip in enumerate(chips)]
        for cp in first:
            cp.start()
        passed = [copy(4 + j, (*chip, mc), sibling) for j, chip in enumerate(chips)]
        for j, chip in enumerate(chips):
            copy(1 + j, (*chip, mc), me).wait_recv()
            passed[j].start()
        copy(0, sibling, me).wait_recv()
        for j, chip in enumerate(chips):
            copy(4 + j, (*chip, 1 - mc), me).wait_recv()
        for cp in first + passed:
            cp.wait_send()
        mine.wait()

    return pl.pallas_call(
        body, out_shape=jax.ShapeDtypeStruct((N_DEV,) + x.shape, x.dtype), in_specs=[_ANY], out_specs=_ANY,
        scratch_shapes=[pltpu.SemaphoreType.DMA((7,)), pltpu.SemaphoreType.DMA((7,)), pltpu.SemaphoreType.DMA],
        name=name,
    )(x)


def _pair_exchange(g, name):
    def body(g_ref, out_ref, send_sems, recv_sems):
        mx, my, mc = lax.axis_index("x"), lax.axis_index("y"), lax.axis_index("c")
        copies = [pltpu.make_async_remote_copy(
            src_ref=g_ref.at[2 * k + (1 - mc)], dst_ref=out_ref.at[k], send_sem=send_sems.at[k],
            recv_sem=recv_sems.at[k], device_id=(mx, my, 1 - mc), device_id_type=_MESH) for k in range(4)]
        for cp in copies:
            cp.start()
        for cp in copies:
            cp.wait()

    return pl.pallas_call(
        body, out_shape=jax.ShapeDtypeStruct((4,) + g.shape[1:], g.dtype), in_specs=[_ANY], out_specs=_ANY,
        scratch_shapes=[pltpu.SemaphoreType.DMA((4,)), pltpu.SemaphoreType.DMA((4,))], name=name,
    )(g)


_HBM = pl.BlockSpec(memory_space=pltpu.HBM)
_SEM = pl.BlockSpec(memory_space=pltpu.SEMAPHORE)
_EFFECT = pltpu.SideEffectType.DATAFLOW_SIDE_EFFECTING
_TOKEN = jax.ShapeDtypeStruct((8, 128), F32)


def _peer(rel):
    pos = (lax.axis_index("x"), lax.axis_index("y"), lax.axis_index("c"))
    return tuple(1 - p if (rel >> (2 - i)) & 1 else p for i, p in enumerate(pos))


def _index_of(dev):
    return 4 * dev[0] + 2 * dev[1] + dev[2]


def _split_copies(src_ref, land_ref, sems, plan):
    n = len(plan)
    return [pltpu.make_async_remote_copy(
        src_ref=src_ref if s is None else src_ref.at[s], dst_ref=land_ref.at[d], send_sem=sems[k],
        recv_sem=sems[n + k], device_id=peer, device_id_type=_MESH) for k, (s, d, peer) in enumerate(plan)]


def _split_start(src, n_land, plan_fn, after, name):
    blk = src.shape[-2:]
    land = lax.empty((n_land,) + blk, src.dtype)
    n = len(plan_fn())

    def body(src_ref, land_ref, after_ref, *outs):
        for cp in _split_copies(src_ref, land_ref, outs[:2 * n], plan_fn()):
            cp.start()
        outs[2 * n + 2][...] = jnp.zeros_like(outs[2 * n + 2])

    res = pl.pallas_call(
        body, name=name,
        out_shape=(pltpu.SemaphoreType.DMA(()),) * (2 * n)
        + (pltpu.HBM(src.shape, src.dtype), pltpu.HBM(land.shape, land.dtype), _TOKEN),
        in_specs=(_HBM, _HBM, _ANY),
        out_specs=(_SEM,) * (2 * n) + (_HBM, _HBM, pl.BlockSpec(memory_space=pltpu.VMEM)),
        input_output_aliases={0: 2 * n, 1: 2 * n + 1},
        compiler_params=pltpu.CompilerParams(has_side_effects=_EFFECT),
    )(pltpu.with_memory_space_constraint(src, pltpu.HBM), pltpu.with_memory_space_constraint(land, pltpu.HBM), after)
    return res[:2 * n], res[2 * n], res[2 * n + 1], res[2 * n + 2]


def _split_wait(sems, src, land, plan_fn, after, name):
    n = len(sems) // 2

    def body(src_ref, land_ref, *rest):
        for cp in _split_copies(src_ref, land_ref, rest[:2 * n], plan_fn()):
            cp.wait_send()
            cp.wait_recv()

    return pl.pallas_call(
        body, name=name,
        out_shape=(pltpu.HBM(src.shape, src.dtype), pltpu.HBM(land.shape, land.dtype)),
        in_specs=(_HBM, _HBM) + (_SEM,) * (2 * n) + (_ANY,), out_specs=(_HBM, _HBM),
        input_output_aliases={0: 0, 1: 1},
        compiler_params=pltpu.CompilerParams(has_side_effects=_EFFECT),
    )(src, land, *sems, after)


def _gather_plan():
    me = _index_of(_peer(0))
    return [(None, me, _peer(rel)) for rel in range(1, N_DEV)]


def _gather_wait_plan():
    return [(None, _index_of(_peer(rel)), _peer(rel)) for rel in range(1, N_DEV)]


def _chip_plan():
    return [(_index_of(_peer(rel)) // 2, j, _peer(rel)) for j, rel in enumerate((4, 2, 6))]


def _pair_sum(g, t1, my_c, name, tr):
    _, r, c = g.shape

    def body(c_ref, g_ref, t_ref, o_ref, ob_ref):
        s = g_ref[...] + t_ref[...]
        o_ref[...] = s
        ob_ref[...] = s.astype(BF16)

    blk = pl.BlockSpec((None, tr, c), lambda k, i, cr: (k, i, 0))
    return pl.pallas_call(
        body,
        grid_spec=pltpu.PrefetchScalarGridSpec(
            num_scalar_prefetch=1, grid=(4, r // tr),
            in_specs=[pl.BlockSpec((None, tr, c), lambda k, i, cr: (2 * k + cr[0], i, 0)), blk],
            out_specs=[blk, blk]),
        out_shape=[jax.ShapeDtypeStruct((4, r, c), F32), jax.ShapeDtypeStruct((4, r, c), BF16)],
        compiler_params=_params(("parallel", "parallel")), name=name,
    )(my_c, g, t1)


def _adam_math(g, w, m, v):
    m = ADAM_B1 * m + (1.0 - ADAM_B1) * g
    v = ADAM_B2 * v + (1.0 - ADAM_B2) * (g * g)
    m_hat = m / (1.0 - ADAM_B1 ** ADAM_STEP)
    v_hat = v / (1.0 - ADAM_B2 ** ADAM_STEP)
    delta = -ADAM_LR * (m_hat / (jnp.sqrt(v_hat) + ADAM_EPS) + ADAM_WD * w)
    return delta, m, v


def _grad_sum(p, t2, my_chip, name, tr):
    _, r, c = p.shape

    def body(k_ref, p_ref, t0_ref, t1_ref, t2_ref, g_out):
        g_out[...] = ((p_ref[...] + t0_ref[...].astype(F32)) + t1_ref[...].astype(F32)) + t2_ref[...].astype(F32)

    def rel(j):
        return pl.BlockSpec((None, tr, c), lambda i, kr: (j, i, 0))

    return pl.pallas_call(
        body,
        grid_spec=pltpu.PrefetchScalarGridSpec(
            num_scalar_prefetch=1, grid=(r // tr,),
            in_specs=[pl.BlockSpec((None, tr, c), lambda i, kr: (kr[0], i, 0)), rel(0), rel(1), rel(2)],
            out_specs=pl.BlockSpec((tr, c), lambda i, kr: (i, 0))),
        out_shape=jax.ShapeDtypeStruct((r, c), F32),
        compiler_params=_params(("parallel",)), name=name,
    )(my_chip, p, t2, t2, t2)


def _adam_many(g, w, m, v, row_tiles, name):
    n = len(g)

    def body(*refs):
        ins, outs = refs[:4 * n], refs[4 * n:]
        for i in range(n):
            res = _adam_math(ins[i][...], ins[n + i][...], ins[2 * n + i][...], ins[3 * n + i][...])
            for kind in range(3):
                outs[kind * n + i][...] = res[kind]

    def spec(a):
        blk = (a.shape[0] // row_tiles,) + a.shape[1:]
        return pl.BlockSpec(blk, lambda t, nd=a.ndim: (t,) + (0,) * (nd - 1))

    specs = [spec(a) for a in g]
    res = pl.pallas_call(
        body, grid=(row_tiles,), in_specs=specs * 4, out_specs=specs * 3,
        out_shape=[jax.ShapeDtypeStruct(a.shape, F32) for a in g] * 3,
        compiler_params=_params(("parallel",)), name=name,
    )(*g, *w, *m, *v)
    return res[:n], res[n:2 * n], res[2 * n:]


def _sum8(g8, name):
    _, r, c = g8.shape

    def body(g_ref, o_ref):
        acc = g_ref[0]
        for j in range(1, N_DEV):
            acc = acc + g_ref[j]
        o_ref[...] = acc

    return pl.pallas_call(
        body, grid=(1,), in_specs=[pl.BlockSpec((N_DEV, r, c), lambda i: (0, 0, 0))],
        out_specs=pl.BlockSpec((r, c), lambda i: (0, 0)), out_shape=jax.ShapeDtypeStruct((r, c), F32),
        compiler_params=_params(("arbitrary",)), name=name,
    )(g8)


def _pack(arrs, pad_rows=8):
    flat = jnp.concatenate([a.reshape(-1) for a in arrs])
    n = flat.shape[0]
    q = PACK_C * pad_rows
    tot = -(-n // q) * q
    if tot != n:
        flat = jnp.concatenate([flat, jnp.zeros((tot - n,), flat.dtype)])
    return flat.reshape(tot // PACK_C, PACK_C)


def _unpack(buf, shapes):
    flat = buf.reshape(-1)
    out, off = [], 0
    for s in shapes:
        n = int(np.prod(s))
        out.append(flat[off:off + n].reshape(s))
        off += n
    return out


GROUPS = (("w_in",),
          ("w_glu", "w_ssm_br", "w_mem_br", "w_attn_br"),
          ("w_up", "w_down"),
          ("w_mem_kv", "w_o"))
GROUP_TR = (400, 384, 512, 256)
MLP_GROUP = 2
MIXER_GROUPS = (1, 3)
ATTN_BR_FOLD = 2


def _stored_shape(name):
    r, c, ax = BIG_SHAPE[name]
    rows, cols = (r // N_DEV, c) if ax == 0 else (c // N_DEV, r)
    return (rows // ATTN_BR_FOLD, cols * ATTN_BR_FOLD) if name == "w_attn_br" else (rows, cols)


def _stored(shard, name):
    a = shard[0].T if BIG_SHAPE[name][2] == 1 else shard[0]
    return a.reshape(_stored_shape(name))


def _unstored(a, name):
    r, c, ax = BIG_SHAPE[name]
    if ax == 0:
        return a.reshape(1, r // N_DEV, c)
    return a.reshape(c // N_DEV, r).T[None]


def _pack_group(d, names):
    return jnp.concatenate([_stored(d[n], n) for n in names], axis=0)


def _split_group(buf, names):
    out, off = {}, 0
    for n in names:
        rows = _stored_shape(n)[0]
        out[n] = buf[..., off:off + rows, :]
        off += rows
    return out


def _full_stored(stacked, name):
    r, c, ax = BIG_SHAPE[name]
    return stacked.reshape((r, c) if ax == 0 else (c, r))


def _stacked_stored(full, name):
    return full.reshape((N_DEV,) + _stored_shape(name))


def _gelu_parts(x):
    c0, c1 = math.sqrt(2.0 / math.pi), 0.044715
    th = jnp.tanh(c0 * (x + c1 * x * x * x))
    return th, c0, c1


def _local_step(x, mem, tgt, wb, sp, late_weights, grads_ready):
    l = x.shape[0]
    w_a, w_g = wb["w_in"][:ZA_W], wb["w_in"][ZA_W:]

    a_re, a_im, bb_re, bb_im = _discretize(sp["ssm_lambda_re"], sp["ssm_lambda_im"], sp["ssm_log_dt"],
                                           sp["ssm_b_re"], sp["ssm_b_im"])
    a_pair = jnp.stack([a_re.reshape(1, SSM_S), a_im.reshape(1, SSM_S)])
    a_conj = jnp.stack([a_re.reshape(1, SSM_S), -a_im.reshape(1, SSM_S)])
    b_re_t, b_im_t = _bd_in(bb_re).astype(BF16), _bd_in(bb_im).astype(BF16)
    c_re_t = _bd_in(sp["ssm_c_re"].transpose(0, 2, 1)).astype(BF16)
    c_im_t = (-_bd_in(sp["ssm_c_im"].transpose(0, 2, 1))).astype(BF16)
    d_row = sp["ssm_d"].reshape(1, SSM_W)

    n1 = _rms_fwd(x, sp["norm1_g"], "rms1")
    za = _mm(n1, w_a, [BF16], tb=True, name="in_proj_a", tn=1664)
    zg = _mm(n1, w_g, [BF16], tb=True, name="in_proj_g")
    for gi in MIXER_GROUPS:
        wb = {**wb, **late_weights(gi, za)}
    u = za[:, :SSM_W]
    mq = za[:, ZA_W - MEM_W:]

    u_s = _scan_order(u)
    s_all = _ssm_scan(u_s, b_re_t, b_im_t, a_pair, reverse=False, name="ssm_scan_fwd")
    ys = _time_order(_mm((s_all, 0), c_re_t, [F32], tb=True, pair2=((s_all, 1), c_im_t), bd=SSM_BD, tm=2048, name="ssm_cs"))

    def gelu_fn(r, b):
        y0 = r[0] + b[0] * r[1].astype(F32)
        th, _, _ = _gelu_parts(y0)
        return [y0, 0.5 * y0 * (1.0 + th)], []
    y0, y1 = _ew(gelu_fn, [ys, u], [d_row], [(SSM_W, F32), (SSM_W, BF16)], [], name="ssm_gelu", tm=512)

    def glu_epi(acc, y1t, bg):
        t = acc + bg
        return t, y1t.astype(F32) * _sigmoid(t)
    t_glu, y2 = _mm(y1, wb["w_glu"], [F32, BF16], epi=glu_epi, mn=[y1], rows=[sp["b_glu"]], name="ssm_glu")
    br_ssm = _mm(y2, wb["w_ssm_br"], [BF16], tb=True, name="ssm_br")

    qkv_p, o_g, lse_g = [], [], []
    for g, d in enumerate(DILATIONS):
        nb = l // d // ATT_WIN
        cols = [za[:, SSM_W + (3 * j + g) * ATT_GW: SSM_W + (3 * j + g + 1) * ATT_GW] for j in range(3)]
        qp, kp, vp = [_to_perm(cc, d) for cc in cols]
        qkv_p.append((qp, kp, vp))
        og, lg = _attn_fwd(qp, kp, vp, nb, "attn_fwd%d" % g)
        o_g.append(_from_perm(og, d))
        lse_g.append(_from_perm(lg, d))

    def merge_fn(r, b):
        o0, o1, o2, l0, l1, l2 = r
        mx = jnp.maximum(jnp.maximum(l0, l1), l2)
        e0, e1, e2 = jnp.exp(l0 - mx), jnp.exp(l1 - mx), jnp.exp(l2 - mx)
        tot = e0 + e1 + e2
        return [(e0 * o0 + e1 * o1 + e2 * o2) / tot, mx + jnp.log(tot)], []
    o_att, lse_tot = _ew(merge_fn, o_g + lse_g, [], [(ATT_GW, F32), (ATT_GW, F32)], [], name="attn_merge", tm=512)
    br_attn = _mm(o_att, wb["w_attn_br"], [BF16], tb=True, name="attn_br")

    mn = _rms_fwd(mem, sp["mem_norm_g"], "rms_mem")
    kv = _mm(mn, wb["w_mem_kv"], [BF16], name="mem_kv")
    mo = _mem_fwd(mq, kv, "mem_attn_fwd")
    br_mem = _mm(mo, wb["w_mem_br"], [BF16], tb=True, name="mem_br")

    merged, h1, n2 = _gated_out_proj(zg, [br_ssm, br_attn, br_mem], sp["b_gate"], wb["w_o"], x, sp["norm2_g"],
                                     "gated_o_proj")

    def up_epi(acc):
        ra = jnp.maximum(acc, 0.0)
        return ra * ra, ra
    wm = late_weights(MLP_GROUP, n2)
    f_act, r_act = _mm(n2, wm["w_up"], [BF16, BF16], tb=True, epi=up_epi, name="mlp_up")
    def down_epi(acc, ht, tv, gf):
        hv = acc + ht
        rs = lax.rsqrt(jnp.mean(hv * hv, axis=-1, keepdims=True) + RMS_EPS)
        err = hv * rs * gf - tv
        dh, dgf = _rms_bwd_tile(hv, err * (1.0 / D_MODEL), gf)
        return dh, dgf, _colsum(err * err) * (0.5 / D_MODEL)
    dh2, d_final_g, loss_cols = _mm(f_act, wm["w_down"], [F32], epi=down_epi, mn=[h1, tgt], rows=[sp["final_g"]],
                                    n_sums=2, tk=1024, name="mlp_down")
    loss = jnp.sum(loss_cols, axis=1, keepdims=True)

    gw, gs = {}, {"final_g": d_final_g}
    d_act = _mm(dh2, wm["w_down"], [BF16], tb=True, epi=lambda acc, ra: (acc * 2.0 * ra.astype(F32),), mn=[r_act],
                name="mlp_down_dx")
    dw_down = _mm(f_act, dh2, [F32], ta=True, name="mlp_down_dw")
    dw_up = _mm(d_act, n2, [F32], ta=True, name="mlp_up_dw")
    token = grads_ready(MLP_GROUP, {"w_up": dw_up, "w_down": dw_down})
    def up_dx_epi(acc, ht, dht, g2):
        dx, dg = _rms_bwd_tile(ht, acc, g2)
        return dx + dht, dg
    dh1, gs["norm2_g"] = _mm(d_act, wm["w_up"], [F32], epi=up_dx_epi, mn=[h1, dh2],
                             rows=[sp["norm2_g"] + token[:1, :1]], n_sums=1, tk=1024, name="mlp_up_dx")
    gw["w_o"] = _mm(merged, dh1, [F32], ta=True, name="o_proj_dw")

    def gate_bwd_epi(dm, *tiles):
        dbr, dz = [], []
        for zt, bt, bias in zip(tiles[0:3], tiles[3:6], tiles[6:9]):
            gt = _sigmoid(zt.astype(F32) + bias)
            dbr.append(dm * gt)
            dz.append(dm * bt.astype(F32) * gt * (1.0 - gt))
        return (*dbr, *dz, *[_colsum(t) for t in dz])
    gate_bias = [sp["b_gate"][:, i * D_MODEL:(i + 1) * D_MODEL] for i in range(3)]
    res = _mm(dh1, wb["w_o"], [BF16] * 6, tb=True, epi=gate_bwd_epi, mn=[(zg, 0), (zg, 1), (zg, 2), br_ssm, br_attn, br_mem],
              rows=gate_bias, n_sums=3, tm=512, name="o_proj_dx")
    (dbr_ssm, dbr_attn, dbr_mem), dzg = res[0:3], res[3:6]
    gs["b_gate"] = jnp.concatenate(res[6:9], axis=1)

    gw["w_ssm_br"] = _mm(dbr_ssm, y2, [F32], ta=True, name="ssm_br_dw")
    dy2 = _mm(dbr_ssm, wb["w_ssm_br"], [F32], name="ssm_br_dx")

    def glu_bwd_fn(r, b):
        dy, y1t, tt = r
        sg = _sigmoid(tt)
        dt = dy * y1t.astype(F32) * sg * (1.0 - sg)
        return [dt, dy * sg], [_colsum(dt)]
    dt_glu, dy1a, gs["b_glu"] = _ew(glu_bwd_fn, [dy2, y1, t_glu], [], [(SSM_W, BF16), (SSM_W, F32)], [SSM_W],
                                    name="ssm_glu_bwd", tm=512)
    gw["w_glu"] = _mm(y1, dt_glu, [F32], ta=True, name="ssm_glu_dw")

    def gelu_bwd_epi(acc, dy1t, y0t):
        th, c0, c1 = _gelu_parts(y0t)
        dg = 0.5 * (1.0 + th) + 0.5 * y0t * (1.0 - th * th) * c0 * (1.0 + 3.0 * c1 * y0t * y0t)
        return ((acc + dy1t) * dg,)
    dy0 = _mm(dt_glu, wb["w_glu"], [F32], tb=True, epi=gelu_bwd_epi, mn=[dy1a, y0], name="ssm_glu_dx")
    gs["ssm_d"] = _ew(lambda r, b: ([], [_colsum(r[0] * r[1].astype(F32))]), [dy0, u], [], [], [SSM_W],
                      name="ssm_dd", tm=512)[0]
    dy0_s = _scan_order(dy0)
    lam, da, d_b, d_c = _ssm_scan(dy0_s, c_re_t, c_im_t, a_conj, reverse=True, s_fwd=s_all, u=u_s,
                                  name="ssm_scan_bwd")
    du = _time_order(_mm((lam, 0), b_re_t, [BF16], tb=True, pair2=((lam, 1), b_im_t),
                         epi=lambda acc, dyt, dr: (acc + dyt * dr,), mn=[dy0_s], rows=[d_row], bd=SSM_BD, tm=2048, name="ssm_bu_dx"))
    gs["a_re"], gs["a_im"] = da[0], da[1]
    gs["bb_re"], gs["bb_im"] = _bd_diag(d_b[0]).transpose(0, 2, 1), _bd_diag(d_b[1]).transpose(0, 2, 1)
    gs["ssm_c_re"], gs["ssm_c_im"] = _bd_diag(d_c[0]), -_bd_diag(d_c[1])

    gw["w_attn_br"] = _mm(dbr_attn, o_att, [F32], ta=True, name="attn_br_dw")

    def do_epi(acc, ot):
        prod = acc * ot
        head = lax.broadcasted_iota(jnp.int32, prod.shape, 1) // ATT_E
        dd = jnp.zeros_like(prod)
        for h in range(ATT_HG):
            dd = jnp.where(head == h, jnp.sum(jnp.where(head == h, prod, 0.0), axis=1, keepdims=True), dd)
        return acc, dd
    do_att, dd_att = _mm(dbr_attn, wb["w_attn_br"], [BF16, F32], epi=do_epi, mn=[o_att], name="attn_br_dx")
    dq_l, dk_l, dv_l = [], [], []
    for g, d in enumerate(DILATIONS):
        nb = l // d // ATT_WIN
        qp, kp, vp = qkv_p[g]
        dq, dk, dv = _attn_bwd(qp, kp, vp, _to_perm(do_att, d), _to_perm(lse_tot, d), _to_perm(dd_att, d),
                               nb, "attn_bwd%d" % g)
        dq_l.append(_from_perm(dq, d))
        dk_l.append(_from_perm(dk, d))
        dv_l.append(_from_perm(dv, d))

    gw["w_mem_br"] = _mm(dbr_mem, mo, [F32], ta=True, name="mem_br_dw")
    dmo = _mm(dbr_mem, wb["w_mem_br"], [BF16], name="mem_br_dx")
    dmq, dkv = _mem_bwd(mq, kv, dmo, "mem_attn_bwd")
    gw["w_mem_kv"] = _mm(mn, dkv, [F32], ta=True, name="mem_kv_dw")
    dmn = _mm(dkv, wb["w_mem_kv"], [F32], tb=True, name="mem_kv_dx")
    token = sum(grads_ready(gi, gw) for gi in MIXER_GROUPS)
    gs["mem_norm_g"] = _rms_bwd(mem, dmn, None, sp["mem_norm_g"] + token[:1, :1], "rms_mem_bwd")[1]

    dza = jnp.concatenate([du] + dq_l + dk_l + dv_l + [dmq], axis=1)
    dn_a = _mm(dza, w_a, [F32], name="in_proj_a_dx", tk=1664)
    dw_a = _mm(dza, n1, [F32], ta=True, name="in_proj_a_dw", tm=1664)
    dw_g = [_mm(dzg[i], n1, [F32], ta=True, name="in_proj_g_dw%d" % i) for i in range(3)]
    gw["w_in"] = jnp.concatenate([dw_a] + dw_g, axis=0)
    token = grads_ready(0, gw)
    def in_dx_epi(acc, pt, xt, dht, g1):
        dx, dg = _rms_bwd_tile(xt, acc + pt, g1)
        return dx + dht, dg
    w_gs = [w_g[i * D_MODEL:(i + 1) * D_MODEL] for i in range(3)]
    grad_x, gs["norm1_g"] = _mm(dzg[0], w_gs[0], [F32], pair2=(dzg[1], w_gs[1], dzg[2], w_gs[2]), epi=in_dx_epi,
                                mn=[dn_a, x, dh1],
                                rows=[sp["norm1_g"] + token[:1, :1]], n_sums=1, tm=512, name="in_proj_g_dx")
    return loss, grad_x, gs


_SMALL_GRAD_ORDER = ("norm1_g", "mem_norm_g", "b_gate", "a_re", "a_im", "bb_re", "bb_im", "ssm_c_re", "ssm_c_im",
                     "ssm_d", "b_glu", "norm2_g", "final_g")


def kernel(x, mem, norm1_g, mem_norm_g, w_in, b_gate, ssm_lambda_re, ssm_lambda_im, ssm_log_dt, ssm_b_re, ssm_b_im, ssm_c_re, ssm_c_im, ssm_d, w_glu, b_glu, w_ssm_br, w_attn_br, w_mem_kv, w_mem_br, w_o, norm2_g, w_up, w_down, final_g, loss_target, m_norm1_g, m_mem_norm_g, m_w_in, m_b_gate, m_ssm_lambda_re, m_ssm_lambda_im, m_ssm_log_dt, m_ssm_b_re, m_ssm_b_im, m_ssm_c_re, m_ssm_c_im, m_ssm_d, m_w_glu, m_b_glu, m_w_ssm_br, m_w_attn_br, m_w_mem_kv, m_w_mem_br, m_w_o, m_norm2_g, m_w_up, m_w_down, m_final_g, v_norm1_g, v_mem_norm_g, v_w_in, v_b_gate, v_ssm_lambda_re, v_ssm_lambda_im, v_ssm_log_dt, v_ssm_b_re, v_ssm_b_im, v_ssm_c_re, v_ssm_c_im, v_ssm_d, v_w_glu, v_b_glu, v_w_ssm_br, v_w_attn_br, v_w_mem_kv, v_w_mem_br, v_w_o, v_norm2_g, v_w_up, v_w_down, v_final_g):
    args = dict(locals())
    w = {n: args[n] for n in ALL_W}
    m = {n: args["m_" + n] for n in ALL_W}
    v = {n: args["v_" + n] for n in ALL_W}
    my_c = lax.axis_index("c").astype(jnp.int32).reshape(1)
    my_chip = (2 * lax.axis_index("x") + lax.axis_index("y")).astype(jnp.int32).reshape(1)

    w_pack = [_pack_group(w, names) for names in GROUPS]
    my_index = (4 * lax.axis_index("x") + 2 * lax.axis_index("y") + lax.axis_index("c")).astype(jnp.int32)
    zero = jnp.zeros((), jnp.int32)
    w_all = _allgather(w_pack[0].astype(BF16), "allgather_weights0")
    wb = {n: _full_stored(part, n) for n, part in _split_group(w_all, GROUPS[0]).items()}
    gathers = {gi: _split_start(w_pack[gi].astype(BF16), N_DEV, _gather_plan, w_all, "weights_gather_start%d" % gi)
               for gi in range(1, len(GROUPS))}

    def gathered(started, after, name):
        sems, src, land, _ = started
        src, land = _split_wait(sems, src, land, _gather_wait_plan, after, name)
        return lax.dynamic_update_slice(land, src[None], (my_index, zero, zero))

    def late_weights(gi, after):
        full = gathered(gathers[gi], after, "weights_gather_wait%d" % gi)
        return {n: _full_stored(part, n) for n, part in _split_group(full, GROUPS[gi]).items()}

    pending = {}

    def grads_ready(gi, grads):
        g_pack = jnp.concatenate([_stacked_stored(grads[n], n) for n in GROUPS[gi]], axis=1)
        t1 = _pair_exchange(g_pack, "grad_pair_exchange%d" % gi)
        p_sum, p_bf = _pair_sum(g_pack, t1, my_c, "grad_pair_sum%d" % gi, GROUP_TR[gi])
        sems, src, land, token = _split_start(p_bf, 3, _chip_plan, p_sum, "grad_chip_exchange_start%d" % gi)
        pending[gi] = (p_sum, sems, src, land)
        return token

    sp = {
        "norm1_g": norm1_g + sum(started[3][:1, :1] for started in gathers.values()), "mem_norm_g": mem_norm_g, "b_gate": b_gate, "b_glu": b_glu, "norm2_g": norm2_g,
        "final_g": final_g.reshape(1, D_MODEL),
        "ssm_lambda_re": ssm_lambda_re[0], "ssm_lambda_im": ssm_lambda_im[0], "ssm_log_dt": ssm_log_dt[0],
        "ssm_b_re": ssm_b_re[0], "ssm_b_im": ssm_b_im[0], "ssm_c_re": ssm_c_re[0], "ssm_c_im": ssm_c_im[0],
        "ssm_d": ssm_d[0],
    }
    loss, grad_x, gs = _local_step(x[0], mem[0], loss_target[0], wb, sp, late_weights, grads_ready)
    loss = lax.psum(loss[0, 0], ("x", "y", "c"))
    sg_shapes = [gs[n].shape for n in _SMALL_GRAD_ORDER]
    sg_started = _split_start(_pack([gs[n] for n in _SMALL_GRAD_ORDER]), N_DEV, _gather_plan, grad_x,
                              "small_grads_gather_start")

    big_g = {}
    for gi, names in enumerate(GROUPS):
        p_sum, sems, src, land = pending[gi]
        t2 = _split_wait(sems, src, land, _chip_plan, grad_x, "grad_chip_exchange_wait%d" % gi)[1]
        g_pack = _grad_sum(p_sum, t2, my_chip, "grad_sum%d" % gi, GROUP_TR[gi])
        for n, part in _split_group(g_pack, names).items():
            big_g[n] = _unstored(part, n)
    rows_of = lambda d, names: [d[n].reshape(d[n].shape[-2:]) for n in names]
    big_out = _adam_many(rows_of(big_g, BIG), rows_of(w, BIG), rows_of(m, BIG), rows_of(v, BIG), 8, "adam_big")
    big = [big_g] + [{n: a[None] for n, a in zip(BIG, outs)} for outs in big_out]

    sg_all = gathered(sg_started, big_out[0][0], "small_grads_gather_wait")
    sg = dict(zip(_SMALL_GRAD_ORDER, _unpack(_sum8(sg_all, "sum_small_grads"), sg_shapes)))
    _, disc_vjp = jax.vjp(_discretize, sp["ssm_lambda_re"], sp["ssm_lambda_im"], sp["ssm_log_dt"],
                          sp["ssm_b_re"], sp["ssm_b_im"])
    d_lre, d_lim, d_ldt, d_bre, d_bim = disc_vjp((sg["a_re"].reshape(SSM_G, SSM_P), sg["a_im"].reshape(SSM_G, SSM_P),
                                                  sg["bb_re"], sg["bb_im"]))
    small_grad = {
        "norm1_g": sg["norm1_g"], "mem_norm_g": sg["mem_norm_g"], "b_gate": sg["b_gate"],
        "ssm_lambda_re": d_lre, "ssm_lambda_im": d_lim, "ssm_log_dt": d_ldt, "ssm_b_re": d_bre, "ssm_b_im": d_bim,
        "ssm_c_re": sg["ssm_c_re"], "ssm_c_im": sg["ssm_c_im"], "ssm_d": sg["ssm_d"], "b_glu": sg["b_glu"],
        "norm2_g": sg["norm2_g"], "final_g": sg["final_g"],
    }
    small_grad = {n: small_grad[n].reshape(w[n].shape) for n in SMALL}

    def squeezed(a):
        return a.reshape(a.shape[1:]) if a.ndim > 2 else a.reshape(1, -1)

    sq = lambda d: [squeezed(d[n]) for n in SMALL]
    small_out = _adam_many(sq(small_grad), sq(w), sq(m), sq(v), 1, "adam_small")
    small = [small_grad] + [{n: a.reshape(w[n].shape) for n, a in zip(SMALL, outs)} for outs in small_out]

    outs = [loss, grad_x[None]]
    for kind in range(4):
        for n in ALL_W:
            outs.append(big[kind][n] if n in BIG else small[kind][n])
    return tuple(outs)
```

```python
import math

import numpy as np
import jax
import jax.numpy as jnp
from jax import lax
from jax.experimental import pallas as pl
from jax.experimental.pallas import tpu as pltpu

F32 = jnp.float32
BF16 = jnp.bfloat16
_MXU = jnp.bfloat16

D_MODEL = 1024
SSM_G, SSM_H, SSM_P = 32, 16, 64
SSM_W = SSM_G * SSM_H
SSM_S = SSM_G * SSM_P
SSM_BD = 4
ATT_E = 64
ATT_HG = 4
ATT_GW = ATT_HG * ATT_E
ATT_WIN = 128
ATT_QB = 8
ATT_QB_FWD = 4
DILATIONS = (1, 4, 16)
MEM_H, MEM_E = 4, 128
MEM_W = MEM_H * MEM_E
ZA_W = SSM_W + 9 * ATT_GW + MEM_W
ZG_W = 3 * D_MODEL
IN_W = ZA_W + ZG_W
RMS_EPS = 1e-6
NEG_INF = -1e30

ADAM_LR, ADAM_B1, ADAM_B2, ADAM_EPS, ADAM_WD, ADAM_STEP = 0.001, 0.9, 0.999, 1e-08, 0.01, 10

N_DEV = 8
PACK_C = 512
_VMEM_LIMIT = 56 * 1024 * 1024
SUBLANES = 16
SCAN_SEG = 128
SCAN_CHAINS = 4
SCAN_UNROLL = 4
SCAN_W = 128

BIG = ("w_in", "w_glu", "w_ssm_br", "w_attn_br", "w_mem_kv", "w_mem_br", "w_o", "w_up", "w_down")
BIG_SHAPE = {
    "w_in": (D_MODEL, IN_W, 1), "w_glu": (SSM_W, SSM_W, 0), "w_ssm_br": (SSM_W, D_MODEL, 1),
    "w_attn_br": (ATT_GW, D_MODEL, 1), "w_mem_kv": (D_MODEL, 2 * MEM_W, 0), "w_mem_br": (MEM_W, D_MODEL, 1),
    "w_o": (D_MODEL, D_MODEL, 0), "w_up": (D_MODEL, 4 * D_MODEL, 1), "w_down": (4 * D_MODEL, D_MODEL, 0),
}
SMALL = ("norm1_g", "mem_norm_g", "b_gate", "ssm_lambda_re", "ssm_lambda_im", "ssm_log_dt", "ssm_b_re",
         "ssm_b_im", "ssm_c_re", "ssm_c_im", "ssm_d", "b_glu", "norm2_g", "final_g")
ALL_W = ("norm1_g", "mem_norm_g", "w_in", "b_gate", "ssm_lambda_re", "ssm_lambda_im", "ssm_log_dt", "ssm_b_re",
         "ssm_b_im", "ssm_c_re", "ssm_c_im", "ssm_d", "w_glu", "b_glu", "w_ssm_br", "w_attn_br", "w_mem_kv",
         "w_mem_br", "w_o", "norm2_g", "w_up", "w_down", "final_g")


def _params(sem):
    return pltpu.CompilerParams(dimension_semantics=sem, vmem_limit_bytes=_VMEM_LIMIT)


def _pick(n, cap):
    if n <= cap:
        return n
    t = (cap // 128) * 128
    while n % t:
        t -= 128
    return t


def _mm(a, b, outs, *, name, ta=False, tb=False, epi=None, mn=(), rows=(), pair2=None, bd=0, n_sums=0,
        tm=1024, tn=1024, tk=2048):
    ab = [a, b] + (list(pair2) if pair2 is not None else [])
    planes = [op[1] if isinstance(op, tuple) else None for op in ab]
    ab = [op[0] if isinstance(op, tuple) else op for op in ab]
    a_shape, b_shape = ab[0].shape[-2:], ab[1].shape[-2:]
    m = a_shape[1] if ta else a_shape[0]
    k = a_shape[0] if ta else a_shape[1]
    n = b_shape[0] if tb else b_shape[1]
    assert k == (b_shape[1] if tb else b_shape[0]), (name, a_shape, b_shape)
    out_n = n
    if bd and ta:
        assert not tb
        tm, tn, tk = m // bd, n // bd, _pick(k, tk)
        grid, out_n = (bd, 1, k // tk), tn
        a_blk = ((tk, tm), lambda i, j, kk: (kk, i))
        b_blk = ((tk, tn), lambda i, j, kk: (kk, i))
        mn_spec = pl.BlockSpec((tm, tn), lambda i, j, kk: (i, 0))
    elif bd:
        tm, tn, tk = _pick(m, tm), n // bd, k // bd
        grid = (m // tm, bd, 1)
        a_blk = ((tm, tk), lambda i, j, kk: (i, j))
        b_blk = ((tn, tk) if tb else (tk, tn), lambda i, j, kk: (j, j))
        mn_spec = pl.BlockSpec((tm, tn), lambda i, j, kk: (i, j))
    else:
        tm, tn, tk = _pick(m, tm), _pick(n, tn), _pick(k, tk)
        grid = (m // tm, n // tn, k // tk)
        a_blk = ((tk, tm), lambda i, j, kk: (kk, i)) if ta else ((tm, tk), lambda i, j, kk: (i, kk))
        b_blk = ((tn, tk), lambda i, j, kk: (j, kk)) if tb else ((tk, tn), lambda i, j, kk: (kk, j))
        mn_spec = pl.BlockSpec((tm, tn), lambda i, j, kk: (i, j))

    def operand_spec(blk, plane):
        shape, imap = blk
        if plane is None:
            return pl.BlockSpec(shape, imap)
        return pl.BlockSpec((None,) + shape, lambda i, j, kk: (plane,) + imap(i, j, kk))

    ab_specs = [operand_spec(a_blk if q % 2 == 0 else b_blk, p) for q, p in enumerate(planes)]
    mn_arrays = [e[0] if isinstance(e, tuple) else e for e in mn]
    mn_specs = [pl.BlockSpec((tm, tn), lambda i, j, kk, c=e[1]: (i, c)) if isinstance(e, tuple) else mn_spec
                for e in mn]
    nk = grid[2]
    row_spec = pl.BlockSpec((1, tn), lambda i, j, kk: (0, j))
    n_ex, n_out = len(mn) + len(rows), len(outs)
    assert n_sums == 0 or (grid[1] == 1 and not bd)
    dims = (((0 if ta else 1,), (1 if tb else 0,)), ((), ()))

    def body(*refs):
        ab_refs, rest = refs[:len(ab)], refs[len(ab):]
        ex, o_refs, acc = rest[:n_ex], rest[n_ex:n_ex + n_out], rest[-1]
        s_refs = rest[n_ex + n_out:n_ex + n_out + n_sums]
        first_row_tile = pl.program_id(0) == 0
        kk = pl.program_id(2)

        @pl.when(kk == 0)
        def _():
            acc[...] = jnp.zeros_like(acc)

        for a_ref, b_ref in zip(ab_refs[0::2], ab_refs[1::2]):
            acc[...] += lax.dot_general(a_ref[...].astype(_MXU), b_ref[...].astype(_MXU), dims,
                                        preferred_element_type=F32)

        @pl.when(kk == nk - 1)
        def _():
            vals = (acc[...],) if epi is None else epi(acc[...], *[r[...] for r in ex])
            for r, v in zip(o_refs, vals):
                r[...] = v.astype(r.dtype)
            for r, v in zip(s_refs, vals[n_out:]):
                r[...] = jnp.where(first_row_tile, v, r[...] + v)

    res = pl.pallas_call(
        body, grid=grid,
        in_specs=ab_specs + mn_specs + [row_spec] * len(rows),
        out_specs=[mn_spec] * n_out + [row_spec] * n_sums,
        out_shape=[jax.ShapeDtypeStruct((m, out_n), dt) for dt in outs]
        + [jax.ShapeDtypeStruct((1, out_n), F32)] * n_sums,
        scratch_shapes=[pltpu.VMEM((tm, tn), F32)],
        compiler_params=_params(("arbitrary" if n_sums else "parallel", "parallel", "arbitrary")), name=name,
    )(*ab, *mn_arrays, *rows)
    return res[0] if n_out + n_sums == 1 else res


def _ew(fn, rows, bcs, out_rows, out_accs, *, name, tm=256):
    r = rows[0].shape[0]
    tm = min(tm, r)
    assert r % tm == 0
    nr, nb, no, na = len(rows), len(bcs), len(out_rows), len(out_accs)

    def body(*refs):
        i = pl.program_id(0)
        r_in, b_in = refs[:nr], refs[nr:nr + nb]
        o_r, o_a = refs[nr + nb:nr + nb + no], refs[nr + nb + no:]
        outs, accs = fn([x[...] for x in r_in], [x[...] for x in b_in])
        for ref, v in zip(o_r, outs):
            ref[...] = v.astype(ref.dtype)
        if na:
            @pl.when(i == 0)
            def _():
                for ref in o_a:
                    ref[...] = jnp.zeros_like(ref)

            for ref, v in zip(o_a, accs):
                ref[...] += v

    res = pl.pallas_call(
        body, grid=(r // tm,),
        in_specs=[pl.BlockSpec((tm, x.shape[1]), lambda i: (i, 0)) for x in rows]
        + [pl.BlockSpec((1, x.shape[1]), lambda i: (0, 0)) for x in bcs],
        out_specs=[pl.BlockSpec((tm, c), lambda i: (i, 0)) for c, _ in out_rows]
        + [pl.BlockSpec((1, c), lambda i: (0, 0)) for c in out_accs],
        out_shape=[jax.ShapeDtypeStruct((r, c), dt) for c, dt in out_rows]
        + [jax.ShapeDtypeStruct((1, c), F32) for c in out_accs],
        compiler_params=_params(("arbitrary",)), name=name,
    )(*rows, *bcs)
    return res


def _colsum(x):
    return jnp.sum(x, axis=0, keepdims=True)


def _sigmoid(x):
    return 1.0 / (1.0 + jnp.exp(-x))


def _rms_bwd_tile(xv, dv, g):
    rs = lax.rsqrt(jnp.mean(xv * xv, axis=-1, keepdims=True) + RMS_EPS)
    gd = dv * g
    dx = rs * gd - xv * (rs * rs * rs) * jnp.mean(gd * xv, axis=-1, keepdims=True)
    return dx, _colsum(dv * xv * rs)


def _rms_fwd(x, g, name):
    def fn(r, b):
        xv = r[0]
        rs = lax.rsqrt(jnp.mean(xv * xv, axis=-1, keepdims=True) + RMS_EPS)
        return [xv * rs * b[0]], []
    return _ew(fn, [x], [g], [(x.shape[1], BF16)], [], name=name)[0]


def _rms_bwd(x, dn, res, g, name):
    def fn(r, b):
        dx, dg = _rms_bwd_tile(r[0], r[1], b[0])
        if res is not None:
            dx = dx + r[2]
        return [dx], [dg]
    rows = [x, dn] + ([res] if res is not None else [])
    return _ew(fn, rows, [g], [(x.shape[1], F32)], [x.shape[1]], name=name)


def _scan_order(x):
    l, c = x.shape
    return x.reshape(l // (SUBLANES * SCAN_SEG), SUBLANES, SCAN_SEG, c).transpose(0, 2, 1, 3).reshape(l, c)


def _time_order(x):
    l, c = x.shape
    return x.reshape(l // (SUBLANES * SCAN_SEG), SCAN_SEG, SUBLANES, c).transpose(0, 2, 1, 3).reshape(l, c)


def _ssm_scan(x, w_re, w_im, a_pair, *, reverse, s_fwd=None, u=None, name):
    l = x.shape[0]
    seg, w = SCAN_SEG, SCAN_W
    bd_w = SSM_W // SSM_BD
    tiles_per_bd = SSM_S // SSM_BD // w
    nch = min(SCAN_CHAINS, l // (SUBLANES * seg))
    chain_rows = SUBLANES * seg
    tb = nch * chain_rows
    nt = l // tb
    with_da = s_fwd is not None
    assert reverse or not with_da

    def tt(t):
        return nt - 1 - t if reverse else t

    def body(*refs):
        if with_da:
            (x_ref, wr_ref, wi_ref, a_ref, sf_ref, sp_ref, u_ref, s_ref, da_ref, dw_ref, dx_ref,
             p_ref, c_ref, b_scr) = refs
        else:
            x_ref, wr_ref, wi_ref, a_ref, s_ref, p_ref, c_ref, b_scr = refs
        t_blk = pl.program_id(1)
        ar, ai = a_ref[0], a_ref[1]

        @pl.when(t_blk == 0)
        def _():
            def pstep(i, carry):
                pr, pi = carry
                p_ref[0, pl.ds(i, 1), :] = pr
                p_ref[1, pl.ds(i, 1), :] = pi
                return pr * ar - pi * ai, pr * ai + pi * ar

            lax.fori_loop(0, seg, pstep, (ar, ai))
            c_ref[...] = jnp.zeros_like(c_ref)
            if with_da:
                da_ref[...] = jnp.zeros_like(da_ref)
                dw_ref[...] = jnp.zeros_like(dw_ref)
                dx_ref[...] = jnp.zeros_like(dx_ref)

        xb = x_ref[...].astype(_MXU)
        b_scr[:, :w] = jnp.dot(xb, wr_ref[...], preferred_element_type=F32)
        b_scr[:, w:] = jnp.dot(xb, wi_ref[...], preferred_element_type=F32)
        arb, aib = jnp.broadcast_to(ar, (SUBLANES, w)), jnp.broadcast_to(ai, (SUBLANES, w))
        zero = jnp.zeros((SUBLANES, w), F32)

        def tile(g, step):
            return pl.ds(pl.multiple_of(g * chain_rows + step * SUBLANES, SUBLANES), SUBLANES)

        def rows(g, i):
            return tile(g, seg - 1 - i if reverse else i)

        def local_step(i, carry):
            out = []
            for g in range(nch):
                sr, si = carry[2 * g], carry[2 * g + 1]
                idx = rows(g, i)
                sr, si = arb * sr - aib * si + b_scr[idx, :w], arb * si + aib * sr + b_scr[idx, w:]
                b_scr[idx, :w] = sr
                b_scr[idx, w:] = si
                out += [sr, si]
            return tuple(out)

        def unrolled(step_fn, first):
            def trip(q, carry):
                for r in range(SCAN_UNROLL):
                    carry = step_fn(first + q * SCAN_UNROLL + r, carry)
                return carry
            return trip

        ends = lax.fori_loop(0, seg // SCAN_UNROLL, unrolled(local_step, 0), (zero,) * (2 * nch))

        a_seg_r, a_seg_i = p_ref[0, seg - 1:seg, :], p_ref[1, seg - 1:seg, :]
        cr, ci = c_ref[0], c_ref[1]
        sub = lax.broadcasted_iota(jnp.int32, (SUBLANES, w), 0)
        ins = [[zero, zero] for _ in range(nch)]
        order = [(g, k) for g in range(nch) for k in range(SUBLANES)]
        for g, k in (order[::-1] if reverse else order):
            ins[g] = [jnp.where(sub == k, cr, ins[g][0]), jnp.where(sub == k, ci, ins[g][1])]
            er, ei = ends[2 * g][k:k + 1], ends[2 * g + 1][k:k + 1]
            cr, ci = er + a_seg_r * cr - a_seg_i * ci, ei + a_seg_r * ci + a_seg_i * cr
        c_ref[0] = cr
        c_ref[1] = ci

        def fix(g, i):
            idx = rows(g, i)
            pr, pi = p_ref[0, pl.ds(i, 1), :], p_ref[1, pl.ds(i, 1), :]
            sr = b_scr[idx, :w] + pr * ins[g][0] - pi * ins[g][1]
            si = b_scr[idx, w:] + pr * ins[g][1] + pi * ins[g][0]
            s_ref.at[0][idx, :] = sr.astype(s_ref.dtype)
            s_ref.at[1][idx, :] = si.astype(s_ref.dtype)
            return sr, si

        if not with_da:
            def fix_step(i, carry):
                for g in range(nch):
                    fix(g, i)
                return carry

            lax.fori_loop(0, seg // SCAN_UNROLL, unrolled(fix_step, 0), 0)
        else:
            def adj_step(i, acc):
                acc_r, acc_i = acc
                for g in range(nch):
                    lr, li = fix(g, i)
                    prev = tile(g, seg - 2 - i)
                    fr, fi = sf_ref.at[0][prev, :].astype(F32), sf_ref.at[1][prev, :].astype(F32)
                    acc_r, acc_i = acc_r + lr * fr + li * fi, acc_i + li * fr - lr * fi
                return acc_r, acc_i

            acc = lax.fori_loop(0, seg // SCAN_UNROLL - 1, unrolled(adj_step, 0), (zero, zero))
            for i in range(seg - SCAN_UNROLL, seg - 1):
                acc = adj_step(i, acc)
            acc_r, acc_i = acc
            first_block = tt(t_blk) == 0
            for g in range(nch):
                lr, li = fix(g, seg - 1)
                seg_ends = tile(g, seg - 1)
                if g == 0:
                    pvr = jnp.where(first_block, 0.0, sp_ref[0, SUBLANES - 1:SUBLANES, :].astype(F32))
                    pvi = jnp.where(first_block, 0.0, sp_ref[1, SUBLANES - 1:SUBLANES, :].astype(F32))
                else:
                    pvr = sf_ref[0, g * chain_rows - 1:g * chain_rows, :].astype(F32)
                    pvi = sf_ref[1, g * chain_rows - 1:g * chain_rows, :].astype(F32)
                fr = jnp.where(sub == 0, pvr, pltpu.roll(sf_ref.at[0][seg_ends, :].astype(F32), 1, 0))
                fi = jnp.where(sub == 0, pvi, pltpu.roll(sf_ref.at[1][seg_ends, :].astype(F32), 1, 0))
                acc_r = acc_r + lr * fr + li * fi
                acc_i = acc_i + li * fr - lr * fi
            da_ref[0] += jnp.sum(acc_r, axis=0, keepdims=True)
            da_ref[1] += jnp.sum(acc_i, axis=0, keepdims=True)
            for plane in range(2):
                dw_ref[plane] += _tn_dot(u_ref[...], s_ref[plane])
                dx_ref[plane] += _tn_dot(xb, sf_ref[plane])

    x_spec = pl.BlockSpec((tb, bd_w), lambda j, t: (tt(t), j // tiles_per_bd))
    w_spec = pl.BlockSpec((bd_w, w), lambda j, t: (j // tiles_per_bd, j))
    d_spec = pl.BlockSpec((2, bd_w, w), lambda j, t: (0, j // tiles_per_bd, j % tiles_per_bd))
    a_spec = pl.BlockSpec((2, 1, w), lambda j, t: (0, 0, j))
    s_spec = pl.BlockSpec((2, tb, w), lambda j, t: (0, tt(t), j))
    in_specs, args = [x_spec, w_spec, w_spec, a_spec], [x, w_re, w_im, a_pair]
    out_specs, out_shape = [s_spec], [jax.ShapeDtypeStruct((2, l, SSM_S), BF16)]
    scratch = [pltpu.VMEM((2, seg, w), F32), pltpu.VMEM((2, 1, w), F32), pltpu.VMEM((tb, 2 * w), F32)]
    if with_da:
        in_specs += [s_spec, pl.BlockSpec((2, SUBLANES, w),
                                          lambda j, t: (0, jnp.maximum(tt(t) * (tb // SUBLANES) - 1, 0), j)),
                     x_spec]
        args += [s_fwd, s_fwd, u]
        out_specs += [a_spec, d_spec, d_spec]
        out_shape += ([jax.ShapeDtypeStruct((2, 1, SSM_S), F32)]
                      + [jax.ShapeDtypeStruct((2, SSM_W, SSM_S // SSM_BD), F32)] * 2)
    res = pl.pallas_call(
        body, grid=(SSM_S // w, nt), in_specs=in_specs, out_specs=out_specs, out_shape=out_shape,
        scratch_shapes=scratch, compiler_params=_params(("parallel", "arbitrary")), name=name,
    )(*args)
    return res if with_da else res[0]


def _nt_dot(x, y):
    return lax.dot_general(x.astype(_MXU), y.astype(_MXU), (((1,), (1,)), ((), ())), preferred_element_type=F32)


def _tn_dot(x, y):
    return lax.dot_general(x.astype(_MXU), y.astype(_MXU), (((0,), (0,)), ((), ())), preferred_element_type=F32)


def _nn_dot(x, y):
    return jnp.dot(x.astype(_MXU), y.astype(_MXU), preferred_element_type=F32)


def _attn_mask2(gb, nb):
    qi = lax.broadcasted_iota(jnp.int32, (ATT_WIN, 2 * ATT_WIN), 0)
    c = lax.broadcasted_iota(jnp.int32, (ATT_WIN, 2 * ATT_WIN), 1)
    has_prev = (gb % nb) != 0
    prev_ok = jnp.logical_and(jnp.logical_and(c < ATT_WIN, c >= qi), has_prev)
    own_ok = jnp.logical_and(c >= ATT_WIN, c - ATT_WIN <= qi)
    return jnp.logical_or(prev_ok, own_ok)


def _attn_specs(qb):
    cur = pl.BlockSpec((qb * ATT_WIN, ATT_GW), lambda i: (i, 0))
    prev = pl.BlockSpec((ATT_WIN, ATT_GW), lambda i: (jnp.maximum(qb * i - 1, 0), 0))
    return cur, prev


def _attn_fwd(q, k, v, nb, name):
    l = q.shape[0]
    scale = ATT_E ** -0.5
    w = ATT_WIN

    qb = ATT_QB_FWD

    def body(q_ref, kc_ref, kp_ref, vc_ref, vp_ref, o_ref, lse_ref):
        i = pl.program_id(0)
        masks = [_attn_mask2(qb * i + b, nb) for b in range(qb)]
        for h in range(ATT_HG):
            sl = slice(h * ATT_E, (h + 1) * ATT_E)
            k_ext = jnp.concatenate([kp_ref[:, sl], kc_ref[:, sl]], axis=0)
            v_ext = jnp.concatenate([vp_ref[:, sl], vc_ref[:, sl]], axis=0)
            for b in range(qb):
                r, kr = slice(b * w, (b + 1) * w), slice(b * w, (b + 2) * w)
                s = jnp.where(masks[b], _nt_dot(q_ref[r, sl], k_ext[kr]) * scale, NEG_INF)
                mx = jnp.max(s, axis=-1, keepdims=True)
                p = jnp.exp(s - mx)
                den = jnp.sum(p, axis=-1, keepdims=True)
                o_ref[r, sl] = _nn_dot(p, v_ext[kr]) / den
                lse_ref[r, sl] = jnp.broadcast_to(mx + jnp.log(den), (w, ATT_E))

    cur, prev = _attn_specs(qb)
    return pl.pallas_call(
        body, grid=(l // (qb * w),), in_specs=[cur, cur, prev, cur, prev], out_specs=[cur, cur],
        out_shape=[jax.ShapeDtypeStruct((l, ATT_GW), F32)] * 2,
        compiler_params=_params(("parallel",)), name=name,
    )(q, k, k, v, v)


def _attn_bwd(q, k, v, do, lse, dd, nb, name):
    l = q.shape[0]
    scale = ATT_E ** -0.5
    w = ATT_WIN
    nblk = l // w

    def body(q_ref, kc_ref, kp_ref, vc_ref, vp_ref, do_ref, lse_ref, dd_ref, qn_ref, don_ref, lsen_ref, ddn_ref,
             dq_ref, dk_ref, dv_ref, dk_acc, dv_acc):
        i = pl.program_id(0)
        masks = [_attn_mask2(ATT_QB * i + b, nb) for b in range(ATT_QB)]
        nxt = ATT_QB * (i + 1)
        nxt_attends = jnp.logical_and(nxt < nblk, (nxt % nb) != 0)
        qi = lax.broadcasted_iota(jnp.int32, (w, w), 0)
        kj = lax.broadcasted_iota(jnp.int32, (w, w), 1)
        mask_n = jnp.logical_and(kj >= qi, nxt_attends)
        dk_acc[...] = jnp.zeros_like(dk_acc)
        dv_acc[...] = jnp.zeros_like(dv_acc)
        for h in range(ATT_HG):
            sl, col = slice(h * ATT_E, (h + 1) * ATT_E), slice(h * ATT_E, h * ATT_E + 1)
            k_ext = jnp.concatenate([kp_ref[:, sl], kc_ref[:, sl]], axis=0)
            v_ext = jnp.concatenate([vp_ref[:, sl], vc_ref[:, sl]], axis=0)
            for b in range(ATT_QB):
                r, kr = slice(b * w, (b + 1) * w), slice(b * w, (b + 2) * w)
                qh, doh, k2, v2 = q_ref[r, sl], do_ref[r, sl], k_ext[kr], v_ext[kr]
                p = jnp.where(masks[b], jnp.exp(_nt_dot(qh, k2) * scale - lse_ref[r, col]), 0.0)
                ds = p * (_nt_dot(doh, v2) - dd_ref[r, col]) * scale
                dq_ref[r, sl] = _nn_dot(ds, k2).astype(dq_ref.dtype)
                dk2, dv2 = _tn_dot(ds, qh), _tn_dot(p, doh)
                dk_acc[r, sl] += dk2[w:]
                dv_acc[r, sl] += dv2[w:]
                if b > 0:
                    rp = slice((b - 1) * w, b * w)
                    dk_acc[rp, sl] += dk2[:w]
                    dv_acc[rp, sl] += dv2[:w]
            last = slice((ATT_QB - 1) * w, ATT_QB * w)
            kl, vl, qn, don = kc_ref[last, sl], vc_ref[last, sl], qn_ref[:, sl], don_ref[:, sl]
            pn = jnp.where(mask_n, jnp.exp(_nt_dot(qn, kl) * scale - lsen_ref[:, col]), 0.0)
            dsn = pn * (_nt_dot(don, vl) - ddn_ref[:, col]) * scale
            dk_acc[last, sl] += _tn_dot(dsn, qn)
            dv_acc[last, sl] += _tn_dot(pn, don)
        dk_ref[...] = dk_acc[...].astype(dk_ref.dtype)
        dv_ref[...] = dv_acc[...].astype(dv_ref.dtype)

    cur, prev = _attn_specs(ATT_QB)
    nxt_spec = pl.BlockSpec((w, ATT_GW), lambda i: (jnp.minimum(ATT_QB * (i + 1), nblk - 1), 0))
    return pl.pallas_call(
        body, grid=(l // (ATT_QB * w),),
        in_specs=[cur, cur, prev, cur, prev, cur, cur, cur, nxt_spec, nxt_spec, nxt_spec, nxt_spec],
        out_specs=[cur] * 3, out_shape=[jax.ShapeDtypeStruct((l, ATT_GW), BF16)] * 3,
        scratch_shapes=[pltpu.VMEM((ATT_QB * w, ATT_GW), F32)] * 2,
        compiler_params=_params(("parallel",)), name=name,
    )(q, k, k, v, v, do, lse, dd, q, do, lse, dd)


def _to_perm(a, d):
    if d == 1:
        return a
    l, c = a.shape
    return a.reshape(l // d, d, c).transpose(1, 0, 2).reshape(l, c)


def _from_perm(a, d):
    if d == 1:
        return a
    l, c = a.shape
    return a.reshape(d, l // d, c).transpose(1, 0, 2).reshape(l, c)


def _mem_probs(qh, kh):
    s = _nt_dot(qh, kh) * (MEM_E ** -0.5)
    e = jnp.exp(s - jnp.max(s, axis=-1, keepdims=True))
    return e / jnp.sum(e, axis=-1, keepdims=True)


def _mem_fwd(mq, kv, name, tm=512):
    l, nm = mq.shape[0], kv.shape[0]

    def body(q_ref, kv_ref, o_ref):
        for h in range(MEM_H):
            sl = slice(h * MEM_E, (h + 1) * MEM_E)
            p = _mem_probs(q_ref[:, sl], kv_ref[:, sl])
            o_ref[:, sl] = _nn_dot(p, kv_ref[:, MEM_W + h * MEM_E:MEM_W + (h + 1) * MEM_E]).astype(o_ref.dtype)

    return pl.pallas_call(
        body, grid=(l // tm,),
        in_specs=[pl.BlockSpec((tm, MEM_W), lambda i: (i, 0)), pl.BlockSpec((nm, 2 * MEM_W), lambda i: (0, 0))],
        out_specs=pl.BlockSpec((tm, MEM_W), lambda i: (i, 0)),
        out_shape=jax.ShapeDtypeStruct((l, MEM_W), BF16),
        compiler_params=_params(("parallel",)), name=name,
    )(mq, kv)


def _mem_bwd(mq, kv, dmo, name, tm=512):
    l, nm = mq.shape[0], kv.shape[0]
    scale = MEM_E ** -0.5

    def body(q_ref, kv_ref, do_ref, dq_ref, dkv_ref):
        @pl.when(pl.program_id(0) == 0)
        def _():
            dkv_ref[...] = jnp.zeros_like(dkv_ref)

        for h in range(MEM_H):
            sl = slice(h * MEM_E, (h + 1) * MEM_E)
            vsl = slice(MEM_W + h * MEM_E, MEM_W + (h + 1) * MEM_E)
            qh, kh, vh, doh = q_ref[:, sl], kv_ref[:, sl], kv_ref[:, vsl], do_ref[:, sl]
            p = _mem_probs(qh, kh)
            dp = _nt_dot(doh, vh)
            ds = p * (dp - jnp.sum(dp * p, axis=-1, keepdims=True)) * scale
            dq_ref[:, sl] = _nn_dot(ds, kh).astype(dq_ref.dtype)
            dkv_ref[:, sl] += _tn_dot(ds, qh)
            dkv_ref[:, vsl] += _tn_dot(p, doh)

    row = pl.BlockSpec((tm, MEM_W), lambda i: (i, 0))
    full = pl.BlockSpec((nm, 2 * MEM_W), lambda i: (0, 0))
    return pl.pallas_call(
        body, grid=(l // tm,), in_specs=[row, full, row], out_specs=[row, full],
        out_shape=[jax.ShapeDtypeStruct((l, MEM_W), BF16), jax.ShapeDtypeStruct((nm, 2 * MEM_W), F32)],
        compiler_params=_params(("arbitrary",)), name=name,
    )(mq, kv, dmo)


def _gated_out_proj(zg, branches, b_gate, w_o, x, g2, name, tm=512):
    l, d = x.shape
    nbr = len(branches)

    def body(zg_ref, *rest):
        br_refs, (bg_ref, w_ref, x_ref, g2_ref, m_ref, h_ref, n_ref) = rest[:nbr], rest[nbr:]
        merged = jnp.zeros((tm, d), F32)
        for i, br_ref in enumerate(br_refs):
            cols = slice(i * d, (i + 1) * d)
            merged += _sigmoid(zg_ref[:, cols].astype(F32) + bg_ref[:, cols]) * br_ref[...].astype(F32)
        mb = merged.astype(BF16)
        m_ref[...] = mb
        hv = jnp.dot(mb.astype(_MXU), w_ref[...].astype(_MXU), preferred_element_type=F32) + x_ref[...]
        h_ref[...] = hv
        rs = lax.rsqrt(jnp.mean(hv * hv, axis=-1, keepdims=True) + RMS_EPS)
        n_ref[...] = (hv * rs * g2_ref[...]).astype(n_ref.dtype)

    row = lambda c: pl.BlockSpec((tm, c), lambda i: (i, 0))
    full = lambda a: pl.BlockSpec(a.shape, lambda i: (0, 0))
    return pl.pallas_call(
        body, grid=(l // tm,),
        in_specs=[row(nbr * d)] + [row(d)] * nbr + [full(b_gate), full(w_o), row(d), full(g2)],
        out_specs=[row(d)] * 3,
        out_shape=[jax.ShapeDtypeStruct((l, d), BF16), jax.ShapeDtypeStruct((l, d), F32),
                   jax.ShapeDtypeStruct((l, d), BF16)],
        compiler_params=_params(("parallel",)), name=name,
    )(zg, *branches, b_gate, w_o, x, g2)


def _discretize(lam_re, lam_im, log_dt, b_re, b_im):
    dt = jnp.exp(log_dt)[:, None]
    mag = jnp.exp(lam_re * dt)
    a_re, a_im = mag * jnp.cos(lam_im * dt), mag * jnp.sin(lam_im * dt)
    nr, ni = a_re - 1.0, a_im
    den = lam_re * lam_re + lam_im * lam_im
    coef_re = (nr * lam_re + ni * lam_im) / den
    coef_im = (ni * lam_re - nr * lam_im) / den
    bb_re = coef_re[..., None] * b_re - coef_im[..., None] * b_im
    bb_im = coef_re[..., None] * b_im + coef_im[..., None] * b_re
    return a_re, a_im, bb_re, bb_im


def _bd_in(bb):
    return jnp.einsum("gph,gk->ghkp", bb, jnp.eye(SSM_G, dtype=bb.dtype)).reshape(SSM_W, SSM_S)


def _bd_diag(x):
    gb = SSM_G // SSM_BD
    t = x.reshape(SSM_BD, gb, SSM_H, gb, SSM_P)
    return jnp.einsum("bghgp->bghp", t).reshape(SSM_G, SSM_H, SSM_P)


_ANY = pl.BlockSpec(memory_space=pl.ANY)
_MESH = pl.DeviceIdType.MESH


def _allgather(x, name):
    def body(x_ref, out_ref, send_sems, recv_sems, local_sem):
        mx, my, mc = lax.axis_index("x"), lax.axis_index("y"), lax.axis_index("c")
        me, sibling = (mx, my, mc), (mx, my, 1 - mc)
        chips = [(1 - mx, my), (mx, 1 - my), (1 - mx, 1 - my)]

        def blk(px, py, pc):
            return out_ref.at[4 * px + 2 * py + pc]

        def copy(k, block, to, src=None):
            return pltpu.make_async_remote_copy(
                src_ref=blk(*block) if src is None else src, dst_ref=blk(*block),
                send_sem=send_sems.at[k], recv_sem=recv_sems.at[k], device_id=to, device_id_type=_MESH)

        mine = pltpu.make_async_copy(x_ref, blk(*me), local_sem)
        mine.start()
        first = [copy(0, me, sibling, src=x_ref)]
        first += [copy(1 + j, me, (*chip, mc), src=x_ref) for j, chip in enumerate(chips)]
        for cp in first:
            cp.start()
        passed = [copy(4 + j, (*chip, mc), sibling) for j, chip in enumerate(chips)]
        for j, chip in enumerate(chips):
            copy(1 + j, (*chip, mc), me).wait_recv()
            passed[j].start()
        copy(0, sibling, me).wait_recv()
        for j, chip in enumerate(chips):
            copy(4 + j, (*chip, 1 - mc), me).wait_recv()
        for cp in first + passed:
            cp.wait_send()
        mine.wait()

    return pl.pallas_call(
        body, out_shape=jax.ShapeDtypeStruct((N_DEV,) + x.shape, x.dtype), in_specs=[_ANY], out_specs=_ANY,
        scratch_shapes=[pltpu.SemaphoreType.DMA((7,)), pltpu.SemaphoreType.DMA((7,)), pltpu.SemaphoreType.DMA],
        name=name,
    )(x)


def _pair_exchange(g, name):
    def body(g_ref, out_ref, send_sems, recv_sems):
        mx, my, mc = lax.axis_index("x"), lax.axis_index("y"), lax.axis_index("c")
        copies = [pltpu.make_async_remote_copy(
            src_ref=g_ref.at[2 * k + (1 - mc)], dst_ref=out_ref.at[k], send_sem=send_sems.at[k],
            recv_sem=recv_sems.at[k], device_id=(mx, my, 1 - mc), device_id_type=_MESH) for k in range(4)]
        for cp in copies:
            cp.start()
        for cp in copies:
            cp.wait()

    return pl.pallas_call(
        body, out_shape=jax.ShapeDtypeStruct((4,) + g.shape[1:], g.dtype), in_specs=[_ANY], out_specs=_ANY,
        scratch_shapes=[pltpu.SemaphoreType.DMA((4,)), pltpu.SemaphoreType.DMA((4,))], name=name,
    )(g)


_HBM = pl.BlockSpec(memory_space=pltpu.HBM)
_SEM = pl.BlockSpec(memory_space=pltpu.SEMAPHORE)
_EFFECT = pltpu.SideEffectType.DATAFLOW_SIDE_EFFECTING
_TOKEN = jax.ShapeDtypeStruct((8, 128), F32)


def _peer(rel):
    pos = (lax.axis_index("x"), lax.axis_index("y"), lax.axis_index("c"))
    return tuple(1 - p if (rel >> (2 - i)) & 1 else p for i, p in enumerate(pos))


def _index_of(dev):
    return 4 * dev[0] + 2 * dev[1] + dev[2]


def _split_copies(src_ref, land_ref, sems, plan):
    n = len(plan)
    return [pltpu.make_async_remote_copy(
        src_ref=src_ref if s is None else src_ref.at[s], dst_ref=land_ref.at[d], send_sem=sems[k],
        recv_sem=sems[n + k], device_id=peer, device_id_type=_MESH) for k, (s, d, peer) in enumerate(plan)]


def _split_start(src, n_land, plan_fn, after, name):
    blk = src.shape[-2:]
    land = lax.empty((n_land,) + blk, src.dtype)
    n = len(plan_fn())

    def body(src_ref, land_ref, after_ref, *outs):
        for cp in _split_copies(src_ref, land_ref, outs[:2 * n], plan_fn()):
            cp.start()
        outs[2 * n + 2][...] = jnp.zeros_like(outs[2 * n + 2])

    res = pl.pallas_call(
        body, name=name,
        out_shape=(pltpu.SemaphoreType.DMA(()),) * (2 * n)
        + (pltpu.HBM(src.shape, src.dtype), pltpu.HBM(land.shape, land.dtype), _TOKEN),
        in_specs=(_HBM, _HBM, _ANY),
        out_specs=(_SEM,) * (2 * n) + (_HBM, _HBM, pl.BlockSpec(memory_space=pltpu.VMEM)),
        input_output_aliases={0: 2 * n, 1: 2 * n + 1},
        compiler_params=pltpu.CompilerParams(has_side_effects=_EFFECT),
    )(pltpu.with_memory_space_constraint(src, pltpu.HBM), pltpu.with_memory_space_constraint(land, pltpu.HBM), after)
    return res[:2 * n], res[2 * n], res[2 * n + 1], res[2 * n + 2]


def _split_wait(sems, src, land, plan_fn, after, name):
    n = len(sems) // 2

    def body(src_ref, land_ref, *rest):
        for cp in _split_copies(src_ref, land_ref, rest[:2 * n], plan_fn()):
            cp.wait_send()
            cp.wait_recv()

    return pl.pallas_call(
        body, name=name,
        out_shape=(pltpu.HBM(src.shape, src.dtype), pltpu.HBM(land.shape, land.dtype)),
        in_specs=(_HBM, _HBM) + (_SEM,) * (2 * n) + (_ANY,), out_specs=(_HBM, _HBM),
        input_output_aliases={0: 0, 1: 1},
        compiler_params=pltpu.CompilerParams(has_side_effects=_EFFECT),
    )(src, land, *sems, after)


def _gather_plan():
    me = _index_of(_peer(0))
    return [(None, me, _peer(rel)) for rel in range(1, N_DEV)]


def _gather_wait_plan():
    return [(None, _index_of(_peer(rel)), _peer(rel)) for rel in range(1, N_DEV)]


def _chip_plan():
    return [(_index_of(_peer(rel)) // 2, j, _peer(rel)) for j, rel in enumerate((4, 2, 6))]


def _owner_plan():
    return [(_index_of(_peer(rel)), rel - 1, _peer(rel)) for rel in range(1, N_DEV)]


def _pair_sum(g, t1, my_c, name, tr):
    _, r, c = g.shape

    def body(c_ref, g_ref, t_ref, o_ref, ob_ref):
        s = g_ref[...] + t_ref[...]
        o_ref[...] = s
        ob_ref[...] = s.astype(BF16)

    blk = pl.BlockSpec((None, tr, c), lambda k, i, cr: (k, i, 0))
    return pl.pallas_call(
        body,
        grid_spec=pltpu.PrefetchScalarGridSpec(
            num_scalar_prefetch=1, grid=(4, r // tr),
            in_specs=[pl.BlockSpec((None, tr, c), lambda k, i, cr: (2 * k + cr[0], i, 0)), blk],
            out_specs=[blk, blk]),
        out_shape=[jax.ShapeDtypeStruct((4, r, c), F32), jax.ShapeDtypeStruct((4, r, c), BF16)],
        compiler_params=_params(("parallel", "parallel")), name=name,
    )(my_c, g, t1)


def _adam_math(g, w, m, v):
    m = ADAM_B1 * m + (1.0 - ADAM_B1) * g
    v = ADAM_B2 * v + (1.0 - ADAM_B2) * (g * g)
    m_hat = m / (1.0 - ADAM_B1 ** ADAM_STEP)
    v_hat = v / (1.0 - ADAM_B2 ** ADAM_STEP)
    delta = -ADAM_LR * (m_hat / (jnp.sqrt(v_hat) + ADAM_EPS) + ADAM_WD * w)
    return delta, m, v


def _grad_sum(own, own_index, recv, name, tr):
    _, r, c = own.shape
    n = recv.shape[0]

    def body(k_ref, own_ref, *rest):
        g = own_ref[...]
        for recv_ref in rest[:n]:
            g = g + recv_ref[...].astype(F32)
        rest[n][...] = g

    def slot(j):
        return pl.BlockSpec((None, tr, c), lambda i, kr: (j, i, 0))

    return pl.pallas_call(
        body,
        grid_spec=pltpu.PrefetchScalarGridSpec(
            num_scalar_prefetch=1, grid=(r // tr,),
            in_specs=[pl.BlockSpec((None, tr, c), lambda i, kr: (kr[0], i, 0))] + [slot(j) for j in range(n)],
            out_specs=pl.BlockSpec((tr, c), lambda i, kr: (i, 0))),
        out_shape=jax.ShapeDtypeStruct((r, c), F32),
        compiler_params=_params(("parallel",)), name=name,
    )(own_index, own, *([recv] * n))


def _adam_many(g, w, m, v, row_tiles, name):
    n = len(g)

    def body(*refs):
        ins, outs = refs[:4 * n], refs[4 * n:]
        for i in range(n):
            res = _adam_math(ins[i][...], ins[n + i][...], ins[2 * n + i][...], ins[3 * n + i][...])
            for kind in range(3):
                outs[kind * n + i][...] = res[kind]

    def spec(a):
        blk = (a.shape[0] // row_tiles,) + a.shape[1:]
        return pl.BlockSpec(blk, lambda t, nd=a.ndim: (t,) + (0,) * (nd - 1))

    specs = [spec(a) for a in g]
    res = pl.pallas_call(
        body, grid=(row_tiles,), in_specs=specs * 4, out_specs=specs * 3,
        out_shape=[jax.ShapeDtypeStruct(a.shape, F32) for a in g] * 3,
        compiler_params=_params(("parallel",)), name=name,
    )(*g, *w, *m, *v)
    return res[:n], res[n:2 * n], res[2 * n:]


def _sum8(g8, name):
    _, r, c = g8.shape

    def body(g_ref, o_ref):
        acc = g_ref[0]
        for j in range(1, N_DEV):
            acc = acc + g_ref[j]
        o_ref[...] = acc

    return pl.pallas_call(
        body, grid=(1,), in_specs=[pl.BlockSpec((N_DEV, r, c), lambda i: (0, 0, 0))],
        out_specs=pl.BlockSpec((r, c), lambda i: (0, 0)), out_shape=jax.ShapeDtypeStruct((r, c), F32),
        compiler_params=_params(("arbitrary",)), name=name,
    )(g8)


def _pack(arrs, pad_rows=8):
    flat = jnp.concatenate([a.reshape(-1) for a in arrs])
    n = flat.shape[0]
    q = PACK_C * pad_rows
    tot = -(-n // q) * q
    if tot != n:
        flat = jnp.concatenate([flat, jnp.zeros((tot - n,), flat.dtype)])
    return flat.reshape(tot // PACK_C, PACK_C)


def _unpack(buf, shapes):
    flat = buf.reshape(-1)
    out, off = [], 0
    for s in shapes:
        n = int(np.prod(s))
        out.append(flat[off:off + n].reshape(s))
        off += n
    return out


GROUPS = (("w_in",),
          ("w_glu", "w_ssm_br", "w_mem_br", "w_attn_br"),
          ("w_up", "w_down"),
          ("w_mem_kv", "w_o"))
GROUP_TR = (400, 384, 512, 256)
MLP_GROUP = 2
MIXER_GROUPS = (1, 3)
ATTN_BR_FOLD = 2


def _stored_shape(name):
    r, c, ax = BIG_SHAPE[name]
    rows, cols = (r // N_DEV, c) if ax == 0 else (c // N_DEV, r)
    return (rows // ATTN_BR_FOLD, cols * ATTN_BR_FOLD) if name == "w_attn_br" else (rows, cols)


def _stored(shard, name):
    a = shard[0].T if BIG_SHAPE[name][2] == 1 else shard[0]
    return a.reshape(_stored_shape(name))


def _unstored(a, name):
    r, c, ax = BIG_SHAPE[name]
    if ax == 0:
        return a.reshape(1, r // N_DEV, c)
    return a.reshape(c // N_DEV, r).T[None]


def _pack_group(d, names):
    return jnp.concatenate([_stored(d[n], n) for n in names], axis=0)


def _split_group(buf, names):
    out, off = {}, 0
    for n in names:
        rows = _stored_shape(n)[0]
        out[n] = buf[..., off:off + rows, :]
        off += rows
    return out


def _full_stored(stacked, name):
    r, c, ax = BIG_SHAPE[name]
    return stacked.reshape((r, c) if ax == 0 else (c, r))


def _stacked_stored(full, name):
    return full.reshape((N_DEV,) + _stored_shape(name))


def _gelu_parts(x):
    c0, c1 = math.sqrt(2.0 / math.pi), 0.044715
    th = jnp.tanh(c0 * (x + c1 * x * x * x))
    return th, c0, c1


def _local_step(x, mem, tgt, wb, sp, late_weights, grads_ready, small_grads_ready):
    l = x.shape[0]
    w_a, w_g = wb["w_in"][:ZA_W], wb["w_in"][ZA_W:]

    a_re, a_im, bb_re, bb_im = _discretize(sp["ssm_lambda_re"], sp["ssm_lambda_im"], sp["ssm_log_dt"],
                                           sp["ssm_b_re"], sp["ssm_b_im"])
    a_pair = jnp.stack([a_re.reshape(1, SSM_S), a_im.reshape(1, SSM_S)])
    a_conj = jnp.stack([a_re.reshape(1, SSM_S), -a_im.reshape(1, SSM_S)])
    b_re_t, b_im_t = _bd_in(bb_re).astype(BF16), _bd_in(bb_im).astype(BF16)
    c_re_t = _bd_in(sp["ssm_c_re"].transpose(0, 2, 1)).astype(BF16)
    c_im_t = (-_bd_in(sp["ssm_c_im"].transpose(0, 2, 1))).astype(BF16)
    d_row = sp["ssm_d"].reshape(1, SSM_W)

    n1 = _rms_fwd(x, sp["norm1_g"], "rms1")
    za = _mm(n1, w_a, [BF16], tb=True, name="in_proj_a", tn=1664)
    zg = _mm(n1, w_g, [BF16], tb=True, name="in_proj_g")
    for gi in MIXER_GROUPS:
        wb = {**wb, **late_weights(gi, za)}
    u = za[:, :SSM_W]
    mq = za[:, ZA_W - MEM_W:]

    u_s = _scan_order(u)
    s_all = _ssm_scan(u_s, b_re_t, b_im_t, a_pair, reverse=False, name="ssm_scan_fwd")
    ys = _time_order(_mm((s_all, 0), c_re_t, [F32], tb=True, pair2=((s_all, 1), c_im_t), bd=SSM_BD, tm=2048, name="ssm_cs"))

    def gelu_fn(r, b):
        y0 = r[0] + b[0] * r[1].astype(F32)
        th, _, _ = _gelu_parts(y0)
        return [y0, 0.5 * y0 * (1.0 + th)], []
    y0, y1 = _ew(gelu_fn, [ys, u], [d_row], [(SSM_W, F32), (SSM_W, BF16)], [], name="ssm_gelu", tm=512)

    def glu_epi(acc, y1t, bg):
        t = acc + bg
        return t, y1t.astype(F32) * _sigmoid(t)
    t_glu, y2 = _mm(y1, wb["w_glu"], [F32, BF16], epi=glu_epi, mn=[y1], rows=[sp["b_glu"]], name="ssm_glu")
    br_ssm = _mm(y2, wb["w_ssm_br"], [BF16], tb=True, name="ssm_br")

    qkv_p, o_g, lse_g = [], [], []
    for g, d in enumerate(DILATIONS):
        nb = l // d // ATT_WIN
        cols = [za[:, SSM_W + (3 * j + g) * ATT_GW: SSM_W + (3 * j + g + 1) * ATT_GW] for j in range(3)]
        qp, kp, vp = [_to_perm(cc, d) for cc in cols]
        qkv_p.append((qp, kp, vp))
        og, lg = _attn_fwd(qp, kp, vp, nb, "attn_fwd%d" % g)
        o_g.append(_from_perm(og, d))
        lse_g.append(_from_perm(lg, d))

    def merge_fn(r, b):
        o0, o1, o2, l0, l1, l2 = r
        mx = jnp.maximum(jnp.maximum(l0, l1), l2)
        e0, e1, e2 = jnp.exp(l0 - mx), jnp.exp(l1 - mx), jnp.exp(l2 - mx)
        tot = e0 + e1 + e2
        return [(e0 * o0 + e1 * o1 + e2 * o2) / tot, mx + jnp.log(tot)], []
    o_att, lse_tot = _ew(merge_fn, o_g + lse_g, [], [(ATT_GW, F32), (ATT_GW, F32)], [], name="attn_merge", tm=512)
    br_attn = _mm(o_att, wb["w_attn_br"], [BF16], tb=True, name="attn_br")

    mn = _rms_fwd(mem, sp["mem_norm_g"], "rms_mem")
    kv = _mm(mn, wb["w_mem_kv"], [BF16], name="mem_kv")
    mo = _mem_fwd(mq, kv, "mem_attn_fwd")
    br_mem = _mm(mo, wb["w_mem_br"], [BF16], tb=True, name="mem_br")

    merged, h1, n2 = _gated_out_proj(zg, [br_ssm, br_attn, br_mem], sp["b_gate"], wb["w_o"], x, sp["norm2_g"],
                                     "gated_o_proj")

    def up_epi(acc):
        ra = jnp.maximum(acc, 0.0)
        return ra * ra, ra
    wm = late_weights(MLP_GROUP, n2)
    f_act, r_act = _mm(n2, wm["w_up"], [BF16, BF16], tb=True, epi=up_epi, name="mlp_up")
    def down_epi(acc, ht, tv, gf):
        hv = acc + ht
        rs = lax.rsqrt(jnp.mean(hv * hv, axis=-1, keepdims=True) + RMS_EPS)
        err = hv * rs * gf - tv
        dh, dgf = _rms_bwd_tile(hv, err * (1.0 / D_MODEL), gf)
        return dh, dgf, _colsum(err * err) * (0.5 / D_MODEL)
    dh2, d_final_g, loss_cols = _mm(f_act, wm["w_down"], [F32], epi=down_epi, mn=[h1, tgt], rows=[sp["final_g"]],
                                    n_sums=2, tk=1024, name="mlp_down")
    loss = jnp.sum(loss_cols, axis=1, keepdims=True)

    gw, gs = {}, {"final_g": d_final_g}
    d_act = _mm(dh2, wm["w_down"], [BF16], tb=True, epi=lambda acc, ra: (acc * 2.0 * ra.astype(F32),), mn=[r_act],
                name="mlp_down_dx")
    dw_down = _mm(f_act, dh2, [F32], ta=True, name="mlp_down_dw")
    dw_up = _mm(d_act, n2, [F32], ta=True, name="mlp_up_dw")
    token = grads_ready(MLP_GROUP, {"w_up": dw_up, "w_down": dw_down})
    def up_dx_epi(acc, ht, dht, g2):
        dx, dg = _rms_bwd_tile(ht, acc, g2)
        return dx + dht, dg
    dh1, gs["norm2_g"] = _mm(d_act, wm["w_up"], [F32], epi=up_dx_epi, mn=[h1, dh2],
                             rows=[sp["norm2_g"] + token[:1, :1]], n_sums=1, tk=1024, name="mlp_up_dx")
    gw["w_o"] = _mm(merged, dh1, [F32], ta=True, name="o_proj_dw")

    def gate_bwd_epi(dm, *tiles):
        dbr, dz = [], []
        for zt, bt, bias in zip(tiles[0:3], tiles[3:6], tiles[6:9]):
            gt = _sigmoid(zt.astype(F32) + bias)
            dbr.append(dm * gt)
            dz.append(dm * bt.astype(F32) * gt * (1.0 - gt))
        return (*dbr, *dz, *[_colsum(t) for t in dz])
    gate_bias = [sp["b_gate"][:, i * D_MODEL:(i + 1) * D_MODEL] for i in range(3)]
    res = _mm(dh1, wb["w_o"], [BF16] * 6, tb=True, epi=gate_bwd_epi, mn=[(zg, 0), (zg, 1), (zg, 2), br_ssm, br_attn, br_mem],
              rows=gate_bias, n_sums=3, tm=512, name="o_proj_dx")
    (dbr_ssm, dbr_attn, dbr_mem), dzg = res[0:3], res[3:6]
    gs["b_gate"] = jnp.concatenate(res[6:9], axis=1)

    gw["w_ssm_br"] = _mm(dbr_ssm, y2, [F32], ta=True, name="ssm_br_dw")
    dy2 = _mm(dbr_ssm, wb["w_ssm_br"], [F32], name="ssm_br_dx")

    def glu_bwd_fn(r, b):
        dy, y1t, tt = r
        sg = _sigmoid(tt)
        dt = dy * y1t.astype(F32) * sg * (1.0 - sg)
        return [dt, dy * sg], [_colsum(dt)]
    dt_glu, dy1a, gs["b_glu"] = _ew(glu_bwd_fn, [dy2, y1, t_glu], [], [(SSM_W, BF16), (SSM_W, F32)], [SSM_W],
                                    name="ssm_glu_bwd", tm=512)
    gw["w_glu"] = _mm(y1, dt_glu, [F32], ta=True, name="ssm_glu_dw")

    def gelu_bwd_epi(acc, dy1t, y0t):
        th, c0, c1 = _gelu_parts(y0t)
        dg = 0.5 * (1.0 + th) + 0.5 * y0t * (1.0 - th * th) * c0 * (1.0 + 3.0 * c1 * y0t * y0t)
        return ((acc + dy1t) * dg,)
    dy0 = _mm(dt_glu, wb["w_glu"], [F32], tb=True, epi=gelu_bwd_epi, mn=[dy1a, y0], name="ssm_glu_dx")
    gs["ssm_d"] = _ew(lambda r, b: ([], [_colsum(r[0] * r[1].astype(F32))]), [dy0, u], [], [], [SSM_W],
                      name="ssm_dd", tm=512)[0]
    dy0_s = _scan_order(dy0)
    lam, da, d_b, d_c = _ssm_scan(dy0_s, c_re_t, c_im_t, a_conj, reverse=True, s_fwd=s_all, u=u_s,
                                  name="ssm_scan_bwd")
    du = _time_order(_mm((lam, 0), b_re_t, [BF16], tb=True, pair2=((lam, 1), b_im_t),
                         epi=lambda acc, dyt, dr: (acc + dyt * dr,), mn=[dy0_s], rows=[d_row], bd=SSM_BD, tm=2048, name="ssm_bu_dx"))
    gs["a_re"], gs["a_im"] = da[0], da[1]
    gs["bb_re"], gs["bb_im"] = _bd_diag(d_b[0]).transpose(0, 2, 1), _bd_diag(d_b[1]).transpose(0, 2, 1)
    gs["ssm_c_re"], gs["ssm_c_im"] = _bd_diag(d_c[0]), -_bd_diag(d_c[1])

    gw["w_attn_br"] = _mm(dbr_attn, o_att, [F32], ta=True, name="attn_br_dw")

    def do_epi(acc, ot):
        prod = acc * ot
        head = lax.broadcasted_iota(jnp.int32, prod.shape, 1) // ATT_E
        dd = jnp.zeros_like(prod)
        for h in range(ATT_HG):
            dd = jnp.where(head == h, jnp.sum(jnp.where(head == h, prod, 0.0), axis=1, keepdims=True), dd)
        return acc, dd
    do_att, dd_att = _mm(dbr_attn, wb["w_attn_br"], [BF16, F32], epi=do_epi, mn=[o_att], name="attn_br_dx")
    dq_l, dk_l, dv_l = [], [], []
    for g, d in enumerate(DILATIONS):
        nb = l // d // ATT_WIN
        qp, kp, vp = qkv_p[g]
        dq, dk, dv = _attn_bwd(qp, kp, vp, _to_perm(do_att, d), _to_perm(lse_tot, d), _to_perm(dd_att, d),
                               nb, "attn_bwd%d" % g)
        dq_l.append(_from_perm(dq, d))
        dk_l.append(_from_perm(dk, d))
        dv_l.append(_from_perm(dv, d))

    gw["w_mem_br"] = _mm(dbr_mem, mo, [F32], ta=True, name="mem_br_dw")
    dmo = _mm(dbr_mem, wb["w_mem_br"], [BF16], name="mem_br_dx")
    dmq, dkv = _mem_bwd(mq, kv, dmo, "mem_attn_bwd")
    gw["w_mem_kv"] = _mm(mn, dkv, [F32], ta=True, name="mem_kv_dw")
    dmn = _mm(dkv, wb["w_mem_kv"], [F32], tb=True, name="mem_kv_dx")
    token = sum(grads_ready(gi, gw) for gi in MIXER_GROUPS)
    gs["mem_norm_g"] = _rms_bwd(mem, dmn, None, sp["mem_norm_g"] + token[:1, :1], "rms_mem_bwd")[1]

    dza = jnp.concatenate([du] + dq_l + dk_l + dv_l + [dmq], axis=1)
    dn_a = _mm(dza, w_a, [F32], name="in_proj_a_dx", tk=1664)
    dw_a = _mm(dza, n1, [F32], ta=True, name="in_proj_a_dw", tm=1664)
    dw_g = [_mm(dzg[i], n1, [F32], ta=True, name="in_proj_g_dw%d" % i) for i in range(3)]
    gw["w_in"] = jnp.concatenate([dw_a] + dw_g, axis=0)
    token = grads_ready(0, gw) + small_grads_ready(gs)
    def in_dx_epi(acc, pt, xt, dht, g1):
        dx, dg = _rms_bwd_tile(xt, acc + pt, g1)
        return dx + dht, dg
    w_gs = [w_g[i * D_MODEL:(i + 1) * D_MODEL] for i in range(3)]
    grad_x, gs["norm1_g"] = _mm(dzg[0], w_gs[0], [F32], pair2=(dzg[1], w_gs[1], dzg[2], w_gs[2]), epi=in_dx_epi,
                                mn=[dn_a, x, dh1],
                                rows=[sp["norm1_g"] + token[:1, :1]], n_sums=1, tm=512, name="in_proj_g_dx")
    return loss, grad_x, gs


_SMALL_GRAD_ORDER = ("norm1_g", "mem_norm_g", "b_gate", "a_re", "a_im", "bb_re", "bb_im", "ssm_c_re", "ssm_c_im",
                     "ssm_d", "b_glu", "norm2_g", "final_g")


def kernel(x, mem, norm1_g, mem_norm_g, w_in, b_gate, ssm_lambda_re, ssm_lambda_im, ssm_log_dt, ssm_b_re, ssm_b_im, ssm_c_re, ssm_c_im, ssm_d, w_glu, b_glu, w_ssm_br, w_attn_br, w_mem_kv, w_mem_br, w_o, norm2_g, w_up, w_down, final_g, loss_target, m_norm1_g, m_mem_norm_g, m_w_in, m_b_gate, m_ssm_lambda_re, m_ssm_lambda_im, m_ssm_log_dt, m_ssm_b_re, m_ssm_b_im, m_ssm_c_re, m_ssm_c_im, m_ssm_d, m_w_glu, m_b_glu, m_w_ssm_br, m_w_attn_br, m_w_mem_kv, m_w_mem_br, m_w_o, m_norm2_g, m_w_up, m_w_down, m_final_g, v_norm1_g, v_mem_norm_g, v_w_in, v_b_gate, v_ssm_lambda_re, v_ssm_lambda_im, v_ssm_log_dt, v_ssm_b_re, v_ssm_b_im, v_ssm_c_re, v_ssm_c_im, v_ssm_d, v_w_glu, v_b_glu, v_w_ssm_br, v_w_attn_br, v_w_mem_kv, v_w_mem_br, v_w_o, v_norm2_g, v_w_up, v_w_down, v_final_g):
    args = dict(locals())
    w = {n: args[n] for n in ALL_W}
    m = {n: args["m_" + n] for n in ALL_W}
    v = {n: args["v_" + n] for n in ALL_W}
    my_c = lax.axis_index("c").astype(jnp.int32).reshape(1)
    my_chip = (2 * lax.axis_index("x") + lax.axis_index("y")).astype(jnp.int32).reshape(1)

    w_pack = [_pack_group(w, names) for names in GROUPS]
    my_index = (4 * lax.axis_index("x") + 2 * lax.axis_index("y") + lax.axis_index("c")).astype(jnp.int32)
    zero = jnp.zeros((), jnp.int32)
    w_all = _allgather(w_pack[0].astype(BF16), "allgather_weights0")
    wb = {n: _full_stored(part, n) for n, part in _split_group(w_all, GROUPS[0]).items()}
    gathers = {gi: _split_start(w_pack[gi].astype(BF16), N_DEV, _gather_plan, w_all, "weights_gather_start%d" % gi)
               for gi in range(1, len(GROUPS))}

    def gathered(started, after, name):
        sems, src, land, _ = started
        src, land = _split_wait(sems, src, land, _gather_wait_plan, after, name)
        return lax.dynamic_update_slice(land, src[None], (my_index, zero, zero))

    def late_weights(gi, after):
        full = gathered(gathers[gi], after, "weights_gather_wait%d" % gi)
        return {n: _full_stored(part, n) for n, part in _split_group(full, GROUPS[gi]).items()}

    pending = {}

    def grads_ready(gi, grads):
        g_pack = jnp.concatenate([_stacked_stored(grads[n], n) for n in GROUPS[gi]], axis=1)
        if gi == 0:
            t1 = _pair_exchange(g_pack, "grad_pair_exchange%d" % gi)
            p_sum, p_bf = _pair_sum(g_pack, t1, my_c, "grad_pair_sum%d" % gi, GROUP_TR[gi])
            started = _split_start(p_bf, 3, _chip_plan, p_sum, "grad_chip_exchange_start%d" % gi)
            pending[gi] = (p_sum, my_chip, started, _chip_plan)
        else:
            started = _split_start(g_pack.astype(BF16), N_DEV - 1, _owner_plan, g_pack, "grad_exchange_start%d" % gi)
            pending[gi] = (g_pack, my_index.reshape(1), started, _owner_plan)
        return started[3]

    early_small = [n for n in _SMALL_GRAD_ORDER if n != "norm1_g"]
    small_started = []

    def small_grads_ready(gs):
        started = _split_start(_pack([gs[n] for n in early_small]), N_DEV, _gather_plan, gs["mem_norm_g"],
                               "small_grads_gather_start")
        small_started.append((started, [gs[n].shape for n in early_small]))
        return started[3]

    sp = {
        "norm1_g": norm1_g + sum(started[3][:1, :1] for started in gathers.values()), "mem_norm_g": mem_norm_g, "b_gate": b_gate, "b_glu": b_glu, "norm2_g": norm2_g,
        "final_g": final_g.reshape(1, D_MODEL),
        "ssm_lambda_re": ssm_lambda_re[0], "ssm_lambda_im": ssm_lambda_im[0], "ssm_log_dt": ssm_log_dt[0],
        "ssm_b_re": ssm_b_re[0], "ssm_b_im": ssm_b_im[0], "ssm_c_re": ssm_c_re[0], "ssm_c_im": ssm_c_im[0],
        "ssm_d": ssm_d[0],
    }
    loss, grad_x, gs = _local_step(x[0], mem[0], loss_target[0], wb, sp, late_weights, grads_ready,
                                     small_grads_ready)
    loss = lax.psum(loss[0, 0], ("x", "y", "c"))
    n1_started = _split_start(_pack([gs["norm1_g"]]), N_DEV, _gather_plan, grad_x, "norm1_grad_gather_start")

    big_g = {}
    for gi, names in enumerate(GROUPS):
        own, own_index, (sems, src, land, _), plan = pending[gi]
        recv = _split_wait(sems, src, land, plan, grad_x, "grad_exchange_wait%d" % gi)[1]
        g_pack = _grad_sum(own, own_index, recv, "grad_sum%d" % gi, GROUP_TR[gi])
        for n, part in _split_group(g_pack, names).items():
            big_g[n] = _unstored(part, n)
    rows_of = lambda d, names: [d[n].reshape(d[n].shape[-2:]) for n in names]
    big_out = _adam_many(rows_of(big_g, BIG), rows_of(w, BIG), rows_of(m, BIG), rows_of(v, BIG), 8, "adam_big")
    big = [big_g] + [{n: a[None] for n, a in zip(BIG, outs)} for outs in big_out]

    (sg_started, sg_shapes), = small_started
    sg_all = jnp.concatenate([gathered(sg_started, big_out[0][0], "small_grads_gather_wait"),
                              gathered(n1_started, big_out[0][0], "norm1_grad_gather_wait")], axis=1)
    sg_sum = _sum8(sg_all, "sum_small_grads")
    n1_rows = n1_started[1].shape[0]
    sg = dict(zip(early_small, _unpack(sg_sum[:-n1_rows], sg_shapes)))
    sg["norm1_g"] = _unpack(sg_sum[-n1_rows:], [gs["norm1_g"].shape])[0]
    _, disc_vjp = jax.vjp(_discretize, sp["ssm_lambda_re"], sp["ssm_lambda_im"], sp["ssm_log_dt"],
                          sp["ssm_b_re"], sp["ssm_b_im"])
    d_lre, d_lim, d_ldt, d_bre, d_bim = disc_vjp((sg["a_re"].reshape(SSM_G, SSM_P), sg["a_im"].reshape(SSM_G, SSM_P),
                                                  sg["bb_re"], sg["bb_im"]))
    small_grad = {
        "norm1_g": sg["norm1_g"], "mem_norm_g": sg["mem_norm_g"], "b_gate": sg["b_gate"],
        "ssm_lambda_re": d_lre, "ssm_lambda_im": d_lim, "ssm_log_dt": d_ldt, "ssm_b_re": d_bre, "ssm_b_im": d_bim,
        "ssm_c_re": sg["ssm_c_re"], "ssm_c_im": sg["ssm_c_im"], "ssm_d": sg["ssm_d"], "b_glu": sg["b_glu"],
        "norm2_g": sg["norm2_g"], "final_g": sg["final_g"],
    }
    small_grad = {n: small_grad[n].reshape(w[n].shape) for n in SMALL}

    def squeezed(a):
        return a.reshape(a.shape[1:]) if a.ndim > 2 else a.reshape(1, -1)

    sq = lambda d: [squeezed(d[n]) for n in SMALL]
    small_out = _adam_many(sq(small_grad), sq(w), sq(m), sq(v), 1, "adam_small")
    small = [small_grad] + [{n: a.reshape(w[n].shape) for n, a in zip(SMALL, outs)} for outs in small_out]

    outs = [loss, grad_x[None]]
    for kind in range(4):
        for n in ALL_W:
            outs.append(big[kind][n] if n in BIG else small[kind][n])
    return tuple(outs)
```

```python
import math

import numpy as np
import jax
import jax.numpy as jnp
from jax import lax
from jax.experimental import pallas as pl
from jax.experimental.pallas import tpu as pltpu

F32 = jnp.float32
BF16 = jnp.bfloat16
_MXU = jnp.bfloat16

D_MODEL = 1024
SSM_G, SSM_H, SSM_P = 32, 16, 64
SSM_W = SSM_G * SSM_H
SSM_S = SSM_G * SSM_P
SSM_BD = 4
ATT_E = 64
ATT_HG = 4
ATT_GW = ATT_HG * ATT_E
ATT_WIN = 128
ATT_QB = 8
ATT_QB_FWD = 4
DILATIONS = (1, 4, 16)
MEM_H, MEM_E = 4, 128
MEM_W = MEM_H * MEM_E
ZA_W = SSM_W + 9 * ATT_GW + MEM_W
ZG_W = 3 * D_MODEL
IN_W = ZA_W + ZG_W
RMS_EPS = 1e-6
NEG_INF = -1e30

ADAM_LR, ADAM_B1, ADAM_B2, ADAM_EPS, ADAM_WD, ADAM_STEP = 0.001, 0.9, 0.999, 1e-08, 0.01, 10

N_DEV = 8
PACK_C = 512
_VMEM_LIMIT = 56 * 1024 * 1024
SUBLANES = 16
SCAN_SEG = 128
SCAN_CHAINS = 4
SCAN_UNROLL = 4
SCAN_W = 128

BIG = ("w_in", "w_glu", "w_ssm_br", "w_attn_br", "w_mem_kv", "w_mem_br", "w_o", "w_up", "w_down")
BIG_SHAPE = {
    "w_in": (D_MODEL, IN_W, 1), "w_glu": (SSM_W, SSM_W, 0), "w_ssm_br": (SSM_W, D_MODEL, 1),
    "w_attn_br": (ATT_GW, D_MODEL, 1), "w_mem_kv": (D_MODEL, 2 * MEM_W, 0), "w_mem_br": (MEM_W, D_MODEL, 1),
    "w_o": (D_MODEL, D_MODEL, 0), "w_up": (D_MODEL, 4 * D_MODEL, 1), "w_down": (4 * D_MODEL, D_MODEL, 0),
}
SMALL = ("norm1_g", "mem_norm_g", "b_gate", "ssm_lambda_re", "ssm_lambda_im", "ssm_log_dt", "ssm_b_re",
         "ssm_b_im", "ssm_c_re", "ssm_c_im", "ssm_d", "b_glu", "norm2_g", "final_g")
ALL_W = ("norm1_g", "mem_norm_g", "w_in", "b_gate", "ssm_lambda_re", "ssm_lambda_im", "ssm_log_dt", "ssm_b_re",
         "ssm_b_im", "ssm_c_re", "ssm_c_im", "ssm_d", "w_glu", "b_glu", "w_ssm_br", "w_attn_br", "w_mem_kv",
         "w_mem_br", "w_o", "norm2_g", "w_up", "w_down", "final_g")


def _params(sem):
    return pltpu.CompilerParams(dimension_semantics=sem, vmem_limit_bytes=_VMEM_LIMIT)


def _pick(n, cap):
    if n <= cap:
        return n
    t = (cap // 128) * 128
    while n % t:
        t -= 128
    return t


def _mm(a, b, outs, *, name, ta=False, tb=False, epi=None, mn=(), rows=(), pair2=None, bd=0, n_sums=0,
        tm=1024, tn=1024, tk=2048):
    ab = [a, b] + (list(pair2) if pair2 is not None else [])
    planes = [op[1] if isinstance(op, tuple) else None for op in ab]
    ab = [op[0] if isinstance(op, tuple) else op for op in ab]
    a_shape, b_shape = ab[0].shape[-2:], ab[1].shape[-2:]
    m = a_shape[1] if ta else a_shape[0]
    k = a_shape[0] if ta else a_shape[1]
    n = b_shape[0] if tb else b_shape[1]
    assert k == (b_shape[1] if tb else b_shape[0]), (name, a_shape, b_shape)
    out_n = n
    if bd and ta:
        assert not tb
        tm, tn, tk = m // bd, n // bd, _pick(k, tk)
        grid, out_n = (bd, 1, k // tk), tn
        a_blk = ((tk, tm), lambda i, j, kk: (kk, i))
        b_blk = ((tk, tn), lambda i, j, kk: (kk, i))
        mn_spec = pl.BlockSpec((tm, tn), lambda i, j, kk: (i, 0))
    elif bd:
        tm, tn, tk = _pick(m, tm), n // bd, k // bd
        grid = (m // tm, bd, 1)
        a_blk = ((tm, tk), lambda i, j, kk: (i, j))
        b_blk = ((tn, tk) if tb else (tk, tn), lambda i, j, kk: (j, j))
        mn_spec = pl.BlockSpec((tm, tn), lambda i, j, kk: (i, j))
    else:
        tm, tn, tk = _pick(m, tm), _pick(n, tn), _pick(k, tk)
        grid = (m // tm, n // tn, k // tk)
        a_blk = ((tk, tm), lambda i, j, kk: (kk, i)) if ta else ((tm, tk), lambda i, j, kk: (i, kk))
        b_blk = ((tn, tk), lambda i, j, kk: (j, kk)) if tb else ((tk, tn), lambda i, j, kk: (kk, j))
        mn_spec = pl.BlockSpec((tm, tn), lambda i, j, kk: (i, j))

    def operand_spec(blk, plane):
        shape, imap = blk
        if plane is None:
            return pl.BlockSpec(shape, imap)
        return pl.BlockSpec((None,) + shape, lambda i, j, kk: (plane,) + imap(i, j, kk))

    ab_specs = [operand_spec(a_blk if q % 2 == 0 else b_blk, p) for q, p in enumerate(planes)]
    mn_arrays = [e[0] if isinstance(e, tuple) else e for e in mn]
    mn_specs = [pl.BlockSpec((tm, tn), lambda i, j, kk, c=e[1]: (i, c)) if isinstance(e, tuple) else mn_spec
                for e in mn]
    nk = grid[2]
    row_spec = pl.BlockSpec((1, tn), lambda i, j, kk: (0, j))
    n_ex, n_out = len(mn) + len(rows), len(outs)
    assert n_sums == 0 or (grid[1] == 1 and not bd)
    dims = (((0 if ta else 1,), (1 if tb else 0,)), ((), ()))

    def body(*refs):
        ab_refs, rest = refs[:len(ab)], refs[len(ab):]
        ex, o_refs, acc = rest[:n_ex], rest[n_ex:n_ex + n_out], rest[-1]
        s_refs = rest[n_ex + n_out:n_ex + n_out + n_sums]
        first_row_tile = pl.program_id(0) == 0
        kk = pl.program_id(2)

        @pl.when(kk == 0)
        def _():
            acc[...] = jnp.zeros_like(acc)

        for a_ref, b_ref in zip(ab_refs[0::2], ab_refs[1::2]):
            acc[...] += lax.dot_general(a_ref[...].astype(_MXU), b_ref[...].astype(_MXU), dims,
                                        preferred_element_type=F32)

        @pl.when(kk == nk - 1)
        def _():
            vals = (acc[...],) if epi is None else epi(acc[...], *[r[...] for r in ex])
            for r, v in zip(o_refs, vals):
                r[...] = v.astype(r.dtype)
            for r, v in zip(s_refs, vals[n_out:]):
                r[...] = jnp.where(first_row_tile, v, r[...] + v)

    res = pl.pallas_call(
        body, grid=grid,
        in_specs=ab_specs + mn_specs + [row_spec] * len(rows),
        out_specs=[mn_spec] * n_out + [row_spec] * n_sums,
        out_shape=[jax.ShapeDtypeStruct((m, out_n), dt) for dt in outs]
        + [jax.ShapeDtypeStruct((1, out_n), F32)] * n_sums,
        scratch_shapes=[pltpu.VMEM((tm, tn), F32)],
        compiler_params=_params(("arbitrary" if n_sums else "parallel", "parallel", "arbitrary")), name=name,
    )(*ab, *mn_arrays, *rows)
    return res[0] if n_out + n_sums == 1 else res


def _ew(fn, rows, bcs, out_rows, out_accs, *, name, tm=256):
    r = rows[0].shape[0]
    tm = min(tm, r)
    assert r % tm == 0
    nr, nb, no, na = len(rows), len(bcs), len(out_rows), len(out_accs)

    def body(*refs):
        i = pl.program_id(0)
        r_in, b_in = refs[:nr], refs[nr:nr + nb]
        o_r, o_a = refs[nr + nb:nr + nb + no], refs[nr + nb + no:]
        outs, accs = fn([x[...] for x in r_in], [x[...] for x in b_in])
        for ref, v in zip(o_r, outs):
            ref[...] = v.astype(ref.dtype)
        if na:
            @pl.when(i == 0)
            def _():
                for ref in o_a:
                    ref[...] = jnp.zeros_like(ref)

            for ref, v in zip(o_a, accs):
                ref[...] += v

    res = pl.pallas_call(
        body, grid=(r // tm,),
        in_specs=[pl.BlockSpec((tm, x.shape[1]), lambda i: (i, 0)) for x in rows]
        + [pl.BlockSpec((1, x.shape[1]), lambda i: (0, 0)) for x in bcs],
        out_specs=[pl.BlockSpec((tm, c), lambda i: (i, 0)) for c, _ in out_rows]
        + [pl.BlockSpec((1, c), lambda i: (0, 0)) for c in out_accs],
        out_shape=[jax.ShapeDtypeStruct((r, c), dt) for c, dt in out_rows]
        + [jax.ShapeDtypeStruct((1, c), F32) for c in out_accs],
        compiler_params=_params(("arbitrary",)), name=name,
    )(*rows, *bcs)
    return res


def _colsum(x):
    return jnp.sum(x, axis=0, keepdims=True)


def _sigmoid(x):
    return 1.0 / (1.0 + jnp.exp(-x))


def _rms_bwd_tile(xv, dv, g):
    rs = lax.rsqrt(jnp.mean(xv * xv, axis=-1, keepdims=True) + RMS_EPS)
    gd = dv * g
    dx = rs * gd - xv * (rs * rs * rs) * jnp.mean(gd * xv, axis=-1, keepdims=True)
    return dx, _colsum(dv * xv * rs)


def _rms_fwd(x, g, name):
    def fn(r, b):
        xv = r[0]
        rs = lax.rsqrt(jnp.mean(xv * xv, axis=-1, keepdims=True) + RMS_EPS)
        return [xv * rs * b[0]], []
    return _ew(fn, [x], [g], [(x.shape[1], BF16)], [], name=name)[0]


def _rms_bwd(x, dn, res, g, name):
    def fn(r, b):
        dx, dg = _rms_bwd_tile(r[0], r[1], b[0])
        if res is not None:
            dx = dx + r[2]
        return [dx], [dg]
    rows = [x, dn] + ([res] if res is not None else [])
    return _ew(fn, rows, [g], [(x.shape[1], F32)], [x.shape[1]], name=name)


def _scan_order(x):
    l, c = x.shape
    return x.reshape(l // (SUBLANES * SCAN_SEG), SUBLANES, SCAN_SEG, c).transpose(0, 2, 1, 3).reshape(l, c)


def _time_order(x):
    l, c = x.shape
    return x.reshape(l // (SUBLANES * SCAN_SEG), SCAN_SEG, SUBLANES, c).transpose(0, 2, 1, 3).reshape(l, c)


def _ssm_scan(x, w_re, w_im, a_pair, *, reverse, s_fwd=None, u=None, name):
    l = x.shape[0]
    seg, w = SCAN_SEG, SCAN_W
    bd_w = SSM_W // SSM_BD
    tiles_per_bd = SSM_S // SSM_BD // w
    nch = min(SCAN_CHAINS, l // (SUBLANES * seg))
    chain_rows = SUBLANES * seg
    tb = nch * chain_rows
    nt = l // tb
    with_da = s_fwd is not None
    assert reverse or not with_da

    def tt(t):
        return nt - 1 - t if reverse else t

    def body(*refs):
        if with_da:
            (x_ref, wr_ref, wi_ref, a_ref, sf_ref, sp_ref, u_ref, s_ref, da_ref, dw_ref, dx_ref,
             p_ref, c_ref, b_scr) = refs
        else:
            x_ref, wr_ref, wi_ref, a_ref, s_ref, p_ref, c_ref, b_scr = refs
        t_blk = pl.program_id(1)
        ar, ai = a_ref[0], a_ref[1]

        @pl.when(t_blk == 0)
        def _():
            def pstep(i, carry):
                pr, pi = carry
                p_ref[0, pl.ds(i, 1), :] = pr
                p_ref[1, pl.ds(i, 1), :] = pi
                return pr * ar - pi * ai, pr * ai + pi * ar

            lax.fori_loop(0, seg, pstep, (ar, ai))
            c_ref[...] = jnp.zeros_like(c_ref)
            if with_da:
                da_ref[...] = jnp.zeros_like(da_ref)
                dw_ref[...] = jnp.zeros_like(dw_ref)
                dx_ref[...] = jnp.zeros_like(dx_ref)

        xb = x_ref[...].astype(_MXU)
        b_scr[:, :w] = jnp.dot(xb, wr_ref[...], preferred_element_type=F32)
        b_scr[:, w:] = jnp.dot(xb, wi_ref[...], preferred_element_type=F32)
        arb, aib = jnp.broadcast_to(ar, (SUBLANES, w)), jnp.broadcast_to(ai, (SUBLANES, w))
        zero = jnp.zeros((SUBLANES, w), F32)

        def tile(g, step):
            return pl.ds(pl.multiple_of(g * chain_rows + step * SUBLANES, SUBLANES), SUBLANES)

        def rows(g, i):
            return tile(g, seg - 1 - i if reverse else i)

        def local_step(i, carry):
            out = []
            for g in range(nch):
                sr, si = carry[2 * g], carry[2 * g + 1]
                idx = rows(g, i)
                sr, si = arb * sr - aib * si + b_scr[idx, :w], arb * si + aib * sr + b_scr[idx, w:]
                b_scr[idx, :w] = sr
                b_scr[idx, w:] = si
                out += [sr, si]
            return tuple(out)

        def unrolled(step_fn, first):
            def trip(q, carry):
                for r in range(SCAN_UNROLL):
                    carry = step_fn(first + q * SCAN_UNROLL + r, carry)
                return carry
            return trip

        ends = lax.fori_loop(0, seg // SCAN_UNROLL, unrolled(local_step, 0), (zero,) * (2 * nch))

        a_seg_r, a_seg_i = p_ref[0, seg - 1:seg, :], p_ref[1, seg - 1:seg, :]
        cr, ci = c_ref[0], c_ref[1]
        sub = lax.broadcasted_iota(jnp.int32, (SUBLANES, w), 0)
        ins = [[zero, zero] for _ in range(nch)]
        order = [(g, k) for g in range(nch) for k in range(SUBLANES)]
        for g, k in (order[::-1] if reverse else order):
            ins[g] = [jnp.where(sub == k, cr, ins[g][0]), jnp.where(sub == k, ci, ins[g][1])]
            er, ei = ends[2 * g][k:k + 1], ends[2 * g + 1][k:k + 1]
            cr, ci = er + a_seg_r * cr - a_seg_i * ci, ei + a_seg_r * ci + a_seg_i * cr
        c_ref[0] = cr
        c_ref[1] = ci

        def fix(g, i):
            idx = rows(g, i)
            pr, pi = p_ref[0, pl.ds(i, 1), :], p_ref[1, pl.ds(i, 1), :]
            sr = b_scr[idx, :w] + pr * ins[g][0] - pi * ins[g][1]
            si = b_scr[idx, w:] + pr * ins[g][1] + pi * ins[g][0]
            s_ref.at[0][idx, :] = sr.astype(s_ref.dtype)
            s_ref.at[1][idx, :] = si.astype(s_ref.dtype)
            return sr, si

        if not with_da:
            def fix_step(i, carry):
                for g in range(nch):
                    fix(g, i)
                return carry

            lax.fori_loop(0, seg // SCAN_UNROLL, unrolled(fix_step, 0), 0)
        else:
            def adj_step(i, acc):
                acc_r, acc_i = acc
                for g in range(nch):
                    lr, li = fix(g, i)
                    prev = tile(g, seg - 2 - i)
                    fr, fi = sf_ref.at[0][prev, :].astype(F32), sf_ref.at[1][prev, :].astype(F32)
                    acc_r, acc_i = acc_r + lr * fr + li * fi, acc_i + li * fr - lr * fi
                return acc_r, acc_i

            acc = lax.fori_loop(0, seg // SCAN_UNROLL - 1, unrolled(adj_step, 0), (zero, zero))
            for i in range(seg - SCAN_UNROLL, seg - 1):
                acc = adj_step(i, acc)
            acc_r, acc_i = acc
            first_block = tt(t_blk) == 0
            for g in range(nch):
                lr, li = fix(g, seg - 1)
                seg_ends = tile(g, seg - 1)
                if g == 0:
                    pvr = jnp.where(first_block, 0.0, sp_ref[0, SUBLANES - 1:SUBLANES, :].astype(F32))
                    pvi = jnp.where(first_block, 0.0, sp_ref[1, SUBLANES - 1:SUBLANES, :].astype(F32))
                else:
                    pvr = sf_ref[0, g * chain_rows - 1:g * chain_rows, :].astype(F32)
                    pvi = sf_ref[1, g * chain_rows - 1:g * chain_rows, :].astype(F32)
                fr = jnp.where(sub == 0, pvr, pltpu.roll(sf_ref.at[0][seg_ends, :].astype(F32), 1, 0))
                fi = jnp.where(sub == 0, pvi, pltpu.roll(sf_ref.at[1][seg_ends, :].astype(F32), 1, 0))
                acc_r = acc_r + lr * fr + li * fi
                acc_i = acc_i + li * fr - lr * fi
            da_ref[0] += jnp.sum(acc_r, axis=0, keepdims=True)
            da_ref[1] += jnp.sum(acc_i, axis=0, keepdims=True)
            for plane in range(2):
                dw_ref[plane] += _tn_dot(u_ref[...], s_ref[plane])
                dx_ref[plane] += _tn_dot(xb, sf_ref[plane])

    x_spec = pl.BlockSpec((tb, bd_w), lambda j, t: (tt(t), j // tiles_per_bd))
    w_spec = pl.BlockSpec((bd_w, w), lambda j, t: (j // tiles_per_bd, j))
    d_spec = pl.BlockSpec((2, bd_w, w), lambda j, t: (0, j // tiles_per_bd, j % tiles_per_bd))
    a_spec = pl.BlockSpec((2, 1, w), lambda j, t: (0, 0, j))
    s_spec = pl.BlockSpec((2, tb, w), lambda j, t: (0, tt(t), j))
    in_specs, args = [x_spec, w_spec, w_spec, a_spec], [x, w_re, w_im, a_pair]
    out_specs, out_shape = [s_spec], [jax.ShapeDtypeStruct((2, l, SSM_S), BF16)]
    scratch = [pltpu.VMEM((2, seg, w), F32), pltpu.VMEM((2, 1, w), F32), pltpu.VMEM((tb, 2 * w), F32)]
    if with_da:
        in_specs += [s_spec, pl.BlockSpec((2, SUBLANES, w),
                                          lambda j, t: (0, jnp.maximum(tt(t) * (tb // SUBLANES) - 1, 0), j)),
                     x_spec]
        args += [s_fwd, s_fwd, u]
        out_specs += [a_spec, d_spec, d_spec]
        out_shape += ([jax.ShapeDtypeStruct((2, 1, SSM_S), F32)]
                      + [jax.ShapeDtypeStruct((2, SSM_W, SSM_S // SSM_BD), F32)] * 2)
    res = pl.pallas_call(
        body, grid=(SSM_S // w, nt), in_specs=in_specs, out_specs=out_specs, out_shape=out_shape,
        scratch_shapes=scratch, compiler_params=_params(("parallel", "arbitrary")), name=name,
    )(*args)
    return res if with_da else res[0]


def _nt_dot(x, y):
    return lax.dot_general(x.astype(_MXU), y.astype(_MXU), (((1,), (1,)), ((), ())), preferred_element_type=F32)


def _tn_dot(x, y):
    return lax.dot_general(x.astype(_MXU), y.astype(_MXU), (((0,), (0,)), ((), ())), preferred_element_type=F32)


def _nn_dot(x, y):
    return jnp.dot(x.astype(_MXU), y.astype(_MXU), preferred_element_type=F32)


def _attn_mask2(gb, nb):
    qi = lax.broadcasted_iota(jnp.int32, (ATT_WIN, 2 * ATT_WIN), 0)
    c = lax.broadcasted_iota(jnp.int32, (ATT_WIN, 2 * ATT_WIN), 1)
    has_prev = (gb % nb) != 0
    prev_ok = jnp.logical_and(jnp.logical_and(c < ATT_WIN, c >= qi), has_prev)
    own_ok = jnp.logical_and(c >= ATT_WIN, c - ATT_WIN <= qi)
    return jnp.logical_or(prev_ok, own_ok)


def _attn_specs(qb):
    cur = pl.BlockSpec((qb * ATT_WIN, ATT_GW), lambda i: (i, 0))
    prev = pl.BlockSpec((ATT_WIN, ATT_GW), lambda i: (jnp.maximum(qb * i - 1, 0), 0))
    return cur, prev


def _attn_fwd(q, k, v, nb, name):
    l = q.shape[0]
    scale = ATT_E ** -0.5
    w = ATT_WIN

    qb = ATT_QB_FWD

    def body(q_ref, kc_ref, kp_ref, vc_ref, vp_ref, o_ref, lse_ref):
        i = pl.program_id(0)
        masks = [_attn_mask2(qb * i + b, nb) for b in range(qb)]
        for h in range(ATT_HG):
            sl = slice(h * ATT_E, (h + 1) * ATT_E)
            k_ext = jnp.concatenate([kp_ref[:, sl], kc_ref[:, sl]], axis=0)
            v_ext = jnp.concatenate([vp_ref[:, sl], vc_ref[:, sl]], axis=0)
            for b in range(qb):
                r, kr = slice(b * w, (b + 1) * w), slice(b * w, (b + 2) * w)
                s = jnp.where(masks[b], _nt_dot(q_ref[r, sl], k_ext[kr]) * scale, NEG_INF)
                mx = jnp.max(s, axis=-1, keepdims=True)
                p = jnp.exp(s - mx)
                den = jnp.sum(p, axis=-1, keepdims=True)
                o_ref[r, sl] = _nn_dot(p, v_ext[kr]) / den
                lse_ref[r, sl] = jnp.broadcast_to(mx + jnp.log(den), (w, ATT_E))

    cur, prev = _attn_specs(qb)
    return pl.pallas_call(
        body, grid=(l // (qb * w),), in_specs=[cur, cur, prev, cur, prev], out_specs=[cur, cur],
        out_shape=[jax.ShapeDtypeStruct((l, ATT_GW), F32)] * 2,
        compiler_params=_params(("parallel",)), name=name,
    )(q, k, k, v, v)


def _attn_bwd(q, k, v, do, lse, dd, nb, name):
    l = q.shape[0]
    scale = ATT_E ** -0.5
    w = ATT_WIN
    nblk = l // w

    def body(q_ref, kc_ref, kp_ref, vc_ref, vp_ref, do_ref, lse_ref, dd_ref, qn_ref, don_ref, lsen_ref, ddn_ref,
             dq_ref, dk_ref, dv_ref, dk_acc, dv_acc):
        i = pl.program_id(0)
        masks = [_attn_mask2(ATT_QB * i + b, nb) for b in range(ATT_QB)]
        nxt = ATT_QB * (i + 1)
        nxt_attends = jnp.logical_and(nxt < nblk, (nxt % nb) != 0)
        qi = lax.broadcasted_iota(jnp.int32, (w, w), 0)
        kj = lax.broadcasted_iota(jnp.int32, (w, w), 1)
        mask_n = jnp.logical_and(kj >= qi, nxt_attends)
        dk_acc[...] = jnp.zeros_like(dk_acc)
        dv_acc[...] = jnp.zeros_like(dv_acc)
        for h in range(ATT_HG):
            sl, col = slice(h * ATT_E, (h + 1) * ATT_E), slice(h * ATT_E, h * ATT_E + 1)
            k_ext = jnp.concatenate([kp_ref[:, sl], kc_ref[:, sl]], axis=0)
            v_ext = jnp.concatenate([vp_ref[:, sl], vc_ref[:, sl]], axis=0)
            for b in range(ATT_QB):
                r, kr = slice(b * w, (b + 1) * w), slice(b * w, (b + 2) * w)
                qh, doh, k2, v2 = q_ref[r, sl], do_ref[r, sl], k_ext[kr], v_ext[kr]
                p = jnp.where(masks[b], jnp.exp(_nt_dot(qh, k2) * scale - lse_ref[r, col]), 0.0)
                ds = p * (_nt_dot(doh, v2) - dd_ref[r, col]) * scale
                dq_ref[r, sl] = _nn_dot(ds, k2).astype(dq_ref.dtype)
                dk2, dv2 = _tn_dot(ds, qh), _tn_dot(p, doh)
                dk_acc[r, sl] += dk2[w:]
                dv_acc[r, sl] += dv2[w:]
                if b > 0:
                    rp = slice((b - 1) * w, b * w)
                    dk_acc[rp, sl] += dk2[:w]
                    dv_acc[rp, sl] += dv2[:w]
            last = slice((ATT_QB - 1) * w, ATT_QB * w)
            kl, vl, qn, don = kc_ref[last, sl], vc_ref[last, sl], qn_ref[:, sl], don_ref[:, sl]
            pn = jnp.where(mask_n, jnp.exp(_nt_dot(qn, kl) * scale - lsen_ref[:, col]), 0.0)
            dsn = pn * (_nt_dot(don, vl) - ddn_ref[:, col]) * scale
            dk_acc[last, sl] += _tn_dot(dsn, qn)
            dv_acc[last, sl] += _tn_dot(pn, don)
        dk_ref[...] = dk_acc[...].astype(dk_ref.dtype)
        dv_ref[...] = dv_acc[...].astype(dv_ref.dtype)

    cur, prev = _attn_specs(ATT_QB)
    nxt_spec = pl.BlockSpec((w, ATT_GW), lambda i: (jnp.minimum(ATT_QB * (i + 1), nblk - 1), 0))
    return pl.pallas_call(
        body, grid=(l // (ATT_QB * w),),
        in_specs=[cur, cur, prev, cur, prev, cur, cur, cur, nxt_spec, nxt_spec, nxt_spec, nxt_spec],
        out_specs=[cur] * 3, out_shape=[jax.ShapeDtypeStruct((l, ATT_GW), BF16)] * 3,
        scratch_shapes=[pltpu.VMEM((ATT_QB * w, ATT_GW), F32)] * 2,
        compiler_params=_params(("parallel",)), name=name,
    )(q, k, k, v, v, do, lse, dd, q, do, lse, dd)


def _to_perm(a, d):
    if d == 1:
        return a
    l, c = a.shape
    return a.reshape(l // d, d, c).transpose(1, 0, 2).reshape(l, c)


def _from_perm(a, d):
    if d == 1:
        return a
    l, c = a.shape
    return a.reshape(d, l // d, c).transpose(1, 0, 2).reshape(l, c)


def _mem_probs(qh, kh):
    s = _nt_dot(qh, kh) * (MEM_E ** -0.5)
    e = jnp.exp(s - jnp.max(s, axis=-1, keepdims=True))
    return e / jnp.sum(e, axis=-1, keepdims=True)


def _mem_fwd(mq, kv, name, tm=512):
    l, nm = mq.shape[0], kv.shape[0]

    def body(q_ref, kv_ref, o_ref):
        for h in range(MEM_H):
            sl = slice(h * MEM_E, (h + 1) * MEM_E)
            p = _mem_probs(q_ref[:, sl], kv_ref[:, sl])
            o_ref[:, sl] = _nn_dot(p, kv_ref[:, MEM_W + h * MEM_E:MEM_W + (h + 1) * MEM_E]).astype(o_ref.dtype)

    return pl.pallas_call(
        body, grid=(l // tm,),
        in_specs=[pl.BlockSpec((tm, MEM_W), lambda i: (i, 0)), pl.BlockSpec((nm, 2 * MEM_W), lambda i: (0, 0))],
        out_specs=pl.BlockSpec((tm, MEM_W), lambda i: (i, 0)),
        out_shape=jax.ShapeDtypeStruct((l, MEM_W), BF16),
        compiler_params=_params(("parallel",)), name=name,
    )(mq, kv)


def _mem_bwd(mq, kv, dmo, name, tm=512):
    l, nm = mq.shape[0], kv.shape[0]
    scale = MEM_E ** -0.5

    def body(q_ref, kv_ref, do_ref, dq_ref, dkv_ref):
        @pl.when(pl.program_id(0) == 0)
        def _():
            dkv_ref[...] = jnp.zeros_like(dkv_ref)

        for h in range(MEM_H):
            sl = slice(h * MEM_E, (h + 1) * MEM_E)
            vsl = slice(MEM_W + h * MEM_E, MEM_W + (h + 1) * MEM_E)
            qh, kh, vh, doh = q_ref[:, sl], kv_ref[:, sl], kv_ref[:, vsl], do_ref[:, sl]
            p = _mem_probs(qh, kh)
            dp = _nt_dot(doh, vh)
            ds = p * (dp - jnp.sum(dp * p, axis=-1, keepdims=True)) * scale
            dq_ref[:, sl] = _nn_dot(ds, kh).astype(dq_ref.dtype)
            dkv_ref[:, sl] += _tn_dot(ds, qh)
            dkv_ref[:, vsl] += _tn_dot(p, doh)

    row = pl.BlockSpec((tm, MEM_W), lambda i: (i, 0))
    full = pl.BlockSpec((nm, 2 * MEM_W), lambda i: (0, 0))
    return pl.pallas_call(
        body, grid=(l // tm,), in_specs=[row, full, row], out_specs=[row, full],
        out_shape=[jax.ShapeDtypeStruct((l, MEM_W), BF16), jax.ShapeDtypeStruct((nm, 2 * MEM_W), F32)],
        compiler_params=_params(("arbitrary",)), name=name,
    )(mq, kv, dmo)


def _gated_out_proj(zg, branches, b_gate, w_o, x, g2, name, tm=512):
    l, d = x.shape
    nbr = len(branches)

    def body(zg_ref, *rest):
        br_refs, (bg_ref, w_ref, x_ref, g2_ref, m_ref, h_ref, n_ref) = rest[:nbr], rest[nbr:]
        merged = jnp.zeros((tm, d), F32)
        for i, br_ref in enumerate(br_refs):
            cols = slice(i * d, (i + 1) * d)
            merged += _sigmoid(zg_ref[:, cols].astype(F32) + bg_ref[:, cols]) * br_ref[...].astype(F32)
        mb = merged.astype(BF16)
        m_ref[...] = mb
        hv = jnp.dot(mb.astype(_MXU), w_ref[...].astype(_MXU), preferred_element_type=F32) + x_ref[...]
        h_ref[...] = hv
        rs = lax.rsqrt(jnp.mean(hv * hv, axis=-1, keepdims=True) + RMS_EPS)
        n_ref[...] = (hv * rs * g2_ref[...]).astype(n_ref.dtype)

    row = lambda c: pl.BlockSpec((tm, c), lambda i: (i, 0))
    full = lambda a: pl.BlockSpec(a.shape, lambda i: (0, 0))
    return pl.pallas_call(
        body, grid=(l // tm,),
        in_specs=[row(nbr * d)] + [row(d)] * nbr + [full(b_gate), full(w_o), row(d), full(g2)],
        out_specs=[row(d)] * 3,
        out_shape=[jax.ShapeDtypeStruct((l, d), BF16), jax.ShapeDtypeStruct((l, d), F32),
                   jax.ShapeDtypeStruct((l, d), BF16)],
        compiler_params=_params(("parallel",)), name=name,
    )(zg, *branches, b_gate, w_o, x, g2)


def _discretize(lam_re, lam_im, log_dt, b_re, b_im):
    dt = jnp.exp(log_dt)[:, None]
    mag = jnp.exp(lam_re * dt)
    a_re, a_im = mag * jnp.cos(lam_im * dt), mag * jnp.sin(lam_im * dt)
    nr, ni = a_re - 1.0, a_im
    den = lam_re * lam_re + lam_im * lam_im
    coef_re = (nr * lam_re + ni * lam_im) / den
    coef_im = (ni * lam_re - nr * lam_im) / den
    bb_re = coef_re[..., None] * b_re - coef_im[..., None] * b_im
    bb_im = coef_re[..., None] * b_im + coef_im[..., None] * b_re
    return a_re, a_im, bb_re, bb_im


def _bd_in(bb):
    return jnp.einsum("gph,gk->ghkp", bb, jnp.eye(SSM_G, dtype=bb.dtype)).reshape(SSM_W, SSM_S)


def _bd_diag(x):
    gb = SSM_G // SSM_BD
    t = x.reshape(SSM_BD, gb, SSM_H, gb, SSM_P)
    return jnp.einsum("bghgp->bghp", t).reshape(SSM_G, SSM_H, SSM_P)


_ANY = pl.BlockSpec(memory_space=pl.ANY)
_MESH = pl.DeviceIdType.MESH


def _allgather(x, name):
    def body(x_ref, out_ref, send_sems, recv_sems, local_sem):
        mx, my, mc = lax.axis_index("x"), lax.axis_index("y"), lax.axis_index("c")
        me, sibling = (mx, my, mc), (mx, my, 1 - mc)
        chips = [(1 - mx, my), (mx, 1 - my), (1 - mx, 1 - my)]

        def blk(px, py, pc):
            return out_ref.at[4 * px + 2 * py + pc]

        def copy(k, block, to, src=None):
            return pltpu.make_async_remote_copy(
                src_ref=blk(*block) if src is None else src, dst_ref=blk(*block),
                send_sem=send_sems.at[k], recv_sem=recv_sems.at[k], device_id=to, device_id_type=_MESH)

        mine = pltpu.make_async_copy(x_ref, blk(*me), local_sem)
        mine.start()
        first = [copy(0, me, sibling, src=x_ref)]
        first += [copy(1 + j, me, (*chip, mc), src=x_ref) for j, chip in enumerate(chips)]
        for cp in first:
            cp.start()
        passed = [copy(4 + j, (*chip, mc), sibling) for j, chip in enumerate(chips)]
        for j, chip in enumerate(chips):
            copy(1 + j, (*chip, mc), me).wait_recv()
            passed[j].start()
        copy(0, sibling, me).wait_recv()
        for j, chip in enumerate(chips):
            copy(4 + j, (*chip, 1 - mc), me).wait_recv()
        for cp in first + passed:
            cp.wait_send()
        mine.wait()

    return pl.pallas_call(
        body, out_shape=jax.ShapeDtypeStruct((N_DEV,) + x.shape, x.dtype), in_specs=[_ANY], out_specs=_ANY,
        scratch_shapes=[pltpu.SemaphoreType.DMA((7,)), pltpu.SemaphoreType.DMA((7,)), pltpu.SemaphoreType.DMA],
        name=name,
    )(x)


def _pair_exchange(g, name):
    def body(g_ref, out_ref, send_sems, recv_sems):
        mx, my, mc = lax.axis_index("x"), lax.axis_index("y"), lax.axis_index("c")
        copies = [pltpu.make_async_remote_copy(
            src_ref=g_ref.at[2 * k + (1 - mc)], dst_ref=out_ref.at[k], send_sem=send_sems.at[k],
            recv_sem=recv_sems.at[k], device_id=(mx, my, 1 - mc), device_id_type=_MESH) for k in range(4)]
        for cp in copies:
            cp.start()
        for cp in copies:
            cp.wait()

    return pl.pallas_call(
        body, out_shape=jax.ShapeDtypeStruct((4,) + g.shape[1:], g.dtype), in_specs=[_ANY], out_specs=_ANY,
        scratch_shapes=[pltpu.SemaphoreType.DMA((4,)), pltpu.SemaphoreType.DMA((4,))], name=name,
    )(g)


_HBM = pl.BlockSpec(memory_space=pltpu.HBM)
_SEM = pl.BlockSpec(memory_space=pltpu.SEMAPHORE)
_EFFECT = pltpu.SideEffectType.DATAFLOW_SIDE_EFFECTING
_TOKEN = jax.ShapeDtypeStruct((8, 128), F32)


def _peer(rel):
    pos = (lax.axis_index("x"), lax.axis_index("y"), lax.axis_index("c"))
    return tuple(1 - p if (rel >> (2 - i)) & 1 else p for i, p in enumerate(pos))


def _index_of(dev):
    return 4 * dev[0] + 2 * dev[1] + dev[2]


def _split_copies(src_ref, land_ref, sems, plan):
    n = len(plan)
    return [pltpu.make_async_remote_copy(
        src_ref=src_ref if s is None else src_ref.at[s], dst_ref=land_ref.at[d], send_sem=sems[k],
        recv_sem=sems[n + k], device_id=peer, device_id_type=_MESH) for k, (s, d, peer) in enumerate(plan)]


def _split_start(src, n_land, plan_fn, after, name):
    blk = src.shape[-2:]
    land = lax.empty((n_land,) + blk, src.dtype)
    n = len(plan_fn())

    def body(src_ref, land_ref, after_ref, *outs):
        for cp in _split_copies(src_ref, land_ref, outs[:2 * n], plan_fn()):
            cp.start()
        outs[2 * n + 2][...] = jnp.zeros_like(outs[2 * n + 2])

    res = pl.pallas_call(
        body, name=name,
        out_shape=(pltpu.SemaphoreType.DMA(()),) * (2 * n)
        + (pltpu.HBM(src.shape, src.dtype), pltpu.HBM(land.shape, land.dtype), _TOKEN),
        in_specs=(_HBM, _HBM, _ANY),
        out_specs=(_SEM,) * (2 * n) + (_HBM, _HBM, pl.BlockSpec(memory_space=pltpu.VMEM)),
        input_output_aliases={0: 2 * n, 1: 2 * n + 1},
        compiler_params=pltpu.CompilerParams(has_side_effects=_EFFECT),
    )(pltpu.with_memory_space_constraint(src, pltpu.HBM), pltpu.with_memory_space_constraint(land, pltpu.HBM), after)
    return res[:2 * n], res[2 * n], res[2 * n + 1], res[2 * n + 2]


def _split_wait(sems, src, land, plan_fn, after, name):
    n = len(sems) // 2

    def body(src_ref, land_ref, *rest):
        for cp in _split_copies(src_ref, land_ref, rest[:2 * n], plan_fn()):
            cp.wait_send()
            cp.wait_recv()

    return pl.pallas_call(
        body, name=name,
        out_shape=(pltpu.HBM(src.shape, src.dtype), pltpu.HBM(land.shape, land.dtype)),
        in_specs=(_HBM, _HBM) + (_SEM,) * (2 * n) + (_ANY,), out_specs=(_HBM, _HBM),
        input_output_aliases={0: 0, 1: 1},
        compiler_params=pltpu.CompilerParams(has_side_effects=_EFFECT),
    )(src, land, *sems, after)


def _gather_plan():
    me = _index_of(_peer(0))
    return [(None, me, _peer(rel)) for rel in range(1, N_DEV)]


def _gather_wait_plan():
    return [(None, _index_of(_peer(rel)), _peer(rel)) for rel in range(1, N_DEV)]


def _chip_plan():
    return [(_index_of(_peer(rel)) // 2, j, _peer(rel)) for j, rel in enumerate((4, 2, 6))]


def _owner_plan():
    return [(_index_of(_peer(rel)), rel - 1, _peer(rel)) for rel in range(1, N_DEV)]


def _pair_sum(g, t1, my_c, name, tr):
    _, r, c = g.shape

    def body(c_ref, g_ref, t_ref, o_ref, ob_ref):
        s = g_ref[...] + t_ref[...]
        o_ref[...] = s
        ob_ref[...] = s.astype(BF16)

    blk = pl.BlockSpec((None, tr, c), lambda k, i, cr: (k, i, 0))
    return pl.pallas_call(
        body,
        grid_spec=pltpu.PrefetchScalarGridSpec(
            num_scalar_prefetch=1, grid=(4, r // tr),
            in_specs=[pl.BlockSpec((None, tr, c), lambda k, i, cr: (2 * k + cr[0], i, 0)), blk],
            out_specs=[blk, blk]),
        out_shape=[jax.ShapeDtypeStruct((4, r, c), F32), jax.ShapeDtypeStruct((4, r, c), BF16)],
        compiler_params=_params(("parallel", "parallel")), name=name,
    )(my_c, g, t1)


def _adam_math(g, w, m, v):
    m = ADAM_B1 * m + (1.0 - ADAM_B1) * g
    v = ADAM_B2 * v + (1.0 - ADAM_B2) * (g * g)
    m_hat = m / (1.0 - ADAM_B1 ** ADAM_STEP)
    v_hat = v / (1.0 - ADAM_B2 ** ADAM_STEP)
    delta = -ADAM_LR * (m_hat / (jnp.sqrt(v_hat) + ADAM_EPS) + ADAM_WD * w)
    return delta, m, v


def _grad_sum(own, own_index, recv, name, tr):
    _, r, c = own.shape
    n = recv.shape[0]

    def body(k_ref, own_ref, *rest):
        g = own_ref[...]
        for recv_ref in rest[:n]:
            g = g + recv_ref[...].astype(F32)
        rest[n][...] = g

    def slot(j):
        return pl.BlockSpec((None, tr, c), lambda i, kr: (j, i, 0))

    return pl.pallas_call(
        body,
        grid_spec=pltpu.PrefetchScalarGridSpec(
            num_scalar_prefetch=1, grid=(r // tr,),
            in_specs=[pl.BlockSpec((None, tr, c), lambda i, kr: (kr[0], i, 0))] + [slot(j) for j in range(n)],
            out_specs=pl.BlockSpec((tr, c), lambda i, kr: (i, 0))),
        out_shape=jax.ShapeDtypeStruct((r, c), F32),
        compiler_params=_params(("parallel",)), name=name,
    )(own_index, own, *([recv] * n))


def _adam_many(g, w, m, v, row_tiles, name):
    n = len(g)

    def body(*refs):
        ins, outs = refs[:4 * n], refs[4 * n:]
        for i in range(n):
            res = _adam_math(ins[i][...], ins[n + i][...], ins[2 * n + i][...], ins[3 * n + i][...])
            for kind in range(3):
                outs[kind * n + i][...] = res[kind]

    def spec(a):
        blk = (a.shape[0] // row_tiles,) + a.shape[1:]
        return pl.BlockSpec(blk, lambda t, nd=a.ndim: (t,) + (0,) * (nd - 1))

    specs = [spec(a) for a in g]
    res = pl.pallas_call(
        body, grid=(row_tiles,), in_specs=specs * 4, out_specs=specs * 3,
        out_shape=[jax.ShapeDtypeStruct(a.shape, F32) for a in g] * 3,
        compiler_params=_params(("parallel",)), name=name,
    )(*g, *w, *m, *v)
    return res[:n], res[n:2 * n], res[2 * n:]


def _sum8(g8, name):
    _, r, c = g8.shape

    def body(g_ref, o_ref):
        acc = g_ref[0]
        for j in range(1, N_DEV):
            acc = acc + g_ref[j]
        o_ref[...] = acc

    return pl.pallas_call(
        body, grid=(1,), in_specs=[pl.BlockSpec((N_DEV, r, c), lambda i: (0, 0, 0))],
        out_specs=pl.BlockSpec((r, c), lambda i: (0, 0)), out_shape=jax.ShapeDtypeStruct((r, c), F32),
        compiler_params=_params(("arbitrary",)), name=name,
    )(g8)


def _pack(arrs, pad_rows=8):
    flat = jnp.concatenate([a.reshape(-1) for a in arrs])
    n = flat.shape[0]
    q = PACK_C * pad_rows
    tot = -(-n // q) * q
    if tot != n:
        flat = jnp.concatenate([flat, jnp.zeros((tot - n,), flat.dtype)])
    return flat.reshape(tot // PACK_C, PACK_C)


def _unpack(buf, shapes):
    flat = buf.reshape(-1)
    out, off = [], 0
    for s in shapes:
        n = int(np.prod(s))
        out.append(flat[off:off + n].reshape(s))
        off += n
    return out


GROUPS = (("w_in",),
          ("w_glu", "w_ssm_br", "w_mem_br", "w_attn_br"),
          ("w_up", "w_down"),
          ("w_mem_kv", "w_o"))
GROUP_TR = (400, 384, 512, 256)
MLP_GROUP = 2
MIXER_GROUPS = (1, 3)
ATTN_BR_FOLD = 2


def _stored_shape(name):
    r, c, ax = BIG_SHAPE[name]
    rows, cols = (r // N_DEV, c) if ax == 0 else (c // N_DEV, r)
    return (rows // ATTN_BR_FOLD, cols * ATTN_BR_FOLD) if name == "w_attn_br" else (rows, cols)


def _stored(shard, name):
    a = shard[0].T if BIG_SHAPE[name][2] == 1 else shard[0]
    return a.reshape(_stored_shape(name))


def _unstored(a, name):
    r, c, ax = BIG_SHAPE[name]
    if ax == 0:
        return a.reshape(1, r // N_DEV, c)
    return a.reshape(c // N_DEV, r).T[None]


def _pack_group(d, names):
    return jnp.concatenate([_stored(d[n], n) for n in names], axis=0)


def _split_group(buf, names):
    out, off = {}, 0
    for n in names:
        rows = _stored_shape(n)[0]
        out[n] = buf[..., off:off + rows, :]
        off += rows
    return out


def _full_stored(stacked, name):
    r, c, ax = BIG_SHAPE[name]
    return stacked.reshape((r, c) if ax == 0 else (c, r))


def _stacked_stored(full, name):
    return full.reshape((N_DEV,) + _stored_shape(name))


def _gelu_parts(x):
    c0, c1 = math.sqrt(2.0 / math.pi), 0.044715
    th = jnp.tanh(c0 * (x + c1 * x * x * x))
    return th, c0, c1


def _local_step(x, mem, tgt, wb, sp, late_weights, grads_ready, small_grads_ready):
    l = x.shape[0]
    w_a, w_g = wb["w_in"][:ZA_W], wb["w_in"][ZA_W:]

    a_re, a_im, bb_re, bb_im = _discretize(sp["ssm_lambda_re"], sp["ssm_lambda_im"], sp["ssm_log_dt"],
                                           sp["ssm_b_re"], sp["ssm_b_im"])
    a_pair = jnp.stack([a_re.reshape(1, SSM_S), a_im.reshape(1, SSM_S)])
    a_conj = jnp.stack([a_re.reshape(1, SSM_S), -a_im.reshape(1, SSM_S)])
    b_re_t, b_im_t = _bd_in(bb_re).astype(BF16), _bd_in(bb_im).astype(BF16)
    c_re_t = _bd_in(sp["ssm_c_re"].transpose(0, 2, 1)).astype(BF16)
    c_im_t = (-_bd_in(sp["ssm_c_im"].transpose(0, 2, 1))).astype(BF16)
    d_row = sp["ssm_d"].reshape(1, SSM_W)

    n1 = _rms_fwd(x, sp["norm1_g"], "rms1")
    za = _mm(n1, w_a, [BF16], tb=True, name="in_proj_a", tn=1664)
    zg = _mm(n1, w_g, [BF16], tb=True, name="in_proj_g")
    for gi in MIXER_GROUPS:
        wb = {**wb, **late_weights(gi, za)}
    u = za[:, :SSM_W]
    mq = za[:, ZA_W - MEM_W:]

    u_s = _scan_order(u)
    s_all = _ssm_scan(u_s, b_re_t, b_im_t, a_pair, reverse=False, name="ssm_scan_fwd")
    ys = _time_order(_mm((s_all, 0), c_re_t, [F32], tb=True, pair2=((s_all, 1), c_im_t), bd=SSM_BD, tm=2048, name="ssm_cs"))

    def gelu_fn(r, b):
        y0 = r[0] + b[0] * r[1].astype(F32)
        th, _, _ = _gelu_parts(y0)
        return [y0, 0.5 * y0 * (1.0 + th)], []
    y0, y1 = _ew(gelu_fn, [ys, u], [d_row], [(SSM_W, F32), (SSM_W, BF16)], [], name="ssm_gelu", tm=512)

    def glu_epi(acc, y1t, bg):
        t = acc + bg
        return t, y1t.astype(F32) * _sigmoid(t)
    t_glu, y2 = _mm(y1, wb["w_glu"], [F32, BF16], epi=glu_epi, mn=[y1], rows=[sp["b_glu"]], name="ssm_glu")
    br_ssm = _mm(y2, wb["w_ssm_br"], [BF16], tb=True, name="ssm_br")

    qkv_p, o_g, lse_g = [], [], []
    for g, d in enumerate(DILATIONS):
        nb = l // d // ATT_WIN
        cols = [za[:, SSM_W + (3 * j + g) * ATT_GW: SSM_W + (3 * j + g + 1) * ATT_GW] for j in range(3)]
        qp, kp, vp = [_to_perm(cc, d) for cc in cols]
        qkv_p.append((qp, kp, vp))
        og, lg = _attn_fwd(qp, kp, vp, nb, "attn_fwd%d" % g)
        o_g.append(_from_perm(og, d))
        lse_g.append(_from_perm(lg, d))

    def merge_fn(r, b):
        o0, o1, o2, l0, l1, l2 = r
        mx = jnp.maximum(jnp.maximum(l0, l1), l2)
        e0, e1, e2 = jnp.exp(l0 - mx), jnp.exp(l1 - mx), jnp.exp(l2 - mx)
        tot = e0 + e1 + e2
        return [(e0 * o0 + e1 * o1 + e2 * o2) / tot, mx + jnp.log(tot)], []
    o_att, lse_tot = _ew(merge_fn, o_g + lse_g, [], [(ATT_GW, F32), (ATT_GW, F32)], [], name="attn_merge", tm=512)
    br_attn = _mm(o_att, wb["w_attn_br"], [BF16], tb=True, name="attn_br")

    mn = _rms_fwd(mem, sp["mem_norm_g"], "rms_mem")
    kv = _mm(mn, wb["w_mem_kv"], [BF16], name="mem_kv")
    mo = _mem_fwd(mq, kv, "mem_attn_fwd")
    br_mem = _mm(mo, wb["w_mem_br"], [BF16], tb=True, name="mem_br")

    merged, h1, n2 = _gated_out_proj(zg, [br_ssm, br_attn, br_mem], sp["b_gate"], wb["w_o"], x, sp["norm2_g"],
                                     "gated_o_proj")

    def up_epi(acc):
        ra = jnp.maximum(acc, 0.0)
        return ra * ra, ra
    wm = late_weights(MLP_GROUP, n2)
    f_act, r_act = _mm(n2, wm["w_up"], [BF16, BF16], tb=True, epi=up_epi, name="mlp_up")
    def down_epi(acc, ht, tv, gf):
        hv = acc + ht
        rs = lax.rsqrt(jnp.mean(hv * hv, axis=-1, keepdims=True) + RMS_EPS)
        err = hv * rs * gf - tv
        dh, dgf = _rms_bwd_tile(hv, err * (1.0 / D_MODEL), gf)
        return dh, dgf, _colsum(err * err) * (0.5 / D_MODEL)
    dh2, d_final_g, loss_cols = _mm(f_act, wm["w_down"], [F32], epi=down_epi, mn=[h1, tgt], rows=[sp["final_g"]],
                                    n_sums=2, tk=1024, name="mlp_down")
    loss = jnp.sum(loss_cols, axis=1, keepdims=True)

    gw, gs = {}, {"final_g": d_final_g}
    d_act = _mm(dh2, wm["w_down"], [BF16], tb=True, epi=lambda acc, ra: (acc * 2.0 * ra.astype(F32),), mn=[r_act],
                name="mlp_down_dx")
    dw_down = _mm(f_act, dh2, [F32], ta=True, name="mlp_down_dw")
    dw_up = _mm(d_act, n2, [F32], ta=True, name="mlp_up_dw")
    token = grads_ready(MLP_GROUP, {"w_up": dw_up, "w_down": dw_down})
    def up_dx_epi(acc, ht, dht, g2):
        dx, dg = _rms_bwd_tile(ht, acc, g2)
        return dx + dht, dg
    dh1, gs["norm2_g"] = _mm(d_act, wm["w_up"], [F32], epi=up_dx_epi, mn=[h1, dh2],
                             rows=[sp["norm2_g"] + token[:1, :1]], n_sums=1, tk=1024, name="mlp_up_dx")
    gw["w_o"] = _mm(merged, dh1, [F32], ta=True, name="o_proj_dw")

    def gate_bwd_epi(dm, *tiles):
        dbr, dz = [], []
        for zt, bt, bias in zip(tiles[0:3], tiles[3:6], tiles[6:9]):
            gt = _sigmoid(zt.astype(F32) + bias)
            dbr.append(dm * gt)
            dz.append(dm * bt.astype(F32) * gt * (1.0 - gt))
        return (*dbr, *dz, *[_colsum(t) for t in dz])
    gate_bias = [sp["b_gate"][:, i * D_MODEL:(i + 1) * D_MODEL] for i in range(3)]
    res = _mm(dh1, wb["w_o"], [BF16] * 6, tb=True, epi=gate_bwd_epi, mn=[(zg, 0), (zg, 1), (zg, 2), br_ssm, br_attn, br_mem],
              rows=gate_bias, n_sums=3, tm=512, name="o_proj_dx")
    (dbr_ssm, dbr_attn, dbr_mem), dzg = res[0:3], res[3:6]
    gs["b_gate"] = jnp.concatenate(res[6:9], axis=1)

    gw["w_ssm_br"] = _mm(dbr_ssm, y2, [F32], ta=True, name="ssm_br_dw")
    def glu_bwd_epi(dy, y1t, tt):
        sg = _sigmoid(tt)
        dt = dy * y1t.astype(F32) * sg * (1.0 - sg)
        return dt, dy * sg, _colsum(dt)
    dt_glu, dy1a, gs["b_glu"] = _mm(dbr_ssm, wb["w_ssm_br"], [BF16, F32], epi=glu_bwd_epi, mn=[y1, t_glu], n_sums=1,
                                    name="ssm_br_dx")
    gw["w_glu"] = _mm(y1, dt_glu, [F32], ta=True, name="ssm_glu_dw")

    def gelu_bwd_epi(acc, dy1t, y0t, ut):
        th, c0, c1 = _gelu_parts(y0t)
        dg = 0.5 * (1.0 + th) + 0.5 * y0t * (1.0 - th * th) * c0 * (1.0 + 3.0 * c1 * y0t * y0t)
        dy = (acc + dy1t) * dg
        return dy, _colsum(dy * ut.astype(F32))
    dy0, gs["ssm_d"] = _mm(dt_glu, wb["w_glu"], [F32], tb=True, epi=gelu_bwd_epi, mn=[dy1a, y0, u], n_sums=1,
                           name="ssm_glu_dx")
    dy0_s = _scan_order(dy0)
    lam, da, d_b, d_c = _ssm_scan(dy0_s, c_re_t, c_im_t, a_conj, reverse=True, s_fwd=s_all, u=u_s,
                                  name="ssm_scan_bwd")
    du = _time_order(_mm((lam, 0), b_re_t, [BF16], tb=True, pair2=((lam, 1), b_im_t),
                         epi=lambda acc, dyt, dr: (acc + dyt * dr,), mn=[dy0_s], rows=[d_row], bd=SSM_BD, tm=2048, name="ssm_bu_dx"))
    gs["a_re"], gs["a_im"] = da[0], da[1]
    gs["bb_re"], gs["bb_im"] = _bd_diag(d_b[0]).transpose(0, 2, 1), _bd_diag(d_b[1]).transpose(0, 2, 1)
    gs["ssm_c_re"], gs["ssm_c_im"] = _bd_diag(d_c[0]), -_bd_diag(d_c[1])

    gw["w_attn_br"] = _mm(dbr_attn, o_att, [F32], ta=True, name="attn_br_dw")

    def do_epi(acc, ot):
        prod = acc * ot
        head = lax.broadcasted_iota(jnp.int32, prod.shape, 1) // ATT_E
        dd = jnp.zeros_like(prod)
        for h in range(ATT_HG):
            dd = jnp.where(head == h, jnp.sum(jnp.where(head == h, prod, 0.0), axis=1, keepdims=True), dd)
        return acc, dd
    do_att, dd_att = _mm(dbr_attn, wb["w_attn_br"], [BF16, F32], epi=do_epi, mn=[o_att], name="attn_br_dx")
    dq_l, dk_l, dv_l = [], [], []
    for g, d in enumerate(DILATIONS):
        nb = l // d // ATT_WIN
        qp, kp, vp = qkv_p[g]
        dq, dk, dv = _attn_bwd(qp, kp, vp, _to_perm(do_att, d), _to_perm(lse_tot, d), _to_perm(dd_att, d),
                               nb, "attn_bwd%d" % g)
        dq_l.append(_from_perm(dq, d))
        dk_l.append(_from_perm(dk, d))
        dv_l.append(_from_perm(dv, d))

    gw["w_mem_br"] = _mm(dbr_mem, mo, [F32], ta=True, name="mem_br_dw")
    dmo = _mm(dbr_mem, wb["w_mem_br"], [BF16], name="mem_br_dx")
    dmq, dkv = _mem_bwd(mq, kv, dmo, "mem_attn_bwd")
    gw["w_mem_kv"] = _mm(mn, dkv, [F32], ta=True, name="mem_kv_dw")
    dmn = _mm(dkv, wb["w_mem_kv"], [F32], tb=True, name="mem_kv_dx")
    token = sum(grads_ready(gi, gw) for gi in MIXER_GROUPS)
    gs["mem_norm_g"] = _rms_bwd(mem, dmn, None, sp["mem_norm_g"] + token[:1, :1], "rms_mem_bwd")[1]

    dza = jnp.concatenate([du] + dq_l + dk_l + dv_l + [dmq], axis=1)
    dn_a = _mm(dza, w_a, [F32], name="in_proj_a_dx", tk=1664)
    dw_a = _mm(dza, n1, [F32], ta=True, name="in_proj_a_dw", tm=1664)
    dw_g = [_mm(dzg[i], n1, [F32], ta=True, name="in_proj_g_dw%d" % i) for i in range(3)]
    gw["w_in"] = jnp.concatenate([dw_a] + dw_g, axis=0)
    token = grads_ready(0, gw) + small_grads_ready(gs)
    def in_dx_epi(acc, pt, xt, dht, g1):
        dx, dg = _rms_bwd_tile(xt, acc + pt, g1)
        return dx + dht, dg
    w_gs = [w_g[i * D_MODEL:(i + 1) * D_MODEL] for i in range(3)]
    grad_x, gs["norm1_g"] = _mm(dzg[0], w_gs[0], [F32], pair2=(dzg[1], w_gs[1], dzg[2], w_gs[2]), epi=in_dx_epi,
                                mn=[dn_a, x, dh1],
                                rows=[sp["norm1_g"] + token[:1, :1]], n_sums=1, tm=512, name="in_proj_g_dx")
    return loss, grad_x, gs


_SMALL_GRAD_ORDER = ("norm1_g", "mem_norm_g", "b_gate", "a_re", "a_im", "bb_re", "bb_im", "ssm_c_re", "ssm_c_im",
                     "ssm_d", "b_glu", "norm2_g", "final_g")


def kernel(x, mem, norm1_g, mem_norm_g, w_in, b_gate, ssm_lambda_re, ssm_lambda_im, ssm_log_dt, ssm_b_re, ssm_b_im, ssm_c_re, ssm_c_im, ssm_d, w_glu, b_glu, w_ssm_br, w_attn_br, w_mem_kv, w_mem_br, w_o, norm2_g, w_up, w_down, final_g, loss_target, m_norm1_g, m_mem_norm_g, m_w_in, m_b_gate, m_ssm_lambda_re, m_ssm_lambda_im, m_ssm_log_dt, m_ssm_b_re, m_ssm_b_im, m_ssm_c_re, m_ssm_c_im, m_ssm_d, m_w_glu, m_b_glu, m_w_ssm_br, m_w_attn_br, m_w_mem_kv, m_w_mem_br, m_w_o, m_norm2_g, m_w_up, m_w_down, m_final_g, v_norm1_g, v_mem_norm_g, v_w_in, v_b_gate, v_ssm_lambda_re, v_ssm_lambda_im, v_ssm_log_dt, v_ssm_b_re, v_ssm_b_im, v_ssm_c_re, v_ssm_c_im, v_ssm_d, v_w_glu, v_b_glu, v_w_ssm_br, v_w_attn_br, v_w_mem_kv, v_w_mem_br, v_w_o, v_norm2_g, v_w_up, v_w_down, v_final_g):
    args = dict(locals())
    w = {n: args[n] for n in ALL_W}
    m = {n: args["m_" + n] for n in ALL_W}
    v = {n: args["v_" + n] for n in ALL_W}
    my_c = lax.axis_index("c").astype(jnp.int32).reshape(1)
    my_chip = (2 * lax.axis_index("x") + lax.axis_index("y")).astype(jnp.int32).reshape(1)

    w_pack = [_pack_group(w, names) for names in GROUPS]
    my_index = (4 * lax.axis_index("x") + 2 * lax.axis_index("y") + lax.axis_index("c")).astype(jnp.int32)
    zero = jnp.zeros((), jnp.int32)
    w_all = _allgather(w_pack[0].astype(BF16), "allgather_weights0")
    wb = {n: _full_stored(part, n) for n, part in _split_group(w_all, GROUPS[0]).items()}
    gathers = {gi: _split_start(w_pack[gi].astype(BF16), N_DEV, _gather_plan, w_all, "weights_gather_start%d" % gi)
               for gi in range(1, len(GROUPS))}

    def gathered(started, after, name):
        sems, src, land, _ = started
        src, land = _split_wait(sems, src, land, _gather_wait_plan, after, name)
        return lax.dynamic_update_slice(land, src[None], (my_index, zero, zero))

    def late_weights(gi, after):
        full = gathered(gathers[gi], after, "weights_gather_wait%d" % gi)
        return {n: _full_stored(part, n) for n, part in _split_group(full, GROUPS[gi]).items()}

    pending = {}

    def grads_ready(gi, grads):
        g_pack = jnp.concatenate([_stacked_stored(grads[n], n) for n in GROUPS[gi]], axis=1)
        if gi == 0:
            t1 = _pair_exchange(g_pack, "grad_pair_exchange%d" % gi)
            p_sum, p_bf = _pair_sum(g_pack, t1, my_c, "grad_pair_sum%d" % gi, GROUP_TR[gi])
            started = _split_start(p_bf, 3, _chip_plan, p_sum, "grad_chip_exchange_start%d" % gi)
            pending[gi] = (p_sum, my_chip, started, _chip_plan)
        else:
            started = _split_start(g_pack.astype(BF16), N_DEV - 1, _owner_plan, g_pack, "grad_exchange_start%d" % gi)
            pending[gi] = (g_pack, my_index.reshape(1), started, _owner_plan)
        return started[3]

    early_small = [n for n in _SMALL_GRAD_ORDER if n != "norm1_g"]
    small_started = []

    def small_grads_ready(gs):
        started = _split_start(_pack([gs[n] for n in early_small]), N_DEV, _gather_plan, gs["mem_norm_g"],
                               "small_grads_gather_start")
        small_started.append((started, [gs[n].shape for n in early_small]))
        return started[3]

    sp = {
        "norm1_g": norm1_g + sum(started[3][:1, :1] for started in gathers.values()), "mem_norm_g": mem_norm_g, "b_gate": b_gate, "b_glu": b_glu, "norm2_g": norm2_g,
        "final_g": final_g.reshape(1, D_MODEL),
        "ssm_lambda_re": ssm_lambda_re[0], "ssm_lambda_im": ssm_lambda_im[0], "ssm_log_dt": ssm_log_dt[0],
        "ssm_b_re": ssm_b_re[0], "ssm_b_im": ssm_b_im[0], "ssm_c_re": ssm_c_re[0], "ssm_c_im": ssm_c_im[0],
        "ssm_d": ssm_d[0],
    }
    loss, grad_x, gs = _local_step(x[0], mem[0], loss_target[0], wb, sp, late_weights, grads_ready,
                                     small_grads_ready)
    loss = lax.psum(loss[0, 0], ("x", "y", "c"))
    n1_started = _split_start(_pack([gs["norm1_g"]]), N_DEV, _gather_plan, grad_x, "norm1_grad_gather_start")

    big_g = {}
    for gi, names in enumerate(GROUPS):
        own, own_index, (sems, src, land, _), plan = pending[gi]
        recv = _split_wait(sems, src, land, plan, grad_x, "grad_exchange_wait%d" % gi)[1]
        g_pack = _grad_sum(own, own_index, recv, "grad_sum%d" % gi, GROUP_TR[gi])
        for n, part in _split_group(g_pack, names).items():
            big_g[n] = _unstored(part, n)
    rows_of = lambda d, names: [d[n].reshape(d[n].shape[-2:]) for n in names]
    big_out = _adam_many(rows_of(big_g, BIG), rows_of(w, BIG), rows_of(m, BIG), rows_of(v, BIG), 8, "adam_big")
    big = [big_g] + [{n: a[None] for n, a in zip(BIG, outs)} for outs in big_out]

    (sg_started, sg_shapes), = small_started
    sg_all = jnp.concatenate([gathered(sg_started, big_out[0][0], "small_grads_gather_wait"),
                              gathered(n1_started, big_out[0][0], "norm1_grad_gather_wait")], axis=1)
    sg_sum = _sum8(sg_all, "sum_small_grads")
    n1_rows = n1_started[1].shape[0]
    sg = dict(zip(early_small, _unpack(sg_sum[:-n1_rows], sg_shapes)))
    sg["norm1_g"] = _unpack(sg_sum[-n1_rows:], [gs["norm1_g"].shape])[0]
    _, disc_vjp = jax.vjp(_discretize, sp["ssm_lambda_re"], sp["ssm_lambda_im"], sp["ssm_log_dt"],
                          sp["ssm_b_re"], sp["ssm_b_im"])
    d_lre, d_lim, d_ldt, d_bre, d_bim = disc_vjp((sg["a_re"].reshape(SSM_G, SSM_P), sg["a_im"].reshape(SSM_G, SSM_P),
                                                  sg["bb_re"], sg["bb_im"]))
    small_grad = {
        "norm1_g": sg["norm1_g"], "mem_norm_g": sg["mem_norm_g"], "b_gate": sg["b_gate"],
        "ssm_lambda_re": d_lre, "ssm_lambda_im": d_lim, "ssm_log_dt": d_ldt, "ssm_b_re": d_bre, "ssm_b_im": d_bim,
        "ssm_c_re": sg["ssm_c_re"], "ssm_c_im": sg["ssm_c_im"], "ssm_d": sg["ssm_d"], "b_glu": sg["b_glu"],
        "norm2_g": sg["norm2_g"], "final_g": sg["final_g"],
    }
    small_grad = {n: small_grad[n].reshape(w[n].shape) for n in SMALL}

    def squeezed(a):
        return a.reshape(a.shape[1:]) if a.ndim > 2 else a.reshape(1, -1)

    sq = lambda d: [squeezed(d[n]) for n in SMALL]
    small_out = _adam_many(sq(small_grad), sq(w), sq(m), sq(v), 1, "adam_small")
    small = [small_grad] + [{n: a.reshape(w[n].shape) for n, a in zip(SMALL, outs)} for outs in small_out]

    outs = [loss, grad_x[None]]
    for kind in range(4):
        for n in ALL_W:
            outs.append(big[kind][n] if n in BIG else small[kind][n])
    return tuple(outs)
```

```python
import math

import numpy as np
import jax
import jax.numpy as jnp
from jax import lax
from jax.experimental import pallas as pl
from jax.experimental.pallas import tpu as pltpu

F32 = jnp.float32
BF16 = jnp.bfloat16
_MXU = jnp.bfloat16

D_MODEL = 1024
SSM_G, SSM_H, SSM_P = 32, 16, 64
SSM_W = SSM_G * SSM_H
SSM_S = SSM_G * SSM_P
SSM_BD = 4
ATT_E = 64
ATT_HG = 4
ATT_GW = ATT_HG * ATT_E
ATT_WIN = 128
ATT_QB = 8
ATT_QB_FWD = 4
DILATIONS = (1, 4, 16)
MEM_H, MEM_E = 4, 128
MEM_W = MEM_H * MEM_E
ZA_W = SSM_W + 9 * ATT_GW + MEM_W
ZG_W = 3 * D_MODEL
IN_W = ZA_W + ZG_W
RMS_EPS = 1e-6
NEG_INF = -1e30

ADAM_LR, ADAM_B1, ADAM_B2, ADAM_EPS, ADAM_WD, ADAM_STEP = 0.001, 0.9, 0.999, 1e-08, 0.01, 10

N_DEV = 8
PACK_C = 512
_VMEM_LIMIT = 56 * 1024 * 1024
SUBLANES = 16
SCAN_SEG = 128
SCAN_CHAINS = 4
SCAN_UNROLL = 4
SCAN_W = 128

BIG = ("w_in", "w_glu", "w_ssm_br", "w_attn_br", "w_mem_kv", "w_mem_br", "w_o", "w_up", "w_down")
BIG_SHAPE = {
    "w_in": (D_MODEL, IN_W, 1), "w_glu": (SSM_W, SSM_W, 0), "w_ssm_br": (SSM_W, D_MODEL, 1),
    "w_attn_br": (ATT_GW, D_MODEL, 1), "w_mem_kv": (D_MODEL, 2 * MEM_W, 0), "w_mem_br": (MEM_W, D_MODEL, 1),
    "w_o": (D_MODEL, D_MODEL, 0), "w_up": (D_MODEL, 4 * D_MODEL, 1), "w_down": (4 * D_MODEL, D_MODEL, 0),
}
SMALL = ("norm1_g", "mem_norm_g", "b_gate", "ssm_lambda_re", "ssm_lambda_im", "ssm_log_dt", "ssm_b_re",
         "ssm_b_im", "ssm_c_re", "ssm_c_im", "ssm_d", "b_glu", "norm2_g", "final_g")
ALL_W = ("norm1_g", "mem_norm_g", "w_in", "b_gate", "ssm_lambda_re", "ssm_lambda_im", "ssm_log_dt", "ssm_b_re",
         "ssm_b_im", "ssm_c_re", "ssm_c_im", "ssm_d", "w_glu", "b_glu", "w_ssm_br", "w_attn_br", "w_mem_kv",
         "w_mem_br", "w_o", "norm2_g", "w_up", "w_down", "final_g")


def _params(sem):
    return pltpu.CompilerParams(dimension_semantics=sem, vmem_limit_bytes=_VMEM_LIMIT)


def _pick(n, cap):
    if n <= cap:
        return n
    t = (cap // 128) * 128
    while n % t:
        t -= 128
    return t


def _mm(a, b, outs, *, name, ta=False, tb=False, epi=None, mn=(), rows=(), pair2=None, bd=0, n_sums=0,
        tm=1024, tn=1024, tk=2048):
    ab = [a, b] + (list(pair2) if pair2 is not None else [])
    planes = [op[1] if isinstance(op, tuple) else None for op in ab]
    ab = [op[0] if isinstance(op, tuple) else op for op in ab]
    a_shape, b_shape = ab[0].shape[-2:], ab[1].shape[-2:]
    m = a_shape[1] if ta else a_shape[0]
    k = a_shape[0] if ta else a_shape[1]
    n = b_shape[0] if tb else b_shape[1]
    assert k == (b_shape[1] if tb else b_shape[0]), (name, a_shape, b_shape)
    out_n = n
    if bd and ta:
        assert not tb
        tm, tn, tk = m // bd, n // bd, _pick(k, tk)
        grid, out_n = (bd, 1, k // tk), tn
        a_blk = ((tk, tm), lambda i, j, kk: (kk, i))
        b_blk = ((tk, tn), lambda i, j, kk: (kk, i))
        mn_spec = pl.BlockSpec((tm, tn), lambda i, j, kk: (i, 0))
    elif bd:
        tm, tn, tk = _pick(m, tm), n // bd, k // bd
        grid = (m // tm, bd, 1)
        a_blk = ((tm, tk), lambda i, j, kk: (i, j))
        b_blk = ((tn, tk) if tb else (tk, tn), lambda i, j, kk: (j, j))
        mn_spec = pl.BlockSpec((tm, tn), lambda i, j, kk: (i, j))
    else:
        tm, tn, tk = _pick(m, tm), _pick(n, tn), _pick(k, tk)
        grid = (m // tm, n // tn, k // tk)
        a_blk = ((tk, tm), lambda i, j, kk: (kk, i)) if ta else ((tm, tk), lambda i, j, kk: (i, kk))
        b_blk = ((tn, tk), lambda i, j, kk: (j, kk)) if tb else ((tk, tn), lambda i, j, kk: (kk, j))
        mn_spec = pl.BlockSpec((tm, tn), lambda i, j, kk: (i, j))

    def operand_spec(blk, plane):
        shape, imap = blk
        if plane is None:
            return pl.BlockSpec(shape, imap)
        return pl.BlockSpec((None,) + shape, lambda i, j, kk: (plane,) + imap(i, j, kk))

    ab_specs = [operand_spec(a_blk if q % 2 == 0 else b_blk, p) for q, p in enumerate(planes)]
    mn_arrays = [e[0] if isinstance(e, tuple) else e for e in mn]
    mn_specs = [pl.BlockSpec((tm, tn), lambda i, j, kk, c=e[1]: (i, c)) if isinstance(e, tuple) else mn_spec
                for e in mn]
    nk = grid[2]
    row_spec = pl.BlockSpec((1, tn), lambda i, j, kk: (0, j))
    n_ex, n_out = len(mn) + len(rows), len(outs)
    assert n_sums == 0 or (grid[1] == 1 and not bd)
    dims = (((0 if ta else 1,), (1 if tb else 0,)), ((), ()))

    def body(*refs):
        ab_refs, rest = refs[:len(ab)], refs[len(ab):]
        ex, o_refs, acc = rest[:n_ex], rest[n_ex:n_ex + n_out], rest[-1]
        s_refs = rest[n_ex + n_out:n_ex + n_out + n_sums]
        first_row_tile = pl.program_id(0) == 0
        kk = pl.program_id(2)

        @pl.when(kk == 0)
        def _():
            acc[...] = jnp.zeros_like(acc)

        for a_ref, b_ref in zip(ab_refs[0::2], ab_refs[1::2]):
            acc[...] += lax.dot_general(a_ref[...].astype(_MXU), b_ref[...].astype(_MXU), dims,
                                        preferred_element_type=F32)

        @pl.when(kk == nk - 1)
        def _():
            vals = (acc[...],) if epi is None else epi(acc[...], *[r[...] for r in ex])
            for r, v in zip(o_refs, vals):
                r[...] = v.astype(r.dtype)
            for r, v in zip(s_refs, vals[n_out:]):
                r[...] = jnp.where(first_row_tile, v, r[...] + v)

    res = pl.pallas_call(
        body, grid=grid,
        in_specs=ab_specs + mn_specs + [row_spec] * len(rows),
        out_specs=[mn_spec] * n_out + [row_spec] * n_sums,
        out_shape=[jax.ShapeDtypeStruct((m, out_n), dt) for dt in outs]
        + [jax.ShapeDtypeStruct((1, out_n), F32)] * n_sums,
        scratch_shapes=[pltpu.VMEM((tm, tn), F32)],
        compiler_params=_params(("arbitrary" if n_sums else "parallel", "parallel", "arbitrary")), name=name,
    )(*ab, *mn_arrays, *rows)
    return res[0] if n_out + n_sums == 1 else res


def _ew(fn, rows, bcs, out_rows, out_accs, *, name, tm=256):
    r = rows[0].shape[0]
    tm = min(tm, r)
    assert r % tm == 0
    nr, nb, no, na = len(rows), len(bcs), len(out_rows), len(out_accs)

    def body(*refs):
        i = pl.program_id(0)
        r_in, b_in = refs[:nr], refs[nr:nr + nb]
        o_r, o_a = refs[nr + nb:nr + nb + no], refs[nr + nb + no:]
        outs, accs = fn([x[...] for x in r_in], [x[...] for x in b_in])
        for ref, v in zip(o_r, outs):
            ref[...] = v.astype(ref.dtype)
        if na:
            @pl.when(i == 0)
            def _():
                for ref in o_a:
                    ref[...] = jnp.zeros_like(ref)

            for ref, v in zip(o_a, accs):
                ref[...] += v

    res = pl.pallas_call(
        body, grid=(r // tm,),
        in_specs=[pl.BlockSpec((tm, x.shape[1]), lambda i: (i, 0)) for x in rows]
        + [pl.BlockSpec((1, x.shape[1]), lambda i: (0, 0)) for x in bcs],
        out_specs=[pl.BlockSpec((tm, c), lambda i: (i, 0)) for c, _ in out_rows]
        + [pl.BlockSpec((1, c), lambda i: (0, 0)) for c in out_accs],
        out_shape=[jax.ShapeDtypeStruct((r, c), dt) for c, dt in out_rows]
        + [jax.ShapeDtypeStruct((1, c), F32) for c in out_accs],
        compiler_params=_params(("arbitrary",)), name=name,
    )(*rows, *bcs)
    return res


def _colsum(x):
    return jnp.sum(x, axis=0, keepdims=True)


def _sigmoid(x):
    return 1.0 / (1.0 + jnp.exp(-x))


def _rms_bwd_tile(xv, dv, g):
    rs = lax.rsqrt(jnp.mean(xv * xv, axis=-1, keepdims=True) + RMS_EPS)
    gd = dv * g
    dx = rs * gd - xv * (rs * rs * rs) * jnp.mean(gd * xv, axis=-1, keepdims=True)
    return dx, _colsum(dv * xv * rs)


def _rms_fwd(x, g, name):
    def fn(r, b):
        xv = r[0]
        rs = lax.rsqrt(jnp.mean(xv * xv, axis=-1, keepdims=True) + RMS_EPS)
        return [xv * rs * b[0]], []
    return _ew(fn, [x], [g], [(x.shape[1], BF16)], [], name=name)[0]


def _rms_bwd(x, dn, res, g, name):
    def fn(r, b):
        dx, dg = _rms_bwd_tile(r[0], r[1], b[0])
        if res is not None:
            dx = dx + r[2]
        return [dx], [dg]
    rows = [x, dn] + ([res] if res is not None else [])
    return _ew(fn, rows, [g], [(x.shape[1], F32)], [x.shape[1]], name=name)


def _scan_order(x):
    l, c = x.shape
    return x.reshape(l // (SUBLANES * SCAN_SEG), SUBLANES, SCAN_SEG, c).transpose(0, 2, 1, 3).reshape(l, c)


def _time_order(x):
    l, c = x.shape
    return x.reshape(l // (SUBLANES * SCAN_SEG), SCAN_SEG, SUBLANES, c).transpose(0, 2, 1, 3).reshape(l, c)


def _ssm_scan(x, w_re, w_im, a_pair, *, reverse, s_fwd=None, u=None, name):
    l = x.shape[0]
    seg, w = SCAN_SEG, SCAN_W
    bd_w = SSM_W // SSM_BD
    tiles_per_bd = SSM_S // SSM_BD // w
    nch = min(SCAN_CHAINS, l // (SUBLANES * seg))
    chain_rows = SUBLANES * seg
    tb = nch * chain_rows
    nt = l // tb
    with_da = s_fwd is not None
    assert reverse or not with_da

    def tt(t):
        return nt - 1 - t if reverse else t

    def body(*refs):
        if with_da:
            (x_ref, wr_ref, wi_ref, a_ref, sf_ref, sp_ref, u_ref, s_ref, da_ref, dw_ref, dx_ref,
             p_ref, c_ref, b_scr, l_scr) = refs
        else:
            x_ref, wr_ref, wi_ref, a_ref, s_ref, p_ref, c_ref, b_scr, l_scr = refs
        t_blk = pl.program_id(1)
        ar, ai = a_ref[0], a_ref[1]

        @pl.when(t_blk == 0)
        def _():
            def pstep(i, carry):
                pr, pi = carry
                p_ref[0, pl.ds(i, 1), :] = pr
                p_ref[1, pl.ds(i, 1), :] = pi
                return pr * ar - pi * ai, pr * ai + pi * ar

            lax.fori_loop(0, seg, pstep, (ar, ai))
            c_ref[...] = jnp.zeros_like(c_ref)
            if with_da:
                da_ref[...] = jnp.zeros_like(da_ref)
                dw_ref[...] = jnp.zeros_like(dw_ref)
                dx_ref[...] = jnp.zeros_like(dx_ref)

        xb = x_ref[...].astype(_MXU)
        b_scr[:, :w] = jnp.dot(xb, wr_ref[...], preferred_element_type=F32)
        b_scr[:, w:] = jnp.dot(xb, wi_ref[...], preferred_element_type=F32)
        arb, aib = jnp.broadcast_to(ar, (SUBLANES, w)), jnp.broadcast_to(ai, (SUBLANES, w))
        zero = jnp.zeros((SUBLANES, w), F32)

        def tile(g, step):
            return pl.ds(pl.multiple_of(g * chain_rows + step * SUBLANES, SUBLANES), SUBLANES)

        def rows(g, i):
            return tile(g, seg - 1 - i if reverse else i)

        def local_step(i, carry):
            out = []
            for g in range(nch):
                sr, si = carry[2 * g], carry[2 * g + 1]
                idx = rows(g, i)
                sr, si = arb * sr - aib * si + b_scr[idx, :w], arb * si + aib * sr + b_scr[idx, w:]
                l_scr[idx, :w] = sr
                l_scr[idx, w:] = si
                out += [sr, si]
            return tuple(out)

        def unrolled(step_fn, first):
            def trip(q, carry):
                for r in range(SCAN_UNROLL):
                    carry = step_fn(first + q * SCAN_UNROLL + r, carry)
                return carry
            return trip

        ends = lax.fori_loop(0, seg // SCAN_UNROLL, unrolled(local_step, 0), (zero,) * (2 * nch))

        a_seg_r, a_seg_i = p_ref[0, seg - 1:seg, :], p_ref[1, seg - 1:seg, :]
        cr, ci = c_ref[0], c_ref[1]
        sub = lax.broadcasted_iota(jnp.int32, (SUBLANES, w), 0)
        ins = [[zero, zero] for _ in range(nch)]
        order = [(g, k) for g in range(nch) for k in range(SUBLANES)]
        for g, k in (order[::-1] if reverse else order):
            ins[g] = [jnp.where(sub == k, cr, ins[g][0]), jnp.where(sub == k, ci, ins[g][1])]
            er, ei = ends[2 * g][k:k + 1], ends[2 * g + 1][k:k + 1]
            cr, ci = er + a_seg_r * cr - a_seg_i * ci, ei + a_seg_r * ci + a_seg_i * cr
        c_ref[0] = cr
        c_ref[1] = ci

        def fix(g, i):
            idx = rows(g, i)
            pr, pi = p_ref[0, pl.ds(i, 1), :], p_ref[1, pl.ds(i, 1), :]
            sr = l_scr[idx, :w] + pr * ins[g][0] - pi * ins[g][1]
            si = l_scr[idx, w:] + pr * ins[g][1] + pi * ins[g][0]
            s_ref.at[0][idx, :] = sr.astype(s_ref.dtype)
            s_ref.at[1][idx, :] = si.astype(s_ref.dtype)
            return sr, si

        if not with_da:
            def fix_step(i, carry):
                for g in range(nch):
                    fix(g, i)
                return carry

            lax.fori_loop(0, seg // SCAN_UNROLL, unrolled(fix_step, 0), 0)
        else:
            def adj_step(i, acc):
                acc_r, acc_i = acc
                for g in range(nch):
                    lr, li = fix(g, i)
                    prev = tile(g, seg - 2 - i)
                    fr, fi = sf_ref.at[0][prev, :].astype(F32), sf_ref.at[1][prev, :].astype(F32)
                    acc_r, acc_i = acc_r + lr * fr + li * fi, acc_i + li * fr - lr * fi
                return acc_r, acc_i

            acc = lax.fori_loop(0, seg // SCAN_UNROLL - 1, unrolled(adj_step, 0), (zero, zero))
            for i in range(seg - SCAN_UNROLL, seg - 1):
                acc = adj_step(i, acc)
            acc_r, acc_i = acc
            first_block = tt(t_blk) == 0
            for g in range(nch):
                lr, li = fix(g, seg - 1)
                seg_ends = tile(g, seg - 1)
                if g == 0:
                    pvr = jnp.where(first_block, 0.0, sp_ref[0, SUBLANES - 1:SUBLANES, :].astype(F32))
                    pvi = jnp.where(first_block, 0.0, sp_ref[1, SUBLANES - 1:SUBLANES, :].astype(F32))
                else:
                    pvr = sf_ref[0, g * chain_rows - 1:g * chain_rows, :].astype(F32)
                    pvi = sf_ref[1, g * chain_rows - 1:g * chain_rows, :].astype(F32)
                fr = jnp.where(sub == 0, pvr, pltpu.roll(sf_ref.at[0][seg_ends, :].astype(F32), 1, 0))
                fi = jnp.where(sub == 0, pvi, pltpu.roll(sf_ref.at[1][seg_ends, :].astype(F32), 1, 0))
                acc_r = acc_r + lr * fr + li * fi
                acc_i = acc_i + li * fr - lr * fi
            da_ref[0] += jnp.sum(acc_r, axis=0, keepdims=True)
            da_ref[1] += jnp.sum(acc_i, axis=0, keepdims=True)
            for plane in range(2):
                dw_ref[plane] += _tn_dot(u_ref[...], s_ref[plane])
                dx_ref[plane] += _tn_dot(xb, sf_ref[plane])

    x_spec = pl.BlockSpec((tb, bd_w), lambda j, t: (tt(t), j // tiles_per_bd))
    w_spec = pl.BlockSpec((bd_w, w), lambda j, t: (j // tiles_per_bd, j))
    d_spec = pl.BlockSpec((2, bd_w, w), lambda j, t: (0, j // tiles_per_bd, j % tiles_per_bd))
    a_spec = pl.BlockSpec((2, 1, w), lambda j, t: (0, 0, j))
    s_spec = pl.BlockSpec((2, tb, w), lambda j, t: (0, tt(t), j))
    in_specs, args = [x_spec, w_spec, w_spec, a_spec], [x, w_re, w_im, a_pair]
    out_specs, out_shape = [s_spec], [jax.ShapeDtypeStruct((2, l, SSM_S), BF16)]
    scratch = [pltpu.VMEM((2, seg, w), F32), pltpu.VMEM((2, 1, w), F32)] + [pltpu.VMEM((tb, 2 * w), F32)] * 2
    if with_da:
        in_specs += [s_spec, pl.BlockSpec((2, SUBLANES, w),
                                          lambda j, t: (0, jnp.maximum(tt(t) * (tb // SUBLANES) - 1, 0), j)),
                     x_spec]
        args += [s_fwd, s_fwd, u]
        out_specs += [a_spec, d_spec, d_spec]
        out_shape += ([jax.ShapeDtypeStruct((2, 1, SSM_S), F32)]
                      + [jax.ShapeDtypeStruct((2, SSM_W, SSM_S // SSM_BD), F32)] * 2)
    res = pl.pallas_call(
        body, grid=(SSM_S // w, nt), in_specs=in_specs, out_specs=out_specs, out_shape=out_shape,
        scratch_shapes=scratch, compiler_params=_params(("parallel", "arbitrary")), name=name,
    )(*args)
    return res if with_da else res[0]


def _nt_dot(x, y):
    return lax.dot_general(x.astype(_MXU), y.astype(_MXU), (((1,), (1,)), ((), ())), preferred_element_type=F32)


def _tn_dot(x, y):
    return lax.dot_general(x.astype(_MXU), y.astype(_MXU), (((0,), (0,)), ((), ())), preferred_element_type=F32)


def _nn_dot(x, y):
    return jnp.dot(x.astype(_MXU), y.astype(_MXU), preferred_element_type=F32)


def _attn_mask2(gb, nb):
    qi = lax.broadcasted_iota(jnp.int32, (ATT_WIN, 2 * ATT_WIN), 0)
    c = lax.broadcasted_iota(jnp.int32, (ATT_WIN, 2 * ATT_WIN), 1)
    has_prev = (gb % nb) != 0
    prev_ok = jnp.logical_and(jnp.logical_and(c < ATT_WIN, c >= qi), has_prev)
    own_ok = jnp.logical_and(c >= ATT_WIN, c - ATT_WIN <= qi)
    return jnp.logical_or(prev_ok, own_ok)


def _attn_specs(qb):
    cur = pl.BlockSpec((qb * ATT_WIN, ATT_GW), lambda i: (i, 0))
    prev = pl.BlockSpec((ATT_WIN, ATT_GW), lambda i: (jnp.maximum(qb * i - 1, 0), 0))
    return cur, prev


def _attn_fwd(q, k, v, nb, name):
    l = q.shape[0]
    scale = ATT_E ** -0.5
    w = ATT_WIN

    qb = ATT_QB_FWD

    def body(q_ref, kc_ref, kp_ref, vc_ref, vp_ref, o_ref, lse_ref):
        i = pl.program_id(0)
        masks = [_attn_mask2(qb * i + b, nb) for b in range(qb)]
        for h in range(ATT_HG):
            sl = slice(h * ATT_E, (h + 1) * ATT_E)
            k_ext = jnp.concatenate([kp_ref[:, sl], kc_ref[:, sl]], axis=0)
            v_ext = jnp.concatenate([vp_ref[:, sl], vc_ref[:, sl]], axis=0)
            for b in range(qb):
                r, kr = slice(b * w, (b + 1) * w), slice(b * w, (b + 2) * w)
                s = jnp.where(masks[b], _nt_dot(q_ref[r, sl], k_ext[kr]) * scale, NEG_INF)
                mx = jnp.max(s, axis=-1, keepdims=True)
                p = jnp.exp(s - mx)
                den = jnp.sum(p, axis=-1, keepdims=True)
                o_ref[r, sl] = _nn_dot(p, v_ext[kr]) / den
                lse_ref[r, sl] = jnp.broadcast_to(mx + jnp.log(den), (w, ATT_E))

    cur, prev = _attn_specs(qb)
    return pl.pallas_call(
        body, grid=(l // (qb * w),), in_specs=[cur, cur, prev, cur, prev], out_specs=[cur, cur],
        out_shape=[jax.ShapeDtypeStruct((l, ATT_GW), F32)] * 2,
        compiler_params=_params(("parallel",)), name=name,
    )(q, k, k, v, v)


def _attn_bwd(q, k, v, do, lse, dd, nb, name):
    l = q.shape[0]
    scale = ATT_E ** -0.5
    w = ATT_WIN
    nblk = l // w

    def body(q_ref, kc_ref, kp_ref, vc_ref, vp_ref, do_ref, lse_ref, dd_ref, qn_ref, don_ref, lsen_ref, ddn_ref,
             dq_ref, dk_ref, dv_ref, dk_acc, dv_acc):
        i = pl.program_id(0)
        masks = [_attn_mask2(ATT_QB * i + b, nb) for b in range(ATT_QB)]
        nxt = ATT_QB * (i + 1)
        nxt_attends = jnp.logical_and(nxt < nblk, (nxt % nb) != 0)
        qi = lax.broadcasted_iota(jnp.int32, (w, w), 0)
        kj = lax.broadcasted_iota(jnp.int32, (w, w), 1)
        mask_n = jnp.logical_and(kj >= qi, nxt_attends)
        dk_acc[...] = jnp.zeros_like(dk_acc)
        dv_acc[...] = jnp.zeros_like(dv_acc)
        for h in range(ATT_HG):
            sl, col = slice(h * ATT_E, (h + 1) * ATT_E), slice(h * ATT_E, h * ATT_E + 1)
            k_ext = jnp.concatenate([kp_ref[:, sl], kc_ref[:, sl]], axis=0)
            v_ext = jnp.concatenate([vp_ref[:, sl], vc_ref[:, sl]], axis=0)
            for b in range(ATT_QB):
                r, kr = slice(b * w, (b + 1) * w), slice(b * w, (b + 2) * w)
                qh, doh, k2, v2 = q_ref[r, sl], do_ref[r, sl], k_ext[kr], v_ext[kr]
                p = jnp.where(masks[b], jnp.exp(_nt_dot(qh, k2) * scale - lse_ref[r, col]), 0.0)
                ds = p * (_nt_dot(doh, v2) - dd_ref[r, col]) * scale
                dq_ref[r, sl] = _nn_dot(ds, k2).astype(dq_ref.dtype)
                dk2, dv2 = _tn_dot(ds, qh), _tn_dot(p, doh)
                dk_acc[r, sl] += dk2[w:]
                dv_acc[r, sl] += dv2[w:]
                if b > 0:
                    rp = slice((b - 1) * w, b * w)
                    dk_acc[rp, sl] += dk2[:w]
                    dv_acc[rp, sl] += dv2[:w]
            last = slice((ATT_QB - 1) * w, ATT_QB * w)
            kl, vl, qn, don = kc_ref[last, sl], vc_ref[last, sl], qn_ref[:, sl], don_ref[:, sl]
            pn = jnp.where(mask_n, jnp.exp(_nt_dot(qn, kl) * scale - lsen_ref[:, col]), 0.0)
            dsn = pn * (_nt_dot(don, vl) - ddn_ref[:, col]) * scale
            dk_acc[last, sl] += _tn_dot(dsn, qn)
            dv_acc[last, sl] += _tn_dot(pn, don)
        dk_ref[...] = dk_acc[...].astype(dk_ref.dtype)
        dv_ref[...] = dv_acc[...].astype(dv_ref.dtype)

    cur, prev = _attn_specs(ATT_QB)
    nxt_spec = pl.BlockSpec((w, ATT_GW), lambda i: (jnp.minimum(ATT_QB * (i + 1), nblk - 1), 0))
    return pl.pallas_call(
        body, grid=(l // (ATT_QB * w),),
        in_specs=[cur, cur, prev, cur, prev, cur, cur, cur, nxt_spec, nxt_spec, nxt_spec, nxt_spec],
        out_specs=[cur] * 3, out_shape=[jax.ShapeDtypeStruct((l, ATT_GW), BF16)] * 3,
        scratch_shapes=[pltpu.VMEM((ATT_QB * w, ATT_GW), F32)] * 2,
        compiler_params=_params(("parallel",)), name=name,
    )(q, k, k, v, v, do, lse, dd, q, do, lse, dd)


def _to_perm(a, d):
    if d == 1:
        return a
    l, c = a.shape
    return a.reshape(l // d, d, c).transpose(1, 0, 2).reshape(l, c)


def _from_perm(a, d):
    if d == 1:
        return a
    l, c = a.shape
    return a.reshape(d, l // d, c).transpose(1, 0, 2).reshape(l, c)


def _mem_probs(qh, kh):
    s = _nt_dot(qh, kh) * (MEM_E ** -0.5)
    e = jnp.exp(s - jnp.max(s, axis=-1, keepdims=True))
    return e / jnp.sum(e, axis=-1, keepdims=True)


def _mem_fwd(mq, kv, name, tm=512):
    l, nm = mq.shape[0], kv.shape[0]

    def body(q_ref, kv_ref, o_ref):
        for h in range(MEM_H):
            sl = slice(h * MEM_E, (h + 1) * MEM_E)
            p = _mem_probs(q_ref[:, sl], kv_ref[:, sl])
            o_ref[:, sl] = _nn_dot(p, kv_ref[:, MEM_W + h * MEM_E:MEM_W + (h + 1) * MEM_E]).astype(o_ref.dtype)

    return pl.pallas_call(
        body, grid=(l // tm,),
        in_specs=[pl.BlockSpec((tm, MEM_W), lambda i: (i, 0)), pl.BlockSpec((nm, 2 * MEM_W), lambda i: (0, 0))],
        out_specs=pl.BlockSpec((tm, MEM_W), lambda i: (i, 0)),
        out_shape=jax.ShapeDtypeStruct((l, MEM_W), BF16),
        compiler_params=_params(("parallel",)), name=name,
    )(mq, kv)


def _mem_bwd(mq, kv, dmo, name, tm=512):
    l, nm = mq.shape[0], kv.shape[0]
    scale = MEM_E ** -0.5

    def body(q_ref, kv_ref, do_ref, dq_ref, dkv_ref):
        @pl.when(pl.program_id(0) == 0)
        def _():
            dkv_ref[...] = jnp.zeros_like(dkv_ref)

        for h in range(MEM_H):
            sl = slice(h * MEM_E, (h + 1) * MEM_E)
            vsl = slice(MEM_W + h * MEM_E, MEM_W + (h + 1) * MEM_E)
            qh, kh, vh, doh = q_ref[:, sl], kv_ref[:, sl], kv_ref[:, vsl], do_ref[:, sl]
            p = _mem_probs(qh, kh)
            dp = _nt_dot(doh, vh)
            ds = p * (dp - jnp.sum(dp * p, axis=-1, keepdims=True)) * scale
            dq_ref[:, sl] = _nn_dot(ds, kh).astype(dq_ref.dtype)
            dkv_ref[:, sl] += _tn_dot(ds, qh)
            dkv_ref[:, vsl] += _tn_dot(p, doh)

    row = pl.BlockSpec((tm, MEM_W), lambda i: (i, 0))
    full = pl.BlockSpec((nm, 2 * MEM_W), lambda i: (0, 0))
    return pl.pallas_call(
        body, grid=(l // tm,), in_specs=[row, full, row], out_specs=[row, full],
        out_shape=[jax.ShapeDtypeStruct((l, MEM_W), BF16), jax.ShapeDtypeStruct((nm, 2 * MEM_W), F32)],
        compiler_params=_params(("arbitrary",)), name=name,
    )(mq, kv, dmo)


def _gated_out_proj(zg, branches, b_gate, w_o, x, g2, name, tm=512):
    l, d = x.shape
    nbr = len(branches)

    def body(zg_ref, *rest):
        br_refs, (bg_ref, w_ref, x_ref, g2_ref, m_ref, h_ref, n_ref) = rest[:nbr], rest[nbr:]
        merged = jnp.zeros((tm, d), F32)
        for i, br_ref in enumerate(br_refs):
            cols = slice(i * d, (i + 1) * d)
            merged += _sigmoid(zg_ref[:, cols].astype(F32) + bg_ref[:, cols]) * br_ref[...].astype(F32)
        mb = merged.astype(BF16)
        m_ref[...] = mb
        hv = jnp.dot(mb.astype(_MXU), w_ref[...].astype(_MXU), preferred_element_type=F32) + x_ref[...]
        h_ref[...] = hv
        rs = lax.rsqrt(jnp.mean(hv * hv, axis=-1, keepdims=True) + RMS_EPS)
        n_ref[...] = (hv * rs * g2_ref[...]).astype(n_ref.dtype)

    row = lambda c: pl.BlockSpec((tm, c), lambda i: (i, 0))
    full = lambda a: pl.BlockSpec(a.shape, lambda i: (0, 0))
    return pl.pallas_call(
        body, grid=(l // tm,),
        in_specs=[row(nbr * d)] + [row(d)] * nbr + [full(b_gate), full(w_o), row(d), full(g2)],
        out_specs=[row(d)] * 3,
        out_shape=[jax.ShapeDtypeStruct((l, d), BF16), jax.ShapeDtypeStruct((l, d), F32),
                   jax.ShapeDtypeStruct((l, d), BF16)],
        compiler_params=_params(("parallel",)), name=name,
    )(zg, *branches, b_gate, w_o, x, g2)


def _discretize(lam_re, lam_im, log_dt, b_re, b_im):
    dt = jnp.exp(log_dt)[:, None]
    mag = jnp.exp(lam_re * dt)
    a_re, a_im = mag * jnp.cos(lam_im * dt), mag * jnp.sin(lam_im * dt)
    nr, ni = a_re - 1.0, a_im
    den = lam_re * lam_re + lam_im * lam_im
    coef_re = (nr * lam_re + ni * lam_im) / den
    coef_im = (ni * lam_re - nr * lam_im) / den
    bb_re = coef_re[..., None] * b_re - coef_im[..., None] * b_im
    bb_im = coef_re[..., None] * b_im + coef_im[..., None] * b_re
    return a_re, a_im, bb_re, bb_im


def _bd_in(bb):
    return jnp.einsum("gph,gk->ghkp", bb, jnp.eye(SSM_G, dtype=bb.dtype)).reshape(SSM_W, SSM_S)


def _bd_diag(x):
    gb = SSM_G // SSM_BD
    t = x.reshape(SSM_BD, gb, SSM_H, gb, SSM_P)
    return jnp.einsum("bghgp->bghp", t).reshape(SSM_G, SSM_H, SSM_P)


_ANY = pl.BlockSpec(memory_space=pl.ANY)
_MESH = pl.DeviceIdType.MESH


def _allgather(x, name):
    def body(x_ref, out_ref, send_sems, recv_sems, local_sem):
        mx, my, mc = lax.axis_index("x"), lax.axis_index("y"), lax.axis_index("c")
        me, sibling = (mx, my, mc), (mx, my, 1 - mc)
        chips = [(1 - mx, my), (mx, 1 - my), (1 - mx, 1 - my)]

        def blk(px, py, pc):
            return out_ref.at[4 * px + 2 * py + pc]

        def copy(k, block, to, src=None):
            return pltpu.make_async_remote_copy(
                src_ref=blk(*block) if src is None else src, dst_ref=blk(*block),
                send_sem=send_sems.at[k], recv_sem=recv_sems.at[k], device_id=to, device_id_type=_MESH)

        mine = pltpu.make_async_copy(x_ref, blk(*me), local_sem)
        mine.start()
        first = [copy(0, me, sibling, src=x_ref)]
        first += [copy(1 + j, me, (*chip, mc), src=x_ref) for j, chip in enumerate(chips)]
        for cp in first:
            cp.start()
        passed = [copy(4 + j, (*chip, mc), sibling) for j, chip in enumerate(chips)]
        for j, chip in enumerate(chips):
            copy(1 + j, (*chip, mc), me).wait_recv()
            passed[j].start()
        copy(0, sibling, me).wait_recv()
        for j, chip in enumerate(chips):
            copy(4 + j, (*chip, 1 - mc), me).wait_recv()
        for cp in first + passed:
            cp.wait_send()
        mine.wait()

    return pl.pallas_call(
        body, out_shape=jax.ShapeDtypeStruct((N_DEV,) + x.shape, x.dtype), in_specs=[_ANY], out_specs=_ANY,
        scratch_shapes=[pltpu.SemaphoreType.DMA((7,)), pltpu.SemaphoreType.DMA((7,)), pltpu.SemaphoreType.DMA],
        name=name,
    )(x)


def _pair_exchange(g, name):
    def body(g_ref, out_ref, send_sems, recv_sems):
        mx, my, mc = lax.axis_index("x"), lax.axis_index("y"), lax.axis_index("c")
        copies = [pltpu.make_async_remote_copy(
            src_ref=g_ref.at[2 * k + (1 - mc)], dst_ref=out_ref.at[k], send_sem=send_sems.at[k],
            recv_sem=recv_sems.at[k], device_id=(mx, my, 1 - mc), device_id_type=_MESH) for k in range(4)]
        for cp in copies:
            cp.start()
        for cp in copies:
            cp.wait()

    return pl.pallas_call(
        body, out_shape=jax.ShapeDtypeStruct((4,) + g.shape[1:], g.dtype), in_specs=[_ANY], out_specs=_ANY,
        scratch_shapes=[pltpu.SemaphoreType.DMA((4,)), pltpu.SemaphoreType.DMA((4,))], name=name,
    )(g)


_HBM = pl.BlockSpec(memory_space=pltpu.HBM)
_SEM = pl.BlockSpec(memory_space=pltpu.SEMAPHORE)
_EFFECT = pltpu.SideEffectType.DATAFLOW_SIDE_EFFECTING
_TOKEN = jax.ShapeDtypeStruct((8, 128), F32)


def _peer(rel):
    pos = (lax.axis_index("x"), lax.axis_index("y"), lax.axis_index("c"))
    return tuple(1 - p if (rel >> (2 - i)) & 1 else p for i, p in enumerate(pos))


def _index_of(dev):
    return 4 * dev[0] + 2 * dev[1] + dev[2]


def _split_copies(src_ref, land_ref, sems, plan):
    n = len(plan)
    return [pltpu.make_async_remote_copy(
        src_ref=src_ref if s is None else src_ref.at[s], dst_ref=land_ref.at[d], send_sem=sems[k],
        recv_sem=sems[n + k], device_id=peer, device_id_type=_MESH) for k, (s, d, peer) in enumerate(plan)]


def _split_start(src, n_land, plan_fn, after, name):
    blk = src.shape[-2:]
    land = lax.empty((n_land,) + blk, src.dtype)
    n = len(plan_fn())

    def body(src_ref, land_ref, after_ref, *outs):
        for cp in _split_copies(src_ref, land_ref, outs[:2 * n], plan_fn()):
            cp.start()
        outs[2 * n + 2][...] = jnp.zeros_like(outs[2 * n + 2])

    res = pl.pallas_call(
        body, name=name,
        out_shape=(pltpu.SemaphoreType.DMA(()),) * (2 * n)
        + (pltpu.HBM(src.shape, src.dtype), pltpu.HBM(land.shape, land.dtype), _TOKEN),
        in_specs=(_HBM, _HBM, _ANY),
        out_specs=(_SEM,) * (2 * n) + (_HBM, _HBM, pl.BlockSpec(memory_space=pltpu.VMEM)),
        input_output_aliases={0: 2 * n, 1: 2 * n + 1},
        compiler_params=pltpu.CompilerParams(has_side_effects=_EFFECT),
    )(pltpu.with_memory_space_constraint(src, pltpu.HBM), pltpu.with_memory_space_constraint(land, pltpu.HBM), after)
    return res[:2 * n], res[2 * n], res[2 * n + 1], res[2 * n + 2]


def _split_wait(sems, src, land, plan_fn, after, name):
    n = len(sems) // 2

    def body(src_ref, land_ref, *rest):
        for cp in _split_copies(src_ref, land_ref, rest[:2 * n], plan_fn()):
            cp.wait_send()
            cp.wait_recv()

    return pl.pallas_call(
        body, name=name,
        out_shape=(pltpu.HBM(src.shape, src.dtype), pltpu.HBM(land.shape, land.dtype)),
        in_specs=(_HBM, _HBM) + (_SEM,) * (2 * n) + (_ANY,), out_specs=(_HBM, _HBM),
        input_output_aliases={0: 0, 1: 1},
        compiler_params=pltpu.CompilerParams(has_side_effects=_EFFECT),
    )(src, land, *sems, after)


def _gather_plan():
    me = _index_of(_peer(0))
    return [(None, me, _peer(rel)) for rel in range(1, N_DEV)]


def _gather_wait_plan():
    return [(None, _index_of(_peer(rel)), _peer(rel)) for rel in range(1, N_DEV)]


def _chip_plan():
    return [(_index_of(_peer(rel)) // 2, j, _peer(rel)) for j, rel in enumerate((4, 2, 6))]


def _owner_plan():
    return [(_index_of(_peer(rel)), rel - 1, _peer(rel)) for rel in range(1, N_DEV)]


def _pair_sum(g, t1, my_c, name, tr):
    _, r, c = g.shape

    def body(c_ref, g_ref, t_ref, o_ref, ob_ref):
        s = g_ref[...] + t_ref[...]
        o_ref[...] = s
        ob_ref[...] = s.astype(BF16)

    blk = pl.BlockSpec((None, tr, c), lambda k, i, cr: (k, i, 0))
    return pl.pallas_call(
        body,
        grid_spec=pltpu.PrefetchScalarGridSpec(
            num_scalar_prefetch=1, grid=(4, r // tr),
            in_specs=[pl.BlockSpec((None, tr, c), lambda k, i, cr: (2 * k + cr[0], i, 0)), blk],
            out_specs=[blk, blk]),
        out_shape=[jax.ShapeDtypeStruct((4, r, c), F32), jax.ShapeDtypeStruct((4, r, c), BF16)],
        compiler_params=_params(("parallel", "parallel")), name=name,
    )(my_c, g, t1)


def _adam_math(g, w, m, v):
    m = ADAM_B1 * m + (1.0 - ADAM_B1) * g
    v = ADAM_B2 * v + (1.0 - ADAM_B2) * (g * g)
    m_hat = m / (1.0 - ADAM_B1 ** ADAM_STEP)
    v_hat = v / (1.0 - ADAM_B2 ** ADAM_STEP)
    delta = -ADAM_LR * (m_hat / (jnp.sqrt(v_hat) + ADAM_EPS) + ADAM_WD * w)
    return delta, m, v


def _grad_sum(own, own_index, recv, name, tr):
    _, r, c = own.shape
    n = recv.shape[0]

    def body(k_ref, own_ref, *rest):
        g = own_ref[...]
        for recv_ref in rest[:n]:
            g = g + recv_ref[...].astype(F32)
        rest[n][...] = g

    def slot(j):
        return pl.BlockSpec((None, tr, c), lambda i, kr: (j, i, 0))

    return pl.pallas_call(
        body,
        grid_spec=pltpu.PrefetchScalarGridSpec(
            num_scalar_prefetch=1, grid=(r // tr,),
            in_specs=[pl.BlockSpec((None, tr, c), lambda i, kr: (kr[0], i, 0))] + [slot(j) for j in range(n)],
            out_specs=pl.BlockSpec((tr, c), lambda i, kr: (i, 0))),
        out_shape=jax.ShapeDtypeStruct((r, c), F32),
        compiler_params=_params(("parallel",)), name=name,
    )(own_index, own, *([recv] * n))


def _adam_many(g, w, m, v, row_tiles, name):
    n = len(g)

    def body(*refs):
        ins, outs = refs[:4 * n], refs[4 * n:]
        for i in range(n):
            res = _adam_math(ins[i][...], ins[n + i][...], ins[2 * n + i][...], ins[3 * n + i][...])
            for kind in range(3):
                outs[kind * n + i][...] = res[kind]

    def spec(a):
        blk = (a.shape[0] // row_tiles,) + a.shape[1:]
        return pl.BlockSpec(blk, lambda t, nd=a.ndim: (t,) + (0,) * (nd - 1))

    specs = [spec(a) for a in g]
    res = pl.pallas_call(
        body, grid=(row_tiles,), in_specs=specs * 4, out_specs=specs * 3,
        out_shape=[jax.ShapeDtypeStruct(a.shape, F32) for a in g] * 3,
        compiler_params=_params(("parallel",)), name=name,
    )(*g, *w, *m, *v)
    return res[:n], res[n:2 * n], res[2 * n:]


def _sum8(g8, name):
    _, r, c = g8.shape

    def body(g_ref, o_ref):
        acc = g_ref[0]
        for j in range(1, N_DEV):
            acc = acc + g_ref[j]
        o_ref[...] = acc

    return pl.pallas_call(
        body, grid=(1,), in_specs=[pl.BlockSpec((N_DEV, r, c), lambda i: (0, 0, 0))],
        out_specs=pl.BlockSpec((r, c), lambda i: (0, 0)), out_shape=jax.ShapeDtypeStruct((r, c), F32),
        compiler_params=_params(("arbitrary",)), name=name,
    )(g8)


def _pack(arrs, pad_rows=8):
    flat = jnp.concatenate([a.reshape(-1) for a in arrs])
    n = flat.shape[0]
    q = PACK_C * pad_rows
    tot = -(-n // q) * q
    if tot != n:
        flat = jnp.concatenate([flat, jnp.zeros((tot - n,), flat.dtype)])
    return flat.reshape(tot // PACK_C, PACK_C)


def _unpack(buf, shapes):
    flat = buf.reshape(-1)
    out, off = [], 0
    for s in shapes:
        n = int(np.prod(s))
        out.append(flat[off:off + n].reshape(s))
        off += n
    return out


GROUPS = (("w_in",),
          ("w_glu", "w_ssm_br", "w_mem_br", "w_attn_br"),
          ("w_up", "w_down"),
          ("w_mem_kv", "w_o"))
GROUP_TR = (400, 384, 512, 256)
MLP_GROUP = 2
MIXER_GROUPS = (1, 3)
ATTN_BR_FOLD = 2


def _stored_shape(name):
    r, c, ax = BIG_SHAPE[name]
    rows, cols = (r // N_DEV, c) if ax == 0 else (c // N_DEV, r)
    return (rows // ATTN_BR_FOLD, cols * ATTN_BR_FOLD) if name == "w_attn_br" else (rows, cols)


def _stored(shard, name):
    a = shard[0].T if BIG_SHAPE[name][2] == 1 else shard[0]
    return a.reshape(_stored_shape(name))


def _unstored(a, name):
    r, c, ax = BIG_SHAPE[name]
    if ax == 0:
        return a.reshape(1, r // N_DEV, c)
    return a.reshape(c // N_DEV, r).T[None]


def _pack_group(d, names):
    return jnp.concatenate([_stored(d[n], n) for n in names], axis=0)


def _split_group(buf, names):
    out, off = {}, 0
    for n in names:
        rows = _stored_shape(n)[0]
        out[n] = buf[..., off:off + rows, :]
        off += rows
    return out


def _full_stored(stacked, name):
    r, c, ax = BIG_SHAPE[name]
    return stacked.reshape((r, c) if ax == 0 else (c, r))


def _stacked_stored(full, name):
    return full.reshape((N_DEV,) + _stored_shape(name))


def _gelu_parts(x):
    c0, c1 = math.sqrt(2.0 / math.pi), 0.044715
    th = jnp.tanh(c0 * (x + c1 * x * x * x))
    return th, c0, c1


def _local_step(x, mem, tgt, wb, sp, late_weights, grads_ready, small_grads_ready):
    l = x.shape[0]
    w_a, w_g = wb["w_in"][:ZA_W], wb["w_in"][ZA_W:]

    a_re, a_im, bb_re, bb_im = _discretize(sp["ssm_lambda_re"], sp["ssm_lambda_im"], sp["ssm_log_dt"],
                                           sp["ssm_b_re"], sp["ssm_b_im"])
    a_pair = jnp.stack([a_re.reshape(1, SSM_S), a_im.reshape(1, SSM_S)])
    a_conj = jnp.stack([a_re.reshape(1, SSM_S), -a_im.reshape(1, SSM_S)])
    b_re_t, b_im_t = _bd_in(bb_re).astype(BF16), _bd_in(bb_im).astype(BF16)
    c_re_t = _bd_in(sp["ssm_c_re"].transpose(0, 2, 1)).astype(BF16)
    c_im_t = (-_bd_in(sp["ssm_c_im"].transpose(0, 2, 1))).astype(BF16)
    d_row = sp["ssm_d"].reshape(1, SSM_W)

    n1 = _rms_fwd(x, sp["norm1_g"], "rms1")
    za = _mm(n1, w_a, [BF16], tb=True, name="in_proj_a", tn=1664)
    zg = _mm(n1, w_g, [BF16], tb=True, name="in_proj_g")
    for gi in MIXER_GROUPS:
        wb = {**wb, **late_weights(gi, za)}
    u = za[:, :SSM_W]
    mq = za[:, ZA_W - MEM_W:]

    u_s = _scan_order(u)
    s_all = _ssm_scan(u_s, b_re_t, b_im_t, a_pair, reverse=False, name="ssm_scan_fwd")
    ys = _time_order(_mm((s_all, 0), c_re_t, [F32], tb=True, pair2=((s_all, 1), c_im_t), bd=SSM_BD, tm=2048, name="ssm_cs"))

    def gelu_fn(r, b):
        y0 = r[0] + b[0] * r[1].astype(F32)
        th, _, _ = _gelu_parts(y0)
        return [y0, 0.5 * y0 * (1.0 + th)], []
    y0, y1 = _ew(gelu_fn, [ys, u], [d_row], [(SSM_W, F32), (SSM_W, BF16)], [], name="ssm_gelu", tm=512)

    def glu_epi(acc, y1t, bg):
        t = acc + bg
        return t, y1t.astype(F32) * _sigmoid(t)
    t_glu, y2 = _mm(y1, wb["w_glu"], [F32, BF16], epi=glu_epi, mn=[y1], rows=[sp["b_glu"]], name="ssm_glu")
    br_ssm = _mm(y2, wb["w_ssm_br"], [BF16], tb=True, name="ssm_br")

    qkv_p, o_g, lse_g = [], [], []
    for g, d in enumerate(DILATIONS):
        nb = l // d // ATT_WIN
        cols = [za[:, SSM_W + (3 * j + g) * ATT_GW: SSM_W + (3 * j + g + 1) * ATT_GW] for j in range(3)]
        qp, kp, vp = [_to_perm(cc, d) for cc in cols]
        qkv_p.append((qp, kp, vp))
        og, lg = _attn_fwd(qp, kp, vp, nb, "attn_fwd%d" % g)
        o_g.append(_from_perm(og, d))
        lse_g.append(_from_perm(lg, d))

    def merge_fn(r, b):
        o0, o1, o2, l0, l1, l2 = r
        mx = jnp.maximum(jnp.maximum(l0, l1), l2)
        e0, e1, e2 = jnp.exp(l0 - mx), jnp.exp(l1 - mx), jnp.exp(l2 - mx)
        tot = e0 + e1 + e2
        return [(e0 * o0 + e1 * o1 + e2 * o2) / tot, mx + jnp.log(tot)], []
    o_att, lse_tot = _ew(merge_fn, o_g + lse_g, [], [(ATT_GW, F32), (ATT_GW, F32)], [], name="attn_merge", tm=512)
    br_attn = _mm(o_att, wb["w_attn_br"], [BF16], tb=True, name="attn_br")

    mn = _rms_fwd(mem, sp["mem_norm_g"], "rms_mem")
    kv = _mm(mn, wb["w_mem_kv"], [BF16], name="mem_kv")
    mo = _mem_fwd(mq, kv, "mem_attn_fwd")
    br_mem = _mm(mo, wb["w_mem_br"], [BF16], tb=True, name="mem_br")

    merged, h1, n2 = _gated_out_proj(zg, [br_ssm, br_attn, br_mem], sp["b_gate"], wb["w_o"], x, sp["norm2_g"],
                                     "gated_o_proj")

    def up_epi(acc):
        ra = jnp.maximum(acc, 0.0)
        return ra * ra, ra
    wm = late_weights(MLP_GROUP, n2)
    f_act, r_act = _mm(n2, wm["w_up"], [BF16, BF16], tb=True, epi=up_epi, name="mlp_up")
    def down_epi(acc, ht, tv, gf):
        hv = acc + ht
        rs = lax.rsqrt(jnp.mean(hv * hv, axis=-1, keepdims=True) + RMS_EPS)
        err = hv * rs * gf - tv
        dh, dgf = _rms_bwd_tile(hv, err * (1.0 / D_MODEL), gf)
        return dh, dgf, _colsum(err * err) * (0.5 / D_MODEL)
    dh2, d_final_g, loss_cols = _mm(f_act, wm["w_down"], [F32], epi=down_epi, mn=[h1, tgt], rows=[sp["final_g"]],
                                    n_sums=2, tk=1024, name="mlp_down")
    loss = jnp.sum(loss_cols, axis=1, keepdims=True)

    gw, gs = {}, {"final_g": d_final_g}
    d_act = _mm(dh2, wm["w_down"], [BF16], tb=True, epi=lambda acc, ra: (acc * 2.0 * ra.astype(F32),), mn=[r_act],
                name="mlp_down_dx")
    dw_down = _mm(f_act, dh2, [F32], ta=True, name="mlp_down_dw")
    dw_up = _mm(d_act, n2, [F32], ta=True, name="mlp_up_dw")
    token = grads_ready(MLP_GROUP, {"w_up": dw_up, "w_down": dw_down})
    def up_dx_epi(acc, ht, dht, g2):
        dx, dg = _rms_bwd_tile(ht, acc, g2)
        return dx + dht, dg
    dh1, gs["norm2_g"] = _mm(d_act, wm["w_up"], [F32], epi=up_dx_epi, mn=[h1, dh2],
                             rows=[sp["norm2_g"] + token[:1, :1]], n_sums=1, tk=1024, name="mlp_up_dx")
    gw["w_o"] = _mm(merged, dh1, [F32], ta=True, name="o_proj_dw")

    def gate_bwd_epi(dm, *tiles):
        dbr, dz = [], []
        for zt, bt, bias in zip(tiles[0:3], tiles[3:6], tiles[6:9]):
            gt = _sigmoid(zt.astype(F32) + bias)
            dbr.append(dm * gt)
            dz.append(dm * bt.astype(F32) * gt * (1.0 - gt))
        return (*dbr, *dz, *[_colsum(t) for t in dz])
    gate_bias = [sp["b_gate"][:, i * D_MODEL:(i + 1) * D_MODEL] for i in range(3)]
    res = _mm(dh1, wb["w_o"], [BF16] * 6, tb=True, epi=gate_bwd_epi, mn=[(zg, 0), (zg, 1), (zg, 2), br_ssm, br_attn, br_mem],
              rows=gate_bias, n_sums=3, tm=512, name="o_proj_dx")
    (dbr_ssm, dbr_attn, dbr_mem), dzg = res[0:3], res[3:6]
    gs["b_gate"] = jnp.concatenate(res[6:9], axis=1)

    gw["w_ssm_br"] = _mm(dbr_ssm, y2, [F32], ta=True, name="ssm_br_dw")
    def glu_bwd_epi(dy, y1t, tt):
        sg = _sigmoid(tt)
        dt = dy * y1t.astype(F32) * sg * (1.0 - sg)
        return dt, dy * sg, _colsum(dt)
    dt_glu, dy1a, gs["b_glu"] = _mm(dbr_ssm, wb["w_ssm_br"], [BF16, F32], epi=glu_bwd_epi, mn=[y1, t_glu], n_sums=1,
                                    name="ssm_br_dx")
    gw["w_glu"] = _mm(y1, dt_glu, [F32], ta=True, name="ssm_glu_dw")

    def gelu_bwd_epi(acc, dy1t, y0t, ut):
        th, c0, c1 = _gelu_parts(y0t)
        dg = 0.5 * (1.0 + th) + 0.5 * y0t * (1.0 - th * th) * c0 * (1.0 + 3.0 * c1 * y0t * y0t)
        dy = (acc + dy1t) * dg
        return dy, _colsum(dy * ut.astype(F32))
    dy0, gs["ssm_d"] = _mm(dt_glu, wb["w_glu"], [F32], tb=True, epi=gelu_bwd_epi, mn=[dy1a, y0, u], n_sums=1,
                           name="ssm_glu_dx")
    dy0_s = _scan_order(dy0)
    lam, da, d_b, d_c = _ssm_scan(dy0_s, c_re_t, c_im_t, a_conj, reverse=True, s_fwd=s_all, u=u_s,
                                  name="ssm_scan_bwd")
    du = _time_order(_mm((lam, 0), b_re_t, [BF16], tb=True, pair2=((lam, 1), b_im_t),
                         epi=lambda acc, dyt, dr: (acc + dyt * dr,), mn=[dy0_s], rows=[d_row], bd=SSM_BD, tm=2048, name="ssm_bu_dx"))
    gs["a_re"], gs["a_im"] = da[0], da[1]
    gs["bb_re"], gs["bb_im"] = _bd_diag(d_b[0]).transpose(0, 2, 1), _bd_diag(d_b[1]).transpose(0, 2, 1)
    gs["ssm_c_re"], gs["ssm_c_im"] = _bd_diag(d_c[0]), -_bd_diag(d_c[1])

    gw["w_attn_br"] = _mm(dbr_attn, o_att, [F32], ta=True, name="attn_br_dw")

    def do_epi(acc, ot):
        prod = acc * ot
        head = lax.broadcasted_iota(jnp.int32, prod.shape, 1) // ATT_E
        dd = jnp.zeros_like(prod)
        for h in range(ATT_HG):
            dd = jnp.where(head == h, jnp.sum(jnp.where(head == h, prod, 0.0), axis=1, keepdims=True), dd)
        return acc, dd
    do_att, dd_att = _mm(dbr_attn, wb["w_attn_br"], [BF16, F32], epi=do_epi, mn=[o_att], name="attn_br_dx")
    dq_l, dk_l, dv_l = [], [], []
    for g, d in enumerate(DILATIONS):
        nb = l // d // ATT_WIN
        qp, kp, vp = qkv_p[g]
        dq, dk, dv = _attn_bwd(qp, kp, vp, _to_perm(do_att, d), _to_perm(lse_tot, d), _to_perm(dd_att, d),
                               nb, "attn_bwd%d" % g)
        dq_l.append(_from_perm(dq, d))
        dk_l.append(_from_perm(dk, d))
        dv_l.append(_from_perm(dv, d))

    gw["w_mem_br"] = _mm(dbr_mem, mo, [F32], ta=True, name="mem_br_dw")
    dmo = _mm(dbr_mem, wb["w_mem_br"], [BF16], name="mem_br_dx")
    dmq, dkv = _mem_bwd(mq, kv, dmo, "mem_attn_bwd")
    gw["w_mem_kv"] = _mm(mn, dkv, [F32], ta=True, name="mem_kv_dw")
    dmn = _mm(dkv, wb["w_mem_kv"], [F32], tb=True, name="mem_kv_dx")
    token = sum(grads_ready(gi, gw) for gi in MIXER_GROUPS)
    gs["mem_norm_g"] = _rms_bwd(mem, dmn, None, sp["mem_norm_g"] + token[:1, :1], "rms_mem_bwd")[1]

    dza = jnp.concatenate([du] + dq_l + dk_l + dv_l + [dmq], axis=1)
    dn_a = _mm(dza, w_a, [F32], name="in_proj_a_dx", tk=1664)
    dw_a = _mm(dza, n1, [F32], ta=True, name="in_proj_a_dw", tm=1664)
    dw_g = [_mm(dzg[i], n1, [F32], ta=True, name="in_proj_g_dw%d" % i) for i in range(3)]
    gw["w_in"] = jnp.concatenate([dw_a] + dw_g, axis=0)
    token = grads_ready(0, gw) + small_grads_ready(gs)
    def in_dx_epi(acc, pt, xt, dht, g1):
        dx, dg = _rms_bwd_tile(xt, acc + pt, g1)
        return dx + dht, dg
    w_gs = [w_g[i * D_MODEL:(i + 1) * D_MODEL] for i in range(3)]
    grad_x, gs["norm1_g"] = _mm(dzg[0], w_gs[0], [F32], pair2=(dzg[1], w_gs[1], dzg[2], w_gs[2]), epi=in_dx_epi,
                                mn=[dn_a, x, dh1],
                                rows=[sp["norm1_g"] + token[:1, :1]], n_sums=1, tm=512, name="in_proj_g_dx")
    return loss, grad_x, gs


_SMALL_GRAD_ORDER = ("norm1_g", "mem_norm_g", "b_gate", "a_re", "a_im", "bb_re", "bb_im", "ssm_c_re", "ssm_c_im",
                     "ssm_d", "b_glu", "norm2_g", "final_g")


def kernel(x, mem, norm1_g, mem_norm_g, w_in, b_gate, ssm_lambda_re, ssm_lambda_im, ssm_log_dt, ssm_b_re, ssm_b_im, ssm_c_re, ssm_c_im, ssm_d, w_glu, b_glu, w_ssm_br, w_attn_br, w_mem_kv, w_mem_br, w_o, norm2_g, w_up, w_down, final_g, loss_target, m_norm1_g, m_mem_norm_g, m_w_in, m_b_gate, m_ssm_lambda_re, m_ssm_lambda_im, m_ssm_log_dt, m_ssm_b_re, m_ssm_b_im, m_ssm_c_re, m_ssm_c_im, m_ssm_d, m_w_glu, m_b_glu, m_w_ssm_br, m_w_attn_br, m_w_mem_kv, m_w_mem_br, m_w_o, m_norm2_g, m_w_up, m_w_down, m_final_g, v_norm1_g, v_mem_norm_g, v_w_in, v_b_gate, v_ssm_lambda_re, v_ssm_lambda_im, v_ssm_log_dt, v_ssm_b_re, v_ssm_b_im, v_ssm_c_re, v_ssm_c_im, v_ssm_d, v_w_glu, v_b_glu, v_w_ssm_br, v_w_attn_br, v_w_mem_kv, v_w_mem_br, v_w_o, v_norm2_g, v_w_up, v_w_down, v_final_g):
    args = dict(locals())
    w = {n: args[n] for n in ALL_W}
    m = {n: args["m_" + n] for n in ALL_W}
    v = {n: args["v_" + n] for n in ALL_W}
    my_c = lax.axis_index("c").astype(jnp.int32).reshape(1)
    my_chip = (2 * lax.axis_index("x") + lax.axis_index("y")).astype(jnp.int32).reshape(1)

    w_pack = [_pack_group(w, names) for names in GROUPS]
    my_index = (4 * lax.axis_index("x") + 2 * lax.axis_index("y") + lax.axis_index("c")).astype(jnp.int32)
    zero = jnp.zeros((), jnp.int32)
    w_all = _allgather(w_pack[0].astype(BF16), "allgather_weights0")
    wb = {n: _full_stored(part, n) for n, part in _split_group(w_all, GROUPS[0]).items()}
    gathers = {gi: _split_start(w_pack[gi].astype(BF16), N_DEV, _gather_plan, w_all, "weights_gather_start%d" % gi)
               for gi in range(1, len(GROUPS))}

    def gathered(started, after, name):
        sems, src, land, _ = started
        src, land = _split_wait(sems, src, land, _gather_wait_plan, after, name)
        return lax.dynamic_update_slice(land, src[None], (my_index, zero, zero))

    def late_weights(gi, after):
        full = gathered(gathers[gi], after, "weights_gather_wait%d" % gi)
        return {n: _full_stored(part, n) for n, part in _split_group(full, GROUPS[gi]).items()}

    pending = {}

    def grads_ready(gi, grads):
        g_pack = jnp.concatenate([_stacked_stored(grads[n], n) for n in GROUPS[gi]], axis=1)
        if gi == 0:
            t1 = _pair_exchange(g_pack, "grad_pair_exchange%d" % gi)
            p_sum, p_bf = _pair_sum(g_pack, t1, my_c, "grad_pair_sum%d" % gi, GROUP_TR[gi])
            started = _split_start(p_bf, 3, _chip_plan, p_sum, "grad_chip_exchange_start%d" % gi)
            pending[gi] = (p_sum, my_chip, started, _chip_plan)
        else:
            started = _split_start(g_pack.astype(BF16), N_DEV - 1, _owner_plan, g_pack, "grad_exchange_start%d" % gi)
            pending[gi] = (g_pack, my_index.reshape(1), started, _owner_plan)
        return started[3]

    early_small = [n for n in _SMALL_GRAD_ORDER if n != "norm1_g"]
    small_started = []

    def small_grads_ready(gs):
        started = _split_start(_pack([gs[n] for n in early_small]), N_DEV, _gather_plan, gs["mem_norm_g"],
                               "small_grads_gather_start")
        small_started.append((started, [gs[n].shape for n in early_small]))
        return started[3]

    sp = {
        "norm1_g": norm1_g + sum(started[3][:1, :1] for started in gathers.values()), "mem_norm_g": mem_norm_g, "b_gate": b_gate, "b_glu": b_glu, "norm2_g": norm2_g,
        "final_g": final_g.reshape(1, D_MODEL),
        "ssm_lambda_re": ssm_lambda_re[0], "ssm_lambda_im": ssm_lambda_im[0], "ssm_log_dt": ssm_log_dt[0],
        "ssm_b_re": ssm_b_re[0], "ssm_b_im": ssm_b_im[0], "ssm_c_re": ssm_c_re[0], "ssm_c_im": ssm_c_im[0],
        "ssm_d": ssm_d[0],
    }
    loss, grad_x, gs = _local_step(x[0], mem[0], loss_target[0], wb, sp, late_weights, grads_ready,
                                     small_grads_ready)
    loss = lax.psum(loss[0, 0], ("x", "y", "c"))
    n1_started = _split_start(_pack([gs["norm1_g"]]), N_DEV, _gather_plan, grad_x, "norm1_grad_gather_start")

    big_g = {}
    for gi, names in enumerate(GROUPS):
        own, own_index, (sems, src, land, _), plan = pending[gi]
        recv = _split_wait(sems, src, land, plan, grad_x, "grad_exchange_wait%d" % gi)[1]
        g_pack = _grad_sum(own, own_index, recv, "grad_sum%d" % gi, GROUP_TR[gi])
        for n, part in _split_group(g_pack, names).items():
            big_g[n] = _unstored(part, n)
    rows_of = lambda d, names: [d[n].reshape(d[n].shape[-2:]) for n in names]
    big_out = _adam_many(rows_of(big_g, BIG), rows_of(w, BIG), rows_of(m, BIG), rows_of(v, BIG), 8, "adam_big")
    big = [big_g] + [{n: a[None] for n, a in zip(BIG, outs)} for outs in big_out]

    (sg_started, sg_shapes), = small_started
    sg_all = jnp.concatenate([gathered(sg_started, big_out[0][0], "small_grads_gather_wait"),
                              gathered(n1_started, big_out[0][0], "norm1_grad_gather_wait")], axis=1)
    sg_sum = _sum8(sg_all, "sum_small_grads")
    n1_rows = n1_started[1].shape[0]
    sg = dict(zip(early_small, _unpack(sg_sum[:-n1_rows], sg_shapes)))
    sg["norm1_g"] = _unpack(sg_sum[-n1_rows:], [gs["norm1_g"].shape])[0]
    _, disc_vjp = jax.vjp(_discretize, sp["ssm_lambda_re"], sp["ssm_lambda_im"], sp["ssm_log_dt"],
                          sp["ssm_b_re"], sp["ssm_b_im"])
    d_lre, d_lim, d_ldt, d_bre, d_bim = disc_vjp((sg["a_re"].reshape(SSM_G, SSM_P), sg["a_im"].reshape(SSM_G, SSM_P),
                                                  sg["bb_re"], sg["bb_im"]))
    small_grad = {
        "norm1_g": sg["norm1_g"], "mem_norm_g": sg["mem_norm_g"], "b_gate": sg["b_gate"],
        "ssm_lambda_re": d_lre, "ssm_lambda_im": d_lim, "ssm_log_dt": d_ldt, "ssm_b_re": d_bre, "ssm_b_im": d_bim,
        "ssm_c_re": sg["ssm_c_re"], "ssm_c_im": sg["ssm_c_im"], "ssm_d": sg["ssm_d"], "b_glu": sg["b_glu"],
        "norm2_g": sg["norm2_g"], "final_g": sg["final_g"],
    }
    small_grad = {n: small_grad[n].reshape(w[n].shape) for n in SMALL}

    def squeezed(a):
        return a.reshape(a.shape[1:]) if a.ndim > 2 else a.reshape(1, -1)

    sq = lambda d: [squeezed(d[n]) for n in SMALL]
    small_out = _adam_many(sq(small_grad), sq(w), sq(m), sq(v), 1, "adam_small")
    small = [small_grad] + [{n: a.reshape(w[n].shape) for n, a in zip(SMALL, outs)} for outs in small_out]

    outs = [loss, grad_x[None]]
    for kind in range(4):
        for n in ALL_W:
            outs.append(big[kind][n] if n in BIG else small[kind][n])
    return tuple(outs)
```

```python
import math

import numpy as np
import jax
import jax.numpy as jnp
from jax import lax
from jax.experimental import pallas as pl
from jax.experimental.pallas import tpu as pltpu

F32 = jnp.float32
BF16 = jnp.bfloat16
_MXU = jnp.bfloat16

D_MODEL = 1024
SSM_G, SSM_H, SSM_P = 32, 16, 64
SSM_W = SSM_G * SSM_H
SSM_S = SSM_G * SSM_P
SSM_BD = 4
ATT_E = 64
ATT_HG = 4
ATT_GW = ATT_HG * ATT_E
ATT_WIN = 128
ATT_QB = 8
ATT_QB_FWD = 4
DILATIONS = (1, 4, 16)
MEM_H, MEM_E = 4, 128
MEM_W = MEM_H * MEM_E
ZA_W = SSM_W + 9 * ATT_GW + MEM_W
ZG_W = 3 * D_MODEL
IN_W = ZA_W + ZG_W
RMS_EPS = 1e-6
NEG_INF = -1e30

ADAM_LR, ADAM_B1, ADAM_B2, ADAM_EPS, ADAM_WD, ADAM_STEP = 0.001, 0.9, 0.999, 1e-08, 0.01, 10

N_DEV = 8
PACK_C = 512
_VMEM_LIMIT = 56 * 1024 * 1024
SUBLANES = 16
SCAN_SEG = 128
SCAN_CHAINS = 4
SCAN_UNROLL = 4
SCAN_W = 128

BIG = ("w_in", "w_glu", "w_ssm_br", "w_attn_br", "w_mem_kv", "w_mem_br", "w_o", "w_up", "w_down")
BIG_SHAPE = {
    "w_in": (D_MODEL, IN_W, 1), "w_glu": (SSM_W, SSM_W, 0), "w_ssm_br": (SSM_W, D_MODEL, 1),
    "w_attn_br": (ATT_GW, D_MODEL, 1), "w_mem_kv": (D_MODEL, 2 * MEM_W, 0), "w_mem_br": (MEM_W, D_MODEL, 1),
    "w_o": (D_MODEL, D_MODEL, 0), "w_up": (D_MODEL, 4 * D_MODEL, 1), "w_down": (4 * D_MODEL, D_MODEL, 0),
}
SMALL = ("norm1_g", "mem_norm_g", "b_gate", "ssm_lambda_re", "ssm_lambda_im", "ssm_log_dt", "ssm_b_re",
         "ssm_b_im", "ssm_c_re", "ssm_c_im", "ssm_d", "b_glu", "norm2_g", "final_g")
ALL_W = ("norm1_g", "mem_norm_g", "w_in", "b_gate", "ssm_lambda_re", "ssm_lambda_im", "ssm_log_dt", "ssm_b_re",
         "ssm_b_im", "ssm_c_re", "ssm_c_im", "ssm_d", "w_glu", "b_glu", "w_ssm_br", "w_attn_br", "w_mem_kv",
         "w_mem_br", "w_o", "norm2_g", "w_up", "w_down", "final_g")


def _params(sem):
    return pltpu.CompilerParams(dimension_semantics=sem, vmem_limit_bytes=_VMEM_LIMIT)


def _pick(n, cap):
    if n <= cap:
        return n
    t = (cap // 128) * 128
    while n % t:
        t -= 128
    return t


def _mm(a, b, outs, *, name, ta=False, tb=False, epi=None, mn=(), rows=(), pair2=None, bd=0, n_sums=0,
        tm=1024, tn=1024, tk=2048):
    ab = [a, b] + (list(pair2) if pair2 is not None else [])
    planes = [op[1] if isinstance(op, tuple) else None for op in ab]
    ab = [op[0] if isinstance(op, tuple) else op for op in ab]
    a_shape, b_shape = ab[0].shape[-2:], ab[1].shape[-2:]
    m = a_shape[1] if ta else a_shape[0]
    k = a_shape[0] if ta else a_shape[1]
    n = b_shape[0] if tb else b_shape[1]
    assert k == (b_shape[1] if tb else b_shape[0]), (name, a_shape, b_shape)
    out_n = n
    if bd and ta:
        assert not tb
        tm, tn, tk = m // bd, n // bd, _pick(k, tk)
        grid, out_n = (bd, 1, k // tk), tn
        a_blk = ((tk, tm), lambda i, j, kk: (kk, i))
        b_blk = ((tk, tn), lambda i, j, kk: (kk, i))
        mn_spec = pl.BlockSpec((tm, tn), lambda i, j, kk: (i, 0))
    elif bd:
        tm, tn, tk = _pick(m, tm), n // bd, k // bd
        grid = (m // tm, bd, 1)
        a_blk = ((tm, tk), lambda i, j, kk: (i, j))
        b_blk = ((tn, tk) if tb else (tk, tn), lambda i, j, kk: (j, j))
        mn_spec = pl.BlockSpec((tm, tn), lambda i, j, kk: (i, j))
    else:
        tm, tn, tk = _pick(m, tm), _pick(n, tn), _pick(k, tk)
        grid = (m // tm, n // tn, k // tk)
        a_blk = ((tk, tm), lambda i, j, kk: (kk, i)) if ta else ((tm, tk), lambda i, j, kk: (i, kk))
        b_blk = ((tn, tk), lambda i, j, kk: (j, kk)) if tb else ((tk, tn), lambda i, j, kk: (kk, j))
        mn_spec = pl.BlockSpec((tm, tn), lambda i, j, kk: (i, j))

    def operand_spec(blk, plane):
        shape, imap = blk
        if plane is None:
            return pl.BlockSpec(shape, imap)
        return pl.BlockSpec((None,) + shape, lambda i, j, kk: (plane,) + imap(i, j, kk))

    ab_specs = [operand_spec(a_blk if q % 2 == 0 else b_blk, p) for q, p in enumerate(planes)]
    mn_arrays = [e[0] if isinstance(e, tuple) else e for e in mn]
    mn_specs = [pl.BlockSpec((tm, tn), lambda i, j, kk, c=e[1]: (i, c)) if isinstance(e, tuple) else mn_spec
                for e in mn]
    nk = grid[2]
    row_spec = pl.BlockSpec((1, tn), lambda i, j, kk: (0, j))
    n_ex, n_out = len(mn) + len(rows), len(outs)
    assert n_sums == 0 or (grid[1] == 1 and not bd)
    dims = (((0 if ta else 1,), (1 if tb else 0,)), ((), ()))

    def body(*refs):
        ab_refs, rest = refs[:len(ab)], refs[len(ab):]
        ex, o_refs = rest[:n_ex], rest[n_ex:n_ex + n_out]
        s_refs = rest[n_ex + n_out:n_ex + n_out + n_sums]
        first_row_tile = pl.program_id(0) == 0
        kk = pl.program_id(2)

        part = None
        for a_ref, b_ref in zip(ab_refs[0::2], ab_refs[1::2]):
            d = lax.dot_general(a_ref[...].astype(_MXU), b_ref[...].astype(_MXU), dims, preferred_element_type=F32)
            part = d if part is None else part + d

        def finish(total):
            vals = (total,) if epi is None else epi(total, *[r[...] for r in ex])
            for r, v in zip(o_refs, vals):
                r[...] = v.astype(r.dtype)
            for r, v in zip(s_refs, vals[n_out:]):
                r[...] = jnp.where(first_row_tile, v, r[...] + v)

        if nk == 1:
            finish(part)
        else:
            acc = rest[-1]

            @pl.when(kk == 0)
            def _():
                acc[...] = part

            @pl.when(jnp.logical_and(kk > 0, kk < nk - 1))
            def _():
                acc[...] += part

            @pl.when(kk == nk - 1)
            def _():
                finish(acc[...] + part)

    res = pl.pallas_call(
        body, grid=grid,
        in_specs=ab_specs + mn_specs + [row_spec] * len(rows),
        out_specs=[mn_spec] * n_out + [row_spec] * n_sums,
        out_shape=[jax.ShapeDtypeStruct((m, out_n), dt) for dt in outs]
        + [jax.ShapeDtypeStruct((1, out_n), F32)] * n_sums,
        scratch_shapes=[pltpu.VMEM((tm, tn), F32)] if nk > 1 else [],
        compiler_params=_params(("arbitrary" if n_sums else "parallel", "parallel", "arbitrary")), name=name,
    )(*ab, *mn_arrays, *rows)
    return res[0] if n_out + n_sums == 1 else res


def _ew(fn, rows, bcs, out_rows, out_accs, *, name, tm=256):
    r = rows[0].shape[0]
    tm = min(tm, r)
    assert r % tm == 0
    nr, nb, no, na = len(rows), len(bcs), len(out_rows), len(out_accs)

    def body(*refs):
        i = pl.program_id(0)
        r_in, b_in = refs[:nr], refs[nr:nr + nb]
        o_r, o_a = refs[nr + nb:nr + nb + no], refs[nr + nb + no:]
        outs, accs = fn([x[...] for x in r_in], [x[...] for x in b_in])
        for ref, v in zip(o_r, outs):
            ref[...] = v.astype(ref.dtype)
        if na:
            @pl.when(i == 0)
            def _():
                for ref in o_a:
                    ref[...] = jnp.zeros_like(ref)

            for ref, v in zip(o_a, accs):
                ref[...] += v

    res = pl.pallas_call(
        body, grid=(r // tm,),
        in_specs=[pl.BlockSpec((tm, x.shape[1]), lambda i: (i, 0)) for x in rows]
        + [pl.BlockSpec((1, x.shape[1]), lambda i: (0, 0)) for x in bcs],
        out_specs=[pl.BlockSpec((tm, c), lambda i: (i, 0)) for c, _ in out_rows]
        + [pl.BlockSpec((1, c), lambda i: (0, 0)) for c in out_accs],
        out_shape=[jax.ShapeDtypeStruct((r, c), dt) for c, dt in out_rows]
        + [jax.ShapeDtypeStruct((1, c), F32) for c in out_accs],
        compiler_params=_params(("arbitrary",)), name=name,
    )(*rows, *bcs)
    return res


def _colsum(x):
    return jnp.sum(x, axis=0, keepdims=True)


def _sigmoid(x):
    return 1.0 / (1.0 + jnp.exp(-x))


def _rms_bwd_tile(xv, dv, g):
    rs = lax.rsqrt(jnp.mean(xv * xv, axis=-1, keepdims=True) + RMS_EPS)
    gd = dv * g
    dx = rs * gd - xv * (rs * rs * rs) * jnp.mean(gd * xv, axis=-1, keepdims=True)
    return dx, _colsum(dv * xv * rs)


def _rms_fwd(x, g, name):
    def fn(r, b):
        xv = r[0]
        rs = lax.rsqrt(jnp.mean(xv * xv, axis=-1, keepdims=True) + RMS_EPS)
        return [xv * rs * b[0]], []
    return _ew(fn, [x], [g], [(x.shape[1], BF16)], [], name=name, tm=1024)[0]


def _rms_bwd(x, dn, res, g, name):
    def fn(r, b):
        dx, dg = _rms_bwd_tile(r[0], r[1], b[0])
        if res is not None:
            dx = dx + r[2]
        return [dx], [dg]
    rows = [x, dn] + ([res] if res is not None else [])
    return _ew(fn, rows, [g], [(x.shape[1], F32)], [x.shape[1]], name=name)


def _scan_order(x):
    l, c = x.shape
    return x.reshape(l // (SUBLANES * SCAN_SEG), SUBLANES, SCAN_SEG, c).transpose(0, 2, 1, 3).reshape(l, c)


def _time_order(x):
    l, c = x.shape
    return x.reshape(l // (SUBLANES * SCAN_SEG), SCAN_SEG, SUBLANES, c).transpose(0, 2, 1, 3).reshape(l, c)


def _ssm_scan(x, w_re, w_im, a_pair, *, reverse, s_fwd=None, u=None, name):
    l = x.shape[0]
    seg, w = SCAN_SEG, SCAN_W
    bd_w = SSM_W // SSM_BD
    tiles_per_bd = SSM_S // SSM_BD // w
    nch = min(SCAN_CHAINS, l // (SUBLANES * seg))
    chain_rows = SUBLANES * seg
    tb = nch * chain_rows
    nt = l // tb
    with_da = s_fwd is not None
    assert reverse or not with_da

    def tt(t):
        return nt - 1 - t if reverse else t

    def body(*refs):
        if with_da:
            (x_ref, wr_ref, wi_ref, a_ref, sf_ref, sp_ref, u_ref, s_ref, da_ref, dw_ref, dx_ref,
             p_ref, c_ref, b_scr, l_scr) = refs
        else:
            x_ref, wr_ref, wi_ref, a_ref, s_ref, p_ref, c_ref, b_scr, l_scr = refs
        t_blk = pl.program_id(1)
        ar, ai = a_ref[0], a_ref[1]

        @pl.when(t_blk == 0)
        def _():
            def pstep(i, carry):
                pr, pi = carry
                p_ref[0, pl.ds(i, 1), :] = pr
                p_ref[1, pl.ds(i, 1), :] = pi
                return pr * ar - pi * ai, pr * ai + pi * ar

            lax.fori_loop(0, seg, pstep, (ar, ai))
            c_ref[...] = jnp.zeros_like(c_ref)
            if with_da:
                da_ref[...] = jnp.zeros_like(da_ref)
                dw_ref[...] = jnp.zeros_like(dw_ref)
                dx_ref[...] = jnp.zeros_like(dx_ref)

        xb = x_ref[...].astype(_MXU)
        b_scr[:, :w] = jnp.dot(xb, wr_ref[...], preferred_element_type=F32)
        b_scr[:, w:] = jnp.dot(xb, wi_ref[...], preferred_element_type=F32)
        arb, aib = jnp.broadcast_to(ar, (SUBLANES, w)), jnp.broadcast_to(ai, (SUBLANES, w))
        zero = jnp.zeros((SUBLANES, w), F32)

        def tile(g, step):
            return pl.ds(pl.multiple_of(g * chain_rows + step * SUBLANES, SUBLANES), SUBLANES)

        def rows(g, i):
            return tile(g, seg - 1 - i if reverse else i)

        def local_step(i, carry):
            out = []
            for g in range(nch):
                sr, si = carry[2 * g], carry[2 * g + 1]
                idx = rows(g, i)
                sr, si = arb * sr - aib * si + b_scr[idx, :w], arb * si + aib * sr + b_scr[idx, w:]
                l_scr[idx, :w] = sr
                l_scr[idx, w:] = si
                out += [sr, si]
            return tuple(out)

        def unrolled(step_fn, first):
            def trip(q, carry):
                for r in range(SCAN_UNROLL):
                    carry = step_fn(first + q * SCAN_UNROLL + r, carry)
                return carry
            return trip

        ends = lax.fori_loop(0, seg // SCAN_UNROLL, unrolled(local_step, 0), (zero,) * (2 * nch))

        a_seg_r, a_seg_i = p_ref[0, seg - 1:seg, :], p_ref[1, seg - 1:seg, :]
        cr, ci = c_ref[0], c_ref[1]
        sub = lax.broadcasted_iota(jnp.int32, (SUBLANES, w), 0)
        ins = [[zero, zero] for _ in range(nch)]
        order = [(g, k) for g in range(nch) for k in range(SUBLANES)]
        for g, k in (order[::-1] if reverse else order):
            ins[g] = [jnp.where(sub == k, cr, ins[g][0]), jnp.where(sub == k, ci, ins[g][1])]
            er, ei = ends[2 * g][k:k + 1], ends[2 * g + 1][k:k + 1]
            cr, ci = er + a_seg_r * cr - a_seg_i * ci, ei + a_seg_r * ci + a_seg_i * cr
        c_ref[0] = cr
        c_ref[1] = ci

        def fix(g, i):
            idx = rows(g, i)
            pr, pi = p_ref[0, pl.ds(i, 1), :], p_ref[1, pl.ds(i, 1), :]
            sr = l_scr[idx, :w] + pr * ins[g][0] - pi * ins[g][1]
            si = l_scr[idx, w:] + pr * ins[g][1] + pi * ins[g][0]
            s_ref.at[0][idx, :] = sr.astype(s_ref.dtype)
            s_ref.at[1][idx, :] = si.astype(s_ref.dtype)
            return sr, si

        if not with_da:
            def fix_step(i, carry):
                for g in range(nch):
                    fix(g, i)
                return carry

            lax.fori_loop(0, seg // SCAN_UNROLL, unrolled(fix_step, 0), 0)
        else:
            def adj_step(i, acc):
                acc_r, acc_i = acc
                for g in range(nch):
                    lr, li = fix(g, i)
                    prev = tile(g, seg - 2 - i)
                    fr, fi = sf_ref.at[0][prev, :].astype(F32), sf_ref.at[1][prev, :].astype(F32)
                    acc_r, acc_i = acc_r + lr * fr + li * fi, acc_i + li * fr - lr * fi
                return acc_r, acc_i

            acc = lax.fori_loop(0, seg // SCAN_UNROLL - 1, unrolled(adj_step, 0), (zero, zero))
            for i in range(seg - SCAN_UNROLL, seg - 1):
                acc = adj_step(i, acc)
            acc_r, acc_i = acc
            first_block = tt(t_blk) == 0
            for g in range(nch):
                lr, li = fix(g, seg - 1)
                seg_ends = tile(g, seg - 1)
                if g == 0:
                    pvr = jnp.where(first_block, 0.0, sp_ref[0, SUBLANES - 1:SUBLANES, :].astype(F32))
                    pvi = jnp.where(first_block, 0.0, sp_ref[1, SUBLANES - 1:SUBLANES, :].astype(F32))
                else:
                    pvr = sf_ref[0, g * chain_rows - 1:g * chain_rows, :].astype(F32)
                    pvi = sf_ref[1, g * chain_rows - 1:g * chain_rows, :].astype(F32)
                fr = jnp.where(sub == 0, pvr, pltpu.roll(sf_ref.at[0][seg_ends, :].astype(F32), 1, 0))
                fi = jnp.where(sub == 0, pvi, pltpu.roll(sf_ref.at[1][seg_ends, :].astype(F32), 1, 0))
                acc_r = acc_r + lr * fr + li * fi
                acc_i = acc_i + li * fr - lr * fi
            da_ref[0] += jnp.sum(acc_r, axis=0, keepdims=True)
            da_ref[1] += jnp.sum(acc_i, axis=0, keepdims=True)
            for plane in range(2):
                dw_ref[plane] += _tn_dot(u_ref[...], s_ref[plane])
                dx_ref[plane] += _tn_dot(xb, sf_ref[plane])

    x_spec = pl.BlockSpec((tb, bd_w), lambda j, t: (tt(t), j // tiles_per_bd))
    w_spec = pl.BlockSpec((bd_w, w), lambda j, t: (j // tiles_per_bd, j))
    d_spec = pl.BlockSpec((2, bd_w, w), lambda j, t: (0, j // tiles_per_bd, j % tiles_per_bd))
    a_spec = pl.BlockSpec((2, 1, w), lambda j, t: (0, 0, j))
    s_spec = pl.BlockSpec((2, tb, w), lambda j, t: (0, tt(t), j))
    in_specs, args = [x_spec, w_spec, w_spec, a_spec], [x, w_re, w_im, a_pair]
    out_specs, out_shape = [s_spec], [jax.ShapeDtypeStruct((2, l, SSM_S), BF16)]
    scratch = [pltpu.VMEM((2, seg, w), F32), pltpu.VMEM((2, 1, w), F32)] + [pltpu.VMEM((tb, 2 * w), F32)] * 2
    if with_da:
        in_specs += [s_spec, pl.BlockSpec((2, SUBLANES, w),
                                          lambda j, t: (0, jnp.maximum(tt(t) * (tb // SUBLANES) - 1, 0), j)),
                     x_spec]
        args += [s_fwd, s_fwd, u]
        out_specs += [a_spec, d_spec, d_spec]
        out_shape += ([jax.ShapeDtypeStruct((2, 1, SSM_S), F32)]
                      + [jax.ShapeDtypeStruct((2, SSM_W, SSM_S // SSM_BD), F32)] * 2)
    res = pl.pallas_call(
        body, grid=(SSM_S // w, nt), in_specs=in_specs, out_specs=out_specs, out_shape=out_shape,
        scratch_shapes=scratch, compiler_params=_params(("parallel", "arbitrary")), name=name,
    )(*args)
    return res if with_da else res[0]


def _nt_dot(x, y):
    return lax.dot_general(x.astype(_MXU), y.astype(_MXU), (((1,), (1,)), ((), ())), preferred_element_type=F32)


def _tn_dot(x, y):
    return lax.dot_general(x.astype(_MXU), y.astype(_MXU), (((0,), (0,)), ((), ())), preferred_element_type=F32)


def _nn_dot(x, y):
    return jnp.dot(x.astype(_MXU), y.astype(_MXU), preferred_element_type=F32)


def _attn_mask2(gb, nb):
    qi = lax.broadcasted_iota(jnp.int32, (ATT_WIN, 2 * ATT_WIN), 0)
    c = lax.broadcasted_iota(jnp.int32, (ATT_WIN, 2 * ATT_WIN), 1)
    has_prev = (gb % nb) != 0
    prev_ok = jnp.logical_and(jnp.logical_and(c < ATT_WIN, c >= qi), has_prev)
    own_ok = jnp.logical_and(c >= ATT_WIN, c - ATT_WIN <= qi)
    return jnp.logical_or(prev_ok, own_ok)


def _attn_specs(qb):
    cur = pl.BlockSpec((qb * ATT_WIN, ATT_GW), lambda i: (i, 0))
    prev = pl.BlockSpec((ATT_WIN, ATT_GW), lambda i: (jnp.maximum(qb * i - 1, 0), 0))
    return cur, prev


def _attn_fwd(q, k, v, nb, name):
    l = q.shape[0]
    scale = ATT_E ** -0.5
    w = ATT_WIN

    qb = ATT_QB_FWD

    def body(q_ref, kc_ref, kp_ref, vc_ref, vp_ref, o_ref, lse_ref):
        i = pl.program_id(0)
        masks = [_attn_mask2(qb * i + b, nb) for b in range(qb)]
        for h in range(ATT_HG):
            sl = slice(h * ATT_E, (h + 1) * ATT_E)
            k_ext = jnp.concatenate([kp_ref[:, sl], kc_ref[:, sl]], axis=0)
            v_ext = jnp.concatenate([vp_ref[:, sl], vc_ref[:, sl]], axis=0)
            for b in range(qb):
                r, kr = slice(b * w, (b + 1) * w), slice(b * w, (b + 2) * w)
                s = jnp.where(masks[b], _nt_dot(q_ref[r, sl], k_ext[kr]) * scale, NEG_INF)
                mx = jnp.max(s, axis=-1, keepdims=True)
                p = jnp.exp(s - mx)
                den = jnp.sum(p, axis=-1, keepdims=True)
                o_ref[r, sl] = _nn_dot(p, v_ext[kr]) / den
                lse_ref[r, sl] = jnp.broadcast_to(mx + jnp.log(den), (w, ATT_E))

    cur, prev = _attn_specs(qb)
    return pl.pallas_call(
        body, grid=(l // (qb * w),), in_specs=[cur, cur, prev, cur, prev], out_specs=[cur, cur],
        out_shape=[jax.ShapeDtypeStruct((l, ATT_GW), F32)] * 2,
        compiler_params=_params(("parallel",)), name=name,
    )(q, k, k, v, v)


def _attn_bwd(q, k, v, do, lse, dd, nb, name):
    l = q.shape[0]
    scale = ATT_E ** -0.5
    w = ATT_WIN
    nblk = l // w

    def body(q_ref, kc_ref, kp_ref, vc_ref, vp_ref, do_ref, lse_ref, dd_ref, qn_ref, don_ref, lsen_ref, ddn_ref,
             dq_ref, dk_ref, dv_ref, dk_acc, dv_acc):
        i = pl.program_id(0)
        masks = [_attn_mask2(ATT_QB * i + b, nb) for b in range(ATT_QB)]
        nxt = ATT_QB * (i + 1)
        nxt_attends = jnp.logical_and(nxt < nblk, (nxt % nb) != 0)
        qi = lax.broadcasted_iota(jnp.int32, (w, w), 0)
        kj = lax.broadcasted_iota(jnp.int32, (w, w), 1)
        mask_n = jnp.logical_and(kj >= qi, nxt_attends)
        dk_acc[...] = jnp.zeros_like(dk_acc)
        dv_acc[...] = jnp.zeros_like(dv_acc)
        for h in range(ATT_HG):
            sl, col = slice(h * ATT_E, (h + 1) * ATT_E), slice(h * ATT_E, h * ATT_E + 1)
            k_ext = jnp.concatenate([kp_ref[:, sl], kc_ref[:, sl]], axis=0)
            v_ext = jnp.concatenate([vp_ref[:, sl], vc_ref[:, sl]], axis=0)
            for b in range(ATT_QB):
                r, kr = slice(b * w, (b + 1) * w), slice(b * w, (b + 2) * w)
                qh, doh, k2, v2 = q_ref[r, sl], do_ref[r, sl], k_ext[kr], v_ext[kr]
                p = jnp.where(masks[b], jnp.exp(_nt_dot(qh, k2) * scale - lse_ref[r, col]), 0.0)
                ds = p * (_nt_dot(doh, v2) - dd_ref[r, col]) * scale
                dq_ref[r, sl] = _nn_dot(ds, k2).astype(dq_ref.dtype)
                dk2, dv2 = _tn_dot(ds, qh), _tn_dot(p, doh)
                dk_acc[r, sl] += dk2[w:]
                dv_acc[r, sl] += dv2[w:]
                if b > 0:
                    rp = slice((b - 1) * w, b * w)
                    dk_acc[rp, sl] += dk2[:w]
                    dv_acc[rp, sl] += dv2[:w]
            last = slice((ATT_QB - 1) * w, ATT_QB * w)
            kl, vl, qn, don = kc_ref[last, sl], vc_ref[last, sl], qn_ref[:, sl], don_ref[:, sl]
            pn = jnp.where(mask_n, jnp.exp(_nt_dot(qn, kl) * scale - lsen_ref[:, col]), 0.0)
            dsn = pn * (_nt_dot(don, vl) - ddn_ref[:, col]) * scale
            dk_acc[last, sl] += _tn_dot(dsn, qn)
            dv_acc[last, sl] += _tn_dot(pn, don)
        dk_ref[...] = dk_acc[...].astype(dk_ref.dtype)
        dv_ref[...] = dv_acc[...].astype(dv_ref.dtype)

    cur, prev = _attn_specs(ATT_QB)
    nxt_spec = pl.BlockSpec((w, ATT_GW), lambda i: (jnp.minimum(ATT_QB * (i + 1), nblk - 1), 0))
    return pl.pallas_call(
        body, grid=(l // (ATT_QB * w),),
        in_specs=[cur, cur, prev, cur, prev, cur, cur, cur, nxt_spec, nxt_spec, nxt_spec, nxt_spec],
        out_specs=[cur] * 3, out_shape=[jax.ShapeDtypeStruct((l, ATT_GW), BF16)] * 3,
        scratch_shapes=[pltpu.VMEM((ATT_QB * w, ATT_GW), F32)] * 2,
        compiler_params=_params(("parallel",)), name=name,
    )(q, k, k, v, v, do, lse, dd, q, do, lse, dd)


def _to_perm(a, d):
    if d == 1:
        return a
    l, c = a.shape
    return a.reshape(l // d, d, c).transpose(1, 0, 2).reshape(l, c)


def _from_perm(a, d):
    if d == 1:
        return a
    l, c = a.shape
    return a.reshape(d, l // d, c).transpose(1, 0, 2).reshape(l, c)


def _mem_probs(qh, kh):
    s = _nt_dot(qh, kh) * (MEM_E ** -0.5)
    e = jnp.exp(s - jnp.max(s, axis=-1, keepdims=True))
    return e / jnp.sum(e, axis=-1, keepdims=True)


def _mem_fwd(mq, kv, name, tm=512):
    l, nm = mq.shape[0], kv.shape[0]

    def body(q_ref, kv_ref, o_ref):
        for h in range(MEM_H):
            sl = slice(h * MEM_E, (h + 1) * MEM_E)
            p = _mem_probs(q_ref[:, sl], kv_ref[:, sl])
            o_ref[:, sl] = _nn_dot(p, kv_ref[:, MEM_W + h * MEM_E:MEM_W + (h + 1) * MEM_E]).astype(o_ref.dtype)

    return pl.pallas_call(
        body, grid=(l // tm,),
        in_specs=[pl.BlockSpec((tm, MEM_W), lambda i: (i, 0)), pl.BlockSpec((nm, 2 * MEM_W), lambda i: (0, 0))],
        out_specs=pl.BlockSpec((tm, MEM_W), lambda i: (i, 0)),
        out_shape=jax.ShapeDtypeStruct((l, MEM_W), BF16),
        compiler_params=_params(("parallel",)), name=name,
    )(mq, kv)


def _mem_bwd(mq, kv, dmo, name, tm=512):
    l, nm = mq.shape[0], kv.shape[0]
    scale = MEM_E ** -0.5

    def body(q_ref, kv_ref, do_ref, dq_ref, dkv_ref):
        @pl.when(pl.program_id(0) == 0)
        def _():
            dkv_ref[...] = jnp.zeros_like(dkv_ref)

        for h in range(MEM_H):
            sl = slice(h * MEM_E, (h + 1) * MEM_E)
            vsl = slice(MEM_W + h * MEM_E, MEM_W + (h + 1) * MEM_E)
            qh, kh, vh, doh = q_ref[:, sl], kv_ref[:, sl], kv_ref[:, vsl], do_ref[:, sl]
            p = _mem_probs(qh, kh)
            dp = _nt_dot(doh, vh)
            ds = p * (dp - jnp.sum(dp * p, axis=-1, keepdims=True)) * scale
            dq_ref[:, sl] = _nn_dot(ds, kh).astype(dq_ref.dtype)
            dkv_ref[:, sl] += _tn_dot(ds, qh)
            dkv_ref[:, vsl] += _tn_dot(p, doh)

    row = pl.BlockSpec((tm, MEM_W), lambda i: (i, 0))
    full = pl.BlockSpec((nm, 2 * MEM_W), lambda i: (0, 0))
    return pl.pallas_call(
        body, grid=(l // tm,), in_specs=[row, full, row], out_specs=[row, full],
        out_shape=[jax.ShapeDtypeStruct((l, MEM_W), BF16), jax.ShapeDtypeStruct((nm, 2 * MEM_W), F32)],
        compiler_params=_params(("arbitrary",)), name=name,
    )(mq, kv, dmo)


def _gated_out_proj(zg, branches, b_gate, w_o, x, g2, name, tm=512):
    l, d = x.shape
    nbr = len(branches)

    def body(zg_ref, *rest):
        br_refs, (bg_ref, w_ref, x_ref, g2_ref, m_ref, h_ref, n_ref) = rest[:nbr], rest[nbr:]
        merged = jnp.zeros((tm, d), F32)
        for i, br_ref in enumerate(br_refs):
            cols = slice(i * d, (i + 1) * d)
            merged += _sigmoid(zg_ref[:, cols].astype(F32) + bg_ref[:, cols]) * br_ref[...].astype(F32)
        mb = merged.astype(BF16)
        m_ref[...] = mb
        hv = jnp.dot(mb.astype(_MXU), w_ref[...].astype(_MXU), preferred_element_type=F32) + x_ref[...]
        h_ref[...] = hv
        rs = lax.rsqrt(jnp.mean(hv * hv, axis=-1, keepdims=True) + RMS_EPS)
        n_ref[...] = (hv * rs * g2_ref[...]).astype(n_ref.dtype)

    row = lambda c: pl.BlockSpec((tm, c), lambda i: (i, 0))
    full = lambda a: pl.BlockSpec(a.shape, lambda i: (0, 0))
    return pl.pallas_call(
        body, grid=(l // tm,),
        in_specs=[row(nbr * d)] + [row(d)] * nbr + [full(b_gate), full(w_o), row(d), full(g2)],
        out_specs=[row(d)] * 3,
        out_shape=[jax.ShapeDtypeStruct((l, d), BF16), jax.ShapeDtypeStruct((l, d), F32),
                   jax.ShapeDtypeStruct((l, d), BF16)],
        compiler_params=_params(("parallel",)), name=name,
    )(zg, *branches, b_gate, w_o, x, g2)


def _discretize(lam_re, lam_im, log_dt, b_re, b_im):
    dt = jnp.exp(log_dt)[:, None]
    mag = jnp.exp(lam_re * dt)
    a_re, a_im = mag * jnp.cos(lam_im * dt), mag * jnp.sin(lam_im * dt)
    nr, ni = a_re - 1.0, a_im
    den = lam_re * lam_re + lam_im * lam_im
    coef_re = (nr * lam_re + ni * lam_im) / den
    coef_im = (ni * lam_re - nr * lam_im) / den
    bb_re = coef_re[..., None] * b_re - coef_im[..., None] * b_im
    bb_im = coef_re[..., None] * b_im + coef_im[..., None] * b_re
    return a_re, a_im, bb_re, bb_im


def _bd_in(bb):
    return jnp.einsum("gph,gk->ghkp", bb, jnp.eye(SSM_G, dtype=bb.dtype)).reshape(SSM_W, SSM_S)


def _bd_diag(x):
    gb = SSM_G // SSM_BD
    t = x.reshape(SSM_BD, gb, SSM_H, gb, SSM_P)
    return jnp.einsum("bghgp->bghp", t).reshape(SSM_G, SSM_H, SSM_P)


_ANY = pl.BlockSpec(memory_space=pl.ANY)
_MESH = pl.DeviceIdType.MESH


def _allgather(x, name):
    def body(x_ref, out_ref, send_sems, recv_sems, local_sem):
        mx, my, mc = lax.axis_index("x"), lax.axis_index("y"), lax.axis_index("c")
        me, sibling = (mx, my, mc), (mx, my, 1 - mc)
        chips = [(1 - mx, my), (mx, 1 - my), (1 - mx, 1 - my)]

        def blk(px, py, pc):
            return out_ref.at[4 * px + 2 * py + pc]

        def copy(k, block, to, src=None):
            return pltpu.make_async_remote_copy(
                src_ref=blk(*block) if src is None else src, dst_ref=blk(*block),
                send_sem=send_sems.at[k], recv_sem=recv_sems.at[k], device_id=to, device_id_type=_MESH)

        mine = pltpu.make_async_copy(x_ref, blk(*me), local_sem)
        mine.start()
        first = [copy(0, me, sibling, src=x_ref)]
        first += [copy(1 + j, me, (*chip, mc), src=x_ref) for j, chip in enumerate(chips)]
        for cp in first:
            cp.start()
        passed = [copy(4 + j, (*chip, mc), sibling) for j, chip in enumerate(chips)]
        for j, chip in enumerate(chips):
            copy(1 + j, (*chip, mc), me).wait_recv()
            passed[j].start()
        copy(0, sibling, me).wait_recv()
        for j, chip in enumerate(chips):
            copy(4 + j, (*chip, 1 - mc), me).wait_recv()
        for cp in first + passed:
            cp.wait_send()
        mine.wait()

    return pl.pallas_call(
        body, out_shape=jax.ShapeDtypeStruct((N_DEV,) + x.shape, x.dtype), in_specs=[_ANY], out_specs=_ANY,
        scratch_shapes=[pltpu.SemaphoreType.DMA((7,)), pltpu.SemaphoreType.DMA((7,)), pltpu.SemaphoreType.DMA],
        name=name,
    )(x)


def _pair_exchange(g, name):
    def body(g_ref, out_ref, send_sems, recv_sems):
        mx, my, mc = lax.axis_index("x"), lax.axis_index("y"), lax.axis_index("c")
        copies = [pltpu.make_async_remote_copy(
            src_ref=g_ref.at[2 * k + (1 - mc)], dst_ref=out_ref.at[k], send_sem=send_sems.at[k],
            recv_sem=recv_sems.at[k], device_id=(mx, my, 1 - mc), device_id_type=_MESH) for k in range(4)]
        for cp in copies:
            cp.start()
        for cp in copies:
            cp.wait()

    return pl.pallas_call(
        body, out_shape=jax.ShapeDtypeStruct((4,) + g.shape[1:], g.dtype), in_specs=[_ANY], out_specs=_ANY,
        scratch_shapes=[pltpu.SemaphoreType.DMA((4,)), pltpu.SemaphoreType.DMA((4,))], name=name,
    )(g)


_HBM = pl.BlockSpec(memory_space=pltpu.HBM)
_SEM = pl.BlockSpec(memory_space=pltpu.SEMAPHORE)
_EFFECT = pltpu.SideEffectType.DATAFLOW_SIDE_EFFECTING
_TOKEN = jax.ShapeDtypeStruct((8, 128), F32)


def _peer(rel):
    pos = (lax.axis_index("x"), lax.axis_index("y"), lax.axis_index("c"))
    return tuple(1 - p if (rel >> (2 - i)) & 1 else p for i, p in enumerate(pos))


def _index_of(dev):
    return 4 * dev[0] + 2 * dev[1] + dev[2]


def _split_copies(src_ref, land_ref, sems, plan):
    n = len(plan)
    return [pltpu.make_async_remote_copy(
        src_ref=src_ref if s is None else src_ref.at[s], dst_ref=land_ref.at[d], send_sem=sems[k],
        recv_sem=sems[n + k], device_id=peer, device_id_type=_MESH) for k, (s, d, peer) in enumerate(plan)]


def _split_start(src, n_land, plan_fn, after, name):
    blk = src.shape[-2:]
    land = lax.empty((n_land,) + blk, src.dtype)
    n = len(plan_fn())

    def body(src_ref, land_ref, after_ref, *outs):
        for cp in _split_copies(src_ref, land_ref, outs[:2 * n], plan_fn()):
            cp.start()
        outs[2 * n + 2][...] = jnp.zeros_like(outs[2 * n + 2])

    res = pl.pallas_call(
        body, name=name,
        out_shape=(pltpu.SemaphoreType.DMA(()),) * (2 * n)
        + (pltpu.HBM(src.shape, src.dtype), pltpu.HBM(land.shape, land.dtype), _TOKEN),
        in_specs=(_HBM, _HBM, _ANY),
        out_specs=(_SEM,) * (2 * n) + (_HBM, _HBM, pl.BlockSpec(memory_space=pltpu.VMEM)),
        input_output_aliases={0: 2 * n, 1: 2 * n + 1},
        compiler_params=pltpu.CompilerParams(has_side_effects=_EFFECT),
    )(pltpu.with_memory_space_constraint(src, pltpu.HBM), pltpu.with_memory_space_constraint(land, pltpu.HBM), after)
    return res[:2 * n], res[2 * n], res[2 * n + 1], res[2 * n + 2]


def _split_wait(sems, src, land, plan_fn, after, name):
    n = len(sems) // 2

    def body(src_ref, land_ref, *rest):
        for cp in _split_copies(src_ref, land_ref, rest[:2 * n], plan_fn()):
            cp.wait_send()
            cp.wait_recv()

    return pl.pallas_call(
        body, name=name,
        out_shape=(pltpu.HBM(src.shape, src.dtype), pltpu.HBM(land.shape, land.dtype)),
        in_specs=(_HBM, _HBM) + (_SEM,) * (2 * n) + (_ANY,), out_specs=(_HBM, _HBM),
        input_output_aliases={0: 0, 1: 1},
        compiler_params=pltpu.CompilerParams(has_side_effects=_EFFECT),
    )(src, land, *sems, after)


def _gather_plan():
    me = _index_of(_peer(0))
    return [(None, me, _peer(rel)) for rel in range(1, N_DEV)]


def _gather_wait_plan():
    return [(None, _index_of(_peer(rel)), _peer(rel)) for rel in range(1, N_DEV)]


def _chip_plan():
    return [(_index_of(_peer(rel)) // 2, j, _peer(rel)) for j, rel in enumerate((4, 2, 6))]


def _owner_plan():
    return [(_index_of(_peer(rel)), rel - 1, _peer(rel)) for rel in range(1, N_DEV)]


def _pair_sum(g, t1, my_c, name, tr):
    _, r, c = g.shape

    def body(c_ref, g_ref, t_ref, o_ref, ob_ref):
        s = g_ref[...] + t_ref[...]
        o_ref[...] = s
        ob_ref[...] = s.astype(BF16)

    blk = pl.BlockSpec((None, tr, c), lambda k, i, cr: (k, i, 0))
    return pl.pallas_call(
        body,
        grid_spec=pltpu.PrefetchScalarGridSpec(
            num_scalar_prefetch=1, grid=(4, r // tr),
            in_specs=[pl.BlockSpec((None, tr, c), lambda k, i, cr: (2 * k + cr[0], i, 0)), blk],
            out_specs=[blk, blk]),
        out_shape=[jax.ShapeDtypeStruct((4, r, c), F32), jax.ShapeDtypeStruct((4, r, c), BF16)],
        compiler_params=_params(("parallel", "parallel")), name=name,
    )(my_c, g, t1)


def _adam_math(g, w, m, v):
    m = ADAM_B1 * m + (1.0 - ADAM_B1) * g
    v = ADAM_B2 * v + (1.0 - ADAM_B2) * (g * g)
    m_hat = m / (1.0 - ADAM_B1 ** ADAM_STEP)
    v_hat = v / (1.0 - ADAM_B2 ** ADAM_STEP)
    delta = -ADAM_LR * (m_hat / (jnp.sqrt(v_hat) + ADAM_EPS) + ADAM_WD * w)
    return delta, m, v


def _grad_sum(own, own_index, recv, name, tr):
    _, r, c = own.shape
    n = recv.shape[0]

    def body(k_ref, own_ref, *rest):
        g = own_ref[...]
        for recv_ref in rest[:n]:
            g = g + recv_ref[...].astype(F32)
        rest[n][...] = g

    def slot(j):
        return pl.BlockSpec((None, tr, c), lambda i, kr: (j, i, 0))

    return pl.pallas_call(
        body,
        grid_spec=pltpu.PrefetchScalarGridSpec(
            num_scalar_prefetch=1, grid=(r // tr,),
            in_specs=[pl.BlockSpec((None, tr, c), lambda i, kr: (kr[0], i, 0))] + [slot(j) for j in range(n)],
            out_specs=pl.BlockSpec((tr, c), lambda i, kr: (i, 0))),
        out_shape=jax.ShapeDtypeStruct((r, c), F32),
        compiler_params=_params(("parallel",)), name=name,
    )(own_index, own, *([recv] * n))


def _adam_many(g, w, m, v, row_tiles, name):
    n = len(g)

    def body(*refs):
        ins, outs = refs[:4 * n], refs[4 * n:]
        for i in range(n):
            res = _adam_math(ins[i][...], ins[n + i][...], ins[2 * n + i][...], ins[3 * n + i][...])
            for kind in range(3):
                outs[kind * n + i][...] = res[kind]

    def spec(a):
        blk = (a.shape[0] // row_tiles,) + a.shape[1:]
        return pl.BlockSpec(blk, lambda t, nd=a.ndim: (t,) + (0,) * (nd - 1))

    specs = [spec(a) for a in g]
    res = pl.pallas_call(
        body, grid=(row_tiles,), in_specs=specs * 4, out_specs=specs * 3,
        out_shape=[jax.ShapeDtypeStruct(a.shape, F32) for a in g] * 3,
        compiler_params=_params(("parallel",)), name=name,
    )(*g, *w, *m, *v)
    return res[:n], res[n:2 * n], res[2 * n:]


def _sum8(g8, name):
    _, r, c = g8.shape

    def body(g_ref, o_ref):
        acc = g_ref[0]
        for j in range(1, N_DEV):
            acc = acc + g_ref[j]
        o_ref[...] = acc

    return pl.pallas_call(
        body, grid=(1,), in_specs=[pl.BlockSpec((N_DEV, r, c), lambda i: (0, 0, 0))],
        out_specs=pl.BlockSpec((r, c), lambda i: (0, 0)), out_shape=jax.ShapeDtypeStruct((r, c), F32),
        compiler_params=_params(("arbitrary",)), name=name,
    )(g8)


def _pack(arrs, pad_rows=8):
    flat = jnp.concatenate([a.reshape(-1) for a in arrs])
    n = flat.shape[0]
    q = PACK_C * pad_rows
    tot = -(-n // q) * q
    if tot != n:
        flat = jnp.concatenate([flat, jnp.zeros((tot - n,), flat.dtype)])
    return flat.reshape(tot // PACK_C, PACK_C)


def _unpack(buf, shapes):
    flat = buf.reshape(-1)
    out, off = [], 0
    for s in shapes:
        n = int(np.prod(s))
        out.append(flat[off:off + n].reshape(s))
        off += n
    return out


GROUPS = (("w_in",),
          ("w_glu", "w_ssm_br", "w_mem_br", "w_attn_br"),
          ("w_up", "w_down"),
          ("w_mem_kv", "w_o"))
GROUP_TR = (400, 384, 512, 256)
MLP_GROUP = 2
MIXER_GROUPS = (1, 3)
ATTN_BR_FOLD = 2


def _stored_shape(name):
    r, c, ax = BIG_SHAPE[name]
    rows, cols = (r // N_DEV, c) if ax == 0 else (c // N_DEV, r)
    return (rows // ATTN_BR_FOLD, cols * ATTN_BR_FOLD) if name == "w_attn_br" else (rows, cols)


def _stored(shard, name):
    a = shard[0].T if BIG_SHAPE[name][2] == 1 else shard[0]
    return a.reshape(_stored_shape(name))


def _unstored(a, name):
    r, c, ax = BIG_SHAPE[name]
    if ax == 0:
        return a.reshape(1, r // N_DEV, c)
    return a.reshape(c // N_DEV, r).T[None]


def _pack_group(d, names):
    return jnp.concatenate([_stored(d[n], n) for n in names], axis=0)


def _split_group(buf, names):
    out, off = {}, 0
    for n in names:
        rows = _stored_shape(n)[0]
        out[n] = buf[..., off:off + rows, :]
        off += rows
    return out


def _full_stored(stacked, name):
    r, c, ax = BIG_SHAPE[name]
    return stacked.reshape((r, c) if ax == 0 else (c, r))


def _stacked_stored(full, name):
    return full.reshape((N_DEV,) + _stored_shape(name))


def _gelu_parts(x):
    c0, c1 = math.sqrt(2.0 / math.pi), 0.044715
    th = jnp.tanh(c0 * (x + c1 * x * x * x))
    return th, c0, c1


def _local_step(x, mem, tgt, wb, sp, late_weights, grads_ready, small_grads_ready):
    l = x.shape[0]
    w_a, w_g = wb["w_in"][:ZA_W], wb["w_in"][ZA_W:]

    a_re, a_im, bb_re, bb_im = _discretize(sp["ssm_lambda_re"], sp["ssm_lambda_im"], sp["ssm_log_dt"],
                                           sp["ssm_b_re"], sp["ssm_b_im"])
    a_pair = jnp.stack([a_re.reshape(1, SSM_S), a_im.reshape(1, SSM_S)])
    a_conj = jnp.stack([a_re.reshape(1, SSM_S), -a_im.reshape(1, SSM_S)])
    b_re_t, b_im_t = _bd_in(bb_re).astype(BF16), _bd_in(bb_im).astype(BF16)
    c_re_t = _bd_in(sp["ssm_c_re"].transpose(0, 2, 1)).astype(BF16)
    c_im_t = (-_bd_in(sp["ssm_c_im"].transpose(0, 2, 1))).astype(BF16)
    d_row = sp["ssm_d"].reshape(1, SSM_W)

    n1 = _rms_fwd(x, sp["norm1_g"], "rms1")
    za = _mm(n1, w_a, [BF16], tb=True, name="in_proj_a", tn=1664)
    zg = _mm(n1, w_g, [BF16], tb=True, name="in_proj_g")
    for gi in MIXER_GROUPS:
        wb = {**wb, **late_weights(gi, za)}
    u = za[:, :SSM_W]
    mq = za[:, ZA_W - MEM_W:]

    u_s = _scan_order(u)
    s_all = _ssm_scan(u_s, b_re_t, b_im_t, a_pair, reverse=False, name="ssm_scan_fwd")
    ys = _time_order(_mm((s_all, 0), c_re_t, [F32], tb=True, pair2=((s_all, 1), c_im_t), bd=SSM_BD, tm=2048, name="ssm_cs"))

    def gelu_fn(r, b):
        y0 = r[0] + b[0] * r[1].astype(F32)
        th, _, _ = _gelu_parts(y0)
        return [y0, 0.5 * y0 * (1.0 + th)], []
    y0, y1 = _ew(gelu_fn, [ys, u], [d_row], [(SSM_W, F32), (SSM_W, BF16)], [], name="ssm_gelu", tm=2048)

    def glu_epi(acc, y1t, bg):
        t = acc + bg
        return t, y1t.astype(F32) * _sigmoid(t)
    t_glu, y2 = _mm(y1, wb["w_glu"], [F32, BF16], epi=glu_epi, mn=[y1], rows=[sp["b_glu"]], name="ssm_glu")
    br_ssm = _mm(y2, wb["w_ssm_br"], [BF16], tb=True, name="ssm_br")

    qkv_p, o_g, lse_g = [], [], []
    for g, d in enumerate(DILATIONS):
        nb = l // d // ATT_WIN
        cols = [za[:, SSM_W + (3 * j + g) * ATT_GW: SSM_W + (3 * j + g + 1) * ATT_GW] for j in range(3)]
        qp, kp, vp = [_to_perm(cc, d) for cc in cols]
        qkv_p.append((qp, kp, vp))
        og, lg = _attn_fwd(qp, kp, vp, nb, "attn_fwd%d" % g)
        o_g.append(_from_perm(og, d))
        lse_g.append(_from_perm(lg, d))

    def merge_fn(r, b):
        o0, o1, o2, l0, l1, l2 = r
        mx = jnp.maximum(jnp.maximum(l0, l1), l2)
        e0, e1, e2 = jnp.exp(l0 - mx), jnp.exp(l1 - mx), jnp.exp(l2 - mx)
        tot = e0 + e1 + e2
        return [(e0 * o0 + e1 * o1 + e2 * o2) / tot, mx + jnp.log(tot)], []
    o_att, lse_tot = _ew(merge_fn, o_g + lse_g, [], [(ATT_GW, F32), (ATT_GW, F32)], [], name="attn_merge", tm=2048)
    br_attn = _mm(o_att, wb["w_attn_br"], [BF16], tb=True, name="attn_br")

    mn = _rms_fwd(mem, sp["mem_norm_g"], "rms_mem")
    kv = _mm(mn, wb["w_mem_kv"], [BF16], name="mem_kv")
    mo = _mem_fwd(mq, kv, "mem_attn_fwd")
    br_mem = _mm(mo, wb["w_mem_br"], [BF16], tb=True, name="mem_br")

    merged, h1, n2 = _gated_out_proj(zg, [br_ssm, br_attn, br_mem], sp["b_gate"], wb["w_o"], x, sp["norm2_g"],
                                     "gated_o_proj")

    def up_epi(acc):
        ra = jnp.maximum(acc, 0.0)
        return ra * ra, ra
    wm = late_weights(MLP_GROUP, n2)
    f_act, r_act = _mm(n2, wm["w_up"], [BF16, BF16], tb=True, epi=up_epi, name="mlp_up")
    def down_epi(acc, ht, tv, gf):
        hv = acc + ht
        rs = lax.rsqrt(jnp.mean(hv * hv, axis=-1, keepdims=True) + RMS_EPS)
        err = hv * rs * gf - tv
        dh, dgf = _rms_bwd_tile(hv, err * (1.0 / D_MODEL), gf)
        return dh, dgf, _colsum(err * err) * (0.5 / D_MODEL)
    dh2, d_final_g, loss_cols = _mm(f_act, wm["w_down"], [F32], epi=down_epi, mn=[h1, tgt], rows=[sp["final_g"]],
                                    n_sums=2, tk=1024, name="mlp_down")
    loss = jnp.sum(loss_cols, axis=1, keepdims=True)

    gw, gs = {}, {"final_g": d_final_g}
    d_act = _mm(dh2, wm["w_down"], [BF16], tb=True, epi=lambda acc, ra: (acc * 2.0 * ra.astype(F32),), mn=[r_act],
                name="mlp_down_dx")
    dw_down = _mm(f_act, dh2, [F32], ta=True, name="mlp_down_dw")
    dw_up = _mm(d_act, n2, [F32], ta=True, name="mlp_up_dw")
    token = grads_ready(MLP_GROUP, {"w_up": dw_up, "w_down": dw_down})
    def up_dx_epi(acc, ht, dht, g2):
        dx, dg = _rms_bwd_tile(ht, acc, g2)
        return dx + dht, dg
    dh1, gs["norm2_g"] = _mm(d_act, wm["w_up"], [F32], epi=up_dx_epi, mn=[h1, dh2],
                             rows=[sp["norm2_g"] + token[:1, :1]], n_sums=1, tk=1024, name="mlp_up_dx")
    gw["w_o"] = _mm(merged, dh1, [F32], ta=True, name="o_proj_dw")

    def gate_bwd_epi(dm, *tiles):
        dbr, dz = [], []
        for zt, bt, bias in zip(tiles[0:3], tiles[3:6], tiles[6:9]):
            gt = _sigmoid(zt.astype(F32) + bias)
            dbr.append(dm * gt)
            dz.append(dm * bt.astype(F32) * gt * (1.0 - gt))
        return (*dbr, *dz, *[_colsum(t) for t in dz])
    gate_bias = [sp["b_gate"][:, i * D_MODEL:(i + 1) * D_MODEL] for i in range(3)]
    res = _mm(dh1, wb["w_o"], [BF16] * 6, tb=True, epi=gate_bwd_epi, mn=[(zg, 0), (zg, 1), (zg, 2), br_ssm, br_attn, br_mem],
              rows=gate_bias, n_sums=3, tm=512, name="o_proj_dx")
    (dbr_ssm, dbr_attn, dbr_mem), dzg = res[0:3], res[3:6]
    gs["b_gate"] = jnp.concatenate(res[6:9], axis=1)

    gw["w_ssm_br"] = _mm(dbr_ssm, y2, [F32], ta=True, name="ssm_br_dw")
    def glu_bwd_epi(dy, y1t, tt):
        sg = _sigmoid(tt)
        dt = dy * y1t.astype(F32) * sg * (1.0 - sg)
        return dt, dy * sg, _colsum(dt)
    dt_glu, dy1a, gs["b_glu"] = _mm(dbr_ssm, wb["w_ssm_br"], [BF16, F32], epi=glu_bwd_epi, mn=[y1, t_glu], n_sums=1,
                                    name="ssm_br_dx")
    gw["w_glu"] = _mm(y1, dt_glu, [F32], ta=True, name="ssm_glu_dw")

    def gelu_bwd_epi(acc, dy1t, y0t, ut):
        th, c0, c1 = _gelu_parts(y0t)
        dg = 0.5 * (1.0 + th) + 0.5 * y0t * (1.0 - th * th) * c0 * (1.0 + 3.0 * c1 * y0t * y0t)
        dy = (acc + dy1t) * dg
        return dy, _colsum(dy * ut.astype(F32))
    dy0, gs["ssm_d"] = _mm(dt_glu, wb["w_glu"], [F32], tb=True, epi=gelu_bwd_epi, mn=[dy1a, y0, u], n_sums=1,
                           name="ssm_glu_dx")
    dy0_s = _scan_order(dy0)
    lam, da, d_b, d_c = _ssm_scan(dy0_s, c_re_t, c_im_t, a_conj, reverse=True, s_fwd=s_all, u=u_s,
                                  name="ssm_scan_bwd")
    du = _time_order(_mm((lam, 0), b_re_t, [BF16], tb=True, pair2=((lam, 1), b_im_t),
                         epi=lambda acc, dyt, dr: (acc + dyt * dr,), mn=[dy0_s], rows=[d_row], bd=SSM_BD, tm=2048, name="ssm_bu_dx"))
    gs["a_re"], gs["a_im"] = da[0], da[1]
    gs["bb_re"], gs["bb_im"] = _bd_diag(d_b[0]).transpose(0, 2, 1), _bd_diag(d_b[1]).transpose(0, 2, 1)
    gs["ssm_c_re"], gs["ssm_c_im"] = _bd_diag(d_c[0]), -_bd_diag(d_c[1])

    gw["w_attn_br"] = _mm(dbr_attn, o_att, [F32], ta=True, name="attn_br_dw")

    def do_epi(acc, ot):
        prod = acc * ot
        head = lax.broadcasted_iota(jnp.int32, prod.shape, 1) // ATT_E
        dd = jnp.zeros_like(prod)
        for h in range(ATT_HG):
            dd = jnp.where(head == h, jnp.sum(jnp.where(head == h, prod, 0.0), axis=1, keepdims=True), dd)
        return acc, dd
    do_att, dd_att = _mm(dbr_attn, wb["w_attn_br"], [BF16, F32], epi=do_epi, mn=[o_att], name="attn_br_dx")
    dq_l, dk_l, dv_l = [], [], []
    for g, d in enumerate(DILATIONS):
        nb = l // d // ATT_WIN
        qp, kp, vp = qkv_p[g]
        dq, dk, dv = _attn_bwd(qp, kp, vp, _to_perm(do_att, d), _to_perm(lse_tot, d), _to_perm(dd_att, d),
                               nb, "attn_bwd%d" % g)
        dq_l.append(_from_perm(dq, d))
        dk_l.append(_from_perm(dk, d))
        dv_l.append(_from_perm(dv, d))

    gw["w_mem_br"] = _mm(dbr_mem, mo, [F32], ta=True, name="mem_br_dw")
    dmo = _mm(dbr_mem, wb["w_mem_br"], [BF16], name="mem_br_dx")
    dmq, dkv = _mem_bwd(mq, kv, dmo, "mem_attn_bwd")
    gw["w_mem_kv"] = _mm(mn, dkv, [F32], ta=True, name="mem_kv_dw")
    dmn = _mm(dkv, wb["w_mem_kv"], [F32], tb=True, name="mem_kv_dx")
    token = sum(grads_ready(gi, gw) for gi in MIXER_GROUPS)
    gs["mem_norm_g"] = _rms_bwd(mem, dmn, None, sp["mem_norm_g"] + token[:1, :1], "rms_mem_bwd")[1]

    dza = jnp.concatenate([du] + dq_l + dk_l + dv_l + [dmq], axis=1)
    dn_a = _mm(dza, w_a, [F32], name="in_proj_a_dx", tk=1664)
    dw_a = _mm(dza, n1, [F32], ta=True, name="in_proj_a_dw", tm=1664)
    dw_g = [_mm(dzg[i], n1, [F32], ta=True, name="in_proj_g_dw%d" % i) for i in range(3)]
    gw["w_in"] = jnp.concatenate([dw_a] + dw_g, axis=0)
    token = grads_ready(0, gw) + small_grads_ready(gs)
    def in_dx_epi(acc, pt, xt, dht, g1):
        dx, dg = _rms_bwd_tile(xt, acc + pt, g1)
        return dx + dht, dg
    w_gs = [w_g[i * D_MODEL:(i + 1) * D_MODEL] for i in range(3)]
    grad_x, gs["norm1_g"] = _mm(dzg[0], w_gs[0], [F32], pair2=(dzg[1], w_gs[1], dzg[2], w_gs[2]), epi=in_dx_epi,
                                mn=[dn_a, x, dh1],
                                rows=[sp["norm1_g"] + token[:1, :1]], n_sums=1, tm=512, name="in_proj_g_dx")
    return loss, grad_x, gs


_SMALL_GRAD_ORDER = ("norm1_g", "mem_norm_g", "b_gate", "a_re", "a_im", "bb_re", "bb_im", "ssm_c_re", "ssm_c_im",
                     "ssm_d", "b_glu", "norm2_g", "final_g")


def kernel(x, mem, norm1_g, mem_norm_g, w_in, b_gate, ssm_lambda_re, ssm_lambda_im, ssm_log_dt, ssm_b_re, ssm_b_im, ssm_c_re, ssm_c_im, ssm_d, w_glu, b_glu, w_ssm_br, w_attn_br, w_mem_kv, w_mem_br, w_o, norm2_g, w_up, w_down, final_g, loss_target, m_norm1_g, m_mem_norm_g, m_w_in, m_b_gate, m_ssm_lambda_re, m_ssm_lambda_im, m_ssm_log_dt, m_ssm_b_re, m_ssm_b_im, m_ssm_c_re, m_ssm_c_im, m_ssm_d, m_w_glu, m_b_glu, m_w_ssm_br, m_w_attn_br, m_w_mem_kv, m_w_mem_br, m_w_o, m_norm2_g, m_w_up, m_w_down, m_final_g, v_norm1_g, v_mem_norm_g, v_w_in, v_b_gate, v_ssm_lambda_re, v_ssm_lambda_im, v_ssm_log_dt, v_ssm_b_re, v_ssm_b_im, v_ssm_c_re, v_ssm_c_im, v_ssm_d, v_w_glu, v_b_glu, v_w_ssm_br, v_w_attn_br, v_w_mem_kv, v_w_mem_br, v_w_o, v_norm2_g, v_w_up, v_w_down, v_final_g):
    args = dict(locals())
    w = {n: args[n] for n in ALL_W}
    m = {n: args["m_" + n] for n in ALL_W}
    v = {n: args["v_" + n] for n in ALL_W}
    my_c = lax.axis_index("c").astype(jnp.int32).reshape(1)
    my_chip = (2 * lax.axis_index("x") + lax.axis_index("y")).astype(jnp.int32).reshape(1)

    w_pack = [_pack_group(w, names) for names in GROUPS]
    my_index = (4 * lax.axis_index("x") + 2 * lax.axis_index("y") + lax.axis_index("c")).astype(jnp.int32)
    zero = jnp.zeros((), jnp.int32)
    w_all = _allgather(w_pack[0].astype(BF16), "allgather_weights0")
    wb = {n: _full_stored(part, n) for n, part in _split_group(w_all, GROUPS[0]).items()}
    gathers = {gi: _split_start(w_pack[gi].astype(BF16), N_DEV, _gather_plan, w_all, "weights_gather_start%d" % gi)
               for gi in range(1, len(GROUPS))}

    def gathered(started, after, name):
        sems, src, land, _ = started
        src, land = _split_wait(sems, src, land, _gather_wait_plan, after, name)
        return lax.dynamic_update_slice(land, src[None], (my_index, zero, zero))

    def late_weights(gi, after):
        full = gathered(gathers[gi], after, "weights_gather_wait%d" % gi)
        return {n: _full_stored(part, n) for n, part in _split_group(full, GROUPS[gi]).items()}

    pending = {}

    def grads_ready(gi, grads):
        g_pack = jnp.concatenate([_stacked_stored(grads[n], n) for n in GROUPS[gi]], axis=1)
        if gi == 0:
            t1 = _pair_exchange(g_pack, "grad_pair_exchange%d" % gi)
            p_sum, p_bf = _pair_sum(g_pack, t1, my_c, "grad_pair_sum%d" % gi, GROUP_TR[gi])
            started = _split_start(p_bf, 3, _chip_plan, p_sum, "grad_chip_exchange_start%d" % gi)
            pending[gi] = (p_sum, my_chip, started, _chip_plan)
        else:
            started = _split_start(g_pack.astype(BF16), N_DEV - 1, _owner_plan, g_pack, "grad_exchange_start%d" % gi)
            pending[gi] = (g_pack, my_index.reshape(1), started, _owner_plan)
        return started[3]

    early_small = [n for n in _SMALL_GRAD_ORDER if n != "norm1_g"]
    small_started = []

    def small_grads_ready(gs):
        started = _split_start(_pack([gs[n] for n in early_small]), N_DEV, _gather_plan, gs["mem_norm_g"],
                               "small_grads_gather_start")
        small_started.append((started, [gs[n].shape for n in early_small]))
        return started[3]

    sp = {
        "norm1_g": norm1_g + sum(started[3][:1, :1] for started in gathers.values()), "mem_norm_g": mem_norm_g, "b_gate": b_gate, "b_glu": b_glu, "norm2_g": norm2_g,
        "final_g": final_g.reshape(1, D_MODEL),
        "ssm_lambda_re": ssm_lambda_re[0], "ssm_lambda_im": ssm_lambda_im[0], "ssm_log_dt": ssm_log_dt[0],
        "ssm_b_re": ssm_b_re[0], "ssm_b_im": ssm_b_im[0], "ssm_c_re": ssm_c_re[0], "ssm_c_im": ssm_c_im[0],
        "ssm_d": ssm_d[0],
    }
    loss, grad_x, gs = _local_step(x[0], mem[0], loss_target[0], wb, sp, late_weights, grads_ready,
                                     small_grads_ready)
    loss = lax.psum(loss[0, 0], ("x", "y", "c"))
    n1_started = _split_start(_pack([gs["norm1_g"]]), N_DEV, _gather_plan, grad_x, "norm1_grad_gather_start")

    big_g = {}
    for gi, names in enumerate(GROUPS):
        own, own_index, (sems, src, land, _), plan = pending[gi]
        recv = _split_wait(sems, src, land, plan, grad_x, "grad_exchange_wait%d" % gi)[1]
        g_pack = _grad_sum(own, own_index, recv, "grad_sum%d" % gi, GROUP_TR[gi])
        for n, part in _split_group(g_pack, names).items():
            big_g[n] = _unstored(part, n)
    rows_of = lambda d, names: [d[n].reshape(d[n].shape[-2:]) for n in names]
    big_out = _adam_many(rows_of(big_g, BIG), rows_of(w, BIG), rows_of(m, BIG), rows_of(v, BIG), 8, "adam_big")
    big = [big_g] + [{n: a[None] for n, a in zip(BIG, outs)} for outs in big_out]

    (sg_started, sg_shapes), = small_started
    sg_all = jnp.concatenate([gathered(sg_started, big_out[0][0], "small_grads_gather_wait"),
                              gathered(n1_started, big_out[0][0], "norm1_grad_gather_wait")], axis=1)
    sg_sum = _sum8(sg_all, "sum_small_grads")
    n1_rows = n1_started[1].shape[0]
    sg = dict(zip(early_small, _unpack(sg_sum[:-n1_rows], sg_shapes)))
    sg["norm1_g"] = _unpack(sg_sum[-n1_rows:], [gs["norm1_g"].shape])[0]
    _, disc_vjp = jax.vjp(_discretize, sp["ssm_lambda_re"], sp["ssm_lambda_im"], sp["ssm_log_dt"],
                          sp["ssm_b_re"], sp["ssm_b_im"])
    d_lre, d_lim, d_ldt, d_bre, d_bim = disc_vjp((sg["a_re"].reshape(SSM_G, SSM_P), sg["a_im"].reshape(SSM_G, SSM_P),
                                                  sg["bb_re"], sg["bb_im"]))
    small_grad = {
        "norm1_g": sg["norm1_g"], "mem_norm_g": sg["mem_norm_g"], "b_gate": sg["b_gate"],
        "ssm_lambda_re": d_lre, "ssm_lambda_im": d_lim, "ssm_log_dt": d_ldt, "ssm_b_re": d_bre, "ssm_b_im": d_bim,
        "ssm_c_re": sg["ssm_c_re"], "ssm_c_im": sg["ssm_c_im"], "ssm_d": sg["ssm_d"], "b_glu": sg["b_glu"],
        "norm2_g": sg["norm2_g"], "final_g": sg["final_g"],
    }
    small_grad = {n: small_grad[n].reshape(w[n].shape) for n in SMALL}

    def squeezed(a):
        return a.reshape(a.shape[1:]) if a.ndim > 2 else a.reshape(1, -1)

    sq = lambda d: [squeezed(d[n]) for n in SMALL]
    small_out = _adam_many(sq(small_grad), sq(w), sq(m), sq(v), 1, "adam_small")
    small = [small_grad] + [{n: a.reshape(w[n].shape) for n, a in zip(SMALL, outs)} for outs in small_out]

    outs = [loss, grad_x[None]]
    for kind in range(4):
        for n in ALL_W:
            outs.append(big[kind][n] if n in BIG else small[kind][n])
    return tuple(outs)
```

```python
import math

import numpy as np
import jax
import jax.numpy as jnp
from jax import lax
from jax.experimental import pallas as pl
from jax.experimental.pallas import tpu as pltpu

F32 = jnp.float32
BF16 = jnp.bfloat16
_MXU = jnp.bfloat16

D_MODEL = 1024
SSM_G, SSM_H, SSM_P = 32, 16, 64
SSM_W = SSM_G * SSM_H
SSM_S = SSM_G * SSM_P
SSM_BD = 4
ATT_E = 64
ATT_HG = 4
ATT_GW = ATT_HG * ATT_E
ATT_WIN = 128
ATT_QB = 8
ATT_QB_FWD = 4
DILATIONS = (1, 4, 16)
MEM_H, MEM_E = 4, 128
MEM_W = MEM_H * MEM_E
ZA_W = SSM_W + 9 * ATT_GW + MEM_W
ZG_W = 3 * D_MODEL
IN_W = ZA_W + ZG_W
RMS_EPS = 1e-6
NEG_INF = -1e30

ADAM_LR, ADAM_B1, ADAM_B2, ADAM_EPS, ADAM_WD, ADAM_STEP = 0.001, 0.9, 0.999, 1e-08, 0.01, 10

N_DEV = 8
PACK_C = 512
_VMEM_LIMIT = 56 * 1024 * 1024
SUBLANES = 16
SCAN_SEG = 128
SCAN_CHAINS = 4
SCAN_UNROLL = 4
SCAN_W = 128

BIG = ("w_in", "w_glu", "w_ssm_br", "w_attn_br", "w_mem_kv", "w_mem_br", "w_o", "w_up", "w_down")
BIG_SHAPE = {
    "w_in": (D_MODEL, IN_W, 1), "w_glu": (SSM_W, SSM_W, 0), "w_ssm_br": (SSM_W, D_MODEL, 1),
    "w_attn_br": (ATT_GW, D_MODEL, 1), "w_mem_kv": (D_MODEL, 2 * MEM_W, 0), "w_mem_br": (MEM_W, D_MODEL, 1),
    "w_o": (D_MODEL, D_MODEL, 0), "w_up": (D_MODEL, 4 * D_MODEL, 1), "w_down": (4 * D_MODEL, D_MODEL, 0),
}
SMALL = ("norm1_g", "mem_norm_g", "b_gate", "ssm_lambda_re", "ssm_lambda_im", "ssm_log_dt", "ssm_b_re",
         "ssm_b_im", "ssm_c_re", "ssm_c_im", "ssm_d", "b_glu", "norm2_g", "final_g")
ALL_W = ("norm1_g", "mem_norm_g", "w_in", "b_gate", "ssm_lambda_re", "ssm_lambda_im", "ssm_log_dt", "ssm_b_re",
         "ssm_b_im", "ssm_c_re", "ssm_c_im", "ssm_d", "w_glu", "b_glu", "w_ssm_br", "w_attn_br", "w_mem_kv",
         "w_mem_br", "w_o", "norm2_g", "w_up", "w_down", "final_g")


def _params(sem):
    return pltpu.CompilerParams(dimension_semantics=sem, vmem_limit_bytes=_VMEM_LIMIT)


def _pick(n, cap):
    if n <= cap:
        return n
    t = (cap // 128) * 128
    while n % t:
        t -= 128
    return t


def _mm(a, b, outs, *, name, ta=False, tb=False, epi=None, mn=(), rows=(), pair2=None, bd=0, n_sums=0,
        tm=1024, tn=1024, tk=2048):
    ab = [a, b] + (list(pair2) if pair2 is not None else [])
    a_shape, b_shape = ab[0].shape, ab[1].shape
    m = a_shape[1] if ta else a_shape[0]
    k = a_shape[0] if ta else a_shape[1]
    n = b_shape[0] if tb else b_shape[1]
    assert k == (b_shape[1] if tb else b_shape[0]), (name, a_shape, b_shape)
    out_n = n
    if bd and ta:
        assert not tb
        tm, tn, tk = m // bd, n // bd, _pick(k, tk)
        grid, out_n = (bd, 1, k // tk), tn
        a_blk = ((tk, tm), lambda i, j, kk: (kk, i))
        b_blk = ((tk, tn), lambda i, j, kk: (kk, i))
        mn_spec = pl.BlockSpec((tm, tn), lambda i, j, kk: (i, 0))
    elif bd:
        tm, tn, tk = _pick(m, tm), n // bd, k // bd
        grid = (m // tm, bd, 1)
        a_blk = ((tm, tk), lambda i, j, kk: (i, j))
        b_blk = ((tn, tk) if tb else (tk, tn), lambda i, j, kk: (j, j))
        mn_spec = pl.BlockSpec((tm, tn), lambda i, j, kk: (i, j))
    else:
        tm, tn, tk = _pick(m, tm), _pick(n, tn), _pick(k, tk)
        grid = (m // tm, n // tn, k // tk)
        a_blk = ((tk, tm), lambda i, j, kk: (kk, i)) if ta else ((tm, tk), lambda i, j, kk: (i, kk))
        b_blk = ((tn, tk), lambda i, j, kk: (j, kk)) if tb else ((tk, tn), lambda i, j, kk: (kk, j))
        mn_spec = pl.BlockSpec((tm, tn), lambda i, j, kk: (i, j))

    ab_specs = [pl.BlockSpec(*(a_blk if q % 2 == 0 else b_blk)) for q in range(len(ab))]
    mn_arrays = [e[0] if isinstance(e, tuple) else e for e in mn]
    mn_specs = [pl.BlockSpec((tm, tn), lambda i, j, kk, c=e[1]: (i, c)) if isinstance(e, tuple) else mn_spec
                for e in mn]
    nk = grid[2]
    row_spec = pl.BlockSpec((1, tn), lambda i, j, kk: (0, j))
    n_ex, n_out = len(mn) + len(rows), len(outs)
    assert n_sums == 0 or (grid[1] == 1 and not bd)
    dims = (((0 if ta else 1,), (1 if tb else 0,)), ((), ()))

    def body(*refs):
        ab_refs, rest = refs[:len(ab)], refs[len(ab):]
        ex, o_refs = rest[:n_ex], rest[n_ex:n_ex + n_out]
        s_refs = rest[n_ex + n_out:n_ex + n_out + n_sums]
        first_row_tile = pl.program_id(0) == 0
        kk = pl.program_id(2)

        part = None
        for a_ref, b_ref in zip(ab_refs[0::2], ab_refs[1::2]):
            d = lax.dot_general(a_ref[...].astype(_MXU), b_ref[...].astype(_MXU), dims, preferred_element_type=F32)
            part = d if part is None else part + d

        def finish(total):
            vals = (total,) if epi is None else epi(total, *[r[...] for r in ex])
            for r, v in zip(o_refs, vals):
                r[...] = v.astype(r.dtype)
            for r, v in zip(s_refs, vals[n_out:]):
                r[...] = jnp.where(first_row_tile, v, r[...] + v)

        if nk == 1:
            finish(part)
        else:
            acc = rest[-1]

            @pl.when(kk == 0)
            def _():
                acc[...] = jnp.zeros_like(acc)

            acc[...] += part

            @pl.when(kk == nk - 1)
            def _():
                finish(acc[...])

    res = pl.pallas_call(
        body, grid=grid,
        in_specs=ab_specs + mn_specs + [row_spec] * len(rows),
        out_specs=[mn_spec] * n_out + [row_spec] * n_sums,
        out_shape=[jax.ShapeDtypeStruct((m, out_n), dt) for dt in outs]
        + [jax.ShapeDtypeStruct((1, out_n), F32)] * n_sums,
        scratch_shapes=[pltpu.VMEM((tm, tn), F32)] if nk > 1 else [],
        compiler_params=_params(("arbitrary" if n_sums else "parallel", "parallel", "arbitrary")), name=name,
    )(*ab, *mn_arrays, *rows)
    return res[0] if n_out + n_sums == 1 else res


def _ew(fn, rows, bcs, out_rows, out_accs, *, name, tm=256):
    r = rows[0].shape[0]
    tm = min(tm, r)
    assert r % tm == 0
    nr, nb, no, na = len(rows), len(bcs), len(out_rows), len(out_accs)

    def body(*refs):
        i = pl.program_id(0)
        r_in, b_in = refs[:nr], refs[nr:nr + nb]
        o_r, o_a = refs[nr + nb:nr + nb + no], refs[nr + nb + no:]
        outs, accs = fn([x[...] for x in r_in], [x[...] for x in b_in])
        for ref, v in zip(o_r, outs):
            ref[...] = v.astype(ref.dtype)
        if na:
            @pl.when(i == 0)
            def _():
                for ref in o_a:
                    ref[...] = jnp.zeros_like(ref)

            for ref, v in zip(o_a, accs):
                ref[...] += v

    res = pl.pallas_call(
        body, grid=(r // tm,),
        in_specs=[pl.BlockSpec((tm, x.shape[1]), lambda i: (i, 0)) for x in rows]
        + [pl.BlockSpec((1, x.shape[1]), lambda i: (0, 0)) for x in bcs],
        out_specs=[pl.BlockSpec((tm, c), lambda i: (i, 0)) for c, _ in out_rows]
        + [pl.BlockSpec((1, c), lambda i: (0, 0)) for c in out_accs],
        out_shape=[jax.ShapeDtypeStruct((r, c), dt) for c, dt in out_rows]
        + [jax.ShapeDtypeStruct((1, c), F32) for c in out_accs],
        compiler_params=_params(("arbitrary",)), name=name,
    )(*rows, *bcs)
    return res


def _colsum(x):
    return jnp.sum(x, axis=0, keepdims=True)


def _sigmoid(x):
    return 1.0 / (1.0 + jnp.exp(-x))


def _rms_bwd_tile(xv, dv, g):
    rs = lax.rsqrt(jnp.mean(xv * xv, axis=-1, keepdims=True) + RMS_EPS)
    gd = dv * g
    dx = rs * gd - xv * (rs * rs * rs) * jnp.mean(gd * xv, axis=-1, keepdims=True)
    return dx, _colsum(dv * xv * rs)


def _rms_fwd(x, g, name):
    def fn(r, b):
        xv = r[0]
        rs = lax.rsqrt(jnp.mean(xv * xv, axis=-1, keepdims=True) + RMS_EPS)
        return [xv * rs * b[0]], []
    return _ew(fn, [x], [g], [(x.shape[1], BF16)], [], name=name, tm=1024)[0]


def _rms_bwd(x, dn, res, g, name):
    def fn(r, b):
        dx, dg = _rms_bwd_tile(r[0], r[1], b[0])
        if res is not None:
            dx = dx + r[2]
        return [dx], [dg]
    rows = [x, dn] + ([res] if res is not None else [])
    return _ew(fn, rows, [g], [(x.shape[1], F32)], [x.shape[1]], name=name)


def _scan_order(x):
    l, c = x.shape
    return x.reshape(l // (SUBLANES * SCAN_SEG), SUBLANES, SCAN_SEG, c).transpose(0, 2, 1, 3).reshape(l, c)


def _time_order(x):
    l, c = x.shape
    return x.reshape(l // (SUBLANES * SCAN_SEG), SCAN_SEG, SUBLANES, c).transpose(0, 2, 1, 3).reshape(l, c)


def _ssm_scan(x, w_re, w_im, a_pair, *, reverse, s_fwd=None, u=None, name):
    l = x.shape[0]
    seg, w = SCAN_SEG, SCAN_W
    bd_w = SSM_W // SSM_BD
    tiles_per_bd = SSM_S // SSM_BD // w
    nch = min(SCAN_CHAINS, l // (SUBLANES * seg))
    chain_rows = SUBLANES * seg
    tb = nch * chain_rows
    nt = l // tb
    with_da = s_fwd is not None
    assert reverse or not with_da

    def tt(t):
        return nt - 1 - t if reverse else t

    def body(*refs):
        if with_da:
            (x_ref, wr_ref, wi_ref, a_ref, sf_ref, sp_ref, u_ref, s_ref, da_ref, dw_ref, dx_ref,
             p_ref, c_ref, b_scr, l_scr) = refs
        else:
            x_ref, wr_ref, wi_ref, a_ref, s_ref, p_ref, c_ref, b_scr, l_scr = refs
        t_blk = pl.program_id(1)
        ar, ai = a_ref[0], a_ref[1]

        @pl.when(t_blk == 0)
        def _():
            def pstep(i, carry):
                pr, pi = carry
                p_ref[0, pl.ds(i, 1), :] = pr
                p_ref[1, pl.ds(i, 1), :] = pi
                return pr * ar - pi * ai, pr * ai + pi * ar

            lax.fori_loop(0, seg, pstep, (ar, ai))
            c_ref[...] = jnp.zeros_like(c_ref)
            if with_da:
                da_ref[...] = jnp.zeros_like(da_ref)
                dw_ref[...] = jnp.zeros_like(dw_ref)
                dx_ref[...] = jnp.zeros_like(dx_ref)

        xb = x_ref[...].astype(_MXU)
        b_scr[:, :w] = jnp.dot(xb, wr_ref[...], preferred_element_type=F32)
        b_scr[:, w:] = jnp.dot(xb, wi_ref[...], preferred_element_type=F32)
        arb, aib = jnp.broadcast_to(ar, (SUBLANES, w)), jnp.broadcast_to(ai, (SUBLANES, w))
        zero = jnp.zeros((SUBLANES, w), F32)

        def tile(g, step):
            return pl.ds(pl.multiple_of(g * chain_rows + step * SUBLANES, SUBLANES), SUBLANES)

        def rows(g, i):
            return tile(g, seg - 1 - i if reverse else i)

        def local_step(i, carry):
            out = []
            for g in range(nch):
                sr, si = carry[2 * g], carry[2 * g + 1]
                idx = rows(g, i)
                sr, si = arb * sr - aib * si + b_scr[idx, :w], arb * si + aib * sr + b_scr[idx, w:]
                l_scr[idx, :w] = sr
                l_scr[idx, w:] = si
                out += [sr, si]
            return tuple(out)

        def unrolled(step_fn, first):
            def trip(q, carry):
                for r in range(SCAN_UNROLL):
                    carry = step_fn(first + q * SCAN_UNROLL + r, carry)
                return carry
            return trip

        ends = lax.fori_loop(0, seg // SCAN_UNROLL, unrolled(local_step, 0), (zero,) * (2 * nch))

        a_seg_r, a_seg_i = p_ref[0, seg - 1:seg, :], p_ref[1, seg - 1:seg, :]
        cr, ci = c_ref[0], c_ref[1]
        sub = lax.broadcasted_iota(jnp.int32, (SUBLANES, w), 0)
        ins = [[zero, zero] for _ in range(nch)]
        order = [(g, k) for g in range(nch) for k in range(SUBLANES)]
        for g, k in (order[::-1] if reverse else order):
            ins[g] = [jnp.where(sub == k, cr, ins[g][0]), jnp.where(sub == k, ci, ins[g][1])]
            er, ei = ends[2 * g][k:k + 1], ends[2 * g + 1][k:k + 1]
            cr, ci = er + a_seg_r * cr - a_seg_i * ci, ei + a_seg_r * ci + a_seg_i * cr
        c_ref[0] = cr
        c_ref[1] = ci

        def fix(g, i):
            idx = rows(g, i)
            pr, pi = p_ref[0, pl.ds(i, 1), :], p_ref[1, pl.ds(i, 1), :]
            sr = l_scr[idx, :w] + pr * ins[g][0] - pi * ins[g][1]
            si = l_scr[idx, w:] + pr * ins[g][1] + pi * ins[g][0]
            s_ref[idx, :w] = sr.astype(s_ref.dtype)
            s_ref[idx, w:] = si.astype(s_ref.dtype)
            return sr, si

        if not with_da:
            def fix_step(i, carry):
                for g in range(nch):
                    fix(g, i)
                return carry

            lax.fori_loop(0, seg // SCAN_UNROLL, unrolled(fix_step, 0), 0)
        else:
            def adj_step(i, acc):
                acc_r, acc_i = acc
                for g in range(nch):
                    lr, li = fix(g, i)
                    prev = tile(g, seg - 2 - i)
                    fr, fi = sf_ref[prev, :w].astype(F32), sf_ref[prev, w:].astype(F32)
                    acc_r, acc_i = acc_r + lr * fr + li * fi, acc_i + li * fr - lr * fi
                return acc_r, acc_i

            acc = lax.fori_loop(0, seg // SCAN_UNROLL - 1, unrolled(adj_step, 0), (zero, zero))
            for i in range(seg - SCAN_UNROLL, seg - 1):
                acc = adj_step(i, acc)
            acc_r, acc_i = acc
            first_block = tt(t_blk) == 0
            for g in range(nch):
                lr, li = fix(g, seg - 1)
                seg_ends = tile(g, seg - 1)
                if g == 0:
                    pvr = jnp.where(first_block, 0.0, sp_ref[SUBLANES - 1:SUBLANES, :w].astype(F32))
                    pvi = jnp.where(first_block, 0.0, sp_ref[SUBLANES - 1:SUBLANES, w:].astype(F32))
                else:
                    pvr = sf_ref[g * chain_rows - 1:g * chain_rows, :w].astype(F32)
                    pvi = sf_ref[g * chain_rows - 1:g * chain_rows, w:].astype(F32)
                fr = jnp.where(sub == 0, pvr, pltpu.roll(sf_ref[seg_ends, :w].astype(F32), 1, 0))
                fi = jnp.where(sub == 0, pvi, pltpu.roll(sf_ref[seg_ends, w:].astype(F32), 1, 0))
                acc_r = acc_r + lr * fr + li * fi
                acc_i = acc_i + li * fr - lr * fi
            da_ref[0] += jnp.sum(acc_r, axis=0, keepdims=True)
            da_ref[1] += jnp.sum(acc_i, axis=0, keepdims=True)
            dw_ref[...] += _tn_dot(u_ref[...], s_ref[...])
            dx_ref[...] += _tn_dot(xb, sf_ref[...])

    x_spec = pl.BlockSpec((tb, bd_w), lambda j, t: (tt(t), j // tiles_per_bd))
    w_spec = pl.BlockSpec((bd_w, w), lambda j, t: (j // tiles_per_bd, j))
    d_spec = pl.BlockSpec((bd_w, 2 * w), lambda j, t: (j // tiles_per_bd, j % tiles_per_bd))
    a_spec = pl.BlockSpec((2, 1, w), lambda j, t: (0, 0, j))
    s_spec = pl.BlockSpec((tb, 2 * w), lambda j, t: (tt(t), j))
    in_specs, args = [x_spec, w_spec, w_spec, a_spec], [x, w_re, w_im, a_pair]
    out_specs, out_shape = [s_spec], [jax.ShapeDtypeStruct((l, 2 * SSM_S), BF16)]
    scratch = [pltpu.VMEM((2, seg, w), F32), pltpu.VMEM((2, 1, w), F32)] + [pltpu.VMEM((tb, 2 * w), F32)] * 2
    if with_da:
        in_specs += [s_spec, pl.BlockSpec((SUBLANES, 2 * w),
                                          lambda j, t: (jnp.maximum(tt(t) * (tb // SUBLANES) - 1, 0), j)),
                     x_spec]
        args += [s_fwd, s_fwd, u]
        out_specs += [a_spec, d_spec, d_spec]
        out_shape += ([jax.ShapeDtypeStruct((2, 1, SSM_S), F32)]
                      + [jax.ShapeDtypeStruct((SSM_W, 2 * SSM_S // SSM_BD), F32)] * 2)
    res = pl.pallas_call(
        body, grid=(SSM_S // w, nt), in_specs=in_specs, out_specs=out_specs, out_shape=out_shape,
        scratch_shapes=scratch, compiler_params=_params(("parallel", "arbitrary")), name=name,
    )(*args)
    return res if with_da else res[0]


def _nt_dot(x, y):
    return lax.dot_general(x.astype(_MXU), y.astype(_MXU), (((1,), (1,)), ((), ())), preferred_element_type=F32)


def _tn_dot(x, y):
    return lax.dot_general(x.astype(_MXU), y.astype(_MXU), (((0,), (0,)), ((), ())), preferred_element_type=F32)


def _nn_dot(x, y):
    return jnp.dot(x.astype(_MXU), y.astype(_MXU), preferred_element_type=F32)


def _attn_mask2(gb, nb):
    qi = lax.broadcasted_iota(jnp.int32, (ATT_WIN, 2 * ATT_WIN), 0)
    c = lax.broadcasted_iota(jnp.int32, (ATT_WIN, 2 * ATT_WIN), 1)
    has_prev = (gb % nb) != 0
    prev_ok = jnp.logical_and(jnp.logical_and(c < ATT_WIN, c >= qi), has_prev)
    own_ok = jnp.logical_and(c >= ATT_WIN, c - ATT_WIN <= qi)
    return jnp.logical_or(prev_ok, own_ok)


def _attn_specs(qb):
    cur = pl.BlockSpec((qb * ATT_WIN, ATT_GW), lambda i: (i, 0))
    prev = pl.BlockSpec((ATT_WIN, ATT_GW), lambda i: (jnp.maximum(qb * i - 1, 0), 0))
    return cur, prev


def _attn_fwd(q, k, v, nb, name):
    l = q.shape[0]
    scale = ATT_E ** -0.5
    w = ATT_WIN

    qb = ATT_QB_FWD

    def body(q_ref, kc_ref, kp_ref, vc_ref, vp_ref, o_ref, lse_ref):
        i = pl.program_id(0)
        masks = [_attn_mask2(qb * i + b, nb) for b in range(qb)]
        for h in range(ATT_HG):
            sl = slice(h * ATT_E, (h + 1) * ATT_E)
            k_ext = jnp.concatenate([kp_ref[:, sl], kc_ref[:, sl]], axis=0)
            v_ext = jnp.concatenate([vp_ref[:, sl], vc_ref[:, sl]], axis=0)
            for b in range(qb):
                r, kr = slice(b * w, (b + 1) * w), slice(b * w, (b + 2) * w)
                s = jnp.where(masks[b], _nt_dot(q_ref[r, sl], k_ext[kr]) * scale, NEG_INF)
                mx = jnp.max(s, axis=-1, keepdims=True)
                p = jnp.exp(s - mx)
                den = jnp.sum(p, axis=-1, keepdims=True)
                o_ref[r, sl] = _nn_dot(p, v_ext[kr]) / den
                lse_ref[r, sl] = jnp.broadcast_to(mx + jnp.log(den), (w, ATT_E))

    cur, prev = _attn_specs(qb)
    return pl.pallas_call(
        body, grid=(l // (qb * w),), in_specs=[cur, cur, prev, cur, prev], out_specs=[cur, cur],
        out_shape=[jax.ShapeDtypeStruct((l, ATT_GW), F32)] * 2,
        compiler_params=_params(("parallel",)), name=name,
    )(q, k, k, v, v)


def _attn_bwd(q, k, v, do, lse, dd, nb, name):
    l = q.shape[0]
    scale = ATT_E ** -0.5
    w = ATT_WIN
    nblk = l // w

    def body(q_ref, kc_ref, kp_ref, vc_ref, vp_ref, do_ref, lse_ref, dd_ref, qn_ref, don_ref, lsen_ref, ddn_ref,
             dq_ref, dk_ref, dv_ref, dk_acc, dv_acc):
        i = pl.program_id(0)
        masks = [_attn_mask2(ATT_QB * i + b, nb) for b in range(ATT_QB)]
        nxt = ATT_QB * (i + 1)
        nxt_attends = jnp.logical_and(nxt < nblk, (nxt % nb) != 0)
        qi = lax.broadcasted_iota(jnp.int32, (w, w), 0)
        kj = lax.broadcasted_iota(jnp.int32, (w, w), 1)
        mask_n = jnp.logical_and(kj >= qi, nxt_attends)
        dk_acc[...] = jnp.zeros_like(dk_acc)
        dv_acc[...] = jnp.zeros_like(dv_acc)
        for h in range(ATT_HG):
            sl, col = slice(h * ATT_E, (h + 1) * ATT_E), slice(h * ATT_E, h * ATT_E + 1)
            k_ext = jnp.concatenate([kp_ref[:, sl], kc_ref[:, sl]], axis=0)
            v_ext = jnp.concatenate([vp_ref[:, sl], vc_ref[:, sl]], axis=0)
            for b in range(ATT_QB):
                r, kr = slice(b * w, (b + 1) * w), slice(b * w, (b + 2) * w)
                qh, doh, k2, v2 = q_ref[r, sl], do_ref[r, sl], k_ext[kr], v_ext[kr]
                p = jnp.where(masks[b], jnp.exp(_nt_dot(qh, k2) * scale - lse_ref[r, col]), 0.0)
                ds = p * (_nt_dot(doh, v2) - dd_ref[r, col]) * scale
                dq_ref[r, sl] = _nn_dot(ds, k2).astype(dq_ref.dtype)
                dk2, dv2 = _tn_dot(ds, qh), _tn_dot(p, doh)
                dk_acc[r, sl] += dk2[w:]
                dv_acc[r, sl] += dv2[w:]
                if b > 0:
                    rp = slice((b - 1) * w, b * w)
                    dk_acc[rp, sl] += dk2[:w]
                    dv_acc[rp, sl] += dv2[:w]
            last = slice((ATT_QB - 1) * w, ATT_QB * w)
            kl, vl, qn, don = kc_ref[last, sl], vc_ref[last, sl], qn_ref[:, sl], don_ref[:, sl]
            pn = jnp.where(mask_n, jnp.exp(_nt_dot(qn, kl) * scale - lsen_ref[:, col]), 0.0)
            dsn = pn * (_nt_dot(don, vl) - ddn_ref[:, col]) * scale
            dk_acc[last, sl] += _tn_dot(dsn, qn)
            dv_acc[last, sl] += _tn_dot(pn, don)
        dk_ref[...] = dk_acc[...].astype(dk_ref.dtype)
        dv_ref[...] = dv_acc[...].astype(dv_ref.dtype)

    cur, prev = _attn_specs(ATT_QB)
    nxt_spec = pl.BlockSpec((w, ATT_GW), lambda i: (jnp.minimum(ATT_QB * (i + 1), nblk - 1), 0))
    return pl.pallas_call(
        body, grid=(l // (ATT_QB * w),),
        in_specs=[cur, cur, prev, cur, prev, cur, cur, cur, nxt_spec, nxt_spec, nxt_spec, nxt_spec],
        out_specs=[cur] * 3, out_shape=[jax.ShapeDtypeStruct((l, ATT_GW), BF16)] * 3,
        scratch_shapes=[pltpu.VMEM((ATT_QB * w, ATT_GW), F32)] * 2,
        compiler_params=_params(("parallel",)), name=name,
    )(q, k, k, v, v, do, lse, dd, q, do, lse, dd)


def _to_perm(a, d):
    if d == 1:
        return a
    l, c = a.shape
    return a.reshape(l // d, d, c).transpose(1, 0, 2).reshape(l, c)


def _from_perm(a, d):
    if d == 1:
        return a
    l, c = a.shape
    return a.reshape(d, l // d, c).transpose(1, 0, 2).reshape(l, c)


def _mem_probs(qh, kh):
    s = _nt_dot(qh, kh) * (MEM_E ** -0.5)
    e = jnp.exp(s - jnp.max(s, axis=-1, keepdims=True))
    return e / jnp.sum(e, axis=-1, keepdims=True)


def _mem_fwd(mq, kv, name, tm=512):
    l, nm = mq.shape[0], kv.shape[0]

    def body(q_ref, kv_ref, o_ref):
        for h in range(MEM_H):
            sl = slice(h * MEM_E, (h + 1) * MEM_E)
            p = _mem_probs(q_ref[:, sl], kv_ref[:, sl])
            o_ref[:, sl] = _nn_dot(p, kv_ref[:, MEM_W + h * MEM_E:MEM_W + (h + 1) * MEM_E]).astype(o_ref.dtype)

    return pl.pallas_call(
        body, grid=(l // tm,),
        in_specs=[pl.BlockSpec((tm, MEM_W), lambda i: (i, 0)), pl.BlockSpec((nm, 2 * MEM_W), lambda i: (0, 0))],
        out_specs=pl.BlockSpec((tm, MEM_W), lambda i: (i, 0)),
        out_shape=jax.ShapeDtypeStruct((l, MEM_W), BF16),
        compiler_params=_params(("parallel",)), name=name,
    )(mq, kv)


def _mem_bwd(mq, kv, dmo, name, tm=512):
    l, nm = mq.shape[0], kv.shape[0]
    scale = MEM_E ** -0.5

    def body(q_ref, kv_ref, do_ref, dq_ref, dkv_ref):
        @pl.when(pl.program_id(0) == 0)
        def _():
            dkv_ref[...] = jnp.zeros_like(dkv_ref)

        for h in range(MEM_H):
            sl = slice(h * MEM_E, (h + 1) * MEM_E)
            vsl = slice(MEM_W + h * MEM_E, MEM_W + (h + 1) * MEM_E)
            qh, kh, vh, doh = q_ref[:, sl], kv_ref[:, sl], kv_ref[:, vsl], do_ref[:, sl]
            p = _mem_probs(qh, kh)
            dp = _nt_dot(doh, vh)
            ds = p * (dp - jnp.sum(dp * p, axis=-1, keepdims=True)) * scale
            dq_ref[:, sl] = _nn_dot(ds, kh).astype(dq_ref.dtype)
            dkv_ref[:, sl] += _tn_dot(ds, qh)
            dkv_ref[:, vsl] += _tn_dot(p, doh)

    row = pl.BlockSpec((tm, MEM_W), lambda i: (i, 0))
    full = pl.BlockSpec((nm, 2 * MEM_W), lambda i: (0, 0))
    return pl.pallas_call(
        body, grid=(l // tm,), in_specs=[row, full, row], out_specs=[row, full],
        out_shape=[jax.ShapeDtypeStruct((l, MEM_W), BF16), jax.ShapeDtypeStruct((nm, 2 * MEM_W), F32)],
        compiler_params=_params(("arbitrary",)), name=name,
    )(mq, kv, dmo)


def _gated_out_proj(zg, branches, b_gate, w_o, x, g2, name, tm=512):
    l, d = x.shape
    nbr = len(branches)

    def body(zg_ref, *rest):
        br_refs, (bg_ref, w_ref, x_ref, g2_ref, m_ref, h_ref, n_ref) = rest[:nbr], rest[nbr:]
        merged = jnp.zeros((tm, d), F32)
        for i, br_ref in enumerate(br_refs):
            cols = slice(i * d, (i + 1) * d)
            merged += _sigmoid(zg_ref[:, cols].astype(F32) + bg_ref[:, cols]) * br_ref[...].astype(F32)
        mb = merged.astype(BF16)
        m_ref[...] = mb
        hv = jnp.dot(mb.astype(_MXU), w_ref[...].astype(_MXU), preferred_element_type=F32) + x_ref[...]
        h_ref[...] = hv
        rs = lax.rsqrt(jnp.mean(hv * hv, axis=-1, keepdims=True) + RMS_EPS)
        n_ref[...] = (hv * rs * g2_ref[...]).astype(n_ref.dtype)

    row = lambda c: pl.BlockSpec((tm, c), lambda i: (i, 0))
    full = lambda a: pl.BlockSpec(a.shape, lambda i: (0, 0))
    return pl.pallas_call(
        body, grid=(l // tm,),
        in_specs=[row(nbr * d)] + [row(d)] * nbr + [full(b_gate), full(w_o), row(d), full(g2)],
        out_specs=[row(d)] * 3,
        out_shape=[jax.ShapeDtypeStruct((l, d), BF16), jax.ShapeDtypeStruct((l, d), F32),
                   jax.ShapeDtypeStruct((l, d), BF16)],
        compiler_params=_params(("parallel",)), name=name,
    )(zg, *branches, b_gate, w_o, x, g2)


def _discretize(lam_re, lam_im, log_dt, b_re, b_im):
    dt = jnp.exp(log_dt)[:, None]
    mag = jnp.exp(lam_re * dt)
    a_re, a_im = mag * jnp.cos(lam_im * dt), mag * jnp.sin(lam_im * dt)
    nr, ni = a_re - 1.0, a_im
    den = lam_re * lam_re + lam_im * lam_im
    coef_re = (nr * lam_re + ni * lam_im) / den
    coef_im = (ni * lam_re - nr * lam_im) / den
    bb_re = coef_re[..., None] * b_re - coef_im[..., None] * b_im
    bb_im = coef_re[..., None] * b_im + coef_im[..., None] * b_re
    return a_re, a_im, bb_re, bb_im


def _tiled(re, im):
    parts = []
    for j in range(SSM_S // SCAN_W):
        parts += [re[:, j * SCAN_W:(j + 1) * SCAN_W], im[:, j * SCAN_W:(j + 1) * SCAN_W]]
    return jnp.concatenate(parts, axis=1)


def _untiled(x):
    t = x.reshape(x.shape[0], -1, 2, SCAN_W)
    return t[:, :, 0, :].reshape(x.shape[0], -1), t[:, :, 1, :].reshape(x.shape[0], -1)


def _bd_in(bb):
    return jnp.einsum("gph,gk->ghkp", bb, jnp.eye(SSM_G, dtype=bb.dtype)).reshape(SSM_W, SSM_S)


def _bd_diag(x):
    gb = SSM_G // SSM_BD
    t = x.reshape(SSM_BD, gb, SSM_H, gb, SSM_P)
    return jnp.einsum("bghgp->bghp", t).reshape(SSM_G, SSM_H, SSM_P)


_ANY = pl.BlockSpec(memory_space=pl.ANY)
_MESH = pl.DeviceIdType.MESH


def _allgather(x, name):
    def body(x_ref, out_ref, send_sems, recv_sems, local_sem):
        mx, my, mc = lax.axis_index("x"), lax.axis_index("y"), lax.axis_index("c")
        me, sibling = (mx, my, mc), (mx, my, 1 - mc)
        chips = [(1 - mx, my), (mx, 1 - my), (1 - mx, 1 - my)]

        def blk(px, py, pc):
            return out_ref.at[4 * px + 2 * py + pc]

        def copy(k, block, to, src=None):
            return pltpu.make_async_remote_copy(
                src_ref=blk(*block) if src is None else src, dst_ref=blk(*block),
                send_sem=send_sems.at[k], recv_sem=recv_sems.at[k], device_id=to, device_id_type=_MESH)

        mine = pltpu.make_async_copy(x_ref, blk(*me), local_sem)
        mine.start()
        first = [copy(0, me, sibling, src=x_ref)]
        first += [copy(1 + j, me, (*chip, mc), src=x_ref) for j, chip in enumerate(chips)]
        for cp in first:
            cp.start()
        passed = [copy(4 + j, (*chip, mc), sibling) for j, chip in enumerate(chips)]
        for j, chip in enumerate(chips):
            copy(1 + j, (*chip, mc), me).wait_recv()
            passed[j].start()
        copy(0, sibling, me).wait_recv()
        for j, chip in enumerate(chips):
            copy(4 + j, (*chip, 1 - mc), me).wait_recv()
        for cp in first + passed:
            cp.wait_send()
        mine.wait()

    return pl.pallas_call(
        body, out_shape=jax.ShapeDtypeStruct((N_DEV,) + x.shape, x.dtype), in_specs=[_ANY], out_specs=_ANY,
        scratch_shapes=[pltpu.SemaphoreType.DMA((7,)), pltpu.SemaphoreType.DMA((7,)), pltpu.SemaphoreType.DMA],
        name=name,
    )(x)


def _pair_exchange(g, name):
    def body(g_ref, out_ref, send_sems, recv_sems):
        mx, my, mc = lax.axis_index("x"), lax.axis_index("y"), lax.axis_index("c")
        copies = [pltpu.make_async_remote_copy(
            src_ref=g_ref.at[2 * k + (1 - mc)], dst_ref=out_ref.at[k], send_sem=send_sems.at[k],
            recv_sem=recv_sems.at[k], device_id=(mx, my, 1 - mc), device_id_type=_MESH) for k in range(4)]
        for cp in copies:
            cp.start()
        for cp in copies:
            cp.wait()

    return pl.pallas_call(
        body, out_shape=jax.ShapeDtypeStruct((4,) + g.shape[1:], g.dtype), in_specs=[_ANY], out_specs=_ANY,
        scratch_shapes=[pltpu.SemaphoreType.DMA((4,)), pltpu.SemaphoreType.DMA((4,))], name=name,
    )(g)


_HBM = pl.BlockSpec(memory_space=pltpu.HBM)
_SEM = pl.BlockSpec(memory_space=pltpu.SEMAPHORE)
_EFFECT = pltpu.SideEffectType.DATAFLOW_SIDE_EFFECTING
_TOKEN = jax.ShapeDtypeStruct((8, 128), F32)


def _peer(rel):
    pos = (lax.axis_index("x"), lax.axis_index("y"), lax.axis_index("c"))
    return tuple(1 - p if (rel >> (2 - i)) & 1 else p for i, p in enumerate(pos))


def _index_of(dev):
    return 4 * dev[0] + 2 * dev[1] + dev[2]


def _split_copies(src_ref, land_ref, sems, plan):
    n = len(plan)
    return [pltpu.make_async_remote_copy(
        src_ref=src_ref if s is None else src_ref.at[s], dst_ref=land_ref.at[d], send_sem=sems[k],
        recv_sem=sems[n + k], device_id=peer, device_id_type=_MESH) for k, (s, d, peer) in enumerate(plan)]


def _split_start(src, n_land, plan_fn, after, name):
    blk = src.shape[-2:]
    land = lax.empty((n_land,) + blk, src.dtype)
    n = len(plan_fn())

    def body(src_ref, land_ref, after_ref, *outs):
        for cp in _split_copies(src_ref, land_ref, outs[:2 * n], plan_fn()):
            cp.start()
        outs[2 * n + 2][...] = jnp.zeros_like(outs[2 * n + 2])

    res = pl.pallas_call(
        body, name=name,
        out_shape=(pltpu.SemaphoreType.DMA(()),) * (2 * n)
        + (pltpu.HBM(src.shape, src.dtype), pltpu.HBM(land.shape, land.dtype), _TOKEN),
        in_specs=(_HBM, _HBM, _ANY),
        out_specs=(_SEM,) * (2 * n) + (_HBM, _HBM, pl.BlockSpec(memory_space=pltpu.VMEM)),
        input_output_aliases={0: 2 * n, 1: 2 * n + 1},
        compiler_params=pltpu.CompilerParams(has_side_effects=_EFFECT),
    )(pltpu.with_memory_space_constraint(src, pltpu.HBM), pltpu.with_memory_space_constraint(land, pltpu.HBM), after)
    return res[:2 * n], res[2 * n], res[2 * n + 1], res[2 * n + 2]


def _split_wait(sems, src, land, plan_fn, after, name):
    n = len(sems) // 2

    def body(src_ref, land_ref, *rest):
        for cp in _split_copies(src_ref, land_ref, rest[:2 * n], plan_fn()):
            cp.wait_send()
            cp.wait_recv()

    return pl.pallas_call(
        body, name=name,
        out_shape=(pltpu.HBM(src.shape, src.dtype), pltpu.HBM(land.shape, land.dtype)),
        in_specs=(_HBM, _HBM) + (_SEM,) * (2 * n) + (_ANY,), out_specs=(_HBM, _HBM),
        input_output_aliases={0: 0, 1: 1},
        compiler_params=pltpu.CompilerParams(has_side_effects=_EFFECT),
    )(src, land, *sems, after)


def _gather_plan():
    me = _index_of(_peer(0))
    return [(None, me, _peer(rel)) for rel in range(1, N_DEV)]


def _gather_wait_plan():
    return [(None, _index_of(_peer(rel)), _peer(rel)) for rel in range(1, N_DEV)]


def _chip_plan():
    return [(_index_of(_peer(rel)) // 2, j, _peer(rel)) for j, rel in enumerate((4, 2, 6))]


def _owner_plan():
    return [(_index_of(_peer(rel)), rel - 1, _peer(rel)) for rel in range(1, N_DEV)]


def _pair_sum(g, t1, my_c, name, tr):
    _, r, c = g.shape

    def body(c_ref, g_ref, t_ref, o_ref, ob_ref):
        s = g_ref[...] + t_ref[...]
        o_ref[...] = s
        ob_ref[...] = s.astype(BF16)

    blk = pl.BlockSpec((None, tr, c), lambda k, i, cr: (k, i, 0))
    return pl.pallas_call(
        body,
        grid_spec=pltpu.PrefetchScalarGridSpec(
            num_scalar_prefetch=1, grid=(4, r // tr),
            in_specs=[pl.BlockSpec((None, tr, c), lambda k, i, cr: (2 * k + cr[0], i, 0)), blk],
            out_specs=[blk, blk]),
        out_shape=[jax.ShapeDtypeStruct((4, r, c), F32), jax.ShapeDtypeStruct((4, r, c), BF16)],
        compiler_params=_params(("parallel", "parallel")), name=name,
    )(my_c, g, t1)


def _adam_math(g, w, m, v):
    m = ADAM_B1 * m + (1.0 - ADAM_B1) * g
    v = ADAM_B2 * v + (1.0 - ADAM_B2) * (g * g)
    m_hat = m / (1.0 - ADAM_B1 ** ADAM_STEP)
    v_hat = v / (1.0 - ADAM_B2 ** ADAM_STEP)
    delta = -ADAM_LR * (m_hat / (jnp.sqrt(v_hat) + ADAM_EPS) + ADAM_WD * w)
    return delta, m, v


def _grad_sum(own, own_index, recv, name, tr):
    _, r, c = own.shape
    n = recv.shape[0]

    def body(k_ref, own_ref, *rest):
        g = own_ref[...]
        for recv_ref in rest[:n]:
            g = g + recv_ref[...].astype(F32)
        rest[n][...] = g

    def slot(j):
        return pl.BlockSpec((None, tr, c), lambda i, kr: (j, i, 0))

    return pl.pallas_call(
        body,
        grid_spec=pltpu.PrefetchScalarGridSpec(
            num_scalar_prefetch=1, grid=(r // tr,),
            in_specs=[pl.BlockSpec((None, tr, c), lambda i, kr: (kr[0], i, 0))] + [slot(j) for j in range(n)],
            out_specs=pl.BlockSpec((tr, c), lambda i, kr: (i, 0))),
        out_shape=jax.ShapeDtypeStruct((r, c), F32),
        compiler_params=_params(("parallel",)), name=name,
    )(own_index, own, *([recv] * n))


def _adam_many(g, w, m, v, row_tiles, name):
    n = len(g)

    def body(*refs):
        ins, outs = refs[:4 * n], refs[4 * n:]
        for i in range(n):
            res = _adam_math(ins[i][...], ins[n + i][...], ins[2 * n + i][...], ins[3 * n + i][...])
            for kind in range(3):
                outs[kind * n + i][...] = res[kind]

    def spec(a):
        blk = (a.shape[0] // row_tiles,) + a.shape[1:]
        return pl.BlockSpec(blk, lambda t, nd=a.ndim: (t,) + (0,) * (nd - 1))

    specs = [spec(a) for a in g]
    res = pl.pallas_call(
        body, grid=(row_tiles,), in_specs=specs * 4, out_specs=specs * 3,
        out_shape=[jax.ShapeDtypeStruct(a.shape, F32) for a in g] * 3,
        compiler_params=_params(("parallel",)), name=name,
    )(*g, *w, *m, *v)
    return res[:n], res[n:2 * n], res[2 * n:]


def _sum8(g8, name):
    _, r, c = g8.shape

    def body(g_ref, o_ref):
        acc = g_ref[0]
        for j in range(1, N_DEV):
            acc = acc + g_ref[j]
        o_ref[...] = acc

    return pl.pallas_call(
        body, grid=(1,), in_specs=[pl.BlockSpec((N_DEV, r, c), lambda i: (0, 0, 0))],
        out_specs=pl.BlockSpec((r, c), lambda i: (0, 0)), out_shape=jax.ShapeDtypeStruct((r, c), F32),
        compiler_params=_params(("arbitrary",)), name=name,
    )(g8)


def _pack(arrs, pad_rows=8):
    flat = jnp.concatenate([a.reshape(-1) for a in arrs])
    n = flat.shape[0]
    q = PACK_C * pad_rows
    tot = -(-n // q) * q
    if tot != n:
        flat = jnp.concatenate([flat, jnp.zeros((tot - n,), flat.dtype)])
    return flat.reshape(tot // PACK_C, PACK_C)


def _unpack(buf, shapes):
    flat = buf.reshape(-1)
    out, off = [], 0
    for s in shapes:
        n = int(np.prod(s))
        out.append(flat[off:off + n].reshape(s))
        off += n
    return out


GROUPS = (("w_in",),
          ("w_glu", "w_ssm_br", "w_mem_br", "w_attn_br"),
          ("w_up", "w_down"),
          ("w_mem_kv", "w_o"))
GROUP_TR = (400, 384, 512, 256)
MLP_GROUP = 2
MIXER_GROUPS = (1, 3)
ATTN_BR_FOLD = 2


def _stored_shape(name):
    r, c, ax = BIG_SHAPE[name]
    rows, cols = (r // N_DEV, c) if ax == 0 else (c // N_DEV, r)
    return (rows // ATTN_BR_FOLD, cols * ATTN_BR_FOLD) if name == "w_attn_br" else (rows, cols)


def _stored(shard, name):
    a = shard[0].T if BIG_SHAPE[name][2] == 1 else shard[0]
    return a.reshape(_stored_shape(name))


def _unstored(a, name):
    r, c, ax = BIG_SHAPE[name]
    if ax == 0:
        return a.reshape(1, r // N_DEV, c)
    return a.reshape(c // N_DEV, r).T[None]


def _pack_group(d, names):
    return jnp.concatenate([_stored(d[n], n) for n in names], axis=0)


def _split_group(buf, names):
    out, off = {}, 0
    for n in names:
        rows = _stored_shape(n)[0]
        out[n] = buf[..., off:off + rows, :]
        off += rows
    return out


def _full_stored(stacked, name):
    r, c, ax = BIG_SHAPE[name]
    return stacked.reshape((r, c) if ax == 0 else (c, r))


def _stacked_stored(full, name):
    return full.reshape((N_DEV,) + _stored_shape(name))


def _gelu_parts(x):
    c0, c1 = math.sqrt(2.0 / math.pi), 0.044715
    th = jnp.tanh(c0 * (x + c1 * x * x * x))
    return th, c0, c1


def _local_step(x, mem, tgt, wb, sp, late_weights, grads_ready, small_grads_ready):
    l = x.shape[0]
    w_a, w_g = wb["w_in"][:ZA_W], wb["w_in"][ZA_W:]

    a_re, a_im, bb_re, bb_im = _discretize(sp["ssm_lambda_re"], sp["ssm_lambda_im"], sp["ssm_log_dt"],
                                           sp["ssm_b_re"], sp["ssm_b_im"])
    a_pair = jnp.stack([a_re.reshape(1, SSM_S), a_im.reshape(1, SSM_S)])
    a_conj = jnp.stack([a_re.reshape(1, SSM_S), -a_im.reshape(1, SSM_S)])
    b_re_t, b_im_t = _bd_in(bb_re).astype(BF16), _bd_in(bb_im).astype(BF16)
    c_re_t = _bd_in(sp["ssm_c_re"].transpose(0, 2, 1)).astype(BF16)
    c_im_t = (-_bd_in(sp["ssm_c_im"].transpose(0, 2, 1))).astype(BF16)
    d_row = sp["ssm_d"].reshape(1, SSM_W)

    n1 = _rms_fwd(x, sp["norm1_g"], "rms1")
    za = _mm(n1, w_a, [BF16], tb=True, name="in_proj_a", tn=1664)
    zg = _mm(n1, w_g, [BF16], tb=True, name="in_proj_g")
    for gi in MIXER_GROUPS:
        wb = {**wb, **late_weights(gi, za)}
    u = za[:, :SSM_W]
    mq = za[:, ZA_W - MEM_W:]

    u_s = _scan_order(u)
    s_all = _ssm_scan(u_s, b_re_t, b_im_t, a_pair, reverse=False, name="ssm_scan_fwd")
    ys = _time_order(_mm(s_all, _tiled(c_re_t, c_im_t), [F32], tb=True, bd=SSM_BD, tm=2048, name="ssm_cs"))

    def gelu_fn(r, b):
        y0 = r[0] + b[0] * r[1].astype(F32)
        th, _, _ = _gelu_parts(y0)
        return [y0, 0.5 * y0 * (1.0 + th)], []
    y0, y1 = _ew(gelu_fn, [ys, u], [d_row], [(SSM_W, F32), (SSM_W, BF16)], [], name="ssm_gelu", tm=2048)

    def glu_epi(acc, y1t, bg):
        t = acc + bg
        return t, y1t.astype(F32) * _sigmoid(t)
    t_glu, y2 = _mm(y1, wb["w_glu"], [F32, BF16], epi=glu_epi, mn=[y1], rows=[sp["b_glu"]], name="ssm_glu")
    br_ssm = _mm(y2, wb["w_ssm_br"], [BF16], tb=True, name="ssm_br")

    qkv_p, o_g, lse_g = [], [], []
    for g, d in enumerate(DILATIONS):
        nb = l // d // ATT_WIN
        cols = [za[:, SSM_W + (3 * j + g) * ATT_GW: SSM_W + (3 * j + g + 1) * ATT_GW] for j in range(3)]
        qp, kp, vp = [_to_perm(cc, d) for cc in cols]
        qkv_p.append((qp, kp, vp))
        og, lg = _attn_fwd(qp, kp, vp, nb, "attn_fwd%d" % g)
        o_g.append(_from_perm(og, d))
        lse_g.append(_from_perm(lg, d))

    def merge_fn(r, b):
        o0, o1, o2, l0, l1, l2 = r
        mx = jnp.maximum(jnp.maximum(l0, l1), l2)
        e0, e1, e2 = jnp.exp(l0 - mx), jnp.exp(l1 - mx), jnp.exp(l2 - mx)
        tot = e0 + e1 + e2
        return [(e0 * o0 + e1 * o1 + e2 * o2) / tot, mx + jnp.log(tot)], []
    o_att, lse_tot = _ew(merge_fn, o_g + lse_g, [], [(ATT_GW, F32), (ATT_GW, F32)], [], name="attn_merge", tm=2048)
    br_attn = _mm(o_att, wb["w_attn_br"], [BF16], tb=True, name="attn_br")

    mn = _rms_fwd(mem, sp["mem_norm_g"], "rms_mem")
    kv = _mm(mn, wb["w_mem_kv"], [BF16], name="mem_kv")
    mo = _mem_fwd(mq, kv, "mem_attn_fwd")
    br_mem = _mm(mo, wb["w_mem_br"], [BF16], tb=True, name="mem_br")

    merged, h1, n2 = _gated_out_proj(zg, [br_ssm, br_attn, br_mem], sp["b_gate"], wb["w_o"], x, sp["norm2_g"],
                                     "gated_o_proj")

    def up_epi(acc):
        ra = jnp.maximum(acc, 0.0)
        return ra * ra, ra
    wm = late_weights(MLP_GROUP, n2)
    f_act, r_act = _mm(n2, wm["w_up"], [BF16, BF16], tb=True, epi=up_epi, name="mlp_up")
    def down_epi(acc, ht, tv, gf):
        hv = acc + ht
        rs = lax.rsqrt(jnp.mean(hv * hv, axis=-1, keepdims=True) + RMS_EPS)
        err = hv * rs * gf - tv
        dh, dgf = _rms_bwd_tile(hv, err * (1.0 / D_MODEL), gf)
        return dh, dgf, _colsum(err * err) * (0.5 / D_MODEL)
    dh2, d_final_g, loss_cols = _mm(f_act, wm["w_down"], [F32], epi=down_epi, mn=[h1, tgt], rows=[sp["final_g"]],
                                    n_sums=2, tk=1024, name="mlp_down")
    loss = jnp.sum(loss_cols, axis=1, keepdims=True)

    gw, gs = {}, {"final_g": d_final_g}
    d_act = _mm(dh2, wm["w_down"], [BF16], tb=True, epi=lambda acc, ra: (acc * 2.0 * ra.astype(F32),), mn=[r_act],
                name="mlp_down_dx")
    dw_down = _mm(f_act, dh2, [F32], ta=True, name="mlp_down_dw")
    dw_up = _mm(d_act, n2, [F32], ta=True, name="mlp_up_dw")
    token = grads_ready(MLP_GROUP, {"w_up": dw_up, "w_down": dw_down})
    def up_dx_epi(acc, ht, dht, g2):
        dx, dg = _rms_bwd_tile(ht, acc, g2)
        return dx + dht, dg
    dh1, gs["norm2_g"] = _mm(d_act, wm["w_up"], [F32], epi=up_dx_epi, mn=[h1, dh2],
                             rows=[sp["norm2_g"] + token[:1, :1]], n_sums=1, tk=1024, name="mlp_up_dx")
    gw["w_o"] = _mm(merged, dh1, [F32], ta=True, name="o_proj_dw")

    def gate_bwd_epi(dm, *tiles):
        dbr, dz = [], []
        for zt, bt, bias in zip(tiles[0:3], tiles[3:6], tiles[6:9]):
            gt = _sigmoid(zt.astype(F32) + bias)
            dbr.append(dm * gt)
            dz.append(dm * bt.astype(F32) * gt * (1.0 - gt))
        return (*dbr, *dz, *[_colsum(t) for t in dz])
    gate_bias = [sp["b_gate"][:, i * D_MODEL:(i + 1) * D_MODEL] for i in range(3)]
    res = _mm(dh1, wb["w_o"], [BF16] * 6, tb=True, epi=gate_bwd_epi, mn=[(zg, 0), (zg, 1), (zg, 2), br_ssm, br_attn, br_mem],
              rows=gate_bias, n_sums=3, tm=512, name="o_proj_dx")
    (dbr_ssm, dbr_attn, dbr_mem), dzg = res[0:3], res[3:6]
    gs["b_gate"] = jnp.concatenate(res[6:9], axis=1)

    gw["w_ssm_br"] = _mm(dbr_ssm, y2, [F32], ta=True, name="ssm_br_dw")
    def glu_bwd_epi(dy, y1t, tt):
        sg = _sigmoid(tt)
        dt = dy * y1t.astype(F32) * sg * (1.0 - sg)
        return dt, dy * sg, _colsum(dt)
    dt_glu, dy1a, gs["b_glu"] = _mm(dbr_ssm, wb["w_ssm_br"], [BF16, F32], epi=glu_bwd_epi, mn=[y1, t_glu], n_sums=1,
                                    name="ssm_br_dx")
    gw["w_glu"] = _mm(y1, dt_glu, [F32], ta=True, name="ssm_glu_dw")

    def gelu_bwd_epi(acc, dy1t, y0t, ut):
        th, c0, c1 = _gelu_parts(y0t)
        dg = 0.5 * (1.0 + th) + 0.5 * y0t * (1.0 - th * th) * c0 * (1.0 + 3.0 * c1 * y0t * y0t)
        dy = (acc + dy1t) * dg
        return dy, _colsum(dy * ut.astype(F32))
    dy0, gs["ssm_d"] = _mm(dt_glu, wb["w_glu"], [F32], tb=True, epi=gelu_bwd_epi, mn=[dy1a, y0, u], n_sums=1,
                           name="ssm_glu_dx")
    dy0_s = _scan_order(dy0)
    lam, da, d_b, d_c = _ssm_scan(dy0_s, c_re_t, c_im_t, a_conj, reverse=True, s_fwd=s_all, u=u_s,
                                  name="ssm_scan_bwd")
    du = _time_order(_mm(lam, _tiled(b_re_t, b_im_t), [BF16], tb=True,
                         epi=lambda acc, dyt, dr: (acc + dyt * dr,), mn=[dy0_s], rows=[d_row], bd=SSM_BD, tm=2048, name="ssm_bu_dx"))
    gs["a_re"], gs["a_im"] = da[0], da[1]
    (dbr, dbi), (dcr, dci) = _untiled(d_b), _untiled(d_c)
    gs["bb_re"], gs["bb_im"] = _bd_diag(dbr).transpose(0, 2, 1), _bd_diag(dbi).transpose(0, 2, 1)
    gs["ssm_c_re"], gs["ssm_c_im"] = _bd_diag(dcr), -_bd_diag(dci)

    gw["w_attn_br"] = _mm(dbr_attn, o_att, [F32], ta=True, name="attn_br_dw")

    def do_epi(acc, ot):
        prod = acc * ot
        head = lax.broadcasted_iota(jnp.int32, prod.shape, 1) // ATT_E
        dd = jnp.zeros_like(prod)
        for h in range(ATT_HG):
            dd = jnp.where(head == h, jnp.sum(jnp.where(head == h, prod, 0.0), axis=1, keepdims=True), dd)
        return acc, dd
    do_att, dd_att = _mm(dbr_attn, wb["w_attn_br"], [BF16, F32], epi=do_epi, mn=[o_att], name="attn_br_dx")
    dq_l, dk_l, dv_l = [], [], []
    for g, d in enumerate(DILATIONS):
        nb = l // d // ATT_WIN
        qp, kp, vp = qkv_p[g]
        dq, dk, dv = _attn_bwd(qp, kp, vp, _to_perm(do_att, d), _to_perm(lse_tot, d), _to_perm(dd_att, d),
                               nb, "attn_bwd%d" % g)
        dq_l.append(_from_perm(dq, d))
        dk_l.append(_from_perm(dk, d))
        dv_l.append(_from_perm(dv, d))

    gw["w_mem_br"] = _mm(dbr_mem, mo, [F32], ta=True, name="mem_br_dw")
    dmo = _mm(dbr_mem, wb["w_mem_br"], [BF16], name="mem_br_dx")
    dmq, dkv = _mem_bwd(mq, kv, dmo, "mem_attn_bwd")
    gw["w_mem_kv"] = _mm(mn, dkv, [F32], ta=True, name="mem_kv_dw")
    dmn = _mm(dkv, wb["w_mem_kv"], [F32], tb=True, name="mem_kv_dx")
    token = sum(grads_ready(gi, gw) for gi in MIXER_GROUPS)
    gs["mem_norm_g"] = _rms_bwd(mem, dmn, None, sp["mem_norm_g"] + token[:1, :1], "rms_mem_bwd")[1]

    dza = jnp.concatenate([du] + dq_l + dk_l + dv_l + [dmq], axis=1)
    dn_a = _mm(dza, w_a, [F32], name="in_proj_a_dx", tk=1664)
    dw_a = _mm(dza, n1, [F32], ta=True, name="in_proj_a_dw", tm=1664)
    dw_g = [_mm(dzg[i], n1, [F32], ta=True, name="in_proj_g_dw%d" % i) for i in range(3)]
    gw["w_in"] = jnp.concatenate([dw_a] + dw_g, axis=0)
    token = grads_ready(0, gw) + small_grads_ready(gs)
    def in_dx_epi(acc, pt, xt, dht, g1):
        dx, dg = _rms_bwd_tile(xt, acc + pt, g1)
        return dx + dht, dg
    w_gs = [w_g[i * D_MODEL:(i + 1) * D_MODEL] for i in range(3)]
    grad_x, gs["norm1_g"] = _mm(dzg[0], w_gs[0], [F32], pair2=(dzg[1], w_gs[1], dzg[2], w_gs[2]), epi=in_dx_epi,
                                mn=[dn_a, x, dh1],
                                rows=[sp["norm1_g"] + token[:1, :1]], n_sums=1, tm=512, name="in_proj_g_dx")
    return loss, grad_x, gs


_SMALL_GRAD_ORDER = ("norm1_g", "mem_norm_g", "b_gate", "a_re", "a_im", "bb_re", "bb_im", "ssm_c_re", "ssm_c_im",
                     "ssm_d", "b_glu", "norm2_g", "final_g")


def kernel(x, mem, norm1_g, mem_norm_g, w_in, b_gate, ssm_lambda_re, ssm_lambda_im, ssm_log_dt, ssm_b_re, ssm_b_im, ssm_c_re, ssm_c_im, ssm_d, w_glu, b_glu, w_ssm_br, w_attn_br, w_mem_kv, w_mem_br, w_o, norm2_g, w_up, w_down, final_g, loss_target, m_norm1_g, m_mem_norm_g, m_w_in, m_b_gate, m_ssm_lambda_re, m_ssm_lambda_im, m_ssm_log_dt, m_ssm_b_re, m_ssm_b_im, m_ssm_c_re, m_ssm_c_im, m_ssm_d, m_w_glu, m_b_glu, m_w_ssm_br, m_w_attn_br, m_w_mem_kv, m_w_mem_br, m_w_o, m_norm2_g, m_w_up, m_w_down, m_final_g, v_norm1_g, v_mem_norm_g, v_w_in, v_b_gate, v_ssm_lambda_re, v_ssm_lambda_im, v_ssm_log_dt, v_ssm_b_re, v_ssm_b_im, v_ssm_c_re, v_ssm_c_im, v_ssm_d, v_w_glu, v_b_glu, v_w_ssm_br, v_w_attn_br, v_w_mem_kv, v_w_mem_br, v_w_o, v_norm2_g, v_w_up, v_w_down, v_final_g):
    args = dict(locals())
    w = {n: args[n] for n in ALL_W}
    m = {n: args["m_" + n] for n in ALL_W}
    v = {n: args["v_" + n] for n in ALL_W}
    my_c = lax.axis_index("c").astype(jnp.int32).reshape(1)
    my_chip = (2 * lax.axis_index("x") + lax.axis_index("y")).astype(jnp.int32).reshape(1)

    w_pack = [_pack_group(w, names) for names in GROUPS]
    my_index = (4 * lax.axis_index("x") + 2 * lax.axis_index("y") + lax.axis_index("c")).astype(jnp.int32)
    zero = jnp.zeros((), jnp.int32)
    w_all = _allgather(w_pack[0].astype(BF16), "allgather_weights0")
    wb = {n: _full_stored(part, n) for n, part in _split_group(w_all, GROUPS[0]).items()}
    gathers = {gi: _split_start(w_pack[gi].astype(BF16), N_DEV, _gather_plan, w_all, "weights_gather_start%d" % gi)
               for gi in range(1, len(GROUPS))}

    def gathered(started, after, name):
        sems, src, land, _ = started
        src, land = _split_wait(sems, src, land, _gather_wait_plan, after, name)
        return lax.dynamic_update_slice(land, src[None], (my_index, zero, zero))

    def late_weights(gi, after):
        full = gathered(gathers[gi], after, "weights_gather_wait%d" % gi)
        return {n: _full_stored(part, n) for n, part in _split_group(full, GROUPS[gi]).items()}

    pending = {}

    def grads_ready(gi, grads):
        g_pack = jnp.concatenate([_stacked_stored(grads[n], n) for n in GROUPS[gi]], axis=1)
        if gi == 0:
            t1 = _pair_exchange(g_pack, "grad_pair_exchange%d" % gi)
            p_sum, p_bf = _pair_sum(g_pack, t1, my_c, "grad_pair_sum%d" % gi, GROUP_TR[gi])
            started = _split_start(p_bf, 3, _chip_plan, p_sum, "grad_chip_exchange_start%d" % gi)
            pending[gi] = (p_sum, my_chip, started, _chip_plan)
        else:
            started = _split_start(g_pack.astype(BF16), N_DEV - 1, _owner_plan, g_pack, "grad_exchange_start%d" % gi)
            pending[gi] = (g_pack, my_index.reshape(1), started, _owner_plan)
        return started[3]

    early_small = [n for n in _SMALL_GRAD_ORDER if n != "norm1_g"]
    small_started = []

    def small_grads_ready(gs):
        started = _split_start(_pack([gs[n] for n in early_small]), N_DEV, _gather_plan, gs["mem_norm_g"],
                               "small_grads_gather_start")
        small_started.append((started, [gs[n].shape for n in early_small]))
        return started[3]

    sp = {
        "norm1_g": norm1_g + sum(started[3][:1, :1] for started in gathers.values()), "mem_norm_g": mem_norm_g, "b_gate": b_gate, "b_glu": b_glu, "norm2_g": norm2_g,
        "final_g": final_g.reshape(1, D_MODEL),
        "ssm_lambda_re": ssm_lambda_re[0], "ssm_lambda_im": ssm_lambda_im[0], "ssm_log_dt": ssm_log_dt[0],
        "ssm_b_re": ssm_b_re[0], "ssm_b_im": ssm_b_im[0], "ssm_c_re": ssm_c_re[0], "ssm_c_im": ssm_c_im[0],
        "ssm_d": ssm_d[0],
    }
    loss, grad_x, gs = _local_step(x[0], mem[0], loss_target[0], wb, sp, late_weights, grads_ready,
                                     small_grads_ready)
    loss = lax.psum(loss[0, 0], ("x", "y", "c"))
    n1_started = _split_start(_pack([gs["norm1_g"]]), N_DEV, _gather_plan, grad_x, "norm1_grad_gather_start")

    big_g = {}
    for gi, names in enumerate(GROUPS):
        own, own_index, (sems, src, land, _), plan = pending[gi]
        recv = _split_wait(sems, src, land, plan, grad_x, "grad_exchange_wait%d" % gi)[1]
        g_pack = _grad_sum(own, own_index, recv, "grad_sum%d" % gi, GROUP_TR[gi])
        for n, part in _split_group(g_pack, names).items():
            big_g[n] = _unstored(part, n)
    rows_of = lambda d, names: [d[n].reshape(d[n].shape[-2:]) for n in names]
    big_out = _adam_many(rows_of(big_g, BIG), rows_of(w, BIG), rows_of(m, BIG), rows_of(v, BIG), 8, "adam_big")
    big = [big_g] + [{n: a[None] for n, a in zip(BIG, outs)} for outs in big_out]

    (sg_started, sg_shapes), = small_started
    sg_all = jnp.concatenate([gathered(sg_started, big_out[0][0], "small_grads_gather_wait"),
                              gathered(n1_started, big_out[0][0], "norm1_grad_gather_wait")], axis=1)
    sg_sum = _sum8(sg_all, "sum_small_grads")
    n1_rows = n1_started[1].shape[0]
    sg = dict(zip(early_small, _unpack(sg_sum[:-n1_rows], sg_shapes)))
    sg["norm1_g"] = _unpack(sg_sum[-n1_rows:], [gs["norm1_g"].shape])[0]
    _, disc_vjp = jax.vjp(_discretize, sp["ssm_lambda_re"], sp["ssm_lambda_im"], sp["ssm_log_dt"],
                          sp["ssm_b_re"], sp["ssm_b_im"])
    d_lre, d_lim, d_ldt, d_bre, d_bim = disc_vjp((sg["a_re"].reshape(SSM_G, SSM_P), sg["a_im"].reshape(SSM_G, SSM_P),
                                                  sg["bb_re"], sg["bb_im"]))
    small_grad = {
        "norm1_g": sg["norm1_g"], "mem_norm_g": sg["mem_norm_g"], "b_gate": sg["b_gate"],
        "ssm_lambda_re": d_lre, "ssm_lambda_im": d_lim, "ssm_log_dt": d_ldt, "ssm_b_re": d_bre, "ssm_b_im": d_bim,
        "ssm_c_re": sg["ssm_c_re"], "ssm_c_im": sg["ssm_c_im"], "ssm_d": sg["ssm_d"], "b_glu": sg["b_glu"],
        "norm2_g": sg["norm2_g"], "final_g": sg["final_g"],
    }
    small_grad = {n: small_grad[n].reshape(w[n].shape) for n in SMALL}

    def squeezed(a):
        return a.reshape(a.shape[1:]) if a.ndim > 2 else a.reshape(1, -1)

    sq = lambda d: [squeezed(d[n]) for n in SMALL]
    small_out = _adam_many(sq(small_grad), sq(w), sq(m), sq(v), 1, "adam_small")
    small = [small_grad] + [{n: a.reshape(w[n].shape) for n, a in zip(SMALL, outs)} for outs in small_out]

    outs = [loss, grad_x[None]]
    for kind in range(4):
        for n in ALL_W:
            outs.append(big[kind][n] if n in BIG else small[kind][n])
    return tuple(outs)
```

```python
import math

import numpy as np
import jax
import jax.numpy as jnp
from jax import lax
from jax.experimental import pallas as pl
from jax.experimental.pallas import tpu as pltpu

F32 = jnp.float32
BF16 = jnp.bfloat16
_MXU = jnp.bfloat16

D_MODEL = 1024
SSM_G, SSM_H, SSM_P = 32, 16, 64
SSM_W = SSM_G * SSM_H
SSM_S = SSM_G * SSM_P
SSM_BD = 4
ATT_E = 64
ATT_HG = 4
ATT_GW = ATT_HG * ATT_E
ATT_WIN = 128
ATT_QB = 8
ATT_QB_FWD = 4
DILATIONS = (1, 4, 16)
MEM_H, MEM_E = 4, 128
MEM_W = MEM_H * MEM_E
ZA_W = SSM_W + 9 * ATT_GW + MEM_W
ZG_W = 3 * D_MODEL
IN_W = ZA_W + ZG_W
RMS_EPS = 1e-6
NEG_INF = -1e30

ADAM_LR, ADAM_B1, ADAM_B2, ADAM_EPS, ADAM_WD, ADAM_STEP = 0.001, 0.9, 0.999, 1e-08, 0.01, 10

N_DEV = 8
PACK_C = 512
_VMEM_LIMIT = 56 * 1024 * 1024
SUBLANES = 16
SCAN_SEG = 128
SCAN_CHAINS = 4
SCAN_UNROLL = 4
SCAN_W = 128

BIG = ("w_in", "w_glu", "w_ssm_br", "w_attn_br", "w_mem_kv", "w_mem_br", "w_o", "w_up", "w_down")
BIG_SHAPE = {
    "w_in": (D_MODEL, IN_W, 1), "w_glu": (SSM_W, SSM_W, 0), "w_ssm_br": (SSM_W, D_MODEL, 1),
    "w_attn_br": (ATT_GW, D_MODEL, 1), "w_mem_kv": (D_MODEL, 2 * MEM_W, 0), "w_mem_br": (MEM_W, D_MODEL, 1),
    "w_o": (D_MODEL, D_MODEL, 0), "w_up": (D_MODEL, 4 * D_MODEL, 1), "w_down": (4 * D_MODEL, D_MODEL, 0),
}
SMALL = ("norm1_g", "mem_norm_g", "b_gate", "ssm_lambda_re", "ssm_lambda_im", "ssm_log_dt", "ssm_b_re",
         "ssm_b_im", "ssm_c_re", "ssm_c_im", "ssm_d", "b_glu", "norm2_g", "final_g")
ALL_W = ("norm1_g", "mem_norm_g", "w_in", "b_gate", "ssm_lambda_re", "ssm_lambda_im", "ssm_log_dt", "ssm_b_re",
         "ssm_b_im", "ssm_c_re", "ssm_c_im", "ssm_d", "w_glu", "b_glu", "w_ssm_br", "w_attn_br", "w_mem_kv",
         "w_mem_br", "w_o", "norm2_g", "w_up", "w_down", "final_g")


def _params(sem):
    return pltpu.CompilerParams(dimension_semantics=sem, vmem_limit_bytes=_VMEM_LIMIT)


def _pick(n, cap):
    if n <= cap:
        return n
    t = (cap // 128) * 128
    while n % t:
        t -= 128
    return t


def _mm(a, b, outs, *, name, ta=False, tb=False, epi=None, mn=(), rows=(), pair2=None, bd=0, n_sums=0,
        tm=1024, tn=1024, tk=2048):
    ab = [a, b] + (list(pair2) if pair2 is not None else [])
    a_shape, b_shape = ab[0].shape, ab[1].shape
    m = a_shape[1] if ta else a_shape[0]
    k = a_shape[0] if ta else a_shape[1]
    n = b_shape[0] if tb else b_shape[1]
    assert k == (b_shape[1] if tb else b_shape[0]), (name, a_shape, b_shape)
    out_n = n
    if bd and ta:
        assert not tb
        tm, tn, tk = m // bd, n // bd, _pick(k, tk)
        grid, out_n = (bd, 1, k // tk), tn
        a_blk = ((tk, tm), lambda i, j, kk: (kk, i))
        b_blk = ((tk, tn), lambda i, j, kk: (kk, i))
        mn_spec = pl.BlockSpec((tm, tn), lambda i, j, kk: (i, 0))
    elif bd:
        tm, tn, tk = _pick(m, tm), n // bd, k // bd
        grid = (m // tm, bd, 1)
        a_blk = ((tm, tk), lambda i, j, kk: (i, j))
        b_blk = ((tn, tk) if tb else (tk, tn), lambda i, j, kk: (j, j))
        mn_spec = pl.BlockSpec((tm, tn), lambda i, j, kk: (i, j))
    else:
        tm, tn, tk = _pick(m, tm), _pick(n, tn), _pick(k, tk)
        grid = (m // tm, n // tn, k // tk)
        a_blk = ((tk, tm), lambda i, j, kk: (kk, i)) if ta else ((tm, tk), lambda i, j, kk: (i, kk))
        b_blk = ((tn, tk), lambda i, j, kk: (j, kk)) if tb else ((tk, tn), lambda i, j, kk: (kk, j))
        mn_spec = pl.BlockSpec((tm, tn), lambda i, j, kk: (i, j))

    ab_specs = [pl.BlockSpec(*(a_blk if q % 2 == 0 else b_blk)) for q in range(len(ab))]
    mn_arrays = [e[0] if isinstance(e, tuple) else e for e in mn]
    mn_specs = [pl.BlockSpec((tm, tn), lambda i, j, kk, c=e[1]: (i, c)) if isinstance(e, tuple) else mn_spec
                for e in mn]
    nk = grid[2]
    row_spec = pl.BlockSpec((1, tn), lambda i, j, kk: (0, j))
    n_ex, n_out = len(mn) + len(rows), len(outs)
    assert n_sums == 0 or (grid[1] == 1 and not bd)
    dims = (((0 if ta else 1,), (1 if tb else 0,)), ((), ()))

    def body(*refs):
        ab_refs, rest = refs[:len(ab)], refs[len(ab):]
        ex, o_refs = rest[:n_ex], rest[n_ex:n_ex + n_out]
        s_refs = rest[n_ex + n_out:n_ex + n_out + n_sums]
        first_row_tile = pl.program_id(0) == 0
        kk = pl.program_id(2)

        pairs = list(zip(ab_refs[0::2], ab_refs[1::2]))

        def product(pair):
            return lax.dot_general(pair[0][...].astype(_MXU), pair[1][...].astype(_MXU), dims,
                                   preferred_element_type=F32)

        def finish(total):
            vals = (total,) if epi is None else epi(total, *[r[...] for r in ex])
            for r, v in zip(o_refs, vals):
                r[...] = v.astype(r.dtype)
            for r, v in zip(s_refs, vals[n_out:]):
                r[...] = jnp.where(first_row_tile, v, r[...] + v)

        if nk == 1:
            total = product(pairs[0])
            for pair in pairs[1:]:
                total = total + product(pair)
            finish(total)
        else:
            acc = rest[-1]

            @pl.when(kk == 0)
            def _():
                acc[...] = jnp.zeros_like(acc)

            for pair in pairs:
                acc[...] += product(pair)

            @pl.when(kk == nk - 1)
            def _():
                finish(acc[...])

    res = pl.pallas_call(
        body, grid=grid,
        in_specs=ab_specs + mn_specs + [row_spec] * len(rows),
        out_specs=[mn_spec] * n_out + [row_spec] * n_sums,
        out_shape=[jax.ShapeDtypeStruct((m, out_n), dt) for dt in outs]
        + [jax.ShapeDtypeStruct((1, out_n), F32)] * n_sums,
        scratch_shapes=[pltpu.VMEM((tm, tn), F32)] if nk > 1 else [],
        compiler_params=_params(("arbitrary" if n_sums else "parallel", "parallel", "arbitrary")), name=name,
    )(*ab, *mn_arrays, *rows)
    return res[0] if n_out + n_sums == 1 else res


def _ew(fn, rows, bcs, out_rows, out_accs, *, name, tm=256):
    r = rows[0].shape[0]
    tm = min(tm, r)
    assert r % tm == 0
    nr, nb, no, na = len(rows), len(bcs), len(out_rows), len(out_accs)

    def body(*refs):
        i = pl.program_id(0)
        r_in, b_in = refs[:nr], refs[nr:nr + nb]
        o_r, o_a = refs[nr + nb:nr + nb + no], refs[nr + nb + no:]
        outs, accs = fn([x[...] for x in r_in], [x[...] for x in b_in])
        for ref, v in zip(o_r, outs):
            ref[...] = v.astype(ref.dtype)
        if na:
            @pl.when(i == 0)
            def _():
                for ref in o_a:
                    ref[...] = jnp.zeros_like(ref)

            for ref, v in zip(o_a, accs):
                ref[...] += v

    res = pl.pallas_call(
        body, grid=(r // tm,),
        in_specs=[pl.BlockSpec((tm, x.shape[1]), lambda i: (i, 0)) for x in rows]
        + [pl.BlockSpec((1, x.shape[1]), lambda i: (0, 0)) for x in bcs],
        out_specs=[pl.BlockSpec((tm, c), lambda i: (i, 0)) for c, _ in out_rows]
        + [pl.BlockSpec((1, c), lambda i: (0, 0)) for c in out_accs],
        out_shape=[jax.ShapeDtypeStruct((r, c), dt) for c, dt in out_rows]
        + [jax.ShapeDtypeStruct((1, c), F32) for c in out_accs],
        compiler_params=_params(("arbitrary",)), name=name,
    )(*rows, *bcs)
    return res


def _colsum(x):
    return jnp.sum(x, axis=0, keepdims=True)


def _sigmoid(x):
    return 1.0 / (1.0 + jnp.exp(-x))


def _rms_bwd_tile(xv, dv, g):
    rs = lax.rsqrt(jnp.mean(xv * xv, axis=-1, keepdims=True) + RMS_EPS)
    gd = dv * g
    dx = rs * gd - xv * (rs * rs * rs) * jnp.mean(gd * xv, axis=-1, keepdims=True)
    return dx, _colsum(dv * xv * rs)


def _rms_fwd(x, g, name):
    def fn(r, b):
        xv = r[0]
        rs = lax.rsqrt(jnp.mean(xv * xv, axis=-1, keepdims=True) + RMS_EPS)
        return [xv * rs * b[0]], []
    return _ew(fn, [x], [g], [(x.shape[1], BF16)], [], name=name, tm=1024)[0]


def _rms_bwd(x, dn, res, g, name):
    def fn(r, b):
        dx, dg = _rms_bwd_tile(r[0], r[1], b[0])
        if res is not None:
            dx = dx + r[2]
        return [dx], [dg]
    rows = [x, dn] + ([res] if res is not None else [])
    return _ew(fn, rows, [g], [(x.shape[1], F32)], [x.shape[1]], name=name)


def _scan_order(x):
    l, c = x.shape
    return x.reshape(l // (SUBLANES * SCAN_SEG), SUBLANES, SCAN_SEG, c).transpose(0, 2, 1, 3).reshape(l, c)


def _time_order(x):
    l, c = x.shape
    return x.reshape(l // (SUBLANES * SCAN_SEG), SCAN_SEG, SUBLANES, c).transpose(0, 2, 1, 3).reshape(l, c)


def _ssm_scan(x, w_re, w_im, a_pair, *, reverse, s_fwd=None, u=None, name):
    l = x.shape[0]
    seg, w = SCAN_SEG, SCAN_W
    bd_w = SSM_W // SSM_BD
    tiles_per_bd = SSM_S // SSM_BD // w
    nch = min(SCAN_CHAINS, l // (SUBLANES * seg))
    chain_rows = SUBLANES * seg
    tb = nch * chain_rows
    nt = l // tb
    with_da = s_fwd is not None
    assert reverse or not with_da

    def tt(t):
        return nt - 1 - t if reverse else t

    def body(*refs):
        if with_da:
            (x_ref, wr_ref, wi_ref, a_ref, sf_ref, sp_ref, u_ref, s_ref, da_ref, dw_ref, dx_ref,
             p_ref, c_ref, b_scr, l_scr) = refs
        else:
            x_ref, wr_ref, wi_ref, a_ref, s_ref, p_ref, c_ref, b_scr, l_scr = refs
        t_blk = pl.program_id(1)
        ar, ai = a_ref[0], a_ref[1]

        @pl.when(t_blk == 0)
        def _():
            def pstep(i, carry):
                pr, pi = carry
                p_ref[0, pl.ds(i, 1), :] = pr
                p_ref[1, pl.ds(i, 1), :] = pi
                return pr * ar - pi * ai, pr * ai + pi * ar

            lax.fori_loop(0, seg, pstep, (ar, ai))
            c_ref[...] = jnp.zeros_like(c_ref)
            if with_da:
                da_ref[...] = jnp.zeros_like(da_ref)
                dw_ref[...] = jnp.zeros_like(dw_ref)
                dx_ref[...] = jnp.zeros_like(dx_ref)

        xb = x_ref[...].astype(_MXU)
        b_scr[:, :w] = jnp.dot(xb, wr_ref[...], preferred_element_type=F32)
        b_scr[:, w:] = jnp.dot(xb, wi_ref[...], preferred_element_type=F32)
        arb, aib = jnp.broadcast_to(ar, (SUBLANES, w)), jnp.broadcast_to(ai, (SUBLANES, w))
        zero = jnp.zeros((SUBLANES, w), F32)

        def tile(g, step):
            return pl.ds(pl.multiple_of(g * chain_rows + step * SUBLANES, SUBLANES), SUBLANES)

        def rows(g, i):
            return tile(g, seg - 1 - i if reverse else i)

        def local_step(i, carry):
            out = []
            for g in range(nch):
                sr, si = carry[2 * g], carry[2 * g + 1]
                idx = rows(g, i)
                sr, si = arb * sr - aib * si + b_scr[idx, :w], arb * si + aib * sr + b_scr[idx, w:]
                l_scr[idx, :w] = sr
                l_scr[idx, w:] = si
                out += [sr, si]
            return tuple(out)

        def unrolled(step_fn, first):
            def trip(q, carry):
                for r in range(SCAN_UNROLL):
                    carry = step_fn(first + q * SCAN_UNROLL + r, carry)
                return carry
            return trip

        ends = lax.fori_loop(0, seg // SCAN_UNROLL, unrolled(local_step, 0), (zero,) * (2 * nch))

        a_seg_r, a_seg_i = p_ref[0, seg - 1:seg, :], p_ref[1, seg - 1:seg, :]
        cr, ci = c_ref[0], c_ref[1]
        sub = lax.broadcasted_iota(jnp.int32, (SUBLANES, w), 0)
        ins = [[zero, zero] for _ in range(nch)]
        order = [(g, k) for g in range(nch) for k in range(SUBLANES)]
        for g, k in (order[::-1] if reverse else order):
            ins[g] = [jnp.where(sub == k, cr, ins[g][0]), jnp.where(sub == k, ci, ins[g][1])]
            er, ei = ends[2 * g][k:k + 1], ends[2 * g + 1][k:k + 1]
            cr, ci = er + a_seg_r * cr - a_seg_i * ci, ei + a_seg_r * ci + a_seg_i * cr
        c_ref[0] = cr
        c_ref[1] = ci

        def fix(g, i):
            idx = rows(g, i)
            pr, pi = p_ref[0, pl.ds(i, 1), :], p_ref[1, pl.ds(i, 1), :]
            sr = l_scr[idx, :w] + pr * ins[g][0] - pi * ins[g][1]
            si = l_scr[idx, w:] + pr * ins[g][1] + pi * ins[g][0]
            s_ref[idx, :w] = sr.astype(s_ref.dtype)
            s_ref[idx, w:] = si.astype(s_ref.dtype)
            return sr, si

        if not with_da:
            def fix_step(i, carry):
                for g in range(nch):
                    fix(g, i)
                return carry

            lax.fori_loop(0, seg // SCAN_UNROLL, unrolled(fix_step, 0), 0)
        else:
            def adj_step(i, acc):
                acc_r, acc_i = acc
                for g in range(nch):
                    lr, li = fix(g, i)
                    prev = tile(g, seg - 2 - i)
                    fr, fi = sf_ref[prev, :w].astype(F32), sf_ref[prev, w:].astype(F32)
                    acc_r, acc_i = acc_r + lr * fr + li * fi, acc_i + li * fr - lr * fi
                return acc_r, acc_i

            acc = lax.fori_loop(0, seg // SCAN_UNROLL - 1, unrolled(adj_step, 0), (zero, zero))
            for i in range(seg - SCAN_UNROLL, seg - 1):
                acc = adj_step(i, acc)
            acc_r, acc_i = acc
            first_block = tt(t_blk) == 0
            for g in range(nch):
                lr, li = fix(g, seg - 1)
                seg_ends = tile(g, seg - 1)
                if g == 0:
                    pvr = jnp.where(first_block, 0.0, sp_ref[SUBLANES - 1:SUBLANES, :w].astype(F32))
                    pvi = jnp.where(first_block, 0.0, sp_ref[SUBLANES - 1:SUBLANES, w:].astype(F32))
                else:
                    pvr = sf_ref[g * chain_rows - 1:g * chain_rows, :w].astype(F32)
                    pvi = sf_ref[g * chain_rows - 1:g * chain_rows, w:].astype(F32)
                fr = jnp.where(sub == 0, pvr, pltpu.roll(sf_ref[seg_ends, :w].astype(F32), 1, 0))
                fi = jnp.where(sub == 0, pvi, pltpu.roll(sf_ref[seg_ends, w:].astype(F32), 1, 0))
                acc_r = acc_r + lr * fr + li * fi
                acc_i = acc_i + li * fr - lr * fi
            da_ref[0] += jnp.sum(acc_r, axis=0, keepdims=True)
            da_ref[1] += jnp.sum(acc_i, axis=0, keepdims=True)
            dw_ref[...] += _tn_dot(u_ref[...], s_ref[...])
            dx_ref[...] += _tn_dot(xb, sf_ref[...])

    x_spec = pl.BlockSpec((tb, bd_w), lambda j, t: (tt(t), j // tiles_per_bd))
    w_spec = pl.BlockSpec((bd_w, w), lambda j, t: (j // tiles_per_bd, j))
    d_spec = pl.BlockSpec((bd_w, 2 * w), lambda j, t: (j // tiles_per_bd, j % tiles_per_bd))
    a_spec = pl.BlockSpec((2, 1, w), lambda j, t: (0, 0, j))
    s_spec = pl.BlockSpec((tb, 2 * w), lambda j, t: (tt(t), j))
    in_specs, args = [x_spec, w_spec, w_spec, a_spec], [x, w_re, w_im, a_pair]
    out_specs, out_shape = [s_spec], [jax.ShapeDtypeStruct((l, 2 * SSM_S), BF16)]
    scratch = [pltpu.VMEM((2, seg, w), F32), pltpu.VMEM((2, 1, w), F32)] + [pltpu.VMEM((tb, 2 * w), F32)] * 2
    if with_da:
        in_specs += [s_spec, pl.BlockSpec((SUBLANES, 2 * w),
                                          lambda j, t: (jnp.maximum(tt(t) * (tb // SUBLANES) - 1, 0), j)),
                     x_spec]
        args += [s_fwd, s_fwd, u]
        out_specs += [a_spec, d_spec, d_spec]
        out_shape += ([jax.ShapeDtypeStruct((2, 1, SSM_S), F32)]
                      + [jax.ShapeDtypeStruct((SSM_W, 2 * SSM_S // SSM_BD), F32)] * 2)
    res = pl.pallas_call(
        body, grid=(SSM_S // w, nt), in_specs=in_specs, out_specs=out_specs, out_shape=out_shape,
        scratch_shapes=scratch, compiler_params=_params(("parallel", "arbitrary")), name=name,
    )(*args)
    return res if with_da else res[0]


def _nt_dot(x, y):
    return lax.dot_general(x.astype(_MXU), y.astype(_MXU), (((1,), (1,)), ((), ())), preferred_element_type=F32)


def _tn_dot(x, y):
    return lax.dot_general(x.astype(_MXU), y.astype(_MXU), (((0,), (0,)), ((), ())), preferred_element_type=F32)


def _nn_dot(x, y):
    return jnp.dot(x.astype(_MXU), y.astype(_MXU), preferred_element_type=F32)


def _attn_mask2(gb, nb):
    qi = lax.broadcasted_iota(jnp.int32, (ATT_WIN, 2 * ATT_WIN), 0)
    c = lax.broadcasted_iota(jnp.int32, (ATT_WIN, 2 * ATT_WIN), 1)
    has_prev = (gb % nb) != 0
    prev_ok = jnp.logical_and(jnp.logical_and(c < ATT_WIN, c >= qi), has_prev)
    own_ok = jnp.logical_and(c >= ATT_WIN, c - ATT_WIN <= qi)
    return jnp.logical_or(prev_ok, own_ok)


def _attn_specs(qb):
    cur = pl.BlockSpec((qb * ATT_WIN, ATT_GW), lambda i: (i, 0))
    prev = pl.BlockSpec((ATT_WIN, ATT_GW), lambda i: (jnp.maximum(qb * i - 1, 0), 0))
    return cur, prev


def _attn_fwd(q, k, v, nb, name):
    l = q.shape[0]
    scale = ATT_E ** -0.5
    w = ATT_WIN

    qb = ATT_QB_FWD

    def body(q_ref, kc_ref, kp_ref, vc_ref, vp_ref, o_ref, lse_ref):
        i = pl.program_id(0)
        masks = [_attn_mask2(qb * i + b, nb) for b in range(qb)]
        for h in range(ATT_HG):
            sl = slice(h * ATT_E, (h + 1) * ATT_E)
            k_ext = jnp.concatenate([kp_ref[:, sl], kc_ref[:, sl]], axis=0)
            v_ext = jnp.concatenate([vp_ref[:, sl], vc_ref[:, sl]], axis=0)
            for b in range(qb):
                r, kr = slice(b * w, (b + 1) * w), slice(b * w, (b + 2) * w)
                s = jnp.where(masks[b], _nt_dot(q_ref[r, sl], k_ext[kr]) * scale, NEG_INF)
                mx = jnp.max(s, axis=-1, keepdims=True)
                p = jnp.exp(s - mx)
                den = jnp.sum(p, axis=-1, keepdims=True)
                o_ref[r, sl] = _nn_dot(p, v_ext[kr]) / den
                lse_ref[r, sl] = jnp.broadcast_to(mx + jnp.log(den), (w, ATT_E))

    cur, prev = _attn_specs(qb)
    return pl.pallas_call(
        body, grid=(l // (qb * w),), in_specs=[cur, cur, prev, cur, prev], out_specs=[cur, cur],
        out_shape=[jax.ShapeDtypeStruct((l, ATT_GW), F32)] * 2,
        compiler_params=_params(("parallel",)), name=name,
    )(q, k, k, v, v)


def _attn_bwd(q, k, v, do, lse, dd, nb, name):
    l = q.shape[0]
    scale = ATT_E ** -0.5
    w = ATT_WIN
    nblk = l // w

    def body(q_ref, kc_ref, kp_ref, vc_ref, vp_ref, do_ref, lse_ref, dd_ref, qn_ref, don_ref, lsen_ref, ddn_ref,
             dq_ref, dk_ref, dv_ref, dk_acc, dv_acc):
        i = pl.program_id(0)
        masks = [_attn_mask2(ATT_QB * i + b, nb) for b in range(ATT_QB)]
        nxt = ATT_QB * (i + 1)
        nxt_attends = jnp.logical_and(nxt < nblk, (nxt % nb) != 0)
        qi = lax.broadcasted_iota(jnp.int32, (w, w), 0)
        kj = lax.broadcasted_iota(jnp.int32, (w, w), 1)
        mask_n = jnp.logical_and(kj >= qi, nxt_attends)
        dk_acc[...] = jnp.zeros_like(dk_acc)
        dv_acc[...] = jnp.zeros_like(dv_acc)
        for h in range(ATT_HG):
            sl, col = slice(h * ATT_E, (h + 1) * ATT_E), slice(h * ATT_E, h * ATT_E + 1)
            k_ext = jnp.concatenate([kp_ref[:, sl], kc_ref[:, sl]], axis=0)
            v_ext = jnp.concatenate([vp_ref[:, sl], vc_ref[:, sl]], axis=0)
            for b in range(ATT_QB):
                r, kr = slice(b * w, (b + 1) * w), slice(b * w, (b + 2) * w)
                qh, doh, k2, v2 = q_ref[r, sl], do_ref[r, sl], k_ext[kr], v_ext[kr]
                p = jnp.where(masks[b], jnp.exp(_nt_dot(qh, k2) * scale - lse_ref[r, col]), 0.0)
                ds = p * (_nt_dot(doh, v2) - dd_ref[r, col]) * scale
                dq_ref[r, sl] = _nn_dot(ds, k2).astype(dq_ref.dtype)
                dk2, dv2 = _tn_dot(ds, qh), _tn_dot(p, doh)
                dk_acc[r, sl] += dk2[w:]
                dv_acc[r, sl] += dv2[w:]
                if b > 0:
                    rp = slice((b - 1) * w, b * w)
                    dk_acc[rp, sl] += dk2[:w]
                    dv_acc[rp, sl] += dv2[:w]
            last = slice((ATT_QB - 1) * w, ATT_QB * w)
            kl, vl, qn, don = kc_ref[last, sl], vc_ref[last, sl], qn_ref[:, sl], don_ref[:, sl]
            pn = jnp.where(mask_n, jnp.exp(_nt_dot(qn, kl) * scale - lsen_ref[:, col]), 0.0)
            dsn = pn * (_nt_dot(don, vl) - ddn_ref[:, col]) * scale
            dk_acc[last, sl] += _tn_dot(dsn, qn)
            dv_acc[last, sl] += _tn_dot(pn, don)
        dk_ref[...] = dk_acc[...].astype(dk_ref.dtype)
        dv_ref[...] = dv_acc[...].astype(dv_ref.dtype)

    cur, prev = _attn_specs(ATT_QB)
    nxt_spec = pl.BlockSpec((w, ATT_GW), lambda i: (jnp.minimum(ATT_QB * (i + 1), nblk - 1), 0))
    return pl.pallas_call(
        body, grid=(l // (ATT_QB * w),),
        in_specs=[cur, cur, prev, cur, prev, cur, cur, cur, nxt_spec, nxt_spec, nxt_spec, nxt_spec],
        out_specs=[cur] * 3, out_shape=[jax.ShapeDtypeStruct((l, ATT_GW), BF16)] * 3,
        scratch_shapes=[pltpu.VMEM((ATT_QB * w, ATT_GW), F32)] * 2,
        compiler_params=_params(("parallel",)), name=name,
    )(q, k, k, v, v, do, lse, dd, q, do, lse, dd)


def _to_perm(a, d):
    if d == 1:
        return a
    l, c = a.shape
    return a.reshape(l // d, d, c).transpose(1, 0, 2).reshape(l, c)


def _from_perm(a, d):
    if d == 1:
        return a
    l, c = a.shape
    return a.reshape(d, l // d, c).transpose(1, 0, 2).reshape(l, c)


def _mem_probs(qh, kh):
    s = _nt_dot(qh, kh) * (MEM_E ** -0.5)
    e = jnp.exp(s - jnp.max(s, axis=-1, keepdims=True))
    return e / jnp.sum(e, axis=-1, keepdims=True)


def _mem_fwd(mq, kv, name, tm=512):
    l, nm = mq.shape[0], kv.shape[0]

    def body(q_ref, kv_ref, o_ref):
        for h in range(MEM_H):
            sl = slice(h * MEM_E, (h + 1) * MEM_E)
            p = _mem_probs(q_ref[:, sl], kv_ref[:, sl])
            o_ref[:, sl] = _nn_dot(p, kv_ref[:, MEM_W + h * MEM_E:MEM_W + (h + 1) * MEM_E]).astype(o_ref.dtype)

    return pl.pallas_call(
        body, grid=(l // tm,),
        in_specs=[pl.BlockSpec((tm, MEM_W), lambda i: (i, 0)), pl.BlockSpec((nm, 2 * MEM_W), lambda i: (0, 0))],
        out_specs=pl.BlockSpec((tm, MEM_W), lambda i: (i, 0)),
        out_shape=jax.ShapeDtypeStruct((l, MEM_W), BF16),
        compiler_params=_params(("parallel",)), name=name,
    )(mq, kv)


def _mem_bwd(mq, kv, dmo, name, tm=512):
    l, nm = mq.shape[0], kv.shape[0]
    scale = MEM_E ** -0.5

    def body(q_ref, kv_ref, do_ref, dq_ref, dkv_ref):
        @pl.when(pl.program_id(0) == 0)
        def _():
            dkv_ref[...] = jnp.zeros_like(dkv_ref)

        for h in range(MEM_H):
            sl = slice(h * MEM_E, (h + 1) * MEM_E)
            vsl = slice(MEM_W + h * MEM_E, MEM_W + (h + 1) * MEM_E)
            qh, kh, vh, doh = q_ref[:, sl], kv_ref[:, sl], kv_ref[:, vsl], do_ref[:, sl]
            p = _mem_probs(qh, kh)
            dp = _nt_dot(doh, vh)
            ds = p * (dp - jnp.sum(dp * p, axis=-1, keepdims=True)) * scale
            dq_ref[:, sl] = _nn_dot(ds, kh).astype(dq_ref.dtype)
            dkv_ref[:, sl] += _tn_dot(ds, qh)
            dkv_ref[:, vsl] += _tn_dot(p, doh)

    row = pl.BlockSpec((tm, MEM_W), lambda i: (i, 0))
    full = pl.BlockSpec((nm, 2 * MEM_W), lambda i: (0, 0))
    return pl.pallas_call(
        body, grid=(l // tm,), in_specs=[row, full, row], out_specs=[row, full],
        out_shape=[jax.ShapeDtypeStruct((l, MEM_W), BF16), jax.ShapeDtypeStruct((nm, 2 * MEM_W), F32)],
        compiler_params=_params(("arbitrary",)), name=name,
    )(mq, kv, dmo)


def _gated_out_proj(zg, branches, b_gate, w_o, x, g2, name, tm=512):
    l, d = x.shape
    nbr = len(branches)

    def body(zg_ref, *rest):
        br_refs, (bg_ref, w_ref, x_ref, g2_ref, m_ref, h_ref, n_ref) = rest[:nbr], rest[nbr:]
        merged = jnp.zeros((tm, d), F32)
        for i, br_ref in enumerate(br_refs):
            cols = slice(i * d, (i + 1) * d)
            merged += _sigmoid(zg_ref[:, cols].astype(F32) + bg_ref[:, cols]) * br_ref[...].astype(F32)
        mb = merged.astype(BF16)
        m_ref[...] = mb
        hv = jnp.dot(mb.astype(_MXU), w_ref[...].astype(_MXU), preferred_element_type=F32) + x_ref[...]
        h_ref[...] = hv
        rs = lax.rsqrt(jnp.mean(hv * hv, axis=-1, keepdims=True) + RMS_EPS)
        n_ref[...] = (hv * rs * g2_ref[...]).astype(n_ref.dtype)

    row = lambda c: pl.BlockSpec((tm, c), lambda i: (i, 0))
    full = lambda a: pl.BlockSpec(a.shape, lambda i: (0, 0))
    return pl.pallas_call(
        body, grid=(l // tm,),
        in_specs=[row(nbr * d)] + [row(d)] * nbr + [full(b_gate), full(w_o), row(d), full(g2)],
        out_specs=[row(d)] * 3,
        out_shape=[jax.ShapeDtypeStruct((l, d), BF16), jax.ShapeDtypeStruct((l, d), F32),
                   jax.ShapeDtypeStruct((l, d), BF16)],
        compiler_params=_params(("parallel",)), name=name,
    )(zg, *branches, b_gate, w_o, x, g2)


def _discretize(lam_re, lam_im, log_dt, b_re, b_im):
    dt = jnp.exp(log_dt)[:, None]
    mag = jnp.exp(lam_re * dt)
    a_re, a_im = mag * jnp.cos(lam_im * dt), mag * jnp.sin(lam_im * dt)
    nr, ni = a_re - 1.0, a_im
    den = lam_re * lam_re + lam_im * lam_im
    coef_re = (nr * lam_re + ni * lam_im) / den
    coef_im = (ni * lam_re - nr * lam_im) / den
    bb_re = coef_re[..., None] * b_re - coef_im[..., None] * b_im
    bb_im = coef_re[..., None] * b_im + coef_im[..., None] * b_re
    return a_re, a_im, bb_re, bb_im


def _tiled(re, im):
    parts = []
    for j in range(SSM_S // SCAN_W):
        parts += [re[:, j * SCAN_W:(j + 1) * SCAN_W], im[:, j * SCAN_W:(j + 1) * SCAN_W]]
    return jnp.concatenate(parts, axis=1)


def _untiled(x):
    tiles = [x[:, j * SCAN_W:(j + 1) * SCAN_W] for j in range(x.shape[1] // SCAN_W)]
    return jnp.concatenate(tiles[0::2], axis=1), jnp.concatenate(tiles[1::2], axis=1)


def _bd_in(bb):
    return jnp.einsum("gph,gk->ghkp", bb, jnp.eye(SSM_G, dtype=bb.dtype)).reshape(SSM_W, SSM_S)


def _bd_diag(x):
    gb = SSM_G // SSM_BD
    t = x.reshape(SSM_BD, gb, SSM_H, gb, SSM_P)
    return jnp.einsum("bghgp->bghp", t).reshape(SSM_G, SSM_H, SSM_P)


_ANY = pl.BlockSpec(memory_space=pl.ANY)
_MESH = pl.DeviceIdType.MESH


def _allgather(x, name):
    def body(x_ref, out_ref, send_sems, recv_sems, local_sem):
        mx, my, mc = lax.axis_index("x"), lax.axis_index("y"), lax.axis_index("c")
        me, sibling = (mx, my, mc), (mx, my, 1 - mc)
        chips = [(1 - mx, my), (mx, 1 - my), (1 - mx, 1 - my)]

        def blk(px, py, pc):
            return out_ref.at[4 * px + 2 * py + pc]

        def copy(k, block, to, src=None):
            return pltpu.make_async_remote_copy(
                src_ref=blk(*block) if src is None else src, dst_ref=blk(*block),
                send_sem=send_sems.at[k], recv_sem=recv_sems.at[k], device_id=to, device_id_type=_MESH)

        mine = pltpu.make_async_copy(x_ref, blk(*me), local_sem)
        mine.start()
        first = [copy(0, me, sibling, src=x_ref)]
        first += [copy(1 + j, me, (*chip, mc), src=x_ref) for j, chip in enumerate(chips)]
        for cp in first:
            cp.start()
        passed = [copy(4 + j, (*chip, mc), sibling) for j, chip in enumerate(chips)]
        for j, chip in enumerate(chips):
            copy(1 + j, (*chip, mc), me).wait_recv()
            passed[j].start()
        copy(0, sibling, me).wait_recv()
        for j, chip in enumerate(chips):
            copy(4 + j, (*chip, 1 - mc), me).wait_recv()
        for cp in first + passed:
            cp.wait_send()
        mine.wait()

    return pl.pallas_call(
        body, out_shape=jax.ShapeDtypeStruct((N_DEV,) + x.shape, x.dtype), in_specs=[_ANY], out_specs=_ANY,
        scratch_shapes=[pltpu.SemaphoreType.DMA((7,)), pltpu.SemaphoreType.DMA((7,)), pltpu.SemaphoreType.DMA],
        name=name,
    )(x)


def _pair_exchange(g, name):
    def body(g_ref, out_ref, send_sems, recv_sems):
        mx, my, mc = lax.axis_index("x"), lax.axis_index("y"), lax.axis_index("c")
        copies = [pltpu.make_async_remote_copy(
            src_ref=g_ref.at[2 * k + (1 - mc)], dst_ref=out_ref.at[k], send_sem=send_sems.at[k],
            recv_sem=recv_sems.at[k], device_id=(mx, my, 1 - mc), device_id_type=_MESH) for k in range(4)]
        for cp in copies:
            cp.start()
        for cp in copies:
            cp.wait()

    return pl.pallas_call(
        body, out_shape=jax.ShapeDtypeStruct((4,) + g.shape[1:], g.dtype), in_specs=[_ANY], out_specs=_ANY,
        scratch_shapes=[pltpu.SemaphoreType.DMA((4,)), pltpu.SemaphoreType.DMA((4,))], name=name,
    )(g)


_HBM = pl.BlockSpec(memory_space=pltpu.HBM)
_SEM = pl.BlockSpec(memory_space=pltpu.SEMAPHORE)
_EFFECT = pltpu.SideEffectType.DATAFLOW_SIDE_EFFECTING
_TOKEN = jax.ShapeDtypeStruct((8, 128), F32)


def _peer(rel):
    pos = (lax.axis_index("x"), lax.axis_index("y"), lax.axis_index("c"))
    return tuple(1 - p if (rel >> (2 - i)) & 1 else p for i, p in enumerate(pos))


def _index_of(dev):
    return 4 * dev[0] + 2 * dev[1] + dev[2]


def _split_copies(src_ref, land_ref, sems, plan):
    n = len(plan)
    return [pltpu.make_async_remote_copy(
        src_ref=src_ref if s is None else src_ref.at[s], dst_ref=land_ref.at[d], send_sem=sems[k],
        recv_sem=sems[n + k], device_id=peer, device_id_type=_MESH) for k, (s, d, peer) in enumerate(plan)]


def _split_start(src, n_land, plan_fn, after, name):
    blk = src.shape[-2:]
    land = lax.empty((n_land,) + blk, src.dtype)
    n = len(plan_fn())

    def body(src_ref, land_ref, after_ref, *outs):
        for cp in _split_copies(src_ref, land_ref, outs[:2 * n], plan_fn()):
            cp.start()
        outs[2 * n + 2][...] = jnp.zeros_like(outs[2 * n + 2])

    res = pl.pallas_call(
        body, name=name,
        out_shape=(pltpu.SemaphoreType.DMA(()),) * (2 * n)
        + (pltpu.HBM(src.shape, src.dtype), pltpu.HBM(land.shape, land.dtype), _TOKEN),
        in_specs=(_HBM, _HBM, _ANY),
        out_specs=(_SEM,) * (2 * n) + (_HBM, _HBM, pl.BlockSpec(memory_space=pltpu.VMEM)),
        input_output_aliases={0: 2 * n, 1: 2 * n + 1},
        compiler_params=pltpu.CompilerParams(has_side_effects=_EFFECT),
    )(pltpu.with_memory_space_constraint(src, pltpu.HBM), pltpu.with_memory_space_constraint(land, pltpu.HBM), after)
    return res[:2 * n], res[2 * n], res[2 * n + 1], res[2 * n + 2]


def _split_wait(sems, src, land, plan_fn, after, name):
    n = len(sems) // 2

    def body(src_ref, land_ref, *rest):
        for cp in _split_copies(src_ref, land_ref, rest[:2 * n], plan_fn()):
            cp.wait_send()
            cp.wait_recv()

    return pl.pallas_call(
        body, name=name,
        out_shape=(pltpu.HBM(src.shape, src.dtype), pltpu.HBM(land.shape, land.dtype)),
        in_specs=(_HBM, _HBM) + (_SEM,) * (2 * n) + (_ANY,), out_specs=(_HBM, _HBM),
        input_output_aliases={0: 0, 1: 1},
        compiler_params=pltpu.CompilerParams(has_side_effects=_EFFECT),
    )(src, land, *sems, after)


def _gather_plan():
    me = _index_of(_peer(0))
    return [(None, me, _peer(rel)) for rel in range(1, N_DEV)]


def _gather_wait_plan():
    return [(None, _index_of(_peer(rel)), _peer(rel)) for rel in range(1, N_DEV)]


def _chip_plan():
    return [(_index_of(_peer(rel)) // 2, j, _peer(rel)) for j, rel in enumerate((4, 2, 6))]


def _owner_plan():
    return [(_index_of(_peer(rel)), rel - 1, _peer(rel)) for rel in range(1, N_DEV)]


def _pair_sum(g, t1, my_c, name, tr):
    _, r, c = g.shape

    def body(c_ref, g_ref, t_ref, o_ref, ob_ref):
        s = g_ref[...] + t_ref[...]
        o_ref[...] = s
        ob_ref[...] = s.astype(BF16)

    blk = pl.BlockSpec((None, tr, c), lambda k, i, cr: (k, i, 0))
    return pl.pallas_call(
        body,
        grid_spec=pltpu.PrefetchScalarGridSpec(
            num_scalar_prefetch=1, grid=(4, r // tr),
            in_specs=[pl.BlockSpec((None, tr, c), lambda k, i, cr: (2 * k + cr[0], i, 0)), blk],
            out_specs=[blk, blk]),
        out_shape=[jax.ShapeDtypeStruct((4, r, c), F32), jax.ShapeDtypeStruct((4, r, c), BF16)],
        compiler_params=_params(("parallel", "parallel")), name=name,
    )(my_c, g, t1)


def _adam_math(g, w, m, v):
    m = ADAM_B1 * m + (1.0 - ADAM_B1) * g
    v = ADAM_B2 * v + (1.0 - ADAM_B2) * (g * g)
    m_hat = m / (1.0 - ADAM_B1 ** ADAM_STEP)
    v_hat = v / (1.0 - ADAM_B2 ** ADAM_STEP)
    delta = -ADAM_LR * (m_hat / (jnp.sqrt(v_hat) + ADAM_EPS) + ADAM_WD * w)
    return delta, m, v


def _grad_sum(own, own_index, recv, name, tr):
    _, r, c = own.shape
    n = recv.shape[0]

    def body(k_ref, own_ref, *rest):
        g = own_ref[...]
        for recv_ref in rest[:n]:
            g = g + recv_ref[...].astype(F32)
        rest[n][...] = g

    def slot(j):
        return pl.BlockSpec((None, tr, c), lambda i, kr: (j, i, 0))

    return pl.pallas_call(
        body,
        grid_spec=pltpu.PrefetchScalarGridSpec(
            num_scalar_prefetch=1, grid=(r // tr,),
            in_specs=[pl.BlockSpec((None, tr, c), lambda i, kr: (kr[0], i, 0))] + [slot(j) for j in range(n)],
            out_specs=pl.BlockSpec((tr, c), lambda i, kr: (i, 0))),
        out_shape=jax.ShapeDtypeStruct((r, c), F32),
        compiler_params=_params(("parallel",)), name=name,
    )(own_index, own, *([recv] * n))


def _adam_many(g, w, m, v, row_tiles, name):
    n = len(g)

    def body(*refs):
        ins, outs = refs[:4 * n], refs[4 * n:]
        for i in range(n):
            res = _adam_math(ins[i][...], ins[n + i][...], ins[2 * n + i][...], ins[3 * n + i][...])
            for kind in range(3):
                outs[kind * n + i][...] = res[kind]

    def spec(a):
        blk = (a.shape[0] // row_tiles,) + a.shape[1:]
        return pl.BlockSpec(blk, lambda t, nd=a.ndim: (t,) + (0,) * (nd - 1))

    specs = [spec(a) for a in g]
    res = pl.pallas_call(
        body, grid=(row_tiles,), in_specs=specs * 4, out_specs=specs * 3,
        out_shape=[jax.ShapeDtypeStruct(a.shape, F32) for a in g] * 3,
        compiler_params=_params(("parallel",)), name=name,
    )(*g, *w, *m, *v)
    return res[:n], res[n:2 * n], res[2 * n:]


def _sum8(g8, name):
    _, r, c = g8.shape

    def body(g_ref, o_ref):
        acc = g_ref[0]
        for j in range(1, N_DEV):
            acc = acc + g_ref[j]
        o_ref[...] = acc

    return pl.pallas_call(
        body, grid=(1,), in_specs=[pl.BlockSpec((N_DEV, r, c), lambda i: (0, 0, 0))],
        out_specs=pl.BlockSpec((r, c), lambda i: (0, 0)), out_shape=jax.ShapeDtypeStruct((r, c), F32),
        compiler_params=_params(("arbitrary",)), name=name,
    )(g8)


def _pack(arrs, pad_rows=8):
    flat = jnp.concatenate([a.reshape(-1) for a in arrs])
    n = flat.shape[0]
    q = PACK_C * pad_rows
    tot = -(-n // q) * q
    if tot != n:
        flat = jnp.concatenate([flat, jnp.zeros((tot - n,), flat.dtype)])
    return flat.reshape(tot // PACK_C, PACK_C)


def _unpack(buf, shapes):
    flat = buf.reshape(-1)
    out, off = [], 0
    for s in shapes:
        n = int(np.prod(s))
        out.append(flat[off:off + n].reshape(s))
        off += n
    return out


GROUPS = (("w_in",),
          ("w_glu", "w_ssm_br", "w_mem_br", "w_attn_br"),
          ("w_up", "w_down"),
          ("w_mem_kv", "w_o"))
GROUP_TR = (400, 384, 512, 256)
MLP_GROUP = 2
MIXER_GROUPS = (1, 3)
ATTN_BR_FOLD = 2


def _stored_shape(name):
    r, c, ax = BIG_SHAPE[name]
    rows, cols = (r // N_DEV, c) if ax == 0 else (c // N_DEV, r)
    return (rows // ATTN_BR_FOLD, cols * ATTN_BR_FOLD) if name == "w_attn_br" else (rows, cols)


def _stored(shard, name):
    a = shard[0].T if BIG_SHAPE[name][2] == 1 else shard[0]
    return a.reshape(_stored_shape(name))


def _unstored(a, name):
    r, c, ax = BIG_SHAPE[name]
    if ax == 0:
        return a.reshape(1, r // N_DEV, c)
    return a.reshape(c // N_DEV, r).T[None]


def _pack_group(d, names):
    return jnp.concatenate([_stored(d[n], n) for n in names], axis=0)


def _split_group(buf, names):
    out, off = {}, 0
    for n in names:
        rows = _stored_shape(n)[0]
        out[n] = buf[..., off:off + rows, :]
        off += rows
    return out


def _full_stored(stacked, name):
    r, c, ax = BIG_SHAPE[name]
    return stacked.reshape((r, c) if ax == 0 else (c, r))


def _stacked_stored(full, name):
    return full.reshape((N_DEV,) + _stored_shape(name))


def _gelu_parts(x):
    c0, c1 = math.sqrt(2.0 / math.pi), 0.044715
    th = jnp.tanh(c0 * (x + c1 * x * x * x))
    return th, c0, c1


def _local_step(x, mem, tgt, wb, sp, late_weights, grads_ready, small_grads_ready):
    l = x.shape[0]
    w_a, w_g = wb["w_in"][:ZA_W], wb["w_in"][ZA_W:]

    a_re, a_im, bb_re, bb_im = _discretize(sp["ssm_lambda_re"], sp["ssm_lambda_im"], sp["ssm_log_dt"],
                                           sp["ssm_b_re"], sp["ssm_b_im"])
    a_pair = jnp.stack([a_re.reshape(1, SSM_S), a_im.reshape(1, SSM_S)])
    a_conj = jnp.stack([a_re.reshape(1, SSM_S), -a_im.reshape(1, SSM_S)])
    b_re_t, b_im_t = _bd_in(bb_re).astype(BF16), _bd_in(bb_im).astype(BF16)
    c_re_t = _bd_in(sp["ssm_c_re"].transpose(0, 2, 1)).astype(BF16)
    c_im_t = (-_bd_in(sp["ssm_c_im"].transpose(0, 2, 1))).astype(BF16)
    d_row = sp["ssm_d"].reshape(1, SSM_W)

    n1 = _rms_fwd(x, sp["norm1_g"], "rms1")
    za = _mm(n1, w_a, [BF16], tb=True, name="in_proj_a", tn=1664)
    zg = _mm(n1, w_g, [BF16], tb=True, name="in_proj_g")
    for gi in MIXER_GROUPS:
        wb = {**wb, **late_weights(gi, za)}
    u = za[:, :SSM_W]
    mq = za[:, ZA_W - MEM_W:]

    u_s = _scan_order(u)
    s_all = _ssm_scan(u_s, b_re_t, b_im_t, a_pair, reverse=False, name="ssm_scan_fwd")
    ys = _time_order(_mm(s_all, _tiled(c_re_t, c_im_t), [F32], tb=True, bd=SSM_BD, tm=2048, name="ssm_cs"))

    def gelu_fn(r, b):
        y0 = r[0] + b[0] * r[1].astype(F32)
        th, _, _ = _gelu_parts(y0)
        return [y0, 0.5 * y0 * (1.0 + th)], []
    y0, y1 = _ew(gelu_fn, [ys, u], [d_row], [(SSM_W, F32), (SSM_W, BF16)], [], name="ssm_gelu", tm=2048)

    def glu_epi(acc, y1t, bg):
        t = acc + bg
        return t, y1t.astype(F32) * _sigmoid(t)
    t_glu, y2 = _mm(y1, wb["w_glu"], [F32, BF16], epi=glu_epi, mn=[y1], rows=[sp["b_glu"]], name="ssm_glu")
    br_ssm = _mm(y2, wb["w_ssm_br"], [BF16], tb=True, name="ssm_br")

    qkv_p, o_g, lse_g = [], [], []
    for g, d in enumerate(DILATIONS):
        nb = l // d // ATT_WIN
        cols = [za[:, SSM_W + (3 * j + g) * ATT_GW: SSM_W + (3 * j + g + 1) * ATT_GW] for j in range(3)]
        qp, kp, vp = [_to_perm(cc, d) for cc in cols]
        qkv_p.append((qp, kp, vp))
        og, lg = _attn_fwd(qp, kp, vp, nb, "attn_fwd%d" % g)
        o_g.append(_from_perm(og, d))
        lse_g.append(_from_perm(lg, d))

    def merge_fn(r, b):
        o0, o1, o2, l0, l1, l2 = r
        mx = jnp.maximum(jnp.maximum(l0, l1), l2)
        e0, e1, e2 = jnp.exp(l0 - mx), jnp.exp(l1 - mx), jnp.exp(l2 - mx)
        tot = e0 + e1 + e2
        return [(e0 * o0 + e1 * o1 + e2 * o2) / tot, mx + jnp.log(tot)], []
    o_att, lse_tot = _ew(merge_fn, o_g + lse_g, [], [(ATT_GW, F32), (ATT_GW, F32)], [], name="attn_merge", tm=2048)
    br_attn = _mm(o_att, wb["w_attn_br"], [BF16], tb=True, name="attn_br")

    mn = _rms_fwd(mem, sp["mem_norm_g"], "rms_mem")
    kv = _mm(mn, wb["w_mem_kv"], [BF16], name="mem_kv")
    mo = _mem_fwd(mq, kv, "mem_attn_fwd")
    br_mem = _mm(mo, wb["w_mem_br"], [BF16], tb=True, name="mem_br")

    merged, h1, n2 = _gated_out_proj(zg, [br_ssm, br_attn, br_mem], sp["b_gate"], wb["w_o"], x, sp["norm2_g"],
                                     "gated_o_proj")

    def up_epi(acc):
        ra = jnp.maximum(acc, 0.0)
        return ra * ra, ra
    wm = late_weights(MLP_GROUP, n2)
    f_act, r_act = _mm(n2, wm["w_up"], [BF16, BF16], tb=True, epi=up_epi, name="mlp_up")
    def down_epi(acc, ht, tv, gf):
        hv = acc + ht
        rs = lax.rsqrt(jnp.mean(hv * hv, axis=-1, keepdims=True) + RMS_EPS)
        err = hv * rs * gf - tv
        dh, dgf = _rms_bwd_tile(hv, err * (1.0 / D_MODEL), gf)
        return dh, dgf, _colsum(err * err) * (0.5 / D_MODEL)
    dh2, d_final_g, loss_cols = _mm(f_act, wm["w_down"], [F32], epi=down_epi, mn=[h1, tgt], rows=[sp["final_g"]],
                                    n_sums=2, tk=1024, name="mlp_down")
    loss = jnp.sum(loss_cols, axis=1, keepdims=True)

    gw, gs = {}, {"final_g": d_final_g}
    d_act = _mm(dh2, wm["w_down"], [BF16], tb=True, epi=lambda acc, ra: (acc * 2.0 * ra.astype(F32),), mn=[r_act],
                name="mlp_down_dx")
    dw_down = _mm(f_act, dh2, [F32], ta=True, name="mlp_down_dw")
    dw_up = _mm(d_act, n2, [F32], ta=True, name="mlp_up_dw")
    token = grads_ready(MLP_GROUP, {"w_up": dw_up, "w_down": dw_down})
    def up_dx_epi(acc, ht, dht, g2):
        dx, dg = _rms_bwd_tile(ht, acc, g2)
        return dx + dht, dg
    dh1, gs["norm2_g"] = _mm(d_act, wm["w_up"], [F32], epi=up_dx_epi, mn=[h1, dh2],
                             rows=[sp["norm2_g"] + token[:1, :1]], n_sums=1, tk=1024, name="mlp_up_dx")
    gw["w_o"] = _mm(merged, dh1, [F32], ta=True, name="o_proj_dw")

    def gate_bwd_epi(dm, *tiles):
        dbr, dz = [], []
        for zt, bt, bias in zip(tiles[0:3], tiles[3:6], tiles[6:9]):
            gt = _sigmoid(zt.astype(F32) + bias)
            dbr.append(dm * gt)
            dz.append(dm * bt.astype(F32) * gt * (1.0 - gt))
        return (*dbr, *dz, *[_colsum(t) for t in dz])
    gate_bias = [sp["b_gate"][:, i * D_MODEL:(i + 1) * D_MODEL] for i in range(3)]
    res = _mm(dh1, wb["w_o"], [BF16] * 6, tb=True, epi=gate_bwd_epi, mn=[(zg, 0), (zg, 1), (zg, 2), br_ssm, br_attn, br_mem],
              rows=gate_bias, n_sums=3, tm=512, name="o_proj_dx")
    (dbr_ssm, dbr_attn, dbr_mem), dzg = res[0:3], res[3:6]
    gs["b_gate"] = jnp.concatenate(res[6:9], axis=1)

    gw["w_ssm_br"] = _mm(dbr_ssm, y2, [F32], ta=True, name="ssm_br_dw")
    def glu_bwd_epi(dy, y1t, tt):
        sg = _sigmoid(tt)
        dt = dy * y1t.astype(F32) * sg * (1.0 - sg)
        return dt, dy * sg, _colsum(dt)
    dt_glu, dy1a, gs["b_glu"] = _mm(dbr_ssm, wb["w_ssm_br"], [BF16, F32], epi=glu_bwd_epi, mn=[y1, t_glu], n_sums=1,
                                    name="ssm_br_dx")
    gw["w_glu"] = _mm(y1, dt_glu, [F32], ta=True, name="ssm_glu_dw")

    def gelu_bwd_epi(acc, dy1t, y0t, ut):
        th, c0, c1 = _gelu_parts(y0t)
        dg = 0.5 * (1.0 + th) + 0.5 * y0t * (1.0 - th * th) * c0 * (1.0 + 3.0 * c1 * y0t * y0t)
        dy = (acc + dy1t) * dg
        return dy, _colsum(dy * ut.astype(F32))
    dy0, gs["ssm_d"] = _mm(dt_glu, wb["w_glu"], [F32], tb=True, epi=gelu_bwd_epi, mn=[dy1a, y0, u], n_sums=1,
                           name="ssm_glu_dx")
    dy0_s = _scan_order(dy0)
    lam, da, d_b, d_c = _ssm_scan(dy0_s, c_re_t, c_im_t, a_conj, reverse=True, s_fwd=s_all, u=u_s,
                                  name="ssm_scan_bwd")
    du = _time_order(_mm(lam, _tiled(b_re_t, b_im_t), [BF16], tb=True,
                         epi=lambda acc, dyt, dr: (acc + dyt * dr,), mn=[dy0_s], rows=[d_row], bd=SSM_BD, tm=2048, name="ssm_bu_dx"))
    gs["a_re"], gs["a_im"] = da[0], da[1]
    (dbr, dbi), (dcr, dci) = _untiled(d_b), _untiled(d_c)
    gs["bb_re"], gs["bb_im"] = _bd_diag(dbr).transpose(0, 2, 1), _bd_diag(dbi).transpose(0, 2, 1)
    gs["ssm_c_re"], gs["ssm_c_im"] = _bd_diag(dcr), -_bd_diag(dci)

    gw["w_attn_br"] = _mm(dbr_attn, o_att, [F32], ta=True, name="attn_br_dw")

    def do_epi(acc, ot):
        prod = acc * ot
        head = lax.broadcasted_iota(jnp.int32, prod.shape, 1) // ATT_E
        dd = jnp.zeros_like(prod)
        for h in range(ATT_HG):
            dd = jnp.where(head == h, jnp.sum(jnp.where(head == h, prod, 0.0), axis=1, keepdims=True), dd)
        return acc, dd
    do_att, dd_att = _mm(dbr_attn, wb["w_attn_br"], [BF16, F32], epi=do_epi, mn=[o_att], name="attn_br_dx")
    dq_l, dk_l, dv_l = [], [], []
    for g, d in enumerate(DILATIONS):
        nb = l // d // ATT_WIN
        qp, kp, vp = qkv_p[g]
        dq, dk, dv = _attn_bwd(qp, kp, vp, _to_perm(do_att, d), _to_perm(lse_tot, d), _to_perm(dd_att, d),
                               nb, "attn_bwd%d" % g)
        dq_l.append(_from_perm(dq, d))
        dk_l.append(_from_perm(dk, d))
        dv_l.append(_from_perm(dv, d))

    gw["w_mem_br"] = _mm(dbr_mem, mo, [F32], ta=True, name="mem_br_dw")
    dmo = _mm(dbr_mem, wb["w_mem_br"], [BF16], name="mem_br_dx")
    dmq, dkv = _mem_bwd(mq, kv, dmo, "mem_attn_bwd")
    gw["w_mem_kv"] = _mm(mn, dkv, [F32], ta=True, name="mem_kv_dw")
    dmn = _mm(dkv, wb["w_mem_kv"], [F32], tb=True, name="mem_kv_dx")
    token = sum(grads_ready(gi, gw) for gi in MIXER_GROUPS)
    gs["mem_norm_g"] = _rms_bwd(mem, dmn, None, sp["mem_norm_g"] + token[:1, :1], "rms_mem_bwd")[1]

    dza = jnp.concatenate([du] + dq_l + dk_l + dv_l + [dmq], axis=1)
    dn_a = _mm(dza, w_a, [F32], name="in_proj_a_dx", tk=1664)
    dw_a = _mm(dza, n1, [F32], ta=True, name="in_proj_a_dw", tm=1664)
    dw_g = [_mm(dzg[i], n1, [F32], ta=True, name="in_proj_g_dw%d" % i) for i in range(3)]
    gw["w_in"] = jnp.concatenate([dw_a] + dw_g, axis=0)
    token = grads_ready(0, gw) + small_grads_ready(gs)
    def in_dx_epi(acc, pt, xt, dht, g1):
        dx, dg = _rms_bwd_tile(xt, acc + pt, g1)
        return dx + dht, dg
    w_gs = [w_g[i * D_MODEL:(i + 1) * D_MODEL] for i in range(3)]
    grad_x, gs["norm1_g"] = _mm(dzg[0], w_gs[0], [F32], pair2=(dzg[1], w_gs[1], dzg[2], w_gs[2]), epi=in_dx_epi,
                                mn=[dn_a, x, dh1],
                                rows=[sp["norm1_g"] + token[:1, :1]], n_sums=1, tm=512, name="in_proj_g_dx")
    return loss, grad_x, gs


_SMALL_GRAD_ORDER = ("norm1_g", "mem_norm_g", "b_gate", "a_re", "a_im", "bb_re", "bb_im", "ssm_c_re", "ssm_c_im",
                     "ssm_d", "b_glu", "norm2_g", "final_g")


def kernel(x, mem, norm1_g, mem_norm_g, w_in, b_gate, ssm_lambda_re, ssm_lambda_im, ssm_log_dt, ssm_b_re, ssm_b_im, ssm_c_re, ssm_c_im, ssm_d, w_glu, b_glu, w_ssm_br, w_attn_br, w_mem_kv, w_mem_br, w_o, norm2_g, w_up, w_down, final_g, loss_target, m_norm1_g, m_mem_norm_g, m_w_in, m_b_gate, m_ssm_lambda_re, m_ssm_lambda_im, m_ssm_log_dt, m_ssm_b_re, m_ssm_b_im, m_ssm_c_re, m_ssm_c_im, m_ssm_d, m_w_glu, m_b_glu, m_w_ssm_br, m_w_attn_br, m_w_mem_kv, m_w_mem_br, m_w_o, m_norm2_g, m_w_up, m_w_down, m_final_g, v_norm1_g, v_mem_norm_g, v_w_in, v_b_gate, v_ssm_lambda_re, v_ssm_lambda_im, v_ssm_log_dt, v_ssm_b_re, v_ssm_b_im, v_ssm_c_re, v_ssm_c_im, v_ssm_d, v_w_glu, v_b_glu, v_w_ssm_br, v_w_attn_br, v_w_mem_kv, v_w_mem_br, v_w_o, v_norm2_g, v_w_up, v_w_down, v_final_g):
    args = dict(locals())
    w = {n: args[n] for n in ALL_W}
    m = {n: args["m_" + n] for n in ALL_W}
    v = {n: args["v_" + n] for n in ALL_W}
    my_c = lax.axis_index("c").astype(jnp.int32).reshape(1)
    my_chip = (2 * lax.axis_index("x") + lax.axis_index("y")).astype(jnp.int32).reshape(1)

    w_pack = [_pack_group(w, names) for names in GROUPS]
    my_index = (4 * lax.axis_index("x") + 2 * lax.axis_index("y") + lax.axis_index("c")).astype(jnp.int32)
    zero = jnp.zeros((), jnp.int32)
    w_all = _allgather(w_pack[0].astype(BF16), "allgather_weights0")
    wb = {n: _full_stored(part, n) for n, part in _split_group(w_all, GROUPS[0]).items()}
    gathers = {gi: _split_start(w_pack[gi].astype(BF16), N_DEV, _gather_plan, w_all, "weights_gather_start%d" % gi)
               for gi in range(1, len(GROUPS))}

    def gathered(started, after, name):
        sems, src, land, _ = started
        src, land = _split_wait(sems, src, land, _gather_wait_plan, after, name)
        return lax.dynamic_update_slice(land, src[None], (my_index, zero, zero))

    def late_weights(gi, after):
        full = gathered(gathers[gi], after, "weights_gather_wait%d" % gi)
        return {n: _full_stored(part, n) for n, part in _split_group(full, GROUPS[gi]).items()}

    pending = {}

    def grads_ready(gi, grads):
        g_pack = jnp.concatenate([_stacked_stored(grads[n], n) for n in GROUPS[gi]], axis=1)
        if gi == 0:
            t1 = _pair_exchange(g_pack, "grad_pair_exchange%d" % gi)
            p_sum, p_bf = _pair_sum(g_pack, t1, my_c, "grad_pair_sum%d" % gi, GROUP_TR[gi])
            started = _split_start(p_bf, 3, _chip_plan, p_sum, "grad_chip_exchange_start%d" % gi)
            pending[gi] = (p_sum, my_chip, started, _chip_plan)
        else:
            started = _split_start(g_pack.astype(BF16), N_DEV - 1, _owner_plan, g_pack, "grad_exchange_start%d" % gi)
            pending[gi] = (g_pack, my_index.reshape(1), started, _owner_plan)
        return started[3]

    early_small = [n for n in _SMALL_GRAD_ORDER if n != "norm1_g"]
    small_started = []

    def small_grads_ready(gs):
        started = _split_start(_pack([gs[n] for n in early_small]), N_DEV, _gather_plan, gs["mem_norm_g"],
                               "small_grads_gather_start")
        small_started.append((started, [gs[n].shape for n in early_small]))
        return started[3]

    sp = {
        "norm1_g": norm1_g + sum(started[3][:1, :1] for started in gathers.values()), "mem_norm_g": mem_norm_g, "b_gate": b_gate, "b_glu": b_glu, "norm2_g": norm2_g,
        "final_g": final_g.reshape(1, D_MODEL),
        "ssm_lambda_re": ssm_lambda_re[0], "ssm_lambda_im": ssm_lambda_im[0], "ssm_log_dt": ssm_log_dt[0],
        "ssm_b_re": ssm_b_re[0], "ssm_b_im": ssm_b_im[0], "ssm_c_re": ssm_c_re[0], "ssm_c_im": ssm_c_im[0],
        "ssm_d": ssm_d[0],
    }
    loss, grad_x, gs = _local_step(x[0], mem[0], loss_target[0], wb, sp, late_weights, grads_ready,
                                     small_grads_ready)
    loss = lax.psum(loss[0, 0], ("x", "y", "c"))
    n1_started = _split_start(_pack([gs["norm1_g"]]), N_DEV, _gather_plan, grad_x, "norm1_grad_gather_start")

    big_g = {}
    for gi, names in enumerate(GROUPS):
        own, own_index, (sems, src, land, _), plan = pending[gi]
        recv = _split_wait(sems, src, land, plan, grad_x, "grad_exchange_wait%d" % gi)[1]
        g_pack = _grad_sum(own, own_index, recv, "grad_sum%d" % gi, GROUP_TR[gi])
        for n, part in _split_group(g_pack, names).items():
            big_g[n] = _unstored(part, n)
    rows_of = lambda d, names: [d[n].reshape(d[n].shape[-2:]) for n in names]
    big_out = _adam_many(rows_of(big_g, BIG), rows_of(w, BIG), rows_of(m, BIG), rows_of(v, BIG), 8, "adam_big")
    big = [big_g] + [{n: a[None] for n, a in zip(BIG, outs)} for outs in big_out]

    (sg_started, sg_shapes), = small_started
    sg_all = jnp.concatenate([gathered(sg_started, big_out[0][0], "small_grads_gather_wait"),
                              gathered(n1_started, big_out[0][0], "norm1_grad_gather_wait")], axis=1)
    sg_sum = _sum8(sg_all, "sum_small_grads")
    n1_rows = n1_started[1].shape[0]
    sg = dict(zip(early_small, _unpack(sg_sum[:-n1_rows], sg_shapes)))
    sg["norm1_g"] = _unpack(sg_sum[-n1_rows:], [gs["norm1_g"].shape])[0]
    _, disc_vjp = jax.vjp(_discretize, sp["ssm_lambda_re"], sp["ssm_lambda_im"], sp["ssm_log_dt"],
                          sp["ssm_b_re"], sp["ssm_b_im"])
    d_lre, d_lim, d_ldt, d_bre, d_bim = disc_vjp((sg["a_re"].reshape(SSM_G, SSM_P), sg["a_im"].reshape(SSM_G, SSM_P),
                                                  sg["bb_re"], sg["bb_im"]))
    small_grad = {
        "norm1_g": sg["norm1_g"], "mem_norm_g": sg["mem_norm_g"], "b_gate": sg["b_gate"],
        "ssm_lambda_re": d_lre, "ssm_lambda_im": d_lim, "ssm_log_dt": d_ldt, "ssm_b_re": d_bre, "ssm_b_im": d_bim,
        "ssm_c_re": sg["ssm_c_re"], "ssm_c_im": sg["ssm_c_im"], "ssm_d": sg["ssm_d"], "b_glu": sg["b_glu"],
        "norm2_g": sg["norm2_g"], "final_g": sg["final_g"],
    }
    small_grad = {n: small_grad[n].reshape(w[n].shape) for n in SMALL}

    def squeezed(a):
        return a.reshape(a.shape[1:]) if a.ndim > 2 else a.reshape(1, -1)

    sq = lambda d: [squeezed(d[n]) for n in SMALL]
    small_out = _adam_many(sq(small_grad), sq(w), sq(m), sq(v), 1, "adam_small")
    small = [small_grad] + [{n: a.reshape(w[n].shape) for n, a in zip(SMALL, outs)} for outs in small_out]

    outs = [loss, grad_x[None]]
    for kind in range(4):
        for n in ALL_W:
            outs.append(big[kind][n] if n in BIG else small[kind][n])
    return tuple(outs)
```

```python
import math

import numpy as np
import jax
import jax.numpy as jnp
from jax import lax
from jax.experimental import pallas as pl
from jax.experimental.pallas import tpu as pltpu

F32 = jnp.float32
BF16 = jnp.bfloat16
_MXU = jnp.bfloat16

D_MODEL = 1024
SSM_G, SSM_H, SSM_P = 32, 16, 64
SSM_W = SSM_G * SSM_H
SSM_S = SSM_G * SSM_P
SSM_BD = 4
ATT_E = 64
ATT_HG = 4
ATT_GW = ATT_HG * ATT_E
ATT_WIN = 128
ATT_QB = 8
ATT_QB_FWD = 4
DILATIONS = (1, 4, 16)
MEM_H, MEM_E = 4, 128
MEM_W = MEM_H * MEM_E
ZA_W = SSM_W + 9 * ATT_GW + MEM_W
ZG_W = 3 * D_MODEL
IN_W = ZA_W + ZG_W
RMS_EPS = 1e-6
NEG_INF = -1e30

ADAM_LR, ADAM_B1, ADAM_B2, ADAM_EPS, ADAM_WD, ADAM_STEP = 0.001, 0.9, 0.999, 1e-08, 0.01, 10

N_DEV = 8
PACK_C = 512
_VMEM_LIMIT = 56 * 1024 * 1024
SUBLANES = 16
SCAN_SEG = 128
SCAN_CHAINS = 4
SCAN_UNROLL = 4
SCAN_W = 128

BIG = ("w_in", "w_glu", "w_ssm_br", "w_attn_br", "w_mem_kv", "w_mem_br", "w_o", "w_up", "w_down")
BIG_SHAPE = {
    "w_in": (D_MODEL, IN_W, 1), "w_glu": (SSM_W, SSM_W, 0), "w_ssm_br": (SSM_W, D_MODEL, 1),
    "w_attn_br": (ATT_GW, D_MODEL, 1), "w_mem_kv": (D_MODEL, 2 * MEM_W, 0), "w_mem_br": (MEM_W, D_MODEL, 1),
    "w_o": (D_MODEL, D_MODEL, 0), "w_up": (D_MODEL, 4 * D_MODEL, 1), "w_down": (4 * D_MODEL, D_MODEL, 0),
}
SMALL = ("norm1_g", "mem_norm_g", "b_gate", "ssm_lambda_re", "ssm_lambda_im", "ssm_log_dt", "ssm_b_re",
         "ssm_b_im", "ssm_c_re", "ssm_c_im", "ssm_d", "b_glu", "norm2_g", "final_g")
ALL_W = ("norm1_g", "mem_norm_g", "w_in", "b_gate", "ssm_lambda_re", "ssm_lambda_im", "ssm_log_dt", "ssm_b_re",
         "ssm_b_im", "ssm_c_re", "ssm_c_im", "ssm_d", "w_glu", "b_glu", "w_ssm_br", "w_attn_br", "w_mem_kv",
         "w_mem_br", "w_o", "norm2_g", "w_up", "w_down", "final_g")


def _params(sem):
    return pltpu.CompilerParams(dimension_semantics=sem, vmem_limit_bytes=_VMEM_LIMIT)


def _pick(n, cap):
    if n <= cap:
        return n
    t = (cap // 128) * 128
    while n % t:
        t -= 128
    return t


def _mm(a, b, outs, *, name, ta=False, tb=False, epi=None, mn=(), rows=(), pair2=None, bd=0, n_sums=0,
        tm=1024, tn=1024, tk=2048):
    ab = [a, b] + (list(pair2) if pair2 is not None else [])
    a_shape, b_shape = ab[0].shape, ab[1].shape
    m = a_shape[1] if ta else a_shape[0]
    k = a_shape[0] if ta else a_shape[1]
    n = b_shape[0] if tb else b_shape[1]
    assert k == (b_shape[1] if tb else b_shape[0]), (name, a_shape, b_shape)
    out_n = n
    if bd and ta:
        assert not tb
        tm, tn, tk = m // bd, n // bd, _pick(k, tk)
        grid, out_n = (bd, 1, k // tk), tn
        a_blk = ((tk, tm), lambda i, j, kk: (kk, i))
        b_blk = ((tk, tn), lambda i, j, kk: (kk, i))
        mn_spec = pl.BlockSpec((tm, tn), lambda i, j, kk: (i, 0))
    elif bd:
        tm, tn, tk = _pick(m, tm), n // bd, k // bd
        grid = (m // tm, bd, 1)
        a_blk = ((tm, tk), lambda i, j, kk: (i, j))
        b_blk = ((tn, tk) if tb else (tk, tn), lambda i, j, kk: (j, j))
        mn_spec = pl.BlockSpec((tm, tn), lambda i, j, kk: (i, j))
    else:
        tm, tn, tk = _pick(m, tm), _pick(n, tn), _pick(k, tk)
        grid = (m // tm, n // tn, k // tk)
        a_blk = ((tk, tm), lambda i, j, kk: (kk, i)) if ta else ((tm, tk), lambda i, j, kk: (i, kk))
        b_blk = ((tn, tk), lambda i, j, kk: (j, kk)) if tb else ((tk, tn), lambda i, j, kk: (kk, j))
        mn_spec = pl.BlockSpec((tm, tn), lambda i, j, kk: (i, j))

    ab_specs = [pl.BlockSpec(*(a_blk if q % 2 == 0 else b_blk)) for q in range(len(ab))]
    mn_arrays = [e[0] if isinstance(e, tuple) else e for e in mn]
    mn_specs = [pl.BlockSpec((tm, tn), lambda i, j, kk, c=e[1]: (i, c)) if isinstance(e, tuple) else mn_spec
                for e in mn]
    nk = grid[2]
    row_spec = pl.BlockSpec((1, tn), lambda i, j, kk: (0, j))
    n_ex, n_out = len(mn) + len(rows), len(outs)
    assert n_sums == 0 or (grid[1] == 1 and not bd)
    dims = (((0 if ta else 1,), (1 if tb else 0,)), ((), ()))

    def body(*refs):
        ab_refs, rest = refs[:len(ab)], refs[len(ab):]
        ex, o_refs = rest[:n_ex], rest[n_ex:n_ex + n_out]
        s_refs = rest[n_ex + n_out:n_ex + n_out + n_sums]
        first_row_tile = pl.program_id(0) == 0
        kk = pl.program_id(2)

        pairs = list(zip(ab_refs[0::2], ab_refs[1::2]))

        def product(pair):
            return lax.dot_general(pair[0][...].astype(_MXU), pair[1][...].astype(_MXU), dims,
                                   preferred_element_type=F32)

        def finish(total):
            vals = (total,) if epi is None else epi(total, *[r[...] for r in ex])
            for r, v in zip(o_refs, vals):
                r[...] = v.astype(r.dtype)
            for r, v in zip(s_refs, vals[n_out:]):
                r[...] = jnp.where(first_row_tile, v, r[...] + v)

        if nk == 1:
            total = product(pairs[0])
            for pair in pairs[1:]:
                total = total + product(pair)
            finish(total)
        else:
            acc = rest[-1]

            @pl.when(kk == 0)
            def _():
                acc[...] = jnp.zeros_like(acc)

            for pair in pairs:
                acc[...] += product(pair)

            @pl.when(kk == nk - 1)
            def _():
                finish(acc[...])

    res = pl.pallas_call(
        body, grid=grid,
        in_specs=ab_specs + mn_specs + [row_spec] * len(rows),
        out_specs=[mn_spec] * n_out + [row_spec] * n_sums,
        out_shape=[jax.ShapeDtypeStruct((m, out_n), dt) for dt in outs]
        + [jax.ShapeDtypeStruct((1, out_n), F32)] * n_sums,
        scratch_shapes=[pltpu.VMEM((tm, tn), F32)] if nk > 1 else [],
        compiler_params=_params(("arbitrary" if n_sums else "parallel", "parallel", "arbitrary")), name=name,
    )(*ab, *mn_arrays, *rows)
    return res[0] if n_out + n_sums == 1 else res


def _ew(fn, rows, bcs, out_rows, out_accs, *, name, tm=256):
    r = rows[0].shape[0]
    tm = min(tm, r)
    assert r % tm == 0
    nr, nb, no, na = len(rows), len(bcs), len(out_rows), len(out_accs)

    def body(*refs):
        i = pl.program_id(0)
        r_in, b_in = refs[:nr], refs[nr:nr + nb]
        o_r, o_a = refs[nr + nb:nr + nb + no], refs[nr + nb + no:]
        outs, accs = fn([x[...] for x in r_in], [x[...] for x in b_in])
        for ref, v in zip(o_r, outs):
            ref[...] = v.astype(ref.dtype)
        if na:
            @pl.when(i == 0)
            def _():
                for ref in o_a:
                    ref[...] = jnp.zeros_like(ref)

            for ref, v in zip(o_a, accs):
                ref[...] += v

    res = pl.pallas_call(
        body, grid=(r // tm,),
        in_specs=[pl.BlockSpec((tm, x.shape[1]), lambda i: (i, 0)) for x in rows]
        + [pl.BlockSpec((1, x.shape[1]), lambda i: (0, 0)) for x in bcs],
        out_specs=[pl.BlockSpec((tm, c), lambda i: (i, 0)) for c, _ in out_rows]
        + [pl.BlockSpec((1, c), lambda i: (0, 0)) for c in out_accs],
        out_shape=[jax.ShapeDtypeStruct((r, c), dt) for c, dt in out_rows]
        + [jax.ShapeDtypeStruct((1, c), F32) for c in out_accs],
        compiler_params=_params(("arbitrary",)), name=name,
    )(*rows, *bcs)
    return res


def _colsum(x):
    return jnp.sum(x, axis=0, keepdims=True)


def _sigmoid(x):
    return 1.0 / (1.0 + jnp.exp(-x))


def _rms_bwd_tile(xv, dv, g):
    rs = lax.rsqrt(jnp.mean(xv * xv, axis=-1, keepdims=True) + RMS_EPS)
    gd = dv * g
    dx = rs * gd - xv * (rs * rs * rs) * jnp.mean(gd * xv, axis=-1, keepdims=True)
    return dx, _colsum(dv * xv * rs)


def _rms_fwd(x, g, name):
    def fn(r, b):
        xv = r[0]
        rs = lax.rsqrt(jnp.mean(xv * xv, axis=-1, keepdims=True) + RMS_EPS)
        return [xv * rs * b[0]], []
    return _ew(fn, [x], [g], [(x.shape[1], BF16)], [], name=name, tm=1024)[0]


def _rms_bwd(x, dn, res, g, name):
    def fn(r, b):
        dx, dg = _rms_bwd_tile(r[0], r[1], b[0])
        if res is not None:
            dx = dx + r[2]
        return [dx], [dg]
    rows = [x, dn] + ([res] if res is not None else [])
    return _ew(fn, rows, [g], [(x.shape[1], F32)], [x.shape[1]], name=name)


def _scan_order(x):
    l, c = x.shape
    return x.reshape(l // (SUBLANES * SCAN_SEG), SUBLANES, SCAN_SEG, c).transpose(0, 2, 1, 3).reshape(l, c)


def _time_order(x):
    l, c = x.shape
    return x.reshape(l // (SUBLANES * SCAN_SEG), SCAN_SEG, SUBLANES, c).transpose(0, 2, 1, 3).reshape(l, c)


def _ssm_scan(x, w_re, w_im, a_pair, *, reverse, s_fwd=None, u=None, name):
    l = x.shape[0]
    seg, w = SCAN_SEG, SCAN_W
    bd_w = SSM_W // SSM_BD
    tiles_per_bd = SSM_S // SSM_BD // w
    nch = min(SCAN_CHAINS, l // (SUBLANES * seg))
    chain_rows = SUBLANES * seg
    tb = nch * chain_rows
    nt = l // tb
    with_da = s_fwd is not None
    assert reverse or not with_da

    def tt(t):
        return nt - 1 - t if reverse else t

    def body(*refs):
        if with_da:
            (x_ref, wr_ref, wi_ref, a_ref, sf_ref, sp_ref, u_ref, s_ref, da_ref, dw_ref, dx_ref,
             p_ref, c_ref, b_scr, l_scr) = refs
        else:
            x_ref, wr_ref, wi_ref, a_ref, s_ref, p_ref, c_ref, b_scr, l_scr = refs
        t_blk = pl.program_id(1)
        ar, ai = a_ref[0], a_ref[1]

        @pl.when(t_blk == 0)
        def _():
            def pstep(i, carry):
                pr, pi = carry
                p_ref[0, pl.ds(i, 1), :] = pr
                p_ref[1, pl.ds(i, 1), :] = pi
                return pr * ar - pi * ai, pr * ai + pi * ar

            lax.fori_loop(0, seg, pstep, (ar, ai))
            c_ref[...] = jnp.zeros_like(c_ref)
            if with_da:
                da_ref[...] = jnp.zeros_like(da_ref)
                dw_ref[...] = jnp.zeros_like(dw_ref)
                dx_ref[...] = jnp.zeros_like(dx_ref)

        xb = x_ref[...].astype(_MXU)
        b_scr[:, :w] = jnp.dot(xb, wr_ref[...], preferred_element_type=F32)
        b_scr[:, w:] = jnp.dot(xb, wi_ref[...], preferred_element_type=F32)
        arb, aib = jnp.broadcast_to(ar, (SUBLANES, w)), jnp.broadcast_to(ai, (SUBLANES, w))
        zero = jnp.zeros((SUBLANES, w), F32)

        def tile(g, step):
            return pl.ds(pl.multiple_of(g * chain_rows + step * SUBLANES, SUBLANES), SUBLANES)

        def rows(g, i):
            return tile(g, seg - 1 - i if reverse else i)

        def local_step(i, carry):
            out = []
            for g in range(nch):
                sr, si = carry[2 * g], carry[2 * g + 1]
                idx = rows(g, i)
                sr, si = arb * sr - aib * si + b_scr[idx, :w], arb * si + aib * sr + b_scr[idx, w:]
                l_scr[idx, :w] = sr
                l_scr[idx, w:] = si
                out += [sr, si]
            return tuple(out)

        def unrolled(step_fn, first):
            def trip(q, carry):
                for r in range(SCAN_UNROLL):
                    carry = step_fn(first + q * SCAN_UNROLL + r, carry)
                return carry
            return trip

        ends = lax.fori_loop(0, seg // SCAN_UNROLL, unrolled(local_step, 0), (zero,) * (2 * nch))

        a_seg_r, a_seg_i = p_ref[0, seg - 1:seg, :], p_ref[1, seg - 1:seg, :]
        cr, ci = c_ref[0], c_ref[1]
        sub = lax.broadcasted_iota(jnp.int32, (SUBLANES, w), 0)
        ins = [[zero, zero] for _ in range(nch)]
        order = [(g, k) for g in range(nch) for k in range(SUBLANES)]
        for g, k in (order[::-1] if reverse else order):
            ins[g] = [jnp.where(sub == k, cr, ins[g][0]), jnp.where(sub == k, ci, ins[g][1])]
            er, ei = ends[2 * g][k:k + 1], ends[2 * g + 1][k:k + 1]
            cr, ci = er + a_seg_r * cr - a_seg_i * ci, ei + a_seg_r * ci + a_seg_i * cr
        c_ref[0] = cr
        c_ref[1] = ci

        def fix(g, i):
            idx = rows(g, i)
            pr, pi = p_ref[0, pl.ds(i, 1), :], p_ref[1, pl.ds(i, 1), :]
            sr = l_scr[idx, :w] + pr * ins[g][0] - pi * ins[g][1]
            si = l_scr[idx, w:] + pr * ins[g][1] + pi * ins[g][0]
            s_ref[idx, :w] = sr.astype(s_ref.dtype)
            s_ref[idx, w:] = si.astype(s_ref.dtype)
            return sr, si

        if not with_da:
            def fix_step(i, carry):
                for g in range(nch):
                    fix(g, i)
                return carry

            lax.fori_loop(0, seg // SCAN_UNROLL, unrolled(fix_step, 0), 0)
        else:
            def adj_step(i, acc):
                acc_r, acc_i = acc
                for g in range(nch):
                    lr, li = fix(g, i)
                    prev = tile(g, seg - 2 - i)
                    fr, fi = sf_ref[prev, :w].astype(F32), sf_ref[prev, w:].astype(F32)
                    acc_r, acc_i = acc_r + lr * fr + li * fi, acc_i + li * fr - lr * fi
                return acc_r, acc_i

            acc = lax.fori_loop(0, seg // SCAN_UNROLL - 1, unrolled(adj_step, 0), (zero, zero))
            for i in range(seg - SCAN_UNROLL, seg - 1):
                acc = adj_step(i, acc)
            acc_r, acc_i = acc
            first_block = tt(t_blk) == 0
            for g in range(nch):
                lr, li = fix(g, seg - 1)
                seg_ends = tile(g, seg - 1)
                if g == 0:
                    pvr = jnp.where(first_block, 0.0, sp_ref[SUBLANES - 1:SUBLANES, :w].astype(F32))
                    pvi = jnp.where(first_block, 0.0, sp_ref[SUBLANES - 1:SUBLANES, w:].astype(F32))
                else:
                    pvr = sf_ref[g * chain_rows - 1:g * chain_rows, :w].astype(F32)
                    pvi = sf_ref[g * chain_rows - 1:g * chain_rows, w:].astype(F32)
                fr = jnp.where(sub == 0, pvr, pltpu.roll(sf_ref[seg_ends, :w].astype(F32), 1, 0))
                fi = jnp.where(sub == 0, pvi, pltpu.roll(sf_ref[seg_ends, w:].astype(F32), 1, 0))
                acc_r = acc_r + lr * fr + li * fi
                acc_i = acc_i + li * fr - lr * fi
            da_ref[0] += jnp.sum(acc_r, axis=0, keepdims=True)
            da_ref[1] += jnp.sum(acc_i, axis=0, keepdims=True)
            dw_ref[...] += _tn_dot(u_ref[...], s_ref[...])
            dx_ref[...] += _tn_dot(xb, sf_ref[...])

    x_spec = pl.BlockSpec((tb, bd_w), lambda j, t: (tt(t), j // tiles_per_bd))
    w_spec = pl.BlockSpec((bd_w, w), lambda j, t: (j // tiles_per_bd, j))
    d_spec = pl.BlockSpec((bd_w, 2 * w), lambda j, t: (j // tiles_per_bd, j % tiles_per_bd))
    a_spec = pl.BlockSpec((2, 1, w), lambda j, t: (0, 0, j))
    s_spec = pl.BlockSpec((tb, 2 * w), lambda j, t: (tt(t), j))
    in_specs, args = [x_spec, w_spec, w_spec, a_spec], [x, w_re, w_im, a_pair]
    out_specs, out_shape = [s_spec], [jax.ShapeDtypeStruct((l, 2 * SSM_S), BF16)]
    scratch = [pltpu.VMEM((2, seg, w), F32), pltpu.VMEM((2, 1, w), F32)] + [pltpu.VMEM((tb, 2 * w), F32)] * 2
    if with_da:
        in_specs += [s_spec, pl.BlockSpec((SUBLANES, 2 * w),
                                          lambda j, t: (jnp.maximum(tt(t) * (tb // SUBLANES) - 1, 0), j)),
                     x_spec]
        args += [s_fwd, s_fwd, u]
        out_specs += [a_spec, d_spec, d_spec]
        out_shape += ([jax.ShapeDtypeStruct((2, 1, SSM_S), F32)]
                      + [jax.ShapeDtypeStruct((SSM_W, 2 * SSM_S // SSM_BD), F32)] * 2)
    res = pl.pallas_call(
        body, grid=(SSM_S // w, nt), in_specs=in_specs, out_specs=out_specs, out_shape=out_shape,
        scratch_shapes=scratch, compiler_params=_params(("parallel", "arbitrary")), name=name,
    )(*args)
    return res if with_da else res[0]


def _nt_dot(x, y):
    return lax.dot_general(x.astype(_MXU), y.astype(_MXU), (((1,), (1,)), ((), ())), preferred_element_type=F32)


def _tn_dot(x, y):
    return lax.dot_general(x.astype(_MXU), y.astype(_MXU), (((0,), (0,)), ((), ())), preferred_element_type=F32)


def _nn_dot(x, y):
    return jnp.dot(x.astype(_MXU), y.astype(_MXU), preferred_element_type=F32)


def _attn_mask2(gb, nb):
    qi = lax.broadcasted_iota(jnp.int32, (ATT_WIN, 2 * ATT_WIN), 0)
    c = lax.broadcasted_iota(jnp.int32, (ATT_WIN, 2 * ATT_WIN), 1)
    has_prev = (gb % nb) != 0
    prev_ok = jnp.logical_and(jnp.logical_and(c < ATT_WIN, c >= qi), has_prev)
    own_ok = jnp.logical_and(c >= ATT_WIN, c - ATT_WIN <= qi)
    return jnp.logical_or(prev_ok, own_ok)


def _attn_specs(qb):
    cur = pl.BlockSpec((qb * ATT_WIN, ATT_GW), lambda i: (i, 0))
    prev = pl.BlockSpec((ATT_WIN, ATT_GW), lambda i: (jnp.maximum(qb * i - 1, 0), 0))
    return cur, prev


def _attn_fwd(q, k, v, nb, name):
    l = q.shape[0]
    scale = ATT_E ** -0.5
    w = ATT_WIN

    qb = ATT_QB_FWD

    def body(q_ref, kc_ref, kp_ref, vc_ref, vp_ref, o_ref, lse_ref):
        i = pl.program_id(0)
        masks = [_attn_mask2(qb * i + b, nb) for b in range(qb)]
        for h in range(ATT_HG):
            sl = slice(h * ATT_E, (h + 1) * ATT_E)
            k_ext = jnp.concatenate([kp_ref[:, sl], kc_ref[:, sl]], axis=0)
            v_ext = jnp.concatenate([vp_ref[:, sl], vc_ref[:, sl]], axis=0)
            for b in range(qb):
                r, kr = slice(b * w, (b + 1) * w), slice(b * w, (b + 2) * w)
                s = jnp.where(masks[b], _nt_dot(q_ref[r, sl], k_ext[kr]) * scale, NEG_INF)
                mx = jnp.max(s, axis=-1, keepdims=True)
                p = jnp.exp(s - mx)
                den = jnp.sum(p, axis=-1, keepdims=True)
                o_ref[r, sl] = _nn_dot(p, v_ext[kr]) / den
                lse_ref[r, sl] = jnp.broadcast_to(mx + jnp.log(den), (w, ATT_E))

    cur, prev = _attn_specs(qb)
    return pl.pallas_call(
        body, grid=(l // (qb * w),), in_specs=[cur, cur, prev, cur, prev], out_specs=[cur, cur],
        out_shape=[jax.ShapeDtypeStruct((l, ATT_GW), F32)] * 2,
        compiler_params=_params(("parallel",)), name=name,
    )(q, k, k, v, v)


def _attn_bwd(q, k, v, do, lse, dd, nb, name):
    l = q.shape[0]
    scale = ATT_E ** -0.5
    w = ATT_WIN
    nblk = l // w

    def body(q_ref, kc_ref, kp_ref, vc_ref, vp_ref, do_ref, lse_ref, dd_ref, qn_ref, don_ref, lsen_ref, ddn_ref,
             dq_ref, dk_ref, dv_ref, dk_acc, dv_acc):
        i = pl.program_id(0)
        masks = [_attn_mask2(ATT_QB * i + b, nb) for b in range(ATT_QB)]
        nxt = ATT_QB * (i + 1)
        nxt_attends = jnp.logical_and(nxt < nblk, (nxt % nb) != 0)
        qi = lax.broadcasted_iota(jnp.int32, (w, w), 0)
        kj = lax.broadcasted_iota(jnp.int32, (w, w), 1)
        mask_n = jnp.logical_and(kj >= qi, nxt_attends)
        dk_acc[...] = jnp.zeros_like(dk_acc)
        dv_acc[...] = jnp.zeros_like(dv_acc)
        for h in range(ATT_HG):
            sl, col = slice(h * ATT_E, (h + 1) * ATT_E), slice(h * ATT_E, h * ATT_E + 1)
            k_ext = jnp.concatenate([kp_ref[:, sl], kc_ref[:, sl]], axis=0)
            v_ext = jnp.concatenate([vp_ref[:, sl], vc_ref[:, sl]], axis=0)
            for b in range(ATT_QB):
                r, kr = slice(b * w, (b + 1) * w), slice(b * w, (b + 2) * w)
                qh, doh, k2, v2 = q_ref[r, sl], do_ref[r, sl], k_ext[kr], v_ext[kr]
                p = jnp.where(masks[b], jnp.exp(_nt_dot(qh, k2) * scale - lse_ref[r, col]), 0.0)
                ds = p * (_nt_dot(doh, v2) - dd_ref[r, col]) * scale
                dq_ref[r, sl] = _nn_dot(ds, k2).astype(dq_ref.dtype)
                dk2, dv2 = _tn_dot(ds, qh), _tn_dot(p, doh)
                dk_acc[r, sl] += dk2[w:]
                dv_acc[r, sl] += dv2[w:]
                if b > 0:
                    rp = slice((b - 1) * w, b * w)
                    dk_acc[rp, sl] += dk2[:w]
                    dv_acc[rp, sl] += dv2[:w]
            last = slice((ATT_QB - 1) * w, ATT_QB * w)
            kl, vl, qn, don = kc_ref[last, sl], vc_ref[last, sl], qn_ref[:, sl], don_ref[:, sl]
            pn = jnp.where(mask_n, jnp.exp(_nt_dot(qn, kl) * scale - lsen_ref[:, col]), 0.0)
            dsn = pn * (_nt_dot(don, vl) - ddn_ref[:, col]) * scale
            dk_acc[last, sl] += _tn_dot(dsn, qn)
            dv_acc[last, sl] += _tn_dot(pn, don)
        dk_ref[...] = dk_acc[...].astype(dk_ref.dtype)
        dv_ref[...] = dv_acc[...].astype(dv_ref.dtype)

    cur, prev = _attn_specs(ATT_QB)
    nxt_spec = pl.BlockSpec((w, ATT_GW), lambda i: (jnp.minimum(ATT_QB * (i + 1), nblk - 1), 0))
    return pl.pallas_call(
        body, grid=(l // (ATT_QB * w),),
        in_specs=[cur, cur, prev, cur, prev, cur, cur, cur, nxt_spec, nxt_spec, nxt_spec, nxt_spec],
        out_specs=[cur] * 3, out_shape=[jax.ShapeDtypeStruct((l, ATT_GW), BF16)] * 3,
        scratch_shapes=[pltpu.VMEM((ATT_QB * w, ATT_GW), F32)] * 2,
        compiler_params=_params(("parallel",)), name=name,
    )(q, k, k, v, v, do, lse, dd, q, do, lse, dd)


def _to_perm(a, d):
    if d == 1:
        return a
    l, c = a.shape
    return a.reshape(l // d, d, c).transpose(1, 0, 2).reshape(l, c)


def _from_perm(a, d):
    if d == 1:
        return a
    l, c = a.shape
    return a.reshape(d, l // d, c).transpose(1, 0, 2).reshape(l, c)


def _mem_probs(qh, kh):
    s = _nt_dot(qh, kh) * (MEM_E ** -0.5)
    e = jnp.exp(s - jnp.max(s, axis=-1, keepdims=True))
    return e / jnp.sum(e, axis=-1, keepdims=True)


def _mem_fwd(mq, kv, name, tm=512):
    l, nm = mq.shape[0], kv.shape[0]

    def body(q_ref, kv_ref, o_ref):
        for h in range(MEM_H):
            sl = slice(h * MEM_E, (h + 1) * MEM_E)
            p = _mem_probs(q_ref[:, sl], kv_ref[:, sl])
            o_ref[:, sl] = _nn_dot(p, kv_ref[:, MEM_W + h * MEM_E:MEM_W + (h + 1) * MEM_E]).astype(o_ref.dtype)

    return pl.pallas_call(
        body, grid=(l // tm,),
        in_specs=[pl.BlockSpec((tm, MEM_W), lambda i: (i, 0)), pl.BlockSpec((nm, 2 * MEM_W), lambda i: (0, 0))],
        out_specs=pl.BlockSpec((tm, MEM_W), lambda i: (i, 0)),
        out_shape=jax.ShapeDtypeStruct((l, MEM_W), BF16),
        compiler_params=_params(("parallel",)), name=name,
    )(mq, kv)


def _mem_bwd(mq, kv, dmo, name, tm=512):
    l, nm = mq.shape[0], kv.shape[0]
    scale = MEM_E ** -0.5

    def body(q_ref, kv_ref, do_ref, dq_ref, dkv_ref):
        @pl.when(pl.program_id(0) == 0)
        def _():
            dkv_ref[...] = jnp.zeros_like(dkv_ref)

        for h in range(MEM_H):
            sl = slice(h * MEM_E, (h + 1) * MEM_E)
            vsl = slice(MEM_W + h * MEM_E, MEM_W + (h + 1) * MEM_E)
            qh, kh, vh, doh = q_ref[:, sl], kv_ref[:, sl], kv_ref[:, vsl], do_ref[:, sl]
            p = _mem_probs(qh, kh)
            dp = _nt_dot(doh, vh)
            ds = p * (dp - jnp.sum(dp * p, axis=-1, keepdims=True)) * scale
            dq_ref[:, sl] = _nn_dot(ds, kh).astype(dq_ref.dtype)
            dkv_ref[:, sl] += _tn_dot(ds, qh)
            dkv_ref[:, vsl] += _tn_dot(p, doh)

    row = pl.BlockSpec((tm, MEM_W), lambda i: (i, 0))
    full = pl.BlockSpec((nm, 2 * MEM_W), lambda i: (0, 0))
    return pl.pallas_call(
        body, grid=(l // tm,), in_specs=[row, full, row], out_specs=[row, full],
        out_shape=[jax.ShapeDtypeStruct((l, MEM_W), BF16), jax.ShapeDtypeStruct((nm, 2 * MEM_W), F32)],
        compiler_params=_params(("arbitrary",)), name=name,
    )(mq, kv, dmo)


def _gated_out_proj(zg, branches, b_gate, w_o, x, g2, name, tm=512):
    l, d = x.shape
    nbr = len(branches)

    def body(zg_ref, *rest):
        br_refs, (bg_ref, w_ref, x_ref, g2_ref, m_ref, h_ref, n_ref) = rest[:nbr], rest[nbr:]
        merged = jnp.zeros((tm, d), F32)
        for i, br_ref in enumerate(br_refs):
            cols = slice(i * d, (i + 1) * d)
            merged += _sigmoid(zg_ref[:, cols].astype(F32) + bg_ref[:, cols]) * br_ref[...].astype(F32)
        mb = merged.astype(BF16)
        m_ref[...] = mb
        hv = jnp.dot(mb.astype(_MXU), w_ref[...].astype(_MXU), preferred_element_type=F32) + x_ref[...]
        h_ref[...] = hv
        rs = lax.rsqrt(jnp.mean(hv * hv, axis=-1, keepdims=True) + RMS_EPS)
        n_ref[...] = (hv * rs * g2_ref[...]).astype(n_ref.dtype)

    row = lambda c: pl.BlockSpec((tm, c), lambda i: (i, 0))
    full = lambda a: pl.BlockSpec(a.shape, lambda i: (0, 0))
    return pl.pallas_call(
        body, grid=(l // tm,),
        in_specs=[row(nbr * d)] + [row(d)] * nbr + [full(b_gate), full(w_o), row(d), full(g2)],
        out_specs=[row(d)] * 3,
        out_shape=[jax.ShapeDtypeStruct((l, d), BF16), jax.ShapeDtypeStruct((l, d), F32),
                   jax.ShapeDtypeStruct((l, d), BF16)],
        compiler_params=_params(("parallel",)), name=name,
    )(zg, *branches, b_gate, w_o, x, g2)


def _discretize(lam_re, lam_im, log_dt, b_re, b_im):
    dt = jnp.exp(log_dt)[:, None]
    mag = jnp.exp(lam_re * dt)
    a_re, a_im = mag * jnp.cos(lam_im * dt), mag * jnp.sin(lam_im * dt)
    nr, ni = a_re - 1.0, a_im
    den = lam_re * lam_re + lam_im * lam_im
    coef_re = (nr * lam_re + ni * lam_im) / den
    coef_im = (ni * lam_re - nr * lam_im) / den
    bb_re = coef_re[..., None] * b_re - coef_im[..., None] * b_im
    bb_im = coef_re[..., None] * b_im + coef_im[..., None] * b_re
    return a_re, a_im, bb_re, bb_im


def _tiled(re, im):
    r = re.shape[0]
    both = jnp.concatenate([re.reshape(r, -1, SCAN_W), im.reshape(r, -1, SCAN_W)], axis=2)
    return both.reshape(r, 2 * re.shape[1])


def _untiled(x):
    r = x.shape[0]
    t = x.reshape(r, -1, 2 * SCAN_W)
    return t[:, :, :SCAN_W].reshape(r, -1), t[:, :, SCAN_W:].reshape(r, -1)


def _bd_in(bb):
    return jnp.einsum("gph,gk->ghkp", bb, jnp.eye(SSM_G, dtype=bb.dtype)).reshape(SSM_W, SSM_S)


def _bd_diag(x):
    gb = SSM_G // SSM_BD
    t = x.reshape(SSM_BD, gb, SSM_H, gb, SSM_P)
    return jnp.einsum("bghgp->bghp", t).reshape(SSM_G, SSM_H, SSM_P)


_ANY = pl.BlockSpec(memory_space=pl.ANY)
_MESH = pl.DeviceIdType.MESH


def _allgather(x, name):
    def body(x_ref, out_ref, send_sems, recv_sems, local_sem):
        mx, my, mc = lax.axis_index("x"), lax.axis_index("y"), lax.axis_index("c")
        me, sibling = (mx, my, mc), (mx, my, 1 - mc)
        chips = [(1 - mx, my), (mx, 1 - my), (1 - mx, 1 - my)]

        def blk(px, py, pc):
            return out_ref.at[4 * px + 2 * py + pc]

        def copy(k, block, to, src=None):
            return pltpu.make_async_remote_copy(
                src_ref=blk(*block) if src is None else src, dst_ref=blk(*block),
                send_sem=send_sems.at[k], recv_sem=recv_sems.at[k], device_id=to, device_id_type=_MESH)

        mine = pltpu.make_async_copy(x_ref, blk(*me), local_sem)
        mine.start()
        first = [copy(0, me, sibling, src=x_ref)]
        first += [copy(1 + j, me, (*chip, mc), src=x_ref) for j, chip in enumerate(chips)]
        for cp in first:
            cp.start()
        passed = [copy(4 + j, (*chip, mc), sibling) for j, chip in enumerate(chips)]
        for j, chip in enumerate(chips):
            copy(1 + j, (*chip, mc), me).wait_recv()
            passed[j].start()
        copy(0, sibling, me).wait_recv()
        for j, chip in enumerate(chips):
            copy(4 + j, (*chip, 1 - mc), me).wait_recv()
        for cp in first + passed:
            cp.wait_send()
        mine.wait()

    return pl.pallas_call(
        body, out_shape=jax.ShapeDtypeStruct((N_DEV,) + x.shape, x.dtype), in_specs=[_ANY], out_specs=_ANY,
        scratch_shapes=[pltpu.SemaphoreType.DMA((7,)), pltpu.SemaphoreType.DMA((7,)), pltpu.SemaphoreType.DMA],
        name=name,
    )(x)


def _pair_exchange(g, name):
    def body(g_ref, out_ref, send_sems, recv_sems):
        mx, my, mc = lax.axis_index("x"), lax.axis_index("y"), lax.axis_index("c")
        copies = [pltpu.make_async_remote_copy(
            src_ref=g_ref.at[2 * k + (1 - mc)], dst_ref=out_ref.at[k], send_sem=send_sems.at[k],
            recv_sem=recv_sems.at[k], device_id=(mx, my, 1 - mc), device_id_type=_MESH) for k in range(4)]
        for cp in copies:
            cp.start()
        for cp in copies:
            cp.wait()

    return pl.pallas_call(
        body, out_shape=jax.ShapeDtypeStruct((4,) + g.shape[1:], g.dtype), in_specs=[_ANY], out_specs=_ANY,
        scratch_shapes=[pltpu.SemaphoreType.DMA((4,)), pltpu.SemaphoreType.DMA((4,))], name=name,
    )(g)


_HBM = pl.BlockSpec(memory_space=pltpu.HBM)
_SEM = pl.BlockSpec(memory_space=pltpu.SEMAPHORE)
_EFFECT = pltpu.SideEffectType.DATAFLOW_SIDE_EFFECTING
_TOKEN = jax.ShapeDtypeStruct((8, 128), F32)


def _peer(rel):
    pos = (lax.axis_index("x"), lax.axis_index("y"), lax.axis_index("c"))
    return tuple(1 - p if (rel >> (2 - i)) & 1 else p for i, p in enumerate(pos))


def _index_of(dev):
    return 4 * dev[0] + 2 * dev[1] + dev[2]


def _split_copies(src_ref, land_ref, sems, plan):
    n = len(plan)
    return [pltpu.make_async_remote_copy(
        src_ref=src_ref if s is None else src_ref.at[s], dst_ref=land_ref.at[d], send_sem=sems[k],
        recv_sem=sems[n + k], device_id=peer, device_id_type=_MESH) for k, (s, d, peer) in enumerate(plan)]


def _split_start(src, n_land, plan_fn, after, name):
    blk = src.shape[-2:]
    land = lax.empty((n_land,) + blk, src.dtype)
    n = len(plan_fn())

    def body(src_ref, land_ref, after_ref, *outs):
        for cp in _split_copies(src_ref, land_ref, outs[:2 * n], plan_fn()):
            cp.start()
        outs[2 * n + 2][...] = jnp.zeros_like(outs[2 * n + 2])

    res = pl.pallas_call(
        body, name=name,
        out_shape=(pltpu.SemaphoreType.DMA(()),) * (2 * n)
        + (pltpu.HBM(src.shape, src.dtype), pltpu.HBM(land.shape, land.dtype), _TOKEN),
        in_specs=(_HBM, _HBM, _ANY),
        out_specs=(_SEM,) * (2 * n) + (_HBM, _HBM, pl.BlockSpec(memory_space=pltpu.VMEM)),
        input_output_aliases={0: 2 * n, 1: 2 * n + 1},
        compiler_params=pltpu.CompilerParams(has_side_effects=_EFFECT),
    )(pltpu.with_memory_space_constraint(src, pltpu.HBM), pltpu.with_memory_space_constraint(land, pltpu.HBM), after)
    return res[:2 * n], res[2 * n], res[2 * n + 1], res[2 * n + 2]


def _split_wait(sems, src, land, plan_fn, after, name):
    n = len(sems) // 2

    def body(src_ref, land_ref, *rest):
        for cp in _split_copies(src_ref, land_ref, rest[:2 * n], plan_fn()):
            cp.wait_send()
            cp.wait_recv()

    return pl.pallas_call(
        body, name=name,
        out_shape=(pltpu.HBM(src.shape, src.dtype), pltpu.HBM(land.shape, land.dtype)),
        in_specs=(_HBM, _HBM) + (_SEM,) * (2 * n) + (_ANY,), out_specs=(_HBM, _HBM),
        input_output_aliases={0: 0, 1: 1},
        compiler_params=pltpu.CompilerParams(has_side_effects=_EFFECT),
    )(src, land, *sems, after)


def _gather_plan():
    me = _index_of(_peer(0))
    return [(None, me, _peer(rel)) for rel in range(1, N_DEV)]


def _gather_wait_plan():
    return [(None, _index_of(_peer(rel)), _peer(rel)) for rel in range(1, N_DEV)]


def _chip_plan():
    return [(_index_of(_peer(rel)) // 2, j, _peer(rel)) for j, rel in enumerate((4, 2, 6))]


def _owner_plan():
    return [(_index_of(_peer(rel)), rel - 1, _peer(rel)) for rel in range(1, N_DEV)]


def _pair_sum(g, t1, my_c, name, tr):
    _, r, c = g.shape

    def body(c_ref, g_ref, t_ref, o_ref, ob_ref):
        s = g_ref[...] + t_ref[...]
        o_ref[...] = s
        ob_ref[...] = s.astype(BF16)

    blk = pl.BlockSpec((None, tr, c), lambda k, i, cr: (k, i, 0))
    return pl.pallas_call(
        body,
        grid_spec=pltpu.PrefetchScalarGridSpec(
            num_scalar_prefetch=1, grid=(4, r // tr),
            in_specs=[pl.BlockSpec((None, tr, c), lambda k, i, cr: (2 * k + cr[0], i, 0)), blk],
            out_specs=[blk, blk]),
        out_shape=[jax.ShapeDtypeStruct((4, r, c), F32), jax.ShapeDtypeStruct((4, r, c), BF16)],
        compiler_params=_params(("parallel", "parallel")), name=name,
    )(my_c, g, t1)


def _adam_math(g, w, m, v):
    m = ADAM_B1 * m + (1.0 - ADAM_B1) * g
    v = ADAM_B2 * v + (1.0 - ADAM_B2) * (g * g)
    m_hat = m / (1.0 - ADAM_B1 ** ADAM_STEP)
    v_hat = v / (1.0 - ADAM_B2 ** ADAM_STEP)
    delta = -ADAM_LR * (m_hat / (jnp.sqrt(v_hat) + ADAM_EPS) + ADAM_WD * w)
    return delta, m, v


def _grad_sum(own, own_index, recv, name, tr):
    _, r, c = own.shape
    n = recv.shape[0]

    def body(k_ref, own_ref, *rest):
        g = own_ref[...]
        for recv_ref in rest[:n]:
            g = g + recv_ref[...].astype(F32)
        rest[n][...] = g

    def slot(j):
        return pl.BlockSpec((None, tr, c), lambda i, kr: (j, i, 0))

    return pl.pallas_call(
        body,
        grid_spec=pltpu.PrefetchScalarGridSpec(
            num_scalar_prefetch=1, grid=(r // tr,),
            in_specs=[pl.BlockSpec((None, tr, c), lambda i, kr: (kr[0], i, 0))] + [slot(j) for j in range(n)],
            out_specs=pl.BlockSpec((tr, c), lambda i, kr: (i, 0))),
        out_shape=jax.ShapeDtypeStruct((r, c), F32),
        compiler_params=_params(("parallel",)), name=name,
    )(own_index, own, *([recv] * n))


def _adam_many(g, w, m, v, row_tiles, name):
    n = len(g)

    def body(*refs):
        ins, outs = refs[:4 * n], refs[4 * n:]
        for i in range(n):
            res = _adam_math(ins[i][...], ins[n + i][...], ins[2 * n + i][...], ins[3 * n + i][...])
            for kind in range(3):
                outs[kind * n + i][...] = res[kind]

    def spec(a):
        blk = (a.shape[0] // row_tiles,) + a.shape[1:]
        return pl.BlockSpec(blk, lambda t, nd=a.ndim: (t,) + (0,) * (nd - 1))

    specs = [spec(a) for a in g]
    res = pl.pallas_call(
        body, grid=(row_tiles,), in_specs=specs * 4, out_specs=specs * 3,
        out_shape=[jax.ShapeDtypeStruct(a.shape, F32) for a in g] * 3,
        compiler_params=_params(("parallel",)), name=name,
    )(*g, *w, *m, *v)
    return res[:n], res[n:2 * n], res[2 * n:]


def _sum8(g8, name):
    _, r, c = g8.shape

    def body(g_ref, o_ref):
        acc = g_ref[0]
        for j in range(1, N_DEV):
            acc = acc + g_ref[j]
        o_ref[...] = acc

    return pl.pallas_call(
        body, grid=(1,), in_specs=[pl.BlockSpec((N_DEV, r, c), lambda i: (0, 0, 0))],
        out_specs=pl.BlockSpec((r, c), lambda i: (0, 0)), out_shape=jax.ShapeDtypeStruct((r, c), F32),
        compiler_params=_params(("arbitrary",)), name=name,
    )(g8)


def _pack(arrs, pad_rows=8):
    flat = jnp.concatenate([a.reshape(-1) for a in arrs])
    n = flat.shape[0]
    q = PACK_C * pad_rows
    tot = -(-n // q) * q
    if tot != n:
        flat = jnp.concatenate([flat, jnp.zeros((tot - n,), flat.dtype)])
    return flat.reshape(tot // PACK_C, PACK_C)


def _unpack(buf, shapes):
    flat = buf.reshape(-1)
    out, off = [], 0
    for s in shapes:
        n = int(np.prod(s))
        out.append(flat[off:off + n].reshape(s))
        off += n
    return out


GROUPS = (("w_in",),
          ("w_glu", "w_ssm_br", "w_mem_br", "w_attn_br"),
          ("w_up", "w_down"),
          ("w_mem_kv", "w_o"))
GROUP_TR = (400, 384, 512, 256)
MLP_GROUP = 2
MIXER_GROUPS = (1, 3)
ATTN_BR_FOLD = 2


def _stored_shape(name):
    r, c, ax = BIG_SHAPE[name]
    rows, cols = (r // N_DEV, c) if ax == 0 else (c // N_DEV, r)
    return (rows // ATTN_BR_FOLD, cols * ATTN_BR_FOLD) if name == "w_attn_br" else (rows, cols)


def _stored(shard, name):
    a = shard[0].T if BIG_SHAPE[name][2] == 1 else shard[0]
    return a.reshape(_stored_shape(name))


def _unstored(a, name):
    r, c, ax = BIG_SHAPE[name]
    if ax == 0:
        return a.reshape(1, r // N_DEV, c)
    return a.reshape(c // N_DEV, r).T[None]


def _pack_group(d, names):
    return jnp.concatenate([_stored(d[n], n) for n in names], axis=0)


def _split_group(buf, names):
    out, off = {}, 0
    for n in names:
        rows = _stored_shape(n)[0]
        out[n] = buf[..., off:off + rows, :]
        off += rows
    return out


def _full_stored(stacked, name):
    r, c, ax = BIG_SHAPE[name]
    return stacked.reshape((r, c) if ax == 0 else (c, r))


def _stacked_stored(full, name):
    return full.reshape((N_DEV,) + _stored_shape(name))


def _gelu_parts(x):
    c0, c1 = math.sqrt(2.0 / math.pi), 0.044715
    th = jnp.tanh(c0 * (x + c1 * x * x * x))
    return th, c0, c1


def _local_step(x, mem, tgt, wb, sp, late_weights, grads_ready, small_grads_ready):
    l = x.shape[0]
    w_a, w_g = wb["w_in"][:ZA_W], wb["w_in"][ZA_W:]

    a_re, a_im, bb_re, bb_im = _discretize(sp["ssm_lambda_re"], sp["ssm_lambda_im"], sp["ssm_log_dt"],
                                           sp["ssm_b_re"], sp["ssm_b_im"])
    a_pair = jnp.stack([a_re.reshape(1, SSM_S), a_im.reshape(1, SSM_S)])
    a_conj = jnp.stack([a_re.reshape(1, SSM_S), -a_im.reshape(1, SSM_S)])
    b_re_t, b_im_t = _bd_in(bb_re).astype(BF16), _bd_in(bb_im).astype(BF16)
    c_re_t = _bd_in(sp["ssm_c_re"].transpose(0, 2, 1)).astype(BF16)
    c_im_t = (-_bd_in(sp["ssm_c_im"].transpose(0, 2, 1))).astype(BF16)
    d_row = sp["ssm_d"].reshape(1, SSM_W)

    n1 = _rms_fwd(x, sp["norm1_g"], "rms1")
    za = _mm(n1, w_a, [BF16], tb=True, name="in_proj_a", tn=1664)
    zg = _mm(n1, w_g, [BF16], tb=True, name="in_proj_g")
    for gi in MIXER_GROUPS:
        wb = {**wb, **late_weights(gi, za)}
    u = za[:, :SSM_W]
    mq = za[:, ZA_W - MEM_W:]

    u_s = _scan_order(u)
    s_all = _ssm_scan(u_s, b_re_t, b_im_t, a_pair, reverse=False, name="ssm_scan_fwd")
    ys = _time_order(_mm(s_all, _tiled(c_re_t, c_im_t), [F32], tb=True, bd=SSM_BD, tm=2048, name="ssm_cs"))

    def gelu_fn(r, b):
        y0 = r[0] + b[0] * r[1].astype(F32)
        th, _, _ = _gelu_parts(y0)
        return [y0, 0.5 * y0 * (1.0 + th)], []
    y0, y1 = _ew(gelu_fn, [ys, u], [d_row], [(SSM_W, F32), (SSM_W, BF16)], [], name="ssm_gelu", tm=2048)

    def glu_epi(acc, y1t, bg):
        t = acc + bg
        return t, y1t.astype(F32) * _sigmoid(t)
    t_glu, y2 = _mm(y1, wb["w_glu"], [F32, BF16], epi=glu_epi, mn=[y1], rows=[sp["b_glu"]], name="ssm_glu")
    br_ssm = _mm(y2, wb["w_ssm_br"], [BF16], tb=True, name="ssm_br")

    qkv_p, o_g, lse_g = [], [], []
    for g, d in enumerate(DILATIONS):
        nb = l // d // ATT_WIN
        cols = [za[:, SSM_W + (3 * j + g) * ATT_GW: SSM_W + (3 * j + g + 1) * ATT_GW] for j in range(3)]
        qp, kp, vp = [_to_perm(cc, d) for cc in cols]
        qkv_p.append((qp, kp, vp))
        og, lg = _attn_fwd(qp, kp, vp, nb, "attn_fwd%d" % g)
        o_g.append(_from_perm(og, d))
        lse_g.append(_from_perm(lg, d))

    def merge_fn(r, b):
        o0, o1, o2, l0, l1, l2 = r
        mx = jnp.maximum(jnp.maximum(l0, l1), l2)
        e0, e1, e2 = jnp.exp(l0 - mx), jnp.exp(l1 - mx), jnp.exp(l2 - mx)
        tot = e0 + e1 + e2
        return [(e0 * o0 + e1 * o1 + e2 * o2) / tot, mx + jnp.log(tot)], []
    o_att, lse_tot = _ew(merge_fn, o_g + lse_g, [], [(ATT_GW, F32), (ATT_GW, F32)], [], name="attn_merge", tm=2048)
    br_attn = _mm(o_att, wb["w_attn_br"], [BF16], tb=True, name="attn_br")

    mn = _rms_fwd(mem, sp["mem_norm_g"], "rms_mem")
    kv = _mm(mn, wb["w_mem_kv"], [BF16], name="mem_kv")
    mo = _mem_fwd(mq, kv, "mem_attn_fwd")
    br_mem = _mm(mo, wb["w_mem_br"], [BF16], tb=True, name="mem_br")

    merged, h1, n2 = _gated_out_proj(zg, [br_ssm, br_attn, br_mem], sp["b_gate"], wb["w_o"], x, sp["norm2_g"],
                                     "gated_o_proj")

    def up_epi(acc):
        ra = jnp.maximum(acc, 0.0)
        return ra * ra, ra
    wm = late_weights(MLP_GROUP, n2)
    f_act, r_act = _mm(n2, wm["w_up"], [BF16, BF16], tb=True, epi=up_epi, name="mlp_up")
    def down_epi(acc, ht, tv, gf):
        hv = acc + ht
        rs = lax.rsqrt(jnp.mean(hv * hv, axis=-1, keepdims=True) + RMS_EPS)
        err = hv * rs * gf - tv
        dh, dgf = _rms_bwd_tile(hv, err * (1.0 / D_MODEL), gf)
        return dh, dgf, _colsum(err * err) * (0.5 / D_MODEL)
    dh2, d_final_g, loss_cols = _mm(f_act, wm["w_down"], [F32], epi=down_epi, mn=[h1, tgt], rows=[sp["final_g"]],
                                    n_sums=2, tk=1024, name="mlp_down")
    loss = jnp.sum(loss_cols, axis=1, keepdims=True)

    gw, gs = {}, {"final_g": d_final_g}
    d_act = _mm(dh2, wm["w_down"], [BF16], tb=True, epi=lambda acc, ra: (acc * 2.0 * ra.astype(F32),), mn=[r_act],
                name="mlp_down_dx")
    dw_down = _mm(f_act, dh2, [F32], ta=True, name="mlp_down_dw")
    dw_up = _mm(d_act, n2, [F32], ta=True, name="mlp_up_dw")
    token = grads_ready(MLP_GROUP, {"w_up": dw_up, "w_down": dw_down})
    def up_dx_epi(acc, ht, dht, g2):
        dx, dg = _rms_bwd_tile(ht, acc, g2)
        return dx + dht, dg
    dh1, gs["norm2_g"] = _mm(d_act, wm["w_up"], [F32], epi=up_dx_epi, mn=[h1, dh2],
                             rows=[sp["norm2_g"] + token[:1, :1]], n_sums=1, tk=1024, name="mlp_up_dx")
    gw["w_o"] = _mm(merged, dh1, [F32], ta=True, name="o_proj_dw")

    def gate_bwd_epi(dm, *tiles):
        dbr, dz = [], []
        for zt, bt, bias in zip(tiles[0:3], tiles[3:6], tiles[6:9]):
            gt = _sigmoid(zt.astype(F32) + bias)
            dbr.append(dm * gt)
            dz.append(dm * bt.astype(F32) * gt * (1.0 - gt))
        return (*dbr, *dz, *[_colsum(t) for t in dz])
    gate_bias = [sp["b_gate"][:, i * D_MODEL:(i + 1) * D_MODEL] for i in range(3)]
    res = _mm(dh1, wb["w_o"], [BF16] * 6, tb=True, epi=gate_bwd_epi, mn=[(zg, 0), (zg, 1), (zg, 2), br_ssm, br_attn, br_mem],
              rows=gate_bias, n_sums=3, tm=512, name="o_proj_dx")
    (dbr_ssm, dbr_attn, dbr_mem), dzg = res[0:3], res[3:6]
    gs["b_gate"] = jnp.concatenate(res[6:9], axis=1)

    gw["w_ssm_br"] = _mm(dbr_ssm, y2, [F32], ta=True, name="ssm_br_dw")
    def glu_bwd_epi(dy, y1t, tt):
        sg = _sigmoid(tt)
        dt = dy * y1t.astype(F32) * sg * (1.0 - sg)
        return dt, dy * sg, _colsum(dt)
    dt_glu, dy1a, gs["b_glu"] = _mm(dbr_ssm, wb["w_ssm_br"], [BF16, F32], epi=glu_bwd_epi, mn=[y1, t_glu], n_sums=1,
                                    name="ssm_br_dx")
    gw["w_glu"] = _mm(y1, dt_glu, [F32], ta=True, name="ssm_glu_dw")

    def gelu_bwd_epi(acc, dy1t, y0t, ut):
        th, c0, c1 = _gelu_parts(y0t)
        dg = 0.5 * (1.0 + th) + 0.5 * y0t * (1.0 - th * th) * c0 * (1.0 + 3.0 * c1 * y0t * y0t)
        dy = (acc + dy1t) * dg
        return dy, _colsum(dy * ut.astype(F32))
    dy0, gs["ssm_d"] = _mm(dt_glu, wb["w_glu"], [F32], tb=True, epi=gelu_bwd_epi, mn=[dy1a, y0, u], n_sums=1,
                           name="ssm_glu_dx")
    dy0_s = _scan_order(dy0)
    lam, da, d_b, d_c = _ssm_scan(dy0_s, c_re_t, c_im_t, a_conj, reverse=True, s_fwd=s_all, u=u_s,
                                  name="ssm_scan_bwd")
    du = _time_order(_mm(lam, _tiled(b_re_t, b_im_t), [BF16], tb=True,
                         epi=lambda acc, dyt, dr: (acc + dyt * dr,), mn=[dy0_s], rows=[d_row], bd=SSM_BD, tm=2048, name="ssm_bu_dx"))
    gs["a_re"], gs["a_im"] = da[0], da[1]
    (dbr, dbi), (dcr, dci) = _untiled(d_b), _untiled(d_c)
    gs["bb_re"], gs["bb_im"] = _bd_diag(dbr).transpose(0, 2, 1), _bd_diag(dbi).transpose(0, 2, 1)
    gs["ssm_c_re"], gs["ssm_c_im"] = _bd_diag(dcr), -_bd_diag(dci)

    gw["w_attn_br"] = _mm(dbr_attn, o_att, [F32], ta=True, name="attn_br_dw")

    def do_epi(acc, ot):
        prod = acc * ot
        head = lax.broadcasted_iota(jnp.int32, prod.shape, 1) // ATT_E
        dd = jnp.zeros_like(prod)
        for h in range(ATT_HG):
            dd = jnp.where(head == h, jnp.sum(jnp.where(head == h, prod, 0.0), axis=1, keepdims=True), dd)
        return acc, dd
    do_att, dd_att = _mm(dbr_attn, wb["w_attn_br"], [BF16, F32], epi=do_epi, mn=[o_att], name="attn_br_dx")
    dq_l, dk_l, dv_l = [], [], []
    for g, d in enumerate(DILATIONS):
        nb = l // d // ATT_WIN
        qp, kp, vp = qkv_p[g]
        dq, dk, dv = _attn_bwd(qp, kp, vp, _to_perm(do_att, d), _to_perm(lse_tot, d), _to_perm(dd_att, d),
                               nb, "attn_bwd%d" % g)
        dq_l.append(_from_perm(dq, d))
        dk_l.append(_from_perm(dk, d))
        dv_l.append(_from_perm(dv, d))

    gw["w_mem_br"] = _mm(dbr_mem, mo, [F32], ta=True, name="mem_br_dw")
    dmo = _mm(dbr_mem, wb["w_mem_br"], [BF16], name="mem_br_dx")
    dmq, dkv = _mem_bwd(mq, kv, dmo, "mem_attn_bwd")
    gw["w_mem_kv"] = _mm(mn, dkv, [F32], ta=True, name="mem_kv_dw")
    dmn = _mm(dkv, wb["w_mem_kv"], [F32], tb=True, name="mem_kv_dx")
    token = sum(grads_ready(gi, gw) for gi in MIXER_GROUPS)
    gs["mem_norm_g"] = _rms_bwd(mem, dmn, None, sp["mem_norm_g"] + token[:1, :1], "rms_mem_bwd")[1]

    dza = jnp.concatenate([du] + dq_l + dk_l + dv_l + [dmq], axis=1)
    dn_a = _mm(dza, w_a, [F32], name="in_proj_a_dx", tk=1664)
    dw_a = _mm(dza, n1, [F32], ta=True, name="in_proj_a_dw", tm=1664)
    dw_g = [_mm(dzg[i], n1, [F32], ta=True, name="in_proj_g_dw%d" % i) for i in range(3)]
    gw["w_in"] = jnp.concatenate([dw_a] + dw_g, axis=0)
    token = grads_ready(0, gw) + small_grads_ready(gs)
    def in_dx_epi(acc, pt, xt, dht, g1):
        dx, dg = _rms_bwd_tile(xt, acc + pt, g1)
        return dx + dht, dg
    w_gs = [w_g[i * D_MODEL:(i + 1) * D_MODEL] for i in range(3)]
    grad_x, gs["norm1_g"] = _mm(dzg[0], w_gs[0], [F32], pair2=(dzg[1], w_gs[1], dzg[2], w_gs[2]), epi=in_dx_epi,
                                mn=[dn_a, x, dh1],
                                rows=[sp["norm1_g"] + token[:1, :1]], n_sums=1, tm=512, name="in_proj_g_dx")
    return loss, grad_x, gs


_SMALL_GRAD_ORDER = ("norm1_g", "mem_norm_g", "b_gate", "a_re", "a_im", "bb_re", "bb_im", "ssm_c_re", "ssm_c_im",
                     "ssm_d", "b_glu", "norm2_g", "final_g")


def kernel(x, mem, norm1_g, mem_norm_g, w_in, b_gate, ssm_lambda_re, ssm_lambda_im, ssm_log_dt, ssm_b_re, ssm_b_im, ssm_c_re, ssm_c_im, ssm_d, w_glu, b_glu, w_ssm_br, w_attn_br, w_mem_kv, w_mem_br, w_o, norm2_g, w_up, w_down, final_g, loss_target, m_norm1_g, m_mem_norm_g, m_w_in, m_b_gate, m_ssm_lambda_re, m_ssm_lambda_im, m_ssm_log_dt, m_ssm_b_re, m_ssm_b_im, m_ssm_c_re, m_ssm_c_im, m_ssm_d, m_w_glu, m_b_glu, m_w_ssm_br, m_w_attn_br, m_w_mem_kv, m_w_mem_br, m_w_o, m_norm2_g, m_w_up, m_w_down, m_final_g, v_norm1_g, v_mem_norm_g, v_w_in, v_b_gate, v_ssm_lambda_re, v_ssm_lambda_im, v_ssm_log_dt, v_ssm_b_re, v_ssm_b_im, v_ssm_c_re, v_ssm_c_im, v_ssm_d, v_w_glu, v_b_glu, v_w_ssm_br, v_w_attn_br, v_w_mem_kv, v_w_mem_br, v_w_o, v_norm2_g, v_w_up, v_w_down, v_final_g):
    args = dict(locals())
    w = {n: args[n] for n in ALL_W}
    m = {n: args["m_" + n] for n in ALL_W}
    v = {n: args["v_" + n] for n in ALL_W}
    my_c = lax.axis_index("c").astype(jnp.int32).reshape(1)
    my_chip = (2 * lax.axis_index("x") + lax.axis_index("y")).astype(jnp.int32).reshape(1)

    w_pack = [_pack_group(w, names) for names in GROUPS]
    my_index = (4 * lax.axis_index("x") + 2 * lax.axis_index("y") + lax.axis_index("c")).astype(jnp.int32)
    zero = jnp.zeros((), jnp.int32)
    w_all = _allgather(w_pack[0].astype(BF16), "allgather_weights0")
    wb = {n: _full_stored(part, n) for n, part in _split_group(w_all, GROUPS[0]).items()}
    gathers = {gi: _split_start(w_pack[gi].astype(BF16), N_DEV, _gather_plan, w_all, "weights_gather_start%d" % gi)
               for gi in range(1, len(GROUPS))}

    def gathered(started, after, name):
        sems, src, land, _ = started
        src, land = _split_wait(sems, src, land, _gather_wait_plan, after, name)
        return lax.dynamic_update_slice(land, src[None], (my_index, zero, zero))

    def late_weights(gi, after):
        full = gathered(gathers[gi], after, "weights_gather_wait%d" % gi)
        return {n: _full_stored(part, n) for n, part in _split_group(full, GROUPS[gi]).items()}

    pending = {}

    def grads_ready(gi, grads):
        g_pack = jnp.concatenate([_stacked_stored(grads[n], n) for n in GROUPS[gi]], axis=1)
        if gi == 0:
            t1 = _pair_exchange(g_pack, "grad_pair_exchange%d" % gi)
            p_sum, p_bf = _pair_sum(g_pack, t1, my_c, "grad_pair_sum%d" % gi, GROUP_TR[gi])
            started = _split_start(p_bf, 3, _chip_plan, p_sum, "grad_chip_exchange_start%d" % gi)
            pending[gi] = (p_sum, my_chip, started, _chip_plan)
        else:
            started = _split_start(g_pack.astype(BF16), N_DEV - 1, _owner_plan, g_pack, "grad_exchange_start%d" % gi)
            pending[gi] = (g_pack, my_index.reshape(1), started, _owner_plan)
        return started[3]

    early_small = [n for n in _SMALL_GRAD_ORDER if n != "norm1_g"]
    small_started = []

    def small_grads_ready(gs):
        started = _split_start(_pack([gs[n] for n in early_small]), N_DEV, _gather_plan, gs["mem_norm_g"],
                               "small_grads_gather_start")
        small_started.append((started, [gs[n].shape for n in early_small]))
        return started[3]

    sp = {
        "norm1_g": norm1_g + sum(started[3][:1, :1] for started in gathers.values()), "mem_norm_g": mem_norm_g, "b_gate": b_gate, "b_glu": b_glu, "norm2_g": norm2_g,
        "final_g": final_g.reshape(1, D_MODEL),
        "ssm_lambda_re": ssm_lambda_re[0], "ssm_lambda_im": ssm_lambda_im[0], "ssm_log_dt": ssm_log_dt[0],
        "ssm_b_re": ssm_b_re[0], "ssm_b_im": ssm_b_im[0], "ssm_c_re": ssm_c_re[0], "ssm_c_im": ssm_c_im[0],
        "ssm_d": ssm_d[0],
    }
    loss, grad_x, gs = _local_step(x[0], mem[0], loss_target[0], wb, sp, late_weights, grads_ready,
                                     small_grads_ready)
    loss = lax.psum(loss[0, 0], ("x", "y", "c"))
    n1_started = _split_start(_pack([gs["norm1_g"]]), N_DEV, _gather_plan, grad_x, "norm1_grad_gather_start")

    big_g = {}
    for gi, names in enumerate(GROUPS):
        own, own_index, (sems, src, land, _), plan = pending[gi]
        recv = _split_wait(sems, src, land, plan, grad_x, "grad_exchange_wait%d" % gi)[1]
        g_pack = _grad_sum(own, own_index, recv, "grad_sum%d" % gi, GROUP_TR[gi])
        for n, part in _split_group(g_pack, names).items():
            big_g[n] = _unstored(part, n)
    rows_of = lambda d, names: [d[n].reshape(d[n].shape[-2:]) for n in names]
    big_out = _adam_many(rows_of(big_g, BIG), rows_of(w, BIG), rows_of(m, BIG), rows_of(v, BIG), 8, "adam_big")
    big = [big_g] + [{n: a[None] for n, a in zip(BIG, outs)} for outs in big_out]

    (sg_started, sg_shapes), = small_started
    sg_all = jnp.concatenate([gathered(sg_started, big_out[0][0], "small_grads_gather_wait"),
                              gathered(n1_started, big_out[0][0], "norm1_grad_gather_wait")], axis=1)
    sg_sum = _sum8(sg_all, "sum_small_grads")
    n1_rows = n1_started[1].shape[0]
    sg = dict(zip(early_small, _unpack(sg_sum[:-n1_rows], sg_shapes)))
    sg["norm1_g"] = _unpack(sg_sum[-n1_rows:], [gs["norm1_g"].shape])[0]
    _, disc_vjp = jax.vjp(_discretize, sp["ssm_lambda_re"], sp["ssm_lambda_im"], sp["ssm_log_dt"],
                          sp["ssm_b_re"], sp["ssm_b_im"])
    d_lre, d_lim, d_ldt, d_bre, d_bim = disc_vjp((sg["a_re"].reshape(SSM_G, SSM_P), sg["a_im"].reshape(SSM_G, SSM_P),
                                                  sg["bb_re"], sg["bb_im"]))
    small_grad = {
        "norm1_g": sg["norm1_g"], "mem_norm_g": sg["mem_norm_g"], "b_gate": sg["b_gate"],
        "ssm_lambda_re": d_lre, "ssm_lambda_im": d_lim, "ssm_log_dt": d_ldt, "ssm_b_re": d_bre, "ssm_b_im": d_bim,
        "ssm_c_re": sg["ssm_c_re"], "ssm_c_im": sg["ssm_c_im"], "ssm_d": sg["ssm_d"], "b_glu": sg["b_glu"],
        "norm2_g": sg["norm2_g"], "final_g": sg["final_g"],
    }
    small_grad = {n: small_grad[n].reshape(w[n].shape) for n in SMALL}

    def squeezed(a):
        return a.reshape(a.shape[1:]) if a.ndim > 2 else a.reshape(1, -1)

    sq = lambda d: [squeezed(d[n]) for n in SMALL]
    small_out = _adam_many(sq(small_grad), sq(w), sq(m), sq(v), 1, "adam_small")
    small = [small_grad] + [{n: a.reshape(w[n].shape) for n, a in zip(SMALL, outs)} for outs in small_out]

    outs = [loss, grad_x[None]]
    for kind in range(4):
        for n in ALL_W:
            outs.append(big[kind][n] if n in BIG else small[kind][n])
    return tuple(outs)
```

```python
import math

import numpy as np
import jax
import jax.numpy as jnp
from jax import lax
from jax.experimental import pallas as pl
from jax.experimental.pallas import tpu as pltpu

F32 = jnp.float32
BF16 = jnp.bfloat16
_MXU = jnp.bfloat16

D_MODEL = 1024
SSM_G, SSM_H, SSM_P = 32, 16, 64
SSM_W = SSM_G * SSM_H
SSM_S = SSM_G * SSM_P
SSM_BD = 4
ATT_E = 64
ATT_HG = 4
ATT_GW = ATT_HG * ATT_E
ATT_WIN = 128
ATT_QB = 8
ATT_QB_FWD = 4
DILATIONS = (1, 4, 16)
MEM_H, MEM_E = 4, 128
MEM_W = MEM_H * MEM_E
ZA_W = SSM_W + 9 * ATT_GW + MEM_W
ZG_W = 3 * D_MODEL
IN_W = ZA_W + ZG_W
RMS_EPS = 1e-6
NEG_INF = -1e30

ADAM_LR, ADAM_B1, ADAM_B2, ADAM_EPS, ADAM_WD, ADAM_STEP = 0.001, 0.9, 0.999, 1e-08, 0.01, 10

N_DEV = 8
PACK_C = 512
_VMEM_LIMIT = 56 * 1024 * 1024
SUBLANES = 16
SCAN_SEG = 128
SCAN_CHAINS = 4
SCAN_UNROLL = 4
SCAN_W = 128

BIG = ("w_in", "w_glu", "w_ssm_br", "w_attn_br", "w_mem_kv", "w_mem_br", "w_o", "w_up", "w_down")
BIG_SHAPE = {
    "w_in": (D_MODEL, IN_W, 1), "w_glu": (SSM_W, SSM_W, 0), "w_ssm_br": (SSM_W, D_MODEL, 1),
    "w_attn_br": (ATT_GW, D_MODEL, 1), "w_mem_kv": (D_MODEL, 2 * MEM_W, 0), "w_mem_br": (MEM_W, D_MODEL, 1),
    "w_o": (D_MODEL, D_MODEL, 0), "w_up": (D_MODEL, 4 * D_MODEL, 1), "w_down": (4 * D_MODEL, D_MODEL, 0),
}
SMALL = ("norm1_g", "mem_norm_g", "b_gate", "ssm_lambda_re", "ssm_lambda_im", "ssm_log_dt", "ssm_b_re",
         "ssm_b_im", "ssm_c_re", "ssm_c_im", "ssm_d", "b_glu", "norm2_g", "final_g")
ALL_W = ("norm1_g", "mem_norm_g", "w_in", "b_gate", "ssm_lambda_re", "ssm_lambda_im", "ssm_log_dt", "ssm_b_re",
         "ssm_b_im", "ssm_c_re", "ssm_c_im", "ssm_d", "w_glu", "b_glu", "w_ssm_br", "w_attn_br", "w_mem_kv",
         "w_mem_br", "w_o", "norm2_g", "w_up", "w_down", "final_g")


def _params(sem):
    return pltpu.CompilerParams(dimension_semantics=sem, vmem_limit_bytes=_VMEM_LIMIT)


def _pick(n, cap):
    if n <= cap:
        return n
    t = (cap // 128) * 128
    while n % t:
        t -= 128
    return t


def _mm(a, b, outs, *, name, ta=False, tb=False, epi=None, mn=(), rows=(), pair2=None, bd=0, n_sums=0,
        tm=1024, tn=1024, tk=2048):
    ab = [a, b] + (list(pair2) if pair2 is not None else [])
    a_shape, b_shape = ab[0].shape, ab[1].shape
    m = a_shape[1] if ta else a_shape[0]
    k = a_shape[0] if ta else a_shape[1]
    n = b_shape[0] if tb else b_shape[1]
    assert k == (b_shape[1] if tb else b_shape[0]), (name, a_shape, b_shape)
    out_n = n
    if bd and ta:
        assert not tb
        tm, tn, tk = m // bd, n // bd, _pick(k, tk)
        grid, out_n = (bd, 1, k // tk), tn
        a_blk = ((tk, tm), lambda i, j, kk: (kk, i))
        b_blk = ((tk, tn), lambda i, j, kk: (kk, i))
        mn_spec = pl.BlockSpec((tm, tn), lambda i, j, kk: (i, 0))
    elif bd:
        tm, tn, tk = _pick(m, tm), n // bd, k // bd
        grid = (m // tm, bd, 1)
        a_blk = ((tm, tk), lambda i, j, kk: (i, j))
        b_blk = ((tn, tk) if tb else (tk, tn), lambda i, j, kk: (j, j))
        mn_spec = pl.BlockSpec((tm, tn), lambda i, j, kk: (i, j))
    else:
        tm, tn, tk = _pick(m, tm), _pick(n, tn), _pick(k, tk)
        grid = (m // tm, n // tn, k // tk)
        a_blk = ((tk, tm), lambda i, j, kk: (kk, i)) if ta else ((tm, tk), lambda i, j, kk: (i, kk))
        b_blk = ((tn, tk), lambda i, j, kk: (j, kk)) if tb else ((tk, tn), lambda i, j, kk: (kk, j))
        mn_spec = pl.BlockSpec((tm, tn), lambda i, j, kk: (i, j))

    ab_specs = [pl.BlockSpec(*(a_blk if q % 2 == 0 else b_blk)) for q in range(len(ab))]
    mn_arrays = [e[0] if isinstance(e, tuple) else e for e in mn]
    mn_specs = [pl.BlockSpec((tm, tn), lambda i, j, kk, c=e[1]: (i, c)) if isinstance(e, tuple) else mn_spec
                for e in mn]
    nk = grid[2]
    row_spec = pl.BlockSpec((1, tn), lambda i, j, kk: (0, j))
    n_ex, n_out = len(mn) + len(rows), len(outs)
    assert n_sums == 0 or (grid[1] == 1 and not bd)
    dims = (((0 if ta else 1,), (1 if tb else 0,)), ((), ()))

    def body(*refs):
        ab_refs, rest = refs[:len(ab)], refs[len(ab):]
        ex, o_refs = rest[:n_ex], rest[n_ex:n_ex + n_out]
        s_refs = rest[n_ex + n_out:n_ex + n_out + n_sums]
        first_row_tile = pl.program_id(0) == 0
        kk = pl.program_id(2)

        pairs = list(zip(ab_refs[0::2], ab_refs[1::2]))

        def product(pair):
            return lax.dot_general(pair[0][...].astype(_MXU), pair[1][...].astype(_MXU), dims,
                                   preferred_element_type=F32)

        def finish(total):
            vals = (total,) if epi is None else epi(total, *[r[...] for r in ex])
            for r, v in zip(o_refs, vals):
                r[...] = v.astype(r.dtype)
            for r, v in zip(s_refs, vals[n_out:]):
                r[...] = jnp.where(first_row_tile, v, r[...] + v)

        if nk == 1:
            total = product(pairs[0])
            for pair in pairs[1:]:
                total = total + product(pair)
            finish(total)
        else:
            acc = rest[-1]

            @pl.when(kk == 0)
            def _():
                acc[...] = jnp.zeros_like(acc)

            for pair in pairs:
                acc[...] += product(pair)

            @pl.when(kk == nk - 1)
            def _():
                finish(acc[...])

    res = pl.pallas_call(
        body, grid=grid,
        in_specs=ab_specs + mn_specs + [row_spec] * len(rows),
        out_specs=[mn_spec] * n_out + [row_spec] * n_sums,
        out_shape=[jax.ShapeDtypeStruct((m, out_n), dt) for dt in outs]
        + [jax.ShapeDtypeStruct((1, out_n), F32)] * n_sums,
        scratch_shapes=[pltpu.VMEM((tm, tn), F32)] if nk > 1 else [],
        compiler_params=_params(("arbitrary" if n_sums else "parallel", "parallel", "arbitrary")), name=name,
    )(*ab, *mn_arrays, *rows)
    return res[0] if n_out + n_sums == 1 else res


def _ew(fn, rows, bcs, out_rows, out_accs, *, name, tm=256):
    r = rows[0].shape[0]
    tm = min(tm, r)
    assert r % tm == 0
    nr, nb, no, na = len(rows), len(bcs), len(out_rows), len(out_accs)

    def body(*refs):
        i = pl.program_id(0)
        r_in, b_in = refs[:nr], refs[nr:nr + nb]
        o_r, o_a = refs[nr + nb:nr + nb + no], refs[nr + nb + no:]
        outs, accs = fn([x[...] for x in r_in], [x[...] for x in b_in])
        for ref, v in zip(o_r, outs):
            ref[...] = v.astype(ref.dtype)
        if na:
            @pl.when(i == 0)
            def _():
                for ref in o_a:
                    ref[...] = jnp.zeros_like(ref)

            for ref, v in zip(o_a, accs):
                ref[...] += v

    res = pl.pallas_call(
        body, grid=(r // tm,),
        in_specs=[pl.BlockSpec((tm, x.shape[1]), lambda i: (i, 0)) for x in rows]
        + [pl.BlockSpec((1, x.shape[1]), lambda i: (0, 0)) for x in bcs],
        out_specs=[pl.BlockSpec((tm, c), lambda i: (i, 0)) for c, _ in out_rows]
        + [pl.BlockSpec((1, c), lambda i: (0, 0)) for c in out_accs],
        out_shape=[jax.ShapeDtypeStruct((r, c), dt) for c, dt in out_rows]
        + [jax.ShapeDtypeStruct((1, c), F32) for c in out_accs],
        compiler_params=_params(("arbitrary",)), name=name,
    )(*rows, *bcs)
    return res


def _colsum(x):
    return jnp.sum(x, axis=0, keepdims=True)


def _sigmoid(x):
    return 1.0 / (1.0 + jnp.exp(-x))


def _rms_bwd_tile(xv, dv, g):
    rs = lax.rsqrt(jnp.mean(xv * xv, axis=-1, keepdims=True) + RMS_EPS)
    gd = dv * g
    dx = rs * gd - xv * (rs * rs * rs) * jnp.mean(gd * xv, axis=-1, keepdims=True)
    return dx, _colsum(dv * xv * rs)


def _rms_fwd(x, g, name):
    def fn(r, b):
        xv = r[0]
        rs = lax.rsqrt(jnp.mean(xv * xv, axis=-1, keepdims=True) + RMS_EPS)
        return [xv * rs * b[0]], []
    return _ew(fn, [x], [g], [(x.shape[1], BF16)], [], name=name, tm=1024)[0]


def _rms_bwd(x, dn, res, g, name):
    def fn(r, b):
        dx, dg = _rms_bwd_tile(r[0], r[1], b[0])
        if res is not None:
            dx = dx + r[2]
        return [dx], [dg]
    rows = [x, dn] + ([res] if res is not None else [])
    return _ew(fn, rows, [g], [(x.shape[1], F32)], [x.shape[1]], name=name)


def _scan_order(x):
    l, c = x.shape
    return x.reshape(l // (SUBLANES * SCAN_SEG), SUBLANES, SCAN_SEG, c).transpose(0, 2, 1, 3).reshape(l, c)


def _time_order(x):
    l, c = x.shape
    return x.reshape(l // (SUBLANES * SCAN_SEG), SCAN_SEG, SUBLANES, c).transpose(0, 2, 1, 3).reshape(l, c)


def _ssm_scan(x, w_re, w_im, a_pair, *, reverse, s_fwd=None, u=None, name):
    l = x.shape[0]
    seg, w = SCAN_SEG, SCAN_W
    bd_w = SSM_W // SSM_BD
    tiles_per_bd = SSM_S // SSM_BD // w
    nch = min(SCAN_CHAINS, l // (SUBLANES * seg))
    chain_rows = SUBLANES * seg
    tb = nch * chain_rows
    nt = l // tb
    with_da = s_fwd is not None
    assert reverse or not with_da

    def tt(t):
        return nt - 1 - t if reverse else t

    def body(*refs):
        if with_da:
            (x_ref, wr_ref, wi_ref, a_ref, sf_ref, sp_ref, u_ref, s_ref, da_ref, dw_ref, dx_ref,
             p_ref, c_ref, b_scr, l_scr) = refs
        else:
            x_ref, wr_ref, wi_ref, a_ref, s_ref, p_ref, c_ref, b_scr, l_scr = refs
        t_blk = pl.program_id(1)
        ar, ai = a_ref[0], a_ref[1]

        @pl.when(t_blk == 0)
        def _():
            def pstep(i, carry):
                pr, pi = carry
                p_ref[0, pl.ds(i, 1), :] = pr
                p_ref[1, pl.ds(i, 1), :] = pi
                return pr * ar - pi * ai, pr * ai + pi * ar

            lax.fori_loop(0, seg, pstep, (ar, ai))
            c_ref[...] = jnp.zeros_like(c_ref)
            if with_da:
                da_ref[...] = jnp.zeros_like(da_ref)
                dw_ref[...] = jnp.zeros_like(dw_ref)
                dx_ref[...] = jnp.zeros_like(dx_ref)

        xb = x_ref[...].astype(_MXU)
        b_scr[:, :w] = jnp.dot(xb, wr_ref[...], preferred_element_type=F32)
        b_scr[:, w:] = jnp.dot(xb, wi_ref[...], preferred_element_type=F32)
        arb, aib = jnp.broadcast_to(ar, (SUBLANES, w)), jnp.broadcast_to(ai, (SUBLANES, w))
        zero = jnp.zeros((SUBLANES, w), F32)

        def tile(g, step):
            return pl.ds(pl.multiple_of(g * chain_rows + step * SUBLANES, SUBLANES), SUBLANES)

        def rows(g, i):
            return tile(g, seg - 1 - i if reverse else i)

        def local_step(i, carry):
            out = []
            for g in range(nch):
                sr, si = carry[2 * g], carry[2 * g + 1]
                idx = rows(g, i)
                sr, si = arb * sr - aib * si + b_scr[idx, :w], arb * si + aib * sr + b_scr[idx, w:]
                l_scr[idx, :w] = sr
                l_scr[idx, w:] = si
                out += [sr, si]
            return tuple(out)

        def unrolled(step_fn, first):
            def trip(q, carry):
                for r in range(SCAN_UNROLL):
                    carry = step_fn(first + q * SCAN_UNROLL + r, carry)
                return carry
            return trip

        ends = lax.fori_loop(0, seg // SCAN_UNROLL, unrolled(local_step, 0), (zero,) * (2 * nch))

        a_seg_r, a_seg_i = p_ref[0, seg - 1:seg, :], p_ref[1, seg - 1:seg, :]
        cr, ci = c_ref[0], c_ref[1]
        sub = lax.broadcasted_iota(jnp.int32, (SUBLANES, w), 0)
        ins = [[zero, zero] for _ in range(nch)]
        order = [(g, k) for g in range(nch) for k in range(SUBLANES)]
        for g, k in (order[::-1] if reverse else order):
            ins[g] = [jnp.where(sub == k, cr, ins[g][0]), jnp.where(sub == k, ci, ins[g][1])]
            er, ei = ends[2 * g][k:k + 1], ends[2 * g + 1][k:k + 1]
            cr, ci = er + a_seg_r * cr - a_seg_i * ci, ei + a_seg_r * ci + a_seg_i * cr
        c_ref[0] = cr
        c_ref[1] = ci

        def fix(g, i):
            idx = rows(g, i)
            pr, pi = p_ref[0, pl.ds(i, 1), :], p_ref[1, pl.ds(i, 1), :]
            sr = l_scr[idx, :w] + pr * ins[g][0] - pi * ins[g][1]
            si = l_scr[idx, w:] + pr * ins[g][1] + pi * ins[g][0]
            s_ref[idx, :w] = sr.astype(s_ref.dtype)
            s_ref[idx, w:] = si.astype(s_ref.dtype)
            return sr, si

        if not with_da:
            def fix_step(i, carry):
                for g in range(nch):
                    fix(g, i)
                return carry

            lax.fori_loop(0, seg // SCAN_UNROLL, unrolled(fix_step, 0), 0)
        else:
            def adj_step(i, acc):
                acc_r, acc_i = acc
                for g in range(nch):
                    lr, li = fix(g, i)
                    prev = tile(g, seg - 2 - i)
                    fr, fi = sf_ref[prev, :w].astype(F32), sf_ref[prev, w:].astype(F32)
                    acc_r, acc_i = acc_r + lr * fr + li * fi, acc_i + li * fr - lr * fi
                return acc_r, acc_i

            acc = lax.fori_loop(0, seg // SCAN_UNROLL - 1, unrolled(adj_step, 0), (zero, zero))
            for i in range(seg - SCAN_UNROLL, seg - 1):
                acc = adj_step(i, acc)
            acc_r, acc_i = acc
            first_block = tt(t_blk) == 0
            for g in range(nch):
                lr, li = fix(g, seg - 1)
                seg_ends = tile(g, seg - 1)
                if g == 0:
                    pvr = jnp.where(first_block, 0.0, sp_ref[SUBLANES - 1:SUBLANES, :w].astype(F32))
                    pvi = jnp.where(first_block, 0.0, sp_ref[SUBLANES - 1:SUBLANES, w:].astype(F32))
                else:
                    pvr = sf_ref[g * chain_rows - 1:g * chain_rows, :w].astype(F32)
                    pvi = sf_ref[g * chain_rows - 1:g * chain_rows, w:].astype(F32)
                fr = jnp.where(sub == 0, pvr, pltpu.roll(sf_ref[seg_ends, :w].astype(F32), 1, 0))
                fi = jnp.where(sub == 0, pvi, pltpu.roll(sf_ref[seg_ends, w:].astype(F32), 1, 0))
                acc_r = acc_r + lr * fr + li * fi
                acc_i = acc_i + li * fr - lr * fi
            da_ref[0] += jnp.sum(acc_r, axis=0, keepdims=True)
            da_ref[1] += jnp.sum(acc_i, axis=0, keepdims=True)
            dw_ref[...] += _tn_dot(u_ref[...], s_ref[...])
            dx_ref[...] += _tn_dot(xb, sf_ref[...])

    x_spec = pl.BlockSpec((tb, bd_w), lambda j, t: (tt(t), j // tiles_per_bd))
    w_spec = pl.BlockSpec((bd_w, w), lambda j, t: (j // tiles_per_bd, j))
    d_spec = pl.BlockSpec((bd_w, 2 * w), lambda j, t: (j // tiles_per_bd, j % tiles_per_bd))
    a_spec = pl.BlockSpec((2, 1, w), lambda j, t: (0, 0, j))
    s_spec = pl.BlockSpec((tb, 2 * w), lambda j, t: (tt(t), j))
    in_specs, args = [x_spec, w_spec, w_spec, a_spec], [x, w_re, w_im, a_pair]
    out_specs, out_shape = [s_spec], [jax.ShapeDtypeStruct((l, 2 * SSM_S), BF16)]
    scratch = [pltpu.VMEM((2, seg, w), F32), pltpu.VMEM((2, 1, w), F32)] + [pltpu.VMEM((tb, 2 * w), F32)] * 2
    if with_da:
        in_specs += [s_spec, pl.BlockSpec((SUBLANES, 2 * w),
                                          lambda j, t: (jnp.maximum(tt(t) * (tb // SUBLANES) - 1, 0), j)),
                     x_spec]
        args += [s_fwd, s_fwd, u]
        out_specs += [a_spec, d_spec, d_spec]
        out_shape += ([jax.ShapeDtypeStruct((2, 1, SSM_S), F32)]
                      + [jax.ShapeDtypeStruct((SSM_W, 2 * SSM_S // SSM_BD), F32)] * 2)
    res = pl.pallas_call(
        body, grid=(SSM_S // w, nt), in_specs=in_specs, out_specs=out_specs, out_shape=out_shape,
        scratch_shapes=scratch, compiler_params=_params(("parallel", "arbitrary")), name=name,
    )(*args)
    return res if with_da else res[0]


def _nt_dot(x, y):
    return lax.dot_general(x.astype(_MXU), y.astype(_MXU), (((1,), (1,)), ((), ())), preferred_element_type=F32)


def _tn_dot(x, y):
    return lax.dot_general(x.astype(_MXU), y.astype(_MXU), (((0,), (0,)), ((), ())), preferred_element_type=F32)


def _nn_dot(x, y):
    return jnp.dot(x.astype(_MXU), y.astype(_MXU), preferred_element_type=F32)


def _attn_mask2(gb, nb):
    qi = lax.broadcasted_iota(jnp.int32, (ATT_WIN, 2 * ATT_WIN), 0)
    c = lax.broadcasted_iota(jnp.int32, (ATT_WIN, 2 * ATT_WIN), 1)
    has_prev = (gb % nb) != 0
    prev_ok = jnp.logical_and(jnp.logical_and(c < ATT_WIN, c >= qi), has_prev)
    own_ok = jnp.logical_and(c >= ATT_WIN, c - ATT_WIN <= qi)
    return jnp.logical_or(prev_ok, own_ok)


def _attn_specs(qb):
    cur = pl.BlockSpec((qb * ATT_WIN, ATT_GW), lambda i: (i, 0))
    prev = pl.BlockSpec((ATT_WIN, ATT_GW), lambda i: (jnp.maximum(qb * i - 1, 0), 0))
    return cur, prev


def _attn_fwd(q, k, v, nb, name):
    l = q.shape[0]
    scale = ATT_E ** -0.5
    w = ATT_WIN

    qb = ATT_QB_FWD

    def body(q_ref, kc_ref, kp_ref, vc_ref, vp_ref, o_ref, lse_ref):
        i = pl.program_id(0)
        masks = [_attn_mask2(qb * i + b, nb) for b in range(qb)]
        for h in range(ATT_HG):
            sl = slice(h * ATT_E, (h + 1) * ATT_E)
            k_ext = jnp.concatenate([kp_ref[:, sl], kc_ref[:, sl]], axis=0)
            v_ext = jnp.concatenate([vp_ref[:, sl], vc_ref[:, sl]], axis=0)
            for b in range(qb):
                r, kr = slice(b * w, (b + 1) * w), slice(b * w, (b + 2) * w)
                s = jnp.where(masks[b], _nt_dot(q_ref[r, sl], k_ext[kr]) * scale, NEG_INF)
                mx = jnp.max(s, axis=-1, keepdims=True)
                p = jnp.exp(s - mx)
                den = jnp.sum(p, axis=-1, keepdims=True)
                o_ref[r, sl] = _nn_dot(p, v_ext[kr]) / den
                lse_ref[r, sl] = jnp.broadcast_to(mx + jnp.log(den), (w, ATT_E))

    cur, prev = _attn_specs(qb)
    return pl.pallas_call(
        body, grid=(l // (qb * w),), in_specs=[cur, cur, prev, cur, prev], out_specs=[cur, cur],
        out_shape=[jax.ShapeDtypeStruct((l, ATT_GW), F32)] * 2,
        compiler_params=_params(("parallel",)), name=name,
    )(q, k, k, v, v)


def _attn_bwd(q, k, v, do, lse, dd, nb, name):
    l = q.shape[0]
    scale = ATT_E ** -0.5
    w = ATT_WIN
    nblk = l // w

    def body(q_ref, kc_ref, kp_ref, vc_ref, vp_ref, do_ref, lse_ref, dd_ref, qn_ref, don_ref, lsen_ref, ddn_ref,
             dq_ref, dk_ref, dv_ref, dk_acc, dv_acc):
        i = pl.program_id(0)
        masks = [_attn_mask2(ATT_QB * i + b, nb) for b in range(ATT_QB)]
        nxt = ATT_QB * (i + 1)
        nxt_attends = jnp.logical_and(nxt < nblk, (nxt % nb) != 0)
        qi = lax.broadcasted_iota(jnp.int32, (w, w), 0)
        kj = lax.broadcasted_iota(jnp.int32, (w, w), 1)
        mask_n = jnp.logical_and(kj >= qi, nxt_attends)
        dk_acc[...] = jnp.zeros_like(dk_acc)
        dv_acc[...] = jnp.zeros_like(dv_acc)
        for h in range(ATT_HG):
            sl, col = slice(h * ATT_E, (h + 1) * ATT_E), slice(h * ATT_E, h * ATT_E + 1)
            k_ext = jnp.concatenate([kp_ref[:, sl], kc_ref[:, sl]], axis=0)
            v_ext = jnp.concatenate([vp_ref[:, sl], vc_ref[:, sl]], axis=0)
            for b in range(ATT_QB):
                r, kr = slice(b * w, (b + 1) * w), slice(b * w, (b + 2) * w)
                qh, doh, k2, v2 = q_ref[r, sl], do_ref[r, sl], k_ext[kr], v_ext[kr]
                p = jnp.where(masks[b], jnp.exp(_nt_dot(qh, k2) * scale - lse_ref[r, col]), 0.0)
                ds = p * (_nt_dot(doh, v2) - dd_ref[r, col]) * scale
                dq_ref[r, sl] = _nn_dot(ds, k2).astype(dq_ref.dtype)
                dk2, dv2 = _tn_dot(ds, qh), _tn_dot(p, doh)
                dk_acc[r, sl] += dk2[w:]
                dv_acc[r, sl] += dv2[w:]
                if b > 0:
                    rp = slice((b - 1) * w, b * w)
                    dk_acc[rp, sl] += dk2[:w]
                    dv_acc[rp, sl] += dv2[:w]
            last = slice((ATT_QB - 1) * w, ATT_QB * w)
            kl, vl, qn, don = kc_ref[last, sl], vc_ref[last, sl], qn_ref[:, sl], don_ref[:, sl]
            pn = jnp.where(mask_n, jnp.exp(_nt_dot(qn, kl) * scale - lsen_ref[:, col]), 0.0)
            dsn = pn * (_nt_dot(don, vl) - ddn_ref[:, col]) * scale
            dk_acc[last, sl] += _tn_dot(dsn, qn)
            dv_acc[last, sl] += _tn_dot(pn, don)
        dk_ref[...] = dk_acc[...].astype(dk_ref.dtype)
        dv_ref[...] = dv_acc[...].astype(dv_ref.dtype)

    cur, prev = _attn_specs(ATT_QB)
    nxt_spec = pl.BlockSpec((w, ATT_GW), lambda i: (jnp.minimum(ATT_QB * (i + 1), nblk - 1), 0))
    return pl.pallas_call(
        body, grid=(l // (ATT_QB * w),),
        in_specs=[cur, cur, prev, cur, prev, cur, cur, cur, nxt_spec, nxt_spec, nxt_spec, nxt_spec],
        out_specs=[cur] * 3, out_shape=[jax.ShapeDtypeStruct((l, ATT_GW), BF16)] * 3,
        scratch_shapes=[pltpu.VMEM((ATT_QB * w, ATT_GW), F32)] * 2,
        compiler_params=_params(("parallel",)), name=name,
    )(q, k, k, v, v, do, lse, dd, q, do, lse, dd)


def _to_perm(a, d):
    if d == 1:
        return a
    l, c = a.shape
    return a.reshape(l // d, d, c).transpose(1, 0, 2).reshape(l, c)


def _from_perm(a, d):
    if d == 1:
        return a
    l, c = a.shape
    return a.reshape(d, l // d, c).transpose(1, 0, 2).reshape(l, c)


def _mem_probs(qh, kh):
    s = _nt_dot(qh, kh) * (MEM_E ** -0.5)
    e = jnp.exp(s - jnp.max(s, axis=-1, keepdims=True))
    return e / jnp.sum(e, axis=-1, keepdims=True)


def _mem_fwd(mq, kv, name, tm=1024):
    l, nm = mq.shape[0], kv.shape[0]

    def body(q_ref, kv_ref, o_ref):
        for h in range(MEM_H):
            sl = slice(h * MEM_E, (h + 1) * MEM_E)
            p = _mem_probs(q_ref[:, sl], kv_ref[:, sl])
            o_ref[:, sl] = _nn_dot(p, kv_ref[:, MEM_W + h * MEM_E:MEM_W + (h + 1) * MEM_E]).astype(o_ref.dtype)

    return pl.pallas_call(
        body, grid=(l // tm,),
        in_specs=[pl.BlockSpec((tm, MEM_W), lambda i: (i, 0)), pl.BlockSpec((nm, 2 * MEM_W), lambda i: (0, 0))],
        out_specs=pl.BlockSpec((tm, MEM_W), lambda i: (i, 0)),
        out_shape=jax.ShapeDtypeStruct((l, MEM_W), BF16),
        compiler_params=_params(("parallel",)), name=name,
    )(mq, kv)


def _mem_bwd(mq, kv, dmo, name, tm=1024):
    l, nm = mq.shape[0], kv.shape[0]
    scale = MEM_E ** -0.5

    def body(q_ref, kv_ref, do_ref, dq_ref, dkv_ref):
        @pl.when(pl.program_id(0) == 0)
        def _():
            dkv_ref[...] = jnp.zeros_like(dkv_ref)

        for h in range(MEM_H):
            sl = slice(h * MEM_E, (h + 1) * MEM_E)
            vsl = slice(MEM_W + h * MEM_E, MEM_W + (h + 1) * MEM_E)
            qh, kh, vh, doh = q_ref[:, sl], kv_ref[:, sl], kv_ref[:, vsl], do_ref[:, sl]
            p = _mem_probs(qh, kh)
            dp = _nt_dot(doh, vh)
            ds = p * (dp - jnp.sum(dp * p, axis=-1, keepdims=True)) * scale
            dq_ref[:, sl] = _nn_dot(ds, kh).astype(dq_ref.dtype)
            dkv_ref[:, sl] += _tn_dot(ds, qh)
            dkv_ref[:, vsl] += _tn_dot(p, doh)

    row = pl.BlockSpec((tm, MEM_W), lambda i: (i, 0))
    full = pl.BlockSpec((nm, 2 * MEM_W), lambda i: (0, 0))
    return pl.pallas_call(
        body, grid=(l // tm,), in_specs=[row, full, row], out_specs=[row, full],
        out_shape=[jax.ShapeDtypeStruct((l, MEM_W), BF16), jax.ShapeDtypeStruct((nm, 2 * MEM_W), F32)],
        compiler_params=_params(("arbitrary",)), name=name,
    )(mq, kv, dmo)


def _gated_out_proj(zg, branches, b_gate, w_o, x, g2, name, tm=512):
    l, d = x.shape
    nbr = len(branches)

    def body(zg_ref, *rest):
        br_refs, (bg_ref, w_ref, x_ref, g2_ref, m_ref, h_ref, n_ref) = rest[:nbr], rest[nbr:]
        merged = jnp.zeros((tm, d), F32)
        for i, br_ref in enumerate(br_refs):
            cols = slice(i * d, (i + 1) * d)
            merged += _sigmoid(zg_ref[:, cols].astype(F32) + bg_ref[:, cols]) * br_ref[...].astype(F32)
        mb = merged.astype(BF16)
        m_ref[...] = mb
        hv = jnp.dot(mb.astype(_MXU), w_ref[...].astype(_MXU), preferred_element_type=F32) + x_ref[...]
        h_ref[...] = hv
        rs = lax.rsqrt(jnp.mean(hv * hv, axis=-1, keepdims=True) + RMS_EPS)
        n_ref[...] = (hv * rs * g2_ref[...]).astype(n_ref.dtype)

    row = lambda c: pl.BlockSpec((tm, c), lambda i: (i, 0))
    full = lambda a: pl.BlockSpec(a.shape, lambda i: (0, 0))
    return pl.pallas_call(
        body, grid=(l // tm,),
        in_specs=[row(nbr * d)] + [row(d)] * nbr + [full(b_gate), full(w_o), row(d), full(g2)],
        out_specs=[row(d)] * 3,
        out_shape=[jax.ShapeDtypeStruct((l, d), BF16), jax.ShapeDtypeStruct((l, d), F32),
                   jax.ShapeDtypeStruct((l, d), BF16)],
        compiler_params=_params(("parallel",)), name=name,
    )(zg, *branches, b_gate, w_o, x, g2)


def _discretize(lam_re, lam_im, log_dt, b_re, b_im):
    dt = jnp.exp(log_dt)[:, None]
    mag = jnp.exp(lam_re * dt)
    a_re, a_im = mag * jnp.cos(lam_im * dt), mag * jnp.sin(lam_im * dt)
    nr, ni = a_re - 1.0, a_im
    den = lam_re * lam_re + lam_im * lam_im
    coef_re = (nr * lam_re + ni * lam_im) / den
    coef_im = (ni * lam_re - nr * lam_im) / den
    bb_re = coef_re[..., None] * b_re - coef_im[..., None] * b_im
    bb_im = coef_re[..., None] * b_im + coef_im[..., None] * b_re
    return a_re, a_im, bb_re, bb_im


def _tiled(re, im):
    r = re.shape[0]
    both = jnp.concatenate([re.reshape(r, -1, SCAN_W), im.reshape(r, -1, SCAN_W)], axis=2)
    return both.reshape(r, 2 * re.shape[1])


def _untiled(x):
    r = x.shape[0]
    t = x.reshape(r, -1, 2 * SCAN_W)
    return t[:, :, :SCAN_W].reshape(r, -1), t[:, :, SCAN_W:].reshape(r, -1)


def _bd_in(bb):
    return jnp.einsum("gph,gk->ghkp", bb, jnp.eye(SSM_G, dtype=bb.dtype)).reshape(SSM_W, SSM_S)


def _bd_diag(x):
    gb = SSM_G // SSM_BD
    t = x.reshape(SSM_BD, gb, SSM_H, gb, SSM_P)
    return jnp.einsum("bghgp->bghp", t).reshape(SSM_G, SSM_H, SSM_P)


_ANY = pl.BlockSpec(memory_space=pl.ANY)
_MESH = pl.DeviceIdType.MESH


def _allgather(x, name):
    def body(x_ref, out_ref, send_sems, recv_sems, local_sem):
        mx, my, mc = lax.axis_index("x"), lax.axis_index("y"), lax.axis_index("c")
        me, sibling = (mx, my, mc), (mx, my, 1 - mc)
        chips = [(1 - mx, my), (mx, 1 - my), (1 - mx, 1 - my)]

        def blk(px, py, pc):
            return out_ref.at[4 * px + 2 * py + pc]

        def copy(k, block, to, src=None):
            return pltpu.make_async_remote_copy(
                src_ref=blk(*block) if src is None else src, dst_ref=blk(*block),
                send_sem=send_sems.at[k], recv_sem=recv_sems.at[k], device_id=to, device_id_type=_MESH)

        mine = pltpu.make_async_copy(x_ref, blk(*me), local_sem)
        mine.start()
        first = [copy(0, me, sibling, src=x_ref)]
        first += [copy(1 + j, me, (*chip, mc), src=x_ref) for j, chip in enumerate(chips)]
        for cp in first:
            cp.start()
        passed = [copy(4 + j, (*chip, mc), sibling) for j, chip in enumerate(chips)]
        for j, chip in enumerate(chips):
            copy(1 + j, (*chip, mc), me).wait_recv()
            passed[j].start()
        copy(0, sibling, me).wait_recv()
        for j, chip in enumerate(chips):
            copy(4 + j, (*chip, 1 - mc), me).wait_recv()
        for cp in first + passed:
            cp.wait_send()
        mine.wait()

    return pl.pallas_call(
        body, out_shape=jax.ShapeDtypeStruct((N_DEV,) + x.shape, x.dtype), in_specs=[_ANY], out_specs=_ANY,
        scratch_shapes=[pltpu.SemaphoreType.DMA((7,)), pltpu.SemaphoreType.DMA((7,)), pltpu.SemaphoreType.DMA],
        name=name,
    )(x)


def _pair_exchange(g, name):
    def body(g_ref, out_ref, send_sems, recv_sems):
        mx, my, mc = lax.axis_index("x"), lax.axis_index("y"), lax.axis_index("c")
        copies = [pltpu.make_async_remote_copy(
            src_ref=g_ref.at[2 * k + (1 - mc)], dst_ref=out_ref.at[k], send_sem=send_sems.at[k],
            recv_sem=recv_sems.at[k], device_id=(mx, my, 1 - mc), device_id_type=_MESH) for k in range(4)]
        for cp in copies:
            cp.start()
        for cp in copies:
            cp.wait()

    return pl.pallas_call(
        body, out_shape=jax.ShapeDtypeStruct((4,) + g.shape[1:], g.dtype), in_specs=[_ANY], out_specs=_ANY,
        scratch_shapes=[pltpu.SemaphoreType.DMA((4,)), pltpu.SemaphoreType.DMA((4,))], name=name,
    )(g)


_HBM = pl.BlockSpec(memory_space=pltpu.HBM)
_SEM = pl.BlockSpec(memory_space=pltpu.SEMAPHORE)
_EFFECT = pltpu.SideEffectType.DATAFLOW_SIDE_EFFECTING
_TOKEN = jax.ShapeDtypeStruct((8, 128), F32)


def _peer(rel):
    pos = (lax.axis_index("x"), lax.axis_index("y"), lax.axis_index("c"))
    return tuple(1 - p if (rel >> (2 - i)) & 1 else p for i, p in enumerate(pos))


def _index_of(dev):
    return 4 * dev[0] + 2 * dev[1] + dev[2]


def _split_copies(src_ref, land_ref, sems, plan):
    n = len(plan)
    return [pltpu.make_async_remote_copy(
        src_ref=src_ref if s is None else src_ref.at[s], dst_ref=land_ref.at[d], send_sem=sems[k],
        recv_sem=sems[n + k], device_id=peer, device_id_type=_MESH) for k, (s, d, peer) in enumerate(plan)]


def _split_start(src, n_land, plan_fn, after, name):
    blk = src.shape[-2:]
    land = lax.empty((n_land,) + blk, src.dtype)
    n = len(plan_fn())

    def body(src_ref, land_ref, after_ref, *outs):
        for cp in _split_copies(src_ref, land_ref, outs[:2 * n], plan_fn()):
            cp.start()
        outs[2 * n + 2][...] = jnp.zeros_like(outs[2 * n + 2])

    res = pl.pallas_call(
        body, name=name,
        out_shape=(pltpu.SemaphoreType.DMA(()),) * (2 * n)
        + (pltpu.HBM(src.shape, src.dtype), pltpu.HBM(land.shape, land.dtype), _TOKEN),
        in_specs=(_HBM, _HBM, _ANY),
        out_specs=(_SEM,) * (2 * n) + (_HBM, _HBM, pl.BlockSpec(memory_space=pltpu.VMEM)),
        input_output_aliases={0: 2 * n, 1: 2 * n + 1},
        compiler_params=pltpu.CompilerParams(has_side_effects=_EFFECT),
    )(pltpu.with_memory_space_constraint(src, pltpu.HBM), pltpu.with_memory_space_constraint(land, pltpu.HBM), after)
    return res[:2 * n], res[2 * n], res[2 * n + 1], res[2 * n + 2]


def _split_wait(sems, src, land, plan_fn, after, name):
    n = len(sems) // 2

    def body(src_ref, land_ref, *rest):
        for cp in _split_copies(src_ref, land_ref, rest[:2 * n], plan_fn()):
            cp.wait_send()
            cp.wait_recv()

    return pl.pallas_call(
        body, name=name,
        out_shape=(pltpu.HBM(src.shape, src.dtype), pltpu.HBM(land.shape, land.dtype)),
        in_specs=(_HBM, _HBM) + (_SEM,) * (2 * n) + (_ANY,), out_specs=(_HBM, _HBM),
        input_output_aliases={0: 0, 1: 1},
        compiler_params=pltpu.CompilerParams(has_side_effects=_EFFECT),
    )(src, land, *sems, after)


def _gather_plan():
    me = _index_of(_peer(0))
    return [(None, me, _peer(rel)) for rel in range(1, N_DEV)]


def _gather_wait_plan():
    return [(None, _index_of(_peer(rel)), _peer(rel)) for rel in range(1, N_DEV)]


def _chip_plan():
    return [(_index_of(_peer(rel)) // 2, j, _peer(rel)) for j, rel in enumerate((4, 2, 6))]


def _owner_plan():
    return [(_index_of(_peer(rel)), rel - 1, _peer(rel)) for rel in range(1, N_DEV)]


def _pair_sum(g, t1, my_c, name, tr):
    _, r, c = g.shape

    def body(c_ref, g_ref, t_ref, o_ref, ob_ref):
        s = g_ref[...] + t_ref[...]
        o_ref[...] = s
        ob_ref[...] = s.astype(BF16)

    blk = pl.BlockSpec((None, tr, c), lambda k, i, cr: (k, i, 0))
    return pl.pallas_call(
        body,
        grid_spec=pltpu.PrefetchScalarGridSpec(
            num_scalar_prefetch=1, grid=(4, r // tr),
            in_specs=[pl.BlockSpec((None, tr, c), lambda k, i, cr: (2 * k + cr[0], i, 0)), blk],
            out_specs=[blk, blk]),
        out_shape=[jax.ShapeDtypeStruct((4, r, c), F32), jax.ShapeDtypeStruct((4, r, c), BF16)],
        compiler_params=_params(("parallel", "parallel")), name=name,
    )(my_c, g, t1)


def _adam_math(g, w, m, v):
    m = ADAM_B1 * m + (1.0 - ADAM_B1) * g
    v = ADAM_B2 * v + (1.0 - ADAM_B2) * (g * g)
    m_hat = m / (1.0 - ADAM_B1 ** ADAM_STEP)
    v_hat = v / (1.0 - ADAM_B2 ** ADAM_STEP)
    delta = -ADAM_LR * (m_hat / (jnp.sqrt(v_hat) + ADAM_EPS) + ADAM_WD * w)
    return delta, m, v


def _grad_sum(own, own_index, recv, name, tr):
    _, r, c = own.shape
    n = recv.shape[0]

    def body(k_ref, own_ref, *rest):
        g = own_ref[...]
        for recv_ref in rest[:n]:
            g = g + recv_ref[...].astype(F32)
        rest[n][...] = g

    def slot(j):
        return pl.BlockSpec((None, tr, c), lambda i, kr: (j, i, 0))

    return pl.pallas_call(
        body,
        grid_spec=pltpu.PrefetchScalarGridSpec(
            num_scalar_prefetch=1, grid=(r // tr,),
            in_specs=[pl.BlockSpec((None, tr, c), lambda i, kr: (kr[0], i, 0))] + [slot(j) for j in range(n)],
            out_specs=pl.BlockSpec((tr, c), lambda i, kr: (i, 0))),
        out_shape=jax.ShapeDtypeStruct((r, c), F32),
        compiler_params=_params(("parallel",)), name=name,
    )(own_index, own, *([recv] * n))


def _adam_many(g, w, m, v, row_tiles, name):
    n = len(g)

    def body(*refs):
        ins, outs = refs[:4 * n], refs[4 * n:]
        for i in range(n):
            res = _adam_math(ins[i][...], ins[n + i][...], ins[2 * n + i][...], ins[3 * n + i][...])
            for kind in range(3):
                outs[kind * n + i][...] = res[kind]

    def spec(a):
        blk = (a.shape[0] // row_tiles,) + a.shape[1:]
        return pl.BlockSpec(blk, lambda t, nd=a.ndim: (t,) + (0,) * (nd - 1))

    specs = [spec(a) for a in g]
    res = pl.pallas_call(
        body, grid=(row_tiles,), in_specs=specs * 4, out_specs=specs * 3,
        out_shape=[jax.ShapeDtypeStruct(a.shape, F32) for a in g] * 3,
        compiler_params=_params(("parallel",)), name=name,
    )(*g, *w, *m, *v)
    return res[:n], res[n:2 * n], res[2 * n:]


def _sum8(g8, name):
    _, r, c = g8.shape

    def body(g_ref, o_ref):
        acc = g_ref[0]
        for j in range(1, N_DEV):
            acc = acc + g_ref[j]
        o_ref[...] = acc

    return pl.pallas_call(
        body, grid=(1,), in_specs=[pl.BlockSpec((N_DEV, r, c), lambda i: (0, 0, 0))],
        out_specs=pl.BlockSpec((r, c), lambda i: (0, 0)), out_shape=jax.ShapeDtypeStruct((r, c), F32),
        compiler_params=_params(("arbitrary",)), name=name,
    )(g8)


def _pack(arrs, pad_rows=8):
    flat = jnp.concatenate([a.reshape(-1) for a in arrs])
    n = flat.shape[0]
    q = PACK_C * pad_rows
    tot = -(-n // q) * q
    if tot != n:
        flat = jnp.concatenate([flat, jnp.zeros((tot - n,), flat.dtype)])
    return flat.reshape(tot // PACK_C, PACK_C)


def _unpack(buf, shapes):
    flat = buf.reshape(-1)
    out, off = [], 0
    for s in shapes:
        n = int(np.prod(s))
        out.append(flat[off:off + n].reshape(s))
        off += n
    return out


GROUPS = (("w_in",),
          ("w_glu", "w_ssm_br", "w_mem_br", "w_attn_br"),
          ("w_up", "w_down"),
          ("w_mem_kv", "w_o"))
GROUP_TR = (400, 384, 512, 256)
MLP_GROUP = 2
MIXER_GROUPS = (1, 3)
ATTN_BR_FOLD = 2


def _stored_shape(name):
    r, c, ax = BIG_SHAPE[name]
    rows, cols = (r // N_DEV, c) if ax == 0 else (c // N_DEV, r)
    return (rows // ATTN_BR_FOLD, cols * ATTN_BR_FOLD) if name == "w_attn_br" else (rows, cols)


def _stored(shard, name):
    a = shard[0].T if BIG_SHAPE[name][2] == 1 else shard[0]
    return a.reshape(_stored_shape(name))


def _unstored(a, name):
    r, c, ax = BIG_SHAPE[name]
    if ax == 0:
        return a.reshape(1, r // N_DEV, c)
    return a.reshape(c // N_DEV, r).T[None]


def _pack_group(d, names):
    return jnp.concatenate([_stored(d[n], n) for n in names], axis=0)


def _split_group(buf, names):
    out, off = {}, 0
    for n in names:
        rows = _stored_shape(n)[0]
        out[n] = buf[..., off:off + rows, :]
        off += rows
    return out


def _full_stored(stacked, name):
    r, c, ax = BIG_SHAPE[name]
    return stacked.reshape((r, c) if ax == 0 else (c, r))


def _stacked_stored(full, name):
    return full.reshape((N_DEV,) + _stored_shape(name))


def _gelu_parts(x):
    c0, c1 = math.sqrt(2.0 / math.pi), 0.044715
    th = jnp.tanh(c0 * (x + c1 * x * x * x))
    return th, c0, c1


def _local_step(x, mem, tgt, wb, sp, late_weights, grads_ready, small_grads_ready):
    l = x.shape[0]
    w_a, w_g = wb["w_in"][:ZA_W], wb["w_in"][ZA_W:]

    a_re, a_im, bb_re, bb_im = _discretize(sp["ssm_lambda_re"], sp["ssm_lambda_im"], sp["ssm_log_dt"],
                                           sp["ssm_b_re"], sp["ssm_b_im"])
    a_pair = jnp.stack([a_re.reshape(1, SSM_S), a_im.reshape(1, SSM_S)])
    a_conj = jnp.stack([a_re.reshape(1, SSM_S), -a_im.reshape(1, SSM_S)])
    b_re_t, b_im_t = _bd_in(bb_re).astype(BF16), _bd_in(bb_im).astype(BF16)
    c_re_t = _bd_in(sp["ssm_c_re"].transpose(0, 2, 1)).astype(BF16)
    c_im_t = (-_bd_in(sp["ssm_c_im"].transpose(0, 2, 1))).astype(BF16)
    d_row = sp["ssm_d"].reshape(1, SSM_W)

    n1 = _rms_fwd(x, sp["norm1_g"], "rms1")
    za = _mm(n1, w_a, [BF16], tb=True, name="in_proj_a", tn=1664)
    zg = _mm(n1, w_g, [BF16], tb=True, name="in_proj_g")
    for gi in MIXER_GROUPS:
        wb = {**wb, **late_weights(gi, za)}
    u = za[:, :SSM_W]
    mq = za[:, ZA_W - MEM_W:]

    u_s = _scan_order(u)
    s_all = _ssm_scan(u_s, b_re_t, b_im_t, a_pair, reverse=False, name="ssm_scan_fwd")
    def gelu_epi(acc, ut, dr):
        y = acc + dr * ut.astype(F32)
        th, _, _ = _gelu_parts(y)
        return y, 0.5 * y * (1.0 + th)
    y0, y1 = [_time_order(t) for t in _mm(s_all, _tiled(c_re_t, c_im_t), [F32, BF16], tb=True, epi=gelu_epi,
                                          mn=[u_s], rows=[d_row], bd=SSM_BD, tm=2048, name="ssm_cs")]

    def glu_epi(acc, y1t, bg):
        t = acc + bg
        return t, y1t.astype(F32) * _sigmoid(t)
    t_glu, y2 = _mm(y1, wb["w_glu"], [F32, BF16], epi=glu_epi, mn=[y1], rows=[sp["b_glu"]], name="ssm_glu")
    br_ssm = _mm(y2, wb["w_ssm_br"], [BF16], tb=True, name="ssm_br")

    qkv_p, o_g, lse_g = [], [], []
    for g, d in enumerate(DILATIONS):
        nb = l // d // ATT_WIN
        cols = [za[:, SSM_W + (3 * j + g) * ATT_GW: SSM_W + (3 * j + g + 1) * ATT_GW] for j in range(3)]
        qp, kp, vp = [_to_perm(cc, d) for cc in cols]
        qkv_p.append((qp, kp, vp))
        og, lg = _attn_fwd(qp, kp, vp, nb, "attn_fwd%d" % g)
        o_g.append(_from_perm(og, d))
        lse_g.append(_from_perm(lg, d))

    def merge_fn(r, b):
        o0, o1, o2, l0, l1, l2 = r
        mx = jnp.maximum(jnp.maximum(l0, l1), l2)
        e0, e1, e2 = jnp.exp(l0 - mx), jnp.exp(l1 - mx), jnp.exp(l2 - mx)
        tot = e0 + e1 + e2
        return [(e0 * o0 + e1 * o1 + e2 * o2) / tot, mx + jnp.log(tot)], []
    o_att, lse_tot = _ew(merge_fn, o_g + lse_g, [], [(ATT_GW, F32), (ATT_GW, F32)], [], name="attn_merge", tm=2048)
    br_attn = _mm(o_att, wb["w_attn_br"], [BF16], tb=True, name="attn_br")

    mn = _rms_fwd(mem, sp["mem_norm_g"], "rms_mem")
    kv = _mm(mn, wb["w_mem_kv"], [BF16], name="mem_kv")
    mo = _mem_fwd(mq, kv, "mem_attn_fwd")
    br_mem = _mm(mo, wb["w_mem_br"], [BF16], tb=True, name="mem_br")

    merged, h1, n2 = _gated_out_proj(zg, [br_ssm, br_attn, br_mem], sp["b_gate"], wb["w_o"], x, sp["norm2_g"],
                                     "gated_o_proj")

    def up_epi(acc):
        ra = jnp.maximum(acc, 0.0)
        return ra * ra, ra
    wm = late_weights(MLP_GROUP, n2)
    f_act, r_act = _mm(n2, wm["w_up"], [BF16, BF16], tb=True, epi=up_epi, name="mlp_up")
    def down_epi(acc, ht, tv, gf):
        hv = acc + ht
        rs = lax.rsqrt(jnp.mean(hv * hv, axis=-1, keepdims=True) + RMS_EPS)
        err = hv * rs * gf - tv
        dh, dgf = _rms_bwd_tile(hv, err * (1.0 / D_MODEL), gf)
        return dh, dgf, _colsum(err * err) * (0.5 / D_MODEL)
    dh2, d_final_g, loss_cols = _mm(f_act, wm["w_down"], [F32], epi=down_epi, mn=[h1, tgt], rows=[sp["final_g"]],
                                    n_sums=2, tk=1024, name="mlp_down")
    loss = jnp.sum(loss_cols, axis=1, keepdims=True)

    gw, gs = {}, {"final_g": d_final_g}
    d_act = _mm(dh2, wm["w_down"], [BF16], tb=True, epi=lambda acc, ra: (acc * 2.0 * ra.astype(F32),), mn=[r_act],
                name="mlp_down_dx")
    dw_down = _mm(f_act, dh2, [F32], ta=True, name="mlp_down_dw")
    dw_up = _mm(d_act, n2, [F32], ta=True, name="mlp_up_dw")
    token = grads_ready(MLP_GROUP, {"w_up": dw_up, "w_down": dw_down})
    def up_dx_epi(acc, ht, dht, g2):
        dx, dg = _rms_bwd_tile(ht, acc, g2)
        return dx + dht, dg
    dh1, gs["norm2_g"] = _mm(d_act, wm["w_up"], [F32], epi=up_dx_epi, mn=[h1, dh2],
                             rows=[sp["norm2_g"] + token[:1, :1]], n_sums=1, tk=1024, name="mlp_up_dx")
    gw["w_o"] = _mm(merged, dh1, [F32], ta=True, name="o_proj_dw")

    def gate_bwd_epi(dm, *tiles):
        dbr, dz = [], []
        for zt, bt, bias in zip(tiles[0:3], tiles[3:6], tiles[6:9]):
            gt = _sigmoid(zt.astype(F32) + bias)
            dbr.append(dm * gt)
            dz.append(dm * bt.astype(F32) * gt * (1.0 - gt))
        return (*dbr, *dz, *[_colsum(t) for t in dz])
    gate_bias = [sp["b_gate"][:, i * D_MODEL:(i + 1) * D_MODEL] for i in range(3)]
    res = _mm(dh1, wb["w_o"], [BF16] * 6, tb=True, epi=gate_bwd_epi, mn=[(zg, 0), (zg, 1), (zg, 2), br_ssm, br_attn, br_mem],
              rows=gate_bias, n_sums=3, tm=512, name="o_proj_dx")
    (dbr_ssm, dbr_attn, dbr_mem), dzg = res[0:3], res[3:6]
    gs["b_gate"] = jnp.concatenate(res[6:9], axis=1)

    gw["w_ssm_br"] = _mm(dbr_ssm, y2, [F32], ta=True, name="ssm_br_dw")
    def glu_bwd_epi(dy, y1t, tt):
        sg = _sigmoid(tt)
        dt = dy * y1t.astype(F32) * sg * (1.0 - sg)
        return dt, dy * sg, _colsum(dt)
    dt_glu, dy1a, gs["b_glu"] = _mm(dbr_ssm, wb["w_ssm_br"], [BF16, F32], epi=glu_bwd_epi, mn=[y1, t_glu], n_sums=1,
                                    name="ssm_br_dx")
    gw["w_glu"] = _mm(y1, dt_glu, [F32], ta=True, name="ssm_glu_dw")

    def gelu_bwd_epi(acc, dy1t, y0t, ut):
        th, c0, c1 = _gelu_parts(y0t)
        dg = 0.5 * (1.0 + th) + 0.5 * y0t * (1.0 - th * th) * c0 * (1.0 + 3.0 * c1 * y0t * y0t)
        dy = (acc + dy1t) * dg
        return dy, _colsum(dy * ut.astype(F32))
    dy0, gs["ssm_d"] = _mm(dt_glu, wb["w_glu"], [F32], tb=True, epi=gelu_bwd_epi, mn=[dy1a, y0, u], n_sums=1,
                           name="ssm_glu_dx")
    dy0_s = _scan_order(dy0)
    lam, da, d_b, d_c = _ssm_scan(dy0_s, c_re_t, c_im_t, a_conj, reverse=True, s_fwd=s_all, u=u_s,
                                  name="ssm_scan_bwd")
    du = _time_order(_mm(lam, _tiled(b_re_t, b_im_t), [BF16], tb=True,
                         epi=lambda acc, dyt, dr: (acc + dyt * dr,), mn=[dy0_s], rows=[d_row], bd=SSM_BD, tm=2048, name="ssm_bu_dx"))
    gs["a_re"], gs["a_im"] = da[0], da[1]
    (dbr, dbi), (dcr, dci) = _untiled(d_b), _untiled(d_c)
    gs["bb_re"], gs["bb_im"] = _bd_diag(dbr).transpose(0, 2, 1), _bd_diag(dbi).transpose(0, 2, 1)
    gs["ssm_c_re"], gs["ssm_c_im"] = _bd_diag(dcr), -_bd_diag(dci)

    gw["w_attn_br"] = _mm(dbr_attn, o_att, [F32], ta=True, name="attn_br_dw")

    def do_epi(acc, ot):
        prod = acc * ot
        head = lax.broadcasted_iota(jnp.int32, prod.shape, 1) // ATT_E
        dd = jnp.zeros_like(prod)
        for h in range(ATT_HG):
            dd = jnp.where(head == h, jnp.sum(jnp.where(head == h, prod, 0.0), axis=1, keepdims=True), dd)
        return acc, dd
    do_att, dd_att = _mm(dbr_attn, wb["w_attn_br"], [BF16, F32], epi=do_epi, mn=[o_att], name="attn_br_dx")
    dq_l, dk_l, dv_l = [], [], []
    for g, d in enumerate(DILATIONS):
        nb = l // d // ATT_WIN
        qp, kp, vp = qkv_p[g]
        dq, dk, dv = _attn_bwd(qp, kp, vp, _to_perm(do_att, d), _to_perm(lse_tot, d), _to_perm(dd_att, d),
                               nb, "attn_bwd%d" % g)
        dq_l.append(_from_perm(dq, d))
        dk_l.append(_from_perm(dk, d))
        dv_l.append(_from_perm(dv, d))

    gw["w_mem_br"] = _mm(dbr_mem, mo, [F32], ta=True, name="mem_br_dw")
    dmo = _mm(dbr_mem, wb["w_mem_br"], [BF16], name="mem_br_dx")
    dmq, dkv = _mem_bwd(mq, kv, dmo, "mem_attn_bwd")
    gw["w_mem_kv"] = _mm(mn, dkv, [F32], ta=True, name="mem_kv_dw")
    dmn = _mm(dkv, wb["w_mem_kv"], [F32], tb=True, name="mem_kv_dx")
    token = sum(grads_ready(gi, gw) for gi in MIXER_GROUPS)
    gs["mem_norm_g"] = _rms_bwd(mem, dmn, None, sp["mem_norm_g"] + token[:1, :1], "rms_mem_bwd")[1]

    dza = jnp.concatenate([du] + dq_l + dk_l + dv_l + [dmq], axis=1)
    dn_a = _mm(dza, w_a, [F32], name="in_proj_a_dx", tk=1664)
    dw_a = _mm(dza, n1, [F32], ta=True, name="in_proj_a_dw", tm=1664)
    dw_g = [_mm(dzg[i], n1, [F32], ta=True, name="in_proj_g_dw%d" % i) for i in range(3)]
    gw["w_in"] = jnp.concatenate([dw_a] + dw_g, axis=0)
    token = grads_ready(0, gw) + small_grads_ready(gs)
    def in_dx_epi(acc, pt, xt, dht, g1):
        dx, dg = _rms_bwd_tile(xt, acc + pt, g1)
        return dx + dht, dg
    w_gs = [w_g[i * D_MODEL:(i + 1) * D_MODEL] for i in range(3)]
    grad_x, gs["norm1_g"] = _mm(dzg[0], w_gs[0], [F32], pair2=(dzg[1], w_gs[1], dzg[2], w_gs[2]), epi=in_dx_epi,
                                mn=[dn_a, x, dh1],
                                rows=[sp["norm1_g"] + token[:1, :1]], n_sums=1, tm=512, name="in_proj_g_dx")
    return loss, grad_x, gs


_SMALL_GRAD_ORDER = ("norm1_g", "mem_norm_g", "b_gate", "a_re", "a_im", "bb_re", "bb_im", "ssm_c_re", "ssm_c_im",
                     "ssm_d", "b_glu", "norm2_g", "final_g")


def kernel(x, mem, norm1_g, mem_norm_g, w_in, b_gate, ssm_lambda_re, ssm_lambda_im, ssm_log_dt, ssm_b_re, ssm_b_im, ssm_c_re, ssm_c_im, ssm_d, w_glu, b_glu, w_ssm_br, w_attn_br, w_mem_kv, w_mem_br, w_o, norm2_g, w_up, w_down, final_g, loss_target, m_norm1_g, m_mem_norm_g, m_w_in, m_b_gate, m_ssm_lambda_re, m_ssm_lambda_im, m_ssm_log_dt, m_ssm_b_re, m_ssm_b_im, m_ssm_c_re, m_ssm_c_im, m_ssm_d, m_w_glu, m_b_glu, m_w_ssm_br, m_w_attn_br, m_w_mem_kv, m_w_mem_br, m_w_o, m_norm2_g, m_w_up, m_w_down, m_final_g, v_norm1_g, v_mem_norm_g, v_w_in, v_b_gate, v_ssm_lambda_re, v_ssm_lambda_im, v_ssm_log_dt, v_ssm_b_re, v_ssm_b_im, v_ssm_c_re, v_ssm_c_im, v_ssm_d, v_w_glu, v_b_glu, v_w_ssm_br, v_w_attn_br, v_w_mem_kv, v_w_mem_br, v_w_o, v_norm2_g, v_w_up, v_w_down, v_final_g):
    args = dict(locals())
    w = {n: args[n] for n in ALL_W}
    m = {n: args["m_" + n] for n in ALL_W}
    v = {n: args["v_" + n] for n in ALL_W}
    my_c = lax.axis_index("c").astype(jnp.int32).reshape(1)
    my_chip = (2 * lax.axis_index("x") + lax.axis_index("y")).astype(jnp.int32).reshape(1)

    w_pack = [_pack_group(w, names) for names in GROUPS]
    my_index = (4 * lax.axis_index("x") + 2 * lax.axis_index("y") + lax.axis_index("c")).astype(jnp.int32)
    zero = jnp.zeros((), jnp.int32)
    w_all = _allgather(w_pack[0].astype(BF16), "allgather_weights0")
    wb = {n: _full_stored(part, n) for n, part in _split_group(w_all, GROUPS[0]).items()}
    gathers = {gi: _split_start(w_pack[gi].astype(BF16), N_DEV, _gather_plan, w_all, "weights_gather_start%d" % gi)
               for gi in range(1, len(GROUPS))}

    def gathered(started, after, name):
        sems, src, land, _ = started
        src, land = _split_wait(sems, src, land, _gather_wait_plan, after, name)
        return lax.dynamic_update_slice(land, src[None], (my_index, zero, zero))

    def late_weights(gi, after):
        full = gathered(gathers[gi], after, "weights_gather_wait%d" % gi)
        return {n: _full_stored(part, n) for n, part in _split_group(full, GROUPS[gi]).items()}

    pending = {}

    def grads_ready(gi, grads):
        g_pack = jnp.concatenate([_stacked_stored(grads[n], n) for n in GROUPS[gi]], axis=1)
        if gi == 0:
            t1 = _pair_exchange(g_pack, "grad_pair_exchange%d" % gi)
            p_sum, p_bf = _pair_sum(g_pack, t1, my_c, "grad_pair_sum%d" % gi, GROUP_TR[gi])
            started = _split_start(p_bf, 3, _chip_plan, p_sum, "grad_chip_exchange_start%d" % gi)
            pending[gi] = (p_sum, my_chip, started, _chip_plan)
        else:
            started = _split_start(g_pack.astype(BF16), N_DEV - 1, _owner_plan, g_pack, "grad_exchange_start%d" % gi)
            pending[gi] = (g_pack, my_index.reshape(1), started, _owner_plan)
        return started[3]

    early_small = [n for n in _SMALL_GRAD_ORDER if n != "norm1_g"]
    small_started = []

    def small_grads_ready(gs):
        started = _split_start(_pack([gs[n] for n in early_small]), N_DEV, _gather_plan, gs["mem_norm_g"],
                               "small_grads_gather_start")
        small_started.append((started, [gs[n].shape for n in early_small]))
        return started[3]

    sp = {
        "norm1_g": norm1_g + sum(started[3][:1, :1] for started in gathers.values()), "mem_norm_g": mem_norm_g, "b_gate": b_gate, "b_glu": b_glu, "norm2_g": norm2_g,
        "final_g": final_g.reshape(1, D_MODEL),
        "ssm_lambda_re": ssm_lambda_re[0], "ssm_lambda_im": ssm_lambda_im[0], "ssm_log_dt": ssm_log_dt[0],
        "ssm_b_re": ssm_b_re[0], "ssm_b_im": ssm_b_im[0], "ssm_c_re": ssm_c_re[0], "ssm_c_im": ssm_c_im[0],
        "ssm_d": ssm_d[0],
    }
    loss, grad_x, gs = _local_step(x[0], mem[0], loss_target[0], wb, sp, late_weights, grads_ready,
                                     small_grads_ready)
    loss = lax.psum(loss[0, 0], ("x", "y", "c"))
    n1_started = _split_start(_pack([gs["norm1_g"]]), N_DEV, _gather_plan, grad_x, "norm1_grad_gather_start")

    big_g = {}
    for gi, names in enumerate(GROUPS):
        own, own_index, (sems, src, land, _), plan = pending[gi]
        recv = _split_wait(sems, src, land, plan, grad_x, "grad_exchange_wait%d" % gi)[1]
        g_pack = _grad_sum(own, own_index, recv, "grad_sum%d" % gi, GROUP_TR[gi])
        for n, part in _split_group(g_pack, names).items():
            big_g[n] = _unstored(part, n)
    rows_of = lambda d, names: [d[n].reshape(d[n].shape[-2:]) for n in names]
    big_out = _adam_many(rows_of(big_g, BIG), rows_of(w, BIG), rows_of(m, BIG), rows_of(v, BIG), 8, "adam_big")
    big = [big_g] + [{n: a[None] for n, a in zip(BIG, outs)} for outs in big_out]

    (sg_started, sg_shapes), = small_started
    sg_all = jnp.concatenate([gathered(sg_started, big_out[0][0], "small_grads_gather_wait"),
                              gathered(n1_started, big_out[0][0], "norm1_grad_gather_wait")], axis=1)
    sg_sum = _sum8(sg_all, "sum_small_grads")
    n1_rows = n1_started[1].shape[0]
    sg = dict(zip(early_small, _unpack(sg_sum[:-n1_rows], sg_shapes)))
    sg["norm1_g"] = _unpack(sg_sum[-n1_rows:], [gs["norm1_g"].shape])[0]
    _, disc_vjp = jax.vjp(_discretize, sp["ssm_lambda_re"], sp["ssm_lambda_im"], sp["ssm_log_dt"],
                          sp["ssm_b_re"], sp["ssm_b_im"])
    d_lre, d_lim, d_ldt, d_bre, d_bim = disc_vjp((sg["a_re"].reshape(SSM_G, SSM_P), sg["a_im"].reshape(SSM_G, SSM_P),
                                                  sg["bb_re"], sg["bb_im"]))
    small_grad = {
        "norm1_g": sg["norm1_g"], "mem_norm_g": sg["mem_norm_g"], "b_gate": sg["b_gate"],
        "ssm_lambda_re": d_lre, "ssm_lambda_im": d_lim, "ssm_log_dt": d_ldt, "ssm_b_re": d_bre, "ssm_b_im": d_bim,
        "ssm_c_re": sg["ssm_c_re"], "ssm_c_im": sg["ssm_c_im"], "ssm_d": sg["ssm_d"], "b_glu": sg["b_glu"],
        "norm2_g": sg["norm2_g"], "final_g": sg["final_g"],
    }
    small_grad = {n: small_grad[n].reshape(w[n].shape) for n in SMALL}

    def squeezed(a):
        return a.reshape(a.shape[1:]) if a.ndim > 2 else a.reshape(1, -1)

    sq = lambda d: [squeezed(d[n]) for n in SMALL]
    small_out = _adam_many(sq(small_grad), sq(w), sq(m), sq(v), 1, "adam_small")
    small = [small_grad] + [{n: a.reshape(w[n].shape) for n, a in zip(SMALL, outs)} for outs in small_out]

    outs = [loss, grad_x[None]]
    for kind in range(4):
        for n in ALL_W:
            outs.append(big[kind][n] if n in BIG else small[kind][n])
    return tuple(outs)
```

```python
import math

import numpy as np
import jax
import jax.numpy as jnp
from jax import lax
from jax.experimental import pallas as pl
from jax.experimental.pallas import tpu as pltpu

F32 = jnp.float32
BF16 = jnp.bfloat16
_MXU = jnp.bfloat16

D_MODEL = 1024
SSM_G, SSM_H, SSM_P = 32, 16, 64
SSM_W = SSM_G * SSM_H
SSM_S = SSM_G * SSM_P
SSM_BD = 4
ATT_E = 64
ATT_HG = 4
ATT_GW = ATT_HG * ATT_E
ATT_WIN = 128
ATT_QB = 8
ATT_QB_FWD = 4
DILATIONS = (1, 4, 16)
MEM_H, MEM_E = 4, 128
MEM_W = MEM_H * MEM_E
ZA_W = SSM_W + 9 * ATT_GW + MEM_W
ZG_W = 3 * D_MODEL
IN_W = ZA_W + ZG_W
RMS_EPS = 1e-6
NEG_INF = -1e30

ADAM_LR, ADAM_B1, ADAM_B2, ADAM_EPS, ADAM_WD, ADAM_STEP = 0.001, 0.9, 0.999, 1e-08, 0.01, 10

N_DEV = 8
PACK_C = 512
_VMEM_LIMIT = 56 * 1024 * 1024
SUBLANES = 16
SCAN_SEG = 128
SCAN_CHAINS = 4
SCAN_UNROLL = 4
SCAN_W = 128

BIG = ("w_in", "w_glu", "w_ssm_br", "w_attn_br", "w_mem_kv", "w_mem_br", "w_o", "w_up", "w_down")
BIG_SHAPE = {
    "w_in": (D_MODEL, IN_W, 1), "w_glu": (SSM_W, SSM_W, 0), "w_ssm_br": (SSM_W, D_MODEL, 1),
    "w_attn_br": (ATT_GW, D_MODEL, 1), "w_mem_kv": (D_MODEL, 2 * MEM_W, 0), "w_mem_br": (MEM_W, D_MODEL, 1),
    "w_o": (D_MODEL, D_MODEL, 0), "w_up": (D_MODEL, 4 * D_MODEL, 1), "w_down": (4 * D_MODEL, D_MODEL, 0),
}
SMALL = ("norm1_g", "mem_norm_g", "b_gate", "ssm_lambda_re", "ssm_lambda_im", "ssm_log_dt", "ssm_b_re",
         "ssm_b_im", "ssm_c_re", "ssm_c_im", "ssm_d", "b_glu", "norm2_g", "final_g")
ALL_W = ("norm1_g", "mem_norm_g", "w_in", "b_gate", "ssm_lambda_re", "ssm_lambda_im", "ssm_log_dt", "ssm_b_re",
         "ssm_b_im", "ssm_c_re", "ssm_c_im", "ssm_d", "w_glu", "b_glu", "w_ssm_br", "w_attn_br", "w_mem_kv",
         "w_mem_br", "w_o", "norm2_g", "w_up", "w_down", "final_g")


def _params(sem):
    return pltpu.CompilerParams(dimension_semantics=sem, vmem_limit_bytes=_VMEM_LIMIT)


def _pick(n, cap):
    if n <= cap:
        return n
    t = (cap // 128) * 128
    while n % t:
        t -= 128
    return t


def _mm(a, b, outs, *, name, ta=False, tb=False, epi=None, mn=(), rows=(), pair2=None, bd=0, n_sums=0,
        tm=1024, tn=1024, tk=2048):
    ab = [a, b] + (list(pair2) if pair2 is not None else [])
    a_shape, b_shape = ab[0].shape, ab[1].shape
    m = a_shape[1] if ta else a_shape[0]
    k = a_shape[0] if ta else a_shape[1]
    n = b_shape[0] if tb else b_shape[1]
    assert k == (b_shape[1] if tb else b_shape[0]), (name, a_shape, b_shape)
    out_n = n
    if bd and ta:
        assert not tb
        tm, tn, tk = m // bd, n // bd, _pick(k, tk)
        grid, out_n = (bd, 1, k // tk), tn
        a_blk = ((tk, tm), lambda i, j, kk: (kk, i))
        b_blk = ((tk, tn), lambda i, j, kk: (kk, i))
        mn_spec = pl.BlockSpec((tm, tn), lambda i, j, kk: (i, 0))
    elif bd:
        tm, tn, tk = _pick(m, tm), n // bd, k // bd
        grid = (m // tm, bd, 1)
        a_blk = ((tm, tk), lambda i, j, kk: (i, j))
        b_blk = ((tn, tk) if tb else (tk, tn), lambda i, j, kk: (j, j))
        mn_spec = pl.BlockSpec((tm, tn), lambda i, j, kk: (i, j))
    else:
        tm, tn, tk = _pick(m, tm), _pick(n, tn), _pick(k, tk)
        grid = (m // tm, n // tn, k // tk)
        a_blk = ((tk, tm), lambda i, j, kk: (kk, i)) if ta else ((tm, tk), lambda i, j, kk: (i, kk))
        b_blk = ((tn, tk), lambda i, j, kk: (j, kk)) if tb else ((tk, tn), lambda i, j, kk: (kk, j))
        mn_spec = pl.BlockSpec((tm, tn), lambda i, j, kk: (i, j))

    ab_specs = [pl.BlockSpec(*(a_blk if q % 2 == 0 else b_blk)) for q in range(len(ab))]
    mn_arrays = [e[0] if isinstance(e, tuple) else e for e in mn]
    mn_specs = [pl.BlockSpec((tm, tn), lambda i, j, kk, c=e[1]: (i, c)) if isinstance(e, tuple) else mn_spec
                for e in mn]
    nk = grid[2]
    row_spec = pl.BlockSpec((1, tn), lambda i, j, kk: (0, j))
    n_ex, n_out = len(mn) + len(rows), len(outs)
    assert n_sums == 0 or (grid[1] == 1 and not bd)
    dims = (((0 if ta else 1,), (1 if tb else 0,)), ((), ()))

    def body(*refs):
        ab_refs, rest = refs[:len(ab)], refs[len(ab):]
        ex, o_refs = rest[:n_ex], rest[n_ex:n_ex + n_out]
        s_refs = rest[n_ex + n_out:n_ex + n_out + n_sums]
        first_row_tile = pl.program_id(0) == 0
        kk = pl.program_id(2)

        pairs = list(zip(ab_refs[0::2], ab_refs[1::2]))

        def product(pair):
            return lax.dot_general(pair[0][...].astype(_MXU), pair[1][...].astype(_MXU), dims,
                                   preferred_element_type=F32)

        def finish(total):
            vals = (total,) if epi is None else epi(total, *[r[...] for r in ex])
            for r, v in zip(o_refs, vals):
                r[...] = v.astype(r.dtype)
            for r, v in zip(s_refs, vals[n_out:]):
                r[...] = jnp.where(first_row_tile, v, r[...] + v)

        if nk == 1:
            total = product(pairs[0])
            for pair in pairs[1:]:
                total = total + product(pair)
            finish(total)
        else:
            acc = rest[-1]

            @pl.when(kk == 0)
            def _():
                acc[...] = jnp.zeros_like(acc)

            for pair in pairs:
                acc[...] += product(pair)

            @pl.when(kk == nk - 1)
            def _():
                finish(acc[...])

    res = pl.pallas_call(
        body, grid=grid,
        in_specs=ab_specs + mn_specs + [row_spec] * len(rows),
        out_specs=[mn_spec] * n_out + [row_spec] * n_sums,
        out_shape=[jax.ShapeDtypeStruct((m, out_n), dt) for dt in outs]
        + [jax.ShapeDtypeStruct((1, out_n), F32)] * n_sums,
        scratch_shapes=[pltpu.VMEM((tm, tn), F32)] if nk > 1 else [],
        compiler_params=_params(("arbitrary" if n_sums else "parallel", "parallel", "arbitrary")), name=name,
    )(*ab, *mn_arrays, *rows)
    return res[0] if n_out + n_sums == 1 else res


def _ew(fn, rows, bcs, out_rows, out_accs, *, name, tm=256):
    r = rows[0].shape[0]
    tm = min(tm, r)
    assert r % tm == 0
    nr, nb, no, na = len(rows), len(bcs), len(out_rows), len(out_accs)

    def body(*refs):
        i = pl.program_id(0)
        r_in, b_in = refs[:nr], refs[nr:nr + nb]
        o_r, o_a = refs[nr + nb:nr + nb + no], refs[nr + nb + no:]
        outs, accs = fn([x[...] for x in r_in], [x[...] for x in b_in])
        for ref, v in zip(o_r, outs):
            ref[...] = v.astype(ref.dtype)
        if na:
            @pl.when(i == 0)
            def _():
                for ref in o_a:
                    ref[...] = jnp.zeros_like(ref)

            for ref, v in zip(o_a, accs):
                ref[...] += v

    res = pl.pallas_call(
        body, grid=(r // tm,),
        in_specs=[pl.BlockSpec((tm, x.shape[1]), lambda i: (i, 0)) for x in rows]
        + [pl.BlockSpec((1, x.shape[1]), lambda i: (0, 0)) for x in bcs],
        out_specs=[pl.BlockSpec((tm, c), lambda i: (i, 0)) for c, _ in out_rows]
        + [pl.BlockSpec((1, c), lambda i: (0, 0)) for c in out_accs],
        out_shape=[jax.ShapeDtypeStruct((r, c), dt) for c, dt in out_rows]
        + [jax.ShapeDtypeStruct((1, c), F32) for c in out_accs],
        compiler_params=_params(("arbitrary",)), name=name,
    )(*rows, *bcs)
    return res


def _colsum(x):
    return jnp.sum(x, axis=0, keepdims=True)


def _sigmoid(x):
    return 0.5 * jnp.tanh(0.5 * x) + 0.5


def _rms_bwd_tile(xv, dv, g):
    rs = lax.rsqrt(jnp.mean(xv * xv, axis=-1, keepdims=True) + RMS_EPS)
    gd = dv * g
    dx = rs * gd - xv * (rs * rs * rs) * jnp.mean(gd * xv, axis=-1, keepdims=True)
    return dx, _colsum(dv * xv * rs)


def _rms_fwd(x, g, name):
    def fn(r, b):
        xv = r[0]
        rs = lax.rsqrt(jnp.mean(xv * xv, axis=-1, keepdims=True) + RMS_EPS)
        return [xv * rs * b[0]], []
    return _ew(fn, [x], [g], [(x.shape[1], BF16)], [], name=name, tm=1024)[0]


def _rms_bwd(x, dn, res, g, name):
    def fn(r, b):
        dx, dg = _rms_bwd_tile(r[0], r[1], b[0])
        if res is not None:
            dx = dx + r[2]
        return [dx], [dg]
    rows = [x, dn] + ([res] if res is not None else [])
    return _ew(fn, rows, [g], [(x.shape[1], F32)], [x.shape[1]], name=name)


def _scan_order(x):
    l, c = x.shape
    return x.reshape(l // (SUBLANES * SCAN_SEG), SUBLANES, SCAN_SEG, c).transpose(0, 2, 1, 3).reshape(l, c)


def _time_order(x):
    l, c = x.shape
    return x.reshape(l // (SUBLANES * SCAN_SEG), SCAN_SEG, SUBLANES, c).transpose(0, 2, 1, 3).reshape(l, c)


def _ssm_scan(x, w_re, w_im, a_pair, *, reverse, s_fwd=None, u=None, name):
    l = x.shape[0]
    seg, w = SCAN_SEG, SCAN_W
    bd_w = SSM_W // SSM_BD
    tiles_per_bd = SSM_S // SSM_BD // w
    nch = min(SCAN_CHAINS, l // (SUBLANES * seg))
    chain_rows = SUBLANES * seg
    tb = nch * chain_rows
    nt = l // tb
    with_da = s_fwd is not None
    assert reverse or not with_da

    def tt(t):
        return nt - 1 - t if reverse else t

    def body(*refs):
        if with_da:
            (x_ref, wr_ref, wi_ref, a_ref, sf_ref, sp_ref, u_ref, s_ref, da_ref, dw_ref, dx_ref,
             p_ref, c_ref, b_scr, l_scr) = refs
        else:
            x_ref, wr_ref, wi_ref, a_ref, s_ref, p_ref, c_ref, b_scr, l_scr = refs
        t_blk = pl.program_id(1)
        ar, ai = a_ref[0], a_ref[1]

        @pl.when(t_blk == 0)
        def _():
            def pstep(i, carry):
                pr, pi = carry
                p_ref[0, pl.ds(i, 1), :] = pr
                p_ref[1, pl.ds(i, 1), :] = pi
                return pr * ar - pi * ai, pr * ai + pi * ar

            lax.fori_loop(0, seg, pstep, (ar, ai))
            c_ref[...] = jnp.zeros_like(c_ref)
            if with_da:
                da_ref[...] = jnp.zeros_like(da_ref)
                dw_ref[...] = jnp.zeros_like(dw_ref)
                dx_ref[...] = jnp.zeros_like(dx_ref)

        xb = x_ref[...].astype(_MXU)
        b_scr[:, :w] = jnp.dot(xb, wr_ref[...], preferred_element_type=F32)
        b_scr[:, w:] = jnp.dot(xb, wi_ref[...], preferred_element_type=F32)
        arb, aib = jnp.broadcast_to(ar, (SUBLANES, w)), jnp.broadcast_to(ai, (SUBLANES, w))
        zero = jnp.zeros((SUBLANES, w), F32)

        def tile(g, step):
            return pl.ds(pl.multiple_of(g * chain_rows + step * SUBLANES, SUBLANES), SUBLANES)

        def rows(g, i):
            return tile(g, seg - 1 - i if reverse else i)

        def local_step(i, carry):
            out = []
            for g in range(nch):
                sr, si = carry[2 * g], carry[2 * g + 1]
                idx = rows(g, i)
                sr, si = arb * sr - aib * si + b_scr[idx, :w], arb * si + aib * sr + b_scr[idx, w:]
                l_scr[idx, :w] = sr
                l_scr[idx, w:] = si
                out += [sr, si]
            return tuple(out)

        def unrolled(step_fn, first):
            def trip(q, carry):
                for r in range(SCAN_UNROLL):
                    carry = step_fn(first + q * SCAN_UNROLL + r, carry)
                return carry
            return trip

        ends = lax.fori_loop(0, seg // SCAN_UNROLL, unrolled(local_step, 0), (zero,) * (2 * nch))

        a_seg_r, a_seg_i = p_ref[0, seg - 1:seg, :], p_ref[1, seg - 1:seg, :]
        cr, ci = c_ref[0], c_ref[1]
        sub = lax.broadcasted_iota(jnp.int32, (SUBLANES, w), 0)
        ins = [[zero, zero] for _ in range(nch)]
        order = [(g, k) for g in range(nch) for k in range(SUBLANES)]
        for g, k in (order[::-1] if reverse else order):
            ins[g] = [jnp.where(sub == k, cr, ins[g][0]), jnp.where(sub == k, ci, ins[g][1])]
            er, ei = ends[2 * g][k:k + 1], ends[2 * g + 1][k:k + 1]
            cr, ci = er + a_seg_r * cr - a_seg_i * ci, ei + a_seg_r * ci + a_seg_i * cr
        c_ref[0] = cr
        c_ref[1] = ci

        def fix(g, i):
            idx = rows(g, i)
            pr, pi = p_ref[0, pl.ds(i, 1), :], p_ref[1, pl.ds(i, 1), :]
            sr = l_scr[idx, :w] + pr * ins[g][0] - pi * ins[g][1]
            si = l_scr[idx, w:] + pr * ins[g][1] + pi * ins[g][0]
            s_ref[idx, :w] = sr.astype(s_ref.dtype)
            s_ref[idx, w:] = si.astype(s_ref.dtype)
            return sr, si

        if not with_da:
            def fix_step(i, carry):
                for g in range(nch):
                    fix(g, i)
                return carry

            lax.fori_loop(0, seg // SCAN_UNROLL, unrolled(fix_step, 0), 0)
        else:
            def adj_step(i, acc):
                acc_r, acc_i = acc
                for g in range(nch):
                    lr, li = fix(g, i)
                    prev = tile(g, seg - 2 - i)
                    fr, fi = sf_ref[prev, :w].astype(F32), sf_ref[prev, w:].astype(F32)
                    acc_r, acc_i = acc_r + lr * fr + li * fi, acc_i + li * fr - lr * fi
                return acc_r, acc_i

            acc = lax.fori_loop(0, seg // SCAN_UNROLL - 1, unrolled(adj_step, 0), (zero, zero))
            for i in range(seg - SCAN_UNROLL, seg - 1):
                acc = adj_step(i, acc)
            acc_r, acc_i = acc
            first_block = tt(t_blk) == 0
            for g in range(nch):
                lr, li = fix(g, seg - 1)
                seg_ends = tile(g, seg - 1)
                if g == 0:
                    pvr = jnp.where(first_block, 0.0, sp_ref[SUBLANES - 1:SUBLANES, :w].astype(F32))
                    pvi = jnp.where(first_block, 0.0, sp_ref[SUBLANES - 1:SUBLANES, w:].astype(F32))
                else:
                    pvr = sf_ref[g * chain_rows - 1:g * chain_rows, :w].astype(F32)
                    pvi = sf_ref[g * chain_rows - 1:g * chain_rows, w:].astype(F32)
                fr = jnp.where(sub == 0, pvr, pltpu.roll(sf_ref[seg_ends, :w].astype(F32), 1, 0))
                fi = jnp.where(sub == 0, pvi, pltpu.roll(sf_ref[seg_ends, w:].astype(F32), 1, 0))
                acc_r = acc_r + lr * fr + li * fi
                acc_i = acc_i + li * fr - lr * fi
            da_ref[0] += jnp.sum(acc_r, axis=0, keepdims=True)
            da_ref[1] += jnp.sum(acc_i, axis=0, keepdims=True)
            dw_ref[...] += _tn_dot(u_ref[...], s_ref[...])
            dx_ref[...] += _tn_dot(xb, sf_ref[...])

    x_spec = pl.BlockSpec((tb, bd_w), lambda j, t: (tt(t), j // tiles_per_bd))
    w_spec = pl.BlockSpec((bd_w, w), lambda j, t: (j // tiles_per_bd, j))
    d_spec = pl.BlockSpec((bd_w, 2 * w), lambda j, t: (j // tiles_per_bd, j % tiles_per_bd))
    a_spec = pl.BlockSpec((2, 1, w), lambda j, t: (0, 0, j))
    s_spec = pl.BlockSpec((tb, 2 * w), lambda j, t: (tt(t), j))
    in_specs, args = [x_spec, w_spec, w_spec, a_spec], [x, w_re, w_im, a_pair]
    out_specs, out_shape = [s_spec], [jax.ShapeDtypeStruct((l, 2 * SSM_S), BF16)]
    scratch = [pltpu.VMEM((2, seg, w), F32), pltpu.VMEM((2, 1, w), F32)] + [pltpu.VMEM((tb, 2 * w), F32)] * 2
    if with_da:
        in_specs += [s_spec, pl.BlockSpec((SUBLANES, 2 * w),
                                          lambda j, t: (jnp.maximum(tt(t) * (tb // SUBLANES) - 1, 0), j)),
                     x_spec]
        args += [s_fwd, s_fwd, u]
        out_specs += [a_spec, d_spec, d_spec]
        out_shape += ([jax.ShapeDtypeStruct((2, 1, SSM_S), F32)]
                      + [jax.ShapeDtypeStruct((SSM_W, 2 * SSM_S // SSM_BD), F32)] * 2)
    res = pl.pallas_call(
        body, grid=(SSM_S // w, nt), in_specs=in_specs, out_specs=out_specs, out_shape=out_shape,
        scratch_shapes=scratch, compiler_params=_params(("parallel", "arbitrary")), name=name,
    )(*args)
    return res if with_da else res[0]


def _nt_dot(x, y):
    return lax.dot_general(x.astype(_MXU), y.astype(_MXU), (((1,), (1,)), ((), ())), preferred_element_type=F32)


def _tn_dot(x, y):
    return lax.dot_general(x.astype(_MXU), y.astype(_MXU), (((0,), (0,)), ((), ())), preferred_element_type=F32)


def _nn_dot(x, y):
    return jnp.dot(x.astype(_MXU), y.astype(_MXU), preferred_element_type=F32)


def _attn_mask2(gb, nb):
    qi = lax.broadcasted_iota(jnp.int32, (ATT_WIN, 2 * ATT_WIN), 0)
    c = lax.broadcasted_iota(jnp.int32, (ATT_WIN, 2 * ATT_WIN), 1)
    has_prev = (gb % nb) != 0
    prev_ok = jnp.logical_and(jnp.logical_and(c < ATT_WIN, c >= qi), has_prev)
    own_ok = jnp.logical_and(c >= ATT_WIN, c - ATT_WIN <= qi)
    return jnp.logical_or(prev_ok, own_ok)


def _attn_specs(qb):
    cur = pl.BlockSpec((qb * ATT_WIN, ATT_GW), lambda i: (i, 0))
    prev = pl.BlockSpec((ATT_WIN, ATT_GW), lambda i: (jnp.maximum(qb * i - 1, 0), 0))
    return cur, prev


def _attn_fwd(q, k, v, nb, name):
    l = q.shape[0]
    scale = ATT_E ** -0.5
    w = ATT_WIN

    qb = ATT_QB_FWD

    def body(q_ref, kc_ref, kp_ref, vc_ref, vp_ref, o_ref, lse_ref):
        i = pl.program_id(0)
        masks = [_attn_mask2(qb * i + b, nb) for b in range(qb)]
        for h in range(ATT_HG):
            sl = slice(h * ATT_E, (h + 1) * ATT_E)
            k_ext = jnp.concatenate([kp_ref[:, sl], kc_ref[:, sl]], axis=0)
            v_ext = jnp.concatenate([vp_ref[:, sl], vc_ref[:, sl]], axis=0)
            for b in range(qb):
                r, kr = slice(b * w, (b + 1) * w), slice(b * w, (b + 2) * w)
                s = jnp.where(masks[b], _nt_dot(q_ref[r, sl], k_ext[kr]) * scale, NEG_INF)
                mx = jnp.max(s, axis=-1, keepdims=True)
                p = jnp.exp(s - mx)
                den = jnp.sum(p, axis=-1, keepdims=True)
                o_ref[r, sl] = _nn_dot(p, v_ext[kr]) * (1.0 / den)
                lse_ref[r, sl] = jnp.broadcast_to(mx + jnp.log(den), (w, ATT_E))

    cur, prev = _attn_specs(qb)
    return pl.pallas_call(
        body, grid=(l // (qb * w),), in_specs=[cur, cur, prev, cur, prev], out_specs=[cur, cur],
        out_shape=[jax.ShapeDtypeStruct((l, ATT_GW), F32)] * 2,
        compiler_params=_params(("parallel",)), name=name,
    )(q, k, k, v, v)


def _attn_bwd(q, k, v, do, lse, dd, nb, name):
    l = q.shape[0]
    scale = ATT_E ** -0.5
    w = ATT_WIN
    nblk = l // w

    def body(q_ref, kc_ref, kp_ref, vc_ref, vp_ref, do_ref, lse_ref, dd_ref, qn_ref, don_ref, lsen_ref, ddn_ref,
             dq_ref, dk_ref, dv_ref, dk_acc, dv_acc):
        i = pl.program_id(0)
        masks = [_attn_mask2(ATT_QB * i + b, nb) for b in range(ATT_QB)]
        nxt = ATT_QB * (i + 1)
        nxt_attends = jnp.logical_and(nxt < nblk, (nxt % nb) != 0)
        qi = lax.broadcasted_iota(jnp.int32, (w, w), 0)
        kj = lax.broadcasted_iota(jnp.int32, (w, w), 1)
        mask_n = jnp.logical_and(kj >= qi, nxt_attends)
        dk_acc[...] = jnp.zeros_like(dk_acc)
        dv_acc[...] = jnp.zeros_like(dv_acc)
        for h in range(ATT_HG):
            sl, col = slice(h * ATT_E, (h + 1) * ATT_E), slice(h * ATT_E, h * ATT_E + 1)
            k_ext = jnp.concatenate([kp_ref[:, sl], kc_ref[:, sl]], axis=0)
            v_ext = jnp.concatenate([vp_ref[:, sl], vc_ref[:, sl]], axis=0)
            for b in range(ATT_QB):
                r, kr = slice(b * w, (b + 1) * w), slice(b * w, (b + 2) * w)
                qh, doh, k2, v2 = q_ref[r, sl], do_ref[r, sl], k_ext[kr], v_ext[kr]
                p = jnp.where(masks[b], jnp.exp(_nt_dot(qh, k2) * scale - lse_ref[r, col]), 0.0)
                ds = p * (_nt_dot(doh, v2) - dd_ref[r, col]) * scale
                dq_ref[r, sl] = _nn_dot(ds, k2).astype(dq_ref.dtype)
                dk2, dv2 = _tn_dot(ds, qh), _tn_dot(p, doh)
                dk_acc[r, sl] += dk2[w:]
                dv_acc[r, sl] += dv2[w:]
                if b > 0:
                    rp = slice((b - 1) * w, b * w)
                    dk_acc[rp, sl] += dk2[:w]
                    dv_acc[rp, sl] += dv2[:w]
            last = slice((ATT_QB - 1) * w, ATT_QB * w)
            kl, vl, qn, don = kc_ref[last, sl], vc_ref[last, sl], qn_ref[:, sl], don_ref[:, sl]
            pn = jnp.where(mask_n, jnp.exp(_nt_dot(qn, kl) * scale - lsen_ref[:, col]), 0.0)
            dsn = pn * (_nt_dot(don, vl) - ddn_ref[:, col]) * scale
            dk_acc[last, sl] += _tn_dot(dsn, qn)
            dv_acc[last, sl] += _tn_dot(pn, don)
        dk_ref[...] = dk_acc[...].astype(dk_ref.dtype)
        dv_ref[...] = dv_acc[...].astype(dv_ref.dtype)

    cur, prev = _attn_specs(ATT_QB)
    nxt_spec = pl.BlockSpec((w, ATT_GW), lambda i: (jnp.minimum(ATT_QB * (i + 1), nblk - 1), 0))
    return pl.pallas_call(
        body, grid=(l // (ATT_QB * w),),
        in_specs=[cur, cur, prev, cur, prev, cur, cur, cur, nxt_spec, nxt_spec, nxt_spec, nxt_spec],
        out_specs=[cur] * 3, out_shape=[jax.ShapeDtypeStruct((l, ATT_GW), BF16)] * 3,
        scratch_shapes=[pltpu.VMEM((ATT_QB * w, ATT_GW), F32)] * 2,
        compiler_params=_params(("parallel",)), name=name,
    )(q, k, k, v, v, do, lse, dd, q, do, lse, dd)


def _to_perm(a, d):
    if d == 1:
        return a
    l, c = a.shape
    return a.reshape(l // d, d, c).transpose(1, 0, 2).reshape(l, c)


def _from_perm(a, d):
    if d == 1:
        return a
    l, c = a.shape
    return a.reshape(d, l // d, c).transpose(1, 0, 2).reshape(l, c)


def _mem_probs(qh, kh):
    s = _nt_dot(qh, kh) * (MEM_E ** -0.5)
    e = jnp.exp(s - jnp.max(s, axis=-1, keepdims=True))
    return e * (1.0 / jnp.sum(e, axis=-1, keepdims=True))


def _mem_fwd(mq, kv, name, tm=1024):
    l, nm = mq.shape[0], kv.shape[0]

    def body(q_ref, kv_ref, o_ref):
        for h in range(MEM_H):
            sl = slice(h * MEM_E, (h + 1) * MEM_E)
            p = _mem_probs(q_ref[:, sl], kv_ref[:, sl])
            o_ref[:, sl] = _nn_dot(p, kv_ref[:, MEM_W + h * MEM_E:MEM_W + (h + 1) * MEM_E]).astype(o_ref.dtype)

    return pl.pallas_call(
        body, grid=(l // tm,),
        in_specs=[pl.BlockSpec((tm, MEM_W), lambda i: (i, 0)), pl.BlockSpec((nm, 2 * MEM_W), lambda i: (0, 0))],
        out_specs=pl.BlockSpec((tm, MEM_W), lambda i: (i, 0)),
        out_shape=jax.ShapeDtypeStruct((l, MEM_W), BF16),
        compiler_params=_params(("parallel",)), name=name,
    )(mq, kv)


def _mem_bwd(mq, kv, dmo, name, tm=1024):
    l, nm = mq.shape[0], kv.shape[0]
    scale = MEM_E ** -0.5

    def body(q_ref, kv_ref, do_ref, dq_ref, dkv_ref):
        @pl.when(pl.program_id(0) == 0)
        def _():
            dkv_ref[...] = jnp.zeros_like(dkv_ref)

        for h in range(MEM_H):
            sl = slice(h * MEM_E, (h + 1) * MEM_E)
            vsl = slice(MEM_W + h * MEM_E, MEM_W + (h + 1) * MEM_E)
            qh, kh, vh, doh = q_ref[:, sl], kv_ref[:, sl], kv_ref[:, vsl], do_ref[:, sl]
            p = _mem_probs(qh, kh)
            dp = _nt_dot(doh, vh)
            ds = p * (dp - jnp.sum(dp * p, axis=-1, keepdims=True)) * scale
            dq_ref[:, sl] = _nn_dot(ds, kh).astype(dq_ref.dtype)
            dkv_ref[:, sl] += _tn_dot(ds, qh)
            dkv_ref[:, vsl] += _tn_dot(p, doh)

    row = pl.BlockSpec((tm, MEM_W), lambda i: (i, 0))
    full = pl.BlockSpec((nm, 2 * MEM_W), lambda i: (0, 0))
    return pl.pallas_call(
        body, grid=(l // tm,), in_specs=[row, full, row], out_specs=[row, full],
        out_shape=[jax.ShapeDtypeStruct((l, MEM_W), BF16), jax.ShapeDtypeStruct((nm, 2 * MEM_W), F32)],
        compiler_params=_params(("arbitrary",)), name=name,
    )(mq, kv, dmo)


def _gated_out_proj(zg, branches, b_gate, w_o, x, g2, name, tm=512):
    l, d = x.shape
    nbr = len(branches)

    def body(zg_ref, *rest):
        br_refs, (bg_ref, w_ref, x_ref, g2_ref, m_ref, h_ref, n_ref) = rest[:nbr], rest[nbr:]
        merged = jnp.zeros((tm, d), F32)
        for i, br_ref in enumerate(br_refs):
            cols = slice(i * d, (i + 1) * d)
            merged += _sigmoid(zg_ref[:, cols].astype(F32) + bg_ref[:, cols]) * br_ref[...].astype(F32)
        mb = merged.astype(BF16)
        m_ref[...] = mb
        hv = jnp.dot(mb.astype(_MXU), w_ref[...].astype(_MXU), preferred_element_type=F32) + x_ref[...]
        h_ref[...] = hv
        rs = lax.rsqrt(jnp.mean(hv * hv, axis=-1, keepdims=True) + RMS_EPS)
        n_ref[...] = (hv * rs * g2_ref[...]).astype(n_ref.dtype)

    row = lambda c: pl.BlockSpec((tm, c), lambda i: (i, 0))
    full = lambda a: pl.BlockSpec(a.shape, lambda i: (0, 0))
    return pl.pallas_call(
        body, grid=(l // tm,),
        in_specs=[row(nbr * d)] + [row(d)] * nbr + [full(b_gate), full(w_o), row(d), full(g2)],
        out_specs=[row(d)] * 3,
        out_shape=[jax.ShapeDtypeStruct((l, d), BF16), jax.ShapeDtypeStruct((l, d), F32),
                   jax.ShapeDtypeStruct((l, d), BF16)],
        compiler_params=_params(("parallel",)), name=name,
    )(zg, *branches, b_gate, w_o, x, g2)


def _discretize(lam_re, lam_im, log_dt, b_re, b_im):
    dt = jnp.exp(log_dt)[:, None]
    mag = jnp.exp(lam_re * dt)
    a_re, a_im = mag * jnp.cos(lam_im * dt), mag * jnp.sin(lam_im * dt)
    nr, ni = a_re - 1.0, a_im
    den = lam_re * lam_re + lam_im * lam_im
    coef_re = (nr * lam_re + ni * lam_im) / den
    coef_im = (ni * lam_re - nr * lam_im) / den
    bb_re = coef_re[..., None] * b_re - coef_im[..., None] * b_im
    bb_im = coef_re[..., None] * b_im + coef_im[..., None] * b_re
    return a_re, a_im, bb_re, bb_im


def _tiled(re, im):
    r = re.shape[0]
    both = jnp.concatenate([re.reshape(r, -1, SCAN_W), im.reshape(r, -1, SCAN_W)], axis=2)
    return both.reshape(r, 2 * re.shape[1])


def _untiled(x):
    r = x.shape[0]
    t = x.reshape(r, -1, 2 * SCAN_W)
    return t[:, :, :SCAN_W].reshape(r, -1), t[:, :, SCAN_W:].reshape(r, -1)


def _bd_in(bb):
    return jnp.einsum("gph,gk->ghkp", bb, jnp.eye(SSM_G, dtype=bb.dtype)).reshape(SSM_W, SSM_S)


def _bd_diag(x):
    gb = SSM_G // SSM_BD
    t = x.reshape(SSM_BD, gb, SSM_H, gb, SSM_P)
    return jnp.einsum("bghgp->bghp", t).reshape(SSM_G, SSM_H, SSM_P)


_ANY = pl.BlockSpec(memory_space=pl.ANY)
_MESH = pl.DeviceIdType.MESH


def _allgather(x, name):
    def body(x_ref, out_ref, send_sems, recv_sems, local_sem):
        mx, my, mc = lax.axis_index("x"), lax.axis_index("y"), lax.axis_index("c")
        me, sibling = (mx, my, mc), (mx, my, 1 - mc)
        chips = [(1 - mx, my), (mx, 1 - my), (1 - mx, 1 - my)]

        def blk(px, py, pc):
            return out_ref.at[4 * px + 2 * py + pc]

        def copy(k, block, to, src=None):
            return pltpu.make_async_remote_copy(
                src_ref=blk(*block) if src is None else src, dst_ref=blk(*block),
                send_sem=send_sems.at[k], recv_sem=recv_sems.at[k], device_id=to, device_id_type=_MESH)

        mine = pltpu.make_async_copy(x_ref, blk(*me), local_sem)
        mine.start()
        first = [copy(0, me, sibling, src=x_ref)]
        first += [copy(1 + j, me, (*chip, mc), src=x_ref) for j, chip in enumerate(chips)]
        for cp in first:
            cp.start()
        passed = [copy(4 + j, (*chip, mc), sibling) for j, chip in enumerate(chips)]
        for j, chip in enumerate(chips):
            copy(1 + j, (*chip, mc), me).wait_recv()
            passed[j].start()
        copy(0, sibling, me).wait_recv()
        for j, chip in enumerate(chips):
            copy(4 + j, (*chip, 1 - mc), me).wait_recv()
        for cp in first + passed:
            cp.wait_send()
        mine.wait()

    return pl.pallas_call(
        body, out_shape=jax.ShapeDtypeStruct((N_DEV,) + x.shape, x.dtype), in_specs=[_ANY], out_specs=_ANY,
        scratch_shapes=[pltpu.SemaphoreType.DMA((7,)), pltpu.SemaphoreType.DMA((7,)), pltpu.SemaphoreType.DMA],
        name=name,
    )(x)


def _pair_exchange(g, name):
    def body(g_ref, out_ref, send_sems, recv_sems):
        mx, my, mc = lax.axis_index("x"), lax.axis_index("y"), lax.axis_index("c")
        copies = [pltpu.make_async_remote_copy(
            src_ref=g_ref.at[2 * k + (1 - mc)], dst_ref=out_ref.at[k], send_sem=send_sems.at[k],
            recv_sem=recv_sems.at[k], device_id=(mx, my, 1 - mc), device_id_type=_MESH) for k in range(4)]
        for cp in copies:
            cp.start()
        for cp in copies:
            cp.wait()

    return pl.pallas_call(
        body, out_shape=jax.ShapeDtypeStruct((4,) + g.shape[1:], g.dtype), in_specs=[_ANY], out_specs=_ANY,
        scratch_shapes=[pltpu.SemaphoreType.DMA((4,)), pltpu.SemaphoreType.DMA((4,))], name=name,
    )(g)


_HBM = pl.BlockSpec(memory_space=pltpu.HBM)
_SEM = pl.BlockSpec(memory_space=pltpu.SEMAPHORE)
_EFFECT = pltpu.SideEffectType.DATAFLOW_SIDE_EFFECTING
_TOKEN = jax.ShapeDtypeStruct((8, 128), F32)


def _peer(rel):
    pos = (lax.axis_index("x"), lax.axis_index("y"), lax.axis_index("c"))
    return tuple(1 - p if (rel >> (2 - i)) & 1 else p for i, p in enumerate(pos))


def _index_of(dev):
    return 4 * dev[0] + 2 * dev[1] + dev[2]


def _split_copies(src_ref, land_ref, sems, plan):
    n = len(plan)
    return [pltpu.make_async_remote_copy(
        src_ref=src_ref if s is None else src_ref.at[s], dst_ref=land_ref.at[d], send_sem=sems[k],
        recv_sem=sems[n + k], device_id=peer, device_id_type=_MESH) for k, (s, d, peer) in enumerate(plan)]


def _split_start(src, n_land, plan_fn, after, name):
    blk = src.shape[-2:]
    land = lax.empty((n_land,) + blk, src.dtype)
    n = len(plan_fn())

    def body(src_ref, land_ref, after_ref, *outs):
        for cp in _split_copies(src_ref, land_ref, outs[:2 * n], plan_fn()):
            cp.start()
        outs[2 * n + 2][...] = jnp.zeros_like(outs[2 * n + 2])

    res = pl.pallas_call(
        body, name=name,
        out_shape=(pltpu.SemaphoreType.DMA(()),) * (2 * n)
        + (pltpu.HBM(src.shape, src.dtype), pltpu.HBM(land.shape, land.dtype), _TOKEN),
        in_specs=(_HBM, _HBM, _ANY),
        out_specs=(_SEM,) * (2 * n) + (_HBM, _HBM, pl.BlockSpec(memory_space=pltpu.VMEM)),
        input_output_aliases={0: 2 * n, 1: 2 * n + 1},
        compiler_params=pltpu.CompilerParams(has_side_effects=_EFFECT),
    )(pltpu.with_memory_space_constraint(src, pltpu.HBM), pltpu.with_memory_space_constraint(land, pltpu.HBM), after)
    return res[:2 * n], res[2 * n], res[2 * n + 1], res[2 * n + 2]


def _split_wait(sems, src, land, plan_fn, after, name):
    n = len(sems) // 2

    def body(src_ref, land_ref, *rest):
        for cp in _split_copies(src_ref, land_ref, rest[:2 * n], plan_fn()):
            cp.wait_send()
            cp.wait_recv()

    return pl.pallas_call(
        body, name=name,
        out_shape=(pltpu.HBM(src.shape, src.dtype), pltpu.HBM(land.shape, land.dtype)),
        in_specs=(_HBM, _HBM) + (_SEM,) * (2 * n) + (_ANY,), out_specs=(_HBM, _HBM),
        input_output_aliases={0: 0, 1: 1},
        compiler_params=pltpu.CompilerParams(has_side_effects=_EFFECT),
    )(src, land, *sems, after)


def _gather_plan():
    me = _index_of(_peer(0))
    return [(None, me, _peer(rel)) for rel in range(1, N_DEV)]


def _gather_wait_plan():
    return [(None, _index_of(_peer(rel)), _peer(rel)) for rel in range(1, N_DEV)]


def _chip_plan():
    return [(_index_of(_peer(rel)) // 2, j, _peer(rel)) for j, rel in enumerate((4, 2, 6))]


def _owner_plan():
    return [(_index_of(_peer(rel)), rel - 1, _peer(rel)) for rel in range(1, N_DEV)]


def _pair_sum(g, t1, my_c, name, tr):
    _, r, c = g.shape

    def body(c_ref, g_ref, t_ref, o_ref, ob_ref):
        s = g_ref[...] + t_ref[...]
        o_ref[...] = s
        ob_ref[...] = s.astype(BF16)

    blk = pl.BlockSpec((None, tr, c), lambda k, i, cr: (k, i, 0))
    return pl.pallas_call(
        body,
        grid_spec=pltpu.PrefetchScalarGridSpec(
            num_scalar_prefetch=1, grid=(4, r // tr),
            in_specs=[pl.BlockSpec((None, tr, c), lambda k, i, cr: (2 * k + cr[0], i, 0)), blk],
            out_specs=[blk, blk]),
        out_shape=[jax.ShapeDtypeStruct((4, r, c), F32), jax.ShapeDtypeStruct((4, r, c), BF16)],
        compiler_params=_params(("parallel", "parallel")), name=name,
    )(my_c, g, t1)


def _adam_math(g, w, m, v):
    m = ADAM_B1 * m + (1.0 - ADAM_B1) * g
    v = ADAM_B2 * v + (1.0 - ADAM_B2) * (g * g)
    m_hat = m / (1.0 - ADAM_B1 ** ADAM_STEP)
    v_hat = v / (1.0 - ADAM_B2 ** ADAM_STEP)
    delta = -ADAM_LR * (m_hat / (jnp.sqrt(v_hat) + ADAM_EPS) + ADAM_WD * w)
    return delta, m, v


def _grad_sum(own, own_index, recv, name, tr):
    _, r, c = own.shape
    n = recv.shape[0]

    def body(k_ref, own_ref, *rest):
        g = own_ref[...]
        for recv_ref in rest[:n]:
            g = g + recv_ref[...].astype(F32)
        rest[n][...] = g

    def slot(j):
        return pl.BlockSpec((None, tr, c), lambda i, kr: (j, i, 0))

    return pl.pallas_call(
        body,
        grid_spec=pltpu.PrefetchScalarGridSpec(
            num_scalar_prefetch=1, grid=(r // tr,),
            in_specs=[pl.BlockSpec((None, tr, c), lambda i, kr: (kr[0], i, 0))] + [slot(j) for j in range(n)],
            out_specs=pl.BlockSpec((tr, c), lambda i, kr: (i, 0))),
        out_shape=jax.ShapeDtypeStruct((r, c), F32),
        compiler_params=_params(("parallel",)), name=name,
    )(own_index, own, *([recv] * n))


def _adam_many(g, w, m, v, row_tiles, name):
    n = len(g)

    def body(*refs):
        ins, outs = refs[:4 * n], refs[4 * n:]
        for i in range(n):
            res = _adam_math(ins[i][...], ins[n + i][...], ins[2 * n + i][...], ins[3 * n + i][...])
            for kind in range(3):
                outs[kind * n + i][...] = res[kind]

    def spec(a):
        blk = (a.shape[0] // row_tiles,) + a.shape[1:]
        return pl.BlockSpec(blk, lambda t, nd=a.ndim: (t,) + (0,) * (nd - 1))

    specs = [spec(a) for a in g]
    res = pl.pallas_call(
        body, grid=(row_tiles,), in_specs=specs * 4, out_specs=specs * 3,
        out_shape=[jax.ShapeDtypeStruct(a.shape, F32) for a in g] * 3,
        compiler_params=_params(("parallel",)), name=name,
    )(*g, *w, *m, *v)
    return res[:n], res[n:2 * n], res[2 * n:]


def _sum8(g8, name):
    _, r, c = g8.shape

    def body(g_ref, o_ref):
        acc = g_ref[0]
        for j in range(1, N_DEV):
            acc = acc + g_ref[j]
        o_ref[...] = acc

    return pl.pallas_call(
        body, grid=(1,), in_specs=[pl.BlockSpec((N_DEV, r, c), lambda i: (0, 0, 0))],
        out_specs=pl.BlockSpec((r, c), lambda i: (0, 0)), out_shape=jax.ShapeDtypeStruct((r, c), F32),
        compiler_params=_params(("arbitrary",)), name=name,
    )(g8)


def _pack(arrs, pad_rows=8):
    flat = jnp.concatenate([a.reshape(-1) for a in arrs])
    n = flat.shape[0]
    q = PACK_C * pad_rows
    tot = -(-n // q) * q
    if tot != n:
        flat = jnp.concatenate([flat, jnp.zeros((tot - n,), flat.dtype)])
    return flat.reshape(tot // PACK_C, PACK_C)


def _unpack(buf, shapes):
    flat = buf.reshape(-1)
    out, off = [], 0
    for s in shapes:
        n = int(np.prod(s))
        out.append(flat[off:off + n].reshape(s))
        off += n
    return out


GROUPS = (("w_in",),
          ("w_glu", "w_ssm_br", "w_mem_br", "w_attn_br"),
          ("w_up", "w_down"),
          ("w_mem_kv", "w_o"))
GROUP_TR = (400, 384, 512, 256)
MLP_GROUP = 2
MIXER_GROUPS = (1, 3)
ATTN_BR_FOLD = 2


def _stored_shape(name):
    r, c, ax = BIG_SHAPE[name]
    rows, cols = (r // N_DEV, c) if ax == 0 else (c // N_DEV, r)
    return (rows // ATTN_BR_FOLD, cols * ATTN_BR_FOLD) if name == "w_attn_br" else (rows, cols)


def _stored(shard, name):
    a = shard[0].T if BIG_SHAPE[name][2] == 1 else shard[0]
    return a.reshape(_stored_shape(name))


def _unstored(a, name):
    r, c, ax = BIG_SHAPE[name]
    if ax == 0:
        return a.reshape(1, r // N_DEV, c)
    return a.reshape(c // N_DEV, r).T[None]


def _pack_group(d, names):
    return jnp.concatenate([_stored(d[n], n) for n in names], axis=0)


def _split_group(buf, names):
    out, off = {}, 0
    for n in names:
        rows = _stored_shape(n)[0]
        out[n] = buf[..., off:off + rows, :]
        off += rows
    return out


def _full_stored(stacked, name):
    r, c, ax = BIG_SHAPE[name]
    return stacked.reshape((r, c) if ax == 0 else (c, r))


def _stacked_stored(full, name):
    return full.reshape((N_DEV,) + _stored_shape(name))


def _gelu_parts(x):
    c0, c1 = math.sqrt(2.0 / math.pi), 0.044715
    th = jnp.tanh(c0 * (x + c1 * x * x * x))
    return th, c0, c1


def _local_step(x, mem, tgt, wb, sp, late_weights, grads_ready, small_grads_ready):
    l = x.shape[0]
    w_a, w_g = wb["w_in"][:ZA_W], wb["w_in"][ZA_W:]

    a_re, a_im, bb_re, bb_im = _discretize(sp["ssm_lambda_re"], sp["ssm_lambda_im"], sp["ssm_log_dt"],
                                           sp["ssm_b_re"], sp["ssm_b_im"])
    a_pair = jnp.stack([a_re.reshape(1, SSM_S), a_im.reshape(1, SSM_S)])
    a_conj = jnp.stack([a_re.reshape(1, SSM_S), -a_im.reshape(1, SSM_S)])
    b_re_t, b_im_t = _bd_in(bb_re).astype(BF16), _bd_in(bb_im).astype(BF16)
    c_re_t = _bd_in(sp["ssm_c_re"].transpose(0, 2, 1)).astype(BF16)
    c_im_t = (-_bd_in(sp["ssm_c_im"].transpose(0, 2, 1))).astype(BF16)
    d_row = sp["ssm_d"].reshape(1, SSM_W)

    n1 = _rms_fwd(x, sp["norm1_g"], "rms1")
    za = _mm(n1, w_a, [BF16], tb=True, name="in_proj_a", tn=1664)
    zg = _mm(n1, w_g, [BF16], tb=True, name="in_proj_g")
    for gi in MIXER_GROUPS:
        wb = {**wb, **late_weights(gi, za)}
    u = za[:, :SSM_W]
    mq = za[:, ZA_W - MEM_W:]

    u_s = _scan_order(u)
    s_all = _ssm_scan(u_s, b_re_t, b_im_t, a_pair, reverse=False, name="ssm_scan_fwd")
    def gelu_epi(acc, ut, dr):
        y = acc + dr * ut.astype(F32)
        th, _, _ = _gelu_parts(y)
        return y, 0.5 * y * (1.0 + th)
    y0, y1 = [_time_order(t) for t in _mm(s_all, _tiled(c_re_t, c_im_t), [F32, BF16], tb=True, epi=gelu_epi,
                                          mn=[u_s], rows=[d_row], bd=SSM_BD, tm=2048, name="ssm_cs")]

    def glu_epi(acc, y1t, bg):
        t = acc + bg
        return t, y1t.astype(F32) * _sigmoid(t)
    t_glu, y2 = _mm(y1, wb["w_glu"], [F32, BF16], epi=glu_epi, mn=[y1], rows=[sp["b_glu"]], name="ssm_glu")
    br_ssm = _mm(y2, wb["w_ssm_br"], [BF16], tb=True, name="ssm_br")

    qkv_p, o_g, lse_g = [], [], []
    for g, d in enumerate(DILATIONS):
        nb = l // d // ATT_WIN
        cols = [za[:, SSM_W + (3 * j + g) * ATT_GW: SSM_W + (3 * j + g + 1) * ATT_GW] for j in range(3)]
        qp, kp, vp = [_to_perm(cc, d) for cc in cols]
        qkv_p.append((qp, kp, vp))
        og, lg = _attn_fwd(qp, kp, vp, nb, "attn_fwd%d" % g)
        o_g.append(_from_perm(og, d))
        lse_g.append(_from_perm(lg, d))

    def merge_fn(r, b):
        o0, o1, o2, l0, l1, l2 = r
        mx = jnp.maximum(jnp.maximum(l0, l1), l2)
        e0, e1, e2 = jnp.exp(l0 - mx), jnp.exp(l1 - mx), jnp.exp(l2 - mx)
        tot = e0 + e1 + e2
        return [(e0 * o0 + e1 * o1 + e2 * o2) / tot, mx + jnp.log(tot)], []
    o_att, lse_tot = _ew(merge_fn, o_g + lse_g, [], [(ATT_GW, F32), (ATT_GW, F32)], [], name="attn_merge", tm=2048)
    br_attn = _mm(o_att, wb["w_attn_br"], [BF16], tb=True, name="attn_br")

    mn = _rms_fwd(mem, sp["mem_norm_g"], "rms_mem")
    kv = _mm(mn, wb["w_mem_kv"], [BF16], name="mem_kv")
    mo = _mem_fwd(mq, kv, "mem_attn_fwd")
    br_mem = _mm(mo, wb["w_mem_br"], [BF16], tb=True, name="mem_br")

    merged, h1, n2 = _gated_out_proj(zg, [br_ssm, br_attn, br_mem], sp["b_gate"], wb["w_o"], x, sp["norm2_g"],
                                     "gated_o_proj")

    def up_epi(acc):
        ra = jnp.maximum(acc, 0.0)
        return ra * ra, ra
    wm = late_weights(MLP_GROUP, n2)
    f_act, r_act = _mm(n2, wm["w_up"], [BF16, BF16], tb=True, epi=up_epi, name="mlp_up")
    def down_epi(acc, ht, tv, gf):
        hv = acc + ht
        rs = lax.rsqrt(jnp.mean(hv * hv, axis=-1, keepdims=True) + RMS_EPS)
        err = hv * rs * gf - tv
        dh, dgf = _rms_bwd_tile(hv, err * (1.0 / D_MODEL), gf)
        return dh, dgf, _colsum(err * err) * (0.5 / D_MODEL)
    dh2, d_final_g, loss_cols = _mm(f_act, wm["w_down"], [F32], epi=down_epi, mn=[h1, tgt], rows=[sp["final_g"]],
                                    n_sums=2, tk=1024, name="mlp_down")
    loss = jnp.sum(loss_cols, axis=1, keepdims=True)

    gw, gs = {}, {"final_g": d_final_g}
    d_act = _mm(dh2, wm["w_down"], [BF16], tb=True, epi=lambda acc, ra: (acc * 2.0 * ra.astype(F32),), mn=[r_act],
                name="mlp_down_dx")
    dw_down = _mm(f_act, dh2, [F32], ta=True, name="mlp_down_dw")
    dw_up = _mm(d_act, n2, [F32], ta=True, name="mlp_up_dw")
    token = grads_ready(MLP_GROUP, {"w_up": dw_up, "w_down": dw_down})
    def up_dx_epi(acc, ht, dht, g2):
        dx, dg = _rms_bwd_tile(ht, acc, g2)
        return dx + dht, dg
    dh1, gs["norm2_g"] = _mm(d_act, wm["w_up"], [F32], epi=up_dx_epi, mn=[h1, dh2],
                             rows=[sp["norm2_g"] + token[:1, :1]], n_sums=1, tk=1024, name="mlp_up_dx")
    gw["w_o"] = _mm(merged, dh1, [F32], ta=True, name="o_proj_dw")

    def gate_bwd_epi(dm, *tiles):
        dbr, dz = [], []
        for zt, bt, bias in zip(tiles[0:3], tiles[3:6], tiles[6:9]):
            gt = _sigmoid(zt.astype(F32) + bias)
            dbr.append(dm * gt)
            dz.append(dm * bt.astype(F32) * gt * (1.0 - gt))
        return (*dbr, *dz, *[_colsum(t) for t in dz])
    gate_bias = [sp["b_gate"][:, i * D_MODEL:(i + 1) * D_MODEL] for i in range(3)]
    res = _mm(dh1, wb["w_o"], [BF16] * 6, tb=True, epi=gate_bwd_epi, mn=[(zg, 0), (zg, 1), (zg, 2), br_ssm, br_attn, br_mem],
              rows=gate_bias, n_sums=3, tm=512, name="o_proj_dx")
    (dbr_ssm, dbr_attn, dbr_mem), dzg = res[0:3], res[3:6]
    gs["b_gate"] = jnp.concatenate(res[6:9], axis=1)

    gw["w_ssm_br"] = _mm(dbr_ssm, y2, [F32], ta=True, name="ssm_br_dw")
    def glu_bwd_epi(dy, y1t, tt):
        sg = _sigmoid(tt)
        dt = dy * y1t.astype(F32) * sg * (1.0 - sg)
        return dt, dy * sg, _colsum(dt)
    dt_glu, dy1a, gs["b_glu"] = _mm(dbr_ssm, wb["w_ssm_br"], [BF16, F32], epi=glu_bwd_epi, mn=[y1, t_glu], n_sums=1,
                                    name="ssm_br_dx")
    gw["w_glu"] = _mm(y1, dt_glu, [F32], ta=True, name="ssm_glu_dw")

    def gelu_bwd_epi(acc, dy1t, y0t, ut):
        th, c0, c1 = _gelu_parts(y0t)
        dg = 0.5 * (1.0 + th) + 0.5 * y0t * (1.0 - th * th) * c0 * (1.0 + 3.0 * c1 * y0t * y0t)
        dy = (acc + dy1t) * dg
        return dy, _colsum(dy * ut.astype(F32))
    dy0, gs["ssm_d"] = _mm(dt_glu, wb["w_glu"], [F32], tb=True, epi=gelu_bwd_epi, mn=[dy1a, y0, u], n_sums=1,
                           name="ssm_glu_dx")
    dy0_s = _scan_order(dy0)
    lam, da, d_b, d_c = _ssm_scan(dy0_s, c_re_t, c_im_t, a_conj, reverse=True, s_fwd=s_all, u=u_s,
                                  name="ssm_scan_bwd")
    du = _time_order(_mm(lam, _tiled(b_re_t, b_im_t), [BF16], tb=True,
                         epi=lambda acc, dyt, dr: (acc + dyt * dr,), mn=[dy0_s], rows=[d_row], bd=SSM_BD, tm=2048, name="ssm_bu_dx"))
    gs["a_re"], gs["a_im"] = da[0], da[1]
    (dbr, dbi), (dcr, dci) = _untiled(d_b), _untiled(d_c)
    gs["bb_re"], gs["bb_im"] = _bd_diag(dbr).transpose(0, 2, 1), _bd_diag(dbi).transpose(0, 2, 1)
    gs["ssm_c_re"], gs["ssm_c_im"] = _bd_diag(dcr), -_bd_diag(dci)

    gw["w_attn_br"] = _mm(dbr_attn, o_att, [F32], ta=True, name="attn_br_dw")

    def do_epi(acc, ot):
        prod = acc * ot
        head = lax.broadcasted_iota(jnp.int32, prod.shape, 1) // ATT_E
        dd = jnp.zeros_like(prod)
        for h in range(ATT_HG):
            dd = jnp.where(head == h, jnp.sum(jnp.where(head == h, prod, 0.0), axis=1, keepdims=True), dd)
        return acc, dd
    do_att, dd_att = _mm(dbr_attn, wb["w_attn_br"], [BF16, F32], epi=do_epi, mn=[o_att], name="attn_br_dx")
    dq_l, dk_l, dv_l = [], [], []
    for g, d in enumerate(DILATIONS):
        nb = l // d // ATT_WIN
        qp, kp, vp = qkv_p[g]
        dq, dk, dv = _attn_bwd(qp, kp, vp, _to_perm(do_att, d), _to_perm(lse_tot, d), _to_perm(dd_att, d),
                               nb, "attn_bwd%d" % g)
        dq_l.append(_from_perm(dq, d))
        dk_l.append(_from_perm(dk, d))
        dv_l.append(_from_perm(dv, d))

    gw["w_mem_br"] = _mm(dbr_mem, mo, [F32], ta=True, name="mem_br_dw")
    dmo = _mm(dbr_mem, wb["w_mem_br"], [BF16], name="mem_br_dx")
    dmq, dkv = _mem_bwd(mq, kv, dmo, "mem_attn_bwd")
    gw["w_mem_kv"] = _mm(mn, dkv, [F32], ta=True, name="mem_kv_dw")
    dmn = _mm(dkv, wb["w_mem_kv"], [F32], tb=True, name="mem_kv_dx")
    token = sum(grads_ready(gi, gw) for gi in MIXER_GROUPS)
    gs["mem_norm_g"] = _rms_bwd(mem, dmn, None, sp["mem_norm_g"] + token[:1, :1], "rms_mem_bwd")[1]

    dza = jnp.concatenate([du] + dq_l + dk_l + dv_l + [dmq], axis=1)
    dn_a = _mm(dza, w_a, [F32], name="in_proj_a_dx", tk=1664)
    dw_a = _mm(dza, n1, [F32], ta=True, name="in_proj_a_dw", tm=1664)
    dw_g = [_mm(dzg[i], n1, [F32], ta=True, name="in_proj_g_dw%d" % i) for i in range(3)]
    gw["w_in"] = jnp.concatenate([dw_a] + dw_g, axis=0)
    token = grads_ready(0, gw) + small_grads_ready(gs)
    def in_dx_epi(acc, pt, xt, dht, g1):
        dx, dg = _rms_bwd_tile(xt, acc + pt, g1)
        return dx + dht, dg
    w_gs = [w_g[i * D_MODEL:(i + 1) * D_MODEL] for i in range(3)]
    grad_x, gs["norm1_g"] = _mm(dzg[0], w_gs[0], [F32], pair2=(dzg[1], w_gs[1], dzg[2], w_gs[2]), epi=in_dx_epi,
                                mn=[dn_a, x, dh1],
                                rows=[sp["norm1_g"] + token[:1, :1]], n_sums=1, tm=512, name="in_proj_g_dx")
    return loss, grad_x, gs


_SMALL_GRAD_ORDER = ("norm1_g", "mem_norm_g", "b_gate", "a_re", "a_im", "bb_re", "bb_im", "ssm_c_re", "ssm_c_im",
                     "ssm_d", "b_glu", "norm2_g", "final_g")


def kernel(x, mem, norm1_g, mem_norm_g, w_in, b_gate, ssm_lambda_re, ssm_lambda_im, ssm_log_dt, ssm_b_re, ssm_b_im, ssm_c_re, ssm_c_im, ssm_d, w_glu, b_glu, w_ssm_br, w_attn_br, w_mem_kv, w_mem_br, w_o, norm2_g, w_up, w_down, final_g, loss_target, m_norm1_g, m_mem_norm_g, m_w_in, m_b_gate, m_ssm_lambda_re, m_ssm_lambda_im, m_ssm_log_dt, m_ssm_b_re, m_ssm_b_im, m_ssm_c_re, m_ssm_c_im, m_ssm_d, m_w_glu, m_b_glu, m_w_ssm_br, m_w_attn_br, m_w_mem_kv, m_w_mem_br, m_w_o, m_norm2_g, m_w_up, m_w_down, m_final_g, v_norm1_g, v_mem_norm_g, v_w_in, v_b_gate, v_ssm_lambda_re, v_ssm_lambda_im, v_ssm_log_dt, v_ssm_b_re, v_ssm_b_im, v_ssm_c_re, v_ssm_c_im, v_ssm_d, v_w_glu, v_b_glu, v_w_ssm_br, v_w_attn_br, v_w_mem_kv, v_w_mem_br, v_w_o, v_norm2_g, v_w_up, v_w_down, v_final_g):
    args = dict(locals())
    w = {n: args[n] for n in ALL_W}
    m = {n: args["m_" + n] for n in ALL_W}
    v = {n: args["v_" + n] for n in ALL_W}
    my_c = lax.axis_index("c").astype(jnp.int32).reshape(1)
    my_chip = (2 * lax.axis_index("x") + lax.axis_index("y")).astype(jnp.int32).reshape(1)

    w_pack = [_pack_group(w, names) for names in GROUPS]
    my_index = (4 * lax.axis_index("x") + 2 * lax.axis_index("y") + lax.axis_index("c")).astype(jnp.int32)
    zero = jnp.zeros((), jnp.int32)
    w_all = _allgather(w_pack[0].astype(BF16), "allgather_weights0")
    wb = {n: _full_stored(part, n) for n, part in _split_group(w_all, GROUPS[0]).items()}
    gathers = {gi: _split_start(w_pack[gi].astype(BF16), N_DEV, _gather_plan, w_all, "weights_gather_start%d" % gi)
               for gi in range(1, len(GROUPS))}

    def gathered(started, after, name):
        sems, src, land, _ = started
        src, land = _split_wait(sems, src, land, _gather_wait_plan, after, name)
        return lax.dynamic_update_slice(land, src[None], (my_index, zero, zero))

    def late_weights(gi, after):
        full = gathered(gathers[gi], after, "weights_gather_wait%d" % gi)
        return {n: _full_stored(part, n) for n, part in _split_group(full, GROUPS[gi]).items()}

    pending = {}

    def grads_ready(gi, grads):
        g_pack = jnp.concatenate([_stacked_stored(grads[n], n) for n in GROUPS[gi]], axis=1)
        if gi == 0:
            t1 = _pair_exchange(g_pack, "grad_pair_exchange%d" % gi)
            p_sum, p_bf = _pair_sum(g_pack, t1, my_c, "grad_pair_sum%d" % gi, GROUP_TR[gi])
            started = _split_start(p_bf, 3, _chip_plan, p_sum, "grad_chip_exchange_start%d" % gi)
            pending[gi] = (p_sum, my_chip, started, _chip_plan)
        else:
            started = _split_start(g_pack.astype(BF16), N_DEV - 1, _owner_plan, g_pack, "grad_exchange_start%d" % gi)
            pending[gi] = (g_pack, my_index.reshape(1), started, _owner_plan)
        return started[3]

    early_small = [n for n in _SMALL_GRAD_ORDER if n != "norm1_g"]
    small_started = []

    def small_grads_ready(gs):
        started = _split_start(_pack([gs[n] for n in early_small]), N_DEV, _gather_plan, gs["mem_norm_g"],
                               "small_grads_gather_start")
        small_started.append((started, [gs[n].shape for n in early_small]))
        return started[3]

    sp = {
        "norm1_g": norm1_g + sum(started[3][:1, :1] for started in gathers.values()), "mem_norm_g": mem_norm_g, "b_gate": b_gate, "b_glu": b_glu, "norm2_g": norm2_g,
        "final_g": final_g.reshape(1, D_MODEL),
        "ssm_lambda_re": ssm_lambda_re[0], "ssm_lambda_im": ssm_lambda_im[0], "ssm_log_dt": ssm_log_dt[0],
        "ssm_b_re": ssm_b_re[0], "ssm_b_im": ssm_b_im[0], "ssm_c_re": ssm_c_re[0], "ssm_c_im": ssm_c_im[0],
        "ssm_d": ssm_d[0],
    }
    loss, grad_x, gs = _local_step(x[0], mem[0], loss_target[0], wb, sp, late_weights, grads_ready,
                                     small_grads_ready)
    loss = lax.psum(loss[0, 0], ("x", "y", "c"))
    n1_started = _split_start(_pack([gs["norm1_g"]]), N_DEV, _gather_plan, grad_x, "norm1_grad_gather_start")

    big_g = {}
    for gi, names in enumerate(GROUPS):
        own, own_index, (sems, src, land, _), plan = pending[gi]
        recv = _split_wait(sems, src, land, plan, grad_x, "grad_exchange_wait%d" % gi)[1]
        g_pack = _grad_sum(own, own_index, recv, "grad_sum%d" % gi, GROUP_TR[gi])
        for n, part in _split_group(g_pack, names).items():
            big_g[n] = _unstored(part, n)
    rows_of = lambda d, names: [d[n].reshape(d[n].shape[-2:]) for n in names]
    big_out = _adam_many(rows_of(big_g, BIG), rows_of(w, BIG), rows_of(m, BIG), rows_of(v, BIG), 8, "adam_big")
    big = [big_g] + [{n: a[None] for n, a in zip(BIG, outs)} for outs in big_out]

    (sg_started, sg_shapes), = small_started
    sg_all = jnp.concatenate([gathered(sg_started, big_out[0][0], "small_grads_gather_wait"),
                              gathered(n1_started, big_out[0][0], "norm1_grad_gather_wait")], axis=1)
    sg_sum = _sum8(sg_all, "sum_small_grads")
    n1_rows = n1_started[1].shape[0]
    sg = dict(zip(early_small, _unpack(sg_sum[:-n1_rows], sg_shapes)))
    sg["norm1_g"] = _unpack(sg_sum[-n1_rows:], [gs["norm1_g"].shape])[0]
    _, disc_vjp = jax.vjp(_discretize, sp["ssm_lambda_re"], sp["ssm_lambda_im"], sp["ssm_log_dt"],
                          sp["ssm_b_re"], sp["ssm_b_im"])
    d_lre, d_lim, d_ldt, d_bre, d_bim = disc_vjp((sg["a_re"].reshape(SSM_G, SSM_P), sg["a_im"].reshape(SSM_G, SSM_P),
                                                  sg["bb_re"], sg["bb_im"]))
    small_grad = {
        "norm1_g": sg["norm1_g"], "mem_norm_g": sg["mem_norm_g"], "b_gate": sg["b_gate"],
        "ssm_lambda_re": d_lre, "ssm_lambda_im": d_lim, "ssm_log_dt": d_ldt, "ssm_b_re": d_bre, "ssm_b_im": d_bim,
        "ssm_c_re": sg["ssm_c_re"], "ssm_c_im": sg["ssm_c_im"], "ssm_d": sg["ssm_d"], "b_glu": sg["b_glu"],
        "norm2_g": sg["norm2_g"], "final_g": sg["final_g"],
    }
    small_grad = {n: small_grad[n].reshape(w[n].shape) for n in SMALL}

    def squeezed(a):
        return a.reshape(a.shape[1:]) if a.ndim > 2 else a.reshape(1, -1)

    sq = lambda d: [squeezed(d[n]) for n in SMALL]
    small_out = _adam_many(sq(small_grad), sq(w), sq(m), sq(v), 1, "adam_small")
    small = [small_grad] + [{n: a.reshape(w[n].shape) for n, a in zip(SMALL, outs)} for outs in small_out]

    outs = [loss, grad_x[None]]
    for kind in range(4):
        for n in ALL_W:
            outs.append(big[kind][n] if n in BIG else small[kind][n])
    return tuple(outs)
```

```python
import math

import numpy as np
import jax
import jax.numpy as jnp
from jax import lax
from jax.experimental import pallas as pl
from jax.experimental.pallas import tpu as pltpu

F32 = jnp.float32
BF16 = jnp.bfloat16
_MXU = jnp.bfloat16

D_MODEL = 1024
SSM_G, SSM_H, SSM_P = 32, 16, 64
SSM_W = SSM_G * SSM_H
SSM_S = SSM_G * SSM_P
SSM_BD = 4
ATT_E = 64
ATT_HG = 4
ATT_GW = ATT_HG * ATT_E
ATT_WIN = 128
ATT_QB = 8
ATT_QB_FWD = 4
DILATIONS = (1, 4, 16)
MEM_H, MEM_E = 4, 128
MEM_W = MEM_H * MEM_E
ZA_W = SSM_W + 9 * ATT_GW + MEM_W
ZG_W = 3 * D_MODEL
IN_W = ZA_W + ZG_W
RMS_EPS = 1e-6
NEG_INF = -1e30

ADAM_LR, ADAM_B1, ADAM_B2, ADAM_EPS, ADAM_WD, ADAM_STEP = 0.001, 0.9, 0.999, 1e-08, 0.01, 10

N_DEV = 8
PACK_C = 512
_VMEM_LIMIT = 56 * 1024 * 1024
SUBLANES = 16
SCAN_SEG = 128
SCAN_CHAINS = 4
SCAN_UNROLL = 4
SCAN_W = 128

BIG = ("w_in", "w_glu", "w_ssm_br", "w_attn_br", "w_mem_kv", "w_mem_br", "w_o", "w_up", "w_down")
BIG_SHAPE = {
    "w_in": (D_MODEL, IN_W, 1), "w_glu": (SSM_W, SSM_W, 0), "w_ssm_br": (SSM_W, D_MODEL, 1),
    "w_attn_br": (ATT_GW, D_MODEL, 1), "w_mem_kv": (D_MODEL, 2 * MEM_W, 0), "w_mem_br": (MEM_W, D_MODEL, 1),
    "w_o": (D_MODEL, D_MODEL, 0), "w_up": (D_MODEL, 4 * D_MODEL, 1), "w_down": (4 * D_MODEL, D_MODEL, 0),
}
SMALL = ("norm1_g", "mem_norm_g", "b_gate", "ssm_lambda_re", "ssm_lambda_im", "ssm_log_dt", "ssm_b_re",
         "ssm_b_im", "ssm_c_re", "ssm_c_im", "ssm_d", "b_glu", "norm2_g", "final_g")
ALL_W = ("norm1_g", "mem_norm_g", "w_in", "b_gate", "ssm_lambda_re", "ssm_lambda_im", "ssm_log_dt", "ssm_b_re",
         "ssm_b_im", "ssm_c_re", "ssm_c_im", "ssm_d", "w_glu", "b_glu", "w_ssm_br", "w_attn_br", "w_mem_kv",
         "w_mem_br", "w_o", "norm2_g", "w_up", "w_down", "final_g")


def _params(sem):
    return pltpu.CompilerParams(dimension_semantics=sem, vmem_limit_bytes=_VMEM_LIMIT)


def _pick(n, cap):
    if n <= cap:
        return n
    t = (cap // 128) * 128
    while n % t:
        t -= 128
    return t


def _mm(a, b, outs, *, name, ta=False, tb=False, epi=None, mn=(), rows=(), pair2=None, bd=0, n_sums=0,
        square_a=False, tm=1024, tn=1024, tk=2048):
    ab = [a, b] + (list(pair2) if pair2 is not None else [])
    a_shape, b_shape = ab[0].shape, ab[1].shape
    m = a_shape[1] if ta else a_shape[0]
    k = a_shape[0] if ta else a_shape[1]
    n = b_shape[0] if tb else b_shape[1]
    assert k == (b_shape[1] if tb else b_shape[0]), (name, a_shape, b_shape)
    out_n = n
    if bd and ta:
        assert not tb
        tm, tn, tk = m // bd, n // bd, _pick(k, tk)
        grid, out_n = (bd, 1, k // tk), tn
        a_blk = ((tk, tm), lambda i, j, kk: (kk, i))
        b_blk = ((tk, tn), lambda i, j, kk: (kk, i))
        mn_spec = pl.BlockSpec((tm, tn), lambda i, j, kk: (i, 0))
    elif bd:
        tm, tn, tk = _pick(m, tm), n // bd, k // bd
        grid = (m // tm, bd, 1)
        a_blk = ((tm, tk), lambda i, j, kk: (i, j))
        b_blk = ((tn, tk) if tb else (tk, tn), lambda i, j, kk: (j, j))
        mn_spec = pl.BlockSpec((tm, tn), lambda i, j, kk: (i, j))
    else:
        tm, tn, tk = _pick(m, tm), _pick(n, tn), _pick(k, tk)
        grid = (m // tm, n // tn, k // tk)
        a_blk = ((tk, tm), lambda i, j, kk: (kk, i)) if ta else ((tm, tk), lambda i, j, kk: (i, kk))
        b_blk = ((tn, tk), lambda i, j, kk: (j, kk)) if tb else ((tk, tn), lambda i, j, kk: (kk, j))
        mn_spec = pl.BlockSpec((tm, tn), lambda i, j, kk: (i, j))

    ab_specs = [pl.BlockSpec(*(a_blk if q % 2 == 0 else b_blk)) for q in range(len(ab))]
    mn_arrays = [e[0] if isinstance(e, tuple) else e for e in mn]
    mn_specs = [pl.BlockSpec((tm, tn), lambda i, j, kk, c=e[1]: (i, c)) if isinstance(e, tuple) else mn_spec
                for e in mn]
    nk = grid[2]
    row_spec = pl.BlockSpec((1, tn), lambda i, j, kk: (0, j))
    n_ex, n_out = len(mn) + len(rows), len(outs)
    assert n_sums == 0 or (grid[1] == 1 and not bd)
    dims = (((0 if ta else 1,), (1 if tb else 0,)), ((), ()))

    def body(*refs):
        ab_refs, rest = refs[:len(ab)], refs[len(ab):]
        ex, o_refs = rest[:n_ex], rest[n_ex:n_ex + n_out]
        s_refs = rest[n_ex + n_out:n_ex + n_out + n_sums]
        first_row_tile = pl.program_id(0) == 0
        kk = pl.program_id(2)

        pairs = list(zip(ab_refs[0::2], ab_refs[1::2]))

        def product(pair):
            av = pair[0][...]
            if square_a:
                av = av * av
            return lax.dot_general(av.astype(_MXU), pair[1][...].astype(_MXU), dims, preferred_element_type=F32)

        def finish(total):
            vals = (total,) if epi is None else epi(total, *[r[...] for r in ex])
            for r, v in zip(o_refs, vals):
                r[...] = v.astype(r.dtype)
            for r, v in zip(s_refs, vals[n_out:]):
                r[...] = jnp.where(first_row_tile, v, r[...] + v)

        if nk == 1:
            total = product(pairs[0])
            for pair in pairs[1:]:
                total = total + product(pair)
            finish(total)
        else:
            acc = rest[-1]

            @pl.when(kk == 0)
            def _():
                acc[...] = jnp.zeros_like(acc)

            for pair in pairs:
                acc[...] += product(pair)

            @pl.when(kk == nk - 1)
            def _():
                finish(acc[...])

    res = pl.pallas_call(
        body, grid=grid,
        in_specs=ab_specs + mn_specs + [row_spec] * len(rows),
        out_specs=[mn_spec] * n_out + [row_spec] * n_sums,
        out_shape=[jax.ShapeDtypeStruct((m, out_n), dt) for dt in outs]
        + [jax.ShapeDtypeStruct((1, out_n), F32)] * n_sums,
        scratch_shapes=[pltpu.VMEM((tm, tn), F32)] if nk > 1 else [],
        compiler_params=_params(("arbitrary" if n_sums else "parallel", "parallel", "arbitrary")), name=name,
    )(*ab, *mn_arrays, *rows)
    return res[0] if n_out + n_sums == 1 else res


def _ew(fn, rows, bcs, out_rows, out_accs, *, name, tm=256):
    r = rows[0].shape[0]
    tm = min(tm, r)
    assert r % tm == 0
    nr, nb, no, na = len(rows), len(bcs), len(out_rows), len(out_accs)

    def body(*refs):
        i = pl.program_id(0)
        r_in, b_in = refs[:nr], refs[nr:nr + nb]
        o_r, o_a = refs[nr + nb:nr + nb + no], refs[nr + nb + no:]
        outs, accs = fn([x[...] for x in r_in], [x[...] for x in b_in])
        for ref, v in zip(o_r, outs):
            ref[...] = v.astype(ref.dtype)
        if na:
            @pl.when(i == 0)
            def _():
                for ref in o_a:
                    ref[...] = jnp.zeros_like(ref)

            for ref, v in zip(o_a, accs):
                ref[...] += v

    res = pl.pallas_call(
        body, grid=(r // tm,),
        in_specs=[pl.BlockSpec((tm, x.shape[1]), lambda i: (i, 0)) for x in rows]
        + [pl.BlockSpec((1, x.shape[1]), lambda i: (0, 0)) for x in bcs],
        out_specs=[pl.BlockSpec((tm, c), lambda i: (i, 0)) for c, _ in out_rows]
        + [pl.BlockSpec((1, c), lambda i: (0, 0)) for c in out_accs],
        out_shape=[jax.ShapeDtypeStruct((r, c), dt) for c, dt in out_rows]
        + [jax.ShapeDtypeStruct((1, c), F32) for c in out_accs],
        compiler_params=_params(("arbitrary",)), name=name,
    )(*rows, *bcs)
    return res


def _colsum(x):
    return jnp.sum(x, axis=0, keepdims=True)


def _sigmoid(x):
    return 0.5 * jnp.tanh(0.5 * x) + 0.5


def _rms_bwd_tile(xv, dv, g):
    rs = lax.rsqrt(jnp.mean(xv * xv, axis=-1, keepdims=True) + RMS_EPS)
    gd = dv * g
    dx = rs * gd - xv * (rs * rs * rs) * jnp.mean(gd * xv, axis=-1, keepdims=True)
    return dx, _colsum(dv * xv * rs)


def _rms_fwd(x, g, name):
    def fn(r, b):
        xv = r[0]
        rs = lax.rsqrt(jnp.mean(xv * xv, axis=-1, keepdims=True) + RMS_EPS)
        return [xv * rs * b[0]], []
    return _ew(fn, [x], [g], [(x.shape[1], BF16)], [], name=name, tm=1024)[0]


def _rms_bwd(x, dn, res, g, name):
    def fn(r, b):
        dx, dg = _rms_bwd_tile(r[0], r[1], b[0])
        if res is not None:
            dx = dx + r[2]
        return [dx], [dg]
    rows = [x, dn] + ([res] if res is not None else [])
    return _ew(fn, rows, [g], [(x.shape[1], F32)], [x.shape[1]], name=name)


def _scan_order(x):
    l, c = x.shape
    return x.reshape(l // (SUBLANES * SCAN_SEG), SUBLANES, SCAN_SEG, c).transpose(0, 2, 1, 3).reshape(l, c)


def _time_order(x):
    l, c = x.shape
    return x.reshape(l // (SUBLANES * SCAN_SEG), SCAN_SEG, SUBLANES, c).transpose(0, 2, 1, 3).reshape(l, c)


def _ssm_scan(x, w_re, w_im, a_pair, *, reverse, s_fwd=None, u=None, name):
    l = x.shape[0]
    seg, w = SCAN_SEG, SCAN_W
    bd_w = SSM_W // SSM_BD
    tiles_per_bd = SSM_S // SSM_BD // w
    nch = min(SCAN_CHAINS, l // (SUBLANES * seg))
    chain_rows = SUBLANES * seg
    tb = nch * chain_rows
    nt = l // tb
    with_da = s_fwd is not None
    assert reverse or not with_da

    def tt(t):
        return nt - 1 - t if reverse else t

    def body(*refs):
        if with_da:
            (x_ref, wr_ref, wi_ref, a_ref, sf_ref, sp_ref, u_ref, s_ref, da_ref, dw_ref, dx_ref,
             p_ref, c_ref, b_scr, l_scr) = refs
        else:
            x_ref, wr_ref, wi_ref, a_ref, s_ref, p_ref, c_ref, b_scr, l_scr = refs
        t_blk = pl.program_id(1)
        ar, ai = a_ref[0], a_ref[1]

        @pl.when(t_blk == 0)
        def _():
            def pstep(i, carry):
                pr, pi = carry
                p_ref[0, pl.ds(i, 1), :] = pr
                p_ref[1, pl.ds(i, 1), :] = pi
                return pr * ar - pi * ai, pr * ai + pi * ar

            lax.fori_loop(0, seg, pstep, (ar, ai))
            c_ref[...] = jnp.zeros_like(c_ref)
            if with_da:
                da_ref[...] = jnp.zeros_like(da_ref)
                dw_ref[...] = jnp.zeros_like(dw_ref)
                dx_ref[...] = jnp.zeros_like(dx_ref)

        xb = x_ref[...].astype(_MXU)
        b_scr[:, :w] = jnp.dot(xb, wr_ref[...], preferred_element_type=F32)
        b_scr[:, w:] = jnp.dot(xb, wi_ref[...], preferred_element_type=F32)
        arb, aib = jnp.broadcast_to(ar, (SUBLANES, w)), jnp.broadcast_to(ai, (SUBLANES, w))
        zero = jnp.zeros((SUBLANES, w), F32)

        def tile(g, step):
            return pl.ds(pl.multiple_of(g * chain_rows + step * SUBLANES, SUBLANES), SUBLANES)

        def rows(g, i):
            return tile(g, seg - 1 - i if reverse else i)

        def local_step(i, carry):
            out = []
            for g in range(nch):
                sr, si = carry[2 * g], carry[2 * g + 1]
                idx = rows(g, i)
                sr, si = arb * sr - aib * si + b_scr[idx, :w], arb * si + aib * sr + b_scr[idx, w:]
                l_scr[idx, :w] = sr
                l_scr[idx, w:] = si
                out += [sr, si]
            return tuple(out)

        def unrolled(step_fn, first):
            def trip(q, carry):
                for r in range(SCAN_UNROLL):
                    carry = step_fn(first + q * SCAN_UNROLL + r, carry)
                return carry
            return trip

        ends = lax.fori_loop(0, seg // SCAN_UNROLL, unrolled(local_step, 0), (zero,) * (2 * nch))

        a_seg_r, a_seg_i = p_ref[0, seg - 1:seg, :], p_ref[1, seg - 1:seg, :]
        cr, ci = c_ref[0], c_ref[1]
        sub = lax.broadcasted_iota(jnp.int32, (SUBLANES, w), 0)
        ins = [[zero, zero] for _ in range(nch)]
        order = [(g, k) for g in range(nch) for k in range(SUBLANES)]
        for g, k in (order[::-1] if reverse else order):
            ins[g] = [jnp.where(sub == k, cr, ins[g][0]), jnp.where(sub == k, ci, ins[g][1])]
            er, ei = ends[2 * g][k:k + 1], ends[2 * g + 1][k:k + 1]
            cr, ci = er + a_seg_r * cr - a_seg_i * ci, ei + a_seg_r * ci + a_seg_i * cr
        c_ref[0] = cr
        c_ref[1] = ci

        def fix(g, i):
            idx = rows(g, i)
            pr, pi = p_ref[0, pl.ds(i, 1), :], p_ref[1, pl.ds(i, 1), :]
            sr = l_scr[idx, :w] + pr * ins[g][0] - pi * ins[g][1]
            si = l_scr[idx, w:] + pr * ins[g][1] + pi * ins[g][0]
            s_ref[idx, :w] = sr.astype(s_ref.dtype)
            s_ref[idx, w:] = si.astype(s_ref.dtype)
            return sr, si

        if not with_da:
            def fix_step(i, carry):
                for g in range(nch):
                    fix(g, i)
                return carry

            lax.fori_loop(0, seg // SCAN_UNROLL, unrolled(fix_step, 0), 0)
        else:
            def adj_step(i, acc):
                acc_r, acc_i = acc
                for g in range(nch):
                    lr, li = fix(g, i)
                    prev = tile(g, seg - 2 - i)
                    fr, fi = sf_ref[prev, :w].astype(F32), sf_ref[prev, w:].astype(F32)
                    acc_r, acc_i = acc_r + lr * fr + li * fi, acc_i + li * fr - lr * fi
                return acc_r, acc_i

            acc = lax.fori_loop(0, seg // SCAN_UNROLL - 1, unrolled(adj_step, 0), (zero, zero))
            for i in range(seg - SCAN_UNROLL, seg - 1):
                acc = adj_step(i, acc)
            acc_r, acc_i = acc
            first_block = tt(t_blk) == 0
            for g in range(nch):
                lr, li = fix(g, seg - 1)
                seg_ends = tile(g, seg - 1)
                if g == 0:
                    pvr = jnp.where(first_block, 0.0, sp_ref[SUBLANES - 1:SUBLANES, :w].astype(F32))
                    pvi = jnp.where(first_block, 0.0, sp_ref[SUBLANES - 1:SUBLANES, w:].astype(F32))
                else:
                    pvr = sf_ref[g * chain_rows - 1:g * chain_rows, :w].astype(F32)
                    pvi = sf_ref[g * chain_rows - 1:g * chain_rows, w:].astype(F32)
                fr = jnp.where(sub == 0, pvr, pltpu.roll(sf_ref[seg_ends, :w].astype(F32), 1, 0))
                fi = jnp.where(sub == 0, pvi, pltpu.roll(sf_ref[seg_ends, w:].astype(F32), 1, 0))
                acc_r = acc_r + lr * fr + li * fi
                acc_i = acc_i + li * fr - lr * fi
            da_ref[0] += jnp.sum(acc_r, axis=0, keepdims=True)
            da_ref[1] += jnp.sum(acc_i, axis=0, keepdims=True)
            dw_ref[...] += _tn_dot(u_ref[...], s_ref[...])
            dx_ref[...] += _tn_dot(xb, sf_ref[...])

    x_spec = pl.BlockSpec((tb, bd_w), lambda j, t: (tt(t), j // tiles_per_bd))
    w_spec = pl.BlockSpec((bd_w, w), lambda j, t: (j // tiles_per_bd, j))
    d_spec = pl.BlockSpec((bd_w, 2 * w), lambda j, t: (j // tiles_per_bd, j % tiles_per_bd))
    a_spec = pl.BlockSpec((2, 1, w), lambda j, t: (0, 0, j))
    s_spec = pl.BlockSpec((tb, 2 * w), lambda j, t: (tt(t), j))
    in_specs, args = [x_spec, w_spec, w_spec, a_spec], [x, w_re, w_im, a_pair]
    out_specs, out_shape = [s_spec], [jax.ShapeDtypeStruct((l, 2 * SSM_S), BF16)]
    scratch = [pltpu.VMEM((2, seg, w), F32), pltpu.VMEM((2, 1, w), F32)] + [pltpu.VMEM((tb, 2 * w), F32)] * 2
    if with_da:
        in_specs += [s_spec, pl.BlockSpec((SUBLANES, 2 * w),
                                          lambda j, t: (jnp.maximum(tt(t) * (tb // SUBLANES) - 1, 0), j)),
                     x_spec]
        args += [s_fwd, s_fwd, u]
        out_specs += [a_spec, d_spec, d_spec]
        out_shape += ([jax.ShapeDtypeStruct((2, 1, SSM_S), F32)]
                      + [jax.ShapeDtypeStruct((SSM_W, 2 * SSM_S // SSM_BD), F32)] * 2)
    res = pl.pallas_call(
        body, grid=(SSM_S // w, nt), in_specs=in_specs, out_specs=out_specs, out_shape=out_shape,
        scratch_shapes=scratch, compiler_params=_params(("parallel", "arbitrary")), name=name,
    )(*args)
    return res if with_da else res[0]


def _nt_dot(x, y):
    return lax.dot_general(x.astype(_MXU), y.astype(_MXU), (((1,), (1,)), ((), ())), preferred_element_type=F32)


def _tn_dot(x, y):
    return lax.dot_general(x.astype(_MXU), y.astype(_MXU), (((0,), (0,)), ((), ())), preferred_element_type=F32)


def _nn_dot(x, y):
    return jnp.dot(x.astype(_MXU), y.astype(_MXU), preferred_element_type=F32)


def _attn_mask2(gb, nb):
    qi = lax.broadcasted_iota(jnp.int32, (ATT_WIN, 2 * ATT_WIN), 0)
    c = lax.broadcasted_iota(jnp.int32, (ATT_WIN, 2 * ATT_WIN), 1)
    has_prev = (gb % nb) != 0
    prev_ok = jnp.logical_and(jnp.logical_and(c < ATT_WIN, c >= qi), has_prev)
    own_ok = jnp.logical_and(c >= ATT_WIN, c - ATT_WIN <= qi)
    return jnp.logical_or(prev_ok, own_ok)


def _attn_specs(qb):
    cur = pl.BlockSpec((qb * ATT_WIN, ATT_GW), lambda i: (i, 0))
    prev = pl.BlockSpec((ATT_WIN, ATT_GW), lambda i: (jnp.maximum(qb * i - 1, 0), 0))
    return cur, prev


def _attn_fwd(q, k, v, nb, name):
    l = q.shape[0]
    scale = ATT_E ** -0.5
    w = ATT_WIN

    qb = ATT_QB_FWD

    def body(q_ref, kc_ref, kp_ref, vc_ref, vp_ref, o_ref, lse_ref):
        i = pl.program_id(0)
        masks = [_attn_mask2(qb * i + b, nb) for b in range(qb)]
        for h in range(ATT_HG):
            sl = slice(h * ATT_E, (h + 1) * ATT_E)
            k_ext = jnp.concatenate([kp_ref[:, sl], kc_ref[:, sl]], axis=0)
            v_ext = jnp.concatenate([vp_ref[:, sl], vc_ref[:, sl]], axis=0)
            for b in range(qb):
                r, kr = slice(b * w, (b + 1) * w), slice(b * w, (b + 2) * w)
                s = jnp.where(masks[b], _nt_dot(q_ref[r, sl], k_ext[kr]) * scale, NEG_INF)
                mx = jnp.max(s, axis=-1, keepdims=True)
                p = jnp.exp(s - mx)
                den = jnp.sum(p, axis=-1, keepdims=True)
                o_ref[r, sl] = _nn_dot(p, v_ext[kr]) * (1.0 / den)
                lse_ref[r, sl] = jnp.broadcast_to(mx + jnp.log(den), (w, ATT_E))

    cur, prev = _attn_specs(qb)
    return pl.pallas_call(
        body, grid=(l // (qb * w),), in_specs=[cur, cur, prev, cur, prev], out_specs=[cur, cur],
        out_shape=[jax.ShapeDtypeStruct((l, ATT_GW), F32)] * 2,
        compiler_params=_params(("parallel",)), name=name,
    )(q, k, k, v, v)


def _attn_bwd(q, k, v, do, lse, dd, nb, name):
    l = q.shape[0]
    scale = ATT_E ** -0.5
    w = ATT_WIN
    nblk = l // w

    def body(q_ref, kc_ref, kp_ref, vc_ref, vp_ref, do_ref, lse_ref, dd_ref, qn_ref, don_ref, lsen_ref, ddn_ref,
             dq_ref, dk_ref, dv_ref, dk_acc, dv_acc):
        i = pl.program_id(0)
        masks = [_attn_mask2(ATT_QB * i + b, nb) for b in range(ATT_QB)]
        nxt = ATT_QB * (i + 1)
        nxt_attends = jnp.logical_and(nxt < nblk, (nxt % nb) != 0)
        qi = lax.broadcasted_iota(jnp.int32, (w, w), 0)
        kj = lax.broadcasted_iota(jnp.int32, (w, w), 1)
        mask_n = jnp.logical_and(kj >= qi, nxt_attends)
        dk_acc[...] = jnp.zeros_like(dk_acc)
        dv_acc[...] = jnp.zeros_like(dv_acc)
        for h in range(ATT_HG):
            sl, col = slice(h * ATT_E, (h + 1) * ATT_E), slice(h * ATT_E, h * ATT_E + 1)
            k_ext = jnp.concatenate([kp_ref[:, sl], kc_ref[:, sl]], axis=0)
            v_ext = jnp.concatenate([vp_ref[:, sl], vc_ref[:, sl]], axis=0)
            for b in range(ATT_QB):
                r, kr = slice(b * w, (b + 1) * w), slice(b * w, (b + 2) * w)
                qh, doh, k2, v2 = q_ref[r, sl], do_ref[r, sl], k_ext[kr], v_ext[kr]
                p = jnp.where(masks[b], jnp.exp(_nt_dot(qh, k2) * scale - lse_ref[r, col]), 0.0)
                ds = p * (_nt_dot(doh, v2) - dd_ref[r, col]) * scale
                dq_ref[r, sl] = _nn_dot(ds, k2).astype(dq_ref.dtype)
                dk2, dv2 = _tn_dot(ds, qh), _tn_dot(p, doh)
                dk_acc[r, sl] += dk2[w:]
                dv_acc[r, sl] += dv2[w:]
                if b > 0:
                    rp = slice((b - 1) * w, b * w)
                    dk_acc[rp, sl] += dk2[:w]
                    dv_acc[rp, sl] += dv2[:w]
            last = slice((ATT_QB - 1) * w, ATT_QB * w)
            kl, vl, qn, don = kc_ref[last, sl], vc_ref[last, sl], qn_ref[:, sl], don_ref[:, sl]
            pn = jnp.where(mask_n, jnp.exp(_nt_dot(qn, kl) * scale - lsen_ref[:, col]), 0.0)
            dsn = pn * (_nt_dot(don, vl) - ddn_ref[:, col]) * scale
            dk_acc[last, sl] += _tn_dot(dsn, qn)
            dv_acc[last, sl] += _tn_dot(pn, don)
        dk_ref[...] = dk_acc[...].astype(dk_ref.dtype)
        dv_ref[...] = dv_acc[...].astype(dv_ref.dtype)

    cur, prev = _attn_specs(ATT_QB)
    nxt_spec = pl.BlockSpec((w, ATT_GW), lambda i: (jnp.minimum(ATT_QB * (i + 1), nblk - 1), 0))
    return pl.pallas_call(
        body, grid=(l // (ATT_QB * w),),
        in_specs=[cur, cur, prev, cur, prev, cur, cur, cur, nxt_spec, nxt_spec, nxt_spec, nxt_spec],
        out_specs=[cur] * 3, out_shape=[jax.ShapeDtypeStruct((l, ATT_GW), BF16)] * 3,
        scratch_shapes=[pltpu.VMEM((ATT_QB * w, ATT_GW), F32)] * 2,
        compiler_params=_params(("parallel",)), name=name,
    )(q, k, k, v, v, do, lse, dd, q, do, lse, dd)


def _to_perm(a, d):
    if d == 1:
        return a
    l, c = a.shape
    return a.reshape(l // d, d, c).transpose(1, 0, 2).reshape(l, c)


def _from_perm(a, d):
    if d == 1:
        return a
    l, c = a.shape
    return a.reshape(d, l // d, c).transpose(1, 0, 2).reshape(l, c)


def _mem_probs(qh, kh):
    s = _nt_dot(qh, kh) * (MEM_E ** -0.5)
    e = jnp.exp(s - jnp.max(s, axis=-1, keepdims=True))
    return e * (1.0 / jnp.sum(e, axis=-1, keepdims=True))


def _mem_fwd(mq, kv, name, tm=1024):
    l, nm = mq.shape[0], kv.shape[0]

    def body(q_ref, kv_ref, o_ref):
        for h in range(MEM_H):
            sl = slice(h * MEM_E, (h + 1) * MEM_E)
            p = _mem_probs(q_ref[:, sl], kv_ref[:, sl])
            o_ref[:, sl] = _nn_dot(p, kv_ref[:, MEM_W + h * MEM_E:MEM_W + (h + 1) * MEM_E]).astype(o_ref.dtype)

    return pl.pallas_call(
        body, grid=(l // tm,),
        in_specs=[pl.BlockSpec((tm, MEM_W), lambda i: (i, 0)), pl.BlockSpec((nm, 2 * MEM_W), lambda i: (0, 0))],
        out_specs=pl.BlockSpec((tm, MEM_W), lambda i: (i, 0)),
        out_shape=jax.ShapeDtypeStruct((l, MEM_W), BF16),
        compiler_params=_params(("parallel",)), name=name,
    )(mq, kv)


def _mem_bwd(mq, kv, dmo, name, tm=1024):
    l, nm = mq.shape[0], kv.shape[0]
    scale = MEM_E ** -0.5

    def body(q_ref, kv_ref, do_ref, dq_ref, dkv_ref):
        @pl.when(pl.program_id(0) == 0)
        def _():
            dkv_ref[...] = jnp.zeros_like(dkv_ref)

        for h in range(MEM_H):
            sl = slice(h * MEM_E, (h + 1) * MEM_E)
            vsl = slice(MEM_W + h * MEM_E, MEM_W + (h + 1) * MEM_E)
            qh, kh, vh, doh = q_ref[:, sl], kv_ref[:, sl], kv_ref[:, vsl], do_ref[:, sl]
            p = _mem_probs(qh, kh)
            dp = _nt_dot(doh, vh)
            ds = p * (dp - jnp.sum(dp * p, axis=-1, keepdims=True)) * scale
            dq_ref[:, sl] = _nn_dot(ds, kh).astype(dq_ref.dtype)
            dkv_ref[:, sl] += _tn_dot(ds, qh)
            dkv_ref[:, vsl] += _tn_dot(p, doh)

    row = pl.BlockSpec((tm, MEM_W), lambda i: (i, 0))
    full = pl.BlockSpec((nm, 2 * MEM_W), lambda i: (0, 0))
    return pl.pallas_call(
        body, grid=(l // tm,), in_specs=[row, full, row], out_specs=[row, full],
        out_shape=[jax.ShapeDtypeStruct((l, MEM_W), BF16), jax.ShapeDtypeStruct((nm, 2 * MEM_W), F32)],
        compiler_params=_params(("arbitrary",)), name=name,
    )(mq, kv, dmo)


def _gated_out_proj(zg, branches, b_gate, w_o, x, g2, name, tm=512):
    l, d = x.shape
    nbr = len(branches)

    def body(zg_ref, *rest):
        br_refs, (bg_ref, w_ref, x_ref, g2_ref, m_ref, h_ref, n_ref) = rest[:nbr], rest[nbr:]
        merged = jnp.zeros((tm, d), F32)
        for i, br_ref in enumerate(br_refs):
            cols = slice(i * d, (i + 1) * d)
            merged += _sigmoid(zg_ref[:, cols].astype(F32) + bg_ref[:, cols]) * br_ref[...].astype(F32)
        mb = merged.astype(BF16)
        m_ref[...] = mb
        hv = jnp.dot(mb.astype(_MXU), w_ref[...].astype(_MXU), preferred_element_type=F32) + x_ref[...]
        h_ref[...] = hv
        rs = lax.rsqrt(jnp.mean(hv * hv, axis=-1, keepdims=True) + RMS_EPS)
        n_ref[...] = (hv * rs * g2_ref[...]).astype(n_ref.dtype)

    row = lambda c: pl.BlockSpec((tm, c), lambda i: (i, 0))
    full = lambda a: pl.BlockSpec(a.shape, lambda i: (0, 0))
    return pl.pallas_call(
        body, grid=(l // tm,),
        in_specs=[row(nbr * d)] + [row(d)] * nbr + [full(b_gate), full(w_o), row(d), full(g2)],
        out_specs=[row(d)] * 3,
        out_shape=[jax.ShapeDtypeStruct((l, d), BF16), jax.ShapeDtypeStruct((l, d), F32),
                   jax.ShapeDtypeStruct((l, d), BF16)],
        compiler_params=_params(("parallel",)), name=name,
    )(zg, *branches, b_gate, w_o, x, g2)


def _discretize(lam_re, lam_im, log_dt, b_re, b_im):
    dt = jnp.exp(log_dt)[:, None]
    mag = jnp.exp(lam_re * dt)
    a_re, a_im = mag * jnp.cos(lam_im * dt), mag * jnp.sin(lam_im * dt)
    nr, ni = a_re - 1.0, a_im
    den = lam_re * lam_re + lam_im * lam_im
    coef_re = (nr * lam_re + ni * lam_im) / den
    coef_im = (ni * lam_re - nr * lam_im) / den
    bb_re = coef_re[..., None] * b_re - coef_im[..., None] * b_im
    bb_im = coef_re[..., None] * b_im + coef_im[..., None] * b_re
    return a_re, a_im, bb_re, bb_im


def _tiled(re, im):
    r = re.shape[0]
    both = jnp.concatenate([re.reshape(r, -1, SCAN_W), im.reshape(r, -1, SCAN_W)], axis=2)
    return both.reshape(r, 2 * re.shape[1])


def _untiled(x):
    r = x.shape[0]
    t = x.reshape(r, -1, 2 * SCAN_W)
    return t[:, :, :SCAN_W].reshape(r, -1), t[:, :, SCAN_W:].reshape(r, -1)


def _bd_in(bb):
    return jnp.einsum("gph,gk->ghkp", bb, jnp.eye(SSM_G, dtype=bb.dtype)).reshape(SSM_W, SSM_S)


def _bd_diag(x):
    gb = SSM_G // SSM_BD
    t = x.reshape(SSM_BD, gb, SSM_H, gb, SSM_P)
    return jnp.einsum("bghgp->bghp", t).reshape(SSM_G, SSM_H, SSM_P)


_ANY = pl.BlockSpec(memory_space=pl.ANY)
_MESH = pl.DeviceIdType.MESH


def _allgather(x, name):
    def body(x_ref, out_ref, send_sems, recv_sems, local_sem):
        mx, my, mc = lax.axis_index("x"), lax.axis_index("y"), lax.axis_index("c")
        me, sibling = (mx, my, mc), (mx, my, 1 - mc)
        chips = [(1 - mx, my), (mx, 1 - my), (1 - mx, 1 - my)]

        def blk(px, py, pc):
            return out_ref.at[4 * px + 2 * py + pc]

        def copy(k, block, to, src=None):
            return pltpu.make_async_remote_copy(
                src_ref=blk(*block) if src is None else src, dst_ref=blk(*block),
                send_sem=send_sems.at[k], recv_sem=recv_sems.at[k], device_id=to, device_id_type=_MESH)

        mine = pltpu.make_async_copy(x_ref, blk(*me), local_sem)
        mine.start()
        first = [copy(0, me, sibling, src=x_ref)]
        first += [copy(1 + j, me, (*chip, mc), src=x_ref) for j, chip in enumerate(chips)]
        for cp in first:
            cp.start()
        passed = [copy(4 + j, (*chip, mc), sibling) for j, chip in enumerate(chips)]
        for j, chip in enumerate(chips):
            copy(1 + j, (*chip, mc), me).wait_recv()
            passed[j].start()
        copy(0, sibling, me).wait_recv()
        for j, chip in enumerate(chips):
            copy(4 + j, (*chip, 1 - mc), me).wait_recv()
        for cp in first + passed:
            cp.wait_send()
        mine.wait()

    return pl.pallas_call(
        body, out_shape=jax.ShapeDtypeStruct((N_DEV,) + x.shape, x.dtype), in_specs=[_ANY], out_specs=_ANY,
        scratch_shapes=[pltpu.SemaphoreType.DMA((7,)), pltpu.SemaphoreType.DMA((7,)), pltpu.SemaphoreType.DMA],
        name=name,
    )(x)


def _pair_exchange(g, name):
    def body(g_ref, out_ref, send_sems, recv_sems):
        mx, my, mc = lax.axis_index("x"), lax.axis_index("y"), lax.axis_index("c")
        copies = [pltpu.make_async_remote_copy(
            src_ref=g_ref.at[2 * k + (1 - mc)], dst_ref=out_ref.at[k], send_sem=send_sems.at[k],
            recv_sem=recv_sems.at[k], device_id=(mx, my, 1 - mc), device_id_type=_MESH) for k in range(4)]
        for cp in copies:
            cp.start()
        for cp in copies:
            cp.wait()

    return pl.pallas_call(
        body, out_shape=jax.ShapeDtypeStruct((4,) + g.shape[1:], g.dtype), in_specs=[_ANY], out_specs=_ANY,
        scratch_shapes=[pltpu.SemaphoreType.DMA((4,)), pltpu.SemaphoreType.DMA((4,))], name=name,
    )(g)


_HBM = pl.BlockSpec(memory_space=pltpu.HBM)
_SEM = pl.BlockSpec(memory_space=pltpu.SEMAPHORE)
_EFFECT = pltpu.SideEffectType.DATAFLOW_SIDE_EFFECTING
_TOKEN = jax.ShapeDtypeStruct((8, 128), F32)


def _peer(rel):
    pos = (lax.axis_index("x"), lax.axis_index("y"), lax.axis_index("c"))
    return tuple(1 - p if (rel >> (2 - i)) & 1 else p for i, p in enumerate(pos))


def _index_of(dev):
    return 4 * dev[0] + 2 * dev[1] + dev[2]


def _split_copies(src_ref, land_ref, sems, plan):
    n = len(plan)
    return [pltpu.make_async_remote_copy(
        src_ref=src_ref if s is None else src_ref.at[s], dst_ref=land_ref.at[d], send_sem=sems[k],
        recv_sem=sems[n + k], device_id=peer, device_id_type=_MESH) for k, (s, d, peer) in enumerate(plan)]


def _split_start(src, n_land, plan_fn, after, name):
    blk = src.shape[-2:]
    land = lax.empty((n_land,) + blk, src.dtype)
    n = len(plan_fn())

    def body(src_ref, land_ref, after_ref, *outs):
        for cp in _split_copies(src_ref, land_ref, outs[:2 * n], plan_fn()):
            cp.start()
        outs[2 * n + 2][...] = jnp.zeros_like(outs[2 * n + 2])

    res = pl.pallas_call(
        body, name=name,
        out_shape=(pltpu.SemaphoreType.DMA(()),) * (2 * n)
        + (pltpu.HBM(src.shape, src.dtype), pltpu.HBM(land.shape, land.dtype), _TOKEN),
        in_specs=(_HBM, _HBM, _ANY),
        out_specs=(_SEM,) * (2 * n) + (_HBM, _HBM, pl.BlockSpec(memory_space=pltpu.VMEM)),
        input_output_aliases={0: 2 * n, 1: 2 * n + 1},
        compiler_params=pltpu.CompilerParams(has_side_effects=_EFFECT),
    )(pltpu.with_memory_space_constraint(src, pltpu.HBM), pltpu.with_memory_space_constraint(land, pltpu.HBM), after)
    return res[:2 * n], res[2 * n], res[2 * n + 1], res[2 * n + 2]


def _split_wait(sems, src, land, plan_fn, after, name):
    n = len(sems) // 2

    def body(src_ref, land_ref, *rest):
        for cp in _split_copies(src_ref, land_ref, rest[:2 * n], plan_fn()):
            cp.wait_send()
            cp.wait_recv()

    return pl.pallas_call(
        body, name=name,
        out_shape=(pltpu.HBM(src.shape, src.dtype), pltpu.HBM(land.shape, land.dtype)),
        in_specs=(_HBM, _HBM) + (_SEM,) * (2 * n) + (_ANY,), out_specs=(_HBM, _HBM),
        input_output_aliases={0: 0, 1: 1},
        compiler_params=pltpu.CompilerParams(has_side_effects=_EFFECT),
    )(src, land, *sems, after)


def _gather_plan():
    me = _index_of(_peer(0))
    return [(None, me, _peer(rel)) for rel in range(1, N_DEV)]


def _gather_wait_plan():
    return [(None, _index_of(_peer(rel)), _peer(rel)) for rel in range(1, N_DEV)]


def _chip_plan():
    return [(_index_of(_peer(rel)) // 2, j, _peer(rel)) for j, rel in enumerate((4, 2, 6))]


def _owner_plan():
    return [(_index_of(_peer(rel)), rel - 1, _peer(rel)) for rel in range(1, N_DEV)]


def _pair_sum(g, t1, my_c, name, tr):
    _, r, c = g.shape

    def body(c_ref, g_ref, t_ref, o_ref, ob_ref):
        s = g_ref[...] + t_ref[...]
        o_ref[...] = s
        ob_ref[...] = s.astype(BF16)

    blk = pl.BlockSpec((None, tr, c), lambda k, i, cr: (k, i, 0))
    return pl.pallas_call(
        body,
        grid_spec=pltpu.PrefetchScalarGridSpec(
            num_scalar_prefetch=1, grid=(4, r // tr),
            in_specs=[pl.BlockSpec((None, tr, c), lambda k, i, cr: (2 * k + cr[0], i, 0)), blk],
            out_specs=[blk, blk]),
        out_shape=[jax.ShapeDtypeStruct((4, r, c), F32), jax.ShapeDtypeStruct((4, r, c), BF16)],
        compiler_params=_params(("parallel", "parallel")), name=name,
    )(my_c, g, t1)


def _adam_math(g, w, m, v):
    m = ADAM_B1 * m + (1.0 - ADAM_B1) * g
    v = ADAM_B2 * v + (1.0 - ADAM_B2) * (g * g)
    m_hat = m / (1.0 - ADAM_B1 ** ADAM_STEP)
    v_hat = v / (1.0 - ADAM_B2 ** ADAM_STEP)
    delta = -ADAM_LR * (m_hat / (jnp.sqrt(v_hat) + ADAM_EPS) + ADAM_WD * w)
    return delta, m, v


def _grad_sum(own, own_index, recv, name, tr):
    _, r, c = own.shape
    n = recv.shape[0]

    def body(k_ref, own_ref, *rest):
        g = own_ref[...]
        for recv_ref in rest[:n]:
            g = g + recv_ref[...].astype(F32)
        rest[n][...] = g

    def slot(j):
        return pl.BlockSpec((None, tr, c), lambda i, kr: (j, i, 0))

    return pl.pallas_call(
        body,
        grid_spec=pltpu.PrefetchScalarGridSpec(
            num_scalar_prefetch=1, grid=(r // tr,),
            in_specs=[pl.BlockSpec((None, tr, c), lambda i, kr: (kr[0], i, 0))] + [slot(j) for j in range(n)],
            out_specs=pl.BlockSpec((tr, c), lambda i, kr: (i, 0))),
        out_shape=jax.ShapeDtypeStruct((r, c), F32),
        compiler_params=_params(("parallel",)), name=name,
    )(own_index, own, *([recv] * n))


def _adam_many(g, w, m, v, row_tiles, name):
    n = len(g)

    def body(*refs):
        ins, outs = refs[:4 * n], refs[4 * n:]
        for i in range(n):
            res = _adam_math(ins[i][...], ins[n + i][...], ins[2 * n + i][...], ins[3 * n + i][...])
            for kind in range(3):
                outs[kind * n + i][...] = res[kind]

    def spec(a):
        blk = (a.shape[0] // row_tiles,) + a.shape[1:]
        return pl.BlockSpec(blk, lambda t, nd=a.ndim: (t,) + (0,) * (nd - 1))

    specs = [spec(a) for a in g]
    res = pl.pallas_call(
        body, grid=(row_tiles,), in_specs=specs * 4, out_specs=specs * 3,
        out_shape=[jax.ShapeDtypeStruct(a.shape, F32) for a in g] * 3,
        compiler_params=_params(("parallel",)), name=name,
    )(*g, *w, *m, *v)
    return res[:n], res[n:2 * n], res[2 * n:]


def _sum8(g8, name):
    _, r, c = g8.shape

    def body(g_ref, o_ref):
        acc = g_ref[0]
        for j in range(1, N_DEV):
            acc = acc + g_ref[j]
        o_ref[...] = acc

    return pl.pallas_call(
        body, grid=(1,), in_specs=[pl.BlockSpec((N_DEV, r, c), lambda i: (0, 0, 0))],
        out_specs=pl.BlockSpec((r, c), lambda i: (0, 0)), out_shape=jax.ShapeDtypeStruct((r, c), F32),
        compiler_params=_params(("arbitrary",)), name=name,
    )(g8)


def _pack(arrs, pad_rows=8):
    flat = jnp.concatenate([a.reshape(-1) for a in arrs])
    n = flat.shape[0]
    q = PACK_C * pad_rows
    tot = -(-n // q) * q
    if tot != n:
        flat = jnp.concatenate([flat, jnp.zeros((tot - n,), flat.dtype)])
    return flat.reshape(tot // PACK_C, PACK_C)


def _unpack(buf, shapes):
    flat = buf.reshape(-1)
    out, off = [], 0
    for s in shapes:
        n = int(np.prod(s))
        out.append(flat[off:off + n].reshape(s))
        off += n
    return out


GROUPS = (("w_in",),
          ("w_glu", "w_ssm_br", "w_mem_br", "w_attn_br"),
          ("w_up", "w_down"),
          ("w_mem_kv", "w_o"))
GROUP_TR = (400, 384, 512, 256)
MLP_GROUP = 2
MIXER_GROUPS = (1, 3)
ATTN_BR_FOLD = 2


def _stored_shape(name):
    r, c, ax = BIG_SHAPE[name]
    rows, cols = (r // N_DEV, c) if ax == 0 else (c // N_DEV, r)
    return (rows // ATTN_BR_FOLD, cols * ATTN_BR_FOLD) if name == "w_attn_br" else (rows, cols)


def _stored(shard, name):
    a = shard[0].T if BIG_SHAPE[name][2] == 1 else shard[0]
    return a.reshape(_stored_shape(name))


def _unstored(a, name):
    r, c, ax = BIG_SHAPE[name]
    if ax == 0:
        return a.reshape(1, r // N_DEV, c)
    return a.reshape(c // N_DEV, r).T[None]


def _pack_group(d, names):
    return jnp.concatenate([_stored(d[n], n) for n in names], axis=0)


def _split_group(buf, names):
    out, off = {}, 0
    for n in names:
        rows = _stored_shape(n)[0]
        out[n] = buf[..., off:off + rows, :]
        off += rows
    return out


def _full_stored(stacked, name):
    r, c, ax = BIG_SHAPE[name]
    return stacked.reshape((r, c) if ax == 0 else (c, r))


def _stacked_stored(full, name):
    return full.reshape((N_DEV,) + _stored_shape(name))


def _gelu_parts(x):
    c0, c1 = math.sqrt(2.0 / math.pi), 0.044715
    th = jnp.tanh(c0 * (x + c1 * x * x * x))
    return th, c0, c1


def _local_step(x, mem, tgt, wb, sp, late_weights, grads_ready, small_grads_ready):
    l = x.shape[0]
    w_a, w_g = wb["w_in"][:ZA_W], wb["w_in"][ZA_W:]

    a_re, a_im, bb_re, bb_im = _discretize(sp["ssm_lambda_re"], sp["ssm_lambda_im"], sp["ssm_log_dt"],
                                           sp["ssm_b_re"], sp["ssm_b_im"])
    a_pair = jnp.stack([a_re.reshape(1, SSM_S), a_im.reshape(1, SSM_S)])
    a_conj = jnp.stack([a_re.reshape(1, SSM_S), -a_im.reshape(1, SSM_S)])
    b_re_t, b_im_t = _bd_in(bb_re).astype(BF16), _bd_in(bb_im).astype(BF16)
    c_re_t = _bd_in(sp["ssm_c_re"].transpose(0, 2, 1)).astype(BF16)
    c_im_t = (-_bd_in(sp["ssm_c_im"].transpose(0, 2, 1))).astype(BF16)
    d_row = sp["ssm_d"].reshape(1, SSM_W)

    n1 = _rms_fwd(x, sp["norm1_g"], "rms1")
    za = _mm(n1, w_a, [BF16], tb=True, name="in_proj_a", tn=1664)
    zg = _mm(n1, w_g, [BF16], tb=True, name="in_proj_g")
    for gi in MIXER_GROUPS:
        wb = {**wb, **late_weights(gi, za)}
    u = za[:, :SSM_W]
    mq = za[:, ZA_W - MEM_W:]

    u_s = _scan_order(u)
    s_all = _ssm_scan(u_s, b_re_t, b_im_t, a_pair, reverse=False, name="ssm_scan_fwd")
    def gelu_epi(acc, ut, dr):
        y = acc + dr * ut.astype(F32)
        th, _, _ = _gelu_parts(y)
        return y, 0.5 * y * (1.0 + th)
    y0, y1 = [_time_order(t) for t in _mm(s_all, _tiled(c_re_t, c_im_t), [F32, BF16], tb=True, epi=gelu_epi,
                                          mn=[u_s], rows=[d_row], bd=SSM_BD, tm=2048, name="ssm_cs")]

    def glu_epi(acc, y1t, bg):
        t = acc + bg
        return t, y1t.astype(F32) * _sigmoid(t)
    t_glu, y2 = _mm(y1, wb["w_glu"], [F32, BF16], epi=glu_epi, mn=[y1], rows=[sp["b_glu"]], name="ssm_glu")
    br_ssm = _mm(y2, wb["w_ssm_br"], [BF16], tb=True, name="ssm_br")

    qkv_p, o_g, lse_g = [], [], []
    for g, d in enumerate(DILATIONS):
        nb = l // d // ATT_WIN
        cols = [za[:, SSM_W + (3 * j + g) * ATT_GW: SSM_W + (3 * j + g + 1) * ATT_GW] for j in range(3)]
        qp, kp, vp = [_to_perm(cc, d) for cc in cols]
        qkv_p.append((qp, kp, vp))
        og, lg = _attn_fwd(qp, kp, vp, nb, "attn_fwd%d" % g)
        o_g.append(_from_perm(og, d))
        lse_g.append(_from_perm(lg, d))

    def merge_fn(r, b):
        o0, o1, o2, l0, l1, l2 = r
        mx = jnp.maximum(jnp.maximum(l0, l1), l2)
        e0, e1, e2 = jnp.exp(l0 - mx), jnp.exp(l1 - mx), jnp.exp(l2 - mx)
        tot = e0 + e1 + e2
        return [(e0 * o0 + e1 * o1 + e2 * o2) / tot, mx + jnp.log(tot)], []
    o_att, lse_tot = _ew(merge_fn, o_g + lse_g, [], [(ATT_GW, F32), (ATT_GW, F32)], [], name="attn_merge", tm=2048)
    br_attn = _mm(o_att, wb["w_attn_br"], [BF16], tb=True, name="attn_br")

    mn = _rms_fwd(mem, sp["mem_norm_g"], "rms_mem")
    kv = _mm(mn, wb["w_mem_kv"], [BF16], name="mem_kv")
    mo = _mem_fwd(mq, kv, "mem_attn_fwd")
    br_mem = _mm(mo, wb["w_mem_br"], [BF16], tb=True, name="mem_br")

    merged, h1, n2 = _gated_out_proj(zg, [br_ssm, br_attn, br_mem], sp["b_gate"], wb["w_o"], x, sp["norm2_g"],
                                     "gated_o_proj")

    wm = late_weights(MLP_GROUP, n2)
    r_act = _mm(n2, wm["w_up"], [BF16], tb=True, epi=lambda acc: (jnp.maximum(acc, 0.0),), name="mlp_up")
    def down_epi(acc, ht, tv, gf):
        hv = acc + ht
        rs = lax.rsqrt(jnp.mean(hv * hv, axis=-1, keepdims=True) + RMS_EPS)
        err = hv * rs * gf - tv
        dh, dgf = _rms_bwd_tile(hv, err * (1.0 / D_MODEL), gf)
        return dh, dgf, _colsum(err * err) * (0.5 / D_MODEL)
    dh2, d_final_g, loss_cols = _mm(r_act, wm["w_down"], [F32], square_a=True, epi=down_epi, mn=[h1, tgt], rows=[sp["final_g"]],
                                    n_sums=2, tk=1024, name="mlp_down")
    loss = jnp.sum(loss_cols, axis=1, keepdims=True)

    gw, gs = {}, {"final_g": d_final_g}
    d_act = _mm(dh2, wm["w_down"], [BF16], tb=True, epi=lambda acc, ra: (acc * 2.0 * ra.astype(F32),), mn=[r_act],
                name="mlp_down_dx")
    dw_down = _mm(r_act, dh2, [F32], ta=True, square_a=True, name="mlp_down_dw")
    dw_up = _mm(d_act, n2, [F32], ta=True, name="mlp_up_dw")
    token = grads_ready(MLP_GROUP, {"w_up": dw_up, "w_down": dw_down})
    def up_dx_epi(acc, ht, dht, g2):
        dx, dg = _rms_bwd_tile(ht, acc, g2)
        return dx + dht, dg
    dh1, gs["norm2_g"] = _mm(d_act, wm["w_up"], [F32], epi=up_dx_epi, mn=[h1, dh2],
                             rows=[sp["norm2_g"] + token[:1, :1]], n_sums=1, tk=1024, name="mlp_up_dx")
    gw["w_o"] = _mm(merged, dh1, [F32], ta=True, name="o_proj_dw")

    def gate_bwd_epi(dm, *tiles):
        dbr, dz = [], []
        for zt, bt, bias in zip(tiles[0:3], tiles[3:6], tiles[6:9]):
            gt = _sigmoid(zt.astype(F32) + bias)
            dbr.append(dm * gt)
            dz.append(dm * bt.astype(F32) * gt * (1.0 - gt))
        return (*dbr, *dz, *[_colsum(t) for t in dz])
    gate_bias = [sp["b_gate"][:, i * D_MODEL:(i + 1) * D_MODEL] for i in range(3)]
    res = _mm(dh1, wb["w_o"], [BF16] * 6, tb=True, epi=gate_bwd_epi, mn=[(zg, 0), (zg, 1), (zg, 2), br_ssm, br_attn, br_mem],
              rows=gate_bias, n_sums=3, tm=512, name="o_proj_dx")
    (dbr_ssm, dbr_attn, dbr_mem), dzg = res[0:3], res[3:6]
    gs["b_gate"] = jnp.concatenate(res[6:9], axis=1)

    gw["w_ssm_br"] = _mm(dbr_ssm, y2, [F32], ta=True, name="ssm_br_dw")
    def glu_bwd_epi(dy, y1t, tt):
        sg = _sigmoid(tt)
        dt = dy * y1t.astype(F32) * sg * (1.0 - sg)
        return dt, dy * sg, _colsum(dt)
    dt_glu, dy1a, gs["b_glu"] = _mm(dbr_ssm, wb["w_ssm_br"], [BF16, F32], epi=glu_bwd_epi, mn=[y1, t_glu], n_sums=1,
                                    name="ssm_br_dx")
    gw["w_glu"] = _mm(y1, dt_glu, [F32], ta=True, name="ssm_glu_dw")

    def gelu_bwd_epi(acc, dy1t, y0t, ut):
        th, c0, c1 = _gelu_parts(y0t)
        dg = 0.5 * (1.0 + th) + 0.5 * y0t * (1.0 - th * th) * c0 * (1.0 + 3.0 * c1 * y0t * y0t)
        dy = (acc + dy1t) * dg
        return dy, _colsum(dy * ut.astype(F32))
    dy0, gs["ssm_d"] = _mm(dt_glu, wb["w_glu"], [F32], tb=True, epi=gelu_bwd_epi, mn=[dy1a, y0, u], n_sums=1,
                           name="ssm_glu_dx")
    dy0_s = _scan_order(dy0)
    lam, da, d_b, d_c = _ssm_scan(dy0_s, c_re_t, c_im_t, a_conj, reverse=True, s_fwd=s_all, u=u_s,
                                  name="ssm_scan_bwd")
    du = _time_order(_mm(lam, _tiled(b_re_t, b_im_t), [BF16], tb=True,
                         epi=lambda acc, dyt, dr: (acc + dyt * dr,), mn=[dy0_s], rows=[d_row], bd=SSM_BD, tm=2048, name="ssm_bu_dx"))
    gs["a_re"], gs["a_im"] = da[0], da[1]
    (dbr, dbi), (dcr, dci) = _untiled(d_b), _untiled(d_c)
    gs["bb_re"], gs["bb_im"] = _bd_diag(dbr).transpose(0, 2, 1), _bd_diag(dbi).transpose(0, 2, 1)
    gs["ssm_c_re"], gs["ssm_c_im"] = _bd_diag(dcr), -_bd_diag(dci)

    gw["w_attn_br"] = _mm(dbr_attn, o_att, [F32], ta=True, name="attn_br_dw")

    def do_epi(acc, ot):
        prod = acc * ot
        head = lax.broadcasted_iota(jnp.int32, prod.shape, 1) // ATT_E
        dd = jnp.zeros_like(prod)
        for h in range(ATT_HG):
            dd = jnp.where(head == h, jnp.sum(jnp.where(head == h, prod, 0.0), axis=1, keepdims=True), dd)
        return acc, dd
    do_att, dd_att = _mm(dbr_attn, wb["w_attn_br"], [BF16, F32], epi=do_epi, mn=[o_att], name="attn_br_dx")
    dq_l, dk_l, dv_l = [], [], []
    for g, d in enumerate(DILATIONS):
        nb = l // d // ATT_WIN
        qp, kp, vp = qkv_p[g]
        dq, dk, dv = _attn_bwd(qp, kp, vp, _to_perm(do_att, d), _to_perm(lse_tot, d), _to_perm(dd_att, d),
                               nb, "attn_bwd%d" % g)
        dq_l.append(_from_perm(dq, d))
        dk_l.append(_from_perm(dk, d))
        dv_l.append(_from_perm(dv, d))

    gw["w_mem_br"] = _mm(dbr_mem, mo, [F32], ta=True, name="mem_br_dw")
    dmo = _mm(dbr_mem, wb["w_mem_br"], [BF16], name="mem_br_dx")
    dmq, dkv = _mem_bwd(mq, kv, dmo, "mem_attn_bwd")
    gw["w_mem_kv"] = _mm(mn, dkv, [F32], ta=True, name="mem_kv_dw")
    dmn = _mm(dkv, wb["w_mem_kv"], [F32], tb=True, name="mem_kv_dx")
    token = sum(grads_ready(gi, gw) for gi in MIXER_GROUPS)
    gs["mem_norm_g"] = _rms_bwd(mem, dmn, None, sp["mem_norm_g"] + token[:1, :1], "rms_mem_bwd")[1]

    dza = jnp.concatenate([du] + dq_l + dk_l + dv_l + [dmq], axis=1)
    dn_a = _mm(dza, w_a, [F32], name="in_proj_a_dx", tk=1664)
    dw_a = _mm(dza, n1, [F32], ta=True, name="in_proj_a_dw", tm=1664)
    dw_g = [_mm(dzg[i], n1, [F32], ta=True, name="in_proj_g_dw%d" % i) for i in range(3)]
    gw["w_in"] = jnp.concatenate([dw_a] + dw_g, axis=0)
    token = grads_ready(0, gw) + small_grads_ready(gs)
    def in_dx_epi(acc, pt, xt, dht, g1):
        dx, dg = _rms_bwd_tile(xt, acc + pt, g1)
        return dx + dht, dg
    w_gs = [w_g[i * D_MODEL:(i + 1) * D_MODEL] for i in range(3)]
    grad_x, gs["norm1_g"] = _mm(dzg[0], w_gs[0], [F32], pair2=(dzg[1], w_gs[1], dzg[2], w_gs[2]), epi=in_dx_epi,
                                mn=[dn_a, x, dh1],
                                rows=[sp["norm1_g"] + token[:1, :1]], n_sums=1, tm=512, name="in_proj_g_dx")
    return loss, grad_x, gs


_SMALL_GRAD_ORDER = ("norm1_g", "mem_norm_g", "b_gate", "a_re", "a_im", "bb_re", "bb_im", "ssm_c_re", "ssm_c_im",
                     "ssm_d", "b_glu", "norm2_g", "final_g")


def kernel(x, mem, norm1_g, mem_norm_g, w_in, b_gate, ssm_lambda_re, ssm_lambda_im, ssm_log_dt, ssm_b_re, ssm_b_im, ssm_c_re, ssm_c_im, ssm_d, w_glu, b_glu, w_ssm_br, w_attn_br, w_mem_kv, w_mem_br, w_o, norm2_g, w_up, w_down, final_g, loss_target, m_norm1_g, m_mem_norm_g, m_w_in, m_b_gate, m_ssm_lambda_re, m_ssm_lambda_im, m_ssm_log_dt, m_ssm_b_re, m_ssm_b_im, m_ssm_c_re, m_ssm_c_im, m_ssm_d, m_w_glu, m_b_glu, m_w_ssm_br, m_w_attn_br, m_w_mem_kv, m_w_mem_br, m_w_o, m_norm2_g, m_w_up, m_w_down, m_final_g, v_norm1_g, v_mem_norm_g, v_w_in, v_b_gate, v_ssm_lambda_re, v_ssm_lambda_im, v_ssm_log_dt, v_ssm_b_re, v_ssm_b_im, v_ssm_c_re, v_ssm_c_im, v_ssm_d, v_w_glu, v_b_glu, v_w_ssm_br, v_w_attn_br, v_w_mem_kv, v_w_mem_br, v_w_o, v_norm2_g, v_w_up, v_w_down, v_final_g):
    args = dict(locals())
    w = {n: args[n] for n in ALL_W}
    m = {n: args["m_" + n] for n in ALL_W}
    v = {n: args["v_" + n] for n in ALL_W}
    my_c = lax.axis_index("c").astype(jnp.int32).reshape(1)
    my_chip = (2 * lax.axis_index("x") + lax.axis_index("y")).astype(jnp.int32).reshape(1)

    w_pack = [_pack_group(w, names) for names in GROUPS]
    my_index = (4 * lax.axis_index("x") + 2 * lax.axis_index("y") + lax.axis_index("c")).astype(jnp.int32)
    zero = jnp.zeros((), jnp.int32)
    w_all = _allgather(w_pack[0].astype(BF16), "allgather_weights0")
    wb = {n: _full_stored(part, n) for n, part in _split_group(w_all, GROUPS[0]).items()}
    gathers = {gi: _split_start(w_pack[gi].astype(BF16), N_DEV, _gather_plan, w_all, "weights_gather_start%d" % gi)
               for gi in range(1, len(GROUPS))}

    def gathered(started, after, name):
        sems, src, land, _ = started
        src, land = _split_wait(sems, src, land, _gather_wait_plan, after, name)
        return lax.dynamic_update_slice(land, src[None], (my_index, zero, zero))

    def late_weights(gi, after):
        full = gathered(gathers[gi], after, "weights_gather_wait%d" % gi)
        return {n: _full_stored(part, n) for n, part in _split_group(full, GROUPS[gi]).items()}

    pending = {}

    def grads_ready(gi, grads):
        g_pack = jnp.concatenate([_stacked_stored(grads[n], n) for n in GROUPS[gi]], axis=1)
        if gi == 0:
            t1 = _pair_exchange(g_pack, "grad_pair_exchange%d" % gi)
            p_sum, p_bf = _pair_sum(g_pack, t1, my_c, "grad_pair_sum%d" % gi, GROUP_TR[gi])
            started = _split_start(p_bf, 3, _chip_plan, p_sum, "grad_chip_exchange_start%d" % gi)
            pending[gi] = (p_sum, my_chip, started, _chip_plan)
        else:
            started = _split_start(g_pack.astype(BF16), N_DEV - 1, _owner_plan, g_pack, "grad_exchange_start%d" % gi)
            pending[gi] = (g_pack, my_index.reshape(1), started, _owner_plan)
        return started[3]

    early_small = [n for n in _SMALL_GRAD_ORDER if n != "norm1_g"]
    small_started = []

    def small_grads_ready(gs):
        started = _split_start(_pack([gs[n] for n in early_small]), N_DEV, _gather_plan, gs["mem_norm_g"],
                               "small_grads_gather_start")
        small_started.append((started, [gs[n].shape for n in early_small]))
        return started[3]

    sp = {
        "norm1_g": norm1_g + sum(started[3][:1, :1] for started in gathers.values()), "mem_norm_g": mem_norm_g, "b_gate": b_gate, "b_glu": b_glu, "norm2_g": norm2_g,
        "final_g": final_g.reshape(1, D_MODEL),
        "ssm_lambda_re": ssm_lambda_re[0], "ssm_lambda_im": ssm_lambda_im[0], "ssm_log_dt": ssm_log_dt[0],
        "ssm_b_re": ssm_b_re[0], "ssm_b_im": ssm_b_im[0], "ssm_c_re": ssm_c_re[0], "ssm_c_im": ssm_c_im[0],
        "ssm_d": ssm_d[0],
    }
    loss, grad_x, gs = _local_step(x[0], mem[0], loss_target[0], wb, sp, late_weights, grads_ready,
                                     small_grads_ready)
    loss = lax.psum(loss[0, 0], ("x", "y", "c"))
    n1_started = _split_start(_pack([gs["norm1_g"]]), N_DEV, _gather_plan, grad_x, "norm1_grad_gather_start")

    big_g = {}
    for gi, names in enumerate(GROUPS):
        own, own_index, (sems, src, land, _), plan = pending[gi]
        recv = _split_wait(sems, src, land, plan, grad_x, "grad_exchange_wait%d" % gi)[1]
        g_pack = _grad_sum(own, own_index, recv, "grad_sum%d" % gi, GROUP_TR[gi])
        for n, part in _split_group(g_pack, names).items():
            big_g[n] = _unstored(part, n)
    rows_of = lambda d, names: [d[n].reshape(d[n].shape[-2:]) for n in names]
    big_out = _adam_many(rows_of(big_g, BIG), rows_of(w, BIG), rows_of(m, BIG), rows_of(v, BIG), 8, "adam_big")
    big = [big_g] + [{n: a[None] for n, a in zip(BIG, outs)} for outs in big_out]

    (sg_started, sg_shapes), = small_started
    sg_all = jnp.concatenate([gathered(sg_started, big_out[0][0], "small_grads_gather_wait"),
                              gathered(n1_started, big_out[0][0], "norm1_grad_gather_wait")], axis=1)
    sg_sum = _sum8(sg_all, "sum_small_grads")
    n1_rows = n1_started[1].shape[0]
    sg = dict(zip(early_small, _unpack(sg_sum[:-n1_rows], sg_shapes)))
    sg["norm1_g"] = _unpack(sg_sum[-n1_rows:], [gs["norm1_g"].shape])[0]
    _, disc_vjp = jax.vjp(_discretize, sp["ssm_lambda_re"], sp["ssm_lambda_im"], sp["ssm_log_dt"],
                          sp["ssm_b_re"], sp["ssm_b_im"])
    d_lre, d_lim, d_ldt, d_bre, d_bim = disc_vjp((sg["a_re"].reshape(SSM_G, SSM_P), sg["a_im"].reshape(SSM_G, SSM_P),
                                                  sg["bb_re"], sg["bb_im"]))
    small_grad = {
        "norm1_g": sg["norm1_g"], "mem_norm_g": sg["mem_norm_g"], "b_gate": sg["b_gate"],
        "ssm_lambda_re": d_lre, "ssm_lambda_im": d_lim, "ssm_log_dt": d_ldt, "ssm_b_re": d_bre, "ssm_b_im": d_bim,
        "ssm_c_re": sg["ssm_c_re"], "ssm_c_im": sg["ssm_c_im"], "ssm_d": sg["ssm_d"], "b_glu": sg["b_glu"],
        "norm2_g": sg["norm2_g"], "final_g": sg["final_g"],
    }
    small_grad = {n: small_grad[n].reshape(w[n].shape) for n in SMALL}

    def squeezed(a):
        return a.reshape(a.shape[1:]) if a.ndim > 2 else a.reshape(1, -1)

    sq = lambda d: [squeezed(d[n]) for n in SMALL]
    small_out = _adam_many(sq(small_grad), sq(w), sq(m), sq(v), 1, "adam_small")
    small = [small_grad] + [{n: a.reshape(w[n].shape) for n, a in zip(SMALL, outs)} for outs in small_out]

    outs = [loss, grad_x[None]]
    for kind in range(4):
        for n in ALL_W:
            outs.append(big[kind][n] if n in BIG else small[kind][n])
    return tuple(outs)
```

```python
import math

import numpy as np
import jax
import jax.numpy as jnp
from jax import lax
from jax.experimental import pallas as pl
from jax.experimental.pallas import tpu as pltpu

F32 = jnp.float32
BF16 = jnp.bfloat16
_MXU = jnp.bfloat16

D_MODEL = 1024
SSM_G, SSM_H, SSM_P = 32, 16, 64
SSM_W = SSM_G * SSM_H
SSM_S = SSM_G * SSM_P
SSM_BD = 4
ATT_E = 64
ATT_HG = 4
ATT_GW = ATT_HG * ATT_E
ATT_WIN = 128
ATT_QB = 8
ATT_QB_FWD = 4
DILATIONS = (1, 4, 16)
MEM_H, MEM_E = 4, 128
MEM_W = MEM_H * MEM_E
ZA_W = SSM_W + 9 * ATT_GW + MEM_W
ZG_W = 3 * D_MODEL
IN_W = ZA_W + ZG_W
RMS_EPS = 1e-6
NEG_INF = -1e30

ADAM_LR, ADAM_B1, ADAM_B2, ADAM_EPS, ADAM_WD, ADAM_STEP = 0.001, 0.9, 0.999, 1e-08, 0.01, 10

N_DEV = 8
PACK_C = 512
_VMEM_LIMIT = 56 * 1024 * 1024
SUBLANES = 16
SCAN_SEG = 128
SCAN_CHAINS = 4
SCAN_UNROLL = 4
SCAN_W = 128

BIG = ("w_in", "w_glu", "w_ssm_br", "w_attn_br", "w_mem_kv", "w_mem_br", "w_o", "w_up", "w_down")
BIG_SHAPE = {
    "w_in": (D_MODEL, IN_W, 1), "w_glu": (SSM_W, SSM_W, 0), "w_ssm_br": (SSM_W, D_MODEL, 1),
    "w_attn_br": (ATT_GW, D_MODEL, 1), "w_mem_kv": (D_MODEL, 2 * MEM_W, 0), "w_mem_br": (MEM_W, D_MODEL, 1),
    "w_o": (D_MODEL, D_MODEL, 0), "w_up": (D_MODEL, 4 * D_MODEL, 1), "w_down": (4 * D_MODEL, D_MODEL, 0),
}
SMALL = ("norm1_g", "mem_norm_g", "b_gate", "ssm_lambda_re", "ssm_lambda_im", "ssm_log_dt", "ssm_b_re",
         "ssm_b_im", "ssm_c_re", "ssm_c_im", "ssm_d", "b_glu", "norm2_g", "final_g")
ALL_W = ("norm1_g", "mem_norm_g", "w_in", "b_gate", "ssm_lambda_re", "ssm_lambda_im", "ssm_log_dt", "ssm_b_re",
         "ssm_b_im", "ssm_c_re", "ssm_c_im", "ssm_d", "w_glu", "b_glu", "w_ssm_br", "w_attn_br", "w_mem_kv",
         "w_mem_br", "w_o", "norm2_g", "w_up", "w_down", "final_g")


def _params(sem):
    return pltpu.CompilerParams(dimension_semantics=sem, vmem_limit_bytes=_VMEM_LIMIT)


def _pick(n, cap):
    if n <= cap:
        return n
    t = (cap // 128) * 128
    while n % t:
        t -= 128
    return t


def _mm(a, b, outs, *, name, ta=False, tb=False, epi=None, mn=(), rows=(), pair2=None, bd=0, n_sums=0,
        tm=1024, tn=1024, tk=2048):
    ab = [a, b] + (list(pair2) if pair2 is not None else [])
    a_shape, b_shape = ab[0].shape, ab[1].shape
    m = a_shape[1] if ta else a_shape[0]
    k = a_shape[0] if ta else a_shape[1]
    n = b_shape[0] if tb else b_shape[1]
    assert k == (b_shape[1] if tb else b_shape[0]), (name, a_shape, b_shape)
    out_n = n
    if bd and ta:
        assert not tb
        tm, tn, tk = m // bd, n // bd, _pick(k, tk)
        grid, out_n = (bd, 1, k // tk), tn
        a_blk = ((tk, tm), lambda i, j, kk: (kk, i))
        b_blk = ((tk, tn), lambda i, j, kk: (kk, i))
        mn_spec = pl.BlockSpec((tm, tn), lambda i, j, kk: (i, 0))
    elif bd:
        tm, tn, tk = _pick(m, tm), n // bd, k // bd
        grid = (m // tm, bd, 1)
        a_blk = ((tm, tk), lambda i, j, kk: (i, j))
        b_blk = ((tn, tk) if tb else (tk, tn), lambda i, j, kk: (j, j))
        mn_spec = pl.BlockSpec((tm, tn), lambda i, j, kk: (i, j))
    else:
        tm, tn, tk = _pick(m, tm), _pick(n, tn), _pick(k, tk)
        grid = (m // tm, n // tn, k // tk)
        a_blk = ((tk, tm), lambda i, j, kk: (kk, i)) if ta else ((tm, tk), lambda i, j, kk: (i, kk))
        b_blk = ((tn, tk), lambda i, j, kk: (j, kk)) if tb else ((tk, tn), lambda i, j, kk: (kk, j))
        mn_spec = pl.BlockSpec((tm, tn), lambda i, j, kk: (i, j))

    ab_specs = [pl.BlockSpec(*(a_blk if q % 2 == 0 else b_blk)) for q in range(len(ab))]
    mn_arrays = [e[0] if isinstance(e, tuple) else e for e in mn]
    mn_specs = [pl.BlockSpec((tm, tn), lambda i, j, kk, c=e[1]: (i, c)) if isinstance(e, tuple) else mn_spec
                for e in mn]
    nk = grid[2]
    row_spec = pl.BlockSpec((1, tn), lambda i, j, kk: (0, j))
    n_ex, n_out = len(mn) + len(rows), len(outs)
    assert n_sums == 0 or (grid[1] == 1 and not bd)
    dims = (((0 if ta else 1,), (1 if tb else 0,)), ((), ()))

    def body(*refs):
        ab_refs, rest = refs[:len(ab)], refs[len(ab):]
        ex, o_refs = rest[:n_ex], rest[n_ex:n_ex + n_out]
        s_refs = rest[n_ex + n_out:n_ex + n_out + n_sums]
        first_row_tile = pl.program_id(0) == 0
        kk = pl.program_id(2)

        pairs = list(zip(ab_refs[0::2], ab_refs[1::2]))

        def product(pair):
            return lax.dot_general(pair[0][...].astype(_MXU), pair[1][...].astype(_MXU), dims,
                                   preferred_element_type=F32)

        def finish(total):
            vals = (total,) if epi is None else epi(total, *[r[...] for r in ex])
            for r, v in zip(o_refs, vals):
                r[...] = v.astype(r.dtype)
            for r, v in zip(s_refs, vals[n_out:]):
                r[...] = jnp.where(first_row_tile, v, r[...] + v)

        if nk == 1:
            total = product(pairs[0])
            for pair in pairs[1:]:
                total = total + product(pair)
            finish(total)
        else:
            acc = rest[-1]

            @pl.when(kk == 0)
            def _():
                acc[...] = jnp.zeros_like(acc)

            for pair in pairs:
                acc[...] += product(pair)

            @pl.when(kk == nk - 1)
            def _():
                finish(acc[...])

    res = pl.pallas_call(
        body, grid=grid,
        in_specs=ab_specs + mn_specs + [row_spec] * len(rows),
        out_specs=[mn_spec] * n_out + [row_spec] * n_sums,
        out_shape=[jax.ShapeDtypeStruct((m, out_n), dt) for dt in outs]
        + [jax.ShapeDtypeStruct((1, out_n), F32)] * n_sums,
        scratch_shapes=[pltpu.VMEM((tm, tn), F32)] if nk > 1 else [],
        compiler_params=_params(("arbitrary" if n_sums else "parallel", "parallel", "arbitrary")), name=name,
    )(*ab, *mn_arrays, *rows)
    return res[0] if n_out + n_sums == 1 else res


def _ew(fn, rows, bcs, out_rows, out_accs, *, name, tm=256):
    r = rows[0].shape[0]
    tm = min(tm, r)
    assert r % tm == 0
    nr, nb, no, na = len(rows), len(bcs), len(out_rows), len(out_accs)

    def body(*refs):
        i = pl.program_id(0)
        r_in, b_in = refs[:nr], refs[nr:nr + nb]
        o_r, o_a = refs[nr + nb:nr + nb + no], refs[nr + nb + no:]
        outs, accs = fn([x[...] for x in r_in], [x[...] for x in b_in])
        for ref, v in zip(o_r, outs):
            ref[...] = v.astype(ref.dtype)
        if na:
            @pl.when(i == 0)
            def _():
                for ref in o_a:
                    ref[...] = jnp.zeros_like(ref)

            for ref, v in zip(o_a, accs):
                ref[...] += v

    res = pl.pallas_call(
        body, grid=(r // tm,),
        in_specs=[pl.BlockSpec((tm, x.shape[1]), lambda i: (i, 0)) for x in rows]
        + [pl.BlockSpec((1, x.shape[1]), lambda i: (0, 0)) for x in bcs],
        out_specs=[pl.BlockSpec((tm, c), lambda i: (i, 0)) for c, _ in out_rows]
        + [pl.BlockSpec((1, c), lambda i: (0, 0)) for c in out_accs],
        out_shape=[jax.ShapeDtypeStruct((r, c), dt) for c, dt in out_rows]
        + [jax.ShapeDtypeStruct((1, c), F32) for c in out_accs],
        compiler_params=_params(("arbitrary",)), name=name,
    )(*rows, *bcs)
    return res


def _colsum(x):
    return jnp.sum(x, axis=0, keepdims=True)


def _sigmoid(x):
    return 1.0 / (1.0 + jnp.exp(-x))


def _rms_bwd_tile(xv, dv, g):
    rs = lax.rsqrt(jnp.mean(xv * xv, axis=-1, keepdims=True) + RMS_EPS)
    gd = dv * g
    dx = rs * gd - xv * (rs * rs * rs) * jnp.mean(gd * xv, axis=-1, keepdims=True)
    return dx, _colsum(dv * xv * rs)


def _rms_fwd(x, g, name):
    def fn(r, b):
        xv = r[0]
        rs = lax.rsqrt(jnp.mean(xv * xv, axis=-1, keepdims=True) + RMS_EPS)
        return [xv * rs * b[0]], []
    return _ew(fn, [x], [g], [(x.shape[1], BF16)], [], name=name, tm=1024)[0]


def _rms_bwd(x, dn, res, g, name):
    def fn(r, b):
        dx, dg = _rms_bwd_tile(r[0], r[1], b[0])
        if res is not None:
            dx = dx + r[2]
        return [dx], [dg]
    rows = [x, dn] + ([res] if res is not None else [])
    return _ew(fn, rows, [g], [(x.shape[1], F32)], [x.shape[1]], name=name)


def _scan_order(x):
    l, c = x.shape
    return x.reshape(l // (SUBLANES * SCAN_SEG), SUBLANES, SCAN_SEG, c).transpose(0, 2, 1, 3).reshape(l, c)


def _time_order(x):
    l, c = x.shape
    return x.reshape(l // (SUBLANES * SCAN_SEG), SCAN_SEG, SUBLANES, c).transpose(0, 2, 1, 3).reshape(l, c)


def _ssm_scan(x, w_re, w_im, a_pair, *, reverse, s_fwd=None, u=None, name):
    l = x.shape[0]
    seg, w = SCAN_SEG, SCAN_W
    bd_w = SSM_W // SSM_BD
    tiles_per_bd = SSM_S // SSM_BD // w
    nch = min(SCAN_CHAINS, l // (SUBLANES * seg))
    chain_rows = SUBLANES * seg
    tb = nch * chain_rows
    nt = l // tb
    with_da = s_fwd is not None
    assert reverse or not with_da

    def tt(t):
        return nt - 1 - t if reverse else t

    def body(*refs):
        if with_da:
            (x_ref, wr_ref, wi_ref, a_ref, sf_ref, sp_ref, u_ref, s_ref, da_ref, dw_ref, dx_ref,
             p_ref, c_ref, b_scr, l_scr) = refs
        else:
            x_ref, wr_ref, wi_ref, a_ref, s_ref, p_ref, c_ref, b_scr, l_scr = refs
        t_blk = pl.program_id(1)
        ar, ai = a_ref[0], a_ref[1]

        @pl.when(t_blk == 0)
        def _():
            def pstep(i, carry):
                pr, pi = carry
                p_ref[0, pl.ds(i, 1), :] = pr
                p_ref[1, pl.ds(i, 1), :] = pi
                return pr * ar - pi * ai, pr * ai + pi * ar

            lax.fori_loop(0, seg, pstep, (ar, ai))
            c_ref[...] = jnp.zeros_like(c_ref)
            if with_da:
                da_ref[...] = jnp.zeros_like(da_ref)
                dw_ref[...] = jnp.zeros_like(dw_ref)
                dx_ref[...] = jnp.zeros_like(dx_ref)

        xb = x_ref[...].astype(_MXU)
        b_scr[:, :w] = jnp.dot(xb, wr_ref[...], preferred_element_type=F32)
        b_scr[:, w:] = jnp.dot(xb, wi_ref[...], preferred_element_type=F32)
        arb, aib = jnp.broadcast_to(ar, (SUBLANES, w)), jnp.broadcast_to(ai, (SUBLANES, w))
        zero = jnp.zeros((SUBLANES, w), F32)

        def tile(g, step):
            return pl.ds(pl.multiple_of(g * chain_rows + step * SUBLANES, SUBLANES), SUBLANES)

        def rows(g, i):
            return tile(g, seg - 1 - i if reverse else i)

        def local_step(i, carry):
            out = []
            for g in range(nch):
                sr, si = carry[2 * g], carry[2 * g + 1]
                idx = rows(g, i)
                sr, si = arb * sr - aib * si + b_scr[idx, :w], arb * si + aib * sr + b_scr[idx, w:]
                l_scr[idx, :w] = sr
                l_scr[idx, w:] = si
                out += [sr, si]
            return tuple(out)

        def unrolled(step_fn, first):
            def trip(q, carry):
                for r in range(SCAN_UNROLL):
                    carry = step_fn(first + q * SCAN_UNROLL + r, carry)
                return carry
            return trip

        ends = lax.fori_loop(0, seg // SCAN_UNROLL, unrolled(local_step, 0), (zero,) * (2 * nch))

        a_seg_r, a_seg_i = p_ref[0, seg - 1:seg, :], p_ref[1, seg - 1:seg, :]
        cr, ci = c_ref[0], c_ref[1]
        sub = lax.broadcasted_iota(jnp.int32, (SUBLANES, w), 0)
        ins = [[zero, zero] for _ in range(nch)]
        order = [(g, k) for g in range(nch) for k in range(SUBLANES)]
        for g, k in (order[::-1] if reverse else order):
            ins[g] = [jnp.where(sub == k, cr, ins[g][0]), jnp.where(sub == k, ci, ins[g][1])]
            er, ei = ends[2 * g][k:k + 1], ends[2 * g + 1][k:k + 1]
            cr, ci = er + a_seg_r * cr - a_seg_i * ci, ei + a_seg_r * ci + a_seg_i * cr
        c_ref[0] = cr
        c_ref[1] = ci

        def fix(g, i):
            idx = rows(g, i)
            pr, pi = p_ref[0, pl.ds(i, 1), :], p_ref[1, pl.ds(i, 1), :]
            sr = l_scr[idx, :w] + pr * ins[g][0] - pi * ins[g][1]
            si = l_scr[idx, w:] + pr * ins[g][1] + pi * ins[g][0]
            s_ref[idx, :w] = sr.astype(s_ref.dtype)
            s_ref[idx, w:] = si.astype(s_ref.dtype)
            return sr, si

        if not with_da:
            def fix_step(i, carry):
                for g in range(nch):
                    fix(g, i)
                return carry

            lax.fori_loop(0, seg // SCAN_UNROLL, unrolled(fix_step, 0), 0)
        else:
            def adj_step(i, acc):
                acc_r, acc_i = acc
                for g in range(nch):
                    lr, li = fix(g, i)
                    prev = tile(g, seg - 2 - i)
                    fr, fi = sf_ref[prev, :w].astype(F32), sf_ref[prev, w:].astype(F32)
                    acc_r, acc_i = acc_r + lr * fr + li * fi, acc_i + li * fr - lr * fi
                return acc_r, acc_i

            acc = lax.fori_loop(0, seg // SCAN_UNROLL - 1, unrolled(adj_step, 0), (zero, zero))
            for i in range(seg - SCAN_UNROLL, seg - 1):
                acc = adj_step(i, acc)
            acc_r, acc_i = acc
            first_block = tt(t_blk) == 0
            for g in range(nch):
                lr, li = fix(g, seg - 1)
                seg_ends = tile(g, seg - 1)
                if g == 0:
                    pvr = jnp.where(first_block, 0.0, sp_ref[SUBLANES - 1:SUBLANES, :w].astype(F32))
                    pvi = jnp.where(first_block, 0.0, sp_ref[SUBLANES - 1:SUBLANES, w:].astype(F32))
                else:
                    pvr = sf_ref[g * chain_rows - 1:g * chain_rows, :w].astype(F32)
                    pvi = sf_ref[g * chain_rows - 1:g * chain_rows, w:].astype(F32)
                fr = jnp.where(sub == 0, pvr, pltpu.roll(sf_ref[seg_ends, :w].astype(F32), 1, 0))
                fi = jnp.where(sub == 0, pvi, pltpu.roll(sf_ref[seg_ends, w:].astype(F32), 1, 0))
                acc_r = acc_r + lr * fr + li * fi
                acc_i = acc_i + li * fr - lr * fi
            da_ref[0] += jnp.sum(acc_r, axis=0, keepdims=True)
            da_ref[1] += jnp.sum(acc_i, axis=0, keepdims=True)
            dw_ref[...] += _tn_dot(u_ref[...], s_ref[...])
            dx_ref[...] += _tn_dot(xb, sf_ref[...])

    x_spec = pl.BlockSpec((tb, bd_w), lambda j, t: (tt(t), j // tiles_per_bd))
    w_spec = pl.BlockSpec((bd_w, w), lambda j, t: (j // tiles_per_bd, j))
    d_spec = pl.BlockSpec((bd_w, 2 * w), lambda j, t: (j // tiles_per_bd, j % tiles_per_bd))
    a_spec = pl.BlockSpec((2, 1, w), lambda j, t: (0, 0, j))
    s_spec = pl.BlockSpec((tb, 2 * w), lambda j, t: (tt(t), j))
    in_specs, args = [x_spec, w_spec, w_spec, a_spec], [x, w_re, w_im, a_pair]
    out_specs, out_shape = [s_spec], [jax.ShapeDtypeStruct((l, 2 * SSM_S), BF16)]
    scratch = [pltpu.VMEM((2, seg, w), F32), pltpu.VMEM((2, 1, w), F32)] + [pltpu.VMEM((tb, 2 * w), F32)] * 2
    if with_da:
        in_specs += [s_spec, pl.BlockSpec((SUBLANES, 2 * w),
                                          lambda j, t: (jnp.maximum(tt(t) * (tb // SUBLANES) - 1, 0), j)),
                     x_spec]
        args += [s_fwd, s_fwd, u]
        out_specs += [a_spec, d_spec, d_spec]
        out_shape += ([jax.ShapeDtypeStruct((2, 1, SSM_S), F32)]
                      + [jax.ShapeDtypeStruct((SSM_W, 2 * SSM_S // SSM_BD), F32)] * 2)
    res = pl.pallas_call(
        body, grid=(SSM_S // w, nt), in_specs=in_specs, out_specs=out_specs, out_shape=out_shape,
        scratch_shapes=scratch, compiler_params=_params(("parallel", "arbitrary")), name=name,
    )(*args)
    return res if with_da else res[0]


def _nt_dot(x, y):
    return lax.dot_general(x.astype(_MXU), y.astype(_MXU), (((1,), (1,)), ((), ())), preferred_element_type=F32)


def _tn_dot(x, y):
    return lax.dot_general(x.astype(_MXU), y.astype(_MXU), (((0,), (0,)), ((), ())), preferred_element_type=F32)


def _nn_dot(x, y):
    return jnp.dot(x.astype(_MXU), y.astype(_MXU), preferred_element_type=F32)


def _attn_mask2(gb, nb):
    qi = lax.broadcasted_iota(jnp.int32, (ATT_WIN, 2 * ATT_WIN), 0)
    c = lax.broadcasted_iota(jnp.int32, (ATT_WIN, 2 * ATT_WIN), 1)
    has_prev = (gb % nb) != 0
    prev_ok = jnp.logical_and(jnp.logical_and(c < ATT_WIN, c >= qi), has_prev)
    own_ok = jnp.logical_and(c >= ATT_WIN, c - ATT_WIN <= qi)
    return jnp.logical_or(prev_ok, own_ok)


def _attn_specs(qb):
    cur = pl.BlockSpec((qb * ATT_WIN, ATT_GW), lambda i: (i, 0))
    prev = pl.BlockSpec((ATT_WIN, ATT_GW), lambda i: (jnp.maximum(qb * i - 1, 0), 0))
    return cur, prev


def _attn_fwd(q, k, v, nb, name):
    l = q.shape[0]
    scale = ATT_E ** -0.5
    w = ATT_WIN

    qb = ATT_QB_FWD

    def body(q_ref, kc_ref, kp_ref, vc_ref, vp_ref, o_ref, lse_ref):
        i = pl.program_id(0)
        masks = [_attn_mask2(qb * i + b, nb) for b in range(qb)]
        for h in range(ATT_HG):
            sl = slice(h * ATT_E, (h + 1) * ATT_E)
            k_ext = jnp.concatenate([kp_ref[:, sl], kc_ref[:, sl]], axis=0)
            v_ext = jnp.concatenate([vp_ref[:, sl], vc_ref[:, sl]], axis=0)
            for b in range(qb):
                r, kr = slice(b * w, (b + 1) * w), slice(b * w, (b + 2) * w)
                s = jnp.where(masks[b], _nt_dot(q_ref[r, sl], k_ext[kr]) * scale, NEG_INF)
                mx = jnp.max(s, axis=-1, keepdims=True)
                p = jnp.exp(s - mx)
                den = jnp.sum(p, axis=-1, keepdims=True)
                o_ref[r, sl] = _nn_dot(p, v_ext[kr]) / den
                lse_ref[r, sl] = jnp.broadcast_to(mx + jnp.log(den), (w, ATT_E))

    cur, prev = _attn_specs(qb)
    return pl.pallas_call(
        body, grid=(l // (qb * w),), in_specs=[cur, cur, prev, cur, prev], out_specs=[cur, cur],
        out_shape=[jax.ShapeDtypeStruct((l, ATT_GW), F32)] * 2,
        compiler_params=_params(("parallel",)), name=name,
    )(q, k, k, v, v)


def _attn_bwd(q, k, v, do, lse, dd, nb, name):
    l = q.shape[0]
    scale = ATT_E ** -0.5
    w = ATT_WIN
    nblk = l // w

    def body(q_ref, kc_ref, kp_ref, vc_ref, vp_ref, do_ref, lse_ref, dd_ref, qn_ref, don_ref, lsen_ref, ddn_ref,
             dq_ref, dk_ref, dv_ref, dk_acc, dv_acc):
        i = pl.program_id(0)
        masks = [_attn_mask2(ATT_QB * i + b, nb) for b in range(ATT_QB)]
        nxt = ATT_QB * (i + 1)
        nxt_attends = jnp.logical_and(nxt < nblk, (nxt % nb) != 0)
        qi = lax.broadcasted_iota(jnp.int32, (w, w), 0)
        kj = lax.broadcasted_iota(jnp.int32, (w, w), 1)
        mask_n = jnp.logical_and(kj >= qi, nxt_attends)
        dk_acc[...] = jnp.zeros_like(dk_acc)
        dv_acc[...] = jnp.zeros_like(dv_acc)
        for h in range(ATT_HG):
            sl, col = slice(h * ATT_E, (h + 1) * ATT_E), slice(h * ATT_E, h * ATT_E + 1)
            k_ext = jnp.concatenate([kp_ref[:, sl], kc_ref[:, sl]], axis=0)
            v_ext = jnp.concatenate([vp_ref[:, sl], vc_ref[:, sl]], axis=0)
            for b in range(ATT_QB):
                r, kr = slice(b * w, (b + 1) * w), slice(b * w, (b + 2) * w)
                qh, doh, k2, v2 = q_ref[r, sl], do_ref[r, sl], k_ext[kr], v_ext[kr]
                p = jnp.where(masks[b], jnp.exp(_nt_dot(qh, k2) * scale - lse_ref[r, col]), 0.0)
                ds = p * (_nt_dot(doh, v2) - dd_ref[r, col]) * scale
                dq_ref[r, sl] = _nn_dot(ds, k2).astype(dq_ref.dtype)
                dk2, dv2 = _tn_dot(ds, qh), _tn_dot(p, doh)
                dk_acc[r, sl] += dk2[w:]
                dv_acc[r, sl] += dv2[w:]
                if b > 0:
                    rp = slice((b - 1) * w, b * w)
                    dk_acc[rp, sl] += dk2[:w]
                    dv_acc[rp, sl] += dv2[:w]
            last = slice((ATT_QB - 1) * w, ATT_QB * w)
            kl, vl, qn, don = kc_ref[last, sl], vc_ref[last, sl], qn_ref[:, sl], don_ref[:, sl]
            pn = jnp.where(mask_n, jnp.exp(_nt_dot(qn, kl) * scale - lsen_ref[:, col]), 0.0)
            dsn = pn * (_nt_dot(don, vl) - ddn_ref[:, col]) * scale
            dk_acc[last, sl] += _tn_dot(dsn, qn)
            dv_acc[last, sl] += _tn_dot(pn, don)
        dk_ref[...] = dk_acc[...].astype(dk_ref.dtype)
        dv_ref[...] = dv_acc[...].astype(dv_ref.dtype)

    cur, prev = _attn_specs(ATT_QB)
    nxt_spec = pl.BlockSpec((w, ATT_GW), lambda i: (jnp.minimum(ATT_QB * (i + 1), nblk - 1), 0))
    return pl.pallas_call(
        body, grid=(l // (ATT_QB * w),),
        in_specs=[cur, cur, prev, cur, prev, cur, cur, cur, nxt_spec, nxt_spec, nxt_spec, nxt_spec],
        out_specs=[cur] * 3, out_shape=[jax.ShapeDtypeStruct((l, ATT_GW), BF16)] * 3,
        scratch_shapes=[pltpu.VMEM((ATT_QB * w, ATT_GW), F32)] * 2,
        compiler_params=_params(("parallel",)), name=name,
    )(q, k, k, v, v, do, lse, dd, q, do, lse, dd)


def _to_perm(a, d):
    if d == 1:
        return a
    l, c = a.shape
    return a.reshape(l // d, d, c).transpose(1, 0, 2).reshape(l, c)


def _from_perm(a, d):
    if d == 1:
        return a
    l, c = a.shape
    return a.reshape(d, l // d, c).transpose(1, 0, 2).reshape(l, c)


def _mem_probs(qh, kh):
    s = _nt_dot(qh, kh) * (MEM_E ** -0.5)
    e = jnp.exp(s - jnp.max(s, axis=-1, keepdims=True))
    return e / jnp.sum(e, axis=-1, keepdims=True)


def _mem_fwd(mq, kv, name, tm=1024):
    l, nm = mq.shape[0], kv.shape[0]

    def body(q_ref, kv_ref, o_ref):
        for h in range(MEM_H):
            sl = slice(h * MEM_E, (h + 1) * MEM_E)
            p = _mem_probs(q_ref[:, sl], kv_ref[:, sl])
            o_ref[:, sl] = _nn_dot(p, kv_ref[:, MEM_W + h * MEM_E:MEM_W + (h + 1) * MEM_E]).astype(o_ref.dtype)

    return pl.pallas_call(
        body, grid=(l // tm,),
        in_specs=[pl.BlockSpec((tm, MEM_W), lambda i: (i, 0)), pl.BlockSpec((nm, 2 * MEM_W), lambda i: (0, 0))],
        out_specs=pl.BlockSpec((tm, MEM_W), lambda i: (i, 0)),
        out_shape=jax.ShapeDtypeStruct((l, MEM_W), BF16),
        compiler_params=_params(("parallel",)), name=name,
    )(mq, kv)


def _mem_bwd(mq, kv, dmo, name, tm=1024):
    l, nm = mq.shape[0], kv.shape[0]
    scale = MEM_E ** -0.5

    def body(q_ref, kv_ref, do_ref, dq_ref, dkv_ref):
        @pl.when(pl.program_id(0) == 0)
        def _():
            dkv_ref[...] = jnp.zeros_like(dkv_ref)

        for h in range(MEM_H):
            sl = slice(h * MEM_E, (h + 1) * MEM_E)
            vsl = slice(MEM_W + h * MEM_E, MEM_W + (h + 1) * MEM_E)
            qh, kh, vh, doh = q_ref[:, sl], kv_ref[:, sl], kv_ref[:, vsl], do_ref[:, sl]
            p = _mem_probs(qh, kh)
            dp = _nt_dot(doh, vh)
            ds = p * (dp - jnp.sum(dp * p, axis=-1, keepdims=True)) * scale
            dq_ref[:, sl] = _nn_dot(ds, kh).astype(dq_ref.dtype)
            dkv_ref[:, sl] += _tn_dot(ds, qh)
            dkv_ref[:, vsl] += _tn_dot(p, doh)

    row = pl.BlockSpec((tm, MEM_W), lambda i: (i, 0))
    full = pl.BlockSpec((nm, 2 * MEM_W), lambda i: (0, 0))
    return pl.pallas_call(
        body, grid=(l // tm,), in_specs=[row, full, row], out_specs=[row, full],
        out_shape=[jax.ShapeDtypeStruct((l, MEM_W), BF16), jax.ShapeDtypeStruct((nm, 2 * MEM_W), F32)],
        compiler_params=_params(("arbitrary",)), name=name,
    )(mq, kv, dmo)


def _gated_out_proj(zg, branches, b_gate, w_o, x, g2, name, tm=512):
    l, d = x.shape
    nbr = len(branches)

    def body(zg_ref, *rest):
        br_refs, (bg_ref, w_ref, x_ref, g2_ref, m_ref, h_ref, n_ref) = rest[:nbr], rest[nbr:]
        merged = jnp.zeros((tm, d), F32)
        for i, br_ref in enumerate(br_refs):
            cols = slice(i * d, (i + 1) * d)
            merged += _sigmoid(zg_ref[:, cols].astype(F32) + bg_ref[:, cols]) * br_ref[...].astype(F32)
        mb = merged.astype(BF16)
        m_ref[...] = mb
        hv = jnp.dot(mb.astype(_MXU), w_ref[...].astype(_MXU), preferred_element_type=F32) + x_ref[...]
        h_ref[...] = hv
        rs = lax.rsqrt(jnp.mean(hv * hv, axis=-1, keepdims=True) + RMS_EPS)
        n_ref[...] = (hv * rs * g2_ref[...]).astype(n_ref.dtype)

    row = lambda c: pl.BlockSpec((tm, c), lambda i: (i, 0))
    full = lambda a: pl.BlockSpec(a.shape, lambda i: (0, 0))
    return pl.pallas_call(
        body, grid=(l // tm,),
        in_specs=[row(nbr * d)] + [row(d)] * nbr + [full(b_gate), full(w_o), row(d), full(g2)],
        out_specs=[row(d)] * 3,
        out_shape=[jax.ShapeDtypeStruct((l, d), BF16), jax.ShapeDtypeStruct((l, d), F32),
                   jax.ShapeDtypeStruct((l, d), BF16)],
        compiler_params=_params(("parallel",)), name=name,
    )(zg, *branches, b_gate, w_o, x, g2)


def _discretize(lam_re, lam_im, log_dt, b_re, b_im):
    dt = jnp.exp(log_dt)[:, None]
    mag = jnp.exp(lam_re * dt)
    a_re, a_im = mag * jnp.cos(lam_im * dt), mag * jnp.sin(lam_im * dt)
    nr, ni = a_re - 1.0, a_im
    den = lam_re * lam_re + lam_im * lam_im
    coef_re = (nr * lam_re + ni * lam_im) / den
    coef_im = (ni * lam_re - nr * lam_im) / den
    bb_re = coef_re[..., None] * b_re - coef_im[..., None] * b_im
    bb_im = coef_re[..., None] * b_im + coef_im[..., None] * b_re
    return a_re, a_im, bb_re, bb_im


def _tiled(re, im):
    r = re.shape[0]
    both = jnp.concatenate([re.reshape(r, -1, SCAN_W), im.reshape(r, -1, SCAN_W)], axis=2)
    return both.reshape(r, 2 * re.shape[1])


def _untiled(x):
    r = x.shape[0]
    t = x.reshape(r, -1, 2 * SCAN_W)
    return t[:, :, :SCAN_W].reshape(r, -1), t[:, :, SCAN_W:].reshape(r, -1)


def _bd_in(bb):
    return jnp.einsum("gph,gk->ghkp", bb, jnp.eye(SSM_G, dtype=bb.dtype)).reshape(SSM_W, SSM_S)


def _bd_diag(x):
    gb = SSM_G // SSM_BD
    t = x.reshape(SSM_BD, gb, SSM_H, gb, SSM_P)
    return jnp.einsum("bghgp->bghp", t).reshape(SSM_G, SSM_H, SSM_P)


_ANY = pl.BlockSpec(memory_space=pl.ANY)
_MESH = pl.DeviceIdType.MESH


def _allgather(x, name):
    def body(x_ref, out_ref, send_sems, recv_sems, local_sem):
        mx, my, mc = lax.axis_index("x"), lax.axis_index("y"), lax.axis_index("c")
        me, sibling = (mx, my, mc), (mx, my, 1 - mc)
        chips = [(1 - mx, my), (mx, 1 - my), (1 - mx, 1 - my)]

        def blk(px, py, pc):
            return out_ref.at[4 * px + 2 * py + pc]

        def copy(k, block, to, src=None):
            return pltpu.make_async_remote_copy(
                src_ref=blk(*block) if src is None else src, dst_ref=blk(*block),
                send_sem=send_sems.at[k], recv_sem=recv_sems.at[k], device_id=to, device_id_type=_MESH)

        mine = pltpu.make_async_copy(x_ref, blk(*me), local_sem)
        mine.start()
        first = [copy(0, me, sibling, src=x_ref)]
        first += [copy(1 + j, me, (*chip, mc), src=x_ref) for j, chip in enumerate(chips)]
        for cp in first:
            cp.start()
        passed = [copy(4 + j, (*chip, mc), sibling) for j, chip in enumerate(chips)]
        for j, chip in enumerate(chips):
            copy(1 + j, (*chip, mc), me).wait_recv()
            passed[j].start()
        copy(0, sibling, me).wait_recv()
        for j, chip in enumerate(chips):
            copy(4 + j, (*chip, 1 - mc), me).wait_recv()
        for cp in first + passed:
            cp.wait_send()
        mine.wait()

    return pl.pallas_call(
        body, out_shape=jax.ShapeDtypeStruct((N_DEV,) + x.shape, x.dtype), in_specs=[_ANY], out_specs=_ANY,
        scratch_shapes=[pltpu.SemaphoreType.DMA((7,)), pltpu.SemaphoreType.DMA((7,)), pltpu.SemaphoreType.DMA],
        name=name,
    )(x)


_HBM = pl.BlockSpec(memory_space=pltpu.HBM)
_SEM = pl.BlockSpec(memory_space=pltpu.SEMAPHORE)
_EFFECT = pltpu.SideEffectType.DATAFLOW_SIDE_EFFECTING
_TOKEN = jax.ShapeDtypeStruct((8, 128), F32)


def _peer(rel):
    pos = (lax.axis_index("x"), lax.axis_index("y"), lax.axis_index("c"))
    return tuple(1 - p if (rel >> (2 - i)) & 1 else p for i, p in enumerate(pos))


def _index_of(dev):
    return 4 * dev[0] + 2 * dev[1] + dev[2]


def _split_copies(src_ref, land_ref, sems, plan):
    n = len(plan)
    return [pltpu.make_async_remote_copy(
        src_ref=src_ref if s is None else src_ref.at[s], dst_ref=land_ref.at[d], send_sem=sems[k],
        recv_sem=sems[n + k], device_id=peer, device_id_type=_MESH) for k, (s, d, peer) in enumerate(plan)]


def _split_start(src, n_land, plan_fn, after, name):
    blk = src.shape[-2:]
    land = lax.empty((n_land,) + blk, src.dtype)
    n = len(plan_fn())

    def body(src_ref, land_ref, after_ref, *outs):
        for cp in _split_copies(src_ref, land_ref, outs[:2 * n], plan_fn()):
            cp.start()
        outs[2 * n + 2][...] = jnp.zeros_like(outs[2 * n + 2])

    res = pl.pallas_call(
        body, name=name,
        out_shape=(pltpu.SemaphoreType.DMA(()),) * (2 * n)
        + (pltpu.HBM(src.shape, src.dtype), pltpu.HBM(land.shape, land.dtype), _TOKEN),
        in_specs=(_HBM, _HBM, _ANY),
        out_specs=(_SEM,) * (2 * n) + (_HBM, _HBM, pl.BlockSpec(memory_space=pltpu.VMEM)),
        input_output_aliases={0: 2 * n, 1: 2 * n + 1},
        compiler_params=pltpu.CompilerParams(has_side_effects=_EFFECT),
    )(pltpu.with_memory_space_constraint(src, pltpu.HBM), pltpu.with_memory_space_constraint(land, pltpu.HBM), after)
    return res[:2 * n], res[2 * n], res[2 * n + 1], res[2 * n + 2]


def _split_wait(sems, src, land, plan_fn, after, name):
    n = len(sems) // 2

    def body(src_ref, land_ref, *rest):
        for cp in _split_copies(src_ref, land_ref, rest[:2 * n], plan_fn()):
            cp.wait_send()
            cp.wait_recv()

    return pl.pallas_call(
        body, name=name,
        out_shape=(pltpu.HBM(src.shape, src.dtype), pltpu.HBM(land.shape, land.dtype)),
        in_specs=(_HBM, _HBM) + (_SEM,) * (2 * n) + (_ANY,), out_specs=(_HBM, _HBM),
        input_output_aliases={0: 0, 1: 1},
        compiler_params=pltpu.CompilerParams(has_side_effects=_EFFECT),
    )(src, land, *sems, after)


def _gather_plan():
    me = _index_of(_peer(0))
    return [(None, me, _peer(rel)) for rel in range(1, N_DEV)]


def _gather_wait_plan():
    return [(None, _index_of(_peer(rel)), _peer(rel)) for rel in range(1, N_DEV)]


def _owner_plan():
    return [(_index_of(_peer(rel)), rel - 1, _peer(rel)) for rel in range(1, N_DEV)]


def _adam_math(g, w, m, v):
    m = ADAM_B1 * m + (1.0 - ADAM_B1) * g
    v = ADAM_B2 * v + (1.0 - ADAM_B2) * (g * g)
    m_hat = m / (1.0 - ADAM_B1 ** ADAM_STEP)
    v_hat = v / (1.0 - ADAM_B2 ** ADAM_STEP)
    delta = -ADAM_LR * (m_hat / (jnp.sqrt(v_hat) + ADAM_EPS) + ADAM_WD * w)
    return delta, m, v


def _grad_sum(own, own_index, recv, name, tr):
    _, r, c = own.shape
    n = recv.shape[0]

    def body(k_ref, own_ref, *rest):
        g = own_ref[...]
        for recv_ref in rest[:n]:
            g = g + recv_ref[...].astype(F32)
        rest[n][...] = g

    def slot(j):
        return pl.BlockSpec((None, tr, c), lambda i, kr: (j, i, 0))

    return pl.pallas_call(
        body,
        grid_spec=pltpu.PrefetchScalarGridSpec(
            num_scalar_prefetch=1, grid=(r // tr,),
            in_specs=[pl.BlockSpec((None, tr, c), lambda i, kr: (kr[0], i, 0))] + [slot(j) for j in range(n)],
            out_specs=pl.BlockSpec((tr, c), lambda i, kr: (i, 0))),
        out_shape=jax.ShapeDtypeStruct((r, c), F32),
        compiler_params=_params(("parallel",)), name=name,
    )(own_index, own, *([recv] * n))


def _adam_many(g, w, m, v, row_tiles, name):
    n = len(g)

    def body(*refs):
        ins, outs = refs[:4 * n], refs[4 * n:]
        for i in range(n):
            res = _adam_math(ins[i][...], ins[n + i][...], ins[2 * n + i][...], ins[3 * n + i][...])
            for kind in range(3):
                outs[kind * n + i][...] = res[kind]

    def spec(a):
        blk = (a.shape[0] // row_tiles,) + a.shape[1:]
        return pl.BlockSpec(blk, lambda t, nd=a.ndim: (t,) + (0,) * (nd - 1))

    specs = [spec(a) for a in g]
    res = pl.pallas_call(
        body, grid=(row_tiles,), in_specs=specs * 4, out_specs=specs * 3,
        out_shape=[jax.ShapeDtypeStruct(a.shape, F32) for a in g] * 3,
        compiler_params=_params(("parallel",)), name=name,
    )(*g, *w, *m, *v)
    return res[:n], res[n:2 * n], res[2 * n:]


def _sum8(g8, name):
    _, r, c = g8.shape

    def body(g_ref, o_ref):
        acc = g_ref[0]
        for j in range(1, N_DEV):
            acc = acc + g_ref[j]
        o_ref[...] = acc

    return pl.pallas_call(
        body, grid=(1,), in_specs=[pl.BlockSpec((N_DEV, r, c), lambda i: (0, 0, 0))],
        out_specs=pl.BlockSpec((r, c), lambda i: (0, 0)), out_shape=jax.ShapeDtypeStruct((r, c), F32),
        compiler_params=_params(("arbitrary",)), name=name,
    )(g8)


def _pack(arrs, pad_rows=8):
    flat = jnp.concatenate([a.reshape(-1) for a in arrs])
    n = flat.shape[0]
    q = PACK_C * pad_rows
    tot = -(-n // q) * q
    if tot != n:
        flat = jnp.concatenate([flat, jnp.zeros((tot - n,), flat.dtype)])
    return flat.reshape(tot // PACK_C, PACK_C)


def _unpack(buf, shapes):
    flat = buf.reshape(-1)
    out, off = [], 0
    for s in shapes:
        n = int(np.prod(s))
        out.append(flat[off:off + n].reshape(s))
        off += n
    return out


GROUPS = (("w_in",),
          ("w_glu", "w_ssm_br", "w_mem_br", "w_attn_br"),
          ("w_up", "w_down"),
          ("w_mem_kv", "w_o"))
GROUP_TR = (400, 384, 512, 256)
MLP_GROUP = 2
MIXER_GROUPS = (1, 3)
ATTN_BR_FOLD = 2


def _stored_shape(name):
    r, c, ax = BIG_SHAPE[name]
    rows, cols = (r // N_DEV, c) if ax == 0 else (c // N_DEV, r)
    return (rows // ATTN_BR_FOLD, cols * ATTN_BR_FOLD) if name == "w_attn_br" else (rows, cols)


def _stored(shard, name):
    a = shard[0].T if BIG_SHAPE[name][2] == 1 else shard[0]
    return a.reshape(_stored_shape(name))


def _unstored(a, name):
    r, c, ax = BIG_SHAPE[name]
    if ax == 0:
        return a.reshape(1, r // N_DEV, c)
    return a.reshape(c // N_DEV, r).T[None]


def _pack_group(d, names):
    return jnp.concatenate([_stored(d[n], n) for n in names], axis=0)


def _split_group(buf, names):
    out, off = {}, 0
    for n in names:
        rows = _stored_shape(n)[0]
        out[n] = buf[..., off:off + rows, :]
        off += rows
    return out


def _full_stored(stacked, name):
    r, c, ax = BIG_SHAPE[name]
    return stacked.reshape((r, c) if ax == 0 else (c, r))


def _stacked_stored(full, name):
    return full.reshape((N_DEV,) + _stored_shape(name))


def _gelu_parts(x):
    c0, c1 = math.sqrt(2.0 / math.pi), 0.044715
    th = jnp.tanh(c0 * (x + c1 * x * x * x))
    return th, c0, c1


def _local_step(x, mem, tgt, wb, sp, late_weights, grads_ready, small_grads_ready):
    l = x.shape[0]
    w_a, w_g = wb["w_in"][:ZA_W], wb["w_in"][ZA_W:]

    a_re, a_im, bb_re, bb_im = _discretize(sp["ssm_lambda_re"], sp["ssm_lambda_im"], sp["ssm_log_dt"],
                                           sp["ssm_b_re"], sp["ssm_b_im"])
    a_pair = jnp.stack([a_re.reshape(1, SSM_S), a_im.reshape(1, SSM_S)])
    a_conj = jnp.stack([a_re.reshape(1, SSM_S), -a_im.reshape(1, SSM_S)])
    b_re_t, b_im_t = _bd_in(bb_re).astype(BF16), _bd_in(bb_im).astype(BF16)
    c_re_t = _bd_in(sp["ssm_c_re"].transpose(0, 2, 1)).astype(BF16)
    c_im_t = (-_bd_in(sp["ssm_c_im"].transpose(0, 2, 1))).astype(BF16)
    d_row = sp["ssm_d"].reshape(1, SSM_W)

    n1 = _rms_fwd(x, sp["norm1_g"], "rms1")
    za = _mm(n1, w_a, [BF16], tb=True, name="in_proj_a", tn=1664)
    zg = _mm(n1, w_g, [BF16], tb=True, name="in_proj_g")
    for gi in MIXER_GROUPS:
        wb = {**wb, **late_weights(gi, za)}
    u = za[:, :SSM_W]
    mq = za[:, ZA_W - MEM_W:]

    u_s = _scan_order(u)
    s_all = _ssm_scan(u_s, b_re_t, b_im_t, a_pair, reverse=False, name="ssm_scan_fwd")
    def gelu_epi(acc, ut, dr):
        y = acc + dr * ut.astype(F32)
        th, _, _ = _gelu_parts(y)
        return y, 0.5 * y * (1.0 + th)
    y0, y1 = [_time_order(t) for t in _mm(s_all, _tiled(c_re_t, c_im_t), [F32, BF16], tb=True, epi=gelu_epi,
                                          mn=[u_s], rows=[d_row], bd=SSM_BD, tm=2048, name="ssm_cs")]

    def glu_epi(acc, y1t, bg):
        t = acc + bg
        return t, y1t.astype(F32) * _sigmoid(t)
    t_glu, y2 = _mm(y1, wb["w_glu"], [F32, BF16], epi=glu_epi, mn=[y1], rows=[sp["b_glu"]], name="ssm_glu")
    br_ssm = _mm(y2, wb["w_ssm_br"], [BF16], tb=True, name="ssm_br")

    qkv_p, o_g, lse_g = [], [], []
    for g, d in enumerate(DILATIONS):
        nb = l // d // ATT_WIN
        cols = [za[:, SSM_W + (3 * j + g) * ATT_GW: SSM_W + (3 * j + g + 1) * ATT_GW] for j in range(3)]
        qp, kp, vp = [_to_perm(cc, d) for cc in cols]
        qkv_p.append((qp, kp, vp))
        og, lg = _attn_fwd(qp, kp, vp, nb, "attn_fwd%d" % g)
        o_g.append(_from_perm(og, d))
        lse_g.append(_from_perm(lg, d))

    def merge_fn(r, b):
        o0, o1, o2, l0, l1, l2 = r
        mx = jnp.maximum(jnp.maximum(l0, l1), l2)
        e0, e1, e2 = jnp.exp(l0 - mx), jnp.exp(l1 - mx), jnp.exp(l2 - mx)
        tot = e0 + e1 + e2
        return [(e0 * o0 + e1 * o1 + e2 * o2) / tot, mx + jnp.log(tot)], []
    o_att, lse_tot = _ew(merge_fn, o_g + lse_g, [], [(ATT_GW, F32), (ATT_GW, F32)], [], name="attn_merge", tm=2048)
    br_attn = _mm(o_att, wb["w_attn_br"], [BF16], tb=True, name="attn_br")

    mn = _rms_fwd(mem, sp["mem_norm_g"], "rms_mem")
    kv = _mm(mn, wb["w_mem_kv"], [BF16], name="mem_kv")
    mo = _mem_fwd(mq, kv, "mem_attn_fwd")
    br_mem = _mm(mo, wb["w_mem_br"], [BF16], tb=True, name="mem_br")

    merged, h1, n2 = _gated_out_proj(zg, [br_ssm, br_attn, br_mem], sp["b_gate"], wb["w_o"], x, sp["norm2_g"],
                                     "gated_o_proj")

    def up_epi(acc):
        ra = jnp.maximum(acc, 0.0)
        return ra * ra, ra
    wm = late_weights(MLP_GROUP, n2)
    f_act, r_act = _mm(n2, wm["w_up"], [BF16, BF16], tb=True, epi=up_epi, name="mlp_up")
    def down_epi(acc, ht, tv, gf):
        hv = acc + ht
        rs = lax.rsqrt(jnp.mean(hv * hv, axis=-1, keepdims=True) + RMS_EPS)
        err = hv * rs * gf - tv
        dh, dgf = _rms_bwd_tile(hv, err * (1.0 / D_MODEL), gf)
        return dh, dgf, _colsum(err * err) * (0.5 / D_MODEL)
    dh2, d_final_g, loss_cols = _mm(f_act, wm["w_down"], [F32], epi=down_epi, mn=[h1, tgt], rows=[sp["final_g"]],
                                    n_sums=2, tk=1024, name="mlp_down")
    loss = jnp.sum(loss_cols, axis=1, keepdims=True)

    gw, gs = {}, {"final_g": d_final_g}
    d_act = _mm(dh2, wm["w_down"], [BF16], tb=True, epi=lambda acc, ra: (acc * 2.0 * ra.astype(F32),), mn=[r_act],
                name="mlp_down_dx")
    dw_down = _mm(f_act, dh2, [F32], ta=True, name="mlp_down_dw")
    dw_up = _mm(d_act, n2, [F32], ta=True, name="mlp_up_dw")
    token = grads_ready(MLP_GROUP, {"w_up": dw_up, "w_down": dw_down})
    def up_dx_epi(acc, ht, dht, g2):
        dx, dg = _rms_bwd_tile(ht, acc, g2)
        return dx + dht, dg
    dh1, gs["norm2_g"] = _mm(d_act, wm["w_up"], [F32], epi=up_dx_epi, mn=[h1, dh2],
                             rows=[sp["norm2_g"] + token[:1, :1]], n_sums=1, tk=1024, name="mlp_up_dx")
    gw["w_o"] = _mm(merged, dh1, [F32], ta=True, name="o_proj_dw")

    def gate_bwd_epi(dm, *tiles):
        dbr, dz = [], []
        for zt, bt, bias in zip(tiles[0:3], tiles[3:6], tiles[6:9]):
            gt = _sigmoid(zt.astype(F32) + bias)
            dbr.append(dm * gt)
            dz.append(dm * bt.astype(F32) * gt * (1.0 - gt))
        return (*dbr, *dz, *[_colsum(t) for t in dz])
    gate_bias = [sp["b_gate"][:, i * D_MODEL:(i + 1) * D_MODEL] for i in range(3)]
    res = _mm(dh1, wb["w_o"], [BF16] * 6, tb=True, epi=gate_bwd_epi, mn=[(zg, 0), (zg, 1), (zg, 2), br_ssm, br_attn, br_mem],
              rows=gate_bias, n_sums=3, tm=512, name="o_proj_dx")
    (dbr_ssm, dbr_attn, dbr_mem), dzg = res[0:3], res[3:6]
    gs["b_gate"] = jnp.concatenate(res[6:9], axis=1)

    gw["w_ssm_br"] = _mm(dbr_ssm, y2, [F32], ta=True, name="ssm_br_dw")
    def glu_bwd_epi(dy, y1t, tt):
        sg = _sigmoid(tt)
        dt = dy * y1t.astype(F32) * sg * (1.0 - sg)
        return dt, dy * sg, _colsum(dt)
    dt_glu, dy1a, gs["b_glu"] = _mm(dbr_ssm, wb["w_ssm_br"], [BF16, F32], epi=glu_bwd_epi, mn=[y1, t_glu], n_sums=1,
                                    name="ssm_br_dx")
    gw["w_glu"] = _mm(y1, dt_glu, [F32], ta=True, name="ssm_glu_dw")

    def gelu_bwd_epi(acc, dy1t, y0t, ut):
        th, c0, c1 = _gelu_parts(y0t)
        dg = 0.5 * (1.0 + th) + 0.5 * y0t * (1.0 - th * th) * c0 * (1.0 + 3.0 * c1 * y0t * y0t)
        dy = (acc + dy1t) * dg
        return dy, _colsum(dy * ut.astype(F32))
    dy0, gs["ssm_d"] = _mm(dt_glu, wb["w_glu"], [F32], tb=True, epi=gelu_bwd_epi, mn=[dy1a, y0, u], n_sums=1,
                           name="ssm_glu_dx")
    dy0_s = _scan_order(dy0)
    lam, da, d_b, d_c = _ssm_scan(dy0_s, c_re_t, c_im_t, a_conj, reverse=True, s_fwd=s_all, u=u_s,
                                  name="ssm_scan_bwd")
    du = _time_order(_mm(lam, _tiled(b_re_t, b_im_t), [BF16], tb=True,
                         epi=lambda acc, dyt, dr: (acc + dyt * dr,), mn=[dy0_s], rows=[d_row], bd=SSM_BD, tm=2048, name="ssm_bu_dx"))
    gs["a_re"], gs["a_im"] = da[0], da[1]
    (dbr, dbi), (dcr, dci) = _untiled(d_b), _untiled(d_c)
    gs["bb_re"], gs["bb_im"] = _bd_diag(dbr).transpose(0, 2, 1), _bd_diag(dbi).transpose(0, 2, 1)
    gs["ssm_c_re"], gs["ssm_c_im"] = _bd_diag(dcr), -_bd_diag(dci)

    gw["w_attn_br"] = _mm(dbr_attn, o_att, [F32], ta=True, name="attn_br_dw")

    def do_epi(acc, ot):
        prod = acc * ot
        head = lax.broadcasted_iota(jnp.int32, prod.shape, 1) // ATT_E
        dd = jnp.zeros_like(prod)
        for h in range(ATT_HG):
            dd = jnp.where(head == h, jnp.sum(jnp.where(head == h, prod, 0.0), axis=1, keepdims=True), dd)
        return acc, dd
    do_att, dd_att = _mm(dbr_attn, wb["w_attn_br"], [BF16, F32], epi=do_epi, mn=[o_att], name="attn_br_dx")
    dq_l, dk_l, dv_l = [], [], []
    for g, d in enumerate(DILATIONS):
        nb = l // d // ATT_WIN
        qp, kp, vp = qkv_p[g]
        dq, dk, dv = _attn_bwd(qp, kp, vp, _to_perm(do_att, d), _to_perm(lse_tot, d), _to_perm(dd_att, d),
                               nb, "attn_bwd%d" % g)
        dq_l.append(_from_perm(dq, d))
        dk_l.append(_from_perm(dk, d))
        dv_l.append(_from_perm(dv, d))

    gw["w_mem_br"] = _mm(dbr_mem, mo, [F32], ta=True, name="mem_br_dw")
    dmo = _mm(dbr_mem, wb["w_mem_br"], [BF16], name="mem_br_dx")
    dmq, dkv = _mem_bwd(mq, kv, dmo, "mem_attn_bwd")
    gw["w_mem_kv"] = _mm(mn, dkv, [F32], ta=True, name="mem_kv_dw")
    dmn = _mm(dkv, wb["w_mem_kv"], [F32], tb=True, name="mem_kv_dx")
    token = sum(grads_ready(gi, gw) for gi in MIXER_GROUPS)
    gs["mem_norm_g"] = _rms_bwd(mem, dmn, None, sp["mem_norm_g"] + token[:1, :1], "rms_mem_bwd")[1]

    dza = jnp.concatenate([du] + dq_l + dk_l + dv_l + [dmq], axis=1)
    dw_a = _mm(dza, n1, [F32], ta=True, name="in_proj_a_dw", tm=1664)
    dw_g = [_mm(dzg[i], n1, [F32], ta=True, name="in_proj_g_dw%d" % i) for i in range(3)]
    gw["w_in"] = jnp.concatenate([dw_a] + dw_g, axis=0)
    token = (grads_ready(0, gw) + small_grads_ready(gs))[:1, :1]
    dn_a = _mm(dza, w_a + token.astype(w_a.dtype), [F32], name="in_proj_a_dx", tk=1664)
    def in_dx_epi(acc, pt, xt, dht, g1):
        dx, dg = _rms_bwd_tile(xt, acc + pt, g1)
        return dx + dht, dg
    w_gs = [w_g[i * D_MODEL:(i + 1) * D_MODEL] for i in range(3)]
    grad_x, gs["norm1_g"] = _mm(dzg[0], w_gs[0], [F32], pair2=(dzg[1], w_gs[1], dzg[2], w_gs[2]), epi=in_dx_epi,
                                mn=[dn_a, x, dh1],
                                rows=[sp["norm1_g"] + token], n_sums=1, tm=512, name="in_proj_g_dx")
    return loss, grad_x, gs


_SMALL_GRAD_ORDER = ("norm1_g", "mem_norm_g", "b_gate", "a_re", "a_im", "bb_re", "bb_im", "ssm_c_re", "ssm_c_im",
                     "ssm_d", "b_glu", "norm2_g", "final_g")


def kernel(x, mem, norm1_g, mem_norm_g, w_in, b_gate, ssm_lambda_re, ssm_lambda_im, ssm_log_dt, ssm_b_re, ssm_b_im, ssm_c_re, ssm_c_im, ssm_d, w_glu, b_glu, w_ssm_br, w_attn_br, w_mem_kv, w_mem_br, w_o, norm2_g, w_up, w_down, final_g, loss_target, m_norm1_g, m_mem_norm_g, m_w_in, m_b_gate, m_ssm_lambda_re, m_ssm_lambda_im, m_ssm_log_dt, m_ssm_b_re, m_ssm_b_im, m_ssm_c_re, m_ssm_c_im, m_ssm_d, m_w_glu, m_b_glu, m_w_ssm_br, m_w_attn_br, m_w_mem_kv, m_w_mem_br, m_w_o, m_norm2_g, m_w_up, m_w_down, m_final_g, v_norm1_g, v_mem_norm_g, v_w_in, v_b_gate, v_ssm_lambda_re, v_ssm_lambda_im, v_ssm_log_dt, v_ssm_b_re, v_ssm_b_im, v_ssm_c_re, v_ssm_c_im, v_ssm_d, v_w_glu, v_b_glu, v_w_ssm_br, v_w_attn_br, v_w_mem_kv, v_w_mem_br, v_w_o, v_norm2_g, v_w_up, v_w_down, v_final_g):
    args = dict(locals())
    w = {n: args[n] for n in ALL_W}
    m = {n: args["m_" + n] for n in ALL_W}
    v = {n: args["v_" + n] for n in ALL_W}

    w_pack = [_pack_group(w, names) for names in GROUPS]
    my_index = (4 * lax.axis_index("x") + 2 * lax.axis_index("y") + lax.axis_index("c")).astype(jnp.int32)
    zero = jnp.zeros((), jnp.int32)
    w_all = _allgather(w_pack[0].astype(BF16), "allgather_weights0")
    wb = {n: _full_stored(part, n) for n, part in _split_group(w_all, GROUPS[0]).items()}
    gathers = {gi: _split_start(w_pack[gi].astype(BF16), N_DEV, _gather_plan, w_all, "weights_gather_start%d" % gi)
               for gi in range(1, len(GROUPS))}

    def gathered(started, after, name):
        sems, src, land, _ = started
        src, land = _split_wait(sems, src, land, _gather_wait_plan, after, name)
        return lax.dynamic_update_slice(land, src[None], (my_index, zero, zero))

    def late_weights(gi, after):
        full = gathered(gathers[gi], after, "weights_gather_wait%d" % gi)
        return {n: _full_stored(part, n) for n, part in _split_group(full, GROUPS[gi]).items()}

    pending = {}

    def grads_ready(gi, grads):
        g_pack = jnp.concatenate([_stacked_stored(grads[n], n) for n in GROUPS[gi]], axis=1)
        started = _split_start(g_pack.astype(BF16), N_DEV - 1, _owner_plan, g_pack, "grad_exchange_start%d" % gi)
        pending[gi] = (g_pack, started)
        return started[3]

    early_small = [n for n in _SMALL_GRAD_ORDER if n != "norm1_g"]
    small_started = []

    def small_grads_ready(gs):
        started = _split_start(_pack([gs[n] for n in early_small]), N_DEV, _gather_plan, gs["mem_norm_g"],
                               "small_grads_gather_start")
        small_started.append((started, [gs[n].shape for n in early_small]))
        return started[3]

    sp = {
        "norm1_g": norm1_g + sum(started[3][:1, :1] for started in gathers.values()), "mem_norm_g": mem_norm_g, "b_gate": b_gate, "b_glu": b_glu, "norm2_g": norm2_g,
        "final_g": final_g.reshape(1, D_MODEL),
        "ssm_lambda_re": ssm_lambda_re[0], "ssm_lambda_im": ssm_lambda_im[0], "ssm_log_dt": ssm_log_dt[0],
        "ssm_b_re": ssm_b_re[0], "ssm_b_im": ssm_b_im[0], "ssm_c_re": ssm_c_re[0], "ssm_c_im": ssm_c_im[0],
        "ssm_d": ssm_d[0],
    }
    loss, grad_x, gs = _local_step(x[0], mem[0], loss_target[0], wb, sp, late_weights, grads_ready,
                                     small_grads_ready)
    loss = lax.psum(loss[0, 0], ("x", "y", "c"))
    n1_started = _split_start(_pack([gs["norm1_g"]]), N_DEV, _gather_plan, grad_x, "norm1_grad_gather_start")

    big_g = {}
    for gi, names in enumerate(GROUPS):
        own, (sems, src, land, _) = pending[gi]
        recv = _split_wait(sems, src, land, _owner_plan, grad_x, "grad_exchange_wait%d" % gi)[1]
        g_pack = _grad_sum(own, my_index.reshape(1), recv, "grad_sum%d" % gi, GROUP_TR[gi])
        for n, part in _split_group(g_pack, names).items():
            big_g[n] = _unstored(part, n)
    rows_of = lambda d, names: [d[n].reshape(d[n].shape[-2:]) for n in names]
    big_out = _adam_many(rows_of(big_g, BIG), rows_of(w, BIG), rows_of(m, BIG), rows_of(v, BIG), 8, "adam_big")
    big = [big_g] + [{n: a[None] for n, a in zip(BIG, outs)} for outs in big_out]

    (sg_started, sg_shapes), = small_started
    sg_all = jnp.concatenate([gathered(sg_started, big_out[0][0], "small_grads_gather_wait"),
                              gathered(n1_started, big_out[0][0], "norm1_grad_gather_wait")], axis=1)
    sg_sum = _sum8(sg_all, "sum_small_grads")
    n1_rows = n1_started[1].shape[0]
    sg = dict(zip(early_small, _unpack(sg_sum[:-n1_rows], sg_shapes)))
    sg["norm1_g"] = _unpack(sg_sum[-n1_rows:], [gs["norm1_g"].shape])[0]
    _, disc_vjp = jax.vjp(_discretize, sp["ssm_lambda_re"], sp["ssm_lambda_im"], sp["ssm_log_dt"],
                          sp["ssm_b_re"], sp["ssm_b_im"])
    d_lre, d_lim, d_ldt, d_bre, d_bim = disc_vjp((sg["a_re"].reshape(SSM_G, SSM_P), sg["a_im"].reshape(SSM_G, SSM_P),
                                                  sg["bb_re"], sg["bb_im"]))
    small_grad = {
        "norm1_g": sg["norm1_g"], "mem_norm_g": sg["mem_norm_g"], "b_gate": sg["b_gate"],
        "ssm_lambda_re": d_lre, "ssm_lambda_im": d_lim, "ssm_log_dt": d_ldt, "ssm_b_re": d_bre, "ssm_b_im": d_bim,
        "ssm_c_re": sg["ssm_c_re"], "ssm_c_im": sg["ssm_c_im"], "ssm_d": sg["ssm_d"], "b_glu": sg["b_glu"],
        "norm2_g": sg["norm2_g"], "final_g": sg["final_g"],
    }
    small_grad = {n: small_grad[n].reshape(w[n].shape) for n in SMALL}

    def squeezed(a):
        return a.reshape(a.shape[1:]) if a.ndim > 2 else a.reshape(1, -1)

    sq = lambda d: [squeezed(d[n]) for n in SMALL]
    small_out = _adam_many(sq(small_grad), sq(w), sq(m), sq(v), 1, "adam_small")
    small = [small_grad] + [{n: a.reshape(w[n].shape) for n, a in zip(SMALL, outs)} for outs in small_out]

    outs = [loss, grad_x[None]]
    for kind in range(4):
        for n in ALL_W:
            outs.append(big[kind][n] if n in BIG else small[kind][n])
    return tuple(outs)
```

```python
import math

import numpy as np
import jax
import jax.numpy as jnp
from jax import lax
from jax.experimental import pallas as pl
from jax.experimental.pallas import tpu as pltpu

F32 = jnp.float32
BF16 = jnp.bfloat16
_MXU = jnp.bfloat16

D_MODEL = 1024
SSM_G, SSM_H, SSM_P = 32, 16, 64
SSM_W = SSM_G * SSM_H
SSM_S = SSM_G * SSM_P
SSM_BD = 4
ATT_E = 64
ATT_HG = 4
ATT_GW = ATT_HG * ATT_E
ATT_WIN = 128
ATT_QB = 8
ATT_QB_FWD = 4
DILATIONS = (1, 4, 16)
MEM_H, MEM_E = 4, 128
MEM_W = MEM_H * MEM_E
ZA_W = SSM_W + 9 * ATT_GW + MEM_W
ZG_W = 3 * D_MODEL
IN_W = ZA_W + ZG_W
RMS_EPS = 1e-6
NEG_INF = -1e30

ADAM_LR, ADAM_B1, ADAM_B2, ADAM_EPS, ADAM_WD, ADAM_STEP = 0.001, 0.9, 0.999, 1e-08, 0.01, 10

N_DEV = 8
PACK_C = 512
_VMEM_LIMIT = 56 * 1024 * 1024
SUBLANES = 16
SCAN_SEG = 128
SCAN_CHAINS = 4
SCAN_UNROLL = 4
SCAN_W = 128

BIG = ("w_in", "w_glu", "w_ssm_br", "w_attn_br", "w_mem_kv", "w_mem_br", "w_o", "w_up", "w_down")
BIG_SHAPE = {
    "w_in": (D_MODEL, IN_W, 1), "w_glu": (SSM_W, SSM_W, 0), "w_ssm_br": (SSM_W, D_MODEL, 1),
    "w_attn_br": (ATT_GW, D_MODEL, 1), "w_mem_kv": (D_MODEL, 2 * MEM_W, 0), "w_mem_br": (MEM_W, D_MODEL, 1),
    "w_o": (D_MODEL, D_MODEL, 0), "w_up": (D_MODEL, 4 * D_MODEL, 1), "w_down": (4 * D_MODEL, D_MODEL, 0),
}
SMALL = ("norm1_g", "mem_norm_g", "b_gate", "ssm_lambda_re", "ssm_lambda_im", "ssm_log_dt", "ssm_b_re",
         "ssm_b_im", "ssm_c_re", "ssm_c_im", "ssm_d", "b_glu", "norm2_g", "final_g")
ALL_W = ("norm1_g", "mem_norm_g", "w_in", "b_gate", "ssm_lambda_re", "ssm_lambda_im", "ssm_log_dt", "ssm_b_re",
         "ssm_b_im", "ssm_c_re", "ssm_c_im", "ssm_d", "w_glu", "b_glu", "w_ssm_br", "w_attn_br", "w_mem_kv",
         "w_mem_br", "w_o", "norm2_g", "w_up", "w_down", "final_g")


def _params(sem):
    return pltpu.CompilerParams(dimension_semantics=sem, vmem_limit_bytes=_VMEM_LIMIT)


def _pick(n, cap):
    if n <= cap:
        return n
    t = (cap // 128) * 128
    while n % t:
        t -= 128
    return t


def _mm(a, b, outs, *, name, ta=False, tb=False, epi=None, mn=(), rows=(), pair2=None, bd=0, n_sums=0,
        tm=1024, tn=1024, tk=2048):
    ab = [a, b] + (list(pair2) if pair2 is not None else [])
    a_shape, b_shape = ab[0].shape, ab[1].shape
    m = a_shape[1] if ta else a_shape[0]
    k = a_shape[0] if ta else a_shape[1]
    n = b_shape[0] if tb else b_shape[1]
    assert k == (b_shape[1] if tb else b_shape[0]), (name, a_shape, b_shape)
    out_n = n
    if bd and ta:
        assert not tb
        tm, tn, tk = m // bd, n // bd, _pick(k, tk)
        grid, out_n = (bd, 1, k // tk), tn
        a_blk = ((tk, tm), lambda i, j, kk: (kk, i))
        b_blk = ((tk, tn), lambda i, j, kk: (kk, i))
        mn_spec = pl.BlockSpec((tm, tn), lambda i, j, kk: (i, 0))
    elif bd:
        tm, tn, tk = _pick(m, tm), n // bd, k // bd
        grid = (m // tm, bd, 1)
        a_blk = ((tm, tk), lambda i, j, kk: (i, j))
        b_blk = ((tn, tk) if tb else (tk, tn), lambda i, j, kk: (j, j))
        mn_spec = pl.BlockSpec((tm, tn), lambda i, j, kk: (i, j))
    else:
        tm, tn, tk = _pick(m, tm), _pick(n, tn), _pick(k, tk)
        grid = (m // tm, n // tn, k // tk)
        a_blk = ((tk, tm), lambda i, j, kk: (kk, i)) if ta else ((tm, tk), lambda i, j, kk: (i, kk))
        b_blk = ((tn, tk), lambda i, j, kk: (j, kk)) if tb else ((tk, tn), lambda i, j, kk: (kk, j))
        mn_spec = pl.BlockSpec((tm, tn), lambda i, j, kk: (i, j))

    ab_specs = [pl.BlockSpec(*(a_blk if q % 2 == 0 else b_blk)) for q in range(len(ab))]
    mn_arrays = [e[0] if isinstance(e, tuple) else e for e in mn]
    mn_specs = [pl.BlockSpec((tm, tn), lambda i, j, kk, c=e[1]: (i, c)) if isinstance(e, tuple) else mn_spec
                for e in mn]
    nk = grid[2]
    row_spec = pl.BlockSpec((1, tn), lambda i, j, kk: (0, j))
    n_ex, n_out = len(mn) + len(rows), len(outs)
    assert n_sums == 0 or (grid[1] == 1 and not bd)
    dims = (((0 if ta else 1,), (1 if tb else 0,)), ((), ()))

    def body(*refs):
        ab_refs, rest = refs[:len(ab)], refs[len(ab):]
        ex, o_refs = rest[:n_ex], rest[n_ex:n_ex + n_out]
        s_refs = rest[n_ex + n_out:n_ex + n_out + n_sums]
        first_row_tile = pl.program_id(0) == 0
        kk = pl.program_id(2)

        pairs = list(zip(ab_refs[0::2], ab_refs[1::2]))

        def product(pair):
            return lax.dot_general(pair[0][...].astype(_MXU), pair[1][...].astype(_MXU), dims,
                                   preferred_element_type=F32)

        def finish(total):
            vals = (total,) if epi is None else epi(total, *[r[...] for r in ex])
            for r, v in zip(o_refs, vals):
                r[...] = v.astype(r.dtype)
            for r, v in zip(s_refs, vals[n_out:]):
                r[...] = jnp.where(first_row_tile, v, r[...] + v)

        if nk == 1:
            total = product(pairs[0])
            for pair in pairs[1:]:
                total = total + product(pair)
            finish(total)
        else:
            acc = rest[-1]

            @pl.when(kk == 0)
            def _():
                acc[...] = jnp.zeros_like(acc)

            for pair in pairs:
                acc[...] += product(pair)

            @pl.when(kk == nk - 1)
            def _():
                finish(acc[...])

    res = pl.pallas_call(
        body, grid=grid,
        in_specs=ab_specs + mn_specs + [row_spec] * len(rows),
        out_specs=[mn_spec] * n_out + [row_spec] * n_sums,
        out_shape=[jax.ShapeDtypeStruct((m, out_n), dt) for dt in outs]
        + [jax.ShapeDtypeStruct((1, out_n), F32)] * n_sums,
        scratch_shapes=[pltpu.VMEM((tm, tn), F32)] if nk > 1 else [],
        compiler_params=_params(("arbitrary" if n_sums else "parallel", "parallel", "arbitrary")), name=name,
    )(*ab, *mn_arrays, *rows)
    return res[0] if n_out + n_sums == 1 else res


def _ew(fn, rows, bcs, out_rows, out_accs, *, name, tm=256):
    r = rows[0].shape[0]
    tm = min(tm, r)
    assert r % tm == 0
    nr, nb, no, na = len(rows), len(bcs), len(out_rows), len(out_accs)

    def body(*refs):
        i = pl.program_id(0)
        r_in, b_in = refs[:nr], refs[nr:nr + nb]
        o_r, o_a = refs[nr + nb:nr + nb + no], refs[nr + nb + no:]
        outs, accs = fn([x[...] for x in r_in], [x[...] for x in b_in])
        for ref, v in zip(o_r, outs):
            ref[...] = v.astype(ref.dtype)
        if na:
            @pl.when(i == 0)
            def _():
                for ref in o_a:
                    ref[...] = jnp.zeros_like(ref)

            for ref, v in zip(o_a, accs):
                ref[...] += v

    res = pl.pallas_call(
        body, grid=(r // tm,),
        in_specs=[pl.BlockSpec((tm, x.shape[1]), lambda i: (i, 0)) for x in rows]
        + [pl.BlockSpec((1, x.shape[1]), lambda i: (0, 0)) for x in bcs],
        out_specs=[pl.BlockSpec((tm, c), lambda i: (i, 0)) for c, _ in out_rows]
        + [pl.BlockSpec((1, c), lambda i: (0, 0)) for c in out_accs],
        out_shape=[jax.ShapeDtypeStruct((r, c), dt) for c, dt in out_rows]
        + [jax.ShapeDtypeStruct((1, c), F32) for c in out_accs],
        compiler_params=_params(("arbitrary",)), name=name,
    )(*rows, *bcs)
    return res


def _colsum(x):
    return jnp.sum(x, axis=0, keepdims=True)


def _sigmoid(x):
    return 1.0 / (1.0 + jnp.exp(-x))


def _rms_bwd_tile(xv, dv, g):
    rs = lax.rsqrt(jnp.mean(xv * xv, axis=-1, keepdims=True) + RMS_EPS)
    gd = dv * g
    dx = rs * gd - xv * (rs * rs * rs) * jnp.mean(gd * xv, axis=-1, keepdims=True)
    return dx, _colsum(dv * xv * rs)


def _rms_fwd(x, g, name):
    def fn(r, b):
        xv = r[0]
        rs = lax.rsqrt(jnp.mean(xv * xv, axis=-1, keepdims=True) + RMS_EPS)
        return [xv * rs * b[0]], []
    return _ew(fn, [x], [g], [(x.shape[1], BF16)], [], name=name, tm=1024)[0]


def _rms_bwd(x, dn, res, g, name):
    def fn(r, b):
        dx, dg = _rms_bwd_tile(r[0], r[1], b[0])
        if res is not None:
            dx = dx + r[2]
        return [dx], [dg]
    rows = [x, dn] + ([res] if res is not None else [])
    return _ew(fn, rows, [g], [(x.shape[1], F32)], [x.shape[1]], name=name)


def _scan_order(x):
    l, c = x.shape
    return x.reshape(l // (SUBLANES * SCAN_SEG), SUBLANES, SCAN_SEG, c).transpose(0, 2, 1, 3).reshape(l, c)


def _time_order(x):
    l, c = x.shape
    return x.reshape(l // (SUBLANES * SCAN_SEG), SCAN_SEG, SUBLANES, c).transpose(0, 2, 1, 3).reshape(l, c)


def _ssm_scan(x, w_re, w_im, a_pair, *, reverse, s_fwd=None, u=None, name):
    l = x.shape[0]
    seg, w = SCAN_SEG, SCAN_W
    bd_w = SSM_W // SSM_BD
    tiles_per_bd = SSM_S // SSM_BD // w
    nch = min(SCAN_CHAINS, l // (SUBLANES * seg))
    chain_rows = SUBLANES * seg
    tb = nch * chain_rows
    nt = l // tb
    with_da = s_fwd is not None
    assert reverse or not with_da

    def tt(t):
        return nt - 1 - t if reverse else t

    def body(*refs):
        if with_da:
            (x_ref, wr_ref, wi_ref, a_ref, sf_ref, sp_ref, u_ref, s_ref, da_ref, dw_ref, dx_ref,
             p_ref, c_ref, b_scr, l_scr) = refs
        else:
            x_ref, wr_ref, wi_ref, a_ref, s_ref, p_ref, c_ref, b_scr, l_scr = refs
        t_blk = pl.program_id(1)
        ar, ai = a_ref[0], a_ref[1]

        @pl.when(t_blk == 0)
        def _():
            def pstep(i, carry):
                pr, pi = carry
                p_ref[0, pl.ds(i, 1), :] = pr
                p_ref[1, pl.ds(i, 1), :] = pi
                return pr * ar - pi * ai, pr * ai + pi * ar

            lax.fori_loop(0, seg, pstep, (ar, ai))
            c_ref[...] = jnp.zeros_like(c_ref)
            if with_da:
                da_ref[...] = jnp.zeros_like(da_ref)
                dw_ref[...] = jnp.zeros_like(dw_ref)
                dx_ref[...] = jnp.zeros_like(dx_ref)

        xb = x_ref[...].astype(_MXU)
        b_scr[:, :w] = jnp.dot(xb, wr_ref[...], preferred_element_type=F32)
        b_scr[:, w:] = jnp.dot(xb, wi_ref[...], preferred_element_type=F32)
        arb, aib = jnp.broadcast_to(ar, (SUBLANES, w)), jnp.broadcast_to(ai, (SUBLANES, w))
        zero = jnp.zeros((SUBLANES, w), F32)

        def tile(g, step):
            return pl.ds(pl.multiple_of(g * chain_rows + step * SUBLANES, SUBLANES), SUBLANES)

        def rows(g, i):
            return tile(g, seg - 1 - i if reverse else i)

        def local_step(i, carry):
            out = []
            for g in range(nch):
                sr, si = carry[2 * g], carry[2 * g + 1]
                idx = rows(g, i)
                sr, si = arb * sr - aib * si + b_scr[idx, :w], arb * si + aib * sr + b_scr[idx, w:]
                l_scr[idx, :w] = sr
                l_scr[idx, w:] = si
                out += [sr, si]
            return tuple(out)

        def unrolled(step_fn, first):
            def trip(q, carry):
                for r in range(SCAN_UNROLL):
                    carry = step_fn(first + q * SCAN_UNROLL + r, carry)
                return carry
            return trip

        ends = lax.fori_loop(0, seg // SCAN_UNROLL, unrolled(local_step, 0), (zero,) * (2 * nch))

        a_seg_r, a_seg_i = p_ref[0, seg - 1:seg, :], p_ref[1, seg - 1:seg, :]
        cr, ci = c_ref[0], c_ref[1]
        sub = lax.broadcasted_iota(jnp.int32, (SUBLANES, w), 0)
        ins = [[zero, zero] for _ in range(nch)]
        order = [(g, k) for g in range(nch) for k in range(SUBLANES)]
        for g, k in (order[::-1] if reverse else order):
            ins[g] = [jnp.where(sub == k, cr, ins[g][0]), jnp.where(sub == k, ci, ins[g][1])]
            er, ei = ends[2 * g][k:k + 1], ends[2 * g + 1][k:k + 1]
            cr, ci = er + a_seg_r * cr - a_seg_i * ci, ei + a_seg_r * ci + a_seg_i * cr
        c_ref[0] = cr
        c_ref[1] = ci

        def fix(g, i):
            idx = rows(g, i)
            pr, pi = p_ref[0, pl.ds(i, 1), :], p_ref[1, pl.ds(i, 1), :]
            sr = l_scr[idx, :w] + pr * ins[g][0] - pi * ins[g][1]
            si = l_scr[idx, w:] + pr * ins[g][1] + pi * ins[g][0]
            s_ref[idx, :w] = sr.astype(s_ref.dtype)
            s_ref[idx, w:] = si.astype(s_ref.dtype)
            return sr, si

        if not with_da:
            def fix_step(i, carry):
                for g in range(nch):
                    fix(g, i)
                return carry

            lax.fori_loop(0, seg // SCAN_UNROLL, unrolled(fix_step, 0), 0)
        else:
            def adj_step(i, acc):
                acc_r, acc_i = acc
                for g in range(nch):
                    lr, li = fix(g, i)
                    prev = tile(g, seg - 2 - i)
                    fr, fi = sf_ref[prev, :w].astype(F32), sf_ref[prev, w:].astype(F32)
                    acc_r, acc_i = acc_r + lr * fr + li * fi, acc_i + li * fr - lr * fi
                return acc_r, acc_i

            acc = lax.fori_loop(0, seg // SCAN_UNROLL - 1, unrolled(adj_step, 0), (zero, zero))
            for i in range(seg - SCAN_UNROLL, seg - 1):
                acc = adj_step(i, acc)
            acc_r, acc_i = acc
            first_block = tt(t_blk) == 0
            for g in range(nch):
                lr, li = fix(g, seg - 1)
                seg_ends = tile(g, seg - 1)
                if g == 0:
                    pvr = jnp.where(first_block, 0.0, sp_ref[SUBLANES - 1:SUBLANES, :w].astype(F32))
                    pvi = jnp.where(first_block, 0.0, sp_ref[SUBLANES - 1:SUBLANES, w:].astype(F32))
                else:
                    pvr = sf_ref[g * chain_rows - 1:g * chain_rows, :w].astype(F32)
                    pvi = sf_ref[g * chain_rows - 1:g * chain_rows, w:].astype(F32)
                fr = jnp.where(sub == 0, pvr, pltpu.roll(sf_ref[seg_ends, :w].astype(F32), 1, 0))
                fi = jnp.where(sub == 0, pvi, pltpu.roll(sf_ref[seg_ends, w:].astype(F32), 1, 0))
                acc_r = acc_r + lr * fr + li * fi
                acc_i = acc_i + li * fr - lr * fi
            da_ref[0] += jnp.sum(acc_r, axis=0, keepdims=True)
            da_ref[1] += jnp.sum(acc_i, axis=0, keepdims=True)
            dw_ref[...] += _tn_dot(u_ref[...], s_ref[...])
            dx_ref[...] += _tn_dot(xb, sf_ref[...])

    x_spec = pl.BlockSpec((tb, bd_w), lambda j, t: (tt(t), j // tiles_per_bd))
    w_spec = pl.BlockSpec((bd_w, w), lambda j, t: (j // tiles_per_bd, j))
    d_spec = pl.BlockSpec((bd_w, 2 * w), lambda j, t: (j // tiles_per_bd, j % tiles_per_bd))
    a_spec = pl.BlockSpec((2, 1, w), lambda j, t: (0, 0, j))
    s_spec = pl.BlockSpec((tb, 2 * w), lambda j, t: (tt(t), j))
    in_specs, args = [x_spec, w_spec, w_spec, a_spec], [x, w_re, w_im, a_pair]
    out_specs, out_shape = [s_spec], [jax.ShapeDtypeStruct((l, 2 * SSM_S), BF16)]
    scratch = [pltpu.VMEM((2, seg, w), F32), pltpu.VMEM((2, 1, w), F32)] + [pltpu.VMEM((tb, 2 * w), F32)] * 2
    if with_da:
        in_specs += [s_spec, pl.BlockSpec((SUBLANES, 2 * w),
                                          lambda j, t: (jnp.maximum(tt(t) * (tb // SUBLANES) - 1, 0), j)),
                     x_spec]
        args += [s_fwd, s_fwd, u]
        out_specs += [a_spec, d_spec, d_spec]
        out_shape += ([jax.ShapeDtypeStruct((2, 1, SSM_S), F32)]
                      + [jax.ShapeDtypeStruct((SSM_W, 2 * SSM_S // SSM_BD), F32)] * 2)
    res = pl.pallas_call(
        body, grid=(SSM_S // w, nt), in_specs=in_specs, out_specs=out_specs, out_shape=out_shape,
        scratch_shapes=scratch, compiler_params=_params(("parallel", "arbitrary")), name=name,
    )(*args)
    return res if with_da else res[0]


def _nt_dot(x, y):
    return lax.dot_general(x.astype(_MXU), y.astype(_MXU), (((1,), (1,)), ((), ())), preferred_element_type=F32)


def _tn_dot(x, y):
    return lax.dot_general(x.astype(_MXU), y.astype(_MXU), (((0,), (0,)), ((), ())), preferred_element_type=F32)


def _nn_dot(x, y):
    return jnp.dot(x.astype(_MXU), y.astype(_MXU), preferred_element_type=F32)


def _attn_mask2(gb, nb):
    qi = lax.broadcasted_iota(jnp.int32, (ATT_WIN, 2 * ATT_WIN), 0)
    c = lax.broadcasted_iota(jnp.int32, (ATT_WIN, 2 * ATT_WIN), 1)
    has_prev = (gb % nb) != 0
    prev_ok = jnp.logical_and(jnp.logical_and(c < ATT_WIN, c >= qi), has_prev)
    own_ok = jnp.logical_and(c >= ATT_WIN, c - ATT_WIN <= qi)
    return jnp.logical_or(prev_ok, own_ok)


def _attn_specs(qb):
    cur = pl.BlockSpec((qb * ATT_WIN, ATT_GW), lambda i: (i, 0))
    prev = pl.BlockSpec((ATT_WIN, ATT_GW), lambda i: (jnp.maximum(qb * i - 1, 0), 0))
    return cur, prev


def _attn_fwd(q, k, v, nb, name):
    l = q.shape[0]
    scale = ATT_E ** -0.5
    w = ATT_WIN

    qb = ATT_QB_FWD

    def body(q_ref, kc_ref, kp_ref, vc_ref, vp_ref, o_ref, lse_ref):
        i = pl.program_id(0)
        masks = [_attn_mask2(qb * i + b, nb) for b in range(qb)]
        for h in range(ATT_HG):
            sl = slice(h * ATT_E, (h + 1) * ATT_E)
            k_ext = jnp.concatenate([kp_ref[:, sl], kc_ref[:, sl]], axis=0)
            v_ext = jnp.concatenate([vp_ref[:, sl], vc_ref[:, sl]], axis=0)
            for b in range(qb):
                r, kr = slice(b * w, (b + 1) * w), slice(b * w, (b + 2) * w)
                s = jnp.where(masks[b], _nt_dot(q_ref[r, sl], k_ext[kr]) * scale, NEG_INF)
                mx = jnp.max(s, axis=-1, keepdims=True)
                p = jnp.exp(s - mx)
                den = jnp.sum(p, axis=-1, keepdims=True)
                o_ref[r, sl] = _nn_dot(p, v_ext[kr]) / den
                lse_ref[r, sl] = jnp.broadcast_to(mx + jnp.log(den), (w, ATT_E))

    cur, prev = _attn_specs(qb)
    return pl.pallas_call(
        body, grid=(l // (qb * w),), in_specs=[cur, cur, prev, cur, prev], out_specs=[cur, cur],
        out_shape=[jax.ShapeDtypeStruct((l, ATT_GW), F32)] * 2,
        compiler_params=_params(("parallel",)), name=name,
    )(q, k, k, v, v)


def _attn_bwd(q, k, v, do, lse, dd, nb, name):
    l = q.shape[0]
    scale = ATT_E ** -0.5
    w = ATT_WIN
    nblk = l // w

    def body(q_ref, kc_ref, kp_ref, vc_ref, vp_ref, do_ref, lse_ref, dd_ref, qn_ref, don_ref, lsen_ref, ddn_ref,
             dq_ref, dk_ref, dv_ref, dk_acc, dv_acc):
        i = pl.program_id(0)
        masks = [_attn_mask2(ATT_QB * i + b, nb) for b in range(ATT_QB)]
        nxt = ATT_QB * (i + 1)
        nxt_attends = jnp.logical_and(nxt < nblk, (nxt % nb) != 0)
        qi = lax.broadcasted_iota(jnp.int32, (w, w), 0)
        kj = lax.broadcasted_iota(jnp.int32, (w, w), 1)
        mask_n = jnp.logical_and(kj >= qi, nxt_attends)
        dk_acc[...] = jnp.zeros_like(dk_acc)
        dv_acc[...] = jnp.zeros_like(dv_acc)
        for h in range(ATT_HG):
            sl, col = slice(h * ATT_E, (h + 1) * ATT_E), slice(h * ATT_E, h * ATT_E + 1)
            k_ext = jnp.concatenate([kp_ref[:, sl], kc_ref[:, sl]], axis=0)
            v_ext = jnp.concatenate([vp_ref[:, sl], vc_ref[:, sl]], axis=0)
            for b in range(ATT_QB):
                r, kr = slice(b * w, (b + 1) * w), slice(b * w, (b + 2) * w)
                qh, doh, k2, v2 = q_ref[r, sl], do_ref[r, sl], k_ext[kr], v_ext[kr]
                p = jnp.where(masks[b], jnp.exp(_nt_dot(qh, k2) * scale - lse_ref[r, col]), 0.0)
                ds = p * (_nt_dot(doh, v2) - dd_ref[r, col]) * scale
                dq_ref[r, sl] = _nn_dot(ds, k2).astype(dq_ref.dtype)
                dk2, dv2 = _tn_dot(ds, qh), _tn_dot(p, doh)
                dk_acc[r, sl] += dk2[w:]
                dv_acc[r, sl] += dv2[w:]
                if b > 0:
                    rp = slice((b - 1) * w, b * w)
                    dk_acc[rp, sl] += dk2[:w]
                    dv_acc[rp, sl] += dv2[:w]
            last = slice((ATT_QB - 1) * w, ATT_QB * w)
            kl, vl, qn, don = kc_ref[last, sl], vc_ref[last, sl], qn_ref[:, sl], don_ref[:, sl]
            pn = jnp.where(mask_n, jnp.exp(_nt_dot(qn, kl) * scale - lsen_ref[:, col]), 0.0)
            dsn = pn * (_nt_dot(don, vl) - ddn_ref[:, col]) * scale
            dk_acc[last, sl] += _tn_dot(dsn, qn)
            dv_acc[last, sl] += _tn_dot(pn, don)
        dk_ref[...] = dk_acc[...].astype(dk_ref.dtype)
        dv_ref[...] = dv_acc[...].astype(dv_ref.dtype)

    cur, prev = _attn_specs(ATT_QB)
    nxt_spec = pl.BlockSpec((w, ATT_GW), lambda i: (jnp.minimum(ATT_QB * (i + 1), nblk - 1), 0))
    return pl.pallas_call(
        body, grid=(l // (ATT_QB * w),),
        in_specs=[cur, cur, prev, cur, prev, cur, cur, cur, nxt_spec, nxt_spec, nxt_spec, nxt_spec],
        out_specs=[cur] * 3, out_shape=[jax.ShapeDtypeStruct((l, ATT_GW), BF16)] * 3,
        scratch_shapes=[pltpu.VMEM((ATT_QB * w, ATT_GW), F32)] * 2,
        compiler_params=_params(("parallel",)), name=name,
    )(q, k, k, v, v, do, lse, dd, q, do, lse, dd)


def _to_perm(a, d):
    if d == 1:
        return a
    l, c = a.shape
    return a.reshape(l // d, d, c).transpose(1, 0, 2).reshape(l, c)


def _from_perm(a, d):
    if d == 1:
        return a
    l, c = a.shape
    return a.reshape(d, l // d, c).transpose(1, 0, 2).reshape(l, c)


def _merged_attn_proj(o_g, lse_g, w_t, name, tm=1024):
    l = o_g[0].shape[0]
    ng = len(o_g)

    def body(*refs):
        o_refs, l_refs, (w_ref, oa_ref, lt_ref, br_ref) = refs[:ng], refs[ng:2 * ng], refs[2 * ng:]
        ls = [r[...] for r in l_refs]
        mx = ls[0]
        for lv in ls[1:]:
            mx = jnp.maximum(mx, lv)
        es = [jnp.exp(lv - mx) for lv in ls]
        tot, acc = es[0], es[0] * o_refs[0][...]
        for e, o_ref in zip(es[1:], o_refs[1:]):
            tot, acc = tot + e, acc + e * o_ref[...]
        o = acc / tot
        oa_ref[...] = o
        lt_ref[...] = mx + jnp.log(tot)
        br_ref[...] = _nt_dot(o, w_ref[...]).astype(br_ref.dtype)

    row = pl.BlockSpec((tm, ATT_GW), lambda i: (i, 0))
    return pl.pallas_call(
        body, grid=(l // tm,),
        in_specs=[row] * (2 * ng) + [pl.BlockSpec(w_t.shape, lambda i: (0, 0))],
        out_specs=[row, row, pl.BlockSpec((tm, w_t.shape[0]), lambda i: (i, 0))],
        out_shape=[jax.ShapeDtypeStruct((l, ATT_GW), F32)] * 2 + [jax.ShapeDtypeStruct((l, w_t.shape[0]), BF16)],
        compiler_params=_params(("parallel",)), name=name,
    )(*o_g, *lse_g, w_t)


def _mem_probs(qh, kh):
    s = _nt_dot(qh, kh) * (MEM_E ** -0.5)
    e = jnp.exp(s - jnp.max(s, axis=-1, keepdims=True))
    return e / jnp.sum(e, axis=-1, keepdims=True)


def _mem_fwd(mq, kv, name, tm=1024):
    l, nm = mq.shape[0], kv.shape[0]

    def body(q_ref, kv_ref, o_ref):
        for h in range(MEM_H):
            sl = slice(h * MEM_E, (h + 1) * MEM_E)
            p = _mem_probs(q_ref[:, sl], kv_ref[:, sl])
            o_ref[:, sl] = _nn_dot(p, kv_ref[:, MEM_W + h * MEM_E:MEM_W + (h + 1) * MEM_E]).astype(o_ref.dtype)

    return pl.pallas_call(
        body, grid=(l // tm,),
        in_specs=[pl.BlockSpec((tm, MEM_W), lambda i: (i, 0)), pl.BlockSpec((nm, 2 * MEM_W), lambda i: (0, 0))],
        out_specs=pl.BlockSpec((tm, MEM_W), lambda i: (i, 0)),
        out_shape=jax.ShapeDtypeStruct((l, MEM_W), BF16),
        compiler_params=_params(("parallel",)), name=name,
    )(mq, kv)


def _mem_bwd(mq, kv, dmo, name, tm=1024):
    l, nm = mq.shape[0], kv.shape[0]
    scale = MEM_E ** -0.5

    def body(q_ref, kv_ref, do_ref, dq_ref, dkv_ref):
        @pl.when(pl.program_id(0) == 0)
        def _():
            dkv_ref[...] = jnp.zeros_like(dkv_ref)

        for h in range(MEM_H):
            sl = slice(h * MEM_E, (h + 1) * MEM_E)
            vsl = slice(MEM_W + h * MEM_E, MEM_W + (h + 1) * MEM_E)
            qh, kh, vh, doh = q_ref[:, sl], kv_ref[:, sl], kv_ref[:, vsl], do_ref[:, sl]
            p = _mem_probs(qh, kh)
            dp = _nt_dot(doh, vh)
            ds = p * (dp - jnp.sum(dp * p, axis=-1, keepdims=True)) * scale
            dq_ref[:, sl] = _nn_dot(ds, kh).astype(dq_ref.dtype)
            dkv_ref[:, sl] += _tn_dot(ds, qh)
            dkv_ref[:, vsl] += _tn_dot(p, doh)

    row = pl.BlockSpec((tm, MEM_W), lambda i: (i, 0))
    full = pl.BlockSpec((nm, 2 * MEM_W), lambda i: (0, 0))
    return pl.pallas_call(
        body, grid=(l // tm,), in_specs=[row, full, row], out_specs=[row, full],
        out_shape=[jax.ShapeDtypeStruct((l, MEM_W), BF16), jax.ShapeDtypeStruct((nm, 2 * MEM_W), F32)],
        compiler_params=_params(("arbitrary",)), name=name,
    )(mq, kv, dmo)


def _gated_out_proj(zg, branches, b_gate, w_o, x, g2, name, tm=512):
    l, d = x.shape
    nbr = len(branches)

    def body(zg_ref, *rest):
        br_refs, (bg_ref, w_ref, x_ref, g2_ref, m_ref, h_ref, n_ref) = rest[:nbr], rest[nbr:]
        merged = jnp.zeros((tm, d), F32)
        for i, br_ref in enumerate(br_refs):
            cols = slice(i * d, (i + 1) * d)
            merged += _sigmoid(zg_ref[:, cols].astype(F32) + bg_ref[:, cols]) * br_ref[...].astype(F32)
        mb = merged.astype(BF16)
        m_ref[...] = mb
        hv = jnp.dot(mb.astype(_MXU), w_ref[...].astype(_MXU), preferred_element_type=F32) + x_ref[...]
        h_ref[...] = hv
        rs = lax.rsqrt(jnp.mean(hv * hv, axis=-1, keepdims=True) + RMS_EPS)
        n_ref[...] = (hv * rs * g2_ref[...]).astype(n_ref.dtype)

    row = lambda c: pl.BlockSpec((tm, c), lambda i: (i, 0))
    full = lambda a: pl.BlockSpec(a.shape, lambda i: (0, 0))
    return pl.pallas_call(
        body, grid=(l // tm,),
        in_specs=[row(nbr * d)] + [row(d)] * nbr + [full(b_gate), full(w_o), row(d), full(g2)],
        out_specs=[row(d)] * 3,
        out_shape=[jax.ShapeDtypeStruct((l, d), BF16), jax.ShapeDtypeStruct((l, d), F32),
                   jax.ShapeDtypeStruct((l, d), BF16)],
        compiler_params=_params(("parallel",)), name=name,
    )(zg, *branches, b_gate, w_o, x, g2)


def _discretize(lam_re, lam_im, log_dt, b_re, b_im):
    dt = jnp.exp(log_dt)[:, None]
    mag = jnp.exp(lam_re * dt)
    a_re, a_im = mag * jnp.cos(lam_im * dt), mag * jnp.sin(lam_im * dt)
    nr, ni = a_re - 1.0, a_im
    den = lam_re * lam_re + lam_im * lam_im
    coef_re = (nr * lam_re + ni * lam_im) / den
    coef_im = (ni * lam_re - nr * lam_im) / den
    bb_re = coef_re[..., None] * b_re - coef_im[..., None] * b_im
    bb_im = coef_re[..., None] * b_im + coef_im[..., None] * b_re
    return a_re, a_im, bb_re, bb_im


def _tiled(re, im):
    r = re.shape[0]
    both = jnp.concatenate([re.reshape(r, -1, SCAN_W), im.reshape(r, -1, SCAN_W)], axis=2)
    return both.reshape(r, 2 * re.shape[1])


def _untiled(x):
    r = x.shape[0]
    t = x.reshape(r, -1, 2 * SCAN_W)
    return t[:, :, :SCAN_W].reshape(r, -1), t[:, :, SCAN_W:].reshape(r, -1)


def _bd_in(bb):
    return jnp.einsum("gph,gk->ghkp", bb, jnp.eye(SSM_G, dtype=bb.dtype)).reshape(SSM_W, SSM_S)


def _bd_diag(x):
    gb = SSM_G // SSM_BD
    t = x.reshape(SSM_BD, gb, SSM_H, gb, SSM_P)
    return jnp.einsum("bghgp->bghp", t).reshape(SSM_G, SSM_H, SSM_P)


_ANY = pl.BlockSpec(memory_space=pl.ANY)
_MESH = pl.DeviceIdType.MESH


def _allgather(x, name):
    def body(x_ref, out_ref, send_sems, recv_sems, local_sem):
        mx, my, mc = lax.axis_index("x"), lax.axis_index("y"), lax.axis_index("c")
        me, sibling = (mx, my, mc), (mx, my, 1 - mc)
        chips = [(1 - mx, my), (mx, 1 - my), (1 - mx, 1 - my)]

        def blk(px, py, pc):
            return out_ref.at[4 * px + 2 * py + pc]

        def copy(k, block, to, src=None):
            return pltpu.make_async_remote_copy(
                src_ref=blk(*block) if src is None else src, dst_ref=blk(*block),
                send_sem=send_sems.at[k], recv_sem=recv_sems.at[k], device_id=to, device_id_type=_MESH)

        mine = pltpu.make_async_copy(x_ref, blk(*me), local_sem)
        mine.start()
        first = [copy(0, me, sibling, src=x_ref)]
        first += [copy(1 + j, me, (*chip, mc), src=x_ref) for j, chip in enumerate(chips)]
        for cp in first:
            cp.start()
        passed = [copy(4 + j, (*chip, mc), sibling) for j, chip in enumerate(chips)]
        for j, chip in enumerate(chips):
            copy(1 + j, (*chip, mc), me).wait_recv()
            passed[j].start()
        copy(0, sibling, me).wait_recv()
        for j, chip in enumerate(chips):
            copy(4 + j, (*chip, 1 - mc), me).wait_recv()
        for cp in first + passed:
            cp.wait_send()
        mine.wait()

    return pl.pallas_call(
        body, out_shape=jax.ShapeDtypeStruct((N_DEV,) + x.shape, x.dtype), in_specs=[_ANY], out_specs=_ANY,
        scratch_shapes=[pltpu.SemaphoreType.DMA((7,)), pltpu.SemaphoreType.DMA((7,)), pltpu.SemaphoreType.DMA],
        name=name,
    )(x)


def _pair_exchange(g, name):
    def body(g_ref, out_ref, send_sems, recv_sems):
        mx, my, mc = lax.axis_index("x"), lax.axis_index("y"), lax.axis_index("c")
        copies = [pltpu.make_async_remote_copy(
            src_ref=g_ref.at[2 * k + (1 - mc)], dst_ref=out_ref.at[k], send_sem=send_sems.at[k],
            recv_sem=recv_sems.at[k], device_id=(mx, my, 1 - mc), device_id_type=_MESH) for k in range(4)]
        for cp in copies:
            cp.start()
        for cp in copies:
            cp.wait()

    return pl.pallas_call(
        body, out_shape=jax.ShapeDtypeStruct((4,) + g.shape[1:], g.dtype), in_specs=[_ANY], out_specs=_ANY,
        scratch_shapes=[pltpu.SemaphoreType.DMA((4,)), pltpu.SemaphoreType.DMA((4,))], name=name,
    )(g)


_HBM = pl.BlockSpec(memory_space=pltpu.HBM)
_SEM = pl.BlockSpec(memory_space=pltpu.SEMAPHORE)
_EFFECT = pltpu.SideEffectType.DATAFLOW_SIDE_EFFECTING
_TOKEN = jax.ShapeDtypeStruct((8, 128), F32)


def _peer(rel):
    pos = (lax.axis_index("x"), lax.axis_index("y"), lax.axis_index("c"))
    return tuple(1 - p if (rel >> (2 - i)) & 1 else p for i, p in enumerate(pos))


def _index_of(dev):
    return 4 * dev[0] + 2 * dev[1] + dev[2]


def _split_copies(src_ref, land_ref, sems, plan):
    n = len(plan)
    return [pltpu.make_async_remote_copy(
        src_ref=src_ref if s is None else src_ref.at[s], dst_ref=land_ref.at[d], send_sem=sems[k],
        recv_sem=sems[n + k], device_id=peer, device_id_type=_MESH) for k, (s, d, peer) in enumerate(plan)]


def _split_start(src, n_land, plan_fn, after, name):
    blk = src.shape[-2:]
    land = lax.empty((n_land,) + blk, src.dtype)
    n = len(plan_fn())

    def body(src_ref, land_ref, after_ref, *outs):
        for cp in _split_copies(src_ref, land_ref, outs[:2 * n], plan_fn()):
            cp.start()
        outs[2 * n + 2][...] = jnp.zeros_like(outs[2 * n + 2])

    res = pl.pallas_call(
        body, name=name,
        out_shape=(pltpu.SemaphoreType.DMA(()),) * (2 * n)
        + (pltpu.HBM(src.shape, src.dtype), pltpu.HBM(land.shape, land.dtype), _TOKEN),
        in_specs=(_HBM, _HBM, _ANY),
        out_specs=(_SEM,) * (2 * n) + (_HBM, _HBM, pl.BlockSpec(memory_space=pltpu.VMEM)),
        input_output_aliases={0: 2 * n, 1: 2 * n + 1},
        compiler_params=pltpu.CompilerParams(has_side_effects=_EFFECT),
    )(pltpu.with_memory_space_constraint(src, pltpu.HBM), pltpu.with_memory_space_constraint(land, pltpu.HBM), after)
    return res[:2 * n], res[2 * n], res[2 * n + 1], res[2 * n + 2]


def _split_wait(sems, src, land, plan_fn, after, name):
    n = len(sems) // 2

    def body(src_ref, land_ref, *rest):
        for cp in _split_copies(src_ref, land_ref, rest[:2 * n], plan_fn()):
            cp.wait_send()
            cp.wait_recv()

    return pl.pallas_call(
        body, name=name,
        out_shape=(pltpu.HBM(src.shape, src.dtype), pltpu.HBM(land.shape, land.dtype)),
        in_specs=(_HBM, _HBM) + (_SEM,) * (2 * n) + (_ANY,), out_specs=(_HBM, _HBM),
        input_output_aliases={0: 0, 1: 1},
        compiler_params=pltpu.CompilerParams(has_side_effects=_EFFECT),
    )(src, land, *sems, after)


def _gather_plan():
    me = _index_of(_peer(0))
    return [(None, me, _peer(rel)) for rel in range(1, N_DEV)]


def _gather_wait_plan():
    return [(None, _index_of(_peer(rel)), _peer(rel)) for rel in range(1, N_DEV)]


def _chip_plan():
    return [(_index_of(_peer(rel)) // 2, j, _peer(rel)) for j, rel in enumerate((4, 2, 6))]


def _owner_plan():
    return [(_index_of(_peer(rel)), rel - 1, _peer(rel)) for rel in range(1, N_DEV)]


def _pair_sum(g, t1, my_c, name, tr):
    _, r, c = g.shape

    def body(c_ref, g_ref, t_ref, o_ref, ob_ref):
        s = g_ref[...] + t_ref[...]
        o_ref[...] = s
        ob_ref[...] = s.astype(BF16)

    blk = pl.BlockSpec((None, tr, c), lambda k, i, cr: (k, i, 0))
    return pl.pallas_call(
        body,
        grid_spec=pltpu.PrefetchScalarGridSpec(
            num_scalar_prefetch=1, grid=(4, r // tr),
            in_specs=[pl.BlockSpec((None, tr, c), lambda k, i, cr: (2 * k + cr[0], i, 0)), blk],
            out_specs=[blk, blk]),
        out_shape=[jax.ShapeDtypeStruct((4, r, c), F32), jax.ShapeDtypeStruct((4, r, c), BF16)],
        compiler_params=_params(("parallel", "parallel")), name=name,
    )(my_c, g, t1)


def _adam_math(g, w, m, v):
    m = ADAM_B1 * m + (1.0 - ADAM_B1) * g
    v = ADAM_B2 * v + (1.0 - ADAM_B2) * (g * g)
    m_hat = m / (1.0 - ADAM_B1 ** ADAM_STEP)
    v_hat = v / (1.0 - ADAM_B2 ** ADAM_STEP)
    delta = -ADAM_LR * (m_hat / (jnp.sqrt(v_hat) + ADAM_EPS) + ADAM_WD * w)
    return delta, m, v


def _grad_sum(own, own_index, recv, name, tr):
    _, r, c = own.shape
    n = recv.shape[0]

    def body(k_ref, own_ref, *rest):
        g = own_ref[...]
        for recv_ref in rest[:n]:
            g = g + recv_ref[...].astype(F32)
        rest[n][...] = g

    def slot(j):
        return pl.BlockSpec((None, tr, c), lambda i, kr: (j, i, 0))

    return pl.pallas_call(
        body,
        grid_spec=pltpu.PrefetchScalarGridSpec(
            num_scalar_prefetch=1, grid=(r // tr,),
            in_specs=[pl.BlockSpec((None, tr, c), lambda i, kr: (kr[0], i, 0))] + [slot(j) for j in range(n)],
            out_specs=pl.BlockSpec((tr, c), lambda i, kr: (i, 0))),
        out_shape=jax.ShapeDtypeStruct((r, c), F32),
        compiler_params=_params(("parallel",)), name=name,
    )(own_index, own, *([recv] * n))


def _adam_many(g, w, m, v, row_tiles, name):
    n = len(g)

    def body(*refs):
        ins, outs = refs[:4 * n], refs[4 * n:]
        for i in range(n):
            res = _adam_math(ins[i][...], ins[n + i][...], ins[2 * n + i][...], ins[3 * n + i][...])
            for kind in range(3):
                outs[kind * n + i][...] = res[kind]

    def spec(a):
        blk = (a.shape[0] // row_tiles,) + a.shape[1:]
        return pl.BlockSpec(blk, lambda t, nd=a.ndim: (t,) + (0,) * (nd - 1))

    specs = [spec(a) for a in g]
    res = pl.pallas_call(
        body, grid=(row_tiles,), in_specs=specs * 4, out_specs=specs * 3,
        out_shape=[jax.ShapeDtypeStruct(a.shape, F32) for a in g] * 3,
        compiler_params=_params(("parallel",)), name=name,
    )(*g, *w, *m, *v)
    return res[:n], res[n:2 * n], res[2 * n:]


def _sum8(g8, name):
    _, r, c = g8.shape

    def body(g_ref, o_ref):
        acc = g_ref[0]
        for j in range(1, N_DEV):
            acc = acc + g_ref[j]
        o_ref[...] = acc

    return pl.pallas_call(
        body, grid=(1,), in_specs=[pl.BlockSpec((N_DEV, r, c), lambda i: (0, 0, 0))],
        out_specs=pl.BlockSpec((r, c), lambda i: (0, 0)), out_shape=jax.ShapeDtypeStruct((r, c), F32),
        compiler_params=_params(("arbitrary",)), name=name,
    )(g8)


def _pack(arrs, pad_rows=8):
    flat = jnp.concatenate([a.reshape(-1) for a in arrs])
    n = flat.shape[0]
    q = PACK_C * pad_rows
    tot = -(-n // q) * q
    if tot != n:
        flat = jnp.concatenate([flat, jnp.zeros((tot - n,), flat.dtype)])
    return flat.reshape(tot // PACK_C, PACK_C)


def _unpack(buf, shapes):
    flat = buf.reshape(-1)
    out, off = [], 0
    for s in shapes:
        n = int(np.prod(s))
        out.append(flat[off:off + n].reshape(s))
        off += n
    return out


GROUPS = (("w_in",),
          ("w_glu", "w_ssm_br", "w_mem_br", "w_attn_br"),
          ("w_up", "w_down"),
          ("w_mem_kv", "w_o"))
GROUP_TR = (400, 384, 512, 256)
MLP_GROUP = 2
MIXER_GROUPS = (1, 3)
ATTN_BR_FOLD = 2


def _stored_shape(name):
    r, c, ax = BIG_SHAPE[name]
    rows, cols = (r // N_DEV, c) if ax == 0 else (c // N_DEV, r)
    return (rows // ATTN_BR_FOLD, cols * ATTN_BR_FOLD) if name == "w_attn_br" else (rows, cols)


def _stored(shard, name):
    a = shard[0].T if BIG_SHAPE[name][2] == 1 else shard[0]
    return a.reshape(_stored_shape(name))


def _unstored(a, name):
    r, c, ax = BIG_SHAPE[name]
    if ax == 0:
        return a.reshape(1, r // N_DEV, c)
    return a.reshape(c // N_DEV, r).T[None]


def _pack_group(d, names):
    return jnp.concatenate([_stored(d[n], n) for n in names], axis=0)


def _split_group(buf, names):
    out, off = {}, 0
    for n in names:
        rows = _stored_shape(n)[0]
        out[n] = buf[..., off:off + rows, :]
        off += rows
    return out


def _full_stored(stacked, name):
    r, c, ax = BIG_SHAPE[name]
    return stacked.reshape((r, c) if ax == 0 else (c, r))


def _stacked_stored(full, name):
    return full.reshape((N_DEV,) + _stored_shape(name))


def _gelu_parts(x):
    c0, c1 = math.sqrt(2.0 / math.pi), 0.044715
    th = jnp.tanh(c0 * (x + c1 * x * x * x))
    return th, c0, c1


def _local_step(x, mem, tgt, wb, sp, late_weights, grads_ready, small_grads_ready):
    l = x.shape[0]
    w_a, w_g = wb["w_in"][:ZA_W], wb["w_in"][ZA_W:]

    a_re, a_im, bb_re, bb_im = _discretize(sp["ssm_lambda_re"], sp["ssm_lambda_im"], sp["ssm_log_dt"],
                                           sp["ssm_b_re"], sp["ssm_b_im"])
    a_pair = jnp.stack([a_re.reshape(1, SSM_S), a_im.reshape(1, SSM_S)])
    a_conj = jnp.stack([a_re.reshape(1, SSM_S), -a_im.reshape(1, SSM_S)])
    b_re_t, b_im_t = _bd_in(bb_re).astype(BF16), _bd_in(bb_im).astype(BF16)
    c_re_t = _bd_in(sp["ssm_c_re"].transpose(0, 2, 1)).astype(BF16)
    c_im_t = (-_bd_in(sp["ssm_c_im"].transpose(0, 2, 1))).astype(BF16)
    d_row = sp["ssm_d"].reshape(1, SSM_W)

    n1 = _rms_fwd(x, sp["norm1_g"], "rms1")
    za = _mm(n1, w_a, [BF16], tb=True, name="in_proj_a", tn=1664)
    zg = _mm(n1, w_g, [BF16], tb=True, name="in_proj_g")
    for gi in MIXER_GROUPS:
        wb = {**wb, **late_weights(gi, za)}
    u = za[:, :SSM_W]
    mq = za[:, ZA_W - MEM_W:]

    u_s = _scan_order(u)
    s_all = _ssm_scan(u_s, b_re_t, b_im_t, a_pair, reverse=False, name="ssm_scan_fwd")
    def gelu_epi(acc, ut, dr):
        y = acc + dr * ut.astype(F32)
        th, _, _ = _gelu_parts(y)
        return y, 0.5 * y * (1.0 + th)
    y0, y1 = [_time_order(t) for t in _mm(s_all, _tiled(c_re_t, c_im_t), [F32, BF16], tb=True, epi=gelu_epi,
                                          mn=[u_s], rows=[d_row], bd=SSM_BD, tm=2048, name="ssm_cs")]

    def glu_epi(acc, y1t, bg):
        t = acc + bg
        return t, y1t.astype(F32) * _sigmoid(t)
    t_glu, y2 = _mm(y1, wb["w_glu"], [F32, BF16], epi=glu_epi, mn=[y1], rows=[sp["b_glu"]], name="ssm_glu")
    br_ssm = _mm(y2, wb["w_ssm_br"], [BF16], tb=True, name="ssm_br")

    qkv_p, o_g, lse_g = [], [], []
    for g, d in enumerate(DILATIONS):
        nb = l // d // ATT_WIN
        cols = [za[:, SSM_W + (3 * j + g) * ATT_GW: SSM_W + (3 * j + g + 1) * ATT_GW] for j in range(3)]
        qp, kp, vp = [_to_perm(cc, d) for cc in cols]
        qkv_p.append((qp, kp, vp))
        og, lg = _attn_fwd(qp, kp, vp, nb, "attn_fwd%d" % g)
        o_g.append(_from_perm(og, d))
        lse_g.append(_from_perm(lg, d))

    o_att, lse_tot, br_attn = _merged_attn_proj(o_g, lse_g, wb["w_attn_br"], "attn_merge_br")

    mn = _rms_fwd(mem, sp["mem_norm_g"], "rms_mem")
    kv = _mm(mn, wb["w_mem_kv"], [BF16], name="mem_kv")
    mo = _mem_fwd(mq, kv, "mem_attn_fwd")
    br_mem = _mm(mo, wb["w_mem_br"], [BF16], tb=True, name="mem_br")

    merged, h1, n2 = _gated_out_proj(zg, [br_ssm, br_attn, br_mem], sp["b_gate"], wb["w_o"], x, sp["norm2_g"],
                                     "gated_o_proj")

    def up_epi(acc):
        ra = jnp.maximum(acc, 0.0)
        return ra * ra, ra
    wm = late_weights(MLP_GROUP, n2)
    f_act, r_act = _mm(n2, wm["w_up"], [BF16, BF16], tb=True, epi=up_epi, name="mlp_up")
    def down_epi(acc, ht, tv, gf):
        hv = acc + ht
        rs = lax.rsqrt(jnp.mean(hv * hv, axis=-1, keepdims=True) + RMS_EPS)
        err = hv * rs * gf - tv
        dh, dgf = _rms_bwd_tile(hv, err * (1.0 / D_MODEL), gf)
        return dh, dgf, _colsum(err * err) * (0.5 / D_MODEL)
    dh2, d_final_g, loss_cols = _mm(f_act, wm["w_down"], [F32], epi=down_epi, mn=[h1, tgt], rows=[sp["final_g"]],
                                    n_sums=2, tk=1024, name="mlp_down")
    loss = jnp.sum(loss_cols, axis=1, keepdims=True)

    gw, gs = {}, {"final_g": d_final_g, "loss": loss}
    d_act = _mm(dh2, wm["w_down"], [BF16], tb=True, epi=lambda acc, ra: (acc * 2.0 * ra.astype(F32),), mn=[r_act],
                name="mlp_down_dx")
    dw_down = _mm(f_act, dh2, [F32], ta=True, name="mlp_down_dw")
    dw_up = _mm(d_act, n2, [F32], ta=True, name="mlp_up_dw")
    token = grads_ready(MLP_GROUP, {"w_up": dw_up, "w_down": dw_down})
    def up_dx_epi(acc, ht, dht, g2):
        dx, dg = _rms_bwd_tile(ht, acc, g2)
        return dx + dht, dg
    dh1, gs["norm2_g"] = _mm(d_act, wm["w_up"], [F32], epi=up_dx_epi, mn=[h1, dh2],
                             rows=[sp["norm2_g"] + token[:1, :1]], n_sums=1, tk=1024, name="mlp_up_dx")
    gw["w_o"] = _mm(merged, dh1, [F32], ta=True, name="o_proj_dw")

    def gate_bwd_epi(dm, *tiles):
        dbr, dz = [], []
        for zt, bt, bias in zip(tiles[0:3], tiles[3:6], tiles[6:9]):
            gt = _sigmoid(zt.astype(F32) + bias)
            dbr.append(dm * gt)
            dz.append(dm * bt.astype(F32) * gt * (1.0 - gt))
        return (*dbr, *dz, *[_colsum(t) for t in dz])
    gate_bias = [sp["b_gate"][:, i * D_MODEL:(i + 1) * D_MODEL] for i in range(3)]
    res = _mm(dh1, wb["w_o"], [BF16] * 6, tb=True, epi=gate_bwd_epi, mn=[(zg, 0), (zg, 1), (zg, 2), br_ssm, br_attn, br_mem],
              rows=gate_bias, n_sums=3, tm=512, name="o_proj_dx")
    (dbr_ssm, dbr_attn, dbr_mem), dzg = res[0:3], res[3:6]
    gs["b_gate"] = jnp.concatenate(res[6:9], axis=1)

    gw["w_ssm_br"] = _mm(dbr_ssm, y2, [F32], ta=True, name="ssm_br_dw")
    def glu_bwd_epi(dy, y1t, tt):
        sg = _sigmoid(tt)
        dt = dy * y1t.astype(F32) * sg * (1.0 - sg)
        return dt, dy * sg, _colsum(dt)
    dt_glu, dy1a, gs["b_glu"] = _mm(dbr_ssm, wb["w_ssm_br"], [BF16, F32], epi=glu_bwd_epi, mn=[y1, t_glu], n_sums=1,
                                    name="ssm_br_dx")
    gw["w_glu"] = _mm(y1, dt_glu, [F32], ta=True, name="ssm_glu_dw")

    def gelu_bwd_epi(acc, dy1t, y0t, ut):
        th, c0, c1 = _gelu_parts(y0t)
        dg = 0.5 * (1.0 + th) + 0.5 * y0t * (1.0 - th * th) * c0 * (1.0 + 3.0 * c1 * y0t * y0t)
        dy = (acc + dy1t) * dg
        return dy, _colsum(dy * ut.astype(F32))
    dy0, gs["ssm_d"] = _mm(dt_glu, wb["w_glu"], [F32], tb=True, epi=gelu_bwd_epi, mn=[dy1a, y0, u], n_sums=1,
                           name="ssm_glu_dx")
    dy0_s = _scan_order(dy0)
    lam, da, d_b, d_c = _ssm_scan(dy0_s, c_re_t, c_im_t, a_conj, reverse=True, s_fwd=s_all, u=u_s,
                                  name="ssm_scan_bwd")
    du = _time_order(_mm(lam, _tiled(b_re_t, b_im_t), [BF16], tb=True,
                         epi=lambda acc, dyt, dr: (acc + dyt * dr,), mn=[dy0_s], rows=[d_row], bd=SSM_BD, tm=2048, name="ssm_bu_dx"))
    gs["a_re"], gs["a_im"] = da[0], da[1]
    (dbr, dbi), (dcr, dci) = _untiled(d_b), _untiled(d_c)
    gs["bb_re"], gs["bb_im"] = _bd_diag(dbr).transpose(0, 2, 1), _bd_diag(dbi).transpose(0, 2, 1)
    gs["ssm_c_re"], gs["ssm_c_im"] = _bd_diag(dcr), -_bd_diag(dci)

    gw["w_attn_br"] = _mm(dbr_attn, o_att, [F32], ta=True, name="attn_br_dw")

    def do_epi(acc, ot):
        prod = acc * ot
        head = lax.broadcasted_iota(jnp.int32, prod.shape, 1) // ATT_E
        dd = jnp.zeros_like(prod)
        for h in range(ATT_HG):
            dd = jnp.where(head == h, jnp.sum(jnp.where(head == h, prod, 0.0), axis=1, keepdims=True), dd)
        return acc, dd
    do_att, dd_att = _mm(dbr_attn, wb["w_attn_br"], [BF16, F32], epi=do_epi, mn=[o_att], name="attn_br_dx")
    dq_l, dk_l, dv_l = [], [], []
    for g, d in enumerate(DILATIONS):
        nb = l // d // ATT_WIN
        qp, kp, vp = qkv_p[g]
        dq, dk, dv = _attn_bwd(qp, kp, vp, _to_perm(do_att, d), _to_perm(lse_tot, d), _to_perm(dd_att, d),
                               nb, "attn_bwd%d" % g)
        dq_l.append(_from_perm(dq, d))
        dk_l.append(_from_perm(dk, d))
        dv_l.append(_from_perm(dv, d))

    gw["w_mem_br"] = _mm(dbr_mem, mo, [F32], ta=True, name="mem_br_dw")
    dmo = _mm(dbr_mem, wb["w_mem_br"], [BF16], name="mem_br_dx")
    dmq, dkv = _mem_bwd(mq, kv, dmo, "mem_attn_bwd")
    gw["w_mem_kv"] = _mm(mn, dkv, [F32], ta=True, name="mem_kv_dw")
    dmn = _mm(dkv, wb["w_mem_kv"], [F32], tb=True, name="mem_kv_dx")
    token = sum(grads_ready(gi, gw) for gi in MIXER_GROUPS)
    gs["mem_norm_g"] = _rms_bwd(mem, dmn, None, sp["mem_norm_g"] + token[:1, :1], "rms_mem_bwd")[1]

    dza = jnp.concatenate([du] + dq_l + dk_l + dv_l + [dmq], axis=1)
    dn_a = _mm(dza, w_a, [F32], name="in_proj_a_dx", tk=1664)
    dw_a = _mm(dza, n1, [F32], ta=True, name="in_proj_a_dw", tm=1664)
    dw_g = [_mm(dzg[i], n1, [F32], ta=True, name="in_proj_g_dw%d" % i) for i in range(3)]
    gw["w_in"] = jnp.concatenate([dw_a] + dw_g, axis=0)
    token = grads_ready(0, gw) + small_grads_ready(gs)
    def in_dx_epi(acc, pt, xt, dht, g1):
        dx, dg = _rms_bwd_tile(xt, acc + pt, g1)
        return dx + dht, dg
    w_gs = [w_g[i * D_MODEL:(i + 1) * D_MODEL] for i in range(3)]
    grad_x, gs["norm1_g"] = _mm(dzg[0], w_gs[0], [F32], pair2=(dzg[1], w_gs[1], dzg[2], w_gs[2]), epi=in_dx_epi,
                                mn=[dn_a, x, dh1],
                                rows=[sp["norm1_g"] + token[:1, :1]], n_sums=1, tm=512, name="in_proj_g_dx")
    return loss, grad_x, gs


_SMALL_GRAD_ORDER = ("norm1_g", "mem_norm_g", "b_gate", "a_re", "a_im", "bb_re", "bb_im", "ssm_c_re", "ssm_c_im",
                     "ssm_d", "b_glu", "norm2_g", "final_g", "loss")


def kernel(x, mem, norm1_g, mem_norm_g, w_in, b_gate, ssm_lambda_re, ssm_lambda_im, ssm_log_dt, ssm_b_re, ssm_b_im, ssm_c_re, ssm_c_im, ssm_d, w_glu, b_glu, w_ssm_br, w_attn_br, w_mem_kv, w_mem_br, w_o, norm2_g, w_up, w_down, final_g, loss_target, m_norm1_g, m_mem_norm_g, m_w_in, m_b_gate, m_ssm_lambda_re, m_ssm_lambda_im, m_ssm_log_dt, m_ssm_b_re, m_ssm_b_im, m_ssm_c_re, m_ssm_c_im, m_ssm_d, m_w_glu, m_b_glu, m_w_ssm_br, m_w_attn_br, m_w_mem_kv, m_w_mem_br, m_w_o, m_norm2_g, m_w_up, m_w_down, m_final_g, v_norm1_g, v_mem_norm_g, v_w_in, v_b_gate, v_ssm_lambda_re, v_ssm_lambda_im, v_ssm_log_dt, v_ssm_b_re, v_ssm_b_im, v_ssm_c_re, v_ssm_c_im, v_ssm_d, v_w_glu, v_b_glu, v_w_ssm_br, v_w_attn_br, v_w_mem_kv, v_w_mem_br, v_w_o, v_norm2_g, v_w_up, v_w_down, v_final_g):
    args = dict(locals())
    w = {n: args[n] for n in ALL_W}
    m = {n: args["m_" + n] for n in ALL_W}
    v = {n: args["v_" + n] for n in ALL_W}
    my_c = lax.axis_index("c").astype(jnp.int32).reshape(1)
    my_chip = (2 * lax.axis_index("x") + lax.axis_index("y")).astype(jnp.int32).reshape(1)

    w_pack = [_pack_group(w, names) for names in GROUPS]
    my_index = (4 * lax.axis_index("x") + 2 * lax.axis_index("y") + lax.axis_index("c")).astype(jnp.int32)
    zero = jnp.zeros((), jnp.int32)
    w_all = _allgather(w_pack[0].astype(BF16), "allgather_weights0")
    wb = {n: _full_stored(part, n) for n, part in _split_group(w_all, GROUPS[0]).items()}
    gathers = {gi: _split_start(w_pack[gi].astype(BF16), N_DEV, _gather_plan, w_all, "weights_gather_start%d" % gi)
               for gi in range(1, len(GROUPS))}

    def gathered(started, after, name):
        sems, src, land, _ = started
        src, land = _split_wait(sems, src, land, _gather_wait_plan, after, name)
        return lax.dynamic_update_slice(land, src[None], (my_index, zero, zero))

    def late_weights(gi, after):
        full = gathered(gathers[gi], after, "weights_gather_wait%d" % gi)
        return {n: _full_stored(part, n) for n, part in _split_group(full, GROUPS[gi]).items()}

    pending = {}

    def grads_ready(gi, grads):
        g_pack = jnp.concatenate([_stacked_stored(grads[n], n) for n in GROUPS[gi]], axis=1)
        if gi == 0:
            t1 = _pair_exchange(g_pack, "grad_pair_exchange%d" % gi)
            p_sum, p_bf = _pair_sum(g_pack, t1, my_c, "grad_pair_sum%d" % gi, GROUP_TR[gi])
            started = _split_start(p_bf, 3, _chip_plan, p_sum, "grad_chip_exchange_start%d" % gi)
            pending[gi] = (p_sum, my_chip, started, _chip_plan)
        else:
            started = _split_start(g_pack.astype(BF16), N_DEV - 1, _owner_plan, g_pack, "grad_exchange_start%d" % gi)
            pending[gi] = (g_pack, my_index.reshape(1), started, _owner_plan)
        return started[3]

    early_small = [n for n in _SMALL_GRAD_ORDER if n != "norm1_g"]
    small_started = []

    def small_grads_ready(gs):
        started = _split_start(_pack([gs[n] for n in early_small]), N_DEV, _gather_plan, gs["mem_norm_g"],
                               "small_grads_gather_start")
        small_started.append((started, [gs[n].shape for n in early_small]))
        return started[3]

    sp = {
        "norm1_g": norm1_g + sum(started[3][:1, :1] for started in gathers.values()), "mem_norm_g": mem_norm_g, "b_gate": b_gate, "b_glu": b_glu, "norm2_g": norm2_g,
        "final_g": final_g.reshape(1, D_MODEL),
        "ssm_lambda_re": ssm_lambda_re[0], "ssm_lambda_im": ssm_lambda_im[0], "ssm_log_dt": ssm_log_dt[0],
        "ssm_b_re": ssm_b_re[0], "ssm_b_im": ssm_b_im[0], "ssm_c_re": ssm_c_re[0], "ssm_c_im": ssm_c_im[0],
        "ssm_d": ssm_d[0],
    }
    loss, grad_x, gs = _local_step(x[0], mem[0], loss_target[0], wb, sp, late_weights, grads_ready,
                                     small_grads_ready)
    n1_started = _split_start(_pack([gs["norm1_g"]]), N_DEV, _gather_plan, grad_x, "norm1_grad_gather_start")

    big_g = {}
    for gi, names in enumerate(GROUPS):
        own, own_index, (sems, src, land, _), plan = pending[gi]
        recv = _split_wait(sems, src, land, plan, grad_x, "grad_exchange_wait%d" % gi)[1]
        g_pack = _grad_sum(own, own_index, recv, "grad_sum%d" % gi, GROUP_TR[gi])
        for n, part in _split_group(g_pack, names).items():
            big_g[n] = _unstored(part, n)
    rows_of = lambda d, names: [d[n].reshape(d[n].shape[-2:]) for n in names]
    big_out = _adam_many(rows_of(big_g, BIG), rows_of(w, BIG), rows_of(m, BIG), rows_of(v, BIG), 8, "adam_big")
    big = [big_g] + [{n: a[None] for n, a in zip(BIG, outs)} for outs in big_out]

    (sg_started, sg_shapes), = small_started
    sg_all = jnp.concatenate([gathered(sg_started, big_out[0][0], "small_grads_gather_wait"),
                              gathered(n1_started, big_out[0][0], "norm1_grad_gather_wait")], axis=1)
    sg_sum = _sum8(sg_all, "sum_small_grads")
    n1_rows = n1_started[1].shape[0]
    sg = dict(zip(early_small, _unpack(sg_sum[:-n1_rows], sg_shapes)))
    sg["norm1_g"] = _unpack(sg_sum[-n1_rows:], [gs["norm1_g"].shape])[0]
    _, disc_vjp = jax.vjp(_discretize, sp["ssm_lambda_re"], sp["ssm_lambda_im"], sp["ssm_log_dt"],
                          sp["ssm_b_re"], sp["ssm_b_im"])
    d_lre, d_lim, d_ldt, d_bre, d_bim = disc_vjp((sg["a_re"].reshape(SSM_G, SSM_P), sg["a_im"].reshape(SSM_G, SSM_P),
                                                  sg["bb_re"], sg["bb_im"]))
    small_grad = {
        "norm1_g": sg["norm1_g"], "mem_norm_g": sg["mem_norm_g"], "b_gate": sg["b_gate"],
        "ssm_lambda_re": d_lre, "ssm_lambda_im": d_lim, "ssm_log_dt": d_ldt, "ssm_b_re": d_bre, "ssm_b_im": d_bim,
        "ssm_c_re": sg["ssm_c_re"], "ssm_c_im": sg["ssm_c_im"], "ssm_d": sg["ssm_d"], "b_glu": sg["b_glu"],
        "norm2_g": sg["norm2_g"], "final_g": sg["final_g"],
    }
    small_grad = {n: small_grad[n].reshape(w[n].shape) for n in SMALL}

    def squeezed(a):
        return a.reshape(a.shape[1:]) if a.ndim > 2 else a.reshape(1, -1)

    sq = lambda d: [squeezed(d[n]) for n in SMALL]
    small_out = _adam_many(sq(small_grad), sq(w), sq(m), sq(v), 1, "adam_small")
    small = [small_grad] + [{n: a.reshape(w[n].shape) for n, a in zip(SMALL, outs)} for outs in small_out]

    outs = [sg["loss"][0, 0], grad_x[None]]
    for kind in range(4):
        for n in ALL_W:
            outs.append(big[kind][n] if n in BIG else small[kind][n])
    return tuple(outs)
```

```python
import math

import numpy as np
import jax
import jax.numpy as jnp
from jax import lax
from jax.experimental import pallas as pl
from jax.experimental.pallas import tpu as pltpu

F32 = jnp.float32
BF16 = jnp.bfloat16
_MXU = jnp.bfloat16

D_MODEL = 1024
SSM_G, SSM_H, SSM_P = 32, 16, 64
SSM_W = SSM_G * SSM_H
SSM_S = SSM_G * SSM_P
SSM_BD = 4
ATT_E = 64
ATT_HG = 4
ATT_GW = ATT_HG * ATT_E
ATT_WIN = 128
ATT_QB = 8
ATT_QB_FWD = 4
DILATIONS = (1, 4, 16)
MEM_H, MEM_E = 4, 128
MEM_W = MEM_H * MEM_E
ZA_W = SSM_W + 9 * ATT_GW + MEM_W
ZG_W = 3 * D_MODEL
IN_W = ZA_W + ZG_W
RMS_EPS = 1e-6
NEG_INF = -1e30

ADAM_LR, ADAM_B1, ADAM_B2, ADAM_EPS, ADAM_WD, ADAM_STEP = 0.001, 0.9, 0.999, 1e-08, 0.01, 10

N_DEV = 8
PACK_C = 512
_VMEM_LIMIT = 56 * 1024 * 1024
SUBLANES = 16
SCAN_SEG = 128
SCAN_CHAINS = 4
SCAN_UNROLL = 8
SCAN_W = 128

BIG = ("w_in", "w_glu", "w_ssm_br", "w_attn_br", "w_mem_kv", "w_mem_br", "w_o", "w_up", "w_down")
BIG_SHAPE = {
    "w_in": (D_MODEL, IN_W, 1), "w_glu": (SSM_W, SSM_W, 0), "w_ssm_br": (SSM_W, D_MODEL, 1),
    "w_attn_br": (ATT_GW, D_MODEL, 1), "w_mem_kv": (D_MODEL, 2 * MEM_W, 0), "w_mem_br": (MEM_W, D_MODEL, 1),
    "w_o": (D_MODEL, D_MODEL, 0), "w_up": (D_MODEL, 4 * D_MODEL, 1), "w_down": (4 * D_MODEL, D_MODEL, 0),
}
SMALL = ("norm1_g", "mem_norm_g", "b_gate", "ssm_lambda_re", "ssm_lambda_im", "ssm_log_dt", "ssm_b_re",
         "ssm_b_im", "ssm_c_re", "ssm_c_im", "ssm_d", "b_glu", "norm2_g", "final_g")
ALL_W = ("norm1_g", "mem_norm_g", "w_in", "b_gate", "ssm_lambda_re", "ssm_lambda_im", "ssm_log_dt", "ssm_b_re",
         "ssm_b_im", "ssm_c_re", "ssm_c_im", "ssm_d", "w_glu", "b_glu", "w_ssm_br", "w_attn_br", "w_mem_kv",
         "w_mem_br", "w_o", "norm2_g", "w_up", "w_down", "final_g")


def _params(sem):
    return pltpu.CompilerParams(dimension_semantics=sem, vmem_limit_bytes=_VMEM_LIMIT)


def _pick(n, cap):
    if n <= cap:
        return n
    t = (cap // 128) * 128
    while n % t:
        t -= 128
    return t


def _mm(a, b, outs, *, name, ta=False, tb=False, epi=None, mn=(), rows=(), pair2=None, bd=0, n_sums=0,
        tm=1024, tn=1024, tk=2048):
    ab = [a, b] + (list(pair2) if pair2 is not None else [])
    a_shape, b_shape = ab[0].shape, ab[1].shape
    m = a_shape[1] if ta else a_shape[0]
    k = a_shape[0] if ta else a_shape[1]
    n = b_shape[0] if tb else b_shape[1]
    assert k == (b_shape[1] if tb else b_shape[0]), (name, a_shape, b_shape)
    out_n = n
    if bd and ta:
        assert not tb
        tm, tn, tk = m // bd, n // bd, _pick(k, tk)
        grid, out_n = (bd, 1, k // tk), tn
        a_blk = ((tk, tm), lambda i, j, kk: (kk, i))
        b_blk = ((tk, tn), lambda i, j, kk: (kk, i))
        mn_spec = pl.BlockSpec((tm, tn), lambda i, j, kk: (i, 0))
    elif bd:
        tm, tn, tk = _pick(m, tm), n // bd, k // bd
        grid = (m // tm, bd, 1)
        a_blk = ((tm, tk), lambda i, j, kk: (i, j))
        b_blk = ((tn, tk) if tb else (tk, tn), lambda i, j, kk: (j, j))
        mn_spec = pl.BlockSpec((tm, tn), lambda i, j, kk: (i, j))
    else:
        tm, tn, tk = _pick(m, tm), _pick(n, tn), _pick(k, tk)
        grid = (m // tm, n // tn, k // tk)
        a_blk = ((tk, tm), lambda i, j, kk: (kk, i)) if ta else ((tm, tk), lambda i, j, kk: (i, kk))
        b_blk = ((tn, tk), lambda i, j, kk: (j, kk)) if tb else ((tk, tn), lambda i, j, kk: (kk, j))
        mn_spec = pl.BlockSpec((tm, tn), lambda i, j, kk: (i, j))

    ab_specs = [pl.BlockSpec(*(a_blk if q % 2 == 0 else b_blk)) for q in range(len(ab))]
    mn_arrays = [e[0] if isinstance(e, tuple) else e for e in mn]
    mn_specs = [pl.BlockSpec((tm, tn), lambda i, j, kk, c=e[1]: (i, c)) if isinstance(e, tuple) else mn_spec
                for e in mn]
    nk = grid[2]
    row_spec = pl.BlockSpec((1, tn), lambda i, j, kk: (0, j))
    n_ex, n_out = len(mn) + len(rows), len(outs)
    assert n_sums == 0 or (grid[1] == 1 and not bd)
    dims = (((0 if ta else 1,), (1 if tb else 0,)), ((), ()))

    def body(*refs):
        ab_refs, rest = refs[:len(ab)], refs[len(ab):]
        ex, o_refs = rest[:n_ex], rest[n_ex:n_ex + n_out]
        s_refs = rest[n_ex + n_out:n_ex + n_out + n_sums]
        first_row_tile = pl.program_id(0) == 0
        kk = pl.program_id(2)

        pairs = list(zip(ab_refs[0::2], ab_refs[1::2]))

        def product(pair):
            return lax.dot_general(pair[0][...].astype(_MXU), pair[1][...].astype(_MXU), dims,
                                   preferred_element_type=F32)

        def finish(total):
            vals = (total,) if epi is None else epi(total, *[r[...] for r in ex])
            for r, v in zip(o_refs, vals):
                r[...] = v.astype(r.dtype)
            for r, v in zip(s_refs, vals[n_out:]):
                r[...] = jnp.where(first_row_tile, v, r[...] + v)

        if nk == 1:
            total = product(pairs[0])
            for pair in pairs[1:]:
                total = total + product(pair)
            finish(total)
        else:
            acc = rest[-1]

            @pl.when(kk == 0)
            def _():
                acc[...] = jnp.zeros_like(acc)

            for pair in pairs:
                acc[...] += product(pair)

            @pl.when(kk == nk - 1)
            def _():
                finish(acc[...])

    res = pl.pallas_call(
        body, grid=grid,
        in_specs=ab_specs + mn_specs + [row_spec] * len(rows),
        out_specs=[mn_spec] * n_out + [row_spec] * n_sums,
        out_shape=[jax.ShapeDtypeStruct((m, out_n), dt) for dt in outs]
        + [jax.ShapeDtypeStruct((1, out_n), F32)] * n_sums,
        scratch_shapes=[pltpu.VMEM((tm, tn), F32)] if nk > 1 else [],
        compiler_params=_params(("arbitrary" if n_sums else "parallel", "parallel", "arbitrary")), name=name,
    )(*ab, *mn_arrays, *rows)
    return res[0] if n_out + n_sums == 1 else res


def _ew(fn, rows, bcs, out_rows, out_accs, *, name, tm=256):
    r = rows[0].shape[0]
    tm = min(tm, r)
    assert r % tm == 0
    nr, nb, no, na = len(rows), len(bcs), len(out_rows), len(out_accs)

    def body(*refs):
        i = pl.program_id(0)
        r_in, b_in = refs[:nr], refs[nr:nr + nb]
        o_r, o_a = refs[nr + nb:nr + nb + no], refs[nr + nb + no:]
        outs, accs = fn([x[...] for x in r_in], [x[...] for x in b_in])
        for ref, v in zip(o_r, outs):
            ref[...] = v.astype(ref.dtype)
        if na:
            @pl.when(i == 0)
            def _():
                for ref in o_a:
                    ref[...] = jnp.zeros_like(ref)

            for ref, v in zip(o_a, accs):
                ref[...] += v

    res = pl.pallas_call(
        body, grid=(r // tm,),
        in_specs=[pl.BlockSpec((tm, x.shape[1]), lambda i: (i, 0)) for x in rows]
        + [pl.BlockSpec((1, x.shape[1]), lambda i: (0, 0)) for x in bcs],
        out_specs=[pl.BlockSpec((tm, c), lambda i: (i, 0)) for c, _ in out_rows]
        + [pl.BlockSpec((1, c), lambda i: (0, 0)) for c in out_accs],
        out_shape=[jax.ShapeDtypeStruct((r, c), dt) for c, dt in out_rows]
        + [jax.ShapeDtypeStruct((1, c), F32) for c in out_accs],
        compiler_params=_params(("arbitrary",)), name=name,
    )(*rows, *bcs)
    return res


def _colsum(x):
    return jnp.sum(x, axis=0, keepdims=True)


def _sigmoid(x):
    return 1.0 / (1.0 + jnp.exp(-x))


def _rms_bwd_tile(xv, dv, g):
    rs = lax.rsqrt(jnp.mean(xv * xv, axis=-1, keepdims=True) + RMS_EPS)
    gd = dv * g
    dx = rs * gd - xv * (rs * rs * rs) * jnp.mean(gd * xv, axis=-1, keepdims=True)
    return dx, _colsum(dv * xv * rs)


def _rms_fwd(x, g, name):
    def fn(r, b):
        xv = r[0]
        rs = lax.rsqrt(jnp.mean(xv * xv, axis=-1, keepdims=True) + RMS_EPS)
        return [xv * rs * b[0]], []
    return _ew(fn, [x], [g], [(x.shape[1], BF16)], [], name=name, tm=1024)[0]


def _rms_bwd(x, dn, res, g, name):
    def fn(r, b):
        dx, dg = _rms_bwd_tile(r[0], r[1], b[0])
        if res is not None:
            dx = dx + r[2]
        return [dx], [dg]
    rows = [x, dn] + ([res] if res is not None else [])
    return _ew(fn, rows, [g], [(x.shape[1], F32)], [x.shape[1]], name=name)


def _scan_order(x):
    l, c = x.shape
    return x.reshape(l // (SUBLANES * SCAN_SEG), SUBLANES, SCAN_SEG, c).transpose(0, 2, 1, 3).reshape(l, c)


def _time_order(x):
    l, c = x.shape
    return x.reshape(l // (SUBLANES * SCAN_SEG), SCAN_SEG, SUBLANES, c).transpose(0, 2, 1, 3).reshape(l, c)


def _ssm_scan(x, w_re, w_im, a_pair, *, reverse, s_fwd=None, u=None, name):
    l = x.shape[0]
    seg, w = SCAN_SEG, SCAN_W
    bd_w = SSM_W // SSM_BD
    tiles_per_bd = SSM_S // SSM_BD // w
    nch = min(SCAN_CHAINS, l // (SUBLANES * seg))
    chain_rows = SUBLANES * seg
    tb = nch * chain_rows
    nt = l // tb
    with_da = s_fwd is not None
    assert reverse or not with_da

    def tt(t):
        return nt - 1 - t if reverse else t

    def body(*refs):
        if with_da:
            (x_ref, wr_ref, wi_ref, a_ref, sf_ref, sp_ref, u_ref, s_ref, da_ref, dw_ref, dx_ref,
             p_ref, c_ref, b_scr, l_scr) = refs
        else:
            x_ref, wr_ref, wi_ref, a_ref, s_ref, p_ref, c_ref, b_scr, l_scr = refs
        t_blk = pl.program_id(1)
        ar, ai = a_ref[0], a_ref[1]

        @pl.when(t_blk == 0)
        def _():
            def pstep(i, carry):
                pr, pi = carry
                p_ref[0, pl.ds(i, 1), :] = pr
                p_ref[1, pl.ds(i, 1), :] = pi
                return pr * ar - pi * ai, pr * ai + pi * ar

            lax.fori_loop(0, seg, pstep, (ar, ai))
            c_ref[...] = jnp.zeros_like(c_ref)
            if with_da:
                da_ref[...] = jnp.zeros_like(da_ref)
                dw_ref[...] = jnp.zeros_like(dw_ref)
                dx_ref[...] = jnp.zeros_like(dx_ref)

        xb = x_ref[...].astype(_MXU)
        b_scr[:, :w] = jnp.dot(xb, wr_ref[...], preferred_element_type=F32)
        b_scr[:, w:] = jnp.dot(xb, wi_ref[...], preferred_element_type=F32)
        arb, aib = jnp.broadcast_to(ar, (SUBLANES, w)), jnp.broadcast_to(ai, (SUBLANES, w))
        zero = jnp.zeros((SUBLANES, w), F32)

        def tile(g, step):
            return pl.ds(pl.multiple_of(g * chain_rows + step * SUBLANES, SUBLANES), SUBLANES)

        def rows(g, i):
            return tile(g, seg - 1 - i if reverse else i)

        def local_step(i, carry):
            out = []
            for g in range(nch):
                sr, si = carry[2 * g], carry[2 * g + 1]
                idx = rows(g, i)
                sr, si = arb * sr - aib * si + b_scr[idx, :w], arb * si + aib * sr + b_scr[idx, w:]
                l_scr[idx, :w] = sr
                l_scr[idx, w:] = si
                out += [sr, si]
            return tuple(out)

        def unrolled(step_fn, first):
            def trip(q, carry):
                for r in range(SCAN_UNROLL):
                    carry = step_fn(first + q * SCAN_UNROLL + r, carry)
                return carry
            return trip

        ends = lax.fori_loop(0, seg // SCAN_UNROLL, unrolled(local_step, 0), (zero,) * (2 * nch))

        a_seg_r, a_seg_i = p_ref[0, seg - 1:seg, :], p_ref[1, seg - 1:seg, :]
        cr, ci = c_ref[0], c_ref[1]
        sub = lax.broadcasted_iota(jnp.int32, (SUBLANES, w), 0)
        ins = [[zero, zero] for _ in range(nch)]
        order = [(g, k) for g in range(nch) for k in range(SUBLANES)]
        for g, k in (order[::-1] if reverse else order):
            ins[g] = [jnp.where(sub == k, cr, ins[g][0]), jnp.where(sub == k, ci, ins[g][1])]
            er, ei = ends[2 * g][k:k + 1], ends[2 * g + 1][k:k + 1]
            cr, ci = er + a_seg_r * cr - a_seg_i * ci, ei + a_seg_r * ci + a_seg_i * cr
        c_ref[0] = cr
        c_ref[1] = ci

        def fix(g, i):
            idx = rows(g, i)
            pr, pi = p_ref[0, pl.ds(i, 1), :], p_ref[1, pl.ds(i, 1), :]
            sr = l_scr[idx, :w] + pr * ins[g][0] - pi * ins[g][1]
            si = l_scr[idx, w:] + pr * ins[g][1] + pi * ins[g][0]
            s_ref[idx, :w] = sr.astype(s_ref.dtype)
            s_ref[idx, w:] = si.astype(s_ref.dtype)
            return sr, si

        if not with_da:
            def fix_step(i, carry):
                for g in range(nch):
                    fix(g, i)
                return carry

            lax.fori_loop(0, seg // SCAN_UNROLL, unrolled(fix_step, 0), 0)
        else:
            def adj_step(i, acc):
                acc_r, acc_i = acc
                for g in range(nch):
                    lr, li = fix(g, i)
                    prev = tile(g, seg - 2 - i)
                    fr, fi = sf_ref[prev, :w].astype(F32), sf_ref[prev, w:].astype(F32)
                    acc_r, acc_i = acc_r + lr * fr + li * fi, acc_i + li * fr - lr * fi
                return acc_r, acc_i

            acc = lax.fori_loop(0, seg // SCAN_UNROLL - 1, unrolled(adj_step, 0), (zero, zero))
            for i in range(seg - SCAN_UNROLL, seg - 1):
                acc = adj_step(i, acc)
            acc_r, acc_i = acc
            first_block = tt(t_blk) == 0
            for g in range(nch):
                lr, li = fix(g, seg - 1)
                seg_ends = tile(g, seg - 1)
                if g == 0:
                    pvr = jnp.where(first_block, 0.0, sp_ref[SUBLANES - 1:SUBLANES, :w].astype(F32))
                    pvi = jnp.where(first_block, 0.0, sp_ref[SUBLANES - 1:SUBLANES, w:].astype(F32))
                else:
                    pvr = sf_ref[g * chain_rows - 1:g * chain_rows, :w].astype(F32)
                    pvi = sf_ref[g * chain_rows - 1:g * chain_rows, w:].astype(F32)
                fr = jnp.where(sub == 0, pvr, pltpu.roll(sf_ref[seg_ends, :w].astype(F32), 1, 0))
                fi = jnp.where(sub == 0, pvi, pltpu.roll(sf_ref[seg_ends, w:].astype(F32), 1, 0))
                acc_r = acc_r + lr * fr + li * fi
                acc_i = acc_i + li * fr - lr * fi
            da_ref[0] += jnp.sum(acc_r, axis=0, keepdims=True)
            da_ref[1] += jnp.sum(acc_i, axis=0, keepdims=True)
            dw_ref[...] += _tn_dot(u_ref[...], s_ref[...])
            dx_ref[...] += _tn_dot(xb, sf_ref[...])

    x_spec = pl.BlockSpec((tb, bd_w), lambda j, t: (tt(t), j // tiles_per_bd))
    w_spec = pl.BlockSpec((bd_w, w), lambda j, t: (j // tiles_per_bd, j))
    d_spec = pl.BlockSpec((bd_w, 2 * w), lambda j, t: (j // tiles_per_bd, j % tiles_per_bd))
    a_spec = pl.BlockSpec((2, 1, w), lambda j, t: (0, 0, j))
    s_spec = pl.BlockSpec((tb, 2 * w), lambda j, t: (tt(t), j))
    in_specs, args = [x_spec, w_spec, w_spec, a_spec], [x, w_re, w_im, a_pair]
    out_specs, out_shape = [s_spec], [jax.ShapeDtypeStruct((l, 2 * SSM_S), BF16)]
    scratch = [pltpu.VMEM((2, seg, w), F32), pltpu.VMEM((2, 1, w), F32)] + [pltpu.VMEM((tb, 2 * w), F32)] * 2
    if with_da:
        in_specs += [s_spec, pl.BlockSpec((SUBLANES, 2 * w),
                                          lambda j, t: (jnp.maximum(tt(t) * (tb // SUBLANES) - 1, 0), j)),
                     x_spec]
        args += [s_fwd, s_fwd, u]
        out_specs += [a_spec, d_spec, d_spec]
        out_shape += ([jax.ShapeDtypeStruct((2, 1, SSM_S), F32)]
                      + [jax.ShapeDtypeStruct((SSM_W, 2 * SSM_S // SSM_BD), F32)] * 2)
    res = pl.pallas_call(
        body, grid=(SSM_S // w, nt), in_specs=in_specs, out_specs=out_specs, out_shape=out_shape,
        scratch_shapes=scratch, compiler_params=_params(("parallel", "arbitrary")), name=name,
    )(*args)
    return res if with_da else res[0]


def _nt_dot(x, y):
    return lax.dot_general(x.astype(_MXU), y.astype(_MXU), (((1,), (1,)), ((), ())), preferred_element_type=F32)


def _tn_dot(x, y):
    return lax.dot_general(x.astype(_MXU), y.astype(_MXU), (((0,), (0,)), ((), ())), preferred_element_type=F32)


def _nn_dot(x, y):
    return jnp.dot(x.astype(_MXU), y.astype(_MXU), preferred_element_type=F32)


def _attn_mask2(gb, nb):
    qi = lax.broadcasted_iota(jnp.int32, (ATT_WIN, 2 * ATT_WIN), 0)
    c = lax.broadcasted_iota(jnp.int32, (ATT_WIN, 2 * ATT_WIN), 1)
    has_prev = (gb % nb) != 0
    prev_ok = jnp.logical_and(jnp.logical_and(c < ATT_WIN, c >= qi), has_prev)
    own_ok = jnp.logical_and(c >= ATT_WIN, c - ATT_WIN <= qi)
    return jnp.logical_or(prev_ok, own_ok)


def _attn_specs(qb):
    cur = pl.BlockSpec((qb * ATT_WIN, ATT_GW), lambda i: (i, 0))
    prev = pl.BlockSpec((ATT_WIN, ATT_GW), lambda i: (jnp.maximum(qb * i - 1, 0), 0))
    return cur, prev


def _attn_fwd(q, k, v, nb, name):
    l = q.shape[0]
    scale = ATT_E ** -0.5
    w = ATT_WIN

    qb = ATT_QB_FWD

    def body(q_ref, kc_ref, kp_ref, vc_ref, vp_ref, o_ref, lse_ref):
        i = pl.program_id(0)
        masks = [_attn_mask2(qb * i + b, nb) for b in range(qb)]
        for h in range(ATT_HG):
            sl = slice(h * ATT_E, (h + 1) * ATT_E)
            k_ext = jnp.concatenate([kp_ref[:, sl], kc_ref[:, sl]], axis=0)
            v_ext = jnp.concatenate([vp_ref[:, sl], vc_ref[:, sl]], axis=0)
            for b in range(qb):
                r, kr = slice(b * w, (b + 1) * w), slice(b * w, (b + 2) * w)
                s = jnp.where(masks[b], _nt_dot(q_ref[r, sl], k_ext[kr]) * scale, NEG_INF)
                mx = jnp.max(s, axis=-1, keepdims=True)
                p = jnp.exp(s - mx)
                den = jnp.sum(p, axis=-1, keepdims=True)
                o_ref[r, sl] = _nn_dot(p, v_ext[kr]) / den
                lse_ref[r, sl] = jnp.broadcast_to(mx + jnp.log(den), (w, ATT_E))

    cur, prev = _attn_specs(qb)
    return pl.pallas_call(
        body, grid=(l // (qb * w),), in_specs=[cur, cur, prev, cur, prev], out_specs=[cur, cur],
        out_shape=[jax.ShapeDtypeStruct((l, ATT_GW), F32)] * 2,
        compiler_params=_params(("parallel",)), name=name,
    )(q, k, k, v, v)


def _attn_bwd(q, k, v, do, lse, dd, nb, name):
    l = q.shape[0]
    scale = ATT_E ** -0.5
    w = ATT_WIN
    nblk = l // w

    def body(q_ref, kc_ref, kp_ref, vc_ref, vp_ref, do_ref, lse_ref, dd_ref, qn_ref, don_ref, lsen_ref, ddn_ref,
             dq_ref, dk_ref, dv_ref, dk_acc, dv_acc):
        i = pl.program_id(0)
        masks = [_attn_mask2(ATT_QB * i + b, nb) for b in range(ATT_QB)]
        nxt = ATT_QB * (i + 1)
        nxt_attends = jnp.logical_and(nxt < nblk, (nxt % nb) != 0)
        qi = lax.broadcasted_iota(jnp.int32, (w, w), 0)
        kj = lax.broadcasted_iota(jnp.int32, (w, w), 1)
        mask_n = jnp.logical_and(kj >= qi, nxt_attends)
        dk_acc[...] = jnp.zeros_like(dk_acc)
        dv_acc[...] = jnp.zeros_like(dv_acc)
        for h in range(ATT_HG):
            sl, col = slice(h * ATT_E, (h + 1) * ATT_E), slice(h * ATT_E, h * ATT_E + 1)
            k_ext = jnp.concatenate([kp_ref[:, sl], kc_ref[:, sl]], axis=0)
            v_ext = jnp.concatenate([vp_ref[:, sl], vc_ref[:, sl]], axis=0)
            for b in range(ATT_QB):
                r, kr = slice(b * w, (b + 1) * w), slice(b * w, (b + 2) * w)
                qh, doh, k2, v2 = q_ref[r, sl], do_ref[r, sl], k_ext[kr], v_ext[kr]
                p = jnp.where(masks[b], jnp.exp(_nt_dot(qh, k2) * scale - lse_ref[r, col]), 0.0)
                ds = p * (_nt_dot(doh, v2) - dd_ref[r, col]) * scale
                dq_ref[r, sl] = _nn_dot(ds, k2).astype(dq_ref.dtype)
                dk2, dv2 = _tn_dot(ds, qh), _tn_dot(p, doh)
                dk_acc[r, sl] += dk2[w:]
                dv_acc[r, sl] += dv2[w:]
                if b > 0:
                    rp = slice((b - 1) * w, b * w)
                    dk_acc[rp, sl] += dk2[:w]
                    dv_acc[rp, sl] += dv2[:w]
            last = slice((ATT_QB - 1) * w, ATT_QB * w)
            kl, vl, qn, don = kc_ref[last, sl], vc_ref[last, sl], qn_ref[:, sl], don_ref[:, sl]
            pn = jnp.where(mask_n, jnp.exp(_nt_dot(qn, kl) * scale - lsen_ref[:, col]), 0.0)
            dsn = pn * (_nt_dot(don, vl) - ddn_ref[:, col]) * scale
            dk_acc[last, sl] += _tn_dot(dsn, qn)
            dv_acc[last, sl] += _tn_dot(pn, don)
        dk_ref[...] = dk_acc[...].astype(dk_ref.dtype)
        dv_ref[...] = dv_acc[...].astype(dv_ref.dtype)

    cur, prev = _attn_specs(ATT_QB)
    nxt_spec = pl.BlockSpec((w, ATT_GW), lambda i: (jnp.minimum(ATT_QB * (i + 1), nblk - 1), 0))
    return pl.pallas_call(
        body, grid=(l // (ATT_QB * w),),
        in_specs=[cur, cur, prev, cur, prev, cur, cur, cur, nxt_spec, nxt_spec, nxt_spec, nxt_spec],
        out_specs=[cur] * 3, out_shape=[jax.ShapeDtypeStruct((l, ATT_GW), BF16)] * 3,
        scratch_shapes=[pltpu.VMEM((ATT_QB * w, ATT_GW), F32)] * 2,
        compiler_params=_params(("parallel",)), name=name,
    )(q, k, k, v, v, do, lse, dd, q, do, lse, dd)


def _to_perm(a, d):
    if d == 1:
        return a
    l, c = a.shape
    return a.reshape(l // d, d, c).transpose(1, 0, 2).reshape(l, c)


def _from_perm(a, d):
    if d == 1:
        return a
    l, c = a.shape
    return a.reshape(d, l // d, c).transpose(1, 0, 2).reshape(l, c)


def _merged_attn_proj(o_g, lse_g, w_t, name, tm=1024):
    l = o_g[0].shape[0]
    ng = len(o_g)

    def body(*refs):
        o_refs, l_refs, (w_ref, oa_ref, lt_ref, br_ref) = refs[:ng], refs[ng:2 * ng], refs[2 * ng:]
        ls = [r[...] for r in l_refs]
        mx = ls[0]
        for lv in ls[1:]:
            mx = jnp.maximum(mx, lv)
        es = [jnp.exp(lv - mx) for lv in ls]
        tot, acc = es[0], es[0] * o_refs[0][...]
        for e, o_ref in zip(es[1:], o_refs[1:]):
            tot, acc = tot + e, acc + e * o_ref[...]
        o = acc / tot
        oa_ref[...] = o
        lt_ref[...] = mx + jnp.log(tot)
        br_ref[...] = _nt_dot(o, w_ref[...]).astype(br_ref.dtype)

    row = pl.BlockSpec((tm, ATT_GW), lambda i: (i, 0))
    return pl.pallas_call(
        body, grid=(l // tm,),
        in_specs=[row] * (2 * ng) + [pl.BlockSpec(w_t.shape, lambda i: (0, 0))],
        out_specs=[row, row, pl.BlockSpec((tm, w_t.shape[0]), lambda i: (i, 0))],
        out_shape=[jax.ShapeDtypeStruct((l, ATT_GW), F32)] * 2 + [jax.ShapeDtypeStruct((l, w_t.shape[0]), BF16)],
        compiler_params=_params(("parallel",)), name=name,
    )(*o_g, *lse_g, w_t)


def _mem_probs(qh, kh):
    s = _nt_dot(qh, kh) * (MEM_E ** -0.5)
    e = jnp.exp(s - jnp.max(s, axis=-1, keepdims=True))
    return e / jnp.sum(e, axis=-1, keepdims=True)


def _mem_fwd(mq, kv, name, tm=1024):
    l, nm = mq.shape[0], kv.shape[0]

    def body(q_ref, kv_ref, o_ref):
        for h in range(MEM_H):
            sl = slice(h * MEM_E, (h + 1) * MEM_E)
            p = _mem_probs(q_ref[:, sl], kv_ref[:, sl])
            o_ref[:, sl] = _nn_dot(p, kv_ref[:, MEM_W + h * MEM_E:MEM_W + (h + 1) * MEM_E]).astype(o_ref.dtype)

    return pl.pallas_call(
        body, grid=(l // tm,),
        in_specs=[pl.BlockSpec((tm, MEM_W), lambda i: (i, 0)), pl.BlockSpec((nm, 2 * MEM_W), lambda i: (0, 0))],
        out_specs=pl.BlockSpec((tm, MEM_W), lambda i: (i, 0)),
        out_shape=jax.ShapeDtypeStruct((l, MEM_W), BF16),
        compiler_params=_params(("parallel",)), name=name,
    )(mq, kv)


def _mem_bwd(mq, kv, dmo, name, tm=1024):
    l, nm = mq.shape[0], kv.shape[0]
    scale = MEM_E ** -0.5

    def body(q_ref, kv_ref, do_ref, dq_ref, dkv_ref):
        @pl.when(pl.program_id(0) == 0)
        def _():
            dkv_ref[...] = jnp.zeros_like(dkv_ref)

        for h in range(MEM_H):
            sl = slice(h * MEM_E, (h + 1) * MEM_E)
            vsl = slice(MEM_W + h * MEM_E, MEM_W + (h + 1) * MEM_E)
            qh, kh, vh, doh = q_ref[:, sl], kv_ref[:, sl], kv_ref[:, vsl], do_ref[:, sl]
            p = _mem_probs(qh, kh)
            dp = _nt_dot(doh, vh)
            ds = p * (dp - jnp.sum(dp * p, axis=-1, keepdims=True)) * scale
            dq_ref[:, sl] = _nn_dot(ds, kh).astype(dq_ref.dtype)
            dkv_ref[:, sl] += _tn_dot(ds, qh)
            dkv_ref[:, vsl] += _tn_dot(p, doh)

    row = pl.BlockSpec((tm, MEM_W), lambda i: (i, 0))
    full = pl.BlockSpec((nm, 2 * MEM_W), lambda i: (0, 0))
    return pl.pallas_call(
        body, grid=(l // tm,), in_specs=[row, full, row], out_specs=[row, full],
        out_shape=[jax.ShapeDtypeStruct((l, MEM_W), BF16), jax.ShapeDtypeStruct((nm, 2 * MEM_W), F32)],
        compiler_params=_params(("arbitrary",)), name=name,
    )(mq, kv, dmo)


def _gated_out_proj(zg, branches, b_gate, w_o, x, g2, name, tm=512):
    l, d = x.shape
    nbr = len(branches)

    def body(zg_ref, *rest):
        br_refs, (bg_ref, w_ref, x_ref, g2_ref, m_ref, h_ref, n_ref) = rest[:nbr], rest[nbr:]
        merged = jnp.zeros((tm, d), F32)
        for i, br_ref in enumerate(br_refs):
            cols = slice(i * d, (i + 1) * d)
            merged += _sigmoid(zg_ref[:, cols].astype(F32) + bg_ref[:, cols]) * br_ref[...].astype(F32)
        mb = merged.astype(BF16)
        m_ref[...] = mb
        hv = jnp.dot(mb.astype(_MXU), w_ref[...].astype(_MXU), preferred_element_type=F32) + x_ref[...]
        h_ref[...] = hv
        rs = lax.rsqrt(jnp.mean(hv * hv, axis=-1, keepdims=True) + RMS_EPS)
        n_ref[...] = (hv * rs * g2_ref[...]).astype(n_ref.dtype)

    row = lambda c: pl.BlockSpec((tm, c), lambda i: (i, 0))
    full = lambda a: pl.BlockSpec(a.shape, lambda i: (0, 0))
    return pl.pallas_call(
        body, grid=(l // tm,),
        in_specs=[row(nbr * d)] + [row(d)] * nbr + [full(b_gate), full(w_o), row(d), full(g2)],
        out_specs=[row(d)] * 3,
        out_shape=[jax.ShapeDtypeStruct((l, d), BF16), jax.ShapeDtypeStruct((l, d), F32),
                   jax.ShapeDtypeStruct((l, d), BF16)],
        compiler_params=_params(("parallel",)), name=name,
    )(zg, *branches, b_gate, w_o, x, g2)


def _discretize(lam_re, lam_im, log_dt, b_re, b_im):
    dt = jnp.exp(log_dt)[:, None]
    mag = jnp.exp(lam_re * dt)
    a_re, a_im = mag * jnp.cos(lam_im * dt), mag * jnp.sin(lam_im * dt)
    nr, ni = a_re - 1.0, a_im
    den = lam_re * lam_re + lam_im * lam_im
    coef_re = (nr * lam_re + ni * lam_im) / den
    coef_im = (ni * lam_re - nr * lam_im) / den
    bb_re = coef_re[..., None] * b_re - coef_im[..., None] * b_im
    bb_im = coef_re[..., None] * b_im + coef_im[..., None] * b_re
    return a_re, a_im, bb_re, bb_im


def _tiled(re, im):
    r = re.shape[0]
    both = jnp.concatenate([re.reshape(r, -1, SCAN_W), im.reshape(r, -1, SCAN_W)], axis=2)
    return both.reshape(r, 2 * re.shape[1])


def _untiled(x):
    r = x.shape[0]
    t = x.reshape(r, -1, 2 * SCAN_W)
    return t[:, :, :SCAN_W].reshape(r, -1), t[:, :, SCAN_W:].reshape(r, -1)


def _bd_in(bb):
    return jnp.einsum("gph,gk->ghkp", bb, jnp.eye(SSM_G, dtype=bb.dtype)).reshape(SSM_W, SSM_S)


def _bd_diag(x):
    gb = SSM_G // SSM_BD
    t = x.reshape(SSM_BD, gb, SSM_H, gb, SSM_P)
    return jnp.einsum("bghgp->bghp", t).reshape(SSM_G, SSM_H, SSM_P)


_ANY = pl.BlockSpec(memory_space=pl.ANY)
_MESH = pl.DeviceIdType.MESH


def _allgather(x, name):
    def body(x_ref, out_ref, send_sems, recv_sems, local_sem):
        mx, my, mc = lax.axis_index("x"), lax.axis_index("y"), lax.axis_index("c")
        me, sibling = (mx, my, mc), (mx, my, 1 - mc)
        chips = [(1 - mx, my), (mx, 1 - my), (1 - mx, 1 - my)]

        def blk(px, py, pc):
            return out_ref.at[4 * px + 2 * py + pc]

        def copy(k, block, to, src=None):
            return pltpu.make_async_remote_copy(
                src_ref=blk(*block) if src is None else src, dst_ref=blk(*block),
                send_sem=send_sems.at[k], recv_sem=recv_sems.at[k], device_id=to, device_id_type=_MESH)

        mine = pltpu.make_async_copy(x_ref, blk(*me), local_sem)
        mine.start()
        first = [copy(0, me, sibling, src=x_ref)]
        first += [copy(1 + j, me, (*chip, mc), src=x_ref) for j, chip in enumerate(chips)]
        for cp in first:
            cp.start()
        passed = [copy(4 + j, (*chip, mc), sibling) for j, chip in enumerate(chips)]
        for j, chip in enumerate(chips):
            copy(1 + j, (*chip, mc), me).wait_recv()
            passed[j].start()
        copy(0, sibling, me).wait_recv()
        for j, chip in enumerate(chips):
            copy(4 + j, (*chip, 1 - mc), me).wait_recv()
        for cp in first + passed:
            cp.wait_send()
        mine.wait()

    return pl.pallas_call(
        body, out_shape=jax.ShapeDtypeStruct((N_DEV,) + x.shape, x.dtype), in_specs=[_ANY], out_specs=_ANY,
        scratch_shapes=[pltpu.SemaphoreType.DMA((7,)), pltpu.SemaphoreType.DMA((7,)), pltpu.SemaphoreType.DMA],
        name=name,
    )(x)


def _pair_exchange(g, name):
    def body(g_ref, out_ref, send_sems, recv_sems):
        mx, my, mc = lax.axis_index("x"), lax.axis_index("y"), lax.axis_index("c")
        copies = [pltpu.make_async_remote_copy(
            src_ref=g_ref.at[2 * k + (1 - mc)], dst_ref=out_ref.at[k], send_sem=send_sems.at[k],
            recv_sem=recv_sems.at[k], device_id=(mx, my, 1 - mc), device_id_type=_MESH) for k in range(4)]
        for cp in copies:
            cp.start()
        for cp in copies:
            cp.wait()

    return pl.pallas_call(
        body, out_shape=jax.ShapeDtypeStruct((4,) + g.shape[1:], g.dtype), in_specs=[_ANY], out_specs=_ANY,
        scratch_shapes=[pltpu.SemaphoreType.DMA((4,)), pltpu.SemaphoreType.DMA((4,))], name=name,
    )(g)


_HBM = pl.BlockSpec(memory_space=pltpu.HBM)
_SEM = pl.BlockSpec(memory_space=pltpu.SEMAPHORE)
_EFFECT = pltpu.SideEffectType.DATAFLOW_SIDE_EFFECTING
_TOKEN = jax.ShapeDtypeStruct((8, 128), F32)


def _peer(rel):
    pos = (lax.axis_index("x"), lax.axis_index("y"), lax.axis_index("c"))
    return tuple(1 - p if (rel >> (2 - i)) & 1 else p for i, p in enumerate(pos))


def _index_of(dev):
    return 4 * dev[0] + 2 * dev[1] + dev[2]


def _split_copies(src_ref, land_ref, sems, plan):
    n = len(plan)
    return [pltpu.make_async_remote_copy(
        src_ref=src_ref if s is None else src_ref.at[s], dst_ref=land_ref.at[d], send_sem=sems[k],
        recv_sem=sems[n + k], device_id=peer, device_id_type=_MESH) for k, (s, d, peer) in enumerate(plan)]


def _split_start(src, n_land, plan_fn, after, name):
    blk = src.shape[-2:]
    land = lax.empty((n_land,) + blk, src.dtype)
    n = len(plan_fn())

    def body(src_ref, land_ref, after_ref, *outs):
        for cp in _split_copies(src_ref, land_ref, outs[:2 * n], plan_fn()):
            cp.start()
        outs[2 * n + 2][...] = jnp.zeros_like(outs[2 * n + 2])

    res = pl.pallas_call(
        body, name=name,
        out_shape=(pltpu.SemaphoreType.DMA(()),) * (2 * n)
        + (pltpu.HBM(src.shape, src.dtype), pltpu.HBM(land.shape, land.dtype), _TOKEN),
        in_specs=(_HBM, _HBM, _ANY),
        out_specs=(_SEM,) * (2 * n) + (_HBM, _HBM, pl.BlockSpec(memory_space=pltpu.VMEM)),
        input_output_aliases={0: 2 * n, 1: 2 * n + 1},
        compiler_params=pltpu.CompilerParams(has_side_effects=_EFFECT),
    )(pltpu.with_memory_space_constraint(src, pltpu.HBM), pltpu.with_memory_space_constraint(land, pltpu.HBM), after)
    return res[:2 * n], res[2 * n], res[2 * n + 1], res[2 * n + 2]


def _split_wait(sems, src, land, plan_fn, after, name):
    n = len(sems) // 2

    def body(src_ref, land_ref, *rest):
        for cp in _split_copies(src_ref, land_ref, rest[:2 * n], plan_fn()):
            cp.wait_send()
            cp.wait_recv()

    return pl.pallas_call(
        body, name=name,
        out_shape=(pltpu.HBM(src.shape, src.dtype), pltpu.HBM(land.shape, land.dtype)),
        in_specs=(_HBM, _HBM) + (_SEM,) * (2 * n) + (_ANY,), out_specs=(_HBM, _HBM),
        input_output_aliases={0: 0, 1: 1},
        compiler_params=pltpu.CompilerParams(has_side_effects=_EFFECT),
    )(src, land, *sems, after)


def _gather_plan():
    me = _index_of(_peer(0))
    return [(None, me, _peer(rel)) for rel in range(1, N_DEV)]


def _gather_wait_plan():
    return [(None, _index_of(_peer(rel)), _peer(rel)) for rel in range(1, N_DEV)]


def _chip_plan():
    return [(_index_of(_peer(rel)) // 2, j, _peer(rel)) for j, rel in enumerate((4, 2, 6))]


def _owner_plan():
    return [(_index_of(_peer(rel)), rel - 1, _peer(rel)) for rel in range(1, N_DEV)]


def _pair_sum(g, t1, my_c, name, tr):
    _, r, c = g.shape

    def body(c_ref, g_ref, t_ref, o_ref, ob_ref):
        s = g_ref[...] + t_ref[...]
        o_ref[...] = s
        ob_ref[...] = s.astype(BF16)

    blk = pl.BlockSpec((None, tr, c), lambda k, i, cr: (k, i, 0))
    return pl.pallas_call(
        body,
        grid_spec=pltpu.PrefetchScalarGridSpec(
            num_scalar_prefetch=1, grid=(4, r // tr),
            in_specs=[pl.BlockSpec((None, tr, c), lambda k, i, cr: (2 * k + cr[0], i, 0)), blk],
            out_specs=[blk, blk]),
        out_shape=[jax.ShapeDtypeStruct((4, r, c), F32), jax.ShapeDtypeStruct((4, r, c), BF16)],
        compiler_params=_params(("parallel", "parallel")), name=name,
    )(my_c, g, t1)


def _adam_math(g, w, m, v):
    m = ADAM_B1 * m + (1.0 - ADAM_B1) * g
    v = ADAM_B2 * v + (1.0 - ADAM_B2) * (g * g)
    m_hat = m / (1.0 - ADAM_B1 ** ADAM_STEP)
    v_hat = v / (1.0 - ADAM_B2 ** ADAM_STEP)
    delta = -ADAM_LR * (m_hat / (jnp.sqrt(v_hat) + ADAM_EPS) + ADAM_WD * w)
    return delta, m, v


def _grad_sum(own, own_index, recv, name, tr):
    _, r, c = own.shape
    n = recv.shape[0]

    def body(k_ref, own_ref, *rest):
        g = own_ref[...]
        for recv_ref in rest[:n]:
            g = g + recv_ref[...].astype(F32)
        rest[n][...] = g

    def slot(j):
        return pl.BlockSpec((None, tr, c), lambda i, kr: (j, i, 0))

    return pl.pallas_call(
        body,
        grid_spec=pltpu.PrefetchScalarGridSpec(
            num_scalar_prefetch=1, grid=(r // tr,),
            in_specs=[pl.BlockSpec((None, tr, c), lambda i, kr: (kr[0], i, 0))] + [slot(j) for j in range(n)],
            out_specs=pl.BlockSpec((tr, c), lambda i, kr: (i, 0))),
        out_shape=jax.ShapeDtypeStruct((r, c), F32),
        compiler_params=_params(("parallel",)), name=name,
    )(own_index, own, *([recv] * n))


def _adam_many(g, w, m, v, row_tiles, name):
    n = len(g)

    def body(*refs):
        ins, outs = refs[:4 * n], refs[4 * n:]
        for i in range(n):
            res = _adam_math(ins[i][...], ins[n + i][...], ins[2 * n + i][...], ins[3 * n + i][...])
            for kind in range(3):
                outs[kind * n + i][...] = res[kind]

    def spec(a):
        blk = (a.shape[0] // row_tiles,) + a.shape[1:]
        return pl.BlockSpec(blk, lambda t, nd=a.ndim: (t,) + (0,) * (nd - 1))

    specs = [spec(a) for a in g]
    res = pl.pallas_call(
        body, grid=(row_tiles,), in_specs=specs * 4, out_specs=specs * 3,
        out_shape=[jax.ShapeDtypeStruct(a.shape, F32) for a in g] * 3,
        compiler_params=_params(("parallel",)), name=name,
    )(*g, *w, *m, *v)
    return res[:n], res[n:2 * n], res[2 * n:]


def _sum8(g8, name):
    _, r, c = g8.shape

    def body(g_ref, o_ref):
        acc = g_ref[0]
        for j in range(1, N_DEV):
            acc = acc + g_ref[j]
        o_ref[...] = acc

    return pl.pallas_call(
        body, grid=(1,), in_specs=[pl.BlockSpec((N_DEV, r, c), lambda i: (0, 0, 0))],
        out_specs=pl.BlockSpec((r, c), lambda i: (0, 0)), out_shape=jax.ShapeDtypeStruct((r, c), F32),
        compiler_params=_params(("arbitrary",)), name=name,
    )(g8)


def _pack(arrs, pad_rows=8):
    flat = jnp.concatenate([a.reshape(-1) for a in arrs])
    n = flat.shape[0]
    q = PACK_C * pad_rows
    tot = -(-n // q) * q
    if tot != n:
        flat = jnp.concatenate([flat, jnp.zeros((tot - n,), flat.dtype)])
    return flat.reshape(tot // PACK_C, PACK_C)


def _unpack(buf, shapes):
    flat = buf.reshape(-1)
    out, off = [], 0
    for s in shapes:
        n = int(np.prod(s))
        out.append(flat[off:off + n].reshape(s))
        off += n
    return out


GROUPS = (("w_in",),
          ("w_glu", "w_ssm_br", "w_mem_br", "w_attn_br"),
          ("w_up", "w_down"),
          ("w_mem_kv", "w_o"))
GROUP_TR = (400, 384, 512, 256)
MLP_GROUP = 2
MIXER_GROUPS = (1, 3)
ATTN_BR_FOLD = 2


def _stored_shape(name):
    r, c, ax = BIG_SHAPE[name]
    rows, cols = (r // N_DEV, c) if ax == 0 else (c // N_DEV, r)
    return (rows // ATTN_BR_FOLD, cols * ATTN_BR_FOLD) if name == "w_attn_br" else (rows, cols)


def _stored(shard, name):
    a = shard[0].T if BIG_SHAPE[name][2] == 1 else shard[0]
    return a.reshape(_stored_shape(name))


def _unstored(a, name):
    r, c, ax = BIG_SHAPE[name]
    if ax == 0:
        return a.reshape(1, r // N_DEV, c)
    return a.reshape(c // N_DEV, r).T[None]


def _pack_group(d, names):
    return jnp.concatenate([_stored(d[n], n) for n in names], axis=0)


def _split_group(buf, names):
    out, off = {}, 0
    for n in names:
        rows = _stored_shape(n)[0]
        out[n] = buf[..., off:off + rows, :]
        off += rows
    return out


def _full_stored(stacked, name):
    r, c, ax = BIG_SHAPE[name]
    return stacked.reshape((r, c) if ax == 0 else (c, r))


def _stacked_stored(full, name):
    return full.reshape((N_DEV,) + _stored_shape(name))


def _gelu_parts(x):
    c0, c1 = math.sqrt(2.0 / math.pi), 0.044715
    th = jnp.tanh(c0 * (x + c1 * x * x * x))
    return th, c0, c1


def _local_step(x, mem, tgt, wb, sp, late_weights, grads_ready, small_grads_ready):
    l = x.shape[0]
    w_a, w_g = wb["w_in"][:ZA_W], wb["w_in"][ZA_W:]

    a_re, a_im, bb_re, bb_im = _discretize(sp["ssm_lambda_re"], sp["ssm_lambda_im"], sp["ssm_log_dt"],
                                           sp["ssm_b_re"], sp["ssm_b_im"])
    a_pair = jnp.stack([a_re.reshape(1, SSM_S), a_im.reshape(1, SSM_S)])
    a_conj = jnp.stack([a_re.reshape(1, SSM_S), -a_im.reshape(1, SSM_S)])
    b_re_t, b_im_t = _bd_in(bb_re).astype(BF16), _bd_in(bb_im).astype(BF16)
    c_re_t = _bd_in(sp["ssm_c_re"].transpose(0, 2, 1)).astype(BF16)
    c_im_t = (-_bd_in(sp["ssm_c_im"].transpose(0, 2, 1))).astype(BF16)
    d_row = sp["ssm_d"].reshape(1, SSM_W)

    n1 = _rms_fwd(x, sp["norm1_g"], "rms1")
    za = _mm(n1, w_a, [BF16], tb=True, name="in_proj_a", tn=1664)
    zg = _mm(n1, w_g, [BF16], tb=True, name="in_proj_g")
    for gi in MIXER_GROUPS:
        wb = {**wb, **late_weights(gi, za)}
    u = za[:, :SSM_W]
    mq = za[:, ZA_W - MEM_W:]

    u_s = _scan_order(u)
    s_all = _ssm_scan(u_s, b_re_t, b_im_t, a_pair, reverse=False, name="ssm_scan_fwd")
    def gelu_epi(acc, ut, dr):
        y = acc + dr * ut.astype(F32)
        th, _, _ = _gelu_parts(y)
        return y, 0.5 * y * (1.0 + th)
    y0, y1 = [_time_order(t) for t in _mm(s_all, _tiled(c_re_t, c_im_t), [F32, BF16], tb=True, epi=gelu_epi,
                                          mn=[u_s], rows=[d_row], bd=SSM_BD, tm=2048, name="ssm_cs")]

    def glu_epi(acc, y1t, bg):
        t = acc + bg
        return t, y1t.astype(F32) * _sigmoid(t)
    t_glu, y2 = _mm(y1, wb["w_glu"], [F32, BF16], epi=glu_epi, mn=[y1], rows=[sp["b_glu"]], name="ssm_glu")
    br_ssm = _mm(y2, wb["w_ssm_br"], [BF16], tb=True, name="ssm_br")

    qkv_p, o_g, lse_g = [], [], []
    for g, d in enumerate(DILATIONS):
        nb = l // d // ATT_WIN
        cols = [za[:, SSM_W + (3 * j + g) * ATT_GW: SSM_W + (3 * j + g + 1) * ATT_GW] for j in range(3)]
        qp, kp, vp = [_to_perm(cc, d) for cc in cols]
        qkv_p.append((qp, kp, vp))
        og, lg = _attn_fwd(qp, kp, vp, nb, "attn_fwd%d" % g)
        o_g.append(_from_perm(og, d))
        lse_g.append(_from_perm(lg, d))

    o_att, lse_tot, br_attn = _merged_attn_proj(o_g, lse_g, wb["w_attn_br"], "attn_merge_br")

    mn = _rms_fwd(mem, sp["mem_norm_g"], "rms_mem")
    kv = _mm(mn, wb["w_mem_kv"], [BF16], name="mem_kv")
    mo = _mem_fwd(mq, kv, "mem_attn_fwd")
    br_mem = _mm(mo, wb["w_mem_br"], [BF16], tb=True, name="mem_br")

    merged, h1, n2 = _gated_out_proj(zg, [br_ssm, br_attn, br_mem], sp["b_gate"], wb["w_o"], x, sp["norm2_g"],
                                     "gated_o_proj")

    def up_epi(acc):
        ra = jnp.maximum(acc, 0.0)
        return ra * ra, ra
    wm = late_weights(MLP_GROUP, n2)
    f_act, r_act = _mm(n2, wm["w_up"], [BF16, BF16], tb=True, epi=up_epi, name="mlp_up")
    def down_epi(acc, ht, tv, gf):
        hv = acc + ht
        rs = lax.rsqrt(jnp.mean(hv * hv, axis=-1, keepdims=True) + RMS_EPS)
        err = hv * rs * gf - tv
        dh, dgf = _rms_bwd_tile(hv, err * (1.0 / D_MODEL), gf)
        return dh, dgf, _colsum(err * err) * (0.5 / D_MODEL)
    dh2, d_final_g, loss_cols = _mm(f_act, wm["w_down"], [F32], epi=down_epi, mn=[h1, tgt], rows=[sp["final_g"]],
                                    n_sums=2, tk=1024, name="mlp_down")
    loss = jnp.sum(loss_cols, axis=1, keepdims=True)

    gw, gs = {}, {"final_g": d_final_g, "loss": loss}
    d_act = _mm(dh2, wm["w_down"], [BF16], tb=True, epi=lambda acc, ra: (acc * 2.0 * ra.astype(F32),), mn=[r_act],
                name="mlp_down_dx")
    dw_down = _mm(f_act, dh2, [F32], ta=True, name="mlp_down_dw")
    dw_up = _mm(d_act, n2, [F32], ta=True, name="mlp_up_dw")
    token = grads_ready(MLP_GROUP, {"w_up": dw_up, "w_down": dw_down})
    def up_dx_epi(acc, ht, dht, g2):
        dx, dg = _rms_bwd_tile(ht, acc, g2)
        return dx + dht, dg
    dh1, gs["norm2_g"] = _mm(d_act, wm["w_up"], [F32], epi=up_dx_epi, mn=[h1, dh2],
                             rows=[sp["norm2_g"] + token[:1, :1]], n_sums=1, tk=1024, name="mlp_up_dx")
    gw["w_o"] = _mm(merged, dh1, [F32], ta=True, name="o_proj_dw")

    def gate_bwd_epi(dm, *tiles):
        dbr, dz = [], []
        for zt, bt, bias in zip(tiles[0:3], tiles[3:6], tiles[6:9]):
            gt = _sigmoid(zt.astype(F32) + bias)
            dbr.append(dm * gt)
            dz.append(dm * bt.astype(F32) * gt * (1.0 - gt))
        return (*dbr, *dz, *[_colsum(t) for t in dz])
    gate_bias = [sp["b_gate"][:, i * D_MODEL:(i + 1) * D_MODEL] for i in range(3)]
    res = _mm(dh1, wb["w_o"], [BF16] * 6, tb=True, epi=gate_bwd_epi, mn=[(zg, 0), (zg, 1), (zg, 2), br_ssm, br_attn, br_mem],
              rows=gate_bias, n_sums=3, tm=512, name="o_proj_dx")
    (dbr_ssm, dbr_attn, dbr_mem), dzg = res[0:3], res[3:6]
    gs["b_gate"] = jnp.concatenate(res[6:9], axis=1)

    gw["w_ssm_br"] = _mm(dbr_ssm, y2, [F32], ta=True, name="ssm_br_dw")
    def glu_bwd_epi(dy, y1t, tt):
        sg = _sigmoid(tt)
        dt = dy * y1t.astype(F32) * sg * (1.0 - sg)
        return dt, dy * sg, _colsum(dt)
    dt_glu, dy1a, gs["b_glu"] = _mm(dbr_ssm, wb["w_ssm_br"], [BF16, F32], epi=glu_bwd_epi, mn=[y1, t_glu], n_sums=1,
                                    name="ssm_br_dx")
    gw["w_glu"] = _mm(y1, dt_glu, [F32], ta=True, name="ssm_glu_dw")

    def gelu_bwd_epi(acc, dy1t, y0t, ut):
        th, c0, c1 = _gelu_parts(y0t)
        dg = 0.5 * (1.0 + th) + 0.5 * y0t * (1.0 - th * th) * c0 * (1.0 + 3.0 * c1 * y0t * y0t)
        dy = (acc + dy1t) * dg
        return dy, _colsum(dy * ut.astype(F32))
    dy0, gs["ssm_d"] = _mm(dt_glu, wb["w_glu"], [F32], tb=True, epi=gelu_bwd_epi, mn=[dy1a, y0, u], n_sums=1,
                           name="ssm_glu_dx")
    dy0_s = _scan_order(dy0)
    lam, da, d_b, d_c = _ssm_scan(dy0_s, c_re_t, c_im_t, a_conj, reverse=True, s_fwd=s_all, u=u_s,
                                  name="ssm_scan_bwd")
    du = _time_order(_mm(lam, _tiled(b_re_t, b_im_t), [BF16], tb=True,
                         epi=lambda acc, dyt, dr: (acc + dyt * dr,), mn=[dy0_s], rows=[d_row], bd=SSM_BD, tm=2048, name="ssm_bu_dx"))
    gs["a_re"], gs["a_im"] = da[0], da[1]
    (dbr, dbi), (dcr, dci) = _untiled(d_b), _untiled(d_c)
    gs["bb_re"], gs["bb_im"] = _bd_diag(dbr).transpose(0, 2, 1), _bd_diag(dbi).transpose(0, 2, 1)
    gs["ssm_c_re"], gs["ssm_c_im"] = _bd_diag(dcr), -_bd_diag(dci)

    gw["w_attn_br"] = _mm(dbr_attn, o_att, [F32], ta=True, name="attn_br_dw")

    def do_epi(acc, ot):
        prod = acc * ot
        head = lax.broadcasted_iota(jnp.int32, prod.shape, 1) // ATT_E
        dd = jnp.zeros_like(prod)
        for h in range(ATT_HG):
            dd = jnp.where(head == h, jnp.sum(jnp.where(head == h, prod, 0.0), axis=1, keepdims=True), dd)
        return acc, dd
    do_att, dd_att = _mm(dbr_attn, wb["w_attn_br"], [BF16, F32], epi=do_epi, mn=[o_att], name="attn_br_dx")
    dq_l, dk_l, dv_l = [], [], []
    for g, d in enumerate(DILATIONS):
        nb = l // d // ATT_WIN
        qp, kp, vp = qkv_p[g]
        dq, dk, dv = _attn_bwd(qp, kp, vp, _to_perm(do_att, d), _to_perm(lse_tot, d), _to_perm(dd_att, d),
                               nb, "attn_bwd%d" % g)
        dq_l.append(_from_perm(dq, d))
        dk_l.append(_from_perm(dk, d))
        dv_l.append(_from_perm(dv, d))

    gw["w_mem_br"] = _mm(dbr_mem, mo, [F32], ta=True, name="mem_br_dw")
    dmo = _mm(dbr_mem, wb["w_mem_br"], [BF16], name="mem_br_dx")
    dmq, dkv = _mem_bwd(mq, kv, dmo, "mem_attn_bwd")
    gw["w_mem_kv"] = _mm(mn, dkv, [F32], ta=True, name="mem_kv_dw")
    dmn = _mm(dkv, wb["w_mem_kv"], [F32], tb=True, name="mem_kv_dx")
    token = sum(grads_ready(gi, gw) for gi in MIXER_GROUPS)
    gs["mem_norm_g"] = _rms_bwd(mem, dmn, None, sp["mem_norm_g"] + token[:1, :1], "rms_mem_bwd")[1]

    dza = jnp.concatenate([du] + dq_l + dk_l + dv_l + [dmq], axis=1)
    dn_a = _mm(dza, w_a, [F32], name="in_proj_a_dx", tk=1664)
    dw_a = _mm(dza, n1, [F32], ta=True, name="in_proj_a_dw", tm=1664)
    dw_g = [_mm(dzg[i], n1, [F32], ta=True, name="in_proj_g_dw%d" % i) for i in range(3)]
    gw["w_in"] = jnp.concatenate([dw_a] + dw_g, axis=0)
    token = grads_ready(0, gw) + small_grads_ready(gs)
    def in_dx_epi(acc, pt, xt, dht, g1):
        dx, dg = _rms_bwd_tile(xt, acc + pt, g1)
        return dx + dht, dg
    w_gs = [w_g[i * D_MODEL:(i + 1) * D_MODEL] for i in range(3)]
    grad_x, gs["norm1_g"] = _mm(dzg[0], w_gs[0], [F32], pair2=(dzg[1], w_gs[1], dzg[2], w_gs[2]), epi=in_dx_epi,
                                mn=[dn_a, x, dh1],
                                rows=[sp["norm1_g"] + token[:1, :1]], n_sums=1, tm=512, name="in_proj_g_dx")
    return loss, grad_x, gs


_SMALL_GRAD_ORDER = ("norm1_g", "mem_norm_g", "b_gate", "a_re", "a_im", "bb_re", "bb_im", "ssm_c_re", "ssm_c_im",
                     "ssm_d", "b_glu", "norm2_g", "final_g", "loss")


def kernel(x, mem, norm1_g, mem_norm_g, w_in, b_gate, ssm_lambda_re, ssm_lambda_im, ssm_log_dt, ssm_b_re, ssm_b_im, ssm_c_re, ssm_c_im, ssm_d, w_glu, b_glu, w_ssm_br, w_attn_br, w_mem_kv, w_mem_br, w_o, norm2_g, w_up, w_down, final_g, loss_target, m_norm1_g, m_mem_norm_g, m_w_in, m_b_gate, m_ssm_lambda_re, m_ssm_lambda_im, m_ssm_log_dt, m_ssm_b_re, m_ssm_b_im, m_ssm_c_re, m_ssm_c_im, m_ssm_d, m_w_glu, m_b_glu, m_w_ssm_br, m_w_attn_br, m_w_mem_kv, m_w_mem_br, m_w_o, m_norm2_g, m_w_up, m_w_down, m_final_g, v_norm1_g, v_mem_norm_g, v_w_in, v_b_gate, v_ssm_lambda_re, v_ssm_lambda_im, v_ssm_log_dt, v_ssm_b_re, v_ssm_b_im, v_ssm_c_re, v_ssm_c_im, v_ssm_d, v_w_glu, v_b_glu, v_w_ssm_br, v_w_attn_br, v_w_mem_kv, v_w_mem_br, v_w_o, v_norm2_g, v_w_up, v_w_down, v_final_g):
    args = dict(locals())
    w = {n: args[n] for n in ALL_W}
    m = {n: args["m_" + n] for n in ALL_W}
    v = {n: args["v_" + n] for n in ALL_W}
    my_c = lax.axis_index("c").astype(jnp.int32).reshape(1)
    my_chip = (2 * lax.axis_index("x") + lax.axis_index("y")).astype(jnp.int32).reshape(1)

    w_pack = [_pack_group(w, names) for names in GROUPS]
    my_index = (4 * lax.axis_index("x") + 2 * lax.axis_index("y") + lax.axis_index("c")).astype(jnp.int32)
    zero = jnp.zeros((), jnp.int32)
    w_all = _allgather(w_pack[0].astype(BF16), "allgather_weights0")
    wb = {n: _full_stored(part, n) for n, part in _split_group(w_all, GROUPS[0]).items()}
    gathers = {gi: _split_start(w_pack[gi].astype(BF16), N_DEV, _gather_plan, w_all, "weights_gather_start%d" % gi)
               for gi in range(1, len(GROUPS))}

    def gathered(started, after, name):
        sems, src, land, _ = started
        src, land = _split_wait(sems, src, land, _gather_wait_plan, after, name)
        return lax.dynamic_update_slice(land, src[None], (my_index, zero, zero))

    def late_weights(gi, after):
        full = gathered(gathers[gi], after, "weights_gather_wait%d" % gi)
        return {n: _full_stored(part, n) for n, part in _split_group(full, GROUPS[gi]).items()}

    pending = {}

    def grads_ready(gi, grads):
        g_pack = jnp.concatenate([_stacked_stored(grads[n], n) for n in GROUPS[gi]], axis=1)
        if gi == 0:
            t1 = _pair_exchange(g_pack, "grad_pair_exchange%d" % gi)
            p_sum, p_bf = _pair_sum(g_pack, t1, my_c, "grad_pair_sum%d" % gi, GROUP_TR[gi])
            started = _split_start(p_bf, 3, _chip_plan, p_sum, "grad_chip_exchange_start%d" % gi)
            pending[gi] = (p_sum, my_chip, started, _chip_plan)
        else:
            started = _split_start(g_pack.astype(BF16), N_DEV - 1, _owner_plan, g_pack, "grad_exchange_start%d" % gi)
            pending[gi] = (g_pack, my_index.reshape(1), started, _owner_plan)
        return started[3]

    early_small = [n for n in _SMALL_GRAD_ORDER if n != "norm1_g"]
    small_started = []

    def small_grads_ready(gs):
        started = _split_start(_pack([gs[n] for n in early_small]), N_DEV, _gather_plan, gs["mem_norm_g"],
                               "small_grads_gather_start")
        small_started.append((started, [gs[n].shape for n in early_small]))
        return started[3]

    sp = {
        "norm1_g": norm1_g + sum(started[3][:1, :1] for started in gathers.values()), "mem_norm_g": mem_norm_g, "b_gate": b_gate, "b_glu": b_glu, "norm2_g": norm2_g,
        "final_g": final_g.reshape(1, D_MODEL),
        "ssm_lambda_re": ssm_lambda_re[0], "ssm_lambda_im": ssm_lambda_im[0], "ssm_log_dt": ssm_log_dt[0],
        "ssm_b_re": ssm_b_re[0], "ssm_b_im": ssm_b_im[0], "ssm_c_re": ssm_c_re[0], "ssm_c_im": ssm_c_im[0],
        "ssm_d": ssm_d[0],
    }
    loss, grad_x, gs = _local_step(x[0], mem[0], loss_target[0], wb, sp, late_weights, grads_ready,
                                     small_grads_ready)
    n1_started = _split_start(_pack([gs["norm1_g"]]), N_DEV, _gather_plan, grad_x, "norm1_grad_gather_start")

    big_g = {}
    for gi, names in enumerate(GROUPS):
        own, own_index, (sems, src, land, _), plan = pending[gi]
        recv = _split_wait(sems, src, land, plan, grad_x, "grad_exchange_wait%d" % gi)[1]
        g_pack = _grad_sum(own, own_index, recv, "grad_sum%d" % gi, GROUP_TR[gi])
        for n, part in _split_group(g_pack, names).items():
            big_g[n] = _unstored(part, n)
    rows_of = lambda d, names: [d[n].reshape(d[n].shape[-2:]) for n in names]
    big_out = _adam_many(rows_of(big_g, BIG), rows_of(w, BIG), rows_of(m, BIG), rows_of(v, BIG), 8, "adam_big")
    big = [big_g] + [{n: a[None] for n, a in zip(BIG, outs)} for outs in big_out]

    (sg_started, sg_shapes), = small_started
    sg_all = jnp.concatenate([gathered(sg_started, big_out[0][0], "small_grads_gather_wait"),
                              gathered(n1_started, big_out[0][0], "norm1_grad_gather_wait")], axis=1)
    sg_sum = _sum8(sg_all, "sum_small_grads")
    n1_rows = n1_started[1].shape[0]
    sg = dict(zip(early_small, _unpack(sg_sum[:-n1_rows], sg_shapes)))
    sg["norm1_g"] = _unpack(sg_sum[-n1_rows:], [gs["norm1_g"].shape])[0]
    _, disc_vjp = jax.vjp(_discretize, sp["ssm_lambda_re"], sp["ssm_lambda_im"], sp["ssm_log_dt"],
                          sp["ssm_b_re"], sp["ssm_b_im"])
    d_lre, d_lim, d_ldt, d_bre, d_bim = disc_vjp((sg["a_re"].reshape(SSM_G, SSM_P), sg["a_im"].reshape(SSM_G, SSM_P),
                                                  sg["bb_re"], sg["bb_im"]))
    small_grad = {
        "norm1_g": sg["norm1_g"], "mem_norm_g": sg["mem_norm_g"], "b_gate": sg["b_gate"],
        "ssm_lambda_re": d_lre, "ssm_lambda_im": d_lim, "ssm_log_dt": d_ldt, "ssm_b_re": d_bre, "ssm_b_im": d_bim,
        "ssm_c_re": sg["ssm_c_re"], "ssm_c_im": sg["ssm_c_im"], "ssm_d": sg["ssm_d"], "b_glu": sg["b_glu"],
        "norm2_g": sg["norm2_g"], "final_g": sg["final_g"],
    }
    small_grad = {n: small_grad[n].reshape(w[n].shape) for n in SMALL}

    def squeezed(a):
        return a.reshape(a.shape[1:]) if a.ndim > 2 else a.reshape(1, -1)

    sq = lambda d: [squeezed(d[n]) for n in SMALL]
    small_out = _adam_many(sq(small_grad), sq(w), sq(m), sq(v), 1, "adam_small")
    small = [small_grad] + [{n: a.reshape(w[n].shape) for n, a in zip(SMALL, outs)} for outs in small_out]

    outs = [sg["loss"][0, 0], grad_x[None]]
    for kind in range(4):
        for n in ALL_W:
            outs.append(big[kind][n] if n in BIG else small[kind][n])
    return tuple(outs)
```

```python
import math

import numpy as np
import jax
import jax.numpy as jnp
from jax import lax
from jax.experimental import pallas as pl
from jax.experimental.pallas import tpu as pltpu

F32 = jnp.float32
BF16 = jnp.bfloat16
_MXU = jnp.bfloat16

D_MODEL = 1024
SSM_G, SSM_H, SSM_P = 32, 16, 64
SSM_W = SSM_G * SSM_H
SSM_S = SSM_G * SSM_P
SSM_BD = 4
ATT_E = 64
ATT_HG = 4
ATT_GW = ATT_HG * ATT_E
ATT_WIN = 128
ATT_QB = 8
ATT_QB_FWD = 4
DILATIONS = (1, 4, 16)
MEM_H, MEM_E = 4, 128
MEM_W = MEM_H * MEM_E
ZA_W = SSM_W + 9 * ATT_GW + MEM_W
ZG_W = 3 * D_MODEL
IN_W = ZA_W + ZG_W
RMS_EPS = 1e-6
NEG_INF = -1e30

ADAM_LR, ADAM_B1, ADAM_B2, ADAM_EPS, ADAM_WD, ADAM_STEP = 0.001, 0.9, 0.999, 1e-08, 0.01, 10

N_DEV = 8
PACK_C = 512
_VMEM_LIMIT = 56 * 1024 * 1024
SUBLANES = 16
SCAN_SEG = 128
SCAN_CHAINS = 4
SCAN_UNROLL = 8
SCAN_W = 128

BIG = ("w_in", "w_glu", "w_ssm_br", "w_attn_br", "w_mem_kv", "w_mem_br", "w_o", "w_up", "w_down")
BIG_SHAPE = {
    "w_in": (D_MODEL, IN_W, 1), "w_glu": (SSM_W, SSM_W, 0), "w_ssm_br": (SSM_W, D_MODEL, 1),
    "w_attn_br": (ATT_GW, D_MODEL, 1), "w_mem_kv": (D_MODEL, 2 * MEM_W, 0), "w_mem_br": (MEM_W, D_MODEL, 1),
    "w_o": (D_MODEL, D_MODEL, 0), "w_up": (D_MODEL, 4 * D_MODEL, 1), "w_down": (4 * D_MODEL, D_MODEL, 0),
}
SMALL = ("norm1_g", "mem_norm_g", "b_gate", "ssm_lambda_re", "ssm_lambda_im", "ssm_log_dt", "ssm_b_re",
         "ssm_b_im", "ssm_c_re", "ssm_c_im", "ssm_d", "b_glu", "norm2_g", "final_g")
ALL_W = ("norm1_g", "mem_norm_g", "w_in", "b_gate", "ssm_lambda_re", "ssm_lambda_im", "ssm_log_dt", "ssm_b_re",
         "ssm_b_im", "ssm_c_re", "ssm_c_im", "ssm_d", "w_glu", "b_glu", "w_ssm_br", "w_attn_br", "w_mem_kv",
         "w_mem_br", "w_o", "norm2_g", "w_up", "w_down", "final_g")


def _params(sem):
    return pltpu.CompilerParams(dimension_semantics=sem, vmem_limit_bytes=_VMEM_LIMIT)


def _pick(n, cap):
    if n <= cap:
        return n
    t = (cap // 128) * 128
    while n % t:
        t -= 128
    return t


def _mm(a, b, outs, *, name, ta=False, tb=False, epi=None, mn=(), rows=(), pair2=None, bd=0, n_sums=0,
        tm=1024, tn=1024, tk=2048):
    ab = [a, b] + (list(pair2) if pair2 is not None else [])
    a_shape, b_shape = ab[0].shape, ab[1].shape
    m = a_shape[1] if ta else a_shape[0]
    k = a_shape[0] if ta else a_shape[1]
    n = b_shape[0] if tb else b_shape[1]
    assert k == (b_shape[1] if tb else b_shape[0]), (name, a_shape, b_shape)
    out_n = n
    if bd and ta:
        assert not tb
        tm, tn, tk = m // bd, n // bd, _pick(k, tk)
        grid, out_n = (bd, 1, k // tk), tn
        a_blk = ((tk, tm), lambda i, j, kk: (kk, i))
        b_blk = ((tk, tn), lambda i, j, kk: (kk, i))
        mn_spec = pl.BlockSpec((tm, tn), lambda i, j, kk: (i, 0))
    elif bd:
        tm, tn, tk = _pick(m, tm), n // bd, k // bd
        grid = (m // tm, bd, 1)
        a_blk = ((tm, tk), lambda i, j, kk: (i, j))
        b_blk = ((tn, tk) if tb else (tk, tn), lambda i, j, kk: (j, j))
        mn_spec = pl.BlockSpec((tm, tn), lambda i, j, kk: (i, j))
    else:
        tm, tn, tk = _pick(m, tm), _pick(n, tn), _pick(k, tk)
        grid = (m // tm, n // tn, k // tk)
        a_blk = ((tk, tm), lambda i, j, kk: (kk, i)) if ta else ((tm, tk), lambda i, j, kk: (i, kk))
        b_blk = ((tn, tk), lambda i, j, kk: (j, kk)) if tb else ((tk, tn), lambda i, j, kk: (kk, j))
        mn_spec = pl.BlockSpec((tm, tn), lambda i, j, kk: (i, j))

    ab_specs = [pl.BlockSpec(*(a_blk if q % 2 == 0 else b_blk)) for q in range(len(ab))]
    mn_arrays = [e[0] if isinstance(e, tuple) else e for e in mn]
    mn_specs = [pl.BlockSpec((tm, tn), lambda i, j, kk, c=e[1]: (i, c)) if isinstance(e, tuple) else mn_spec
                for e in mn]
    nk = grid[2]
    row_spec = pl.BlockSpec((1, tn), lambda i, j, kk: (0, j))
    n_ex, n_out = len(mn) + len(rows), len(outs)
    assert n_sums == 0 or (grid[1] == 1 and not bd)
    dims = (((0 if ta else 1,), (1 if tb else 0,)), ((), ()))

    def body(*refs):
        ab_refs, rest = refs[:len(ab)], refs[len(ab):]
        ex, o_refs = rest[:n_ex], rest[n_ex:n_ex + n_out]
        s_refs = rest[n_ex + n_out:n_ex + n_out + n_sums]
        first_row_tile = pl.program_id(0) == 0
        kk = pl.program_id(2)

        pairs = list(zip(ab_refs[0::2], ab_refs[1::2]))

        def product(pair):
            return lax.dot_general(pair[0][...].astype(_MXU), pair[1][...].astype(_MXU), dims,
                                   preferred_element_type=F32)

        def finish(total):
            vals = (total,) if epi is None else epi(total, *[r[...] for r in ex])
            for r, v in zip(o_refs, vals):
                r[...] = v.astype(r.dtype)
            for r, v in zip(s_refs, vals[n_out:]):
                r[...] = jnp.where(first_row_tile, v, r[...] + v)

        if nk == 1:
            total = product(pairs[0])
            for pair in pairs[1:]:
                total = total + product(pair)
            finish(total)
        else:
            acc = rest[-1]

            @pl.when(kk == 0)
            def _():
                acc[...] = jnp.zeros_like(acc)

            for pair in pairs:
                acc[...] += product(pair)

            @pl.when(kk == nk - 1)
            def _():
                finish(acc[...])

    res = pl.pallas_call(
        body, grid=grid,
        in_specs=ab_specs + mn_specs + [row_spec] * len(rows),
        out_specs=[mn_spec] * n_out + [row_spec] * n_sums,
        out_shape=[jax.ShapeDtypeStruct((m, out_n), dt) for dt in outs]
        + [jax.ShapeDtypeStruct((1, out_n), F32)] * n_sums,
        scratch_shapes=[pltpu.VMEM((tm, tn), F32)] if nk > 1 else [],
        compiler_params=_params(("arbitrary" if n_sums else "parallel", "parallel", "arbitrary")), name=name,
    )(*ab, *mn_arrays, *rows)
    return res[0] if n_out + n_sums == 1 else res


def _ew(fn, rows, bcs, out_rows, out_accs, *, name, tm=256):
    r = rows[0].shape[0]
    tm = min(tm, r)
    assert r % tm == 0
    nr, nb, no, na = len(rows), len(bcs), len(out_rows), len(out_accs)

    def body(*refs):
        i = pl.program_id(0)
        r_in, b_in = refs[:nr], refs[nr:nr + nb]
        o_r, o_a = refs[nr + nb:nr + nb + no], refs[nr + nb + no:]
        outs, accs = fn([x[...] for x in r_in], [x[...] for x in b_in])
        for ref, v in zip(o_r, outs):
            ref[...] = v.astype(ref.dtype)
        if na:
            @pl.when(i == 0)
            def _():
                for ref in o_a:
                    ref[...] = jnp.zeros_like(ref)

            for ref, v in zip(o_a, accs):
                ref[...] += v

    res = pl.pallas_call(
        body, grid=(r // tm,),
        in_specs=[pl.BlockSpec((tm, x.shape[1]), lambda i: (i, 0)) for x in rows]
        + [pl.BlockSpec((1, x.shape[1]), lambda i: (0, 0)) for x in bcs],
        out_specs=[pl.BlockSpec((tm, c), lambda i: (i, 0)) for c, _ in out_rows]
        + [pl.BlockSpec((1, c), lambda i: (0, 0)) for c in out_accs],
        out_shape=[jax.ShapeDtypeStruct((r, c), dt) for c, dt in out_rows]
        + [jax.ShapeDtypeStruct((1, c), F32) for c in out_accs],
        compiler_params=_params(("arbitrary",)), name=name,
    )(*rows, *bcs)
    return res


def _colsum(x):
    return jnp.sum(x, axis=0, keepdims=True)


def _sigmoid(x):
    return 1.0 / (1.0 + jnp.exp(-x))


def _rms_bwd_tile(xv, dv, g):
    rs = lax.rsqrt(jnp.mean(xv * xv, axis=-1, keepdims=True) + RMS_EPS)
    gd = dv * g
    dx = rs * gd - xv * (rs * rs * rs) * jnp.mean(gd * xv, axis=-1, keepdims=True)
    return dx, _colsum(dv * xv * rs)


def _rms_fwd(x, g, name):
    def fn(r, b):
        xv = r[0]
        rs = lax.rsqrt(jnp.mean(xv * xv, axis=-1, keepdims=True) + RMS_EPS)
        return [xv * rs * b[0]], []
    return _ew(fn, [x], [g], [(x.shape[1], BF16)], [], name=name, tm=1024)[0]


def _rms_bwd(x, dn, res, g, name):
    def fn(r, b):
        dx, dg = _rms_bwd_tile(r[0], r[1], b[0])
        if res is not None:
            dx = dx + r[2]
        return [dx], [dg]
    rows = [x, dn] + ([res] if res is not None else [])
    return _ew(fn, rows, [g], [(x.shape[1], F32)], [x.shape[1]], name=name)


def _scan_order(x):
    l, c = x.shape
    return x.reshape(l // (SUBLANES * SCAN_SEG), SUBLANES, SCAN_SEG, c).transpose(0, 2, 1, 3).reshape(l, c)


def _time_order(x):
    l, c = x.shape
    return x.reshape(l // (SUBLANES * SCAN_SEG), SCAN_SEG, SUBLANES, c).transpose(0, 2, 1, 3).reshape(l, c)


def _ssm_scan(x, w_re, w_im, a_pair, *, reverse, s_fwd=None, u=None, name):
    l = x.shape[0]
    seg, w = SCAN_SEG, SCAN_W
    bd_w = SSM_W // SSM_BD
    tiles_per_bd = SSM_S // SSM_BD // w
    nch = min(SCAN_CHAINS, l // (SUBLANES * seg))
    chain_rows = SUBLANES * seg
    tb = nch * chain_rows
    nt = l // tb
    with_da = s_fwd is not None
    assert reverse or not with_da

    def tt(t):
        return nt - 1 - t if reverse else t

    def body(*refs):
        if with_da:
            (x_ref, wr_ref, wi_ref, a_ref, sf_ref, sp_ref, u_ref, s_ref, da_ref, dw_ref, dx_ref,
             p_ref, c_ref, b_scr, l_scr) = refs
        else:
            x_ref, wr_ref, wi_ref, a_ref, s_ref, p_ref, c_ref, b_scr, l_scr = refs
        t_blk = pl.program_id(1)
        ar, ai = a_ref[0], a_ref[1]

        @pl.when(t_blk == 0)
        def _():
            def pstep(i, carry):
                pr, pi = carry
                p_ref[0, pl.ds(i, 1), :] = pr
                p_ref[1, pl.ds(i, 1), :] = pi
                return pr * ar - pi * ai, pr * ai + pi * ar

            lax.fori_loop(0, seg, pstep, (ar, ai))
            c_ref[...] = jnp.zeros_like(c_ref)
            if with_da:
                da_ref[...] = jnp.zeros_like(da_ref)
                dw_ref[...] = jnp.zeros_like(dw_ref)
                dx_ref[...] = jnp.zeros_like(dx_ref)

        xb = x_ref[...].astype(_MXU)
        b_scr[:, :w] = jnp.dot(xb, wr_ref[...], preferred_element_type=F32)
        b_scr[:, w:] = jnp.dot(xb, wi_ref[...], preferred_element_type=F32)
        arb, aib = jnp.broadcast_to(ar, (SUBLANES, w)), jnp.broadcast_to(ai, (SUBLANES, w))
        zero = jnp.zeros((SUBLANES, w), F32)

        def tile(g, step):
            return pl.ds(pl.multiple_of(g * chain_rows + step * SUBLANES, SUBLANES), SUBLANES)

        def rows(g, i):
            return tile(g, seg - 1 - i if reverse else i)

        def local_step(i, carry):
            out = []
            for g in range(nch):
                sr, si = carry[2 * g], carry[2 * g + 1]
                idx = rows(g, i)
                sr, si = arb * sr - aib * si + b_scr[idx, :w], arb * si + aib * sr + b_scr[idx, w:]
                l_scr[idx, :w] = sr
                l_scr[idx, w:] = si
                out += [sr, si]
            return tuple(out)

        def unrolled(step_fn, first):
            def trip(q, carry):
                for r in range(SCAN_UNROLL):
                    carry = step_fn(first + q * SCAN_UNROLL + r, carry)
                return carry
            return trip

        ends = lax.fori_loop(0, seg // SCAN_UNROLL, unrolled(local_step, 0), (zero,) * (2 * nch))

        a_seg_r, a_seg_i = p_ref[0, seg - 1:seg, :], p_ref[1, seg - 1:seg, :]
        cr, ci = c_ref[0], c_ref[1]
        sub = lax.broadcasted_iota(jnp.int32, (SUBLANES, w), 0)
        ins = [[zero, zero] for _ in range(nch)]
        order = [(g, k) for g in range(nch) for k in range(SUBLANES)]
        for g, k in (order[::-1] if reverse else order):
            ins[g] = [jnp.where(sub == k, cr, ins[g][0]), jnp.where(sub == k, ci, ins[g][1])]
            er, ei = ends[2 * g][k:k + 1], ends[2 * g + 1][k:k + 1]
            cr, ci = er + a_seg_r * cr - a_seg_i * ci, ei + a_seg_r * ci + a_seg_i * cr
        c_ref[0] = cr
        c_ref[1] = ci

        def fix(g, i):
            idx = rows(g, i)
            pr, pi = p_ref[0, pl.ds(i, 1), :], p_ref[1, pl.ds(i, 1), :]
            sr = l_scr[idx, :w] + pr * ins[g][0] - pi * ins[g][1]
            si = l_scr[idx, w:] + pr * ins[g][1] + pi * ins[g][0]
            s_ref[idx, :w] = sr.astype(s_ref.dtype)
            s_ref[idx, w:] = si.astype(s_ref.dtype)
            return sr, si

        if not with_da:
            def fix_step(i, carry):
                for g in range(nch):
                    fix(g, i)
                return carry

            lax.fori_loop(0, seg // SCAN_UNROLL, unrolled(fix_step, 0), 0)
        else:
            def adj_step(i, acc):
                acc_r, acc_i = acc
                for g in range(nch):
                    lr, li = fix(g, i)
                    prev = tile(g, seg - 2 - i)
                    fr, fi = sf_ref[prev, :w].astype(F32), sf_ref[prev, w:].astype(F32)
                    acc_r, acc_i = acc_r + lr * fr + li * fi, acc_i + li * fr - lr * fi
                return acc_r, acc_i

            acc = lax.fori_loop(0, seg // SCAN_UNROLL - 1, unrolled(adj_step, 0), (zero, zero))
            for i in range(seg - SCAN_UNROLL, seg - 1):
                acc = adj_step(i, acc)
            acc_r, acc_i = acc
            first_block = tt(t_blk) == 0
            for g in range(nch):
                lr, li = fix(g, seg - 1)
                seg_ends = tile(g, seg - 1)
                if g == 0:
                    pvr = jnp.where(first_block, 0.0, sp_ref[SUBLANES - 1:SUBLANES, :w].astype(F32))
                    pvi = jnp.where(first_block, 0.0, sp_ref[SUBLANES - 1:SUBLANES, w:].astype(F32))
                else:
                    pvr = sf_ref[g * chain_rows - 1:g * chain_rows, :w].astype(F32)
                    pvi = sf_ref[g * chain_rows - 1:g * chain_rows, w:].astype(F32)
                fr = jnp.where(sub == 0, pvr, pltpu.roll(sf_ref[seg_ends, :w].astype(F32), 1, 0))
                fi = jnp.where(sub == 0, pvi, pltpu.roll(sf_ref[seg_ends, w:].astype(F32), 1, 0))
                acc_r = acc_r + lr * fr + li * fi
                acc_i = acc_i + li * fr - lr * fi
            da_ref[0] += jnp.sum(acc_r, axis=0, keepdims=True)
            da_ref[1] += jnp.sum(acc_i, axis=0, keepdims=True)
            dw_ref[...] += _tn_dot(u_ref[...], s_ref[...])
            dx_ref[...] += _tn_dot(xb, sf_ref[...])

    x_spec = pl.BlockSpec((tb, bd_w), lambda j, t: (tt(t), j // tiles_per_bd))
    w_spec = pl.BlockSpec((bd_w, w), lambda j, t: (j // tiles_per_bd, j))
    d_spec = pl.BlockSpec((bd_w, 2 * w), lambda j, t: (j // tiles_per_bd, j % tiles_per_bd))
    a_spec = pl.BlockSpec((2, 1, w), lambda j, t: (0, 0, j))
    s_spec = pl.BlockSpec((tb, 2 * w), lambda j, t: (tt(t), j))
    in_specs, args = [x_spec, w_spec, w_spec, a_spec], [x, w_re, w_im, a_pair]
    out_specs, out_shape = [s_spec], [jax.ShapeDtypeStruct((l, 2 * SSM_S), BF16)]
    scratch = [pltpu.VMEM((2, seg, w), F32), pltpu.VMEM((2, 1, w), F32)] + [pltpu.VMEM((tb, 2 * w), F32)] * 2
    if with_da:
        in_specs += [s_spec, pl.BlockSpec((SUBLANES, 2 * w),
                                          lambda j, t: (jnp.maximum(tt(t) * (tb // SUBLANES) - 1, 0), j)),
                     x_spec]
        args += [s_fwd, s_fwd, u]
        out_specs += [a_spec, d_spec, d_spec]
        out_shape += ([jax.ShapeDtypeStruct((2, 1, SSM_S), F32)]
                      + [jax.ShapeDtypeStruct((SSM_W, 2 * SSM_S // SSM_BD), F32)] * 2)
    res = pl.pallas_call(
        body, grid=(SSM_S // w, nt), in_specs=in_specs, out_specs=out_specs, out_shape=out_shape,
        scratch_shapes=scratch, compiler_params=_params(("parallel", "arbitrary")), name=name,
    )(*args)
    return res if with_da else res[0]


def _nt_dot(x, y):
    return lax.dot_general(x.astype(_MXU), y.astype(_MXU), (((1,), (1,)), ((), ())), preferred_element_type=F32)


def _tn_dot(x, y):
    return lax.dot_general(x.astype(_MXU), y.astype(_MXU), (((0,), (0,)), ((), ())), preferred_element_type=F32)


def _nn_dot(x, y):
    return jnp.dot(x.astype(_MXU), y.astype(_MXU), preferred_element_type=F32)


def _attn_mask2(gb, nb):
    qi = lax.broadcasted_iota(jnp.int32, (ATT_WIN, 2 * ATT_WIN), 0)
    c = lax.broadcasted_iota(jnp.int32, (ATT_WIN, 2 * ATT_WIN), 1)
    has_prev = (gb % nb) != 0
    prev_ok = jnp.logical_and(jnp.logical_and(c < ATT_WIN, c >= qi), has_prev)
    own_ok = jnp.logical_and(c >= ATT_WIN, c - ATT_WIN <= qi)
    return jnp.logical_or(prev_ok, own_ok)


def _attn_specs(qb):
    cur = pl.BlockSpec((qb * ATT_WIN, ATT_GW), lambda i: (i, 0))
    prev = pl.BlockSpec((ATT_WIN, ATT_GW), lambda i: (jnp.maximum(qb * i - 1, 0), 0))
    return cur, prev


def _attn_fwd(q, k, v, nb, name):
    l = q.shape[0]
    scale = ATT_E ** -0.5
    w = ATT_WIN

    qb = ATT_QB_FWD

    def body(q_ref, kc_ref, kp_ref, vc_ref, vp_ref, o_ref, lse_ref):
        i = pl.program_id(0)
        masks = [_attn_mask2(qb * i + b, nb) for b in range(qb)]
        for h in range(ATT_HG):
            sl = slice(h * ATT_E, (h + 1) * ATT_E)
            k_ext = jnp.concatenate([kp_ref[:, sl], kc_ref[:, sl]], axis=0)
            v_ext = jnp.concatenate([vp_ref[:, sl], vc_ref[:, sl]], axis=0)
            for b in range(qb):
                r, kr = slice(b * w, (b + 1) * w), slice(b * w, (b + 2) * w)
                s = jnp.where(masks[b], _nt_dot(q_ref[r, sl], k_ext[kr]) * scale, NEG_INF)
                mx = jnp.max(s, axis=-1, keepdims=True)
                p = jnp.exp(s - mx)
                den = jnp.sum(p, axis=-1, keepdims=True)
                o_ref[r, sl] = _nn_dot(p, v_ext[kr]) / den
                lse_ref[r, sl] = jnp.broadcast_to(mx + jnp.log(den), (w, ATT_E))

    cur, prev = _attn_specs(qb)
    return pl.pallas_call(
        body, grid=(l // (qb * w),), in_specs=[cur, cur, prev, cur, prev], out_specs=[cur, cur],
        out_shape=[jax.ShapeDtypeStruct((l, ATT_GW), F32)] * 2,
        compiler_params=_params(("parallel",)), name=name,
    )(q, k, k, v, v)


def _attn_bwd(q, k, v, do, lse, dd, nb, name):
    l = q.shape[0]
    scale = ATT_E ** -0.5
    w = ATT_WIN
    nblk = l // w

    def body(q_ref, kc_ref, kp_ref, vc_ref, vp_ref, do_ref, lse_ref, dd_ref, qn_ref, don_ref, lsen_ref, ddn_ref,
             dq_ref, dk_ref, dv_ref, dk_acc, dv_acc):
        i = pl.program_id(0)
        masks = [_attn_mask2(ATT_QB * i + b, nb) for b in range(ATT_QB)]
        nxt = ATT_QB * (i + 1)
        nxt_attends = jnp.logical_and(nxt < nblk, (nxt % nb) != 0)
        qi = lax.broadcasted_iota(jnp.int32, (w, w), 0)
        kj = lax.broadcasted_iota(jnp.int32, (w, w), 1)
        mask_n = jnp.logical_and(kj >= qi, nxt_attends)
        dk_acc[...] = jnp.zeros_like(dk_acc)
        dv_acc[...] = jnp.zeros_like(dv_acc)
        for h in range(ATT_HG):
            sl, col = slice(h * ATT_E, (h + 1) * ATT_E), slice(h * ATT_E, h * ATT_E + 1)
            k_ext = jnp.concatenate([kp_ref[:, sl], kc_ref[:, sl]], axis=0)
            v_ext = jnp.concatenate([vp_ref[:, sl], vc_ref[:, sl]], axis=0)
            for b in range(ATT_QB):
                r, kr = slice(b * w, (b + 1) * w), slice(b * w, (b + 2) * w)
                qh, doh, k2, v2 = q_ref[r, sl], do_ref[r, sl], k_ext[kr], v_ext[kr]
                p = jnp.where(masks[b], jnp.exp(_nt_dot(qh, k2) * scale - lse_ref[r, col]), 0.0)
                ds = p * (_nt_dot(doh, v2) - dd_ref[r, col]) * scale
                dq_ref[r, sl] = _nn_dot(ds, k2).astype(dq_ref.dtype)
                dk2, dv2 = _tn_dot(ds, qh), _tn_dot(p, doh)
                dk_acc[r, sl] += dk2[w:]
                dv_acc[r, sl] += dv2[w:]
                if b > 0:
                    rp = slice((b - 1) * w, b * w)
                    dk_acc[rp, sl] += dk2[:w]
                    dv_acc[rp, sl] += dv2[:w]
            last = slice((ATT_QB - 1) * w, ATT_QB * w)
            kl, vl, qn, don = kc_ref[last, sl], vc_ref[last, sl], qn_ref[:, sl], don_ref[:, sl]
            pn = jnp.where(mask_n, jnp.exp(_nt_dot(qn, kl) * scale - lsen_ref[:, col]), 0.0)
            dsn = pn * (_nt_dot(don, vl) - ddn_ref[:, col]) * scale
            dk_acc[last, sl] += _tn_dot(dsn, qn)
            dv_acc[last, sl] += _tn_dot(pn, don)
        dk_ref[...] = dk_acc[...].astype(dk_ref.dtype)
        dv_ref[...] = dv_acc[...].astype(dv_ref.dtype)

    cur, prev = _attn_specs(ATT_QB)
    nxt_spec = pl.BlockSpec((w, ATT_GW), lambda i: (jnp.minimum(ATT_QB * (i + 1), nblk - 1), 0))
    return pl.pallas_call(
        body, grid=(l // (ATT_QB * w),),
        in_specs=[cur, cur, prev, cur, prev, cur, cur, cur, nxt_spec, nxt_spec, nxt_spec, nxt_spec],
        out_specs=[cur] * 3, out_shape=[jax.ShapeDtypeStruct((l, ATT_GW), BF16)] * 3,
        scratch_shapes=[pltpu.VMEM((ATT_QB * w, ATT_GW), F32)] * 2,
        compiler_params=_params(("parallel",)), name=name,
    )(q, k, k, v, v, do, lse, dd, q, do, lse, dd)


def _to_perm(a, d):
    if d == 1:
        return a
    l, c = a.shape
    return a.reshape(l // d, d, c).transpose(1, 0, 2).reshape(l, c)


def _from_perm(a, d):
    if d == 1:
        return a
    l, c = a.shape
    return a.reshape(d, l // d, c).transpose(1, 0, 2).reshape(l, c)


def _merged_attn_proj(o_g, lse_g, w_t, name, tm=1024):
    l = o_g[0].shape[0]
    ng = len(o_g)

    def body(*refs):
        o_refs, l_refs, (w_ref, oa_ref, lt_ref, br_ref) = refs[:ng], refs[ng:2 * ng], refs[2 * ng:]
        ls = [r[...] for r in l_refs]
        mx = ls[0]
        for lv in ls[1:]:
            mx = jnp.maximum(mx, lv)
        es = [jnp.exp(lv - mx) for lv in ls]
        tot, acc = es[0], es[0] * o_refs[0][...]
        for e, o_ref in zip(es[1:], o_refs[1:]):
            tot, acc = tot + e, acc + e * o_ref[...]
        o = acc / tot
        oa_ref[...] = o
        lt_ref[...] = mx + jnp.log(tot)
        br_ref[...] = _nt_dot(o, w_ref[...]).astype(br_ref.dtype)

    row = pl.BlockSpec((tm, ATT_GW), lambda i: (i, 0))
    return pl.pallas_call(
        body, grid=(l // tm,),
        in_specs=[row] * (2 * ng) + [pl.BlockSpec(w_t.shape, lambda i: (0, 0))],
        out_specs=[row, row, pl.BlockSpec((tm, w_t.shape[0]), lambda i: (i, 0))],
        out_shape=[jax.ShapeDtypeStruct((l, ATT_GW), F32)] * 2 + [jax.ShapeDtypeStruct((l, w_t.shape[0]), BF16)],
        compiler_params=_params(("parallel",)), name=name,
    )(*o_g, *lse_g, w_t)


def _mem_probs(qh, kh):
    s = _nt_dot(qh, kh) * (MEM_E ** -0.5)
    e = jnp.exp(s - jnp.max(s, axis=-1, keepdims=True))
    return e / jnp.sum(e, axis=-1, keepdims=True)


def _mem_fwd(mq, kv, w_t, name, tm=1024):
    l, nm = mq.shape[0], kv.shape[0]

    def body(q_ref, kv_ref, w_ref, o_ref, br_ref):
        for h in range(MEM_H):
            sl = slice(h * MEM_E, (h + 1) * MEM_E)
            p = _mem_probs(q_ref[:, sl], kv_ref[:, sl])
            o_ref[:, sl] = _nn_dot(p, kv_ref[:, MEM_W + h * MEM_E:MEM_W + (h + 1) * MEM_E]).astype(o_ref.dtype)
        br_ref[...] = _nt_dot(o_ref[...], w_ref[...]).astype(br_ref.dtype)

    return pl.pallas_call(
        body, grid=(l // tm,),
        in_specs=[pl.BlockSpec((tm, MEM_W), lambda i: (i, 0)), pl.BlockSpec((nm, 2 * MEM_W), lambda i: (0, 0)),
                  pl.BlockSpec(w_t.shape, lambda i: (0, 0))],
        out_specs=[pl.BlockSpec((tm, MEM_W), lambda i: (i, 0)), pl.BlockSpec((tm, w_t.shape[0]), lambda i: (i, 0))],
        out_shape=[jax.ShapeDtypeStruct((l, MEM_W), BF16), jax.ShapeDtypeStruct((l, w_t.shape[0]), BF16)],
        compiler_params=_params(("parallel",)), name=name,
    )(mq, kv, w_t)


def _mem_bwd(mq, kv, dmo, name, tm=1024):
    l, nm = mq.shape[0], kv.shape[0]
    scale = MEM_E ** -0.5

    def body(q_ref, kv_ref, do_ref, dq_ref, dkv_ref):
        @pl.when(pl.program_id(0) == 0)
        def _():
            dkv_ref[...] = jnp.zeros_like(dkv_ref)

        for h in range(MEM_H):
            sl = slice(h * MEM_E, (h + 1) * MEM_E)
            vsl = slice(MEM_W + h * MEM_E, MEM_W + (h + 1) * MEM_E)
            qh, kh, vh, doh = q_ref[:, sl], kv_ref[:, sl], kv_ref[:, vsl], do_ref[:, sl]
            p = _mem_probs(qh, kh)
            dp = _nt_dot(doh, vh)
            ds = p * (dp - jnp.sum(dp * p, axis=-1, keepdims=True)) * scale
            dq_ref[:, sl] = _nn_dot(ds, kh).astype(dq_ref.dtype)
            dkv_ref[:, sl] += _tn_dot(ds, qh)
            dkv_ref[:, vsl] += _tn_dot(p, doh)

    row = pl.BlockSpec((tm, MEM_W), lambda i: (i, 0))
    full = pl.BlockSpec((nm, 2 * MEM_W), lambda i: (0, 0))
    return pl.pallas_call(
        body, grid=(l // tm,), in_specs=[row, full, row], out_specs=[row, full],
        out_shape=[jax.ShapeDtypeStruct((l, MEM_W), BF16), jax.ShapeDtypeStruct((nm, 2 * MEM_W), F32)],
        compiler_params=_params(("arbitrary",)), name=name,
    )(mq, kv, dmo)


def _gated_out_proj(zg, branches, b_gate, w_o, x, g2, name, tm=512):
    l, d = x.shape
    nbr = len(branches)

    def body(zg_ref, *rest):
        br_refs, (bg_ref, w_ref, x_ref, g2_ref, m_ref, h_ref, n_ref) = rest[:nbr], rest[nbr:]
        merged = jnp.zeros((tm, d), F32)
        for i, br_ref in enumerate(br_refs):
            cols = slice(i * d, (i + 1) * d)
            merged += _sigmoid(zg_ref[:, cols].astype(F32) + bg_ref[:, cols]) * br_ref[...].astype(F32)
        mb = merged.astype(BF16)
        m_ref[...] = mb
        hv = jnp.dot(mb.astype(_MXU), w_ref[...].astype(_MXU), preferred_element_type=F32) + x_ref[...]
        h_ref[...] = hv
        rs = lax.rsqrt(jnp.mean(hv * hv, axis=-1, keepdims=True) + RMS_EPS)
        n_ref[...] = (hv * rs * g2_ref[...]).astype(n_ref.dtype)

    row = lambda c: pl.BlockSpec((tm, c), lambda i: (i, 0))
    full = lambda a: pl.BlockSpec(a.shape, lambda i: (0, 0))
    return pl.pallas_call(
        body, grid=(l // tm,),
        in_specs=[row(nbr * d)] + [row(d)] * nbr + [full(b_gate), full(w_o), row(d), full(g2)],
        out_specs=[row(d)] * 3,
        out_shape=[jax.ShapeDtypeStruct((l, d), BF16), jax.ShapeDtypeStruct((l, d), F32),
                   jax.ShapeDtypeStruct((l, d), BF16)],
        compiler_params=_params(("parallel",)), name=name,
    )(zg, *branches, b_gate, w_o, x, g2)


def _discretize(lam_re, lam_im, log_dt, b_re, b_im):
    dt = jnp.exp(log_dt)[:, None]
    mag = jnp.exp(lam_re * dt)
    a_re, a_im = mag * jnp.cos(lam_im * dt), mag * jnp.sin(lam_im * dt)
    nr, ni = a_re - 1.0, a_im
    den = lam_re * lam_re + lam_im * lam_im
    coef_re = (nr * lam_re + ni * lam_im) / den
    coef_im = (ni * lam_re - nr * lam_im) / den
    bb_re = coef_re[..., None] * b_re - coef_im[..., None] * b_im
    bb_im = coef_re[..., None] * b_im + coef_im[..., None] * b_re
    return a_re, a_im, bb_re, bb_im


def _tiled(re, im):
    r = re.shape[0]
    both = jnp.concatenate([re.reshape(r, -1, SCAN_W), im.reshape(r, -1, SCAN_W)], axis=2)
    return both.reshape(r, 2 * re.shape[1])


def _untiled(x):
    r = x.shape[0]
    t = x.reshape(r, -1, 2 * SCAN_W)
    return t[:, :, :SCAN_W].reshape(r, -1), t[:, :, SCAN_W:].reshape(r, -1)


def _bd_in(bb):
    return jnp.einsum("gph,gk->ghkp", bb, jnp.eye(SSM_G, dtype=bb.dtype)).reshape(SSM_W, SSM_S)


def _bd_diag(x):
    gb = SSM_G // SSM_BD
    t = x.reshape(SSM_BD, gb, SSM_H, gb, SSM_P)
    return jnp.einsum("bghgp->bghp", t).reshape(SSM_G, SSM_H, SSM_P)


_ANY = pl.BlockSpec(memory_space=pl.ANY)
_MESH = pl.DeviceIdType.MESH


def _allgather(x, name):
    def body(x_ref, out_ref, send_sems, recv_sems, local_sem):
        mx, my, mc = lax.axis_index("x"), lax.axis_index("y"), lax.axis_index("c")
        me, sibling = (mx, my, mc), (mx, my, 1 - mc)
        chips = [(1 - mx, my), (mx, 1 - my), (1 - mx, 1 - my)]

        def blk(px, py, pc):
            return out_ref.at[4 * px + 2 * py + pc]

        def copy(k, block, to, src=None):
            return pltpu.make_async_remote_copy(
                src_ref=blk(*block) if src is None else src, dst_ref=blk(*block),
                send_sem=send_sems.at[k], recv_sem=recv_sems.at[k], device_id=to, device_id_type=_MESH)

        mine = pltpu.make_async_copy(x_ref, blk(*me), local_sem)
        mine.start()
        first = [copy(0, me, sibling, src=x_ref)]
        first += [copy(1 + j, me, (*chip, mc), src=x_ref) for j, chip in enumerate(chips)]
        for cp in first:
            cp.start()
        passed = [copy(4 + j, (*chip, mc), sibling) for j, chip in enumerate(chips)]
        for j, chip in enumerate(chips):
            copy(1 + j, (*chip, mc), me).wait_recv()
            passed[j].start()
        copy(0, sibling, me).wait_recv()
        for j, chip in enumerate(chips):
            copy(4 + j, (*chip, 1 - mc), me).wait_recv()
        for cp in first + passed:
            cp.wait_send()
        mine.wait()

    return pl.pallas_call(
        body, out_shape=jax.ShapeDtypeStruct((N_DEV,) + x.shape, x.dtype), in_specs=[_ANY], out_specs=_ANY,
        scratch_shapes=[pltpu.SemaphoreType.DMA((7,)), pltpu.SemaphoreType.DMA((7,)), pltpu.SemaphoreType.DMA],
        name=name,
    )(x)


def _pair_exchange(g, name):
    def body(g_ref, out_ref, send_sems, recv_sems):
        mx, my, mc = lax.axis_index("x"), lax.axis_index("y"), lax.axis_index("c")
        copies = [pltpu.make_async_remote_copy(
            src_ref=g_ref.at[2 * k + (1 - mc)], dst_ref=out_ref.at[k], send_sem=send_sems.at[k],
            recv_sem=recv_sems.at[k], device_id=(mx, my, 1 - mc), device_id_type=_MESH) for k in range(4)]
        for cp in copies:
            cp.start()
        for cp in copies:
            cp.wait()

    return pl.pallas_call(
        body, out_shape=jax.ShapeDtypeStruct((4,) + g.shape[1:], g.dtype), in_specs=[_ANY], out_specs=_ANY,
        scratch_shapes=[pltpu.SemaphoreType.DMA((4,)), pltpu.SemaphoreType.DMA((4,))], name=name,
    )(g)


_HBM = pl.BlockSpec(memory_space=pltpu.HBM)
_SEM = pl.BlockSpec(memory_space=pltpu.SEMAPHORE)
_EFFECT = pltpu.SideEffectType.DATAFLOW_SIDE_EFFECTING
_TOKEN = jax.ShapeDtypeStruct((8, 128), F32)


def _peer(rel):
    pos = (lax.axis_index("x"), lax.axis_index("y"), lax.axis_index("c"))
    return tuple(1 - p if (rel >> (2 - i)) & 1 else p for i, p in enumerate(pos))


def _index_of(dev):
    return 4 * dev[0] + 2 * dev[1] + dev[2]


def _split_copies(src_ref, land_ref, sems, plan):
    n = len(plan)
    return [pltpu.make_async_remote_copy(
        src_ref=src_ref if s is None else src_ref.at[s], dst_ref=land_ref.at[d], send_sem=sems[k],
        recv_sem=sems[n + k], device_id=peer, device_id_type=_MESH) for k, (s, d, peer) in enumerate(plan)]


def _split_start(src, n_land, plan_fn, after, name):
    blk = src.shape[-2:]
    land = lax.empty((n_land,) + blk, src.dtype)
    n = len(plan_fn())

    def body(src_ref, land_ref, after_ref, *outs):
        for cp in _split_copies(src_ref, land_ref, outs[:2 * n], plan_fn()):
            cp.start()
        outs[2 * n + 2][...] = jnp.zeros_like(outs[2 * n + 2])

    res = pl.pallas_call(
        body, name=name,
        out_shape=(pltpu.SemaphoreType.DMA(()),) * (2 * n)
        + (pltpu.HBM(src.shape, src.dtype), pltpu.HBM(land.shape, land.dtype), _TOKEN),
        in_specs=(_HBM, _HBM, _ANY),
        out_specs=(_SEM,) * (2 * n) + (_HBM, _HBM, pl.BlockSpec(memory_space=pltpu.VMEM)),
        input_output_aliases={0: 2 * n, 1: 2 * n + 1},
        compiler_params=pltpu.CompilerParams(has_side_effects=_EFFECT),
    )(pltpu.with_memory_space_constraint(src, pltpu.HBM), pltpu.with_memory_space_constraint(land, pltpu.HBM), after)
    return res[:2 * n], res[2 * n], res[2 * n + 1], res[2 * n + 2]


def _split_wait(sems, src, land, plan_fn, after, name):
    n = len(sems) // 2

    def body(src_ref, land_ref, *rest):
        for cp in _split_copies(src_ref, land_ref, rest[:2 * n], plan_fn()):
            cp.wait_send()
            cp.wait_recv()

    return pl.pallas_call(
        body, name=name,
        out_shape=(pltpu.HBM(src.shape, src.dtype), pltpu.HBM(land.shape, land.dtype)),
        in_specs=(_HBM, _HBM) + (_SEM,) * (2 * n) + (_ANY,), out_specs=(_HBM, _HBM),
        input_output_aliases={0: 0, 1: 1},
        compiler_params=pltpu.CompilerParams(has_side_effects=_EFFECT),
    )(src, land, *sems, after)


def _gather_plan():
    me = _index_of(_peer(0))
    return [(None, me, _peer(rel)) for rel in range(1, N_DEV)]


def _gather_wait_plan():
    return [(None, _index_of(_peer(rel)), _peer(rel)) for rel in range(1, N_DEV)]


def _chip_plan():
    return [(_index_of(_peer(rel)) // 2, j, _peer(rel)) for j, rel in enumerate((4, 2, 6))]


def _owner_plan():
    return [(_index_of(_peer(rel)), rel - 1, _peer(rel)) for rel in range(1, N_DEV)]


def _pair_sum(g, t1, my_c, name, tr):
    _, r, c = g.shape

    def body(c_ref, g_ref, t_ref, o_ref, ob_ref):
        s = g_ref[...] + t_ref[...]
        o_ref[...] = s
        ob_ref[...] = s.astype(BF16)

    blk = pl.BlockSpec((None, tr, c), lambda k, i, cr: (k, i, 0))
    return pl.pallas_call(
        body,
        grid_spec=pltpu.PrefetchScalarGridSpec(
            num_scalar_prefetch=1, grid=(4, r // tr),
            in_specs=[pl.BlockSpec((None, tr, c), lambda k, i, cr: (2 * k + cr[0], i, 0)), blk],
            out_specs=[blk, blk]),
        out_shape=[jax.ShapeDtypeStruct((4, r, c), F32), jax.ShapeDtypeStruct((4, r, c), BF16)],
        compiler_params=_params(("parallel", "parallel")), name=name,
    )(my_c, g, t1)


def _adam_math(g, w, m, v):
    m = ADAM_B1 * m + (1.0 - ADAM_B1) * g
    v = ADAM_B2 * v + (1.0 - ADAM_B2) * (g * g)
    m_hat = m / (1.0 - ADAM_B1 ** ADAM_STEP)
    v_hat = v / (1.0 - ADAM_B2 ** ADAM_STEP)
    delta = -ADAM_LR * (m_hat / (jnp.sqrt(v_hat) + ADAM_EPS) + ADAM_WD * w)
    return delta, m, v


def _grad_sum(own, own_index, recv, name, tr):
    _, r, c = own.shape
    n = recv.shape[0]

    def body(k_ref, own_ref, *rest):
        g = own_ref[...]
        for recv_ref in rest[:n]:
            g = g + recv_ref[...].astype(F32)
        rest[n][...] = g

    def slot(j):
        return pl.BlockSpec((None, tr, c), lambda i, kr: (j, i, 0))

    return pl.pallas_call(
        body,
        grid_spec=pltpu.PrefetchScalarGridSpec(
            num_scalar_prefetch=1, grid=(r // tr,),
            in_specs=[pl.BlockSpec((None, tr, c), lambda i, kr: (kr[0], i, 0))] + [slot(j) for j in range(n)],
            out_specs=pl.BlockSpec((tr, c), lambda i, kr: (i, 0))),
        out_shape=jax.ShapeDtypeStruct((r, c), F32),
        compiler_params=_params(("parallel",)), name=name,
    )(own_index, own, *([recv] * n))


def _adam_many(g, w, m, v, row_tiles, name):
    n = len(g)

    def body(*refs):
        ins, outs = refs[:4 * n], refs[4 * n:]
        for i in range(n):
            res = _adam_math(ins[i][...], ins[n + i][...], ins[2 * n + i][...], ins[3 * n + i][...])
            for kind in range(3):
                outs[kind * n + i][...] = res[kind]

    def spec(a):
        blk = (a.shape[0] // row_tiles,) + a.shape[1:]
        return pl.BlockSpec(blk, lambda t, nd=a.ndim: (t,) + (0,) * (nd - 1))

    specs = [spec(a) for a in g]
    res = pl.pallas_call(
        body, grid=(row_tiles,), in_specs=specs * 4, out_specs=specs * 3,
        out_shape=[jax.ShapeDtypeStruct(a.shape, F32) for a in g] * 3,
        compiler_params=_params(("parallel",)), name=name,
    )(*g, *w, *m, *v)
    return res[:n], res[n:2 * n], res[2 * n:]


def _sum8(g8, name):
    _, r, c = g8.shape

    def body(g_ref, o_ref):
        acc = g_ref[0]
        for j in range(1, N_DEV):
            acc = acc + g_ref[j]
        o_ref[...] = acc

    return pl.pallas_call(
        body, grid=(1,), in_specs=[pl.BlockSpec((N_DEV, r, c), lambda i: (0, 0, 0))],
        out_specs=pl.BlockSpec((r, c), lambda i: (0, 0)), out_shape=jax.ShapeDtypeStruct((r, c), F32),
        compiler_params=_params(("arbitrary",)), name=name,
    )(g8)


def _pack(arrs, pad_rows=8):
    flat = jnp.concatenate([a.reshape(-1) for a in arrs])
    n = flat.shape[0]
    q = PACK_C * pad_rows
    tot = -(-n // q) * q
    if tot != n:
        flat = jnp.concatenate([flat, jnp.zeros((tot - n,), flat.dtype)])
    return flat.reshape(tot // PACK_C, PACK_C)


def _unpack(buf, shapes):
    flat = buf.reshape(-1)
    out, off = [], 0
    for s in shapes:
        n = int(np.prod(s))
        out.append(flat[off:off + n].reshape(s))
        off += n
    return out


GROUPS = (("w_in",),
          ("w_glu", "w_ssm_br", "w_mem_br", "w_attn_br"),
          ("w_up", "w_down"),
          ("w_mem_kv", "w_o"))
GROUP_TR = (400, 384, 512, 256)
MLP_GROUP = 2
MIXER_GROUPS = (1, 3)
ATTN_BR_FOLD = 2


def _stored_shape(name):
    r, c, ax = BIG_SHAPE[name]
    rows, cols = (r // N_DEV, c) if ax == 0 else (c // N_DEV, r)
    return (rows // ATTN_BR_FOLD, cols * ATTN_BR_FOLD) if name == "w_attn_br" else (rows, cols)


def _stored(shard, name):
    a = shard[0].T if BIG_SHAPE[name][2] == 1 else shard[0]
    return a.reshape(_stored_shape(name))


def _unstored(a, name):
    r, c, ax = BIG_SHAPE[name]
    if ax == 0:
        return a.reshape(1, r // N_DEV, c)
    return a.reshape(c // N_DEV, r).T[None]


def _pack_group(d, names):
    return jnp.concatenate([_stored(d[n], n) for n in names], axis=0)


def _split_group(buf, names):
    out, off = {}, 0
    for n in names:
        rows = _stored_shape(n)[0]
        out[n] = buf[..., off:off + rows, :]
        off += rows
    return out


def _full_stored(stacked, name):
    r, c, ax = BIG_SHAPE[name]
    return stacked.reshape((r, c) if ax == 0 else (c, r))


def _stacked_stored(full, name):
    return full.reshape((N_DEV,) + _stored_shape(name))


def _gelu_parts(x):
    c0, c1 = math.sqrt(2.0 / math.pi), 0.044715
    th = jnp.tanh(c0 * (x + c1 * x * x * x))
    return th, c0, c1


def _local_step(x, mem, tgt, wb, sp, late_weights, grads_ready, small_grads_ready):
    l = x.shape[0]
    w_a, w_g = wb["w_in"][:ZA_W], wb["w_in"][ZA_W:]

    a_re, a_im, bb_re, bb_im = _discretize(sp["ssm_lambda_re"], sp["ssm_lambda_im"], sp["ssm_log_dt"],
                                           sp["ssm_b_re"], sp["ssm_b_im"])
    a_pair = jnp.stack([a_re.reshape(1, SSM_S), a_im.reshape(1, SSM_S)])
    a_conj = jnp.stack([a_re.reshape(1, SSM_S), -a_im.reshape(1, SSM_S)])
    b_re_t, b_im_t = _bd_in(bb_re).astype(BF16), _bd_in(bb_im).astype(BF16)
    c_re_t = _bd_in(sp["ssm_c_re"].transpose(0, 2, 1)).astype(BF16)
    c_im_t = (-_bd_in(sp["ssm_c_im"].transpose(0, 2, 1))).astype(BF16)
    d_row = sp["ssm_d"].reshape(1, SSM_W)

    n1 = _rms_fwd(x, sp["norm1_g"], "rms1")
    za = _mm(n1, w_a, [BF16], tb=True, name="in_proj_a", tn=1664)
    zg = _mm(n1, w_g, [BF16], tb=True, name="in_proj_g")
    for gi in MIXER_GROUPS:
        wb = {**wb, **late_weights(gi, za)}
    u = za[:, :SSM_W]
    mq = za[:, ZA_W - MEM_W:]

    u_s = _scan_order(u)
    s_all = _ssm_scan(u_s, b_re_t, b_im_t, a_pair, reverse=False, name="ssm_scan_fwd")
    def gelu_epi(acc, ut, dr):
        y = acc + dr * ut.astype(F32)
        th, _, _ = _gelu_parts(y)
        return y, 0.5 * y * (1.0 + th)
    y0, y1 = [_time_order(t) for t in _mm(s_all, _tiled(c_re_t, c_im_t), [F32, BF16], tb=True, epi=gelu_epi,
                                          mn=[u_s], rows=[d_row], bd=SSM_BD, tm=2048, name="ssm_cs")]

    def glu_epi(acc, y1t, bg):
        t = acc + bg
        return t, y1t.astype(F32) * _sigmoid(t)
    t_glu, y2 = _mm(y1, wb["w_glu"], [F32, BF16], epi=glu_epi, mn=[y1], rows=[sp["b_glu"]], name="ssm_glu")
    br_ssm = _mm(y2, wb["w_ssm_br"], [BF16], tb=True, name="ssm_br")

    qkv_p, o_g, lse_g = [], [], []
    for g, d in enumerate(DILATIONS):
        nb = l // d // ATT_WIN
        cols = [za[:, SSM_W + (3 * j + g) * ATT_GW: SSM_W + (3 * j + g + 1) * ATT_GW] for j in range(3)]
        qp, kp, vp = [_to_perm(cc, d) for cc in cols]
        qkv_p.append((qp, kp, vp))
        og, lg = _attn_fwd(qp, kp, vp, nb, "attn_fwd%d" % g)
        o_g.append(_from_perm(og, d))
        lse_g.append(_from_perm(lg, d))

    o_att, lse_tot, br_attn = _merged_attn_proj(o_g, lse_g, wb["w_attn_br"], "attn_merge_br")

    mn = _rms_fwd(mem, sp["mem_norm_g"], "rms_mem")
    kv = _mm(mn, wb["w_mem_kv"], [BF16], name="mem_kv")
    mo, br_mem = _mem_fwd(mq, kv, wb["w_mem_br"], "mem_attn_fwd")

    merged, h1, n2 = _gated_out_proj(zg, [br_ssm, br_attn, br_mem], sp["b_gate"], wb["w_o"], x, sp["norm2_g"],
                                     "gated_o_proj")

    def up_epi(acc):
        ra = jnp.maximum(acc, 0.0)
        return ra * ra, ra
    wm = late_weights(MLP_GROUP, n2)
    f_act, r_act = _mm(n2, wm["w_up"], [BF16, BF16], tb=True, epi=up_epi, name="mlp_up")
    def down_epi(acc, ht, tv, gf):
        hv = acc + ht
        rs = lax.rsqrt(jnp.mean(hv * hv, axis=-1, keepdims=True) + RMS_EPS)
        err = hv * rs * gf - tv
        dh, dgf = _rms_bwd_tile(hv, err * (1.0 / D_MODEL), gf)
        return dh, dgf, _colsum(err * err) * (0.5 / D_MODEL)
    dh2, d_final_g, loss_cols = _mm(f_act, wm["w_down"], [F32], epi=down_epi, mn=[h1, tgt], rows=[sp["final_g"]],
                                    n_sums=2, tk=1024, name="mlp_down")
    loss = jnp.sum(loss_cols, axis=1, keepdims=True)

    gw, gs = {}, {"final_g": d_final_g, "loss": loss}
    d_act = _mm(dh2, wm["w_down"], [BF16], tb=True, epi=lambda acc, ra: (acc * 2.0 * ra.astype(F32),), mn=[r_act],
                name="mlp_down_dx")
    dw_down = _mm(f_act, dh2, [F32], ta=True, name="mlp_down_dw")
    dw_up = _mm(d_act, n2, [F32], ta=True, name="mlp_up_dw")
    token = grads_ready(MLP_GROUP, {"w_up": dw_up, "w_down": dw_down})
    def up_dx_epi(acc, ht, dht, g2):
        dx, dg = _rms_bwd_tile(ht, acc, g2)
        return dx + dht, dg
    dh1, gs["norm2_g"] = _mm(d_act, wm["w_up"], [F32], epi=up_dx_epi, mn=[h1, dh2],
                             rows=[sp["norm2_g"] + token[:1, :1]], n_sums=1, tk=1024, name="mlp_up_dx")
    gw["w_o"] = _mm(merged, dh1, [F32], ta=True, name="o_proj_dw")

    def gate_bwd_epi(dm, *tiles):
        dbr, dz = [], []
        for zt, bt, bias in zip(tiles[0:3], tiles[3:6], tiles[6:9]):
            gt = _sigmoid(zt.astype(F32) + bias)
            dbr.append(dm * gt)
            dz.append(dm * bt.astype(F32) * gt * (1.0 - gt))
        return (*dbr, *dz, *[_colsum(t) for t in dz])
    gate_bias = [sp["b_gate"][:, i * D_MODEL:(i + 1) * D_MODEL] for i in range(3)]
    res = _mm(dh1, wb["w_o"], [BF16] * 6, tb=True, epi=gate_bwd_epi, mn=[(zg, 0), (zg, 1), (zg, 2), br_ssm, br_attn, br_mem],
              rows=gate_bias, n_sums=3, tm=512, name="o_proj_dx")
    (dbr_ssm, dbr_attn, dbr_mem), dzg = res[0:3], res[3:6]
    gs["b_gate"] = jnp.concatenate(res[6:9], axis=1)

    gw["w_ssm_br"] = _mm(dbr_ssm, y2, [F32], ta=True, name="ssm_br_dw")
    def glu_bwd_epi(dy, y1t, tt):
        sg = _sigmoid(tt)
        dt = dy * y1t.astype(F32) * sg * (1.0 - sg)
        return dt, dy * sg, _colsum(dt)
    dt_glu, dy1a, gs["b_glu"] = _mm(dbr_ssm, wb["w_ssm_br"], [BF16, F32], epi=glu_bwd_epi, mn=[y1, t_glu], n_sums=1,
                                    name="ssm_br_dx")
    gw["w_glu"] = _mm(y1, dt_glu, [F32], ta=True, name="ssm_glu_dw")

    def gelu_bwd_epi(acc, dy1t, y0t, ut):
        th, c0, c1 = _gelu_parts(y0t)
        dg = 0.5 * (1.0 + th) + 0.5 * y0t * (1.0 - th * th) * c0 * (1.0 + 3.0 * c1 * y0t * y0t)
        dy = (acc + dy1t) * dg
        return dy, _colsum(dy * ut.astype(F32))
    dy0, gs["ssm_d"] = _mm(dt_glu, wb["w_glu"], [F32], tb=True, epi=gelu_bwd_epi, mn=[dy1a, y0, u], n_sums=1,
                           name="ssm_glu_dx")
    dy0_s = _scan_order(dy0)
    lam, da, d_b, d_c = _ssm_scan(dy0_s, c_re_t, c_im_t, a_conj, reverse=True, s_fwd=s_all, u=u_s,
                                  name="ssm_scan_bwd")
    du = _time_order(_mm(lam, _tiled(b_re_t, b_im_t), [BF16], tb=True,
                         epi=lambda acc, dyt, dr: (acc + dyt * dr,), mn=[dy0_s], rows=[d_row], bd=SSM_BD, tm=2048, name="ssm_bu_dx"))
    gs["a_re"], gs["a_im"] = da[0], da[1]
    (dbr, dbi), (dcr, dci) = _untiled(d_b), _untiled(d_c)
    gs["bb_re"], gs["bb_im"] = _bd_diag(dbr).transpose(0, 2, 1), _bd_diag(dbi).transpose(0, 2, 1)
    gs["ssm_c_re"], gs["ssm_c_im"] = _bd_diag(dcr), -_bd_diag(dci)

    gw["w_attn_br"] = _mm(dbr_attn, o_att, [F32], ta=True, name="attn_br_dw")

    def do_epi(acc, ot):
        prod = acc * ot
        head = lax.broadcasted_iota(jnp.int32, prod.shape, 1) // ATT_E
        dd = jnp.zeros_like(prod)
        for h in range(ATT_HG):
            dd = jnp.where(head == h, jnp.sum(jnp.where(head == h, prod, 0.0), axis=1, keepdims=True), dd)
        return acc, dd
    do_att, dd_att = _mm(dbr_attn, wb["w_attn_br"], [BF16, F32], epi=do_epi, mn=[o_att], name="attn_br_dx")
    dq_l, dk_l, dv_l = [], [], []
    for g, d in enumerate(DILATIONS):
        nb = l // d // ATT_WIN
        qp, kp, vp = qkv_p[g]
        dq, dk, dv = _attn_bwd(qp, kp, vp, _to_perm(do_att, d), _to_perm(lse_tot, d), _to_perm(dd_att, d),
                               nb, "attn_bwd%d" % g)
        dq_l.append(_from_perm(dq, d))
        dk_l.append(_from_perm(dk, d))
        dv_l.append(_from_perm(dv, d))

    gw["w_mem_br"] = _mm(dbr_mem, mo, [F32], ta=True, name="mem_br_dw")
    dmo = _mm(dbr_mem, wb["w_mem_br"], [BF16], name="mem_br_dx")
    dmq, dkv = _mem_bwd(mq, kv, dmo, "mem_attn_bwd")
    gw["w_mem_kv"] = _mm(mn, dkv, [F32], ta=True, name="mem_kv_dw")
    dmn = _mm(dkv, wb["w_mem_kv"], [F32], tb=True, name="mem_kv_dx")
    token = sum(grads_ready(gi, gw) for gi in MIXER_GROUPS)
    gs["mem_norm_g"] = _rms_bwd(mem, dmn, None, sp["mem_norm_g"] + token[:1, :1], "rms_mem_bwd")[1]

    dza = jnp.concatenate([du] + dq_l + dk_l + dv_l + [dmq], axis=1)
    dn_a = _mm(dza, w_a, [F32], name="in_proj_a_dx", tk=1664)
    dw_a = _mm(dza, n1, [F32], ta=True, name="in_proj_a_dw", tm=1664)
    dw_g = [_mm(dzg[i], n1, [F32], ta=True, name="in_proj_g_dw%d" % i) for i in range(3)]
    gw["w_in"] = jnp.concatenate([dw_a] + dw_g, axis=0)
    token = grads_ready(0, gw) + small_grads_ready(gs)
    def in_dx_epi(acc, pt, xt, dht, g1):
        dx, dg = _rms_bwd_tile(xt, acc + pt, g1)
        return dx + dht, dg
    w_gs = [w_g[i * D_MODEL:(i + 1) * D_MODEL] for i in range(3)]
    grad_x, gs["norm1_g"] = _mm(dzg[0], w_gs[0], [F32], pair2=(dzg[1], w_gs[1], dzg[2], w_gs[2]), epi=in_dx_epi,
                                mn=[dn_a, x, dh1],
                                rows=[sp["norm1_g"] + token[:1, :1]], n_sums=1, tm=512, name="in_proj_g_dx")
    return loss, grad_x, gs


_SMALL_GRAD_ORDER = ("norm1_g", "mem_norm_g", "b_gate", "a_re", "a_im", "bb_re", "bb_im", "ssm_c_re", "ssm_c_im",
                     "ssm_d", "b_glu", "norm2_g", "final_g", "loss")


def kernel(x, mem, norm1_g, mem_norm_g, w_in, b_gate, ssm_lambda_re, ssm_lambda_im, ssm_log_dt, ssm_b_re, ssm_b_im, ssm_c_re, ssm_c_im, ssm_d, w_glu, b_glu, w_ssm_br, w_attn_br, w_mem_kv, w_mem_br, w_o, norm2_g, w_up, w_down, final_g, loss_target, m_norm1_g, m_mem_norm_g, m_w_in, m_b_gate, m_ssm_lambda_re, m_ssm_lambda_im, m_ssm_log_dt, m_ssm_b_re, m_ssm_b_im, m_ssm_c_re, m_ssm_c_im, m_ssm_d, m_w_glu, m_b_glu, m_w_ssm_br, m_w_attn_br, m_w_mem_kv, m_w_mem_br, m_w_o, m_norm2_g, m_w_up, m_w_down, m_final_g, v_norm1_g, v_mem_norm_g, v_w_in, v_b_gate, v_ssm_lambda_re, v_ssm_lambda_im, v_ssm_log_dt, v_ssm_b_re, v_ssm_b_im, v_ssm_c_re, v_ssm_c_im, v_ssm_d, v_w_glu, v_b_glu, v_w_ssm_br, v_w_attn_br, v_w_mem_kv, v_w_mem_br, v_w_o, v_norm2_g, v_w_up, v_w_down, v_final_g):
    args = dict(locals())
    w = {n: args[n] for n in ALL_W}
    m = {n: args["m_" + n] for n in ALL_W}
    v = {n: args["v_" + n] for n in ALL_W}
    my_c = lax.axis_index("c").astype(jnp.int32).reshape(1)
    my_chip = (2 * lax.axis_index("x") + lax.axis_index("y")).astype(jnp.int32).reshape(1)

    w_pack = [_pack_group(w, names) for names in GROUPS]
    my_index = (4 * lax.axis_index("x") + 2 * lax.axis_index("y") + lax.axis_index("c")).astype(jnp.int32)
    zero = jnp.zeros((), jnp.int32)
    w_all = _allgather(w_pack[0].astype(BF16), "allgather_weights0")
    wb = {n: _full_stored(part, n) for n, part in _split_group(w_all, GROUPS[0]).items()}
    gathers = {gi: _split_start(w_pack[gi].astype(BF16), N_DEV, _gather_plan, w_all, "weights_gather_start%d" % gi)
               for gi in range(1, len(GROUPS))}

    def gathered(started, after, name):
        sems, src, land, _ = started
        src, land = _split_wait(sems, src, land, _gather_wait_plan, after, name)
        return lax.dynamic_update_slice(land, src[None], (my_index, zero, zero))

    def late_weights(gi, after):
        full = gathered(gathers[gi], after, "weights_gather_wait%d" % gi)
        return {n: _full_stored(part, n) for n, part in _split_group(full, GROUPS[gi]).items()}

    pending = {}

    def grads_ready(gi, grads):
        g_pack = jnp.concatenate([_stacked_stored(grads[n], n) for n in GROUPS[gi]], axis=1)
        if gi == 0:
            t1 = _pair_exchange(g_pack, "grad_pair_exchange%d" % gi)
            p_sum, p_bf = _pair_sum(g_pack, t1, my_c, "grad_pair_sum%d" % gi, GROUP_TR[gi])
            started = _split_start(p_bf, 3, _chip_plan, p_sum, "grad_chip_exchange_start%d" % gi)
            pending[gi] = (p_sum, my_chip, started, _chip_plan)
        else:
            started = _split_start(g_pack.astype(BF16), N_DEV - 1, _owner_plan, g_pack, "grad_exchange_start%d" % gi)
            pending[gi] = (g_pack, my_index.reshape(1), started, _owner_plan)
        return started[3]

    early_small = [n for n in _SMALL_GRAD_ORDER if n != "norm1_g"]
    small_started = []

    def small_grads_ready(gs):
        started = _split_start(_pack([gs[n] for n in early_small]), N_DEV, _gather_plan, gs["mem_norm_g"],
                               "small_grads_gather_start")
        small_started.append((started, [gs[n].shape for n in early_small]))
        return started[3]

    sp = {
        "norm1_g": norm1_g + sum(started[3][:1, :1] for started in gathers.values()), "mem_norm_g": mem_norm_g, "b_gate": b_gate, "b_glu": b_glu, "norm2_g": norm2_g,
        "final_g": final_g.reshape(1, D_MODEL),
        "ssm_lambda_re": ssm_lambda_re[0], "ssm_lambda_im": ssm_lambda_im[0], "ssm_log_dt": ssm_log_dt[0],
        "ssm_b_re": ssm_b_re[0], "ssm_b_im": ssm_b_im[0], "ssm_c_re": ssm_c_re[0], "ssm_c_im": ssm_c_im[0],
        "ssm_d": ssm_d[0],
    }
    loss, grad_x, gs = _local_step(x[0], mem[0], loss_target[0], wb, sp, late_weights, grads_ready,
                                     small_grads_ready)
    n1_started = _split_start(_pack([gs["norm1_g"]]), N_DEV, _gather_plan, grad_x, "norm1_grad_gather_start")

    big_g = {}
    for gi, names in enumerate(GROUPS):
        own, own_index, (sems, src, land, _), plan = pending[gi]
        recv = _split_wait(sems, src, land, plan, grad_x, "grad_exchange_wait%d" % gi)[1]
        g_pack = _grad_sum(own, own_index, recv, "grad_sum%d" % gi, GROUP_TR[gi])
        for n, part in _split_group(g_pack, names).items():
            big_g[n] = _unstored(part, n)
    rows_of = lambda d, names: [d[n].reshape(d[n].shape[-2:]) for n in names]
    big_out = _adam_many(rows_of(big_g, BIG), rows_of(w, BIG), rows_of(m, BIG), rows_of(v, BIG), 8, "adam_big")
    big = [big_g] + [{n: a[None] for n, a in zip(BIG, outs)} for outs in big_out]

    (sg_started, sg_shapes), = small_started
    sg_all = jnp.concatenate([gathered(sg_started, big_out[0][0], "small_grads_gather_wait"),
                              gathered(n1_started, big_out[0][0], "norm1_grad_gather_wait")], axis=1)
    sg_sum = _sum8(sg_all, "sum_small_grads")
    n1_rows = n1_started[1].shape[0]
    sg = dict(zip(early_small, _unpack(sg_sum[:-n1_rows], sg_shapes)))
    sg["norm1_g"] = _unpack(sg_sum[-n1_rows:], [gs["norm1_g"].shape])[0]
    _, disc_vjp = jax.vjp(_discretize, sp["ssm_lambda_re"], sp["ssm_lambda_im"], sp["ssm_log_dt"],
                          sp["ssm_b_re"], sp["ssm_b_im"])
    d_lre, d_lim, d_ldt, d_bre, d_bim = disc_vjp((sg["a_re"].reshape(SSM_G, SSM_P), sg["a_im"].reshape(SSM_G, SSM_P),
                                                  sg["bb_re"], sg["bb_im"]))
    small_grad = {
        "norm1_g": sg["norm1_g"], "mem_norm_g": sg["mem_norm_g"], "b_gate": sg["b_gate"],
        "ssm_lambda_re": d_lre, "ssm_lambda_im": d_lim, "ssm_log_dt": d_ldt, "ssm_b_re": d_bre, "ssm_b_im": d_bim,
        "ssm_c_re": sg["ssm_c_re"], "ssm_c_im": sg["ssm_c_im"], "ssm_d": sg["ssm_d"], "b_glu": sg["b_glu"],
        "norm2_g": sg["norm2_g"], "final_g": sg["final_g"],
    }
    small_grad = {n: small_grad[n].reshape(w[n].shape) for n in SMALL}

    def squeezed(a):
        return a.reshape(a.shape[1:]) if a.ndim > 2 else a.reshape(1, -1)

    sq = lambda d: [squeezed(d[n]) for n in SMALL]
    small_out = _adam_many(sq(small_grad), sq(w), sq(m), sq(v), 1, "adam_small")
    small = [small_grad] + [{n: a.reshape(w[n].shape) for n, a in zip(SMALL, outs)} for outs in small_out]

    outs = [sg["loss"][0, 0], grad_x[None]]
    for kind in range(4):
        for n in ALL_W:
            outs.append(big[kind][n] if n in BIG else small[kind][n])
    return tuple(outs)
```

```python
import math

import numpy as np
import jax
import jax.numpy as jnp
from jax import lax
from jax.experimental import pallas as pl
from jax.experimental.pallas import tpu as pltpu

F32 = jnp.float32
BF16 = jnp.bfloat16
_MXU = jnp.bfloat16

D_MODEL = 1024
SSM_G, SSM_H, SSM_P = 32, 16, 64
SSM_W = SSM_G * SSM_H
SSM_S = SSM_G * SSM_P
SSM_BD = 4
ATT_E = 64
ATT_HG = 4
ATT_GW = ATT_HG * ATT_E
ATT_WIN = 128
ATT_QB = 16
ATT_QB_FWD = 4
DILATIONS = (1, 4, 16)
MEM_H, MEM_E = 4, 128
MEM_W = MEM_H * MEM_E
ZA_W = SSM_W + 9 * ATT_GW + MEM_W
ZG_W = 3 * D_MODEL
IN_W = ZA_W + ZG_W
RMS_EPS = 1e-6
NEG_INF = -1e30

ADAM_LR, ADAM_B1, ADAM_B2, ADAM_EPS, ADAM_WD, ADAM_STEP = 0.001, 0.9, 0.999, 1e-08, 0.01, 10

N_DEV = 8
PACK_C = 512
_VMEM_LIMIT = 56 * 1024 * 1024
SUBLANES = 16
SCAN_SEG = 128
SCAN_CHAINS = 4
SCAN_UNROLL = 8
SCAN_W = 128

BIG = ("w_in", "w_glu", "w_ssm_br", "w_attn_br", "w_mem_kv", "w_mem_br", "w_o", "w_up", "w_down")
BIG_SHAPE = {
    "w_in": (D_MODEL, IN_W, 1), "w_glu": (SSM_W, SSM_W, 0), "w_ssm_br": (SSM_W, D_MODEL, 1),
    "w_attn_br": (ATT_GW, D_MODEL, 1), "w_mem_kv": (D_MODEL, 2 * MEM_W, 0), "w_mem_br": (MEM_W, D_MODEL, 1),
    "w_o": (D_MODEL, D_MODEL, 0), "w_up": (D_MODEL, 4 * D_MODEL, 1), "w_down": (4 * D_MODEL, D_MODEL, 0),
}
SMALL = ("norm1_g", "mem_norm_g", "b_gate", "ssm_lambda_re", "ssm_lambda_im", "ssm_log_dt", "ssm_b_re",
         "ssm_b_im", "ssm_c_re", "ssm_c_im", "ssm_d", "b_glu", "norm2_g", "final_g")
ALL_W = ("norm1_g", "mem_norm_g", "w_in", "b_gate", "ssm_lambda_re", "ssm_lambda_im", "ssm_log_dt", "ssm_b_re",
         "ssm_b_im", "ssm_c_re", "ssm_c_im", "ssm_d", "w_glu", "b_glu", "w_ssm_br", "w_attn_br", "w_mem_kv",
         "w_mem_br", "w_o", "norm2_g", "w_up", "w_down", "final_g")


def _params(sem):
    return pltpu.CompilerParams(dimension_semantics=sem, vmem_limit_bytes=_VMEM_LIMIT)


def _pick(n, cap):
    if n <= cap:
        return n
    t = (cap // 128) * 128
    while n % t:
        t -= 128
    return t


def _mm(a, b, outs, *, name, ta=False, tb=False, epi=None, mn=(), rows=(), pair2=None, bd=0, n_sums=0,
        tm=1024, tn=1024, tk=2048):
    ab = [a, b] + (list(pair2) if pair2 is not None else [])
    a_shape, b_shape = ab[0].shape, ab[1].shape
    m = a_shape[1] if ta else a_shape[0]
    k = a_shape[0] if ta else a_shape[1]
    n = b_shape[0] if tb else b_shape[1]
    assert k == (b_shape[1] if tb else b_shape[0]), (name, a_shape, b_shape)
    out_n = n
    if bd and ta:
        assert not tb
        tm, tn, tk = m // bd, n // bd, _pick(k, tk)
        grid, out_n = (bd, 1, k // tk), tn
        a_blk = ((tk, tm), lambda i, j, kk: (kk, i))
        b_blk = ((tk, tn), lambda i, j, kk: (kk, i))
        mn_spec = pl.BlockSpec((tm, tn), lambda i, j, kk: (i, 0))
    elif bd:
        tm, tn, tk = _pick(m, tm), n // bd, k // bd
        grid = (m // tm, bd, 1)
        a_blk = ((tm, tk), lambda i, j, kk: (i, j))
        b_blk = ((tn, tk) if tb else (tk, tn), lambda i, j, kk: (j, j))
        mn_spec = pl.BlockSpec((tm, tn), lambda i, j, kk: (i, j))
    else:
        tm, tn, tk = _pick(m, tm), _pick(n, tn), _pick(k, tk)
        grid = (m // tm, n // tn, k // tk)
        a_blk = ((tk, tm), lambda i, j, kk: (kk, i)) if ta else ((tm, tk), lambda i, j, kk: (i, kk))
        b_blk = ((tn, tk), lambda i, j, kk: (j, kk)) if tb else ((tk, tn), lambda i, j, kk: (kk, j))
        mn_spec = pl.BlockSpec((tm, tn), lambda i, j, kk: (i, j))

    ab_specs = [pl.BlockSpec(*(a_blk if q % 2 == 0 else b_blk)) for q in range(len(ab))]
    mn_arrays = [e[0] if isinstance(e, tuple) else e for e in mn]
    mn_specs = [pl.BlockSpec((tm, tn), lambda i, j, kk, c=e[1]: (i, c)) if isinstance(e, tuple) else mn_spec
                for e in mn]
    nk = grid[2]
    row_spec = pl.BlockSpec((1, tn), lambda i, j, kk: (0, j))
    n_ex, n_out = len(mn) + len(rows), len(outs)
    assert n_sums == 0 or (grid[1] == 1 and not bd)
    dims = (((0 if ta else 1,), (1 if tb else 0,)), ((), ()))

    def body(*refs):
        ab_refs, rest = refs[:len(ab)], refs[len(ab):]
        ex, o_refs = rest[:n_ex], rest[n_ex:n_ex + n_out]
        s_refs = rest[n_ex + n_out:n_ex + n_out + n_sums]
        first_row_tile = pl.program_id(0) == 0
        kk = pl.program_id(2)

        pairs = list(zip(ab_refs[0::2], ab_refs[1::2]))

        def product(pair):
            return lax.dot_general(pair[0][...].astype(_MXU), pair[1][...].astype(_MXU), dims,
                                   preferred_element_type=F32)

        def finish(total):
            vals = (total,) if epi is None else epi(total, *[r[...] for r in ex])
            for r, v in zip(o_refs, vals):
                r[...] = v.astype(r.dtype)
            for r, v in zip(s_refs, vals[n_out:]):
                r[...] = jnp.where(first_row_tile, v, r[...] + v)

        if nk == 1:
            total = product(pairs[0])
            for pair in pairs[1:]:
                total = total + product(pair)
            finish(total)
        else:
            acc = rest[-1]

            @pl.when(kk == 0)
            def _():
                acc[...] = jnp.zeros_like(acc)

            for pair in pairs:
                acc[...] += product(pair)

            @pl.when(kk == nk - 1)
            def _():
                finish(acc[...])

    res = pl.pallas_call(
        body, grid=grid,
        in_specs=ab_specs + mn_specs + [row_spec] * len(rows),
        out_specs=[mn_spec] * n_out + [row_spec] * n_sums,
        out_shape=[jax.ShapeDtypeStruct((m, out_n), dt) for dt in outs]
        + [jax.ShapeDtypeStruct((1, out_n), F32)] * n_sums,
        scratch_shapes=[pltpu.VMEM((tm, tn), F32)] if nk > 1 else [],
        compiler_params=_params(("arbitrary" if n_sums else "parallel", "parallel", "arbitrary")), name=name,
    )(*ab, *mn_arrays, *rows)
    return res[0] if n_out + n_sums == 1 else res


def _ew(fn, rows, bcs, out_rows, out_accs, *, name, tm=256):
    r = rows[0].shape[0]
    tm = min(tm, r)
    assert r % tm == 0
    nr, nb, no, na = len(rows), len(bcs), len(out_rows), len(out_accs)

    def body(*refs):
        i = pl.program_id(0)
        r_in, b_in = refs[:nr], refs[nr:nr + nb]
        o_r, o_a = refs[nr + nb:nr + nb + no], refs[nr + nb + no:]
        outs, accs = fn([x[...] for x in r_in], [x[...] for x in b_in])
        for ref, v in zip(o_r, outs):
            ref[...] = v.astype(ref.dtype)
        if na:
            @pl.when(i == 0)
            def _():
                for ref in o_a:
                    ref[...] = jnp.zeros_like(ref)

            for ref, v in zip(o_a, accs):
                ref[...] += v

    res = pl.pallas_call(
        body, grid=(r // tm,),
        in_specs=[pl.BlockSpec((tm, x.shape[1]), lambda i: (i, 0)) for x in rows]
        + [pl.BlockSpec((1, x.shape[1]), lambda i: (0, 0)) for x in bcs],
        out_specs=[pl.BlockSpec((tm, c), lambda i: (i, 0)) for c, _ in out_rows]
        + [pl.BlockSpec((1, c), lambda i: (0, 0)) for c in out_accs],
        out_shape=[jax.ShapeDtypeStruct((r, c), dt) for c, dt in out_rows]
        + [jax.ShapeDtypeStruct((1, c), F32) for c in out_accs],
        compiler_params=_params(("arbitrary",)), name=name,
    )(*rows, *bcs)
    return res


def _colsum(x):
    return jnp.sum(x, axis=0, keepdims=True)


def _sigmoid(x):
    return 1.0 / (1.0 + jnp.exp(-x))


def _rms_bwd_tile(xv, dv, g):
    rs = lax.rsqrt(jnp.mean(xv * xv, axis=-1, keepdims=True) + RMS_EPS)
    gd = dv * g
    dx = rs * gd - xv * (rs * rs * rs) * jnp.mean(gd * xv, axis=-1, keepdims=True)
    return dx, _colsum(dv * xv * rs)


def _rms_fwd(x, g, name):
    def fn(r, b):
        xv = r[0]
        rs = lax.rsqrt(jnp.mean(xv * xv, axis=-1, keepdims=True) + RMS_EPS)
        return [xv * rs * b[0]], []
    return _ew(fn, [x], [g], [(x.shape[1], BF16)], [], name=name, tm=1024)[0]


def _rms_bwd(x, dn, res, g, name):
    def fn(r, b):
        dx, dg = _rms_bwd_tile(r[0], r[1], b[0])
        if res is not None:
            dx = dx + r[2]
        return [dx], [dg]
    rows = [x, dn] + ([res] if res is not None else [])
    return _ew(fn, rows, [g], [(x.shape[1], F32)], [x.shape[1]], name=name)


def _scan_order(x):
    l, c = x.shape
    return x.reshape(l // (SUBLANES * SCAN_SEG), SUBLANES, SCAN_SEG, c).transpose(0, 2, 1, 3).reshape(l, c)


def _time_order(x):
    l, c = x.shape
    return x.reshape(l // (SUBLANES * SCAN_SEG), SCAN_SEG, SUBLANES, c).transpose(0, 2, 1, 3).reshape(l, c)


def _ssm_scan(x, w_re, w_im, a_pair, *, reverse, s_fwd=None, u=None, name):
    l = x.shape[0]
    seg, w = SCAN_SEG, SCAN_W
    bd_w = SSM_W // SSM_BD
    tiles_per_bd = SSM_S // SSM_BD // w
    nch = min(SCAN_CHAINS, l // (SUBLANES * seg))
    chain_rows = SUBLANES * seg
    tb = nch * chain_rows
    nt = l // tb
    with_da = s_fwd is not None
    assert reverse or not with_da

    def tt(t):
        return nt - 1 - t if reverse else t

    def body(*refs):
        if with_da:
            (x_ref, wr_ref, wi_ref, a_ref, sf_ref, sp_ref, u_ref, s_ref, da_ref, dw_ref, dx_ref,
             p_ref, c_ref, b_scr, l_scr) = refs
        else:
            x_ref, wr_ref, wi_ref, a_ref, s_ref, p_ref, c_ref, b_scr, l_scr = refs
        t_blk = pl.program_id(1)
        ar, ai = a_ref[0], a_ref[1]

        @pl.when(t_blk == 0)
        def _():
            def pstep(i, carry):
                pr, pi = carry
                p_ref[0, pl.ds(i, 1), :] = pr
                p_ref[1, pl.ds(i, 1), :] = pi
                return pr * ar - pi * ai, pr * ai + pi * ar

            lax.fori_loop(0, seg, pstep, (ar, ai))
            c_ref[...] = jnp.zeros_like(c_ref)
            if with_da:
                da_ref[...] = jnp.zeros_like(da_ref)
                dw_ref[...] = jnp.zeros_like(dw_ref)
                dx_ref[...] = jnp.zeros_like(dx_ref)

        xb = x_ref[...].astype(_MXU)
        b_scr[:, :w] = jnp.dot(xb, wr_ref[...], preferred_element_type=F32)
        b_scr[:, w:] = jnp.dot(xb, wi_ref[...], preferred_element_type=F32)
        arb, aib = jnp.broadcast_to(ar, (SUBLANES, w)), jnp.broadcast_to(ai, (SUBLANES, w))
        zero = jnp.zeros((SUBLANES, w), F32)

        def tile(g, step):
            return pl.ds(pl.multiple_of(g * chain_rows + step * SUBLANES, SUBLANES), SUBLANES)

        def rows(g, i):
            return tile(g, seg - 1 - i if reverse else i)

        def local_step(i, carry):
            out = []
            for g in range(nch):
                sr, si = carry[2 * g], carry[2 * g + 1]
                idx = rows(g, i)
                sr, si = arb * sr - aib * si + b_scr[idx, :w], arb * si + aib * sr + b_scr[idx, w:]
                l_scr[idx, :w] = sr
                l_scr[idx, w:] = si
                out += [sr, si]
            return tuple(out)

        def unrolled(step_fn, first):
            def trip(q, carry):
                for r in range(SCAN_UNROLL):
                    carry = step_fn(first + q * SCAN_UNROLL + r, carry)
                return carry
            return trip

        ends = lax.fori_loop(0, seg // SCAN_UNROLL, unrolled(local_step, 0), (zero,) * (2 * nch))

        a_seg_r, a_seg_i = p_ref[0, seg - 1:seg, :], p_ref[1, seg - 1:seg, :]
        cr, ci = c_ref[0], c_ref[1]
        sub = lax.broadcasted_iota(jnp.int32, (SUBLANES, w), 0)
        ins = [[zero, zero] for _ in range(nch)]
        order = [(g, k) for g in range(nch) for k in range(SUBLANES)]
        for g, k in (order[::-1] if reverse else order):
            ins[g] = [jnp.where(sub == k, cr, ins[g][0]), jnp.where(sub == k, ci, ins[g][1])]
            er, ei = ends[2 * g][k:k + 1], ends[2 * g + 1][k:k + 1]
            cr, ci = er + a_seg_r * cr - a_seg_i * ci, ei + a_seg_r * ci + a_seg_i * cr
        c_ref[0] = cr
        c_ref[1] = ci

        def fix(g, i):
            idx = rows(g, i)
            pr, pi = p_ref[0, pl.ds(i, 1), :], p_ref[1, pl.ds(i, 1), :]
            sr = l_scr[idx, :w] + pr * ins[g][0] - pi * ins[g][1]
            si = l_scr[idx, w:] + pr * ins[g][1] + pi * ins[g][0]
            s_ref[idx, :w] = sr.astype(s_ref.dtype)
            s_ref[idx, w:] = si.astype(s_ref.dtype)
            return sr, si

        if not with_da:
            def fix_step(i, carry):
                for g in range(nch):
                    fix(g, i)
                return carry

            lax.fori_loop(0, seg // SCAN_UNROLL, unrolled(fix_step, 0), 0)
        else:
            def adj_step(i, acc):
                acc_r, acc_i = acc
                for g in range(nch):
                    lr, li = fix(g, i)
                    prev = tile(g, seg - 2 - i)
                    fr, fi = sf_ref[prev, :w].astype(F32), sf_ref[prev, w:].astype(F32)
                    acc_r, acc_i = acc_r + lr * fr + li * fi, acc_i + li * fr - lr * fi
                return acc_r, acc_i

            acc = lax.fori_loop(0, seg // SCAN_UNROLL - 1, unrolled(adj_step, 0), (zero, zero))
            for i in range(seg - SCAN_UNROLL, seg - 1):
                acc = adj_step(i, acc)
            acc_r, acc_i = acc
            first_block = tt(t_blk) == 0
            for g in range(nch):
                lr, li = fix(g, seg - 1)
                seg_ends = tile(g, seg - 1)
                if g == 0:
                    pvr = jnp.where(first_block, 0.0, sp_ref[SUBLANES - 1:SUBLANES, :w].astype(F32))
                    pvi = jnp.where(first_block, 0.0, sp_ref[SUBLANES - 1:SUBLANES, w:].astype(F32))
                else:
                    pvr = sf_ref[g * chain_rows - 1:g * chain_rows, :w].astype(F32)
                    pvi = sf_ref[g * chain_rows - 1:g * chain_rows, w:].astype(F32)
                fr = jnp.where(sub == 0, pvr, pltpu.roll(sf_ref[seg_ends, :w].astype(F32), 1, 0))
                fi = jnp.where(sub == 0, pvi, pltpu.roll(sf_ref[seg_ends, w:].astype(F32), 1, 0))
                acc_r = acc_r + lr * fr + li * fi
                acc_i = acc_i + li * fr - lr * fi
            da_ref[0] += jnp.sum(acc_r, axis=0, keepdims=True)
            da_ref[1] += jnp.sum(acc_i, axis=0, keepdims=True)
            dw_ref[...] += _tn_dot(u_ref[...], s_ref[...])
            dx_ref[...] += _tn_dot(xb, sf_ref[...])

    x_spec = pl.BlockSpec((tb, bd_w), lambda j, t: (tt(t), j // tiles_per_bd))
    w_spec = pl.BlockSpec((bd_w, w), lambda j, t: (j // tiles_per_bd, j))
    d_spec = pl.BlockSpec((bd_w, 2 * w), lambda j, t: (j // tiles_per_bd, j % tiles_per_bd))
    a_spec = pl.BlockSpec((2, 1, w), lambda j, t: (0, 0, j))
    s_spec = pl.BlockSpec((tb, 2 * w), lambda j, t: (tt(t), j))
    in_specs, args = [x_spec, w_spec, w_spec, a_spec], [x, w_re, w_im, a_pair]
    out_specs, out_shape = [s_spec], [jax.ShapeDtypeStruct((l, 2 * SSM_S), BF16)]
    scratch = [pltpu.VMEM((2, seg, w), F32), pltpu.VMEM((2, 1, w), F32)] + [pltpu.VMEM((tb, 2 * w), F32)] * 2
    if with_da:
        in_specs += [s_spec, pl.BlockSpec((SUBLANES, 2 * w),
                                          lambda j, t: (jnp.maximum(tt(t) * (tb // SUBLANES) - 1, 0), j)),
                     x_spec]
        args += [s_fwd, s_fwd, u]
        out_specs += [a_spec, d_spec, d_spec]
        out_shape += ([jax.ShapeDtypeStruct((2, 1, SSM_S), F32)]
                      + [jax.ShapeDtypeStruct((SSM_W, 2 * SSM_S // SSM_BD), F32)] * 2)
    res = pl.pallas_call(
        body, grid=(SSM_S // w, nt), in_specs=in_specs, out_specs=out_specs, out_shape=out_shape,
        scratch_shapes=scratch, compiler_params=_params(("parallel", "arbitrary")), name=name,
    )(*args)
    return res if with_da else res[0]


def _nt_dot(x, y):
    return lax.dot_general(x.astype(_MXU), y.astype(_MXU), (((1,), (1,)), ((), ())), preferred_element_type=F32)


def _tn_dot(x, y):
    return lax.dot_general(x.astype(_MXU), y.astype(_MXU), (((0,), (0,)), ((), ())), preferred_element_type=F32)


def _nn_dot(x, y):
    return jnp.dot(x.astype(_MXU), y.astype(_MXU), preferred_element_type=F32)


def _attn_mask2(gb, nb):
    qi = lax.broadcasted_iota(jnp.int32, (ATT_WIN, 2 * ATT_WIN), 0)
    c = lax.broadcasted_iota(jnp.int32, (ATT_WIN, 2 * ATT_WIN), 1)
    has_prev = (gb % nb) != 0
    prev_ok = jnp.logical_and(jnp.logical_and(c < ATT_WIN, c >= qi), has_prev)
    own_ok = jnp.logical_and(c >= ATT_WIN, c - ATT_WIN <= qi)
    return jnp.logical_or(prev_ok, own_ok)


def _attn_specs(qb):
    cur = pl.BlockSpec((qb * ATT_WIN, ATT_GW), lambda i: (i, 0))
    prev = pl.BlockSpec((ATT_WIN, ATT_GW), lambda i: (jnp.maximum(qb * i - 1, 0), 0))
    return cur, prev


def _attn_fwd(q, k, v, nb, name):
    l = q.shape[0]
    scale = ATT_E ** -0.5
    w = ATT_WIN

    qb = ATT_QB_FWD

    def body(q_ref, kc_ref, kp_ref, vc_ref, vp_ref, o_ref, lse_ref):
        i = pl.program_id(0)
        masks = [_attn_mask2(qb * i + b, nb) for b in range(qb)]
        for h in range(ATT_HG):
            sl = slice(h * ATT_E, (h + 1) * ATT_E)
            k_ext = jnp.concatenate([kp_ref[:, sl], kc_ref[:, sl]], axis=0)
            v_ext = jnp.concatenate([vp_ref[:, sl], vc_ref[:, sl]], axis=0)
            for b in range(qb):
                r, kr = slice(b * w, (b + 1) * w), slice(b * w, (b + 2) * w)
                s = jnp.where(masks[b], _nt_dot(q_ref[r, sl], k_ext[kr]) * scale, NEG_INF)
                mx = jnp.max(s, axis=-1, keepdims=True)
                p = jnp.exp(s - mx)
                den = jnp.sum(p, axis=-1, keepdims=True)
                o_ref[r, sl] = _nn_dot(p, v_ext[kr]) / den
                lse_ref[r, sl] = jnp.broadcast_to(mx + jnp.log(den), (w, ATT_E))

    cur, prev = _attn_specs(qb)
    return pl.pallas_call(
        body, grid=(l // (qb * w),), in_specs=[cur, cur, prev, cur, prev], out_specs=[cur, cur],
        out_shape=[jax.ShapeDtypeStruct((l, ATT_GW), F32)] * 2,
        compiler_params=_params(("parallel",)), name=name,
    )(q, k, k, v, v)


def _attn_bwd(q, k, v, do, lse, dd, nb, name):
    l = q.shape[0]
    scale = ATT_E ** -0.5
    w = ATT_WIN
    nblk = l // w

    def body(q_ref, kc_ref, kp_ref, vc_ref, vp_ref, do_ref, lse_ref, dd_ref, qn_ref, don_ref, lsen_ref, ddn_ref,
             dq_ref, dk_ref, dv_ref, dk_acc, dv_acc):
        i = pl.program_id(0)
        masks = [_attn_mask2(ATT_QB * i + b, nb) for b in range(ATT_QB)]
        nxt = ATT_QB * (i + 1)
        nxt_attends = jnp.logical_and(nxt < nblk, (nxt % nb) != 0)
        qi = lax.broadcasted_iota(jnp.int32, (w, w), 0)
        kj = lax.broadcasted_iota(jnp.int32, (w, w), 1)
        mask_n = jnp.logical_and(kj >= qi, nxt_attends)
        dk_acc[...] = jnp.zeros_like(dk_acc)
        dv_acc[...] = jnp.zeros_like(dv_acc)
        for h in range(ATT_HG):
            sl, col = slice(h * ATT_E, (h + 1) * ATT_E), slice(h * ATT_E, h * ATT_E + 1)
            k_ext = jnp.concatenate([kp_ref[:, sl], kc_ref[:, sl]], axis=0)
            v_ext = jnp.concatenate([vp_ref[:, sl], vc_ref[:, sl]], axis=0)
            for b in range(ATT_QB):
                r, kr = slice(b * w, (b + 1) * w), slice(b * w, (b + 2) * w)
                qh, doh, k2, v2 = q_ref[r, sl], do_ref[r, sl], k_ext[kr], v_ext[kr]
                p = jnp.where(masks[b], jnp.exp(_nt_dot(qh, k2) * scale - lse_ref[r, col]), 0.0)
                ds = p * (_nt_dot(doh, v2) - dd_ref[r, col]) * scale
                dq_ref[r, sl] = _nn_dot(ds, k2).astype(dq_ref.dtype)
                dk2, dv2 = _tn_dot(ds, qh), _tn_dot(p, doh)
                dk_acc[r, sl] += dk2[w:]
                dv_acc[r, sl] += dv2[w:]
                if b > 0:
                    rp = slice((b - 1) * w, b * w)
                    dk_acc[rp, sl] += dk2[:w]
                    dv_acc[rp, sl] += dv2[:w]
            last = slice((ATT_QB - 1) * w, ATT_QB * w)
            kl, vl, qn, don = kc_ref[last, sl], vc_ref[last, sl], qn_ref[:, sl], don_ref[:, sl]
            pn = jnp.where(mask_n, jnp.exp(_nt_dot(qn, kl) * scale - lsen_ref[:, col]), 0.0)
            dsn = pn * (_nt_dot(don, vl) - ddn_ref[:, col]) * scale
            dk_acc[last, sl] += _tn_dot(dsn, qn)
            dv_acc[last, sl] += _tn_dot(pn, don)
        dk_ref[...] = dk_acc[...].astype(dk_ref.dtype)
        dv_ref[...] = dv_acc[...].astype(dv_ref.dtype)

    cur, prev = _attn_specs(ATT_QB)
    nxt_spec = pl.BlockSpec((w, ATT_GW), lambda i: (jnp.minimum(ATT_QB * (i + 1), nblk - 1), 0))
    return pl.pallas_call(
        body, grid=(l // (ATT_QB * w),),
        in_specs=[cur, cur, prev, cur, prev, cur, cur, cur, nxt_spec, nxt_spec, nxt_spec, nxt_spec],
        out_specs=[cur] * 3, out_shape=[jax.ShapeDtypeStruct((l, ATT_GW), BF16)] * 3,
        scratch_shapes=[pltpu.VMEM((ATT_QB * w, ATT_GW), F32)] * 2,
        compiler_params=_params(("parallel",)), name=name,
    )(q, k, k, v, v, do, lse, dd, q, do, lse, dd)


def _to_perm(a, d):
    if d == 1:
        return a
    l, c = a.shape
    return a.reshape(l // d, d, c).transpose(1, 0, 2).reshape(l, c)


def _from_perm(a, d):
    if d == 1:
        return a
    l, c = a.shape
    return a.reshape(d, l // d, c).transpose(1, 0, 2).reshape(l, c)


def _merged_attn_proj(o_g, lse_g, w_t, name, tm=1024):
    l = o_g[0].shape[0]
    ng = len(o_g)

    def body(*refs):
        o_refs, l_refs, (w_ref, oa_ref, lt_ref, br_ref) = refs[:ng], refs[ng:2 * ng], refs[2 * ng:]
        ls = [r[...] for r in l_refs]
        mx = ls[0]
        for lv in ls[1:]:
            mx = jnp.maximum(mx, lv)
        es = [jnp.exp(lv - mx) for lv in ls]
        tot, acc = es[0], es[0] * o_refs[0][...]
        for e, o_ref in zip(es[1:], o_refs[1:]):
            tot, acc = tot + e, acc + e * o_ref[...]
        o = acc / tot
        oa_ref[...] = o
        lt_ref[...] = mx + jnp.log(tot)
        br_ref[...] = _nt_dot(o, w_ref[...]).astype(br_ref.dtype)

    row = pl.BlockSpec((tm, ATT_GW), lambda i: (i, 0))
    return pl.pallas_call(
        body, grid=(l // tm,),
        in_specs=[row] * (2 * ng) + [pl.BlockSpec(w_t.shape, lambda i: (0, 0))],
        out_specs=[row, row, pl.BlockSpec((tm, w_t.shape[0]), lambda i: (i, 0))],
        out_shape=[jax.ShapeDtypeStruct((l, ATT_GW), F32)] * 2 + [jax.ShapeDtypeStruct((l, w_t.shape[0]), BF16)],
        compiler_params=_params(("parallel",)), name=name,
    )(*o_g, *lse_g, w_t)


def _mem_probs(qh, kh):
    s = _nt_dot(qh, kh) * (MEM_E ** -0.5)
    e = jnp.exp(s - jnp.max(s, axis=-1, keepdims=True))
    return e / jnp.sum(e, axis=-1, keepdims=True)


def _mem_fwd(mq, kv, name, tm=1024):
    l, nm = mq.shape[0], kv.shape[0]

    def body(q_ref, kv_ref, o_ref):
        for h in range(MEM_H):
            sl = slice(h * MEM_E, (h + 1) * MEM_E)
            p = _mem_probs(q_ref[:, sl], kv_ref[:, sl])
            o_ref[:, sl] = _nn_dot(p, kv_ref[:, MEM_W + h * MEM_E:MEM_W + (h + 1) * MEM_E]).astype(o_ref.dtype)

    return pl.pallas_call(
        body, grid=(l // tm,),
        in_specs=[pl.BlockSpec((tm, MEM_W), lambda i: (i, 0)), pl.BlockSpec((nm, 2 * MEM_W), lambda i: (0, 0))],
        out_specs=pl.BlockSpec((tm, MEM_W), lambda i: (i, 0)),
        out_shape=jax.ShapeDtypeStruct((l, MEM_W), BF16),
        compiler_params=_params(("parallel",)), name=name,
    )(mq, kv)


def _mem_bwd(mq, kv, dmo, name, tm=1024):
    l, nm = mq.shape[0], kv.shape[0]
    scale = MEM_E ** -0.5

    def body(q_ref, kv_ref, do_ref, dq_ref, dkv_ref):
        @pl.when(pl.program_id(0) == 0)
        def _():
            dkv_ref[...] = jnp.zeros_like(dkv_ref)

        for h in range(MEM_H):
            sl = slice(h * MEM_E, (h + 1) * MEM_E)
            vsl = slice(MEM_W + h * MEM_E, MEM_W + (h + 1) * MEM_E)
            qh, kh, vh, doh = q_ref[:, sl], kv_ref[:, sl], kv_ref[:, vsl], do_ref[:, sl]
            p = _mem_probs(qh, kh)
            dp = _nt_dot(doh, vh)
            ds = p * (dp - jnp.sum(dp * p, axis=-1, keepdims=True)) * scale
            dq_ref[:, sl] = _nn_dot(ds, kh).astype(dq_ref.dtype)
            dkv_ref[:, sl] += _tn_dot(ds, qh)
            dkv_ref[:, vsl] += _tn_dot(p, doh)

    row = pl.BlockSpec((tm, MEM_W), lambda i: (i, 0))
    full = pl.BlockSpec((nm, 2 * MEM_W), lambda i: (0, 0))
    return pl.pallas_call(
        body, grid=(l // tm,), in_specs=[row, full, row], out_specs=[row, full],
        out_shape=[jax.ShapeDtypeStruct((l, MEM_W), BF16), jax.ShapeDtypeStruct((nm, 2 * MEM_W), F32)],
        compiler_params=_params(("arbitrary",)), name=name,
    )(mq, kv, dmo)


def _gated_out_proj(zg, branches, b_gate, w_o, x, g2, name, tm=512):
    l, d = x.shape
    nbr = len(branches)

    def body(zg_ref, *rest):
        br_refs, (bg_ref, w_ref, x_ref, g2_ref, m_ref, h_ref, n_ref) = rest[:nbr], rest[nbr:]
        merged = jnp.zeros((tm, d), F32)
        for i, br_ref in enumerate(br_refs):
            cols = slice(i * d, (i + 1) * d)
            merged += _sigmoid(zg_ref[:, cols].astype(F32) + bg_ref[:, cols]) * br_ref[...].astype(F32)
        mb = merged.astype(BF16)
        m_ref[...] = mb
        hv = jnp.dot(mb.astype(_MXU), w_ref[...].astype(_MXU), preferred_element_type=F32) + x_ref[...]
        h_ref[...] = hv
        rs = lax.rsqrt(jnp.mean(hv * hv, axis=-1, keepdims=True) + RMS_EPS)
        n_ref[...] = (hv * rs * g2_ref[...]).astype(n_ref.dtype)

    row = lambda c: pl.BlockSpec((tm, c), lambda i: (i, 0))
    full = lambda a: pl.BlockSpec(a.shape, lambda i: (0, 0))
    return pl.pallas_call(
        body, grid=(l // tm,),
        in_specs=[row(nbr * d)] + [row(d)] * nbr + [full(b_gate), full(w_o), row(d), full(g2)],
        out_specs=[row(d)] * 3,
        out_shape=[jax.ShapeDtypeStruct((l, d), BF16), jax.ShapeDtypeStruct((l, d), F32),
                   jax.ShapeDtypeStruct((l, d), BF16)],
        compiler_params=_params(("parallel",)), name=name,
    )(zg, *branches, b_gate, w_o, x, g2)


def _discretize(lam_re, lam_im, log_dt, b_re, b_im):
    dt = jnp.exp(log_dt)[:, None]
    mag = jnp.exp(lam_re * dt)
    a_re, a_im = mag * jnp.cos(lam_im * dt), mag * jnp.sin(lam_im * dt)
    nr, ni = a_re - 1.0, a_im
    den = lam_re * lam_re + lam_im * lam_im
    coef_re = (nr * lam_re + ni * lam_im) / den
    coef_im = (ni * lam_re - nr * lam_im) / den
    bb_re = coef_re[..., None] * b_re - coef_im[..., None] * b_im
    bb_im = coef_re[..., None] * b_im + coef_im[..., None] * b_re
    return a_re, a_im, bb_re, bb_im


def _tiled(re, im):
    r = re.shape[0]
    both = jnp.concatenate([re.reshape(r, -1, SCAN_W), im.reshape(r, -1, SCAN_W)], axis=2)
    return both.reshape(r, 2 * re.shape[1])


def _untiled(x):
    r = x.shape[0]
    t = x.reshape(r, -1, 2 * SCAN_W)
    return t[:, :, :SCAN_W].reshape(r, -1), t[:, :, SCAN_W:].reshape(r, -1)


def _bd_in(bb):
    return jnp.einsum("gph,gk->ghkp", bb, jnp.eye(SSM_G, dtype=bb.dtype)).reshape(SSM_W, SSM_S)


def _bd_diag(x):
    gb = SSM_G // SSM_BD
    t = x.reshape(SSM_BD, gb, SSM_H, gb, SSM_P)
    return jnp.einsum("bghgp->bghp", t).reshape(SSM_G, SSM_H, SSM_P)


_ANY = pl.BlockSpec(memory_space=pl.ANY)
_MESH = pl.DeviceIdType.MESH


def _allgather(x, name):
    def body(x_ref, out_ref, send_sems, recv_sems, local_sem):
        mx, my, mc = lax.axis_index("x"), lax.axis_index("y"), lax.axis_index("c")
        me, sibling = (mx, my, mc), (mx, my, 1 - mc)
        chips = [(1 - mx, my), (mx, 1 - my), (1 - mx, 1 - my)]

        def blk(px, py, pc):
            return out_ref.at[4 * px + 2 * py + pc]

        def copy(k, block, to, src=None):
            return pltpu.make_async_remote_copy(
                src_ref=blk(*block) if src is None else src, dst_ref=blk(*block),
                send_sem=send_sems.at[k], recv_sem=recv_sems.at[k], device_id=to, device_id_type=_MESH)

        mine = pltpu.make_async_copy(x_ref, blk(*me), local_sem)
        mine.start()
        first = [copy(0, me, sibling, src=x_ref)]
        first += [copy(1 + j, me, (*chip, mc), src=x_ref) for j, chip in enumerate(chips)]
        for cp in first:
            cp.start()
        passed = [copy(4 + j, (*chip, mc), sibling) for j, chip in enumerate(chips)]
        for j, chip in enumerate(chips):
            copy(1 + j, (*chip, mc), me).wait_recv()
            passed[j].start()
        copy(0, sibling, me).wait_recv()
        for j, chip in enumerate(chips):
            copy(4 + j, (*chip, 1 - mc), me).wait_recv()
        for cp in first + passed:
            cp.wait_send()
        mine.wait()

    return pl.pallas_call(
        body, out_shape=jax.ShapeDtypeStruct((N_DEV,) + x.shape, x.dtype), in_specs=[_ANY], out_specs=_ANY,
        scratch_shapes=[pltpu.SemaphoreType.DMA((7,)), pltpu.SemaphoreType.DMA((7,)), pltpu.SemaphoreType.DMA],
        name=name,
    )(x)


def _pair_exchange(g, name):
    def body(g_ref, out_ref, send_sems, recv_sems):
        mx, my, mc = lax.axis_index("x"), lax.axis_index("y"), lax.axis_index("c")
        copies = [pltpu.make_async_remote_copy(
            src_ref=g_ref.at[2 * k + (1 - mc)], dst_ref=out_ref.at[k], send_sem=send_sems.at[k],
            recv_sem=recv_sems.at[k], device_id=(mx, my, 1 - mc), device_id_type=_MESH) for k in range(4)]
        for cp in copies:
            cp.start()
        for cp in copies:
            cp.wait()

    return pl.pallas_call(
        body, out_shape=jax.ShapeDtypeStruct((4,) + g.shape[1:], g.dtype), in_specs=[_ANY], out_specs=_ANY,
        scratch_shapes=[pltpu.SemaphoreType.DMA((4,)), pltpu.SemaphoreType.DMA((4,))], name=name,
    )(g)


_HBM = pl.BlockSpec(memory_space=pltpu.HBM)
_SEM = pl.BlockSpec(memory_space=pltpu.SEMAPHORE)
_EFFECT = pltpu.SideEffectType.DATAFLOW_SIDE_EFFECTING
_TOKEN = jax.ShapeDtypeStruct((8, 128), F32)


def _peer(rel):
    pos = (lax.axis_index("x"), lax.axis_index("y"), lax.axis_index("c"))
    return tuple(1 - p if (rel >> (2 - i)) & 1 else p for i, p in enumerate(pos))


def _index_of(dev):
    return 4 * dev[0] + 2 * dev[1] + dev[2]


def _split_copies(src_ref, land_ref, sems, plan):
    n = len(plan)
    return [pltpu.make_async_remote_copy(
        src_ref=src_ref if s is None else src_ref.at[s], dst_ref=land_ref.at[d], send_sem=sems[k],
        recv_sem=sems[n + k], device_id=peer, device_id_type=_MESH) for k, (s, d, peer) in enumerate(plan)]


def _split_start(src, n_land, plan_fn, after, name):
    blk = src.shape[-2:]
    land = lax.empty((n_land,) + blk, src.dtype)
    n = len(plan_fn())

    def body(src_ref, land_ref, after_ref, *outs):
        for cp in _split_copies(src_ref, land_ref, outs[:2 * n], plan_fn()):
            cp.start()
        outs[2 * n + 2][...] = jnp.zeros_like(outs[2 * n + 2])

    res = pl.pallas_call(
        body, name=name,
        out_shape=(pltpu.SemaphoreType.DMA(()),) * (2 * n)
        + (pltpu.HBM(src.shape, src.dtype), pltpu.HBM(land.shape, land.dtype), _TOKEN),
        in_specs=(_HBM, _HBM, _ANY),
        out_specs=(_SEM,) * (2 * n) + (_HBM, _HBM, pl.BlockSpec(memory_space=pltpu.VMEM)),
        input_output_aliases={0: 2 * n, 1: 2 * n + 1},
        compiler_params=pltpu.CompilerParams(has_side_effects=_EFFECT),
    )(pltpu.with_memory_space_constraint(src, pltpu.HBM), pltpu.with_memory_space_constraint(land, pltpu.HBM), after)
    return res[:2 * n], res[2 * n], res[2 * n + 1], res[2 * n + 2]


def _split_wait(sems, src, land, plan_fn, after, name):
    n = len(sems) // 2

    def body(src_ref, land_ref, *rest):
        for cp in _split_copies(src_ref, land_ref, rest[:2 * n], plan_fn()):
            cp.wait_send()
            cp.wait_recv()

    return pl.pallas_call(
        body, name=name,
        out_shape=(pltpu.HBM(src.shape, src.dtype), pltpu.HBM(land.shape, land.dtype)),
        in_specs=(_HBM, _HBM) + (_SEM,) * (2 * n) + (_ANY,), out_specs=(_HBM, _HBM),
        input_output_aliases={0: 0, 1: 1},
        compiler_params=pltpu.CompilerParams(has_side_effects=_EFFECT),
    )(src, land, *sems, after)


def _gather_plan():
    me = _index_of(_peer(0))
    return [(None, me, _peer(rel)) for rel in range(1, N_DEV)]


def _gather_wait_plan():
    return [(None, _index_of(_peer(rel)), _peer(rel)) for rel in range(1, N_DEV)]


def _chip_plan():
    return [(_index_of(_peer(rel)) // 2, j, _peer(rel)) for j, rel in enumerate((4, 2, 6))]


def _owner_plan():
    return [(_index_of(_peer(rel)), rel - 1, _peer(rel)) for rel in range(1, N_DEV)]


def _pair_sum(g, t1, my_c, name, tr):
    _, r, c = g.shape

    def body(c_ref, g_ref, t_ref, o_ref, ob_ref):
        s = g_ref[...] + t_ref[...]
        o_ref[...] = s
        ob_ref[...] = s.astype(BF16)

    blk = pl.BlockSpec((None, tr, c), lambda k, i, cr: (k, i, 0))
    return pl.pallas_call(
        body,
        grid_spec=pltpu.PrefetchScalarGridSpec(
            num_scalar_prefetch=1, grid=(4, r // tr),
            in_specs=[pl.BlockSpec((None, tr, c), lambda k, i, cr: (2 * k + cr[0], i, 0)), blk],
            out_specs=[blk, blk]),
        out_shape=[jax.ShapeDtypeStruct((4, r, c), F32), jax.ShapeDtypeStruct((4, r, c), BF16)],
        compiler_params=_params(("parallel", "parallel")), name=name,
    )(my_c, g, t1)


def _adam_math(g, w, m, v):
    m = ADAM_B1 * m + (1.0 - ADAM_B1) * g
    v = ADAM_B2 * v + (1.0 - ADAM_B2) * (g * g)
    m_hat = m / (1.0 - ADAM_B1 ** ADAM_STEP)
    v_hat = v / (1.0 - ADAM_B2 ** ADAM_STEP)
    delta = -ADAM_LR * (m_hat / (jnp.sqrt(v_hat) + ADAM_EPS) + ADAM_WD * w)
    return delta, m, v


def _grad_sum(own, own_index, recv, name, tr):
    _, r, c = own.shape
    n = recv.shape[0]

    def body(k_ref, own_ref, *rest):
        g = own_ref[...]
        for recv_ref in rest[:n]:
            g = g + recv_ref[...].astype(F32)
        rest[n][...] = g

    def slot(j):
        return pl.BlockSpec((None, tr, c), lambda i, kr: (j, i, 0))

    return pl.pallas_call(
        body,
        grid_spec=pltpu.PrefetchScalarGridSpec(
            num_scalar_prefetch=1, grid=(r // tr,),
            in_specs=[pl.BlockSpec((None, tr, c), lambda i, kr: (kr[0], i, 0))] + [slot(j) for j in range(n)],
            out_specs=pl.BlockSpec((tr, c), lambda i, kr: (i, 0))),
        out_shape=jax.ShapeDtypeStruct((r, c), F32),
        compiler_params=_params(("parallel",)), name=name,
    )(own_index, own, *([recv] * n))


def _adam_many(g, w, m, v, row_tiles, name):
    n = len(g)

    def body(*refs):
        ins, outs = refs[:4 * n], refs[4 * n:]
        for i in range(n):
            res = _adam_math(ins[i][...], ins[n + i][...], ins[2 * n + i][...], ins[3 * n + i][...])
            for kind in range(3):
                outs[kind * n + i][...] = res[kind]

    def spec(a):
        blk = (a.shape[0] // row_tiles,) + a.shape[1:]
        return pl.BlockSpec(blk, lambda t, nd=a.ndim: (t,) + (0,) * (nd - 1))

    specs = [spec(a) for a in g]
    res = pl.pallas_call(
        body, grid=(row_tiles,), in_specs=specs * 4, out_specs=specs * 3,
        out_shape=[jax.ShapeDtypeStruct(a.shape, F32) for a in g] * 3,
        compiler_params=_params(("parallel",)), name=name,
    )(*g, *w, *m, *v)
    return res[:n], res[n:2 * n], res[2 * n:]


def _sum8(g8, name):
    _, r, c = g8.shape

    def body(g_ref, o_ref):
        acc = g_ref[0]
        for j in range(1, N_DEV):
            acc = acc + g_ref[j]
        o_ref[...] = acc

    return pl.pallas_call(
        body, grid=(1,), in_specs=[pl.BlockSpec((N_DEV, r, c), lambda i: (0, 0, 0))],
        out_specs=pl.BlockSpec((r, c), lambda i: (0, 0)), out_shape=jax.ShapeDtypeStruct((r, c), F32),
        compiler_params=_params(("arbitrary",)), name=name,
    )(g8)


def _pack(arrs, pad_rows=8):
    flat = jnp.concatenate([a.reshape(-1) for a in arrs])
    n = flat.shape[0]
    q = PACK_C * pad_rows
    tot = -(-n // q) * q
    if tot != n:
        flat = jnp.concatenate([flat, jnp.zeros((tot - n,), flat.dtype)])
    return flat.reshape(tot // PACK_C, PACK_C)


def _unpack(buf, shapes):
    flat = buf.reshape(-1)
    out, off = [], 0
    for s in shapes:
        n = int(np.prod(s))
        out.append(flat[off:off + n].reshape(s))
        off += n
    return out


GROUPS = (("w_in",),
          ("w_glu", "w_ssm_br", "w_mem_br", "w_attn_br"),
          ("w_up", "w_down"),
          ("w_mem_kv", "w_o"))
GROUP_TR = (400, 384, 512, 256)
MLP_GROUP = 2
MIXER_GROUPS = (1, 3)
ATTN_BR_FOLD = 2


def _stored_shape(name):
    r, c, ax = BIG_SHAPE[name]
    rows, cols = (r // N_DEV, c) if ax == 0 else (c // N_DEV, r)
    return (rows // ATTN_BR_FOLD, cols * ATTN_BR_FOLD) if name == "w_attn_br" else (rows, cols)


def _stored(shard, name):
    a = shard[0].T if BIG_SHAPE[name][2] == 1 else shard[0]
    return a.reshape(_stored_shape(name))


def _unstored(a, name):
    r, c, ax = BIG_SHAPE[name]
    if ax == 0:
        return a.reshape(1, r // N_DEV, c)
    return a.reshape(c // N_DEV, r).T[None]


def _pack_group(d, names):
    return jnp.concatenate([_stored(d[n], n) for n in names], axis=0)


def _split_group(buf, names):
    out, off = {}, 0
    for n in names:
        rows = _stored_shape(n)[0]
        out[n] = buf[..., off:off + rows, :]
        off += rows
    return out


def _full_stored(stacked, name):
    r, c, ax = BIG_SHAPE[name]
    return stacked.reshape((r, c) if ax == 0 else (c, r))


def _stacked_stored(full, name):
    return full.reshape((N_DEV,) + _stored_shape(name))


def _gelu_parts(x):
    c0, c1 = math.sqrt(2.0 / math.pi), 0.044715
    th = jnp.tanh(c0 * (x + c1 * x * x * x))
    return th, c0, c1


def _local_step(x, mem, tgt, wb, sp, late_weights, grads_ready, small_grads_ready):
    l = x.shape[0]
    w_a, w_g = wb["w_in"][:ZA_W], wb["w_in"][ZA_W:]

    a_re, a_im, bb_re, bb_im = _discretize(sp["ssm_lambda_re"], sp["ssm_lambda_im"], sp["ssm_log_dt"],
                                           sp["ssm_b_re"], sp["ssm_b_im"])
    a_pair = jnp.stack([a_re.reshape(1, SSM_S), a_im.reshape(1, SSM_S)])
    a_conj = jnp.stack([a_re.reshape(1, SSM_S), -a_im.reshape(1, SSM_S)])
    b_re_t, b_im_t = _bd_in(bb_re).astype(BF16), _bd_in(bb_im).astype(BF16)
    c_re_t = _bd_in(sp["ssm_c_re"].transpose(0, 2, 1)).astype(BF16)
    c_im_t = (-_bd_in(sp["ssm_c_im"].transpose(0, 2, 1))).astype(BF16)
    d_row = sp["ssm_d"].reshape(1, SSM_W)

    n1 = _rms_fwd(x, sp["norm1_g"], "rms1")
    za = _mm(n1, w_a, [BF16], tb=True, name="in_proj_a", tn=1664)
    zg = _mm(n1, w_g, [BF16], tb=True, name="in_proj_g")
    for gi in MIXER_GROUPS:
        wb = {**wb, **late_weights(gi, za)}
    u = za[:, :SSM_W]
    mq = za[:, ZA_W - MEM_W:]

    u_s = _scan_order(u)
    s_all = _ssm_scan(u_s, b_re_t, b_im_t, a_pair, reverse=False, name="ssm_scan_fwd")
    def gelu_epi(acc, ut, dr):
        y = acc + dr * ut.astype(F32)
        th, _, _ = _gelu_parts(y)
        return y, 0.5 * y * (1.0 + th)
    y0, y1 = [_time_order(t) for t in _mm(s_all, _tiled(c_re_t, c_im_t), [F32, BF16], tb=True, epi=gelu_epi,
                                          mn=[u_s], rows=[d_row], bd=SSM_BD, tm=2048, name="ssm_cs")]

    def glu_epi(acc, y1t, bg):
        t = acc + bg
        return t, y1t.astype(F32) * _sigmoid(t)
    t_glu, y2 = _mm(y1, wb["w_glu"], [F32, BF16], epi=glu_epi, mn=[y1], rows=[sp["b_glu"]], name="ssm_glu")
    br_ssm = _mm(y2, wb["w_ssm_br"], [BF16], tb=True, name="ssm_br")

    qkv_p, o_g, lse_g = [], [], []
    for g, d in enumerate(DILATIONS):
        nb = l // d // ATT_WIN
        cols = [za[:, SSM_W + (3 * j + g) * ATT_GW: SSM_W + (3 * j + g + 1) * ATT_GW] for j in range(3)]
        qp, kp, vp = [_to_perm(cc, d) for cc in cols]
        qkv_p.append((qp, kp, vp))
        og, lg = _attn_fwd(qp, kp, vp, nb, "attn_fwd%d" % g)
        o_g.append(_from_perm(og, d))
        lse_g.append(_from_perm(lg, d))

    o_att, lse_tot, br_attn = _merged_attn_proj(o_g, lse_g, wb["w_attn_br"], "attn_merge_br")

    mn = _rms_fwd(mem, sp["mem_norm_g"], "rms_mem")
    kv = _mm(mn, wb["w_mem_kv"], [BF16], name="mem_kv")
    mo = _mem_fwd(mq, kv, "mem_attn_fwd")
    br_mem = _mm(mo, wb["w_mem_br"], [BF16], tb=True, name="mem_br")

    merged, h1, n2 = _gated_out_proj(zg, [br_ssm, br_attn, br_mem], sp["b_gate"], wb["w_o"], x, sp["norm2_g"],
                                     "gated_o_proj")

    def up_epi(acc):
        ra = jnp.maximum(acc, 0.0)
        return ra * ra, ra
    wm = late_weights(MLP_GROUP, n2)
    f_act, r_act = _mm(n2, wm["w_up"], [BF16, BF16], tb=True, epi=up_epi, name="mlp_up")
    def down_epi(acc, ht, tv, gf):
        hv = acc + ht
        rs = lax.rsqrt(jnp.mean(hv * hv, axis=-1, keepdims=True) + RMS_EPS)
        err = hv * rs * gf - tv
        dh, dgf = _rms_bwd_tile(hv, err * (1.0 / D_MODEL), gf)
        return dh, dgf, _colsum(err * err) * (0.5 / D_MODEL)
    dh2, d_final_g, loss_cols = _mm(f_act, wm["w_down"], [F32], epi=down_epi, mn=[h1, tgt], rows=[sp["final_g"]],
                                    n_sums=2, tk=1024, name="mlp_down")
    loss = jnp.sum(loss_cols, axis=1, keepdims=True)

    gw, gs = {}, {"final_g": d_final_g, "loss": loss}
    d_act = _mm(dh2, wm["w_down"], [BF16], tb=True, epi=lambda acc, ra: (acc * 2.0 * ra.astype(F32),), mn=[r_act],
                name="mlp_down_dx")
    dw_down = _mm(f_act, dh2, [F32], ta=True, name="mlp_down_dw")
    dw_up = _mm(d_act, n2, [F32], ta=True, name="mlp_up_dw")
    token = grads_ready(MLP_GROUP, {"w_up": dw_up, "w_down": dw_down})
    def up_dx_epi(acc, ht, dht, g2):
        dx, dg = _rms_bwd_tile(ht, acc, g2)
        return dx + dht, dg
    dh1, gs["norm2_g"] = _mm(d_act, wm["w_up"], [F32], epi=up_dx_epi, mn=[h1, dh2],
                             rows=[sp["norm2_g"] + token[:1, :1]], n_sums=1, tk=1024, name="mlp_up_dx")
    gw["w_o"] = _mm(merged, dh1, [F32], ta=True, name="o_proj_dw")

    def gate_bwd_epi(dm, *tiles):
        dbr, dz = [], []
        for zt, bt, bias in zip(tiles[0:3], tiles[3:6], tiles[6:9]):
            gt = _sigmoid(zt.astype(F32) + bias)
            dbr.append(dm * gt)
            dz.append(dm * bt.astype(F32) * gt * (1.0 - gt))
        return (*dbr, *dz, *[_colsum(t) for t in dz])
    gate_bias = [sp["b_gate"][:, i * D_MODEL:(i + 1) * D_MODEL] for i in range(3)]
    res = _mm(dh1, wb["w_o"], [BF16] * 6, tb=True, epi=gate_bwd_epi, mn=[(zg, 0), (zg, 1), (zg, 2), br_ssm, br_attn, br_mem],
              rows=gate_bias, n_sums=3, tm=512, name="o_proj_dx")
    (dbr_ssm, dbr_attn, dbr_mem), dzg = res[0:3], res[3:6]
    gs["b_gate"] = jnp.concatenate(res[6:9], axis=1)

    gw["w_ssm_br"] = _mm(dbr_ssm, y2, [F32], ta=True, name="ssm_br_dw")
    def glu_bwd_epi(dy, y1t, tt):
        sg = _sigmoid(tt)
        dt = dy * y1t.astype(F32) * sg * (1.0 - sg)
        return dt, dy * sg, _colsum(dt)
    dt_glu, dy1a, gs["b_glu"] = _mm(dbr_ssm, wb["w_ssm_br"], [BF16, F32], epi=glu_bwd_epi, mn=[y1, t_glu], n_sums=1,
                                    name="ssm_br_dx")
    gw["w_glu"] = _mm(y1, dt_glu, [F32], ta=True, name="ssm_glu_dw")

    def gelu_bwd_epi(acc, dy1t, y0t, ut):
        th, c0, c1 = _gelu_parts(y0t)
        dg = 0.5 * (1.0 + th) + 0.5 * y0t * (1.0 - th * th) * c0 * (1.0 + 3.0 * c1 * y0t * y0t)
        dy = (acc + dy1t) * dg
        return dy, _colsum(dy * ut.astype(F32))
    dy0, gs["ssm_d"] = _mm(dt_glu, wb["w_glu"], [F32], tb=True, epi=gelu_bwd_epi, mn=[dy1a, y0, u], n_sums=1,
                           name="ssm_glu_dx")
    dy0_s = _scan_order(dy0)
    lam, da, d_b, d_c = _ssm_scan(dy0_s, c_re_t, c_im_t, a_conj, reverse=True, s_fwd=s_all, u=u_s,
                                  name="ssm_scan_bwd")
    du = _time_order(_mm(lam, _tiled(b_re_t, b_im_t), [BF16], tb=True,
                         epi=lambda acc, dyt, dr: (acc + dyt * dr,), mn=[dy0_s], rows=[d_row], bd=SSM_BD, tm=2048, name="ssm_bu_dx"))
    gs["a_re"], gs["a_im"] = da[0], da[1]
    (dbr, dbi), (dcr, dci) = _untiled(d_b), _untiled(d_c)
    gs["bb_re"], gs["bb_im"] = _bd_diag(dbr).transpose(0, 2, 1), _bd_diag(dbi).transpose(0, 2, 1)
    gs["ssm_c_re"], gs["ssm_c_im"] = _bd_diag(dcr), -_bd_diag(dci)

    gw["w_attn_br"] = _mm(dbr_attn, o_att, [F32], ta=True, name="attn_br_dw")

    def do_epi(acc, ot):
        prod = acc * ot
        head = lax.broadcasted_iota(jnp.int32, prod.shape, 1) // ATT_E
        dd = jnp.zeros_like(prod)
        for h in range(ATT_HG):
            dd = jnp.where(head == h, jnp.sum(jnp.where(head == h, prod, 0.0), axis=1, keepdims=True), dd)
        return acc, dd
    do_att, dd_att = _mm(dbr_attn, wb["w_attn_br"], [BF16, F32], epi=do_epi, mn=[o_att], name="attn_br_dx")
    dq_l, dk_l, dv_l = [], [], []
    for g, d in enumerate(DILATIONS):
        nb = l // d // ATT_WIN
        qp, kp, vp = qkv_p[g]
        dq, dk, dv = _attn_bwd(qp, kp, vp, _to_perm(do_att, d), _to_perm(lse_tot, d), _to_perm(dd_att, d),
                               nb, "attn_bwd%d" % g)
        dq_l.append(_from_perm(dq, d))
        dk_l.append(_from_perm(dk, d))
        dv_l.append(_from_perm(dv, d))

    gw["w_mem_br"] = _mm(dbr_mem, mo, [F32], ta=True, name="mem_br_dw")
    dmo = _mm(dbr_mem, wb["w_mem_br"], [BF16], name="mem_br_dx")
    dmq, dkv = _mem_bwd(mq, kv, dmo, "mem_attn_bwd")
    gw["w_mem_kv"] = _mm(mn, dkv, [F32], ta=True, name="mem_kv_dw")
    dmn = _mm(dkv, wb["w_mem_kv"], [F32], tb=True, name="mem_kv_dx")
    token = sum(grads_ready(gi, gw) for gi in MIXER_GROUPS)
    gs["mem_norm_g"] = _rms_bwd(mem, dmn, None, sp["mem_norm_g"] + token[:1, :1], "rms_mem_bwd")[1]

    dza = jnp.concatenate([du] + dq_l + dk_l + dv_l + [dmq], axis=1)
    dn_a = _mm(dza, w_a, [F32], name="in_proj_a_dx", tk=1664)
    dw_a = _mm(dza, n1, [F32], ta=True, name="in_proj_a_dw", tm=1664)
    dw_g = [_mm(dzg[i], n1, [F32], ta=True, name="in_proj_g_dw%d" % i) for i in range(3)]
    gw["w_in"] = jnp.concatenate([dw_a] + dw_g, axis=0)
    token = grads_ready(0, gw) + small_grads_ready(gs)
    def in_dx_epi(acc, pt, xt, dht, g1):
        dx, dg = _rms_bwd_tile(xt, acc + pt, g1)
        return dx + dht, dg
    w_gs = [w_g[i * D_MODEL:(i + 1) * D_MODEL] for i in range(3)]
    grad_x, gs["norm1_g"] = _mm(dzg[0], w_gs[0], [F32], pair2=(dzg[1], w_gs[1], dzg[2], w_gs[2]), epi=in_dx_epi,
                                mn=[dn_a, x, dh1],
                                rows=[sp["norm1_g"] + token[:1, :1]], n_sums=1, tm=512, name="in_proj_g_dx")
    return loss, grad_x, gs


_SMALL_GRAD_ORDER = ("norm1_g", "mem_norm_g", "b_gate", "a_re", "a_im", "bb_re", "bb_im", "ssm_c_re", "ssm_c_im",
                     "ssm_d", "b_glu", "norm2_g", "final_g", "loss")


def kernel(x, mem, norm1_g, mem_norm_g, w_in, b_gate, ssm_lambda_re, ssm_lambda_im, ssm_log_dt, ssm_b_re, ssm_b_im, ssm_c_re, ssm_c_im, ssm_d, w_glu, b_glu, w_ssm_br, w_attn_br, w_mem_kv, w_mem_br, w_o, norm2_g, w_up, w_down, final_g, loss_target, m_norm1_g, m_mem_norm_g, m_w_in, m_b_gate, m_ssm_lambda_re, m_ssm_lambda_im, m_ssm_log_dt, m_ssm_b_re, m_ssm_b_im, m_ssm_c_re, m_ssm_c_im, m_ssm_d, m_w_glu, m_b_glu, m_w_ssm_br, m_w_attn_br, m_w_mem_kv, m_w_mem_br, m_w_o, m_norm2_g, m_w_up, m_w_down, m_final_g, v_norm1_g, v_mem_norm_g, v_w_in, v_b_gate, v_ssm_lambda_re, v_ssm_lambda_im, v_ssm_log_dt, v_ssm_b_re, v_ssm_b_im, v_ssm_c_re, v_ssm_c_im, v_ssm_d, v_w_glu, v_b_glu, v_w_ssm_br, v_w_attn_br, v_w_mem_kv, v_w_mem_br, v_w_o, v_norm2_g, v_w_up, v_w_down, v_final_g):
    args = dict(locals())
    w = {n: args[n] for n in ALL_W}
    m = {n: args["m_" + n] for n in ALL_W}
    v = {n: args["v_" + n] for n in ALL_W}
    my_c = lax.axis_index("c").astype(jnp.int32).reshape(1)
    my_chip = (2 * lax.axis_index("x") + lax.axis_index("y")).astype(jnp.int32).reshape(1)

    w_pack = [_pack_group(w, names) for names in GROUPS]
    my_index = (4 * lax.axis_index("x") + 2 * lax.axis_index("y") + lax.axis_index("c")).astype(jnp.int32)
    zero = jnp.zeros((), jnp.int32)
    w_all = _allgather(w_pack[0].astype(BF16), "allgather_weights0")
    wb = {n: _full_stored(part, n) for n, part in _split_group(w_all, GROUPS[0]).items()}
    gathers = {gi: _split_start(w_pack[gi].astype(BF16), N_DEV, _gather_plan, w_all, "weights_gather_start%d" % gi)
               for gi in range(1, len(GROUPS))}

    def gathered(started, after, name):
        sems, src, land, _ = started
        src, land = _split_wait(sems, src, land, _gather_wait_plan, after, name)
        return lax.dynamic_update_slice(land, src[None], (my_index, zero, zero))

    def late_weights(gi, after):
        full = gathered(gathers[gi], after, "weights_gather_wait%d" % gi)
        return {n: _full_stored(part, n) for n, part in _split_group(full, GROUPS[gi]).items()}

    pending = {}

    def grads_ready(gi, grads):
        g_pack = jnp.concatenate([_stacked_stored(grads[n], n) for n in GROUPS[gi]], axis=1)
        if gi == 0:
            t1 = _pair_exchange(g_pack, "grad_pair_exchange%d" % gi)
            p_sum, p_bf = _pair_sum(g_pack, t1, my_c, "grad_pair_sum%d" % gi, GROUP_TR[gi])
            started = _split_start(p_bf, 3, _chip_plan, p_sum, "grad_chip_exchange_start%d" % gi)
            pending[gi] = (p_sum, my_chip, started, _chip_plan)
        else:
            started = _split_start(g_pack.astype(BF16), N_DEV - 1, _owner_plan, g_pack, "grad_exchange_start%d" % gi)
            pending[gi] = (g_pack, my_index.reshape(1), started, _owner_plan)
        return started[3]

    early_small = [n for n in _SMALL_GRAD_ORDER if n != "norm1_g"]
    small_started = []

    def small_grads_ready(gs):
        started = _split_start(_pack([gs[n] for n in early_small]), N_DEV, _gather_plan, gs["mem_norm_g"],
                               "small_grads_gather_start")
        small_started.append((started, [gs[n].shape for n in early_small]))
        return started[3]

    sp = {
        "norm1_g": norm1_g + sum(started[3][:1, :1] for started in gathers.values()), "mem_norm_g": mem_norm_g, "b_gate": b_gate, "b_glu": b_glu, "norm2_g": norm2_g,
        "final_g": final_g.reshape(1, D_MODEL),
        "ssm_lambda_re": ssm_lambda_re[0], "ssm_lambda_im": ssm_lambda_im[0], "ssm_log_dt": ssm_log_dt[0],
        "ssm_b_re": ssm_b_re[0], "ssm_b_im": ssm_b_im[0], "ssm_c_re": ssm_c_re[0], "ssm_c_im": ssm_c_im[0],
        "ssm_d": ssm_d[0],
    }
    loss, grad_x, gs = _local_step(x[0], mem[0], loss_target[0], wb, sp, late_weights, grads_ready,
                                     small_grads_ready)
    n1_started = _split_start(_pack([gs["norm1_g"]]), N_DEV, _gather_plan, grad_x, "norm1_grad_gather_start")

    big_g = {}
    for gi, names in enumerate(GROUPS):
        own, own_index, (sems, src, land, _), plan = pending[gi]
        recv = _split_wait(sems, src, land, plan, grad_x, "grad_exchange_wait%d" % gi)[1]
        g_pack = _grad_sum(own, own_index, recv, "grad_sum%d" % gi, GROUP_TR[gi])
        for n, part in _split_group(g_pack, names).items():
            big_g[n] = _unstored(part, n)
    rows_of = lambda d, names: [d[n].reshape(d[n].shape[-2:]) for n in names]
    big_out = _adam_many(rows_of(big_g, BIG), rows_of(w, BIG), rows_of(m, BIG), rows_of(v, BIG), 8, "adam_big")
    big = [big_g] + [{n: a[None] for n, a in zip(BIG, outs)} for outs in big_out]

    (sg_started, sg_shapes), = small_started
    sg_all = jnp.concatenate([gathered(sg_started, big_out[0][0], "small_grads_gather_wait"),
                              gathered(n1_started, big_out[0][0], "norm1_grad_gather_wait")], axis=1)
    sg_sum = _sum8(sg_all, "sum_small_grads")
    n1_rows = n1_started[1].shape[0]
    sg = dict(zip(early_small, _unpack(sg_sum[:-n1_rows], sg_shapes)))
    sg["norm1_g"] = _unpack(sg_sum[-n1_rows:], [gs["norm1_g"].shape])[0]
    _, disc_vjp = jax.vjp(_discretize, sp["ssm_lambda_re"], sp["ssm_lambda_im"], sp["ssm_log_dt"],
                          sp["ssm_b_re"], sp["ssm_b_im"])
    d_lre, d_lim, d_ldt, d_bre, d_bim = disc_vjp((sg["a_re"].reshape(SSM_G, SSM_P), sg["a_im"].reshape(SSM_G, SSM_P),
                                                  sg["bb_re"], sg["bb_im"]))
    small_grad = {
        "norm1_g": sg["norm1_g"], "mem_norm_g": sg["mem_norm_g"], "b_gate": sg["b_gate"],
        "ssm_lambda_re": d_lre, "ssm_lambda_im": d_lim, "ssm_log_dt": d_ldt, "ssm_b_re": d_bre, "ssm_b_im": d_bim,
        "ssm_c_re": sg["ssm_c_re"], "ssm_c_im": sg["ssm_c_im"], "ssm_d": sg["ssm_d"], "b_glu": sg["b_glu"],
        "norm2_g": sg["norm2_g"], "final_g": sg["final_g"],
    }
    small_grad = {n: small_grad[n].reshape(w[n].shape) for n in SMALL}

    def squeezed(a):
        return a.reshape(a.shape[1:]) if a.ndim > 2 else a.reshape(1, -1)

    sq = lambda d: [squeezed(d[n]) for n in SMALL]
    small_out = _adam_many(sq(small_grad), sq(w), sq(m), sq(v), 1, "adam_small")
    small = [small_grad] + [{n: a.reshape(w[n].shape) for n, a in zip(SMALL, outs)} for outs in small_out]

    outs = [sg["loss"][0, 0], grad_x[None]]
    for kind in range(4):
        for n in ALL_W:
            outs.append(big[kind][n] if n in BIG else small[kind][n])
    return tuple(outs)
```
